```python
import jax, jax.numpy as jnp
from jax import lax
import numpy as np

D_MODEL = 1024
BATCH = 8
SEQ = 4096
DEPTH = 2

CHUNK = 64
QBLOCK = 128
MEM_LEN = 256
EPS = 1e-6
FOX_HEADS = 4
FOX_HD = 64
GLA_HEADS = 4
GLA_DK = 64
GLA_DV = 128
GLA_GATE_RANK = 16
GLA_TAU = 16.0
MLA_HEADS = 4
MLA_Q_RANK = 256
MLA_KV_RANK = 128
MLA_NOPE = 64
MLA_ROPE = 32
MLA_VD = 64
ROPE_BASE = 10000.0
XA_HEADS = 4
XA_HD = 128
D_FF = 4 * D_MODEL
N_BRANCH = 3

FOX_W = FOX_HEADS * FOX_HD
GLA_K_W = GLA_HEADS * GLA_DK
GLA_V_W = GLA_HEADS * GLA_DV
MLA_QK_HD = MLA_NOPE + MLA_ROPE
MLA_W = MLA_HEADS * MLA_VD
XA_W = XA_HEADS * XA_HD

IN_SIZES = (FOX_W, FOX_W, FOX_W, FOX_HEADS,
            GLA_K_W, GLA_K_W, GLA_V_W, GLA_GATE_RANK, GLA_V_W,
            MLA_Q_RANK, MLA_KV_RANK, MLA_ROPE,
            N_BRANCH * D_MODEL)
N_IN = sum(IN_SIZES)

kernel_name = 'hybrid_fox_gla_mla_gated_encoder'

F32 = jnp.float32


def rms_norm(x, g):
    xf = x.astype(F32)
    y = xf * lax.rsqrt(jnp.mean(xf * xf, axis=-1, keepdims=True) + EPS)
    return (y * g.astype(F32)).astype(x.dtype)


def split_heads(z, n):
    b, s, w = z.shape
    return z.reshape(b, s, n, w // n).transpose(0, 2, 1, 3)


def merge_heads(z):
    b, h, s, d = z.shape
    return z.transpose(0, 2, 1, 3).reshape(b, s, h * d)


def split_cols(z, sizes):
    out, start = [], 0
    for n in sizes:
        out.append(z[..., start:start + n])
        start += n
    return out


def rope(x, pos):
    half = x.shape[-1] // 2
    inv = ROPE_BASE ** (-jnp.arange(half, dtype=F32) / half)
    ang = pos.astype(F32)[:, None] * inv[None, :]
    cos, sin = jnp.cos(ang), jnp.sin(ang)
    xf = x.astype(F32)
    x1, x2 = xf[..., :half], xf[..., half:]
    return jnp.concatenate([x1 * cos - x2 * sin, x2 * cos + x1 * sin], axis=-1).astype(x.dtype)


def block_sweep_attention(q, k, v, scale, chunk_causal, log_decay=None):
    b, h, s, dk = q.shape
    nb = s // QBLOCK
    qb = q.reshape(b, h, nb, QBLOCK, dk).transpose(2, 0, 1, 3, 4)
    kpos = jnp.arange(s)
    idx = jnp.arange(nb)

    def one_block(args):
        i, qi = args[0], args[1]
        qpos = i * QBLOCK + jnp.arange(QBLOCK)
        logits = jnp.einsum('bhqd,bhkd->bhqk', qi, k, preferred_element_type=F32) * scale
        if log_decay is not None:
            logits = logits + args[2][..., :, None] - log_decay[:, :, None, :]
        limit = ((qpos // CHUNK) + 1) * CHUNK if chunk_causal else qpos + 1
        mask = kpos[None, :] < limit[:, None]
        p = jax.nn.softmax(jnp.where(mask, logits, -jnp.inf), axis=-1)
        return jnp.einsum('bhqk,bhkd->bhqd', p.astype(v.dtype), v)

    if log_decay is None:
        xs = (idx, qb)
    else:
        db = log_decay.reshape(b, h, nb, QBLOCK).transpose(2, 0, 1, 3)
        xs = (idx, qb, db)
    out = lax.map(one_block, xs)
    return out.transpose(1, 2, 0, 3, 4).reshape(b, h, s, v.shape[-1])


def fox_branch(q, k, v, f_logit, b_f):
    log_f = jax.nn.log_sigmoid(f_logit.astype(F32) + b_f.astype(F32))
    cum = jnp.cumsum(log_f, axis=1).transpose(0, 2, 1)
    o = block_sweep_attention(split_heads(q, FOX_HEADS), split_heads(k, FOX_HEADS),
                              split_heads(v, FOX_HEADS), FOX_HD ** -0.5,
                              chunk_causal=False, log_decay=cum)
    return merge_heads(o)


def gla_branch(q, k, v, g_low, r, w_gate, b_gate, g_out):
    b, s, _ = q.shape
    nc = s // CHUNK
    dt = q.dtype
    log_a = jax.nn.log_sigmoid((g_low @ w_gate + b_gate).astype(F32)) / GLA_TAU

    def chunks(z, d):
        return split_heads(z, GLA_HEADS).reshape(b, GLA_HEADS, nc, CHUNK, d)

    qc = chunks(q, GLA_DK).astype(F32) * (GLA_DK ** -0.5)
    kc = chunks(k, GLA_DK).astype(F32)
    vc = chunks(v, GLA_DV).astype(F32)
    cum = jnp.cumsum(chunks(log_a, GLA_DK), axis=3)
    end = cum[:, :, :, -1:, :]
    k_dec = kc * jnp.exp(end - cum)
    u = jnp.einsum('bhcld,bhcle->cbhde', k_dec, vc)
    a = jnp.exp(end[:, :, :, 0, :]).transpose(2, 0, 1, 3)

    def step(state, inp):
        u_c, a_c = inp
        state = a_c[..., None] * state + u_c
        return state, state

    _, states = lax.scan(step, jnp.zeros((b, GLA_HEADS, GLA_DK, GLA_DV), F32), (u, a))
    o = jnp.einsum('bhcld,cbhde->bhcle', qc, states).reshape(b, GLA_HEADS, s, GLA_DV)
    o = merge_heads(rms_norm(o, g_out)).astype(dt)
    return o * jax.nn.silu(r)


def mla_branch(c_q, c_kv, k_rope_in, g_q, w_uq, g_kv, w_ukv, pos):
    q = split_heads(rms_norm(c_q, g_q) @ w_uq, MLA_HEADS)
    kv = split_heads(rms_norm(c_kv, g_kv) @ w_ukv, MLA_HEADS)
    q_nope, q_rope = q[..., :MLA_NOPE], q[..., MLA_NOPE:]
    k_nope, v = kv[..., :MLA_NOPE], kv[..., MLA_NOPE:]
    k_rope = rope(k_rope_in, pos)[:, None]
    qh = jnp.concatenate([q_nope, rope(q_rope, pos)], axis=-1)
    kh = jnp.concatenate([k_nope, jnp.broadcast_to(k_rope, k_nope.shape[:-1] + (MLA_ROPE,))], axis=-1)
    o = block_sweep_attention(qh, kh, v, MLA_QK_HD ** -0.5, chunk_causal=True)
    return merge_heads(o)


def memory_cross_attention(h, m, w_xq, w_xkv, w_xo):
    q = split_heads(h @ w_xq, XA_HEADS)
    k, v = jnp.split(m @ w_xkv, 2, axis=-1)
    k, v = split_heads(k, XA_HEADS), split_heads(v, XA_HEADS)
    logits = jnp.einsum('bhqd,bhkd->bhqk', q, k, preferred_element_type=F32) * (XA_HD ** -0.5)
    p = jax.nn.softmax(logits, axis=-1)
    o = jnp.einsum('bhqk,bhkd->bhqd', p.astype(v.dtype), v)
    return merge_heads(o) @ w_xo


def _fwd_setup_inputs(seed: int = 0) -> dict:
    key = jax.random.key(seed)
    ks = jax.random.split(key, 32)

    def dense(k, shape, fan_in):
        return jax.random.normal(k, shape, F32) * (fan_in ** -0.5)

    def gain(k, shape):
        return 1.0 + 0.02 * jax.random.normal(k, shape, F32)

    def bias(k, shape, scale):
        return scale * jax.random.normal(k, shape, F32)

    L, D = DEPTH, D_MODEL
    return {
        'x': jax.random.normal(ks[0], (BATCH, SEQ, D), F32),
        'mem': jax.random.normal(ks[1], (BATCH, MEM_LEN, D), F32),
        'g_mix': gain(ks[2], (L, D)),
        'w_in': dense(ks[3], (L, D, N_IN), D),
        'b_fox_forget': bias(ks[4], (L, FOX_HEADS), 0.1),
        'w_gla_gate': dense(ks[5], (L, GLA_GATE_RANK, GLA_K_W), GLA_GATE_RANK),
        'b_gla_gate': bias(ks[6], (L, GLA_K_W), 0.1),
        'g_gla_out': gain(ks[7], (L, GLA_DV)),
        'g_mla_q': gain(ks[8], (L, MLA_Q_RANK)),
        'w_mla_uq': dense(ks[9], (L, MLA_Q_RANK, MLA_HEADS * MLA_QK_HD), MLA_Q_RANK),
        'g_mla_kv': gain(ks[10], (L, MLA_KV_RANK)),
        'w_mla_ukv': dense(ks[11], (L, MLA_KV_RANK, MLA_HEADS * (MLA_NOPE + MLA_VD)), MLA_KV_RANK),
        'b_branch_gate': bias(ks[12], (L, N_BRANCH * D), 0.1),
        'w_up_fox': dense(ks[13], (L, FOX_W, D), FOX_W),
        'w_up_gla': dense(ks[14], (L, GLA_V_W, D), GLA_V_W),
        'w_up_mla': dense(ks[15], (L, MLA_W, D), MLA_W),
        'w_out': dense(ks[16], (L, D, D), D),
        'g_xa': gain(ks[17], (L, D)),
        'g_mem': gain(ks[18], (L, D)),
        'w_xq': dense(ks[19], (L, D, XA_W), D),
        'w_xkv': dense(ks[20], (L, D, 2 * XA_W), D),
        'w_xo': dense(ks[21], (L, XA_W, D), XA_W),
        'g_mlp': gain(ks[22], (L, D)),
        'w_mlp1': dense(ks[23], (L, D, D_FF), D),
        'w_mlp2': dense(ks[24], (L, D_FF, D), D_FF),
        'g_final': gain(ks[25], (D,)),
    }


def _fwd_reference(x, mem, g_mix, w_in, b_fox_forget, w_gla_gate, b_gla_gate, g_gla_out,
              g_mla_q, w_mla_uq, g_mla_kv, w_mla_ukv, b_branch_gate,
              w_up_fox, w_up_gla, w_up_mla, w_out, g_xa, g_mem, w_xq, w_xkv, w_xo,
              g_mlp, w_mlp1, w_mlp2, g_final):
    b, s, d = x.shape
    pos = jnp.arange(s)
    for l in range(DEPTH):
        h = rms_norm(x, g_mix[l])
        z = h @ w_in[l]
        (fq, fk, fv, ff, gq, gk, gv, glow, gr, mq, mkv, mkr, zg) = split_cols(z, IN_SIZES)
        o_fox = fox_branch(fq, fk, fv, ff, b_fox_forget[l])
        o_gla = gla_branch(gq, gk, gv, glow, gr, w_gla_gate[l], b_gla_gate[l], g_gla_out[l])
        o_mla = mla_branch(mq, mkv, mkr, g_mla_q[l], w_mla_uq[l], g_mla_kv[l], w_mla_ukv[l], pos)
        gates = jax.nn.sigmoid((zg + b_branch_gate[l]).astype(F32)).astype(x.dtype)
        gates = gates.reshape(b, s, N_BRANCH, d)
        y = (gates[:, :, 0] * (o_fox @ w_up_fox[l])
             + gates[:, :, 1] * (o_gla @ w_up_gla[l])
             + gates[:, :, 2] * (o_mla @ w_up_mla[l]))
        x = x + y @ w_out[l]
        x = x + memory_cross_attention(rms_norm(x, g_xa[l]), rms_norm(mem, g_mem[l]),
                                       w_xq[l], w_xkv[l], w_xo[l])
        hm = rms_norm(x, g_mlp[l])
        x = x + jnp.square(jax.nn.relu(hm @ w_mlp1[l])) @ w_mlp2[l]
    return rms_norm(x, g_final)


import jax as _jax
import jax.numpy as _jnp

TWIN_FORMAT = 'train_step'
FWD_PARAMS = ['x', 'mem', 'g_mix', 'w_in', 'b_fox_forget', 'w_gla_gate', 'b_gla_gate', 'g_gla_out', 'g_mla_q', 'w_mla_uq', 'g_mla_kv', 'w_mla_ukv', 'b_branch_gate', 'w_up_fox', 'w_up_gla', 'w_up_mla', 'w_out', 'g_xa', 'g_mem', 'w_xq', 'w_xkv', 'w_xo', 'g_mlp', 'w_mlp1', 'w_mlp2', 'g_final']
TWIN_WEIGHTS = ['g_mix', 'w_in', 'b_fox_forget', 'w_gla_gate', 'b_gla_gate', 'g_gla_out', 'g_mla_q', 'w_mla_uq', 'g_mla_kv', 'w_mla_ukv', 'b_branch_gate', 'w_up_fox', 'w_up_gla', 'w_up_mla', 'w_out', 'g_xa', 'g_mem', 'w_xq', 'w_xkv', 'w_xo', 'g_mlp', 'w_mlp1', 'w_mlp2', 'g_final']
TWIN_DIFF_INPUT = 'x'
TWIN_INPUTS = ['x', 'mem', 'g_mix', 'w_in', 'b_fox_forget', 'w_gla_gate', 'b_gla_gate', 'g_gla_out', 'g_mla_q', 'w_mla_uq', 'g_mla_kv', 'w_mla_ukv', 'b_branch_gate', 'w_up_fox', 'w_up_gla', 'w_up_mla', 'w_out', 'g_xa', 'g_mem', 'w_xq', 'w_xkv', 'w_xo', 'g_mlp', 'w_mlp1', 'w_mlp2', 'g_final', 'loss_target', 'm_g_mix', 'm_w_in', 'm_b_fox_forget', 'm_w_gla_gate', 'm_b_gla_gate', 'm_g_gla_out', 'm_g_mla_q', 'm_w_mla_uq', 'm_g_mla_kv', 'm_w_mla_ukv', 'm_b_branch_gate', 'm_w_up_fox', 'm_w_up_gla', 'm_w_up_mla', 'm_w_out', 'm_g_xa', 'm_g_mem', 'm_w_xq', 'm_w_xkv', 'm_w_xo', 'm_g_mlp', 'm_w_mlp1', 'm_w_mlp2', 'm_g_final', 'v_g_mix', 'v_w_in', 'v_b_fox_forget', 'v_w_gla_gate', 'v_b_gla_gate', 'v_g_gla_out', 'v_g_mla_q', 'v_w_mla_uq', 'v_g_mla_kv', 'v_w_mla_ukv', 'v_b_branch_gate', 'v_w_up_fox', 'v_w_up_gla', 'v_w_up_mla', 'v_w_out', 'v_g_xa', 'v_g_mem', 'v_w_xq', 'v_w_xkv', 'v_w_xo', 'v_g_mlp', 'v_w_mlp1', 'v_w_mlp2', 'v_g_final']
TWIN_OUTPUTS = ['loss', 'grad_x', 'grad_g_mix', 'grad_w_in', 'grad_b_fox_forget', 'grad_w_gla_gate', 'grad_b_gla_gate', 'grad_g_gla_out', 'grad_g_mla_q', 'grad_w_mla_uq', 'grad_g_mla_kv', 'grad_w_mla_ukv', 'grad_b_branch_gate', 'grad_w_up_fox', 'grad_w_up_gla', 'grad_w_up_mla', 'grad_w_out', 'grad_g_xa', 'grad_g_mem', 'grad_w_xq', 'grad_w_xkv', 'grad_w_xo', 'grad_g_mlp', 'grad_w_mlp1', 'grad_w_mlp2', 'grad_g_final', 'delta_g_mix', 'delta_w_in', 'delta_b_fox_forget', 'delta_w_gla_gate', 'delta_b_gla_gate', 'delta_g_gla_out', 'delta_g_mla_q', 'delta_w_mla_uq', 'delta_g_mla_kv', 'delta_w_mla_ukv', 'delta_b_branch_gate', 'delta_w_up_fox', 'delta_w_up_gla', 'delta_w_up_mla', 'delta_w_out', 'delta_g_xa', 'delta_g_mem', 'delta_w_xq', 'delta_w_xkv', 'delta_w_xo', 'delta_g_mlp', 'delta_w_mlp1', 'delta_w_mlp2', 'delta_g_final', 'new_m_g_mix', 'new_m_w_in', 'new_m_b_fox_forget', 'new_m_w_gla_gate', 'new_m_b_gla_gate', 'new_m_g_gla_out', 'new_m_g_mla_q', 'new_m_w_mla_uq', 'new_m_g_mla_kv', 'new_m_w_mla_ukv', 'new_m_b_branch_gate', 'new_m_w_up_fox', 'new_m_w_up_gla', 'new_m_w_up_mla', 'new_m_w_out', 'new_m_g_xa', 'new_m_g_mem', 'new_m_w_xq', 'new_m_w_xkv', 'new_m_w_xo', 'new_m_g_mlp', 'new_m_w_mlp1', 'new_m_w_mlp2', 'new_m_g_final', 'new_v_g_mix', 'new_v_w_in', 'new_v_b_fox_forget', 'new_v_w_gla_gate', 'new_v_b_gla_gate', 'new_v_g_gla_out', 'new_v_g_mla_q', 'new_v_w_mla_uq', 'new_v_g_mla_kv', 'new_v_w_mla_ukv', 'new_v_b_branch_gate', 'new_v_w_up_fox', 'new_v_w_up_gla', 'new_v_w_up_mla', 'new_v_w_out', 'new_v_g_xa', 'new_v_g_mem', 'new_v_w_xq', 'new_v_w_xkv', 'new_v_w_xo', 'new_v_g_mlp', 'new_v_w_mlp1', 'new_v_w_mlp2', 'new_v_g_final']
TWIN_LEAF_KINDS = {'loss': 'loss', 'grad_x': 'grad_x', 'grad_g_mix': 'grad_w', 'grad_w_in': 'grad_w', 'grad_b_fox_forget': 'grad_w', 'grad_w_gla_gate': 'grad_w', 'grad_b_gla_gate': 'grad_w', 'grad_g_gla_out': 'grad_w', 'grad_g_mla_q': 'grad_w', 'grad_w_mla_uq': 'grad_w', 'grad_g_mla_kv': 'grad_w', 'grad_w_mla_ukv': 'grad_w', 'grad_b_branch_gate': 'grad_w', 'grad_w_up_fox': 'grad_w', 'grad_w_up_gla': 'grad_w', 'grad_w_up_mla': 'grad_w', 'grad_w_out': 'grad_w', 'grad_g_xa': 'grad_w', 'grad_g_mem': 'grad_w', 'grad_w_xq': 'grad_w', 'grad_w_xkv': 'grad_w', 'grad_w_xo': 'grad_w', 'grad_g_mlp': 'grad_w', 'grad_w_mlp1': 'grad_w', 'grad_w_mlp2': 'grad_w', 'grad_g_final': 'grad_w', 'delta_g_mix': 'delta_w', 'delta_w_in': 'delta_w', 'delta_b_fox_forget': 'delta_w', 'delta_w_gla_gate': 'delta_w', 'delta_b_gla_gate': 'delta_w', 'delta_g_gla_out': 'delta_w', 'delta_g_mla_q': 'delta_w', 'delta_w_mla_uq': 'delta_w', 'delta_g_mla_kv': 'delta_w', 'delta_w_mla_ukv': 'delta_w', 'delta_b_branch_gate': 'delta_w', 'delta_w_up_fox': 'delta_w', 'delta_w_up_gla': 'delta_w', 'delta_w_up_mla': 'delta_w', 'delta_w_out': 'delta_w', 'delta_g_xa': 'delta_w', 'delta_g_mem': 'delta_w', 'delta_w_xq': 'delta_w', 'delta_w_xkv': 'delta_w', 'delta_w_xo': 'delta_w', 'delta_g_mlp': 'delta_w', 'delta_w_mlp1': 'delta_w', 'delta_w_mlp2': 'delta_w', 'delta_g_final': 'delta_w', 'new_m_g_mix': 'new_m', 'new_m_w_in': 'new_m', 'new_m_b_fox_forget': 'new_m', 'new_m_w_gla_gate': 'new_m', 'new_m_b_gla_gate': 'new_m', 'new_m_g_gla_out': 'new_m', 'new_m_g_mla_q': 'new_m', 'new_m_w_mla_uq': 'new_m', 'new_m_g_mla_kv': 'new_m', 'new_m_w_mla_ukv': 'new_m', 'new_m_b_branch_gate': 'new_m', 'new_m_w_up_fox': 'new_m', 'new_m_w_up_gla': 'new_m', 'new_m_w_up_mla': 'new_m', 'new_m_w_out': 'new_m', 'new_m_g_xa': 'new_m', 'new_m_g_mem': 'new_m', 'new_m_w_xq': 'new_m', 'new_m_w_xkv': 'new_m', 'new_m_w_xo': 'new_m', 'new_m_g_mlp': 'new_m', 'new_m_w_mlp1': 'new_m', 'new_m_w_mlp2': 'new_m', 'new_m_g_final': 'new_m', 'new_v_g_mix': 'new_v', 'new_v_w_in': 'new_v', 'new_v_b_fox_forget': 'new_v', 'new_v_w_gla_gate': 'new_v', 'new_v_b_gla_gate': 'new_v', 'new_v_g_gla_out': 'new_v', 'new_v_g_mla_q': 'new_v', 'new_v_w_mla_uq': 'new_v', 'new_v_g_mla_kv': 'new_v', 'new_v_w_mla_ukv': 'new_v', 'new_v_b_branch_gate': 'new_v', 'new_v_w_up_fox': 'new_v', 'new_v_w_up_gla': 'new_v', 'new_v_w_up_mla': 'new_v', 'new_v_w_out': 'new_v', 'new_v_g_xa': 'new_v', 'new_v_g_mem': 'new_v', 'new_v_w_xq': 'new_v', 'new_v_w_xkv': 'new_v', 'new_v_w_xo': 'new_v', 'new_v_g_mlp': 'new_v', 'new_v_w_mlp1': 'new_v', 'new_v_w_mlp2': 'new_v', 'new_v_g_final': 'new_v'}


def _forward(args):
    return _fwd_reference(*[args[k] for k in FWD_PARAMS])


def _output_shape():
    out = _jax.eval_shape(lambda: _forward(_fwd_setup_inputs(0)))
    return out.shape, out.dtype

N_MICROBATCH = 1
ADAM_LR = 0.001
ADAM_B1 = 0.9
ADAM_B2 = 0.999
ADAM_EPS = 1e-08
ADAM_WD = 0.01
ADAM_STEP = 10
PER_EXAMPLE_BATCH_AXIS = {'x': 0, 'mem': 0, 'loss_target': 0}
SHARED_INPUTS = []
_WEIGHT_DTYPES = {'g_mix': _jnp.float32, 'w_in': _jnp.float32, 'b_fox_forget': _jnp.float32, 'w_gla_gate': _jnp.float32, 'b_gla_gate': _jnp.float32, 'g_gla_out': _jnp.float32, 'g_mla_q': _jnp.float32, 'w_mla_uq': _jnp.float32, 'g_mla_kv': _jnp.float32, 'w_mla_ukv': _jnp.float32, 'b_branch_gate': _jnp.float32, 'w_up_fox': _jnp.float32, 'w_up_gla': _jnp.float32, 'w_up_mla': _jnp.float32, 'w_out': _jnp.float32, 'g_xa': _jnp.float32, 'g_mem': _jnp.float32, 'w_xq': _jnp.float32, 'w_xkv': _jnp.float32, 'w_xo': _jnp.float32, 'g_mlp': _jnp.float32, 'w_mlp1': _jnp.float32, 'w_mlp2': _jnp.float32, 'g_final': _jnp.float32}
MOMENT_SCALE = {'g_mix': 1.355814e-01, 'w_in': 5.630708e-02, 'b_fox_forget': 3.259728e-01, 'w_gla_gate': 1.443271e-02, 'b_gla_gate': 4.899403e-02, 'g_gla_out': 1.457157e-01, 'g_mla_q': 2.487591e-02, 'w_mla_uq': 2.077453e-02, 'g_mla_kv': 6.785739e-02, 'w_mla_ukv': 3.153607e-02, 'b_branch_gate': 1.812974e-02, 'w_up_fox': 5.930535e-02, 'w_up_gla': 5.063858e-02, 'w_up_mla': 1.916117e-02, 'w_out': 8.031230e-02, 'g_xa': 1.680049e-02, 'g_mem': 2.498533e-02, 'w_xq': 2.325692e-02, 'w_xkv': 2.421781e-02, 'w_xo': 1.747615e-02, 'g_mlp': 1.457938e-01, 'w_mlp1': 7.375307e-02, 'w_mlp2': 1.448031e-01, 'g_final': 3.260499e+01}


def _to_microbatches(a, axis):
    t = _jnp.moveaxis(a, axis, 0)
    t = t.reshape((N_MICROBATCH, t.shape[0] // N_MICROBATCH) + t.shape[1:])
    return _jnp.moveaxis(t, 1, axis + 1)


def setup_inputs(seed: int = 0) -> dict:
    inp = _fwd_setup_inputs(seed)
    key = _jax.random.fold_in(_jax.random.key(seed), 7919)
    shape, _ = _output_shape()
    out = dict(inp)
    out["loss_target"] = _jax.random.normal(_jax.random.fold_in(key, 0), shape, _jnp.float32)
    for i, name in enumerate(TWIN_WEIGHTS):
        w = inp[name].astype(_jnp.float32)
        if MOMENT_SCALE is None:
            s = _jnp.sqrt(_jnp.mean(_jnp.square(w)) + 1e-30)
        else:
            s = MOMENT_SCALE[name]
        km, kv = _jax.random.split(_jax.random.fold_in(key, i + 1))
        out[name] = w
        out["m_" + name] = s * _jax.random.normal(km, w.shape, _jnp.float32)
        out["v_" + name] = (s * s) * _jax.random.uniform(kv, w.shape, _jnp.float32, 0.5, 1.5)
    if N_MICROBATCH > 1:
        for name, axis in PER_EXAMPLE_BATCH_AXIS.items():
            out[name] = _to_microbatches(out[name], axis)
    return {'x': out['x'], 'mem': out['mem'], 'g_mix': out['g_mix'], 'w_in': out['w_in'], 'b_fox_forget': out['b_fox_forget'], 'w_gla_gate': out['w_gla_gate'], 'b_gla_gate': out['b_gla_gate'], 'g_gla_out': out['g_gla_out'], 'g_mla_q': out['g_mla_q'], 'w_mla_uq': out['w_mla_uq'], 'g_mla_kv': out['g_mla_kv'], 'w_mla_ukv': out['w_mla_ukv'], 'b_branch_gate': out['b_branch_gate'], 'w_up_fox': out['w_up_fox'], 'w_up_gla': out['w_up_gla'], 'w_up_mla': out['w_up_mla'], 'w_out': out['w_out'], 'g_xa': out['g_xa'], 'g_mem': out['g_mem'], 'w_xq': out['w_xq'], 'w_xkv': out['w_xkv'], 'w_xo': out['w_xo'], 'g_mlp': out['g_mlp'], 'w_mlp1': out['w_mlp1'], 'w_mlp2': out['w_mlp2'], 'g_final': out['g_final'], 'loss_target': out['loss_target'], 'm_g_mix': out['m_g_mix'], 'm_w_in': out['m_w_in'], 'm_b_fox_forget': out['m_b_fox_forget'], 'm_w_gla_gate': out['m_w_gla_gate'], 'm_b_gla_gate': out['m_b_gla_gate'], 'm_g_gla_out': out['m_g_gla_out'], 'm_g_mla_q': out['m_g_mla_q'], 'm_w_mla_uq': out['m_w_mla_uq'], 'm_g_mla_kv': out['m_g_mla_kv'], 'm_w_mla_ukv': out['m_w_mla_ukv'], 'm_b_branch_gate': out['m_b_branch_gate'], 'm_w_up_fox': out['m_w_up_fox'], 'm_w_up_gla': out['m_w_up_gla'], 'm_w_up_mla': out['m_w_up_mla'], 'm_w_out': out['m_w_out'], 'm_g_xa': out['m_g_xa'], 'm_g_mem': out['m_g_mem'], 'm_w_xq': out['m_w_xq'], 'm_w_xkv': out['m_w_xkv'], 'm_w_xo': out['m_w_xo'], 'm_g_mlp': out['m_g_mlp'], 'm_w_mlp1': out['m_w_mlp1'], 'm_w_mlp2': out['m_w_mlp2'], 'm_g_final': out['m_g_final'], 'v_g_mix': out['v_g_mix'], 'v_w_in': out['v_w_in'], 'v_b_fox_forget': out['v_b_fox_forget'], 'v_w_gla_gate': out['v_w_gla_gate'], 'v_b_gla_gate': out['v_b_gla_gate'], 'v_g_gla_out': out['v_g_gla_out'], 'v_g_mla_q': out['v_g_mla_q'], 'v_w_mla_uq': out['v_w_mla_uq'], 'v_g_mla_kv': out['v_g_mla_kv'], 'v_w_mla_ukv': out['v_w_mla_ukv'], 'v_b_branch_gate': out['v_b_branch_gate'], 'v_w_up_fox': out['v_w_up_fox'], 'v_w_up_gla': out['v_w_up_gla'], 'v_w_up_mla': out['v_w_up_mla'], 'v_w_out': out['v_w_out'], 'v_g_xa': out['v_g_xa'], 'v_g_mem': out['v_g_mem'], 'v_w_xq': out['v_w_xq'], 'v_w_xkv': out['v_w_xkv'], 'v_w_xo': out['v_w_xo'], 'v_g_mlp': out['v_g_mlp'], 'v_w_mlp1': out['v_w_mlp1'], 'v_w_mlp2': out['v_w_mlp2'], 'v_g_final': out['v_g_final']}


def _loss(weights, diff, rest, loss_target):
    with _jax.named_scope("forward"):
        args = {**rest, TWIN_DIFF_INPUT: diff, **{k: w.astype(_WEIGHT_DTYPES[k]) for k, w in weights.items()}}
        y = _forward(args)
    with _jax.named_scope("loss_head"):
        err = _jnp.square(y.astype(_jnp.float32) - loss_target)
        return 0.5 * _jnp.sum(_jnp.mean(err, axis=-1)) if err.ndim else 0.5 * err


def _adamw(w, g, m, v):
    m = ADAM_B1 * m + (1.0 - ADAM_B1) * g
    v = ADAM_B2 * v + (1.0 - ADAM_B2) * _jnp.square(g)
    m_hat = m / (1.0 - ADAM_B1 ** ADAM_STEP)
    v_hat = v / (1.0 - ADAM_B2 ** ADAM_STEP)
    delta = -ADAM_LR * (m_hat / (_jnp.sqrt(v_hat) + ADAM_EPS) + ADAM_WD * w)
    return delta, m, v


def reference(x, mem, g_mix, w_in, b_fox_forget, w_gla_gate, b_gla_gate, g_gla_out, g_mla_q, w_mla_uq, g_mla_kv, w_mla_ukv, b_branch_gate, w_up_fox, w_up_gla, w_up_mla, w_out, g_xa, g_mem, w_xq, w_xkv, w_xo, g_mlp, w_mlp1, w_mlp2, g_final, loss_target, m_g_mix, m_w_in, m_b_fox_forget, m_w_gla_gate, m_b_gla_gate, m_g_gla_out, m_g_mla_q, m_w_mla_uq, m_g_mla_kv, m_w_mla_ukv, m_b_branch_gate, m_w_up_fox, m_w_up_gla, m_w_up_mla, m_w_out, m_g_xa, m_g_mem, m_w_xq, m_w_xkv, m_w_xo, m_g_mlp, m_w_mlp1, m_w_mlp2, m_g_final, v_g_mix, v_w_in, v_b_fox_forget, v_w_gla_gate, v_b_gla_gate, v_g_gla_out, v_g_mla_q, v_w_mla_uq, v_g_mla_kv, v_w_mla_ukv, v_b_branch_gate, v_w_up_fox, v_w_up_gla, v_w_up_mla, v_w_out, v_g_xa, v_g_mem, v_w_xq, v_w_xkv, v_w_xo, v_g_mlp, v_w_mlp1, v_w_mlp2, v_g_final):
    given = dict(x=x, mem=mem, g_mix=g_mix, w_in=w_in, b_fox_forget=b_fox_forget, w_gla_gate=w_gla_gate, b_gla_gate=b_gla_gate, g_gla_out=g_gla_out, g_mla_q=g_mla_q, w_mla_uq=w_mla_uq, g_mla_kv=g_mla_kv, w_mla_ukv=w_mla_ukv, b_branch_gate=b_branch_gate, w_up_fox=w_up_fox, w_up_gla=w_up_gla, w_up_mla=w_up_mla, w_out=w_out, g_xa=g_xa, g_mem=g_mem, w_xq=w_xq, w_xkv=w_xkv, w_xo=w_xo, g_mlp=g_mlp, w_mlp1=w_mlp1, w_mlp2=w_mlp2, g_final=g_final, loss_target=loss_target, m_g_mix=m_g_mix, m_w_in=m_w_in, m_b_fox_forget=m_b_fox_forget, m_w_gla_gate=m_w_gla_gate, m_b_gla_gate=m_b_gla_gate, m_g_gla_out=m_g_gla_out, m_g_mla_q=m_g_mla_q, m_w_mla_uq=m_w_mla_uq, m_g_mla_kv=m_g_mla_kv, m_w_mla_ukv=m_w_mla_ukv, m_b_branch_gate=m_b_branch_gate, m_w_up_fox=m_w_up_fox, m_w_up_gla=m_w_up_gla, m_w_up_mla=m_w_up_mla, m_w_out=m_w_out, m_g_xa=m_g_xa, m_g_mem=m_g_mem, m_w_xq=m_w_xq, m_w_xkv=m_w_xkv, m_w_xo=m_w_xo, m_g_mlp=m_g_mlp, m_w_mlp1=m_w_mlp1, m_w_mlp2=m_w_mlp2, m_g_final=m_g_final, v_g_mix=v_g_mix, v_w_in=v_w_in, v_b_fox_forget=v_b_fox_forget, v_w_gla_gate=v_w_gla_gate, v_b_gla_gate=v_b_gla_gate, v_g_gla_out=v_g_gla_out, v_g_mla_q=v_g_mla_q, v_w_mla_uq=v_w_mla_uq, v_g_mla_kv=v_g_mla_kv, v_w_mla_ukv=v_w_mla_ukv, v_b_branch_gate=v_b_branch_gate, v_w_up_fox=v_w_up_fox, v_w_up_gla=v_w_up_gla, v_w_up_mla=v_w_up_mla, v_w_out=v_w_out, v_g_xa=v_g_xa, v_g_mem=v_g_mem, v_w_xq=v_w_xq, v_w_xkv=v_w_xkv, v_w_xo=v_w_xo, v_g_mlp=v_g_mlp, v_w_mlp1=v_w_mlp1, v_w_mlp2=v_w_mlp2, v_g_final=v_g_final)
    weights = {n: given[n] for n in TWIN_WEIGHTS}
    shared = {n: given[n] for n in SHARED_INPUTS}
    per_example = {n: given[n] for n in ['x', 'mem']}
    grad_fn = _jax.value_and_grad(_loss, argnums=(0, 1))

    def one_microbatch(ex, loss_target):
        ex = dict(ex)
        diff = ex.pop(TWIN_DIFF_INPUT)
        return grad_fn(weights, diff, {**shared, **ex}, loss_target)

    if N_MICROBATCH == 1:
        loss, (grad_w, grad_x) = one_microbatch(per_example, given["loss_target"])
    else:
        def body(carry, xs):
            loss_sum, grad_sum = carry
            l_k, (gw_k, gx_k) = one_microbatch(xs[0], xs[1])
            with _jax.named_scope("update"):
                return (loss_sum + l_k, _jax.tree.map(_jnp.add, grad_sum, gw_k)), gx_k

        init = (_jnp.zeros((), _jnp.float32), _jax.tree.map(_jnp.zeros_like, weights))
        (loss, grad_w), grad_x = _jax.lax.scan(body, init, (per_example, given["loss_target"]))
    with _jax.named_scope("update"):
        delta_w, new_m, new_v = {}, {}, {}
        for n in TWIN_WEIGHTS:
            delta_w[n], new_m[n], new_v[n] = _adamw(weights[n], grad_w[n], given["m_" + n], given["v_" + n])
    return (loss, grad_x, *[grad_w[n] for n in TWIN_WEIGHTS], *[delta_w[n] for n in TWIN_WEIGHTS],
            *[new_m[n] for n in TWIN_WEIGHTS], *[new_v[n] for n in TWIN_WEIGHTS])
```

```python
import functools
import math

import jax
import jax.numpy as jnp
from jax import lax
from jax.experimental import pallas as pl
from jax.experimental.pallas import tpu as pltpu

F32 = jnp.float32
BF16 = jnp.bfloat16
MESH = pl.DeviceIdType.MESH

D_MODEL = 1024
DEPTH = 2
CHUNK = 64
EPS = 1e-6
FOX_HEADS, FOX_HD = 4, 64
GLA_HEADS, GLA_DK, GLA_DV, GLA_RANK, GLA_TAU = 4, 64, 128, 16, 16.0
MLA_HEADS, MLA_Q_RANK, MLA_KV_RANK, MLA_NOPE, MLA_ROPE, MLA_VD = 4, 256, 128, 64, 32, 64
ROPE_BASE = 10000.0
XA_HEADS, XA_HD = 4, 128
D_FF = 4 * D_MODEL
IN_SIZES = (256, 256, 256, 4, 256, 256, 512, 16, 512, 256, 128, 32, 3072)
N_IN = sum(IN_SIZES)

ADAM_LR, ADAM_B1, ADAM_B2, ADAM_EPS, ADAM_WD, ADAM_STEP = 0.001, 0.9, 0.999, 1e-08, 0.01, 10

N_CHIPS = 4
N_DEV = 8
LANES = 128
VMEM_LIMIT = 48 * 1024 * 1024
MASK_VALUE = -1e30

BIG = (('w_in', 2), ('w_gla_gate', 2), ('w_mla_uq', 2), ('w_mla_ukv', 2), ('w_up_fox', 2), ('w_up_gla', 2),
       ('w_up_mla', 2), ('w_out', 1), ('w_xq', 1), ('w_xkv', 1), ('w_xo', 2), ('w_mlp1', 2), ('w_mlp2', 1))
SMALL = ('g_mix', 'b_fox_forget', 'b_gla_gate', 'g_gla_out', 'g_mla_q', 'g_mla_kv', 'b_branch_gate',
         'g_xa', 'g_mem', 'g_mlp', 'g_final')
ORDER = ('g_mix', 'w_in', 'b_fox_forget', 'w_gla_gate', 'b_gla_gate', 'g_gla_out', 'g_mla_q', 'w_mla_uq',
         'g_mla_kv', 'w_mla_ukv', 'b_branch_gate', 'w_up_fox', 'w_up_gla', 'w_up_mla', 'w_out', 'g_xa', 'g_mem',
         'w_xq', 'w_xkv', 'w_xo', 'g_mlp', 'w_mlp1', 'w_mlp2', 'g_final')
PACK_W = 1024
PACK_ROW_ALIGN = 256


def _params(*sem):
    return pltpu.CompilerParams(dimension_semantics=sem, vmem_limit_bytes=VMEM_LIMIT)


def _sig(x):
    return 1.0 / (1.0 + jnp.exp(-x))


def _logsig(x):
    return jnp.minimum(x, 0.0) - jnp.log(1.0 + jnp.exp(-jnp.abs(x)))


NN = (((1,), (0,)), ((), ()))
NT = (((1,), (1,)), ((), ()))
TN = (((0,), (0,)), ((), ()))


def _dot(a, b, dims=NN):
    return lax.dot_general(a, b, dims, preferred_element_type=F32)


def _tri_dot(tri, x):
    hi = x.astype(BF16)
    r1 = x - hi.astype(F32)
    mid = r1.astype(BF16)
    lo = (r1 - mid.astype(F32)).astype(BF16)
    return _dot(tri, hi) + _dot(tri, mid) + _dot(tri, lo)


def _mm(a, b, *, mode, out_dtype, name, tm=512, tn=512, norm_g=None, emit_norm=False, a_fn=None, extras=(),
        epilogue=None):
    if mode == 'tn':
        k, m = a.shape
    else:
        m, k = a.shape
    n = b.shape[0] if mode == 'nt' else b.shape[1]
    assert (b.shape[1] if mode == 'nt' else b.shape[0]) == k, (name, a.shape, b.shape)
    tm, tn = min(tm, m), min(tn, n)
    assert m % tm == 0 and n % tn == 0, (name, m, n, tm, tn)
    a_spec = pl.BlockSpec((k, tm), lambda i, j: (0, i)) if mode == 'tn' else pl.BlockSpec((tm, k), lambda i, j: (i, 0))
    b_spec = pl.BlockSpec((tn, k), lambda i, j: (j, 0)) if mode == 'nt' else pl.BlockSpec((k, tn), lambda i, j: (0, j))
    dims = {'nn': NN, 'nt': NT, 'tn': TN}[mode]
    has_norm = norm_g is not None
    assert not (has_norm and mode != 'nn')
    n_ex = len(extras)

    def body(*refs):
        a_ref, b_ref = refs[0], refs[1]
        pos = 2
        g_ref = None
        if has_norm:
            g_ref = refs[pos]
            pos += 1
        ex_refs = refs[pos:pos + n_ex]
        pos += n_ex
        o_ref = refs[pos]
        pos += 1
        h_ref = None
        if emit_norm:
            h_ref = refs[pos]
            pos += 1
        if has_norm:
            an_ref = refs[pos]

            @pl.when(pl.program_id(1) == 0)
            def _():
                xf = a_ref[...].astype(F32)
                y = xf * lax.rsqrt(jnp.mean(xf * xf, axis=-1, keepdims=True) + EPS) * g_ref[...]
                an_ref[...] = y.astype(BF16)
                if emit_norm:
                    h_ref[...] = y.astype(BF16)

            av = an_ref[...]
        else:
            av = a_ref[...]
            if a_fn is not None:
                av = a_fn(av)
            av = av.astype(BF16)
        acc = _dot(av, b_ref[...].astype(BF16), dims)
        if epilogue is not None:
            acc = epilogue(acc, *[r[...] for r in ex_refs])
        o_ref[...] = acc.astype(out_dtype)

    in_specs = [a_spec, b_spec]
    args = [a, b]
    if has_norm:
        in_specs.append(pl.BlockSpec((1, k), lambda i, j: (0, 0)))
        args.append(norm_g)
    for arr, blk, imap in extras:
        in_specs.append(pl.BlockSpec(blk, imap))
        args.append(arr)
    out_shape = [jax.ShapeDtypeStruct((m, n), out_dtype)]
    out_specs = [pl.BlockSpec((tm, tn), lambda i, j: (i, j))]
    if emit_norm:
        out_shape.append(jax.ShapeDtypeStruct((m, k), BF16))
        out_specs.append(pl.BlockSpec((tm, k), lambda i, j: (i, 0)))
    scratch = [pltpu.VMEM((tm, k), BF16)] if has_norm else []
    res = pl.pallas_call(
        body, name=name, grid=(m // tm, n // tn), in_specs=in_specs, out_specs=out_specs, out_shape=out_shape,
        scratch_shapes=scratch, compiler_params=_params('arbitrary', 'arbitrary'))(*args)
    return res if emit_norm else res[0]


def _mn(tm=512, tn=512, col_off=0):
    return (tm, tn), (lambda i, j: (i, j + col_off))


def _nvec(tn=512, col_off=0):
    return (1, tn), (lambda i, j: (0, j + col_off))


def _rowwise(fn, rows, consts, outs, sums=(), *, name, ts=256):
    r = rows[0].shape[0]
    ts = min(ts, r)
    assert r % ts == 0, (name, r, ts)
    nr, nc, no, ns = len(rows), len(consts), len(outs), len(sums)

    def body(*refs):
        vals = fn(*[x[...] for x in refs[:nr + nc]])
        for q in range(no):
            refs[nr + nc + q][...] = vals[q].astype(outs[q][1])
        if ns:
            @pl.when(pl.program_id(0) == 0)
            def _():
                for q in range(ns):
                    refs[nr + nc + no + q][...] = jnp.zeros((1, sums[q]), F32)

            for q in range(ns):
                refs[nr + nc + no + q][...] += jnp.sum(vals[no + q].astype(F32), axis=0, keepdims=True)

    in_specs = [pl.BlockSpec((ts, x.shape[1]), lambda i: (i, 0)) for x in rows]
    in_specs += [pl.BlockSpec(x.shape, lambda i, nd=x.ndim: (0,) * nd) for x in consts]
    out_specs = [pl.BlockSpec((ts, w), lambda i: (i, 0)) for w, _ in outs]
    out_specs += [pl.BlockSpec((1, w), lambda i: (0, 0)) for w in sums]
    out_shape = [jax.ShapeDtypeStruct((r, w), dt) for w, dt in outs]
    out_shape += [jax.ShapeDtypeStruct((1, w), F32) for w in sums]
    return pl.pallas_call(body, name=name, grid=(r // ts,), in_specs=in_specs, out_specs=out_specs,
                          out_shape=out_shape, compiler_params=_params('arbitrary'))(*rows, *consts)


def _cumsum_rows(x, *, reverse, name, bs=256):
    s, w = x.shape
    bs = min(bs, s)
    nb = s // bs

    def body(x_ref, o_ref, carry):
        @pl.when(pl.program_id(0) == 0)
        def _():
            carry[...] = jnp.zeros_like(carry)

        r = lax.broadcasted_iota(jnp.int32, (bs, bs), 0)
        c = lax.broadcasted_iota(jnp.int32, (bs, bs), 1)
        tri = jnp.where((c >= r) if reverse else (c <= r), 1.0, 0.0).astype(BF16)
        xv = x_ref[...]
        o_ref[...] = _tri_dot(tri, xv) + carry[...]
        carry[...] += jnp.sum(xv, axis=0, keepdims=True)

    imap = (lambda i: (nb - 1 - i, 0)) if reverse else (lambda i: (i, 0))
    return pl.pallas_call(body, name=name, grid=(nb,), in_specs=[pl.BlockSpec((bs, w), imap)],
                          out_specs=pl.BlockSpec((bs, w), imap), out_shape=jax.ShapeDtypeStruct((s, w), F32),
                          scratch_shapes=[pltpu.VMEM((1, w), F32)], compiler_params=_params('arbitrary'))(x)


def _mask(mode, i, j, bq, bk):
    qpos = i * bq + lax.broadcasted_iota(jnp.int32, (bq, bk), 0)
    kpos = j * bk + lax.broadcasted_iota(jnp.int32, (bq, bk), 1)
    if mode == 'causal':
        return kpos <= qpos
    return kpos < (jnp.right_shift(qpos, int(math.log2(CHUNK))) + 1) * CHUNK


def _attn_fwd(q, k, v, cq, ck, *, scale, mode, name, blk=512):
    h, s, dk = q.shape
    t, dv = k.shape[1], v.shape[2]
    bq = min(blk, s)
    bk = min(blk, t)
    nq, nk = s // bq, t // bk
    bias = cq is not None
    tri = mode != 'full'
    assert not tri or (bq == bk and bq % CHUNK == 0)

    def body(*refs):
        if bias:
            q_ref, k_ref, v_ref, cq_ref, ck_ref, o_ref, lse_ref, m_s, l_s, acc_s = refs
        else:
            q_ref, k_ref, v_ref, o_ref, lse_ref, m_s, l_s, acc_s = refs
        i, j = pl.program_id(1), pl.program_id(2)

        @pl.when(j == 0)
        def _():
            m_s[...] = jnp.full_like(m_s, MASK_VALUE)
            l_s[...] = jnp.zeros_like(l_s)
            acc_s[...] = jnp.zeros_like(acc_s)

        def compute():
            sc = _dot(q_ref[0], k_ref[0], NT) * scale
            if bias:
                sc = sc + cq_ref[0] - ck_ref[0]
            if tri:
                sc = jnp.where(_mask(mode, i, j, bq, bk), sc, MASK_VALUE)
            m_prev = m_s[...]
            m_new = jnp.maximum(m_prev, jnp.max(sc, axis=1, keepdims=True))
            alpha = jnp.exp(m_prev - m_new)
            p = jnp.exp(sc - m_new)
            l_s[...] = alpha * l_s[...] + jnp.sum(p, axis=1, keepdims=True)
            acc_s[...] = alpha * acc_s[...] + _dot(p.astype(BF16), v_ref[0])
            m_s[...] = m_new

        if tri:
            pl.when(j <= i)(compute)
        else:
            compute()

        @pl.when(j == nk - 1)
        def _():
            o_ref[0] = (acc_s[...] / l_s[...]).astype(o_ref.dtype)
            lse_ref[0] = m_s[...] + jnp.log(l_s[...])

    kv_j = (lambda hh, i, j: (hh, jnp.minimum(i, j), 0)) if tri else (lambda hh, i, j: (hh, j, 0))
    in_specs = [pl.BlockSpec((1, bq, dk), lambda hh, i, j: (hh, i, 0)), pl.BlockSpec((1, bk, dk), kv_j),
                pl.BlockSpec((1, bk, dv), kv_j)]
    args = [q, k, v]
    if bias:
        ck_j = (lambda hh, i, j: (hh, 0, jnp.minimum(i, j))) if tri else (lambda hh, i, j: (hh, 0, j))
        in_specs += [pl.BlockSpec((1, bq, 1), lambda hh, i, j: (hh, i, 0)), pl.BlockSpec((1, 1, bk), ck_j)]
        args += [cq, ck]
    return pl.pallas_call(
        body, name=name, grid=(h, nq, nk), in_specs=in_specs,
        out_specs=[pl.BlockSpec((1, bq, dv), lambda hh, i, j: (hh, i, 0)),
                   pl.BlockSpec((1, bq, 1), lambda hh, i, j: (hh, i, 0))],
        out_shape=[jax.ShapeDtypeStruct((h, s, dv), BF16), jax.ShapeDtypeStruct((h, s, 1), F32)],
        scratch_shapes=[pltpu.VMEM((bq, 1), F32), pltpu.VMEM((bq, 1), F32), pltpu.VMEM((bq, dv), F32)],
        compiler_params=_params('arbitrary', 'arbitrary', 'arbitrary'))(*args)


def _attn_bwd(q, k, v, o, do, lse, cq, ck, *, scale, mode, name, blk=512):
    h, s, dk = q.shape
    t, dv = k.shape[1], v.shape[2]
    bq = min(blk, s)
    bk = min(blk, t)
    nq, nk = s // bq, t // bk
    bias = cq is not None
    tri = mode != 'full'

    def body(*refs):
        if bias:
            (q_ref, k_ref, v_ref, o_ref, do_ref, lse_ref, cq_ref, ck_ref, dq_ref, dk_ref, dv_ref, dck_ref, dcq_ref,
             dk_s, dv_s, dck_s) = refs
        else:
            q_ref, k_ref, v_ref, o_ref, do_ref, lse_ref, dq_ref, dk_ref, dv_ref, dk_s, dv_s = refs
        j, i = pl.program_id(1), pl.program_id(2)

        @pl.when((j == 0) & (i == 0))
        def _():
            dq_ref[...] = jnp.zeros_like(dq_ref)
            if bias:
                dcq_ref[...] = jnp.zeros_like(dcq_ref)

        @pl.when(i == 0)
        def _():
            dk_s[...] = jnp.zeros_like(dk_s)
            dv_s[...] = jnp.zeros_like(dv_s)
            if bias:
                dck_s[...] = jnp.zeros_like(dck_s)

        def compute():
            qv, kv, vv, dov = q_ref[0], k_ref[0], v_ref[0], do_ref[0]
            sc = _dot(qv, kv, NT) * scale
            if bias:
                sc = sc + cq_ref[0] - ck_ref[0]
            if tri:
                sc = jnp.where(_mask(mode, i, j, bq, bk), sc, MASK_VALUE)
            p = jnp.exp(sc - lse_ref[0])
            dp = _dot(dov, vv, NT)
            delta = jnp.sum(dov.astype(F32) * o_ref[0].astype(F32), axis=1, keepdims=True)
            ds = p * (dp - delta)
            dsb = ds.astype(BF16)
            dv_s[...] += _dot(p.astype(BF16), dov, TN)
            dk_s[...] += scale * _dot(dsb, qv, TN)
            row0 = pl.multiple_of(i * bq, bq)
            dq_ref[0, pl.ds(row0, bq), :] += scale * _dot(dsb, kv)
            if bias:
                dck_s[...] -= jnp.sum(ds, axis=0, keepdims=True)
                dcq_ref[0, pl.ds(row0, bq), :] += jnp.sum(ds, axis=1, keepdims=True)

        if tri:
            pl.when(i >= j)(compute)
        else:
            compute()

        @pl.when(i == nq - 1)
        def _():
            dk_ref[0] = dk_s[...]
            dv_ref[0] = dv_s[...]
            if bias:
                dck_ref[0] = dck_s[...]

    q_i = (lambda hh, j, i: (hh, jnp.maximum(i, j), 0)) if tri else (lambda hh, j, i: (hh, i, 0))
    in_specs = [pl.BlockSpec((1, bq, dk), q_i), pl.BlockSpec((1, bk, dk), lambda hh, j, i: (hh, j, 0)),
                pl.BlockSpec((1, bk, dv), lambda hh, j, i: (hh, j, 0)), pl.BlockSpec((1, bq, dv), q_i),
                pl.BlockSpec((1, bq, dv), q_i), pl.BlockSpec((1, bq, 1), q_i)]
    args = [q, k, v, o, do, lse]
    out_specs = [pl.BlockSpec((1, s, dk), lambda hh, j, i: (hh, 0, 0)),
                 pl.BlockSpec((1, bk, dk), lambda hh, j, i: (hh, j, 0)),
                 pl.BlockSpec((1, bk, dv), lambda hh, j, i: (hh, j, 0))]
    out_shape = [jax.ShapeDtypeStruct((h, s, dk), F32), jax.ShapeDtypeStruct((h, t, dk), F32),
                 jax.ShapeDtypeStruct((h, t, dv), F32)]
    scratch = [pltpu.VMEM((bk, dk), F32), pltpu.VMEM((bk, dv), F32)]
    if bias:
        in_specs += [pl.BlockSpec((1, bq, 1), q_i), pl.BlockSpec((1, 1, bk), lambda hh, j, i: (hh, 0, j))]
        args += [cq, ck]
        out_specs += [pl.BlockSpec((1, 1, bk), lambda hh, j, i: (hh, 0, j)),
                      pl.BlockSpec((1, s, 1), lambda hh, j, i: (hh, 0, 0))]
        out_shape += [jax.ShapeDtypeStruct((h, 1, t), F32), jax.ShapeDtypeStruct((h, s, 1), F32)]
        scratch.append(pltpu.VMEM((1, bk), F32))
    return pl.pallas_call(body, name=name, grid=(h, nk, nq), in_specs=in_specs, out_specs=out_specs,
                          out_shape=out_shape, scratch_shapes=scratch,
                          compiler_params=_params('arbitrary', 'arbitrary', 'arbitrary'))(*args)


def _gla_chunk(la_c, k_c):
    r = lax.broadcasted_iota(jnp.int32, (CHUNK, CHUNK), 0)
    c = lax.broadcasted_iota(jnp.int32, (CHUNK, CHUNK), 1)
    tri = jnp.where(c <= r, 1.0, 0.0).astype(BF16)
    cum = _tri_dot(tri, la_c)
    end = jnp.sum(la_c, axis=0, keepdims=True)
    dec = jnp.exp(end - cum)
    return dec, k_c * dec, jnp.exp(end)


def _gla_fwd(q, k, v, la, *, name, blk=512):
    h, s, dk = q.shape
    dv = v.shape[2]
    bs = min(blk, s)
    ncb = bs // CHUNK
    nb = s // bs

    def body(q_ref, k_ref, v_ref, la_ref, o_ref, st_ref, st):
        @pl.when(pl.program_id(1) == 0)
        def _():
            st[...] = jnp.zeros_like(st)

        for c in range(ncb):
            sl = pl.ds(c * CHUNK, CHUNK)
            _, kf, a = _gla_chunk(la_ref[0, sl, :], k_ref[0, sl, :])
            ut = _dot(v_ref[0, sl, :].astype(BF16), kf.astype(BF16), TN)
            new = a * st[...] + ut
            st[...] = new
            st_ref[0, c] = new
            qs = (q_ref[0, sl, :] * (GLA_DK ** -0.5)).astype(BF16)
            o_ref[0, sl, :] = _dot(qs, new.astype(BF16), NT)

    row = lambda w: pl.BlockSpec((1, bs, w), lambda hh, b: (hh, b, 0))
    return pl.pallas_call(
        body, name=name, grid=(h, nb), in_specs=[row(dk), row(dk), row(dv), row(dk)],
        out_specs=[row(dv), pl.BlockSpec((1, ncb, dv, dk), lambda hh, b: (hh, b, 0, 0))],
        out_shape=[jax.ShapeDtypeStruct((h, s, dv), F32), jax.ShapeDtypeStruct((h, s // CHUNK, dv, dk), F32)],
        scratch_shapes=[pltpu.VMEM((dv, dk), F32)], compiler_params=_params('arbitrary', 'arbitrary'))(q, k, v, la)


def _gla_bwd(q, k, v, la, st_all, st_prev, do, *, name, blk=512):
    h, s, dk = q.shape
    dv = v.shape[2]
    bs = min(blk, s)
    ncb = bs // CHUNK
    nb = s // bs

    def body(q_ref, k_ref, v_ref, la_ref, st_ref, sp_ref, do_ref, dq_ref, dk_ref, dv_ref, dla_ref, ga):
        @pl.when(pl.program_id(1) == 0)
        def _():
            ga[...] = jnp.zeros_like(ga)

        r = lax.broadcasted_iota(jnp.int32, (CHUNK, CHUNK), 0)
        cc = lax.broadcasted_iota(jnp.int32, (CHUNK, CHUNK), 1)
        tri_rev = jnp.where(cc >= r, 1.0, 0.0).astype(BF16)
        for c in reversed(range(ncb)):
            sl = pl.ds(c * CHUNK, CHUNK)
            dec, kf, a = _gla_chunk(la_ref[0, sl, :], k_ref[0, sl, :])
            kd = kf.astype(BF16)
            qs = (q_ref[0, sl, :] * (GLA_DK ** -0.5)).astype(BF16)
            dob = do_ref[0, sl, :].astype(BF16)
            g = _dot(dob, qs, TN) + ga[...]
            gb = g.astype(BF16)
            dq_ref[0, sl, :] = (GLA_DK ** -0.5) * _dot(dob, st_ref[0, c].astype(BF16))
            dv_ref[0, sl, :] = _dot(kd, gb, NT)
            dkd = _dot(v_ref[0, sl, :].astype(BF16), gb)
            dk_ref[0, sl, :] = dkd * dec
            e = dkd * kf
            da = jnp.sum(g * sp_ref[0, c], axis=0, keepdims=True)
            dend = jnp.sum(e, axis=0, keepdims=True) + da * a
            dla_ref[0, sl, :] = dend - _tri_dot(tri_rev, e)
            ga[...] = a * g

    row = lambda w: pl.BlockSpec((1, bs, w), lambda hh, b: (hh, nb - 1 - b, 0))
    stspec = pl.BlockSpec((1, ncb, dv, dk), lambda hh, b: (hh, nb - 1 - b, 0, 0))
    return pl.pallas_call(
        body, name=name, grid=(h, nb), in_specs=[row(dk), row(dk), row(dv), row(dk), stspec, stspec, row(dv)],
        out_specs=[row(dk), row(dk), row(dv), row(dk)],
        out_shape=[jax.ShapeDtypeStruct((h, s, dk), F32), jax.ShapeDtypeStruct((h, s, dk), F32),
                   jax.ShapeDtypeStruct((h, s, dv), F32), jax.ShapeDtypeStruct((h, s, dk), F32)],
        scratch_shapes=[pltpu.VMEM((dv, dk), F32)],
        compiler_params=_params('arbitrary', 'arbitrary'))(q, k, v, la, st_all, st_prev, do)


def _place():
    return lax.axis_index('x'), lax.axis_index('y'), lax.axis_index('c')


ANY = pl.BlockSpec(memory_space=pl.ANY)


def _all_gather8(blk, *, name):
    m, n = blk.shape

    def body(x_ref, out_ref, send_sems, recv_sems, local_sem):
        x, y, c = _place()
        me, sibling = (x, y, c), (x, y, 1 - c)
        chips = [(1 - x, y), (x, 1 - y), (1 - x, 1 - y)]

        def slot(px, py, pc):
            return out_ref.at[4 * px + 2 * py + pc]

        def copy(q, block, to, src=None):
            return pltpu.make_async_remote_copy(
                src_ref=slot(*block) if src is None else src, dst_ref=slot(*block), send_sem=send_sems.at[q],
                recv_sem=recv_sems.at[q], device_id=to, device_id_type=MESH)

        mine = pltpu.make_async_copy(x_ref, slot(*me), local_sem)
        mine.start()
        first = [copy(0, me, sibling, src=x_ref)]
        first += [copy(1 + q, me, (*chip, c), src=x_ref) for q, chip in enumerate(chips)]
        for cp in first:
            cp.start()
        passed = [copy(4 + q, (*chip, c), sibling) for q, chip in enumerate(chips)]
        for q, chip in enumerate(chips):
            copy(1 + q, (*chip, c), me).wait_recv()
            passed[q].start()
        copy(0, sibling, me).wait_recv()
        for q, chip in enumerate(chips):
            copy(4 + q, (*chip, 1 - c), me).wait_recv()
        for cp in first + passed:
            cp.wait_send()
        mine.wait()

    return pl.pallas_call(
        body, name=name, in_specs=[ANY], out_specs=ANY, out_shape=jax.ShapeDtypeStruct((N_DEV, m, n), blk.dtype),
        scratch_shapes=[pltpu.SemaphoreType.DMA((7,)), pltpu.SemaphoreType.DMA((7,)), pltpu.SemaphoreType.DMA(())],
    )(blk)


def _sems(*counts):
    return [pltpu.SemaphoreType.DMA((n,)) for n in counts]


def _gather_shards(ws, *, name):
    n = len(ws)

    def body(*refs):
        ins, outs = refs[:n], refs[n:2 * n]
        send_sems, recv_sems, local_sems = refs[2 * n:]
        x, y, c = _place()
        me_chip = 2 * x + y
        chips = [(1 - x, y), (x, 1 - y), (1 - x, 1 - y)]

        def copy(q, k, src, dst, to):
            return pltpu.make_async_remote_copy(src_ref=src, dst_ref=dst, send_sem=send_sems.at[6 * q + k],
                                                recv_sem=recv_sems.at[6 * q + k], device_id=to, device_id_type=MESH)

        local, first, passed = [], [], []
        for q in range(n):
            local.append(pltpu.make_async_copy(ins[q], outs[q].at[me_chip], local_sems.at[q]))
            local[-1].start()
            for k, (px, py) in enumerate(chips):
                first.append(copy(q, k, ins[q].at[c], outs[q].at[me_chip, c], (px, py, c)))
                first[-1].start()
        for q in range(n):
            for k, (px, py) in enumerate(chips):
                slot = outs[q].at[2 * px + py, c]
                copy(q, k, slot, slot, (px, py, c)).wait_recv()
                passed.append(copy(q, 3 + k, slot, slot, (x, y, 1 - c)))
                passed[-1].start()
        for q in range(n):
            for k, (px, py) in enumerate(chips):
                slot = outs[q].at[2 * px + py, 1 - c]
                copy(q, 3 + k, slot, slot, (x, y, 1 - c)).wait_recv()
        for cp in first + passed:
            cp.wait_send()
        for cp in local:
            cp.wait()

    return pl.pallas_call(
        body, name=name, in_specs=[ANY] * n, out_specs=[ANY] * n,
        out_shape=[jax.ShapeDtypeStruct((N_CHIPS,) + w.shape, w.dtype) for w in ws],
        scratch_shapes=_sems(6 * n, 6 * n, n))(*ws)


def _swap_layers(gs, *, name):
    n = len(gs)

    def body(*refs):
        ins, own, got = refs[:n], refs[n:2 * n], refs[2 * n:3 * n]
        send_sems, recv_sems, local_sems = refs[3 * n:]
        x, y, c = _place()
        local, cps = [], []
        for q in range(n):
            local.append(pltpu.make_async_copy(ins[q].at[:, pl.ds(c, 1)], own[q], local_sems.at[q]))
            local[-1].start()
            cps.append(pltpu.make_async_remote_copy(
                src_ref=ins[q].at[:, pl.ds(1 - c, 1)], dst_ref=got[q], send_sem=send_sems.at[q],
                recv_sem=recv_sems.at[q], device_id=(x, y, 1 - c), device_id_type=MESH))
            cps[-1].start()
        for cp in cps + local:
            cp.wait()

    shp = [jax.ShapeDtypeStruct((N_CHIPS, 1) + g.shape[2:], g.dtype) for g in gs]
    res = pl.pallas_call(body, name=name, in_specs=[ANY] * n, out_specs=[ANY] * (2 * n), out_shape=shp + shp,
                         scratch_shapes=_sems(n, n, n))(*gs)
    return res[:n], res[n:]


def _chip_exchange(ps, *, name):
    n = len(ps)

    def body(*refs):
        ins, outs = refs[:n], refs[n:2 * n]
        send_sems, recv_sems, local_sems = refs[2 * n:]
        x, y, c = _place()
        chips = [(1 - x, y), (x, 1 - y), (1 - x, 1 - y)]
        local, cps = [], []
        for q in range(n):
            local.append(pltpu.make_async_copy(ins[q].at[2 * x + y], outs[q].at[3], local_sems.at[q]))
            local[-1].start()
            for k, (px, py) in enumerate(chips):
                cps.append(pltpu.make_async_remote_copy(
                    src_ref=ins[q].at[2 * px + py], dst_ref=outs[q].at[k], send_sem=send_sems.at[3 * q + k],
                    recv_sem=recv_sems.at[3 * q + k], device_id=(px, py, c), device_id_type=MESH))
                cps[-1].start()
        for cp in cps + local:
            cp.wait()

    return pl.pallas_call(body, name=name, in_specs=[ANY] * n, out_specs=[ANY] * n,
                          out_shape=[jax.ShapeDtypeStruct(p.shape, p.dtype) for p in ps],
                          scratch_shapes=_sems(3 * n, 3 * n, n))(*ps)


def _join_layers(fs, *, name):
    n = len(fs)

    def body(*refs):
        ins, outs = refs[:n], refs[n:2 * n]
        send_sems, recv_sems, local_sems = refs[2 * n:]
        x, y, c = _place()
        local, cps = [], []
        for q in range(n):
            local.append(pltpu.make_async_copy(ins[q], outs[q].at[c], local_sems.at[q]))
            local[-1].start()
            cps.append(pltpu.make_async_remote_copy(
                src_ref=ins[q], dst_ref=outs[q].at[c], send_sem=send_sems.at[q], recv_sem=recv_sems.at[q],
                device_id=(x, y, 1 - c), device_id_type=MESH))
            cps[-1].start()
        for cp in cps + local:
            cp.wait()

    return pl.pallas_call(body, name=name, in_specs=[ANY] * n, out_specs=[ANY] * n,
                          out_shape=[jax.ShapeDtypeStruct((2,) + f.shape, f.dtype) for f in fs],
                          scratch_shapes=_sems(n, n, n))(*fs)


def _sum_lead(r, *, name, ts=256):
    _, k, n = r.shape
    ts = min(ts, k)

    def body(r_ref, o_ref):
        f = lambda q: r_ref[q].astype(F32)
        o_ref[...] = ((f(3) + f(0)) + f(1)) + f(2)

    return pl.pallas_call(body, name=name, grid=(k // ts,), in_specs=[pl.BlockSpec((4, ts, n), lambda i: (0, i, 0))],
                          out_specs=pl.BlockSpec((ts, n), lambda i: (i, 0)),
                          out_shape=jax.ShapeDtypeStruct((k, n), F32), compiler_params=_params('arbitrary'))(r)


WIN_SHARD = N_IN // N_CHIPS
WIN_PAD = -(-WIN_SHARD // LANES) * LANES


def _full_layer(gathered, axis, l):
    sh = gathered[:, l]
    _, k, n = sh.shape
    if axis == 2:
        return sh.transpose(1, 0, 2).reshape(k, N_CHIPS * n)
    return sh.reshape(N_CHIPS * k, n)


def _win_cols(wp, o, n):
    parts = []
    while n > 0:
        j, r = divmod(o, WIN_SHARD)
        take = min(n, WIN_SHARD - r)
        parts.append(wp[:, j * WIN_PAD + r:j * WIN_PAD + r + take])
        o, n = o + take, n - take
    return parts[0] if len(parts) == 1 else jnp.concatenate(parts, axis=1)


def _split_full(full, axis):
    l, k, n = full.shape
    if axis == 2:
        return jnp.stack([full[:, :, j * (n // N_CHIPS):(j + 1) * (n // N_CHIPS)] for j in range(N_CHIPS)])
    return full.reshape(l, N_CHIPS, k // N_CHIPS, n).transpose(1, 0, 2, 3)


def _heads(z, n):
    s, w = z.shape
    return z.reshape(s, n, w // n).transpose(1, 0, 2)


def _merge(z):
    n, s, d = z.shape
    return z.transpose(1, 0, 2).reshape(s, n * d)


def _padc(a, w):
    return jnp.pad(a, ((0, 0), (0, w - a.shape[1])))


def _swap16(a):
    return jnp.concatenate([a[..., 16:32], a[..., 0:16]], axis=-1)


B_FF, B_GLOW, B_MKV, B_MKR, B_MKRS, B_MQ, B_GR, B_GQ, B_GK, B_GV, B_END = (
    0, 128, 256, 384, 512, 640, 896, 1408, 1664, 1920, 2432)
B_W = 2560
O_FQ, O_FF, O_GQ, O_GLOW, O_GR, O_MQ, O_MKV, O_MKR, O_ZG = 0, 768, 772, 1796, 1812, 2324, 2580, 2708, 2740


def _repack_layer_weights(w):
    wi = functools.partial(_win_cols, w['w_in'])
    out = dict(w)
    out['in_a'] = wi(O_FQ, 768)
    kr = wi(O_MKR, 32)
    out['in_b'] = jnp.concatenate([
        _padc(wi(O_FF, 4), 128), _padc(wi(O_GLOW, 16), 128), wi(O_MKV, 128), _padc(kr, 128), _padc(_swap16(kr), 128),
        wi(O_MQ, 256), wi(O_GR, 512), wi(O_GQ, 1024), jnp.zeros((D_MODEL, B_W - B_END), kr.dtype)], axis=1)
    out['in_c'] = wi(O_ZG, 3072)
    uq = w['w_mla_uq'].reshape(MLA_Q_RANK, MLA_HEADS, MLA_NOPE + MLA_ROPE)
    rope = uq[:, :, MLA_NOPE:]
    out['uq'] = jnp.concatenate([uq[:, :, :MLA_NOPE].reshape(MLA_Q_RANK, -1), rope.reshape(MLA_Q_RANK, -1),
                                 _swap16(rope).reshape(MLA_Q_RANK, -1)], axis=1)
    ukv = w['w_mla_ukv'].reshape(MLA_KV_RANK, MLA_HEADS, MLA_NOPE + MLA_VD)
    out['ukv'] = jnp.concatenate([ukv[:, :, :MLA_NOPE].reshape(MLA_KV_RANK, -1),
                                  ukv[:, :, MLA_NOPE:].reshape(MLA_KV_RANK, -1)], axis=1)
    out['gate'] = jnp.pad(w['w_gla_gate'], ((0, 128 - GLA_RANK), (0, 0)))
    return out


def _unpack_layer_grads(g):
    a, b, c = g['in_a'], g['in_b'], g['in_c']
    kr = b[:, B_MKR:B_MKR + 32] + _swap16(b[:, B_MKRS:B_MKRS + 32])
    w_in = jnp.concatenate([a, b[:, B_FF:B_FF + 4], b[:, B_GQ:B_GQ + 1024], b[:, B_GLOW:B_GLOW + 16],
                            b[:, B_GR:B_GR + 512], b[:, B_MQ:B_MQ + 256], b[:, B_MKV:B_MKV + 128], kr, c], axis=1)
    uq = g['uq']
    nope = uq[:, :256].reshape(MLA_Q_RANK, MLA_HEADS, MLA_NOPE)
    rope = (uq[:, 256:384].reshape(MLA_Q_RANK, MLA_HEADS, MLA_ROPE)
            + _swap16(uq[:, 384:512].reshape(MLA_Q_RANK, MLA_HEADS, MLA_ROPE)))
    w_uq = jnp.concatenate([nope, rope], axis=2).reshape(MLA_Q_RANK, -1)
    ukv = g['ukv']
    w_ukv = jnp.concatenate([ukv[:, :256].reshape(MLA_KV_RANK, MLA_HEADS, MLA_NOPE),
                             ukv[:, 256:].reshape(MLA_KV_RANK, MLA_HEADS, MLA_VD)], axis=2).reshape(MLA_KV_RANK, -1)
    out = {'w_in': w_in, 'w_mla_uq': w_uq, 'w_mla_ukv': w_ukv, 'w_gla_gate': g['gate'][:GLA_RANK]}
    for nm in ('w_up_fox', 'w_up_gla', 'w_up_mla', 'w_out', 'w_xq', 'w_xkv', 'w_xo', 'w_mlp1', 'w_mlp2'):
        out[nm] = g[nm]
    return out


def _rope_tables(s):
    half = MLA_ROPE // 2
    inv = ROPE_BASE ** (-jnp.arange(half, dtype=F32) / half)
    ang = jnp.arange(s).astype(F32)[:, None] * inv[None, :]
    cos, sin = jnp.cos(ang), jnp.sin(ang)
    c1 = jnp.concatenate([cos, cos], axis=1)
    s1 = jnp.concatenate([-sin, sin], axis=1)
    return jnp.tile(c1, (1, MLA_HEADS)), jnp.tile(s1, (1, MLA_HEADS)), _padc(c1, 128), _padc(s1, 128)


def _rms_bwd(x, dh, g):
    r = lax.rsqrt(jnp.mean(x * x, axis=-1, keepdims=True) + EPS)
    xh = x * r
    gd = dh * g
    return r * (gd - xh * jnp.mean(gd * xh, axis=-1, keepdims=True)), dh * xh


def _norm_bwd_call(x, dh, g, dres, name):
    w = x.shape[1]

    def with_res(xv, dv, rv, gv):
        dx, dg = _rms_bwd(xv, dv.astype(F32), gv)
        return rv + dx, dg

    def plain(xv, dv, gv):
        return _rms_bwd(xv, dv.astype(F32), gv)

    if dres is None:
        return _rowwise(plain, [x, dh], [g], [(w, F32)], [w], name=name)
    return _rowwise(with_res, [x, dh, dres], [g], [(w, F32)], [w], name=name)


def _gla_out_fwd(oraw, gr, g_out):
    outs = []
    for hh in range(GLA_HEADS):
        sl = slice(hh * GLA_DV, (hh + 1) * GLA_DV)
        oh = oraw[:, sl]
        n = oh * lax.rsqrt(jnp.mean(oh * oh, axis=-1, keepdims=True) + EPS) * g_out
        r = gr[:, sl]
        outs.append(n * (r * _sig(r)))
    return (jnp.concatenate(outs, axis=1),)


def _gla_out_bwd(oraw, gr, dout, g_out):
    d_o, d_r, dg = [], [], 0.0
    for hh in range(GLA_HEADS):
        sl = slice(hh * GLA_DV, (hh + 1) * GLA_DV)
        oh, r, do = oraw[:, sl], gr[:, sl], dout[:, sl].astype(F32)
        rs = lax.rsqrt(jnp.mean(oh * oh, axis=-1, keepdims=True) + EPS)
        sg = _sig(r)
        dn = do * (r * sg)
        d_r.append(do * (oh * rs * g_out) * (sg + r * sg * (1.0 - sg)))
        dx, dgh = _rms_bwd(oh, dn, g_out)
        d_o.append(dx)
        dg = dg + dgh
    return jnp.concatenate(d_o, axis=1), jnp.concatenate(d_r, axis=1), dg


def _adam(w, g, m, v):
    m = ADAM_B1 * m + (1.0 - ADAM_B1) * g
    v = ADAM_B2 * v + (1.0 - ADAM_B2) * (g * g)
    m_hat = m / (1.0 - ADAM_B1 ** ADAM_STEP)
    v_hat = v / (1.0 - ADAM_B2 ** ADAM_STEP)
    return -ADAM_LR * (m_hat / (jnp.sqrt(v_hat) + ADAM_EPS) + ADAM_WD * w), m, v


def _layer_fwd(x, mem, w, p, tabs, tag):
    c4, s4, ck, sk = tabs
    sv = {'x0': x}
    nm = lambda t: f'{t}_{tag}'
    za, h = _mm(x, w['in_a'], mode='nn', out_dtype=BF16, norm_g=p['g_mix'], emit_norm=True, tn=768, name=nm('in_a'))
    zb = _mm(h, w['in_b'], mode='nn', out_dtype=F32, name=nm('in_b'))
    zc = _mm(h, w['in_c'], mode='nn', out_dtype=F32, name=nm('in_c'))
    sv.update(h=h, zc=zc)
    ff = zb[:, B_FF:B_FF + 128]
    (lf,) = _rowwise(lambda f, b: (_logsig(f + b),), [ff], [p['b_fox']], [(128, F32)], name=nm('fox_lf'))
    cum = _cumsum_rows(lf, reverse=False, name=nm('fox_cum'))
    cumh = cum[:, :FOX_HEADS].T
    fq, fk, fv = (_heads(za[:, o:o + 256], FOX_HEADS) for o in (0, 256, 512))
    cq, ckk = cumh[:, :, None], cumh[:, None, :]
    o_fox, lse_fox = _attn_fwd(fq, fk, fv, cq, ckk, scale=FOX_HD ** -0.5, mode='causal', name=nm('fox_attn'))
    sv.update(ff=ff, fq=fq, fk=fk, fv=fv, cq=cq, ck=ckk, o_fox=o_fox, lse_fox=lse_fox)
    glow = zb[:, B_GLOW:B_GLOW + 128]
    gr = zb[:, B_GR:B_GR + 512]

    def gate_fn(gl, wg, bg):
        return (_logsig(_dot(gl.astype(BF16), wg) + bg) / GLA_TAU,)

    (la,) = _rowwise(gate_fn, [glow], [w['gate'], p['b_gla']], [(256, F32)], name=nm('gla_gate'))
    gq, gk, lah = (_heads(t, GLA_HEADS) for t in (zb[:, B_GQ:B_GQ + 256], zb[:, B_GK:B_GK + 256], la))
    gv = _heads(zb[:, B_GV:B_GV + 512], GLA_HEADS)
    oraw_h, states = _gla_fwd(gq, gk, gv, lah, name=nm('gla'))
    oraw = _merge(oraw_h)
    (o_gla,) = _rowwise(_gla_out_fwd, [oraw, gr], [p['g_gla_out']], [(512, BF16)], name=nm('gla_out'))
    sv.update(glow=glow, gr=gr, gq=gq, gk=gk, gv=gv, lah=lah, states=states, oraw=oraw, o_gla=o_gla)
    mq = zb[:, B_MQ:B_MQ + 256]
    mkv = zb[:, B_MKV:B_MKV + 128]
    mkr2 = zb[:, B_MKR:B_MKR + 256]
    qp, cqn = _mm(mq, w['uq'], mode='nn', out_dtype=F32, norm_g=p['g_mla_q'], emit_norm=True, name=nm('mla_uq'))
    kvp, ckvn = _mm(mkv, w['ukv'], mode='nn', out_dtype=BF16, norm_g=p['g_mla_kv'], emit_norm=True,
                    name=nm('mla_ukv'))

    def rope_fn(qv, kr, c4v, s4v, ckv, skv):
        return qv[:, 256:384] * c4v + qv[:, 384:512] * s4v, kr[:, 0:128] * ckv + kr[:, 128:256] * skv

    q_rope, k_rope = _rowwise(rope_fn, [qp, mkr2, c4, s4, ck, sk], [], [(128, BF16), (128, BF16)], name=nm('rope'))
    qh = jnp.concatenate([_heads(qp[:, :256].astype(BF16), MLA_HEADS), _heads(q_rope, MLA_HEADS)], axis=2)
    kh = jnp.concatenate([_heads(kvp[:, :256], MLA_HEADS),
                          jnp.broadcast_to(k_rope[None, :, :MLA_ROPE], (MLA_HEADS, x.shape[0], MLA_ROPE))], axis=2)
    vh = _heads(kvp[:, 256:], MLA_HEADS)
    o_mla, lse_mla = _attn_fwd(qh, kh, vh, None, None, scale=(MLA_NOPE + MLA_ROPE) ** -0.5, mode='chunk',
                               name=nm('mla_attn'))
    sv.update(mq=mq, mkv=mkv, cqn=cqn, ckvn=ckvn, qh=qh, kh=kh, vh=vh, o_mla=o_mla, lse_mla=lse_mla)
    of_m, om_m = _merge(o_fox), _merge(o_mla)
    sv.update(of_m=of_m, om_m=om_m)
    b_br = p['b_branch']

    def first(acc, zg, bb):
        return _sig(zg + bb) * acc

    def more(acc, zg, bb, prev):
        return prev + _sig(zg + bb) * acc

    y = _mm(of_m, w['w_up_fox'], mode='nn', out_dtype=F32, name=nm('up_fox'), epilogue=first,
            extras=[(zc, *_mn(col_off=0)), (b_br, *_nvec(col_off=0))])
    y = _mm(o_gla, w['w_up_gla'], mode='nn', out_dtype=F32, name=nm('up_gla'), epilogue=more,
            extras=[(zc, *_mn(col_off=2)), (b_br, *_nvec(col_off=2)), (y, *_mn())])
    y = _mm(om_m, w['w_up_mla'], mode='nn', out_dtype=BF16, name=nm('up_mla'), epilogue=more,
            extras=[(zc, *_mn(col_off=4)), (b_br, *_nvec(col_off=4)), (y, *_mn())])
    add = lambda acc, res: res + acc
    x1 = _mm(y, w['w_out'], mode='nn', out_dtype=F32, name=nm('out'), epilogue=add, extras=[(x, *_mn())])
    sv.update(y=y, x1=x1)
    qx, hx = _mm(x1, w['w_xq'], mode='nn', out_dtype=BF16, norm_g=p['g_xa'], emit_norm=True, name=nm('xq'))
    kvx, mn = _mm(mem, w['w_xkv'], mode='nn', out_dtype=BF16, norm_g=p['g_mem'], emit_norm=True, name=nm('xkv'))
    qxh, kxh, vxh = _heads(qx, XA_HEADS), _heads(kvx[:, :512], XA_HEADS), _heads(kvx[:, 512:], XA_HEADS)
    ox, lse_x = _attn_fwd(qxh, kxh, vxh, None, None, scale=XA_HD ** -0.5, mode='full', name=nm('xa_attn'))
    ox_m = _merge(ox)
    x2 = _mm(ox_m, w['w_xo'], mode='nn', out_dtype=F32, name=nm('xo'), epilogue=add, extras=[(x1, *_mn())])
    sv.update(hx=hx, mn=mn, qxh=qxh, kxh=kxh, vxh=vxh, ox=ox, lse_x=lse_x, ox_m=ox_m, x2=x2)
    hpre, hm = _mm(x2, w['w_mlp1'], mode='nn', out_dtype=BF16, norm_g=p['g_mlp'], emit_norm=True, name=nm('mlp1'))
    relu2 = lambda t: jnp.square(jnp.maximum(t.astype(F32), 0.0))
    x3 = _mm(hpre, w['w_mlp2'], mode='nn', out_dtype=F32, name=nm('mlp2'), a_fn=relu2, epilogue=add,
             extras=[(x2, *_mn())])
    sv.update(hpre=hpre, hm=hm)
    return x3, sv


def _layer_bwd(dx3, mem, w, p, tabs, sv, tag):
    c4, s4, ck, sk = tabs
    nm = lambda t: f'{t}_{tag}'
    s = dx3.shape[0]
    gw, gs = {}, {}
    relu2 = lambda t: jnp.square(jnp.maximum(t.astype(F32), 0.0))
    gw['w_mlp2'] = _mm(sv['hpre'], dx3, mode='tn', out_dtype=F32, name=nm('d_mlp2'), a_fn=relu2, tn=256)
    dact = lambda acc, hp: acc * (2.0 * jnp.maximum(hp.astype(F32), 0.0))
    dhpre = _mm(dx3, w['w_mlp2'], mode='nt', out_dtype=BF16, name=nm('d_act'), epilogue=dact,
                extras=[(sv['hpre'], *_mn())])
    gw['w_mlp1'] = _mm(sv['hm'], dhpre, mode='tn', out_dtype=F32, name=nm('d_mlp1'))
    dhm = _mm(dhpre, w['w_mlp1'], mode='nt', out_dtype=F32, name=nm('d_hm'))
    dx2, gs['g_mlp'] = _norm_bwd_call(sv['x2'], dhm, p['g_mlp'], dx3, nm('d_norm_mlp'))
    gw['w_xo'] = _mm(sv['ox_m'], dx2, mode='tn', out_dtype=F32, name=nm('d_xo'), tn=256)
    dox = _mm(dx2, w['w_xo'], mode='nt', out_dtype=BF16, name=nm('d_ox'))
    dqx, dkx, dvx = _attn_bwd(sv['qxh'], sv['kxh'], sv['vxh'], sv['ox'], _heads(dox, XA_HEADS), sv['lse_x'], None,
                              None, scale=XA_HD ** -0.5, mode='full', name=nm('xa_bwd'))
    dqx_m = _merge(dqx).astype(BF16)
    dkvx = jnp.concatenate([_merge(dkx), _merge(dvx)], axis=1).astype(BF16)
    gw['w_xq'] = _mm(sv['hx'], dqx_m, mode='tn', out_dtype=F32, name=nm('d_xq'))
    dhx = _mm(dqx_m, w['w_xq'], mode='nt', out_dtype=F32, name=nm('d_hx'))
    gw['w_xkv'] = _mm(sv['mn'], dkvx, mode='tn', out_dtype=F32, name=nm('d_xkv'))
    dmn = _mm(dkvx, w['w_xkv'], mode='nt', out_dtype=F32, name=nm('d_mn'))
    _, gs['g_mem'] = _norm_bwd_call(mem, dmn, p['g_mem'], None, nm('d_norm_mem'))
    dx1, gs['g_xa'] = _norm_bwd_call(sv['x1'], dhx, p['g_xa'], dx2, nm('d_norm_xa'))
    gw['w_out'] = _mm(sv['y'], dx1, mode='tn', out_dtype=F32, name=nm('d_out'), tn=256)
    dy = _mm(dx1, w['w_out'], mode='nt', out_dtype=BF16, name=nm('d_y'))
    zc, b_br = sv['zc'], p['b_branch']

    def du_fn(dyv, zg, bb):
        g = _sig(zg + bb)
        d = dyv.astype(F32)
        return d * g[:, 0:1024], d * g[:, 1024:2048], d * g[:, 2048:3072]

    du = _rowwise(du_fn, [dy, zc], [b_br], [(D_MODEL, BF16)] * 3, name=nm('d_u'))

    def dgate(acc, dyv, zg, bb):
        g = _sig(zg + bb)
        return dyv.astype(F32) * acc * g * (1.0 - g)

    dzc, do_br = [], []
    for q, (o_m, wn) in enumerate(((sv['of_m'], 'w_up_fox'), (sv['o_gla'], 'w_up_gla'), (sv['om_m'], 'w_up_mla'))):
        dzc.append(_mm(o_m, w[wn], mode='nn', out_dtype=F32, name=nm(f'd_zg{q}'), epilogue=dgate,
                       extras=[(dy, *_mn()), (zc, *_mn(col_off=2 * q)), (b_br, *_nvec(col_off=2 * q))]))
        gw[wn] = _mm(o_m, du[q], mode='tn', out_dtype=F32, name=nm(f'd_up{q}'))
        do_br.append(_mm(du[q], w[wn], mode='nt', out_dtype=F32 if q == 1 else BF16, name=nm(f'd_o{q}')))
    dzc = jnp.concatenate(dzc, axis=1)
    (gs['b_branch'],) = _rowwise(lambda t: (t,), [dzc], [], [], [3072], name=nm('d_bbranch'))
    dfq, dfk, dfv, dck, dcq = _attn_bwd(sv['fq'], sv['fk'], sv['fv'], sv['o_fox'], _heads(do_br[0], FOX_HEADS),
                                   sv['lse_fox'], sv['cq'], sv['ck'], scale=FOX_HD ** -0.5, mode='causal',
                                   name=nm('fox_bwd'))
    dcum = _padc((dck[:, 0, :] + dcq[:, :, 0]).T, 128)
    dlf = _cumsum_rows(dcum, reverse=True, name=nm('fox_dcum'))

    def dff_fn(dl, f, b):
        d = dl * _sig(-(f + b))
        return d, d

    dff, db_fox = _rowwise(dff_fn, [dlf, sv['ff']], [p['b_fox']], [(128, F32)], [128], name=nm('fox_dff'))
    gs['b_fox'] = db_fox
    dza = jnp.concatenate([_merge(dfq), _merge(dfk), _merge(dfv)], axis=1).astype(BF16)
    dqh, dkh, dvh = _attn_bwd(sv['qh'], sv['kh'], sv['vh'], sv['o_mla'], _heads(do_br[2], MLA_HEADS), sv['lse_mla'],
                              None, None, scale=(MLA_NOPE + MLA_ROPE) ** -0.5, mode='chunk', name=nm('mla_bwd'))
    dq_rope = _merge(dqh[:, :, MLA_NOPE:])
    dk_rope = _padc(jnp.sum(dkh[:, :, MLA_NOPE:], axis=0), 128)

    def drope_fn(dq, dk, c4v, s4v, ckv, skv):
        return dq * c4v, dq * s4v, dk * ckv, dk * skv

    dqr, dqrs, dkr, dkrs = _rowwise(drope_fn, [dq_rope, dk_rope, c4, s4, ck, sk], [], [(128, BF16)] * 4,
                                    name=nm('d_rope'))
    dqp = jnp.concatenate([_merge(dqh[:, :, :MLA_NOPE]).astype(BF16), dqr, dqrs], axis=1)
    dkvp = jnp.concatenate([_merge(dkh[:, :, :MLA_NOPE]), _merge(dvh)], axis=1).astype(BF16)
    gw['uq'] = _mm(sv['cqn'], dqp, mode='tn', out_dtype=F32, name=nm('d_uq'))
    dcqn = _mm(dqp, w['uq'], mode='nt', out_dtype=F32, name=nm('d_cqn'))
    gw['ukv'] = _mm(sv['ckvn'], dkvp, mode='tn', out_dtype=F32, name=nm('d_ukv'))
    dckvn = _mm(dkvp, w['ukv'], mode='nt', out_dtype=F32, name=nm('d_ckvn'))
    dmq, gs['g_mla_q'] = _norm_bwd_call(sv['mq'], dcqn, p['g_mla_q'], None, nm('d_norm_q'))
    dmkv, gs['g_mla_kv'] = _norm_bwd_call(sv['mkv'], dckvn, p['g_mla_kv'], None, nm('d_norm_kv'))
    doraw, dgr, gs['g_gla_out'] = _rowwise(_gla_out_bwd, [sv['oraw'], sv['gr'], do_br[1]], [p['g_gla_out']],
                                           [(512, F32), (512, BF16)], [128], name=nm('d_gla_out'))
    st = sv['states']
    st_prev = jnp.concatenate([jnp.zeros_like(st[:, :1]), st[:, :-1]], axis=1)
    dgq, dgk, dgv, dla = _gla_bwd(sv['gq'], sv['gk'], sv['gv'], sv['lah'], st, st_prev, _heads(doraw, GLA_HEADS),
                                  name=nm('gla_bwd'))

    def dgate_fn(dl, gl, wg, bg):
        pre = _dot(gl.astype(BF16), wg) + bg
        dpre = dl * (1.0 / GLA_TAU) * _sig(-pre)
        return dpre, _dot(dpre.astype(BF16), wg, NT), dpre

    dpre, dglow, gs['b_gla'] = _rowwise(dgate_fn, [_merge(dla), sv['glow']], [w['gate'], p['b_gla']],
                                        [(256, BF16), (128, BF16)], [256], name=nm('d_gla_gate'))
    gw['gate'] = _mm(sv['glow'], dpre, mode='tn', out_dtype=F32, name=nm('d_wgate'))
    bf = lambda t: t.astype(BF16)
    dzb = jnp.concatenate([bf(dff), dglow, bf(dmkv), dkr, dkrs, bf(dmq), dgr, bf(_merge(dgq)), bf(_merge(dgk)),
                           bf(_merge(dgv)), jnp.zeros((s, B_W - B_END), BF16)], axis=1)
    h = sv['h']
    gw['in_a'] = _mm(h, dza, mode='tn', out_dtype=F32, name=nm('d_in_a'), tn=768)
    gw['in_b'] = _mm(h, dzb, mode='tn', out_dtype=F32, name=nm('d_in_b'))
    gw['in_c'] = _mm(h, dzc, mode='tn', out_dtype=F32, name=nm('d_in_c'), tn=256)
    add = lambda acc, prev: prev + acc
    dh = _mm(dza, w['in_a'], mode='nt', out_dtype=F32, name=nm('d_h_a'))
    dh = _mm(dzb, w['in_b'], mode='nt', out_dtype=F32, name=nm('d_h_b'), epilogue=add, extras=[(dh, *_mn())])
    dh = _mm(dzc, w['in_c'], mode='nt', out_dtype=F32, name=nm('d_h_c'), epilogue=add, extras=[(dh, *_mn())])
    dx0, gs['g_mix'] = _norm_bwd_call(sv['x0'], dh, p['g_mix'], dx1, nm('d_norm_mix'))
    return dx0, gw, gs


def _loss_head(x, target, g_final):
    d = x.shape[1]

    def fn(xv, tv, gv):
        r = lax.rsqrt(jnp.mean(xv * xv, axis=-1, keepdims=True) + EPS)
        xh = xv * r
        e = xh * gv - tv
        dy = e * (1.0 / d)
        gd = dy * gv
        dx = r * (gd - xh * jnp.mean(gd * xh, axis=-1, keepdims=True))
        row_loss = 0.5 * jnp.mean(e * e, axis=-1, keepdims=True)
        return dx, dy * xh, jnp.broadcast_to(row_loss, (xv.shape[0], LANES))

    return _rowwise(fn, [x, target], [g_final], [(d, F32)], [d, LANES], name='loss_head')


def _small_sizes(shapes):
    return [math.prod(shapes[nm]) for nm in SMALL]


def _step(args):
    shapes = {nm: args[nm].shape for nm in ORDER}
    x, mem, target = args['x'][0], args['mem'][0], args['loss_target'][0]
    s = x.shape[0]

    def wire(nm):
        w = args[nm].astype(BF16)
        return jnp.pad(w, ((0, 0), (0, 0), (0, WIN_PAD - WIN_SHARD))) if nm == 'w_in' else w

    gathered = dict(zip([nm for nm, _ in BIG], _gather_shards([wire(nm) for nm, _ in BIG], name='gather_weights')))

    tabs = _rope_tables(s)
    layers_w, layers_p = [], []
    for l in range(DEPTH):
        layers_w.append(_repack_layer_weights({nm: _full_layer(gathered[nm], ax, l) for nm, ax in BIG}))
        layers_p.append({
            'g_mix': args['g_mix'][l][None], 'b_fox': _padc(args['b_fox_forget'][l][None], 128),
            'b_gla': args['b_gla_gate'][l][None], 'g_gla_out': args['g_gla_out'][l][None],
            'g_mla_q': args['g_mla_q'][l][None], 'g_mla_kv': args['g_mla_kv'][l][None],
            'b_branch': args['b_branch_gate'][l][None], 'g_xa': args['g_xa'][l][None],
            'g_mem': args['g_mem'][l][None], 'g_mlp': args['g_mlp'][l][None]})

    saved = []
    xl = x
    for l in range(DEPTH):
        xl, sv = _layer_fwd(xl, mem, layers_w[l], layers_p[l], tabs, f'l{l}')
        saved.append(sv)
    dx, dg_final, loss_lanes = _loss_head(xl, target, args['g_final'][None])
    gw_layers, gs_layers = [None] * DEPTH, [None] * DEPTH
    for l in reversed(range(DEPTH)):
        dx, gw, gs = _layer_bwd(dx, mem, layers_w[l], layers_p[l], tabs, saved[l], f'l{l}')
        gw_layers[l], gs_layers[l] = _unpack_layer_grads(gw), gs
    grad_x = dx[None]

    names = [nm for nm, _ in BIG]
    per_chip = [_split_full(jnp.stack([gw_layers[l][nm] for l in range(DEPTH)]), ax).astype(BF16) for nm, ax in BIG]
    own, got = _swap_layers(per_chip, name='grads_core_swap')
    pairs = []
    for nm, a, b in zip(names, own, got):
        k, n = a.shape[2:]
        (p,) = _rowwise(lambda u, v: (u.astype(F32) + v.astype(F32),),
                        [a.reshape(N_CHIPS * k, n), b.reshape(N_CHIPS * k, n)], [], [(n, BF16)], name=f'pair_sum_{nm}')
        pairs.append(p.reshape(N_CHIPS, k, n))
    from_chips = _chip_exchange(pairs, name='grads_chip_exchange')
    layer_sums = [_sum_lead(r, name=f'chip_sum_{nm}') for nm, r in zip(names, from_chips)]
    gshard = dict(zip(names, _join_layers(layer_sums, name='grads_join')))

    small_g = []
    for nm, key in (('g_mix', 'g_mix'), ('b_fox_forget', 'b_fox'), ('b_gla_gate', 'b_gla'),
                    ('g_gla_out', 'g_gla_out'), ('g_mla_q', 'g_mla_q'), ('g_mla_kv', 'g_mla_kv'),
                    ('b_branch_gate', 'b_branch'), ('g_xa', 'g_xa'), ('g_mem', 'g_mem'), ('g_mlp', 'g_mlp')):
        width = shapes[nm][1]
        small_g.append(jnp.concatenate([gs_layers[l][key][0, :width] for l in range(DEPTH)]))
    small_g.append(dg_final[0])
    small_g.append(loss_lanes[0, :1])
    flat = jnp.concatenate(small_g)
    n_small = flat.shape[0]
    srows = -(-n_small // (8 * LANES)) * 8
    pad = lambda v: jnp.pad(v, (0, srows * LANES - v.shape[0])).reshape(srows, LANES)
    all_small = _all_gather8(pad(flat), name='gather_small')
    sw, sm, svv = (pad(jnp.concatenate([args[pre + nm].reshape(-1) for nm in SMALL] + [jnp.zeros((1,), F32)]))
                   for pre in ('', 'm_', 'v_'))

    def small_body(g_ref, w_ref, m_ref, v_ref, go_ref, d_ref, mo_ref, vo_ref):
        g = g_ref[0]
        for q in range(1, N_DEV):
            g = g + g_ref[q]
        go_ref[...] = g
        d_ref[...], mo_ref[...], vo_ref[...] = _adam(w_ref[...], g, m_ref[...], v_ref[...])

    sg, sd, snm, snv = pl.pallas_call(
        small_body, name='small_sum_adam', out_shape=[jax.ShapeDtypeStruct((srows, LANES), F32)] * 4,
        compiler_params=pltpu.CompilerParams(vmem_limit_bytes=VMEM_LIMIT))(all_small, sw, sm, svv)

    def unsmall(buf):
        v, out, off = buf.reshape(-1), {}, 0
        for nm in SMALL:
            nel = math.prod(shapes[nm])
            out[nm] = v[off:off + nel].reshape(shapes[nm])
            off += nel
        return out, v[off]

    res = {}
    (res['grad'], loss), (res['delta'], _), (res['m'], _), (res['v'], _) = (unsmall(t) for t in (sg, sd, snm, snv))

    for nm, _ in BIG:
        shp = args[nm].shape
        view = lambda t: t.reshape(shp[0] * shp[1], shp[2])
        d, m2, v2 = _rowwise(_adam, [view(args[nm]), view(gshard[nm]), view(args['m_' + nm]), view(args['v_' + nm])],
                             [], [(shp[2], F32)] * 3, name=f'adam_{nm}')
        res['grad'][nm], res['delta'][nm], res['m'][nm], res['v'][nm] = (
            gshard[nm], d.reshape(shp), m2.reshape(shp), v2.reshape(shp))

    return (loss, grad_x, *[res['grad'][nm] for nm in ORDER], *[res['delta'][nm] for nm in ORDER],
            *[res['m'][nm] for nm in ORDER], *[res['v'][nm] for nm in ORDER])


def kernel(x, mem, g_mix, w_in, b_fox_forget, w_gla_gate, b_gla_gate, g_gla_out, g_mla_q, w_mla_uq, g_mla_kv, w_mla_ukv, b_branch_gate, w_up_fox, w_up_gla, w_up_mla, w_out, g_xa, g_mem, w_xq, w_xkv, w_xo, g_mlp, w_mlp1, w_mlp2, g_final, loss_target, m_g_mix, m_w_in, m_b_fox_forget, m_w_gla_gate, m_b_gla_gate, m_g_gla_out, m_g_mla_q, m_w_mla_uq, m_g_mla_kv, m_w_mla_ukv, m_b_branch_gate, m_w_up_fox, m_w_up_gla, m_w_up_mla, m_w_out, m_g_xa, m_g_mem, m_w_xq, m_w_xkv, m_w_xo, m_g_mlp, m_w_mlp1, m_w_mlp2, m_g_final, v_g_mix, v_w_in, v_b_fox_forget, v_w_gla_gate, v_b_gla_gate, v_g_gla_out, v_g_mla_q, v_w_mla_uq, v_g_mla_kv, v_w_mla_ukv, v_b_branch_gate, v_w_up_fox, v_w_up_gla, v_w_up_mla, v_w_out, v_g_xa, v_g_mem, v_w_xq, v_w_xkv, v_w_xo, v_g_mlp, v_w_mlp1, v_w_mlp2, v_g_final):
    return _step(dict(locals()))
```

```python
import functools
import math

import jax
import jax.numpy as jnp
from jax import lax
from jax.experimental import pallas as pl
from jax.experimental.pallas import tpu as pltpu

F32 = jnp.float32
BF16 = jnp.bfloat16
MESH = pl.DeviceIdType.MESH

D_MODEL = 1024
DEPTH = 2
CHUNK = 64
EPS = 1e-6
FOX_HEADS, FOX_HD = 4, 64
GLA_HEADS, GLA_DK, GLA_DV, GLA_RANK, GLA_TAU = 4, 64, 128, 16, 16.0
MLA_HEADS, MLA_Q_RANK, MLA_KV_RANK, MLA_NOPE, MLA_ROPE, MLA_VD = 4, 256, 128, 64, 32, 64
ROPE_BASE = 10000.0
XA_HEADS, XA_HD = 4, 128
D_FF = 4 * D_MODEL
IN_SIZES = (256, 256, 256, 4, 256, 256, 512, 16, 512, 256, 128, 32, 3072)
N_IN = sum(IN_SIZES)

ADAM_LR, ADAM_B1, ADAM_B2, ADAM_EPS, ADAM_WD, ADAM_STEP = 0.001, 0.9, 0.999, 1e-08, 0.01, 10

N_CHIPS = 4
N_DEV = 8
LANES = 128
VMEM_LIMIT = 48 * 1024 * 1024
MASK_VALUE = -1e30

BIG = (('w_in', 2), ('w_gla_gate', 2), ('w_mla_uq', 2), ('w_mla_ukv', 2), ('w_up_fox', 2), ('w_up_gla', 2),
       ('w_up_mla', 2), ('w_out', 1), ('w_xq', 1), ('w_xkv', 1), ('w_xo', 2), ('w_mlp1', 2), ('w_mlp2', 1))
SMALL = ('g_mix', 'b_fox_forget', 'b_gla_gate', 'g_gla_out', 'g_mla_q', 'g_mla_kv', 'b_branch_gate',
         'g_xa', 'g_mem', 'g_mlp', 'g_final')
ORDER = ('g_mix', 'w_in', 'b_fox_forget', 'w_gla_gate', 'b_gla_gate', 'g_gla_out', 'g_mla_q', 'w_mla_uq',
         'g_mla_kv', 'w_mla_ukv', 'b_branch_gate', 'w_up_fox', 'w_up_gla', 'w_up_mla', 'w_out', 'g_xa', 'g_mem',
         'w_xq', 'w_xkv', 'w_xo', 'g_mlp', 'w_mlp1', 'w_mlp2', 'g_final')
PACK_W = 1024
PACK_ROW_ALIGN = 256


def _params(*sem):
    return pltpu.CompilerParams(dimension_semantics=sem, vmem_limit_bytes=VMEM_LIMIT)


def _sig(x):
    return 1.0 / (1.0 + jnp.exp(-x))


def _logsig(x):
    return jnp.minimum(x, 0.0) - jnp.log(1.0 + jnp.exp(-jnp.abs(x)))


NN = (((1,), (0,)), ((), ()))
NT = (((1,), (1,)), ((), ()))
TN = (((0,), (0,)), ((), ()))


def _dot(a, b, dims=NN):
    return lax.dot_general(a, b, dims, preferred_element_type=F32)


def _tri_dot(tri, x):
    hi = x.astype(BF16)
    r1 = x - hi.astype(F32)
    mid = r1.astype(BF16)
    lo = (r1 - mid.astype(F32)).astype(BF16)
    return _dot(tri, hi) + _dot(tri, mid) + _dot(tri, lo)


def _mm(a, b, *, mode, out_dtype, name, tm=512, tn=512, norm_g=None, emit_norm=False, a_fn=None, extras=(),
        epilogue=None):
    if mode == 'tn':
        k, m = a.shape
    else:
        m, k = a.shape
    n = b.shape[0] if mode == 'nt' else b.shape[1]
    assert (b.shape[1] if mode == 'nt' else b.shape[0]) == k, (name, a.shape, b.shape)
    tm, tn = min(tm, m), min(tn, n)
    assert m % tm == 0 and n % tn == 0, (name, m, n, tm, tn)
    a_spec = pl.BlockSpec((k, tm), lambda i, j: (0, i)) if mode == 'tn' else pl.BlockSpec((tm, k), lambda i, j: (i, 0))
    b_spec = pl.BlockSpec((tn, k), lambda i, j: (j, 0)) if mode == 'nt' else pl.BlockSpec((k, tn), lambda i, j: (0, j))
    dims = {'nn': NN, 'nt': NT, 'tn': TN}[mode]
    has_norm = norm_g is not None
    assert not (has_norm and mode != 'nn')
    n_ex = len(extras)

    def body(*refs):
        a_ref, b_ref = refs[0], refs[1]
        pos = 2
        g_ref = None
        if has_norm:
            g_ref = refs[pos]
            pos += 1
        ex_refs = refs[pos:pos + n_ex]
        pos += n_ex
        o_ref = refs[pos]
        pos += 1
        h_ref = None
        if emit_norm:
            h_ref = refs[pos]
            pos += 1
        if has_norm:
            an_ref = refs[pos]

            @pl.when(pl.program_id(1) == 0)
            def _():
                xf = a_ref[...].astype(F32)
                y = xf * lax.rsqrt(jnp.mean(xf * xf, axis=-1, keepdims=True) + EPS) * g_ref[...]
                an_ref[...] = y.astype(BF16)
                if emit_norm:
                    h_ref[...] = y.astype(BF16)

            av = an_ref[...]
        else:
            av = a_ref[...]
            if a_fn is not None:
                av = a_fn(av)
            av = av.astype(BF16)
        acc = _dot(av, b_ref[...].astype(BF16), dims)
        if epilogue is not None:
            acc = epilogue(acc, *[r[...] for r in ex_refs])
        o_ref[...] = acc.astype(out_dtype)

    in_specs = [a_spec, b_spec]
    args = [a, b]
    if has_norm:
        in_specs.append(pl.BlockSpec((1, k), lambda i, j: (0, 0)))
        args.append(norm_g)
    for arr, blk, imap in extras:
        in_specs.append(pl.BlockSpec(blk, imap))
        args.append(arr)
    out_shape = [jax.ShapeDtypeStruct((m, n), out_dtype)]
    out_specs = [pl.BlockSpec((tm, tn), lambda i, j: (i, j))]
    if emit_norm:
        out_shape.append(jax.ShapeDtypeStruct((m, k), BF16))
        out_specs.append(pl.BlockSpec((tm, k), lambda i, j: (i, 0)))
    scratch = [pltpu.VMEM((tm, k), BF16)] if has_norm else []
    res = pl.pallas_call(
        body, name=name, grid=(m // tm, n // tn), in_specs=in_specs, out_specs=out_specs, out_shape=out_shape,
        scratch_shapes=scratch, compiler_params=_params('arbitrary', 'arbitrary'))(*args)
    return res if emit_norm else res[0]


def _mn(tm=512, tn=512, col_off=0):
    return (tm, tn), (lambda i, j: (i, j + col_off))


def _nvec(tn=512, col_off=0):
    return (1, tn), (lambda i, j: (0, j + col_off))


def _rowwise(fn, rows, consts, outs, sums=(), *, name, ts=256):
    r = rows[0].shape[0]
    ts = min(ts, r)
    assert r % ts == 0, (name, r, ts)
    nr, nc, no, ns = len(rows), len(consts), len(outs), len(sums)

    def body(*refs):
        vals = fn(*[x[...] for x in refs[:nr + nc]])
        for q in range(no):
            refs[nr + nc + q][...] = vals[q].astype(outs[q][1])
        if ns:
            @pl.when(pl.program_id(0) == 0)
            def _():
                for q in range(ns):
                    refs[nr + nc + no + q][...] = jnp.zeros((1, sums[q]), F32)

            for q in range(ns):
                refs[nr + nc + no + q][...] += jnp.sum(vals[no + q].astype(F32), axis=0, keepdims=True)

    in_specs = [pl.BlockSpec((ts, x.shape[1]), lambda i: (i, 0)) for x in rows]
    in_specs += [pl.BlockSpec(x.shape, lambda i, nd=x.ndim: (0,) * nd) for x in consts]
    out_specs = [pl.BlockSpec((ts, w), lambda i: (i, 0)) for w, _ in outs]
    out_specs += [pl.BlockSpec((1, w), lambda i: (0, 0)) for w in sums]
    out_shape = [jax.ShapeDtypeStruct((r, w), dt) for w, dt in outs]
    out_shape += [jax.ShapeDtypeStruct((1, w), F32) for w in sums]
    return pl.pallas_call(body, name=name, grid=(r // ts,), in_specs=in_specs, out_specs=out_specs,
                          out_shape=out_shape, compiler_params=_params('arbitrary'))(*rows, *consts)


def _cumsum_rows(x, *, reverse, name, bs=256):
    s, w = x.shape
    bs = min(bs, s)
    nb = s // bs

    def body(x_ref, o_ref, carry):
        @pl.when(pl.program_id(0) == 0)
        def _():
            carry[...] = jnp.zeros_like(carry)

        r = lax.broadcasted_iota(jnp.int32, (bs, bs), 0)
        c = lax.broadcasted_iota(jnp.int32, (bs, bs), 1)
        tri = jnp.where((c >= r) if reverse else (c <= r), 1.0, 0.0).astype(BF16)
        xv = x_ref[...]
        o_ref[...] = _tri_dot(tri, xv) + carry[...]
        carry[...] += jnp.sum(xv, axis=0, keepdims=True)

    imap = (lambda i: (nb - 1 - i, 0)) if reverse else (lambda i: (i, 0))
    return pl.pallas_call(body, name=name, grid=(nb,), in_specs=[pl.BlockSpec((bs, w), imap)],
                          out_specs=pl.BlockSpec((bs, w), imap), out_shape=jax.ShapeDtypeStruct((s, w), F32),
                          scratch_shapes=[pltpu.VMEM((1, w), F32)], compiler_params=_params('arbitrary'))(x)


def _mask(mode, i, j, bq, bk):
    qpos = i * bq + lax.broadcasted_iota(jnp.int32, (bq, bk), 0)
    kpos = j * bk + lax.broadcasted_iota(jnp.int32, (bq, bk), 1)
    if mode == 'causal':
        return kpos <= qpos
    return kpos < (jnp.right_shift(qpos, int(math.log2(CHUNK))) + 1) * CHUNK


def _attn_fwd(q, k, v, cq, ck, *, scale, mode, name, blk=512):
    h, s, dk = q.shape
    t, dv = k.shape[1], v.shape[2]
    bq = min(blk, s)
    bk = min(blk, t)
    nq, nk = s // bq, t // bk
    bias = cq is not None
    tri = mode != 'full'
    assert not tri or (bq == bk and bq % CHUNK == 0)

    def body(*refs):
        if bias:
            q_ref, k_ref, v_ref, cq_ref, ck_ref, o_ref, lse_ref, m_s, l_s, acc_s = refs
        else:
            q_ref, k_ref, v_ref, o_ref, lse_ref, m_s, l_s, acc_s = refs
        i, j = pl.program_id(1), pl.program_id(2)

        @pl.when(j == 0)
        def _():
            m_s[...] = jnp.full_like(m_s, MASK_VALUE)
            l_s[...] = jnp.zeros_like(l_s)
            acc_s[...] = jnp.zeros_like(acc_s)

        def compute():
            sc = _dot(q_ref[0], k_ref[0], NT) * scale
            if bias:
                sc = sc + cq_ref[0] - ck_ref[0]
            if tri:
                sc = jnp.where(_mask(mode, i, j, bq, bk), sc, MASK_VALUE)
            m_prev = m_s[...]
            m_new = jnp.maximum(m_prev, jnp.max(sc, axis=1, keepdims=True))
            alpha = jnp.exp(m_prev - m_new)
            p = jnp.exp(sc - m_new)
            l_s[...] = alpha * l_s[...] + jnp.sum(p, axis=1, keepdims=True)
            acc_s[...] = alpha * acc_s[...] + _dot(p.astype(BF16), v_ref[0])
            m_s[...] = m_new

        if tri:
            pl.when(j <= i)(compute)
        else:
            compute()

        @pl.when(j == nk - 1)
        def _():
            o_ref[0] = (acc_s[...] / l_s[...]).astype(o_ref.dtype)
            lse_ref[0] = m_s[...] + jnp.log(l_s[...])

    kv_j = (lambda hh, i, j: (hh, jnp.minimum(i, j), 0)) if tri else (lambda hh, i, j: (hh, j, 0))
    in_specs = [pl.BlockSpec((1, bq, dk), lambda hh, i, j: (hh, i, 0)), pl.BlockSpec((1, bk, dk), kv_j),
                pl.BlockSpec((1, bk, dv), kv_j)]
    args = [q, k, v]
    if bias:
        ck_j = (lambda hh, i, j: (hh, 0, jnp.minimum(i, j))) if tri else (lambda hh, i, j: (hh, 0, j))
        in_specs += [pl.BlockSpec((1, bq, 1), lambda hh, i, j: (hh, i, 0)), pl.BlockSpec((1, 1, bk), ck_j)]
        args += [cq, ck]
    return pl.pallas_call(
        body, name=name, grid=(h, nq, nk), in_specs=in_specs,
        out_specs=[pl.BlockSpec((1, bq, dv), lambda hh, i, j: (hh, i, 0)),
                   pl.BlockSpec((1, bq, 1), lambda hh, i, j: (hh, i, 0))],
        out_shape=[jax.ShapeDtypeStruct((h, s, dv), BF16), jax.ShapeDtypeStruct((h, s, 1), F32)],
        scratch_shapes=[pltpu.VMEM((bq, 1), F32), pltpu.VMEM((bq, 1), F32), pltpu.VMEM((bq, dv), F32)],
        compiler_params=_params('arbitrary', 'arbitrary', 'arbitrary'))(*args)


def _attn_bwd(q, k, v, o, do, lse, cq, ck, *, scale, mode, name, blk=512):
    h, s, dk = q.shape
    t, dv = k.shape[1], v.shape[2]
    bq = min(blk, s)
    bk = min(blk, t)
    nq, nk = s // bq, t // bk
    bias = cq is not None
    tri = mode != 'full'

    def body(*refs):
        if bias:
            (q_ref, k_ref, v_ref, o_ref, do_ref, lse_ref, cq_ref, ck_ref, dq_ref, dk_ref, dv_ref, dck_ref, dcq_ref,
             dk_s, dv_s, dck_s) = refs
        else:
            q_ref, k_ref, v_ref, o_ref, do_ref, lse_ref, dq_ref, dk_ref, dv_ref, dk_s, dv_s = refs
        j, i = pl.program_id(1), pl.program_id(2)

        @pl.when((j == 0) & (i == 0))
        def _():
            dq_ref[...] = jnp.zeros_like(dq_ref)
            if bias:
                dcq_ref[...] = jnp.zeros_like(dcq_ref)

        @pl.when(i == 0)
        def _():
            dk_s[...] = jnp.zeros_like(dk_s)
            dv_s[...] = jnp.zeros_like(dv_s)
            if bias:
                dck_s[...] = jnp.zeros_like(dck_s)

        def compute():
            qv, kv, vv, dov = q_ref[0], k_ref[0], v_ref[0], do_ref[0]
            sc = _dot(qv, kv, NT) * scale
            if bias:
                sc = sc + cq_ref[0] - ck_ref[0]
            if tri:
                sc = jnp.where(_mask(mode, i, j, bq, bk), sc, MASK_VALUE)
            p = jnp.exp(sc - lse_ref[0])
            dp = _dot(dov, vv, NT)
            delta = jnp.sum(dov.astype(F32) * o_ref[0].astype(F32), axis=1, keepdims=True)
            ds = p * (dp - delta)
            dsb = ds.astype(BF16)
            dv_s[...] += _dot(p.astype(BF16), dov, TN)
            dk_s[...] += scale * _dot(dsb, qv, TN)
            row0 = pl.multiple_of(i * bq, bq)
            dq_ref[0, pl.ds(row0, bq), :] += scale * _dot(dsb, kv)
            if bias:
                dck_s[...] -= jnp.sum(ds, axis=0, keepdims=True)
                dcq_ref[0, pl.ds(row0, bq), :] += jnp.sum(ds, axis=1, keepdims=True)

        if tri:
            pl.when(i >= j)(compute)
        else:
            compute()

        @pl.when(i == nq - 1)
        def _():
            dk_ref[0] = dk_s[...]
            dv_ref[0] = dv_s[...]
            if bias:
                dck_ref[0] = dck_s[...]

    q_i = (lambda hh, j, i: (hh, jnp.maximum(i, j), 0)) if tri else (lambda hh, j, i: (hh, i, 0))
    in_specs = [pl.BlockSpec((1, bq, dk), q_i), pl.BlockSpec((1, bk, dk), lambda hh, j, i: (hh, j, 0)),
                pl.BlockSpec((1, bk, dv), lambda hh, j, i: (hh, j, 0)), pl.BlockSpec((1, bq, dv), q_i),
                pl.BlockSpec((1, bq, dv), q_i), pl.BlockSpec((1, bq, 1), q_i)]
    args = [q, k, v, o, do, lse]
    out_specs = [pl.BlockSpec((1, s, dk), lambda hh, j, i: (hh, 0, 0)),
                 pl.BlockSpec((1, bk, dk), lambda hh, j, i: (hh, j, 0)),
                 pl.BlockSpec((1, bk, dv), lambda hh, j, i: (hh, j, 0))]
    out_shape = [jax.ShapeDtypeStruct((h, s, dk), F32), jax.ShapeDtypeStruct((h, t, dk), F32),
                 jax.ShapeDtypeStruct((h, t, dv), F32)]
    scratch = [pltpu.VMEM((bk, dk), F32), pltpu.VMEM((bk, dv), F32)]
    if bias:
        in_specs += [pl.BlockSpec((1, bq, 1), q_i), pl.BlockSpec((1, 1, bk), lambda hh, j, i: (hh, 0, j))]
        args += [cq, ck]
        out_specs += [pl.BlockSpec((1, 1, bk), lambda hh, j, i: (hh, 0, j)),
                      pl.BlockSpec((1, s, 1), lambda hh, j, i: (hh, 0, 0))]
        out_shape += [jax.ShapeDtypeStruct((h, 1, t), F32), jax.ShapeDtypeStruct((h, s, 1), F32)]
        scratch.append(pltpu.VMEM((1, bk), F32))
    return pl.pallas_call(body, name=name, grid=(h, nk, nq), in_specs=in_specs, out_specs=out_specs,
                          out_shape=out_shape, scratch_shapes=scratch,
                          compiler_params=_params('arbitrary', 'arbitrary', 'arbitrary'))(*args)


def _gla_chunk(la_c, k_c):
    r = lax.broadcasted_iota(jnp.int32, (CHUNK, CHUNK), 0)
    c = lax.broadcasted_iota(jnp.int32, (CHUNK, CHUNK), 1)
    tri = jnp.where(c <= r, 1.0, 0.0).astype(BF16)
    cum = _tri_dot(tri, la_c)
    end = jnp.sum(la_c, axis=0, keepdims=True)
    dec = jnp.exp(end - cum)
    return dec, k_c * dec, jnp.exp(end)


def _gla_fwd(q, k, v, la, *, name, blk=512):
    h, s, dk = q.shape
    dv = v.shape[2]
    bs = min(blk, s)
    ncb = bs // CHUNK
    nb = s // bs

    def body(q_ref, k_ref, v_ref, la_ref, o_ref, st_ref, st):
        @pl.when(pl.program_id(1) == 0)
        def _():
            st[...] = jnp.zeros_like(st)

        for c in range(ncb):
            sl = pl.ds(c * CHUNK, CHUNK)
            _, kf, a = _gla_chunk(la_ref[0, sl, :], k_ref[0, sl, :])
            ut = _dot(v_ref[0, sl, :].astype(BF16), kf.astype(BF16), TN)
            new = a * st[...] + ut
            st[...] = new
            st_ref[0, c] = new
            qs = (q_ref[0, sl, :] * (GLA_DK ** -0.5)).astype(BF16)
            o_ref[0, sl, :] = _dot(qs, new.astype(BF16), NT)

    row = lambda w: pl.BlockSpec((1, bs, w), lambda hh, b: (hh, b, 0))
    return pl.pallas_call(
        body, name=name, grid=(h, nb), in_specs=[row(dk), row(dk), row(dv), row(dk)],
        out_specs=[row(dv), pl.BlockSpec((1, ncb, dv, dk), lambda hh, b: (hh, b, 0, 0))],
        out_shape=[jax.ShapeDtypeStruct((h, s, dv), F32), jax.ShapeDtypeStruct((h, s // CHUNK, dv, dk), F32)],
        scratch_shapes=[pltpu.VMEM((dv, dk), F32)], compiler_params=_params('arbitrary', 'arbitrary'))(q, k, v, la)


def _gla_bwd(q, k, v, la, st_all, st_prev, do, *, name, blk=512):
    h, s, dk = q.shape
    dv = v.shape[2]
    bs = min(blk, s)
    ncb = bs // CHUNK
    nb = s // bs

    def body(q_ref, k_ref, v_ref, la_ref, st_ref, sp_ref, do_ref, dq_ref, dk_ref, dv_ref, dla_ref, ga):
        @pl.when(pl.program_id(1) == 0)
        def _():
            ga[...] = jnp.zeros_like(ga)

        r = lax.broadcasted_iota(jnp.int32, (CHUNK, CHUNK), 0)
        cc = lax.broadcasted_iota(jnp.int32, (CHUNK, CHUNK), 1)
        tri_rev = jnp.where(cc >= r, 1.0, 0.0).astype(BF16)
        for c in reversed(range(ncb)):
            sl = pl.ds(c * CHUNK, CHUNK)
            dec, kf, a = _gla_chunk(la_ref[0, sl, :], k_ref[0, sl, :])
            kd = kf.astype(BF16)
            qs = (q_ref[0, sl, :] * (GLA_DK ** -0.5)).astype(BF16)
            dob = do_ref[0, sl, :].astype(BF16)
            g = _dot(dob, qs, TN) + ga[...]
            gb = g.astype(BF16)
            dq_ref[0, sl, :] = (GLA_DK ** -0.5) * _dot(dob, st_ref[0, c].astype(BF16))
            dv_ref[0, sl, :] = _dot(kd, gb, NT)
            dkd = _dot(v_ref[0, sl, :].astype(BF16), gb)
            dk_ref[0, sl, :] = dkd * dec
            e = dkd * kf
            da = jnp.sum(g * sp_ref[0, c], axis=0, keepdims=True)
            dend = jnp.sum(e, axis=0, keepdims=True) + da * a
            dla_ref[0, sl, :] = dend - _tri_dot(tri_rev, e)
            ga[...] = a * g

    row = lambda w: pl.BlockSpec((1, bs, w), lambda hh, b: (hh, nb - 1 - b, 0))
    stspec = pl.BlockSpec((1, ncb, dv, dk), lambda hh, b: (hh, nb - 1 - b, 0, 0))
    return pl.pallas_call(
        body, name=name, grid=(h, nb), in_specs=[row(dk), row(dk), row(dv), row(dk), stspec, stspec, row(dv)],
        out_specs=[row(dk), row(dk), row(dv), row(dk)],
        out_shape=[jax.ShapeDtypeStruct((h, s, dk), F32), jax.ShapeDtypeStruct((h, s, dk), F32),
                   jax.ShapeDtypeStruct((h, s, dv), F32), jax.ShapeDtypeStruct((h, s, dk), F32)],
        scratch_shapes=[pltpu.VMEM((dv, dk), F32)],
        compiler_params=_params('arbitrary', 'arbitrary'))(q, k, v, la, st_all, st_prev, do)


def _place():
    return lax.axis_index('x'), lax.axis_index('y'), lax.axis_index('c')


ANY = pl.BlockSpec(memory_space=pl.ANY)


def _all_gather8(blk, *, name):
    m, n = blk.shape

    def body(x_ref, out_ref, send_sems, recv_sems, local_sem):
        x, y, c = _place()
        me, sibling = (x, y, c), (x, y, 1 - c)
        chips = [(1 - x, y), (x, 1 - y), (1 - x, 1 - y)]

        def slot(px, py, pc):
            return out_ref.at[4 * px + 2 * py + pc]

        def copy(q, block, to, src=None):
            return pltpu.make_async_remote_copy(
                src_ref=slot(*block) if src is None else src, dst_ref=slot(*block), send_sem=send_sems.at[q],
                recv_sem=recv_sems.at[q], device_id=to, device_id_type=MESH)

        mine = pltpu.make_async_copy(x_ref, slot(*me), local_sem)
        mine.start()
        first = [copy(0, me, sibling, src=x_ref)]
        first += [copy(1 + q, me, (*chip, c), src=x_ref) for q, chip in enumerate(chips)]
        for cp in first:
            cp.start()
        passed = [copy(4 + q, (*chip, c), sibling) for q, chip in enumerate(chips)]
        for q, chip in enumerate(chips):
            copy(1 + q, (*chip, c), me).wait_recv()
            passed[q].start()
        copy(0, sibling, me).wait_recv()
        for q, chip in enumerate(chips):
            copy(4 + q, (*chip, 1 - c), me).wait_recv()
        for cp in first + passed:
            cp.wait_send()
        mine.wait()

    return pl.pallas_call(
        body, name=name, in_specs=[ANY], out_specs=ANY, out_shape=jax.ShapeDtypeStruct((N_DEV, m, n), blk.dtype),
        scratch_shapes=[pltpu.SemaphoreType.DMA((7,)), pltpu.SemaphoreType.DMA((7,)), pltpu.SemaphoreType.DMA(())],
    )(blk)


def _sems(*counts):
    return [pltpu.SemaphoreType.DMA((n,)) for n in counts]


def _gather_shards(ws, *, name):
    n = len(ws)

    def body(*refs):
        ins, outs = refs[:n], refs[n:2 * n]
        send_sems, recv_sems = refs[2 * n:]
        x, y, c = _place()
        me_chip = 2 * x + y
        sibling = (x, y, 1 - c)
        chips = [(1 - x, y), (x, 1 - y), (1 - x, 1 - y)]

        def copy(q, k, src, dst, to):
            return pltpu.make_async_remote_copy(src_ref=src, dst_ref=dst, send_sem=send_sems.at[7 * q + k],
                                                recv_sem=recv_sems.at[7 * q + k], device_id=to, device_id_type=MESH)

        sent = []
        for q in range(n):
            for k, (px, py) in enumerate(chips):
                sent.append(copy(q, k, ins[q].at[c], outs[q].at[me_chip, c], (px, py, c)))
                sent[-1].start()
            sent.append(copy(q, 6, ins[q], outs[q].at[me_chip], sibling))
            sent[-1].start()
        for q in range(n):
            for k, (px, py) in enumerate(chips):
                slot = outs[q].at[2 * px + py, c]
                copy(q, k, slot, slot, (px, py, c)).wait_recv()
                sent.append(copy(q, 3 + k, slot, slot, sibling))
                sent[-1].start()
        for q in range(n):
            for k, (px, py) in enumerate(chips):
                slot = outs[q].at[2 * px + py, 1 - c]
                copy(q, 3 + k, slot, slot, sibling).wait_recv()
            copy(q, 6, ins[q], outs[q].at[me_chip], sibling).wait_recv()
        for cp in sent:
            cp.wait_send()

    return pl.pallas_call(
        body, name=name, in_specs=[ANY] * n, out_specs=[ANY] * n,
        out_shape=[jax.ShapeDtypeStruct((N_CHIPS,) + w.shape, w.dtype) for w in ws],
        scratch_shapes=_sems(7 * n, 7 * n))(*ws)


def _to_sibling(gs, *, name):
    n = len(gs)

    def body(*refs):
        ins, outs = refs[:n], refs[n:2 * n]
        send_sems, recv_sems = refs[2 * n:]
        x, y, c = _place()
        cps = [pltpu.make_async_remote_copy(
            src_ref=ins[q], dst_ref=outs[q], send_sem=send_sems.at[q], recv_sem=recv_sems.at[q],
            device_id=(x, y, 1 - c), device_id_type=MESH) for q in range(n)]
        for cp in cps:
            cp.start()
        for cp in cps:
            cp.wait()

    return pl.pallas_call(body, name=name, in_specs=[ANY] * n, out_specs=[ANY] * n,
                          out_shape=[jax.ShapeDtypeStruct(g.shape, g.dtype) for g in gs],
                          scratch_shapes=_sems(n, n))(*gs)


def _chip_exchange(ps, *, name):
    n = len(ps)

    def body(*refs):
        ins, outs = refs[:n], refs[n:2 * n]
        send_sems, recv_sems = refs[2 * n:]
        x, y, c = _place()
        chips = [(1 - x, y), (x, 1 - y), (1 - x, 1 - y)]
        cps = []
        for q in range(n):
            for k, (px, py) in enumerate(chips):
                cps.append(pltpu.make_async_remote_copy(
                    src_ref=ins[q].at[2 * px + py], dst_ref=outs[q].at[k], send_sem=send_sems.at[3 * q + k],
                    recv_sem=recv_sems.at[3 * q + k], device_id=(px, py, c), device_id_type=MESH))
                cps[-1].start()
        for cp in cps:
            cp.wait()

    return pl.pallas_call(body, name=name, in_specs=[ANY] * n, out_specs=[ANY] * n,
                          out_shape=[jax.ShapeDtypeStruct((3,) + p.shape[1:], p.dtype) for p in ps],
                          scratch_shapes=_sems(3 * n, 3 * n))(*ps)


def _sum_chips(own, r, *, name, ts=256):
    k, n = own.shape
    ts = min(ts, k)

    def body(own_ref, r_ref, o_ref):
        f = lambda q: r_ref[q].astype(F32)
        o_ref[...] = ((own_ref[...].astype(F32) + f(0)) + f(1)) + f(2)

    return pl.pallas_call(
        body, name=name, grid=(k // ts,),
        in_specs=[pl.BlockSpec((ts, n), lambda i: (i, 0)), pl.BlockSpec((3, ts, n), lambda i: (0, i, 0))],
        out_specs=pl.BlockSpec((ts, n), lambda i: (i, 0)), out_shape=jax.ShapeDtypeStruct((k, n), F32),
        compiler_params=_params('arbitrary'))(own, r)


WIN_SHARD = N_IN // N_CHIPS
WIN_PAD = -(-WIN_SHARD // LANES) * LANES


def _full_layer(gathered, axis, l):
    sh = gathered[:, l]
    _, k, n = sh.shape
    if axis == 2:
        return sh.transpose(1, 0, 2).reshape(k, N_CHIPS * n)
    return sh.reshape(N_CHIPS * k, n)


def _win_cols(wp, o, n):
    parts = []
    while n > 0:
        j, r = divmod(o, WIN_SHARD)
        take = min(n, WIN_SHARD - r)
        parts.append(wp[:, j * WIN_PAD + r:j * WIN_PAD + r + take])
        o, n = o + take, n - take
    return parts[0] if len(parts) == 1 else jnp.concatenate(parts, axis=1)


def _split_full(full, axis):
    l, k, n = full.shape
    if axis == 2:
        return jnp.stack([full[:, :, j * (n // N_CHIPS):(j + 1) * (n // N_CHIPS)] for j in range(N_CHIPS)])
    return full.reshape(l, N_CHIPS, k // N_CHIPS, n).transpose(1, 0, 2, 3)


def _heads(z, n):
    s, w = z.shape
    return z.reshape(s, n, w // n).transpose(1, 0, 2)


def _merge(z):
    n, s, d = z.shape
    return z.transpose(1, 0, 2).reshape(s, n * d)


def _padc(a, w):
    return jnp.pad(a, ((0, 0), (0, w - a.shape[1])))


def _swap16(a):
    return jnp.concatenate([a[..., 16:32], a[..., 0:16]], axis=-1)


B_FF, B_GLOW, B_MKV, B_MKR, B_MKRS, B_MQ, B_GR, B_GQ, B_GK, B_GV, B_END = (
    0, 128, 256, 384, 512, 640, 896, 1408, 1664, 1920, 2432)
B_W = 2560
O_FQ, O_FF, O_GQ, O_GLOW, O_GR, O_MQ, O_MKV, O_MKR, O_ZG = 0, 768, 772, 1796, 1812, 2324, 2580, 2708, 2740


def _repack_layer_weights(w):
    wi = functools.partial(_win_cols, w['w_in'])
    out = dict(w)
    out['in_a'] = wi(O_FQ, 768)
    kr = wi(O_MKR, 32)
    out['in_b'] = jnp.concatenate([
        _padc(wi(O_FF, 4), 128), _padc(wi(O_GLOW, 16), 128), wi(O_MKV, 128), _padc(kr, 128), _padc(_swap16(kr), 128),
        wi(O_MQ, 256), wi(O_GR, 512), wi(O_GQ, 1024), jnp.zeros((D_MODEL, B_W - B_END), kr.dtype)], axis=1)
    out['in_c'] = wi(O_ZG, 3072)
    uq = w['w_mla_uq'].reshape(MLA_Q_RANK, MLA_HEADS, MLA_NOPE + MLA_ROPE)
    rope = uq[:, :, MLA_NOPE:]
    out['uq'] = jnp.concatenate([uq[:, :, :MLA_NOPE].reshape(MLA_Q_RANK, -1), rope.reshape(MLA_Q_RANK, -1),
                                 _swap16(rope).reshape(MLA_Q_RANK, -1)], axis=1)
    ukv = w['w_mla_ukv'].reshape(MLA_KV_RANK, MLA_HEADS, MLA_NOPE + MLA_VD)
    out['ukv'] = jnp.concatenate([ukv[:, :, :MLA_NOPE].reshape(MLA_KV_RANK, -1),
                                  ukv[:, :, MLA_NOPE:].reshape(MLA_KV_RANK, -1)], axis=1)
    out['gate'] = jnp.pad(w['w_gla_gate'], ((0, 128 - GLA_RANK), (0, 0)))
    return out


def _unpack_layer_grads(g):
    a, b, c = g['in_a'], g['in_b'], g['in_c']
    kr = b[:, B_MKR:B_MKR + 32] + _swap16(b[:, B_MKRS:B_MKRS + 32])
    w_in = jnp.concatenate([a, b[:, B_FF:B_FF + 4], b[:, B_GQ:B_GQ + 1024], b[:, B_GLOW:B_GLOW + 16],
                            b[:, B_GR:B_GR + 512], b[:, B_MQ:B_MQ + 256], b[:, B_MKV:B_MKV + 128], kr, c], axis=1)
    uq = g['uq']
    nope = uq[:, :256].reshape(MLA_Q_RANK, MLA_HEADS, MLA_NOPE)
    rope = (uq[:, 256:384].reshape(MLA_Q_RANK, MLA_HEADS, MLA_ROPE)
            + _swap16(uq[:, 384:512].reshape(MLA_Q_RANK, MLA_HEADS, MLA_ROPE)))
    w_uq = jnp.concatenate([nope, rope], axis=2).reshape(MLA_Q_RANK, -1)
    ukv = g['ukv']
    w_ukv = jnp.concatenate([ukv[:, :256].reshape(MLA_KV_RANK, MLA_HEADS, MLA_NOPE),
                             ukv[:, 256:].reshape(MLA_KV_RANK, MLA_HEADS, MLA_VD)], axis=2).reshape(MLA_KV_RANK, -1)
    out = {'w_in': w_in, 'w_mla_uq': w_uq, 'w_mla_ukv': w_ukv, 'w_gla_gate': g['gate'][:GLA_RANK]}
    for nm in ('w_up_fox', 'w_up_gla', 'w_up_mla', 'w_out', 'w_xq', 'w_xkv', 'w_xo', 'w_mlp1', 'w_mlp2'):
        out[nm] = g[nm]
    return out


def _rope_tables(s):
    half = MLA_ROPE // 2
    inv = ROPE_BASE ** (-jnp.arange(half, dtype=F32) / half)
    ang = jnp.arange(s).astype(F32)[:, None] * inv[None, :]
    cos, sin = jnp.cos(ang), jnp.sin(ang)
    c1 = jnp.concatenate([cos, cos], axis=1)
    s1 = jnp.concatenate([-sin, sin], axis=1)
    return jnp.tile(c1, (1, MLA_HEADS)), jnp.tile(s1, (1, MLA_HEADS)), _padc(c1, 128), _padc(s1, 128)


def _rms_bwd(x, dh, g):
    r = lax.rsqrt(jnp.mean(x * x, axis=-1, keepdims=True) + EPS)
    xh = x * r
    gd = dh * g
    return r * (gd - xh * jnp.mean(gd * xh, axis=-1, keepdims=True)), dh * xh


def _norm_bwd_call(x, dh, g, dres, name):
    w = x.shape[1]

    def with_res(xv, dv, rv, gv):
        dx, dg = _rms_bwd(xv, dv.astype(F32), gv)
        return rv + dx, dg

    def plain(xv, dv, gv):
        return _rms_bwd(xv, dv.astype(F32), gv)

    if dres is None:
        return _rowwise(plain, [x, dh], [g], [(w, F32)], [w], name=name)
    return _rowwise(with_res, [x, dh, dres], [g], [(w, F32)], [w], name=name)


def _gla_out_fwd(oraw, gr, g_out):
    outs = []
    for hh in range(GLA_HEADS):
        sl = slice(hh * GLA_DV, (hh + 1) * GLA_DV)
        oh = oraw[:, sl]
        n = oh * lax.rsqrt(jnp.mean(oh * oh, axis=-1, keepdims=True) + EPS) * g_out
        r = gr[:, sl]
        outs.append(n * (r * _sig(r)))
    return (jnp.concatenate(outs, axis=1),)


def _gla_out_bwd(oraw, gr, dout, g_out):
    d_o, d_r, dg = [], [], 0.0
    for hh in range(GLA_HEADS):
        sl = slice(hh * GLA_DV, (hh + 1) * GLA_DV)
        oh, r, do = oraw[:, sl], gr[:, sl], dout[:, sl].astype(F32)
        rs = lax.rsqrt(jnp.mean(oh * oh, axis=-1, keepdims=True) + EPS)
        sg = _sig(r)
        dn = do * (r * sg)
        d_r.append(do * (oh * rs * g_out) * (sg + r * sg * (1.0 - sg)))
        dx, dgh = _rms_bwd(oh, dn, g_out)
        d_o.append(dx)
        dg = dg + dgh
    return jnp.concatenate(d_o, axis=1), jnp.concatenate(d_r, axis=1), dg


def _adam(w, g, m, v):
    m = ADAM_B1 * m + (1.0 - ADAM_B1) * g
    v = ADAM_B2 * v + (1.0 - ADAM_B2) * (g * g)
    m_hat = m / (1.0 - ADAM_B1 ** ADAM_STEP)
    v_hat = v / (1.0 - ADAM_B2 ** ADAM_STEP)
    return -ADAM_LR * (m_hat / (jnp.sqrt(v_hat) + ADAM_EPS) + ADAM_WD * w), m, v


def _layer_fwd(x, mem, w, p, tabs, tag):
    c4, s4, ck, sk = tabs
    sv = {'x0': x}
    nm = lambda t: f'{t}_{tag}'
    za, h = _mm(x, w['in_a'], mode='nn', out_dtype=BF16, norm_g=p['g_mix'], emit_norm=True, tn=768, name=nm('in_a'))
    zb = _mm(h, w['in_b'], mode='nn', out_dtype=F32, name=nm('in_b'))
    zc = _mm(h, w['in_c'], mode='nn', out_dtype=F32, name=nm('in_c'))
    sv.update(h=h, zc=zc)
    ff = zb[:, B_FF:B_FF + 128]
    (lf,) = _rowwise(lambda f, b: (_logsig(f + b),), [ff], [p['b_fox']], [(128, F32)], name=nm('fox_lf'))
    cum = _cumsum_rows(lf, reverse=False, name=nm('fox_cum'))
    cumh = cum[:, :FOX_HEADS].T
    fq, fk, fv = (_heads(za[:, o:o + 256], FOX_HEADS) for o in (0, 256, 512))
    cq, ckk = cumh[:, :, None], cumh[:, None, :]
    o_fox, lse_fox = _attn_fwd(fq, fk, fv, cq, ckk, scale=FOX_HD ** -0.5, mode='causal', name=nm('fox_attn'))
    sv.update(ff=ff, fq=fq, fk=fk, fv=fv, cq=cq, ck=ckk, o_fox=o_fox, lse_fox=lse_fox)
    glow = zb[:, B_GLOW:B_GLOW + 128]
    gr = zb[:, B_GR:B_GR + 512]

    def gate_fn(gl, wg, bg):
        return (_logsig(_dot(gl.astype(BF16), wg) + bg) / GLA_TAU,)

    (la,) = _rowwise(gate_fn, [glow], [w['gate'], p['b_gla']], [(256, F32)], name=nm('gla_gate'))
    gq, gk, lah = (_heads(t, GLA_HEADS) for t in (zb[:, B_GQ:B_GQ + 256], zb[:, B_GK:B_GK + 256], la))
    gv = _heads(zb[:, B_GV:B_GV + 512], GLA_HEADS)
    oraw_h, states = _gla_fwd(gq, gk, gv, lah, name=nm('gla'))
    oraw = _merge(oraw_h)
    (o_gla,) = _rowwise(_gla_out_fwd, [oraw, gr], [p['g_gla_out']], [(512, BF16)], name=nm('gla_out'))
    sv.update(glow=glow, gr=gr, gq=gq, gk=gk, gv=gv, lah=lah, states=states, oraw=oraw, o_gla=o_gla)
    mq = zb[:, B_MQ:B_MQ + 256]
    mkv = zb[:, B_MKV:B_MKV + 128]
    mkr2 = zb[:, B_MKR:B_MKR + 256]
    qp, cqn = _mm(mq, w['uq'], mode='nn', out_dtype=F32, norm_g=p['g_mla_q'], emit_norm=True, name=nm('mla_uq'))
    kvp, ckvn = _mm(mkv, w['ukv'], mode='nn', out_dtype=BF16, norm_g=p['g_mla_kv'], emit_norm=True,
                    name=nm('mla_ukv'))

    def rope_fn(qv, kr, c4v, s4v, ckv, skv):
        return qv[:, 256:384] * c4v + qv[:, 384:512] * s4v, kr[:, 0:128] * ckv + kr[:, 128:256] * skv

    q_rope, k_rope = _rowwise(rope_fn, [qp, mkr2, c4, s4, ck, sk], [], [(128, BF16), (128, BF16)], name=nm('rope'))
    qh = jnp.concatenate([_heads(qp[:, :256].astype(BF16), MLA_HEADS), _heads(q_rope, MLA_HEADS)], axis=2)
    kh = jnp.concatenate([_heads(kvp[:, :256], MLA_HEADS),
                          jnp.broadcast_to(k_rope[None, :, :MLA_ROPE], (MLA_HEADS, x.shape[0], MLA_ROPE))], axis=2)
    vh = _heads(kvp[:, 256:], MLA_HEADS)
    o_mla, lse_mla = _attn_fwd(qh, kh, vh, None, None, scale=(MLA_NOPE + MLA_ROPE) ** -0.5, mode='chunk',
                               name=nm('mla_attn'))
    sv.update(mq=mq, mkv=mkv, cqn=cqn, ckvn=ckvn, qh=qh, kh=kh, vh=vh, o_mla=o_mla, lse_mla=lse_mla)
    of_m, om_m = _merge(o_fox), _merge(o_mla)
    sv.update(of_m=of_m, om_m=om_m)
    b_br = p['b_branch']

    def first(acc, zg, bb):
        return _sig(zg + bb) * acc

    def more(acc, zg, bb, prev):
        return prev + _sig(zg + bb) * acc

    y = _mm(of_m, w['w_up_fox'], mode='nn', out_dtype=F32, name=nm('up_fox'), epilogue=first,
            extras=[(zc, *_mn(col_off=0)), (b_br, *_nvec(col_off=0))])
    y = _mm(o_gla, w['w_up_gla'], mode='nn', out_dtype=F32, name=nm('up_gla'), epilogue=more,
            extras=[(zc, *_mn(col_off=2)), (b_br, *_nvec(col_off=2)), (y, *_mn())])
    y = _mm(om_m, w['w_up_mla'], mode='nn', out_dtype=BF16, name=nm('up_mla'), epilogue=more,
            extras=[(zc, *_mn(col_off=4)), (b_br, *_nvec(col_off=4)), (y, *_mn())])
    add = lambda acc, res: res + acc
    x1 = _mm(y, w['w_out'], mode='nn', out_dtype=F32, name=nm('out'), epilogue=add, extras=[(x, *_mn())])
    sv.update(y=y, x1=x1)
    qx, hx = _mm(x1, w['w_xq'], mode='nn', out_dtype=BF16, norm_g=p['g_xa'], emit_norm=True, name=nm('xq'))
    kvx, mn = _mm(mem, w['w_xkv'], mode='nn', out_dtype=BF16, norm_g=p['g_mem'], emit_norm=True, name=nm('xkv'))
    qxh, kxh, vxh = _heads(qx, XA_HEADS), _heads(kvx[:, :512], XA_HEADS), _heads(kvx[:, 512:], XA_HEADS)
    ox, lse_x = _attn_fwd(qxh, kxh, vxh, None, None, scale=XA_HD ** -0.5, mode='full', name=nm('xa_attn'))
    ox_m = _merge(ox)
    x2 = _mm(ox_m, w['w_xo'], mode='nn', out_dtype=F32, name=nm('xo'), epilogue=add, extras=[(x1, *_mn())])
    sv.update(hx=hx, mn=mn, qxh=qxh, kxh=kxh, vxh=vxh, ox=ox, lse_x=lse_x, ox_m=ox_m, x2=x2)
    hpre, hm = _mm(x2, w['w_mlp1'], mode='nn', out_dtype=BF16, norm_g=p['g_mlp'], emit_norm=True, name=nm('mlp1'))
    relu2 = lambda t: jnp.square(jnp.maximum(t.astype(F32), 0.0))
    x3 = _mm(hpre, w['w_mlp2'], mode='nn', out_dtype=F32, name=nm('mlp2'), a_fn=relu2, epilogue=add,
             extras=[(x2, *_mn())])
    sv.update(hpre=hpre, hm=hm)
    return x3, sv


def _layer_bwd(dx3, mem, w, p, tabs, sv, tag):
    c4, s4, ck, sk = tabs
    nm = lambda t: f'{t}_{tag}'
    s = dx3.shape[0]
    gw, gs = {}, {}
    relu2 = lambda t: jnp.square(jnp.maximum(t.astype(F32), 0.0))
    gw['w_mlp2'] = _mm(sv['hpre'], dx3, mode='tn', out_dtype=F32, name=nm('d_mlp2'), a_fn=relu2, tn=256)
    dact = lambda acc, hp: acc * (2.0 * jnp.maximum(hp.astype(F32), 0.0))
    dhpre = _mm(dx3, w['w_mlp2'], mode='nt', out_dtype=BF16, name=nm('d_act'), epilogue=dact,
                extras=[(sv['hpre'], *_mn())])
    gw['w_mlp1'] = _mm(sv['hm'], dhpre, mode='tn', out_dtype=F32, name=nm('d_mlp1'))
    dhm = _mm(dhpre, w['w_mlp1'], mode='nt', out_dtype=F32, name=nm('d_hm'))
    dx2, gs['g_mlp'] = _norm_bwd_call(sv['x2'], dhm, p['g_mlp'], dx3, nm('d_norm_mlp'))
    gw['w_xo'] = _mm(sv['ox_m'], dx2, mode='tn', out_dtype=F32, name=nm('d_xo'), tn=256)
    dox = _mm(dx2, w['w_xo'], mode='nt', out_dtype=BF16, name=nm('d_ox'))
    dqx, dkx, dvx = _attn_bwd(sv['qxh'], sv['kxh'], sv['vxh'], sv['ox'], _heads(dox, XA_HEADS), sv['lse_x'], None,
                              None, scale=XA_HD ** -0.5, mode='full', name=nm('xa_bwd'))
    dqx_m = _merge(dqx).astype(BF16)
    dkvx = jnp.concatenate([_merge(dkx), _merge(dvx)], axis=1).astype(BF16)
    gw['w_xq'] = _mm(sv['hx'], dqx_m, mode='tn', out_dtype=F32, name=nm('d_xq'))
    dhx = _mm(dqx_m, w['w_xq'], mode='nt', out_dtype=F32, name=nm('d_hx'))
    gw['w_xkv'] = _mm(sv['mn'], dkvx, mode='tn', out_dtype=F32, name=nm('d_xkv'))
    dmn = _mm(dkvx, w['w_xkv'], mode='nt', out_dtype=F32, name=nm('d_mn'))
    _, gs['g_mem'] = _norm_bwd_call(mem, dmn, p['g_mem'], None, nm('d_norm_mem'))
    dx1, gs['g_xa'] = _norm_bwd_call(sv['x1'], dhx, p['g_xa'], dx2, nm('d_norm_xa'))
    gw['w_out'] = _mm(sv['y'], dx1, mode='tn', out_dtype=F32, name=nm('d_out'), tn=256)
    dy = _mm(dx1, w['w_out'], mode='nt', out_dtype=BF16, name=nm('d_y'))
    zc, b_br = sv['zc'], p['b_branch']

    def du_fn(dyv, zg, bb):
        g = _sig(zg + bb)
        d = dyv.astype(F32)
        return d * g[:, 0:1024], d * g[:, 1024:2048], d * g[:, 2048:3072]

    du = _rowwise(du_fn, [dy, zc], [b_br], [(D_MODEL, BF16)] * 3, name=nm('d_u'))

    def dgate(acc, dyv, zg, bb):
        g = _sig(zg + bb)
        return dyv.astype(F32) * acc * g * (1.0 - g)

    dzc, do_br = [], []
    for q, (o_m, wn) in enumerate(((sv['of_m'], 'w_up_fox'), (sv['o_gla'], 'w_up_gla'), (sv['om_m'], 'w_up_mla'))):
        dzc.append(_mm(o_m, w[wn], mode='nn', out_dtype=F32, name=nm(f'd_zg{q}'), epilogue=dgate,
                       extras=[(dy, *_mn()), (zc, *_mn(col_off=2 * q)), (b_br, *_nvec(col_off=2 * q))]))
        gw[wn] = _mm(o_m, du[q], mode='tn', out_dtype=F32, name=nm(f'd_up{q}'))
        do_br.append(_mm(du[q], w[wn], mode='nt', out_dtype=F32 if q == 1 else BF16, name=nm(f'd_o{q}')))
    dzc = jnp.concatenate(dzc, axis=1)
    (gs['b_branch'],) = _rowwise(lambda t: (t,), [dzc], [], [], [3072], name=nm('d_bbranch'))
    dfq, dfk, dfv, dck, dcq = _attn_bwd(sv['fq'], sv['fk'], sv['fv'], sv['o_fox'], _heads(do_br[0], FOX_HEADS),
                                   sv['lse_fox'], sv['cq'], sv['ck'], scale=FOX_HD ** -0.5, mode='causal',
                                   name=nm('fox_bwd'))
    dcum = _padc((dck[:, 0, :] + dcq[:, :, 0]).T, 128)
    dlf = _cumsum_rows(dcum, reverse=True, name=nm('fox_dcum'))

    def dff_fn(dl, f, b):
        d = dl * _sig(-(f + b))
        return d, d

    dff, db_fox = _rowwise(dff_fn, [dlf, sv['ff']], [p['b_fox']], [(128, F32)], [128], name=nm('fox_dff'))
    gs['b_fox'] = db_fox
    dza = jnp.concatenate([_merge(dfq), _merge(dfk), _merge(dfv)], axis=1).astype(BF16)
    dqh, dkh, dvh = _attn_bwd(sv['qh'], sv['kh'], sv['vh'], sv['o_mla'], _heads(do_br[2], MLA_HEADS), sv['lse_mla'],
                              None, None, scale=(MLA_NOPE + MLA_ROPE) ** -0.5, mode='chunk', name=nm('mla_bwd'))
    dq_rope = _merge(dqh[:, :, MLA_NOPE:])
    dk_rope = _padc(jnp.sum(dkh[:, :, MLA_NOPE:], axis=0), 128)

    def drope_fn(dq, dk, c4v, s4v, ckv, skv):
        return dq * c4v, dq * s4v, dk * ckv, dk * skv

    dqr, dqrs, dkr, dkrs = _rowwise(drope_fn, [dq_rope, dk_rope, c4, s4, ck, sk], [], [(128, BF16)] * 4,
                                    name=nm('d_rope'))
    dqp = jnp.concatenate([_merge(dqh[:, :, :MLA_NOPE]).astype(BF16), dqr, dqrs], axis=1)
    dkvp = jnp.concatenate([_merge(dkh[:, :, :MLA_NOPE]), _merge(dvh)], axis=1).astype(BF16)
    gw['uq'] = _mm(sv['cqn'], dqp, mode='tn', out_dtype=F32, name=nm('d_uq'))
    dcqn = _mm(dqp, w['uq'], mode='nt', out_dtype=F32, name=nm('d_cqn'))
    gw['ukv'] = _mm(sv['ckvn'], dkvp, mode='tn', out_dtype=F32, name=nm('d_ukv'))
    dckvn = _mm(dkvp, w['ukv'], mode='nt', out_dtype=F32, name=nm('d_ckvn'))
    dmq, gs['g_mla_q'] = _norm_bwd_call(sv['mq'], dcqn, p['g_mla_q'], None, nm('d_norm_q'))
    dmkv, gs['g_mla_kv'] = _norm_bwd_call(sv['mkv'], dckvn, p['g_mla_kv'], None, nm('d_norm_kv'))
    doraw, dgr, gs['g_gla_out'] = _rowwise(_gla_out_bwd, [sv['oraw'], sv['gr'], do_br[1]], [p['g_gla_out']],
                                           [(512, F32), (512, BF16)], [128], name=nm('d_gla_out'))
    st = sv['states']
    st_prev = jnp.concatenate([jnp.zeros_like(st[:, :1]), st[:, :-1]], axis=1)
    dgq, dgk, dgv, dla = _gla_bwd(sv['gq'], sv['gk'], sv['gv'], sv['lah'], st, st_prev, _heads(doraw, GLA_HEADS),
                                  name=nm('gla_bwd'))

    def dgate_fn(dl, gl, wg, bg):
        pre = _dot(gl.astype(BF16), wg) + bg
        dpre = dl * (1.0 / GLA_TAU) * _sig(-pre)
        return dpre, _dot(dpre.astype(BF16), wg, NT), dpre

    dpre, dglow, gs['b_gla'] = _rowwise(dgate_fn, [_merge(dla), sv['glow']], [w['gate'], p['b_gla']],
                                        [(256, BF16), (128, BF16)], [256], name=nm('d_gla_gate'))
    gw['gate'] = _mm(sv['glow'], dpre, mode='tn', out_dtype=F32, name=nm('d_wgate'))
    bf = lambda t: t.astype(BF16)
    dzb = jnp.concatenate([bf(dff), dglow, bf(dmkv), dkr, dkrs, bf(dmq), dgr, bf(_merge(dgq)), bf(_merge(dgk)),
                           bf(_merge(dgv)), jnp.zeros((s, B_W - B_END), BF16)], axis=1)
    h = sv['h']
    gw['in_a'] = _mm(h, dza, mode='tn', out_dtype=F32, name=nm('d_in_a'), tn=768)
    gw['in_b'] = _mm(h, dzb, mode='tn', out_dtype=F32, name=nm('d_in_b'))
    gw['in_c'] = _mm(h, dzc, mode='tn', out_dtype=F32, name=nm('d_in_c'), tn=256)
    add = lambda acc, prev: prev + acc
    dh = _mm(dza, w['in_a'], mode='nt', out_dtype=F32, name=nm('d_h_a'))
    dh = _mm(dzb, w['in_b'], mode='nt', out_dtype=F32, name=nm('d_h_b'), epilogue=add, extras=[(dh, *_mn())])
    dh = _mm(dzc, w['in_c'], mode='nt', out_dtype=F32, name=nm('d_h_c'), epilogue=add, extras=[(dh, *_mn())])
    dx0, gs['g_mix'] = _norm_bwd_call(sv['x0'], dh, p['g_mix'], dx1, nm('d_norm_mix'))
    return dx0, gw, gs


def _loss_head(x, target, g_final):
    d = x.shape[1]

    def fn(xv, tv, gv):
        r = lax.rsqrt(jnp.mean(xv * xv, axis=-1, keepdims=True) + EPS)
        xh = xv * r
        e = xh * gv - tv
        dy = e * (1.0 / d)
        gd = dy * gv
        dx = r * (gd - xh * jnp.mean(gd * xh, axis=-1, keepdims=True))
        row_loss = 0.5 * jnp.mean(e * e, axis=-1, keepdims=True)
        return dx, dy * xh, jnp.broadcast_to(row_loss, (xv.shape[0], LANES))

    return _rowwise(fn, [x, target], [g_final], [(d, F32)], [d, LANES], name='loss_head')


def _small_sizes(shapes):
    return [math.prod(shapes[nm]) for nm in SMALL]


def _step(args):
    shapes = {nm: args[nm].shape for nm in ORDER}
    x, mem, target = args['x'][0], args['mem'][0], args['loss_target'][0]
    s = x.shape[0]

    def wire(nm):
        w = args[nm].astype(BF16)
        return jnp.pad(w, ((0, 0), (0, 0), (0, WIN_PAD - WIN_SHARD))) if nm == 'w_in' else w

    gathered = dict(zip([nm for nm, _ in BIG], _gather_shards([wire(nm) for nm, _ in BIG], name='gather_weights')))

    tabs = _rope_tables(s)
    layers_w, layers_p = [], []
    for l in range(DEPTH):
        layers_w.append(_repack_layer_weights({nm: _full_layer(gathered[nm], ax, l) for nm, ax in BIG}))
        layers_p.append({
            'g_mix': args['g_mix'][l][None], 'b_fox': _padc(args['b_fox_forget'][l][None], 128),
            'b_gla': args['b_gla_gate'][l][None], 'g_gla_out': args['g_gla_out'][l][None],
            'g_mla_q': args['g_mla_q'][l][None], 'g_mla_kv': args['g_mla_kv'][l][None],
            'b_branch': args['b_branch_gate'][l][None], 'g_xa': args['g_xa'][l][None],
            'g_mem': args['g_mem'][l][None], 'g_mlp': args['g_mlp'][l][None]})

    saved = []
    xl = x
    for l in range(DEPTH):
        xl, sv = _layer_fwd(xl, mem, layers_w[l], layers_p[l], tabs, f'l{l}')
        saved.append(sv)
    dx, dg_final, loss_lanes = _loss_head(xl, target, args['g_final'][None])
    gw_layers, gs_layers = [None] * DEPTH, [None] * DEPTH
    for l in reversed(range(DEPTH)):
        dx, gw, gs = _layer_bwd(dx, mem, layers_w[l], layers_p[l], tabs, saved[l], f'l{l}')
        gw_layers[l], gs_layers[l] = _unpack_layer_grads(gw), gs
    grad_x = dx[None]

    names = [nm for nm, _ in BIG]
    cidx = lax.axis_index('c')
    chip = 2 * lax.axis_index('x') + lax.axis_index('y')
    pick = lambda t, i, axis: lax.dynamic_index_in_dim(t, i, axis=axis, keepdims=False)
    per_chip = [_split_full(jnp.stack([gw_layers[l][nm] for l in range(DEPTH)]), ax).astype(BF16) for nm, ax in BIG]
    got = _to_sibling([pick(g, 1 - cidx, 1) for g in per_chip], name='grads_core_swap')
    pairs = []
    for nm, g, b in zip(names, per_chip, got):
        k, n = b.shape[1:]
        (p,) = _rowwise(lambda u, v: (u.astype(F32) + v.astype(F32),),
                        [pick(g, cidx, 1).reshape(N_CHIPS * k, n), b.reshape(N_CHIPS * k, n)], [], [(n, BF16)],
                        name=f'pair_sum_{nm}')
        pairs.append(p.reshape(N_CHIPS, k, n))
    from_chips = _chip_exchange(pairs, name='grads_chip_exchange')
    mine = [_sum_chips(pick(p, chip, 0), r, name=f'chip_sum_{nm}') for nm, p, r in zip(names, pairs, from_chips)]
    theirs = _to_sibling(mine, name='grads_join')
    gshard = {nm: jnp.where(cidx == 0, jnp.stack([a, b]), jnp.stack([b, a])) for nm, a, b in zip(names, mine, theirs)}

    small_g = []
    for nm, key in (('g_mix', 'g_mix'), ('b_fox_forget', 'b_fox'), ('b_gla_gate', 'b_gla'),
                    ('g_gla_out', 'g_gla_out'), ('g_mla_q', 'g_mla_q'), ('g_mla_kv', 'g_mla_kv'),
                    ('b_branch_gate', 'b_branch'), ('g_xa', 'g_xa'), ('g_mem', 'g_mem'), ('g_mlp', 'g_mlp')):
        width = shapes[nm][1]
        small_g.append(jnp.concatenate([gs_layers[l][key][0, :width] for l in range(DEPTH)]))
    small_g.append(dg_final[0])
    small_g.append(loss_lanes[0, :1])
    flat = jnp.concatenate(small_g)
    n_small = flat.shape[0]
    srows = -(-n_small // (8 * LANES)) * 8
    pad = lambda v: jnp.pad(v, (0, srows * LANES - v.shape[0])).reshape(srows, LANES)
    all_small = _all_gather8(pad(flat), name='gather_small')
    sw, sm, svv = (pad(jnp.concatenate([args[pre + nm].reshape(-1) for nm in SMALL] + [jnp.zeros((1,), F32)]))
                   for pre in ('', 'm_', 'v_'))

    def small_body(g_ref, w_ref, m_ref, v_ref, go_ref, d_ref, mo_ref, vo_ref):
        g = g_ref[0]
        for q in range(1, N_DEV):
            g = g + g_ref[q]
        go_ref[...] = g
        d_ref[...], mo_ref[...], vo_ref[...] = _adam(w_ref[...], g, m_ref[...], v_ref[...])

    sg, sd, snm, snv = pl.pallas_call(
        small_body, name='small_sum_adam', out_shape=[jax.ShapeDtypeStruct((srows, LANES), F32)] * 4,
        compiler_params=pltpu.CompilerParams(vmem_limit_bytes=VMEM_LIMIT))(all_small, sw, sm, svv)

    def unsmall(buf):
        v, out, off = buf.reshape(-1), {}, 0
        for nm in SMALL:
            nel = math.prod(shapes[nm])
            out[nm] = v[off:off + nel].reshape(shapes[nm])
            off += nel
        return out, v[off]

    res = {}
    (res['grad'], loss), (res['delta'], _), (res['m'], _), (res['v'], _) = (unsmall(t) for t in (sg, sd, snm, snv))

    for nm, _ in BIG:
        shp = args[nm].shape
        view = lambda t: t.reshape(shp[0] * shp[1], shp[2])
        d, m2, v2 = _rowwise(_adam, [view(args[nm]), view(gshard[nm]), view(args['m_' + nm]), view(args['v_' + nm])],
                             [], [(shp[2], F32)] * 3, name=f'adam_{nm}')
        res['grad'][nm], res['delta'][nm], res['m'][nm], res['v'][nm] = (
            gshard[nm], d.reshape(shp), m2.reshape(shp), v2.reshape(shp))

    return (loss, grad_x, *[res['grad'][nm] for nm in ORDER], *[res['delta'][nm] for nm in ORDER],
            *[res['m'][nm] for nm in ORDER], *[res['v'][nm] for nm in ORDER])


def kernel(x, mem, g_mix, w_in, b_fox_forget, w_gla_gate, b_gla_gate, g_gla_out, g_mla_q, w_mla_uq, g_mla_kv, w_mla_ukv, b_branch_gate, w_up_fox, w_up_gla, w_up_mla, w_out, g_xa, g_mem, w_xq, w_xkv, w_xo, g_mlp, w_mlp1, w_mlp2, g_final, loss_target, m_g_mix, m_w_in, m_b_fox_forget, m_w_gla_gate, m_b_gla_gate, m_g_gla_out, m_g_mla_q, m_w_mla_uq, m_g_mla_kv, m_w_mla_ukv, m_b_branch_gate, m_w_up_fox, m_w_up_gla, m_w_up_mla, m_w_out, m_g_xa, m_g_mem, m_w_xq, m_w_xkv, m_w_xo, m_g_mlp, m_w_mlp1, m_w_mlp2, m_g_final, v_g_mix, v_w_in, v_b_fox_forget, v_w_gla_gate, v_b_gla_gate, v_g_gla_out, v_g_mla_q, v_w_mla_uq, v_g_mla_kv, v_w_mla_ukv, v_b_branch_gate, v_w_up_fox, v_w_up_gla, v_w_up_mla, v_w_out, v_g_xa, v_g_mem, v_w_xq, v_w_xkv, v_w_xo, v_g_mlp, v_w_mlp1, v_w_mlp2, v_g_final):
    return _step(dict(locals()))
```

```python
import functools
import math
import typing

import jax
import jax.numpy as jnp
from jax import lax
from jax.experimental import pallas as pl
from jax.experimental.pallas import tpu as pltpu

F32 = jnp.float32
BF16 = jnp.bfloat16
MESH = pl.DeviceIdType.MESH

D_MODEL = 1024
DEPTH = 2
CHUNK = 64
EPS = 1e-6
FOX_HEADS, FOX_HD = 4, 64
GLA_HEADS, GLA_DK, GLA_DV, GLA_RANK, GLA_TAU = 4, 64, 128, 16, 16.0
MLA_HEADS, MLA_Q_RANK, MLA_KV_RANK, MLA_NOPE, MLA_ROPE, MLA_VD = 4, 256, 128, 64, 32, 64
ROPE_BASE = 10000.0
XA_HEADS, XA_HD = 4, 128
D_FF = 4 * D_MODEL
IN_SIZES = (256, 256, 256, 4, 256, 256, 512, 16, 512, 256, 128, 32, 3072)
N_IN = sum(IN_SIZES)

ADAM_LR, ADAM_B1, ADAM_B2, ADAM_EPS, ADAM_WD, ADAM_STEP = 0.001, 0.9, 0.999, 1e-08, 0.01, 10

N_CHIPS = 4
N_DEV = 8
LANES = 128
VMEM_LIMIT = 48 * 1024 * 1024
MASK_VALUE = -1e30

BIG = (('w_in', 2), ('w_gla_gate', 2), ('w_mla_uq', 2), ('w_mla_ukv', 2), ('w_up_fox', 2), ('w_up_gla', 2),
       ('w_up_mla', 2), ('w_out', 1), ('w_xq', 1), ('w_xkv', 1), ('w_xo', 2), ('w_mlp1', 2), ('w_mlp2', 1))
SMALL = ('g_mix', 'b_fox_forget', 'b_gla_gate', 'g_gla_out', 'g_mla_q', 'g_mla_kv', 'b_branch_gate',
         'g_xa', 'g_mem', 'g_mlp', 'g_final')
ORDER = ('g_mix', 'w_in', 'b_fox_forget', 'w_gla_gate', 'b_gla_gate', 'g_gla_out', 'g_mla_q', 'w_mla_uq',
         'g_mla_kv', 'w_mla_ukv', 'b_branch_gate', 'w_up_fox', 'w_up_gla', 'w_up_mla', 'w_out', 'g_xa', 'g_mem',
         'w_xq', 'w_xkv', 'w_xo', 'g_mlp', 'w_mlp1', 'w_mlp2', 'g_final')


def _params(*sem):
    return pltpu.CompilerParams(dimension_semantics=sem, vmem_limit_bytes=VMEM_LIMIT)


def _sig(x):
    return 1.0 / (1.0 + jnp.exp(-x))


def _logsig(x):
    return jnp.minimum(x, 0.0) - jnp.log(1.0 + jnp.exp(-jnp.abs(x)))


NN = (((1,), (0,)), ((), ()))
NT = (((1,), (1,)), ((), ()))
TN = (((0,), (0,)), ((), ()))


def _dot(a, b, dims=NN):
    return lax.dot_general(a, b, dims, preferred_element_type=F32)


class Cols(typing.NamedTuple):
    arr: jax.Array
    width: int
    blk: int


def _tri_dot(tri, x):
    hi = x.astype(BF16)
    r1 = x - hi.astype(F32)
    mid = r1.astype(BF16)
    lo = (r1 - mid.astype(F32)).astype(BF16)
    return _dot(tri, hi) + _dot(tri, mid) + _dot(tri, lo)


def _mm(a, b, *, mode, out_dtype, name, tm=512, tn=512, norm_g=None, emit_norm=False, a_fn=None, extras=(),
        epilogue=None):
    a_blk = 0
    if isinstance(a, Cols):
        a, width, a_blk = a
        a_shape = (a.shape[0], width)
    else:
        a_shape = a.shape
    if mode == 'tn':
        k, m = a_shape
    else:
        m, k = a_shape
    n = b.shape[0] if mode == 'nt' else b.shape[1]
    assert (b.shape[1] if mode == 'nt' else b.shape[0]) == k, (name, a.shape, b.shape)
    tm, tn = min(tm, m), min(tn, n)
    assert m % tm == 0 and n % tn == 0, (name, m, n, tm, tn)
    assert a_blk == 0 or (mode == 'nn') or (mode == 'tn' and tm == m)
    if mode == 'tn':
        a_spec = pl.BlockSpec((k, tm), lambda i, j: (0, i + a_blk))
    else:
        a_spec = pl.BlockSpec((tm, k), lambda i, j: (i, a_blk))
    b_spec = pl.BlockSpec((tn, k), lambda i, j: (j, 0)) if mode == 'nt' else pl.BlockSpec((k, tn), lambda i, j: (0, j))
    dims = {'nn': NN, 'nt': NT, 'tn': TN}[mode]
    has_norm = norm_g is not None
    assert not (has_norm and mode != 'nn')
    n_ex = len(extras)

    def body(*refs):
        a_ref, b_ref = refs[0], refs[1]
        pos = 2
        g_ref = None
        if has_norm:
            g_ref = refs[pos]
            pos += 1
        ex_refs = refs[pos:pos + n_ex]
        pos += n_ex
        o_ref = refs[pos]
        pos += 1
        h_ref = None
        if emit_norm:
            h_ref = refs[pos]
            pos += 1
        if has_norm:
            an_ref = refs[pos]

            @pl.when(pl.program_id(1) == 0)
            def _():
                xf = a_ref[...].astype(F32)
                y = xf * lax.rsqrt(jnp.mean(xf * xf, axis=-1, keepdims=True) + EPS) * g_ref[...]
                an_ref[...] = y.astype(BF16)
                if emit_norm:
                    h_ref[...] = y.astype(BF16)

            av = an_ref[...]
        else:
            av = a_ref[...]
            if a_fn is not None:
                av = a_fn(av)
            av = av.astype(BF16)
        acc = _dot(av, b_ref[...].astype(BF16), dims)
        if epilogue is not None:
            acc = epilogue(acc, *[r[...] for r in ex_refs])
        o_ref[...] = acc.astype(out_dtype)

    in_specs = [a_spec, b_spec]
    args = [a, b]
    if has_norm:
        in_specs.append(pl.BlockSpec((1, k), lambda i, j: (0, 0)))
        args.append(norm_g)
    for arr, blk, imap in extras:
        in_specs.append(pl.BlockSpec(blk, imap))
        args.append(arr)
    out_shape = [jax.ShapeDtypeStruct((m, n), out_dtype)]
    out_specs = [pl.BlockSpec((tm, tn), lambda i, j: (i, j))]
    if emit_norm:
        out_shape.append(jax.ShapeDtypeStruct((m, k), BF16))
        out_specs.append(pl.BlockSpec((tm, k), lambda i, j: (i, 0)))
    scratch = [pltpu.VMEM((tm, k), BF16)] if has_norm else []
    res = pl.pallas_call(
        body, name=name, grid=(m // tm, n // tn), in_specs=in_specs, out_specs=out_specs, out_shape=out_shape,
        scratch_shapes=scratch, compiler_params=_params('arbitrary', 'arbitrary'))(*args)
    return res if emit_norm else res[0]


def _mn(tm=512, tn=512, col_off=0):
    return (tm, tn), (lambda i, j: (i, j + col_off))


def _nvec(tn=512, col_off=0):
    return (1, tn), (lambda i, j: (0, j + col_off))


def _rowwise(fn, rows, consts, outs, sums=(), *, name, ts=256):
    views = [x if isinstance(x, Cols) else Cols(x, x.shape[1], 0) for x in rows]
    rows = [v.arr for v in views]
    r = rows[0].shape[0]
    ts = min(ts, r)
    assert r % ts == 0, (name, r, ts)
    nr, nc, no, ns = len(rows), len(consts), len(outs), len(sums)

    def body(*refs):
        vals = fn(*[x[...] for x in refs[:nr + nc]])
        for q in range(no):
            refs[nr + nc + q][...] = vals[q].astype(outs[q][1])
        if ns:
            @pl.when(pl.program_id(0) == 0)
            def _():
                for q in range(ns):
                    refs[nr + nc + no + q][...] = jnp.zeros((1, sums[q]), F32)

            for q in range(ns):
                refs[nr + nc + no + q][...] += jnp.sum(vals[no + q].astype(F32), axis=0, keepdims=True)

    in_specs = [pl.BlockSpec((ts, v.width), lambda i, blk=v.blk: (i, blk)) for v in views]
    in_specs += [pl.BlockSpec(x.shape, lambda i, nd=x.ndim: (0,) * nd) for x in consts]
    out_specs = [pl.BlockSpec((ts, w), lambda i: (i, 0)) for w, _ in outs]
    out_specs += [pl.BlockSpec((1, w), lambda i: (0, 0)) for w in sums]
    out_shape = [jax.ShapeDtypeStruct((r, w), dt) for w, dt in outs]
    out_shape += [jax.ShapeDtypeStruct((1, w), F32) for w in sums]
    return pl.pallas_call(body, name=name, grid=(r // ts,), in_specs=in_specs, out_specs=out_specs,
                          out_shape=out_shape, compiler_params=_params('arbitrary'))(*rows, *consts)


def _cumsum_rows(x, *, reverse, name, bs=256):
    s, w = x.shape
    bs = min(bs, s)
    nb = s // bs

    def body(x_ref, o_ref, carry):
        @pl.when(pl.program_id(0) == 0)
        def _():
            carry[...] = jnp.zeros_like(carry)

        r = lax.broadcasted_iota(jnp.int32, (bs, bs), 0)
        c = lax.broadcasted_iota(jnp.int32, (bs, bs), 1)
        tri = jnp.where((c >= r) if reverse else (c <= r), 1.0, 0.0).astype(BF16)
        xv = x_ref[...]
        o_ref[...] = _tri_dot(tri, xv) + carry[...]
        carry[...] += jnp.sum(xv, axis=0, keepdims=True)

    imap = (lambda i: (nb - 1 - i, 0)) if reverse else (lambda i: (i, 0))
    return pl.pallas_call(body, name=name, grid=(nb,), in_specs=[pl.BlockSpec((bs, w), imap)],
                          out_specs=pl.BlockSpec((bs, w), imap), out_shape=jax.ShapeDtypeStruct((s, w), F32),
                          scratch_shapes=[pltpu.VMEM((1, w), F32)], compiler_params=_params('arbitrary'))(x)


def _mask(mode, i, j, bq, bk):
    qpos = i * bq + lax.broadcasted_iota(jnp.int32, (bq, bk), 0)
    kpos = j * bk + lax.broadcasted_iota(jnp.int32, (bq, bk), 1)
    if mode == 'causal':
        return kpos <= qpos
    return kpos < (jnp.right_shift(qpos, int(math.log2(CHUNK))) + 1) * CHUNK


ROPE_SHIFT = int(math.log2(MLA_ROPE))


def _lane_masks(g, b, rope):
    lane = lax.broadcasted_iota(jnp.int32, (1, LANES), 1)
    heads = [None if g == 1 else (lane >= hh * (LANES // g)) & (lane < (hh + 1) * (LANES // g)) for hh in range(g)]
    ropes = [jnp.right_shift(lane, ROPE_SHIFT) == b * g + hh for hh in range(g)] if rope else [None] * g
    return heads, ropes


def _sel(mask, x):
    return x if mask is None else jnp.where(mask, x, jnp.zeros_like(x))


def _mattn_fwd(q, k, v, *, qc, kc, vc, nb, g, scale, mode, name, ck=None, qr=None, qrc=0, kr=None, blk=512):
    s, t = q.shape[0], k.shape[0]
    bq, bk = min(blk, s), min(blk, t)
    nq, nk = s // bq, t // bk
    tri = mode != 'full'
    bias, rope = ck is not None, qr is not None
    assert not tri or (bq == bk and bq % CHUNK == 0)

    def body(*refs):
        refs = list(refs)
        q_ref, k_ref, v_ref = refs[:3]
        pos = 3
        ck_ref = qr_ref = kr_ref = None
        if bias:
            ck_ref = refs[pos]
            pos += 1
        if rope:
            qr_ref, kr_ref = refs[pos:pos + 2]
            pos += 2
        o_ref, lse_ref, m_s, l_s, acc_s = refs[pos:]
        b, i, j = pl.program_id(0), pl.program_id(1), pl.program_id(2)
        heads, ropes = _lane_masks(g, b, rope)

        @pl.when(j == 0)
        def _():
            m_s[...] = jnp.full_like(m_s, MASK_VALUE)
            l_s[...] = jnp.zeros_like(l_s)
            acc_s[...] = jnp.zeros_like(acc_s)

        def compute(masked):
            q2, k2, v2 = q_ref[...], k_ref[...], v_ref[...]
            for hh in range(g):
                sc = _dot(_sel(heads[hh], q2), k2, NT)
                if rope:
                    sc = sc + _dot(_sel(ropes[hh], qr_ref[...]), kr_ref[...], NT)
                sc = sc * scale
                if bias:
                    sc = sc - ck_ref[0, hh:hh + 1, :]
                if masked:
                    sc = jnp.where(_mask(mode, i, j, bq, bk), sc, MASK_VALUE)
                m_prev = m_s[hh]
                m_new = jnp.maximum(m_prev, jnp.max(sc, axis=1, keepdims=True))
                alpha = jnp.exp(m_prev - m_new)
                p = jnp.exp(sc - m_new)
                l_s[hh] = alpha * l_s[hh] + jnp.sum(p, axis=1, keepdims=True)
                m_s[hh] = m_new
                pv = _dot(p.astype(BF16), _sel(heads[hh], v2))
                acc = acc_s[...]
                acc_s[...] = (acc * alpha if g == 1 else jnp.where(heads[hh], acc * alpha, acc)) + pv

        if tri:
            pl.when(j < i)(functools.partial(compute, False))
            pl.when(j == i)(functools.partial(compute, True))
        else:
            compute(False)

        @pl.when(j == nk - 1)
        def _():
            lane = lax.broadcasted_iota(jnp.int32, (bq, LANES), 1)
            l_full, lse = l_s[0], jnp.zeros((bq, LANES), F32)
            for hh in range(g):
                if hh:
                    l_full = jnp.where(heads[hh], l_s[hh], l_full)
                lse = jnp.where(lane == hh, m_s[hh] + jnp.log(l_s[hh]), lse)
            o_ref[...] = (acc_s[...] / l_full).astype(o_ref.dtype)
            lse_ref[...] = lse

    jj = (lambda i, j: jnp.minimum(i, j)) if tri else (lambda i, j: j)
    in_specs = [pl.BlockSpec((bq, LANES), lambda b, i, j: (i, qc + b)),
                pl.BlockSpec((bk, LANES), lambda b, i, j: (jj(i, j), kc + b)),
                pl.BlockSpec((bk, LANES), lambda b, i, j: (jj(i, j), vc + b))]
    args = [q, k, v]
    if bias:
        in_specs.append(pl.BlockSpec((1, 8, bk), lambda b, i, j: (b, 0, jj(i, j))))
        args.append(ck)
    if rope:
        in_specs += [pl.BlockSpec((bq, LANES), lambda b, i, j: (i, qrc)),
                     pl.BlockSpec((bk, LANES), lambda b, i, j: (jj(i, j), 0))]
        args += [qr, kr]
    out = pl.BlockSpec((bq, LANES), lambda b, i, j: (i, b))
    return pl.pallas_call(
        body, name=name, grid=(nb, nq, nk), in_specs=in_specs, out_specs=[out, out],
        out_shape=[jax.ShapeDtypeStruct((s, LANES * nb), BF16), jax.ShapeDtypeStruct((s, LANES * nb), F32)],
        scratch_shapes=[pltpu.VMEM((g, bq, 1), F32), pltpu.VMEM((g, bq, 1), F32), pltpu.VMEM((bq, LANES), F32)],
        compiler_params=_params('arbitrary', 'arbitrary', 'arbitrary'))(*args)


def _mattn_bwd(q, k, v, o, do, lse, *, qc, kc, vc, nb, g, scale, mode, name, ck=None, qr=None, qrc=0, kr=None,
               blk=512):
    s, t = q.shape[0], k.shape[0]
    bq, bk = min(blk, s), min(blk, t)
    nq, nk = s // bq, t // bk
    tri = mode != 'full'
    bias, rope = ck is not None, qr is not None

    def body(*refs):
        refs = list(refs)
        q_ref, k_ref, v_ref, o_ref, do_ref, lse_ref = refs[:6]
        pos = 6
        ck_ref = qr_ref = kr_ref = dck_ref = dcq_ref = dqr_ref = dkr_ref = dck_s = None
        if bias:
            ck_ref = refs[pos]
            pos += 1
        if rope:
            qr_ref, kr_ref = refs[pos:pos + 2]
            pos += 2
        dq_ref, dk_ref, dv_ref = refs[pos:pos + 3]
        pos += 3
        if bias:
            dck_ref, dcq_ref = refs[pos:pos + 2]
            pos += 2
        if rope:
            dqr_ref, dkr_ref = refs[pos:pos + 2]
            pos += 2
        dk_s, dv_s = refs[pos:pos + 2]
        if bias:
            dck_s = refs[pos + 2]
        b, j, i = pl.program_id(0), pl.program_id(1), pl.program_id(2)
        heads, ropes = _lane_masks(g, b, rope)

        @pl.when((j == 0) & (i == 0))
        def _():
            dq_ref[...] = jnp.zeros_like(dq_ref)
            if bias:
                dcq_ref[...] = jnp.zeros_like(dcq_ref)

        if rope:
            @pl.when((b == 0) & (j == 0) & (i == 0))
            def _():
                dqr_ref[...] = jnp.zeros_like(dqr_ref)
                dkr_ref[...] = jnp.zeros_like(dkr_ref)

        @pl.when(i == 0)
        def _():
            dk_s[...] = jnp.zeros_like(dk_s)
            dv_s[...] = jnp.zeros_like(dv_s)
            if bias:
                dck_s[...] = jnp.zeros_like(dck_s)

        def compute(masked):
            q2, k2, v2, do2 = q_ref[...], k_ref[...], v_ref[...], do_ref[...]
            dd = do2.astype(F32) * o_ref[...].astype(F32)
            lse2 = lse_ref[...]
            lane = lax.broadcasted_iota(jnp.int32, (bq, LANES), 1)
            rq = pl.ds(pl.multiple_of(i * bq, bq), bq)
            rk = pl.ds(pl.multiple_of(j * bk, bk), bk)
            for hh in range(g):
                qm = _sel(heads[hh], q2)
                sc = _dot(qm, k2, NT)
                if rope:
                    qrm = _sel(ropes[hh], qr_ref[...])
                    sc = sc + _dot(qrm, kr_ref[...], NT)
                sc = sc * scale
                if bias:
                    sc = sc - ck_ref[0, hh:hh + 1, :]
                if masked:
                    sc = jnp.where(_mask(mode, i, j, bq, bk), sc, MASK_VALUE)
                p = jnp.exp(sc - jnp.sum(jnp.where(lane == hh, lse2, 0.0), axis=1, keepdims=True))
                dom = _sel(heads[hh], do2)
                dp = _dot(dom, v2, NT)
                delta = jnp.sum(_sel(heads[hh], dd), axis=1, keepdims=True)
                ds = p * (dp - delta)
                dsb = ds.astype(BF16)
                dv_s[...] += _dot(p.astype(BF16), dom, TN)
                dk_s[...] += scale * _dot(dsb, qm, TN)
                dq_ref[rq, :] += scale * _dot(dsb, _sel(heads[hh], k2))
                if rope:
                    dqr_ref[rq, :] += scale * _dot(dsb, _sel(ropes[hh], kr_ref[...]))
                    dkr_ref[rk, :] += scale * _dot(dsb, qrm, TN)
                if bias:
                    dck_s[hh:hh + 1, :] -= jnp.sum(ds, axis=0, keepdims=True)
                    dcq_ref[rq, :] += jnp.where(lane == hh, jnp.sum(ds, axis=1, keepdims=True), 0.0)

        if tri:
            pl.when(i > j)(functools.partial(compute, False))
            pl.when(i == j)(functools.partial(compute, True))
        else:
            compute(False)

        @pl.when(i == nq - 1)
        def _():
            dk_ref[...] = dk_s[...]
            dv_ref[...] = dv_s[...]
            if bias:
                dck_ref[0] = dck_s[...]

    ii = (lambda j, i: jnp.maximum(i, j)) if tri else (lambda j, i: i)
    qrow = lambda col: pl.BlockSpec((bq, LANES), lambda b, j, i: (ii(j, i), col(b)))
    krow = lambda col: pl.BlockSpec((bk, LANES), lambda b, j, i: (j, col(b)))
    in_specs = [qrow(lambda b: qc + b), krow(lambda b: kc + b), krow(lambda b: vc + b), qrow(lambda b: b),
                qrow(lambda b: b), qrow(lambda b: b)]
    args = [q, k, v, o, do, lse]
    whole = lambda rows: pl.BlockSpec((rows, LANES), lambda b, j, i: (0, b))
    out_specs = [whole(s), krow(lambda b: b), krow(lambda b: b)]
    out_shape = [jax.ShapeDtypeStruct((s, LANES * nb), F32), jax.ShapeDtypeStruct((t, LANES * nb), F32),
                 jax.ShapeDtypeStruct((t, LANES * nb), F32)]
    scratch = [pltpu.VMEM((bk, LANES), F32), pltpu.VMEM((bk, LANES), F32)]
    if bias:
        in_specs.append(pl.BlockSpec((1, 8, bk), lambda b, j, i: (b, 0, j)))
        args.append(ck)
        out_specs += [pl.BlockSpec((1, 8, bk), lambda b, j, i: (b, 0, j)), whole(s)]
        out_shape += [jax.ShapeDtypeStruct((nb, 8, t), F32), jax.ShapeDtypeStruct((s, LANES * nb), F32)]
    if rope:
        in_specs += [qrow(lambda b: qrc), krow(lambda b: 0)]
        args += [qr, kr]
        out_specs += [pl.BlockSpec((s, LANES), lambda b, j, i: (0, 0)), pl.BlockSpec((t, LANES), lambda b, j, i: (0, 0))]
        out_shape += [jax.ShapeDtypeStruct((s, LANES), F32), jax.ShapeDtypeStruct((t, LANES), F32)]
    if bias:
        scratch.append(pltpu.VMEM((8, bk), F32))
    return pl.pallas_call(body, name=name, grid=(nb, nk, nq), in_specs=in_specs, out_specs=out_specs,
                          out_shape=out_shape, scratch_shapes=scratch,
                          compiler_params=_params('arbitrary', 'arbitrary', 'arbitrary'))(*args)


def _gla_chunk(la_c, k_c):
    r = lax.broadcasted_iota(jnp.int32, (CHUNK, CHUNK), 0)
    c = lax.broadcasted_iota(jnp.int32, (CHUNK, CHUNK), 1)
    tri = jnp.where(c <= r, 1.0, 0.0).astype(BF16)
    cum = _tri_dot(tri, la_c)
    end = jnp.sum(la_c, axis=0, keepdims=True)
    dec = jnp.exp(end - cum)
    return dec, k_c * dec, jnp.exp(end)


GLA_PAIRS = GLA_HEADS // 2


def _gla_fwd(z, la, *, qc, kc, vc, name, blk=512):
    s = z.shape[0]
    bs = min(blk, s)
    ncb = bs // CHUNK
    nblk = s // bs

    def body(q_ref, k_ref, va_ref, vb_ref, la_ref, o_ref, st_ref, st):
        @pl.when(pl.program_id(1) == 0)
        def _():
            st[...] = jnp.zeros_like(st)

        heads, _ = _lane_masks(2, 0, False)
        v_refs = (va_ref, vb_ref)
        for c in range(ncb):
            sl = pl.ds(c * CHUNK, CHUNK)
            _, kf, a = _gla_chunk(la_ref[sl, :], k_ref[sl, :])
            qs = q_ref[sl, :] * (GLA_DK ** -0.5)
            for hh in range(2):
                ut = _dot(v_refs[hh][sl, :].astype(BF16), _sel(heads[hh], kf).astype(BF16), TN)
                new = a * st[hh] + ut
                st[hh] = new
                st_ref[0, c, hh] = new
                o_ref[sl, hh * GLA_DV:(hh + 1) * GLA_DV] = _dot(_sel(heads[hh], qs).astype(BF16), new.astype(BF16), NT)

    col = lambda c0, m=1: pl.BlockSpec((bs, LANES), lambda b, i: (i, c0 + m * b))
    return pl.pallas_call(
        body, name=name, grid=(GLA_PAIRS, nblk),
        in_specs=[col(qc), col(kc), col(vc, 2), col(vc + 1, 2), col(0)],
        out_specs=[pl.BlockSpec((bs, 2 * GLA_DV), lambda b, i: (i, b)),
                   pl.BlockSpec((1, ncb, 2, GLA_DV, LANES), lambda b, i: (b, i, 0, 0, 0))],
        out_shape=[jax.ShapeDtypeStruct((s, GLA_HEADS * GLA_DV), F32),
                   jax.ShapeDtypeStruct((GLA_PAIRS, s // CHUNK, 2, GLA_DV, LANES), F32)],
        scratch_shapes=[pltpu.VMEM((2, GLA_DV, LANES), F32)],
        compiler_params=_params('arbitrary', 'arbitrary'))(z, z, z, z, la)


def _gla_bwd(z, la, st_all, st_prev, do, *, qc, kc, vc, name, blk=512):
    s = z.shape[0]
    bs = min(blk, s)
    ncb = bs // CHUNK
    nblk = s // bs

    def body(q_ref, k_ref, va_ref, vb_ref, la_ref, st_ref, sp_ref, do_ref, dq_ref, dk_ref, dv_ref, dla_ref, ga):
        @pl.when(pl.program_id(1) == 0)
        def _():
            ga[...] = jnp.zeros_like(ga)

        r = lax.broadcasted_iota(jnp.int32, (CHUNK, CHUNK), 0)
        cc = lax.broadcasted_iota(jnp.int32, (CHUNK, CHUNK), 1)
        tri_rev = jnp.where(cc >= r, 1.0, 0.0).astype(BF16)
        heads, _ = _lane_masks(2, 0, False)
        v_refs = (va_ref, vb_ref)
        for c in reversed(range(ncb)):
            sl = pl.ds(c * CHUNK, CHUNK)
            dec, kf, a = _gla_chunk(la_ref[sl, :], k_ref[sl, :])
            qs = q_ref[sl, :] * (GLA_DK ** -0.5)
            dq2 = jnp.zeros((CHUNK, LANES), F32)
            dkd = jnp.zeros((CHUNK, LANES), F32)
            da = jnp.zeros((1, LANES), F32)
            for hh in range(2):
                hv = slice(hh * GLA_DV, (hh + 1) * GLA_DV)
                dob = do_ref[sl, hv].astype(BF16)
                g = _dot(dob, _sel(heads[hh], qs).astype(BF16), TN) + ga[hh]
                gb = g.astype(BF16)
                dq2 = dq2 + _dot(dob, st_ref[0, c, hh].astype(BF16))
                dv_ref[sl, hv] = _dot(_sel(heads[hh], kf).astype(BF16), gb, NT)
                dkd = dkd + _dot(v_refs[hh][sl, :].astype(BF16), gb)
                da = da + jnp.sum(g * sp_ref[0, c, hh], axis=0, keepdims=True)
                ga[hh] = a * g
            dq_ref[sl, :] = (GLA_DK ** -0.5) * dq2
            dk_ref[sl, :] = dkd * dec
            e = dkd * kf
            dend = jnp.sum(e, axis=0, keepdims=True) + da * a
            dla_ref[sl, :] = dend - _tri_dot(tri_rev, e)

    rev = lambda i: nblk - 1 - i
    col = lambda c0, m=1: pl.BlockSpec((bs, LANES), lambda b, i: (rev(i), c0 + m * b))
    wide = pl.BlockSpec((bs, 2 * GLA_DV), lambda b, i: (rev(i), b))
    stspec = pl.BlockSpec((1, ncb, 2, GLA_DV, LANES), lambda b, i: (b, rev(i), 0, 0, 0))
    return pl.pallas_call(
        body, name=name, grid=(GLA_PAIRS, nblk),
        in_specs=[col(qc), col(kc), col(vc, 2), col(vc + 1, 2), col(0), stspec, stspec, wide],
        out_specs=[col(0), col(0), wide, col(0)],
        out_shape=[jax.ShapeDtypeStruct((s, GLA_HEADS * GLA_DK), F32), jax.ShapeDtypeStruct((s, GLA_HEADS * GLA_DK), F32),
                   jax.ShapeDtypeStruct((s, GLA_HEADS * GLA_DV), F32), jax.ShapeDtypeStruct((s, GLA_HEADS * GLA_DK), F32)],
        scratch_shapes=[pltpu.VMEM((2, GLA_DV, LANES), F32)],
        compiler_params=_params('arbitrary', 'arbitrary'))(z, z, z, z, la, st_all, st_prev, do)


def _place():
    return lax.axis_index('x'), lax.axis_index('y'), lax.axis_index('c')


ANY = pl.BlockSpec(memory_space=pl.ANY)


def _all_gather8(blk, *, name):
    m, n = blk.shape

    def body(x_ref, out_ref, send_sems, recv_sems, local_sem):
        x, y, c = _place()
        me, sibling = (x, y, c), (x, y, 1 - c)
        chips = [(1 - x, y), (x, 1 - y), (1 - x, 1 - y)]

        def slot(px, py, pc):
            return out_ref.at[4 * px + 2 * py + pc]

        def copy(q, block, to, src=None):
            return pltpu.make_async_remote_copy(
                src_ref=slot(*block) if src is None else src, dst_ref=slot(*block), send_sem=send_sems.at[q],
                recv_sem=recv_sems.at[q], device_id=to, device_id_type=MESH)

        mine = pltpu.make_async_copy(x_ref, slot(*me), local_sem)
        mine.start()
        first = [copy(0, me, sibling, src=x_ref)]
        first += [copy(1 + q, me, (*chip, c), src=x_ref) for q, chip in enumerate(chips)]
        for cp in first:
            cp.start()
        passed = [copy(4 + q, (*chip, c), sibling) for q, chip in enumerate(chips)]
        for q, chip in enumerate(chips):
            copy(1 + q, (*chip, c), me).wait_recv()
            passed[q].start()
        copy(0, sibling, me).wait_recv()
        for q, chip in enumerate(chips):
            copy(4 + q, (*chip, 1 - c), me).wait_recv()
        for cp in first + passed:
            cp.wait_send()
        mine.wait()

    return pl.pallas_call(
        body, name=name, in_specs=[ANY], out_specs=ANY, out_shape=jax.ShapeDtypeStruct((N_DEV, m, n), blk.dtype),
        scratch_shapes=[pltpu.SemaphoreType.DMA((7,)), pltpu.SemaphoreType.DMA((7,)), pltpu.SemaphoreType.DMA(())],
    )(blk)


def _sems(*counts):
    return [pltpu.SemaphoreType.DMA((n,)) for n in counts]


def _gather_shards(ws, *, name):
    n = len(ws)

    def body(*refs):
        ins, outs = refs[:n], refs[n:2 * n]
        send_sems, recv_sems = refs[2 * n:]
        x, y, c = _place()
        me_chip = 2 * x + y
        sibling = (x, y, 1 - c)
        chips = [(1 - x, y), (x, 1 - y), (1 - x, 1 - y)]

        def copy(q, k, src, dst, to):
            return pltpu.make_async_remote_copy(src_ref=src, dst_ref=dst, send_sem=send_sems.at[7 * q + k],
                                                recv_sem=recv_sems.at[7 * q + k], device_id=to, device_id_type=MESH)

        sent = []
        for q in range(n):
            for k, (px, py) in enumerate(chips):
                sent.append(copy(q, k, ins[q].at[c], outs[q].at[me_chip, c], (px, py, c)))
                sent[-1].start()
            sent.append(copy(q, 6, ins[q], outs[q].at[me_chip], sibling))
            sent[-1].start()
        for q in range(n):
            for k, (px, py) in enumerate(chips):
                slot = outs[q].at[2 * px + py, c]
                copy(q, k, slot, slot, (px, py, c)).wait_recv()
                sent.append(copy(q, 3 + k, slot, slot, sibling))
                sent[-1].start()
        for q in range(n):
            for k, (px, py) in enumerate(chips):
                slot = outs[q].at[2 * px + py, 1 - c]
                copy(q, 3 + k, slot, slot, sibling).wait_recv()
            copy(q, 6, ins[q], outs[q].at[me_chip], sibling).wait_recv()
        for cp in sent:
            cp.wait_send()

    return pl.pallas_call(
        body, name=name, in_specs=[ANY] * n, out_specs=[ANY] * n,
        out_shape=[jax.ShapeDtypeStruct((N_CHIPS,) + w.shape, w.dtype) for w in ws],
        scratch_shapes=_sems(7 * n, 7 * n))(*ws)


def _to_sibling(gs, *, name):
    n = len(gs)

    def body(*refs):
        ins, outs = refs[:n], refs[n:2 * n]
        send_sems, recv_sems = refs[2 * n:]
        x, y, c = _place()
        cps = [pltpu.make_async_remote_copy(
            src_ref=ins[q], dst_ref=outs[q], send_sem=send_sems.at[q], recv_sem=recv_sems.at[q],
            device_id=(x, y, 1 - c), device_id_type=MESH) for q in range(n)]
        for cp in cps:
            cp.start()
        for cp in cps:
            cp.wait()

    return pl.pallas_call(body, name=name, in_specs=[ANY] * n, out_specs=[ANY] * n,
                          out_shape=[jax.ShapeDtypeStruct(g.shape, g.dtype) for g in gs],
                          scratch_shapes=_sems(n, n))(*gs)


def _chip_exchange(ps, *, name):
    n = len(ps)

    def body(*refs):
        ins, outs = refs[:n], refs[n:2 * n]
        send_sems, recv_sems = refs[2 * n:]
        x, y, c = _place()
        chips = [(1 - x, y), (x, 1 - y), (1 - x, 1 - y)]
        cps = []
        for q in range(n):
            for k, (px, py) in enumerate(chips):
                cps.append(pltpu.make_async_remote_copy(
                    src_ref=ins[q].at[2 * px + py], dst_ref=outs[q].at[k], send_sem=send_sems.at[3 * q + k],
                    recv_sem=recv_sems.at[3 * q + k], device_id=(px, py, c), device_id_type=MESH))
                cps[-1].start()
        for cp in cps:
            cp.wait()

    return pl.pallas_call(body, name=name, in_specs=[ANY] * n, out_specs=[ANY] * n,
                          out_shape=[jax.ShapeDtypeStruct((3,) + p.shape[1:], p.dtype) for p in ps],
                          scratch_shapes=_sems(3 * n, 3 * n))(*ps)


def _sum_chips(own, r, *, name, ts=256):
    k, n = own.shape
    ts = min(ts, k)

    def body(own_ref, r_ref, o_ref):
        f = lambda q: r_ref[q].astype(F32)
        o_ref[...] = ((own_ref[...].astype(F32) + f(0)) + f(1)) + f(2)

    return pl.pallas_call(
        body, name=name, grid=(k // ts,),
        in_specs=[pl.BlockSpec((ts, n), lambda i: (i, 0)), pl.BlockSpec((3, ts, n), lambda i: (0, i, 0))],
        out_specs=pl.BlockSpec((ts, n), lambda i: (i, 0)), out_shape=jax.ShapeDtypeStruct((k, n), F32),
        compiler_params=_params('arbitrary'))(own, r)


WIN_SHARD = N_IN // N_CHIPS
WIN_PAD = -(-WIN_SHARD // LANES) * LANES


def _full_layer(gathered, axis, l):
    sh = gathered[:, l]
    _, k, n = sh.shape
    if axis == 2:
        return sh.transpose(1, 0, 2).reshape(k, N_CHIPS * n)
    return sh.reshape(N_CHIPS * k, n)


def _win_cols(wp, o, n):
    parts = []
    while n > 0:
        j, r = divmod(o, WIN_SHARD)
        take = min(n, WIN_SHARD - r)
        parts.append(wp[:, j * WIN_PAD + r:j * WIN_PAD + r + take])
        o, n = o + take, n - take
    return parts[0] if len(parts) == 1 else jnp.concatenate(parts, axis=1)


def _split_full(full, axis):
    l, k, n = full.shape
    if axis == 2:
        return jnp.stack([full[:, :, j * (n // N_CHIPS):(j + 1) * (n // N_CHIPS)] for j in range(N_CHIPS)])
    return full.reshape(l, N_CHIPS, k // N_CHIPS, n).transpose(1, 0, 2, 3)


def _padc(a, w):
    return jnp.pad(a, ((0, 0), (0, w - a.shape[1])))


def _swap16(a):
    return jnp.concatenate([a[..., 16:32], a[..., 0:16]], axis=-1)


B_GR, B_GQ, B_GK, B_GV, B_MQ, B_MKR, B_MKRS, B_FF, B_GLOW, B_MKV, B_END = (
    0, 512, 768, 1024, 1536, 1792, 1920, 2048, 2176, 2304, 2432)
B_W = 2560
O_FQ, O_FF, O_GQ, O_GLOW, O_GR, O_MQ, O_MKV, O_MKR, O_ZG = 0, 768, 772, 1796, 1812, 2324, 2580, 2708, 2740


def _repack_layer_weights(w):
    wi = functools.partial(_win_cols, w['w_in'])
    out = dict(w)
    out['in_a'] = wi(O_FQ, 768)
    kr = wi(O_MKR, 32)
    out['in_b'] = jnp.concatenate([
        wi(O_GR, 512), wi(O_GQ, 1024), wi(O_MQ, 256), jnp.tile(kr, (1, MLA_HEADS)), jnp.tile(_swap16(kr), (1, MLA_HEADS)),
        _padc(wi(O_FF, 4), 128), _padc(wi(O_GLOW, 16), 128), wi(O_MKV, 128),
        jnp.zeros((D_MODEL, B_W - B_END), kr.dtype)], axis=1)
    out['in_c'] = wi(O_ZG, 3072)
    uq = w['w_mla_uq'].reshape(MLA_Q_RANK, MLA_HEADS, MLA_NOPE + MLA_ROPE)
    rope = uq[:, :, MLA_NOPE:]
    out['uq'] = jnp.concatenate([uq[:, :, :MLA_NOPE].reshape(MLA_Q_RANK, -1), rope.reshape(MLA_Q_RANK, -1),
                                 _swap16(rope).reshape(MLA_Q_RANK, -1)], axis=1)
    ukv = w['w_mla_ukv'].reshape(MLA_KV_RANK, MLA_HEADS, MLA_NOPE + MLA_VD)
    out['ukv'] = jnp.concatenate([ukv[:, :, :MLA_NOPE].reshape(MLA_KV_RANK, -1),
                                  ukv[:, :, MLA_NOPE:].reshape(MLA_KV_RANK, -1)], axis=1)
    out['gate'] = jnp.pad(w['w_gla_gate'], ((0, 128 - GLA_RANK), (0, 0)))
    return out


def _unpack_layer_grads(g):
    a, b, c = g['in_a'], g['in_b'], g['in_c']
    fold = lambda o: sum(b[:, o + MLA_ROPE * q:o + MLA_ROPE * (q + 1)] for q in range(MLA_HEADS))
    kr = fold(B_MKR) + _swap16(fold(B_MKRS))
    w_in = jnp.concatenate([a, b[:, B_FF:B_FF + 4], b[:, B_GQ:B_GQ + 1024], b[:, B_GLOW:B_GLOW + 16],
                            b[:, B_GR:B_GR + 512], b[:, B_MQ:B_MQ + 256], b[:, B_MKV:B_MKV + 128], kr, c], axis=1)
    uq = g['uq']
    nope = uq[:, :256].reshape(MLA_Q_RANK, MLA_HEADS, MLA_NOPE)
    rope = (uq[:, 256:384].reshape(MLA_Q_RANK, MLA_HEADS, MLA_ROPE)
            + _swap16(uq[:, 384:512].reshape(MLA_Q_RANK, MLA_HEADS, MLA_ROPE)))
    w_uq = jnp.concatenate([nope, rope], axis=2).reshape(MLA_Q_RANK, -1)
    ukv = g['ukv']
    w_ukv = jnp.concatenate([ukv[:, :256].reshape(MLA_KV_RANK, MLA_HEADS, MLA_NOPE),
                             ukv[:, 256:].reshape(MLA_KV_RANK, MLA_HEADS, MLA_VD)], axis=2).reshape(MLA_KV_RANK, -1)
    out = {'w_in': w_in, 'w_mla_uq': w_uq, 'w_mla_ukv': w_ukv, 'w_gla_gate': g['gate'][:GLA_RANK]}
    for nm in ('w_up_fox', 'w_up_gla', 'w_up_mla', 'w_out', 'w_xq', 'w_xkv', 'w_xo', 'w_mlp1', 'w_mlp2'):
        out[nm] = g[nm]
    return out


def _rope_tables(s):
    half = MLA_ROPE // 2
    inv = ROPE_BASE ** (-jnp.arange(half, dtype=F32) / half)
    ang = jnp.arange(s).astype(F32)[:, None] * inv[None, :]
    cos, sin = jnp.cos(ang), jnp.sin(ang)
    c1 = jnp.concatenate([cos, cos], axis=1)
    s1 = jnp.concatenate([-sin, sin], axis=1)
    return jnp.tile(c1, (1, MLA_HEADS)), jnp.tile(s1, (1, MLA_HEADS))


def _rms_bwd(x, dh, g):
    r = lax.rsqrt(jnp.mean(x * x, axis=-1, keepdims=True) + EPS)
    xh = x * r
    gd = dh * g
    return r * (gd - xh * jnp.mean(gd * xh, axis=-1, keepdims=True)), dh * xh


def _norm_bwd_call(x, dh, g, dres, name):
    w = x.width if isinstance(x, Cols) else x.shape[1]

    def with_res(xv, dv, rv, gv):
        dx, dg = _rms_bwd(xv, dv.astype(F32), gv)
        return rv + dx, dg

    def plain(xv, dv, gv):
        return _rms_bwd(xv, dv.astype(F32), gv)

    if dres is None:
        return _rowwise(plain, [x, dh], [g], [(w, F32)], [w], name=name)
    return _rowwise(with_res, [x, dh, dres], [g], [(w, F32)], [w], name=name)


def _gla_out_fwd(oraw, gr, g_out):
    outs = []
    for hh in range(GLA_HEADS):
        sl = slice(hh * GLA_DV, (hh + 1) * GLA_DV)
        oh = oraw[:, sl]
        n = oh * lax.rsqrt(jnp.mean(oh * oh, axis=-1, keepdims=True) + EPS) * g_out
        r = gr[:, sl]
        outs.append(n * (r * _sig(r)))
    return (jnp.concatenate(outs, axis=1),)


def _gla_out_bwd(oraw, gr, dout, g_out):
    d_o, d_r, dg = [], [], 0.0
    for hh in range(GLA_HEADS):
        sl = slice(hh * GLA_DV, (hh + 1) * GLA_DV)
        oh, r, do = oraw[:, sl], gr[:, sl], dout[:, sl].astype(F32)
        rs = lax.rsqrt(jnp.mean(oh * oh, axis=-1, keepdims=True) + EPS)
        sg = _sig(r)
        dn = do * (r * sg)
        d_r.append(do * (oh * rs * g_out) * (sg + r * sg * (1.0 - sg)))
        dx, dgh = _rms_bwd(oh, dn, g_out)
        d_o.append(dx)
        dg = dg + dgh
    return jnp.concatenate(d_o, axis=1), jnp.concatenate(d_r, axis=1), dg


def _adam(w, g, m, v):
    m = ADAM_B1 * m + (1.0 - ADAM_B1) * g
    v = ADAM_B2 * v + (1.0 - ADAM_B2) * (g * g)
    m_hat = m / (1.0 - ADAM_B1 ** ADAM_STEP)
    v_hat = v / (1.0 - ADAM_B2 ** ADAM_STEP)
    return -ADAM_LR * (m_hat / (jnp.sqrt(v_hat) + ADAM_EPS) + ADAM_WD * w), m, v


def _layer_fwd(x, mem, w, p, tabs, tag):
    c4, s4 = tabs
    sv = {'x0': x}
    nm = lambda t: f'{t}_{tag}'
    za, h = _mm(x, w['in_a'], mode='nn', out_dtype=BF16, norm_g=p['g_mix'], emit_norm=True, tn=768, name=nm('in_a'))
    zb = _mm(h, w['in_b'], mode='nn', out_dtype=F32, name=nm('in_b'))
    zc = _mm(h, w['in_c'], mode='nn', out_dtype=F32, name=nm('in_c'))
    sv.update(h=h, zc=zc)
    ff = Cols(zb, 128, B_FF // 128)
    (lf,) = _rowwise(lambda f, b: (_logsig(f + b),), [ff], [p['b_fox']], [(128, F32)], name=nm('fox_lf'))
    cum = _cumsum_rows(lf, reverse=False, name=nm('fox_cum'))
    ckf = jnp.pad(cum[:, :FOX_HEADS].T.reshape(2, 2, x.shape[0]), ((0, 0), (0, 6), (0, 0)))
    fox = dict(qc=0, kc=2, vc=4, nb=2, g=2, scale=FOX_HD ** -0.5, mode='causal', ck=ckf)
    o_fox, lse_fox = _mattn_fwd(za, za, za, name=nm('fox_attn'), **fox)
    sv.update(ff=ff, za=za, fox=fox, o_fox=o_fox, lse_fox=lse_fox)
    glow = Cols(zb, 128, B_GLOW // 128)
    gr = Cols(zb, 512, B_GR // 512)

    def gate_fn(gl, wg, bg):
        return (_logsig(_dot(gl.astype(BF16), wg) + bg) / GLA_TAU,)

    (la,) = _rowwise(gate_fn, [glow], [w['gate'], p['b_gla']], [(256, F32)], name=nm('gla_gate'))
    gla = dict(qc=B_GQ // LANES, kc=B_GK // LANES, vc=B_GV // LANES)
    oraw, states = _gla_fwd(zb, la, name=nm('gla'), **gla)
    (o_gla,) = _rowwise(_gla_out_fwd, [oraw, gr], [p['g_gla_out']], [(512, BF16)], name=nm('gla_out'))
    sv.update(glow=glow, gr=gr, zb=zb, la=la, gla=gla, states=states, oraw=oraw, o_gla=o_gla)
    mq = Cols(zb, 256, B_MQ // 256)
    mkv = Cols(zb, 128, B_MKV // 128)
    mkr2 = Cols(zb, 256, B_MKR // 256)
    qp, cqn = _mm(mq, w['uq'], mode='nn', out_dtype=F32, norm_g=p['g_mla_q'], emit_norm=True, name=nm('mla_uq'))
    kvp, ckvn = _mm(mkv, w['ukv'], mode='nn', out_dtype=BF16, norm_g=p['g_mla_kv'], emit_norm=True,
                    name=nm('mla_ukv'))

    def rope_fn(qv, kr, c4v, s4v):
        q_rope = qv[:, 256:384] * c4v + qv[:, 384:512] * s4v
        return jnp.concatenate([qv[:, 0:256], q_rope], axis=1), kr[:, 0:128] * c4v + kr[:, 128:256] * s4v

    qall, kr4 = _rowwise(rope_fn, [qp, mkr2, c4, s4], [], [(384, BF16), (128, BF16)], name=nm('rope'))
    mla = dict(qc=0, kc=0, vc=2, nb=2, g=2, scale=(MLA_NOPE + MLA_ROPE) ** -0.5, mode='chunk', qr=qall, qrc=2, kr=kr4)
    o_mla, lse_mla = _mattn_fwd(qall, kvp, kvp, name=nm('mla_attn'), **mla)
    sv.update(mq=mq, mkv=mkv, cqn=cqn, ckvn=ckvn, qall=qall, kvp=kvp, mla=mla, o_mla=o_mla, lse_mla=lse_mla)
    of_m, om_m = o_fox, o_mla
    sv.update(of_m=of_m, om_m=om_m)
    b_br = p['b_branch']

    def first(acc, zg, bb):
        return _sig(zg + bb) * acc

    def more(acc, zg, bb, prev):
        return prev + _sig(zg + bb) * acc

    y = _mm(of_m, w['w_up_fox'], mode='nn', out_dtype=F32, name=nm('up_fox'), epilogue=first,
            extras=[(zc, *_mn(col_off=0)), (b_br, *_nvec(col_off=0))])
    y = _mm(o_gla, w['w_up_gla'], mode='nn', out_dtype=F32, name=nm('up_gla'), epilogue=more,
            extras=[(zc, *_mn(col_off=2)), (b_br, *_nvec(col_off=2)), (y, *_mn())])
    y = _mm(om_m, w['w_up_mla'], mode='nn', out_dtype=BF16, name=nm('up_mla'), epilogue=more,
            extras=[(zc, *_mn(col_off=4)), (b_br, *_nvec(col_off=4)), (y, *_mn())])
    add = lambda acc, res: res + acc
    x1 = _mm(y, w['w_out'], mode='nn', out_dtype=F32, name=nm('out'), epilogue=add, extras=[(x, *_mn())])
    sv.update(y=y, x1=x1)
    qx, hx = _mm(x1, w['w_xq'], mode='nn', out_dtype=BF16, norm_g=p['g_xa'], emit_norm=True, name=nm('xq'))
    kvx, mn = _mm(mem, w['w_xkv'], mode='nn', out_dtype=BF16, norm_g=p['g_mem'], emit_norm=True, name=nm('xkv'))
    xa = dict(qc=0, kc=0, vc=4, nb=4, g=1, scale=XA_HD ** -0.5, mode='full')
    ox_m, lse_x = _mattn_fwd(qx, kvx, kvx, name=nm('xa_attn'), **xa)
    x2 = _mm(ox_m, w['w_xo'], mode='nn', out_dtype=F32, name=nm('xo'), epilogue=add, extras=[(x1, *_mn())])
    sv.update(hx=hx, mn=mn, qx=qx, kvx=kvx, xa=xa, lse_x=lse_x, ox_m=ox_m, x2=x2)
    hpre, hm = _mm(x2, w['w_mlp1'], mode='nn', out_dtype=BF16, norm_g=p['g_mlp'], emit_norm=True, name=nm('mlp1'))
    relu2 = lambda t: jnp.square(jnp.maximum(t.astype(F32), 0.0))
    x3 = _mm(hpre, w['w_mlp2'], mode='nn', out_dtype=F32, name=nm('mlp2'), a_fn=relu2, epilogue=add,
             extras=[(x2, *_mn())])
    sv.update(hpre=hpre, hm=hm)
    return x3, sv


def _layer_bwd(dx3, mem, w, p, tabs, sv, tag):
    c4, s4 = tabs
    nm = lambda t: f'{t}_{tag}'
    s = dx3.shape[0]
    gw, gs = {}, {}
    relu2 = lambda t: jnp.square(jnp.maximum(t.astype(F32), 0.0))
    gw['w_mlp2'] = _mm(sv['hpre'], dx3, mode='tn', out_dtype=F32, name=nm('d_mlp2'), a_fn=relu2, tn=256)
    dact = lambda acc, hp: acc * (2.0 * jnp.maximum(hp.astype(F32), 0.0))
    dhpre = _mm(dx3, w['w_mlp2'], mode='nt', out_dtype=BF16, name=nm('d_act'), epilogue=dact,
                extras=[(sv['hpre'], *_mn())])
    gw['w_mlp1'] = _mm(sv['hm'], dhpre, mode='tn', out_dtype=F32, name=nm('d_mlp1'))
    dhm = _mm(dhpre, w['w_mlp1'], mode='nt', out_dtype=F32, name=nm('d_hm'))
    dx2, gs['g_mlp'] = _norm_bwd_call(sv['x2'], dhm, p['g_mlp'], dx3, nm('d_norm_mlp'))
    gw['w_xo'] = _mm(sv['ox_m'], dx2, mode='tn', out_dtype=F32, name=nm('d_xo'), tn=256)
    dox = _mm(dx2, w['w_xo'], mode='nt', out_dtype=BF16, name=nm('d_ox'))
    dqx_m, dkx, dvx = _mattn_bwd(sv['qx'], sv['kvx'], sv['kvx'], sv['ox_m'], dox, sv['lse_x'], name=nm('xa_bwd'),
                                 **sv['xa'])
    dkvx = jnp.concatenate([dkx, dvx], axis=1).astype(BF16)
    gw['w_xq'] = _mm(sv['hx'], dqx_m, mode='tn', out_dtype=F32, name=nm('d_xq'), tn=256)
    dhx = _mm(dqx_m, w['w_xq'], mode='nt', out_dtype=F32, name=nm('d_hx'))
    gw['w_xkv'] = _mm(sv['mn'], dkvx, mode='tn', out_dtype=F32, name=nm('d_xkv'))
    dmn = _mm(dkvx, w['w_xkv'], mode='nt', out_dtype=F32, name=nm('d_mn'))
    _, gs['g_mem'] = _norm_bwd_call(mem, dmn, p['g_mem'], None, nm('d_norm_mem'))
    dx1, gs['g_xa'] = _norm_bwd_call(sv['x1'], dhx, p['g_xa'], dx2, nm('d_norm_xa'))
    gw['w_out'] = _mm(sv['y'], dx1, mode='tn', out_dtype=F32, name=nm('d_out'), tn=256)
    dy = _mm(dx1, w['w_out'], mode='nt', out_dtype=BF16, name=nm('d_y'))
    zc, b_br = sv['zc'], p['b_branch']

    def du_fn(dyv, zg, bb):
        g = _sig(zg + bb)
        d = dyv.astype(F32)
        return d * g[:, 0:1024], d * g[:, 1024:2048], d * g[:, 2048:3072]

    du = _rowwise(du_fn, [dy, zc], [b_br], [(D_MODEL, BF16)] * 3, name=nm('d_u'))

    def dgate(acc, dyv, zg, bb):
        g = _sig(zg + bb)
        return dyv.astype(F32) * acc * g * (1.0 - g)

    dzc, do_br = [], []
    for q, (o_m, wn) in enumerate(((sv['of_m'], 'w_up_fox'), (sv['o_gla'], 'w_up_gla'), (sv['om_m'], 'w_up_mla'))):
        dzc.append(_mm(o_m, w[wn], mode='nn', out_dtype=F32, name=nm(f'd_zg{q}'), epilogue=dgate,
                       extras=[(dy, *_mn()), (zc, *_mn(col_off=2 * q)), (b_br, *_nvec(col_off=2 * q))]))
        gw[wn] = _mm(o_m, du[q], mode='tn', out_dtype=F32, name=nm(f'd_up{q}'))
        do_br.append(_mm(du[q], w[wn], mode='nt', out_dtype=F32 if q == 1 else BF16, name=nm(f'd_o{q}')))
    dzc = jnp.concatenate(dzc, axis=1)
    (gs['b_branch'],) = _rowwise(lambda t: (t,), [dzc], [], [], [3072], name=nm('d_bbranch'))
    za = sv['za']
    dfq, dfk, dfv, dck, dcq = _mattn_bwd(za, za, za, sv['o_fox'], do_br[0], sv['lse_fox'], name=nm('fox_bwd'),
                                         **sv['fox'])
    dcum = _padc(dck[:, :2, :].reshape(FOX_HEADS, s).T + dcq.reshape(s, 2, LANES)[:, :, :2].reshape(s, FOX_HEADS), 128)
    dlf = _cumsum_rows(dcum, reverse=True, name=nm('fox_dcum'))

    def dff_fn(dl, f, b):
        d = dl * _sig(-(f + b))
        return d, d

    dff, db_fox = _rowwise(dff_fn, [dlf, sv['ff']], [p['b_fox']], [(128, F32)], [128], name=nm('fox_dff'))
    gs['b_fox'] = db_fox
    dza = jnp.concatenate([dfq, dfk, dfv], axis=1).astype(BF16)
    dqn, dkn, dvv, dq_rope, dk_rope = _mattn_bwd(sv['qall'], sv['kvp'], sv['kvp'], sv['o_mla'], do_br[2],
                                                 sv['lse_mla'], name=nm('mla_bwd'), **sv['mla'])

    def drope_fn(dn, dq, dk, c4v, s4v):
        return jnp.concatenate([dn, dq * c4v, dq * s4v], axis=1), jnp.concatenate([dk * c4v, dk * s4v], axis=1)

    dqp, dmkr2 = _rowwise(drope_fn, [dqn, dq_rope, dk_rope, c4, s4], [], [(512, BF16), (256, BF16)], name=nm('d_rope'))
    dkvp = jnp.concatenate([dkn, dvv], axis=1).astype(BF16)
    gw['uq'] = _mm(sv['cqn'], dqp, mode='tn', out_dtype=F32, name=nm('d_uq'))
    dcqn = _mm(dqp, w['uq'], mode='nt', out_dtype=F32, name=nm('d_cqn'))
    gw['ukv'] = _mm(sv['ckvn'], dkvp, mode='tn', out_dtype=F32, name=nm('d_ukv'))
    dckvn = _mm(dkvp, w['ukv'], mode='nt', out_dtype=F32, name=nm('d_ckvn'))
    dmq, gs['g_mla_q'] = _norm_bwd_call(sv['mq'], dcqn, p['g_mla_q'], None, nm('d_norm_q'))
    dmkv, gs['g_mla_kv'] = _norm_bwd_call(sv['mkv'], dckvn, p['g_mla_kv'], None, nm('d_norm_kv'))
    doraw, dgr, gs['g_gla_out'] = _rowwise(_gla_out_bwd, [sv['oraw'], sv['gr'], do_br[1]], [p['g_gla_out']],
                                           [(512, F32), (512, BF16)], [128], name=nm('d_gla_out'))
    st = sv['states']
    st_prev = jnp.concatenate([jnp.zeros_like(st[:, :1]), st[:, :-1]], axis=1)
    dgq, dgk, dgv, dla = _gla_bwd(sv['zb'], sv['la'], st, st_prev, doraw, name=nm('gla_bwd'), **sv['gla'])

    def dgate_fn(dl, gl, wg, bg):
        pre = _dot(gl.astype(BF16), wg) + bg
        dpre = dl * (1.0 / GLA_TAU) * _sig(-pre)
        return dpre, _dot(dpre.astype(BF16), wg, NT), dpre

    dpre, dglow, gs['b_gla'] = _rowwise(dgate_fn, [dla, sv['glow']], [w['gate'], p['b_gla']],
                                        [(256, BF16), (128, BF16)], [256], name=nm('d_gla_gate'))
    gw['gate'] = _mm(sv['glow'], dpre, mode='tn', out_dtype=F32, name=nm('d_wgate'))
    bf = lambda t: t.astype(BF16)
    dzb = jnp.concatenate([dgr, bf(dgq), bf(dgk), bf(dgv), bf(dmq), dmkr2, bf(dff), dglow, bf(dmkv),
                           jnp.zeros((s, B_W - B_END), BF16)], axis=1)
    h = sv['h']
    gw['in_a'] = _mm(h, dza, mode='tn', out_dtype=F32, name=nm('d_in_a'), tn=768)
    gw['in_b'] = _mm(h, dzb, mode='tn', out_dtype=F32, name=nm('d_in_b'))
    gw['in_c'] = _mm(h, dzc, mode='tn', out_dtype=F32, name=nm('d_in_c'), tn=256)
    add = lambda acc, prev: prev + acc
    dh = _mm(dza, w['in_a'], mode='nt', out_dtype=F32, name=nm('d_h_a'))
    dh = _mm(dzb, w['in_b'], mode='nt', out_dtype=F32, name=nm('d_h_b'), epilogue=add, extras=[(dh, *_mn())])
    dh = _mm(dzc, w['in_c'], mode='nt', out_dtype=F32, name=nm('d_h_c'), epilogue=add, extras=[(dh, *_mn())])
    dx0, gs['g_mix'] = _norm_bwd_call(sv['x0'], dh, p['g_mix'], dx1, nm('d_norm_mix'))
    return dx0, gw, gs


def _loss_head(x, target, g_final):
    d = x.shape[1]

    def fn(xv, tv, gv):
        r = lax.rsqrt(jnp.mean(xv * xv, axis=-1, keepdims=True) + EPS)
        xh = xv * r
        e = xh * gv - tv
        dy = e * (1.0 / d)
        gd = dy * gv
        dx = r * (gd - xh * jnp.mean(gd * xh, axis=-1, keepdims=True))
        row_loss = 0.5 * jnp.mean(e * e, axis=-1, keepdims=True)
        return dx, dy * xh, jnp.broadcast_to(row_loss, (xv.shape[0], LANES))

    return _rowwise(fn, [x, target], [g_final], [(d, F32)], [d, LANES], name='loss_head')


def _small_sizes(shapes):
    return [math.prod(shapes[nm]) for nm in SMALL]


def _step(args):
    shapes = {nm: args[nm].shape for nm in ORDER}
    x, mem, target = args['x'][0], args['mem'][0], args['loss_target'][0]
    s = x.shape[0]

    def wire(nm):
        w = args[nm].astype(BF16)
        return jnp.pad(w, ((0, 0), (0, 0), (0, WIN_PAD - WIN_SHARD))) if nm == 'w_in' else w

    gathered = dict(zip([nm for nm, _ in BIG], _gather_shards([wire(nm) for nm, _ in BIG], name='gather_weights')))

    tabs = _rope_tables(s)
    layers_w, layers_p = [], []
    for l in range(DEPTH):
        layers_w.append(_repack_layer_weights({nm: _full_layer(gathered[nm], ax, l) for nm, ax in BIG}))
        layers_p.append({
            'g_mix': args['g_mix'][l][None], 'b_fox': _padc(args['b_fox_forget'][l][None], 128),
            'b_gla': args['b_gla_gate'][l][None], 'g_gla_out': args['g_gla_out'][l][None],
            'g_mla_q': args['g_mla_q'][l][None], 'g_mla_kv': args['g_mla_kv'][l][None],
            'b_branch': args['b_branch_gate'][l][None], 'g_xa': args['g_xa'][l][None],
            'g_mem': args['g_mem'][l][None], 'g_mlp': args['g_mlp'][l][None]})

    saved = []
    xl = x
    for l in range(DEPTH):
        xl, sv = _layer_fwd(xl, mem, layers_w[l], layers_p[l], tabs, f'l{l}')
        saved.append(sv)
    dx, dg_final, loss_lanes = _loss_head(xl, target, args['g_final'][None])
    gw_layers, gs_layers = [None] * DEPTH, [None] * DEPTH
    for l in reversed(range(DEPTH)):
        dx, gw, gs = _layer_bwd(dx, mem, layers_w[l], layers_p[l], tabs, saved[l], f'l{l}')
        gw_layers[l], gs_layers[l] = _unpack_layer_grads(gw), gs
    grad_x = dx[None]

    names = [nm for nm, _ in BIG]
    cidx = lax.axis_index('c')
    chip = 2 * lax.axis_index('x') + lax.axis_index('y')
    pick = lambda t, i, axis: lax.dynamic_index_in_dim(t, i, axis=axis, keepdims=False)
    per_chip = [_split_full(jnp.stack([gw_layers[l][nm] for l in range(DEPTH)]), ax).astype(BF16) for nm, ax in BIG]
    got = _to_sibling([pick(g, 1 - cidx, 1) for g in per_chip], name='grads_core_swap')
    pairs = []
    for nm, g, b in zip(names, per_chip, got):
        k, n = b.shape[1:]
        (p,) = _rowwise(lambda u, v: (u.astype(F32) + v.astype(F32),),
                        [pick(g, cidx, 1).reshape(N_CHIPS * k, n), b.reshape(N_CHIPS * k, n)], [], [(n, BF16)],
                        name=f'pair_sum_{nm}')
        pairs.append(p.reshape(N_CHIPS, k, n))
    from_chips = _chip_exchange(pairs, name='grads_chip_exchange')
    mine = [_sum_chips(pick(p, chip, 0), r, name=f'chip_sum_{nm}') for nm, p, r in zip(names, pairs, from_chips)]
    theirs = _to_sibling(mine, name='grads_join')
    gshard = {nm: jnp.where(cidx == 0, jnp.stack([a, b]), jnp.stack([b, a])) for nm, a, b in zip(names, mine, theirs)}

    small_g = []
    for nm, key in (('g_mix', 'g_mix'), ('b_fox_forget', 'b_fox'), ('b_gla_gate', 'b_gla'),
                    ('g_gla_out', 'g_gla_out'), ('g_mla_q', 'g_mla_q'), ('g_mla_kv', 'g_mla_kv'),
                    ('b_branch_gate', 'b_branch'), ('g_xa', 'g_xa'), ('g_mem', 'g_mem'), ('g_mlp', 'g_mlp')):
        width = shapes[nm][1]
        small_g.append(jnp.concatenate([gs_layers[l][key][0, :width] for l in range(DEPTH)]))
    small_g.append(dg_final[0])
    small_g.append(loss_lanes[0, :1])
    flat = jnp.concatenate(small_g)
    n_small = flat.shape[0]
    srows = -(-n_small // (8 * LANES)) * 8
    pad = lambda v: jnp.pad(v, (0, srows * LANES - v.shape[0])).reshape(srows, LANES)
    all_small = _all_gather8(pad(flat), name='gather_small')
    sw, sm, svv = (pad(jnp.concatenate([args[pre + nm].reshape(-1) for nm in SMALL] + [jnp.zeros((1,), F32)]))
                   for pre in ('', 'm_', 'v_'))

    def small_body(g_ref, w_ref, m_ref, v_ref, go_ref, d_ref, mo_ref, vo_ref):
        g = g_ref[0]
        for q in range(1, N_DEV):
            g = g + g_ref[q]
        go_ref[...] = g
        d_ref[...], mo_ref[...], vo_ref[...] = _adam(w_ref[...], g, m_ref[...], v_ref[...])

    sg, sd, snm, snv = pl.pallas_call(
        small_body, name='small_sum_adam', out_shape=[jax.ShapeDtypeStruct((srows, LANES), F32)] * 4,
        compiler_params=pltpu.CompilerParams(vmem_limit_bytes=VMEM_LIMIT))(all_small, sw, sm, svv)

    def unsmall(buf):
        v, out, off = buf.reshape(-1), {}, 0
        for nm in SMALL:
            nel = math.prod(shapes[nm])
            out[nm] = v[off:off + nel].reshape(shapes[nm])
            off += nel
        return out, v[off]

    res = {}
    (res['grad'], loss), (res['delta'], _), (res['m'], _), (res['v'], _) = (unsmall(t) for t in (sg, sd, snm, snv))

    for nm, _ in BIG:
        shp = args[nm].shape
        view = lambda t: t.reshape(shp[0] * shp[1], shp[2])
        d, m2, v2 = _rowwise(_adam, [view(args[nm]), view(gshard[nm]), view(args['m_' + nm]), view(args['v_' + nm])],
                             [], [(shp[2], F32)] * 3, name=f'adam_{nm}')
        res['grad'][nm], res['delta'][nm], res['m'][nm], res['v'][nm] = (
            gshard[nm], d.reshape(shp), m2.reshape(shp), v2.reshape(shp))

    return (loss, grad_x, *[res['grad'][nm] for nm in ORDER], *[res['delta'][nm] for nm in ORDER],
            *[res['m'][nm] for nm in ORDER], *[res['v'][nm] for nm in ORDER])


def kernel(x, mem, g_mix, w_in, b_fox_forget, w_gla_gate, b_gla_gate, g_gla_out, g_mla_q, w_mla_uq, g_mla_kv, w_mla_ukv, b_branch_gate, w_up_fox, w_up_gla, w_up_mla, w_out, g_xa, g_mem, w_xq, w_xkv, w_xo, g_mlp, w_mlp1, w_mlp2, g_final, loss_target, m_g_mix, m_w_in, m_b_fox_forget, m_w_gla_gate, m_b_gla_gate, m_g_gla_out, m_g_mla_q, m_w_mla_uq, m_g_mla_kv, m_w_mla_ukv, m_b_branch_gate, m_w_up_fox, m_w_up_gla, m_w_up_mla, m_w_out, m_g_xa, m_g_mem, m_w_xq, m_w_xkv, m_w_xo, m_g_mlp, m_w_mlp1, m_w_mlp2, m_g_final, v_g_mix, v_w_in, v_b_fox_forget, v_w_gla_gate, v_b_gla_gate, v_g_gla_out, v_g_mla_q, v_w_mla_uq, v_g_mla_kv, v_w_mla_ukv, v_b_branch_gate, v_w_up_fox, v_w_up_gla, v_w_up_mla, v_w_out, v_g_xa, v_g_mem, v_w_xq, v_w_xkv, v_w_xo, v_g_mlp, v_w_mlp1, v_w_mlp2, v_g_final):
    return _step(dict(locals()))
```

```python
import functools
import math
import typing

import jax
import jax.numpy as jnp
from jax import lax
from jax.experimental import pallas as pl
from jax.experimental.pallas import tpu as pltpu

F32 = jnp.float32
BF16 = jnp.bfloat16
MESH = pl.DeviceIdType.MESH

D_MODEL = 1024
DEPTH = 2
CHUNK = 64
EPS = 1e-6
FOX_HEADS, FOX_HD = 4, 64
GLA_HEADS, GLA_DK, GLA_DV, GLA_RANK, GLA_TAU = 4, 64, 128, 16, 16.0
MLA_HEADS, MLA_Q_RANK, MLA_KV_RANK, MLA_NOPE, MLA_ROPE, MLA_VD = 4, 256, 128, 64, 32, 64
ROPE_BASE = 10000.0
XA_HEADS, XA_HD = 4, 128
D_FF = 4 * D_MODEL
IN_SIZES = (256, 256, 256, 4, 256, 256, 512, 16, 512, 256, 128, 32, 3072)
N_IN = sum(IN_SIZES)

ADAM_LR, ADAM_B1, ADAM_B2, ADAM_EPS, ADAM_WD, ADAM_STEP = 0.001, 0.9, 0.999, 1e-08, 0.01, 10

N_CHIPS = 4
N_DEV = 8
LANES = 128
VMEM_LIMIT = 48 * 1024 * 1024
MASK_VALUE = -1e30

BIG = (('w_in', 2), ('w_gla_gate', 2), ('w_mla_uq', 2), ('w_mla_ukv', 2), ('w_up_fox', 2), ('w_up_gla', 2),
       ('w_up_mla', 2), ('w_out', 1), ('w_xq', 1), ('w_xkv', 1), ('w_xo', 2), ('w_mlp1', 2), ('w_mlp2', 1))
SMALL = ('g_mix', 'b_fox_forget', 'b_gla_gate', 'g_gla_out', 'g_mla_q', 'g_mla_kv', 'b_branch_gate',
         'g_xa', 'g_mem', 'g_mlp', 'g_final')
ORDER = ('g_mix', 'w_in', 'b_fox_forget', 'w_gla_gate', 'b_gla_gate', 'g_gla_out', 'g_mla_q', 'w_mla_uq',
         'g_mla_kv', 'w_mla_ukv', 'b_branch_gate', 'w_up_fox', 'w_up_gla', 'w_up_mla', 'w_out', 'g_xa', 'g_mem',
         'w_xq', 'w_xkv', 'w_xo', 'g_mlp', 'w_mlp1', 'w_mlp2', 'g_final')


def _params(*sem):
    return pltpu.CompilerParams(dimension_semantics=sem, vmem_limit_bytes=VMEM_LIMIT)


def _sig(x):
    return 1.0 / (1.0 + jnp.exp(-x))


def _logsig(x):
    return jnp.minimum(x, 0.0) - jnp.log(1.0 + jnp.exp(-jnp.abs(x)))


NN = (((1,), (0,)), ((), ()))
NT = (((1,), (1,)), ((), ()))
TN = (((0,), (0,)), ((), ()))


def _dot(a, b, dims=NN):
    return lax.dot_general(a, b, dims, preferred_element_type=F32)


class Cols(typing.NamedTuple):
    arr: jax.Array
    width: int
    blk: int


def _tri_dot(tri, x):
    hi = x.astype(BF16)
    r1 = x - hi.astype(F32)
    mid = r1.astype(BF16)
    lo = (r1 - mid.astype(F32)).astype(BF16)
    return _dot(tri, hi) + _dot(tri, mid) + _dot(tri, lo)


MM_TILES = ((1024, 1024), (1024, 512), (512, 1024), (512, 512), (512, 256), (256, 512), (256, 256), (128, 128))
MM_VMEM_BUDGET = 38 * 1024 * 1024


def _mm_tiles(m, n, k, a_bytes, b_bytes, out_bytes, ex_bytes, has_norm, emit_norm, has_fn):
    for tm, tn in MM_TILES:
        tm, tn = min(tm, m), min(tn, n)
        if m % tm or n % tn:
            continue
        blocks = tm * k * a_bytes + k * tn * b_bytes + tm * tn * (out_bytes + ex_bytes) + (tm * k * 2 if emit_norm else 0)
        temps = tm * tn * 4 + (tm * k * 2 if has_norm else 0) + (tm * k * 6 if has_fn or has_norm else 0)
        if 2 * blocks + temps <= MM_VMEM_BUDGET:
            return tm, tn
    raise ValueError((m, n, k))


def _mm(a, b, *, mode, out_dtype, name, norm_g=None, emit_norm=False, a_fn=None, extras=(), epilogue=None):
    a_blk = 0
    if isinstance(a, Cols):
        a, width, a_blk = a
        a_shape = (a.shape[0], width)
    else:
        a_shape = a.shape
    if mode == 'tn':
        k, m = a_shape
    else:
        m, k = a_shape
    n = b.shape[0] if mode == 'nt' else b.shape[1]
    assert (b.shape[1] if mode == 'nt' else b.shape[0]) == k, (name, a.shape, b.shape)
    has_norm = norm_g is not None
    ex_bytes = sum(arr.dtype.itemsize for arr, kind, _ in extras if kind == 'mn')
    tm, tn = _mm_tiles(m, n, k, a.dtype.itemsize, b.dtype.itemsize, jnp.dtype(out_dtype).itemsize, ex_bytes, has_norm,
                       emit_norm, a_fn is not None)
    assert all(col % tn == 0 for _, _, col in extras), (name, tn)
    assert a_blk == 0 or (mode == 'nn') or (mode == 'tn' and tm == m)
    if mode == 'tn':
        a_spec = pl.BlockSpec((k, tm), lambda i, j: (0, i + a_blk))
    else:
        a_spec = pl.BlockSpec((tm, k), lambda i, j: (i, a_blk))
    b_spec = pl.BlockSpec((tn, k), lambda i, j: (j, 0)) if mode == 'nt' else pl.BlockSpec((k, tn), lambda i, j: (0, j))
    dims = {'nn': NN, 'nt': NT, 'tn': TN}[mode]
    assert not (has_norm and mode != 'nn')
    n_ex = len(extras)

    def body(*refs):
        a_ref, b_ref = refs[0], refs[1]
        pos = 2
        g_ref = None
        if has_norm:
            g_ref = refs[pos]
            pos += 1
        ex_refs = refs[pos:pos + n_ex]
        pos += n_ex
        o_ref = refs[pos]
        pos += 1
        h_ref = None
        if emit_norm:
            h_ref = refs[pos]
            pos += 1
        if has_norm:
            an_ref = refs[pos]

            @pl.when(pl.program_id(1) == 0)
            def _():
                xf = a_ref[...].astype(F32)
                y = xf * lax.rsqrt(jnp.mean(xf * xf, axis=-1, keepdims=True) + EPS) * g_ref[...]
                an_ref[...] = y.astype(BF16)
                if emit_norm:
                    h_ref[...] = y.astype(BF16)

            av = an_ref[...]
        else:
            av = a_ref[...]
            if a_fn is not None:
                av = a_fn(av)
            av = av.astype(BF16)
        acc = _dot(av, b_ref[...].astype(BF16), dims)
        if epilogue is not None:
            acc = epilogue(acc, *[r[...] for r in ex_refs])
        o_ref[...] = acc.astype(out_dtype)

    in_specs = [a_spec, b_spec]
    args = [a, b]
    if has_norm:
        in_specs.append(pl.BlockSpec((1, k), lambda i, j: (0, 0)))
        args.append(norm_g)
    for arr, kind, col in extras:
        if kind == 'mn':
            in_specs.append(pl.BlockSpec((tm, tn), lambda i, j, o=col // tn: (i, j + o)))
        else:
            in_specs.append(pl.BlockSpec((1, tn), lambda i, j, o=col // tn: (0, j + o)))
        args.append(arr)
    out_shape = [jax.ShapeDtypeStruct((m, n), out_dtype)]
    out_specs = [pl.BlockSpec((tm, tn), lambda i, j: (i, j))]
    if emit_norm:
        out_shape.append(jax.ShapeDtypeStruct((m, k), BF16))
        out_specs.append(pl.BlockSpec((tm, k), lambda i, j: (i, 0)))
    scratch = [pltpu.VMEM((tm, k), BF16)] if has_norm else []
    res = pl.pallas_call(
        body, name=name, grid=(m // tm, n // tn), in_specs=in_specs, out_specs=out_specs, out_shape=out_shape,
        scratch_shapes=scratch, compiler_params=_params('arbitrary', 'arbitrary'))(*args)
    return res if emit_norm else res[0]


def _mn(col_off=0):
    return 'mn', col_off


def _nvec(col_off=0):
    return 'n', col_off


def _rowwise(fn, rows, consts, outs, sums=(), *, name, ts=256):
    views = [x if isinstance(x, Cols) else Cols(x, x.shape[1], 0) for x in rows]
    rows = [v.arr for v in views]
    r = rows[0].shape[0]
    ts = min(ts, r)
    assert r % ts == 0, (name, r, ts)
    nr, nc, no, ns = len(rows), len(consts), len(outs), len(sums)

    def body(*refs):
        vals = fn(*[x[...] for x in refs[:nr + nc]])
        for q in range(no):
            refs[nr + nc + q][...] = vals[q].astype(outs[q][1])
        if ns:
            @pl.when(pl.program_id(0) == 0)
            def _():
                for q in range(ns):
                    refs[nr + nc + no + q][...] = jnp.zeros((1, sums[q]), F32)

            for q in range(ns):
                refs[nr + nc + no + q][...] += jnp.sum(vals[no + q].astype(F32), axis=0, keepdims=True)

    in_specs = [pl.BlockSpec((ts, v.width), lambda i, blk=v.blk: (i, blk)) for v in views]
    in_specs += [pl.BlockSpec(x.shape, lambda i, nd=x.ndim: (0,) * nd) for x in consts]
    out_specs = [pl.BlockSpec((ts, w), lambda i: (i, 0)) for w, _ in outs]
    out_specs += [pl.BlockSpec((1, w), lambda i: (0, 0)) for w in sums]
    out_shape = [jax.ShapeDtypeStruct((r, w), dt) for w, dt in outs]
    out_shape += [jax.ShapeDtypeStruct((1, w), F32) for w in sums]
    return pl.pallas_call(body, name=name, grid=(r // ts,), in_specs=in_specs, out_specs=out_specs,
                          out_shape=out_shape, compiler_params=_params('arbitrary'))(*rows, *consts)


def _cumsum_rows(x, *, reverse, name, bs=256):
    s, w = x.shape
    bs = min(bs, s)
    nb = s // bs

    def body(x_ref, o_ref, carry):
        @pl.when(pl.program_id(0) == 0)
        def _():
            carry[...] = jnp.zeros_like(carry)

        r = lax.broadcasted_iota(jnp.int32, (bs, bs), 0)
        c = lax.broadcasted_iota(jnp.int32, (bs, bs), 1)
        tri = jnp.where((c >= r) if reverse else (c <= r), 1.0, 0.0).astype(BF16)
        xv = x_ref[...]
        o_ref[...] = _tri_dot(tri, xv) + carry[...]
        carry[...] += jnp.sum(xv, axis=0, keepdims=True)

    imap = (lambda i: (nb - 1 - i, 0)) if reverse else (lambda i: (i, 0))
    return pl.pallas_call(body, name=name, grid=(nb,), in_specs=[pl.BlockSpec((bs, w), imap)],
                          out_specs=pl.BlockSpec((bs, w), imap), out_shape=jax.ShapeDtypeStruct((s, w), F32),
                          scratch_shapes=[pltpu.VMEM((1, w), F32)], compiler_params=_params('arbitrary'))(x)


def _mask(mode, i, j, bq, bk):
    qpos = i * bq + lax.broadcasted_iota(jnp.int32, (bq, bk), 0)
    kpos = j * bk + lax.broadcasted_iota(jnp.int32, (bq, bk), 1)
    if mode == 'causal':
        return kpos <= qpos
    return kpos < (jnp.right_shift(qpos, int(math.log2(CHUNK))) + 1) * CHUNK


ROPE_SHIFT = int(math.log2(MLA_ROPE))


def _lane_masks(g, b, rope):
    lane = lax.broadcasted_iota(jnp.int32, (1, LANES), 1)
    heads = [None if g == 1 else (lane >= hh * (LANES // g)) & (lane < (hh + 1) * (LANES // g)) for hh in range(g)]
    ropes = [jnp.right_shift(lane, ROPE_SHIFT) == b * g + hh for hh in range(g)] if rope else [None] * g
    return heads, ropes


def _sel(mask, x):
    return x if mask is None else jnp.where(mask, x, jnp.zeros_like(x))


def _mattn_fwd(q, k, v, *, qc, kc, vc, nb, g, scale, mode, name, ck=None, qr=None, qrc=0, kr=None, blk=512):
    s, t = q.shape[0], k.shape[0]
    bq, bk = min(blk, s), min(blk, t)
    nq, nk = s // bq, t // bk
    tri = mode != 'full'
    bias, rope = ck is not None, qr is not None
    assert not tri or (bq == bk and bq % CHUNK == 0)

    def body(*refs):
        refs = list(refs)
        q_ref, k_ref, v_ref = refs[:3]
        pos = 3
        ck_ref = qr_ref = kr_ref = None
        if bias:
            ck_ref = refs[pos]
            pos += 1
        if rope:
            qr_ref, kr_ref = refs[pos:pos + 2]
            pos += 2
        o_ref, lse_ref, m_s, l_s, acc_s = refs[pos:]
        b, i, j = pl.program_id(0), pl.program_id(1), pl.program_id(2)
        heads, ropes = _lane_masks(g, b, rope)

        @pl.when(j == 0)
        def _():
            m_s[...] = jnp.full_like(m_s, MASK_VALUE)
            l_s[...] = jnp.zeros_like(l_s)
            acc_s[...] = jnp.zeros_like(acc_s)

        def compute(masked):
            q2, k2, v2 = q_ref[...], k_ref[...], v_ref[...]
            alphas, pvs = [], []
            for hh in range(g):
                sc = _dot(_sel(heads[hh], q2), k2, NT)
                if rope:
                    sc = sc + _dot(_sel(ropes[hh], qr_ref[...]), kr_ref[...], NT)
                sc = sc * scale
                if bias:
                    sc = sc - ck_ref[0, hh:hh + 1, :]
                if masked:
                    sc = jnp.where(_mask(mode, i, j, bq, bk), sc, MASK_VALUE)
                m_prev = m_s[hh]
                m_new = jnp.maximum(m_prev, jnp.max(sc, axis=1, keepdims=True))
                alpha = jnp.exp(m_prev - m_new)
                p = jnp.exp(sc - m_new)
                l_s[hh] = alpha * l_s[hh] + jnp.sum(p, axis=1, keepdims=True)
                m_s[hh] = m_new
                alphas.append(alpha)
                pvs.append(_dot(p.astype(BF16), _sel(heads[hh], v2)))
            alpha = alphas[0]
            for hh in range(1, g):
                alpha = jnp.where(heads[hh], alphas[hh], alpha)
            acc_s[...] = acc_s[...] * alpha + sum(pvs[1:], pvs[0])

        if tri:
            pl.when(j < i)(functools.partial(compute, False))
            pl.when(j == i)(functools.partial(compute, True))
        else:
            compute(False)

        @pl.when(j == nk - 1)
        def _():
            lane = lax.broadcasted_iota(jnp.int32, (bq, LANES), 1)
            l_full, lse = l_s[0], jnp.zeros((bq, LANES), F32)
            for hh in range(g):
                if hh:
                    l_full = jnp.where(heads[hh], l_s[hh], l_full)
                lse = jnp.where(lane == hh, m_s[hh] + jnp.log(l_s[hh]), lse)
            o_ref[...] = (acc_s[...] / l_full).astype(o_ref.dtype)
            lse_ref[...] = lse

    jj = (lambda i, j: jnp.minimum(i, j)) if tri else (lambda i, j: j)
    in_specs = [pl.BlockSpec((bq, LANES), lambda b, i, j: (i, qc + b)),
                pl.BlockSpec((bk, LANES), lambda b, i, j: (jj(i, j), kc + b)),
                pl.BlockSpec((bk, LANES), lambda b, i, j: (jj(i, j), vc + b))]
    args = [q, k, v]
    if bias:
        in_specs.append(pl.BlockSpec((1, 8, bk), lambda b, i, j: (b, 0, jj(i, j))))
        args.append(ck)
    if rope:
        in_specs += [pl.BlockSpec((bq, LANES), lambda b, i, j: (i, qrc)),
                     pl.BlockSpec((bk, LANES), lambda b, i, j: (jj(i, j), 0))]
        args += [qr, kr]
    out = pl.BlockSpec((bq, LANES), lambda b, i, j: (i, b))
    return pl.pallas_call(
        body, name=name, grid=(nb, nq, nk), in_specs=in_specs, out_specs=[out, out],
        out_shape=[jax.ShapeDtypeStruct((s, LANES * nb), BF16), jax.ShapeDtypeStruct((s, LANES * nb), F32)],
        scratch_shapes=[pltpu.VMEM((g, bq, 1), F32), pltpu.VMEM((g, bq, 1), F32), pltpu.VMEM((bq, LANES), F32)],
        compiler_params=_params('arbitrary', 'arbitrary', 'arbitrary'))(*args)


def _mattn_bwd(q, k, v, o, do, lse, *, qc, kc, vc, nb, g, scale, mode, name, ck=None, qr=None, qrc=0, kr=None,
               blk=512):
    s, t = q.shape[0], k.shape[0]
    bq, bk = min(blk, s), min(blk, t)
    nq, nk = s // bq, t // bk
    tri = mode != 'full'
    bias, rope = ck is not None, qr is not None

    def body(*refs):
        refs = list(refs)
        q_ref, k_ref, v_ref, o_ref, do_ref, lse_ref = refs[:6]
        pos = 6
        ck_ref = qr_ref = kr_ref = dck_ref = dcq_ref = dqr_ref = dkr_ref = dck_s = None
        if bias:
            ck_ref = refs[pos]
            pos += 1
        if rope:
            qr_ref, kr_ref = refs[pos:pos + 2]
            pos += 2
        dq_ref, dk_ref, dv_ref = refs[pos:pos + 3]
        pos += 3
        if bias:
            dck_ref, dcq_ref = refs[pos:pos + 2]
            pos += 2
        if rope:
            dqr_ref, dkr_ref = refs[pos:pos + 2]
            pos += 2
        dk_s, dv_s = refs[pos:pos + 2]
        if bias:
            dck_s = refs[pos + 2]
        b, j, i = pl.program_id(0), pl.program_id(1), pl.program_id(2)
        heads, ropes = _lane_masks(g, b, rope)

        @pl.when((j == 0) & (i == 0))
        def _():
            dq_ref[...] = jnp.zeros_like(dq_ref)
            if bias:
                dcq_ref[...] = jnp.zeros_like(dcq_ref)

        if rope:
            @pl.when((b == 0) & (j == 0) & (i == 0))
            def _():
                dqr_ref[...] = jnp.zeros_like(dqr_ref)
                dkr_ref[...] = jnp.zeros_like(dkr_ref)

        @pl.when(i == 0)
        def _():
            dk_s[...] = jnp.zeros_like(dk_s)
            dv_s[...] = jnp.zeros_like(dv_s)
            if bias:
                dck_s[...] = jnp.zeros_like(dck_s)

        def compute(masked):
            q2, k2, v2, do2 = q_ref[...], k_ref[...], v_ref[...], do_ref[...]
            dd = do2.astype(F32) * o_ref[...].astype(F32)
            lse2 = lse_ref[...]
            lane = lax.broadcasted_iota(jnp.int32, (bq, LANES), 1)
            rq = pl.ds(pl.multiple_of(i * bq, bq), bq)
            rk = pl.ds(pl.multiple_of(j * bk, bk), bk)
            add = lambda tot, x: x if tot is None else tot + x
            dv_t = dk_t = dq_t = dqr_t = dkr_t = dcq_t = None
            for hh in range(g):
                qm = _sel(heads[hh], q2)
                sc = _dot(qm, k2, NT)
                if rope:
                    qrm = _sel(ropes[hh], qr_ref[...])
                    sc = sc + _dot(qrm, kr_ref[...], NT)
                sc = sc * scale
                if bias:
                    sc = sc - ck_ref[0, hh:hh + 1, :]
                if masked:
                    sc = jnp.where(_mask(mode, i, j, bq, bk), sc, MASK_VALUE)
                p = jnp.exp(sc - jnp.sum(jnp.where(lane == hh, lse2, 0.0), axis=1, keepdims=True))
                dom = _sel(heads[hh], do2)
                dp = _dot(dom, v2, NT)
                delta = jnp.sum(_sel(heads[hh], dd), axis=1, keepdims=True)
                ds = p * (dp - delta)
                dsb = ds.astype(BF16)
                dv_t = add(dv_t, _dot(p.astype(BF16), dom, TN))
                dk_t = add(dk_t, _dot(dsb, qm, TN))
                dq_t = add(dq_t, _dot(dsb, _sel(heads[hh], k2)))
                if rope:
                    dqr_t = add(dqr_t, _dot(dsb, _sel(ropes[hh], kr_ref[...])))
                    dkr_t = add(dkr_t, _dot(dsb, qrm, TN))
                if bias:
                    dck_s[hh:hh + 1, :] -= jnp.sum(ds, axis=0, keepdims=True)
                    dcq_t = add(dcq_t, jnp.where(lane == hh, jnp.sum(ds, axis=1, keepdims=True), 0.0))
            dv_s[...] += dv_t
            dk_s[...] += scale * dk_t
            dq_ref[rq, :] += scale * dq_t
            if rope:
                dqr_ref[rq, :] += scale * dqr_t
                dkr_ref[rk, :] += scale * dkr_t
            if bias:
                dcq_ref[rq, :] += dcq_t

        if tri:
            pl.when(i > j)(functools.partial(compute, False))
            pl.when(i == j)(functools.partial(compute, True))
        else:
            compute(False)

        @pl.when(i == nq - 1)
        def _():
            dk_ref[...] = dk_s[...]
            dv_ref[...] = dv_s[...]
            if bias:
                dck_ref[0] = dck_s[...]

    ii = (lambda j, i: jnp.maximum(i, j)) if tri else (lambda j, i: i)
    qrow = lambda col: pl.BlockSpec((bq, LANES), lambda b, j, i: (ii(j, i), col(b)))
    krow = lambda col: pl.BlockSpec((bk, LANES), lambda b, j, i: (j, col(b)))
    in_specs = [qrow(lambda b: qc + b), krow(lambda b: kc + b), krow(lambda b: vc + b), qrow(lambda b: b),
                qrow(lambda b: b), qrow(lambda b: b)]
    args = [q, k, v, o, do, lse]
    whole = lambda rows: pl.BlockSpec((rows, LANES), lambda b, j, i: (0, b))
    out_specs = [whole(s), krow(lambda b: b), krow(lambda b: b)]
    out_shape = [jax.ShapeDtypeStruct((s, LANES * nb), F32), jax.ShapeDtypeStruct((t, LANES * nb), F32),
                 jax.ShapeDtypeStruct((t, LANES * nb), F32)]
    scratch = [pltpu.VMEM((bk, LANES), F32), pltpu.VMEM((bk, LANES), F32)]
    if bias:
        in_specs.append(pl.BlockSpec((1, 8, bk), lambda b, j, i: (b, 0, j)))
        args.append(ck)
        out_specs += [pl.BlockSpec((1, 8, bk), lambda b, j, i: (b, 0, j)), whole(s)]
        out_shape += [jax.ShapeDtypeStruct((nb, 8, t), F32), jax.ShapeDtypeStruct((s, LANES * nb), F32)]
    if rope:
        in_specs += [qrow(lambda b: qrc), krow(lambda b: 0)]
        args += [qr, kr]
        out_specs += [pl.BlockSpec((s, LANES), lambda b, j, i: (0, 0)), pl.BlockSpec((t, LANES), lambda b, j, i: (0, 0))]
        out_shape += [jax.ShapeDtypeStruct((s, LANES), F32), jax.ShapeDtypeStruct((t, LANES), F32)]
    if bias:
        scratch.append(pltpu.VMEM((8, bk), F32))
    return pl.pallas_call(body, name=name, grid=(nb, nk, nq), in_specs=in_specs, out_specs=out_specs,
                          out_shape=out_shape, scratch_shapes=scratch,
                          compiler_params=_params('arbitrary', 'arbitrary', 'arbitrary'))(*args)


def _gla_chunk(la_c, k_c):
    r = lax.broadcasted_iota(jnp.int32, (CHUNK, CHUNK), 0)
    c = lax.broadcasted_iota(jnp.int32, (CHUNK, CHUNK), 1)
    tri = jnp.where(c <= r, 1.0, 0.0).astype(BF16)
    cum = _tri_dot(tri, la_c)
    end = jnp.sum(la_c, axis=0, keepdims=True)
    dec = jnp.exp(end - cum)
    return dec, k_c * dec, jnp.exp(end)


GLA_PAIRS = GLA_HEADS // 2


def _gla_fwd(z, la, *, qc, kc, vc, name, blk=512):
    s = z.shape[0]
    bs = min(blk, s)
    ncb = bs // CHUNK
    nblk = s // bs

    def body(q_ref, k_ref, va_ref, vb_ref, la_ref, o_ref, st_ref, st):
        @pl.when(pl.program_id(1) == 0)
        def _():
            st[...] = jnp.zeros_like(st)

        heads, _ = _lane_masks(2, 0, False)
        v_refs = (va_ref, vb_ref)
        for c in range(ncb):
            sl = pl.ds(c * CHUNK, CHUNK)
            _, kf, a = _gla_chunk(la_ref[sl, :], k_ref[sl, :])
            qs = q_ref[sl, :] * (GLA_DK ** -0.5)
            for hh in range(2):
                ut = _dot(v_refs[hh][sl, :].astype(BF16), _sel(heads[hh], kf).astype(BF16), TN)
                new = a * st[hh] + ut
                st[hh] = new
                st_ref[0, c, hh] = new
                o_ref[sl, hh * GLA_DV:(hh + 1) * GLA_DV] = _dot(_sel(heads[hh], qs).astype(BF16), new.astype(BF16), NT)

    col = lambda c0, m=1: pl.BlockSpec((bs, LANES), lambda b, i: (i, c0 + m * b))
    return pl.pallas_call(
        body, name=name, grid=(GLA_PAIRS, nblk),
        in_specs=[col(qc), col(kc), col(vc, 2), col(vc + 1, 2), col(0)],
        out_specs=[pl.BlockSpec((bs, 2 * GLA_DV), lambda b, i: (i, b)),
                   pl.BlockSpec((1, ncb, 2, GLA_DV, LANES), lambda b, i: (b, i, 0, 0, 0))],
        out_shape=[jax.ShapeDtypeStruct((s, GLA_HEADS * GLA_DV), F32),
                   jax.ShapeDtypeStruct((GLA_PAIRS, s // CHUNK, 2, GLA_DV, LANES), F32)],
        scratch_shapes=[pltpu.VMEM((2, GLA_DV, LANES), F32)],
        compiler_params=_params('arbitrary', 'arbitrary'))(z, z, z, z, la)


def _gla_bwd(z, la, st_all, st_prev, do, *, qc, kc, vc, name, blk=512):
    s = z.shape[0]
    bs = min(blk, s)
    ncb = bs // CHUNK
    nblk = s // bs

    def body(q_ref, k_ref, va_ref, vb_ref, la_ref, st_ref, sp_ref, do_ref, dq_ref, dk_ref, dv_ref, dla_ref, ga):
        @pl.when(pl.program_id(1) == 0)
        def _():
            ga[...] = jnp.zeros_like(ga)

        r = lax.broadcasted_iota(jnp.int32, (CHUNK, CHUNK), 0)
        cc = lax.broadcasted_iota(jnp.int32, (CHUNK, CHUNK), 1)
        tri_rev = jnp.where(cc >= r, 1.0, 0.0).astype(BF16)
        heads, _ = _lane_masks(2, 0, False)
        v_refs = (va_ref, vb_ref)
        for c in reversed(range(ncb)):
            sl = pl.ds(c * CHUNK, CHUNK)
            dec, kf, a = _gla_chunk(la_ref[sl, :], k_ref[sl, :])
            qs = q_ref[sl, :] * (GLA_DK ** -0.5)
            dq2 = jnp.zeros((CHUNK, LANES), F32)
            dkd = jnp.zeros((CHUNK, LANES), F32)
            da = jnp.zeros((1, LANES), F32)
            for hh in range(2):
                hv = slice(hh * GLA_DV, (hh + 1) * GLA_DV)
                dob = do_ref[sl, hv].astype(BF16)
                g = _dot(dob, _sel(heads[hh], qs).astype(BF16), TN) + ga[hh]
                gb = g.astype(BF16)
                dq2 = dq2 + _dot(dob, st_ref[0, c, hh].astype(BF16))
                dv_ref[sl, hv] = _dot(_sel(heads[hh], kf).astype(BF16), gb, NT)
                dkd = dkd + _dot(v_refs[hh][sl, :].astype(BF16), gb)
                da = da + jnp.sum(g * sp_ref[0, c, hh], axis=0, keepdims=True)
                ga[hh] = a * g
            dq_ref[sl, :] = (GLA_DK ** -0.5) * dq2
            dk_ref[sl, :] = dkd * dec
            e = dkd * kf
            dend = jnp.sum(e, axis=0, keepdims=True) + da * a
            dla_ref[sl, :] = dend - _tri_dot(tri_rev, e)

    rev = lambda i: nblk - 1 - i
    col = lambda c0, m=1: pl.BlockSpec((bs, LANES), lambda b, i: (rev(i), c0 + m * b))
    wide = pl.BlockSpec((bs, 2 * GLA_DV), lambda b, i: (rev(i), b))
    stspec = pl.BlockSpec((1, ncb, 2, GLA_DV, LANES), lambda b, i: (b, rev(i), 0, 0, 0))
    return pl.pallas_call(
        body, name=name, grid=(GLA_PAIRS, nblk),
        in_specs=[col(qc), col(kc), col(vc, 2), col(vc + 1, 2), col(0), stspec, stspec, wide],
        out_specs=[col(0), col(0), wide, col(0)],
        out_shape=[jax.ShapeDtypeStruct((s, GLA_HEADS * GLA_DK), F32), jax.ShapeDtypeStruct((s, GLA_HEADS * GLA_DK), F32),
                   jax.ShapeDtypeStruct((s, GLA_HEADS * GLA_DV), F32), jax.ShapeDtypeStruct((s, GLA_HEADS * GLA_DK), F32)],
        scratch_shapes=[pltpu.VMEM((2, GLA_DV, LANES), F32)],
        compiler_params=_params('arbitrary', 'arbitrary'))(z, z, z, z, la, st_all, st_prev, do)


def _place():
    return lax.axis_index('x'), lax.axis_index('y'), lax.axis_index('c')


ANY = pl.BlockSpec(memory_space=pl.ANY)


def _all_gather8(blk, *, name):
    m, n = blk.shape

    def body(x_ref, out_ref, send_sems, recv_sems, local_sem):
        x, y, c = _place()
        me, sibling = (x, y, c), (x, y, 1 - c)
        chips = [(1 - x, y), (x, 1 - y), (1 - x, 1 - y)]

        def slot(px, py, pc):
            return out_ref.at[4 * px + 2 * py + pc]

        def copy(q, block, to, src=None):
            return pltpu.make_async_remote_copy(
                src_ref=slot(*block) if src is None else src, dst_ref=slot(*block), send_sem=send_sems.at[q],
                recv_sem=recv_sems.at[q], device_id=to, device_id_type=MESH)

        mine = pltpu.make_async_copy(x_ref, slot(*me), local_sem)
        mine.start()
        first = [copy(0, me, sibling, src=x_ref)]
        first += [copy(1 + q, me, (*chip, c), src=x_ref) for q, chip in enumerate(chips)]
        for cp in first:
            cp.start()
        passed = [copy(4 + q, (*chip, c), sibling) for q, chip in enumerate(chips)]
        for q, chip in enumerate(chips):
            copy(1 + q, (*chip, c), me).wait_recv()
            passed[q].start()
        copy(0, sibling, me).wait_recv()
        for q, chip in enumerate(chips):
            copy(4 + q, (*chip, 1 - c), me).wait_recv()
        for cp in first + passed:
            cp.wait_send()
        mine.wait()

    return pl.pallas_call(
        body, name=name, in_specs=[ANY], out_specs=ANY, out_shape=jax.ShapeDtypeStruct((N_DEV, m, n), blk.dtype),
        scratch_shapes=[pltpu.SemaphoreType.DMA((7,)), pltpu.SemaphoreType.DMA((7,)), pltpu.SemaphoreType.DMA(())],
    )(blk)


def _sems(*counts):
    return [pltpu.SemaphoreType.DMA((n,)) for n in counts]


def _gather_shards(ws, *, name):
    n = len(ws)

    def body(*refs):
        ins, outs = refs[:n], refs[n:2 * n]
        send_sems, recv_sems = refs[2 * n:]
        x, y, c = _place()
        me_chip = 2 * x + y
        sibling = (x, y, 1 - c)
        chips = [(1 - x, y), (x, 1 - y), (1 - x, 1 - y)]

        def copy(q, k, src, dst, to):
            return pltpu.make_async_remote_copy(src_ref=src, dst_ref=dst, send_sem=send_sems.at[7 * q + k],
                                                recv_sem=recv_sems.at[7 * q + k], device_id=to, device_id_type=MESH)

        sent = []
        for q in range(n):
            for k, (px, py) in enumerate(chips):
                sent.append(copy(q, k, ins[q].at[c], outs[q].at[me_chip, c], (px, py, c)))
                sent[-1].start()
            sent.append(copy(q, 6, ins[q], outs[q].at[me_chip], sibling))
            sent[-1].start()
        for q in range(n):
            for k, (px, py) in enumerate(chips):
                slot = outs[q].at[2 * px + py, c]
                copy(q, k, slot, slot, (px, py, c)).wait_recv()
                sent.append(copy(q, 3 + k, slot, slot, sibling))
                sent[-1].start()
        for q in range(n):
            for k, (px, py) in enumerate(chips):
                slot = outs[q].at[2 * px + py, 1 - c]
                copy(q, 3 + k, slot, slot, sibling).wait_recv()
            copy(q, 6, ins[q], outs[q].at[me_chip], sibling).wait_recv()
        for cp in sent:
            cp.wait_send()

    return pl.pallas_call(
        body, name=name, in_specs=[ANY] * n, out_specs=[ANY] * n,
        out_shape=[jax.ShapeDtypeStruct((N_CHIPS,) + w.shape, w.dtype) for w in ws],
        scratch_shapes=_sems(7 * n, 7 * n))(*ws)


def _to_sibling(gs, *, name):
    n = len(gs)

    def body(*refs):
        ins, outs = refs[:n], refs[n:2 * n]
        send_sems, recv_sems = refs[2 * n:]
        x, y, c = _place()
        cps = [pltpu.make_async_remote_copy(
            src_ref=ins[q], dst_ref=outs[q], send_sem=send_sems.at[q], recv_sem=recv_sems.at[q],
            device_id=(x, y, 1 - c), device_id_type=MESH) for q in range(n)]
        for cp in cps:
            cp.start()
        for cp in cps:
            cp.wait()

    return pl.pallas_call(body, name=name, in_specs=[ANY] * n, out_specs=[ANY] * n,
                          out_shape=[jax.ShapeDtypeStruct(g.shape, g.dtype) for g in gs],
                          scratch_shapes=_sems(n, n))(*gs)


def _chip_exchange(ps, *, name):
    n = len(ps)

    def body(*refs):
        ins, outs = refs[:n], refs[n:2 * n]
        send_sems, recv_sems = refs[2 * n:]
        x, y, c = _place()
        chips = [(1 - x, y), (x, 1 - y), (1 - x, 1 - y)]
        cps = []
        for q in range(n):
            for k, (px, py) in enumerate(chips):
                cps.append(pltpu.make_async_remote_copy(
                    src_ref=ins[q].at[2 * px + py], dst_ref=outs[q].at[k], send_sem=send_sems.at[3 * q + k],
                    recv_sem=recv_sems.at[3 * q + k], device_id=(px, py, c), device_id_type=MESH))
                cps[-1].start()
        for cp in cps:
            cp.wait()

    return pl.pallas_call(body, name=name, in_specs=[ANY] * n, out_specs=[ANY] * n,
                          out_shape=[jax.ShapeDtypeStruct((3,) + p.shape[1:], p.dtype) for p in ps],
                          scratch_shapes=_sems(3 * n, 3 * n))(*ps)


def _sum_chips(own, r, *, name, ts=256):
    k, n = own.shape
    ts = min(ts, k)

    def body(own_ref, r_ref, o_ref):
        f = lambda q: r_ref[q].astype(F32)
        o_ref[...] = ((own_ref[...].astype(F32) + f(0)) + f(1)) + f(2)

    return pl.pallas_call(
        body, name=name, grid=(k // ts,),
        in_specs=[pl.BlockSpec((ts, n), lambda i: (i, 0)), pl.BlockSpec((3, ts, n), lambda i: (0, i, 0))],
        out_specs=pl.BlockSpec((ts, n), lambda i: (i, 0)), out_shape=jax.ShapeDtypeStruct((k, n), F32),
        compiler_params=_params('arbitrary'))(own, r)


WIN_SHARD = N_IN // N_CHIPS
WIN_PAD = -(-WIN_SHARD // LANES) * LANES


def _full_layer(gathered, axis, l):
    sh = gathered[:, l]
    _, k, n = sh.shape
    if axis == 2:
        return sh.transpose(1, 0, 2).reshape(k, N_CHIPS * n)
    return sh.reshape(N_CHIPS * k, n)


def _win_cols(wp, o, n):
    parts = []
    while n > 0:
        j, r = divmod(o, WIN_SHARD)
        take = min(n, WIN_SHARD - r)
        parts.append(wp[:, j * WIN_PAD + r:j * WIN_PAD + r + take])
        o, n = o + take, n - take
    return parts[0] if len(parts) == 1 else jnp.concatenate(parts, axis=1)


def _split_full(full, axis):
    l, k, n = full.shape
    if axis == 2:
        return jnp.stack([full[:, :, j * (n // N_CHIPS):(j + 1) * (n // N_CHIPS)] for j in range(N_CHIPS)])
    return full.reshape(l, N_CHIPS, k // N_CHIPS, n).transpose(1, 0, 2, 3)


def _padc(a, w):
    return jnp.pad(a, ((0, 0), (0, w - a.shape[1])))


def _swap16(a):
    return jnp.concatenate([a[..., 16:32], a[..., 0:16]], axis=-1)


B_GR, B_GQ, B_GK, B_GV, B_MQ, B_MKR, B_MKRS, B_FF, B_GLOW, B_MKV, B_END = (
    0, 512, 768, 1024, 1536, 1792, 1920, 2048, 2176, 2304, 2432)
B_W = 2560
O_FQ, O_FF, O_GQ, O_GLOW, O_GR, O_MQ, O_MKV, O_MKR, O_ZG = 0, 768, 772, 1796, 1812, 2324, 2580, 2708, 2740


def _repack_layer_weights(w):
    wi = functools.partial(_win_cols, w['w_in'])
    out = dict(w)
    out['in_a'] = wi(O_FQ, 768)
    kr = wi(O_MKR, 32)
    out['in_b'] = jnp.concatenate([
        wi(O_GR, 512), wi(O_GQ, 1024), wi(O_MQ, 256), jnp.tile(kr, (1, MLA_HEADS)), jnp.tile(_swap16(kr), (1, MLA_HEADS)),
        _padc(wi(O_FF, 4), 128), _padc(wi(O_GLOW, 16), 128), wi(O_MKV, 128),
        jnp.zeros((D_MODEL, B_W - B_END), kr.dtype)], axis=1)
    out['in_c'] = wi(O_ZG, 3072)
    uq = w['w_mla_uq'].reshape(MLA_Q_RANK, MLA_HEADS, MLA_NOPE + MLA_ROPE)
    rope = uq[:, :, MLA_NOPE:]
    out['uq'] = jnp.concatenate([uq[:, :, :MLA_NOPE].reshape(MLA_Q_RANK, -1), rope.reshape(MLA_Q_RANK, -1),
                                 _swap16(rope).reshape(MLA_Q_RANK, -1)], axis=1)
    ukv = w['w_mla_ukv'].reshape(MLA_KV_RANK, MLA_HEADS, MLA_NOPE + MLA_VD)
    out['ukv'] = jnp.concatenate([ukv[:, :, :MLA_NOPE].reshape(MLA_KV_RANK, -1),
                                  ukv[:, :, MLA_NOPE:].reshape(MLA_KV_RANK, -1)], axis=1)
    out['gate'] = jnp.pad(w['w_gla_gate'], ((0, 128 - GLA_RANK), (0, 0)))
    return out


def _unpack_layer_grads(g):
    a, b, c = g['in_a'], g['in_b'], g['in_c']
    fold = lambda o: sum(b[:, o + MLA_ROPE * q:o + MLA_ROPE * (q + 1)] for q in range(MLA_HEADS))
    kr = fold(B_MKR) + _swap16(fold(B_MKRS))
    w_in = jnp.concatenate([a, b[:, B_FF:B_FF + 4], b[:, B_GQ:B_GQ + 1024], b[:, B_GLOW:B_GLOW + 16],
                            b[:, B_GR:B_GR + 512], b[:, B_MQ:B_MQ + 256], b[:, B_MKV:B_MKV + 128], kr, c], axis=1)
    uq = g['uq']
    nope = uq[:, :256].reshape(MLA_Q_RANK, MLA_HEADS, MLA_NOPE)
    rope = (uq[:, 256:384].reshape(MLA_Q_RANK, MLA_HEADS, MLA_ROPE)
            + _swap16(uq[:, 384:512].reshape(MLA_Q_RANK, MLA_HEADS, MLA_ROPE)))
    w_uq = jnp.concatenate([nope, rope], axis=2).reshape(MLA_Q_RANK, -1)
    ukv = g['ukv']
    w_ukv = jnp.concatenate([ukv[:, :256].reshape(MLA_KV_RANK, MLA_HEADS, MLA_NOPE),
                             ukv[:, 256:].reshape(MLA_KV_RANK, MLA_HEADS, MLA_VD)], axis=2).reshape(MLA_KV_RANK, -1)
    out = {'w_in': w_in, 'w_mla_uq': w_uq, 'w_mla_ukv': w_ukv, 'w_gla_gate': g['gate'][:GLA_RANK]}
    for nm in ('w_up_fox', 'w_up_gla', 'w_up_mla', 'w_out', 'w_xq', 'w_xkv', 'w_xo', 'w_mlp1', 'w_mlp2'):
        out[nm] = g[nm]
    return out


def _rope_tables(s):
    half = MLA_ROPE // 2
    inv = ROPE_BASE ** (-jnp.arange(half, dtype=F32) / half)
    ang = jnp.arange(s).astype(F32)[:, None] * inv[None, :]
    cos, sin = jnp.cos(ang), jnp.sin(ang)
    c1 = jnp.concatenate([cos, cos], axis=1)
    s1 = jnp.concatenate([-sin, sin], axis=1)
    return jnp.tile(c1, (1, MLA_HEADS)), jnp.tile(s1, (1, MLA_HEADS))


def _rms_bwd(x, dh, g):
    r = lax.rsqrt(jnp.mean(x * x, axis=-1, keepdims=True) + EPS)
    xh = x * r
    gd = dh * g
    return r * (gd - xh * jnp.mean(gd * xh, axis=-1, keepdims=True)), dh * xh


def _norm_bwd_call(x, dh, g, dres, name):
    w = x.width if isinstance(x, Cols) else x.shape[1]

    def with_res(xv, dv, rv, gv):
        dx, dg = _rms_bwd(xv, dv.astype(F32), gv)
        return rv + dx, dg

    def plain(xv, dv, gv):
        return _rms_bwd(xv, dv.astype(F32), gv)

    if dres is None:
        return _rowwise(plain, [x, dh], [g], [(w, F32)], [w], name=name)
    return _rowwise(with_res, [x, dh, dres], [g], [(w, F32)], [w], name=name)


def _gla_out_fwd(oraw, gr, g_out):
    outs = []
    for hh in range(GLA_HEADS):
        sl = slice(hh * GLA_DV, (hh + 1) * GLA_DV)
        oh = oraw[:, sl]
        n = oh * lax.rsqrt(jnp.mean(oh * oh, axis=-1, keepdims=True) + EPS) * g_out
        r = gr[:, sl]
        outs.append(n * (r * _sig(r)))
    return (jnp.concatenate(outs, axis=1),)


def _gla_out_bwd(oraw, gr, dout, g_out):
    d_o, d_r, dg = [], [], 0.0
    for hh in range(GLA_HEADS):
        sl = slice(hh * GLA_DV, (hh + 1) * GLA_DV)
        oh, r, do = oraw[:, sl], gr[:, sl], dout[:, sl].astype(F32)
        rs = lax.rsqrt(jnp.mean(oh * oh, axis=-1, keepdims=True) + EPS)
        sg = _sig(r)
        dn = do * (r * sg)
        d_r.append(do * (oh * rs * g_out) * (sg + r * sg * (1.0 - sg)))
        dx, dgh = _rms_bwd(oh, dn, g_out)
        d_o.append(dx)
        dg = dg + dgh
    return jnp.concatenate(d_o, axis=1), jnp.concatenate(d_r, axis=1), dg


def _adam(w, g, m, v):
    m = ADAM_B1 * m + (1.0 - ADAM_B1) * g
    v = ADAM_B2 * v + (1.0 - ADAM_B2) * (g * g)
    m_hat = m / (1.0 - ADAM_B1 ** ADAM_STEP)
    v_hat = v / (1.0 - ADAM_B2 ** ADAM_STEP)
    return -ADAM_LR * (m_hat / (jnp.sqrt(v_hat) + ADAM_EPS) + ADAM_WD * w), m, v


def _layer_fwd(x, mem, w, p, tabs, tag):
    c4, s4 = tabs
    sv = {'x0': x}
    nm = lambda t: f'{t}_{tag}'
    za, h = _mm(x, w['in_a'], mode='nn', out_dtype=BF16, norm_g=p['g_mix'], emit_norm=True, name=nm('in_a'))
    zb = _mm(h, w['in_b'], mode='nn', out_dtype=F32, name=nm('in_b'))
    zc = _mm(h, w['in_c'], mode='nn', out_dtype=F32, name=nm('in_c'))
    sv.update(h=h, zc=zc)
    ff = Cols(zb, 128, B_FF // 128)
    (lf,) = _rowwise(lambda f, b: (_logsig(f + b),), [ff], [p['b_fox']], [(128, F32)], name=nm('fox_lf'))
    cum = _cumsum_rows(lf, reverse=False, name=nm('fox_cum'))
    ckf = jnp.pad(cum[:, :FOX_HEADS].T.reshape(2, 2, x.shape[0]), ((0, 0), (0, 6), (0, 0)))
    fox = dict(qc=0, kc=2, vc=4, nb=2, g=2, scale=FOX_HD ** -0.5, mode='causal', ck=ckf)
    o_fox, lse_fox = _mattn_fwd(za, za, za, name=nm('fox_attn'), **fox)
    sv.update(ff=ff, za=za, fox=fox, o_fox=o_fox, lse_fox=lse_fox)
    glow = Cols(zb, 128, B_GLOW // 128)
    gr = Cols(zb, 512, B_GR // 512)

    def gate_fn(gl, wg, bg):
        return (_logsig(_dot(gl.astype(BF16), wg) + bg) / GLA_TAU,)

    (la,) = _rowwise(gate_fn, [glow], [w['gate'], p['b_gla']], [(256, F32)], name=nm('gla_gate'))
    gla = dict(qc=B_GQ // LANES, kc=B_GK // LANES, vc=B_GV // LANES)
    oraw, states = _gla_fwd(zb, la, name=nm('gla'), **gla)
    (o_gla,) = _rowwise(_gla_out_fwd, [oraw, gr], [p['g_gla_out']], [(512, BF16)], name=nm('gla_out'))
    sv.update(glow=glow, gr=gr, zb=zb, la=la, gla=gla, states=states, oraw=oraw, o_gla=o_gla)
    mq = Cols(zb, 256, B_MQ // 256)
    mkv = Cols(zb, 128, B_MKV // 128)
    mkr2 = Cols(zb, 256, B_MKR // 256)
    qp, cqn = _mm(mq, w['uq'], mode='nn', out_dtype=F32, norm_g=p['g_mla_q'], emit_norm=True, name=nm('mla_uq'))
    kvp, ckvn = _mm(mkv, w['ukv'], mode='nn', out_dtype=BF16, norm_g=p['g_mla_kv'], emit_norm=True,
                    name=nm('mla_ukv'))

    def rope_fn(qv, kr, c4v, s4v):
        q_rope = qv[:, 256:384] * c4v + qv[:, 384:512] * s4v
        return jnp.concatenate([qv[:, 0:256], q_rope], axis=1), kr[:, 0:128] * c4v + kr[:, 128:256] * s4v

    qall, kr4 = _rowwise(rope_fn, [qp, mkr2, c4, s4], [], [(384, BF16), (128, BF16)], name=nm('rope'))
    mla = dict(qc=0, kc=0, vc=2, nb=2, g=2, scale=(MLA_NOPE + MLA_ROPE) ** -0.5, mode='chunk', qr=qall, qrc=2, kr=kr4)
    o_mla, lse_mla = _mattn_fwd(qall, kvp, kvp, name=nm('mla_attn'), **mla)
    sv.update(mq=mq, mkv=mkv, cqn=cqn, ckvn=ckvn, qall=qall, kvp=kvp, mla=mla, o_mla=o_mla, lse_mla=lse_mla)
    of_m, om_m = o_fox, o_mla
    sv.update(of_m=of_m, om_m=om_m)
    b_br = p['b_branch']

    def first(acc, zg, bb):
        return _sig(zg + bb) * acc

    def more(acc, zg, bb, prev):
        return prev + _sig(zg + bb) * acc

    y = _mm(of_m, w['w_up_fox'], mode='nn', out_dtype=F32, name=nm('up_fox'), epilogue=first,
            extras=[(zc, *_mn(col_off=0)), (b_br, *_nvec(col_off=0))])
    y = _mm(o_gla, w['w_up_gla'], mode='nn', out_dtype=F32, name=nm('up_gla'), epilogue=more,
            extras=[(zc, *_mn(col_off=1024)), (b_br, *_nvec(col_off=1024)), (y, *_mn())])
    y = _mm(om_m, w['w_up_mla'], mode='nn', out_dtype=BF16, name=nm('up_mla'), epilogue=more,
            extras=[(zc, *_mn(col_off=2048)), (b_br, *_nvec(col_off=2048)), (y, *_mn())])
    add = lambda acc, res: res + acc
    x1 = _mm(y, w['w_out'], mode='nn', out_dtype=F32, name=nm('out'), epilogue=add, extras=[(x, *_mn())])
    sv.update(y=y, x1=x1)
    qx, hx = _mm(x1, w['w_xq'], mode='nn', out_dtype=BF16, norm_g=p['g_xa'], emit_norm=True, name=nm('xq'))
    kvx, mn = _mm(mem, w['w_xkv'], mode='nn', out_dtype=BF16, norm_g=p['g_mem'], emit_norm=True, name=nm('xkv'))
    xa = dict(qc=0, kc=0, vc=4, nb=4, g=1, scale=XA_HD ** -0.5, mode='full')
    ox_m, lse_x = _mattn_fwd(qx, kvx, kvx, name=nm('xa_attn'), **xa)
    x2 = _mm(ox_m, w['w_xo'], mode='nn', out_dtype=F32, name=nm('xo'), epilogue=add, extras=[(x1, *_mn())])
    sv.update(hx=hx, mn=mn, qx=qx, kvx=kvx, xa=xa, lse_x=lse_x, ox_m=ox_m, x2=x2)
    hpre, hm = _mm(x2, w['w_mlp1'], mode='nn', out_dtype=BF16, norm_g=p['g_mlp'], emit_norm=True, name=nm('mlp1'))
    relu2 = lambda t: jnp.square(jnp.maximum(t.astype(F32), 0.0))
    x3 = _mm(hpre, w['w_mlp2'], mode='nn', out_dtype=F32, name=nm('mlp2'), a_fn=relu2, epilogue=add,
             extras=[(x2, *_mn())])
    sv.update(hpre=hpre, hm=hm)
    return x3, sv


def _layer_bwd(dx3, mem, w, p, tabs, sv, tag):
    c4, s4 = tabs
    nm = lambda t: f'{t}_{tag}'
    s = dx3.shape[0]
    gw, gs = {}, {}
    relu2 = lambda t: jnp.square(jnp.maximum(t.astype(F32), 0.0))
    gw['w_mlp2'] = _mm(sv['hpre'], dx3, mode='tn', out_dtype=F32, name=nm('d_mlp2'), a_fn=relu2)
    dact = lambda acc, hp: acc * (2.0 * jnp.maximum(hp.astype(F32), 0.0))
    dhpre = _mm(dx3, w['w_mlp2'], mode='nt', out_dtype=BF16, name=nm('d_act'), epilogue=dact,
                extras=[(sv['hpre'], *_mn())])
    gw['w_mlp1'] = _mm(sv['hm'], dhpre, mode='tn', out_dtype=F32, name=nm('d_mlp1'))
    dhm = _mm(dhpre, w['w_mlp1'], mode='nt', out_dtype=F32, name=nm('d_hm'))
    dx2, gs['g_mlp'] = _norm_bwd_call(sv['x2'], dhm, p['g_mlp'], dx3, nm('d_norm_mlp'))
    gw['w_xo'] = _mm(sv['ox_m'], dx2, mode='tn', out_dtype=F32, name=nm('d_xo'))
    dox = _mm(dx2, w['w_xo'], mode='nt', out_dtype=BF16, name=nm('d_ox'))
    dqx_m, dkx, dvx = _mattn_bwd(sv['qx'], sv['kvx'], sv['kvx'], sv['ox_m'], dox, sv['lse_x'], name=nm('xa_bwd'),
                                 **sv['xa'])
    dkvx = jnp.concatenate([dkx, dvx], axis=1).astype(BF16)
    gw['w_xq'] = _mm(sv['hx'], dqx_m, mode='tn', out_dtype=F32, name=nm('d_xq'))
    dhx = _mm(dqx_m, w['w_xq'], mode='nt', out_dtype=F32, name=nm('d_hx'))
    gw['w_xkv'] = _mm(sv['mn'], dkvx, mode='tn', out_dtype=F32, name=nm('d_xkv'))
    dmn = _mm(dkvx, w['w_xkv'], mode='nt', out_dtype=F32, name=nm('d_mn'))
    _, gs['g_mem'] = _norm_bwd_call(mem, dmn, p['g_mem'], None, nm('d_norm_mem'))
    dx1, gs['g_xa'] = _norm_bwd_call(sv['x1'], dhx, p['g_xa'], dx2, nm('d_norm_xa'))
    gw['w_out'] = _mm(sv['y'], dx1, mode='tn', out_dtype=F32, name=nm('d_out'))
    dy = _mm(dx1, w['w_out'], mode='nt', out_dtype=BF16, name=nm('d_y'))
    zc, b_br = sv['zc'], p['b_branch']

    def du_fn(dyv, zg, bb):
        g = _sig(zg + bb)
        d = dyv.astype(F32)
        return d * g[:, 0:1024], d * g[:, 1024:2048], d * g[:, 2048:3072]

    du = _rowwise(du_fn, [dy, zc], [b_br], [(D_MODEL, BF16)] * 3, name=nm('d_u'))

    def dgate(acc, dyv, zg, bb):
        g = _sig(zg + bb)
        return dyv.astype(F32) * acc * g * (1.0 - g)

    dzc, do_br = [], []
    for q, (o_m, wn) in enumerate(((sv['of_m'], 'w_up_fox'), (sv['o_gla'], 'w_up_gla'), (sv['om_m'], 'w_up_mla'))):
        dzc.append(_mm(o_m, w[wn], mode='nn', out_dtype=F32, name=nm(f'd_zg{q}'), epilogue=dgate,
                       extras=[(dy, *_mn()), (zc, *_mn(col_off=1024 * q)), (b_br, *_nvec(col_off=1024 * q))]))
        gw[wn] = _mm(o_m, du[q], mode='tn', out_dtype=F32, name=nm(f'd_up{q}'))
        do_br.append(_mm(du[q], w[wn], mode='nt', out_dtype=F32 if q == 1 else BF16, name=nm(f'd_o{q}')))
    dzc = jnp.concatenate(dzc, axis=1)
    (gs['b_branch'],) = _rowwise(lambda t: (t,), [dzc], [], [], [3072], name=nm('d_bbranch'))
    za = sv['za']
    dfq, dfk, dfv, dck, dcq = _mattn_bwd(za, za, za, sv['o_fox'], do_br[0], sv['lse_fox'], name=nm('fox_bwd'),
                                         **sv['fox'])
    dcum = _padc(dck[:, :2, :].reshape(FOX_HEADS, s).T + dcq.reshape(s, 2, LANES)[:, :, :2].reshape(s, FOX_HEADS), 128)
    dlf = _cumsum_rows(dcum, reverse=True, name=nm('fox_dcum'))

    def dff_fn(dl, f, b):
        d = dl * _sig(-(f + b))
        return d, d

    dff, db_fox = _rowwise(dff_fn, [dlf, sv['ff']], [p['b_fox']], [(128, F32)], [128], name=nm('fox_dff'))
    gs['b_fox'] = db_fox
    dza = jnp.concatenate([dfq, dfk, dfv], axis=1).astype(BF16)
    dqn, dkn, dvv, dq_rope, dk_rope = _mattn_bwd(sv['qall'], sv['kvp'], sv['kvp'], sv['o_mla'], do_br[2],
                                                 sv['lse_mla'], name=nm('mla_bwd'), **sv['mla'])

    def drope_fn(dn, dq, dk, c4v, s4v):
        return jnp.concatenate([dn, dq * c4v, dq * s4v], axis=1), jnp.concatenate([dk * c4v, dk * s4v], axis=1)

    dqp, dmkr2 = _rowwise(drope_fn, [dqn, dq_rope, dk_rope, c4, s4], [], [(512, BF16), (256, BF16)], name=nm('d_rope'))
    dkvp = jnp.concatenate([dkn, dvv], axis=1).astype(BF16)
    gw['uq'] = _mm(sv['cqn'], dqp, mode='tn', out_dtype=F32, name=nm('d_uq'))
    dcqn = _mm(dqp, w['uq'], mode='nt', out_dtype=F32, name=nm('d_cqn'))
    gw['ukv'] = _mm(sv['ckvn'], dkvp, mode='tn', out_dtype=F32, name=nm('d_ukv'))
    dckvn = _mm(dkvp, w['ukv'], mode='nt', out_dtype=F32, name=nm('d_ckvn'))
    dmq, gs['g_mla_q'] = _norm_bwd_call(sv['mq'], dcqn, p['g_mla_q'], None, nm('d_norm_q'))
    dmkv, gs['g_mla_kv'] = _norm_bwd_call(sv['mkv'], dckvn, p['g_mla_kv'], None, nm('d_norm_kv'))
    doraw, dgr, gs['g_gla_out'] = _rowwise(_gla_out_bwd, [sv['oraw'], sv['gr'], do_br[1]], [p['g_gla_out']],
                                           [(512, F32), (512, BF16)], [128], name=nm('d_gla_out'))
    st = sv['states']
    st_prev = jnp.concatenate([jnp.zeros_like(st[:, :1]), st[:, :-1]], axis=1)
    dgq, dgk, dgv, dla = _gla_bwd(sv['zb'], sv['la'], st, st_prev, doraw, name=nm('gla_bwd'), **sv['gla'])

    def dgate_fn(dl, gl, wg, bg):
        pre = _dot(gl.astype(BF16), wg) + bg
        dpre = dl * (1.0 / GLA_TAU) * _sig(-pre)
        return dpre, _dot(dpre.astype(BF16), wg, NT), dpre

    dpre, dglow, gs['b_gla'] = _rowwise(dgate_fn, [dla, sv['glow']], [w['gate'], p['b_gla']],
                                        [(256, BF16), (128, BF16)], [256], name=nm('d_gla_gate'))
    gw['gate'] = _mm(sv['glow'], dpre, mode='tn', out_dtype=F32, name=nm('d_wgate'))
    bf = lambda t: t.astype(BF16)
    dzb = jnp.concatenate([dgr, bf(dgq), bf(dgk), bf(dgv), bf(dmq), dmkr2, bf(dff), dglow, bf(dmkv),
                           jnp.zeros((s, B_W - B_END), BF16)], axis=1)
    h = sv['h']
    gw['in_a'] = _mm(h, dza, mode='tn', out_dtype=F32, name=nm('d_in_a'))
    gw['in_b'] = _mm(h, dzb, mode='tn', out_dtype=F32, name=nm('d_in_b'))
    gw['in_c'] = _mm(h, dzc, mode='tn', out_dtype=F32, name=nm('d_in_c'))
    add = lambda acc, prev: prev + acc
    dh = _mm(dza, w['in_a'], mode='nt', out_dtype=F32, name=nm('d_h_a'))
    dh = _mm(dzb, w['in_b'], mode='nt', out_dtype=F32, name=nm('d_h_b'), epilogue=add, extras=[(dh, *_mn())])
    dh = _mm(dzc, w['in_c'], mode='nt', out_dtype=F32, name=nm('d_h_c'), epilogue=add, extras=[(dh, *_mn())])
    dx0, gs['g_mix'] = _norm_bwd_call(sv['x0'], dh, p['g_mix'], dx1, nm('d_norm_mix'))
    return dx0, gw, gs


def _loss_head(x, target, g_final):
    d = x.shape[1]

    def fn(xv, tv, gv):
        r = lax.rsqrt(jnp.mean(xv * xv, axis=-1, keepdims=True) + EPS)
        xh = xv * r
        e = xh * gv - tv
        dy = e * (1.0 / d)
        gd = dy * gv
        dx = r * (gd - xh * jnp.mean(gd * xh, axis=-1, keepdims=True))
        row_loss = 0.5 * jnp.mean(e * e, axis=-1, keepdims=True)
        return dx, dy * xh, jnp.broadcast_to(row_loss, (xv.shape[0], LANES))

    return _rowwise(fn, [x, target], [g_final], [(d, F32)], [d, LANES], name='loss_head')


def _small_sizes(shapes):
    return [math.prod(shapes[nm]) for nm in SMALL]


def _step(args):
    shapes = {nm: args[nm].shape for nm in ORDER}
    x, mem, target = args['x'][0], args['mem'][0], args['loss_target'][0]
    s = x.shape[0]

    def wire(nm):
        w = args[nm].astype(BF16)
        return jnp.pad(w, ((0, 0), (0, 0), (0, WIN_PAD - WIN_SHARD))) if nm == 'w_in' else w

    gathered = dict(zip([nm for nm, _ in BIG], _gather_shards([wire(nm) for nm, _ in BIG], name='gather_weights')))

    tabs = _rope_tables(s)
    layers_w, layers_p = [], []
    for l in range(DEPTH):
        layers_w.append(_repack_layer_weights({nm: _full_layer(gathered[nm], ax, l) for nm, ax in BIG}))
        layers_p.append({
            'g_mix': args['g_mix'][l][None], 'b_fox': _padc(args['b_fox_forget'][l][None], 128),
            'b_gla': args['b_gla_gate'][l][None], 'g_gla_out': args['g_gla_out'][l][None],
            'g_mla_q': args['g_mla_q'][l][None], 'g_mla_kv': args['g_mla_kv'][l][None],
            'b_branch': args['b_branch_gate'][l][None], 'g_xa': args['g_xa'][l][None],
            'g_mem': args['g_mem'][l][None], 'g_mlp': args['g_mlp'][l][None]})

    saved = []
    xl = x
    for l in range(DEPTH):
        xl, sv = _layer_fwd(xl, mem, layers_w[l], layers_p[l], tabs, f'l{l}')
        saved.append(sv)
    dx, dg_final, loss_lanes = _loss_head(xl, target, args['g_final'][None])
    gw_layers, gs_layers = [None] * DEPTH, [None] * DEPTH
    for l in reversed(range(DEPTH)):
        dx, gw, gs = _layer_bwd(dx, mem, layers_w[l], layers_p[l], tabs, saved[l], f'l{l}')
        gw_layers[l], gs_layers[l] = _unpack_layer_grads(gw), gs
    grad_x = dx[None]

    names = [nm for nm, _ in BIG]
    cidx = lax.axis_index('c')
    chip = 2 * lax.axis_index('x') + lax.axis_index('y')
    pick = lambda t, i, axis: lax.dynamic_index_in_dim(t, i, axis=axis, keepdims=False)
    per_chip = [_split_full(jnp.stack([gw_layers[l][nm] for l in range(DEPTH)]), ax).astype(BF16) for nm, ax in BIG]
    got = _to_sibling([pick(g, 1 - cidx, 1) for g in per_chip], name='grads_core_swap')
    pairs = []
    for nm, g, b in zip(names, per_chip, got):
        k, n = b.shape[1:]
        (p,) = _rowwise(lambda u, v: (u.astype(F32) + v.astype(F32),),
                        [pick(g, cidx, 1).reshape(N_CHIPS * k, n), b.reshape(N_CHIPS * k, n)], [], [(n, BF16)],
                        name=f'pair_sum_{nm}')
        pairs.append(p.reshape(N_CHIPS, k, n))
    from_chips = _chip_exchange(pairs, name='grads_chip_exchange')
    mine = [_sum_chips(pick(p, chip, 0), r, name=f'chip_sum_{nm}') for nm, p, r in zip(names, pairs, from_chips)]
    theirs = _to_sibling(mine, name='grads_join')
    gshard = {nm: jnp.where(cidx == 0, jnp.stack([a, b]), jnp.stack([b, a])) for nm, a, b in zip(names, mine, theirs)}

    small_g = []
    for nm, key in (('g_mix', 'g_mix'), ('b_fox_forget', 'b_fox'), ('b_gla_gate', 'b_gla'),
                    ('g_gla_out', 'g_gla_out'), ('g_mla_q', 'g_mla_q'), ('g_mla_kv', 'g_mla_kv'),
                    ('b_branch_gate', 'b_branch'), ('g_xa', 'g_xa'), ('g_mem', 'g_mem'), ('g_mlp', 'g_mlp')):
        width = shapes[nm][1]
        small_g.append(jnp.concatenate([gs_layers[l][key][0, :width] for l in range(DEPTH)]))
    small_g.append(dg_final[0])
    small_g.append(loss_lanes[0, :1])
    flat = jnp.concatenate(small_g)
    n_small = flat.shape[0]
    srows = -(-n_small // (8 * LANES)) * 8
    pad = lambda v: jnp.pad(v, (0, srows * LANES - v.shape[0])).reshape(srows, LANES)
    all_small = _all_gather8(pad(flat), name='gather_small')
    sw, sm, svv = (pad(jnp.concatenate([args[pre + nm].reshape(-1) for nm in SMALL] + [jnp.zeros((1,), F32)]))
                   for pre in ('', 'm_', 'v_'))

    def small_body(g_ref, w_ref, m_ref, v_ref, go_ref, d_ref, mo_ref, vo_ref):
        g = g_ref[0]
        for q in range(1, N_DEV):
            g = g + g_ref[q]
        go_ref[...] = g
        d_ref[...], mo_ref[...], vo_ref[...] = _adam(w_ref[...], g, m_ref[...], v_ref[...])

    sg, sd, snm, snv = pl.pallas_call(
        small_body, name='small_sum_adam', out_shape=[jax.ShapeDtypeStruct((srows, LANES), F32)] * 4,
        compiler_params=pltpu.CompilerParams(vmem_limit_bytes=VMEM_LIMIT))(all_small, sw, sm, svv)

    def unsmall(buf):
        v, out, off = buf.reshape(-1), {}, 0
        for nm in SMALL:
            nel = math.prod(shapes[nm])
            out[nm] = v[off:off + nel].reshape(shapes[nm])
            off += nel
        return out, v[off]

    res = {}
    (res['grad'], loss), (res['delta'], _), (res['m'], _), (res['v'], _) = (unsmall(t) for t in (sg, sd, snm, snv))

    for nm, _ in BIG:
        shp = args[nm].shape
        view = lambda t: t.reshape(shp[0] * shp[1], shp[2])
        d, m2, v2 = _rowwise(_adam, [view(args[nm]), view(gshard[nm]), view(args['m_' + nm]), view(args['v_' + nm])],
                             [], [(shp[2], F32)] * 3, name=f'adam_{nm}')
        res['grad'][nm], res['delta'][nm], res['m'][nm], res['v'][nm] = (
            gshard[nm], d.reshape(shp), m2.reshape(shp), v2.reshape(shp))

    return (loss, grad_x, *[res['grad'][nm] for nm in ORDER], *[res['delta'][nm] for nm in ORDER],
            *[res['m'][nm] for nm in ORDER], *[res['v'][nm] for nm in ORDER])


def kernel(x, mem, g_mix, w_in, b_fox_forget, w_gla_gate, b_gla_gate, g_gla_out, g_mla_q, w_mla_uq, g_mla_kv, w_mla_ukv, b_branch_gate, w_up_fox, w_up_gla, w_up_mla, w_out, g_xa, g_mem, w_xq, w_xkv, w_xo, g_mlp, w_mlp1, w_mlp2, g_final, loss_target, m_g_mix, m_w_in, m_b_fox_forget, m_w_gla_gate, m_b_gla_gate, m_g_gla_out, m_g_mla_q, m_w_mla_uq, m_g_mla_kv, m_w_mla_ukv, m_b_branch_gate, m_w_up_fox, m_w_up_gla, m_w_up_mla, m_w_out, m_g_xa, m_g_mem, m_w_xq, m_w_xkv, m_w_xo, m_g_mlp, m_w_mlp1, m_w_mlp2, m_g_final, v_g_mix, v_w_in, v_b_fox_forget, v_w_gla_gate, v_b_gla_gate, v_g_gla_out, v_g_mla_q, v_w_mla_uq, v_g_mla_kv, v_w_mla_ukv, v_b_branch_gate, v_w_up_fox, v_w_up_gla, v_w_up_mla, v_w_out, v_g_xa, v_g_mem, v_w_xq, v_w_xkv, v_w_xo, v_g_mlp, v_w_mlp1, v_w_mlp2, v_g_final):
    return _step(dict(locals()))
```

```python
import functools
import math
import typing

import jax
import jax.numpy as jnp
from jax import lax
from jax.experimental import pallas as pl
from jax.experimental.pallas import tpu as pltpu

F32 = jnp.float32
BF16 = jnp.bfloat16
MESH = pl.DeviceIdType.MESH

D_MODEL = 1024
DEPTH = 2
CHUNK = 64
EPS = 1e-6
FOX_HEADS, FOX_HD = 4, 64
GLA_HEADS, GLA_DK, GLA_DV, GLA_RANK, GLA_TAU = 4, 64, 128, 16, 16.0
MLA_HEADS, MLA_Q_RANK, MLA_KV_RANK, MLA_NOPE, MLA_ROPE, MLA_VD = 4, 256, 128, 64, 32, 64
ROPE_BASE = 10000.0
XA_HEADS, XA_HD = 4, 128
D_FF = 4 * D_MODEL
IN_SIZES = (256, 256, 256, 4, 256, 256, 512, 16, 512, 256, 128, 32, 3072)
N_IN = sum(IN_SIZES)

ADAM_LR, ADAM_B1, ADAM_B2, ADAM_EPS, ADAM_WD, ADAM_STEP = 0.001, 0.9, 0.999, 1e-08, 0.01, 10

N_CHIPS = 4
N_DEV = 8
LANES = 128
VMEM_LIMIT = 48 * 1024 * 1024
MASK_VALUE = -1e30

BIG = (('w_in', 2), ('w_gla_gate', 2), ('w_mla_uq', 2), ('w_mla_ukv', 2), ('w_up_fox', 2), ('w_up_gla', 2),
       ('w_up_mla', 2), ('w_out', 1), ('w_xq', 1), ('w_xkv', 1), ('w_xo', 2), ('w_mlp1', 2), ('w_mlp2', 1))
SMALL = ('g_mix', 'b_fox_forget', 'b_gla_gate', 'g_gla_out', 'g_mla_q', 'g_mla_kv', 'b_branch_gate',
         'g_xa', 'g_mem', 'g_mlp', 'g_final')
ORDER = ('g_mix', 'w_in', 'b_fox_forget', 'w_gla_gate', 'b_gla_gate', 'g_gla_out', 'g_mla_q', 'w_mla_uq',
         'g_mla_kv', 'w_mla_ukv', 'b_branch_gate', 'w_up_fox', 'w_up_gla', 'w_up_mla', 'w_out', 'g_xa', 'g_mem',
         'w_xq', 'w_xkv', 'w_xo', 'g_mlp', 'w_mlp1', 'w_mlp2', 'g_final')


def _params(*sem):
    return pltpu.CompilerParams(dimension_semantics=sem, vmem_limit_bytes=VMEM_LIMIT)


def _sig(x):
    return 1.0 / (1.0 + jnp.exp(-x))


def _logsig(x):
    return jnp.minimum(x, 0.0) - jnp.log(1.0 + jnp.exp(-jnp.abs(x)))


NN = (((1,), (0,)), ((), ()))
NT = (((1,), (1,)), ((), ()))
TN = (((0,), (0,)), ((), ()))


def _dot(a, b, dims=NN):
    return lax.dot_general(a, b, dims, preferred_element_type=F32)


class Cols(typing.NamedTuple):
    arr: jax.Array
    width: int
    blk: int


def _tri_dot(tri, x):
    hi = x.astype(BF16)
    r1 = x - hi.astype(F32)
    mid = r1.astype(BF16)
    lo = (r1 - mid.astype(F32)).astype(BF16)
    return _dot(tri, hi) + _dot(tri, mid) + _dot(tri, lo)


MM_TILES = ((1024, 1024), (1024, 512), (512, 1024), (512, 512), (512, 256), (256, 512), (256, 256), (128, 128))
MM_VMEM_BUDGET = 38 * 1024 * 1024


def _mm_tiles(m, n, k, a_bytes, b_bytes, out_bytes, ex_bytes, has_norm, emit_norm, has_fn):
    for tm, tn in MM_TILES:
        tm, tn = min(tm, m), min(tn, n)
        if m % tm or n % tn:
            continue
        blocks = tm * k * a_bytes + k * tn * b_bytes + tm * tn * (out_bytes + ex_bytes) + (tm * k * 2 if emit_norm else 0)
        temps = tm * tn * 4 + (tm * k * 2 if has_norm else 0) + (tm * k * 6 if has_fn or has_norm else 0)
        if 2 * blocks + temps <= MM_VMEM_BUDGET:
            return tm, tn
    raise ValueError((m, n, k))


def _mm(a, b, *, mode, out_dtype, name, norm_g=None, emit_norm=False, a_fn=None, extras=(), epilogue=None):
    a_blk = 0
    if isinstance(a, Cols):
        a, width, a_blk = a
        a_shape = (a.shape[0], width)
    else:
        a_shape = a.shape
    if mode == 'tn':
        k, m = a_shape
    else:
        m, k = a_shape
    n = b.shape[0] if mode == 'nt' else b.shape[1]
    assert (b.shape[1] if mode == 'nt' else b.shape[0]) == k, (name, a.shape, b.shape)
    has_norm = norm_g is not None
    ex_bytes = sum(arr.dtype.itemsize for arr, kind, _ in extras if kind == 'mn')
    tm, tn = _mm_tiles(m, n, k, a.dtype.itemsize, b.dtype.itemsize, jnp.dtype(out_dtype).itemsize, ex_bytes, has_norm,
                       emit_norm, a_fn is not None)
    assert all(col % tn == 0 for _, _, col in extras), (name, tn)
    assert a_blk == 0 or (mode == 'nn') or (mode == 'tn' and tm == m)
    if mode == 'tn':
        a_spec = pl.BlockSpec((k, tm), lambda i, j: (0, i + a_blk))
    else:
        a_spec = pl.BlockSpec((tm, k), lambda i, j: (i, a_blk))
    b_spec = pl.BlockSpec((tn, k), lambda i, j: (j, 0)) if mode == 'nt' else pl.BlockSpec((k, tn), lambda i, j: (0, j))
    dims = {'nn': NN, 'nt': NT, 'tn': TN}[mode]
    assert not (has_norm and mode != 'nn')
    n_ex = len(extras)

    def body(*refs):
        a_ref, b_ref = refs[0], refs[1]
        pos = 2
        g_ref = None
        if has_norm:
            g_ref = refs[pos]
            pos += 1
        ex_refs = refs[pos:pos + n_ex]
        pos += n_ex
        o_ref = refs[pos]
        pos += 1
        h_ref = None
        if emit_norm:
            h_ref = refs[pos]
            pos += 1
        if has_norm:
            an_ref = refs[pos]

            @pl.when(pl.program_id(1) == 0)
            def _():
                xf = a_ref[...].astype(F32)
                y = xf * lax.rsqrt(jnp.mean(xf * xf, axis=-1, keepdims=True) + EPS) * g_ref[...]
                an_ref[...] = y.astype(BF16)
                if emit_norm:
                    h_ref[...] = y.astype(BF16)

            av = an_ref[...]
        else:
            av = a_ref[...]
            if a_fn is not None:
                av = a_fn(av)
            av = av.astype(BF16)
        acc = _dot(av, b_ref[...].astype(BF16), dims)
        if epilogue is not None:
            acc = epilogue(acc, *[r[...] for r in ex_refs])
        o_ref[...] = acc.astype(out_dtype)

    in_specs = [a_spec, b_spec]
    args = [a, b]
    if has_norm:
        in_specs.append(pl.BlockSpec((1, k), lambda i, j: (0, 0)))
        args.append(norm_g)
    for arr, kind, col in extras:
        if kind == 'mn':
            in_specs.append(pl.BlockSpec((tm, tn), lambda i, j, o=col // tn: (i, j + o)))
        else:
            in_specs.append(pl.BlockSpec((1, tn), lambda i, j, o=col // tn: (0, j + o)))
        args.append(arr)
    out_shape = [jax.ShapeDtypeStruct((m, n), out_dtype)]
    out_specs = [pl.BlockSpec((tm, tn), lambda i, j: (i, j))]
    if emit_norm:
        out_shape.append(jax.ShapeDtypeStruct((m, k), BF16))
        out_specs.append(pl.BlockSpec((tm, k), lambda i, j: (i, 0)))
    scratch = [pltpu.VMEM((tm, k), BF16)] if has_norm else []
    res = pl.pallas_call(
        body, name=name, grid=(m // tm, n // tn), in_specs=in_specs, out_specs=out_specs, out_shape=out_shape,
        scratch_shapes=scratch, compiler_params=_params('arbitrary', 'arbitrary'))(*args)
    return res if emit_norm else res[0]


def _mn(col_off=0):
    return 'mn', col_off


def _nvec(col_off=0):
    return 'n', col_off


def _rowwise(fn, rows, consts, outs, sums=(), *, name, ts=256):
    views = [x if isinstance(x, Cols) else Cols(x, x.shape[1], 0) for x in rows]
    rows = [v.arr for v in views]
    r = rows[0].shape[0]
    ts = min(ts, r)
    assert r % ts == 0, (name, r, ts)
    nr, nc, no, ns = len(rows), len(consts), len(outs), len(sums)

    def body(*refs):
        vals = fn(*[x[...] for x in refs[:nr + nc]])
        for q in range(no):
            refs[nr + nc + q][...] = vals[q].astype(outs[q][1])
        if ns:
            @pl.when(pl.program_id(0) == 0)
            def _():
                for q in range(ns):
                    refs[nr + nc + no + q][...] = jnp.zeros((1, sums[q]), F32)

            for q in range(ns):
                refs[nr + nc + no + q][...] += jnp.sum(vals[no + q].astype(F32), axis=0, keepdims=True)

    in_specs = [pl.BlockSpec((ts, v.width), lambda i, blk=v.blk: (i, blk)) for v in views]
    in_specs += [pl.BlockSpec(x.shape, lambda i, nd=x.ndim: (0,) * nd) for x in consts]
    out_specs = [pl.BlockSpec((ts, w), lambda i: (i, 0)) for w, _ in outs]
    out_specs += [pl.BlockSpec((1, w), lambda i: (0, 0)) for w in sums]
    out_shape = [jax.ShapeDtypeStruct((r, w), dt) for w, dt in outs]
    out_shape += [jax.ShapeDtypeStruct((1, w), F32) for w in sums]
    return pl.pallas_call(body, name=name, grid=(r // ts,), in_specs=in_specs, out_specs=out_specs,
                          out_shape=out_shape, compiler_params=_params('arbitrary'))(*rows, *consts)


def _cumsum_rows(x, *, reverse, name, bs=256):
    s, w = x.shape
    bs = min(bs, s)
    nb = s // bs

    def body(x_ref, o_ref, carry):
        @pl.when(pl.program_id(0) == 0)
        def _():
            carry[...] = jnp.zeros_like(carry)

        r = lax.broadcasted_iota(jnp.int32, (bs, bs), 0)
        c = lax.broadcasted_iota(jnp.int32, (bs, bs), 1)
        tri = jnp.where((c >= r) if reverse else (c <= r), 1.0, 0.0).astype(BF16)
        xv = x_ref[...]
        o_ref[...] = _tri_dot(tri, xv) + carry[...]
        carry[...] += jnp.sum(xv, axis=0, keepdims=True)

    imap = (lambda i: (nb - 1 - i, 0)) if reverse else (lambda i: (i, 0))
    return pl.pallas_call(body, name=name, grid=(nb,), in_specs=[pl.BlockSpec((bs, w), imap)],
                          out_specs=pl.BlockSpec((bs, w), imap), out_shape=jax.ShapeDtypeStruct((s, w), F32),
                          scratch_shapes=[pltpu.VMEM((1, w), F32)], compiler_params=_params('arbitrary'))(x)


def _mask(mode, q0, k0, bq, bk):
    qpos = q0 + lax.broadcasted_iota(jnp.int32, (bq, bk), 0)
    kpos = k0 + lax.broadcasted_iota(jnp.int32, (bq, bk), 1)
    if mode == 'causal':
        return kpos <= qpos
    return kpos < (jnp.right_shift(qpos, int(math.log2(CHUNK))) + 1) * CHUNK


ROPE_SHIFT = int(math.log2(MLA_ROPE))
ATTN_ROW_SLAB = 512


def _lane_masks(g, b, rope):
    lane = lax.broadcasted_iota(jnp.int32, (1, LANES), 1)
    heads = [None if g == 1 else (lane >= hh * (LANES // g)) & (lane < (hh + 1) * (LANES // g)) for hh in range(g)]
    ropes = [jnp.right_shift(lane, ROPE_SHIFT) == b * g + hh for hh in range(g)] if rope else [None] * g
    return heads, ropes


def _sel(mask, x):
    return x if mask is None else jnp.where(mask, x, jnp.zeros_like(x))


def _carried(comm, refs, n_in, n_out):
    ci, co = len(comm.ins), len(comm.out_shapes)
    ins = refs[n_in:n_in + ci]
    outs = refs[n_in + ci + n_out:n_in + ci + n_out + co]
    rest = refs[:n_in] + refs[n_in + ci:n_in + ci + n_out] + refs[n_in + ci + n_out + co:-2]
    return rest, (ins, outs, refs[-2], refs[-1])


def _mattn_fwd(q, k, v, *, qc, kc, vc, nb, g, scale, mode, name, ck=None, qr=None, qrc=0, kr=None, blk=512,
               comm=None):
    s, t = q.shape[0], k.shape[0]
    bq, bk = min(blk, s), min(blk, t)
    nq, nk = s // bq, t // bk
    tri = mode != 'full'
    bias, rope = ck is not None, qr is not None
    assert not tri or (bq == bk and bq % CHUNK == 0)
    rs = min(ATTN_ROW_SLAB, bq)
    n_in = 3 + bias + 2 * rope

    def body(*refs):
        refs = list(refs)
        b, i, j = pl.program_id(0), pl.program_id(1), pl.program_id(2)
        if comm is not None:
            refs, comm_refs = _carried(comm, refs, n_in, 2)
            pl.when((b == 0) & (i == 0) & (j == 0))(lambda: comm.start(*comm_refs))
        q_ref, k_ref, v_ref = refs[:3]
        pos = 3
        ck_ref = qr_ref = kr_ref = None
        if bias:
            ck_ref = refs[pos]
            pos += 1
        if rope:
            qr_ref, kr_ref = refs[pos:pos + 2]
            pos += 2
        o_ref, lse_ref, m_s, l_s, acc_s = refs[pos:]
        heads, ropes = _lane_masks(g, b, rope)

        @pl.when(j == 0)
        def _():
            m_s[...] = jnp.full_like(m_s, MASK_VALUE)
            l_s[...] = jnp.zeros_like(l_s)
            acc_s[...] = jnp.zeros_like(acc_s)

        def compute(masked):
            k2, v2 = k_ref[...], v_ref[...]
            for r in range(bq // rs):
                rows = pl.ds(r * rs, rs)
                q2 = q_ref[rows, :]
                alphas, pvs = [], []
                for hh in range(g):
                    sc = _dot(_sel(heads[hh], q2), k2, NT)
                    if rope:
                        sc = sc + _dot(_sel(ropes[hh], qr_ref[rows, :]), kr_ref[...], NT)
                    sc = sc * scale
                    if bias:
                        sc = sc - ck_ref[0, hh:hh + 1, :]
                    if masked:
                        sc = jnp.where(_mask(mode, i * bq + r * rs, j * bk, rs, bk), sc, MASK_VALUE)
                    m_prev = m_s[hh, rows]
                    m_new = jnp.maximum(m_prev, jnp.max(sc, axis=1, keepdims=True))
                    alpha = jnp.exp(m_prev - m_new)
                    p = jnp.exp(sc - m_new)
                    l_s[hh, rows] = alpha * l_s[hh, rows] + jnp.sum(p, axis=1, keepdims=True)
                    m_s[hh, rows] = m_new
                    alphas.append(alpha)
                    pvs.append(_dot(p.astype(BF16), _sel(heads[hh], v2)))
                alpha = alphas[0]
                for hh in range(1, g):
                    alpha = jnp.where(heads[hh], alphas[hh], alpha)
                acc_s[rows, :] = acc_s[rows, :] * alpha + sum(pvs[1:], pvs[0])

        if tri:
            pl.when(j < i)(functools.partial(compute, False))
            pl.when(j == i)(functools.partial(compute, True))
        else:
            compute(False)

        @pl.when(j == nk - 1)
        def _():
            lane = lax.broadcasted_iota(jnp.int32, (bq, LANES), 1)
            l_full, lse = l_s[0], jnp.zeros((bq, LANES), F32)
            for hh in range(g):
                if hh:
                    l_full = jnp.where(heads[hh], l_s[hh], l_full)
                lse = jnp.where(lane == hh, m_s[hh] + jnp.log(l_s[hh]), lse)
            o_ref[...] = (acc_s[...] / l_full).astype(o_ref.dtype)
            lse_ref[...] = lse

        if comm is not None:
            pl.when((b == nb - 1) & (i == nq - 1) & (j == nk - 1))(lambda: comm.finish(*comm_refs))

    jj = (lambda i, j: jnp.minimum(i, j)) if tri else (lambda i, j: j)
    in_specs = [pl.BlockSpec((bq, LANES), lambda b, i, j: (i, qc + b)),
                pl.BlockSpec((bk, LANES), lambda b, i, j: (jj(i, j), kc + b)),
                pl.BlockSpec((bk, LANES), lambda b, i, j: (jj(i, j), vc + b))]
    args = [q, k, v]
    if bias:
        in_specs.append(pl.BlockSpec((1, 8, bk), lambda b, i, j: (b, 0, jj(i, j))))
        args.append(ck)
    if rope:
        in_specs += [pl.BlockSpec((bq, LANES), lambda b, i, j: (i, qrc)),
                     pl.BlockSpec((bk, LANES), lambda b, i, j: (jj(i, j), 0))]
        args += [qr, kr]
    out = pl.BlockSpec((bq, LANES), lambda b, i, j: (i, b))
    out_specs = [out, out]
    out_shape = [jax.ShapeDtypeStruct((s, LANES * nb), BF16), jax.ShapeDtypeStruct((s, LANES * nb), F32)]
    scratch = [pltpu.VMEM((g, bq, 1), F32), pltpu.VMEM((g, bq, 1), F32), pltpu.VMEM((bq, LANES), F32)]
    if comm is not None:
        in_specs += [ANY] * len(comm.ins)
        args += comm.ins
        out_specs += [ANY] * len(comm.out_shapes)
        out_shape += comm.out_shapes
        scratch += _sems(comm.n_sems, comm.n_sems)
    res = pl.pallas_call(body, name=name, grid=(nb, nq, nk), in_specs=in_specs, out_specs=out_specs, out_shape=out_shape,
                         scratch_shapes=scratch, compiler_params=_params('arbitrary', 'arbitrary', 'arbitrary'))(*args)
    return res if comm is None else (res[0], res[1], res[2:])


def _mattn_bwd(q, k, v, o, do, lse, *, qc, kc, vc, nb, g, scale, mode, name, ck=None, qr=None, qrc=0, kr=None,
               blk=512, comm=None):
    s, t = q.shape[0], k.shape[0]
    bq, bk = min(blk, s), min(blk, t)
    nq, nk = s // bq, t // bk
    tri = mode != 'full'
    bias, rope = ck is not None, qr is not None
    rs = min(ATTN_ROW_SLAB, bq)
    n_in, n_out = 6 + bias + 2 * rope, 3 + 2 * bias + 2 * rope

    def body(*refs):
        refs = list(refs)
        if comm is not None:
            refs, comm_refs = _carried(comm, refs, n_in, n_out)
            first = (pl.program_id(0) == 0) & (pl.program_id(1) == 0) & (pl.program_id(2) == 0)
            pl.when(first)(lambda: comm.start(*comm_refs))
        q_ref, k_ref, v_ref, o_ref, do_ref, lse_ref = refs[:6]
        pos = 6
        ck_ref = qr_ref = kr_ref = dck_ref = dcq_ref = dqr_ref = dkr_ref = dck_s = None
        if bias:
            ck_ref = refs[pos]
            pos += 1
        if rope:
            qr_ref, kr_ref = refs[pos:pos + 2]
            pos += 2
        dq_ref, dk_ref, dv_ref = refs[pos:pos + 3]
        pos += 3
        if bias:
            dck_ref, dcq_ref = refs[pos:pos + 2]
            pos += 2
        if rope:
            dqr_ref, dkr_ref = refs[pos:pos + 2]
            pos += 2
        dk_s, dv_s = refs[pos:pos + 2]
        if bias:
            dck_s = refs[pos + 2]
        b, j, i = pl.program_id(0), pl.program_id(1), pl.program_id(2)
        heads, ropes = _lane_masks(g, b, rope)

        @pl.when((j == 0) & (i == 0))
        def _():
            dq_ref[...] = jnp.zeros_like(dq_ref)
            if bias:
                dcq_ref[...] = jnp.zeros_like(dcq_ref)

        if rope:
            @pl.when((b == 0) & (j == 0) & (i == 0))
            def _():
                dqr_ref[...] = jnp.zeros_like(dqr_ref)
                dkr_ref[...] = jnp.zeros_like(dkr_ref)

        @pl.when(i == 0)
        def _():
            dk_s[...] = jnp.zeros_like(dk_s)
            dv_s[...] = jnp.zeros_like(dv_s)
            if bias:
                dck_s[...] = jnp.zeros_like(dck_s)

        def compute(masked):
            k2, v2 = k_ref[...], v_ref[...]
            lane = lax.broadcasted_iota(jnp.int32, (rs, LANES), 1)
            rk = pl.ds(pl.multiple_of(j * bk, bk), bk)
            add = lambda tot, x: x if tot is None else tot + x
            dv_t = dk_t = dkr_t = None
            dck_t = [None] * g
            for r in range(bq // rs):
                rows = pl.ds(r * rs, rs)
                rq = pl.ds(pl.multiple_of(i * bq + r * rs, rs), rs)
                q2, do2, lse2 = q_ref[rows, :], do_ref[rows, :], lse_ref[rows, :]
                dd = do2.astype(F32) * o_ref[rows, :].astype(F32)
                dq_t = dqr_t = dcq_t = None
                for hh in range(g):
                    qm = _sel(heads[hh], q2)
                    sc = _dot(qm, k2, NT)
                    if rope:
                        qrm = _sel(ropes[hh], qr_ref[rows, :])
                        sc = sc + _dot(qrm, kr_ref[...], NT)
                    sc = sc * scale
                    if bias:
                        sc = sc - ck_ref[0, hh:hh + 1, :]
                    if masked:
                        sc = jnp.where(_mask(mode, i * bq + r * rs, j * bk, rs, bk), sc, MASK_VALUE)
                    p = jnp.exp(sc - jnp.sum(jnp.where(lane == hh, lse2, 0.0), axis=1, keepdims=True))
                    dom = _sel(heads[hh], do2)
                    dp = _dot(dom, v2, NT)
                    delta = jnp.sum(_sel(heads[hh], dd), axis=1, keepdims=True)
                    ds = p * (dp - delta)
                    dsb = ds.astype(BF16)
                    dv_t = add(dv_t, _dot(p.astype(BF16), dom, TN))
                    dk_t = add(dk_t, _dot(dsb, qm, TN))
                    dq_t = add(dq_t, _dot(dsb, _sel(heads[hh], k2)))
                    if rope:
                        dqr_t = add(dqr_t, _dot(dsb, _sel(ropes[hh], kr_ref[...])))
                        dkr_t = add(dkr_t, _dot(dsb, qrm, TN))
                    if bias:
                        dck_t[hh] = add(dck_t[hh], jnp.sum(ds, axis=0, keepdims=True))
                        dcq_t = add(dcq_t, jnp.where(lane == hh, jnp.sum(ds, axis=1, keepdims=True), 0.0))
                dq_ref[rq, :] += scale * dq_t
                if rope:
                    dqr_ref[rq, :] += scale * dqr_t
                if bias:
                    dcq_ref[rq, :] += dcq_t
            dv_s[...] += dv_t
            dk_s[...] += scale * dk_t
            if rope:
                dkr_ref[rk, :] += scale * dkr_t
            if bias:
                for hh in range(g):
                    dck_s[hh:hh + 1, :] -= dck_t[hh]

        if tri:
            pl.when(i > j)(functools.partial(compute, False))
            pl.when(i == j)(functools.partial(compute, True))
        else:
            compute(False)

        @pl.when(i == nq - 1)
        def _():
            dk_ref[...] = dk_s[...]
            dv_ref[...] = dv_s[...]
            if bias:
                dck_ref[0] = dck_s[...]

        if comm is not None:
            pl.when((b == nb - 1) & (j == nk - 1) & (i == nq - 1))(lambda: comm.finish(*comm_refs))

    ii = (lambda j, i: jnp.maximum(i, j)) if tri else (lambda j, i: i)
    qrow = lambda col: pl.BlockSpec((bq, LANES), lambda b, j, i: (ii(j, i), col(b)))
    krow = lambda col: pl.BlockSpec((bk, LANES), lambda b, j, i: (j, col(b)))
    in_specs = [qrow(lambda b: qc + b), krow(lambda b: kc + b), krow(lambda b: vc + b), qrow(lambda b: b),
                qrow(lambda b: b), qrow(lambda b: b)]
    args = [q, k, v, o, do, lse]
    whole = lambda rows: pl.BlockSpec((rows, LANES), lambda b, j, i: (0, b))
    out_specs = [whole(s), krow(lambda b: b), krow(lambda b: b)]
    out_shape = [jax.ShapeDtypeStruct((s, LANES * nb), F32), jax.ShapeDtypeStruct((t, LANES * nb), F32),
                 jax.ShapeDtypeStruct((t, LANES * nb), F32)]
    scratch = [pltpu.VMEM((bk, LANES), F32), pltpu.VMEM((bk, LANES), F32)]
    if bias:
        in_specs.append(pl.BlockSpec((1, 8, bk), lambda b, j, i: (b, 0, j)))
        args.append(ck)
        out_specs += [pl.BlockSpec((1, 8, bk), lambda b, j, i: (b, 0, j)), whole(s)]
        out_shape += [jax.ShapeDtypeStruct((nb, 8, t), F32), jax.ShapeDtypeStruct((s, LANES * nb), F32)]
    if rope:
        in_specs += [qrow(lambda b: qrc), krow(lambda b: 0)]
        args += [qr, kr]
        out_specs += [pl.BlockSpec((s, LANES), lambda b, j, i: (0, 0)), pl.BlockSpec((t, LANES), lambda b, j, i: (0, 0))]
        out_shape += [jax.ShapeDtypeStruct((s, LANES), F32), jax.ShapeDtypeStruct((t, LANES), F32)]
    if bias:
        scratch.append(pltpu.VMEM((8, bk), F32))
    if comm is not None:
        in_specs += [ANY] * len(comm.ins)
        args += comm.ins
        out_specs += [ANY] * len(comm.out_shapes)
        out_shape += comm.out_shapes
        scratch += _sems(comm.n_sems, comm.n_sems)
    res = pl.pallas_call(body, name=name, grid=(nb, nk, nq), in_specs=in_specs, out_specs=out_specs,
                         out_shape=out_shape, scratch_shapes=scratch,
                         compiler_params=_params('arbitrary', 'arbitrary', 'arbitrary'))(*args)
    return res if comm is None else (*res[:n_out], res[n_out:])


def _gla_chunk(la_c, k_c):
    r = lax.broadcasted_iota(jnp.int32, (CHUNK, CHUNK), 0)
    c = lax.broadcasted_iota(jnp.int32, (CHUNK, CHUNK), 1)
    tri = jnp.where(c <= r, 1.0, 0.0).astype(BF16)
    cum = _tri_dot(tri, la_c)
    end = jnp.sum(la_c, axis=0, keepdims=True)
    dec = jnp.exp(end - cum)
    return dec, k_c * dec, jnp.exp(end)


GLA_PAIRS = GLA_HEADS // 2


def _gla_fwd(z, la, *, qc, kc, vc, name, blk=512):
    s = z.shape[0]
    bs = min(blk, s)
    ncb = bs // CHUNK
    nblk = s // bs

    def body(q_ref, k_ref, va_ref, vb_ref, la_ref, o_ref, st_ref, st):
        @pl.when(pl.program_id(1) == 0)
        def _():
            st[...] = jnp.zeros_like(st)

        heads, _ = _lane_masks(2, 0, False)
        v_refs = (va_ref, vb_ref)
        for c in range(ncb):
            sl = pl.ds(c * CHUNK, CHUNK)
            _, kf, a = _gla_chunk(la_ref[sl, :], k_ref[sl, :])
            qs = q_ref[sl, :] * (GLA_DK ** -0.5)
            for hh in range(2):
                ut = _dot(v_refs[hh][sl, :].astype(BF16), _sel(heads[hh], kf).astype(BF16), TN)
                new = a * st[hh] + ut
                st[hh] = new
                st_ref[0, c, hh] = new
                o_ref[sl, hh * GLA_DV:(hh + 1) * GLA_DV] = _dot(_sel(heads[hh], qs).astype(BF16), new.astype(BF16), NT)

    col = lambda c0, m=1: pl.BlockSpec((bs, LANES), lambda b, i: (i, c0 + m * b))
    return pl.pallas_call(
        body, name=name, grid=(GLA_PAIRS, nblk),
        in_specs=[col(qc), col(kc), col(vc, 2), col(vc + 1, 2), col(0)],
        out_specs=[pl.BlockSpec((bs, 2 * GLA_DV), lambda b, i: (i, b)),
                   pl.BlockSpec((1, ncb, 2, GLA_DV, LANES), lambda b, i: (b, i, 0, 0, 0))],
        out_shape=[jax.ShapeDtypeStruct((s, GLA_HEADS * GLA_DV), F32),
                   jax.ShapeDtypeStruct((GLA_PAIRS, s // CHUNK, 2, GLA_DV, LANES), F32)],
        scratch_shapes=[pltpu.VMEM((2, GLA_DV, LANES), F32)],
        compiler_params=_params('arbitrary', 'arbitrary'))(z, z, z, z, la)


def _gla_bwd(z, la, st_all, st_prev, do, *, qc, kc, vc, name, blk=512):
    s = z.shape[0]
    bs = min(blk, s)
    ncb = bs // CHUNK
    nblk = s // bs

    def body(q_ref, k_ref, va_ref, vb_ref, la_ref, st_ref, sp_ref, do_ref, dq_ref, dk_ref, dv_ref, dla_ref, ga):
        @pl.when(pl.program_id(1) == 0)
        def _():
            ga[...] = jnp.zeros_like(ga)

        r = lax.broadcasted_iota(jnp.int32, (CHUNK, CHUNK), 0)
        cc = lax.broadcasted_iota(jnp.int32, (CHUNK, CHUNK), 1)
        tri_rev = jnp.where(cc >= r, 1.0, 0.0).astype(BF16)
        heads, _ = _lane_masks(2, 0, False)
        v_refs = (va_ref, vb_ref)
        for c in reversed(range(ncb)):
            sl = pl.ds(c * CHUNK, CHUNK)
            dec, kf, a = _gla_chunk(la_ref[sl, :], k_ref[sl, :])
            qs = q_ref[sl, :] * (GLA_DK ** -0.5)
            dq2 = jnp.zeros((CHUNK, LANES), F32)
            dkd = jnp.zeros((CHUNK, LANES), F32)
            da = jnp.zeros((1, LANES), F32)
            for hh in range(2):
                hv = slice(hh * GLA_DV, (hh + 1) * GLA_DV)
                dob = do_ref[sl, hv].astype(BF16)
                g = _dot(dob, _sel(heads[hh], qs).astype(BF16), TN) + ga[hh]
                gb = g.astype(BF16)
                dq2 = dq2 + _dot(dob, st_ref[0, c, hh].astype(BF16))
                dv_ref[sl, hv] = _dot(_sel(heads[hh], kf).astype(BF16), gb, NT)
                dkd = dkd + _dot(v_refs[hh][sl, :].astype(BF16), gb)
                da = da + jnp.sum(g * sp_ref[0, c, hh], axis=0, keepdims=True)
                ga[hh] = a * g
            dq_ref[sl, :] = (GLA_DK ** -0.5) * dq2
            dk_ref[sl, :] = dkd * dec
            e = dkd * kf
            dend = jnp.sum(e, axis=0, keepdims=True) + da * a
            dla_ref[sl, :] = dend - _tri_dot(tri_rev, e)

    rev = lambda i: nblk - 1 - i
    col = lambda c0, m=1: pl.BlockSpec((bs, LANES), lambda b, i: (rev(i), c0 + m * b))
    wide = pl.BlockSpec((bs, 2 * GLA_DV), lambda b, i: (rev(i), b))
    stspec = pl.BlockSpec((1, ncb, 2, GLA_DV, LANES), lambda b, i: (b, rev(i), 0, 0, 0))
    return pl.pallas_call(
        body, name=name, grid=(GLA_PAIRS, nblk),
        in_specs=[col(qc), col(kc), col(vc, 2), col(vc + 1, 2), col(0), stspec, stspec, wide],
        out_specs=[col(0), col(0), wide, col(0)],
        out_shape=[jax.ShapeDtypeStruct((s, GLA_HEADS * GLA_DK), F32), jax.ShapeDtypeStruct((s, GLA_HEADS * GLA_DK), F32),
                   jax.ShapeDtypeStruct((s, GLA_HEADS * GLA_DV), F32), jax.ShapeDtypeStruct((s, GLA_HEADS * GLA_DK), F32)],
        scratch_shapes=[pltpu.VMEM((2, GLA_DV, LANES), F32)],
        compiler_params=_params('arbitrary', 'arbitrary'))(z, z, z, z, la, st_all, st_prev, do)


def _place():
    return lax.axis_index('x'), lax.axis_index('y'), lax.axis_index('c')


ANY = pl.BlockSpec(memory_space=pl.ANY)


def _all_gather8(blk, *, name):
    m, n = blk.shape

    def body(x_ref, out_ref, send_sems, recv_sems, local_sem):
        x, y, c = _place()
        me, sibling = (x, y, c), (x, y, 1 - c)
        chips = [(1 - x, y), (x, 1 - y), (1 - x, 1 - y)]

        def slot(px, py, pc):
            return out_ref.at[4 * px + 2 * py + pc]

        def copy(q, block, to, src=None):
            return pltpu.make_async_remote_copy(
                src_ref=slot(*block) if src is None else src, dst_ref=slot(*block), send_sem=send_sems.at[q],
                recv_sem=recv_sems.at[q], device_id=to, device_id_type=MESH)

        mine = pltpu.make_async_copy(x_ref, slot(*me), local_sem)
        mine.start()
        first = [copy(0, me, sibling, src=x_ref)]
        first += [copy(1 + q, me, (*chip, c), src=x_ref) for q, chip in enumerate(chips)]
        for cp in first:
            cp.start()
        passed = [copy(4 + q, (*chip, c), sibling) for q, chip in enumerate(chips)]
        for q, chip in enumerate(chips):
            copy(1 + q, (*chip, c), me).wait_recv()
            passed[q].start()
        copy(0, sibling, me).wait_recv()
        for q, chip in enumerate(chips):
            copy(4 + q, (*chip, 1 - c), me).wait_recv()
        for cp in first + passed:
            cp.wait_send()
        mine.wait()

    return pl.pallas_call(
        body, name=name, in_specs=[ANY], out_specs=ANY, out_shape=jax.ShapeDtypeStruct((N_DEV, m, n), blk.dtype),
        scratch_shapes=[pltpu.SemaphoreType.DMA((7,)), pltpu.SemaphoreType.DMA((7,)), pltpu.SemaphoreType.DMA(())],
    )(blk)


def _sems(*counts):
    return [pltpu.SemaphoreType.DMA((n,)) for n in counts]


class Comm(typing.NamedTuple):
    ins: list
    out_shapes: list
    n_sems: int
    start: typing.Callable
    finish: typing.Callable


def _remote(src, dst, send_sems, recv_sems, idx, to):
    return lambda: pltpu.make_async_remote_copy(src_ref=src, dst_ref=dst, send_sem=send_sems.at[idx],
                                                recv_sem=recv_sems.at[idx], device_id=to, device_id_type=MESH)


def _comm_from(copies, ins, out_shapes, n_sems):
    def start(*refs):
        for cp in copies(*refs)[0]:
            cp().start()

    def finish(*refs):
        sent, received = copies(*refs)
        for cp in received:
            cp().wait_recv()
        for cp in sent:
            cp().wait_send()

    return Comm(list(ins), list(out_shapes), n_sems, start, finish)


def _run_comm(comm, *, name, alias=False):
    n_in, n_out = len(comm.ins), len(comm.out_shapes)

    def body(*refs):
        ins, outs, sems = refs[:n_in], refs[n_in:n_in + n_out], refs[n_in + n_out:]
        comm.start(ins, outs, *sems)
        comm.finish(ins, outs, *sems)

    return pl.pallas_call(body, name=name, in_specs=[ANY] * n_in, out_specs=[ANY] * n_out, out_shape=comm.out_shapes,
                          input_output_aliases={q: q for q in range(n_in)} if alias else {},
                          scratch_shapes=_sems(comm.n_sems, comm.n_sems))(*comm.ins)


def _half(rows, c):
    h = rows // 2
    return pl.ds(pl.multiple_of(c * h, h), h)


def _gather_over_ici(ws):
    def copies(ins, outs, send_sems, recv_sems):
        x, y, c = _place()
        me_chip = 2 * x + y
        sent, received = [], []
        for q, w in enumerate(ws):
            half = _half(w.shape[0], c)
            for k, (px, py) in enumerate([(1 - x, y), (x, 1 - y), (1 - x, 1 - y)]):
                sent.append(_remote(ins[q].at[half], outs[q].at[me_chip, half], send_sems, recv_sems, 4 * q + k, (px, py, c)))
                slot = outs[q].at[2 * px + py, half]
                received.append(_remote(slot, slot, send_sems, recv_sems, 4 * q + k, (px, py, c)))
            whole = _remote(ins[q], outs[q].at[me_chip], send_sems, recv_sems, 4 * q + 3, (x, y, 1 - c))
            sent.append(whole)
            received.append(whole)
        return sent, received

    return _comm_from(copies, ws, [jax.ShapeDtypeStruct((N_CHIPS,) + w.shape, w.dtype) for w in ws], 4 * len(ws))


def _gather_over_d2d(parts):
    def copies(ins, outs, send_sems, recv_sems):
        x, y, c = _place()
        sent, received = [], []
        for q, w in enumerate(parts):
            for k, (px, py) in enumerate([(1 - x, y), (x, 1 - y), (1 - x, 1 - y)]):
                mine = outs[q].at[2 * px + py, _half(w.shape[1], c)]
                theirs = outs[q].at[2 * px + py, _half(w.shape[1], 1 - c)]
                sent.append(_remote(mine, mine, send_sems, recv_sems, 3 * q + k, (x, y, 1 - c)))
                received.append(_remote(theirs, theirs, send_sems, recv_sems, 3 * q + k, (x, y, 1 - c)))
        return sent, received

    return _comm_from(copies, parts, [jax.ShapeDtypeStruct(w.shape, w.dtype) for w in parts], 3 * len(parts))


def _to_sibling(gs, *, name):
    n = len(gs)

    def body(*refs):
        ins, outs = refs[:n], refs[n:2 * n]
        send_sems, recv_sems = refs[2 * n:]
        x, y, c = _place()
        cps = [pltpu.make_async_remote_copy(
            src_ref=ins[q], dst_ref=outs[q], send_sem=send_sems.at[q], recv_sem=recv_sems.at[q],
            device_id=(x, y, 1 - c), device_id_type=MESH) for q in range(n)]
        for cp in cps:
            cp.start()
        for cp in cps:
            cp.wait()

    return pl.pallas_call(body, name=name, in_specs=[ANY] * n, out_specs=[ANY] * n,
                          out_shape=[jax.ShapeDtypeStruct(g.shape, g.dtype) for g in gs],
                          scratch_shapes=_sems(n, n))(*gs)


def _chip_exchange(ps):
    def copies(ins, outs, send_sems, recv_sems):
        x, y, c = _place()
        cps = [_remote(ins[q].at[2 * px + py], outs[q].at[k], send_sems, recv_sems, 3 * q + k, (px, py, c))
               for q in range(len(ps)) for k, (px, py) in enumerate([(1 - x, y), (x, 1 - y), (1 - x, 1 - y)])]
        return cps, cps

    return _comm_from(copies, ps, [jax.ShapeDtypeStruct((3,) + p.shape[1:], p.dtype) for p in ps], 3 * len(ps))


def _sum_chips(own, r, *, name, ts=256):
    k, n = own.shape
    ts = min(ts, k)

    def body(own_ref, r_ref, o_ref):
        f = lambda q: r_ref[q].astype(F32)
        o_ref[...] = ((own_ref[...].astype(F32) + f(0)) + f(1)) + f(2)

    return pl.pallas_call(
        body, name=name, grid=(k // ts,),
        in_specs=[pl.BlockSpec((ts, n), lambda i: (i, 0)), pl.BlockSpec((3, ts, n), lambda i: (0, i, 0))],
        out_specs=pl.BlockSpec((ts, n), lambda i: (i, 0)), out_shape=jax.ShapeDtypeStruct((k, n), F32),
        compiler_params=_params('arbitrary'))(own, r)


WIN_SHARD = N_IN // N_CHIPS
WIN_PAD = -(-WIN_SHARD // LANES) * LANES
GATE_WIRE_ROWS = 32


def _full_layer(sh, axis):
    _, k, n = sh.shape
    if axis == 2:
        return sh.transpose(1, 0, 2).reshape(k, N_CHIPS * n)
    return sh.reshape(N_CHIPS * k, n)


def _win_cols(wp, o, n):
    parts = []
    while n > 0:
        j, r = divmod(o, WIN_SHARD)
        take = min(n, WIN_SHARD - r)
        parts.append(wp[:, j * WIN_PAD + r:j * WIN_PAD + r + take])
        o, n = o + take, n - take
    return parts[0] if len(parts) == 1 else jnp.concatenate(parts, axis=1)


def _split_full(full, axis):
    k, n = full.shape
    if axis == 2:
        return jnp.stack([full[:, j * (n // N_CHIPS):(j + 1) * (n // N_CHIPS)] for j in range(N_CHIPS)])
    return full.reshape(N_CHIPS, k // N_CHIPS, n)


def _padc(a, w):
    return jnp.pad(a, ((0, 0), (0, w - a.shape[1])))


def _swap16(a):
    return jnp.concatenate([a[..., 16:32], a[..., 0:16]], axis=-1)


B_GR, B_GQ, B_GK, B_GV, B_MQ, B_MKR, B_MKRS, B_FF, B_GLOW, B_MKV, B_END = (
    0, 512, 768, 1024, 1536, 1792, 1920, 2048, 2176, 2304, 2432)
B_W = 2560
O_FQ, O_FF, O_GQ, O_GLOW, O_GR, O_MQ, O_MKV, O_MKR, O_ZG = 0, 768, 772, 1796, 1812, 2324, 2580, 2708, 2740


def _repack_layer_weights(w):
    wi = functools.partial(_win_cols, w['w_in'])
    out = dict(w)
    out['in_a'] = wi(O_FQ, 768)
    kr = wi(O_MKR, 32)
    out['in_b'] = jnp.concatenate([
        wi(O_GR, 512), wi(O_GQ, 1024), wi(O_MQ, 256), jnp.tile(kr, (1, MLA_HEADS)), jnp.tile(_swap16(kr), (1, MLA_HEADS)),
        _padc(wi(O_FF, 4), 128), _padc(wi(O_GLOW, 16), 128), wi(O_MKV, 128),
        jnp.zeros((D_MODEL, B_W - B_END), kr.dtype)], axis=1)
    out['in_c'] = wi(O_ZG, 3072)
    uq = w['w_mla_uq'].reshape(MLA_Q_RANK, MLA_HEADS, MLA_NOPE + MLA_ROPE)
    rope = uq[:, :, MLA_NOPE:]
    out['uq'] = jnp.concatenate([uq[:, :, :MLA_NOPE].reshape(MLA_Q_RANK, -1), rope.reshape(MLA_Q_RANK, -1),
                                 _swap16(rope).reshape(MLA_Q_RANK, -1)], axis=1)
    ukv = w['w_mla_ukv'].reshape(MLA_KV_RANK, MLA_HEADS, MLA_NOPE + MLA_VD)
    out['ukv'] = jnp.concatenate([ukv[:, :, :MLA_NOPE].reshape(MLA_KV_RANK, -1),
                                  ukv[:, :, MLA_NOPE:].reshape(MLA_KV_RANK, -1)], axis=1)
    out['gate'] = jnp.pad(w['w_gla_gate'], ((0, 128 - GLA_RANK), (0, 0)))
    return out


def _unpack_layer_grads(g):
    a, b, c = g['in_a'], g['in_b'], g['in_c']
    fold = lambda o: sum(b[:, o + MLA_ROPE * q:o + MLA_ROPE * (q + 1)] for q in range(MLA_HEADS))
    kr = fold(B_MKR) + _swap16(fold(B_MKRS))
    w_in = jnp.concatenate([a, b[:, B_FF:B_FF + 4], b[:, B_GQ:B_GQ + 1024], b[:, B_GLOW:B_GLOW + 16],
                            b[:, B_GR:B_GR + 512], b[:, B_MQ:B_MQ + 256], b[:, B_MKV:B_MKV + 128], kr, c], axis=1)
    uq = g['uq']
    nope = uq[:, :256].reshape(MLA_Q_RANK, MLA_HEADS, MLA_NOPE)
    rope = (uq[:, 256:384].reshape(MLA_Q_RANK, MLA_HEADS, MLA_ROPE)
            + _swap16(uq[:, 384:512].reshape(MLA_Q_RANK, MLA_HEADS, MLA_ROPE)))
    w_uq = jnp.concatenate([nope, rope], axis=2).reshape(MLA_Q_RANK, -1)
    ukv = g['ukv']
    w_ukv = jnp.concatenate([ukv[:, :256].reshape(MLA_KV_RANK, MLA_HEADS, MLA_NOPE),
                             ukv[:, 256:].reshape(MLA_KV_RANK, MLA_HEADS, MLA_VD)], axis=2).reshape(MLA_KV_RANK, -1)
    out = {'w_in': w_in, 'w_mla_uq': w_uq, 'w_mla_ukv': w_ukv, 'w_gla_gate': g['gate'][:GLA_RANK]}
    for nm in ('w_up_fox', 'w_up_gla', 'w_up_mla', 'w_out', 'w_xq', 'w_xkv', 'w_xo', 'w_mlp1', 'w_mlp2'):
        out[nm] = g[nm]
    return out


def _rope_tables(s):
    half = MLA_ROPE // 2
    inv = ROPE_BASE ** (-jnp.arange(half, dtype=F32) / half)
    ang = jnp.arange(s).astype(F32)[:, None] * inv[None, :]
    cos, sin = jnp.cos(ang), jnp.sin(ang)
    c1 = jnp.concatenate([cos, cos], axis=1)
    s1 = jnp.concatenate([-sin, sin], axis=1)
    return jnp.tile(c1, (1, MLA_HEADS)), jnp.tile(s1, (1, MLA_HEADS))


def _rms_bwd(x, dh, g):
    r = lax.rsqrt(jnp.mean(x * x, axis=-1, keepdims=True) + EPS)
    xh = x * r
    gd = dh * g
    return r * (gd - xh * jnp.mean(gd * xh, axis=-1, keepdims=True)), dh * xh


def _norm_bwd_call(x, dh, g, dres, name):
    w = x.width if isinstance(x, Cols) else x.shape[1]

    def with_res(xv, dv, rv, gv):
        dx, dg = _rms_bwd(xv, dv.astype(F32), gv)
        return rv + dx, dg

    def plain(xv, dv, gv):
        return _rms_bwd(xv, dv.astype(F32), gv)

    if dres is None:
        return _rowwise(plain, [x, dh], [g], [(w, F32)], [w], name=name)
    return _rowwise(with_res, [x, dh, dres], [g], [(w, F32)], [w], name=name)


def _gla_out_fwd(oraw, gr, g_out):
    outs = []
    for hh in range(GLA_HEADS):
        sl = slice(hh * GLA_DV, (hh + 1) * GLA_DV)
        oh = oraw[:, sl]
        n = oh * lax.rsqrt(jnp.mean(oh * oh, axis=-1, keepdims=True) + EPS) * g_out
        r = gr[:, sl]
        outs.append(n * (r * _sig(r)))
    return (jnp.concatenate(outs, axis=1),)


def _gla_out_bwd(oraw, gr, dout, g_out):
    d_o, d_r, dg = [], [], 0.0
    for hh in range(GLA_HEADS):
        sl = slice(hh * GLA_DV, (hh + 1) * GLA_DV)
        oh, r, do = oraw[:, sl], gr[:, sl], dout[:, sl].astype(F32)
        rs = lax.rsqrt(jnp.mean(oh * oh, axis=-1, keepdims=True) + EPS)
        sg = _sig(r)
        dn = do * (r * sg)
        d_r.append(do * (oh * rs * g_out) * (sg + r * sg * (1.0 - sg)))
        dx, dgh = _rms_bwd(oh, dn, g_out)
        d_o.append(dx)
        dg = dg + dgh
    return jnp.concatenate(d_o, axis=1), jnp.concatenate(d_r, axis=1), dg


def _adam(w, g, m, v):
    m = ADAM_B1 * m + (1.0 - ADAM_B1) * g
    v = ADAM_B2 * v + (1.0 - ADAM_B2) * (g * g)
    m_hat = m / (1.0 - ADAM_B1 ** ADAM_STEP)
    v_hat = v / (1.0 - ADAM_B2 ** ADAM_STEP)
    return -ADAM_LR * (m_hat / (jnp.sqrt(v_hat) + ADAM_EPS) + ADAM_WD * w), m, v


def _layer_fwd(x, mem, w, p, tabs, tag, carry=None):
    c4, s4 = tabs
    sv = {'x0': x}
    nm = lambda t: f'{t}_{tag}'
    za, h = _mm(x, w['in_a'], mode='nn', out_dtype=BF16, norm_g=p['g_mix'], emit_norm=True, name=nm('in_a'))
    zb = _mm(h, w['in_b'], mode='nn', out_dtype=F32, name=nm('in_b'))
    zc = _mm(h, w['in_c'], mode='nn', out_dtype=F32, name=nm('in_c'))
    sv.update(h=h, zc=zc)
    ff = Cols(zb, 128, B_FF // 128)
    (lf,) = _rowwise(lambda f, b: (_logsig(f + b),), [ff], [p['b_fox']], [(128, F32)], name=nm('fox_lf'))
    cum = _cumsum_rows(lf, reverse=False, name=nm('fox_cum'))
    ckf = jnp.pad(cum[:, :FOX_HEADS].T.reshape(2, 2, x.shape[0]), ((0, 0), (0, 6), (0, 0)))
    fox = dict(qc=0, kc=2, vc=4, nb=2, g=2, scale=FOX_HD ** -0.5, mode='causal', ck=ckf)
    if carry is None:
        o_fox, lse_fox = _mattn_fwd(za, za, za, name=nm('fox_attn'), **fox)
    else:
        o_fox, lse_fox, sv['carried'] = _mattn_fwd(za, za, za, name=nm('fox_attn'), comm=carry, **fox)
    sv.update(ff=ff, za=za, fox=fox, o_fox=o_fox, lse_fox=lse_fox)
    glow = Cols(zb, 128, B_GLOW // 128)
    gr = Cols(zb, 512, B_GR // 512)

    def gate_fn(gl, wg, bg):
        return (_logsig(_dot(gl.astype(BF16), wg) + bg) / GLA_TAU,)

    (la,) = _rowwise(gate_fn, [glow], [w['gate'], p['b_gla']], [(256, F32)], name=nm('gla_gate'))
    gla = dict(qc=B_GQ // LANES, kc=B_GK // LANES, vc=B_GV // LANES)
    oraw, states = _gla_fwd(zb, la, name=nm('gla'), **gla)
    (o_gla,) = _rowwise(_gla_out_fwd, [oraw, gr], [p['g_gla_out']], [(512, BF16)], name=nm('gla_out'))
    sv.update(glow=glow, gr=gr, zb=zb, la=la, gla=gla, states=states, oraw=oraw, o_gla=o_gla)
    mq = Cols(zb, 256, B_MQ // 256)
    mkv = Cols(zb, 128, B_MKV // 128)
    mkr2 = Cols(zb, 256, B_MKR // 256)
    qp, cqn = _mm(mq, w['uq'], mode='nn', out_dtype=F32, norm_g=p['g_mla_q'], emit_norm=True, name=nm('mla_uq'))
    kvp, ckvn = _mm(mkv, w['ukv'], mode='nn', out_dtype=BF16, norm_g=p['g_mla_kv'], emit_norm=True,
                    name=nm('mla_ukv'))

    def rope_fn(qv, kr, c4v, s4v):
        q_rope = qv[:, 256:384] * c4v + qv[:, 384:512] * s4v
        return jnp.concatenate([qv[:, 0:256], q_rope], axis=1), kr[:, 0:128] * c4v + kr[:, 128:256] * s4v

    qall, kr4 = _rowwise(rope_fn, [qp, mkr2, c4, s4], [], [(384, BF16), (128, BF16)], name=nm('rope'))
    mla = dict(qc=0, kc=0, vc=2, nb=2, g=2, scale=(MLA_NOPE + MLA_ROPE) ** -0.5, mode='chunk', qr=qall, qrc=2, kr=kr4)
    o_mla, lse_mla = _mattn_fwd(qall, kvp, kvp, name=nm('mla_attn'), **mla)
    sv.update(mq=mq, mkv=mkv, cqn=cqn, ckvn=ckvn, qall=qall, kvp=kvp, mla=mla, o_mla=o_mla, lse_mla=lse_mla)
    of_m, om_m = o_fox, o_mla
    sv.update(of_m=of_m, om_m=om_m)
    b_br = p['b_branch']

    def first(acc, zg, bb):
        return _sig(zg + bb) * acc

    def more(acc, zg, bb, prev):
        return prev + _sig(zg + bb) * acc

    y = _mm(of_m, w['w_up_fox'], mode='nn', out_dtype=F32, name=nm('up_fox'), epilogue=first,
            extras=[(zc, *_mn(col_off=0)), (b_br, *_nvec(col_off=0))])
    y = _mm(o_gla, w['w_up_gla'], mode='nn', out_dtype=F32, name=nm('up_gla'), epilogue=more,
            extras=[(zc, *_mn(col_off=1024)), (b_br, *_nvec(col_off=1024)), (y, *_mn())])
    y = _mm(om_m, w['w_up_mla'], mode='nn', out_dtype=BF16, name=nm('up_mla'), epilogue=more,
            extras=[(zc, *_mn(col_off=2048)), (b_br, *_nvec(col_off=2048)), (y, *_mn())])
    add = lambda acc, res: res + acc
    x1 = _mm(y, w['w_out'], mode='nn', out_dtype=F32, name=nm('out'), epilogue=add, extras=[(x, *_mn())])
    sv.update(y=y, x1=x1)
    qx, hx = _mm(x1, w['w_xq'], mode='nn', out_dtype=BF16, norm_g=p['g_xa'], emit_norm=True, name=nm('xq'))
    kvx, mn = _mm(mem, w['w_xkv'], mode='nn', out_dtype=BF16, norm_g=p['g_mem'], emit_norm=True, name=nm('xkv'))
    xa = dict(qc=0, kc=0, vc=4, nb=4, g=1, scale=XA_HD ** -0.5, mode='full')
    ox_m, lse_x = _mattn_fwd(qx, kvx, kvx, name=nm('xa_attn'), **xa)
    x2 = _mm(ox_m, w['w_xo'], mode='nn', out_dtype=F32, name=nm('xo'), epilogue=add, extras=[(x1, *_mn())])
    sv.update(hx=hx, mn=mn, qx=qx, kvx=kvx, xa=xa, lse_x=lse_x, ox_m=ox_m, x2=x2)
    hpre, hm = _mm(x2, w['w_mlp1'], mode='nn', out_dtype=BF16, norm_g=p['g_mlp'], emit_norm=True, name=nm('mlp1'))
    relu2 = lambda t: jnp.square(jnp.maximum(t.astype(F32), 0.0))
    x3 = _mm(hpre, w['w_mlp2'], mode='nn', out_dtype=F32, name=nm('mlp2'), a_fn=relu2, epilogue=add,
             extras=[(x2, *_mn())])
    sv.update(hpre=hpre, hm=hm)
    return x3, sv


def _layer_bwd(dx3, mem, w, p, tabs, sv, tag, carry=None):
    c4, s4 = tabs
    nm = lambda t: f'{t}_{tag}'
    s = dx3.shape[0]
    gw, gs = {}, {}
    relu2 = lambda t: jnp.square(jnp.maximum(t.astype(F32), 0.0))
    gw['w_mlp2'] = _mm(sv['hpre'], dx3, mode='tn', out_dtype=F32, name=nm('d_mlp2'), a_fn=relu2)
    dact = lambda acc, hp: acc * (2.0 * jnp.maximum(hp.astype(F32), 0.0))
    dhpre = _mm(dx3, w['w_mlp2'], mode='nt', out_dtype=BF16, name=nm('d_act'), epilogue=dact,
                extras=[(sv['hpre'], *_mn())])
    gw['w_mlp1'] = _mm(sv['hm'], dhpre, mode='tn', out_dtype=F32, name=nm('d_mlp1'))
    dhm = _mm(dhpre, w['w_mlp1'], mode='nt', out_dtype=F32, name=nm('d_hm'))
    dx2, gs['g_mlp'] = _norm_bwd_call(sv['x2'], dhm, p['g_mlp'], dx3, nm('d_norm_mlp'))
    gw['w_xo'] = _mm(sv['ox_m'], dx2, mode='tn', out_dtype=F32, name=nm('d_xo'))
    dox = _mm(dx2, w['w_xo'], mode='nt', out_dtype=BF16, name=nm('d_ox'))
    dqx_m, dkx, dvx = _mattn_bwd(sv['qx'], sv['kvx'], sv['kvx'], sv['ox_m'], dox, sv['lse_x'], name=nm('xa_bwd'),
                                 **sv['xa'])
    dkvx = jnp.concatenate([dkx, dvx], axis=1).astype(BF16)
    gw['w_xq'] = _mm(sv['hx'], dqx_m, mode='tn', out_dtype=F32, name=nm('d_xq'))
    dhx = _mm(dqx_m, w['w_xq'], mode='nt', out_dtype=F32, name=nm('d_hx'))
    gw['w_xkv'] = _mm(sv['mn'], dkvx, mode='tn', out_dtype=F32, name=nm('d_xkv'))
    dmn = _mm(dkvx, w['w_xkv'], mode='nt', out_dtype=F32, name=nm('d_mn'))
    _, gs['g_mem'] = _norm_bwd_call(mem, dmn, p['g_mem'], None, nm('d_norm_mem'))
    dx1, gs['g_xa'] = _norm_bwd_call(sv['x1'], dhx, p['g_xa'], dx2, nm('d_norm_xa'))
    gw['w_out'] = _mm(sv['y'], dx1, mode='tn', out_dtype=F32, name=nm('d_out'))
    dy = _mm(dx1, w['w_out'], mode='nt', out_dtype=BF16, name=nm('d_y'))
    zc, b_br = sv['zc'], p['b_branch']

    def du_fn(dyv, zg, bb):
        g = _sig(zg + bb)
        d = dyv.astype(F32)
        return d * g[:, 0:1024], d * g[:, 1024:2048], d * g[:, 2048:3072]

    du = _rowwise(du_fn, [dy, zc], [b_br], [(D_MODEL, BF16)] * 3, name=nm('d_u'))

    def dgate(acc, dyv, zg, bb):
        g = _sig(zg + bb)
        return dyv.astype(F32) * acc * g * (1.0 - g)

    dzc, do_br = [], []
    for q, (o_m, wn) in enumerate(((sv['of_m'], 'w_up_fox'), (sv['o_gla'], 'w_up_gla'), (sv['om_m'], 'w_up_mla'))):
        dzc.append(_mm(o_m, w[wn], mode='nn', out_dtype=F32, name=nm(f'd_zg{q}'), epilogue=dgate,
                       extras=[(dy, *_mn()), (zc, *_mn(col_off=1024 * q)), (b_br, *_nvec(col_off=1024 * q))]))
        gw[wn] = _mm(o_m, du[q], mode='tn', out_dtype=F32, name=nm(f'd_up{q}'))
        do_br.append(_mm(du[q], w[wn], mode='nt', out_dtype=F32 if q == 1 else BF16, name=nm(f'd_o{q}')))
    dzc = jnp.concatenate(dzc, axis=1)
    (gs['b_branch'],) = _rowwise(lambda t: (t,), [dzc], [], [], [3072], name=nm('d_bbranch'))
    za = sv['za']
    dfq, dfk, dfv, dck, dcq = _mattn_bwd(za, za, za, sv['o_fox'], do_br[0], sv['lse_fox'], name=nm('fox_bwd'),
                                         **sv['fox'])
    dcum = _padc(dck[:, :2, :].reshape(FOX_HEADS, s).T + dcq.reshape(s, 2, LANES)[:, :, :2].reshape(s, FOX_HEADS), 128)
    dlf = _cumsum_rows(dcum, reverse=True, name=nm('fox_dcum'))

    def dff_fn(dl, f, b):
        d = dl * _sig(-(f + b))
        return d, d

    dff, db_fox = _rowwise(dff_fn, [dlf, sv['ff']], [p['b_fox']], [(128, F32)], [128], name=nm('fox_dff'))
    gs['b_fox'] = db_fox
    dza = jnp.concatenate([dfq, dfk, dfv], axis=1).astype(BF16)
    dqn, dkn, dvv, dq_rope, dk_rope, *carried = _mattn_bwd(sv['qall'], sv['kvp'], sv['kvp'], sv['o_mla'], do_br[2],
                                                           sv['lse_mla'], name=nm('mla_bwd'), comm=carry, **sv['mla'])

    def drope_fn(dn, dq, dk, c4v, s4v):
        return jnp.concatenate([dn, dq * c4v, dq * s4v], axis=1), jnp.concatenate([dk * c4v, dk * s4v], axis=1)

    dqp, dmkr2 = _rowwise(drope_fn, [dqn, dq_rope, dk_rope, c4, s4], [], [(512, BF16), (256, BF16)], name=nm('d_rope'))
    dkvp = jnp.concatenate([dkn, dvv], axis=1).astype(BF16)
    gw['uq'] = _mm(sv['cqn'], dqp, mode='tn', out_dtype=F32, name=nm('d_uq'))
    dcqn = _mm(dqp, w['uq'], mode='nt', out_dtype=F32, name=nm('d_cqn'))
    gw['ukv'] = _mm(sv['ckvn'], dkvp, mode='tn', out_dtype=F32, name=nm('d_ukv'))
    dckvn = _mm(dkvp, w['ukv'], mode='nt', out_dtype=F32, name=nm('d_ckvn'))
    dmq, gs['g_mla_q'] = _norm_bwd_call(sv['mq'], dcqn, p['g_mla_q'], None, nm('d_norm_q'))
    dmkv, gs['g_mla_kv'] = _norm_bwd_call(sv['mkv'], dckvn, p['g_mla_kv'], None, nm('d_norm_kv'))
    doraw, dgr, gs['g_gla_out'] = _rowwise(_gla_out_bwd, [sv['oraw'], sv['gr'], do_br[1]], [p['g_gla_out']],
                                           [(512, F32), (512, BF16)], [128], name=nm('d_gla_out'))
    st = sv['states']
    st_prev = jnp.concatenate([jnp.zeros_like(st[:, :1]), st[:, :-1]], axis=1)
    dgq, dgk, dgv, dla = _gla_bwd(sv['zb'], sv['la'], st, st_prev, doraw, name=nm('gla_bwd'), **sv['gla'])

    def dgate_fn(dl, gl, wg, bg):
        pre = _dot(gl.astype(BF16), wg) + bg
        dpre = dl * (1.0 / GLA_TAU) * _sig(-pre)
        return dpre, _dot(dpre.astype(BF16), wg, NT), dpre

    dpre, dglow, gs['b_gla'] = _rowwise(dgate_fn, [dla, sv['glow']], [w['gate'], p['b_gla']],
                                        [(256, BF16), (128, BF16)], [256], name=nm('d_gla_gate'))
    gw['gate'] = _mm(sv['glow'], dpre, mode='tn', out_dtype=F32, name=nm('d_wgate'))
    bf = lambda t: t.astype(BF16)
    dzb = jnp.concatenate([dgr, bf(dgq), bf(dgk), bf(dgv), bf(dmq), dmkr2, bf(dff), dglow, bf(dmkv),
                           jnp.zeros((s, B_W - B_END), BF16)], axis=1)
    h = sv['h']
    gw['in_a'] = _mm(h, dza, mode='tn', out_dtype=F32, name=nm('d_in_a'))
    gw['in_b'] = _mm(h, dzb, mode='tn', out_dtype=F32, name=nm('d_in_b'))
    gw['in_c'] = _mm(h, dzc, mode='tn', out_dtype=F32, name=nm('d_in_c'))
    add = lambda acc, prev: prev + acc
    dh = _mm(dza, w['in_a'], mode='nt', out_dtype=F32, name=nm('d_h_a'))
    dh = _mm(dzb, w['in_b'], mode='nt', out_dtype=F32, name=nm('d_h_b'), epilogue=add, extras=[(dh, *_mn())])
    dh = _mm(dzc, w['in_c'], mode='nt', out_dtype=F32, name=nm('d_h_c'), epilogue=add, extras=[(dh, *_mn())])
    dx0, gs['g_mix'] = _norm_bwd_call(sv['x0'], dh, p['g_mix'], dx1, nm('d_norm_mix'))
    return (dx0, gw, gs) if carry is None else (dx0, gw, gs, carried[0])


def _loss_head(x, target, g_final):
    d = x.shape[1]

    def fn(xv, tv, gv):
        r = lax.rsqrt(jnp.mean(xv * xv, axis=-1, keepdims=True) + EPS)
        xh = xv * r
        e = xh * gv - tv
        dy = e * (1.0 / d)
        gd = dy * gv
        dx = r * (gd - xh * jnp.mean(gd * xh, axis=-1, keepdims=True))
        row_loss = 0.5 * jnp.mean(e * e, axis=-1, keepdims=True)
        return dx, dy * xh, jnp.broadcast_to(row_loss, (xv.shape[0], LANES))

    return _rowwise(fn, [x, target], [g_final], [(d, F32)], [d, LANES], name='loss_head')


def _small_sizes(shapes):
    return [math.prod(shapes[nm]) for nm in SMALL]


def _step(args):
    shapes = {nm: args[nm].shape for nm in ORDER}
    x, mem, target = args['x'][0], args['mem'][0], args['loss_target'][0]
    s = x.shape[0]

    def wire(nm, l):
        w = args[nm][l].astype(BF16)
        if nm == 'w_in':
            w = jnp.pad(w, ((0, 0), (0, WIN_PAD - WIN_SHARD)))
        if nm == 'w_gla_gate':
            w = jnp.pad(w, ((0, GATE_WIRE_ROWS - GLA_RANK), (0, 0)))
        return w

    def whole_weights(parts):
        full = {nm: _full_layer(p, ax) for (nm, ax), p in zip(BIG, parts)}
        full['w_gla_gate'] = full['w_gla_gate'][:GLA_RANK]
        return _repack_layer_weights(full)

    def gather_rest(parts, l):
        return whole_weights(_run_comm(_gather_over_d2d(parts), name=f'gather_d2d_l{l}', alias=True))

    tabs = _rope_tables(s)
    layers_p = []
    for l in range(DEPTH):
        layers_p.append({
            'g_mix': args['g_mix'][l][None], 'b_fox': _padc(args['b_fox_forget'][l][None], 128),
            'b_gla': args['b_gla_gate'][l][None], 'g_gla_out': args['g_gla_out'][l][None],
            'g_mla_q': args['g_mla_q'][l][None], 'g_mla_kv': args['g_mla_kv'][l][None],
            'b_branch': args['b_branch_gate'][l][None], 'g_xa': args['g_xa'][l][None],
            'g_mem': args['g_mem'][l][None], 'g_mlp': args['g_mlp'][l][None]})

    wires = lambda l: [wire(nm, l) for nm, _ in BIG]
    parts = _run_comm(_gather_over_ici(wires(0)), name='gather_ici_l0')
    saved, layers_w = [], []
    xl = x
    for l in range(DEPTH):
        layers_w.append(gather_rest(parts, l))
        carry = _gather_over_ici(wires(l + 1)) if l + 1 < DEPTH else None
        xl, sv = _layer_fwd(xl, mem, layers_w[l], layers_p[l], tabs, f'l{l}', carry=carry)
        parts = sv.pop('carried', None)
        saved.append(sv)
    dx, dg_final, loss_lanes = _loss_head(xl, target, args['g_final'][None])
    names = [nm for nm, _ in BIG]
    cidx = lax.axis_index('c')
    chip = 2 * lax.axis_index('x') + lax.axis_index('y')

    def pair_sums(gw, tag):
        mine, theirs = [], []
        for nm, ax in BIG:
            shards = _split_full(gw[nm], ax).astype(BF16)
            h = shards.shape[1] // 2
            mine.append(lax.dynamic_slice_in_dim(shards, cidx * h, h, axis=1))
            theirs.append(lax.dynamic_slice_in_dim(shards, (1 - cidx) * h, h, axis=1))
        got = _to_sibling(theirs, name=f'grads_swap_{tag}')
        pairs = []
        for nm, a, b in zip(names, mine, got):
            _, h, n = a.shape
            (p,) = _rowwise(lambda u, v: (u.astype(F32) + v.astype(F32),),
                            [a.reshape(N_CHIPS * h, n), b.reshape(N_CHIPS * h, n)], [], [(n, BF16)],
                            name=f'pair_sum_{nm}_{tag}')
            pairs.append(p.reshape(N_CHIPS, h, n))
        return pairs

    def finish(pairs, from_chips, tag):
        own = [lax.dynamic_index_in_dim(p, chip, axis=0, keepdims=False) for p in pairs]
        mine = [_sum_chips(o, r, name=f'chip_sum_{nm}_{tag}') for nm, o, r in zip(names, own, from_chips)]
        theirs = _to_sibling(mine, name=f'grads_join_{tag}')
        return {nm: jnp.where(cidx == 0, jnp.concatenate([a, b]), jnp.concatenate([b, a]))
                for nm, a, b in zip(names, mine, theirs)}

    gs_layers, pairs, from_chips = [None] * DEPTH, [None] * DEPTH, [None] * DEPTH
    carry = None
    for l in reversed(range(DEPTH)):
        res = _layer_bwd(dx, mem, layers_w[l], layers_p[l], tabs, saved[l], f'l{l}', carry=carry)
        dx, gw, gs_layers[l] = res[:3]
        if carry is not None:
            from_chips[l + 1] = res[3]
        pairs[l] = pair_sums(_unpack_layer_grads(gw), f'l{l}')
        carry = _chip_exchange(pairs[l]) if l > 0 else None
    from_chips[0] = _run_comm(_chip_exchange(pairs[0]), name='grads_exchange_l0')
    grad_x = dx[None]
    done = [finish(pairs[l], from_chips[l], f'l{l}') for l in range(DEPTH)]
    gshard = {nm: jnp.stack([done[l][nm] for l in range(DEPTH)]) for nm in names}

    small_g = []
    for nm, key in (('g_mix', 'g_mix'), ('b_fox_forget', 'b_fox'), ('b_gla_gate', 'b_gla'),
                    ('g_gla_out', 'g_gla_out'), ('g_mla_q', 'g_mla_q'), ('g_mla_kv', 'g_mla_kv'),
                    ('b_branch_gate', 'b_branch'), ('g_xa', 'g_xa'), ('g_mem', 'g_mem'), ('g_mlp', 'g_mlp')):
        width = shapes[nm][1]
        small_g.append(jnp.concatenate([gs_layers[l][key][0, :width] for l in range(DEPTH)]))
    small_g.append(dg_final[0])
    small_g.append(loss_lanes[0, :1])
    flat = jnp.concatenate(small_g)
    n_small = flat.shape[0]
    srows = -(-n_small // (8 * LANES)) * 8
    pad = lambda v: jnp.pad(v, (0, srows * LANES - v.shape[0])).reshape(srows, LANES)
    all_small = _all_gather8(pad(flat), name='gather_small')
    sw, sm, svv = (pad(jnp.concatenate([args[pre + nm].reshape(-1) for nm in SMALL] + [jnp.zeros((1,), F32)]))
                   for pre in ('', 'm_', 'v_'))

    def small_body(g_ref, w_ref, m_ref, v_ref, go_ref, d_ref, mo_ref, vo_ref):
        g = g_ref[0]
        for q in range(1, N_DEV):
            g = g + g_ref[q]
        go_ref[...] = g
        d_ref[...], mo_ref[...], vo_ref[...] = _adam(w_ref[...], g, m_ref[...], v_ref[...])

    sg, sd, snm, snv = pl.pallas_call(
        small_body, name='small_sum_adam', out_shape=[jax.ShapeDtypeStruct((srows, LANES), F32)] * 4,
        compiler_params=pltpu.CompilerParams(vmem_limit_bytes=VMEM_LIMIT))(all_small, sw, sm, svv)

    def unsmall(buf):
        v, out, off = buf.reshape(-1), {}, 0
        for nm in SMALL:
            nel = math.prod(shapes[nm])
            out[nm] = v[off:off + nel].reshape(shapes[nm])
            off += nel
        return out, v[off]

    res = {}
    (res['grad'], loss), (res['delta'], _), (res['m'], _), (res['v'], _) = (unsmall(t) for t in (sg, sd, snm, snv))

    for nm, _ in BIG:
        shp = args[nm].shape
        view = lambda t: t.reshape(shp[0] * shp[1], shp[2])
        d, m2, v2 = _rowwise(_adam, [view(args[nm]), view(gshard[nm]), view(args['m_' + nm]), view(args['v_' + nm])],
                             [], [(shp[2], F32)] * 3, name=f'adam_{nm}')
        res['grad'][nm], res['delta'][nm], res['m'][nm], res['v'][nm] = (
            gshard[nm], d.reshape(shp), m2.reshape(shp), v2.reshape(shp))

    return (loss, grad_x, *[res['grad'][nm] for nm in ORDER], *[res['delta'][nm] for nm in ORDER],
            *[res['m'][nm] for nm in ORDER], *[res['v'][nm] for nm in ORDER])


def kernel(x, mem, g_mix, w_in, b_fox_forget, w_gla_gate, b_gla_gate, g_gla_out, g_mla_q, w_mla_uq, g_mla_kv, w_mla_ukv, b_branch_gate, w_up_fox, w_up_gla, w_up_mla, w_out, g_xa, g_mem, w_xq, w_xkv, w_xo, g_mlp, w_mlp1, w_mlp2, g_final, loss_target, m_g_mix, m_w_in, m_b_fox_forget, m_w_gla_gate, m_b_gla_gate, m_g_gla_out, m_g_mla_q, m_w_mla_uq, m_g_mla_kv, m_w_mla_ukv, m_b_branch_gate, m_w_up_fox, m_w_up_gla, m_w_up_mla, m_w_out, m_g_xa, m_g_mem, m_w_xq, m_w_xkv, m_w_xo, m_g_mlp, m_w_mlp1, m_w_mlp2, m_g_final, v_g_mix, v_w_in, v_b_fox_forget, v_w_gla_gate, v_b_gla_gate, v_g_gla_out, v_g_mla_q, v_w_mla_uq, v_g_mla_kv, v_w_mla_ukv, v_b_branch_gate, v_w_up_fox, v_w_up_gla, v_w_up_mla, v_w_out, v_g_xa, v_g_mem, v_w_xq, v_w_xkv, v_w_xo, v_g_mlp, v_w_mlp1, v_w_mlp2, v_g_final):
    return _step(dict(locals()))
```

```python
import functools
import math
import typing

import jax
import jax.numpy as jnp
from jax import lax
from jax.experimental import pallas as pl
from jax.experimental.pallas import tpu as pltpu

F32 = jnp.float32
BF16 = jnp.bfloat16
MESH = pl.DeviceIdType.MESH

D_MODEL = 1024
DEPTH = 2
CHUNK = 64
EPS = 1e-6
FOX_HEADS, FOX_HD = 4, 64
GLA_HEADS, GLA_DK, GLA_DV, GLA_RANK, GLA_TAU = 4, 64, 128, 16, 16.0
MLA_HEADS, MLA_Q_RANK, MLA_KV_RANK, MLA_NOPE, MLA_ROPE, MLA_VD = 4, 256, 128, 64, 32, 64
ROPE_BASE = 10000.0
XA_HEADS, XA_HD = 4, 128
D_FF = 4 * D_MODEL
IN_SIZES = (256, 256, 256, 4, 256, 256, 512, 16, 512, 256, 128, 32, 3072)
N_IN = sum(IN_SIZES)

ADAM_LR, ADAM_B1, ADAM_B2, ADAM_EPS, ADAM_WD, ADAM_STEP = 0.001, 0.9, 0.999, 1e-08, 0.01, 10

N_CHIPS = 4
N_DEV = 8
LANES = 128
VMEM_LIMIT = 48 * 1024 * 1024
MASK_VALUE = -1e30

BIG = (('w_in', 2), ('w_gla_gate', 2), ('w_mla_uq', 2), ('w_mla_ukv', 2), ('w_up_fox', 2), ('w_up_gla', 2),
       ('w_up_mla', 2), ('w_out', 1), ('w_xq', 1), ('w_xkv', 1), ('w_xo', 2), ('w_mlp1', 2), ('w_mlp2', 1))
SMALL = ('g_mix', 'b_fox_forget', 'b_gla_gate', 'g_gla_out', 'g_mla_q', 'g_mla_kv', 'b_branch_gate',
         'g_xa', 'g_mem', 'g_mlp', 'g_final')
ORDER = ('g_mix', 'w_in', 'b_fox_forget', 'w_gla_gate', 'b_gla_gate', 'g_gla_out', 'g_mla_q', 'w_mla_uq',
         'g_mla_kv', 'w_mla_ukv', 'b_branch_gate', 'w_up_fox', 'w_up_gla', 'w_up_mla', 'w_out', 'g_xa', 'g_mem',
         'w_xq', 'w_xkv', 'w_xo', 'g_mlp', 'w_mlp1', 'w_mlp2', 'g_final')


def _params(*sem):
    return pltpu.CompilerParams(dimension_semantics=sem, vmem_limit_bytes=VMEM_LIMIT)


def _sig(x):
    return 1.0 / (1.0 + jnp.exp(-x))


def _logsig(x):
    return jnp.minimum(x, 0.0) - jnp.log(1.0 + jnp.exp(-jnp.abs(x)))


NN = (((1,), (0,)), ((), ()))
NT = (((1,), (1,)), ((), ()))
TN = (((0,), (0,)), ((), ()))


def _dot(a, b, dims=NN):
    return lax.dot_general(a, b, dims, preferred_element_type=F32)


class Cols(typing.NamedTuple):
    arr: jax.Array
    width: int
    blk: int


def _tri_dot(tri, x):
    hi = x.astype(BF16)
    r1 = x - hi.astype(F32)
    mid = r1.astype(BF16)
    lo = (r1 - mid.astype(F32)).astype(BF16)
    return _dot(tri, hi) + _dot(tri, mid) + _dot(tri, lo)


MM_TILES = ((1024, 1024), (1024, 512), (512, 1024), (512, 512), (512, 256), (256, 512), (256, 256), (128, 128))
MM_VMEM_BUDGET = 38 * 1024 * 1024


def _mm_tiles(m, n, k, a_bytes, b_bytes, out_bytes, ex_bytes, has_norm, emit_norm, has_fn):
    for tm, tn in MM_TILES:
        tm, tn = min(tm, m), min(tn, n)
        if m % tm or n % tn:
            continue
        blocks = tm * k * a_bytes + k * tn * b_bytes + tm * tn * (out_bytes + ex_bytes) + (tm * k * 2 if emit_norm else 0)
        temps = tm * tn * 4 + (tm * k * 2 if has_norm else 0) + (tm * k * 6 if has_fn or has_norm else 0)
        if 2 * blocks + temps <= MM_VMEM_BUDGET:
            return tm, tn
    raise ValueError((m, n, k))


def _mm(a, b, *, mode, out_dtype, name, norm_g=None, emit_norm=False, a_fn=None, extras=(), epilogue=None):
    a_blk = 0
    if isinstance(a, Cols):
        a, width, a_blk = a
        a_shape = (a.shape[0], width)
    else:
        a_shape = a.shape
    if mode == 'tn':
        k, m = a_shape
    else:
        m, k = a_shape
    n = b.shape[0] if mode == 'nt' else b.shape[1]
    assert (b.shape[1] if mode == 'nt' else b.shape[0]) == k, (name, a.shape, b.shape)
    has_norm = norm_g is not None
    ex_bytes = sum(arr.dtype.itemsize for arr, kind, _ in extras if kind == 'mn')
    tm, tn = _mm_tiles(m, n, k, a.dtype.itemsize, b.dtype.itemsize, jnp.dtype(out_dtype).itemsize, ex_bytes, has_norm,
                       emit_norm, a_fn is not None)
    assert all(col % tn == 0 for _, _, col in extras), (name, tn)
    assert a_blk == 0 or (mode == 'nn') or (mode == 'tn' and tm == m)
    if mode == 'tn':
        a_spec = pl.BlockSpec((k, tm), lambda i, j: (0, i + a_blk))
    else:
        a_spec = pl.BlockSpec((tm, k), lambda i, j: (i, a_blk))
    b_spec = pl.BlockSpec((tn, k), lambda i, j: (j, 0)) if mode == 'nt' else pl.BlockSpec((k, tn), lambda i, j: (0, j))
    dims = {'nn': NN, 'nt': NT, 'tn': TN}[mode]
    assert not (has_norm and mode != 'nn')
    n_ex = len(extras)

    def body(*refs):
        a_ref, b_ref = refs[0], refs[1]
        pos = 2
        g_ref = None
        if has_norm:
            g_ref = refs[pos]
            pos += 1
        ex_refs = refs[pos:pos + n_ex]
        pos += n_ex
        o_ref = refs[pos]
        pos += 1
        h_ref = None
        if emit_norm:
            h_ref = refs[pos]
            pos += 1
        if has_norm:
            an_ref = refs[pos]

            @pl.when(pl.program_id(1) == 0)
            def _():
                xf = a_ref[...].astype(F32)
                y = xf * lax.rsqrt(jnp.mean(xf * xf, axis=-1, keepdims=True) + EPS) * g_ref[...]
                an_ref[...] = y.astype(BF16)
                if emit_norm:
                    h_ref[...] = y.astype(BF16)

            av = an_ref[...]
        else:
            av = a_ref[...]
            if a_fn is not None:
                av = a_fn(av)
            av = av.astype(BF16)
        acc = _dot(av, b_ref[...].astype(BF16), dims)
        if epilogue is not None:
            acc = epilogue(acc, *[r[...] for r in ex_refs])
        o_ref[...] = acc.astype(out_dtype)

    in_specs = [a_spec, b_spec]
    args = [a, b]
    if has_norm:
        in_specs.append(pl.BlockSpec((1, k), lambda i, j: (0, 0)))
        args.append(norm_g)
    for arr, kind, col in extras:
        if kind == 'mn':
            in_specs.append(pl.BlockSpec((tm, tn), lambda i, j, o=col // tn: (i, j + o)))
        else:
            in_specs.append(pl.BlockSpec((1, tn), lambda i, j, o=col // tn: (0, j + o)))
        args.append(arr)
    out_shape = [jax.ShapeDtypeStruct((m, n), out_dtype)]
    out_specs = [pl.BlockSpec((tm, tn), lambda i, j: (i, j))]
    if emit_norm:
        out_shape.append(jax.ShapeDtypeStruct((m, k), BF16))
        out_specs.append(pl.BlockSpec((tm, k), lambda i, j: (i, 0)))
    scratch = [pltpu.VMEM((tm, k), BF16)] if has_norm else []
    res = pl.pallas_call(
        body, name=name, grid=(m // tm, n // tn), in_specs=in_specs, out_specs=out_specs, out_shape=out_shape,
        scratch_shapes=scratch, compiler_params=_params('arbitrary', 'arbitrary'))(*args)
    return res if emit_norm else res[0]


def _mn(col_off=0):
    return 'mn', col_off


def _nvec(col_off=0):
    return 'n', col_off


def _rowwise(fn, rows, consts, outs, sums=(), *, name, ts=256):
    views = [x if isinstance(x, Cols) else Cols(x, x.shape[1], 0) for x in rows]
    rows = [v.arr for v in views]
    r = rows[0].shape[0]
    ts = min(ts, r)
    assert r % ts == 0, (name, r, ts)
    nr, nc, no, ns = len(rows), len(consts), len(outs), len(sums)

    def body(*refs):
        vals = fn(*[x[...] for x in refs[:nr + nc]])
        for q in range(no):
            refs[nr + nc + q][...] = vals[q].astype(outs[q][1])
        if ns:
            @pl.when(pl.program_id(0) == 0)
            def _():
                for q in range(ns):
                    refs[nr + nc + no + q][...] = jnp.zeros((1, sums[q]), F32)

            for q in range(ns):
                refs[nr + nc + no + q][...] += jnp.sum(vals[no + q].astype(F32), axis=0, keepdims=True)

    in_specs = [pl.BlockSpec((ts, v.width), lambda i, blk=v.blk: (i, blk)) for v in views]
    in_specs += [pl.BlockSpec(x.shape, lambda i, nd=x.ndim: (0,) * nd) for x in consts]
    out_specs = [pl.BlockSpec((ts, w), lambda i: (i, 0)) for w, _ in outs]
    out_specs += [pl.BlockSpec((1, w), lambda i: (0, 0)) for w in sums]
    out_shape = [jax.ShapeDtypeStruct((r, w), dt) for w, dt in outs]
    out_shape += [jax.ShapeDtypeStruct((1, w), F32) for w in sums]
    return pl.pallas_call(body, name=name, grid=(r // ts,), in_specs=in_specs, out_specs=out_specs,
                          out_shape=out_shape, compiler_params=_params('arbitrary'))(*rows, *consts)


def _cumsum_rows(x, *, reverse, name, bs=256):
    s, w = x.shape
    bs = min(bs, s)
    nb = s // bs

    def body(x_ref, o_ref, carry):
        @pl.when(pl.program_id(0) == 0)
        def _():
            carry[...] = jnp.zeros_like(carry)

        r = lax.broadcasted_iota(jnp.int32, (bs, bs), 0)
        c = lax.broadcasted_iota(jnp.int32, (bs, bs), 1)
        tri = jnp.where((c >= r) if reverse else (c <= r), 1.0, 0.0).astype(BF16)
        xv = x_ref[...]
        o_ref[...] = _tri_dot(tri, xv) + carry[...]
        carry[...] += jnp.sum(xv, axis=0, keepdims=True)

    imap = (lambda i: (nb - 1 - i, 0)) if reverse else (lambda i: (i, 0))
    return pl.pallas_call(body, name=name, grid=(nb,), in_specs=[pl.BlockSpec((bs, w), imap)],
                          out_specs=pl.BlockSpec((bs, w), imap), out_shape=jax.ShapeDtypeStruct((s, w), F32),
                          scratch_shapes=[pltpu.VMEM((1, w), F32)], compiler_params=_params('arbitrary'))(x)


def _mask(mode, q0, k0, bq, bk):
    qpos = q0 + lax.broadcasted_iota(jnp.int32, (bq, bk), 0)
    kpos = k0 + lax.broadcasted_iota(jnp.int32, (bq, bk), 1)
    if mode == 'causal':
        return kpos <= qpos
    return kpos < (jnp.right_shift(qpos, int(math.log2(CHUNK))) + 1) * CHUNK


ROPE_SHIFT = int(math.log2(MLA_ROPE))
ATTN_ROW_SLAB = 512


def _lane_masks(g, b, rope):
    lane = lax.broadcasted_iota(jnp.int32, (1, LANES), 1)
    heads = [None if g == 1 else (lane >= hh * (LANES // g)) & (lane < (hh + 1) * (LANES // g)) for hh in range(g)]
    ropes = [jnp.right_shift(lane, ROPE_SHIFT) == b * g + hh for hh in range(g)] if rope else [None] * g
    return heads, ropes


def _sel(mask, x):
    return x if mask is None else jnp.where(mask, x, jnp.zeros_like(x))


def _carried(comm, refs, n_in, n_out):
    ci, co = len(comm.ins), len(comm.out_shapes)
    ins = refs[n_in:n_in + ci]
    outs = refs[n_in + ci + n_out:n_in + ci + n_out + co]
    rest = refs[:n_in] + refs[n_in + ci:n_in + ci + n_out] + refs[n_in + ci + n_out + co:-2]
    return rest, (ins, outs, refs[-2], refs[-1])


def _mattn_fwd(q, k, v, *, qc, kc, vc, nb, g, scale, mode, name, ck=None, qr=None, qrc=0, kr=None, blk=512,
               comm=None):
    s, t = q.shape[0], k.shape[0]
    bq, bk = min(blk, s), min(blk, t)
    nq, nk = s // bq, t // bk
    tri = mode != 'full'
    bias, rope = ck is not None, qr is not None
    assert not tri or (bq == bk and bq % CHUNK == 0)
    rs = min(ATTN_ROW_SLAB, bq)
    n_in = 3 + bias + 2 * rope

    def body(*refs):
        refs = list(refs)
        b, i, j = pl.program_id(0), pl.program_id(1), pl.program_id(2)
        if comm is not None:
            refs, comm_refs = _carried(comm, refs, n_in, 2)
            pl.when((b == 0) & (i == 0) & (j == 0))(lambda: comm.start(*comm_refs))
        q_ref, k_ref, v_ref = refs[:3]
        pos = 3
        ck_ref = qr_ref = kr_ref = None
        if bias:
            ck_ref = refs[pos]
            pos += 1
        if rope:
            qr_ref, kr_ref = refs[pos:pos + 2]
            pos += 2
        o_ref, lse_ref, m_s, l_s, acc_s = refs[pos:]
        heads, ropes = _lane_masks(g, b, rope)

        @pl.when(j == 0)
        def _():
            m_s[...] = jnp.full_like(m_s, MASK_VALUE)
            l_s[...] = jnp.zeros_like(l_s)
            acc_s[...] = jnp.zeros_like(acc_s)

        def compute(masked):
            k2, v2 = k_ref[...], v_ref[...]
            for r in range(bq // rs):
                rows = pl.ds(r * rs, rs)
                q2 = q_ref[rows, :]
                alphas, pvs = [], []
                for hh in range(g):
                    sc = _dot(_sel(heads[hh], q2), k2, NT)
                    if rope:
                        sc = sc + _dot(_sel(ropes[hh], qr_ref[rows, :]), kr_ref[...], NT)
                    sc = sc * scale
                    if bias:
                        sc = sc - ck_ref[0, hh:hh + 1, :]
                    if masked:
                        sc = jnp.where(_mask(mode, i * bq + r * rs, j * bk, rs, bk), sc, MASK_VALUE)
                    m_prev = m_s[hh, rows]
                    m_new = jnp.maximum(m_prev, jnp.max(sc, axis=1, keepdims=True))
                    alpha = jnp.exp(m_prev - m_new)
                    p = jnp.exp(sc - m_new)
                    l_s[hh, rows] = alpha * l_s[hh, rows] + jnp.sum(p, axis=1, keepdims=True)
                    m_s[hh, rows] = m_new
                    alphas.append(alpha)
                    pvs.append(_dot(p.astype(BF16), _sel(heads[hh], v2)))
                alpha = alphas[0]
                for hh in range(1, g):
                    alpha = jnp.where(heads[hh], alphas[hh], alpha)
                acc_s[rows, :] = acc_s[rows, :] * alpha + sum(pvs[1:], pvs[0])

        if tri:
            pl.when(j < i)(functools.partial(compute, False))
            pl.when(j == i)(functools.partial(compute, True))
        else:
            compute(False)

        @pl.when(j == nk - 1)
        def _():
            lane = lax.broadcasted_iota(jnp.int32, (bq, LANES), 1)
            l_full, lse = l_s[0], jnp.zeros((bq, LANES), F32)
            for hh in range(g):
                if hh:
                    l_full = jnp.where(heads[hh], l_s[hh], l_full)
                lse = jnp.where(lane == hh, m_s[hh] + jnp.log(l_s[hh]), lse)
            o_ref[...] = (acc_s[...] / l_full).astype(o_ref.dtype)
            lse_ref[...] = lse

        if comm is not None:
            pl.when((b == nb - 1) & (i == nq - 1) & (j == nk - 1))(lambda: comm.finish(*comm_refs))

    jj = (lambda i, j: jnp.minimum(i, j)) if tri else (lambda i, j: j)
    in_specs = [pl.BlockSpec((bq, LANES), lambda b, i, j: (i, qc + b)),
                pl.BlockSpec((bk, LANES), lambda b, i, j: (jj(i, j), kc + b)),
                pl.BlockSpec((bk, LANES), lambda b, i, j: (jj(i, j), vc + b))]
    args = [q, k, v]
    if bias:
        in_specs.append(pl.BlockSpec((1, 8, bk), lambda b, i, j: (b, 0, jj(i, j))))
        args.append(ck)
    if rope:
        in_specs += [pl.BlockSpec((bq, LANES), lambda b, i, j: (i, qrc)),
                     pl.BlockSpec((bk, LANES), lambda b, i, j: (jj(i, j), 0))]
        args += [qr, kr]
    out = pl.BlockSpec((bq, LANES), lambda b, i, j: (i, b))
    out_specs = [out, out]
    out_shape = [jax.ShapeDtypeStruct((s, LANES * nb), BF16), jax.ShapeDtypeStruct((s, LANES * nb), F32)]
    scratch = [pltpu.VMEM((g, bq, 1), F32), pltpu.VMEM((g, bq, 1), F32), pltpu.VMEM((bq, LANES), F32)]
    if comm is not None:
        in_specs += [ANY] * len(comm.ins)
        args += comm.ins
        out_specs += [ANY] * len(comm.out_shapes)
        out_shape += comm.out_shapes
        scratch += _sems(comm.n_sems, comm.n_sems)
    res = pl.pallas_call(body, name=name, grid=(nb, nq, nk), in_specs=in_specs, out_specs=out_specs, out_shape=out_shape,
                         scratch_shapes=scratch, compiler_params=_params('arbitrary', 'arbitrary', 'arbitrary'))(*args)
    return res if comm is None else (res[0], res[1], res[2:])


def _mattn_bwd(q, k, v, o, do, lse, *, qc, kc, vc, nb, g, scale, mode, name, ck=None, qr=None, qrc=0, kr=None,
               blk=512, comm=None):
    s, t = q.shape[0], k.shape[0]
    bq, bk = min(blk, s), min(blk, t)
    nq, nk = s // bq, t // bk
    tri = mode != 'full'
    bias, rope = ck is not None, qr is not None
    rs = min(ATTN_ROW_SLAB, bq)
    n_in, n_out = 6 + bias + 2 * rope, 3 + 2 * bias + 2 * rope

    def body(*refs):
        refs = list(refs)
        if comm is not None:
            refs, comm_refs = _carried(comm, refs, n_in, n_out)
            first = (pl.program_id(0) == 0) & (pl.program_id(1) == 0) & (pl.program_id(2) == 0)
            pl.when(first)(lambda: comm.start(*comm_refs))
        q_ref, k_ref, v_ref, o_ref, do_ref, lse_ref = refs[:6]
        pos = 6
        ck_ref = qr_ref = kr_ref = dck_ref = dcq_ref = dqr_ref = dkr_ref = dck_s = None
        if bias:
            ck_ref = refs[pos]
            pos += 1
        if rope:
            qr_ref, kr_ref = refs[pos:pos + 2]
            pos += 2
        dq_ref, dk_ref, dv_ref = refs[pos:pos + 3]
        pos += 3
        if bias:
            dck_ref, dcq_ref = refs[pos:pos + 2]
            pos += 2
        if rope:
            dqr_ref, dkr_ref = refs[pos:pos + 2]
            pos += 2
        dk_s, dv_s = refs[pos:pos + 2]
        if bias:
            dck_s = refs[pos + 2]
        b, j, i = pl.program_id(0), pl.program_id(1), pl.program_id(2)
        heads, ropes = _lane_masks(g, b, rope)

        @pl.when((j == 0) & (i == 0))
        def _():
            dq_ref[...] = jnp.zeros_like(dq_ref)
            if bias:
                dcq_ref[...] = jnp.zeros_like(dcq_ref)

        if rope:
            @pl.when((b == 0) & (j == 0) & (i == 0))
            def _():
                dqr_ref[...] = jnp.zeros_like(dqr_ref)
                dkr_ref[...] = jnp.zeros_like(dkr_ref)

        @pl.when(i == 0)
        def _():
            dk_s[...] = jnp.zeros_like(dk_s)
            dv_s[...] = jnp.zeros_like(dv_s)
            if bias:
                dck_s[...] = jnp.zeros_like(dck_s)

        def compute(masked):
            k2, v2 = k_ref[...], v_ref[...]
            lane = lax.broadcasted_iota(jnp.int32, (rs, LANES), 1)
            rk = pl.ds(pl.multiple_of(j * bk, bk), bk)
            add = lambda tot, x: x if tot is None else tot + x
            dv_t = dk_t = dkr_t = None
            dck_t = [None] * g
            for r in range(bq // rs):
                rows = pl.ds(r * rs, rs)
                rq = pl.ds(pl.multiple_of(i * bq + r * rs, rs), rs)
                q2, do2, lse2 = q_ref[rows, :], do_ref[rows, :], lse_ref[rows, :]
                dd = do2.astype(F32) * o_ref[rows, :].astype(F32)
                dq_t = dqr_t = dcq_t = None
                for hh in range(g):
                    qm = _sel(heads[hh], q2)
                    sc = _dot(qm, k2, NT)
                    if rope:
                        qrm = _sel(ropes[hh], qr_ref[rows, :])
                        sc = sc + _dot(qrm, kr_ref[...], NT)
                    sc = sc * scale
                    if bias:
                        sc = sc - ck_ref[0, hh:hh + 1, :]
                    if masked:
                        sc = jnp.where(_mask(mode, i * bq + r * rs, j * bk, rs, bk), sc, MASK_VALUE)
                    p = jnp.exp(sc - jnp.sum(jnp.where(lane == hh, lse2, 0.0), axis=1, keepdims=True))
                    dom = _sel(heads[hh], do2)
                    dp = _dot(dom, v2, NT)
                    delta = jnp.sum(_sel(heads[hh], dd), axis=1, keepdims=True)
                    ds = p * (dp - delta)
                    dsb = ds.astype(BF16)
                    dv_t = add(dv_t, _dot(p.astype(BF16), dom, TN))
                    dk_t = add(dk_t, _dot(dsb, qm, TN))
                    dq_t = add(dq_t, _dot(dsb, _sel(heads[hh], k2)))
                    if rope:
                        dqr_t = add(dqr_t, _dot(dsb, _sel(ropes[hh], kr_ref[...])))
                        dkr_t = add(dkr_t, _dot(dsb, qrm, TN))
                    if bias:
                        dck_t[hh] = add(dck_t[hh], jnp.sum(ds, axis=0, keepdims=True))
                        dcq_t = add(dcq_t, jnp.where(lane == hh, jnp.sum(ds, axis=1, keepdims=True), 0.0))
                dq_ref[rq, :] += scale * dq_t
                if rope:
                    dqr_ref[rq, :] += scale * dqr_t
                if bias:
                    dcq_ref[rq, :] += dcq_t
            dv_s[...] += dv_t
            dk_s[...] += scale * dk_t
            if rope:
                dkr_ref[rk, :] += scale * dkr_t
            if bias:
                for hh in range(g):
                    dck_s[hh:hh + 1, :] -= dck_t[hh]

        if tri:
            pl.when(i > j)(functools.partial(compute, False))
            pl.when(i == j)(functools.partial(compute, True))
        else:
            compute(False)

        @pl.when(i == nq - 1)
        def _():
            dk_ref[...] = dk_s[...]
            dv_ref[...] = dv_s[...]
            if bias:
                dck_ref[0] = dck_s[...]

        if comm is not None:
            pl.when((b == nb - 1) & (j == nk - 1) & (i == nq - 1))(lambda: comm.finish(*comm_refs))

    ii = (lambda j, i: jnp.maximum(i, j)) if tri else (lambda j, i: i)
    qrow = lambda col: pl.BlockSpec((bq, LANES), lambda b, j, i: (ii(j, i), col(b)))
    krow = lambda col: pl.BlockSpec((bk, LANES), lambda b, j, i: (j, col(b)))
    in_specs = [qrow(lambda b: qc + b), krow(lambda b: kc + b), krow(lambda b: vc + b), qrow(lambda b: b),
                qrow(lambda b: b), qrow(lambda b: b)]
    args = [q, k, v, o, do, lse]
    whole = lambda rows: pl.BlockSpec((rows, LANES), lambda b, j, i: (0, b))
    out_specs = [whole(s), krow(lambda b: b), krow(lambda b: b)]
    out_shape = [jax.ShapeDtypeStruct((s, LANES * nb), F32), jax.ShapeDtypeStruct((t, LANES * nb), F32),
                 jax.ShapeDtypeStruct((t, LANES * nb), F32)]
    scratch = [pltpu.VMEM((bk, LANES), F32), pltpu.VMEM((bk, LANES), F32)]
    if bias:
        in_specs.append(pl.BlockSpec((1, 8, bk), lambda b, j, i: (b, 0, j)))
        args.append(ck)
        out_specs += [pl.BlockSpec((1, 8, bk), lambda b, j, i: (b, 0, j)), whole(s)]
        out_shape += [jax.ShapeDtypeStruct((nb, 8, t), F32), jax.ShapeDtypeStruct((s, LANES * nb), F32)]
    if rope:
        in_specs += [qrow(lambda b: qrc), krow(lambda b: 0)]
        args += [qr, kr]
        out_specs += [pl.BlockSpec((s, LANES), lambda b, j, i: (0, 0)), pl.BlockSpec((t, LANES), lambda b, j, i: (0, 0))]
        out_shape += [jax.ShapeDtypeStruct((s, LANES), F32), jax.ShapeDtypeStruct((t, LANES), F32)]
    if bias:
        scratch.append(pltpu.VMEM((8, bk), F32))
    if comm is not None:
        in_specs += [ANY] * len(comm.ins)
        args += comm.ins
        out_specs += [ANY] * len(comm.out_shapes)
        out_shape += comm.out_shapes
        scratch += _sems(comm.n_sems, comm.n_sems)
    res = pl.pallas_call(body, name=name, grid=(nb, nk, nq), in_specs=in_specs, out_specs=out_specs,
                         out_shape=out_shape, scratch_shapes=scratch,
                         compiler_params=_params('arbitrary', 'arbitrary', 'arbitrary'))(*args)
    return res if comm is None else (*res[:n_out], res[n_out:])


def _gla_chunk(la_c, k_c):
    r = lax.broadcasted_iota(jnp.int32, (CHUNK, CHUNK), 0)
    c = lax.broadcasted_iota(jnp.int32, (CHUNK, CHUNK), 1)
    tri = jnp.where(c <= r, 1.0, 0.0).astype(BF16)
    cum = _tri_dot(tri, la_c)
    end = jnp.sum(la_c, axis=0, keepdims=True)
    dec = jnp.exp(end - cum)
    return dec, k_c * dec, jnp.exp(end)


GLA_PAIRS = GLA_HEADS // 2


def _gla_fwd(z, la, *, qc, kc, vc, name, blk=512):
    s = z.shape[0]
    bs = min(blk, s)
    ncb = bs // CHUNK
    nblk = s // bs

    def body(q_ref, k_ref, va_ref, vb_ref, la_ref, o_ref, st_ref, st):
        @pl.when(pl.program_id(1) == 0)
        def _():
            st[...] = jnp.zeros_like(st)

        heads, _ = _lane_masks(2, 0, False)
        v_refs = (va_ref, vb_ref)
        for c in range(ncb):
            sl = pl.ds(c * CHUNK, CHUNK)
            _, kf, a = _gla_chunk(la_ref[sl, :], k_ref[sl, :])
            qs = q_ref[sl, :] * (GLA_DK ** -0.5)
            for hh in range(2):
                ut = _dot(v_refs[hh][sl, :].astype(BF16), _sel(heads[hh], kf).astype(BF16), TN)
                new = a * st[hh] + ut
                st[hh] = new
                st_ref[0, c, hh] = new
                o_ref[sl, hh * GLA_DV:(hh + 1) * GLA_DV] = _dot(_sel(heads[hh], qs).astype(BF16), new.astype(BF16), NT)

    col = lambda c0, m=1: pl.BlockSpec((bs, LANES), lambda b, i: (i, c0 + m * b))
    return pl.pallas_call(
        body, name=name, grid=(GLA_PAIRS, nblk),
        in_specs=[col(qc), col(kc), col(vc, 2), col(vc + 1, 2), col(0)],
        out_specs=[pl.BlockSpec((bs, 2 * GLA_DV), lambda b, i: (i, b)),
                   pl.BlockSpec((1, ncb, 2, GLA_DV, LANES), lambda b, i: (b, i, 0, 0, 0))],
        out_shape=[jax.ShapeDtypeStruct((s, GLA_HEADS * GLA_DV), F32),
                   jax.ShapeDtypeStruct((GLA_PAIRS, s // CHUNK, 2, GLA_DV, LANES), F32)],
        scratch_shapes=[pltpu.VMEM((2, GLA_DV, LANES), F32)],
        compiler_params=_params('arbitrary', 'arbitrary'))(z, z, z, z, la)


def _gla_bwd(z, la, st_all, st_prev, do, *, qc, kc, vc, name, blk=512):
    s = z.shape[0]
    bs = min(blk, s)
    ncb = bs // CHUNK
    nblk = s // bs

    def body(q_ref, k_ref, va_ref, vb_ref, la_ref, st_ref, sp_ref, do_ref, dq_ref, dk_ref, dv_ref, dla_ref, ga):
        @pl.when(pl.program_id(1) == 0)
        def _():
            ga[...] = jnp.zeros_like(ga)

        r = lax.broadcasted_iota(jnp.int32, (CHUNK, CHUNK), 0)
        cc = lax.broadcasted_iota(jnp.int32, (CHUNK, CHUNK), 1)
        tri_rev = jnp.where(cc >= r, 1.0, 0.0).astype(BF16)
        heads, _ = _lane_masks(2, 0, False)
        v_refs = (va_ref, vb_ref)
        for c in reversed(range(ncb)):
            sl = pl.ds(c * CHUNK, CHUNK)
            dec, kf, a = _gla_chunk(la_ref[sl, :], k_ref[sl, :])
            qs = q_ref[sl, :] * (GLA_DK ** -0.5)
            dq2 = jnp.zeros((CHUNK, LANES), F32)
            dkd = jnp.zeros((CHUNK, LANES), F32)
            da = jnp.zeros((1, LANES), F32)
            for hh in range(2):
                hv = slice(hh * GLA_DV, (hh + 1) * GLA_DV)
                dob = do_ref[sl, hv].astype(BF16)
                g = _dot(dob, _sel(heads[hh], qs).astype(BF16), TN) + ga[hh]
                gb = g.astype(BF16)
                dq2 = dq2 + _dot(dob, st_ref[0, c, hh].astype(BF16))
                dv_ref[sl, hv] = _dot(_sel(heads[hh], kf).astype(BF16), gb, NT)
                dkd = dkd + _dot(v_refs[hh][sl, :].astype(BF16), gb)
                da = da + jnp.sum(g * sp_ref[0, c, hh], axis=0, keepdims=True)
                ga[hh] = a * g
            dq_ref[sl, :] = (GLA_DK ** -0.5) * dq2
            dk_ref[sl, :] = dkd * dec
            e = dkd * kf
            dend = jnp.sum(e, axis=0, keepdims=True) + da * a
            dla_ref[sl, :] = dend - _tri_dot(tri_rev, e)

    rev = lambda i: nblk - 1 - i
    col = lambda c0, m=1: pl.BlockSpec((bs, LANES), lambda b, i: (rev(i), c0 + m * b))
    wide = pl.BlockSpec((bs, 2 * GLA_DV), lambda b, i: (rev(i), b))
    stspec = pl.BlockSpec((1, ncb, 2, GLA_DV, LANES), lambda b, i: (b, rev(i), 0, 0, 0))
    return pl.pallas_call(
        body, name=name, grid=(GLA_PAIRS, nblk),
        in_specs=[col(qc), col(kc), col(vc, 2), col(vc + 1, 2), col(0), stspec, stspec, wide],
        out_specs=[col(0), col(0), wide, col(0)],
        out_shape=[jax.ShapeDtypeStruct((s, GLA_HEADS * GLA_DK), F32), jax.ShapeDtypeStruct((s, GLA_HEADS * GLA_DK), F32),
                   jax.ShapeDtypeStruct((s, GLA_HEADS * GLA_DV), F32), jax.ShapeDtypeStruct((s, GLA_HEADS * GLA_DK), F32)],
        scratch_shapes=[pltpu.VMEM((2, GLA_DV, LANES), F32)],
        compiler_params=_params('arbitrary', 'arbitrary'))(z, z, z, z, la, st_all, st_prev, do)


def _place():
    return lax.axis_index('x'), lax.axis_index('y'), lax.axis_index('c')


ANY = pl.BlockSpec(memory_space=pl.ANY)


def _all_gather8(blk, *, name):
    m, n = blk.shape

    def body(x_ref, out_ref, send_sems, recv_sems, local_sem):
        x, y, c = _place()
        me, sibling = (x, y, c), (x, y, 1 - c)
        chips = [(1 - x, y), (x, 1 - y), (1 - x, 1 - y)]

        def slot(px, py, pc):
            return out_ref.at[4 * px + 2 * py + pc]

        def copy(q, block, to, src=None):
            return pltpu.make_async_remote_copy(
                src_ref=slot(*block) if src is None else src, dst_ref=slot(*block), send_sem=send_sems.at[q],
                recv_sem=recv_sems.at[q], device_id=to, device_id_type=MESH)

        mine = pltpu.make_async_copy(x_ref, slot(*me), local_sem)
        mine.start()
        first = [copy(0, me, sibling, src=x_ref)]
        first += [copy(1 + q, me, (*chip, c), src=x_ref) for q, chip in enumerate(chips)]
        for cp in first:
            cp.start()
        passed = [copy(4 + q, (*chip, c), sibling) for q, chip in enumerate(chips)]
        for q, chip in enumerate(chips):
            copy(1 + q, (*chip, c), me).wait_recv()
            passed[q].start()
        copy(0, sibling, me).wait_recv()
        for q, chip in enumerate(chips):
            copy(4 + q, (*chip, 1 - c), me).wait_recv()
        for cp in first + passed:
            cp.wait_send()
        mine.wait()

    return pl.pallas_call(
        body, name=name, in_specs=[ANY], out_specs=ANY, out_shape=jax.ShapeDtypeStruct((N_DEV, m, n), blk.dtype),
        scratch_shapes=[pltpu.SemaphoreType.DMA((7,)), pltpu.SemaphoreType.DMA((7,)), pltpu.SemaphoreType.DMA(())],
    )(blk)


def _sems(*counts):
    return [pltpu.SemaphoreType.DMA((n,)) for n in counts]


class Comm(typing.NamedTuple):
    ins: list
    out_shapes: list
    n_sems: int
    start: typing.Callable
    finish: typing.Callable


def _remote(src, dst, send_sems, recv_sems, idx, to):
    return lambda: pltpu.make_async_remote_copy(src_ref=src, dst_ref=dst, send_sem=send_sems.at[idx],
                                                recv_sem=recv_sems.at[idx], device_id=to, device_id_type=MESH)


def _comm_from(copies, ins, out_shapes, n_sems):
    def start(*refs):
        for cp in copies(*refs)[0]:
            cp().start()

    def finish(*refs):
        sent, received = copies(*refs)
        for cp in received:
            cp().wait_recv()
        for cp in sent:
            cp().wait_send()

    return Comm(list(ins), list(out_shapes), n_sems, start, finish)


def _run_comm(comm, *, name, alias=False):
    n_in, n_out = len(comm.ins), len(comm.out_shapes)

    def body(*refs):
        ins, outs, sems = refs[:n_in], refs[n_in:n_in + n_out], refs[n_in + n_out:]
        comm.start(ins, outs, *sems)
        comm.finish(ins, outs, *sems)

    return pl.pallas_call(body, name=name, in_specs=[ANY] * n_in, out_specs=[ANY] * n_out, out_shape=comm.out_shapes,
                          input_output_aliases={q: q for q in range(n_in)} if alias else {},
                          scratch_shapes=_sems(comm.n_sems, comm.n_sems))(*comm.ins)


def _half(rows, c):
    h = rows // 2
    return pl.ds(pl.multiple_of(c * h, h), h)


def _gather_over_ici(ws):
    def copies(ins, outs, send_sems, recv_sems):
        x, y, c = _place()
        me_chip = 2 * x + y
        sent, received = [], []
        for q, w in enumerate(ws):
            half = _half(w.shape[0], c)
            for k, (px, py) in enumerate([(1 - x, y), (x, 1 - y), (1 - x, 1 - y)]):
                sent.append(_remote(ins[q].at[half], outs[q].at[me_chip, half], send_sems, recv_sems, 4 * q + k, (px, py, c)))
                slot = outs[q].at[2 * px + py, half]
                received.append(_remote(slot, slot, send_sems, recv_sems, 4 * q + k, (px, py, c)))
            whole = _remote(ins[q], outs[q].at[me_chip], send_sems, recv_sems, 4 * q + 3, (x, y, 1 - c))
            sent.append(whole)
            received.append(whole)
        return sent, received

    return _comm_from(copies, ws, [jax.ShapeDtypeStruct((N_CHIPS,) + w.shape, w.dtype) for w in ws], 4 * len(ws))


def _gather_over_d2d(parts):
    def copies(ins, outs, send_sems, recv_sems):
        x, y, c = _place()
        sent, received = [], []
        for q, w in enumerate(parts):
            for k, (px, py) in enumerate([(1 - x, y), (x, 1 - y), (1 - x, 1 - y)]):
                mine = outs[q].at[2 * px + py, _half(w.shape[1], c)]
                theirs = outs[q].at[2 * px + py, _half(w.shape[1], 1 - c)]
                sent.append(_remote(mine, mine, send_sems, recv_sems, 3 * q + k, (x, y, 1 - c)))
                received.append(_remote(theirs, theirs, send_sems, recv_sems, 3 * q + k, (x, y, 1 - c)))
        return sent, received

    return _comm_from(copies, parts, [jax.ShapeDtypeStruct(w.shape, w.dtype) for w in parts], 3 * len(parts))


def _to_sibling(gs, *, name):
    n = len(gs)

    def body(*refs):
        ins, outs = refs[:n], refs[n:2 * n]
        send_sems, recv_sems = refs[2 * n:]
        x, y, c = _place()
        cps = [pltpu.make_async_remote_copy(
            src_ref=ins[q], dst_ref=outs[q], send_sem=send_sems.at[q], recv_sem=recv_sems.at[q],
            device_id=(x, y, 1 - c), device_id_type=MESH) for q in range(n)]
        for cp in cps:
            cp.start()
        for cp in cps:
            cp.wait()

    return pl.pallas_call(body, name=name, in_specs=[ANY] * n, out_specs=[ANY] * n,
                          out_shape=[jax.ShapeDtypeStruct(g.shape, g.dtype) for g in gs],
                          scratch_shapes=_sems(n, n))(*gs)


def _chip_exchange(ps):
    def copies(ins, outs, send_sems, recv_sems):
        x, y, c = _place()
        cps = [_remote(ins[q].at[2 * px + py], outs[q].at[k], send_sems, recv_sems, 3 * q + k, (px, py, c))
               for q in range(len(ps)) for k, (px, py) in enumerate([(1 - x, y), (x, 1 - y), (1 - x, 1 - y)])]
        return cps, cps

    return _comm_from(copies, ps, [jax.ShapeDtypeStruct((3,) + p.shape[1:], p.dtype) for p in ps], 3 * len(ps))


def _sum_chips(own, r, *, name, ts=256):
    k, n = own.shape
    ts = min(ts, k)

    def body(own_ref, r_ref, o_ref):
        f = lambda q: r_ref[q].astype(F32)
        o_ref[...] = ((own_ref[...].astype(F32) + f(0)) + f(1)) + f(2)

    return pl.pallas_call(
        body, name=name, grid=(k // ts,),
        in_specs=[pl.BlockSpec((ts, n), lambda i: (i, 0)), pl.BlockSpec((3, ts, n), lambda i: (0, i, 0))],
        out_specs=pl.BlockSpec((ts, n), lambda i: (i, 0)), out_shape=jax.ShapeDtypeStruct((k, n), F32),
        compiler_params=_params('arbitrary'))(own, r)


WIN_SHARD = N_IN // N_CHIPS
WIN_PAD = -(-WIN_SHARD // LANES) * LANES
GATE_WIRE_ROWS = 32


def _full_layer(sh, axis):
    _, k, n = sh.shape
    if axis == 2:
        return sh.transpose(1, 0, 2).reshape(k, N_CHIPS * n)
    return sh.reshape(N_CHIPS * k, n)


def _win_cols(wp, o, n):
    parts = []
    while n > 0:
        j, r = divmod(o, WIN_SHARD)
        take = min(n, WIN_SHARD - r)
        parts.append(wp[:, j * WIN_PAD + r:j * WIN_PAD + r + take])
        o, n = o + take, n - take
    return parts[0] if len(parts) == 1 else jnp.concatenate(parts, axis=1)


def _split_full(full, axis):
    k, n = full.shape
    if axis == 2:
        return jnp.stack([full[:, j * (n // N_CHIPS):(j + 1) * (n // N_CHIPS)] for j in range(N_CHIPS)])
    return full.reshape(N_CHIPS, k // N_CHIPS, n)


def _padc(a, w):
    return jnp.pad(a, ((0, 0), (0, w - a.shape[1])))


def _swap16(a):
    return jnp.concatenate([a[..., 16:32], a[..., 0:16]], axis=-1)


B_GR, B_GQ, B_GK, B_GV, B_MQ, B_MKR, B_MKRS, B_FF, B_GLOW, B_MKV, B_END = (
    0, 512, 768, 1024, 1536, 1792, 1920, 2048, 2176, 2304, 2432)
B_W = 2560
O_FQ, O_FF, O_GQ, O_GLOW, O_GR, O_MQ, O_MKV, O_MKR, O_ZG = 0, 768, 772, 1796, 1812, 2324, 2580, 2708, 2740


def _repack_layer_weights(w):
    wi = functools.partial(_win_cols, w['w_in'])
    out = dict(w)
    out['in_a'] = wi(O_FQ, 768)
    kr = wi(O_MKR, 32)
    out['in_b'] = jnp.concatenate([
        wi(O_GR, 512), wi(O_GQ, 1024), wi(O_MQ, 256), jnp.tile(kr, (1, MLA_HEADS)), jnp.tile(_swap16(kr), (1, MLA_HEADS)),
        _padc(wi(O_FF, 4), 128), _padc(wi(O_GLOW, 16), 128), wi(O_MKV, 128),
        jnp.zeros((D_MODEL, B_W - B_END), kr.dtype)], axis=1)
    out['in_c'] = wi(O_ZG, 3072)
    uq = w['w_mla_uq'].reshape(MLA_Q_RANK, MLA_HEADS, MLA_NOPE + MLA_ROPE)
    rope = uq[:, :, MLA_NOPE:]
    out['uq'] = jnp.concatenate([uq[:, :, :MLA_NOPE].reshape(MLA_Q_RANK, -1), rope.reshape(MLA_Q_RANK, -1),
                                 _swap16(rope).reshape(MLA_Q_RANK, -1)], axis=1)
    ukv = w['w_mla_ukv'].reshape(MLA_KV_RANK, MLA_HEADS, MLA_NOPE + MLA_VD)
    out['ukv'] = jnp.concatenate([ukv[:, :, :MLA_NOPE].reshape(MLA_KV_RANK, -1),
                                  ukv[:, :, MLA_NOPE:].reshape(MLA_KV_RANK, -1)], axis=1)
    out['gate'] = jnp.pad(w['w_gla_gate'], ((0, 128 - GLA_RANK), (0, 0)))
    return out


def _unpack_layer_grads(g):
    a, b, c = g['in_a'], g['in_b'], g['in_c']
    fold = lambda o: sum(b[:, o + MLA_ROPE * q:o + MLA_ROPE * (q + 1)] for q in range(MLA_HEADS))
    kr = fold(B_MKR) + _swap16(fold(B_MKRS))
    w_in = jnp.concatenate([a, b[:, B_FF:B_FF + 4], b[:, B_GQ:B_GQ + 1024], b[:, B_GLOW:B_GLOW + 16],
                            b[:, B_GR:B_GR + 512], b[:, B_MQ:B_MQ + 256], b[:, B_MKV:B_MKV + 128], kr, c], axis=1)
    uq = g['uq']
    nope = uq[:, :256].reshape(MLA_Q_RANK, MLA_HEADS, MLA_NOPE)
    rope = (uq[:, 256:384].reshape(MLA_Q_RANK, MLA_HEADS, MLA_ROPE)
            + _swap16(uq[:, 384:512].reshape(MLA_Q_RANK, MLA_HEADS, MLA_ROPE)))
    w_uq = jnp.concatenate([nope, rope], axis=2).reshape(MLA_Q_RANK, -1)
    ukv = g['ukv']
    w_ukv = jnp.concatenate([ukv[:, :256].reshape(MLA_KV_RANK, MLA_HEADS, MLA_NOPE),
                             ukv[:, 256:].reshape(MLA_KV_RANK, MLA_HEADS, MLA_VD)], axis=2).reshape(MLA_KV_RANK, -1)
    out = {'w_in': w_in, 'w_mla_uq': w_uq, 'w_mla_ukv': w_ukv, 'w_gla_gate': g['gate'][:GLA_RANK]}
    for nm in ('w_up_fox', 'w_up_gla', 'w_up_mla', 'w_out', 'w_xq', 'w_xkv', 'w_xo', 'w_mlp1', 'w_mlp2'):
        out[nm] = g[nm]
    return out


def _rope_tables(s):
    half = MLA_ROPE // 2
    inv = ROPE_BASE ** (-jnp.arange(half, dtype=F32) / half)
    ang = jnp.arange(s).astype(F32)[:, None] * inv[None, :]
    cos, sin = jnp.cos(ang), jnp.sin(ang)
    c1 = jnp.concatenate([cos, cos], axis=1)
    s1 = jnp.concatenate([-sin, sin], axis=1)
    return jnp.tile(c1, (1, MLA_HEADS)), jnp.tile(s1, (1, MLA_HEADS))


def _rms_bwd(x, dh, g):
    r = lax.rsqrt(jnp.mean(x * x, axis=-1, keepdims=True) + EPS)
    xh = x * r
    gd = dh * g
    return r * (gd - xh * jnp.mean(gd * xh, axis=-1, keepdims=True)), dh * xh


def _norm_bwd_call(x, dh, g, dres, name):
    w = x.width if isinstance(x, Cols) else x.shape[1]

    def with_res(xv, dv, rv, gv):
        dx, dg = _rms_bwd(xv, dv.astype(F32), gv)
        return rv + dx, dg

    def plain(xv, dv, gv):
        return _rms_bwd(xv, dv.astype(F32), gv)

    if dres is None:
        return _rowwise(plain, [x, dh], [g], [(w, F32)], [w], name=name)
    return _rowwise(with_res, [x, dh, dres], [g], [(w, F32)], [w], name=name)


def _gla_out_fwd(oraw, gr, g_out):
    outs = []
    for hh in range(GLA_HEADS):
        sl = slice(hh * GLA_DV, (hh + 1) * GLA_DV)
        oh = oraw[:, sl]
        n = oh * lax.rsqrt(jnp.mean(oh * oh, axis=-1, keepdims=True) + EPS) * g_out
        r = gr[:, sl]
        outs.append(n * (r * _sig(r)))
    return (jnp.concatenate(outs, axis=1),)


def _gla_out_bwd(oraw, gr, dout, g_out):
    d_o, d_r, dg = [], [], 0.0
    for hh in range(GLA_HEADS):
        sl = slice(hh * GLA_DV, (hh + 1) * GLA_DV)
        oh, r, do = oraw[:, sl], gr[:, sl], dout[:, sl].astype(F32)
        rs = lax.rsqrt(jnp.mean(oh * oh, axis=-1, keepdims=True) + EPS)
        sg = _sig(r)
        dn = do * (r * sg)
        d_r.append(do * (oh * rs * g_out) * (sg + r * sg * (1.0 - sg)))
        dx, dgh = _rms_bwd(oh, dn, g_out)
        d_o.append(dx)
        dg = dg + dgh
    return jnp.concatenate(d_o, axis=1), jnp.concatenate(d_r, axis=1), dg


def _adam(w, g, m, v):
    m = ADAM_B1 * m + (1.0 - ADAM_B1) * g
    v = ADAM_B2 * v + (1.0 - ADAM_B2) * (g * g)
    m_hat = m / (1.0 - ADAM_B1 ** ADAM_STEP)
    v_hat = v / (1.0 - ADAM_B2 ** ADAM_STEP)
    return -ADAM_LR * (m_hat / (jnp.sqrt(v_hat) + ADAM_EPS) + ADAM_WD * w), m, v


def _layer_fwd(x, mem, w, p, tabs, tag, carry_fox=None, after_fox=None, carry_mla=None):
    c4, s4 = tabs
    sv = {'x0': x}
    nm = lambda t: f'{t}_{tag}'
    za, h = _mm(x, w['in_a'], mode='nn', out_dtype=BF16, norm_g=p['g_mix'], emit_norm=True, name=nm('in_a'))
    zb = _mm(h, w['in_b'], mode='nn', out_dtype=F32, name=nm('in_b'))
    zc = _mm(h, w['in_c'], mode='nn', out_dtype=F32, name=nm('in_c'))
    sv.update(h=h, zc=zc)
    ff = Cols(zb, 128, B_FF // 128)
    (lf,) = _rowwise(lambda f, b: (_logsig(f + b),), [ff], [p['b_fox']], [(128, F32)], name=nm('fox_lf'))
    cum = _cumsum_rows(lf, reverse=False, name=nm('fox_cum'))
    ckf = jnp.pad(cum[:, :FOX_HEADS].T.reshape(2, 2, x.shape[0]), ((0, 0), (0, 6), (0, 0)))
    fox = dict(qc=0, kc=2, vc=4, nb=2, g=2, scale=FOX_HD ** -0.5, mode='causal', ck=ckf)
    o_fox, lse_fox, *carried = _mattn_fwd(za, za, za, name=nm('fox_attn'), comm=carry_fox, **fox)
    if after_fox is not None:
        w = {**w, **after_fox(carried[0])}
    sv.update(ff=ff, za=za, fox=fox, o_fox=o_fox, lse_fox=lse_fox)
    glow = Cols(zb, 128, B_GLOW // 128)
    gr = Cols(zb, 512, B_GR // 512)

    def gate_fn(gl, wg, bg):
        return (_logsig(_dot(gl.astype(BF16), wg) + bg) / GLA_TAU,)

    (la,) = _rowwise(gate_fn, [glow], [w['gate'], p['b_gla']], [(256, F32)], name=nm('gla_gate'))
    gla = dict(qc=B_GQ // LANES, kc=B_GK // LANES, vc=B_GV // LANES)
    oraw, states = _gla_fwd(zb, la, name=nm('gla'), **gla)
    (o_gla,) = _rowwise(_gla_out_fwd, [oraw, gr], [p['g_gla_out']], [(512, BF16)], name=nm('gla_out'))
    sv.update(glow=glow, gr=gr, zb=zb, la=la, gla=gla, states=states, oraw=oraw, o_gla=o_gla)
    mq = Cols(zb, 256, B_MQ // 256)
    mkv = Cols(zb, 128, B_MKV // 128)
    mkr2 = Cols(zb, 256, B_MKR // 256)
    qp, cqn = _mm(mq, w['uq'], mode='nn', out_dtype=F32, norm_g=p['g_mla_q'], emit_norm=True, name=nm('mla_uq'))
    kvp, ckvn = _mm(mkv, w['ukv'], mode='nn', out_dtype=BF16, norm_g=p['g_mla_kv'], emit_norm=True,
                    name=nm('mla_ukv'))

    def rope_fn(qv, kr, c4v, s4v):
        q_rope = qv[:, 256:384] * c4v + qv[:, 384:512] * s4v
        return jnp.concatenate([qv[:, 0:256], q_rope], axis=1), kr[:, 0:128] * c4v + kr[:, 128:256] * s4v

    qall, kr4 = _rowwise(rope_fn, [qp, mkr2, c4, s4], [], [(384, BF16), (128, BF16)], name=nm('rope'))
    mla = dict(qc=0, kc=0, vc=2, nb=2, g=2, scale=(MLA_NOPE + MLA_ROPE) ** -0.5, mode='chunk', qr=qall, qrc=2, kr=kr4)
    o_mla, lse_mla, *carried = _mattn_fwd(qall, kvp, kvp, name=nm('mla_attn'), comm=carry_mla, **mla)
    if carry_mla is not None:
        sv['carried_mla'] = carried[0]
    sv.update(mq=mq, mkv=mkv, cqn=cqn, ckvn=ckvn, qall=qall, kvp=kvp, mla=mla, o_mla=o_mla, lse_mla=lse_mla)
    of_m, om_m = o_fox, o_mla
    sv.update(of_m=of_m, om_m=om_m)
    b_br = p['b_branch']

    def first(acc, zg, bb):
        return _sig(zg + bb) * acc

    def more(acc, zg, bb, prev):
        return prev + _sig(zg + bb) * acc

    y = _mm(of_m, w['w_up_fox'], mode='nn', out_dtype=F32, name=nm('up_fox'), epilogue=first,
            extras=[(zc, *_mn(col_off=0)), (b_br, *_nvec(col_off=0))])
    y = _mm(o_gla, w['w_up_gla'], mode='nn', out_dtype=F32, name=nm('up_gla'), epilogue=more,
            extras=[(zc, *_mn(col_off=1024)), (b_br, *_nvec(col_off=1024)), (y, *_mn())])
    y = _mm(om_m, w['w_up_mla'], mode='nn', out_dtype=BF16, name=nm('up_mla'), epilogue=more,
            extras=[(zc, *_mn(col_off=2048)), (b_br, *_nvec(col_off=2048)), (y, *_mn())])
    add = lambda acc, res: res + acc
    x1 = _mm(y, w['w_out'], mode='nn', out_dtype=F32, name=nm('out'), epilogue=add, extras=[(x, *_mn())])
    sv.update(y=y, x1=x1)
    qx, hx = _mm(x1, w['w_xq'], mode='nn', out_dtype=BF16, norm_g=p['g_xa'], emit_norm=True, name=nm('xq'))
    kvx, mn = _mm(mem, w['w_xkv'], mode='nn', out_dtype=BF16, norm_g=p['g_mem'], emit_norm=True, name=nm('xkv'))
    xa = dict(qc=0, kc=0, vc=4, nb=4, g=1, scale=XA_HD ** -0.5, mode='full')
    ox_m, lse_x = _mattn_fwd(qx, kvx, kvx, name=nm('xa_attn'), **xa)
    x2 = _mm(ox_m, w['w_xo'], mode='nn', out_dtype=F32, name=nm('xo'), epilogue=add, extras=[(x1, *_mn())])
    sv.update(hx=hx, mn=mn, qx=qx, kvx=kvx, xa=xa, lse_x=lse_x, ox_m=ox_m, x2=x2)
    hpre, hm = _mm(x2, w['w_mlp1'], mode='nn', out_dtype=BF16, norm_g=p['g_mlp'], emit_norm=True, name=nm('mlp1'))
    relu2 = lambda t: jnp.square(jnp.maximum(t.astype(F32), 0.0))
    x3 = _mm(hpre, w['w_mlp2'], mode='nn', out_dtype=F32, name=nm('mlp2'), a_fn=relu2, epilogue=add,
             extras=[(x2, *_mn())])
    sv.update(hpre=hpre, hm=hm, w=w)
    return x3, sv


EARLY = ('w_mlp1', 'w_mlp2', 'w_xo', 'w_xq', 'w_xkv', 'w_out', 'w_up_fox', 'w_up_gla', 'w_up_mla')
LATE = ('w_in', 'w_gla_gate', 'w_mla_uq', 'w_mla_ukv')


def _layer_bwd(dx3, mem, w, p, tabs, sv, tag, carry_mla=None, early=None):
    c4, s4 = tabs
    nm = lambda t: f'{t}_{tag}'
    s = dx3.shape[0]
    gw, gs = {}, {}
    relu2 = lambda t: jnp.square(jnp.maximum(t.astype(F32), 0.0))
    gw['w_mlp2'] = _mm(sv['hpre'], dx3, mode='tn', out_dtype=F32, name=nm('d_mlp2'), a_fn=relu2)
    dact = lambda acc, hp: acc * (2.0 * jnp.maximum(hp.astype(F32), 0.0))
    dhpre = _mm(dx3, w['w_mlp2'], mode='nt', out_dtype=BF16, name=nm('d_act'), epilogue=dact,
                extras=[(sv['hpre'], *_mn())])
    gw['w_mlp1'] = _mm(sv['hm'], dhpre, mode='tn', out_dtype=F32, name=nm('d_mlp1'))
    dhm = _mm(dhpre, w['w_mlp1'], mode='nt', out_dtype=F32, name=nm('d_hm'))
    dx2, gs['g_mlp'] = _norm_bwd_call(sv['x2'], dhm, p['g_mlp'], dx3, nm('d_norm_mlp'))
    gw['w_xo'] = _mm(sv['ox_m'], dx2, mode='tn', out_dtype=F32, name=nm('d_xo'))
    dox = _mm(dx2, w['w_xo'], mode='nt', out_dtype=BF16, name=nm('d_ox'))
    dqx_m, dkx, dvx = _mattn_bwd(sv['qx'], sv['kvx'], sv['kvx'], sv['ox_m'], dox, sv['lse_x'], name=nm('xa_bwd'),
                                 **sv['xa'])
    dkvx = jnp.concatenate([dkx, dvx], axis=1).astype(BF16)
    gw['w_xq'] = _mm(sv['hx'], dqx_m, mode='tn', out_dtype=F32, name=nm('d_xq'))
    dhx = _mm(dqx_m, w['w_xq'], mode='nt', out_dtype=F32, name=nm('d_hx'))
    gw['w_xkv'] = _mm(sv['mn'], dkvx, mode='tn', out_dtype=F32, name=nm('d_xkv'))
    dmn = _mm(dkvx, w['w_xkv'], mode='nt', out_dtype=F32, name=nm('d_mn'))
    _, gs['g_mem'] = _norm_bwd_call(mem, dmn, p['g_mem'], None, nm('d_norm_mem'))
    dx1, gs['g_xa'] = _norm_bwd_call(sv['x1'], dhx, p['g_xa'], dx2, nm('d_norm_xa'))
    gw['w_out'] = _mm(sv['y'], dx1, mode='tn', out_dtype=F32, name=nm('d_out'))
    dy = _mm(dx1, w['w_out'], mode='nt', out_dtype=BF16, name=nm('d_y'))
    zc, b_br = sv['zc'], p['b_branch']

    def du_fn(dyv, zg, bb):
        g = _sig(zg + bb)
        d = dyv.astype(F32)
        return d * g[:, 0:1024], d * g[:, 1024:2048], d * g[:, 2048:3072]

    du = _rowwise(du_fn, [dy, zc], [b_br], [(D_MODEL, BF16)] * 3, name=nm('d_u'))

    def dgate(acc, dyv, zg, bb):
        g = _sig(zg + bb)
        return dyv.astype(F32) * acc * g * (1.0 - g)

    dzc, do_br = [], []
    for q, (o_m, wn) in enumerate(((sv['of_m'], 'w_up_fox'), (sv['o_gla'], 'w_up_gla'), (sv['om_m'], 'w_up_mla'))):
        dzc.append(_mm(o_m, w[wn], mode='nn', out_dtype=F32, name=nm(f'd_zg{q}'), epilogue=dgate,
                       extras=[(dy, *_mn()), (zc, *_mn(col_off=1024 * q)), (b_br, *_nvec(col_off=1024 * q))]))
        gw[wn] = _mm(o_m, du[q], mode='tn', out_dtype=F32, name=nm(f'd_up{q}'))
        do_br.append(_mm(du[q], w[wn], mode='nt', out_dtype=F32 if q == 1 else BF16, name=nm(f'd_o{q}')))
    dzc = jnp.concatenate(dzc, axis=1)
    (gs['b_branch'],) = _rowwise(lambda t: (t,), [dzc], [], [], [3072], name=nm('d_bbranch'))
    za = sv['za']
    carry_fox = None if early is None else early({nm_: gw[nm_] for nm_ in EARLY})
    dfq, dfk, dfv, dck, dcq, *carried_fox = _mattn_bwd(za, za, za, sv['o_fox'], do_br[0], sv['lse_fox'],
                                                       name=nm('fox_bwd'), comm=carry_fox, **sv['fox'])
    dcum = _padc(dck[:, :2, :].reshape(FOX_HEADS, s).T + dcq.reshape(s, 2, LANES)[:, :, :2].reshape(s, FOX_HEADS), 128)
    dlf = _cumsum_rows(dcum, reverse=True, name=nm('fox_dcum'))

    def dff_fn(dl, f, b):
        d = dl * _sig(-(f + b))
        return d, d

    dff, db_fox = _rowwise(dff_fn, [dlf, sv['ff']], [p['b_fox']], [(128, F32)], [128], name=nm('fox_dff'))
    gs['b_fox'] = db_fox
    dza = jnp.concatenate([dfq, dfk, dfv], axis=1).astype(BF16)
    dqn, dkn, dvv, dq_rope, dk_rope, *carried_mla = _mattn_bwd(sv['qall'], sv['kvp'], sv['kvp'], sv['o_mla'], do_br[2],
                                                               sv['lse_mla'], name=nm('mla_bwd'), comm=carry_mla,
                                                               **sv['mla'])

    def drope_fn(dn, dq, dk, c4v, s4v):
        return jnp.concatenate([dn, dq * c4v, dq * s4v], axis=1), jnp.concatenate([dk * c4v, dk * s4v], axis=1)

    dqp, dmkr2 = _rowwise(drope_fn, [dqn, dq_rope, dk_rope, c4, s4], [], [(512, BF16), (256, BF16)], name=nm('d_rope'))
    dkvp = jnp.concatenate([dkn, dvv], axis=1).astype(BF16)
    gw['uq'] = _mm(sv['cqn'], dqp, mode='tn', out_dtype=F32, name=nm('d_uq'))
    dcqn = _mm(dqp, w['uq'], mode='nt', out_dtype=F32, name=nm('d_cqn'))
    gw['ukv'] = _mm(sv['ckvn'], dkvp, mode='tn', out_dtype=F32, name=nm('d_ukv'))
    dckvn = _mm(dkvp, w['ukv'], mode='nt', out_dtype=F32, name=nm('d_ckvn'))
    dmq, gs['g_mla_q'] = _norm_bwd_call(sv['mq'], dcqn, p['g_mla_q'], None, nm('d_norm_q'))
    dmkv, gs['g_mla_kv'] = _norm_bwd_call(sv['mkv'], dckvn, p['g_mla_kv'], None, nm('d_norm_kv'))
    doraw, dgr, gs['g_gla_out'] = _rowwise(_gla_out_bwd, [sv['oraw'], sv['gr'], do_br[1]], [p['g_gla_out']],
                                           [(512, F32), (512, BF16)], [128], name=nm('d_gla_out'))
    st = sv['states']
    st_prev = jnp.concatenate([jnp.zeros_like(st[:, :1]), st[:, :-1]], axis=1)
    dgq, dgk, dgv, dla = _gla_bwd(sv['zb'], sv['la'], st, st_prev, doraw, name=nm('gla_bwd'), **sv['gla'])

    def dgate_fn(dl, gl, wg, bg):
        pre = _dot(gl.astype(BF16), wg) + bg
        dpre = dl * (1.0 / GLA_TAU) * _sig(-pre)
        return dpre, _dot(dpre.astype(BF16), wg, NT), dpre

    dpre, dglow, gs['b_gla'] = _rowwise(dgate_fn, [dla, sv['glow']], [w['gate'], p['b_gla']],
                                        [(256, BF16), (128, BF16)], [256], name=nm('d_gla_gate'))
    gw['gate'] = _mm(sv['glow'], dpre, mode='tn', out_dtype=F32, name=nm('d_wgate'))
    bf = lambda t: t.astype(BF16)
    dzb = jnp.concatenate([dgr, bf(dgq), bf(dgk), bf(dgv), bf(dmq), dmkr2, bf(dff), dglow, bf(dmkv),
                           jnp.zeros((s, B_W - B_END), BF16)], axis=1)
    h = sv['h']
    gw['in_a'] = _mm(h, dza, mode='tn', out_dtype=F32, name=nm('d_in_a'))
    gw['in_b'] = _mm(h, dzb, mode='tn', out_dtype=F32, name=nm('d_in_b'))
    gw['in_c'] = _mm(h, dzc, mode='tn', out_dtype=F32, name=nm('d_in_c'))
    add = lambda acc, prev: prev + acc
    dh = _mm(dza, w['in_a'], mode='nt', out_dtype=F32, name=nm('d_h_a'))
    dh = _mm(dzb, w['in_b'], mode='nt', out_dtype=F32, name=nm('d_h_b'), epilogue=add, extras=[(dh, *_mn())])
    dh = _mm(dzc, w['in_c'], mode='nt', out_dtype=F32, name=nm('d_h_c'), epilogue=add, extras=[(dh, *_mn())])
    dx0, gs['g_mix'] = _norm_bwd_call(sv['x0'], dh, p['g_mix'], dx1, nm('d_norm_mix'))
    return dx0, gw, gs, (carried_mla or [None])[0], (carried_fox or [None])[0]


def _loss_head(x, target, g_final):
    d = x.shape[1]

    def fn(xv, tv, gv):
        r = lax.rsqrt(jnp.mean(xv * xv, axis=-1, keepdims=True) + EPS)
        xh = xv * r
        e = xh * gv - tv
        dy = e * (1.0 / d)
        gd = dy * gv
        dx = r * (gd - xh * jnp.mean(gd * xh, axis=-1, keepdims=True))
        row_loss = 0.5 * jnp.mean(e * e, axis=-1, keepdims=True)
        return dx, dy * xh, jnp.broadcast_to(row_loss, (xv.shape[0], LANES))

    return _rowwise(fn, [x, target], [g_final], [(d, F32)], [d, LANES], name='loss_head')


def _small_sizes(shapes):
    return [math.prod(shapes[nm]) for nm in SMALL]


def _step(args):
    shapes = {nm: args[nm].shape for nm in ORDER}
    x, mem, target = args['x'][0], args['mem'][0], args['loss_target'][0]
    s = x.shape[0]

    def wire(nm, l):
        w = args[nm][l].astype(BF16)
        if nm == 'w_in':
            w = jnp.pad(w, ((0, 0), (0, WIN_PAD - WIN_SHARD)))
        if nm == 'w_gla_gate':
            w = jnp.pad(w, ((0, GATE_WIRE_ROWS - GLA_RANK), (0, 0)))
        return w

    axis_of = dict(BIG)
    names = tuple(nm for nm, _ in BIG)
    wires = lambda l, nms: [wire(nm, l) for nm in nms]

    def whole(parts, nms, tag):
        parts = _run_comm(_gather_over_d2d(parts), name=f'gather_d2d_{tag}', alias=True)
        full = {nm: _full_layer(p, axis_of[nm]) for nm, p in zip(nms, parts)}
        if 'w_gla_gate' in full:
            full['w_gla_gate'] = full['w_gla_gate'][:GLA_RANK]
        return full

    tabs = _rope_tables(s)
    layers_p = []
    for l in range(DEPTH):
        layers_p.append({
            'g_mix': args['g_mix'][l][None], 'b_fox': _padc(args['b_fox_forget'][l][None], 128),
            'b_gla': args['b_gla_gate'][l][None], 'g_gla_out': args['g_gla_out'][l][None],
            'g_mla_q': args['g_mla_q'][l][None], 'g_mla_kv': args['g_mla_kv'][l][None],
            'b_branch': args['b_branch_gate'][l][None], 'g_xa': args['g_xa'][l][None],
            'g_mem': args['g_mem'][l][None], 'g_mlp': args['g_mlp'][l][None]})

    first = _run_comm(_gather_over_ici(wires(0, LATE)), name='gather_ici_first_l0')
    w_now = _repack_layer_weights(whole(first, LATE, 'first_l0'))
    saved = []
    xl = x
    for l in range(DEPTH):
        carry_fox = _gather_over_ici(wires(0, EARLY)) if l == 0 else None
        after_fox = (lambda parts: whole(parts, EARLY, 'rest_l0')) if l == 0 else None
        carry_mla = _gather_over_ici(wires(l + 1, names)) if l + 1 < DEPTH else None
        xl, sv = _layer_fwd(xl, mem, w_now, layers_p[l], tabs, f'l{l}', carry_fox=carry_fox, after_fox=after_fox,
                            carry_mla=carry_mla)
        saved.append(sv)
        if carry_mla is not None:
            w_now = _repack_layer_weights(whole(sv.pop('carried_mla'), names, f'l{l + 1}'))
    dx, dg_final, loss_lanes = _loss_head(xl, target, args['g_final'][None])
    cidx = lax.axis_index('c')
    chip = 2 * lax.axis_index('x') + lax.axis_index('y')

    def pair_sums(gw, nms, tag):
        mine, theirs = [], []
        for nm in nms:
            shards = _split_full(gw[nm], axis_of[nm]).astype(BF16)
            h = shards.shape[1] // 2
            mine.append(lax.dynamic_slice_in_dim(shards, cidx * h, h, axis=1))
            theirs.append(lax.dynamic_slice_in_dim(shards, (1 - cidx) * h, h, axis=1))
        got = _to_sibling(theirs, name=f'grads_swap_{tag}')
        pairs = []
        for nm, a, b in zip(nms, mine, got):
            _, h, n = a.shape
            (p,) = _rowwise(lambda u, v: (u.astype(F32) + v.astype(F32),),
                            [a.reshape(N_CHIPS * h, n), b.reshape(N_CHIPS * h, n)], [], [(n, BF16)],
                            name=f'pair_sum_{nm}_{tag}')
            pairs.append(p.reshape(N_CHIPS, h, n))
        return pairs

    def finish(pairs, from_chips, nms, tag):
        own = [lax.dynamic_index_in_dim(p, chip, axis=0, keepdims=False) for p in pairs]
        mine = [_sum_chips(o, r, name=f'chip_sum_{nm}_{tag}') for nm, o, r in zip(nms, own, from_chips)]
        theirs = _to_sibling(mine, name=f'grads_join_{tag}')
        return {nm: jnp.where(cidx == 0, jnp.concatenate([a, b]), jnp.concatenate([b, a]))
                for nm, a, b in zip(nms, mine, theirs)}

    gs_layers, done = [None] * DEPTH, [{} for _ in range(DEPTH)]
    above = None
    for l in reversed(range(DEPTH)):
        lowest, early_pairs = l == 0, []

        def early(gw_early, l=l, early_pairs=early_pairs):
            early_pairs.extend(pair_sums(gw_early, EARLY, f'early_l{l}'))
            return _chip_exchange(early_pairs)

        carry_mla = None if above is None else _chip_exchange(above[1])
        dx, gw, gs_layers[l], got_mla, got_fox = _layer_bwd(
            dx, mem, saved[l]['w'], layers_p[l], tabs, saved[l], f'l{l}', carry_mla=carry_mla,
            early=early if lowest else None)
        if above is not None:
            done[above[0]].update(finish(above[1], got_mla, names, f'l{above[0]}'))
        grads = _unpack_layer_grads(gw)
        if lowest:
            done[l].update(finish(early_pairs, got_fox, EARLY, f'early_l{l}'))
            late_pairs = pair_sums(grads, LATE, f'late_l{l}')
            from_late = _run_comm(_chip_exchange(late_pairs), name=f'grads_exchange_late_l{l}')
            done[l].update(finish(late_pairs, from_late, LATE, f'late_l{l}'))
        else:
            above = (l, pair_sums(grads, names, f'l{l}'))
    grad_x = dx[None]
    gshard = {nm: jnp.stack([done[l][nm] for l in range(DEPTH)]) for nm in names}

    small_g = []
    for nm, key in (('g_mix', 'g_mix'), ('b_fox_forget', 'b_fox'), ('b_gla_gate', 'b_gla'),
                    ('g_gla_out', 'g_gla_out'), ('g_mla_q', 'g_mla_q'), ('g_mla_kv', 'g_mla_kv'),
                    ('b_branch_gate', 'b_branch'), ('g_xa', 'g_xa'), ('g_mem', 'g_mem'), ('g_mlp', 'g_mlp')):
        width = shapes[nm][1]
        small_g.append(jnp.concatenate([gs_layers[l][key][0, :width] for l in range(DEPTH)]))
    small_g.append(dg_final[0])
    small_g.append(loss_lanes[0, :1])
    flat = jnp.concatenate(small_g)
    n_small = flat.shape[0]
    srows = -(-n_small // (8 * LANES)) * 8
    pad = lambda v: jnp.pad(v, (0, srows * LANES - v.shape[0])).reshape(srows, LANES)
    all_small = _all_gather8(pad(flat), name='gather_small')
    sw, sm, svv = (pad(jnp.concatenate([args[pre + nm].reshape(-1) for nm in SMALL] + [jnp.zeros((1,), F32)]))
                   for pre in ('', 'm_', 'v_'))

    def small_body(g_ref, w_ref, m_ref, v_ref, go_ref, d_ref, mo_ref, vo_ref):
        g = g_ref[0]
        for q in range(1, N_DEV):
            g = g + g_ref[q]
        go_ref[...] = g
        d_ref[...], mo_ref[...], vo_ref[...] = _adam(w_ref[...], g, m_ref[...], v_ref[...])

    sg, sd, snm, snv = pl.pallas_call(
        small_body, name='small_sum_adam', out_shape=[jax.ShapeDtypeStruct((srows, LANES), F32)] * 4,
        compiler_params=pltpu.CompilerParams(vmem_limit_bytes=VMEM_LIMIT))(all_small, sw, sm, svv)

    def unsmall(buf):
        v, out, off = buf.reshape(-1), {}, 0
        for nm in SMALL:
            nel = math.prod(shapes[nm])
            out[nm] = v[off:off + nel].reshape(shapes[nm])
            off += nel
        return out, v[off]

    res = {}
    (res['grad'], loss), (res['delta'], _), (res['m'], _), (res['v'], _) = (unsmall(t) for t in (sg, sd, snm, snv))

    for nm, _ in BIG:
        shp = args[nm].shape
        view = lambda t: t.reshape(shp[0] * shp[1], shp[2])
        d, m2, v2 = _rowwise(_adam, [view(args[nm]), view(gshard[nm]), view(args['m_' + nm]), view(args['v_' + nm])],
                             [], [(shp[2], F32)] * 3, name=f'adam_{nm}')
        res['grad'][nm], res['delta'][nm], res['m'][nm], res['v'][nm] = (
            gshard[nm], d.reshape(shp), m2.reshape(shp), v2.reshape(shp))

    return (loss, grad_x, *[res['grad'][nm] for nm in ORDER], *[res['delta'][nm] for nm in ORDER],
            *[res['m'][nm] for nm in ORDER], *[res['v'][nm] for nm in ORDER])


def kernel(x, mem, g_mix, w_in, b_fox_forget, w_gla_gate, b_gla_gate, g_gla_out, g_mla_q, w_mla_uq, g_mla_kv, w_mla_ukv, b_branch_gate, w_up_fox, w_up_gla, w_up_mla, w_out, g_xa, g_mem, w_xq, w_xkv, w_xo, g_mlp, w_mlp1, w_mlp2, g_final, loss_target, m_g_mix, m_w_in, m_b_fox_forget, m_w_gla_gate, m_b_gla_gate, m_g_gla_out, m_g_mla_q, m_w_mla_uq, m_g_mla_kv, m_w_mla_ukv, m_b_branch_gate, m_w_up_fox, m_w_up_gla, m_w_up_mla, m_w_out, m_g_xa, m_g_mem, m_w_xq, m_w_xkv, m_w_xo, m_g_mlp, m_w_mlp1, m_w_mlp2, m_g_final, v_g_mix, v_w_in, v_b_fox_forget, v_w_gla_gate, v_b_gla_gate, v_g_gla_out, v_g_mla_q, v_w_mla_uq, v_g_mla_kv, v_w_mla_ukv, v_b_branch_gate, v_w_up_fox, v_w_up_gla, v_w_up_mla, v_w_out, v_g_xa, v_g_mem, v_w_xq, v_w_xkv, v_w_xo, v_g_mlp, v_w_mlp1, v_w_mlp2, v_g_final):
    return _step(dict(locals()))
```

```python
import functools
import math
import typing

import jax
import jax.numpy as jnp
from jax import lax
from jax.experimental import pallas as pl
from jax.experimental.pallas import tpu as pltpu

F32 = jnp.float32
BF16 = jnp.bfloat16
MESH = pl.DeviceIdType.MESH

D_MODEL = 1024
DEPTH = 2
CHUNK = 64
EPS = 1e-6
FOX_HEADS, FOX_HD = 4, 64
GLA_HEADS, GLA_DK, GLA_DV, GLA_RANK, GLA_TAU = 4, 64, 128, 16, 16.0
MLA_HEADS, MLA_Q_RANK, MLA_KV_RANK, MLA_NOPE, MLA_ROPE, MLA_VD = 4, 256, 128, 64, 32, 64
ROPE_BASE = 10000.0
XA_HEADS, XA_HD = 4, 128
D_FF = 4 * D_MODEL
IN_SIZES = (256, 256, 256, 4, 256, 256, 512, 16, 512, 256, 128, 32, 3072)
N_IN = sum(IN_SIZES)

ADAM_LR, ADAM_B1, ADAM_B2, ADAM_EPS, ADAM_WD, ADAM_STEP = 0.001, 0.9, 0.999, 1e-08, 0.01, 10

N_CHIPS = 4
N_DEV = 8
LANES = 128
VMEM_LIMIT = 48 * 1024 * 1024
MASK_VALUE = -1e30

BIG = (('w_in', 2), ('w_gla_gate', 2), ('w_mla_uq', 2), ('w_mla_ukv', 2), ('w_up_fox', 2), ('w_up_gla', 2),
       ('w_up_mla', 2), ('w_out', 1), ('w_xq', 1), ('w_xkv', 1), ('w_xo', 2), ('w_mlp1', 2), ('w_mlp2', 1))
SMALL = ('g_mix', 'b_fox_forget', 'b_gla_gate', 'g_gla_out', 'g_mla_q', 'g_mla_kv', 'b_branch_gate',
         'g_xa', 'g_mem', 'g_mlp', 'g_final')
ORDER = ('g_mix', 'w_in', 'b_fox_forget', 'w_gla_gate', 'b_gla_gate', 'g_gla_out', 'g_mla_q', 'w_mla_uq',
         'g_mla_kv', 'w_mla_ukv', 'b_branch_gate', 'w_up_fox', 'w_up_gla', 'w_up_mla', 'w_out', 'g_xa', 'g_mem',
         'w_xq', 'w_xkv', 'w_xo', 'g_mlp', 'w_mlp1', 'w_mlp2', 'g_final')


def _params(*sem):
    return pltpu.CompilerParams(dimension_semantics=sem, vmem_limit_bytes=VMEM_LIMIT)


def _sig(x):
    return 1.0 / (1.0 + jnp.exp(-x))


def _logsig(x):
    return jnp.minimum(x, 0.0) - jnp.log(1.0 + jnp.exp(-jnp.abs(x)))


NN = (((1,), (0,)), ((), ()))
NT = (((1,), (1,)), ((), ()))
TN = (((0,), (0,)), ((), ()))


def _dot(a, b, dims=NN):
    return lax.dot_general(a, b, dims, preferred_element_type=F32)


class Cols(typing.NamedTuple):
    arr: jax.Array
    width: int
    blk: int


def _tri_dot(tri, x):
    hi = x.astype(BF16)
    r1 = x - hi.astype(F32)
    mid = r1.astype(BF16)
    lo = (r1 - mid.astype(F32)).astype(BF16)
    return _dot(tri, hi) + _dot(tri, mid) + _dot(tri, lo)


MM_TILES = ((1024, 1024), (1024, 512), (512, 1024), (512, 512), (512, 256), (256, 512), (256, 256), (128, 128))
MM_VMEM_BUDGET = 38 * 1024 * 1024


def _mm_tiles(m, n, k, a_bytes, b_bytes, out_bytes, ex_bytes, has_norm, emit_norm, has_fn):
    for tm, tn in MM_TILES:
        tm, tn = min(tm, m), min(tn, n)
        if m % tm or n % tn:
            continue
        blocks = tm * k * a_bytes + k * tn * b_bytes + tm * tn * (out_bytes + ex_bytes) + (tm * k * 2 if emit_norm else 0)
        temps = tm * tn * 4 + (tm * k * 2 if has_norm else 0) + (tm * k * 6 if has_fn or has_norm else 0)
        if 2 * blocks + temps <= MM_VMEM_BUDGET:
            return tm, tn
    raise ValueError((m, n, k))


def _mm(a, b, *, mode, out_dtype, name, norm_g=None, emit_norm=False, a_fn=None, extras=(), epilogue=None):
    a_blk = 0
    if isinstance(a, Cols):
        a, width, a_blk = a
        a_shape = (a.shape[0], width)
    else:
        a_shape = a.shape
    if mode == 'tn':
        k, m = a_shape
    else:
        m, k = a_shape
    n = b.shape[0] if mode == 'nt' else b.shape[1]
    assert (b.shape[1] if mode == 'nt' else b.shape[0]) == k, (name, a.shape, b.shape)
    has_norm = norm_g is not None
    ex_bytes = sum(arr.dtype.itemsize for arr, kind, _ in extras if kind == 'mn')
    tm, tn = _mm_tiles(m, n, k, a.dtype.itemsize, b.dtype.itemsize, jnp.dtype(out_dtype).itemsize, ex_bytes, has_norm,
                       emit_norm, a_fn is not None)
    assert all(col % tn == 0 for _, _, col in extras), (name, tn)
    assert a_blk == 0 or (mode == 'nn') or (mode == 'tn' and tm == m)
    if mode == 'tn':
        a_spec = pl.BlockSpec((k, tm), lambda i, j: (0, i + a_blk))
    else:
        a_spec = pl.BlockSpec((tm, k), lambda i, j: (i, a_blk))
    b_spec = pl.BlockSpec((tn, k), lambda i, j: (j, 0)) if mode == 'nt' else pl.BlockSpec((k, tn), lambda i, j: (0, j))
    dims = {'nn': NN, 'nt': NT, 'tn': TN}[mode]
    assert not (has_norm and mode != 'nn')
    n_ex = len(extras)

    def body(*refs):
        a_ref, b_ref = refs[0], refs[1]
        pos = 2
        g_ref = None
        if has_norm:
            g_ref = refs[pos]
            pos += 1
        ex_refs = refs[pos:pos + n_ex]
        pos += n_ex
        o_ref = refs[pos]
        pos += 1
        h_ref = None
        if emit_norm:
            h_ref = refs[pos]
            pos += 1
        if has_norm:
            an_ref = refs[pos]

            @pl.when(pl.program_id(1) == 0)
            def _():
                xf = a_ref[...].astype(F32)
                y = xf * lax.rsqrt(jnp.mean(xf * xf, axis=-1, keepdims=True) + EPS) * g_ref[...]
                an_ref[...] = y.astype(BF16)
                if emit_norm:
                    h_ref[...] = y.astype(BF16)

            av = an_ref[...]
        else:
            av = a_ref[...]
            if a_fn is not None:
                av = a_fn(av)
            av = av.astype(BF16)
        acc = _dot(av, b_ref[...].astype(BF16), dims)
        if epilogue is not None:
            acc = epilogue(acc, *[r[...] for r in ex_refs])
        o_ref[...] = acc.astype(out_dtype)

    in_specs = [a_spec, b_spec]
    args = [a, b]
    if has_norm:
        in_specs.append(pl.BlockSpec((1, k), lambda i, j: (0, 0)))
        args.append(norm_g)
    for arr, kind, col in extras:
        if kind == 'mn':
            in_specs.append(pl.BlockSpec((tm, tn), lambda i, j, o=col // tn: (i, j + o)))
        else:
            in_specs.append(pl.BlockSpec((1, tn), lambda i, j, o=col // tn: (0, j + o)))
        args.append(arr)
    out_shape = [jax.ShapeDtypeStruct((m, n), out_dtype)]
    out_specs = [pl.BlockSpec((tm, tn), lambda i, j: (i, j))]
    if emit_norm:
        out_shape.append(jax.ShapeDtypeStruct((m, k), BF16))
        out_specs.append(pl.BlockSpec((tm, k), lambda i, j: (i, 0)))
    scratch = [pltpu.VMEM((tm, k), BF16)] if has_norm else []
    res = pl.pallas_call(
        body, name=name, grid=(m // tm, n // tn), in_specs=in_specs, out_specs=out_specs, out_shape=out_shape,
        scratch_shapes=scratch, compiler_params=_params('arbitrary', 'arbitrary'))(*args)
    return res if emit_norm else res[0]


def _mn(col_off=0):
    return 'mn', col_off


def _nvec(col_off=0):
    return 'n', col_off


def _rowwise(fn, rows, consts, outs, sums=(), *, name, ts=256):
    views = [x if isinstance(x, Cols) else Cols(x, x.shape[1], 0) for x in rows]
    rows = [v.arr for v in views]
    r = rows[0].shape[0]
    ts = min(ts, r)
    assert r % ts == 0, (name, r, ts)
    nr, nc, no, ns = len(rows), len(consts), len(outs), len(sums)

    def body(*refs):
        vals = fn(*[x[...] for x in refs[:nr + nc]])
        for q in range(no):
            refs[nr + nc + q][...] = vals[q].astype(outs[q][1])
        if ns:
            @pl.when(pl.program_id(0) == 0)
            def _():
                for q in range(ns):
                    refs[nr + nc + no + q][...] = jnp.zeros((1, sums[q]), F32)

            for q in range(ns):
                refs[nr + nc + no + q][...] += jnp.sum(vals[no + q].astype(F32), axis=0, keepdims=True)

    in_specs = [pl.BlockSpec((ts, v.width), lambda i, blk=v.blk: (i, blk)) for v in views]
    in_specs += [pl.BlockSpec(x.shape, lambda i, nd=x.ndim: (0,) * nd) for x in consts]
    out_specs = [pl.BlockSpec((ts, w), lambda i: (i, 0)) for w, _ in outs]
    out_specs += [pl.BlockSpec((1, w), lambda i: (0, 0)) for w in sums]
    out_shape = [jax.ShapeDtypeStruct((r, w), dt) for w, dt in outs]
    out_shape += [jax.ShapeDtypeStruct((1, w), F32) for w in sums]
    return pl.pallas_call(body, name=name, grid=(r // ts,), in_specs=in_specs, out_specs=out_specs,
                          out_shape=out_shape, compiler_params=_params('arbitrary'))(*rows, *consts)


def _cumsum_rows(x, *, reverse, name, bs=256):
    s, w = x.shape
    bs = min(bs, s)
    nb = s // bs

    def body(x_ref, o_ref, carry):
        @pl.when(pl.program_id(0) == 0)
        def _():
            carry[...] = jnp.zeros_like(carry)

        r = lax.broadcasted_iota(jnp.int32, (bs, bs), 0)
        c = lax.broadcasted_iota(jnp.int32, (bs, bs), 1)
        tri = jnp.where((c >= r) if reverse else (c <= r), 1.0, 0.0).astype(BF16)
        xv = x_ref[...]
        o_ref[...] = _tri_dot(tri, xv) + carry[...]
        carry[...] += jnp.sum(xv, axis=0, keepdims=True)

    imap = (lambda i: (nb - 1 - i, 0)) if reverse else (lambda i: (i, 0))
    return pl.pallas_call(body, name=name, grid=(nb,), in_specs=[pl.BlockSpec((bs, w), imap)],
                          out_specs=pl.BlockSpec((bs, w), imap), out_shape=jax.ShapeDtypeStruct((s, w), F32),
                          scratch_shapes=[pltpu.VMEM((1, w), F32)], compiler_params=_params('arbitrary'))(x)


def _mask(mode, q0, k0, bq, bk):
    qpos = q0 + lax.broadcasted_iota(jnp.int32, (bq, bk), 0)
    kpos = k0 + lax.broadcasted_iota(jnp.int32, (bq, bk), 1)
    if mode == 'causal':
        return kpos <= qpos
    return kpos < (jnp.right_shift(qpos, int(math.log2(CHUNK))) + 1) * CHUNK


ROPE_SHIFT = int(math.log2(MLA_ROPE))
ATTN_ROW_SLAB = 512


def _lane_masks(g, b, rope):
    lane = lax.broadcasted_iota(jnp.int32, (1, LANES), 1)
    heads = [None if g == 1 else (lane >= hh * (LANES // g)) & (lane < (hh + 1) * (LANES // g)) for hh in range(g)]
    ropes = [jnp.right_shift(lane, ROPE_SHIFT) == b * g + hh for hh in range(g)] if rope else [None] * g
    return heads, ropes


def _sel(mask, x):
    return x if mask is None else jnp.where(mask, x, jnp.zeros_like(x))


class Step(typing.NamedTuple):
    qi: typing.Any
    kj: typing.Any
    first: typing.Any
    last: typing.Any
    plain: typing.Any
    masked: typing.Any


def _fwd_steps(tri, nq, nk):
    if not tri:
        return (nq, nk), lambda i, j: Step(i, j, j == 0, j == nk - 1, True, False)
    if nq % 2:
        return (nq, nk), lambda i, j: Step(i, jnp.minimum(i, j), j == 0, j == nk - 1, j < i, j == i)

    def at(i, t):
        low = t <= i
        diag = (t == i) | (t == nq)
        return Step(jnp.where(low, i, nq - 1 - i), jnp.where(low, t, t - (i + 1)), (t == 0) | (t == i + 1), diag,
                    jnp.logical_not(diag), diag)

    return (nq // 2, nq + 1), at


def _bwd_steps(tri, nq, nk):
    if not tri:
        return (nk, nq), lambda j, i: Step(i, j, i == 0, i == nq - 1, True, False)
    if nk % 2:
        return (nk, nq), lambda j, i: Step(jnp.maximum(i, j), j, i == 0, i == nq - 1, i > j, i == j)

    def at(j, t):
        n1 = nq - j
        low = t < n1
        diag = (t == 0) | (t == n1)
        return Step(jnp.where(low, j + t, nk - 1 - j + t - n1), jnp.where(low, j, nk - 1 - j), diag,
                    (t == n1 - 1) | (t == nq), jnp.logical_not(diag), diag)

    return (nk // 2, nq + 1), at


def _carried(comm, refs, n_in, n_out):
    ci, co = len(comm.ins), len(comm.out_shapes)
    ins = refs[n_in:n_in + ci]
    outs = refs[n_in + ci + n_out:n_in + ci + n_out + co]
    rest = refs[:n_in] + refs[n_in + ci:n_in + ci + n_out] + refs[n_in + ci + n_out + co:-2]
    return rest, (ins, outs, refs[-2], refs[-1])


def _mattn_fwd(q, k, v, *, qc, kc, vc, nb, g, scale, mode, name, ck=None, qr=None, qrc=0, kr=None, blk=512,
               comm=None):
    s, t = q.shape[0], k.shape[0]
    bq, bk = min(blk, s), min(blk, t)
    nq, nk = s // bq, t // bk
    tri = mode != 'full'
    bias, rope = ck is not None, qr is not None
    assert not tri or (bq == bk and bq % CHUNK == 0)
    rs = min(ATTN_ROW_SLAB, bq)
    n_in = 3 + bias + 2 * rope
    (n1, n2), step_at = _fwd_steps(tri, nq, nk)

    def body(*refs):
        refs = list(refs)
        b, p1, p2 = pl.program_id(0), pl.program_id(1), pl.program_id(2)
        st = step_at(p1, p2)
        i, j = st.qi, st.kj
        if comm is not None:
            refs, comm_refs = _carried(comm, refs, n_in, 2)
            pl.when((b == 0) & (p1 == 0) & (p2 == 0))(lambda: comm.start(*comm_refs))
        q_ref, k_ref, v_ref = refs[:3]
        pos = 3
        ck_ref = qr_ref = kr_ref = None
        if bias:
            ck_ref = refs[pos]
            pos += 1
        if rope:
            qr_ref, kr_ref = refs[pos:pos + 2]
            pos += 2
        o_ref, lse_ref, m_s, l_s, acc_s = refs[pos:]
        heads, ropes = _lane_masks(g, b, rope)

        @pl.when(st.first)
        def _():
            m_s[...] = jnp.full_like(m_s, MASK_VALUE)
            l_s[...] = jnp.zeros_like(l_s)
            acc_s[...] = jnp.zeros_like(acc_s)

        def compute(masked):
            k2, v2 = k_ref[...], v_ref[...]
            for r in range(bq // rs):
                rows = pl.ds(r * rs, rs)
                q2 = q_ref[rows, :]
                alphas, pvs = [], []
                for hh in range(g):
                    sc = _dot(_sel(heads[hh], q2), k2, NT)
                    if rope:
                        sc = sc + _dot(_sel(ropes[hh], qr_ref[rows, :]), kr_ref[...], NT)
                    sc = sc * scale
                    if bias:
                        sc = sc - ck_ref[0, hh:hh + 1, :]
                    if masked:
                        sc = jnp.where(_mask(mode, i * bq + r * rs, j * bk, rs, bk), sc, MASK_VALUE)
                    m_prev = m_s[hh, rows]
                    m_new = jnp.maximum(m_prev, jnp.max(sc, axis=1, keepdims=True))
                    alpha = jnp.exp(m_prev - m_new)
                    p = jnp.exp(sc - m_new)
                    l_s[hh, rows] = alpha * l_s[hh, rows] + jnp.sum(p, axis=1, keepdims=True)
                    m_s[hh, rows] = m_new
                    alphas.append(alpha)
                    pvs.append(_dot(p.astype(BF16), _sel(heads[hh], v2)))
                alpha = alphas[0]
                for hh in range(1, g):
                    alpha = jnp.where(heads[hh], alphas[hh], alpha)
                acc_s[rows, :] = acc_s[rows, :] * alpha + sum(pvs[1:], pvs[0])

        if tri:
            pl.when(st.plain)(functools.partial(compute, False))
            pl.when(st.masked)(functools.partial(compute, True))
        else:
            compute(False)

        @pl.when(st.last)
        def _():
            lane = lax.broadcasted_iota(jnp.int32, (bq, LANES), 1)
            l_full, lse = l_s[0], jnp.zeros((bq, LANES), F32)
            for hh in range(g):
                if hh:
                    l_full = jnp.where(heads[hh], l_s[hh], l_full)
                lse = jnp.where(lane == hh, m_s[hh] + jnp.log(l_s[hh]), lse)
            o_ref[...] = (acc_s[...] / l_full).astype(o_ref.dtype)
            lse_ref[...] = lse

        if comm is not None:
            pl.when((b == nb - 1) & (p1 == n1 - 1) & (p2 == n2 - 1))(lambda: comm.finish(*comm_refs))

    qi = lambda p1, p2: step_at(p1, p2).qi
    kj = lambda p1, p2: step_at(p1, p2).kj
    in_specs = [pl.BlockSpec((bq, LANES), lambda b, p1, p2: (qi(p1, p2), qc + b)),
                pl.BlockSpec((bk, LANES), lambda b, p1, p2: (kj(p1, p2), kc + b)),
                pl.BlockSpec((bk, LANES), lambda b, p1, p2: (kj(p1, p2), vc + b))]
    args = [q, k, v]
    if bias:
        in_specs.append(pl.BlockSpec((1, 8, bk), lambda b, p1, p2: (b, 0, kj(p1, p2))))
        args.append(ck)
    if rope:
        in_specs += [pl.BlockSpec((bq, LANES), lambda b, p1, p2: (qi(p1, p2), qrc)),
                     pl.BlockSpec((bk, LANES), lambda b, p1, p2: (kj(p1, p2), 0))]
        args += [qr, kr]
    out = pl.BlockSpec((bq, LANES), lambda b, p1, p2: (qi(p1, p2), b))
    out_specs = [out, out]
    out_shape = [jax.ShapeDtypeStruct((s, LANES * nb), BF16), jax.ShapeDtypeStruct((s, LANES * nb), F32)]
    scratch = [pltpu.VMEM((g, bq, 1), F32), pltpu.VMEM((g, bq, 1), F32), pltpu.VMEM((bq, LANES), F32)]
    if comm is not None:
        in_specs += [ANY] * len(comm.ins)
        args += comm.ins
        out_specs += [ANY] * len(comm.out_shapes)
        out_shape += comm.out_shapes
        scratch += _sems(comm.n_sems, comm.n_sems)
    res = pl.pallas_call(body, name=name, grid=(nb, n1, n2), in_specs=in_specs, out_specs=out_specs, out_shape=out_shape,
                         scratch_shapes=scratch, compiler_params=_params('arbitrary', 'arbitrary', 'arbitrary'))(*args)
    return res if comm is None else (res[0], res[1], res[2:])


def _mattn_bwd(q, k, v, o, do, lse, *, qc, kc, vc, nb, g, scale, mode, name, ck=None, qr=None, qrc=0, kr=None,
               blk=512, comm=None):
    s, t = q.shape[0], k.shape[0]
    bq, bk = min(blk, s), min(blk, t)
    nq, nk = s // bq, t // bk
    tri = mode != 'full'
    bias, rope = ck is not None, qr is not None
    rs = min(ATTN_ROW_SLAB, bq)
    n_in, n_out = 6 + bias + 2 * rope, 3 + 2 * bias + 2 * rope
    (n1, n2), step_at = _bwd_steps(tri, nq, nk)

    def body(*refs):
        refs = list(refs)
        if comm is not None:
            refs, comm_refs = _carried(comm, refs, n_in, n_out)
            first = (pl.program_id(0) == 0) & (pl.program_id(1) == 0) & (pl.program_id(2) == 0)
            pl.when(first)(lambda: comm.start(*comm_refs))
        q_ref, k_ref, v_ref, o_ref, do_ref, lse_ref = refs[:6]
        pos = 6
        ck_ref = qr_ref = kr_ref = dck_ref = dcq_ref = dqr_ref = dkr_ref = dck_s = None
        if bias:
            ck_ref = refs[pos]
            pos += 1
        if rope:
            qr_ref, kr_ref = refs[pos:pos + 2]
            pos += 2
        dq_ref, dk_ref, dv_ref = refs[pos:pos + 3]
        pos += 3
        if bias:
            dck_ref, dcq_ref = refs[pos:pos + 2]
            pos += 2
        if rope:
            dqr_ref, dkr_ref = refs[pos:pos + 2]
            pos += 2
        dk_s, dv_s = refs[pos:pos + 2]
        if bias:
            dck_s = refs[pos + 2]
        b, p1, p2 = pl.program_id(0), pl.program_id(1), pl.program_id(2)
        st = step_at(p1, p2)
        i, j = st.qi, st.kj
        heads, ropes = _lane_masks(g, b, rope)

        @pl.when((p1 == 0) & (p2 == 0))
        def _():
            dq_ref[...] = jnp.zeros_like(dq_ref)
            if bias:
                dcq_ref[...] = jnp.zeros_like(dcq_ref)

        if rope:
            @pl.when((b == 0) & (p1 == 0) & (p2 == 0))
            def _():
                dqr_ref[...] = jnp.zeros_like(dqr_ref)
                dkr_ref[...] = jnp.zeros_like(dkr_ref)

        @pl.when(st.first)
        def _():
            dk_s[...] = jnp.zeros_like(dk_s)
            dv_s[...] = jnp.zeros_like(dv_s)
            if bias:
                dck_s[...] = jnp.zeros_like(dck_s)

        def compute(masked):
            k2, v2 = k_ref[...], v_ref[...]
            lane = lax.broadcasted_iota(jnp.int32, (rs, LANES), 1)
            rk = pl.ds(pl.multiple_of(j * bk, bk), bk)
            add = lambda tot, x: x if tot is None else tot + x
            dv_t = dk_t = dkr_t = None
            dck_t = [None] * g
            for r in range(bq // rs):
                rows = pl.ds(r * rs, rs)
                rq = pl.ds(pl.multiple_of(i * bq + r * rs, rs), rs)
                q2, do2, lse2 = q_ref[rows, :], do_ref[rows, :], lse_ref[rows, :]
                dd = do2.astype(F32) * o_ref[rows, :].astype(F32)
                dq_t = dqr_t = dcq_t = None
                for hh in range(g):
                    qm = _sel(heads[hh], q2)
                    sc = _dot(qm, k2, NT)
                    if rope:
                        qrm = _sel(ropes[hh], qr_ref[rows, :])
                        sc = sc + _dot(qrm, kr_ref[...], NT)
                    sc = sc * scale
                    if bias:
                        sc = sc - ck_ref[0, hh:hh + 1, :]
                    if masked:
                        sc = jnp.where(_mask(mode, i * bq + r * rs, j * bk, rs, bk), sc, MASK_VALUE)
                    p = jnp.exp(sc - jnp.sum(jnp.where(lane == hh, lse2, 0.0), axis=1, keepdims=True))
                    dom = _sel(heads[hh], do2)
                    dp = _dot(dom, v2, NT)
                    delta = jnp.sum(_sel(heads[hh], dd), axis=1, keepdims=True)
                    ds = p * (dp - delta)
                    dsb = ds.astype(BF16)
                    dv_t = add(dv_t, _dot(p.astype(BF16), dom, TN))
                    dk_t = add(dk_t, _dot(dsb, qm, TN))
                    dq_t = add(dq_t, _dot(dsb, _sel(heads[hh], k2)))
                    if rope:
                        dqr_t = add(dqr_t, _dot(dsb, _sel(ropes[hh], kr_ref[...])))
                        dkr_t = add(dkr_t, _dot(dsb, qrm, TN))
                    if bias:
                        dck_t[hh] = add(dck_t[hh], jnp.sum(ds, axis=0, keepdims=True))
                        dcq_t = add(dcq_t, jnp.where(lane == hh, jnp.sum(ds, axis=1, keepdims=True), 0.0))
                dq_ref[rq, :] += scale * dq_t
                if rope:
                    dqr_ref[rq, :] += scale * dqr_t
                if bias:
                    dcq_ref[rq, :] += dcq_t
            dv_s[...] += dv_t
            dk_s[...] += scale * dk_t
            if rope:
                dkr_ref[rk, :] += scale * dkr_t
            if bias:
                for hh in range(g):
                    dck_s[hh:hh + 1, :] -= dck_t[hh]

        if tri:
            pl.when(st.plain)(functools.partial(compute, False))
            pl.when(st.masked)(functools.partial(compute, True))
        else:
            compute(False)

        @pl.when(st.last)
        def _():
            dk_ref[...] = dk_s[...]
            dv_ref[...] = dv_s[...]
            if bias:
                dck_ref[0] = dck_s[...]

        if comm is not None:
            pl.when((b == nb - 1) & (p1 == n1 - 1) & (p2 == n2 - 1))(lambda: comm.finish(*comm_refs))

    qrow = lambda col: pl.BlockSpec((bq, LANES), lambda b, p1, p2: (step_at(p1, p2).qi, col(b)))
    krow = lambda col: pl.BlockSpec((bk, LANES), lambda b, p1, p2: (step_at(p1, p2).kj, col(b)))
    in_specs = [qrow(lambda b: qc + b), krow(lambda b: kc + b), krow(lambda b: vc + b), qrow(lambda b: b),
                qrow(lambda b: b), qrow(lambda b: b)]
    args = [q, k, v, o, do, lse]
    whole = lambda rows: pl.BlockSpec((rows, LANES), lambda b, j, i: (0, b))
    out_specs = [whole(s), krow(lambda b: b), krow(lambda b: b)]
    out_shape = [jax.ShapeDtypeStruct((s, LANES * nb), F32), jax.ShapeDtypeStruct((t, LANES * nb), F32),
                 jax.ShapeDtypeStruct((t, LANES * nb), F32)]
    scratch = [pltpu.VMEM((bk, LANES), F32), pltpu.VMEM((bk, LANES), F32)]
    if bias:
        ckj = pl.BlockSpec((1, 8, bk), lambda b, p1, p2: (b, 0, step_at(p1, p2).kj))
        in_specs.append(ckj)
        args.append(ck)
        out_specs += [ckj, whole(s)]
        out_shape += [jax.ShapeDtypeStruct((nb, 8, t), F32), jax.ShapeDtypeStruct((s, LANES * nb), F32)]
    if rope:
        in_specs += [qrow(lambda b: qrc), krow(lambda b: 0)]
        args += [qr, kr]
        out_specs += [pl.BlockSpec((s, LANES), lambda b, j, i: (0, 0)), pl.BlockSpec((t, LANES), lambda b, j, i: (0, 0))]
        out_shape += [jax.ShapeDtypeStruct((s, LANES), F32), jax.ShapeDtypeStruct((t, LANES), F32)]
    if bias:
        scratch.append(pltpu.VMEM((8, bk), F32))
    if comm is not None:
        in_specs += [ANY] * len(comm.ins)
        args += comm.ins
        out_specs += [ANY] * len(comm.out_shapes)
        out_shape += comm.out_shapes
        scratch += _sems(comm.n_sems, comm.n_sems)
    res = pl.pallas_call(body, name=name, grid=(nb, n1, n2), in_specs=in_specs, out_specs=out_specs,
                         out_shape=out_shape, scratch_shapes=scratch,
                         compiler_params=_params('arbitrary', 'arbitrary', 'arbitrary'))(*args)
    return res if comm is None else (*res[:n_out], res[n_out:])


def _gla_chunk(la_c, k_c):
    r = lax.broadcasted_iota(jnp.int32, (CHUNK, CHUNK), 0)
    c = lax.broadcasted_iota(jnp.int32, (CHUNK, CHUNK), 1)
    tri = jnp.where(c <= r, 1.0, 0.0).astype(BF16)
    cum = _tri_dot(tri, la_c)
    end = jnp.sum(la_c, axis=0, keepdims=True)
    dec = jnp.exp(end - cum)
    return dec, k_c * dec, jnp.exp(end)


GLA_PAIRS = GLA_HEADS // 2


def _gla_fwd(z, la, *, qc, kc, vc, name, blk=512):
    s = z.shape[0]
    bs = min(blk, s)
    ncb = bs // CHUNK
    nblk = s // bs

    def body(q_ref, k_ref, va_ref, vb_ref, la_ref, o_ref, st_ref, st):
        @pl.when(pl.program_id(1) == 0)
        def _():
            st[...] = jnp.zeros_like(st)

        heads, _ = _lane_masks(2, 0, False)
        v_refs = (va_ref, vb_ref)
        for c in range(ncb):
            sl = pl.ds(c * CHUNK, CHUNK)
            _, kf, a = _gla_chunk(la_ref[sl, :], k_ref[sl, :])
            qs = q_ref[sl, :] * (GLA_DK ** -0.5)
            for hh in range(2):
                ut = _dot(v_refs[hh][sl, :].astype(BF16), _sel(heads[hh], kf).astype(BF16), TN)
                new = a * st[hh] + ut
                st[hh] = new
                st_ref[0, c, hh] = new
                o_ref[sl, hh * GLA_DV:(hh + 1) * GLA_DV] = _dot(_sel(heads[hh], qs).astype(BF16), new.astype(BF16), NT)

    col = lambda c0, m=1: pl.BlockSpec((bs, LANES), lambda b, i: (i, c0 + m * b))
    return pl.pallas_call(
        body, name=name, grid=(GLA_PAIRS, nblk),
        in_specs=[col(qc), col(kc), col(vc, 2), col(vc + 1, 2), col(0)],
        out_specs=[pl.BlockSpec((bs, 2 * GLA_DV), lambda b, i: (i, b)),
                   pl.BlockSpec((1, ncb, 2, GLA_DV, LANES), lambda b, i: (b, i, 0, 0, 0))],
        out_shape=[jax.ShapeDtypeStruct((s, GLA_HEADS * GLA_DV), F32),
                   jax.ShapeDtypeStruct((GLA_PAIRS, s // CHUNK, 2, GLA_DV, LANES), F32)],
        scratch_shapes=[pltpu.VMEM((2, GLA_DV, LANES), F32)],
        compiler_params=_params('arbitrary', 'arbitrary'))(z, z, z, z, la)


def _gla_bwd(z, la, st_all, st_prev, do, *, qc, kc, vc, name, blk=512):
    s = z.shape[0]
    bs = min(blk, s)
    ncb = bs // CHUNK
    nblk = s // bs

    def body(q_ref, k_ref, va_ref, vb_ref, la_ref, st_ref, sp_ref, do_ref, dq_ref, dk_ref, dv_ref, dla_ref, ga):
        @pl.when(pl.program_id(1) == 0)
        def _():
            ga[...] = jnp.zeros_like(ga)

        r = lax.broadcasted_iota(jnp.int32, (CHUNK, CHUNK), 0)
        cc = lax.broadcasted_iota(jnp.int32, (CHUNK, CHUNK), 1)
        tri_rev = jnp.where(cc >= r, 1.0, 0.0).astype(BF16)
        heads, _ = _lane_masks(2, 0, False)
        v_refs = (va_ref, vb_ref)
        for c in reversed(range(ncb)):
            sl = pl.ds(c * CHUNK, CHUNK)
            dec, kf, a = _gla_chunk(la_ref[sl, :], k_ref[sl, :])
            qs = q_ref[sl, :] * (GLA_DK ** -0.5)
            dq2 = jnp.zeros((CHUNK, LANES), F32)
            dkd = jnp.zeros((CHUNK, LANES), F32)
            da = jnp.zeros((1, LANES), F32)
            for hh in range(2):
                hv = slice(hh * GLA_DV, (hh + 1) * GLA_DV)
                dob = do_ref[sl, hv].astype(BF16)
                g = _dot(dob, _sel(heads[hh], qs).astype(BF16), TN) + ga[hh]
                gb = g.astype(BF16)
                dq2 = dq2 + _dot(dob, st_ref[0, c, hh].astype(BF16))
                dv_ref[sl, hv] = _dot(_sel(heads[hh], kf).astype(BF16), gb, NT)
                dkd = dkd + _dot(v_refs[hh][sl, :].astype(BF16), gb)
                da = da + jnp.sum(g * sp_ref[0, c, hh], axis=0, keepdims=True)
                ga[hh] = a * g
            dq_ref[sl, :] = (GLA_DK ** -0.5) * dq2
            dk_ref[sl, :] = dkd * dec
            e = dkd * kf
            dend = jnp.sum(e, axis=0, keepdims=True) + da * a
            dla_ref[sl, :] = dend - _tri_dot(tri_rev, e)

    rev = lambda i: nblk - 1 - i
    col = lambda c0, m=1: pl.BlockSpec((bs, LANES), lambda b, i: (rev(i), c0 + m * b))
    wide = pl.BlockSpec((bs, 2 * GLA_DV), lambda b, i: (rev(i), b))
    stspec = pl.BlockSpec((1, ncb, 2, GLA_DV, LANES), lambda b, i: (b, rev(i), 0, 0, 0))
    return pl.pallas_call(
        body, name=name, grid=(GLA_PAIRS, nblk),
        in_specs=[col(qc), col(kc), col(vc, 2), col(vc + 1, 2), col(0), stspec, stspec, wide],
        out_specs=[col(0), col(0), wide, col(0)],
        out_shape=[jax.ShapeDtypeStruct((s, GLA_HEADS * GLA_DK), F32), jax.ShapeDtypeStruct((s, GLA_HEADS * GLA_DK), F32),
                   jax.ShapeDtypeStruct((s, GLA_HEADS * GLA_DV), F32), jax.ShapeDtypeStruct((s, GLA_HEADS * GLA_DK), F32)],
        scratch_shapes=[pltpu.VMEM((2, GLA_DV, LANES), F32)],
        compiler_params=_params('arbitrary', 'arbitrary'))(z, z, z, z, la, st_all, st_prev, do)


def _place():
    return lax.axis_index('x'), lax.axis_index('y'), lax.axis_index('c')


ANY = pl.BlockSpec(memory_space=pl.ANY)


def _all_gather8(blk, *, name):
    m, n = blk.shape

    def body(x_ref, out_ref, send_sems, recv_sems, local_sem):
        x, y, c = _place()
        me, sibling = (x, y, c), (x, y, 1 - c)
        chips = [(1 - x, y), (x, 1 - y), (1 - x, 1 - y)]

        def slot(px, py, pc):
            return out_ref.at[4 * px + 2 * py + pc]

        def copy(q, block, to, src=None):
            return pltpu.make_async_remote_copy(
                src_ref=slot(*block) if src is None else src, dst_ref=slot(*block), send_sem=send_sems.at[q],
                recv_sem=recv_sems.at[q], device_id=to, device_id_type=MESH)

        mine = pltpu.make_async_copy(x_ref, slot(*me), local_sem)
        mine.start()
        first = [copy(0, me, sibling, src=x_ref)]
        first += [copy(1 + q, me, (*chip, c), src=x_ref) for q, chip in enumerate(chips)]
        for cp in first:
            cp.start()
        passed = [copy(4 + q, (*chip, c), sibling) for q, chip in enumerate(chips)]
        for q, chip in enumerate(chips):
            copy(1 + q, (*chip, c), me).wait_recv()
            passed[q].start()
        copy(0, sibling, me).wait_recv()
        for q, chip in enumerate(chips):
            copy(4 + q, (*chip, 1 - c), me).wait_recv()
        for cp in first + passed:
            cp.wait_send()
        mine.wait()

    return pl.pallas_call(
        body, name=name, in_specs=[ANY], out_specs=ANY, out_shape=jax.ShapeDtypeStruct((N_DEV, m, n), blk.dtype),
        scratch_shapes=[pltpu.SemaphoreType.DMA((7,)), pltpu.SemaphoreType.DMA((7,)), pltpu.SemaphoreType.DMA(())],
    )(blk)


def _sems(*counts):
    return [pltpu.SemaphoreType.DMA((n,)) for n in counts]


class Comm(typing.NamedTuple):
    ins: list
    out_shapes: list
    n_sems: int
    start: typing.Callable
    finish: typing.Callable


def _remote(src, dst, send_sems, recv_sems, idx, to):
    return lambda: pltpu.make_async_remote_copy(src_ref=src, dst_ref=dst, send_sem=send_sems.at[idx],
                                                recv_sem=recv_sems.at[idx], device_id=to, device_id_type=MESH)


def _comm_from(copies, ins, out_shapes, n_sems):
    def start(*refs):
        for cp in copies(*refs)[0]:
            cp().start()

    def finish(*refs):
        sent, received = copies(*refs)
        for cp in received:
            cp().wait_recv()
        for cp in sent:
            cp().wait_send()

    return Comm(list(ins), list(out_shapes), n_sems, start, finish)


def _run_comm(comm, *, name, alias=False):
    n_in, n_out = len(comm.ins), len(comm.out_shapes)

    def body(*refs):
        ins, outs, sems = refs[:n_in], refs[n_in:n_in + n_out], refs[n_in + n_out:]
        comm.start(ins, outs, *sems)
        comm.finish(ins, outs, *sems)

    return pl.pallas_call(body, name=name, in_specs=[ANY] * n_in, out_specs=[ANY] * n_out, out_shape=comm.out_shapes,
                          input_output_aliases={q: q for q in range(n_in)} if alias else {},
                          scratch_shapes=_sems(comm.n_sems, comm.n_sems))(*comm.ins)


def _half(rows, c):
    h = rows // 2
    return pl.ds(pl.multiple_of(c * h, h), h)


def _gather_over_ici(ws):
    def copies(ins, outs, send_sems, recv_sems):
        x, y, c = _place()
        me_chip = 2 * x + y
        sent, received = [], []
        for q, w in enumerate(ws):
            half = _half(w.shape[0], c)
            for k, (px, py) in enumerate([(1 - x, y), (x, 1 - y), (1 - x, 1 - y)]):
                sent.append(_remote(ins[q].at[half], outs[q].at[me_chip, half], send_sems, recv_sems, 4 * q + k, (px, py, c)))
                slot = outs[q].at[2 * px + py, half]
                received.append(_remote(slot, slot, send_sems, recv_sems, 4 * q + k, (px, py, c)))
            whole = _remote(ins[q], outs[q].at[me_chip], send_sems, recv_sems, 4 * q + 3, (x, y, 1 - c))
            sent.append(whole)
            received.append(whole)
        return sent, received

    return _comm_from(copies, ws, [jax.ShapeDtypeStruct((N_CHIPS,) + w.shape, w.dtype) for w in ws], 4 * len(ws))


def _gather_over_d2d(parts):
    def copies(ins, outs, send_sems, recv_sems):
        x, y, c = _place()
        sent, received = [], []
        for q, w in enumerate(parts):
            for k, (px, py) in enumerate([(1 - x, y), (x, 1 - y), (1 - x, 1 - y)]):
                mine = outs[q].at[2 * px + py, _half(w.shape[1], c)]
                theirs = outs[q].at[2 * px + py, _half(w.shape[1], 1 - c)]
                sent.append(_remote(mine, mine, send_sems, recv_sems, 3 * q + k, (x, y, 1 - c)))
                received.append(_remote(theirs, theirs, send_sems, recv_sems, 3 * q + k, (x, y, 1 - c)))
        return sent, received

    return _comm_from(copies, parts, [jax.ShapeDtypeStruct(w.shape, w.dtype) for w in parts], 3 * len(parts))


def _to_sibling(gs, *, name):
    n = len(gs)

    def body(*refs):
        ins, outs = refs[:n], refs[n:2 * n]
        send_sems, recv_sems = refs[2 * n:]
        x, y, c = _place()
        cps = [pltpu.make_async_remote_copy(
            src_ref=ins[q], dst_ref=outs[q], send_sem=send_sems.at[q], recv_sem=recv_sems.at[q],
            device_id=(x, y, 1 - c), device_id_type=MESH) for q in range(n)]
        for cp in cps:
            cp.start()
        for cp in cps:
            cp.wait()

    return pl.pallas_call(body, name=name, in_specs=[ANY] * n, out_specs=[ANY] * n,
                          out_shape=[jax.ShapeDtypeStruct(g.shape, g.dtype) for g in gs],
                          scratch_shapes=_sems(n, n))(*gs)


def _chip_exchange(ps):
    def copies(ins, outs, send_sems, recv_sems):
        x, y, c = _place()
        cps = [_remote(ins[q].at[2 * px + py], outs[q].at[k], send_sems, recv_sems, 3 * q + k, (px, py, c))
               for q in range(len(ps)) for k, (px, py) in enumerate([(1 - x, y), (x, 1 - y), (1 - x, 1 - y)])]
        return cps, cps

    return _comm_from(copies, ps, [jax.ShapeDtypeStruct((3,) + p.shape[1:], p.dtype) for p in ps], 3 * len(ps))


def _sum_chips(own, r, *, name, ts=256):
    k, n = own.shape
    ts = min(ts, k)

    def body(own_ref, r_ref, o_ref):
        f = lambda q: r_ref[q].astype(F32)
        o_ref[...] = ((own_ref[...].astype(F32) + f(0)) + f(1)) + f(2)

    return pl.pallas_call(
        body, name=name, grid=(k // ts,),
        in_specs=[pl.BlockSpec((ts, n), lambda i: (i, 0)), pl.BlockSpec((3, ts, n), lambda i: (0, i, 0))],
        out_specs=pl.BlockSpec((ts, n), lambda i: (i, 0)), out_shape=jax.ShapeDtypeStruct((k, n), F32),
        compiler_params=_params('arbitrary'))(own, r)


WIN_SHARD = N_IN // N_CHIPS
WIN_PAD = -(-WIN_SHARD // LANES) * LANES
GATE_WIRE_ROWS = 32


def _full_layer(sh, axis):
    _, k, n = sh.shape
    if axis == 2:
        return sh.transpose(1, 0, 2).reshape(k, N_CHIPS * n)
    return sh.reshape(N_CHIPS * k, n)


def _win_cols(wp, o, n):
    parts = []
    while n > 0:
        j, r = divmod(o, WIN_SHARD)
        take = min(n, WIN_SHARD - r)
        parts.append(wp[:, j * WIN_PAD + r:j * WIN_PAD + r + take])
        o, n = o + take, n - take
    return parts[0] if len(parts) == 1 else jnp.concatenate(parts, axis=1)


def _split_full(full, axis):
    k, n = full.shape
    if axis == 2:
        return jnp.stack([full[:, j * (n // N_CHIPS):(j + 1) * (n // N_CHIPS)] for j in range(N_CHIPS)])
    return full.reshape(N_CHIPS, k // N_CHIPS, n)


def _padc(a, w):
    return jnp.pad(a, ((0, 0), (0, w - a.shape[1])))


def _swap16(a):
    return jnp.concatenate([a[..., 16:32], a[..., 0:16]], axis=-1)


B_GR, B_GQ, B_GK, B_GV, B_MQ, B_MKR, B_MKRS, B_FF, B_GLOW, B_MKV, B_END = (
    0, 512, 768, 1024, 1536, 1792, 1920, 2048, 2176, 2304, 2432)
B_W = 2560
O_FQ, O_FF, O_GQ, O_GLOW, O_GR, O_MQ, O_MKV, O_MKR, O_ZG = 0, 768, 772, 1796, 1812, 2324, 2580, 2708, 2740


def _repack_layer_weights(w):
    wi = functools.partial(_win_cols, w['w_in'])
    out = dict(w)
    out['in_a'] = wi(O_FQ, 768)
    kr = wi(O_MKR, 32)
    out['in_b'] = jnp.concatenate([
        wi(O_GR, 512), wi(O_GQ, 1024), wi(O_MQ, 256), jnp.tile(kr, (1, MLA_HEADS)), jnp.tile(_swap16(kr), (1, MLA_HEADS)),
        _padc(wi(O_FF, 4), 128), _padc(wi(O_GLOW, 16), 128), wi(O_MKV, 128),
        jnp.zeros((D_MODEL, B_W - B_END), kr.dtype)], axis=1)
    out['in_c'] = wi(O_ZG, 3072)
    uq = w['w_mla_uq'].reshape(MLA_Q_RANK, MLA_HEADS, MLA_NOPE + MLA_ROPE)
    rope = uq[:, :, MLA_NOPE:]
    out['uq'] = jnp.concatenate([uq[:, :, :MLA_NOPE].reshape(MLA_Q_RANK, -1), rope.reshape(MLA_Q_RANK, -1),
                                 _swap16(rope).reshape(MLA_Q_RANK, -1)], axis=1)
    ukv = w['w_mla_ukv'].reshape(MLA_KV_RANK, MLA_HEADS, MLA_NOPE + MLA_VD)
    out['ukv'] = jnp.concatenate([ukv[:, :, :MLA_NOPE].reshape(MLA_KV_RANK, -1),
                                  ukv[:, :, MLA_NOPE:].reshape(MLA_KV_RANK, -1)], axis=1)
    out['gate'] = jnp.pad(w['w_gla_gate'], ((0, 128 - GLA_RANK), (0, 0)))
    return out


def _unpack_layer_grads(g):
    a, b, c = g['in_a'], g['in_b'], g['in_c']
    fold = lambda o: sum(b[:, o + MLA_ROPE * q:o + MLA_ROPE * (q + 1)] for q in range(MLA_HEADS))
    kr = fold(B_MKR) + _swap16(fold(B_MKRS))
    w_in = jnp.concatenate([a, b[:, B_FF:B_FF + 4], b[:, B_GQ:B_GQ + 1024], b[:, B_GLOW:B_GLOW + 16],
                            b[:, B_GR:B_GR + 512], b[:, B_MQ:B_MQ + 256], b[:, B_MKV:B_MKV + 128], kr, c], axis=1)
    uq = g['uq']
    nope = uq[:, :256].reshape(MLA_Q_RANK, MLA_HEADS, MLA_NOPE)
    rope = (uq[:, 256:384].reshape(MLA_Q_RANK, MLA_HEADS, MLA_ROPE)
            + _swap16(uq[:, 384:512].reshape(MLA_Q_RANK, MLA_HEADS, MLA_ROPE)))
    w_uq = jnp.concatenate([nope, rope], axis=2).reshape(MLA_Q_RANK, -1)
    ukv = g['ukv']
    w_ukv = jnp.concatenate([ukv[:, :256].reshape(MLA_KV_RANK, MLA_HEADS, MLA_NOPE),
                             ukv[:, 256:].reshape(MLA_KV_RANK, MLA_HEADS, MLA_VD)], axis=2).reshape(MLA_KV_RANK, -1)
    out = {'w_in': w_in, 'w_mla_uq': w_uq, 'w_mla_ukv': w_ukv, 'w_gla_gate': g['gate'][:GLA_RANK]}
    for nm in ('w_up_fox', 'w_up_gla', 'w_up_mla', 'w_out', 'w_xq', 'w_xkv', 'w_xo', 'w_mlp1', 'w_mlp2'):
        out[nm] = g[nm]
    return out


def _rope_tables(s):
    half = MLA_ROPE // 2
    inv = ROPE_BASE ** (-jnp.arange(half, dtype=F32) / half)
    ang = jnp.arange(s).astype(F32)[:, None] * inv[None, :]
    cos, sin = jnp.cos(ang), jnp.sin(ang)
    c1 = jnp.concatenate([cos, cos], axis=1)
    s1 = jnp.concatenate([-sin, sin], axis=1)
    return jnp.tile(c1, (1, MLA_HEADS)), jnp.tile(s1, (1, MLA_HEADS))


def _rms_bwd(x, dh, g):
    r = lax.rsqrt(jnp.mean(x * x, axis=-1, keepdims=True) + EPS)
    xh = x * r
    gd = dh * g
    return r * (gd - xh * jnp.mean(gd * xh, axis=-1, keepdims=True)), dh * xh


def _norm_bwd_call(x, dh, g, dres, name):
    w = x.width if isinstance(x, Cols) else x.shape[1]

    def with_res(xv, dv, rv, gv):
        dx, dg = _rms_bwd(xv, dv.astype(F32), gv)
        return rv + dx, dg

    def plain(xv, dv, gv):
        return _rms_bwd(xv, dv.astype(F32), gv)

    if dres is None:
        return _rowwise(plain, [x, dh], [g], [(w, F32)], [w], name=name)
    return _rowwise(with_res, [x, dh, dres], [g], [(w, F32)], [w], name=name)


def _gla_out_fwd(oraw, gr, g_out):
    outs = []
    for hh in range(GLA_HEADS):
        sl = slice(hh * GLA_DV, (hh + 1) * GLA_DV)
        oh = oraw[:, sl]
        n = oh * lax.rsqrt(jnp.mean(oh * oh, axis=-1, keepdims=True) + EPS) * g_out
        r = gr[:, sl]
        outs.append(n * (r * _sig(r)))
    return (jnp.concatenate(outs, axis=1),)


def _gla_out_bwd(oraw, gr, dout, g_out):
    d_o, d_r, dg = [], [], 0.0
    for hh in range(GLA_HEADS):
        sl = slice(hh * GLA_DV, (hh + 1) * GLA_DV)
        oh, r, do = oraw[:, sl], gr[:, sl], dout[:, sl].astype(F32)
        rs = lax.rsqrt(jnp.mean(oh * oh, axis=-1, keepdims=True) + EPS)
        sg = _sig(r)
        dn = do * (r * sg)
        d_r.append(do * (oh * rs * g_out) * (sg + r * sg * (1.0 - sg)))
        dx, dgh = _rms_bwd(oh, dn, g_out)
        d_o.append(dx)
        dg = dg + dgh
    return jnp.concatenate(d_o, axis=1), jnp.concatenate(d_r, axis=1), dg


def _adam(w, g, m, v):
    m = ADAM_B1 * m + (1.0 - ADAM_B1) * g
    v = ADAM_B2 * v + (1.0 - ADAM_B2) * (g * g)
    m_hat = m / (1.0 - ADAM_B1 ** ADAM_STEP)
    v_hat = v / (1.0 - ADAM_B2 ** ADAM_STEP)
    return -ADAM_LR * (m_hat / (jnp.sqrt(v_hat) + ADAM_EPS) + ADAM_WD * w), m, v


def _layer_fwd(x, mem, w, p, tabs, tag, carry_fox=None, after_fox=None, carry_mla=None):
    c4, s4 = tabs
    sv = {'x0': x}
    nm = lambda t: f'{t}_{tag}'
    za, h = _mm(x, w['in_a'], mode='nn', out_dtype=BF16, norm_g=p['g_mix'], emit_norm=True, name=nm('in_a'))
    zb = _mm(h, w['in_b'], mode='nn', out_dtype=F32, name=nm('in_b'))
    zc = _mm(h, w['in_c'], mode='nn', out_dtype=F32, name=nm('in_c'))
    sv.update(h=h, zc=zc)
    ff = Cols(zb, 128, B_FF // 128)
    (lf,) = _rowwise(lambda f, b: (_logsig(f + b),), [ff], [p['b_fox']], [(128, F32)], name=nm('fox_lf'))
    cum = _cumsum_rows(lf, reverse=False, name=nm('fox_cum'))
    ckf = jnp.pad(cum[:, :FOX_HEADS].T.reshape(2, 2, x.shape[0]), ((0, 0), (0, 6), (0, 0)))
    fox = dict(qc=0, kc=2, vc=4, nb=2, g=2, scale=FOX_HD ** -0.5, mode='causal', ck=ckf)
    o_fox, lse_fox, *carried = _mattn_fwd(za, za, za, name=nm('fox_attn'), comm=carry_fox, **fox)
    if after_fox is not None:
        w = {**w, **after_fox(carried[0])}
    sv.update(ff=ff, za=za, fox=fox, o_fox=o_fox, lse_fox=lse_fox)
    glow = Cols(zb, 128, B_GLOW // 128)
    gr = Cols(zb, 512, B_GR // 512)

    def gate_fn(gl, wg, bg):
        return (_logsig(_dot(gl.astype(BF16), wg) + bg) / GLA_TAU,)

    (la,) = _rowwise(gate_fn, [glow], [w['gate'], p['b_gla']], [(256, F32)], name=nm('gla_gate'))
    gla = dict(qc=B_GQ // LANES, kc=B_GK // LANES, vc=B_GV // LANES)
    oraw, states = _gla_fwd(zb, la, name=nm('gla'), **gla)
    (o_gla,) = _rowwise(_gla_out_fwd, [oraw, gr], [p['g_gla_out']], [(512, BF16)], name=nm('gla_out'))
    sv.update(glow=glow, gr=gr, zb=zb, la=la, gla=gla, states=states, oraw=oraw, o_gla=o_gla)
    mq = Cols(zb, 256, B_MQ // 256)
    mkv = Cols(zb, 128, B_MKV // 128)
    mkr2 = Cols(zb, 256, B_MKR // 256)
    qp, cqn = _mm(mq, w['uq'], mode='nn', out_dtype=F32, norm_g=p['g_mla_q'], emit_norm=True, name=nm('mla_uq'))
    kvp, ckvn = _mm(mkv, w['ukv'], mode='nn', out_dtype=BF16, norm_g=p['g_mla_kv'], emit_norm=True,
                    name=nm('mla_ukv'))

    def rope_fn(qv, kr, c4v, s4v):
        q_rope = qv[:, 256:384] * c4v + qv[:, 384:512] * s4v
        return jnp.concatenate([qv[:, 0:256], q_rope], axis=1), kr[:, 0:128] * c4v + kr[:, 128:256] * s4v

    qall, kr4 = _rowwise(rope_fn, [qp, mkr2, c4, s4], [], [(384, BF16), (128, BF16)], name=nm('rope'))
    mla = dict(qc=0, kc=0, vc=2, nb=2, g=2, scale=(MLA_NOPE + MLA_ROPE) ** -0.5, mode='chunk', qr=qall, qrc=2, kr=kr4)
    o_mla, lse_mla, *carried = _mattn_fwd(qall, kvp, kvp, name=nm('mla_attn'), comm=carry_mla, **mla)
    if carry_mla is not None:
        sv['carried_mla'] = carried[0]
    sv.update(mq=mq, mkv=mkv, cqn=cqn, ckvn=ckvn, qall=qall, kvp=kvp, mla=mla, o_mla=o_mla, lse_mla=lse_mla)
    of_m, om_m = o_fox, o_mla
    sv.update(of_m=of_m, om_m=om_m)
    b_br = p['b_branch']

    def first(acc, zg, bb):
        return _sig(zg + bb) * acc

    def more(acc, zg, bb, prev):
        return prev + _sig(zg + bb) * acc

    y = _mm(of_m, w['w_up_fox'], mode='nn', out_dtype=F32, name=nm('up_fox'), epilogue=first,
            extras=[(zc, *_mn(col_off=0)), (b_br, *_nvec(col_off=0))])
    y = _mm(o_gla, w['w_up_gla'], mode='nn', out_dtype=F32, name=nm('up_gla'), epilogue=more,
            extras=[(zc, *_mn(col_off=1024)), (b_br, *_nvec(col_off=1024)), (y, *_mn())])
    y = _mm(om_m, w['w_up_mla'], mode='nn', out_dtype=BF16, name=nm('up_mla'), epilogue=more,
            extras=[(zc, *_mn(col_off=2048)), (b_br, *_nvec(col_off=2048)), (y, *_mn())])
    add = lambda acc, res: res + acc
    x1 = _mm(y, w['w_out'], mode='nn', out_dtype=F32, name=nm('out'), epilogue=add, extras=[(x, *_mn())])
    sv.update(y=y, x1=x1)
    qx, hx = _mm(x1, w['w_xq'], mode='nn', out_dtype=BF16, norm_g=p['g_xa'], emit_norm=True, name=nm('xq'))
    kvx, mn = _mm(mem, w['w_xkv'], mode='nn', out_dtype=BF16, norm_g=p['g_mem'], emit_norm=True, name=nm('xkv'))
    xa = dict(qc=0, kc=0, vc=4, nb=4, g=1, scale=XA_HD ** -0.5, mode='full')
    ox_m, lse_x = _mattn_fwd(qx, kvx, kvx, name=nm('xa_attn'), **xa)
    x2 = _mm(ox_m, w['w_xo'], mode='nn', out_dtype=F32, name=nm('xo'), epilogue=add, extras=[(x1, *_mn())])
    sv.update(hx=hx, mn=mn, qx=qx, kvx=kvx, xa=xa, lse_x=lse_x, ox_m=ox_m, x2=x2)
    hpre, hm = _mm(x2, w['w_mlp1'], mode='nn', out_dtype=BF16, norm_g=p['g_mlp'], emit_norm=True, name=nm('mlp1'))
    relu2 = lambda t: jnp.square(jnp.maximum(t.astype(F32), 0.0))
    x3 = _mm(hpre, w['w_mlp2'], mode='nn', out_dtype=F32, name=nm('mlp2'), a_fn=relu2, epilogue=add,
             extras=[(x2, *_mn())])
    sv.update(hpre=hpre, hm=hm, w=w)
    return x3, sv


EARLY = ('w_mlp1', 'w_mlp2', 'w_xo', 'w_xq', 'w_xkv', 'w_out', 'w_up_fox', 'w_up_gla', 'w_up_mla')
LATE = ('w_in', 'w_gla_gate', 'w_mla_uq', 'w_mla_ukv')


def _layer_bwd(dx3, mem, w, p, tabs, sv, tag, carry_mla=None, early=None):
    c4, s4 = tabs
    nm = lambda t: f'{t}_{tag}'
    s = dx3.shape[0]
    gw, gs = {}, {}
    relu2 = lambda t: jnp.square(jnp.maximum(t.astype(F32), 0.0))
    gw['w_mlp2'] = _mm(sv['hpre'], dx3, mode='tn', out_dtype=F32, name=nm('d_mlp2'), a_fn=relu2)
    dact = lambda acc, hp: acc * (2.0 * jnp.maximum(hp.astype(F32), 0.0))
    dhpre = _mm(dx3, w['w_mlp2'], mode='nt', out_dtype=BF16, name=nm('d_act'), epilogue=dact,
                extras=[(sv['hpre'], *_mn())])
    gw['w_mlp1'] = _mm(sv['hm'], dhpre, mode='tn', out_dtype=F32, name=nm('d_mlp1'))
    dhm = _mm(dhpre, w['w_mlp1'], mode='nt', out_dtype=F32, name=nm('d_hm'))
    dx2, gs['g_mlp'] = _norm_bwd_call(sv['x2'], dhm, p['g_mlp'], dx3, nm('d_norm_mlp'))
    gw['w_xo'] = _mm(sv['ox_m'], dx2, mode='tn', out_dtype=F32, name=nm('d_xo'))
    dox = _mm(dx2, w['w_xo'], mode='nt', out_dtype=BF16, name=nm('d_ox'))
    dqx_m, dkx, dvx = _mattn_bwd(sv['qx'], sv['kvx'], sv['kvx'], sv['ox_m'], dox, sv['lse_x'], name=nm('xa_bwd'),
                                 **sv['xa'])
    dkvx = jnp.concatenate([dkx, dvx], axis=1).astype(BF16)
    gw['w_xq'] = _mm(sv['hx'], dqx_m, mode='tn', out_dtype=F32, name=nm('d_xq'))
    dhx = _mm(dqx_m, w['w_xq'], mode='nt', out_dtype=F32, name=nm('d_hx'))
    gw['w_xkv'] = _mm(sv['mn'], dkvx, mode='tn', out_dtype=F32, name=nm('d_xkv'))
    dmn = _mm(dkvx, w['w_xkv'], mode='nt', out_dtype=F32, name=nm('d_mn'))
    _, gs['g_mem'] = _norm_bwd_call(mem, dmn, p['g_mem'], None, nm('d_norm_mem'))
    dx1, gs['g_xa'] = _norm_bwd_call(sv['x1'], dhx, p['g_xa'], dx2, nm('d_norm_xa'))
    gw['w_out'] = _mm(sv['y'], dx1, mode='tn', out_dtype=F32, name=nm('d_out'))
    dy = _mm(dx1, w['w_out'], mode='nt', out_dtype=BF16, name=nm('d_y'))
    zc, b_br = sv['zc'], p['b_branch']

    def du_fn(dyv, zg, bb):
        g = _sig(zg + bb)
        d = dyv.astype(F32)
        return d * g[:, 0:1024], d * g[:, 1024:2048], d * g[:, 2048:3072]

    du = _rowwise(du_fn, [dy, zc], [b_br], [(D_MODEL, BF16)] * 3, name=nm('d_u'))

    def dgate(acc, dyv, zg, bb):
        g = _sig(zg + bb)
        return dyv.astype(F32) * acc * g * (1.0 - g)

    dzc, do_br = [], []
    for q, (o_m, wn) in enumerate(((sv['of_m'], 'w_up_fox'), (sv['o_gla'], 'w_up_gla'), (sv['om_m'], 'w_up_mla'))):
        dzc.append(_mm(o_m, w[wn], mode='nn', out_dtype=F32, name=nm(f'd_zg{q}'), epilogue=dgate,
                       extras=[(dy, *_mn()), (zc, *_mn(col_off=1024 * q)), (b_br, *_nvec(col_off=1024 * q))]))
        gw[wn] = _mm(o_m, du[q], mode='tn', out_dtype=F32, name=nm(f'd_up{q}'))
        do_br.append(_mm(du[q], w[wn], mode='nt', out_dtype=F32 if q == 1 else BF16, name=nm(f'd_o{q}')))
    dzc = jnp.concatenate(dzc, axis=1)
    (gs['b_branch'],) = _rowwise(lambda t: (t,), [dzc], [], [], [3072], name=nm('d_bbranch'))
    za = sv['za']
    carry_fox = None if early is None else early({nm_: gw[nm_] for nm_ in EARLY})
    dfq, dfk, dfv, dck, dcq, *carried_fox = _mattn_bwd(za, za, za, sv['o_fox'], do_br[0], sv['lse_fox'],
                                                       name=nm('fox_bwd'), comm=carry_fox, **sv['fox'])
    dcum = _padc(dck[:, :2, :].reshape(FOX_HEADS, s).T + dcq.reshape(s, 2, LANES)[:, :, :2].reshape(s, FOX_HEADS), 128)
    dlf = _cumsum_rows(dcum, reverse=True, name=nm('fox_dcum'))

    def dff_fn(dl, f, b):
        d = dl * _sig(-(f + b))
        return d, d

    dff, db_fox = _rowwise(dff_fn, [dlf, sv['ff']], [p['b_fox']], [(128, F32)], [128], name=nm('fox_dff'))
    gs['b_fox'] = db_fox
    dza = jnp.concatenate([dfq, dfk, dfv], axis=1).astype(BF16)
    dqn, dkn, dvv, dq_rope, dk_rope, *carried_mla = _mattn_bwd(sv['qall'], sv['kvp'], sv['kvp'], sv['o_mla'], do_br[2],
                                                               sv['lse_mla'], name=nm('mla_bwd'), comm=carry_mla,
                                                               **sv['mla'])

    def drope_fn(dn, dq, dk, c4v, s4v):
        return jnp.concatenate([dn, dq * c4v, dq * s4v], axis=1), jnp.concatenate([dk * c4v, dk * s4v], axis=1)

    dqp, dmkr2 = _rowwise(drope_fn, [dqn, dq_rope, dk_rope, c4, s4], [], [(512, BF16), (256, BF16)], name=nm('d_rope'))
    dkvp = jnp.concatenate([dkn, dvv], axis=1).astype(BF16)
    gw['uq'] = _mm(sv['cqn'], dqp, mode='tn', out_dtype=F32, name=nm('d_uq'))
    dcqn = _mm(dqp, w['uq'], mode='nt', out_dtype=F32, name=nm('d_cqn'))
    gw['ukv'] = _mm(sv['ckvn'], dkvp, mode='tn', out_dtype=F32, name=nm('d_ukv'))
    dckvn = _mm(dkvp, w['ukv'], mode='nt', out_dtype=F32, name=nm('d_ckvn'))
    dmq, gs['g_mla_q'] = _norm_bwd_call(sv['mq'], dcqn, p['g_mla_q'], None, nm('d_norm_q'))
    dmkv, gs['g_mla_kv'] = _norm_bwd_call(sv['mkv'], dckvn, p['g_mla_kv'], None, nm('d_norm_kv'))
    doraw, dgr, gs['g_gla_out'] = _rowwise(_gla_out_bwd, [sv['oraw'], sv['gr'], do_br[1]], [p['g_gla_out']],
                                           [(512, F32), (512, BF16)], [128], name=nm('d_gla_out'))
    st = sv['states']
    st_prev = jnp.concatenate([jnp.zeros_like(st[:, :1]), st[:, :-1]], axis=1)
    dgq, dgk, dgv, dla = _gla_bwd(sv['zb'], sv['la'], st, st_prev, doraw, name=nm('gla_bwd'), **sv['gla'])

    def dgate_fn(dl, gl, wg, bg):
        pre = _dot(gl.astype(BF16), wg) + bg
        dpre = dl * (1.0 / GLA_TAU) * _sig(-pre)
        return dpre, _dot(dpre.astype(BF16), wg, NT), dpre

    dpre, dglow, gs['b_gla'] = _rowwise(dgate_fn, [dla, sv['glow']], [w['gate'], p['b_gla']],
                                        [(256, BF16), (128, BF16)], [256], name=nm('d_gla_gate'))
    gw['gate'] = _mm(sv['glow'], dpre, mode='tn', out_dtype=F32, name=nm('d_wgate'))
    bf = lambda t: t.astype(BF16)
    dzb = jnp.concatenate([dgr, bf(dgq), bf(dgk), bf(dgv), bf(dmq), dmkr2, bf(dff), dglow, bf(dmkv),
                           jnp.zeros((s, B_W - B_END), BF16)], axis=1)
    h = sv['h']
    gw['in_a'] = _mm(h, dza, mode='tn', out_dtype=F32, name=nm('d_in_a'))
    gw['in_b'] = _mm(h, dzb, mode='tn', out_dtype=F32, name=nm('d_in_b'))
    gw['in_c'] = _mm(h, dzc, mode='tn', out_dtype=F32, name=nm('d_in_c'))
    add = lambda acc, prev: prev + acc
    dh = _mm(dza, w['in_a'], mode='nt', out_dtype=F32, name=nm('d_h_a'))
    dh = _mm(dzb, w['in_b'], mode='nt', out_dtype=F32, name=nm('d_h_b'), epilogue=add, extras=[(dh, *_mn())])
    dh = _mm(dzc, w['in_c'], mode='nt', out_dtype=F32, name=nm('d_h_c'), epilogue=add, extras=[(dh, *_mn())])
    dx0, gs['g_mix'] = _norm_bwd_call(sv['x0'], dh, p['g_mix'], dx1, nm('d_norm_mix'))
    return dx0, gw, gs, (carried_mla or [None])[0], (carried_fox or [None])[0]


def _loss_head(x, target, g_final):
    d = x.shape[1]

    def fn(xv, tv, gv):
        r = lax.rsqrt(jnp.mean(xv * xv, axis=-1, keepdims=True) + EPS)
        xh = xv * r
        e = xh * gv - tv
        dy = e * (1.0 / d)
        gd = dy * gv
        dx = r * (gd - xh * jnp.mean(gd * xh, axis=-1, keepdims=True))
        row_loss = 0.5 * jnp.mean(e * e, axis=-1, keepdims=True)
        return dx, dy * xh, jnp.broadcast_to(row_loss, (xv.shape[0], LANES))

    return _rowwise(fn, [x, target], [g_final], [(d, F32)], [d, LANES], name='loss_head')


def _small_sizes(shapes):
    return [math.prod(shapes[nm]) for nm in SMALL]


def _step(args):
    shapes = {nm: args[nm].shape for nm in ORDER}
    x, mem, target = args['x'][0], args['mem'][0], args['loss_target'][0]
    s = x.shape[0]

    def wire(nm, l):
        w = args[nm][l].astype(BF16)
        if nm == 'w_in':
            w = jnp.pad(w, ((0, 0), (0, WIN_PAD - WIN_SHARD)))
        if nm == 'w_gla_gate':
            w = jnp.pad(w, ((0, GATE_WIRE_ROWS - GLA_RANK), (0, 0)))
        return w

    axis_of = dict(BIG)
    names = tuple(nm for nm, _ in BIG)
    wires = lambda l, nms: [wire(nm, l) for nm in nms]

    def whole(parts, nms, tag):
        parts = _run_comm(_gather_over_d2d(parts), name=f'gather_d2d_{tag}', alias=True)
        full = {nm: _full_layer(p, axis_of[nm]) for nm, p in zip(nms, parts)}
        if 'w_gla_gate' in full:
            full['w_gla_gate'] = full['w_gla_gate'][:GLA_RANK]
        return full

    tabs = _rope_tables(s)
    layers_p = []
    for l in range(DEPTH):
        layers_p.append({
            'g_mix': args['g_mix'][l][None], 'b_fox': _padc(args['b_fox_forget'][l][None], 128),
            'b_gla': args['b_gla_gate'][l][None], 'g_gla_out': args['g_gla_out'][l][None],
            'g_mla_q': args['g_mla_q'][l][None], 'g_mla_kv': args['g_mla_kv'][l][None],
            'b_branch': args['b_branch_gate'][l][None], 'g_xa': args['g_xa'][l][None],
            'g_mem': args['g_mem'][l][None], 'g_mlp': args['g_mlp'][l][None]})

    first = _run_comm(_gather_over_ici(wires(0, LATE)), name='gather_ici_first_l0')
    w_now = _repack_layer_weights(whole(first, LATE, 'first_l0'))
    saved = []
    xl = x
    for l in range(DEPTH):
        carry_fox = _gather_over_ici(wires(0, EARLY)) if l == 0 else None
        after_fox = (lambda parts: whole(parts, EARLY, 'rest_l0')) if l == 0 else None
        carry_mla = _gather_over_ici(wires(l + 1, names)) if l + 1 < DEPTH else None
        xl, sv = _layer_fwd(xl, mem, w_now, layers_p[l], tabs, f'l{l}', carry_fox=carry_fox, after_fox=after_fox,
                            carry_mla=carry_mla)
        saved.append(sv)
        if carry_mla is not None:
            w_now = _repack_layer_weights(whole(sv.pop('carried_mla'), names, f'l{l + 1}'))
    dx, dg_final, loss_lanes = _loss_head(xl, target, args['g_final'][None])
    cidx = lax.axis_index('c')
    chip = 2 * lax.axis_index('x') + lax.axis_index('y')

    def pair_sums(gw, nms, tag):
        mine, theirs = [], []
        for nm in nms:
            shards = _split_full(gw[nm], axis_of[nm]).astype(BF16)
            h = shards.shape[1] // 2
            mine.append(lax.dynamic_slice_in_dim(shards, cidx * h, h, axis=1))
            theirs.append(lax.dynamic_slice_in_dim(shards, (1 - cidx) * h, h, axis=1))
        got = _to_sibling(theirs, name=f'grads_swap_{tag}')
        pairs = []
        for nm, a, b in zip(nms, mine, got):
            _, h, n = a.shape
            (p,) = _rowwise(lambda u, v: (u.astype(F32) + v.astype(F32),),
                            [a.reshape(N_CHIPS * h, n), b.reshape(N_CHIPS * h, n)], [], [(n, BF16)],
                            name=f'pair_sum_{nm}_{tag}')
            pairs.append(p.reshape(N_CHIPS, h, n))
        return pairs

    def finish(pairs, from_chips, nms, tag):
        own = [lax.dynamic_index_in_dim(p, chip, axis=0, keepdims=False) for p in pairs]
        mine = [_sum_chips(o, r, name=f'chip_sum_{nm}_{tag}') for nm, o, r in zip(nms, own, from_chips)]
        theirs = _to_sibling(mine, name=f'grads_join_{tag}')
        return {nm: jnp.where(cidx == 0, jnp.concatenate([a, b]), jnp.concatenate([b, a]))
                for nm, a, b in zip(nms, mine, theirs)}

    gs_layers, done = [None] * DEPTH, [{} for _ in range(DEPTH)]
    above = None
    for l in reversed(range(DEPTH)):
        lowest, early_pairs = l == 0, []

        def early(gw_early, l=l, early_pairs=early_pairs):
            early_pairs.extend(pair_sums(gw_early, EARLY, f'early_l{l}'))
            return _chip_exchange(early_pairs)

        carry_mla = None if above is None else _chip_exchange(above[1])
        dx, gw, gs_layers[l], got_mla, got_fox = _layer_bwd(
            dx, mem, saved[l]['w'], layers_p[l], tabs, saved[l], f'l{l}', carry_mla=carry_mla,
            early=early if lowest else None)
        if above is not None:
            done[above[0]].update(finish(above[1], got_mla, names, f'l{above[0]}'))
        grads = _unpack_layer_grads(gw)
        if lowest:
            done[l].update(finish(early_pairs, got_fox, EARLY, f'early_l{l}'))
            late_pairs = pair_sums(grads, LATE, f'late_l{l}')
            from_late = _run_comm(_chip_exchange(late_pairs), name=f'grads_exchange_late_l{l}')
            done[l].update(finish(late_pairs, from_late, LATE, f'late_l{l}'))
        else:
            above = (l, pair_sums(grads, names, f'l{l}'))
    grad_x = dx[None]
    gshard = {nm: jnp.stack([done[l][nm] for l in range(DEPTH)]) for nm in names}

    small_g = []
    for nm, key in (('g_mix', 'g_mix'), ('b_fox_forget', 'b_fox'), ('b_gla_gate', 'b_gla'),
                    ('g_gla_out', 'g_gla_out'), ('g_mla_q', 'g_mla_q'), ('g_mla_kv', 'g_mla_kv'),
                    ('b_branch_gate', 'b_branch'), ('g_xa', 'g_xa'), ('g_mem', 'g_mem'), ('g_mlp', 'g_mlp')):
        width = shapes[nm][1]
        small_g.append(jnp.concatenate([gs_layers[l][key][0, :width] for l in range(DEPTH)]))
    small_g.append(dg_final[0])
    small_g.append(loss_lanes[0, :1])
    flat = jnp.concatenate(small_g)
    n_small = flat.shape[0]
    srows = -(-n_small // (8 * LANES)) * 8
    pad = lambda v: jnp.pad(v, (0, srows * LANES - v.shape[0])).reshape(srows, LANES)
    all_small = _all_gather8(pad(flat), name='gather_small')
    sw, sm, svv = (pad(jnp.concatenate([args[pre + nm].reshape(-1) for nm in SMALL] + [jnp.zeros((1,), F32)]))
                   for pre in ('', 'm_', 'v_'))

    def small_body(g_ref, w_ref, m_ref, v_ref, go_ref, d_ref, mo_ref, vo_ref):
        g = g_ref[0]
        for q in range(1, N_DEV):
            g = g + g_ref[q]
        go_ref[...] = g
        d_ref[...], mo_ref[...], vo_ref[...] = _adam(w_ref[...], g, m_ref[...], v_ref[...])

    sg, sd, snm, snv = pl.pallas_call(
        small_body, name='small_sum_adam', out_shape=[jax.ShapeDtypeStruct((srows, LANES), F32)] * 4,
        compiler_params=pltpu.CompilerParams(vmem_limit_bytes=VMEM_LIMIT))(all_small, sw, sm, svv)

    def unsmall(buf):
        v, out, off = buf.reshape(-1), {}, 0
        for nm in SMALL:
            nel = math.prod(shapes[nm])
            out[nm] = v[off:off + nel].reshape(shapes[nm])
            off += nel
        return out, v[off]

    res = {}
    (res['grad'], loss), (res['delta'], _), (res['m'], _), (res['v'], _) = (unsmall(t) for t in (sg, sd, snm, snv))

    for nm, _ in BIG:
        shp = args[nm].shape
        view = lambda t: t.reshape(shp[0] * shp[1], shp[2])
        d, m2, v2 = _rowwise(_adam, [view(args[nm]), view(gshard[nm]), view(args['m_' + nm]), view(args['v_' + nm])],
                             [], [(shp[2], F32)] * 3, name=f'adam_{nm}')
        res['grad'][nm], res['delta'][nm], res['m'][nm], res['v'][nm] = (
            gshard[nm], d.reshape(shp), m2.reshape(shp), v2.reshape(shp))

    return (loss, grad_x, *[res['grad'][nm] for nm in ORDER], *[res['delta'][nm] for nm in ORDER],
            *[res['m'][nm] for nm in ORDER], *[res['v'][nm] for nm in ORDER])


def kernel(x, mem, g_mix, w_in, b_fox_forget, w_gla_gate, b_gla_gate, g_gla_out, g_mla_q, w_mla_uq, g_mla_kv, w_mla_ukv, b_branch_gate, w_up_fox, w_up_gla, w_up_mla, w_out, g_xa, g_mem, w_xq, w_xkv, w_xo, g_mlp, w_mlp1, w_mlp2, g_final, loss_target, m_g_mix, m_w_in, m_b_fox_forget, m_w_gla_gate, m_b_gla_gate, m_g_gla_out, m_g_mla_q, m_w_mla_uq, m_g_mla_kv, m_w_mla_ukv, m_b_branch_gate, m_w_up_fox, m_w_up_gla, m_w_up_mla, m_w_out, m_g_xa, m_g_mem, m_w_xq, m_w_xkv, m_w_xo, m_g_mlp, m_w_mlp1, m_w_mlp2, m_g_final, v_g_mix, v_w_in, v_b_fox_forget, v_w_gla_gate, v_b_gla_gate, v_g_gla_out, v_g_mla_q, v_w_mla_uq, v_g_mla_kv, v_w_mla_ukv, v_b_branch_gate, v_w_up_fox, v_w_up_gla, v_w_up_mla, v_w_out, v_g_xa, v_g_mem, v_w_xq, v_w_xkv, v_w_xo, v_g_mlp, v_w_mlp1, v_w_mlp2, v_g_final):
    return _step(dict(locals()))
```

```python
import functools
import math
import typing

import jax
import jax.numpy as jnp
from jax import lax
from jax.experimental import pallas as pl
from jax.experimental.pallas import tpu as pltpu

F32 = jnp.float32
BF16 = jnp.bfloat16
MESH = pl.DeviceIdType.MESH

D_MODEL = 1024
DEPTH = 2
CHUNK = 64
EPS = 1e-6
FOX_HEADS, FOX_HD = 4, 64
GLA_HEADS, GLA_DK, GLA_DV, GLA_RANK, GLA_TAU = 4, 64, 128, 16, 16.0
MLA_HEADS, MLA_Q_RANK, MLA_KV_RANK, MLA_NOPE, MLA_ROPE, MLA_VD = 4, 256, 128, 64, 32, 64
ROPE_BASE = 10000.0
XA_HEADS, XA_HD = 4, 128
D_FF = 4 * D_MODEL
IN_SIZES = (256, 256, 256, 4, 256, 256, 512, 16, 512, 256, 128, 32, 3072)
N_IN = sum(IN_SIZES)

ADAM_LR, ADAM_B1, ADAM_B2, ADAM_EPS, ADAM_WD, ADAM_STEP = 0.001, 0.9, 0.999, 1e-08, 0.01, 10

N_CHIPS = 4
N_DEV = 8
LANES = 128
VMEM_LIMIT = 48 * 1024 * 1024
MASK_VALUE = -1e30

BIG = (('w_in', 2), ('w_gla_gate', 2), ('w_mla_uq', 2), ('w_mla_ukv', 2), ('w_up_fox', 2), ('w_up_gla', 2),
       ('w_up_mla', 2), ('w_out', 1), ('w_xq', 1), ('w_xkv', 1), ('w_xo', 2), ('w_mlp1', 2), ('w_mlp2', 1))
SMALL = ('g_mix', 'b_fox_forget', 'b_gla_gate', 'g_gla_out', 'g_mla_q', 'g_mla_kv', 'b_branch_gate',
         'g_xa', 'g_mem', 'g_mlp', 'g_final')
ORDER = ('g_mix', 'w_in', 'b_fox_forget', 'w_gla_gate', 'b_gla_gate', 'g_gla_out', 'g_mla_q', 'w_mla_uq',
         'g_mla_kv', 'w_mla_ukv', 'b_branch_gate', 'w_up_fox', 'w_up_gla', 'w_up_mla', 'w_out', 'g_xa', 'g_mem',
         'w_xq', 'w_xkv', 'w_xo', 'g_mlp', 'w_mlp1', 'w_mlp2', 'g_final')


def _params(*sem):
    return pltpu.CompilerParams(dimension_semantics=sem, vmem_limit_bytes=VMEM_LIMIT)


def _sig(x):
    return 1.0 / (1.0 + jnp.exp(-x))


def _logsig(x):
    return jnp.minimum(x, 0.0) - jnp.log(1.0 + jnp.exp(-jnp.abs(x)))


NN = (((1,), (0,)), ((), ()))
NT = (((1,), (1,)), ((), ()))
TN = (((0,), (0,)), ((), ()))


def _dot(a, b, dims=NN):
    return lax.dot_general(a, b, dims, preferred_element_type=F32)


class Cols(typing.NamedTuple):
    arr: jax.Array
    width: int
    blk: int


def _tri_dot(tri, x):
    hi = x.astype(BF16)
    r1 = x - hi.astype(F32)
    mid = r1.astype(BF16)
    lo = (r1 - mid.astype(F32)).astype(BF16)
    return _dot(tri, hi) + _dot(tri, mid) + _dot(tri, lo)


MM_TILES = ((1024, 1024), (1024, 512), (512, 1024), (512, 512), (512, 256), (256, 512), (256, 256), (128, 128))
MM_VMEM_BUDGET = 38 * 1024 * 1024


def _mm_tiles(m, n, k, a_bytes, b_bytes, out_bytes, ex_bytes, has_norm, emit_norm, has_fn):
    for tm, tn in MM_TILES:
        tm, tn = min(tm, m), min(tn, n)
        if m % tm or n % tn:
            continue
        blocks = tm * k * a_bytes + k * tn * b_bytes + tm * tn * (out_bytes + ex_bytes) + (tm * k * 2 if emit_norm else 0)
        temps = tm * tn * 4 + (tm * k * 2 if has_norm else 0) + (tm * k * 6 if has_fn or has_norm else 0)
        if 2 * blocks + temps <= MM_VMEM_BUDGET:
            return tm, tn
    raise ValueError((m, n, k))


def _mm(a, b, *, mode, out_dtype, name, norm_g=None, emit_norm=False, a_fn=None, extras=(), epilogue=None):
    a_blk = 0
    if isinstance(a, Cols):
        a, width, a_blk = a
        a_shape = (a.shape[0], width)
    else:
        a_shape = a.shape
    if mode == 'tn':
        k, m = a_shape
    else:
        m, k = a_shape
    n = b.shape[0] if mode == 'nt' else b.shape[1]
    assert (b.shape[1] if mode == 'nt' else b.shape[0]) == k, (name, a.shape, b.shape)
    has_norm = norm_g is not None
    ex_bytes = sum(arr.dtype.itemsize for arr, kind, _ in extras if kind == 'mn')
    tm, tn = _mm_tiles(m, n, k, a.dtype.itemsize, b.dtype.itemsize, jnp.dtype(out_dtype).itemsize, ex_bytes, has_norm,
                       emit_norm, a_fn is not None)
    assert all(col % tn == 0 for _, _, col in extras), (name, tn)
    assert a_blk == 0 or (mode == 'nn') or (mode == 'tn' and tm == m)
    if mode == 'tn':
        a_spec = pl.BlockSpec((k, tm), lambda i, j: (0, i + a_blk))
    else:
        a_spec = pl.BlockSpec((tm, k), lambda i, j: (i, a_blk))
    b_spec = pl.BlockSpec((tn, k), lambda i, j: (j, 0)) if mode == 'nt' else pl.BlockSpec((k, tn), lambda i, j: (0, j))
    dims = {'nn': NN, 'nt': NT, 'tn': TN}[mode]
    assert not (has_norm and mode != 'nn')
    n_ex = len(extras)

    def body(*refs):
        a_ref, b_ref = refs[0], refs[1]
        pos = 2
        g_ref = None
        if has_norm:
            g_ref = refs[pos]
            pos += 1
        ex_refs = refs[pos:pos + n_ex]
        pos += n_ex
        o_ref = refs[pos]
        pos += 1
        h_ref = None
        if emit_norm:
            h_ref = refs[pos]
            pos += 1
        if has_norm or a_fn is not None:
            an_ref = refs[pos]

            @pl.when(pl.program_id(1) == 0)
            def _():
                if has_norm:
                    xf = a_ref[...].astype(F32)
                    y = xf * lax.rsqrt(jnp.mean(xf * xf, axis=-1, keepdims=True) + EPS) * g_ref[...]
                else:
                    y = a_fn(a_ref[...])
                an_ref[...] = y.astype(BF16)
                if emit_norm:
                    h_ref[...] = y.astype(BF16)

            av = an_ref[...]
        else:
            av = a_ref[...].astype(BF16)
        acc = _dot(av, b_ref[...].astype(BF16), dims)
        if epilogue is not None:
            acc = epilogue(acc, *[r[...] for r in ex_refs])
        o_ref[...] = acc.astype(out_dtype)

    in_specs = [a_spec, b_spec]
    args = [a, b]
    if has_norm:
        in_specs.append(pl.BlockSpec((1, k), lambda i, j: (0, 0)))
        args.append(norm_g)
    for arr, kind, col in extras:
        if kind == 'mn':
            in_specs.append(pl.BlockSpec((tm, tn), lambda i, j, o=col // tn: (i, j + o)))
        else:
            in_specs.append(pl.BlockSpec((1, tn), lambda i, j, o=col // tn: (0, j + o)))
        args.append(arr)
    out_shape = [jax.ShapeDtypeStruct((m, n), out_dtype)]
    out_specs = [pl.BlockSpec((tm, tn), lambda i, j: (i, j))]
    if emit_norm:
        out_shape.append(jax.ShapeDtypeStruct((m, k), BF16))
        out_specs.append(pl.BlockSpec((tm, k), lambda i, j: (i, 0)))
    scratch = [pltpu.VMEM((k, tm) if mode == 'tn' else (tm, k), BF16)] if has_norm or a_fn is not None else []
    res = pl.pallas_call(
        body, name=name, grid=(m // tm, n // tn), in_specs=in_specs, out_specs=out_specs, out_shape=out_shape,
        scratch_shapes=scratch, compiler_params=_params('arbitrary', 'arbitrary'))(*args)
    return res if emit_norm else res[0]


def _mn(col_off=0):
    return 'mn', col_off


def _nvec(col_off=0):
    return 'n', col_off


def _rowwise(fn, rows, consts, outs, sums=(), *, name, ts=256):
    views = [x if isinstance(x, Cols) else Cols(x, x.shape[1], 0) for x in rows]
    rows = [v.arr for v in views]
    r = rows[0].shape[0]
    ts = min(ts, r)
    assert r % ts == 0, (name, r, ts)
    nr, nc, no, ns = len(rows), len(consts), len(outs), len(sums)

    def body(*refs):
        vals = fn(*[x[...] for x in refs[:nr + nc]])
        for q in range(no):
            refs[nr + nc + q][...] = vals[q].astype(outs[q][1])
        if ns:
            @pl.when(pl.program_id(0) == 0)
            def _():
                for q in range(ns):
                    refs[nr + nc + no + q][...] = jnp.zeros((1, sums[q]), F32)

            for q in range(ns):
                refs[nr + nc + no + q][...] += jnp.sum(vals[no + q].astype(F32), axis=0, keepdims=True)

    in_specs = [pl.BlockSpec((ts, v.width), lambda i, blk=v.blk: (i, blk)) for v in views]
    in_specs += [pl.BlockSpec(x.shape, lambda i, nd=x.ndim: (0,) * nd) for x in consts]
    out_specs = [pl.BlockSpec((ts, w), lambda i: (i, 0)) for w, _ in outs]
    out_specs += [pl.BlockSpec((1, w), lambda i: (0, 0)) for w in sums]
    out_shape = [jax.ShapeDtypeStruct((r, w), dt) for w, dt in outs]
    out_shape += [jax.ShapeDtypeStruct((1, w), F32) for w in sums]
    return pl.pallas_call(body, name=name, grid=(r // ts,), in_specs=in_specs, out_specs=out_specs,
                          out_shape=out_shape, compiler_params=_params('arbitrary'))(*rows, *consts)


def _cumsum_rows(x, *, reverse, name, bs=256):
    s, w = x.shape
    bs = min(bs, s)
    nb = s // bs

    def body(x_ref, o_ref, carry):
        @pl.when(pl.program_id(0) == 0)
        def _():
            carry[...] = jnp.zeros_like(carry)

        r = lax.broadcasted_iota(jnp.int32, (bs, bs), 0)
        c = lax.broadcasted_iota(jnp.int32, (bs, bs), 1)
        tri = jnp.where((c >= r) if reverse else (c <= r), 1.0, 0.0).astype(BF16)
        xv = x_ref[...]
        o_ref[...] = _tri_dot(tri, xv) + carry[...]
        carry[...] += jnp.sum(xv, axis=0, keepdims=True)

    imap = (lambda i: (nb - 1 - i, 0)) if reverse else (lambda i: (i, 0))
    return pl.pallas_call(body, name=name, grid=(nb,), in_specs=[pl.BlockSpec((bs, w), imap)],
                          out_specs=pl.BlockSpec((bs, w), imap), out_shape=jax.ShapeDtypeStruct((s, w), F32),
                          scratch_shapes=[pltpu.VMEM((1, w), F32)], compiler_params=_params('arbitrary'))(x)


def _mask(mode, q0, k0, bq, bk):
    qpos = q0 + lax.broadcasted_iota(jnp.int32, (bq, bk), 0)
    kpos = k0 + lax.broadcasted_iota(jnp.int32, (bq, bk), 1)
    if mode == 'causal':
        return kpos <= qpos
    return kpos < (jnp.right_shift(qpos, int(math.log2(CHUNK))) + 1) * CHUNK


ROPE_SHIFT = int(math.log2(MLA_ROPE))
ATTN_ROW_SLAB = 512


def _lane_masks(g, b, rope):
    lane = lax.broadcasted_iota(jnp.int32, (1, LANES), 1)
    heads = [None if g == 1 else (lane >= hh * (LANES // g)) & (lane < (hh + 1) * (LANES // g)) for hh in range(g)]
    ropes = [jnp.right_shift(lane, ROPE_SHIFT) == b * g + hh for hh in range(g)] if rope else [None] * g
    return heads, ropes


def _sel(mask, x):
    return x if mask is None else jnp.where(mask, x, jnp.zeros_like(x))


class Step(typing.NamedTuple):
    qi: typing.Any
    kj: typing.Any
    first: typing.Any
    last: typing.Any
    plain: typing.Any
    masked: typing.Any


def _fwd_steps(tri, nq, nk):
    if not tri:
        return (nq, nk), lambda i, j: Step(i, j, j == 0, j == nk - 1, True, False)
    if nq % 2:
        return (nq, nk), lambda i, j: Step(i, jnp.minimum(i, j), j == 0, j == nk - 1, j < i, j == i)

    def at(i, t):
        low = t <= i
        diag = (t == i) | (t == nq)
        return Step(jnp.where(low, i, nq - 1 - i), jnp.where(low, t, t - (i + 1)), (t == 0) | (t == i + 1), diag,
                    jnp.logical_not(diag), diag)

    return (nq // 2, nq + 1), at


def _bwd_steps(tri, nq, nk):
    if not tri:
        return (nk, nq), lambda j, i: Step(i, j, i == 0, i == nq - 1, True, False)
    if nk % 2:
        return (nk, nq), lambda j, i: Step(jnp.maximum(i, j), j, i == 0, i == nq - 1, i > j, i == j)

    def at(j, t):
        n1 = nq - j
        low = t < n1
        diag = (t == 0) | (t == n1)
        return Step(jnp.where(low, j + t, nk - 1 - j + t - n1), jnp.where(low, j, nk - 1 - j), diag,
                    (t == n1 - 1) | (t == nq), jnp.logical_not(diag), diag)

    return (nk // 2, nq + 1), at


def _carried(comm, refs, n_in, n_out):
    ci, co = len(comm.ins), len(comm.out_shapes)
    ins = refs[n_in:n_in + ci]
    outs = refs[n_in + ci + n_out:n_in + ci + n_out + co]
    rest = refs[:n_in] + refs[n_in + ci:n_in + ci + n_out] + refs[n_in + ci + n_out + co:-2]
    return rest, (ins, outs, refs[-2], refs[-1])


def _mattn_fwd(q, k, v, *, qc, kc, vc, nb, g, scale, mode, name, ck=None, qr=None, qrc=0, kr=None, blk=512,
               comm=None):
    s, t = q.shape[0], k.shape[0]
    bq, bk = min(blk, s), min(blk, t)
    nq, nk = s // bq, t // bk
    tri = mode != 'full'
    bias, rope = ck is not None, qr is not None
    assert not tri or (bq == bk and bq % CHUNK == 0)
    rs = min(ATTN_ROW_SLAB, bq)
    n_in = 3 + bias + 2 * rope
    (n1, n2), step_at = _fwd_steps(tri, nq, nk)

    def body(*refs):
        refs = list(refs)
        b, p1, p2 = pl.program_id(0), pl.program_id(1), pl.program_id(2)
        st = step_at(p1, p2)
        i, j = st.qi, st.kj
        if comm is not None:
            refs, comm_refs = _carried(comm, refs, n_in, 2)
            pl.when((b == 0) & (p1 == 0) & (p2 == 0))(lambda: comm.start(*comm_refs))
        q_ref, k_ref, v_ref = refs[:3]
        pos = 3
        ck_ref = qr_ref = kr_ref = None
        if bias:
            ck_ref = refs[pos]
            pos += 1
        if rope:
            qr_ref, kr_ref = refs[pos:pos + 2]
            pos += 2
        o_ref, lse_ref, m_s, l_s, acc_s = refs[pos:]
        heads, ropes = _lane_masks(g, b, rope)

        @pl.when(st.first)
        def _():
            m_s[...] = jnp.full_like(m_s, MASK_VALUE)
            l_s[...] = jnp.zeros_like(l_s)
            acc_s[...] = jnp.zeros_like(acc_s)

        def compute(masked):
            k2, v2 = k_ref[...], v_ref[...]
            for r in range(bq // rs):
                rows = pl.ds(r * rs, rs)
                q2 = q_ref[rows, :]
                alphas, pvs = [], []
                for hh in range(g):
                    sc = _dot(_sel(heads[hh], q2), k2, NT)
                    if rope:
                        sc = sc + _dot(_sel(ropes[hh], qr_ref[rows, :]), kr_ref[...], NT)
                    sc = sc * scale
                    if bias:
                        sc = sc - ck_ref[0, hh:hh + 1, :]
                    if masked:
                        sc = jnp.where(_mask(mode, i * bq + r * rs, j * bk, rs, bk), sc, MASK_VALUE)
                    m_prev = m_s[hh, rows]
                    m_new = jnp.maximum(m_prev, jnp.max(sc, axis=1, keepdims=True))
                    alpha = jnp.exp(m_prev - m_new)
                    p = jnp.exp(sc - m_new)
                    l_s[hh, rows] = alpha * l_s[hh, rows] + jnp.sum(p, axis=1, keepdims=True)
                    m_s[hh, rows] = m_new
                    alphas.append(alpha)
                    pvs.append(_dot(p.astype(BF16), _sel(heads[hh], v2)))
                alpha = alphas[0]
                for hh in range(1, g):
                    alpha = jnp.where(heads[hh], alphas[hh], alpha)
                acc_s[rows, :] = acc_s[rows, :] * alpha + sum(pvs[1:], pvs[0])

        if tri:
            pl.when(st.plain)(functools.partial(compute, False))
            pl.when(st.masked)(functools.partial(compute, True))
        else:
            compute(False)

        @pl.when(st.last)
        def _():
            lane = lax.broadcasted_iota(jnp.int32, (bq, LANES), 1)
            l_full, lse = l_s[0], jnp.zeros((bq, LANES), F32)
            for hh in range(g):
                if hh:
                    l_full = jnp.where(heads[hh], l_s[hh], l_full)
                lse = jnp.where(lane == hh, m_s[hh] + jnp.log(l_s[hh]), lse)
            o_ref[...] = (acc_s[...] / l_full).astype(o_ref.dtype)
            lse_ref[...] = lse

        if comm is not None:
            pl.when((b == nb - 1) & (p1 == n1 - 1) & (p2 == n2 - 1))(lambda: comm.finish(*comm_refs))

    qi = lambda p1, p2: step_at(p1, p2).qi
    kj = lambda p1, p2: step_at(p1, p2).kj
    in_specs = [pl.BlockSpec((bq, LANES), lambda b, p1, p2: (qi(p1, p2), qc + b)),
                pl.BlockSpec((bk, LANES), lambda b, p1, p2: (kj(p1, p2), kc + b)),
                pl.BlockSpec((bk, LANES), lambda b, p1, p2: (kj(p1, p2), vc + b))]
    args = [q, k, v]
    if bias:
        in_specs.append(pl.BlockSpec((1, 8, bk), lambda b, p1, p2: (b, 0, kj(p1, p2))))
        args.append(ck)
    if rope:
        in_specs += [pl.BlockSpec((bq, LANES), lambda b, p1, p2: (qi(p1, p2), qrc)),
                     pl.BlockSpec((bk, LANES), lambda b, p1, p2: (kj(p1, p2), 0))]
        args += [qr, kr]
    out = pl.BlockSpec((bq, LANES), lambda b, p1, p2: (qi(p1, p2), b))
    out_specs = [out, out]
    out_shape = [jax.ShapeDtypeStruct((s, LANES * nb), BF16), jax.ShapeDtypeStruct((s, LANES * nb), F32)]
    scratch = [pltpu.VMEM((g, bq, 1), F32), pltpu.VMEM((g, bq, 1), F32), pltpu.VMEM((bq, LANES), F32)]
    if comm is not None:
        in_specs += [ANY] * len(comm.ins)
        args += comm.ins
        out_specs += [ANY] * len(comm.out_shapes)
        out_shape += comm.out_shapes
        scratch += _sems(comm.n_sems, comm.n_sems)
    res = pl.pallas_call(body, name=name, grid=(nb, n1, n2), in_specs=in_specs, out_specs=out_specs, out_shape=out_shape,
                         scratch_shapes=scratch, compiler_params=_params('arbitrary', 'arbitrary', 'arbitrary'))(*args)
    return res if comm is None else (res[0], res[1], res[2:])


def _mattn_bwd(q, k, v, o, do, lse, *, qc, kc, vc, nb, g, scale, mode, name, ck=None, qr=None, qrc=0, kr=None,
               blk=512, comm=None):
    s, t = q.shape[0], k.shape[0]
    bq, bk = min(blk, s), min(blk, t)
    nq, nk = s // bq, t // bk
    tri = mode != 'full'
    bias, rope = ck is not None, qr is not None
    rs = min(ATTN_ROW_SLAB, bq)
    n_in, n_out = 6 + bias + 2 * rope, 3 + 2 * bias + 2 * rope
    (n1, n2), step_at = _bwd_steps(tri, nq, nk)

    def body(*refs):
        refs = list(refs)
        if comm is not None:
            refs, comm_refs = _carried(comm, refs, n_in, n_out)
            first = (pl.program_id(0) == 0) & (pl.program_id(1) == 0) & (pl.program_id(2) == 0)
            pl.when(first)(lambda: comm.start(*comm_refs))
        q_ref, k_ref, v_ref, o_ref, do_ref, lse_ref = refs[:6]
        pos = 6
        ck_ref = qr_ref = kr_ref = dck_ref = dcq_ref = dqr_ref = dkr_ref = dck_s = None
        if bias:
            ck_ref = refs[pos]
            pos += 1
        if rope:
            qr_ref, kr_ref = refs[pos:pos + 2]
            pos += 2
        dq_ref, dk_ref, dv_ref = refs[pos:pos + 3]
        pos += 3
        if bias:
            dck_ref, dcq_ref = refs[pos:pos + 2]
            pos += 2
        if rope:
            dqr_ref, dkr_ref = refs[pos:pos + 2]
            pos += 2
        dk_s, dv_s = refs[pos:pos + 2]
        if bias:
            dck_s = refs[pos + 2]
        b, p1, p2 = pl.program_id(0), pl.program_id(1), pl.program_id(2)
        st = step_at(p1, p2)
        i, j = st.qi, st.kj
        heads, ropes = _lane_masks(g, b, rope)

        @pl.when((p1 == 0) & (p2 == 0))
        def _():
            dq_ref[...] = jnp.zeros_like(dq_ref)
            if bias:
                dcq_ref[...] = jnp.zeros_like(dcq_ref)

        if rope:
            @pl.when((b == 0) & (p1 == 0) & (p2 == 0))
            def _():
                dqr_ref[...] = jnp.zeros_like(dqr_ref)
                dkr_ref[...] = jnp.zeros_like(dkr_ref)

        @pl.when(st.first)
        def _():
            dk_s[...] = jnp.zeros_like(dk_s)
            dv_s[...] = jnp.zeros_like(dv_s)
            if bias:
                dck_s[...] = jnp.zeros_like(dck_s)

        def compute(masked):
            k2, v2 = k_ref[...], v_ref[...]
            lane = lax.broadcasted_iota(jnp.int32, (rs, LANES), 1)
            rk = pl.ds(pl.multiple_of(j * bk, bk), bk)
            add = lambda tot, x: x if tot is None else tot + x
            dv_t = dk_t = dkr_t = None
            dck_t = [None] * g
            for r in range(bq // rs):
                rows = pl.ds(r * rs, rs)
                rq = pl.ds(pl.multiple_of(i * bq + r * rs, rs), rs)
                q2, do2, lse2 = q_ref[rows, :], do_ref[rows, :], lse_ref[rows, :]
                dd = do2.astype(F32) * o_ref[rows, :].astype(F32)
                dq_t = dqr_t = dcq_t = None
                for hh in range(g):
                    qm = _sel(heads[hh], q2)
                    sc = _dot(qm, k2, NT)
                    if rope:
                        qrm = _sel(ropes[hh], qr_ref[rows, :])
                        sc = sc + _dot(qrm, kr_ref[...], NT)
                    sc = sc * scale
                    if bias:
                        sc = sc - ck_ref[0, hh:hh + 1, :]
                    if masked:
                        sc = jnp.where(_mask(mode, i * bq + r * rs, j * bk, rs, bk), sc, MASK_VALUE)
                    p = jnp.exp(sc - jnp.sum(jnp.where(lane == hh, lse2, 0.0), axis=1, keepdims=True))
                    dom = _sel(heads[hh], do2)
                    dp = _dot(dom, v2, NT)
                    delta = jnp.sum(_sel(heads[hh], dd), axis=1, keepdims=True)
                    ds = p * (dp - delta)
                    dsb = ds.astype(BF16)
                    dv_t = add(dv_t, _dot(p.astype(BF16), dom, TN))
                    dk_t = add(dk_t, _dot(dsb, qm, TN))
                    dq_t = add(dq_t, _dot(dsb, _sel(heads[hh], k2)))
                    if rope:
                        dqr_t = add(dqr_t, _dot(dsb, _sel(ropes[hh], kr_ref[...])))
                        dkr_t = add(dkr_t, _dot(dsb, qrm, TN))
                    if bias:
                        dck_t[hh] = add(dck_t[hh], jnp.sum(ds, axis=0, keepdims=True))
                        dcq_t = add(dcq_t, jnp.where(lane == hh, jnp.sum(ds, axis=1, keepdims=True), 0.0))
                dq_ref[rq, :] += scale * dq_t
                if rope:
                    dqr_ref[rq, :] += scale * dqr_t
                if bias:
                    dcq_ref[rq, :] += dcq_t
            dv_s[...] += dv_t
            dk_s[...] += scale * dk_t
            if rope:
                dkr_ref[rk, :] += scale * dkr_t
            if bias:
                for hh in range(g):
                    dck_s[hh:hh + 1, :] -= dck_t[hh]

        if tri:
            pl.when(st.plain)(functools.partial(compute, False))
            pl.when(st.masked)(functools.partial(compute, True))
        else:
            compute(False)

        @pl.when(st.last)
        def _():
            dk_ref[...] = dk_s[...]
            dv_ref[...] = dv_s[...]
            if bias:
                dck_ref[0] = dck_s[...]

        if comm is not None:
            pl.when((b == nb - 1) & (p1 == n1 - 1) & (p2 == n2 - 1))(lambda: comm.finish(*comm_refs))

    qrow = lambda col: pl.BlockSpec((bq, LANES), lambda b, p1, p2: (step_at(p1, p2).qi, col(b)))
    krow = lambda col: pl.BlockSpec((bk, LANES), lambda b, p1, p2: (step_at(p1, p2).kj, col(b)))
    in_specs = [qrow(lambda b: qc + b), krow(lambda b: kc + b), krow(lambda b: vc + b), qrow(lambda b: b),
                qrow(lambda b: b), qrow(lambda b: b)]
    args = [q, k, v, o, do, lse]
    whole = lambda rows: pl.BlockSpec((rows, LANES), lambda b, j, i: (0, b))
    out_specs = [whole(s), krow(lambda b: b), krow(lambda b: b)]
    out_shape = [jax.ShapeDtypeStruct((s, LANES * nb), F32), jax.ShapeDtypeStruct((t, LANES * nb), F32),
                 jax.ShapeDtypeStruct((t, LANES * nb), F32)]
    scratch = [pltpu.VMEM((bk, LANES), F32), pltpu.VMEM((bk, LANES), F32)]
    if bias:
        ckj = pl.BlockSpec((1, 8, bk), lambda b, p1, p2: (b, 0, step_at(p1, p2).kj))
        in_specs.append(ckj)
        args.append(ck)
        out_specs += [ckj, whole(s)]
        out_shape += [jax.ShapeDtypeStruct((nb, 8, t), F32), jax.ShapeDtypeStruct((s, LANES * nb), F32)]
    if rope:
        in_specs += [qrow(lambda b: qrc), krow(lambda b: 0)]
        args += [qr, kr]
        out_specs += [pl.BlockSpec((s, LANES), lambda b, j, i: (0, 0)), pl.BlockSpec((t, LANES), lambda b, j, i: (0, 0))]
        out_shape += [jax.ShapeDtypeStruct((s, LANES), F32), jax.ShapeDtypeStruct((t, LANES), F32)]
    if bias:
        scratch.append(pltpu.VMEM((8, bk), F32))
    if comm is not None:
        in_specs += [ANY] * len(comm.ins)
        args += comm.ins
        out_specs += [ANY] * len(comm.out_shapes)
        out_shape += comm.out_shapes
        scratch += _sems(comm.n_sems, comm.n_sems)
    res = pl.pallas_call(body, name=name, grid=(nb, n1, n2), in_specs=in_specs, out_specs=out_specs,
                         out_shape=out_shape, scratch_shapes=scratch,
                         compiler_params=_params('arbitrary', 'arbitrary', 'arbitrary'))(*args)
    return res if comm is None else (*res[:n_out], res[n_out:])


def _gla_chunk(la_c, k_c):
    r = lax.broadcasted_iota(jnp.int32, (CHUNK, CHUNK), 0)
    c = lax.broadcasted_iota(jnp.int32, (CHUNK, CHUNK), 1)
    tri = jnp.where(c <= r, 1.0, 0.0).astype(BF16)
    cum = _tri_dot(tri, la_c)
    end = jnp.sum(la_c, axis=0, keepdims=True)
    dec = jnp.exp(end - cum)
    return dec, k_c * dec, jnp.exp(end)


GLA_PAIRS = GLA_HEADS // 2


def _gla_fwd(z, la, *, qc, kc, vc, name, blk=512):
    s = z.shape[0]
    bs = min(blk, s)
    ncb = bs // CHUNK
    nblk = s // bs

    def body(q_ref, k_ref, va_ref, vb_ref, la_ref, o_ref, st_ref, st):
        @pl.when(pl.program_id(1) == 0)
        def _():
            st[...] = jnp.zeros_like(st)

        heads, _ = _lane_masks(2, 0, False)
        v_refs = (va_ref, vb_ref)
        for c in range(ncb):
            sl = pl.ds(c * CHUNK, CHUNK)
            _, kf, a = _gla_chunk(la_ref[sl, :], k_ref[sl, :])
            qs = q_ref[sl, :] * (GLA_DK ** -0.5)
            for hh in range(2):
                ut = _dot(v_refs[hh][sl, :].astype(BF16), _sel(heads[hh], kf).astype(BF16), TN)
                new = a * st[hh] + ut
                st[hh] = new
                st_ref[0, c, hh] = new
                o_ref[sl, hh * GLA_DV:(hh + 1) * GLA_DV] = _dot(_sel(heads[hh], qs).astype(BF16), new.astype(BF16), NT)

    col = lambda c0, m=1: pl.BlockSpec((bs, LANES), lambda b, i: (i, c0 + m * b))
    return pl.pallas_call(
        body, name=name, grid=(GLA_PAIRS, nblk),
        in_specs=[col(qc), col(kc), col(vc, 2), col(vc + 1, 2), col(0)],
        out_specs=[pl.BlockSpec((bs, 2 * GLA_DV), lambda b, i: (i, b)),
                   pl.BlockSpec((1, ncb, 2, GLA_DV, LANES), lambda b, i: (b, i, 0, 0, 0))],
        out_shape=[jax.ShapeDtypeStruct((s, GLA_HEADS * GLA_DV), F32),
                   jax.ShapeDtypeStruct((GLA_PAIRS, s // CHUNK, 2, GLA_DV, LANES), F32)],
        scratch_shapes=[pltpu.VMEM((2, GLA_DV, LANES), F32)],
        compiler_params=_params('arbitrary', 'arbitrary'))(z, z, z, z, la)


def _gla_bwd(z, la, st_all, st_prev, do, *, qc, kc, vc, name, blk=512):
    s = z.shape[0]
    bs = min(blk, s)
    ncb = bs // CHUNK
    nblk = s // bs

    def body(q_ref, k_ref, va_ref, vb_ref, la_ref, st_ref, sp_ref, do_ref, dq_ref, dk_ref, dv_ref, dla_ref, ga):
        @pl.when(pl.program_id(1) == 0)
        def _():
            ga[...] = jnp.zeros_like(ga)

        r = lax.broadcasted_iota(jnp.int32, (CHUNK, CHUNK), 0)
        cc = lax.broadcasted_iota(jnp.int32, (CHUNK, CHUNK), 1)
        tri_rev = jnp.where(cc >= r, 1.0, 0.0).astype(BF16)
        heads, _ = _lane_masks(2, 0, False)
        v_refs = (va_ref, vb_ref)
        for c in reversed(range(ncb)):
            sl = pl.ds(c * CHUNK, CHUNK)
            dec, kf, a = _gla_chunk(la_ref[sl, :], k_ref[sl, :])
            qs = q_ref[sl, :] * (GLA_DK ** -0.5)
            dq2 = jnp.zeros((CHUNK, LANES), F32)
            dkd = jnp.zeros((CHUNK, LANES), F32)
            da = jnp.zeros((1, LANES), F32)
            for hh in range(2):
                hv = slice(hh * GLA_DV, (hh + 1) * GLA_DV)
                dob = do_ref[sl, hv].astype(BF16)
                g = _dot(dob, _sel(heads[hh], qs).astype(BF16), TN) + ga[hh]
                gb = g.astype(BF16)
                dq2 = dq2 + _dot(dob, st_ref[0, c, hh].astype(BF16))
                dv_ref[sl, hv] = _dot(_sel(heads[hh], kf).astype(BF16), gb, NT)
                dkd = dkd + _dot(v_refs[hh][sl, :].astype(BF16), gb)
                da = da + jnp.sum(g * sp_ref[0, c, hh], axis=0, keepdims=True)
                ga[hh] = a * g
            dq_ref[sl, :] = (GLA_DK ** -0.5) * dq2
            dk_ref[sl, :] = dkd * dec
            e = dkd * kf
            dend = jnp.sum(e, axis=0, keepdims=True) + da * a
            dla_ref[sl, :] = dend - _tri_dot(tri_rev, e)

    rev = lambda i: nblk - 1 - i
    col = lambda c0, m=1: pl.BlockSpec((bs, LANES), lambda b, i: (rev(i), c0 + m * b))
    wide = pl.BlockSpec((bs, 2 * GLA_DV), lambda b, i: (rev(i), b))
    stspec = pl.BlockSpec((1, ncb, 2, GLA_DV, LANES), lambda b, i: (b, rev(i), 0, 0, 0))
    return pl.pallas_call(
        body, name=name, grid=(GLA_PAIRS, nblk),
        in_specs=[col(qc), col(kc), col(vc, 2), col(vc + 1, 2), col(0), stspec, stspec, wide],
        out_specs=[col(0), col(0), wide, col(0)],
        out_shape=[jax.ShapeDtypeStruct((s, GLA_HEADS * GLA_DK), F32), jax.ShapeDtypeStruct((s, GLA_HEADS * GLA_DK), F32),
                   jax.ShapeDtypeStruct((s, GLA_HEADS * GLA_DV), F32), jax.ShapeDtypeStruct((s, GLA_HEADS * GLA_DK), F32)],
        scratch_shapes=[pltpu.VMEM((2, GLA_DV, LANES), F32)],
        compiler_params=_params('arbitrary', 'arbitrary'))(z, z, z, z, la, st_all, st_prev, do)


def _place():
    return lax.axis_index('x'), lax.axis_index('y'), lax.axis_index('c')


ANY = pl.BlockSpec(memory_space=pl.ANY)


def _all_gather8(blk, *, name):
    m, n = blk.shape

    def body(x_ref, out_ref, send_sems, recv_sems, local_sem):
        x, y, c = _place()
        me, sibling = (x, y, c), (x, y, 1 - c)
        chips = [(1 - x, y), (x, 1 - y), (1 - x, 1 - y)]

        def slot(px, py, pc):
            return out_ref.at[4 * px + 2 * py + pc]

        def copy(q, block, to, src=None):
            return pltpu.make_async_remote_copy(
                src_ref=slot(*block) if src is None else src, dst_ref=slot(*block), send_sem=send_sems.at[q],
                recv_sem=recv_sems.at[q], device_id=to, device_id_type=MESH)

        mine = pltpu.make_async_copy(x_ref, slot(*me), local_sem)
        mine.start()
        first = [copy(0, me, sibling, src=x_ref)]
        first += [copy(1 + q, me, (*chip, c), src=x_ref) for q, chip in enumerate(chips)]
        for cp in first:
            cp.start()
        passed = [copy(4 + q, (*chip, c), sibling) for q, chip in enumerate(chips)]
        for q, chip in enumerate(chips):
            copy(1 + q, (*chip, c), me).wait_recv()
            passed[q].start()
        copy(0, sibling, me).wait_recv()
        for q, chip in enumerate(chips):
            copy(4 + q, (*chip, 1 - c), me).wait_recv()
        for cp in first + passed:
            cp.wait_send()
        mine.wait()

    return pl.pallas_call(
        body, name=name, in_specs=[ANY], out_specs=ANY, out_shape=jax.ShapeDtypeStruct((N_DEV, m, n), blk.dtype),
        scratch_shapes=[pltpu.SemaphoreType.DMA((7,)), pltpu.SemaphoreType.DMA((7,)), pltpu.SemaphoreType.DMA(())],
    )(blk)


def _sems(*counts):
    return [pltpu.SemaphoreType.DMA((n,)) for n in counts]


class Comm(typing.NamedTuple):
    ins: list
    out_shapes: list
    n_sems: int
    start: typing.Callable
    finish: typing.Callable


def _remote(src, dst, send_sems, recv_sems, idx, to):
    return lambda: pltpu.make_async_remote_copy(src_ref=src, dst_ref=dst, send_sem=send_sems.at[idx],
                                                recv_sem=recv_sems.at[idx], device_id=to, device_id_type=MESH)


def _comm_from(copies, ins, out_shapes, n_sems):
    def start(*refs):
        for cp in copies(*refs)[0]:
            cp().start()

    def finish(*refs):
        sent, received = copies(*refs)
        for cp in received:
            cp().wait_recv()
        for cp in sent:
            cp().wait_send()

    return Comm(list(ins), list(out_shapes), n_sems, start, finish)


def _run_comm(comm, *, name, alias=False):
    n_in, n_out = len(comm.ins), len(comm.out_shapes)

    def body(*refs):
        ins, outs, sems = refs[:n_in], refs[n_in:n_in + n_out], refs[n_in + n_out:]
        comm.start(ins, outs, *sems)
        comm.finish(ins, outs, *sems)

    return pl.pallas_call(body, name=name, in_specs=[ANY] * n_in, out_specs=[ANY] * n_out, out_shape=comm.out_shapes,
                          input_output_aliases={q: q for q in range(n_in)} if alias else {},
                          scratch_shapes=_sems(comm.n_sems, comm.n_sems))(*comm.ins)


def _half(rows, c):
    h = rows // 2
    return pl.ds(pl.multiple_of(c * h, h), h)


def _gathered(ref, chip, rows, side):
    if not side:
        return ref.at[chip, rows]
    n = ref.shape[1] // N_CHIPS
    return ref.at[rows, pl.ds(pl.multiple_of(chip * n, n), n)]


def _gather_over_ici(ws, side):
    def copies(ins, outs, send_sems, recv_sems):
        x, y, c = _place()
        me_chip = 2 * x + y
        sent, received = [], []
        for q, w in enumerate(ws):
            half, every = _half(w.shape[0], c), pl.ds(0, w.shape[0])
            for k, (px, py) in enumerate([(1 - x, y), (x, 1 - y), (1 - x, 1 - y)]):
                sent.append(_remote(ins[q].at[half], _gathered(outs[q], me_chip, half, side[q]), send_sems, recv_sems,
                                    4 * q + k, (px, py, c)))
                slot = _gathered(outs[q], 2 * px + py, half, side[q])
                received.append(_remote(slot, slot, send_sems, recv_sems, 4 * q + k, (px, py, c)))
            whole = _remote(ins[q], _gathered(outs[q], me_chip, every, side[q]), send_sems, recv_sems, 4 * q + 3,
                            (x, y, 1 - c))
            sent.append(whole)
            received.append(whole)
        return sent, received

    shapes = [jax.ShapeDtypeStruct((w.shape[0], N_CHIPS * w.shape[1]) if sd else (N_CHIPS,) + w.shape, w.dtype)
              for w, sd in zip(ws, side)]
    return _comm_from(copies, ws, shapes, 4 * len(ws))


def _gather_over_d2d(parts, side):
    def copies(ins, outs, send_sems, recv_sems):
        x, y, c = _place()
        sent, received = [], []
        for q, w in enumerate(parts):
            rows = w.shape[0] if side[q] else w.shape[1]
            for k, (px, py) in enumerate([(1 - x, y), (x, 1 - y), (1 - x, 1 - y)]):
                mine = _gathered(outs[q], 2 * px + py, _half(rows, c), side[q])
                theirs = _gathered(outs[q], 2 * px + py, _half(rows, 1 - c), side[q])
                sent.append(_remote(mine, mine, send_sems, recv_sems, 3 * q + k, (x, y, 1 - c)))
                received.append(_remote(theirs, theirs, send_sems, recv_sems, 3 * q + k, (x, y, 1 - c)))
        return sent, received

    return _comm_from(copies, parts, [jax.ShapeDtypeStruct(w.shape, w.dtype) for w in parts], 3 * len(parts))


def _to_sibling(gs, *, name):
    n = len(gs)

    def body(*refs):
        ins, outs = refs[:n], refs[n:2 * n]
        send_sems, recv_sems = refs[2 * n:]
        x, y, c = _place()
        cps = [pltpu.make_async_remote_copy(
            src_ref=ins[q], dst_ref=outs[q], send_sem=send_sems.at[q], recv_sem=recv_sems.at[q],
            device_id=(x, y, 1 - c), device_id_type=MESH) for q in range(n)]
        for cp in cps:
            cp.start()
        for cp in cps:
            cp.wait()

    return pl.pallas_call(body, name=name, in_specs=[ANY] * n, out_specs=[ANY] * n,
                          out_shape=[jax.ShapeDtypeStruct(g.shape, g.dtype) for g in gs],
                          scratch_shapes=_sems(n, n))(*gs)


def _chip_exchange(ps):
    def copies(ins, outs, send_sems, recv_sems):
        x, y, c = _place()
        cps = [_remote(ins[q].at[2 * px + py], outs[q].at[k], send_sems, recv_sems, 3 * q + k, (px, py, c))
               for q in range(len(ps)) for k, (px, py) in enumerate([(1 - x, y), (x, 1 - y), (1 - x, 1 - y)])]
        return cps, cps

    return _comm_from(copies, ps, [jax.ShapeDtypeStruct((3,) + p.shape[1:], p.dtype) for p in ps], 3 * len(ps))


def _sum_chips(own, r, *, name, ts=256):
    k, n = own.shape
    ts = min(ts, k)

    def body(own_ref, r_ref, o_ref):
        f = lambda q: r_ref[q].astype(F32)
        o_ref[...] = ((own_ref[...].astype(F32) + f(0)) + f(1)) + f(2)

    return pl.pallas_call(
        body, name=name, grid=(k // ts,),
        in_specs=[pl.BlockSpec((ts, n), lambda i: (i, 0)), pl.BlockSpec((3, ts, n), lambda i: (0, i, 0))],
        out_specs=pl.BlockSpec((ts, n), lambda i: (i, 0)), out_shape=jax.ShapeDtypeStruct((k, n), F32),
        compiler_params=_params('arbitrary'))(own, r)


WIN_SHARD = N_IN // N_CHIPS
WIN_PAD = -(-WIN_SHARD // LANES) * LANES
GATE_WIRE_ROWS = 32


def _full_layer(sh, axis):
    _, k, n = sh.shape
    if axis == 2:
        return sh.transpose(1, 0, 2).reshape(k, N_CHIPS * n)
    return sh.reshape(N_CHIPS * k, n)


def _win_cols(wp, o, n):
    parts = []
    while n > 0:
        j, r = divmod(o, WIN_SHARD)
        take = min(n, WIN_SHARD - r)
        parts.append(wp[:, j * WIN_PAD + r:j * WIN_PAD + r + take])
        o, n = o + take, n - take
    return parts[0] if len(parts) == 1 else jnp.concatenate(parts, axis=1)


def _split_full(full, axis):
    k, n = full.shape
    if axis == 2:
        return jnp.stack([full[:, j * (n // N_CHIPS):(j + 1) * (n // N_CHIPS)] for j in range(N_CHIPS)])
    return full.reshape(N_CHIPS, k // N_CHIPS, n)


def _padc(a, w):
    return jnp.pad(a, ((0, 0), (0, w - a.shape[1])))


def _swap16(a):
    return jnp.concatenate([a[..., 16:32], a[..., 0:16]], axis=-1)


B_GR, B_GQ, B_GK, B_GV, B_MQ, B_MKR, B_MKRS, B_FF, B_GLOW, B_MKV, B_END = (
    0, 512, 768, 1024, 1536, 1792, 1920, 2048, 2176, 2304, 2432)
B_W = 2560
O_FQ, O_FF, O_GQ, O_GLOW, O_GR, O_MQ, O_MKV, O_MKR, O_ZG = 0, 768, 772, 1796, 1812, 2324, 2580, 2708, 2740


def _repack_layer_weights(w):
    wi = functools.partial(_win_cols, w['w_in'])
    out = dict(w)
    out['in_a'] = wi(O_FQ, 768)
    kr = wi(O_MKR, 32)
    out['in_b'] = jnp.concatenate([
        wi(O_GR, 512), wi(O_GQ, 1024), wi(O_MQ, 256), jnp.tile(kr, (1, MLA_HEADS)), jnp.tile(_swap16(kr), (1, MLA_HEADS)),
        _padc(wi(O_FF, 4), 128), _padc(wi(O_GLOW, 16), 128), wi(O_MKV, 128),
        jnp.zeros((D_MODEL, B_W - B_END), kr.dtype)], axis=1)
    out['in_c'] = wi(O_ZG, 3072)
    uq = w['w_mla_uq'].reshape(MLA_Q_RANK, MLA_HEADS, MLA_NOPE + MLA_ROPE)
    rope = uq[:, :, MLA_NOPE:]
    out['uq'] = jnp.concatenate([uq[:, :, :MLA_NOPE].reshape(MLA_Q_RANK, -1), rope.reshape(MLA_Q_RANK, -1),
                                 _swap16(rope).reshape(MLA_Q_RANK, -1)], axis=1)
    ukv = w['w_mla_ukv'].reshape(MLA_KV_RANK, MLA_HEADS, MLA_NOPE + MLA_VD)
    out['ukv'] = jnp.concatenate([ukv[:, :, :MLA_NOPE].reshape(MLA_KV_RANK, -1),
                                  ukv[:, :, MLA_NOPE:].reshape(MLA_KV_RANK, -1)], axis=1)
    out['gate'] = jnp.pad(w['w_gla_gate'], ((0, 128 - GLA_RANK), (0, 0)))
    return out


def _unpack_layer_grads(g):
    a, b, c = g['in_a'], g['in_b'], g['in_c']
    fold = lambda o: sum(b[:, o + MLA_ROPE * q:o + MLA_ROPE * (q + 1)] for q in range(MLA_HEADS))
    kr = fold(B_MKR) + _swap16(fold(B_MKRS))
    w_in = jnp.concatenate([a, b[:, B_FF:B_FF + 4], b[:, B_GQ:B_GQ + 1024], b[:, B_GLOW:B_GLOW + 16],
                            b[:, B_GR:B_GR + 512], b[:, B_MQ:B_MQ + 256], b[:, B_MKV:B_MKV + 128], kr, c], axis=1)
    uq = g['uq']
    nope = uq[:, :256].reshape(MLA_Q_RANK, MLA_HEADS, MLA_NOPE)
    rope = (uq[:, 256:384].reshape(MLA_Q_RANK, MLA_HEADS, MLA_ROPE)
            + _swap16(uq[:, 384:512].reshape(MLA_Q_RANK, MLA_HEADS, MLA_ROPE)))
    w_uq = jnp.concatenate([nope, rope], axis=2).reshape(MLA_Q_RANK, -1)
    ukv = g['ukv']
    w_ukv = jnp.concatenate([ukv[:, :256].reshape(MLA_KV_RANK, MLA_HEADS, MLA_NOPE),
                             ukv[:, 256:].reshape(MLA_KV_RANK, MLA_HEADS, MLA_VD)], axis=2).reshape(MLA_KV_RANK, -1)
    out = {'w_in': w_in, 'w_mla_uq': w_uq, 'w_mla_ukv': w_ukv, 'w_gla_gate': g['gate'][:GLA_RANK]}
    for nm in ('w_up_fox', 'w_up_gla', 'w_up_mla', 'w_out', 'w_xq', 'w_xkv', 'w_xo', 'w_mlp1', 'w_mlp2'):
        out[nm] = g[nm]
    return out


def _rope_tables(s):
    half = MLA_ROPE // 2
    inv = ROPE_BASE ** (-jnp.arange(half, dtype=F32) / half)
    ang = jnp.arange(s).astype(F32)[:, None] * inv[None, :]
    cos, sin = jnp.cos(ang), jnp.sin(ang)
    c1 = jnp.concatenate([cos, cos], axis=1)
    s1 = jnp.concatenate([-sin, sin], axis=1)
    return jnp.tile(c1, (1, MLA_HEADS)), jnp.tile(s1, (1, MLA_HEADS))


def _rms_bwd(x, dh, g):
    r = lax.rsqrt(jnp.mean(x * x, axis=-1, keepdims=True) + EPS)
    xh = x * r
    gd = dh * g
    return r * (gd - xh * jnp.mean(gd * xh, axis=-1, keepdims=True)), dh * xh


def _norm_bwd_call(x, dh, g, dres, name):
    w = x.width if isinstance(x, Cols) else x.shape[1]

    def with_res(xv, dv, rv, gv):
        dx, dg = _rms_bwd(xv, dv.astype(F32), gv)
        return rv + dx, dg

    def plain(xv, dv, gv):
        return _rms_bwd(xv, dv.astype(F32), gv)

    if dres is None:
        return _rowwise(plain, [x, dh], [g], [(w, F32)], [w], name=name)
    return _rowwise(with_res, [x, dh, dres], [g], [(w, F32)], [w], name=name)


def _gla_out_fwd(oraw, gr, g_out):
    outs = []
    for hh in range(GLA_HEADS):
        sl = slice(hh * GLA_DV, (hh + 1) * GLA_DV)
        oh = oraw[:, sl]
        n = oh * lax.rsqrt(jnp.mean(oh * oh, axis=-1, keepdims=True) + EPS) * g_out
        r = gr[:, sl]
        outs.append(n * (r * _sig(r)))
    return (jnp.concatenate(outs, axis=1),)


def _gla_out_bwd(oraw, gr, dout, g_out):
    d_o, d_r, dg = [], [], 0.0
    for hh in range(GLA_HEADS):
        sl = slice(hh * GLA_DV, (hh + 1) * GLA_DV)
        oh, r, do = oraw[:, sl], gr[:, sl], dout[:, sl].astype(F32)
        rs = lax.rsqrt(jnp.mean(oh * oh, axis=-1, keepdims=True) + EPS)
        sg = _sig(r)
        dn = do * (r * sg)
        d_r.append(do * (oh * rs * g_out) * (sg + r * sg * (1.0 - sg)))
        dx, dgh = _rms_bwd(oh, dn, g_out)
        d_o.append(dx)
        dg = dg + dgh
    return jnp.concatenate(d_o, axis=1), jnp.concatenate(d_r, axis=1), dg


def _adam(w, g, m, v):
    m = ADAM_B1 * m + (1.0 - ADAM_B1) * g
    v = ADAM_B2 * v + (1.0 - ADAM_B2) * (g * g)
    m_hat = m / (1.0 - ADAM_B1 ** ADAM_STEP)
    v_hat = v / (1.0 - ADAM_B2 ** ADAM_STEP)
    return -ADAM_LR * (m_hat / (jnp.sqrt(v_hat) + ADAM_EPS) + ADAM_WD * w), m, v


def _layer_fwd(x, mem, w, p, tabs, tag, carry_fox=None, after_fox=None, carry_mla=None):
    c4, s4 = tabs
    sv = {'x0': x}
    nm = lambda t: f'{t}_{tag}'
    za, h = _mm(x, w['in_a'], mode='nn', out_dtype=BF16, norm_g=p['g_mix'], emit_norm=True, name=nm('in_a'))
    zb = _mm(h, w['in_b'], mode='nn', out_dtype=F32, name=nm('in_b'))
    zc = _mm(h, w['in_c'], mode='nn', out_dtype=F32, name=nm('in_c'))
    sv.update(h=h, zc=zc)
    ff = Cols(zb, 128, B_FF // 128)
    (lf,) = _rowwise(lambda f, b: (_logsig(f + b),), [ff], [p['b_fox']], [(128, F32)], name=nm('fox_lf'))
    cum = _cumsum_rows(lf, reverse=False, name=nm('fox_cum'))
    ckf = jnp.pad(cum[:, :FOX_HEADS].T.reshape(2, 2, x.shape[0]), ((0, 0), (0, 6), (0, 0)))
    fox = dict(qc=0, kc=2, vc=4, nb=2, g=2, scale=FOX_HD ** -0.5, mode='causal', ck=ckf)
    o_fox, lse_fox, *carried = _mattn_fwd(za, za, za, name=nm('fox_attn'), comm=carry_fox, **fox)
    if after_fox is not None:
        w = {**w, **after_fox(carried[0])}
    sv.update(ff=ff, za=za, fox=fox, o_fox=o_fox, lse_fox=lse_fox)
    glow = Cols(zb, 128, B_GLOW // 128)
    gr = Cols(zb, 512, B_GR // 512)

    def gate_fn(gl, wg, bg):
        return (_logsig(_dot(gl.astype(BF16), wg) + bg) / GLA_TAU,)

    (la,) = _rowwise(gate_fn, [glow], [w['gate'], p['b_gla']], [(256, F32)], name=nm('gla_gate'))
    gla = dict(qc=B_GQ // LANES, kc=B_GK // LANES, vc=B_GV // LANES)
    oraw, states = _gla_fwd(zb, la, name=nm('gla'), **gla)
    (o_gla,) = _rowwise(_gla_out_fwd, [oraw, gr], [p['g_gla_out']], [(512, BF16)], name=nm('gla_out'))
    sv.update(glow=glow, gr=gr, zb=zb, la=la, gla=gla, states=states, oraw=oraw, o_gla=o_gla)
    mq = Cols(zb, 256, B_MQ // 256)
    mkv = Cols(zb, 128, B_MKV // 128)
    mkr2 = Cols(zb, 256, B_MKR // 256)
    qp, cqn = _mm(mq, w['uq'], mode='nn', out_dtype=F32, norm_g=p['g_mla_q'], emit_norm=True, name=nm('mla_uq'))
    kvp, ckvn = _mm(mkv, w['ukv'], mode='nn', out_dtype=BF16, norm_g=p['g_mla_kv'], emit_norm=True,
                    name=nm('mla_ukv'))

    def rope_fn(qv, kr, c4v, s4v):
        q_rope = qv[:, 256:384] * c4v + qv[:, 384:512] * s4v
        return jnp.concatenate([qv[:, 0:256], q_rope], axis=1), kr[:, 0:128] * c4v + kr[:, 128:256] * s4v

    qall, kr4 = _rowwise(rope_fn, [qp, mkr2, c4, s4], [], [(384, BF16), (128, BF16)], name=nm('rope'))
    mla = dict(qc=0, kc=0, vc=2, nb=2, g=2, scale=(MLA_NOPE + MLA_ROPE) ** -0.5, mode='chunk', qr=qall, qrc=2, kr=kr4)
    o_mla, lse_mla, *carried = _mattn_fwd(qall, kvp, kvp, name=nm('mla_attn'), comm=carry_mla, **mla)
    if carry_mla is not None:
        sv['carried_mla'] = carried[0]
    sv.update(mq=mq, mkv=mkv, cqn=cqn, ckvn=ckvn, qall=qall, kvp=kvp, mla=mla, o_mla=o_mla, lse_mla=lse_mla)
    of_m, om_m = o_fox, o_mla
    sv.update(of_m=of_m, om_m=om_m)
    b_br = p['b_branch']

    def first(acc, zg, bb):
        return _sig(zg + bb) * acc

    def more(acc, zg, bb, prev):
        return prev + _sig(zg + bb) * acc

    y = _mm(of_m, w['w_up_fox'], mode='nn', out_dtype=F32, name=nm('up_fox'), epilogue=first,
            extras=[(zc, *_mn(col_off=0)), (b_br, *_nvec(col_off=0))])
    y = _mm(o_gla, w['w_up_gla'], mode='nn', out_dtype=F32, name=nm('up_gla'), epilogue=more,
            extras=[(zc, *_mn(col_off=1024)), (b_br, *_nvec(col_off=1024)), (y, *_mn())])
    y = _mm(om_m, w['w_up_mla'], mode='nn', out_dtype=BF16, name=nm('up_mla'), epilogue=more,
            extras=[(zc, *_mn(col_off=2048)), (b_br, *_nvec(col_off=2048)), (y, *_mn())])
    add = lambda acc, res: res + acc
    x1 = _mm(y, w['w_out'], mode='nn', out_dtype=F32, name=nm('out'), epilogue=add, extras=[(x, *_mn())])
    sv.update(y=y, x1=x1)
    qx, hx = _mm(x1, w['w_xq'], mode='nn', out_dtype=BF16, norm_g=p['g_xa'], emit_norm=True, name=nm('xq'))
    kvx, mn = _mm(mem, w['w_xkv'], mode='nn', out_dtype=BF16, norm_g=p['g_mem'], emit_norm=True, name=nm('xkv'))
    xa = dict(qc=0, kc=0, vc=4, nb=4, g=1, scale=XA_HD ** -0.5, mode='full')
    ox_m, lse_x = _mattn_fwd(qx, kvx, kvx, name=nm('xa_attn'), **xa)
    x2 = _mm(ox_m, w['w_xo'], mode='nn', out_dtype=F32, name=nm('xo'), epilogue=add, extras=[(x1, *_mn())])
    sv.update(hx=hx, mn=mn, qx=qx, kvx=kvx, xa=xa, lse_x=lse_x, ox_m=ox_m, x2=x2)
    hpre, hm = _mm(x2, w['w_mlp1'], mode='nn', out_dtype=BF16, norm_g=p['g_mlp'], emit_norm=True, name=nm('mlp1'))
    relu2 = lambda t: jnp.square(jnp.maximum(t.astype(F32), 0.0))
    x3 = _mm(hpre, w['w_mlp2'], mode='nn', out_dtype=F32, name=nm('mlp2'), a_fn=relu2, epilogue=add,
             extras=[(x2, *_mn())])
    sv.update(hpre=hpre, hm=hm, w=w)
    return x3, sv


EARLY = ('w_mlp1', 'w_mlp2', 'w_xo', 'w_xq', 'w_xkv', 'w_out', 'w_up_fox', 'w_up_gla', 'w_up_mla')
LATE = ('w_in', 'w_gla_gate', 'w_mla_uq', 'w_mla_ukv')


def _layer_bwd(dx3, mem, w, p, tabs, sv, tag, carry_mla=None, early=None):
    c4, s4 = tabs
    nm = lambda t: f'{t}_{tag}'
    s = dx3.shape[0]
    gw, gs = {}, {}
    relu2 = lambda t: jnp.square(jnp.maximum(t.astype(F32), 0.0))
    gw['w_mlp2'] = _mm(sv['hpre'], dx3, mode='tn', out_dtype=F32, name=nm('d_mlp2'), a_fn=relu2)
    dact = lambda acc, hp: acc * (2.0 * jnp.maximum(hp.astype(F32), 0.0))
    dhpre = _mm(dx3, w['w_mlp2'], mode='nt', out_dtype=BF16, name=nm('d_act'), epilogue=dact,
                extras=[(sv['hpre'], *_mn())])
    gw['w_mlp1'] = _mm(sv['hm'], dhpre, mode='tn', out_dtype=F32, name=nm('d_mlp1'))
    dhm = _mm(dhpre, w['w_mlp1'], mode='nt', out_dtype=F32, name=nm('d_hm'))
    dx2, gs['g_mlp'] = _norm_bwd_call(sv['x2'], dhm, p['g_mlp'], dx3, nm('d_norm_mlp'))
    gw['w_xo'] = _mm(sv['ox_m'], dx2, mode='tn', out_dtype=F32, name=nm('d_xo'))
    dox = _mm(dx2, w['w_xo'], mode='nt', out_dtype=BF16, name=nm('d_ox'))
    dqx_m, dkx, dvx = _mattn_bwd(sv['qx'], sv['kvx'], sv['kvx'], sv['ox_m'], dox, sv['lse_x'], name=nm('xa_bwd'),
                                 **sv['xa'])
    dkvx = jnp.concatenate([dkx, dvx], axis=1).astype(BF16)
    gw['w_xq'] = _mm(sv['hx'], dqx_m, mode='tn', out_dtype=F32, name=nm('d_xq'))
    dhx = _mm(dqx_m, w['w_xq'], mode='nt', out_dtype=F32, name=nm('d_hx'))
    gw['w_xkv'] = _mm(sv['mn'], dkvx, mode='tn', out_dtype=F32, name=nm('d_xkv'))
    dmn = _mm(dkvx, w['w_xkv'], mode='nt', out_dtype=F32, name=nm('d_mn'))
    _, gs['g_mem'] = _norm_bwd_call(mem, dmn, p['g_mem'], None, nm('d_norm_mem'))
    dx1, gs['g_xa'] = _norm_bwd_call(sv['x1'], dhx, p['g_xa'], dx2, nm('d_norm_xa'))
    gw['w_out'] = _mm(sv['y'], dx1, mode='tn', out_dtype=F32, name=nm('d_out'))
    dy = _mm(dx1, w['w_out'], mode='nt', out_dtype=BF16, name=nm('d_y'))
    zc, b_br = sv['zc'], p['b_branch']

    def du_fn(dyv, zg, bb):
        g = _sig(zg + bb)
        d = dyv.astype(F32)
        return d * g[:, 0:1024], d * g[:, 1024:2048], d * g[:, 2048:3072]

    du = _rowwise(du_fn, [dy, zc], [b_br], [(D_MODEL, BF16)] * 3, name=nm('d_u'))

    def dgate(acc, dyv, zg, bb):
        g = _sig(zg + bb)
        return dyv.astype(F32) * acc * g * (1.0 - g)

    dzc, do_br = [], []
    for q, (o_m, wn) in enumerate(((sv['of_m'], 'w_up_fox'), (sv['o_gla'], 'w_up_gla'), (sv['om_m'], 'w_up_mla'))):
        dzc.append(_mm(o_m, w[wn], mode='nn', out_dtype=F32, name=nm(f'd_zg{q}'), epilogue=dgate,
                       extras=[(dy, *_mn()), (zc, *_mn(col_off=1024 * q)), (b_br, *_nvec(col_off=1024 * q))]))
        gw[wn] = _mm(o_m, du[q], mode='tn', out_dtype=F32, name=nm(f'd_up{q}'))
        do_br.append(_mm(du[q], w[wn], mode='nt', out_dtype=F32 if q == 1 else BF16, name=nm(f'd_o{q}')))
    dzc = jnp.concatenate(dzc, axis=1)
    (gs['b_branch'],) = _rowwise(lambda t: (t,), [dzc], [], [], [3072], name=nm('d_bbranch'))
    za = sv['za']
    carry_fox = None if early is None else early({nm_: gw[nm_] for nm_ in EARLY})
    dfq, dfk, dfv, dck, dcq, *carried_fox = _mattn_bwd(za, za, za, sv['o_fox'], do_br[0], sv['lse_fox'],
                                                       name=nm('fox_bwd'), comm=carry_fox, **sv['fox'])
    dcum = _padc(dck[:, :2, :].reshape(FOX_HEADS, s).T + dcq.reshape(s, 2, LANES)[:, :, :2].reshape(s, FOX_HEADS), 128)
    dlf = _cumsum_rows(dcum, reverse=True, name=nm('fox_dcum'))

    def dff_fn(dl, f, b):
        d = dl * _sig(-(f + b))
        return d, d

    dff, db_fox = _rowwise(dff_fn, [dlf, sv['ff']], [p['b_fox']], [(128, F32)], [128], name=nm('fox_dff'))
    gs['b_fox'] = db_fox
    dza = jnp.concatenate([dfq, dfk, dfv], axis=1).astype(BF16)
    dqn, dkn, dvv, dq_rope, dk_rope, *carried_mla = _mattn_bwd(sv['qall'], sv['kvp'], sv['kvp'], sv['o_mla'], do_br[2],
                                                               sv['lse_mla'], name=nm('mla_bwd'), comm=carry_mla,
                                                               **sv['mla'])

    def drope_fn(dn, dq, dk, c4v, s4v):
        return jnp.concatenate([dn, dq * c4v, dq * s4v], axis=1), jnp.concatenate([dk * c4v, dk * s4v], axis=1)

    dqp, dmkr2 = _rowwise(drope_fn, [dqn, dq_rope, dk_rope, c4, s4], [], [(512, BF16), (256, BF16)], name=nm('d_rope'))
    dkvp = jnp.concatenate([dkn, dvv], axis=1).astype(BF16)
    gw['uq'] = _mm(sv['cqn'], dqp, mode='tn', out_dtype=F32, name=nm('d_uq'))
    dcqn = _mm(dqp, w['uq'], mode='nt', out_dtype=F32, name=nm('d_cqn'))
    gw['ukv'] = _mm(sv['ckvn'], dkvp, mode='tn', out_dtype=F32, name=nm('d_ukv'))
    dckvn = _mm(dkvp, w['ukv'], mode='nt', out_dtype=F32, name=nm('d_ckvn'))
    dmq, gs['g_mla_q'] = _norm_bwd_call(sv['mq'], dcqn, p['g_mla_q'], None, nm('d_norm_q'))
    dmkv, gs['g_mla_kv'] = _norm_bwd_call(sv['mkv'], dckvn, p['g_mla_kv'], None, nm('d_norm_kv'))
    doraw, dgr, gs['g_gla_out'] = _rowwise(_gla_out_bwd, [sv['oraw'], sv['gr'], do_br[1]], [p['g_gla_out']],
                                           [(512, F32), (512, BF16)], [128], name=nm('d_gla_out'))
    st = sv['states']
    st_prev = jnp.concatenate([jnp.zeros_like(st[:, :1]), st[:, :-1]], axis=1)
    dgq, dgk, dgv, dla = _gla_bwd(sv['zb'], sv['la'], st, st_prev, doraw, name=nm('gla_bwd'), **sv['gla'])

    def dgate_fn(dl, gl, wg, bg):
        pre = _dot(gl.astype(BF16), wg) + bg
        dpre = dl * (1.0 / GLA_TAU) * _sig(-pre)
        return dpre, _dot(dpre.astype(BF16), wg, NT), dpre

    dpre, dglow, gs['b_gla'] = _rowwise(dgate_fn, [dla, sv['glow']], [w['gate'], p['b_gla']],
                                        [(256, BF16), (128, BF16)], [256], name=nm('d_gla_gate'))
    gw['gate'] = _mm(sv['glow'], dpre, mode='tn', out_dtype=F32, name=nm('d_wgate'))
    bf = lambda t: t.astype(BF16)
    dzb = jnp.concatenate([dgr, bf(dgq), bf(dgk), bf(dgv), bf(dmq), dmkr2, bf(dff), dglow, bf(dmkv),
                           jnp.zeros((s, B_W - B_END), BF16)], axis=1)
    h = sv['h']
    gw['in_a'] = _mm(h, dza, mode='tn', out_dtype=F32, name=nm('d_in_a'))
    gw['in_b'] = _mm(h, dzb, mode='tn', out_dtype=F32, name=nm('d_in_b'))
    gw['in_c'] = _mm(h, dzc, mode='tn', out_dtype=F32, name=nm('d_in_c'))
    add = lambda acc, prev: prev + acc
    dh = _mm(dza, w['in_a'], mode='nt', out_dtype=F32, name=nm('d_h_a'))
    dh = _mm(dzb, w['in_b'], mode='nt', out_dtype=F32, name=nm('d_h_b'), epilogue=add, extras=[(dh, *_mn())])
    dh = _mm(dzc, w['in_c'], mode='nt', out_dtype=F32, name=nm('d_h_c'), epilogue=add, extras=[(dh, *_mn())])
    dx0, gs['g_mix'] = _norm_bwd_call(sv['x0'], dh, p['g_mix'], dx1, nm('d_norm_mix'))
    return dx0, gw, gs, (carried_mla or [None])[0], (carried_fox or [None])[0]


def _loss_head(x, target, g_final):
    d = x.shape[1]

    def fn(xv, tv, gv):
        r = lax.rsqrt(jnp.mean(xv * xv, axis=-1, keepdims=True) + EPS)
        xh = xv * r
        e = xh * gv - tv
        dy = e * (1.0 / d)
        gd = dy * gv
        dx = r * (gd - xh * jnp.mean(gd * xh, axis=-1, keepdims=True))
        row_loss = 0.5 * jnp.mean(e * e, axis=-1, keepdims=True)
        return dx, dy * xh, jnp.broadcast_to(row_loss, (xv.shape[0], LANES))

    return _rowwise(fn, [x, target], [g_final], [(d, F32)], [d, LANES], name='loss_head')


def _small_sizes(shapes):
    return [math.prod(shapes[nm]) for nm in SMALL]


def _step(args):
    shapes = {nm: args[nm].shape for nm in ORDER}
    x, mem, target = args['x'][0], args['mem'][0], args['loss_target'][0]
    s = x.shape[0]

    def wire(nm, l):
        w = args[nm][l].astype(BF16)
        if nm == 'w_in':
            w = jnp.pad(w, ((0, 0), (0, WIN_PAD - WIN_SHARD)))
        if nm == 'w_gla_gate':
            w = jnp.pad(w, ((0, GATE_WIRE_ROWS - GLA_RANK), (0, 0)))
        return w

    axis_of = dict(BIG)
    names = tuple(nm for nm, _ in BIG)
    wires = lambda l, nms: [wire(nm, l) for nm in nms]
    width = lambda nm: WIN_PAD if nm == 'w_in' else args[nm].shape[2]
    side_by_side = lambda nms: [axis_of[nm] == 2 and width(nm) % LANES == 0 for nm in nms]
    over_ici = lambda l, nms: _gather_over_ici(wires(l, nms), side_by_side(nms))

    def whole(parts, nms, tag):
        side = side_by_side(nms)
        parts = _run_comm(_gather_over_d2d(parts, side), name=f'gather_d2d_{tag}', alias=True)
        full = {nm: p if sd else _full_layer(p, axis_of[nm]) for nm, p, sd in zip(nms, parts, side)}
        if 'w_gla_gate' in full:
            full['w_gla_gate'] = full['w_gla_gate'][:GLA_RANK]
        return full

    tabs = _rope_tables(s)
    layers_p = []
    for l in range(DEPTH):
        layers_p.append({
            'g_mix': args['g_mix'][l][None], 'b_fox': _padc(args['b_fox_forget'][l][None], 128),
            'b_gla': args['b_gla_gate'][l][None], 'g_gla_out': args['g_gla_out'][l][None],
            'g_mla_q': args['g_mla_q'][l][None], 'g_mla_kv': args['g_mla_kv'][l][None],
            'b_branch': args['b_branch_gate'][l][None], 'g_xa': args['g_xa'][l][None],
            'g_mem': args['g_mem'][l][None], 'g_mlp': args['g_mlp'][l][None]})

    first = _run_comm(over_ici(0, LATE), name='gather_ici_first_l0')
    w_now = _repack_layer_weights(whole(first, LATE, 'first_l0'))
    saved = []
    xl = x
    for l in range(DEPTH):
        carry_fox = over_ici(0, EARLY) if l == 0 else None
        after_fox = (lambda parts: whole(parts, EARLY, 'rest_l0')) if l == 0 else None
        carry_mla = over_ici(l + 1, names) if l + 1 < DEPTH else None
        xl, sv = _layer_fwd(xl, mem, w_now, layers_p[l], tabs, f'l{l}', carry_fox=carry_fox, after_fox=after_fox,
                            carry_mla=carry_mla)
        saved.append(sv)
        if carry_mla is not None:
            w_now = _repack_layer_weights(whole(sv.pop('carried_mla'), names, f'l{l + 1}'))
    dx, dg_final, loss_lanes = _loss_head(xl, target, args['g_final'][None])
    cidx = lax.axis_index('c')
    chip = 2 * lax.axis_index('x') + lax.axis_index('y')

    def pair_sums(gw, nms, tag):
        mine, theirs = [], []
        for nm in nms:
            shards = _split_full(gw[nm], axis_of[nm]).astype(BF16)
            h = shards.shape[1] // 2
            mine.append(lax.dynamic_slice_in_dim(shards, cidx * h, h, axis=1))
            theirs.append(lax.dynamic_slice_in_dim(shards, (1 - cidx) * h, h, axis=1))
        got = _to_sibling(theirs, name=f'grads_swap_{tag}')
        pairs = []
        for nm, a, b in zip(nms, mine, got):
            _, h, n = a.shape
            (p,) = _rowwise(lambda u, v: (u.astype(F32) + v.astype(F32),),
                            [a.reshape(N_CHIPS * h, n), b.reshape(N_CHIPS * h, n)], [], [(n, BF16)],
                            name=f'pair_sum_{nm}_{tag}')
            pairs.append(p.reshape(N_CHIPS, h, n))
        return pairs

    def finish(pairs, from_chips, nms, tag):
        own = [lax.dynamic_index_in_dim(p, chip, axis=0, keepdims=False) for p in pairs]
        mine = [_sum_chips(o, r, name=f'chip_sum_{nm}_{tag}') for nm, o, r in zip(nms, own, from_chips)]
        theirs = _to_sibling(mine, name=f'grads_join_{tag}')
        return {nm: jnp.where(cidx == 0, jnp.concatenate([a, b]), jnp.concatenate([b, a]))
                for nm, a, b in zip(nms, mine, theirs)}

    gs_layers, done = [None] * DEPTH, [{} for _ in range(DEPTH)]
    above = None
    for l in reversed(range(DEPTH)):
        lowest, early_pairs = l == 0, []

        def early(gw_early, l=l, early_pairs=early_pairs):
            early_pairs.extend(pair_sums(gw_early, EARLY, f'early_l{l}'))
            return _chip_exchange(early_pairs)

        carry_mla = None if above is None else _chip_exchange(above[1])
        dx, gw, gs_layers[l], got_mla, got_fox = _layer_bwd(
            dx, mem, saved[l]['w'], layers_p[l], tabs, saved[l], f'l{l}', carry_mla=carry_mla,
            early=early if lowest else None)
        if above is not None:
            done[above[0]].update(finish(above[1], got_mla, names, f'l{above[0]}'))
        grads = _unpack_layer_grads(gw)
        if lowest:
            done[l].update(finish(early_pairs, got_fox, EARLY, f'early_l{l}'))
            late_pairs = pair_sums(grads, LATE, f'late_l{l}')
            from_late = _run_comm(_chip_exchange(late_pairs), name=f'grads_exchange_late_l{l}')
            done[l].update(finish(late_pairs, from_late, LATE, f'late_l{l}'))
        else:
            above = (l, pair_sums(grads, names, f'l{l}'))
    grad_x = dx[None]
    gshard = {nm: jnp.stack([done[l][nm] for l in range(DEPTH)]) for nm in names}

    small_g = []
    for nm, key in (('g_mix', 'g_mix'), ('b_fox_forget', 'b_fox'), ('b_gla_gate', 'b_gla'),
                    ('g_gla_out', 'g_gla_out'), ('g_mla_q', 'g_mla_q'), ('g_mla_kv', 'g_mla_kv'),
                    ('b_branch_gate', 'b_branch'), ('g_xa', 'g_xa'), ('g_mem', 'g_mem'), ('g_mlp', 'g_mlp')):
        width = shapes[nm][1]
        small_g.append(jnp.concatenate([gs_layers[l][key][0, :width] for l in range(DEPTH)]))
    small_g.append(dg_final[0])
    small_g.append(loss_lanes[0, :1])
    flat = jnp.concatenate(small_g)
    n_small = flat.shape[0]
    srows = -(-n_small // (8 * LANES)) * 8
    pad = lambda v: jnp.pad(v, (0, srows * LANES - v.shape[0])).reshape(srows, LANES)
    all_small = _all_gather8(pad(flat), name='gather_small')
    sw, sm, svv = (pad(jnp.concatenate([args[pre + nm].reshape(-1) for nm in SMALL] + [jnp.zeros((1,), F32)]))
                   for pre in ('', 'm_', 'v_'))

    def small_body(g_ref, w_ref, m_ref, v_ref, go_ref, d_ref, mo_ref, vo_ref):
        g = g_ref[0]
        for q in range(1, N_DEV):
            g = g + g_ref[q]
        go_ref[...] = g
        d_ref[...], mo_ref[...], vo_ref[...] = _adam(w_ref[...], g, m_ref[...], v_ref[...])

    sg, sd, snm, snv = pl.pallas_call(
        small_body, name='small_sum_adam', out_shape=[jax.ShapeDtypeStruct((srows, LANES), F32)] * 4,
        compiler_params=pltpu.CompilerParams(vmem_limit_bytes=VMEM_LIMIT))(all_small, sw, sm, svv)

    def unsmall(buf):
        v, out, off = buf.reshape(-1), {}, 0
        for nm in SMALL:
            nel = math.prod(shapes[nm])
            out[nm] = v[off:off + nel].reshape(shapes[nm])
            off += nel
        return out, v[off]

    res = {}
    (res['grad'], loss), (res['delta'], _), (res['m'], _), (res['v'], _) = (unsmall(t) for t in (sg, sd, snm, snv))

    for nm, _ in BIG:
        shp = args[nm].shape
        view = lambda t: t.reshape(shp[0] * shp[1], shp[2])
        d, m2, v2 = _rowwise(_adam, [view(args[nm]), view(gshard[nm]), view(args['m_' + nm]), view(args['v_' + nm])],
                             [], [(shp[2], F32)] * 3, name=f'adam_{nm}')
        res['grad'][nm], res['delta'][nm], res['m'][nm], res['v'][nm] = (
            gshard[nm], d.reshape(shp), m2.reshape(shp), v2.reshape(shp))

    return (loss, grad_x, *[res['grad'][nm] for nm in ORDER], *[res['delta'][nm] for nm in ORDER],
            *[res['m'][nm] for nm in ORDER], *[res['v'][nm] for nm in ORDER])


def kernel(x, mem, g_mix, w_in, b_fox_forget, w_gla_gate, b_gla_gate, g_gla_out, g_mla_q, w_mla_uq, g_mla_kv, w_mla_ukv, b_branch_gate, w_up_fox, w_up_gla, w_up_mla, w_out, g_xa, g_mem, w_xq, w_xkv, w_xo, g_mlp, w_mlp1, w_mlp2, g_final, loss_target, m_g_mix, m_w_in, m_b_fox_forget, m_w_gla_gate, m_b_gla_gate, m_g_gla_out, m_g_mla_q, m_w_mla_uq, m_g_mla_kv, m_w_mla_ukv, m_b_branch_gate, m_w_up_fox, m_w_up_gla, m_w_up_mla, m_w_out, m_g_xa, m_g_mem, m_w_xq, m_w_xkv, m_w_xo, m_g_mlp, m_w_mlp1, m_w_mlp2, m_g_final, v_g_mix, v_w_in, v_b_fox_forget, v_w_gla_gate, v_b_gla_gate, v_g_gla_out, v_g_mla_q, v_w_mla_uq, v_g_mla_kv, v_w_mla_ukv, v_b_branch_gate, v_w_up_fox, v_w_up_gla, v_w_up_mla, v_w_out, v_g_xa, v_g_mem, v_w_xq, v_w_xkv, v_w_xo, v_g_mlp, v_w_mlp1, v_w_mlp2, v_g_final):
    return _step(dict(locals()))
```

```python
import functools
import math
import typing

import jax
import jax.numpy as jnp
from jax import lax
from jax.experimental import pallas as pl
from jax.experimental.pallas import tpu as pltpu

F32 = jnp.float32
BF16 = jnp.bfloat16
MESH = pl.DeviceIdType.MESH

D_MODEL = 1024
DEPTH = 2
CHUNK = 64
EPS = 1e-6
FOX_HEADS, FOX_HD = 4, 64
GLA_HEADS, GLA_DK, GLA_DV, GLA_RANK, GLA_TAU = 4, 64, 128, 16, 16.0
MLA_HEADS, MLA_Q_RANK, MLA_KV_RANK, MLA_NOPE, MLA_ROPE, MLA_VD = 4, 256, 128, 64, 32, 64
ROPE_BASE = 10000.0
XA_HEADS, XA_HD = 4, 128
D_FF = 4 * D_MODEL
IN_SIZES = (256, 256, 256, 4, 256, 256, 512, 16, 512, 256, 128, 32, 3072)
N_IN = sum(IN_SIZES)

ADAM_LR, ADAM_B1, ADAM_B2, ADAM_EPS, ADAM_WD, ADAM_STEP = 0.001, 0.9, 0.999, 1e-08, 0.01, 10

N_CHIPS = 4
N_DEV = 8
LANES = 128
VMEM_LIMIT = 48 * 1024 * 1024
MASK_VALUE = -1e30

BIG = (('w_in', 2), ('w_gla_gate', 2), ('w_mla_uq', 2), ('w_mla_ukv', 2), ('w_up_fox', 2), ('w_up_gla', 2),
       ('w_up_mla', 2), ('w_out', 1), ('w_xq', 1), ('w_xkv', 1), ('w_xo', 2), ('w_mlp1', 2), ('w_mlp2', 1))
SMALL = ('g_mix', 'b_fox_forget', 'b_gla_gate', 'g_gla_out', 'g_mla_q', 'g_mla_kv', 'b_branch_gate',
         'g_xa', 'g_mem', 'g_mlp', 'g_final')
ORDER = ('g_mix', 'w_in', 'b_fox_forget', 'w_gla_gate', 'b_gla_gate', 'g_gla_out', 'g_mla_q', 'w_mla_uq',
         'g_mla_kv', 'w_mla_ukv', 'b_branch_gate', 'w_up_fox', 'w_up_gla', 'w_up_mla', 'w_out', 'g_xa', 'g_mem',
         'w_xq', 'w_xkv', 'w_xo', 'g_mlp', 'w_mlp1', 'w_mlp2', 'g_final')


def _params(*sem):
    return pltpu.CompilerParams(dimension_semantics=sem, vmem_limit_bytes=VMEM_LIMIT)


def _sig(x):
    return 1.0 / (1.0 + jnp.exp(-x))


def _logsig(x):
    return jnp.minimum(x, 0.0) - jnp.log(1.0 + jnp.exp(-jnp.abs(x)))


NN = (((1,), (0,)), ((), ()))
NT = (((1,), (1,)), ((), ()))
TN = (((0,), (0,)), ((), ()))


def _dot(a, b, dims=NN):
    return lax.dot_general(a, b, dims, preferred_element_type=F32)


class Cols(typing.NamedTuple):
    arr: jax.Array
    width: int
    blk: int


def _tri_dot(tri, x):
    hi = x.astype(BF16)
    r1 = x - hi.astype(F32)
    mid = r1.astype(BF16)
    lo = (r1 - mid.astype(F32)).astype(BF16)
    return _dot(tri, hi) + _dot(tri, mid) + _dot(tri, lo)


MM_TILES = ((1024, 1024), (1024, 512), (512, 1024), (512, 512), (512, 256), (256, 512), (256, 256), (128, 128))
MM_VMEM_BUDGET = 38 * 1024 * 1024


def _mm_tiles(m, n, k, a_bytes, b_bytes, out_bytes, ex_bytes, has_norm, emit_norm, has_fn):
    for tm, tn in MM_TILES:
        tm, tn = min(tm, m), min(tn, n)
        if m % tm or n % tn:
            continue
        blocks = tm * k * a_bytes + k * tn * b_bytes + tm * tn * (out_bytes + ex_bytes) + (tm * k * 2 if emit_norm else 0)
        temps = tm * tn * 4 + (tm * k * 2 if has_norm else 0) + (tm * k * 6 if has_fn or has_norm else 0)
        if 2 * blocks + temps <= MM_VMEM_BUDGET:
            return tm, tn
    raise ValueError((m, n, k))


def _mm(a, b, *, mode, out_dtype, name, norm_g=None, emit_norm=False, a_fn=None, extras=(), epilogue=None):
    a_blk = 0
    if isinstance(a, Cols):
        a, width, a_blk = a
        a_shape = (a.shape[0], width)
    else:
        a_shape = a.shape
    if mode == 'tn':
        k, m = a_shape
    else:
        m, k = a_shape
    n = b.shape[0] if mode == 'nt' else b.shape[1]
    assert (b.shape[1] if mode == 'nt' else b.shape[0]) == k, (name, a.shape, b.shape)
    has_norm = norm_g is not None
    ex_bytes = sum(arr.dtype.itemsize for arr, kind, _ in extras if kind == 'mn')
    tm, tn = _mm_tiles(m, n, k, a.dtype.itemsize, b.dtype.itemsize, jnp.dtype(out_dtype).itemsize, ex_bytes, has_norm,
                       emit_norm, a_fn is not None)
    assert all(col % tn == 0 for _, _, col in extras), (name, tn)
    assert a_blk == 0 or (mode == 'nn') or (mode == 'tn' and tm == m)
    if mode == 'tn':
        a_spec = pl.BlockSpec((k, tm), lambda i, j: (0, i + a_blk))
    else:
        a_spec = pl.BlockSpec((tm, k), lambda i, j: (i, a_blk))
    b_spec = pl.BlockSpec((tn, k), lambda i, j: (j, 0)) if mode == 'nt' else pl.BlockSpec((k, tn), lambda i, j: (0, j))
    dims = {'nn': NN, 'nt': NT, 'tn': TN}[mode]
    assert not (has_norm and mode != 'nn')
    n_ex = len(extras)

    def body(*refs):
        a_ref, b_ref = refs[0], refs[1]
        pos = 2
        g_ref = None
        if has_norm:
            g_ref = refs[pos]
            pos += 1
        ex_refs = refs[pos:pos + n_ex]
        pos += n_ex
        o_ref = refs[pos]
        pos += 1
        h_ref = None
        if emit_norm:
            h_ref = refs[pos]
            pos += 1
        if has_norm:
            an_ref = refs[pos]

            @pl.when(pl.program_id(1) == 0)
            def _():
                xf = a_ref[...].astype(F32)
                y = xf * lax.rsqrt(jnp.mean(xf * xf, axis=-1, keepdims=True) + EPS) * g_ref[...]
                an_ref[...] = y.astype(BF16)
                if emit_norm:
                    h_ref[...] = y.astype(BF16)

            av = an_ref[...]
        else:
            av = a_ref[...]
            if a_fn is not None:
                av = a_fn(av)
            av = av.astype(BF16)
        acc = _dot(av, b_ref[...].astype(BF16), dims)
        if epilogue is not None:
            acc = epilogue(acc, *[r[...] for r in ex_refs])
        o_ref[...] = acc.astype(out_dtype)

    in_specs = [a_spec, b_spec]
    args = [a, b]
    if has_norm:
        in_specs.append(pl.BlockSpec((1, k), lambda i, j: (0, 0)))
        args.append(norm_g)
    for arr, kind, col in extras:
        if kind == 'mn':
            in_specs.append(pl.BlockSpec((tm, tn), lambda i, j, o=col // tn: (i, j + o)))
        else:
            in_specs.append(pl.BlockSpec((1, tn), lambda i, j, o=col // tn: (0, j + o)))
        args.append(arr)
    out_shape = [jax.ShapeDtypeStruct((m, n), out_dtype)]
    out_specs = [pl.BlockSpec((tm, tn), lambda i, j: (i, j))]
    if emit_norm:
        out_shape.append(jax.ShapeDtypeStruct((m, k), BF16))
        out_specs.append(pl.BlockSpec((tm, k), lambda i, j: (i, 0)))
    scratch = [pltpu.VMEM((tm, k), BF16)] if has_norm else []
    res = pl.pallas_call(
        body, name=name, grid=(m // tm, n // tn), in_specs=in_specs, out_specs=out_specs, out_shape=out_shape,
        scratch_shapes=scratch, compiler_params=_params('arbitrary', 'arbitrary'))(*args)
    return res if emit_norm else res[0]


def _mn(col_off=0):
    return 'mn', col_off


def _nvec(col_off=0):
    return 'n', col_off


def _rowwise(fn, rows, consts, outs, sums=(), *, name, ts=256):
    views = [x if isinstance(x, Cols) else Cols(x, x.shape[1], 0) for x in rows]
    rows = [v.arr for v in views]
    r = rows[0].shape[0]
    ts = min(ts, r)
    assert r % ts == 0, (name, r, ts)
    nr, nc, no, ns = len(rows), len(consts), len(outs), len(sums)

    def body(*refs):
        vals = fn(*[x[...] for x in refs[:nr + nc]])
        for q in range(no):
            refs[nr + nc + q][...] = vals[q].astype(outs[q][1])
        if ns:
            @pl.when(pl.program_id(0) == 0)
            def _():
                for q in range(ns):
                    refs[nr + nc + no + q][...] = jnp.zeros((1, sums[q]), F32)

            for q in range(ns):
                refs[nr + nc + no + q][...] += jnp.sum(vals[no + q].astype(F32), axis=0, keepdims=True)

    in_specs = [pl.BlockSpec((ts, v.width), lambda i, blk=v.blk: (i, blk)) for v in views]
    in_specs += [pl.BlockSpec(x.shape, lambda i, nd=x.ndim: (0,) * nd) for x in consts]
    out_specs = [pl.BlockSpec((ts, w), lambda i: (i, 0)) for w, _ in outs]
    out_specs += [pl.BlockSpec((1, w), lambda i: (0, 0)) for w in sums]
    out_shape = [jax.ShapeDtypeStruct((r, w), dt) for w, dt in outs]
    out_shape += [jax.ShapeDtypeStruct((1, w), F32) for w in sums]
    return pl.pallas_call(body, name=name, grid=(r // ts,), in_specs=in_specs, out_specs=out_specs,
                          out_shape=out_shape, compiler_params=_params('arbitrary'))(*rows, *consts)


def _cumsum_rows(x, *, reverse, name, bs=256):
    s, w = x.shape
    bs = min(bs, s)
    nb = s // bs

    def body(x_ref, o_ref, carry):
        @pl.when(pl.program_id(0) == 0)
        def _():
            carry[...] = jnp.zeros_like(carry)

        r = lax.broadcasted_iota(jnp.int32, (bs, bs), 0)
        c = lax.broadcasted_iota(jnp.int32, (bs, bs), 1)
        tri = jnp.where((c >= r) if reverse else (c <= r), 1.0, 0.0).astype(BF16)
        xv = x_ref[...]
        o_ref[...] = _tri_dot(tri, xv) + carry[...]
        carry[...] += jnp.sum(xv, axis=0, keepdims=True)

    imap = (lambda i: (nb - 1 - i, 0)) if reverse else (lambda i: (i, 0))
    return pl.pallas_call(body, name=name, grid=(nb,), in_specs=[pl.BlockSpec((bs, w), imap)],
                          out_specs=pl.BlockSpec((bs, w), imap), out_shape=jax.ShapeDtypeStruct((s, w), F32),
                          scratch_shapes=[pltpu.VMEM((1, w), F32)], compiler_params=_params('arbitrary'))(x)


def _mask(mode, q0, k0, bq, bk):
    qpos = q0 + lax.broadcasted_iota(jnp.int32, (bq, bk), 0)
    kpos = k0 + lax.broadcasted_iota(jnp.int32, (bq, bk), 1)
    if mode == 'causal':
        return kpos <= qpos
    return kpos < (jnp.right_shift(qpos, int(math.log2(CHUNK))) + 1) * CHUNK


ROPE_SHIFT = int(math.log2(MLA_ROPE))
FOX_SCALE, MLA_SCALE, XA_SCALE = FOX_HD ** -0.5, (MLA_NOPE + MLA_ROPE) ** -0.5, XA_HD ** -0.5
ATTN_ROW_SLAB = 512


def _lane_masks(g, b, rope):
    lane = lax.broadcasted_iota(jnp.int32, (1, LANES), 1)
    heads = [None if g == 1 else (lane >= hh * (LANES // g)) & (lane < (hh + 1) * (LANES // g)) for hh in range(g)]
    ropes = [jnp.right_shift(lane, ROPE_SHIFT) == b * g + hh for hh in range(g)] if rope else [None] * g
    return heads, ropes


def _sel(mask, x):
    return x if mask is None else jnp.where(mask, x, jnp.zeros_like(x))


class Step(typing.NamedTuple):
    qi: typing.Any
    kj: typing.Any
    first: typing.Any
    last: typing.Any
    plain: typing.Any
    masked: typing.Any


def _fwd_steps(tri, nq, nk):
    if not tri:
        return (nq, nk), lambda i, j: Step(i, j, j == 0, j == nk - 1, True, False)
    if nq % 2:
        return (nq, nk), lambda i, j: Step(i, jnp.minimum(i, j), j == 0, j == nk - 1, j < i, j == i)

    def at(i, t):
        low = t <= i
        diag = (t == i) | (t == nq)
        return Step(jnp.where(low, i, nq - 1 - i), jnp.where(low, t, t - (i + 1)), (t == 0) | (t == i + 1), diag,
                    jnp.logical_not(diag), diag)

    return (nq // 2, nq + 1), at


def _bwd_steps(tri, nq, nk):
    if not tri:
        return (nk, nq), lambda j, i: Step(i, j, i == 0, i == nq - 1, True, False)
    if nk % 2:
        return (nk, nq), lambda j, i: Step(jnp.maximum(i, j), j, i == 0, i == nq - 1, i > j, i == j)

    def at(j, t):
        n1 = nq - j
        low = t < n1
        diag = (t == 0) | (t == n1)
        return Step(jnp.where(low, j + t, nk - 1 - j + t - n1), jnp.where(low, j, nk - 1 - j), diag,
                    (t == n1 - 1) | (t == nq), jnp.logical_not(diag), diag)

    return (nk // 2, nq + 1), at


def _carried(comm, refs, n_in, n_out):
    ci, co = len(comm.ins), len(comm.out_shapes)
    ins = refs[n_in:n_in + ci]
    outs = refs[n_in + ci + n_out:n_in + ci + n_out + co]
    rest = refs[:n_in] + refs[n_in + ci:n_in + ci + n_out] + refs[n_in + ci + n_out + co:-2]
    return rest, (ins, outs, refs[-2], refs[-1])


def _mattn_fwd(q, k, v, *, qc, kc, vc, nb, g, mode, name, dq_scale=1.0, ck=None, qr=None, qrc=0, kr=None, blk=512,
               comm=None):
    s, t = q.shape[0], k.shape[0]
    bq, bk = min(blk, s), min(blk, t)
    nq, nk = s // bq, t // bk
    tri = mode != 'full'
    bias, rope = ck is not None, qr is not None
    assert not tri or (bq == bk and bq % CHUNK == 0)
    rs = min(ATTN_ROW_SLAB, bq)
    n_in = 3 + bias + 2 * rope
    (n1, n2), step_at = _fwd_steps(tri, nq, nk)

    def body(*refs):
        refs = list(refs)
        b, p1, p2 = pl.program_id(0), pl.program_id(1), pl.program_id(2)
        st = step_at(p1, p2)
        i, j = st.qi, st.kj
        if comm is not None:
            refs, comm_refs = _carried(comm, refs, n_in, 2)
            pl.when((b == 0) & (p1 == 0) & (p2 == 0))(lambda: comm.start(*comm_refs))
        q_ref, k_ref, v_ref = refs[:3]
        pos = 3
        ck_ref = qr_ref = kr_ref = None
        if bias:
            ck_ref = refs[pos]
            pos += 1
        if rope:
            qr_ref, kr_ref = refs[pos:pos + 2]
            pos += 2
        o_ref, lse_ref, m_s, l_s, acc_s = refs[pos:]
        heads, ropes = _lane_masks(g, b, rope)

        @pl.when(st.first)
        def _():
            m_s[...] = jnp.full_like(m_s, MASK_VALUE)
            l_s[...] = jnp.zeros_like(l_s)
            acc_s[...] = jnp.zeros_like(acc_s)

        def compute(masked):
            k2, v2 = k_ref[...], v_ref[...]
            for r in range(bq // rs):
                rows = pl.ds(r * rs, rs)
                q2 = q_ref[rows, :]
                alphas, pvs = [], []
                for hh in range(g):
                    sc = _dot(_sel(heads[hh], q2), k2, NT)
                    if rope:
                        sc = sc + _dot(_sel(ropes[hh], qr_ref[rows, :]), kr_ref[...], NT)
                    if bias:
                        sc = sc - ck_ref[0, hh:hh + 1, :]
                    if masked:
                        sc = jnp.where(_mask(mode, i * bq + r * rs, j * bk, rs, bk), sc, MASK_VALUE)
                    m_prev = m_s[hh, rows]
                    m_new = jnp.maximum(m_prev, jnp.max(sc, axis=1, keepdims=True))
                    alpha = jnp.exp(m_prev - m_new)
                    p = jnp.exp(sc - m_new)
                    l_s[hh, rows] = alpha * l_s[hh, rows] + jnp.sum(p, axis=1, keepdims=True)
                    m_s[hh, rows] = m_new
                    alphas.append(alpha)
                    pvs.append(_dot(p.astype(BF16), _sel(heads[hh], v2)))
                alpha = alphas[0]
                for hh in range(1, g):
                    alpha = jnp.where(heads[hh], alphas[hh], alpha)
                acc_s[rows, :] = acc_s[rows, :] * alpha + sum(pvs[1:], pvs[0])

        if tri:
            pl.when(st.plain)(functools.partial(compute, False))
            pl.when(st.masked)(functools.partial(compute, True))
        else:
            compute(False)

        @pl.when(st.last)
        def _():
            lane = lax.broadcasted_iota(jnp.int32, (bq, LANES), 1)
            l_full, lse = l_s[0], jnp.zeros((bq, LANES), F32)
            for hh in range(g):
                if hh:
                    l_full = jnp.where(heads[hh], l_s[hh], l_full)
                lse = jnp.where(lane == hh, m_s[hh] + jnp.log(l_s[hh]), lse)
            o_ref[...] = (acc_s[...] / l_full).astype(o_ref.dtype)
            lse_ref[...] = lse

        if comm is not None:
            pl.when((b == nb - 1) & (p1 == n1 - 1) & (p2 == n2 - 1))(lambda: comm.finish(*comm_refs))

    qi = lambda p1, p2: step_at(p1, p2).qi
    kj = lambda p1, p2: step_at(p1, p2).kj
    in_specs = [pl.BlockSpec((bq, LANES), lambda b, p1, p2: (qi(p1, p2), qc + b)),
                pl.BlockSpec((bk, LANES), lambda b, p1, p2: (kj(p1, p2), kc + b)),
                pl.BlockSpec((bk, LANES), lambda b, p1, p2: (kj(p1, p2), vc + b))]
    args = [q, k, v]
    if bias:
        in_specs.append(pl.BlockSpec((1, 8, bk), lambda b, p1, p2: (b, 0, kj(p1, p2))))
        args.append(ck)
    if rope:
        in_specs += [pl.BlockSpec((bq, LANES), lambda b, p1, p2: (qi(p1, p2), qrc)),
                     pl.BlockSpec((bk, LANES), lambda b, p1, p2: (kj(p1, p2), 0))]
        args += [qr, kr]
    out = pl.BlockSpec((bq, LANES), lambda b, p1, p2: (qi(p1, p2), b))
    out_specs = [out, out]
    out_shape = [jax.ShapeDtypeStruct((s, LANES * nb), BF16), jax.ShapeDtypeStruct((s, LANES * nb), F32)]
    scratch = [pltpu.VMEM((g, bq, 1), F32), pltpu.VMEM((g, bq, 1), F32), pltpu.VMEM((bq, LANES), F32)]
    if comm is not None:
        in_specs += [ANY] * len(comm.ins)
        args += comm.ins
        out_specs += [ANY] * len(comm.out_shapes)
        out_shape += comm.out_shapes
        scratch += _sems(comm.n_sems, comm.n_sems)
    res = pl.pallas_call(body, name=name, grid=(nb, n1, n2), in_specs=in_specs, out_specs=out_specs, out_shape=out_shape,
                         scratch_shapes=scratch, compiler_params=_params('arbitrary', 'arbitrary', 'arbitrary'))(*args)
    return res if comm is None else (res[0], res[1], res[2:])


def _mattn_bwd(q, k, v, o, do, lse, *, qc, kc, vc, nb, g, mode, name, dq_scale=1.0, ck=None, qr=None, qrc=0, kr=None,
               blk=512, comm=None):
    s, t = q.shape[0], k.shape[0]
    bq, bk = min(blk, s), min(blk, t)
    nq, nk = s // bq, t // bk
    tri = mode != 'full'
    bias, rope = ck is not None, qr is not None
    rs = min(ATTN_ROW_SLAB, bq)
    n_in, n_out = 6 + bias + 2 * rope, 3 + 2 * bias + 2 * rope
    (n1, n2), step_at = _bwd_steps(tri, nq, nk)

    def body(*refs):
        refs = list(refs)
        if comm is not None:
            refs, comm_refs = _carried(comm, refs, n_in, n_out)
            first = (pl.program_id(0) == 0) & (pl.program_id(1) == 0) & (pl.program_id(2) == 0)
            pl.when(first)(lambda: comm.start(*comm_refs))
        q_ref, k_ref, v_ref, o_ref, do_ref, lse_ref = refs[:6]
        pos = 6
        ck_ref = qr_ref = kr_ref = dck_ref = dcq_ref = dqr_ref = dkr_ref = dck_s = None
        if bias:
            ck_ref = refs[pos]
            pos += 1
        if rope:
            qr_ref, kr_ref = refs[pos:pos + 2]
            pos += 2
        dq_ref, dk_ref, dv_ref = refs[pos:pos + 3]
        pos += 3
        if bias:
            dck_ref, dcq_ref = refs[pos:pos + 2]
            pos += 2
        if rope:
            dqr_ref, dkr_ref = refs[pos:pos + 2]
            pos += 2
        dk_s, dv_s = refs[pos:pos + 2]
        if bias:
            dck_s = refs[pos + 2]
        b, p1, p2 = pl.program_id(0), pl.program_id(1), pl.program_id(2)
        st = step_at(p1, p2)
        i, j = st.qi, st.kj
        heads, ropes = _lane_masks(g, b, rope)

        @pl.when((p1 == 0) & (p2 == 0))
        def _():
            dq_ref[...] = jnp.zeros_like(dq_ref)
            if bias:
                dcq_ref[...] = jnp.zeros_like(dcq_ref)

        if rope:
            @pl.when((b == 0) & (p1 == 0) & (p2 == 0))
            def _():
                dqr_ref[...] = jnp.zeros_like(dqr_ref)
                dkr_ref[...] = jnp.zeros_like(dkr_ref)

        @pl.when(st.first)
        def _():
            dk_s[...] = jnp.zeros_like(dk_s)
            dv_s[...] = jnp.zeros_like(dv_s)
            if bias:
                dck_s[...] = jnp.zeros_like(dck_s)

        def compute(masked):
            k2, v2 = k_ref[...], v_ref[...]
            lane = lax.broadcasted_iota(jnp.int32, (rs, LANES), 1)
            rk = pl.ds(pl.multiple_of(j * bk, bk), bk)
            add = lambda tot, x: x if tot is None else tot + x
            dv_t = dk_t = dkr_t = None
            dck_t = [None] * g
            for r in range(bq // rs):
                rows = pl.ds(r * rs, rs)
                rq = pl.ds(pl.multiple_of(i * bq + r * rs, rs), rs)
                q2, do2, lse2 = q_ref[rows, :], do_ref[rows, :], lse_ref[rows, :]
                dd = do2.astype(F32) * o_ref[rows, :].astype(F32)
                dq_t = dqr_t = dcq_t = None
                for hh in range(g):
                    qm = _sel(heads[hh], q2)
                    sc = _dot(qm, k2, NT)
                    if rope:
                        qrm = _sel(ropes[hh], qr_ref[rows, :])
                        sc = sc + _dot(qrm, kr_ref[...], NT)
                    if bias:
                        sc = sc - ck_ref[0, hh:hh + 1, :]
                    if masked:
                        sc = jnp.where(_mask(mode, i * bq + r * rs, j * bk, rs, bk), sc, MASK_VALUE)
                    p = jnp.exp(sc - jnp.sum(jnp.where(lane == hh, lse2, 0.0), axis=1, keepdims=True))
                    dom = _sel(heads[hh], do2)
                    dp = _dot(dom, v2, NT)
                    delta = jnp.sum(_sel(heads[hh], dd), axis=1, keepdims=True)
                    ds = p * (dp - delta)
                    dsb = ds.astype(BF16)
                    dv_t = add(dv_t, _dot(p.astype(BF16), dom, TN))
                    dk_t = add(dk_t, _dot(dsb, qm, TN))
                    dq_t = add(dq_t, _dot(dsb, _sel(heads[hh], k2)))
                    if rope:
                        dqr_t = add(dqr_t, _dot(dsb, _sel(ropes[hh], kr_ref[...])))
                        dkr_t = add(dkr_t, _dot(dsb, qrm, TN))
                    if bias:
                        dck_t[hh] = add(dck_t[hh], jnp.sum(ds, axis=0, keepdims=True))
                        dcq_t = add(dcq_t, jnp.where(lane == hh, jnp.sum(ds, axis=1, keepdims=True), 0.0))
                dq_ref[rq, :] += dq_t if dq_scale == 1.0 else dq_scale * dq_t
                if rope:
                    dqr_ref[rq, :] += dq_scale * dqr_t
                if bias:
                    dcq_ref[rq, :] += dcq_t
            dv_s[...] += dv_t
            dk_s[...] += dk_t
            if rope:
                dkr_ref[rk, :] += dkr_t
            if bias:
                for hh in range(g):
                    dck_s[hh:hh + 1, :] -= dck_t[hh]

        if tri:
            pl.when(st.plain)(functools.partial(compute, False))
            pl.when(st.masked)(functools.partial(compute, True))
        else:
            compute(False)

        @pl.when(st.last)
        def _():
            dk_ref[...] = dk_s[...]
            dv_ref[...] = dv_s[...]
            if bias:
                dck_ref[0] = dck_s[...]

        if comm is not None:
            pl.when((b == nb - 1) & (p1 == n1 - 1) & (p2 == n2 - 1))(lambda: comm.finish(*comm_refs))

    qrow = lambda col: pl.BlockSpec((bq, LANES), lambda b, p1, p2: (step_at(p1, p2).qi, col(b)))
    krow = lambda col: pl.BlockSpec((bk, LANES), lambda b, p1, p2: (step_at(p1, p2).kj, col(b)))
    in_specs = [qrow(lambda b: qc + b), krow(lambda b: kc + b), krow(lambda b: vc + b), qrow(lambda b: b),
                qrow(lambda b: b), qrow(lambda b: b)]
    args = [q, k, v, o, do, lse]
    whole = lambda rows: pl.BlockSpec((rows, LANES), lambda b, j, i: (0, b))
    out_specs = [whole(s), krow(lambda b: b), krow(lambda b: b)]
    out_shape = [jax.ShapeDtypeStruct((s, LANES * nb), F32), jax.ShapeDtypeStruct((t, LANES * nb), F32),
                 jax.ShapeDtypeStruct((t, LANES * nb), F32)]
    scratch = [pltpu.VMEM((bk, LANES), F32), pltpu.VMEM((bk, LANES), F32)]
    if bias:
        ckj = pl.BlockSpec((1, 8, bk), lambda b, p1, p2: (b, 0, step_at(p1, p2).kj))
        in_specs.append(ckj)
        args.append(ck)
        out_specs += [ckj, whole(s)]
        out_shape += [jax.ShapeDtypeStruct((nb, 8, t), F32), jax.ShapeDtypeStruct((s, LANES * nb), F32)]
    if rope:
        in_specs += [qrow(lambda b: qrc), krow(lambda b: 0)]
        args += [qr, kr]
        out_specs += [pl.BlockSpec((s, LANES), lambda b, j, i: (0, 0)), pl.BlockSpec((t, LANES), lambda b, j, i: (0, 0))]
        out_shape += [jax.ShapeDtypeStruct((s, LANES), F32), jax.ShapeDtypeStruct((t, LANES), F32)]
    if bias:
        scratch.append(pltpu.VMEM((8, bk), F32))
    if comm is not None:
        in_specs += [ANY] * len(comm.ins)
        args += comm.ins
        out_specs += [ANY] * len(comm.out_shapes)
        out_shape += comm.out_shapes
        scratch += _sems(comm.n_sems, comm.n_sems)
    res = pl.pallas_call(body, name=name, grid=(nb, n1, n2), in_specs=in_specs, out_specs=out_specs,
                         out_shape=out_shape, scratch_shapes=scratch,
                         compiler_params=_params('arbitrary', 'arbitrary', 'arbitrary'))(*args)
    return res if comm is None else (*res[:n_out], res[n_out:])


def _gla_chunk(la_c, k_c):
    r = lax.broadcasted_iota(jnp.int32, (CHUNK, CHUNK), 0)
    c = lax.broadcasted_iota(jnp.int32, (CHUNK, CHUNK), 1)
    tri = jnp.where(c <= r, 1.0, 0.0).astype(BF16)
    cum = _tri_dot(tri, la_c)
    end = jnp.sum(la_c, axis=0, keepdims=True)
    dec = jnp.exp(end - cum)
    return dec, k_c * dec, jnp.exp(end)


GLA_PAIRS = GLA_HEADS // 2


def _gla_fwd(z, la, *, qc, kc, vc, name, blk=512):
    s = z.shape[0]
    bs = min(blk, s)
    ncb = bs // CHUNK
    nblk = s // bs

    def body(q_ref, k_ref, va_ref, vb_ref, la_ref, o_ref, st_ref, st):
        @pl.when(pl.program_id(1) == 0)
        def _():
            st[...] = jnp.zeros_like(st)

        heads, _ = _lane_masks(2, 0, False)
        v_refs = (va_ref, vb_ref)
        for c in range(ncb):
            sl = pl.ds(c * CHUNK, CHUNK)
            _, kf, a = _gla_chunk(la_ref[sl, :], k_ref[sl, :])
            qs = q_ref[sl, :] * (GLA_DK ** -0.5)
            for hh in range(2):
                ut = _dot(v_refs[hh][sl, :].astype(BF16), _sel(heads[hh], kf).astype(BF16), TN)
                new = a * st[hh] + ut
                st[hh] = new
                st_ref[0, c, hh] = new
                o_ref[sl, hh * GLA_DV:(hh + 1) * GLA_DV] = _dot(_sel(heads[hh], qs).astype(BF16), new.astype(BF16), NT)

    col = lambda c0, m=1: pl.BlockSpec((bs, LANES), lambda b, i: (i, c0 + m * b))
    return pl.pallas_call(
        body, name=name, grid=(GLA_PAIRS, nblk),
        in_specs=[col(qc), col(kc), col(vc, 2), col(vc + 1, 2), col(0)],
        out_specs=[pl.BlockSpec((bs, 2 * GLA_DV), lambda b, i: (i, b)),
                   pl.BlockSpec((1, ncb, 2, GLA_DV, LANES), lambda b, i: (b, i, 0, 0, 0))],
        out_shape=[jax.ShapeDtypeStruct((s, GLA_HEADS * GLA_DV), F32),
                   jax.ShapeDtypeStruct((GLA_PAIRS, s // CHUNK, 2, GLA_DV, LANES), F32)],
        scratch_shapes=[pltpu.VMEM((2, GLA_DV, LANES), F32)],
        compiler_params=_params('arbitrary', 'arbitrary'))(z, z, z, z, la)


def _gla_bwd(z, la, st_all, st_prev, do, *, qc, kc, vc, name, blk=512):
    s = z.shape[0]
    bs = min(blk, s)
    ncb = bs // CHUNK
    nblk = s // bs

    def body(q_ref, k_ref, va_ref, vb_ref, la_ref, st_ref, sp_ref, do_ref, dq_ref, dk_ref, dv_ref, dla_ref, ga):
        @pl.when(pl.program_id(1) == 0)
        def _():
            ga[...] = jnp.zeros_like(ga)

        r = lax.broadcasted_iota(jnp.int32, (CHUNK, CHUNK), 0)
        cc = lax.broadcasted_iota(jnp.int32, (CHUNK, CHUNK), 1)
        tri_rev = jnp.where(cc >= r, 1.0, 0.0).astype(BF16)
        heads, _ = _lane_masks(2, 0, False)
        v_refs = (va_ref, vb_ref)
        for c in reversed(range(ncb)):
            sl = pl.ds(c * CHUNK, CHUNK)
            dec, kf, a = _gla_chunk(la_ref[sl, :], k_ref[sl, :])
            qs = q_ref[sl, :] * (GLA_DK ** -0.5)
            dq2 = jnp.zeros((CHUNK, LANES), F32)
            dkd = jnp.zeros((CHUNK, LANES), F32)
            da = jnp.zeros((1, LANES), F32)
            for hh in range(2):
                hv = slice(hh * GLA_DV, (hh + 1) * GLA_DV)
                dob = do_ref[sl, hv].astype(BF16)
                g = _dot(dob, _sel(heads[hh], qs).astype(BF16), TN) + ga[hh]
                gb = g.astype(BF16)
                dq2 = dq2 + _dot(dob, st_ref[0, c, hh].astype(BF16))
                dv_ref[sl, hv] = _dot(_sel(heads[hh], kf).astype(BF16), gb, NT)
                dkd = dkd + _dot(v_refs[hh][sl, :].astype(BF16), gb)
                da = da + jnp.sum(g * sp_ref[0, c, hh], axis=0, keepdims=True)
                ga[hh] = a * g
            dq_ref[sl, :] = (GLA_DK ** -0.5) * dq2
            dk_ref[sl, :] = dkd * dec
            e = dkd * kf
            dend = jnp.sum(e, axis=0, keepdims=True) + da * a
            dla_ref[sl, :] = dend - _tri_dot(tri_rev, e)

    rev = lambda i: nblk - 1 - i
    col = lambda c0, m=1: pl.BlockSpec((bs, LANES), lambda b, i: (rev(i), c0 + m * b))
    wide = pl.BlockSpec((bs, 2 * GLA_DV), lambda b, i: (rev(i), b))
    stspec = pl.BlockSpec((1, ncb, 2, GLA_DV, LANES), lambda b, i: (b, rev(i), 0, 0, 0))
    return pl.pallas_call(
        body, name=name, grid=(GLA_PAIRS, nblk),
        in_specs=[col(qc), col(kc), col(vc, 2), col(vc + 1, 2), col(0), stspec, stspec, wide],
        out_specs=[col(0), col(0), wide, col(0)],
        out_shape=[jax.ShapeDtypeStruct((s, GLA_HEADS * GLA_DK), F32), jax.ShapeDtypeStruct((s, GLA_HEADS * GLA_DK), F32),
                   jax.ShapeDtypeStruct((s, GLA_HEADS * GLA_DV), F32), jax.ShapeDtypeStruct((s, GLA_HEADS * GLA_DK), F32)],
        scratch_shapes=[pltpu.VMEM((2, GLA_DV, LANES), F32)],
        compiler_params=_params('arbitrary', 'arbitrary'))(z, z, z, z, la, st_all, st_prev, do)


def _place():
    return lax.axis_index('x'), lax.axis_index('y'), lax.axis_index('c')


ANY = pl.BlockSpec(memory_space=pl.ANY)


def _all_gather8(blk, *, name):
    m, n = blk.shape

    def body(x_ref, out_ref, send_sems, recv_sems, local_sem):
        x, y, c = _place()
        me, sibling = (x, y, c), (x, y, 1 - c)
        chips = [(1 - x, y), (x, 1 - y), (1 - x, 1 - y)]

        def slot(px, py, pc):
            return out_ref.at[4 * px + 2 * py + pc]

        def copy(q, block, to, src=None):
            return pltpu.make_async_remote_copy(
                src_ref=slot(*block) if src is None else src, dst_ref=slot(*block), send_sem=send_sems.at[q],
                recv_sem=recv_sems.at[q], device_id=to, device_id_type=MESH)

        mine = pltpu.make_async_copy(x_ref, slot(*me), local_sem)
        mine.start()
        first = [copy(0, me, sibling, src=x_ref)]
        first += [copy(1 + q, me, (*chip, c), src=x_ref) for q, chip in enumerate(chips)]
        for cp in first:
            cp.start()
        passed = [copy(4 + q, (*chip, c), sibling) for q, chip in enumerate(chips)]
        for q, chip in enumerate(chips):
            copy(1 + q, (*chip, c), me).wait_recv()
            passed[q].start()
        copy(0, sibling, me).wait_recv()
        for q, chip in enumerate(chips):
            copy(4 + q, (*chip, 1 - c), me).wait_recv()
        for cp in first + passed:
            cp.wait_send()
        mine.wait()

    return pl.pallas_call(
        body, name=name, in_specs=[ANY], out_specs=ANY, out_shape=jax.ShapeDtypeStruct((N_DEV, m, n), blk.dtype),
        scratch_shapes=[pltpu.SemaphoreType.DMA((7,)), pltpu.SemaphoreType.DMA((7,)), pltpu.SemaphoreType.DMA(())],
    )(blk)


def _sems(*counts):
    return [pltpu.SemaphoreType.DMA((n,)) for n in counts]


class Comm(typing.NamedTuple):
    ins: list
    out_shapes: list
    n_sems: int
    start: typing.Callable
    finish: typing.Callable


def _remote(src, dst, send_sems, recv_sems, idx, to):
    return lambda: pltpu.make_async_remote_copy(src_ref=src, dst_ref=dst, send_sem=send_sems.at[idx],
                                                recv_sem=recv_sems.at[idx], device_id=to, device_id_type=MESH)


def _comm_from(copies, ins, out_shapes, n_sems):
    def start(*refs):
        for cp in copies(*refs)[0]:
            cp().start()

    def finish(*refs):
        sent, received = copies(*refs)
        for cp in received:
            cp().wait_recv()
        for cp in sent:
            cp().wait_send()

    return Comm(list(ins), list(out_shapes), n_sems, start, finish)


def _run_comm(comm, *, name, alias=False):
    n_in, n_out = len(comm.ins), len(comm.out_shapes)

    def body(*refs):
        ins, outs, sems = refs[:n_in], refs[n_in:n_in + n_out], refs[n_in + n_out:]
        comm.start(ins, outs, *sems)
        comm.finish(ins, outs, *sems)

    return pl.pallas_call(body, name=name, in_specs=[ANY] * n_in, out_specs=[ANY] * n_out, out_shape=comm.out_shapes,
                          input_output_aliases={q: q for q in range(n_in)} if alias else {},
                          scratch_shapes=_sems(comm.n_sems, comm.n_sems))(*comm.ins)


def _half(rows, c):
    h = rows // 2
    return pl.ds(pl.multiple_of(c * h, h), h)


def _gathered(ref, chip, rows, side):
    if not side:
        return ref.at[chip, rows]
    n = ref.shape[1] // N_CHIPS
    return ref.at[rows, pl.ds(pl.multiple_of(chip * n, n), n)]


def _gather_over_ici(ws, side):
    def copies(ins, outs, send_sems, recv_sems):
        x, y, c = _place()
        me_chip = 2 * x + y
        sent, received = [], []
        for q, w in enumerate(ws):
            half, every = _half(w.shape[0], c), pl.ds(0, w.shape[0])
            for k, (px, py) in enumerate([(1 - x, y), (x, 1 - y), (1 - x, 1 - y)]):
                sent.append(_remote(ins[q].at[half], _gathered(outs[q], me_chip, half, side[q]), send_sems, recv_sems,
                                    4 * q + k, (px, py, c)))
                slot = _gathered(outs[q], 2 * px + py, half, side[q])
                received.append(_remote(slot, slot, send_sems, recv_sems, 4 * q + k, (px, py, c)))
            whole = _remote(ins[q], _gathered(outs[q], me_chip, every, side[q]), send_sems, recv_sems, 4 * q + 3,
                            (x, y, 1 - c))
            sent.append(whole)
            received.append(whole)
        return sent, received

    shapes = [jax.ShapeDtypeStruct((w.shape[0], N_CHIPS * w.shape[1]) if sd else (N_CHIPS,) + w.shape, w.dtype)
              for w, sd in zip(ws, side)]
    return _comm_from(copies, ws, shapes, 4 * len(ws))


def _gather_over_d2d(parts, side):
    def copies(ins, outs, send_sems, recv_sems):
        x, y, c = _place()
        sent, received = [], []
        for q, w in enumerate(parts):
            rows = w.shape[0] if side[q] else w.shape[1]
            for k, (px, py) in enumerate([(1 - x, y), (x, 1 - y), (1 - x, 1 - y)]):
                mine = _gathered(outs[q], 2 * px + py, _half(rows, c), side[q])
                theirs = _gathered(outs[q], 2 * px + py, _half(rows, 1 - c), side[q])
                sent.append(_remote(mine, mine, send_sems, recv_sems, 3 * q + k, (x, y, 1 - c)))
                received.append(_remote(theirs, theirs, send_sems, recv_sems, 3 * q + k, (x, y, 1 - c)))
        return sent, received

    return _comm_from(copies, parts, [jax.ShapeDtypeStruct(w.shape, w.dtype) for w in parts], 3 * len(parts))


def _to_sibling(gs, *, name):
    n = len(gs)

    def body(*refs):
        ins, outs = refs[:n], refs[n:2 * n]
        send_sems, recv_sems = refs[2 * n:]
        x, y, c = _place()
        cps = [pltpu.make_async_remote_copy(
            src_ref=ins[q], dst_ref=outs[q], send_sem=send_sems.at[q], recv_sem=recv_sems.at[q],
            device_id=(x, y, 1 - c), device_id_type=MESH) for q in range(n)]
        for cp in cps:
            cp.start()
        for cp in cps:
            cp.wait()

    return pl.pallas_call(body, name=name, in_specs=[ANY] * n, out_specs=[ANY] * n,
                          out_shape=[jax.ShapeDtypeStruct(g.shape, g.dtype) for g in gs],
                          scratch_shapes=_sems(n, n))(*gs)


def _chip_exchange(ps):
    def copies(ins, outs, send_sems, recv_sems):
        x, y, c = _place()
        cps = [_remote(ins[q].at[2 * px + py], outs[q].at[k], send_sems, recv_sems, 3 * q + k, (px, py, c))
               for q in range(len(ps)) for k, (px, py) in enumerate([(1 - x, y), (x, 1 - y), (1 - x, 1 - y)])]
        return cps, cps

    return _comm_from(copies, ps, [jax.ShapeDtypeStruct((3,) + p.shape[1:], p.dtype) for p in ps], 3 * len(ps))


def _sum_chips(own, r, *, name, ts=256):
    k, n = own.shape
    ts = min(ts, k)

    def body(own_ref, r_ref, o_ref):
        f = lambda q: r_ref[q].astype(F32)
        o_ref[...] = ((own_ref[...].astype(F32) + f(0)) + f(1)) + f(2)

    return pl.pallas_call(
        body, name=name, grid=(k // ts,),
        in_specs=[pl.BlockSpec((ts, n), lambda i: (i, 0)), pl.BlockSpec((3, ts, n), lambda i: (0, i, 0))],
        out_specs=pl.BlockSpec((ts, n), lambda i: (i, 0)), out_shape=jax.ShapeDtypeStruct((k, n), F32),
        compiler_params=_params('arbitrary'))(own, r)


WIN_SHARD = N_IN // N_CHIPS
WIN_PAD = -(-WIN_SHARD // LANES) * LANES
GATE_WIRE_ROWS = 32


def _full_layer(sh, axis):
    _, k, n = sh.shape
    if axis == 2:
        return sh.transpose(1, 0, 2).reshape(k, N_CHIPS * n)
    return sh.reshape(N_CHIPS * k, n)


def _win_cols(wp, o, n):
    parts = []
    while n > 0:
        j, r = divmod(o, WIN_SHARD)
        take = min(n, WIN_SHARD - r)
        parts.append(wp[:, j * WIN_PAD + r:j * WIN_PAD + r + take])
        o, n = o + take, n - take
    return parts[0] if len(parts) == 1 else jnp.concatenate(parts, axis=1)


def _split_full(full, axis):
    k, n = full.shape
    if axis == 2:
        return jnp.stack([full[:, j * (n // N_CHIPS):(j + 1) * (n // N_CHIPS)] for j in range(N_CHIPS)])
    return full.reshape(N_CHIPS, k // N_CHIPS, n)


def _padc(a, w):
    return jnp.pad(a, ((0, 0), (0, w - a.shape[1])))


def _swap16(a):
    return jnp.concatenate([a[..., 16:32], a[..., 0:16]], axis=-1)


B_GR, B_GQ, B_GK, B_GV, B_MQ, B_MKR, B_MKRS, B_FF, B_GLOW, B_MKV, B_END = (
    0, 512, 768, 1024, 1536, 1792, 1920, 2048, 2176, 2304, 2432)
B_W = 2560
O_FQ, O_FF, O_GQ, O_GLOW, O_GR, O_MQ, O_MKV, O_MKR, O_ZG = 0, 768, 772, 1796, 1812, 2324, 2580, 2708, 2740


def _repack_layer_weights(w):
    wi = functools.partial(_win_cols, w['w_in'])
    out = dict(w)
    out['in_a'] = jnp.concatenate([wi(O_FQ, 256) * FOX_SCALE, wi(O_FQ + 256, 512)], axis=1)
    kr = wi(O_MKR, 32)
    out['in_b'] = jnp.concatenate([
        wi(O_GR, 512), wi(O_GQ, 1024), wi(O_MQ, 256), jnp.tile(kr, (1, MLA_HEADS)), jnp.tile(_swap16(kr), (1, MLA_HEADS)),
        _padc(wi(O_FF, 4), 128), _padc(wi(O_GLOW, 16), 128), wi(O_MKV, 128),
        jnp.zeros((D_MODEL, B_W - B_END), kr.dtype)], axis=1)
    out['in_c'] = wi(O_ZG, 3072)
    uq = w['w_mla_uq'].reshape(MLA_Q_RANK, MLA_HEADS, MLA_NOPE + MLA_ROPE)
    rope = uq[:, :, MLA_NOPE:]
    out['uq'] = jnp.concatenate([uq[:, :, :MLA_NOPE].reshape(MLA_Q_RANK, -1), rope.reshape(MLA_Q_RANK, -1),
                                 _swap16(rope).reshape(MLA_Q_RANK, -1)], axis=1)
    ukv = w['w_mla_ukv'].reshape(MLA_KV_RANK, MLA_HEADS, MLA_NOPE + MLA_VD)
    out['ukv'] = jnp.concatenate([ukv[:, :, :MLA_NOPE].reshape(MLA_KV_RANK, -1),
                                  ukv[:, :, MLA_NOPE:].reshape(MLA_KV_RANK, -1)], axis=1)
    out['gate'] = jnp.pad(w['w_gla_gate'], ((0, 128 - GLA_RANK), (0, 0)))
    return out


def _unpack_layer_grads(g):
    a, b, c = g['in_a'], g['in_b'], g['in_c']
    a = jnp.concatenate([a[:, :256] * FOX_SCALE, a[:, 256:]], axis=1)
    fold = lambda o: sum(b[:, o + MLA_ROPE * q:o + MLA_ROPE * (q + 1)] for q in range(MLA_HEADS))
    kr = fold(B_MKR) + _swap16(fold(B_MKRS))
    w_in = jnp.concatenate([a, b[:, B_FF:B_FF + 4], b[:, B_GQ:B_GQ + 1024], b[:, B_GLOW:B_GLOW + 16],
                            b[:, B_GR:B_GR + 512], b[:, B_MQ:B_MQ + 256], b[:, B_MKV:B_MKV + 128], kr, c], axis=1)
    uq = g['uq']
    nope = uq[:, :256].reshape(MLA_Q_RANK, MLA_HEADS, MLA_NOPE)
    rope = (uq[:, 256:384].reshape(MLA_Q_RANK, MLA_HEADS, MLA_ROPE)
            + _swap16(uq[:, 384:512].reshape(MLA_Q_RANK, MLA_HEADS, MLA_ROPE)))
    w_uq = jnp.concatenate([nope, rope], axis=2).reshape(MLA_Q_RANK, -1)
    ukv = g['ukv']
    w_ukv = jnp.concatenate([ukv[:, :256].reshape(MLA_KV_RANK, MLA_HEADS, MLA_NOPE),
                             ukv[:, 256:].reshape(MLA_KV_RANK, MLA_HEADS, MLA_VD)], axis=2).reshape(MLA_KV_RANK, -1)
    out = {'w_in': w_in, 'w_mla_uq': w_uq, 'w_mla_ukv': w_ukv, 'w_gla_gate': g['gate'][:GLA_RANK]}
    for nm in ('w_up_fox', 'w_up_gla', 'w_up_mla', 'w_out', 'w_xq', 'w_xkv', 'w_xo', 'w_mlp1', 'w_mlp2'):
        out[nm] = g[nm]
    return out


def _rope_tables(s):
    half = MLA_ROPE // 2
    inv = ROPE_BASE ** (-jnp.arange(half, dtype=F32) / half)
    ang = jnp.arange(s).astype(F32)[:, None] * inv[None, :]
    cos, sin = jnp.cos(ang), jnp.sin(ang)
    c1 = jnp.concatenate([cos, cos], axis=1)
    s1 = jnp.concatenate([-sin, sin], axis=1)
    return jnp.tile(c1, (1, MLA_HEADS)), jnp.tile(s1, (1, MLA_HEADS))


def _rms_bwd(x, dh, g):
    r = lax.rsqrt(jnp.mean(x * x, axis=-1, keepdims=True) + EPS)
    xh = x * r
    gd = dh * g
    return r * (gd - xh * jnp.mean(gd * xh, axis=-1, keepdims=True)), dh * xh


def _norm_bwd_call(x, dh, g, dres, name):
    w = x.width if isinstance(x, Cols) else x.shape[1]

    def with_res(xv, dv, rv, gv):
        dx, dg = _rms_bwd(xv, dv.astype(F32), gv)
        return rv + dx, dg

    def plain(xv, dv, gv):
        return _rms_bwd(xv, dv.astype(F32), gv)

    if dres is None:
        return _rowwise(plain, [x, dh], [g], [(w, F32)], [w], name=name)
    return _rowwise(with_res, [x, dh, dres], [g], [(w, F32)], [w], name=name)


def _gla_out_fwd(oraw, gr, g_out):
    outs = []
    for hh in range(GLA_HEADS):
        sl = slice(hh * GLA_DV, (hh + 1) * GLA_DV)
        oh = oraw[:, sl]
        n = oh * lax.rsqrt(jnp.mean(oh * oh, axis=-1, keepdims=True) + EPS) * g_out
        r = gr[:, sl]
        outs.append(n * (r * _sig(r)))
    return (jnp.concatenate(outs, axis=1),)


def _gla_out_bwd(oraw, gr, dout, g_out):
    d_o, d_r, dg = [], [], 0.0
    for hh in range(GLA_HEADS):
        sl = slice(hh * GLA_DV, (hh + 1) * GLA_DV)
        oh, r, do = oraw[:, sl], gr[:, sl], dout[:, sl].astype(F32)
        rs = lax.rsqrt(jnp.mean(oh * oh, axis=-1, keepdims=True) + EPS)
        sg = _sig(r)
        dn = do * (r * sg)
        d_r.append(do * (oh * rs * g_out) * (sg + r * sg * (1.0 - sg)))
        dx, dgh = _rms_bwd(oh, dn, g_out)
        d_o.append(dx)
        dg = dg + dgh
    return jnp.concatenate(d_o, axis=1), jnp.concatenate(d_r, axis=1), dg


def _adam(w, g, m, v):
    m = ADAM_B1 * m + (1.0 - ADAM_B1) * g
    v = ADAM_B2 * v + (1.0 - ADAM_B2) * (g * g)
    m_hat = m / (1.0 - ADAM_B1 ** ADAM_STEP)
    v_hat = v / (1.0 - ADAM_B2 ** ADAM_STEP)
    return -ADAM_LR * (m_hat / (jnp.sqrt(v_hat) + ADAM_EPS) + ADAM_WD * w), m, v


def _layer_fwd(x, mem, w, p, tabs, tag, carry_fox=None, after_fox=None, carry_mla=None):
    c4, s4 = tabs
    sv = {'x0': x}
    nm = lambda t: f'{t}_{tag}'
    za, h = _mm(x, w['in_a'], mode='nn', out_dtype=BF16, norm_g=p['g_mix'], emit_norm=True, name=nm('in_a'))
    zb = _mm(h, w['in_b'], mode='nn', out_dtype=F32, name=nm('in_b'))
    zc = _mm(h, w['in_c'], mode='nn', out_dtype=F32, name=nm('in_c'))
    sv.update(h=h, zc=zc)
    ff = Cols(zb, 128, B_FF // 128)
    (lf,) = _rowwise(lambda f, b: (_logsig(f + b),), [ff], [p['b_fox']], [(128, F32)], name=nm('fox_lf'))
    cum = _cumsum_rows(lf, reverse=False, name=nm('fox_cum'))
    ckf = jnp.pad(cum[:, :FOX_HEADS].T.reshape(2, 2, x.shape[0]), ((0, 0), (0, 6), (0, 0)))
    fox = dict(qc=0, kc=2, vc=4, nb=2, g=2, mode='causal', ck=ckf)
    o_fox, lse_fox, *carried = _mattn_fwd(za, za, za, name=nm('fox_attn'), comm=carry_fox, **fox)
    if after_fox is not None:
        w = {**w, **after_fox(carried[0])}
    sv.update(ff=ff, za=za, fox=fox, o_fox=o_fox, lse_fox=lse_fox)
    glow = Cols(zb, 128, B_GLOW // 128)
    gr = Cols(zb, 512, B_GR // 512)

    def gate_fn(gl, wg, bg):
        return (_logsig(_dot(gl.astype(BF16), wg) + bg) / GLA_TAU,)

    (la,) = _rowwise(gate_fn, [glow], [w['gate'], p['b_gla']], [(256, F32)], name=nm('gla_gate'))
    gla = dict(qc=B_GQ // LANES, kc=B_GK // LANES, vc=B_GV // LANES)
    oraw, states = _gla_fwd(zb, la, name=nm('gla'), **gla)
    (o_gla,) = _rowwise(_gla_out_fwd, [oraw, gr], [p['g_gla_out']], [(512, BF16)], name=nm('gla_out'))
    sv.update(glow=glow, gr=gr, zb=zb, la=la, gla=gla, states=states, oraw=oraw, o_gla=o_gla)
    mq = Cols(zb, 256, B_MQ // 256)
    mkv = Cols(zb, 128, B_MKV // 128)
    mkr2 = Cols(zb, 256, B_MKR // 256)
    qp, cqn = _mm(mq, w['uq'], mode='nn', out_dtype=F32, norm_g=p['g_mla_q'], emit_norm=True, name=nm('mla_uq'))
    kvp, ckvn = _mm(mkv, w['ukv'], mode='nn', out_dtype=BF16, norm_g=p['g_mla_kv'], emit_norm=True,
                    name=nm('mla_ukv'))

    def rope_fn(qv, kr, c4v, s4v):
        q_rope = qv[:, 256:384] * c4v + qv[:, 384:512] * s4v
        q_scaled = jnp.concatenate([qv[:, 0:256], q_rope], axis=1) * MLA_SCALE
        return q_scaled, kr[:, 0:128] * c4v + kr[:, 128:256] * s4v

    qall, kr4 = _rowwise(rope_fn, [qp, mkr2, c4, s4], [], [(384, BF16), (128, BF16)], name=nm('rope'))
    mla = dict(qc=0, kc=0, vc=2, nb=2, g=2, dq_scale=MLA_SCALE, mode='chunk', qr=qall, qrc=2, kr=kr4)
    o_mla, lse_mla, *carried = _mattn_fwd(qall, kvp, kvp, name=nm('mla_attn'), comm=carry_mla, **mla)
    if carry_mla is not None:
        sv['carried_mla'] = carried[0]
    sv.update(mq=mq, mkv=mkv, cqn=cqn, ckvn=ckvn, qall=qall, kvp=kvp, mla=mla, o_mla=o_mla, lse_mla=lse_mla)
    of_m, om_m = o_fox, o_mla
    sv.update(of_m=of_m, om_m=om_m)
    b_br = p['b_branch']

    def first(acc, zg, bb):
        return _sig(zg + bb) * acc

    def more(acc, zg, bb, prev):
        return prev + _sig(zg + bb) * acc

    y = _mm(of_m, w['w_up_fox'], mode='nn', out_dtype=F32, name=nm('up_fox'), epilogue=first,
            extras=[(zc, *_mn(col_off=0)), (b_br, *_nvec(col_off=0))])
    y = _mm(o_gla, w['w_up_gla'], mode='nn', out_dtype=F32, name=nm('up_gla'), epilogue=more,
            extras=[(zc, *_mn(col_off=1024)), (b_br, *_nvec(col_off=1024)), (y, *_mn())])
    y = _mm(om_m, w['w_up_mla'], mode='nn', out_dtype=BF16, name=nm('up_mla'), epilogue=more,
            extras=[(zc, *_mn(col_off=2048)), (b_br, *_nvec(col_off=2048)), (y, *_mn())])
    add = lambda acc, res: res + acc
    x1 = _mm(y, w['w_out'], mode='nn', out_dtype=F32, name=nm('out'), epilogue=add, extras=[(x, *_mn())])
    sv.update(y=y, x1=x1)
    qx, hx = _mm(x1, w['w_xq'], mode='nn', out_dtype=BF16, norm_g=p['g_xa'], emit_norm=True, name=nm('xq'),
                 epilogue=lambda acc: acc * XA_SCALE)
    kvx, mn = _mm(mem, w['w_xkv'], mode='nn', out_dtype=BF16, norm_g=p['g_mem'], emit_norm=True, name=nm('xkv'))
    xa = dict(qc=0, kc=0, vc=4, nb=4, g=1, dq_scale=XA_SCALE, mode='full')
    ox_m, lse_x = _mattn_fwd(qx, kvx, kvx, name=nm('xa_attn'), **xa)
    x2 = _mm(ox_m, w['w_xo'], mode='nn', out_dtype=F32, name=nm('xo'), epilogue=add, extras=[(x1, *_mn())])
    sv.update(hx=hx, mn=mn, qx=qx, kvx=kvx, xa=xa, lse_x=lse_x, ox_m=ox_m, x2=x2)
    hpre, hm = _mm(x2, w['w_mlp1'], mode='nn', out_dtype=BF16, norm_g=p['g_mlp'], emit_norm=True, name=nm('mlp1'))
    relu2 = lambda t: jnp.square(jnp.maximum(t.astype(F32), 0.0))
    x3 = _mm(hpre, w['w_mlp2'], mode='nn', out_dtype=F32, name=nm('mlp2'), a_fn=relu2, epilogue=add,
             extras=[(x2, *_mn())])
    sv.update(hpre=hpre, hm=hm, w=w)
    return x3, sv


EARLY = ('w_mlp1', 'w_mlp2', 'w_xo', 'w_xq', 'w_xkv', 'w_out', 'w_up_fox', 'w_up_gla', 'w_up_mla')
LATE = ('w_in', 'w_gla_gate', 'w_mla_uq', 'w_mla_ukv')


def _layer_bwd(dx3, mem, w, p, tabs, sv, tag, carry_mla=None, early=None):
    c4, s4 = tabs
    nm = lambda t: f'{t}_{tag}'
    s = dx3.shape[0]
    gw, gs = {}, {}
    relu2 = lambda t: jnp.square(jnp.maximum(t.astype(F32), 0.0))
    gw['w_mlp2'] = _mm(sv['hpre'], dx3, mode='tn', out_dtype=F32, name=nm('d_mlp2'), a_fn=relu2)
    dact = lambda acc, hp: acc * (2.0 * jnp.maximum(hp.astype(F32), 0.0))
    dhpre = _mm(dx3, w['w_mlp2'], mode='nt', out_dtype=BF16, name=nm('d_act'), epilogue=dact,
                extras=[(sv['hpre'], *_mn())])
    gw['w_mlp1'] = _mm(sv['hm'], dhpre, mode='tn', out_dtype=F32, name=nm('d_mlp1'))
    dhm = _mm(dhpre, w['w_mlp1'], mode='nt', out_dtype=F32, name=nm('d_hm'))
    dx2, gs['g_mlp'] = _norm_bwd_call(sv['x2'], dhm, p['g_mlp'], dx3, nm('d_norm_mlp'))
    gw['w_xo'] = _mm(sv['ox_m'], dx2, mode='tn', out_dtype=F32, name=nm('d_xo'))
    dox = _mm(dx2, w['w_xo'], mode='nt', out_dtype=BF16, name=nm('d_ox'))
    dqx_m, dkx, dvx = _mattn_bwd(sv['qx'], sv['kvx'], sv['kvx'], sv['ox_m'], dox, sv['lse_x'], name=nm('xa_bwd'),
                                 **sv['xa'])
    dkvx = jnp.concatenate([dkx, dvx], axis=1).astype(BF16)
    gw['w_xq'] = _mm(sv['hx'], dqx_m, mode='tn', out_dtype=F32, name=nm('d_xq'))
    dhx = _mm(dqx_m, w['w_xq'], mode='nt', out_dtype=F32, name=nm('d_hx'))
    gw['w_xkv'] = _mm(sv['mn'], dkvx, mode='tn', out_dtype=F32, name=nm('d_xkv'))
    dmn = _mm(dkvx, w['w_xkv'], mode='nt', out_dtype=F32, name=nm('d_mn'))
    _, gs['g_mem'] = _norm_bwd_call(mem, dmn, p['g_mem'], None, nm('d_norm_mem'))
    dx1, gs['g_xa'] = _norm_bwd_call(sv['x1'], dhx, p['g_xa'], dx2, nm('d_norm_xa'))
    gw['w_out'] = _mm(sv['y'], dx1, mode='tn', out_dtype=F32, name=nm('d_out'))
    dy = _mm(dx1, w['w_out'], mode='nt', out_dtype=BF16, name=nm('d_y'))
    zc, b_br = sv['zc'], p['b_branch']

    def du_fn(dyv, zg, bb):
        g = _sig(zg + bb)
        d = dyv.astype(F32)
        return d * g[:, 0:1024], d * g[:, 1024:2048], d * g[:, 2048:3072]

    du = _rowwise(du_fn, [dy, zc], [b_br], [(D_MODEL, BF16)] * 3, name=nm('d_u'))

    def dgate(acc, dyv, zg, bb):
        g = _sig(zg + bb)
        return dyv.astype(F32) * acc * g * (1.0 - g)

    dzc, do_br = [], []
    for q, (o_m, wn) in enumerate(((sv['of_m'], 'w_up_fox'), (sv['o_gla'], 'w_up_gla'), (sv['om_m'], 'w_up_mla'))):
        dzc.append(_mm(o_m, w[wn], mode='nn', out_dtype=F32, name=nm(f'd_zg{q}'), epilogue=dgate,
                       extras=[(dy, *_mn()), (zc, *_mn(col_off=1024 * q)), (b_br, *_nvec(col_off=1024 * q))]))
        gw[wn] = _mm(o_m, du[q], mode='tn', out_dtype=F32, name=nm(f'd_up{q}'))
        do_br.append(_mm(du[q], w[wn], mode='nt', out_dtype=F32 if q == 1 else BF16, name=nm(f'd_o{q}')))
    dzc = jnp.concatenate(dzc, axis=1)
    (gs['b_branch'],) = _rowwise(lambda t: (t,), [dzc], [], [], [3072], name=nm('d_bbranch'))
    za = sv['za']
    carry_fox = None if early is None else early({nm_: gw[nm_] for nm_ in EARLY})
    dfq, dfk, dfv, dck, dcq, *carried_fox = _mattn_bwd(za, za, za, sv['o_fox'], do_br[0], sv['lse_fox'],
                                                       name=nm('fox_bwd'), comm=carry_fox, **sv['fox'])
    dcum = _padc(dck[:, :2, :].reshape(FOX_HEADS, s).T + dcq.reshape(s, 2, LANES)[:, :, :2].reshape(s, FOX_HEADS), 128)
    dlf = _cumsum_rows(dcum, reverse=True, name=nm('fox_dcum'))

    def dff_fn(dl, f, b):
        d = dl * _sig(-(f + b))
        return d, d

    dff, db_fox = _rowwise(dff_fn, [dlf, sv['ff']], [p['b_fox']], [(128, F32)], [128], name=nm('fox_dff'))
    gs['b_fox'] = db_fox
    dza = jnp.concatenate([dfq, dfk, dfv], axis=1).astype(BF16)
    dqn, dkn, dvv, dq_rope, dk_rope, *carried_mla = _mattn_bwd(sv['qall'], sv['kvp'], sv['kvp'], sv['o_mla'], do_br[2],
                                                               sv['lse_mla'], name=nm('mla_bwd'), comm=carry_mla,
                                                               **sv['mla'])

    def drope_fn(dn, dq, dk, c4v, s4v):
        return jnp.concatenate([dn, dq * c4v, dq * s4v], axis=1), jnp.concatenate([dk * c4v, dk * s4v], axis=1)

    dqp, dmkr2 = _rowwise(drope_fn, [dqn, dq_rope, dk_rope, c4, s4], [], [(512, BF16), (256, BF16)], name=nm('d_rope'))
    dkvp = jnp.concatenate([dkn, dvv], axis=1).astype(BF16)
    gw['uq'] = _mm(sv['cqn'], dqp, mode='tn', out_dtype=F32, name=nm('d_uq'))
    dcqn = _mm(dqp, w['uq'], mode='nt', out_dtype=F32, name=nm('d_cqn'))
    gw['ukv'] = _mm(sv['ckvn'], dkvp, mode='tn', out_dtype=F32, name=nm('d_ukv'))
    dckvn = _mm(dkvp, w['ukv'], mode='nt', out_dtype=F32, name=nm('d_ckvn'))
    dmq, gs['g_mla_q'] = _norm_bwd_call(sv['mq'], dcqn, p['g_mla_q'], None, nm('d_norm_q'))
    dmkv, gs['g_mla_kv'] = _norm_bwd_call(sv['mkv'], dckvn, p['g_mla_kv'], None, nm('d_norm_kv'))
    doraw, dgr, gs['g_gla_out'] = _rowwise(_gla_out_bwd, [sv['oraw'], sv['gr'], do_br[1]], [p['g_gla_out']],
                                           [(512, F32), (512, BF16)], [128], name=nm('d_gla_out'))
    st = sv['states']
    st_prev = jnp.concatenate([jnp.zeros_like(st[:, :1]), st[:, :-1]], axis=1)
    dgq, dgk, dgv, dla = _gla_bwd(sv['zb'], sv['la'], st, st_prev, doraw, name=nm('gla_bwd'), **sv['gla'])

    def dgate_fn(dl, gl, wg, bg):
        pre = _dot(gl.astype(BF16), wg) + bg
        dpre = dl * (1.0 / GLA_TAU) * _sig(-pre)
        return dpre, _dot(dpre.astype(BF16), wg, NT), dpre

    dpre, dglow, gs['b_gla'] = _rowwise(dgate_fn, [dla, sv['glow']], [w['gate'], p['b_gla']],
                                        [(256, BF16), (128, BF16)], [256], name=nm('d_gla_gate'))
    gw['gate'] = _mm(sv['glow'], dpre, mode='tn', out_dtype=F32, name=nm('d_wgate'))
    bf = lambda t: t.astype(BF16)
    dzb = jnp.concatenate([dgr, bf(dgq), bf(dgk), bf(dgv), bf(dmq), dmkr2, bf(dff), dglow, bf(dmkv),
                           jnp.zeros((s, B_W - B_END), BF16)], axis=1)
    h = sv['h']
    gw['in_a'] = _mm(h, dza, mode='tn', out_dtype=F32, name=nm('d_in_a'))
    gw['in_b'] = _mm(h, dzb, mode='tn', out_dtype=F32, name=nm('d_in_b'))
    gw['in_c'] = _mm(h, dzc, mode='tn', out_dtype=F32, name=nm('d_in_c'))
    add = lambda acc, prev: prev + acc
    dh = _mm(dza, w['in_a'], mode='nt', out_dtype=F32, name=nm('d_h_a'))
    dh = _mm(dzb, w['in_b'], mode='nt', out_dtype=F32, name=nm('d_h_b'), epilogue=add, extras=[(dh, *_mn())])
    dh = _mm(dzc, w['in_c'], mode='nt', out_dtype=F32, name=nm('d_h_c'), epilogue=add, extras=[(dh, *_mn())])
    dx0, gs['g_mix'] = _norm_bwd_call(sv['x0'], dh, p['g_mix'], dx1, nm('d_norm_mix'))
    return dx0, gw, gs, (carried_mla or [None])[0], (carried_fox or [None])[0]


def _loss_head(x, target, g_final):
    d = x.shape[1]

    def fn(xv, tv, gv):
        r = lax.rsqrt(jnp.mean(xv * xv, axis=-1, keepdims=True) + EPS)
        xh = xv * r
        e = xh * gv - tv
        dy = e * (1.0 / d)
        gd = dy * gv
        dx = r * (gd - xh * jnp.mean(gd * xh, axis=-1, keepdims=True))
        row_loss = 0.5 * jnp.mean(e * e, axis=-1, keepdims=True)
        return dx, dy * xh, jnp.broadcast_to(row_loss, (xv.shape[0], LANES))

    return _rowwise(fn, [x, target], [g_final], [(d, F32)], [d, LANES], name='loss_head')


def _small_sizes(shapes):
    return [math.prod(shapes[nm]) for nm in SMALL]


def _step(args):
    shapes = {nm: args[nm].shape for nm in ORDER}
    x, mem, target = args['x'][0], args['mem'][0], args['loss_target'][0]
    s = x.shape[0]

    def wire(nm, l):
        w = args[nm][l].astype(BF16)
        if nm == 'w_in':
            w = jnp.pad(w, ((0, 0), (0, WIN_PAD - WIN_SHARD)))
        if nm == 'w_gla_gate':
            w = jnp.pad(w, ((0, GATE_WIRE_ROWS - GLA_RANK), (0, 0)))
        return w

    axis_of = dict(BIG)
    names = tuple(nm for nm, _ in BIG)
    wires = lambda l, nms: [wire(nm, l) for nm in nms]
    width = lambda nm: WIN_PAD if nm == 'w_in' else args[nm].shape[2]
    side_by_side = lambda nms: [axis_of[nm] == 2 and width(nm) % LANES == 0 for nm in nms]
    over_ici = lambda l, nms: _gather_over_ici(wires(l, nms), side_by_side(nms))

    def whole(parts, nms, tag):
        side = side_by_side(nms)
        parts = _run_comm(_gather_over_d2d(parts, side), name=f'gather_d2d_{tag}', alias=True)
        full = {nm: p if sd else _full_layer(p, axis_of[nm]) for nm, p, sd in zip(nms, parts, side)}
        if 'w_gla_gate' in full:
            full['w_gla_gate'] = full['w_gla_gate'][:GLA_RANK]
        return full

    tabs = _rope_tables(s)
    layers_p = []
    for l in range(DEPTH):
        layers_p.append({
            'g_mix': args['g_mix'][l][None], 'b_fox': _padc(args['b_fox_forget'][l][None], 128),
            'b_gla': args['b_gla_gate'][l][None], 'g_gla_out': args['g_gla_out'][l][None],
            'g_mla_q': args['g_mla_q'][l][None], 'g_mla_kv': args['g_mla_kv'][l][None],
            'b_branch': args['b_branch_gate'][l][None], 'g_xa': args['g_xa'][l][None],
            'g_mem': args['g_mem'][l][None], 'g_mlp': args['g_mlp'][l][None]})

    first = _run_comm(over_ici(0, LATE), name='gather_ici_first_l0')
    w_now = _repack_layer_weights(whole(first, LATE, 'first_l0'))
    saved = []
    xl = x
    for l in range(DEPTH):
        carry_fox = over_ici(0, EARLY) if l == 0 else None
        after_fox = (lambda parts: whole(parts, EARLY, 'rest_l0')) if l == 0 else None
        carry_mla = over_ici(l + 1, names) if l + 1 < DEPTH else None
        xl, sv = _layer_fwd(xl, mem, w_now, layers_p[l], tabs, f'l{l}', carry_fox=carry_fox, after_fox=after_fox,
                            carry_mla=carry_mla)
        saved.append(sv)
        if carry_mla is not None:
            w_now = _repack_layer_weights(whole(sv.pop('carried_mla'), names, f'l{l + 1}'))
    dx, dg_final, loss_lanes = _loss_head(xl, target, args['g_final'][None])
    cidx = lax.axis_index('c')
    chip = 2 * lax.axis_index('x') + lax.axis_index('y')

    def pair_sums(gw, nms, tag):
        mine, theirs = [], []
        for nm in nms:
            shards = _split_full(gw[nm], axis_of[nm]).astype(BF16)
            h = shards.shape[1] // 2
            mine.append(lax.dynamic_slice_in_dim(shards, cidx * h, h, axis=1))
            theirs.append(lax.dynamic_slice_in_dim(shards, (1 - cidx) * h, h, axis=1))
        got = _to_sibling(theirs, name=f'grads_swap_{tag}')
        pairs = []
        for nm, a, b in zip(nms, mine, got):
            _, h, n = a.shape
            (p,) = _rowwise(lambda u, v: (u.astype(F32) + v.astype(F32),),
                            [a.reshape(N_CHIPS * h, n), b.reshape(N_CHIPS * h, n)], [], [(n, BF16)],
                            name=f'pair_sum_{nm}_{tag}')
            pairs.append(p.reshape(N_CHIPS, h, n))
        return pairs

    def finish(pairs, from_chips, nms, tag):
        own = [lax.dynamic_index_in_dim(p, chip, axis=0, keepdims=False) for p in pairs]
        mine = [_sum_chips(o, r, name=f'chip_sum_{nm}_{tag}') for nm, o, r in zip(nms, own, from_chips)]
        theirs = _to_sibling(mine, name=f'grads_join_{tag}')
        return {nm: jnp.where(cidx == 0, jnp.concatenate([a, b]), jnp.concatenate([b, a]))
                for nm, a, b in zip(nms, mine, theirs)}

    gs_layers, done = [None] * DEPTH, [{} for _ in range(DEPTH)]
    above = None
    for l in reversed(range(DEPTH)):
        lowest, early_pairs = l == 0, []

        def early(gw_early, l=l, early_pairs=early_pairs):
            early_pairs.extend(pair_sums(gw_early, EARLY, f'early_l{l}'))
            return _chip_exchange(early_pairs)

        carry_mla = None if above is None else _chip_exchange(above[1])
        dx, gw, gs_layers[l], got_mla, got_fox = _layer_bwd(
            dx, mem, saved[l]['w'], layers_p[l], tabs, saved[l], f'l{l}', carry_mla=carry_mla,
            early=early if lowest else None)
        if above is not None:
            done[above[0]].update(finish(above[1], got_mla, names, f'l{above[0]}'))
        grads = _unpack_layer_grads(gw)
        if lowest:
            done[l].update(finish(early_pairs, got_fox, EARLY, f'early_l{l}'))
            late_pairs = pair_sums(grads, LATE, f'late_l{l}')
            from_late = _run_comm(_chip_exchange(late_pairs), name=f'grads_exchange_late_l{l}')
            done[l].update(finish(late_pairs, from_late, LATE, f'late_l{l}'))
        else:
            above = (l, pair_sums(grads, names, f'l{l}'))
    grad_x = dx[None]
    gshard = {nm: jnp.stack([done[l][nm] for l in range(DEPTH)]) for nm in names}

    small_g = []
    for nm, key in (('g_mix', 'g_mix'), ('b_fox_forget', 'b_fox'), ('b_gla_gate', 'b_gla'),
                    ('g_gla_out', 'g_gla_out'), ('g_mla_q', 'g_mla_q'), ('g_mla_kv', 'g_mla_kv'),
                    ('b_branch_gate', 'b_branch'), ('g_xa', 'g_xa'), ('g_mem', 'g_mem'), ('g_mlp', 'g_mlp')):
        width = shapes[nm][1]
        small_g.append(jnp.concatenate([gs_layers[l][key][0, :width] for l in range(DEPTH)]))
    small_g.append(dg_final[0])
    small_g.append(loss_lanes[0, :1])
    flat = jnp.concatenate(small_g)
    n_small = flat.shape[0]
    srows = -(-n_small // (8 * LANES)) * 8
    pad = lambda v: jnp.pad(v, (0, srows * LANES - v.shape[0])).reshape(srows, LANES)
    all_small = _all_gather8(pad(flat), name='gather_small')
    sw, sm, svv = (pad(jnp.concatenate([args[pre + nm].reshape(-1) for nm in SMALL] + [jnp.zeros((1,), F32)]))
                   for pre in ('', 'm_', 'v_'))

    def small_body(g_ref, w_ref, m_ref, v_ref, go_ref, d_ref, mo_ref, vo_ref):
        g = g_ref[0]
        for q in range(1, N_DEV):
            g = g + g_ref[q]
        go_ref[...] = g
        d_ref[...], mo_ref[...], vo_ref[...] = _adam(w_ref[...], g, m_ref[...], v_ref[...])

    sg, sd, snm, snv = pl.pallas_call(
        small_body, name='small_sum_adam', out_shape=[jax.ShapeDtypeStruct((srows, LANES), F32)] * 4,
        compiler_params=pltpu.CompilerParams(vmem_limit_bytes=VMEM_LIMIT))(all_small, sw, sm, svv)

    def unsmall(buf):
        v, out, off = buf.reshape(-1), {}, 0
        for nm in SMALL:
            nel = math.prod(shapes[nm])
            out[nm] = v[off:off + nel].reshape(shapes[nm])
            off += nel
        return out, v[off]

    res = {}
    (res['grad'], loss), (res['delta'], _), (res['m'], _), (res['v'], _) = (unsmall(t) for t in (sg, sd, snm, snv))

    for nm, _ in BIG:
        shp = args[nm].shape
        view = lambda t: t.reshape(shp[0] * shp[1], shp[2])
        d, m2, v2 = _rowwise(_adam, [view(args[nm]), view(gshard[nm]), view(args['m_' + nm]), view(args['v_' + nm])],
                             [], [(shp[2], F32)] * 3, name=f'adam_{nm}')
        res['grad'][nm], res['delta'][nm], res['m'][nm], res['v'][nm] = (
            gshard[nm], d.reshape(shp), m2.reshape(shp), v2.reshape(shp))

    return (loss, grad_x, *[res['grad'][nm] for nm in ORDER], *[res['delta'][nm] for nm in ORDER],
            *[res['m'][nm] for nm in ORDER], *[res['v'][nm] for nm in ORDER])


def kernel(x, mem, g_mix, w_in, b_fox_forget, w_gla_gate, b_gla_gate, g_gla_out, g_mla_q, w_mla_uq, g_mla_kv, w_mla_ukv, b_branch_gate, w_up_fox, w_up_gla, w_up_mla, w_out, g_xa, g_mem, w_xq, w_xkv, w_xo, g_mlp, w_mlp1, w_mlp2, g_final, loss_target, m_g_mix, m_w_in, m_b_fox_forget, m_w_gla_gate, m_b_gla_gate, m_g_gla_out, m_g_mla_q, m_w_mla_uq, m_g_mla_kv, m_w_mla_ukv, m_b_branch_gate, m_w_up_fox, m_w_up_gla, m_w_up_mla, m_w_out, m_g_xa, m_g_mem, m_w_xq, m_w_xkv, m_w_xo, m_g_mlp, m_w_mlp1, m_w_mlp2, m_g_final, v_g_mix, v_w_in, v_b_fox_forget, v_w_gla_gate, v_b_gla_gate, v_g_gla_out, v_g_mla_q, v_w_mla_uq, v_g_mla_kv, v_w_mla_ukv, v_b_branch_gate, v_w_up_fox, v_w_up_gla, v_w_up_mla, v_w_out, v_g_xa, v_g_mem, v_w_xq, v_w_xkv, v_w_xo, v_g_mlp, v_w_mlp1, v_w_mlp2, v_g_final):
    return _step(dict(locals()))
```

```python
import functools
import math
import typing

import jax
import jax.numpy as jnp
from jax import lax
from jax.experimental import pallas as pl
from jax.experimental.pallas import tpu as pltpu

F32 = jnp.float32
BF16 = jnp.bfloat16
MESH = pl.DeviceIdType.MESH

D_MODEL = 1024
DEPTH = 2
CHUNK = 64
EPS = 1e-6
FOX_HEADS, FOX_HD = 4, 64
GLA_HEADS, GLA_DK, GLA_DV, GLA_RANK, GLA_TAU = 4, 64, 128, 16, 16.0
MLA_HEADS, MLA_Q_RANK, MLA_KV_RANK, MLA_NOPE, MLA_ROPE, MLA_VD = 4, 256, 128, 64, 32, 64
ROPE_BASE = 10000.0
XA_HEADS, XA_HD = 4, 128
D_FF = 4 * D_MODEL
IN_SIZES = (256, 256, 256, 4, 256, 256, 512, 16, 512, 256, 128, 32, 3072)
N_IN = sum(IN_SIZES)

ADAM_LR, ADAM_B1, ADAM_B2, ADAM_EPS, ADAM_WD, ADAM_STEP = 0.001, 0.9, 0.999, 1e-08, 0.01, 10

N_CHIPS = 4
N_DEV = 8
LANES = 128
VMEM_LIMIT = 48 * 1024 * 1024
MASK_VALUE = -1e30

BIG = (('w_in', 2), ('w_gla_gate', 2), ('w_mla_uq', 2), ('w_mla_ukv', 2), ('w_up_fox', 2), ('w_up_gla', 2),
       ('w_up_mla', 2), ('w_out', 1), ('w_xq', 1), ('w_xkv', 1), ('w_xo', 2), ('w_mlp1', 2), ('w_mlp2', 1))
SMALL = ('g_mix', 'b_fox_forget', 'b_gla_gate', 'g_gla_out', 'g_mla_q', 'g_mla_kv', 'b_branch_gate',
         'g_xa', 'g_mem', 'g_mlp', 'g_final')
ORDER = ('g_mix', 'w_in', 'b_fox_forget', 'w_gla_gate', 'b_gla_gate', 'g_gla_out', 'g_mla_q', 'w_mla_uq',
         'g_mla_kv', 'w_mla_ukv', 'b_branch_gate', 'w_up_fox', 'w_up_gla', 'w_up_mla', 'w_out', 'g_xa', 'g_mem',
         'w_xq', 'w_xkv', 'w_xo', 'g_mlp', 'w_mlp1', 'w_mlp2', 'g_final')


def _params(*sem):
    return pltpu.CompilerParams(dimension_semantics=sem, vmem_limit_bytes=VMEM_LIMIT)


def _sig(x):
    return 1.0 / (1.0 + jnp.exp(-x))


def _logsig(x):
    return jnp.minimum(x, 0.0) - jnp.log(1.0 + jnp.exp(-jnp.abs(x)))


NN = (((1,), (0,)), ((), ()))
NT = (((1,), (1,)), ((), ()))
TN = (((0,), (0,)), ((), ()))


def _dot(a, b, dims=NN):
    return lax.dot_general(a, b, dims, preferred_element_type=F32)


class Cols(typing.NamedTuple):
    arr: jax.Array
    width: int
    blk: int


def _tri_dot(tri, x):
    hi = x.astype(BF16)
    r1 = x - hi.astype(F32)
    mid = r1.astype(BF16)
    lo = (r1 - mid.astype(F32)).astype(BF16)
    return _dot(tri, hi) + _dot(tri, mid) + _dot(tri, lo)


MM_TILES = ((1024, 1024), (1024, 512), (512, 1024), (512, 512), (512, 256), (256, 512), (256, 256), (128, 128))
MM_VMEM_BUDGET = 38 * 1024 * 1024


def _mm_tiles(m, n, k, a_bytes, b_bytes, out_bytes, ex_bytes, has_norm, emit_norm, has_fn):
    for tm, tn in MM_TILES:
        tm, tn = min(tm, m), min(tn, n)
        if m % tm or n % tn:
            continue
        blocks = tm * k * a_bytes + k * tn * b_bytes + tm * tn * (out_bytes + ex_bytes) + (tm * k * 2 if emit_norm else 0)
        temps = tm * tn * 4 + (tm * k * 2 if has_norm else 0) + (tm * k * 6 if has_fn or has_norm else 0)
        if 2 * blocks + temps <= MM_VMEM_BUDGET:
            return tm, tn
    raise ValueError((m, n, k))


def _mm(a, b, *, mode, out_dtype, name, norm_g=None, emit_norm=False, a_fn=None, extras=(), epilogue=None,
        col_sums=False):
    a_blk = 0
    if isinstance(a, Cols):
        a, width, a_blk = a
        a_shape = (a.shape[0], width)
    else:
        a_shape = a.shape
    if mode == 'tn':
        k, m = a_shape
    else:
        m, k = a_shape
    n = b.shape[0] if mode == 'nt' else b.shape[1]
    assert (b.shape[1] if mode == 'nt' else b.shape[0]) == k, (name, a.shape, b.shape)
    has_norm = norm_g is not None
    ex_bytes = sum(arr.dtype.itemsize for arr, kind, _ in extras if kind == 'mn')
    tm, tn = _mm_tiles(m, n, k, a.dtype.itemsize, b.dtype.itemsize, jnp.dtype(out_dtype).itemsize, ex_bytes, has_norm,
                       emit_norm, a_fn is not None)
    assert all(col % tn == 0 for _, _, col in extras), (name, tn)
    assert a_blk == 0 or (mode == 'nn') or (mode == 'tn' and tm == m)
    assert not (col_sums and (has_norm or emit_norm))
    ij = (lambda f: lambda g0, g1: f(g1, g0)) if col_sums else (lambda f: f)
    spec = lambda blk, f: pl.BlockSpec(blk, ij(f))
    if mode == 'tn':
        a_spec = spec((k, tm), lambda i, j: (0, i + a_blk))
    else:
        a_spec = spec((tm, k), lambda i, j: (i, a_blk))
    b_spec = spec((tn, k), lambda i, j: (j, 0)) if mode == 'nt' else spec((k, tn), lambda i, j: (0, j))
    dims = {'nn': NN, 'nt': NT, 'tn': TN}[mode]
    assert not (has_norm and mode != 'nn')
    n_ex = len(extras)

    def body(*refs):
        a_ref, b_ref = refs[0], refs[1]
        pos = 2
        g_ref = None
        if has_norm:
            g_ref = refs[pos]
            pos += 1
        ex_refs = refs[pos:pos + n_ex]
        pos += n_ex
        o_ref = refs[pos]
        pos += 1
        h_ref = None
        if emit_norm:
            h_ref = refs[pos]
            pos += 1
        if has_norm:
            an_ref = refs[pos]

            @pl.when(pl.program_id(1) == 0)
            def _():
                xf = a_ref[...].astype(F32)
                y = xf * lax.rsqrt(jnp.mean(xf * xf, axis=-1, keepdims=True) + EPS) * g_ref[...]
                an_ref[...] = y.astype(BF16)
                if emit_norm:
                    h_ref[...] = y.astype(BF16)

            av = an_ref[...]
        else:
            av = a_ref[...]
            if a_fn is not None:
                av = a_fn(av)
            av = av.astype(BF16)
        acc = _dot(av, b_ref[...].astype(BF16), dims)
        if epilogue is not None:
            acc = epilogue(acc, *[r[...] for r in ex_refs])
        o_ref[...] = acc.astype(out_dtype)
        if col_sums:
            sum_ref = refs[pos]

            @pl.when(pl.program_id(1) == 0)
            def _():
                sum_ref[...] = jnp.zeros_like(sum_ref)

            sum_ref[...] += jnp.sum(acc, axis=0, keepdims=True)

    in_specs = [a_spec, b_spec]
    args = [a, b]
    if has_norm:
        in_specs.append(pl.BlockSpec((1, k), lambda i, j: (0, 0)))
        args.append(norm_g)
    for arr, kind, col in extras:
        if kind == 'mn':
            in_specs.append(spec((tm, tn), lambda i, j, o=col // tn: (i, j + o)))
        else:
            in_specs.append(spec((1, tn), lambda i, j, o=col // tn: (0, j + o)))
        args.append(arr)
    out_shape = [jax.ShapeDtypeStruct((m, n), out_dtype)]
    out_specs = [spec((tm, tn), lambda i, j: (i, j))]
    if emit_norm:
        out_shape.append(jax.ShapeDtypeStruct((m, k), BF16))
        out_specs.append(pl.BlockSpec((tm, k), lambda i, j: (i, 0)))
    if col_sums:
        out_shape.append(jax.ShapeDtypeStruct((1, n), F32))
        out_specs.append(spec((1, tn), lambda i, j: (0, j)))
    scratch = [pltpu.VMEM((tm, k), BF16)] if has_norm else []
    grid = (n // tn, m // tm) if col_sums else (m // tm, n // tn)
    res = pl.pallas_call(
        body, name=name, grid=grid, in_specs=in_specs, out_specs=out_specs, out_shape=out_shape,
        scratch_shapes=scratch, compiler_params=_params('arbitrary', 'arbitrary'))(*args)
    return res if emit_norm or col_sums else res[0]


def _mn(col_off=0):
    return 'mn', col_off


def _nvec(col_off=0):
    return 'n', col_off


def _rowwise(fn, rows, consts, outs, sums=(), *, name, ts=256):
    views = [x if isinstance(x, Cols) else Cols(x, x.shape[1], 0) for x in rows]
    rows = [v.arr for v in views]
    r = rows[0].shape[0]
    ts = min(ts, r)
    assert r % ts == 0, (name, r, ts)
    nr, nc, no, ns = len(rows), len(consts), len(outs), len(sums)

    def body(*refs):
        vals = fn(*[x[...] for x in refs[:nr + nc]])
        for q in range(no):
            refs[nr + nc + q][...] = vals[q].astype(outs[q][1])
        if ns:
            @pl.when(pl.program_id(0) == 0)
            def _():
                for q in range(ns):
                    refs[nr + nc + no + q][...] = jnp.zeros((1, sums[q]), F32)

            for q in range(ns):
                refs[nr + nc + no + q][...] += jnp.sum(vals[no + q].astype(F32), axis=0, keepdims=True)

    in_specs = [pl.BlockSpec((ts, v.width), lambda i, blk=v.blk: (i, blk)) for v in views]
    in_specs += [pl.BlockSpec(x.shape, lambda i, nd=x.ndim: (0,) * nd) for x in consts]
    out_specs = [pl.BlockSpec((ts, w), lambda i: (i, 0)) for w, _ in outs]
    out_specs += [pl.BlockSpec((1, w), lambda i: (0, 0)) for w in sums]
    out_shape = [jax.ShapeDtypeStruct((r, w), dt) for w, dt in outs]
    out_shape += [jax.ShapeDtypeStruct((1, w), F32) for w in sums]
    return pl.pallas_call(body, name=name, grid=(r // ts,), in_specs=in_specs, out_specs=out_specs,
                          out_shape=out_shape, compiler_params=_params('arbitrary'))(*rows, *consts)


def _cumsum_rows(x, *, reverse, name, bs=256):
    s, w = x.shape
    bs = min(bs, s)
    nb = s // bs

    def body(x_ref, o_ref, carry):
        @pl.when(pl.program_id(0) == 0)
        def _():
            carry[...] = jnp.zeros_like(carry)

        r = lax.broadcasted_iota(jnp.int32, (bs, bs), 0)
        c = lax.broadcasted_iota(jnp.int32, (bs, bs), 1)
        tri = jnp.where((c >= r) if reverse else (c <= r), 1.0, 0.0).astype(BF16)
        xv = x_ref[...]
        o_ref[...] = _tri_dot(tri, xv) + carry[...]
        carry[...] += jnp.sum(xv, axis=0, keepdims=True)

    imap = (lambda i: (nb - 1 - i, 0)) if reverse else (lambda i: (i, 0))
    return pl.pallas_call(body, name=name, grid=(nb,), in_specs=[pl.BlockSpec((bs, w), imap)],
                          out_specs=pl.BlockSpec((bs, w), imap), out_shape=jax.ShapeDtypeStruct((s, w), F32),
                          scratch_shapes=[pltpu.VMEM((1, w), F32)], compiler_params=_params('arbitrary'))(x)


def _mask(mode, q0, k0, bq, bk):
    qpos = q0 + lax.broadcasted_iota(jnp.int32, (bq, bk), 0)
    kpos = k0 + lax.broadcasted_iota(jnp.int32, (bq, bk), 1)
    if mode == 'causal':
        return kpos <= qpos
    return kpos < (jnp.right_shift(qpos, int(math.log2(CHUNK))) + 1) * CHUNK


ROPE_SHIFT = int(math.log2(MLA_ROPE))
FOX_SCALE, MLA_SCALE, XA_SCALE = FOX_HD ** -0.5, (MLA_NOPE + MLA_ROPE) ** -0.5, XA_HD ** -0.5
ATTN_ROW_SLAB = 512


def _lane_masks(g, b, rope):
    lane = lax.broadcasted_iota(jnp.int32, (1, LANES), 1)
    heads = [None if g == 1 else (lane >= hh * (LANES // g)) & (lane < (hh + 1) * (LANES // g)) for hh in range(g)]
    ropes = [jnp.right_shift(lane, ROPE_SHIFT) == b * g + hh for hh in range(g)] if rope else [None] * g
    return heads, ropes


def _sel(mask, x):
    return x if mask is None else jnp.where(mask, x, jnp.zeros_like(x))


class Step(typing.NamedTuple):
    qi: typing.Any
    kj: typing.Any
    first: typing.Any
    last: typing.Any
    plain: typing.Any
    masked: typing.Any


def _fwd_steps(tri, nq, nk):
    if not tri:
        return (nq, nk), lambda i, j: Step(i, j, j == 0, j == nk - 1, True, False)
    if nq % 2:
        return (nq, nk), lambda i, j: Step(i, jnp.minimum(i, j), j == 0, j == nk - 1, j < i, j == i)

    def at(i, t):
        low = t <= i
        diag = (t == i) | (t == nq)
        return Step(jnp.where(low, i, nq - 1 - i), jnp.where(low, t, t - (i + 1)), (t == 0) | (t == i + 1), diag,
                    jnp.logical_not(diag), diag)

    return (nq // 2, nq + 1), at


def _bwd_steps(tri, nq, nk):
    if not tri:
        return (nk, nq), lambda j, i: Step(i, j, i == 0, i == nq - 1, True, False)
    if nk % 2:
        return (nk, nq), lambda j, i: Step(jnp.maximum(i, j), j, i == 0, i == nq - 1, i > j, i == j)

    def at(j, t):
        n1 = nq - j
        low = t < n1
        diag = (t == 0) | (t == n1)
        return Step(jnp.where(low, j + t, nk - 1 - j + t - n1), jnp.where(low, j, nk - 1 - j), diag,
                    (t == n1 - 1) | (t == nq), jnp.logical_not(diag), diag)

    return (nk // 2, nq + 1), at


def _carried(comm, refs, n_in, n_out):
    ci, co = len(comm.ins), len(comm.out_shapes)
    ins = refs[n_in:n_in + ci]
    outs = refs[n_in + ci + n_out:n_in + ci + n_out + co]
    rest = refs[:n_in] + refs[n_in + ci:n_in + ci + n_out] + refs[n_in + ci + n_out + co:-2]
    return rest, (ins, outs, refs[-2], refs[-1])


def _mattn_fwd(q, k, v, *, qc, kc, vc, nb, g, mode, name, dq_scale=1.0, ck=None, qr=None, qrc=0, kr=None, blk=512,
               comm=None):
    s, t = q.shape[0], k.shape[0]
    bq, bk = min(blk, s), min(blk, t)
    nq, nk = s // bq, t // bk
    tri = mode != 'full'
    bias, rope = ck is not None, qr is not None
    assert not tri or (bq == bk and bq % CHUNK == 0)
    rs = min(ATTN_ROW_SLAB, bq)
    n_in = 3 + bias + 2 * rope
    (n1, n2), step_at = _fwd_steps(tri, nq, nk)

    def body(*refs):
        refs = list(refs)
        b, p1, p2 = pl.program_id(0), pl.program_id(1), pl.program_id(2)
        st = step_at(p1, p2)
        i, j = st.qi, st.kj
        if comm is not None:
            refs, comm_refs = _carried(comm, refs, n_in, 2)
            pl.when((b == 0) & (p1 == 0) & (p2 == 0))(lambda: comm.start(*comm_refs))
        q_ref, k_ref, v_ref = refs[:3]
        pos = 3
        ck_ref = qr_ref = kr_ref = None
        if bias:
            ck_ref = refs[pos]
            pos += 1
        if rope:
            qr_ref, kr_ref = refs[pos:pos + 2]
            pos += 2
        o_ref, lse_ref, m_s, l_s, acc_s = refs[pos:]
        heads, ropes = _lane_masks(g, b, rope)

        @pl.when(st.first)
        def _():
            m_s[...] = jnp.full_like(m_s, MASK_VALUE)
            l_s[...] = jnp.zeros_like(l_s)
            acc_s[...] = jnp.zeros_like(acc_s)

        def compute(masked):
            k2, v2 = k_ref[...], v_ref[...]
            for r in range(bq // rs):
                rows = pl.ds(r * rs, rs)
                q2 = q_ref[rows, :]
                alphas, pvs = [], []
                for hh in range(g):
                    sc = _dot(_sel(heads[hh], q2), k2, NT)
                    if rope:
                        sc = sc + _dot(_sel(ropes[hh], qr_ref[rows, :]), kr_ref[...], NT)
                    if bias:
                        sc = sc - ck_ref[0, hh:hh + 1, :]
                    if masked:
                        sc = jnp.where(_mask(mode, i * bq + r * rs, j * bk, rs, bk), sc, MASK_VALUE)
                    m_prev = m_s[hh, rows]
                    m_new = jnp.maximum(m_prev, jnp.max(sc, axis=1, keepdims=True))
                    alpha = jnp.exp(m_prev - m_new)
                    p = jnp.exp(sc - m_new)
                    l_s[hh, rows] = alpha * l_s[hh, rows] + jnp.sum(p, axis=1, keepdims=True)
                    m_s[hh, rows] = m_new
                    alphas.append(alpha)
                    pvs.append(_dot(p.astype(BF16), _sel(heads[hh], v2)))
                alpha = alphas[0]
                for hh in range(1, g):
                    alpha = jnp.where(heads[hh], alphas[hh], alpha)
                acc_s[rows, :] = acc_s[rows, :] * alpha + sum(pvs[1:], pvs[0])

        if tri:
            pl.when(st.plain)(functools.partial(compute, False))
            pl.when(st.masked)(functools.partial(compute, True))
        else:
            compute(False)

        @pl.when(st.last)
        def _():
            lane = lax.broadcasted_iota(jnp.int32, (bq, LANES), 1)
            l_full, lse = l_s[0], jnp.zeros((bq, LANES), F32)
            for hh in range(g):
                if hh:
                    l_full = jnp.where(heads[hh], l_s[hh], l_full)
                lse = jnp.where(lane == hh, m_s[hh] + jnp.log(l_s[hh]), lse)
            o_ref[...] = (acc_s[...] / l_full).astype(o_ref.dtype)
            lse_ref[...] = lse

        if comm is not None:
            pl.when((b == nb - 1) & (p1 == n1 - 1) & (p2 == n2 - 1))(lambda: comm.finish(*comm_refs))

    qi = lambda p1, p2: step_at(p1, p2).qi
    kj = lambda p1, p2: step_at(p1, p2).kj
    in_specs = [pl.BlockSpec((bq, LANES), lambda b, p1, p2: (qi(p1, p2), qc + b)),
                pl.BlockSpec((bk, LANES), lambda b, p1, p2: (kj(p1, p2), kc + b)),
                pl.BlockSpec((bk, LANES), lambda b, p1, p2: (kj(p1, p2), vc + b))]
    args = [q, k, v]
    if bias:
        in_specs.append(pl.BlockSpec((1, 8, bk), lambda b, p1, p2: (b, 0, kj(p1, p2))))
        args.append(ck)
    if rope:
        in_specs += [pl.BlockSpec((bq, LANES), lambda b, p1, p2: (qi(p1, p2), qrc)),
                     pl.BlockSpec((bk, LANES), lambda b, p1, p2: (kj(p1, p2), 0))]
        args += [qr, kr]
    out = pl.BlockSpec((bq, LANES), lambda b, p1, p2: (qi(p1, p2), b))
    out_specs = [out, out]
    out_shape = [jax.ShapeDtypeStruct((s, LANES * nb), BF16), jax.ShapeDtypeStruct((s, LANES * nb), F32)]
    scratch = [pltpu.VMEM((g, bq, 1), F32), pltpu.VMEM((g, bq, 1), F32), pltpu.VMEM((bq, LANES), F32)]
    if comm is not None:
        in_specs += [ANY] * len(comm.ins)
        args += comm.ins
        out_specs += [ANY] * len(comm.out_shapes)
        out_shape += comm.out_shapes
        scratch += _sems(comm.n_sems, comm.n_sems)
    res = pl.pallas_call(body, name=name, grid=(nb, n1, n2), in_specs=in_specs, out_specs=out_specs, out_shape=out_shape,
                         scratch_shapes=scratch, compiler_params=_params('arbitrary', 'arbitrary', 'arbitrary'))(*args)
    return res if comm is None else (res[0], res[1], res[2:])


def _mattn_bwd(q, k, v, o, do, lse, *, qc, kc, vc, nb, g, mode, name, dq_scale=1.0, ck=None, qr=None, qrc=0, kr=None,
               blk=512, comm=None):
    s, t = q.shape[0], k.shape[0]
    bq, bk = min(blk, s), min(blk, t)
    nq, nk = s // bq, t // bk
    tri = mode != 'full'
    bias, rope = ck is not None, qr is not None
    rs = min(ATTN_ROW_SLAB, bq)
    n_in, n_out = 6 + bias + 2 * rope, 3 + 2 * bias + 2 * rope
    (n1, n2), step_at = _bwd_steps(tri, nq, nk)

    def body(*refs):
        refs = list(refs)
        if comm is not None:
            refs, comm_refs = _carried(comm, refs, n_in, n_out)
            first = (pl.program_id(0) == 0) & (pl.program_id(1) == 0) & (pl.program_id(2) == 0)
            pl.when(first)(lambda: comm.start(*comm_refs))
        q_ref, k_ref, v_ref, o_ref, do_ref, lse_ref = refs[:6]
        pos = 6
        ck_ref = qr_ref = kr_ref = dck_ref = dcq_ref = dqr_ref = dkr_ref = dck_s = None
        if bias:
            ck_ref = refs[pos]
            pos += 1
        if rope:
            qr_ref, kr_ref = refs[pos:pos + 2]
            pos += 2
        dq_ref, dk_ref, dv_ref = refs[pos:pos + 3]
        pos += 3
        if bias:
            dck_ref, dcq_ref = refs[pos:pos + 2]
            pos += 2
        if rope:
            dqr_ref, dkr_ref = refs[pos:pos + 2]
            pos += 2
        dk_s, dv_s = refs[pos:pos + 2]
        if bias:
            dck_s = refs[pos + 2]
        b, p1, p2 = pl.program_id(0), pl.program_id(1), pl.program_id(2)
        st = step_at(p1, p2)
        i, j = st.qi, st.kj
        heads, ropes = _lane_masks(g, b, rope)

        @pl.when((p1 == 0) & (p2 == 0))
        def _():
            dq_ref[...] = jnp.zeros_like(dq_ref)
            if bias:
                dcq_ref[...] = jnp.zeros_like(dcq_ref)

        if rope:
            @pl.when((b == 0) & (p1 == 0) & (p2 == 0))
            def _():
                dqr_ref[...] = jnp.zeros_like(dqr_ref)
                dkr_ref[...] = jnp.zeros_like(dkr_ref)

        @pl.when(st.first)
        def _():
            dk_s[...] = jnp.zeros_like(dk_s)
            dv_s[...] = jnp.zeros_like(dv_s)
            if bias:
                dck_s[...] = jnp.zeros_like(dck_s)

        def compute(masked):
            k2, v2 = k_ref[...], v_ref[...]
            lane = lax.broadcasted_iota(jnp.int32, (rs, LANES), 1)
            rk = pl.ds(pl.multiple_of(j * bk, bk), bk)
            add = lambda tot, x: x if tot is None else tot + x
            dv_t = dk_t = dkr_t = None
            dck_t = [None] * g
            for r in range(bq // rs):
                rows = pl.ds(r * rs, rs)
                rq = pl.ds(pl.multiple_of(i * bq + r * rs, rs), rs)
                q2, do2, lse2 = q_ref[rows, :], do_ref[rows, :], lse_ref[rows, :]
                dd = do2.astype(F32) * o_ref[rows, :].astype(F32)
                dq_t = dqr_t = dcq_t = None
                for hh in range(g):
                    qm = _sel(heads[hh], q2)
                    sc = _dot(qm, k2, NT)
                    if rope:
                        qrm = _sel(ropes[hh], qr_ref[rows, :])
                        sc = sc + _dot(qrm, kr_ref[...], NT)
                    if bias:
                        sc = sc - ck_ref[0, hh:hh + 1, :]
                    if masked:
                        sc = jnp.where(_mask(mode, i * bq + r * rs, j * bk, rs, bk), sc, MASK_VALUE)
                    p = jnp.exp(sc - jnp.sum(jnp.where(lane == hh, lse2, 0.0), axis=1, keepdims=True))
                    dom = _sel(heads[hh], do2)
                    dp = _dot(dom, v2, NT)
                    delta = jnp.sum(_sel(heads[hh], dd), axis=1, keepdims=True)
                    ds = p * (dp - delta)
                    dsb = ds.astype(BF16)
                    dv_t = add(dv_t, _dot(p.astype(BF16), dom, TN))
                    dk_t = add(dk_t, _dot(dsb, qm, TN))
                    dq_t = add(dq_t, _dot(dsb, _sel(heads[hh], k2)))
                    if rope:
                        dqr_t = add(dqr_t, _dot(dsb, _sel(ropes[hh], kr_ref[...])))
                        dkr_t = add(dkr_t, _dot(dsb, qrm, TN))
                    if bias:
                        dck_t[hh] = add(dck_t[hh], jnp.sum(ds, axis=0, keepdims=True))
                        dcq_t = add(dcq_t, jnp.where(lane == hh, jnp.sum(ds, axis=1, keepdims=True), 0.0))
                dq_ref[rq, :] += dq_t if dq_scale == 1.0 else dq_scale * dq_t
                if rope:
                    dqr_ref[rq, :] += dq_scale * dqr_t
                if bias:
                    dcq_ref[rq, :] += dcq_t
            dv_s[...] += dv_t
            dk_s[...] += dk_t
            if rope:
                dkr_ref[rk, :] += dkr_t
            if bias:
                for hh in range(g):
                    dck_s[hh:hh + 1, :] -= dck_t[hh]

        if tri:
            pl.when(st.plain)(functools.partial(compute, False))
            pl.when(st.masked)(functools.partial(compute, True))
        else:
            compute(False)

        @pl.when(st.last)
        def _():
            dk_ref[...] = dk_s[...]
            dv_ref[...] = dv_s[...]
            if bias:
                dck_ref[0] = dck_s[...]

        if comm is not None:
            pl.when((b == nb - 1) & (p1 == n1 - 1) & (p2 == n2 - 1))(lambda: comm.finish(*comm_refs))

    qrow = lambda col: pl.BlockSpec((bq, LANES), lambda b, p1, p2: (step_at(p1, p2).qi, col(b)))
    krow = lambda col: pl.BlockSpec((bk, LANES), lambda b, p1, p2: (step_at(p1, p2).kj, col(b)))
    in_specs = [qrow(lambda b: qc + b), krow(lambda b: kc + b), krow(lambda b: vc + b), qrow(lambda b: b),
                qrow(lambda b: b), qrow(lambda b: b)]
    args = [q, k, v, o, do, lse]
    whole = lambda rows: pl.BlockSpec((rows, LANES), lambda b, j, i: (0, b))
    out_specs = [whole(s), krow(lambda b: b), krow(lambda b: b)]
    out_shape = [jax.ShapeDtypeStruct((s, LANES * nb), F32), jax.ShapeDtypeStruct((t, LANES * nb), F32),
                 jax.ShapeDtypeStruct((t, LANES * nb), F32)]
    scratch = [pltpu.VMEM((bk, LANES), F32), pltpu.VMEM((bk, LANES), F32)]
    if bias:
        ckj = pl.BlockSpec((1, 8, bk), lambda b, p1, p2: (b, 0, step_at(p1, p2).kj))
        in_specs.append(ckj)
        args.append(ck)
        out_specs += [ckj, whole(s)]
        out_shape += [jax.ShapeDtypeStruct((nb, 8, t), F32), jax.ShapeDtypeStruct((s, LANES * nb), F32)]
    if rope:
        in_specs += [qrow(lambda b: qrc), krow(lambda b: 0)]
        args += [qr, kr]
        out_specs += [pl.BlockSpec((s, LANES), lambda b, j, i: (0, 0)), pl.BlockSpec((t, LANES), lambda b, j, i: (0, 0))]
        out_shape += [jax.ShapeDtypeStruct((s, LANES), F32), jax.ShapeDtypeStruct((t, LANES), F32)]
    if bias:
        scratch.append(pltpu.VMEM((8, bk), F32))
    if comm is not None:
        in_specs += [ANY] * len(comm.ins)
        args += comm.ins
        out_specs += [ANY] * len(comm.out_shapes)
        out_shape += comm.out_shapes
        scratch += _sems(comm.n_sems, comm.n_sems)
    res = pl.pallas_call(body, name=name, grid=(nb, n1, n2), in_specs=in_specs, out_specs=out_specs,
                         out_shape=out_shape, scratch_shapes=scratch,
                         compiler_params=_params('arbitrary', 'arbitrary', 'arbitrary'))(*args)
    return res if comm is None else (*res[:n_out], res[n_out:])


def _gla_chunk(la_c, k_c):
    r = lax.broadcasted_iota(jnp.int32, (CHUNK, CHUNK), 0)
    c = lax.broadcasted_iota(jnp.int32, (CHUNK, CHUNK), 1)
    tri = jnp.where(c <= r, 1.0, 0.0).astype(BF16)
    cum = _tri_dot(tri, la_c)
    end = jnp.sum(la_c, axis=0, keepdims=True)
    dec = jnp.exp(end - cum)
    return dec, k_c * dec, jnp.exp(end)


GLA_PAIRS = GLA_HEADS // 2


def _gla_fwd(z, la, *, qc, kc, vc, name, blk=512):
    s = z.shape[0]
    bs = min(blk, s)
    ncb = bs // CHUNK
    nblk = s // bs

    def body(q_ref, k_ref, va_ref, vb_ref, la_ref, o_ref, st_ref, st):
        @pl.when(pl.program_id(1) == 0)
        def _():
            st[...] = jnp.zeros_like(st)

        heads, _ = _lane_masks(2, 0, False)
        v_refs = (va_ref, vb_ref)
        for c in range(ncb):
            sl = pl.ds(c * CHUNK, CHUNK)
            _, kf, a = _gla_chunk(la_ref[sl, :], k_ref[sl, :])
            qs = q_ref[sl, :] * (GLA_DK ** -0.5)
            for hh in range(2):
                ut = _dot(v_refs[hh][sl, :].astype(BF16), _sel(heads[hh], kf).astype(BF16), TN)
                new = a * st[hh] + ut
                st[hh] = new
                st_ref[0, c, hh] = new
                o_ref[sl, hh * GLA_DV:(hh + 1) * GLA_DV] = _dot(_sel(heads[hh], qs).astype(BF16), new.astype(BF16), NT)

    col = lambda c0, m=1: pl.BlockSpec((bs, LANES), lambda b, i: (i, c0 + m * b))
    return pl.pallas_call(
        body, name=name, grid=(GLA_PAIRS, nblk),
        in_specs=[col(qc), col(kc), col(vc, 2), col(vc + 1, 2), col(0)],
        out_specs=[pl.BlockSpec((bs, 2 * GLA_DV), lambda b, i: (i, b)),
                   pl.BlockSpec((1, ncb, 2, GLA_DV, LANES), lambda b, i: (b, i, 0, 0, 0))],
        out_shape=[jax.ShapeDtypeStruct((s, GLA_HEADS * GLA_DV), F32),
                   jax.ShapeDtypeStruct((GLA_PAIRS, s // CHUNK, 2, GLA_DV, LANES), F32)],
        scratch_shapes=[pltpu.VMEM((2, GLA_DV, LANES), F32)],
        compiler_params=_params('arbitrary', 'arbitrary'))(z, z, z, z, la)


def _gla_bwd(z, la, st_all, st_prev, do, *, qc, kc, vc, name, blk=512):
    s = z.shape[0]
    bs = min(blk, s)
    ncb = bs // CHUNK
    nblk = s // bs

    def body(q_ref, k_ref, va_ref, vb_ref, la_ref, st_ref, sp_ref, do_ref, dq_ref, dk_ref, dv_ref, dla_ref, ga):
        @pl.when(pl.program_id(1) == 0)
        def _():
            ga[...] = jnp.zeros_like(ga)

        r = lax.broadcasted_iota(jnp.int32, (CHUNK, CHUNK), 0)
        cc = lax.broadcasted_iota(jnp.int32, (CHUNK, CHUNK), 1)
        tri_rev = jnp.where(cc >= r, 1.0, 0.0).astype(BF16)
        heads, _ = _lane_masks(2, 0, False)
        v_refs = (va_ref, vb_ref)
        for c in reversed(range(ncb)):
            sl = pl.ds(c * CHUNK, CHUNK)
            dec, kf, a = _gla_chunk(la_ref[sl, :], k_ref[sl, :])
            qs = q_ref[sl, :] * (GLA_DK ** -0.5)
            dq2 = jnp.zeros((CHUNK, LANES), F32)
            dkd = jnp.zeros((CHUNK, LANES), F32)
            da = jnp.zeros((1, LANES), F32)
            for hh in range(2):
                hv = slice(hh * GLA_DV, (hh + 1) * GLA_DV)
                dob = do_ref[sl, hv].astype(BF16)
                g = _dot(dob, _sel(heads[hh], qs).astype(BF16), TN) + ga[hh]
                gb = g.astype(BF16)
                dq2 = dq2 + _dot(dob, st_ref[0, c, hh].astype(BF16))
                dv_ref[sl, hv] = _dot(_sel(heads[hh], kf).astype(BF16), gb, NT)
                dkd = dkd + _dot(v_refs[hh][sl, :].astype(BF16), gb)
                da = da + jnp.sum(g * sp_ref[0, c, hh], axis=0, keepdims=True)
                ga[hh] = a * g
            dq_ref[sl, :] = (GLA_DK ** -0.5) * dq2
            dk_ref[sl, :] = dkd * dec
            e = dkd * kf
            dend = jnp.sum(e, axis=0, keepdims=True) + da * a
            dla_ref[sl, :] = dend - _tri_dot(tri_rev, e)

    rev = lambda i: nblk - 1 - i
    col = lambda c0, m=1: pl.BlockSpec((bs, LANES), lambda b, i: (rev(i), c0 + m * b))
    wide = pl.BlockSpec((bs, 2 * GLA_DV), lambda b, i: (rev(i), b))
    stspec = pl.BlockSpec((1, ncb, 2, GLA_DV, LANES), lambda b, i: (b, rev(i), 0, 0, 0))
    return pl.pallas_call(
        body, name=name, grid=(GLA_PAIRS, nblk),
        in_specs=[col(qc), col(kc), col(vc, 2), col(vc + 1, 2), col(0), stspec, stspec, wide],
        out_specs=[col(0), col(0), wide, col(0)],
        out_shape=[jax.ShapeDtypeStruct((s, GLA_HEADS * GLA_DK), F32), jax.ShapeDtypeStruct((s, GLA_HEADS * GLA_DK), F32),
                   jax.ShapeDtypeStruct((s, GLA_HEADS * GLA_DV), F32), jax.ShapeDtypeStruct((s, GLA_HEADS * GLA_DK), F32)],
        scratch_shapes=[pltpu.VMEM((2, GLA_DV, LANES), F32)],
        compiler_params=_params('arbitrary', 'arbitrary'))(z, z, z, z, la, st_all, st_prev, do)


def _place():
    return lax.axis_index('x'), lax.axis_index('y'), lax.axis_index('c')


ANY = pl.BlockSpec(memory_space=pl.ANY)


def _all_gather8(blk, *, name):
    m, n = blk.shape

    def body(x_ref, out_ref, send_sems, recv_sems, local_sem):
        x, y, c = _place()
        me, sibling = (x, y, c), (x, y, 1 - c)
        chips = [(1 - x, y), (x, 1 - y), (1 - x, 1 - y)]

        def slot(px, py, pc):
            return out_ref.at[4 * px + 2 * py + pc]

        def copy(q, block, to, src=None):
            return pltpu.make_async_remote_copy(
                src_ref=slot(*block) if src is None else src, dst_ref=slot(*block), send_sem=send_sems.at[q],
                recv_sem=recv_sems.at[q], device_id=to, device_id_type=MESH)

        mine = pltpu.make_async_copy(x_ref, slot(*me), local_sem)
        mine.start()
        first = [copy(0, me, sibling, src=x_ref)]
        first += [copy(1 + q, me, (*chip, c), src=x_ref) for q, chip in enumerate(chips)]
        for cp in first:
            cp.start()
        passed = [copy(4 + q, (*chip, c), sibling) for q, chip in enumerate(chips)]
        for q, chip in enumerate(chips):
            copy(1 + q, (*chip, c), me).wait_recv()
            passed[q].start()
        copy(0, sibling, me).wait_recv()
        for q, chip in enumerate(chips):
            copy(4 + q, (*chip, 1 - c), me).wait_recv()
        for cp in first + passed:
            cp.wait_send()
        mine.wait()

    return pl.pallas_call(
        body, name=name, in_specs=[ANY], out_specs=ANY, out_shape=jax.ShapeDtypeStruct((N_DEV, m, n), blk.dtype),
        scratch_shapes=[pltpu.SemaphoreType.DMA((7,)), pltpu.SemaphoreType.DMA((7,)), pltpu.SemaphoreType.DMA(())],
    )(blk)


def _sems(*counts):
    return [pltpu.SemaphoreType.DMA((n,)) for n in counts]


class Comm(typing.NamedTuple):
    ins: list
    out_shapes: list
    n_sems: int
    start: typing.Callable
    finish: typing.Callable


def _remote(src, dst, send_sems, recv_sems, idx, to):
    return lambda: pltpu.make_async_remote_copy(src_ref=src, dst_ref=dst, send_sem=send_sems.at[idx],
                                                recv_sem=recv_sems.at[idx], device_id=to, device_id_type=MESH)


def _comm_from(copies, ins, out_shapes, n_sems):
    def start(*refs):
        for cp in copies(*refs)[0]:
            cp().start()

    def finish(*refs):
        sent, received = copies(*refs)
        for cp in received:
            cp().wait_recv()
        for cp in sent:
            cp().wait_send()

    return Comm(list(ins), list(out_shapes), n_sems, start, finish)


def _run_comm(comm, *, name, alias=False):
    n_in, n_out = len(comm.ins), len(comm.out_shapes)

    def body(*refs):
        ins, outs, sems = refs[:n_in], refs[n_in:n_in + n_out], refs[n_in + n_out:]
        comm.start(ins, outs, *sems)
        comm.finish(ins, outs, *sems)

    return pl.pallas_call(body, name=name, in_specs=[ANY] * n_in, out_specs=[ANY] * n_out, out_shape=comm.out_shapes,
                          input_output_aliases={q: q for q in range(n_in)} if alias else {},
                          scratch_shapes=_sems(comm.n_sems, comm.n_sems))(*comm.ins)


def _half(rows, c):
    h = rows // 2
    return pl.ds(pl.multiple_of(c * h, h), h)


def _gathered(ref, chip, rows, side):
    if not side:
        return ref.at[chip, rows]
    n = ref.shape[1] // N_CHIPS
    return ref.at[rows, pl.ds(pl.multiple_of(chip * n, n), n)]


def _gather_over_ici(ws, side):
    def copies(ins, outs, send_sems, recv_sems):
        x, y, c = _place()
        me_chip = 2 * x + y
        sent, received = [], []
        for q, w in enumerate(ws):
            half, every = _half(w.shape[0], c), pl.ds(0, w.shape[0])
            for k, (px, py) in enumerate([(1 - x, y), (x, 1 - y), (1 - x, 1 - y)]):
                sent.append(_remote(ins[q].at[half], _gathered(outs[q], me_chip, half, side[q]), send_sems, recv_sems,
                                    4 * q + k, (px, py, c)))
                slot = _gathered(outs[q], 2 * px + py, half, side[q])
                received.append(_remote(slot, slot, send_sems, recv_sems, 4 * q + k, (px, py, c)))
            whole = _remote(ins[q], _gathered(outs[q], me_chip, every, side[q]), send_sems, recv_sems, 4 * q + 3,
                            (x, y, 1 - c))
            sent.append(whole)
            received.append(whole)
        return sent, received

    shapes = [jax.ShapeDtypeStruct((w.shape[0], N_CHIPS * w.shape[1]) if sd else (N_CHIPS,) + w.shape, w.dtype)
              for w, sd in zip(ws, side)]
    return _comm_from(copies, ws, shapes, 4 * len(ws))


def _gather_over_d2d(parts, side):
    def copies(ins, outs, send_sems, recv_sems):
        x, y, c = _place()
        sent, received = [], []
        for q, w in enumerate(parts):
            rows = w.shape[0] if side[q] else w.shape[1]
            for k, (px, py) in enumerate([(1 - x, y), (x, 1 - y), (1 - x, 1 - y)]):
                mine = _gathered(outs[q], 2 * px + py, _half(rows, c), side[q])
                theirs = _gathered(outs[q], 2 * px + py, _half(rows, 1 - c), side[q])
                sent.append(_remote(mine, mine, send_sems, recv_sems, 3 * q + k, (x, y, 1 - c)))
                received.append(_remote(theirs, theirs, send_sems, recv_sems, 3 * q + k, (x, y, 1 - c)))
        return sent, received

    return _comm_from(copies, parts, [jax.ShapeDtypeStruct(w.shape, w.dtype) for w in parts], 3 * len(parts))


def _to_sibling(gs, *, name):
    n = len(gs)

    def body(*refs):
        ins, outs = refs[:n], refs[n:2 * n]
        send_sems, recv_sems = refs[2 * n:]
        x, y, c = _place()
        cps = [pltpu.make_async_remote_copy(
            src_ref=ins[q], dst_ref=outs[q], send_sem=send_sems.at[q], recv_sem=recv_sems.at[q],
            device_id=(x, y, 1 - c), device_id_type=MESH) for q in range(n)]
        for cp in cps:
            cp.start()
        for cp in cps:
            cp.wait()

    return pl.pallas_call(body, name=name, in_specs=[ANY] * n, out_specs=[ANY] * n,
                          out_shape=[jax.ShapeDtypeStruct(g.shape, g.dtype) for g in gs],
                          scratch_shapes=_sems(n, n))(*gs)


def _chip_exchange(ps):
    def copies(ins, outs, send_sems, recv_sems):
        x, y, c = _place()
        cps = [_remote(ins[q].at[2 * px + py], outs[q].at[k], send_sems, recv_sems, 3 * q + k, (px, py, c))
               for q in range(len(ps)) for k, (px, py) in enumerate([(1 - x, y), (x, 1 - y), (1 - x, 1 - y)])]
        return cps, cps

    return _comm_from(copies, ps, [jax.ShapeDtypeStruct((3,) + p.shape[1:], p.dtype) for p in ps], 3 * len(ps))


def _sum_chips(own, r, *, name, ts=256):
    k, n = own.shape
    ts = min(ts, k)

    def body(own_ref, r_ref, o_ref):
        f = lambda q: r_ref[q].astype(F32)
        o_ref[...] = ((own_ref[...].astype(F32) + f(0)) + f(1)) + f(2)

    return pl.pallas_call(
        body, name=name, grid=(k // ts,),
        in_specs=[pl.BlockSpec((ts, n), lambda i: (i, 0)), pl.BlockSpec((3, ts, n), lambda i: (0, i, 0))],
        out_specs=pl.BlockSpec((ts, n), lambda i: (i, 0)), out_shape=jax.ShapeDtypeStruct((k, n), F32),
        compiler_params=_params('arbitrary'))(own, r)


WIN_SHARD = N_IN // N_CHIPS
WIN_PAD = -(-WIN_SHARD // LANES) * LANES
GATE_WIRE_ROWS = 32


def _full_layer(sh, axis):
    _, k, n = sh.shape
    if axis == 2:
        return sh.transpose(1, 0, 2).reshape(k, N_CHIPS * n)
    return sh.reshape(N_CHIPS * k, n)


def _win_cols(wp, o, n):
    parts = []
    while n > 0:
        j, r = divmod(o, WIN_SHARD)
        take = min(n, WIN_SHARD - r)
        parts.append(wp[:, j * WIN_PAD + r:j * WIN_PAD + r + take])
        o, n = o + take, n - take
    return parts[0] if len(parts) == 1 else jnp.concatenate(parts, axis=1)


def _split_full(full, axis):
    k, n = full.shape
    if axis == 2:
        return jnp.stack([full[:, j * (n // N_CHIPS):(j + 1) * (n // N_CHIPS)] for j in range(N_CHIPS)])
    return full.reshape(N_CHIPS, k // N_CHIPS, n)


def _padc(a, w):
    return jnp.pad(a, ((0, 0), (0, w - a.shape[1])))


def _swap16(a):
    return jnp.concatenate([a[..., 16:32], a[..., 0:16]], axis=-1)


B_GR, B_GQ, B_GK, B_GV, B_MQ, B_MKR, B_MKRS, B_FF, B_GLOW, B_MKV, B_END = (
    0, 512, 768, 1024, 1536, 1792, 1920, 2048, 2176, 2304, 2432)
B_W = 2560
O_FQ, O_FF, O_GQ, O_GLOW, O_GR, O_MQ, O_MKV, O_MKR, O_ZG = 0, 768, 772, 1796, 1812, 2324, 2580, 2708, 2740


def _repack_layer_weights(w):
    wi = functools.partial(_win_cols, w['w_in'])
    out = dict(w)
    out['in_a'] = jnp.concatenate([wi(O_FQ, 256) * FOX_SCALE, wi(O_FQ + 256, 512)], axis=1)
    kr = wi(O_MKR, 32)
    out['in_b'] = jnp.concatenate([
        wi(O_GR, 512), wi(O_GQ, 1024), wi(O_MQ, 256), jnp.tile(kr, (1, MLA_HEADS)), jnp.tile(_swap16(kr), (1, MLA_HEADS)),
        _padc(wi(O_FF, 4), 128), _padc(wi(O_GLOW, 16), 128), wi(O_MKV, 128),
        jnp.zeros((D_MODEL, B_W - B_END), kr.dtype)], axis=1)
    out['in_c'] = wi(O_ZG, 3072)
    uq = w['w_mla_uq'].reshape(MLA_Q_RANK, MLA_HEADS, MLA_NOPE + MLA_ROPE)
    rope = uq[:, :, MLA_NOPE:]
    out['uq'] = jnp.concatenate([uq[:, :, :MLA_NOPE].reshape(MLA_Q_RANK, -1), rope.reshape(MLA_Q_RANK, -1),
                                 _swap16(rope).reshape(MLA_Q_RANK, -1)], axis=1)
    ukv = w['w_mla_ukv'].reshape(MLA_KV_RANK, MLA_HEADS, MLA_NOPE + MLA_VD)
    out['ukv'] = jnp.concatenate([ukv[:, :, :MLA_NOPE].reshape(MLA_KV_RANK, -1),
                                  ukv[:, :, MLA_NOPE:].reshape(MLA_KV_RANK, -1)], axis=1)
    out['gate'] = jnp.pad(w['w_gla_gate'], ((0, 128 - GLA_RANK), (0, 0)))
    return out


def _unpack_layer_grads(g):
    a, b, c = g['in_a'], g['in_b'], g['in_c']
    a = jnp.concatenate([a[:, :256] * FOX_SCALE, a[:, 256:]], axis=1)
    fold = lambda o: sum(b[:, o + MLA_ROPE * q:o + MLA_ROPE * (q + 1)] for q in range(MLA_HEADS))
    kr = fold(B_MKR) + _swap16(fold(B_MKRS))
    w_in = jnp.concatenate([a, b[:, B_FF:B_FF + 4], b[:, B_GQ:B_GQ + 1024], b[:, B_GLOW:B_GLOW + 16],
                            b[:, B_GR:B_GR + 512], b[:, B_MQ:B_MQ + 256], b[:, B_MKV:B_MKV + 128], kr, c], axis=1)
    uq = g['uq']
    nope = uq[:, :256].reshape(MLA_Q_RANK, MLA_HEADS, MLA_NOPE)
    rope = (uq[:, 256:384].reshape(MLA_Q_RANK, MLA_HEADS, MLA_ROPE)
            + _swap16(uq[:, 384:512].reshape(MLA_Q_RANK, MLA_HEADS, MLA_ROPE)))
    w_uq = jnp.concatenate([nope, rope], axis=2).reshape(MLA_Q_RANK, -1)
    ukv = g['ukv']
    w_ukv = jnp.concatenate([ukv[:, :256].reshape(MLA_KV_RANK, MLA_HEADS, MLA_NOPE),
                             ukv[:, 256:].reshape(MLA_KV_RANK, MLA_HEADS, MLA_VD)], axis=2).reshape(MLA_KV_RANK, -1)
    out = {'w_in': w_in, 'w_mla_uq': w_uq, 'w_mla_ukv': w_ukv, 'w_gla_gate': g['gate'][:GLA_RANK]}
    for nm in ('w_up_fox', 'w_up_gla', 'w_up_mla', 'w_out', 'w_xq', 'w_xkv', 'w_xo', 'w_mlp1', 'w_mlp2'):
        out[nm] = g[nm]
    return out


def _rope_tables(s):
    half = MLA_ROPE // 2
    inv = ROPE_BASE ** (-jnp.arange(half, dtype=F32) / half)
    ang = jnp.arange(s).astype(F32)[:, None] * inv[None, :]
    cos, sin = jnp.cos(ang), jnp.sin(ang)
    c1 = jnp.concatenate([cos, cos], axis=1)
    s1 = jnp.concatenate([-sin, sin], axis=1)
    return jnp.tile(c1, (1, MLA_HEADS)), jnp.tile(s1, (1, MLA_HEADS))


def _rms_bwd(x, dh, g):
    r = lax.rsqrt(jnp.mean(x * x, axis=-1, keepdims=True) + EPS)
    xh = x * r
    gd = dh * g
    return r * (gd - xh * jnp.mean(gd * xh, axis=-1, keepdims=True)), dh * xh


def _norm_bwd_call(x, dh, g, dres, name):
    w = x.width if isinstance(x, Cols) else x.shape[1]

    def with_res(xv, dv, rv, gv):
        dx, dg = _rms_bwd(xv, dv.astype(F32), gv)
        return rv + dx, dg

    def plain(xv, dv, gv):
        return _rms_bwd(xv, dv.astype(F32), gv)

    if dres is None:
        return _rowwise(plain, [x, dh], [g], [(w, F32)], [w], name=name)
    return _rowwise(with_res, [x, dh, dres], [g], [(w, F32)], [w], name=name)


def _gla_out_fwd(oraw, gr, g_out):
    outs = []
    for hh in range(GLA_HEADS):
        sl = slice(hh * GLA_DV, (hh + 1) * GLA_DV)
        oh = oraw[:, sl]
        n = oh * lax.rsqrt(jnp.mean(oh * oh, axis=-1, keepdims=True) + EPS) * g_out
        r = gr[:, sl]
        outs.append(n * (r * _sig(r)))
    return (jnp.concatenate(outs, axis=1),)


def _gla_out_bwd(oraw, gr, dout, g_out):
    d_o, d_r, dg = [], [], 0.0
    for hh in range(GLA_HEADS):
        sl = slice(hh * GLA_DV, (hh + 1) * GLA_DV)
        oh, r, do = oraw[:, sl], gr[:, sl], dout[:, sl].astype(F32)
        rs = lax.rsqrt(jnp.mean(oh * oh, axis=-1, keepdims=True) + EPS)
        sg = _sig(r)
        dn = do * (r * sg)
        d_r.append(do * (oh * rs * g_out) * (sg + r * sg * (1.0 - sg)))
        dx, dgh = _rms_bwd(oh, dn, g_out)
        d_o.append(dx)
        dg = dg + dgh
    return jnp.concatenate(d_o, axis=1), jnp.concatenate(d_r, axis=1), dg


def _adam(w, g, m, v):
    m = ADAM_B1 * m + (1.0 - ADAM_B1) * g
    v = ADAM_B2 * v + (1.0 - ADAM_B2) * (g * g)
    m_hat = m / (1.0 - ADAM_B1 ** ADAM_STEP)
    v_hat = v / (1.0 - ADAM_B2 ** ADAM_STEP)
    return -ADAM_LR * (m_hat / (jnp.sqrt(v_hat) + ADAM_EPS) + ADAM_WD * w), m, v


def _layer_fwd(x, mem, w, p, tabs, tag, carry_fox=None, after_fox=None, carry_mla=None):
    c4, s4 = tabs
    sv = {'x0': x}
    nm = lambda t: f'{t}_{tag}'
    za, h = _mm(x, w['in_a'], mode='nn', out_dtype=BF16, norm_g=p['g_mix'], emit_norm=True, name=nm('in_a'))
    zb = _mm(h, w['in_b'], mode='nn', out_dtype=F32, name=nm('in_b'))
    zc = _mm(h, w['in_c'], mode='nn', out_dtype=F32, name=nm('in_c'))
    sv.update(h=h, zc=zc)
    ff = Cols(zb, 128, B_FF // 128)
    (lf,) = _rowwise(lambda f, b: (_logsig(f + b),), [ff], [p['b_fox']], [(128, F32)], name=nm('fox_lf'))
    cum = _cumsum_rows(lf, reverse=False, name=nm('fox_cum'))
    ckf = jnp.pad(cum[:, :FOX_HEADS].T.reshape(2, 2, x.shape[0]), ((0, 0), (0, 6), (0, 0)))
    fox = dict(qc=0, kc=2, vc=4, nb=2, g=2, mode='causal', ck=ckf)
    o_fox, lse_fox, *carried = _mattn_fwd(za, za, za, name=nm('fox_attn'), comm=carry_fox, **fox)
    if after_fox is not None:
        w = {**w, **after_fox(carried[0])}
    sv.update(ff=ff, za=za, fox=fox, o_fox=o_fox, lse_fox=lse_fox)
    glow = Cols(zb, 128, B_GLOW // 128)
    gr = Cols(zb, 512, B_GR // 512)

    def gate_fn(gl, wg, bg):
        return (_logsig(_dot(gl.astype(BF16), wg) + bg) / GLA_TAU,)

    (la,) = _rowwise(gate_fn, [glow], [w['gate'], p['b_gla']], [(256, F32)], name=nm('gla_gate'))
    gla = dict(qc=B_GQ // LANES, kc=B_GK // LANES, vc=B_GV // LANES)
    oraw, states = _gla_fwd(zb, la, name=nm('gla'), **gla)
    (o_gla,) = _rowwise(_gla_out_fwd, [oraw, gr], [p['g_gla_out']], [(512, BF16)], name=nm('gla_out'))
    sv.update(glow=glow, gr=gr, zb=zb, la=la, gla=gla, states=states, oraw=oraw, o_gla=o_gla)
    mq = Cols(zb, 256, B_MQ // 256)
    mkv = Cols(zb, 128, B_MKV // 128)
    mkr2 = Cols(zb, 256, B_MKR // 256)
    qp, cqn = _mm(mq, w['uq'], mode='nn', out_dtype=F32, norm_g=p['g_mla_q'], emit_norm=True, name=nm('mla_uq'))
    kvp, ckvn = _mm(mkv, w['ukv'], mode='nn', out_dtype=BF16, norm_g=p['g_mla_kv'], emit_norm=True,
                    name=nm('mla_ukv'))

    def rope_fn(qv, kr, c4v, s4v):
        q_rope = qv[:, 256:384] * c4v + qv[:, 384:512] * s4v
        q_scaled = jnp.concatenate([qv[:, 0:256], q_rope], axis=1) * MLA_SCALE
        return q_scaled, kr[:, 0:128] * c4v + kr[:, 128:256] * s4v

    qall, kr4 = _rowwise(rope_fn, [qp, mkr2, c4, s4], [], [(384, BF16), (128, BF16)], name=nm('rope'))
    mla = dict(qc=0, kc=0, vc=2, nb=2, g=2, dq_scale=MLA_SCALE, mode='chunk', qr=qall, qrc=2, kr=kr4)
    o_mla, lse_mla, *carried = _mattn_fwd(qall, kvp, kvp, name=nm('mla_attn'), comm=carry_mla, **mla)
    if carry_mla is not None:
        sv['carried_mla'] = carried[0]
    sv.update(mq=mq, mkv=mkv, cqn=cqn, ckvn=ckvn, qall=qall, kvp=kvp, mla=mla, o_mla=o_mla, lse_mla=lse_mla)
    of_m, om_m = o_fox, o_mla
    sv.update(of_m=of_m, om_m=om_m)
    b_br = p['b_branch']

    def first(acc, zg, bb):
        return _sig(zg + bb) * acc

    def more(acc, zg, bb, prev):
        return prev + _sig(zg + bb) * acc

    y = _mm(of_m, w['w_up_fox'], mode='nn', out_dtype=F32, name=nm('up_fox'), epilogue=first,
            extras=[(zc, *_mn(col_off=0)), (b_br, *_nvec(col_off=0))])
    y = _mm(o_gla, w['w_up_gla'], mode='nn', out_dtype=F32, name=nm('up_gla'), epilogue=more,
            extras=[(zc, *_mn(col_off=1024)), (b_br, *_nvec(col_off=1024)), (y, *_mn())])
    y = _mm(om_m, w['w_up_mla'], mode='nn', out_dtype=BF16, name=nm('up_mla'), epilogue=more,
            extras=[(zc, *_mn(col_off=2048)), (b_br, *_nvec(col_off=2048)), (y, *_mn())])
    add = lambda acc, res: res + acc
    x1 = _mm(y, w['w_out'], mode='nn', out_dtype=F32, name=nm('out'), epilogue=add, extras=[(x, *_mn())])
    sv.update(y=y, x1=x1)
    qx, hx = _mm(x1, w['w_xq'], mode='nn', out_dtype=BF16, norm_g=p['g_xa'], emit_norm=True, name=nm('xq'),
                 epilogue=lambda acc: acc * XA_SCALE)
    kvx, mn = _mm(mem, w['w_xkv'], mode='nn', out_dtype=BF16, norm_g=p['g_mem'], emit_norm=True, name=nm('xkv'))
    xa = dict(qc=0, kc=0, vc=4, nb=4, g=1, dq_scale=XA_SCALE, mode='full')
    ox_m, lse_x = _mattn_fwd(qx, kvx, kvx, name=nm('xa_attn'), **xa)
    x2 = _mm(ox_m, w['w_xo'], mode='nn', out_dtype=F32, name=nm('xo'), epilogue=add, extras=[(x1, *_mn())])
    sv.update(hx=hx, mn=mn, qx=qx, kvx=kvx, xa=xa, lse_x=lse_x, ox_m=ox_m, x2=x2)
    hpre, hm = _mm(x2, w['w_mlp1'], mode='nn', out_dtype=BF16, norm_g=p['g_mlp'], emit_norm=True, name=nm('mlp1'))
    relu2 = lambda t: jnp.square(jnp.maximum(t.astype(F32), 0.0))
    x3 = _mm(hpre, w['w_mlp2'], mode='nn', out_dtype=F32, name=nm('mlp2'), a_fn=relu2, epilogue=add,
             extras=[(x2, *_mn())])
    sv.update(hpre=hpre, hm=hm, w=w)
    return x3, sv


EARLY = ('w_mlp1', 'w_mlp2', 'w_xo', 'w_xq', 'w_xkv', 'w_out', 'w_up_fox', 'w_up_gla', 'w_up_mla')
LATE = ('w_in', 'w_gla_gate', 'w_mla_uq', 'w_mla_ukv')


def _layer_bwd(dx3, mem, w, p, tabs, sv, tag, carry_mla=None, early=None):
    c4, s4 = tabs
    nm = lambda t: f'{t}_{tag}'
    s = dx3.shape[0]
    gw, gs = {}, {}
    relu2 = lambda t: jnp.square(jnp.maximum(t.astype(F32), 0.0))
    gw['w_mlp2'] = _mm(sv['hpre'], dx3, mode='tn', out_dtype=F32, name=nm('d_mlp2'), a_fn=relu2)
    dact = lambda acc, hp: acc * (2.0 * jnp.maximum(hp.astype(F32), 0.0))
    dhpre = _mm(dx3, w['w_mlp2'], mode='nt', out_dtype=BF16, name=nm('d_act'), epilogue=dact,
                extras=[(sv['hpre'], *_mn())])
    gw['w_mlp1'] = _mm(sv['hm'], dhpre, mode='tn', out_dtype=F32, name=nm('d_mlp1'))
    dhm = _mm(dhpre, w['w_mlp1'], mode='nt', out_dtype=F32, name=nm('d_hm'))
    dx2, gs['g_mlp'] = _norm_bwd_call(sv['x2'], dhm, p['g_mlp'], dx3, nm('d_norm_mlp'))
    gw['w_xo'] = _mm(sv['ox_m'], dx2, mode='tn', out_dtype=F32, name=nm('d_xo'))
    dox = _mm(dx2, w['w_xo'], mode='nt', out_dtype=BF16, name=nm('d_ox'))
    dqx_m, dkx, dvx = _mattn_bwd(sv['qx'], sv['kvx'], sv['kvx'], sv['ox_m'], dox, sv['lse_x'], name=nm('xa_bwd'),
                                 **sv['xa'])
    dkvx = jnp.concatenate([dkx, dvx], axis=1).astype(BF16)
    gw['w_xq'] = _mm(sv['hx'], dqx_m, mode='tn', out_dtype=F32, name=nm('d_xq'))
    dhx = _mm(dqx_m, w['w_xq'], mode='nt', out_dtype=F32, name=nm('d_hx'))
    gw['w_xkv'] = _mm(sv['mn'], dkvx, mode='tn', out_dtype=F32, name=nm('d_xkv'))
    dmn = _mm(dkvx, w['w_xkv'], mode='nt', out_dtype=F32, name=nm('d_mn'))
    _, gs['g_mem'] = _norm_bwd_call(mem, dmn, p['g_mem'], None, nm('d_norm_mem'))
    dx1, gs['g_xa'] = _norm_bwd_call(sv['x1'], dhx, p['g_xa'], dx2, nm('d_norm_xa'))
    gw['w_out'] = _mm(sv['y'], dx1, mode='tn', out_dtype=F32, name=nm('d_out'))
    dy = _mm(dx1, w['w_out'], mode='nt', out_dtype=BF16, name=nm('d_y'))
    zc, b_br = sv['zc'], p['b_branch']

    def du_fn(dyv, zg, bb):
        g = _sig(zg + bb)
        d = dyv.astype(F32)
        return d * g[:, 0:1024], d * g[:, 1024:2048], d * g[:, 2048:3072]

    du = _rowwise(du_fn, [dy, zc], [b_br], [(D_MODEL, BF16)] * 3, name=nm('d_u'))

    def dgate(acc, dyv, zg, bb):
        g = _sig(zg + bb)
        return dyv.astype(F32) * acc * g * (1.0 - g)

    dzc, db_br, do_br = [], [], []
    for q, (o_m, wn) in enumerate(((sv['of_m'], 'w_up_fox'), (sv['o_gla'], 'w_up_gla'), (sv['om_m'], 'w_up_mla'))):
        dz, db = _mm(o_m, w[wn], mode='nn', out_dtype=BF16, name=nm(f'd_zg{q}'), epilogue=dgate, col_sums=True,
                     extras=[(dy, *_mn()), (zc, *_mn(col_off=1024 * q)), (b_br, *_nvec(col_off=1024 * q))])
        dzc.append(dz)
        db_br.append(db)
        gw[wn] = _mm(o_m, du[q], mode='tn', out_dtype=F32, name=nm(f'd_up{q}'))
        do_br.append(_mm(du[q], w[wn], mode='nt', out_dtype=F32 if q == 1 else BF16, name=nm(f'd_o{q}')))
    dzc = jnp.concatenate(dzc, axis=1)
    gs['b_branch'] = jnp.concatenate(db_br, axis=1)
    za = sv['za']
    carry_fox = None if early is None else early({nm_: gw[nm_] for nm_ in EARLY})
    dfq, dfk, dfv, dck, dcq, *carried_fox = _mattn_bwd(za, za, za, sv['o_fox'], do_br[0], sv['lse_fox'],
                                                       name=nm('fox_bwd'), comm=carry_fox, **sv['fox'])
    dcum = _padc(dck[:, :2, :].reshape(FOX_HEADS, s).T + dcq.reshape(s, 2, LANES)[:, :, :2].reshape(s, FOX_HEADS), 128)
    dlf = _cumsum_rows(dcum, reverse=True, name=nm('fox_dcum'))

    def dff_fn(dl, f, b):
        d = dl * _sig(-(f + b))
        return d, d

    dff, db_fox = _rowwise(dff_fn, [dlf, sv['ff']], [p['b_fox']], [(128, F32)], [128], name=nm('fox_dff'))
    gs['b_fox'] = db_fox
    dza = jnp.concatenate([dfq, dfk, dfv], axis=1).astype(BF16)
    dqn, dkn, dvv, dq_rope, dk_rope, *carried_mla = _mattn_bwd(sv['qall'], sv['kvp'], sv['kvp'], sv['o_mla'], do_br[2],
                                                               sv['lse_mla'], name=nm('mla_bwd'), comm=carry_mla,
                                                               **sv['mla'])

    def drope_fn(dn, dq, dk, c4v, s4v):
        return jnp.concatenate([dn, dq * c4v, dq * s4v], axis=1), jnp.concatenate([dk * c4v, dk * s4v], axis=1)

    dqp, dmkr2 = _rowwise(drope_fn, [dqn, dq_rope, dk_rope, c4, s4], [], [(512, BF16), (256, BF16)], name=nm('d_rope'))
    dkvp = jnp.concatenate([dkn, dvv], axis=1).astype(BF16)
    gw['uq'] = _mm(sv['cqn'], dqp, mode='tn', out_dtype=F32, name=nm('d_uq'))
    dcqn = _mm(dqp, w['uq'], mode='nt', out_dtype=F32, name=nm('d_cqn'))
    gw['ukv'] = _mm(sv['ckvn'], dkvp, mode='tn', out_dtype=F32, name=nm('d_ukv'))
    dckvn = _mm(dkvp, w['ukv'], mode='nt', out_dtype=F32, name=nm('d_ckvn'))
    dmq, gs['g_mla_q'] = _norm_bwd_call(sv['mq'], dcqn, p['g_mla_q'], None, nm('d_norm_q'))
    dmkv, gs['g_mla_kv'] = _norm_bwd_call(sv['mkv'], dckvn, p['g_mla_kv'], None, nm('d_norm_kv'))
    doraw, dgr, gs['g_gla_out'] = _rowwise(_gla_out_bwd, [sv['oraw'], sv['gr'], do_br[1]], [p['g_gla_out']],
                                           [(512, F32), (512, BF16)], [128], name=nm('d_gla_out'))
    st = sv['states']
    st_prev = jnp.concatenate([jnp.zeros_like(st[:, :1]), st[:, :-1]], axis=1)
    dgq, dgk, dgv, dla = _gla_bwd(sv['zb'], sv['la'], st, st_prev, doraw, name=nm('gla_bwd'), **sv['gla'])

    def dgate_fn(dl, gl, wg, bg):
        pre = _dot(gl.astype(BF16), wg) + bg
        dpre = dl * (1.0 / GLA_TAU) * _sig(-pre)
        return dpre, _dot(dpre.astype(BF16), wg, NT), dpre

    dpre, dglow, gs['b_gla'] = _rowwise(dgate_fn, [dla, sv['glow']], [w['gate'], p['b_gla']],
                                        [(256, BF16), (128, BF16)], [256], name=nm('d_gla_gate'))
    gw['gate'] = _mm(sv['glow'], dpre, mode='tn', out_dtype=F32, name=nm('d_wgate'))
    bf = lambda t: t.astype(BF16)
    dzb = jnp.concatenate([dgr, bf(dgq), bf(dgk), bf(dgv), bf(dmq), dmkr2, bf(dff), dglow, bf(dmkv),
                           jnp.zeros((s, B_W - B_END), BF16)], axis=1)
    h = sv['h']
    gw['in_a'] = _mm(h, dza, mode='tn', out_dtype=F32, name=nm('d_in_a'))
    gw['in_b'] = _mm(h, dzb, mode='tn', out_dtype=F32, name=nm('d_in_b'))
    gw['in_c'] = _mm(h, dzc, mode='tn', out_dtype=F32, name=nm('d_in_c'))
    add = lambda acc, prev: prev + acc
    dh = _mm(dza, w['in_a'], mode='nt', out_dtype=F32, name=nm('d_h_a'))
    dh = _mm(dzb, w['in_b'], mode='nt', out_dtype=F32, name=nm('d_h_b'), epilogue=add, extras=[(dh, *_mn())])
    dh = _mm(dzc, w['in_c'], mode='nt', out_dtype=F32, name=nm('d_h_c'), epilogue=add, extras=[(dh, *_mn())])
    dx0, gs['g_mix'] = _norm_bwd_call(sv['x0'], dh, p['g_mix'], dx1, nm('d_norm_mix'))
    return dx0, gw, gs, (carried_mla or [None])[0], (carried_fox or [None])[0]


def _loss_head(x, target, g_final):
    d = x.shape[1]

    def fn(xv, tv, gv):
        r = lax.rsqrt(jnp.mean(xv * xv, axis=-1, keepdims=True) + EPS)
        xh = xv * r
        e = xh * gv - tv
        dy = e * (1.0 / d)
        gd = dy * gv
        dx = r * (gd - xh * jnp.mean(gd * xh, axis=-1, keepdims=True))
        row_loss = 0.5 * jnp.mean(e * e, axis=-1, keepdims=True)
        return dx, dy * xh, jnp.broadcast_to(row_loss, (xv.shape[0], LANES))

    return _rowwise(fn, [x, target], [g_final], [(d, F32)], [d, LANES], name='loss_head')


def _small_sizes(shapes):
    return [math.prod(shapes[nm]) for nm in SMALL]


def _step(args):
    shapes = {nm: args[nm].shape for nm in ORDER}
    x, mem, target = args['x'][0], args['mem'][0], args['loss_target'][0]
    s = x.shape[0]

    def wire(nm, l):
        w = args[nm][l].astype(BF16)
        if nm == 'w_in':
            w = jnp.pad(w, ((0, 0), (0, WIN_PAD - WIN_SHARD)))
        if nm == 'w_gla_gate':
            w = jnp.pad(w, ((0, GATE_WIRE_ROWS - GLA_RANK), (0, 0)))
        return w

    axis_of = dict(BIG)
    names = tuple(nm for nm, _ in BIG)
    wires = lambda l, nms: [wire(nm, l) for nm in nms]
    width = lambda nm: WIN_PAD if nm == 'w_in' else args[nm].shape[2]
    side_by_side = lambda nms: [axis_of[nm] == 2 and width(nm) % LANES == 0 for nm in nms]
    over_ici = lambda l, nms: _gather_over_ici(wires(l, nms), side_by_side(nms))

    def whole(parts, nms, tag):
        side = side_by_side(nms)
        parts = _run_comm(_gather_over_d2d(parts, side), name=f'gather_d2d_{tag}', alias=True)
        full = {nm: p if sd else _full_layer(p, axis_of[nm]) for nm, p, sd in zip(nms, parts, side)}
        if 'w_gla_gate' in full:
            full['w_gla_gate'] = full['w_gla_gate'][:GLA_RANK]
        return full

    tabs = _rope_tables(s)
    layers_p = []
    for l in range(DEPTH):
        layers_p.append({
            'g_mix': args['g_mix'][l][None], 'b_fox': _padc(args['b_fox_forget'][l][None], 128),
            'b_gla': args['b_gla_gate'][l][None], 'g_gla_out': args['g_gla_out'][l][None],
            'g_mla_q': args['g_mla_q'][l][None], 'g_mla_kv': args['g_mla_kv'][l][None],
            'b_branch': args['b_branch_gate'][l][None], 'g_xa': args['g_xa'][l][None],
            'g_mem': args['g_mem'][l][None], 'g_mlp': args['g_mlp'][l][None]})

    first = _run_comm(over_ici(0, LATE), name='gather_ici_first_l0')
    w_now = _repack_layer_weights(whole(first, LATE, 'first_l0'))
    saved = []
    xl = x
    for l in range(DEPTH):
        carry_fox = over_ici(0, EARLY) if l == 0 else None
        after_fox = (lambda parts: whole(parts, EARLY, 'rest_l0')) if l == 0 else None
        carry_mla = over_ici(l + 1, names) if l + 1 < DEPTH else None
        xl, sv = _layer_fwd(xl, mem, w_now, layers_p[l], tabs, f'l{l}', carry_fox=carry_fox, after_fox=after_fox,
                            carry_mla=carry_mla)
        saved.append(sv)
        if carry_mla is not None:
            w_now = _repack_layer_weights(whole(sv.pop('carried_mla'), names, f'l{l + 1}'))
    dx, dg_final, loss_lanes = _loss_head(xl, target, args['g_final'][None])
    cidx = lax.axis_index('c')
    chip = 2 * lax.axis_index('x') + lax.axis_index('y')

    def pair_sums(gw, nms, tag):
        mine, theirs = [], []
        for nm in nms:
            shards = _split_full(gw[nm], axis_of[nm]).astype(BF16)
            h = shards.shape[1] // 2
            mine.append(lax.dynamic_slice_in_dim(shards, cidx * h, h, axis=1))
            theirs.append(lax.dynamic_slice_in_dim(shards, (1 - cidx) * h, h, axis=1))
        got = _to_sibling(theirs, name=f'grads_swap_{tag}')
        pairs = []
        for nm, a, b in zip(nms, mine, got):
            _, h, n = a.shape
            (p,) = _rowwise(lambda u, v: (u.astype(F32) + v.astype(F32),),
                            [a.reshape(N_CHIPS * h, n), b.reshape(N_CHIPS * h, n)], [], [(n, BF16)],
                            name=f'pair_sum_{nm}_{tag}')
            pairs.append(p.reshape(N_CHIPS, h, n))
        return pairs

    def finish(pairs, from_chips, nms, tag):
        own = [lax.dynamic_index_in_dim(p, chip, axis=0, keepdims=False) for p in pairs]
        mine = [_sum_chips(o, r, name=f'chip_sum_{nm}_{tag}') for nm, o, r in zip(nms, own, from_chips)]
        theirs = _to_sibling(mine, name=f'grads_join_{tag}')
        return {nm: jnp.where(cidx == 0, jnp.concatenate([a, b]), jnp.concatenate([b, a]))
                for nm, a, b in zip(nms, mine, theirs)}

    gs_layers, done = [None] * DEPTH, [{} for _ in range(DEPTH)]
    above = None
    for l in reversed(range(DEPTH)):
        lowest, early_pairs = l == 0, []

        def early(gw_early, l=l, early_pairs=early_pairs):
            early_pairs.extend(pair_sums(gw_early, EARLY, f'early_l{l}'))
            return _chip_exchange(early_pairs)

        carry_mla = None if above is None else _chip_exchange(above[1])
        dx, gw, gs_layers[l], got_mla, got_fox = _layer_bwd(
            dx, mem, saved[l]['w'], layers_p[l], tabs, saved[l], f'l{l}', carry_mla=carry_mla,
            early=early if lowest else None)
        if above is not None:
            done[above[0]].update(finish(above[1], got_mla, names, f'l{above[0]}'))
        grads = _unpack_layer_grads(gw)
        if lowest:
            done[l].update(finish(early_pairs, got_fox, EARLY, f'early_l{l}'))
            late_pairs = pair_sums(grads, LATE, f'late_l{l}')
            from_late = _run_comm(_chip_exchange(late_pairs), name=f'grads_exchange_late_l{l}')
            done[l].update(finish(late_pairs, from_late, LATE, f'late_l{l}'))
        else:
            above = (l, pair_sums(grads, names, f'l{l}'))
    grad_x = dx[None]
    gshard = {nm: jnp.stack([done[l][nm] for l in range(DEPTH)]) for nm in names}

    small_g = []
    for nm, key in (('g_mix', 'g_mix'), ('b_fox_forget', 'b_fox'), ('b_gla_gate', 'b_gla'),
                    ('g_gla_out', 'g_gla_out'), ('g_mla_q', 'g_mla_q'), ('g_mla_kv', 'g_mla_kv'),
                    ('b_branch_gate', 'b_branch'), ('g_xa', 'g_xa'), ('g_mem', 'g_mem'), ('g_mlp', 'g_mlp')):
        width = shapes[nm][1]
        small_g.append(jnp.concatenate([gs_layers[l][key][0, :width] for l in range(DEPTH)]))
    small_g.append(dg_final[0])
    small_g.append(loss_lanes[0, :1])
    flat = jnp.concatenate(small_g)
    n_small = flat.shape[0]
    srows = -(-n_small // (8 * LANES)) * 8
    pad = lambda v: jnp.pad(v, (0, srows * LANES - v.shape[0])).reshape(srows, LANES)
    all_small = _all_gather8(pad(flat), name='gather_small')
    sw, sm, svv = (pad(jnp.concatenate([args[pre + nm].reshape(-1) for nm in SMALL] + [jnp.zeros((1,), F32)]))
                   for pre in ('', 'm_', 'v_'))

    def small_body(g_ref, w_ref, m_ref, v_ref, go_ref, d_ref, mo_ref, vo_ref):
        g = g_ref[0]
        for q in range(1, N_DEV):
            g = g + g_ref[q]
        go_ref[...] = g
        d_ref[...], mo_ref[...], vo_ref[...] = _adam(w_ref[...], g, m_ref[...], v_ref[...])

    sg, sd, snm, snv = pl.pallas_call(
        small_body, name='small_sum_adam', out_shape=[jax.ShapeDtypeStruct((srows, LANES), F32)] * 4,
        compiler_params=pltpu.CompilerParams(vmem_limit_bytes=VMEM_LIMIT))(all_small, sw, sm, svv)

    def unsmall(buf):
        v, out, off = buf.reshape(-1), {}, 0
        for nm in SMALL:
            nel = math.prod(shapes[nm])
            out[nm] = v[off:off + nel].reshape(shapes[nm])
            off += nel
        return out, v[off]

    res = {}
    (res['grad'], loss), (res['delta'], _), (res['m'], _), (res['v'], _) = (unsmall(t) for t in (sg, sd, snm, snv))

    for nm, _ in BIG:
        shp = args[nm].shape
        view = lambda t: t.reshape(shp[0] * shp[1], shp[2])
        d, m2, v2 = _rowwise(_adam, [view(args[nm]), view(gshard[nm]), view(args['m_' + nm]), view(args['v_' + nm])],
                             [], [(shp[2], F32)] * 3, name=f'adam_{nm}')
        res['grad'][nm], res['delta'][nm], res['m'][nm], res['v'][nm] = (
            gshard[nm], d.reshape(shp), m2.reshape(shp), v2.reshape(shp))

    return (loss, grad_x, *[res['grad'][nm] for nm in ORDER], *[res['delta'][nm] for nm in ORDER],
            *[res['m'][nm] for nm in ORDER], *[res['v'][nm] for nm in ORDER])


def kernel(x, mem, g_mix, w_in, b_fox_forget, w_gla_gate, b_gla_gate, g_gla_out, g_mla_q, w_mla_uq, g_mla_kv, w_mla_ukv, b_branch_gate, w_up_fox, w_up_gla, w_up_mla, w_out, g_xa, g_mem, w_xq, w_xkv, w_xo, g_mlp, w_mlp1, w_mlp2, g_final, loss_target, m_g_mix, m_w_in, m_b_fox_forget, m_w_gla_gate, m_b_gla_gate, m_g_gla_out, m_g_mla_q, m_w_mla_uq, m_g_mla_kv, m_w_mla_ukv, m_b_branch_gate, m_w_up_fox, m_w_up_gla, m_w_up_mla, m_w_out, m_g_xa, m_g_mem, m_w_xq, m_w_xkv, m_w_xo, m_g_mlp, m_w_mlp1, m_w_mlp2, m_g_final, v_g_mix, v_w_in, v_b_fox_forget, v_w_gla_gate, v_b_gla_gate, v_g_gla_out, v_g_mla_q, v_w_mla_uq, v_g_mla_kv, v_w_mla_ukv, v_b_branch_gate, v_w_up_fox, v_w_up_gla, v_w_up_mla, v_w_out, v_g_xa, v_g_mem, v_w_xq, v_w_xkv, v_w_xo, v_g_mlp, v_w_mlp1, v_w_mlp2, v_g_final):
    return _step(dict(locals()))
```

```python
import functools
import math
import typing

import jax
import jax.numpy as jnp
from jax import lax
from jax.experimental import pallas as pl
from jax.experimental.pallas import tpu as pltpu

F32 = jnp.float32
BF16 = jnp.bfloat16
MESH = pl.DeviceIdType.MESH

D_MODEL = 1024
DEPTH = 2
CHUNK = 64
EPS = 1e-6
FOX_HEADS, FOX_HD = 4, 64
GLA_HEADS, GLA_DK, GLA_DV, GLA_RANK, GLA_TAU = 4, 64, 128, 16, 16.0
MLA_HEADS, MLA_Q_RANK, MLA_KV_RANK, MLA_NOPE, MLA_ROPE, MLA_VD = 4, 256, 128, 64, 32, 64
ROPE_BASE = 10000.0
XA_HEADS, XA_HD = 4, 128
D_FF = 4 * D_MODEL
IN_SIZES = (256, 256, 256, 4, 256, 256, 512, 16, 512, 256, 128, 32, 3072)
N_IN = sum(IN_SIZES)

ADAM_LR, ADAM_B1, ADAM_B2, ADAM_EPS, ADAM_WD, ADAM_STEP = 0.001, 0.9, 0.999, 1e-08, 0.01, 10

N_CHIPS = 4
N_DEV = 8
LANES = 128
VMEM_LIMIT = 48 * 1024 * 1024
MASK_VALUE = -1e30

BIG = (('w_in', 2), ('w_gla_gate', 2), ('w_mla_uq', 2), ('w_mla_ukv', 2), ('w_up_fox', 2), ('w_up_gla', 2),
       ('w_up_mla', 2), ('w_out', 1), ('w_xq', 1), ('w_xkv', 1), ('w_xo', 2), ('w_mlp1', 2), ('w_mlp2', 1))
SMALL = ('g_mix', 'b_fox_forget', 'b_gla_gate', 'g_gla_out', 'g_mla_q', 'g_mla_kv', 'b_branch_gate',
         'g_xa', 'g_mem', 'g_mlp', 'g_final')
ORDER = ('g_mix', 'w_in', 'b_fox_forget', 'w_gla_gate', 'b_gla_gate', 'g_gla_out', 'g_mla_q', 'w_mla_uq',
         'g_mla_kv', 'w_mla_ukv', 'b_branch_gate', 'w_up_fox', 'w_up_gla', 'w_up_mla', 'w_out', 'g_xa', 'g_mem',
         'w_xq', 'w_xkv', 'w_xo', 'g_mlp', 'w_mlp1', 'w_mlp2', 'g_final')


def _params(*sem):
    return pltpu.CompilerParams(dimension_semantics=sem, vmem_limit_bytes=VMEM_LIMIT)


def _sig(x):
    return 1.0 / (1.0 + jnp.exp(-x))


def _logsig(x):
    return jnp.minimum(x, 0.0) - jnp.log(1.0 + jnp.exp(-jnp.abs(x)))


NN = (((1,), (0,)), ((), ()))
NT = (((1,), (1,)), ((), ()))
TN = (((0,), (0,)), ((), ()))


def _dot(a, b, dims=NN):
    return lax.dot_general(a, b, dims, preferred_element_type=F32)


class Cols(typing.NamedTuple):
    arr: jax.Array
    width: int
    blk: int


def _tri_dot(tri, x):
    hi = x.astype(BF16)
    r1 = x - hi.astype(F32)
    mid = r1.astype(BF16)
    lo = (r1 - mid.astype(F32)).astype(BF16)
    return _dot(tri, hi) + _dot(tri, mid) + _dot(tri, lo)


MM_TILES = ((1024, 1024), (1024, 512), (512, 1024), (512, 512), (256, 1024), (512, 256), (256, 512), (256, 256),
            (128, 1024), (128, 128))
MM_VMEM_BUDGET = 38 * 1024 * 1024


def _mm_tiles(m, n, k, a_bytes, b_bytes, out_bytes, ex_bytes, has_norm, emit_norm, has_fn, full_rows):
    for tm, tn in MM_TILES:
        tm, tn = min(tm, m), min(tn, n)
        if m % tm or n % tn or (full_rows and tn != n):
            continue
        blocks = tm * k * a_bytes + k * tn * b_bytes + tm * tn * (out_bytes + ex_bytes) + (tm * k * 2 if emit_norm else 0)
        temps = tm * tn * 4 + (tm * k * 2 if has_norm else 0) + (tm * k * 6 if has_fn or has_norm else 0)
        if 2 * blocks + temps <= MM_VMEM_BUDGET:
            return tm, tn
    raise ValueError((m, n, k))


def _mm(a, b, *, mode, out_dtype, name, norm_g=None, emit_norm=False, a_fn=None, extras=(), epilogue=None,
        col_sums=False, full_rows=False):
    a_blk = 0
    if isinstance(a, Cols):
        a, width, a_blk = a
        a_shape = (a.shape[0], width)
    else:
        a_shape = a.shape
    if mode == 'tn':
        k, m = a_shape
    else:
        m, k = a_shape
    n = b.shape[0] if mode == 'nt' else b.shape[1]
    assert (b.shape[1] if mode == 'nt' else b.shape[0]) == k, (name, a.shape, b.shape)
    has_norm = norm_g is not None
    ex_bytes = sum(arr.dtype.itemsize for arr, kind, _ in extras if kind == 'mn')
    tm, tn = _mm_tiles(m, n, k, a.dtype.itemsize, b.dtype.itemsize, jnp.dtype(out_dtype).itemsize, ex_bytes, has_norm,
                       emit_norm, a_fn is not None, full_rows)
    assert all(col % tn == 0 for _, _, col in extras), (name, tn)
    assert a_blk == 0 or (mode == 'nn') or (mode == 'tn' and tm == m)
    assert not (col_sums and (has_norm or emit_norm))
    ij = (lambda f: lambda g0, g1: f(g1, g0)) if col_sums else (lambda f: f)
    spec = lambda blk, f: pl.BlockSpec(blk, ij(f))
    if mode == 'tn':
        a_spec = spec((k, tm), lambda i, j: (0, i + a_blk))
    else:
        a_spec = spec((tm, k), lambda i, j: (i, a_blk))
    b_spec = spec((tn, k), lambda i, j: (j, 0)) if mode == 'nt' else spec((k, tn), lambda i, j: (0, j))
    dims = {'nn': NN, 'nt': NT, 'tn': TN}[mode]
    assert not (has_norm and mode != 'nn')
    n_ex = len(extras)

    def body(*refs):
        a_ref, b_ref = refs[0], refs[1]
        pos = 2
        g_ref = None
        if has_norm:
            g_ref = refs[pos]
            pos += 1
        ex_refs = refs[pos:pos + n_ex]
        pos += n_ex
        o_ref = refs[pos]
        pos += 1
        h_ref = None
        if emit_norm:
            h_ref = refs[pos]
            pos += 1
        if has_norm:
            an_ref = refs[pos]

            @pl.when(pl.program_id(1) == 0)
            def _():
                xf = a_ref[...].astype(F32)
                y = xf * lax.rsqrt(jnp.mean(xf * xf, axis=-1, keepdims=True) + EPS) * g_ref[...]
                an_ref[...] = y.astype(BF16)
                if emit_norm:
                    h_ref[...] = y.astype(BF16)

            av = an_ref[...]
        else:
            av = a_ref[...]
            if a_fn is not None:
                av = a_fn(av)
            av = av.astype(BF16)
        acc = _dot(av, b_ref[...].astype(BF16), dims)
        if epilogue is not None:
            acc = epilogue(acc, *[r[...] for r in ex_refs])
        acc, to_sum = acc if isinstance(acc, tuple) else (acc, acc)
        o_ref[...] = acc.astype(out_dtype)
        if col_sums:
            sum_ref = refs[pos]

            @pl.when(pl.program_id(1) == 0)
            def _():
                sum_ref[...] = jnp.zeros_like(sum_ref)

            sum_ref[...] += jnp.sum(to_sum, axis=0, keepdims=True)

    in_specs = [a_spec, b_spec]
    args = [a, b]
    if has_norm:
        in_specs.append(pl.BlockSpec((1, k), lambda i, j: (0, 0)))
        args.append(norm_g)
    for arr, kind, col in extras:
        if kind == 'mn':
            in_specs.append(spec((tm, tn), lambda i, j, o=col // tn: (i, j + o)))
        else:
            in_specs.append(spec((1, tn), lambda i, j, o=col // tn: (0, j + o)))
        args.append(arr)
    out_shape = [jax.ShapeDtypeStruct((m, n), out_dtype)]
    out_specs = [spec((tm, tn), lambda i, j: (i, j))]
    if emit_norm:
        out_shape.append(jax.ShapeDtypeStruct((m, k), BF16))
        out_specs.append(pl.BlockSpec((tm, k), lambda i, j: (i, 0)))
    if col_sums:
        out_shape.append(jax.ShapeDtypeStruct((1, n), F32))
        out_specs.append(spec((1, tn), lambda i, j: (0, j)))
    scratch = [pltpu.VMEM((tm, k), BF16)] if has_norm else []
    grid = (n // tn, m // tm) if col_sums else (m // tm, n // tn)
    res = pl.pallas_call(
        body, name=name, grid=grid, in_specs=in_specs, out_specs=out_specs, out_shape=out_shape,
        scratch_shapes=scratch, compiler_params=_params('arbitrary', 'arbitrary'))(*args)
    return res if emit_norm or col_sums else res[0]


def _mn(col_off=0):
    return 'mn', col_off


def _nvec(col_off=0):
    return 'n', col_off


def _rowwise(fn, rows, consts, outs, sums=(), *, name, ts=256):
    views = [x if isinstance(x, Cols) else Cols(x, x.shape[1], 0) for x in rows]
    rows = [v.arr for v in views]
    r = rows[0].shape[0]
    ts = min(ts, r)
    assert r % ts == 0, (name, r, ts)
    nr, nc, no, ns = len(rows), len(consts), len(outs), len(sums)

    def body(*refs):
        vals = fn(*[x[...] for x in refs[:nr + nc]])
        for q in range(no):
            refs[nr + nc + q][...] = vals[q].astype(outs[q][1])
        if ns:
            @pl.when(pl.program_id(0) == 0)
            def _():
                for q in range(ns):
                    refs[nr + nc + no + q][...] = jnp.zeros((1, sums[q]), F32)

            for q in range(ns):
                refs[nr + nc + no + q][...] += jnp.sum(vals[no + q].astype(F32), axis=0, keepdims=True)

    in_specs = [pl.BlockSpec((ts, v.width), lambda i, blk=v.blk: (i, blk)) for v in views]
    in_specs += [pl.BlockSpec(x.shape, lambda i, nd=x.ndim: (0,) * nd) for x in consts]
    out_specs = [pl.BlockSpec((ts, w), lambda i: (i, 0)) for w, _ in outs]
    out_specs += [pl.BlockSpec((1, w), lambda i: (0, 0)) for w in sums]
    out_shape = [jax.ShapeDtypeStruct((r, w), dt) for w, dt in outs]
    out_shape += [jax.ShapeDtypeStruct((1, w), F32) for w in sums]
    return pl.pallas_call(body, name=name, grid=(r // ts,), in_specs=in_specs, out_specs=out_specs,
                          out_shape=out_shape, compiler_params=_params('arbitrary'))(*rows, *consts)


def _cumsum_rows(x, *, reverse, name, bs=256):
    s, w = x.shape
    bs = min(bs, s)
    nb = s // bs

    def body(x_ref, o_ref, carry):
        @pl.when(pl.program_id(0) == 0)
        def _():
            carry[...] = jnp.zeros_like(carry)

        r = lax.broadcasted_iota(jnp.int32, (bs, bs), 0)
        c = lax.broadcasted_iota(jnp.int32, (bs, bs), 1)
        tri = jnp.where((c >= r) if reverse else (c <= r), 1.0, 0.0).astype(BF16)
        xv = x_ref[...]
        o_ref[...] = _tri_dot(tri, xv) + carry[...]
        carry[...] += jnp.sum(xv, axis=0, keepdims=True)

    imap = (lambda i: (nb - 1 - i, 0)) if reverse else (lambda i: (i, 0))
    return pl.pallas_call(body, name=name, grid=(nb,), in_specs=[pl.BlockSpec((bs, w), imap)],
                          out_specs=pl.BlockSpec((bs, w), imap), out_shape=jax.ShapeDtypeStruct((s, w), F32),
                          scratch_shapes=[pltpu.VMEM((1, w), F32)], compiler_params=_params('arbitrary'))(x)


def _mask(mode, q0, k0, bq, bk):
    qpos = q0 + lax.broadcasted_iota(jnp.int32, (bq, bk), 0)
    kpos = k0 + lax.broadcasted_iota(jnp.int32, (bq, bk), 1)
    if mode == 'causal':
        return kpos <= qpos
    return kpos < (jnp.right_shift(qpos, int(math.log2(CHUNK))) + 1) * CHUNK


ROPE_SHIFT = int(math.log2(MLA_ROPE))
FOX_SCALE, MLA_SCALE, XA_SCALE = FOX_HD ** -0.5, (MLA_NOPE + MLA_ROPE) ** -0.5, XA_HD ** -0.5
ATTN_ROW_SLAB = 512


def _lane_masks(g, b, rope):
    lane = lax.broadcasted_iota(jnp.int32, (1, LANES), 1)
    heads = [None if g == 1 else (lane >= hh * (LANES // g)) & (lane < (hh + 1) * (LANES // g)) for hh in range(g)]
    ropes = [jnp.right_shift(lane, ROPE_SHIFT) == b * g + hh for hh in range(g)] if rope else [None] * g
    return heads, ropes


def _sel(mask, x):
    return x if mask is None else jnp.where(mask, x, jnp.zeros_like(x))


class Step(typing.NamedTuple):
    qi: typing.Any
    kj: typing.Any
    first: typing.Any
    last: typing.Any
    plain: typing.Any
    masked: typing.Any


def _fwd_steps(tri, nq, nk):
    if not tri:
        return (nq, nk), lambda i, j: Step(i, j, j == 0, j == nk - 1, True, False)
    if nq % 2:
        return (nq, nk), lambda i, j: Step(i, jnp.minimum(i, j), j == 0, j == nk - 1, j < i, j == i)

    def at(i, t):
        low = t <= i
        diag = (t == i) | (t == nq)
        return Step(jnp.where(low, i, nq - 1 - i), jnp.where(low, t, t - (i + 1)), (t == 0) | (t == i + 1), diag,
                    jnp.logical_not(diag), diag)

    return (nq // 2, nq + 1), at


def _bwd_steps(tri, nq, nk):
    if not tri:
        return (nk, nq), lambda j, i: Step(i, j, i == 0, i == nq - 1, True, False)
    if nk % 2:
        return (nk, nq), lambda j, i: Step(jnp.maximum(i, j), j, i == 0, i == nq - 1, i > j, i == j)

    def at(j, t):
        n1 = nq - j
        low = t < n1
        diag = (t == 0) | (t == n1)
        return Step(jnp.where(low, j + t, nk - 1 - j + t - n1), jnp.where(low, j, nk - 1 - j), diag,
                    (t == n1 - 1) | (t == nq), jnp.logical_not(diag), diag)

    return (nk // 2, nq + 1), at


def _carried(comm, refs, n_in, n_out):
    ci, co = len(comm.ins), len(comm.out_shapes)
    ins = refs[n_in:n_in + ci]
    outs = refs[n_in + ci + n_out:n_in + ci + n_out + co]
    rest = refs[:n_in] + refs[n_in + ci:n_in + ci + n_out] + refs[n_in + ci + n_out + co:-2]
    return rest, (ins, outs, refs[-2], refs[-1])


def _mattn_fwd(q, k, v, *, qc, kc, vc, nb, g, mode, name, dq_scale=1.0, ck=None, qr=None, qrc=0, kr=None, blk=512,
               comm=None):
    s, t = q.shape[0], k.shape[0]
    bq, bk = min(blk, s), min(blk, t)
    nq, nk = s // bq, t // bk
    tri = mode != 'full'
    bias, rope = ck is not None, qr is not None
    assert not tri or (bq == bk and bq % CHUNK == 0)
    rs = min(ATTN_ROW_SLAB, bq)
    n_in = 3 + bias + 2 * rope
    (n1, n2), step_at = _fwd_steps(tri, nq, nk)

    def body(*refs):
        refs = list(refs)
        b, p1, p2 = pl.program_id(0), pl.program_id(1), pl.program_id(2)
        st = step_at(p1, p2)
        i, j = st.qi, st.kj
        if comm is not None:
            refs, comm_refs = _carried(comm, refs, n_in, 2)
            pl.when((b == 0) & (p1 == 0) & (p2 == 0))(lambda: comm.start(*comm_refs))
        q_ref, k_ref, v_ref = refs[:3]
        pos = 3
        ck_ref = qr_ref = kr_ref = None
        if bias:
            ck_ref = refs[pos]
            pos += 1
        if rope:
            qr_ref, kr_ref = refs[pos:pos + 2]
            pos += 2
        o_ref, lse_ref, m_s, l_s, acc_s = refs[pos:]
        heads, ropes = _lane_masks(g, b, rope)

        @pl.when(st.first)
        def _():
            m_s[...] = jnp.full_like(m_s, MASK_VALUE)
            l_s[...] = jnp.zeros_like(l_s)
            acc_s[...] = jnp.zeros_like(acc_s)

        def compute(masked):
            k2, v2 = k_ref[...], v_ref[...]
            for r in range(bq // rs):
                rows = pl.ds(r * rs, rs)
                q2 = q_ref[rows, :]
                alphas, pvs = [], []
                for hh in range(g):
                    sc = _dot(_sel(heads[hh], q2), k2, NT)
                    if rope:
                        sc = sc + _dot(_sel(ropes[hh], qr_ref[rows, :]), kr_ref[...], NT)
                    if bias:
                        sc = sc - ck_ref[0, hh:hh + 1, :]
                    if masked:
                        sc = jnp.where(_mask(mode, i * bq + r * rs, j * bk, rs, bk), sc, MASK_VALUE)
                    m_prev = m_s[hh, rows]
                    m_new = jnp.maximum(m_prev, jnp.max(sc, axis=1, keepdims=True))
                    alpha = jnp.exp(m_prev - m_new)
                    p = jnp.exp(sc - m_new)
                    l_s[hh, rows] = alpha * l_s[hh, rows] + jnp.sum(p, axis=1, keepdims=True)
                    m_s[hh, rows] = m_new
                    alphas.append(alpha)
                    pvs.append(_dot(p.astype(BF16), _sel(heads[hh], v2)))
                alpha = alphas[0]
                for hh in range(1, g):
                    alpha = jnp.where(heads[hh], alphas[hh], alpha)
                acc_s[rows, :] = acc_s[rows, :] * alpha + sum(pvs[1:], pvs[0])

        if tri:
            pl.when(st.plain)(functools.partial(compute, False))
            pl.when(st.masked)(functools.partial(compute, True))
        else:
            compute(False)

        @pl.when(st.last)
        def _():
            lane = lax.broadcasted_iota(jnp.int32, (bq, LANES), 1)
            l_full, lse = l_s[0], jnp.zeros((bq, LANES), F32)
            for hh in range(g):
                if hh:
                    l_full = jnp.where(heads[hh], l_s[hh], l_full)
                lse = jnp.where(lane == hh, m_s[hh] + jnp.log(l_s[hh]), lse)
            o_ref[...] = (acc_s[...] / l_full).astype(o_ref.dtype)
            lse_ref[...] = lse

        if comm is not None:
            pl.when((b == nb - 1) & (p1 == n1 - 1) & (p2 == n2 - 1))(lambda: comm.finish(*comm_refs))

    qi = lambda p1, p2: step_at(p1, p2).qi
    kj = lambda p1, p2: step_at(p1, p2).kj
    in_specs = [pl.BlockSpec((bq, LANES), lambda b, p1, p2: (qi(p1, p2), qc + b)),
                pl.BlockSpec((bk, LANES), lambda b, p1, p2: (kj(p1, p2), kc + b)),
                pl.BlockSpec((bk, LANES), lambda b, p1, p2: (kj(p1, p2), vc + b))]
    args = [q, k, v]
    if bias:
        in_specs.append(pl.BlockSpec((1, 8, bk), lambda b, p1, p2: (b, 0, kj(p1, p2))))
        args.append(ck)
    if rope:
        in_specs += [pl.BlockSpec((bq, LANES), lambda b, p1, p2: (qi(p1, p2), qrc)),
                     pl.BlockSpec((bk, LANES), lambda b, p1, p2: (kj(p1, p2), 0))]
        args += [qr, kr]
    out = pl.BlockSpec((bq, LANES), lambda b, p1, p2: (qi(p1, p2), b))
    out_specs = [out, out]
    out_shape = [jax.ShapeDtypeStruct((s, LANES * nb), BF16), jax.ShapeDtypeStruct((s, LANES * nb), F32)]
    scratch = [pltpu.VMEM((g, bq, 1), F32), pltpu.VMEM((g, bq, 1), F32), pltpu.VMEM((bq, LANES), F32)]
    if comm is not None:
        in_specs += [ANY] * len(comm.ins)
        args += comm.ins
        out_specs += [ANY] * len(comm.out_shapes)
        out_shape += comm.out_shapes
        scratch += _sems(comm.n_sems, comm.n_sems)
    res = pl.pallas_call(body, name=name, grid=(nb, n1, n2), in_specs=in_specs, out_specs=out_specs, out_shape=out_shape,
                         scratch_shapes=scratch, compiler_params=_params('arbitrary', 'arbitrary', 'arbitrary'))(*args)
    return res if comm is None else (res[0], res[1], res[2:])


def _mattn_bwd(q, k, v, o, do, lse, *, qc, kc, vc, nb, g, mode, name, dq_scale=1.0, ck=None, qr=None, qrc=0, kr=None,
               blk=512, comm=None):
    s, t = q.shape[0], k.shape[0]
    bq, bk = min(blk, s), min(blk, t)
    nq, nk = s // bq, t // bk
    tri = mode != 'full'
    bias, rope = ck is not None, qr is not None
    rs = min(ATTN_ROW_SLAB, bq)
    n_in, n_out = 6 + bias + 2 * rope, 3 + 2 * bias + 2 * rope
    (n1, n2), step_at = _bwd_steps(tri, nq, nk)

    def body(*refs):
        refs = list(refs)
        if comm is not None:
            refs, comm_refs = _carried(comm, refs, n_in, n_out)
            first = (pl.program_id(0) == 0) & (pl.program_id(1) == 0) & (pl.program_id(2) == 0)
            pl.when(first)(lambda: comm.start(*comm_refs))
        q_ref, k_ref, v_ref, o_ref, do_ref, lse_ref = refs[:6]
        pos = 6
        ck_ref = qr_ref = kr_ref = dck_ref = dcq_ref = dqr_ref = dkr_ref = dck_s = None
        if bias:
            ck_ref = refs[pos]
            pos += 1
        if rope:
            qr_ref, kr_ref = refs[pos:pos + 2]
            pos += 2
        dq_ref, dk_ref, dv_ref = refs[pos:pos + 3]
        pos += 3
        if bias:
            dck_ref, dcq_ref = refs[pos:pos + 2]
            pos += 2
        if rope:
            dqr_ref, dkr_ref = refs[pos:pos + 2]
            pos += 2
        dk_s, dv_s = refs[pos:pos + 2]
        if bias:
            dck_s = refs[pos + 2]
        b, p1, p2 = pl.program_id(0), pl.program_id(1), pl.program_id(2)
        st = step_at(p1, p2)
        i, j = st.qi, st.kj
        heads, ropes = _lane_masks(g, b, rope)

        @pl.when((p1 == 0) & (p2 == 0))
        def _():
            dq_ref[...] = jnp.zeros_like(dq_ref)
            if bias:
                dcq_ref[...] = jnp.zeros_like(dcq_ref)

        if rope:
            @pl.when((b == 0) & (p1 == 0) & (p2 == 0))
            def _():
                dqr_ref[...] = jnp.zeros_like(dqr_ref)
                dkr_ref[...] = jnp.zeros_like(dkr_ref)

        @pl.when(st.first)
        def _():
            dk_s[...] = jnp.zeros_like(dk_s)
            dv_s[...] = jnp.zeros_like(dv_s)
            if bias:
                dck_s[...] = jnp.zeros_like(dck_s)

        def compute(masked):
            k2, v2 = k_ref[...], v_ref[...]
            lane = lax.broadcasted_iota(jnp.int32, (rs, LANES), 1)
            rk = pl.ds(pl.multiple_of(j * bk, bk), bk)
            add = lambda tot, x: x if tot is None else tot + x
            dv_t = dk_t = dkr_t = None
            dck_t = [None] * g
            for r in range(bq // rs):
                rows = pl.ds(r * rs, rs)
                rq = pl.ds(pl.multiple_of(i * bq + r * rs, rs), rs)
                q2, do2, lse2 = q_ref[rows, :], do_ref[rows, :], lse_ref[rows, :]
                dd = do2.astype(F32) * o_ref[rows, :].astype(F32)
                dq_t = dqr_t = dcq_t = None
                for hh in range(g):
                    qm = _sel(heads[hh], q2)
                    sc = _dot(qm, k2, NT)
                    if rope:
                        qrm = _sel(ropes[hh], qr_ref[rows, :])
                        sc = sc + _dot(qrm, kr_ref[...], NT)
                    if bias:
                        sc = sc - ck_ref[0, hh:hh + 1, :]
                    if masked:
                        sc = jnp.where(_mask(mode, i * bq + r * rs, j * bk, rs, bk), sc, MASK_VALUE)
                    p = jnp.exp(sc - jnp.sum(jnp.where(lane == hh, lse2, 0.0), axis=1, keepdims=True))
                    dom = _sel(heads[hh], do2)
                    dp = _dot(dom, v2, NT)
                    delta = jnp.sum(_sel(heads[hh], dd), axis=1, keepdims=True)
                    ds = p * (dp - delta)
                    dsb = ds.astype(BF16)
                    dv_t = add(dv_t, _dot(p.astype(BF16), dom, TN))
                    dk_t = add(dk_t, _dot(dsb, qm, TN))
                    dq_t = add(dq_t, _dot(dsb, _sel(heads[hh], k2)))
                    if rope:
                        dqr_t = add(dqr_t, _dot(dsb, _sel(ropes[hh], kr_ref[...])))
                        dkr_t = add(dkr_t, _dot(dsb, qrm, TN))
                    if bias:
                        dck_t[hh] = add(dck_t[hh], jnp.sum(ds, axis=0, keepdims=True))
                        dcq_t = add(dcq_t, jnp.where(lane == hh, jnp.sum(ds, axis=1, keepdims=True), 0.0))
                dq_ref[rq, :] += dq_t if dq_scale == 1.0 else dq_scale * dq_t
                if rope:
                    dqr_ref[rq, :] += dq_scale * dqr_t
                if bias:
                    dcq_ref[rq, :] += dcq_t
            dv_s[...] += dv_t
            dk_s[...] += dk_t
            if rope:
                dkr_ref[rk, :] += dkr_t
            if bias:
                for hh in range(g):
                    dck_s[hh:hh + 1, :] -= dck_t[hh]

        if tri:
            pl.when(st.plain)(functools.partial(compute, False))
            pl.when(st.masked)(functools.partial(compute, True))
        else:
            compute(False)

        @pl.when(st.last)
        def _():
            dk_ref[...] = dk_s[...]
            dv_ref[...] = dv_s[...]
            if bias:
                dck_ref[0] = dck_s[...]

        if comm is not None:
            pl.when((b == nb - 1) & (p1 == n1 - 1) & (p2 == n2 - 1))(lambda: comm.finish(*comm_refs))

    qrow = lambda col: pl.BlockSpec((bq, LANES), lambda b, p1, p2: (step_at(p1, p2).qi, col(b)))
    krow = lambda col: pl.BlockSpec((bk, LANES), lambda b, p1, p2: (step_at(p1, p2).kj, col(b)))
    in_specs = [qrow(lambda b: qc + b), krow(lambda b: kc + b), krow(lambda b: vc + b), qrow(lambda b: b),
                qrow(lambda b: b), qrow(lambda b: b)]
    args = [q, k, v, o, do, lse]
    whole = lambda rows: pl.BlockSpec((rows, LANES), lambda b, j, i: (0, b))
    out_specs = [whole(s), krow(lambda b: b), krow(lambda b: b)]
    out_shape = [jax.ShapeDtypeStruct((s, LANES * nb), F32), jax.ShapeDtypeStruct((t, LANES * nb), F32),
                 jax.ShapeDtypeStruct((t, LANES * nb), F32)]
    scratch = [pltpu.VMEM((bk, LANES), F32), pltpu.VMEM((bk, LANES), F32)]
    if bias:
        ckj = pl.BlockSpec((1, 8, bk), lambda b, p1, p2: (b, 0, step_at(p1, p2).kj))
        in_specs.append(ckj)
        args.append(ck)
        out_specs += [ckj, whole(s)]
        out_shape += [jax.ShapeDtypeStruct((nb, 8, t), F32), jax.ShapeDtypeStruct((s, LANES * nb), F32)]
    if rope:
        in_specs += [qrow(lambda b: qrc), krow(lambda b: 0)]
        args += [qr, kr]
        out_specs += [pl.BlockSpec((s, LANES), lambda b, j, i: (0, 0)), pl.BlockSpec((t, LANES), lambda b, j, i: (0, 0))]
        out_shape += [jax.ShapeDtypeStruct((s, LANES), F32), jax.ShapeDtypeStruct((t, LANES), F32)]
    if bias:
        scratch.append(pltpu.VMEM((8, bk), F32))
    if comm is not None:
        in_specs += [ANY] * len(comm.ins)
        args += comm.ins
        out_specs += [ANY] * len(comm.out_shapes)
        out_shape += comm.out_shapes
        scratch += _sems(comm.n_sems, comm.n_sems)
    res = pl.pallas_call(body, name=name, grid=(nb, n1, n2), in_specs=in_specs, out_specs=out_specs,
                         out_shape=out_shape, scratch_shapes=scratch,
                         compiler_params=_params('arbitrary', 'arbitrary', 'arbitrary'))(*args)
    return res if comm is None else (*res[:n_out], res[n_out:])


def _gla_chunk(la_c, k_c):
    r = lax.broadcasted_iota(jnp.int32, (CHUNK, CHUNK), 0)
    c = lax.broadcasted_iota(jnp.int32, (CHUNK, CHUNK), 1)
    tri = jnp.where(c <= r, 1.0, 0.0).astype(BF16)
    cum = _tri_dot(tri, la_c)
    end = jnp.sum(la_c, axis=0, keepdims=True)
    dec = jnp.exp(end - cum)
    return dec, k_c * dec, jnp.exp(end)


GLA_PAIRS = GLA_HEADS // 2


def _gla_fwd(z, la, *, qc, kc, vc, name, blk=512):
    s = z.shape[0]
    bs = min(blk, s)
    ncb = bs // CHUNK
    nblk = s // bs

    def body(q_ref, k_ref, va_ref, vb_ref, la_ref, o_ref, st_ref, st):
        @pl.when(pl.program_id(1) == 0)
        def _():
            st[...] = jnp.zeros_like(st)

        heads, _ = _lane_masks(2, 0, False)
        v_refs = (va_ref, vb_ref)
        for c in range(ncb):
            sl = pl.ds(c * CHUNK, CHUNK)
            _, kf, a = _gla_chunk(la_ref[sl, :], k_ref[sl, :])
            qs = q_ref[sl, :] * (GLA_DK ** -0.5)
            for hh in range(2):
                ut = _dot(v_refs[hh][sl, :].astype(BF16), _sel(heads[hh], kf).astype(BF16), TN)
                new = a * st[hh] + ut
                st[hh] = new
                st_ref[0, c, hh] = new
                o_ref[sl, hh * GLA_DV:(hh + 1) * GLA_DV] = _dot(_sel(heads[hh], qs).astype(BF16), new.astype(BF16), NT)

    col = lambda c0, m=1: pl.BlockSpec((bs, LANES), lambda b, i: (i, c0 + m * b))
    return pl.pallas_call(
        body, name=name, grid=(GLA_PAIRS, nblk),
        in_specs=[col(qc), col(kc), col(vc, 2), col(vc + 1, 2), col(0)],
        out_specs=[pl.BlockSpec((bs, 2 * GLA_DV), lambda b, i: (i, b)),
                   pl.BlockSpec((1, ncb, 2, GLA_DV, LANES), lambda b, i: (b, i, 0, 0, 0))],
        out_shape=[jax.ShapeDtypeStruct((s, GLA_HEADS * GLA_DV), F32),
                   jax.ShapeDtypeStruct((GLA_PAIRS, s // CHUNK, 2, GLA_DV, LANES), F32)],
        scratch_shapes=[pltpu.VMEM((2, GLA_DV, LANES), F32)],
        compiler_params=_params('arbitrary', 'arbitrary'))(z, z, z, z, la)


def _gla_bwd(z, la, st_all, st_prev, do, *, qc, kc, vc, name, blk=512):
    s = z.shape[0]
    bs = min(blk, s)
    ncb = bs // CHUNK
    nblk = s // bs

    def body(q_ref, k_ref, va_ref, vb_ref, la_ref, st_ref, sp_ref, do_ref, dq_ref, dk_ref, dv_ref, dla_ref, ga):
        @pl.when(pl.program_id(1) == 0)
        def _():
            ga[...] = jnp.zeros_like(ga)

        r = lax.broadcasted_iota(jnp.int32, (CHUNK, CHUNK), 0)
        cc = lax.broadcasted_iota(jnp.int32, (CHUNK, CHUNK), 1)
        tri_rev = jnp.where(cc >= r, 1.0, 0.0).astype(BF16)
        heads, _ = _lane_masks(2, 0, False)
        v_refs = (va_ref, vb_ref)
        for c in reversed(range(ncb)):
            sl = pl.ds(c * CHUNK, CHUNK)
            dec, kf, a = _gla_chunk(la_ref[sl, :], k_ref[sl, :])
            qs = q_ref[sl, :] * (GLA_DK ** -0.5)
            dq2 = jnp.zeros((CHUNK, LANES), F32)
            dkd = jnp.zeros((CHUNK, LANES), F32)
            da = jnp.zeros((1, LANES), F32)
            for hh in range(2):
                hv = slice(hh * GLA_DV, (hh + 1) * GLA_DV)
                dob = do_ref[sl, hv].astype(BF16)
                g = _dot(dob, _sel(heads[hh], qs).astype(BF16), TN) + ga[hh]
                gb = g.astype(BF16)
                dq2 = dq2 + _dot(dob, st_ref[0, c, hh].astype(BF16))
                dv_ref[sl, hv] = _dot(_sel(heads[hh], kf).astype(BF16), gb, NT)
                dkd = dkd + _dot(v_refs[hh][sl, :].astype(BF16), gb)
                da = da + jnp.sum(g * sp_ref[0, c, hh], axis=0, keepdims=True)
                ga[hh] = a * g
            dq_ref[sl, :] = (GLA_DK ** -0.5) * dq2
            dk_ref[sl, :] = dkd * dec
            e = dkd * kf
            dend = jnp.sum(e, axis=0, keepdims=True) + da * a
            dla_ref[sl, :] = dend - _tri_dot(tri_rev, e)

    rev = lambda i: nblk - 1 - i
    col = lambda c0, m=1: pl.BlockSpec((bs, LANES), lambda b, i: (rev(i), c0 + m * b))
    wide = pl.BlockSpec((bs, 2 * GLA_DV), lambda b, i: (rev(i), b))
    stspec = pl.BlockSpec((1, ncb, 2, GLA_DV, LANES), lambda b, i: (b, rev(i), 0, 0, 0))
    return pl.pallas_call(
        body, name=name, grid=(GLA_PAIRS, nblk),
        in_specs=[col(qc), col(kc), col(vc, 2), col(vc + 1, 2), col(0), stspec, stspec, wide],
        out_specs=[col(0), col(0), wide, col(0)],
        out_shape=[jax.ShapeDtypeStruct((s, GLA_HEADS * GLA_DK), F32), jax.ShapeDtypeStruct((s, GLA_HEADS * GLA_DK), F32),
                   jax.ShapeDtypeStruct((s, GLA_HEADS * GLA_DV), F32), jax.ShapeDtypeStruct((s, GLA_HEADS * GLA_DK), F32)],
        scratch_shapes=[pltpu.VMEM((2, GLA_DV, LANES), F32)],
        compiler_params=_params('arbitrary', 'arbitrary'))(z, z, z, z, la, st_all, st_prev, do)


def _place():
    return lax.axis_index('x'), lax.axis_index('y'), lax.axis_index('c')


ANY = pl.BlockSpec(memory_space=pl.ANY)


def _all_gather8(blk, *, name):
    m, n = blk.shape

    def body(x_ref, out_ref, send_sems, recv_sems, local_sem):
        x, y, c = _place()
        me, sibling = (x, y, c), (x, y, 1 - c)
        chips = [(1 - x, y), (x, 1 - y), (1 - x, 1 - y)]

        def slot(px, py, pc):
            return out_ref.at[4 * px + 2 * py + pc]

        def copy(q, block, to, src=None):
            return pltpu.make_async_remote_copy(
                src_ref=slot(*block) if src is None else src, dst_ref=slot(*block), send_sem=send_sems.at[q],
                recv_sem=recv_sems.at[q], device_id=to, device_id_type=MESH)

        mine = pltpu.make_async_copy(x_ref, slot(*me), local_sem)
        mine.start()
        first = [copy(0, me, sibling, src=x_ref)]
        first += [copy(1 + q, me, (*chip, c), src=x_ref) for q, chip in enumerate(chips)]
        for cp in first:
            cp.start()
        passed = [copy(4 + q, (*chip, c), sibling) for q, chip in enumerate(chips)]
        for q, chip in enumerate(chips):
            copy(1 + q, (*chip, c), me).wait_recv()
            passed[q].start()
        copy(0, sibling, me).wait_recv()
        for q, chip in enumerate(chips):
            copy(4 + q, (*chip, 1 - c), me).wait_recv()
        for cp in first + passed:
            cp.wait_send()
        mine.wait()

    return pl.pallas_call(
        body, name=name, in_specs=[ANY], out_specs=ANY, out_shape=jax.ShapeDtypeStruct((N_DEV, m, n), blk.dtype),
        scratch_shapes=[pltpu.SemaphoreType.DMA((7,)), pltpu.SemaphoreType.DMA((7,)), pltpu.SemaphoreType.DMA(())],
    )(blk)


def _sems(*counts):
    return [pltpu.SemaphoreType.DMA((n,)) for n in counts]


class Comm(typing.NamedTuple):
    ins: list
    out_shapes: list
    n_sems: int
    start: typing.Callable
    finish: typing.Callable


def _remote(src, dst, send_sems, recv_sems, idx, to):
    return lambda: pltpu.make_async_remote_copy(src_ref=src, dst_ref=dst, send_sem=send_sems.at[idx],
                                                recv_sem=recv_sems.at[idx], device_id=to, device_id_type=MESH)


def _comm_from(copies, ins, out_shapes, n_sems):
    def start(*refs):
        for cp in copies(*refs)[0]:
            cp().start()

    def finish(*refs):
        sent, received = copies(*refs)
        for cp in received:
            cp().wait_recv()
        for cp in sent:
            cp().wait_send()

    return Comm(list(ins), list(out_shapes), n_sems, start, finish)


def _run_comm(comm, *, name, alias=False):
    n_in, n_out = len(comm.ins), len(comm.out_shapes)

    def body(*refs):
        ins, outs, sems = refs[:n_in], refs[n_in:n_in + n_out], refs[n_in + n_out:]
        comm.start(ins, outs, *sems)
        comm.finish(ins, outs, *sems)

    return pl.pallas_call(body, name=name, in_specs=[ANY] * n_in, out_specs=[ANY] * n_out, out_shape=comm.out_shapes,
                          input_output_aliases={q: q for q in range(n_in)} if alias else {},
                          scratch_shapes=_sems(comm.n_sems, comm.n_sems))(*comm.ins)


def _half(rows, c):
    h = rows // 2
    return pl.ds(pl.multiple_of(c * h, h), h)


def _gathered(ref, chip, rows, side):
    if not side:
        return ref.at[chip, rows]
    n = ref.shape[1] // N_CHIPS
    return ref.at[rows, pl.ds(pl.multiple_of(chip * n, n), n)]


def _gather_over_ici(ws, side):
    def copies(ins, outs, send_sems, recv_sems):
        x, y, c = _place()
        me_chip = 2 * x + y
        sent, received = [], []
        for q, w in enumerate(ws):
            half, every = _half(w.shape[0], c), pl.ds(0, w.shape[0])
            for k, (px, py) in enumerate([(1 - x, y), (x, 1 - y), (1 - x, 1 - y)]):
                sent.append(_remote(ins[q].at[half], _gathered(outs[q], me_chip, half, side[q]), send_sems, recv_sems,
                                    4 * q + k, (px, py, c)))
                slot = _gathered(outs[q], 2 * px + py, half, side[q])
                received.append(_remote(slot, slot, send_sems, recv_sems, 4 * q + k, (px, py, c)))
            whole = _remote(ins[q], _gathered(outs[q], me_chip, every, side[q]), send_sems, recv_sems, 4 * q + 3,
                            (x, y, 1 - c))
            sent.append(whole)
            received.append(whole)
        return sent, received

    shapes = [jax.ShapeDtypeStruct((w.shape[0], N_CHIPS * w.shape[1]) if sd else (N_CHIPS,) + w.shape, w.dtype)
              for w, sd in zip(ws, side)]
    return _comm_from(copies, ws, shapes, 4 * len(ws))


def _gather_over_d2d(parts, side):
    def copies(ins, outs, send_sems, recv_sems):
        x, y, c = _place()
        sent, received = [], []
        for q, w in enumerate(parts):
            rows = w.shape[0] if side[q] else w.shape[1]
            for k, (px, py) in enumerate([(1 - x, y), (x, 1 - y), (1 - x, 1 - y)]):
                mine = _gathered(outs[q], 2 * px + py, _half(rows, c), side[q])
                theirs = _gathered(outs[q], 2 * px + py, _half(rows, 1 - c), side[q])
                sent.append(_remote(mine, mine, send_sems, recv_sems, 3 * q + k, (x, y, 1 - c)))
                received.append(_remote(theirs, theirs, send_sems, recv_sems, 3 * q + k, (x, y, 1 - c)))
        return sent, received

    return _comm_from(copies, parts, [jax.ShapeDtypeStruct(w.shape, w.dtype) for w in parts], 3 * len(parts))


def _to_sibling(gs, *, name):
    n = len(gs)

    def body(*refs):
        ins, outs = refs[:n], refs[n:2 * n]
        send_sems, recv_sems = refs[2 * n:]
        x, y, c = _place()
        cps = [pltpu.make_async_remote_copy(
            src_ref=ins[q], dst_ref=outs[q], send_sem=send_sems.at[q], recv_sem=recv_sems.at[q],
            device_id=(x, y, 1 - c), device_id_type=MESH) for q in range(n)]
        for cp in cps:
            cp.start()
        for cp in cps:
            cp.wait()

    return pl.pallas_call(body, name=name, in_specs=[ANY] * n, out_specs=[ANY] * n,
                          out_shape=[jax.ShapeDtypeStruct(g.shape, g.dtype) for g in gs],
                          scratch_shapes=_sems(n, n))(*gs)


def _chip_exchange(ps):
    def copies(ins, outs, send_sems, recv_sems):
        x, y, c = _place()
        cps = [_remote(ins[q].at[2 * px + py], outs[q].at[k], send_sems, recv_sems, 3 * q + k, (px, py, c))
               for q in range(len(ps)) for k, (px, py) in enumerate([(1 - x, y), (x, 1 - y), (1 - x, 1 - y)])]
        return cps, cps

    return _comm_from(copies, ps, [jax.ShapeDtypeStruct((3,) + p.shape[1:], p.dtype) for p in ps], 3 * len(ps))


def _sum_chips(own, r, *, name, ts=256):
    k, n = own.shape
    ts = min(ts, k)

    def body(own_ref, r_ref, o_ref):
        f = lambda q: r_ref[q].astype(F32)
        o_ref[...] = ((own_ref[...].astype(F32) + f(0)) + f(1)) + f(2)

    return pl.pallas_call(
        body, name=name, grid=(k // ts,),
        in_specs=[pl.BlockSpec((ts, n), lambda i: (i, 0)), pl.BlockSpec((3, ts, n), lambda i: (0, i, 0))],
        out_specs=pl.BlockSpec((ts, n), lambda i: (i, 0)), out_shape=jax.ShapeDtypeStruct((k, n), F32),
        compiler_params=_params('arbitrary'))(own, r)


WIN_SHARD = N_IN // N_CHIPS
WIN_PAD = -(-WIN_SHARD // LANES) * LANES
GATE_WIRE_ROWS = 32


def _full_layer(sh, axis):
    _, k, n = sh.shape
    if axis == 2:
        return sh.transpose(1, 0, 2).reshape(k, N_CHIPS * n)
    return sh.reshape(N_CHIPS * k, n)


def _win_cols(wp, o, n):
    parts = []
    while n > 0:
        j, r = divmod(o, WIN_SHARD)
        take = min(n, WIN_SHARD - r)
        parts.append(wp[:, j * WIN_PAD + r:j * WIN_PAD + r + take])
        o, n = o + take, n - take
    return parts[0] if len(parts) == 1 else jnp.concatenate(parts, axis=1)


def _split_full(full, axis):
    k, n = full.shape
    if axis == 2:
        return jnp.stack([full[:, j * (n // N_CHIPS):(j + 1) * (n // N_CHIPS)] for j in range(N_CHIPS)])
    return full.reshape(N_CHIPS, k // N_CHIPS, n)


def _padc(a, w):
    return jnp.pad(a, ((0, 0), (0, w - a.shape[1])))


def _swap16(a):
    return jnp.concatenate([a[..., 16:32], a[..., 0:16]], axis=-1)


B_GR, B_GQ, B_GK, B_GV, B_MQ, B_MKR, B_MKRS, B_FF, B_GLOW, B_MKV, B_END = (
    0, 512, 768, 1024, 1536, 1792, 1920, 2048, 2176, 2304, 2432)
B_W = 2560
O_FQ, O_FF, O_GQ, O_GLOW, O_GR, O_MQ, O_MKV, O_MKR, O_ZG = 0, 768, 772, 1796, 1812, 2324, 2580, 2708, 2740


def _repack_layer_weights(w):
    wi = functools.partial(_win_cols, w['w_in'])
    out = dict(w)
    out['in_a'] = jnp.concatenate([wi(O_FQ, 256) * FOX_SCALE, wi(O_FQ + 256, 512)], axis=1)
    kr = wi(O_MKR, 32)
    out['in_b'] = jnp.concatenate([
        wi(O_GR, 512), wi(O_GQ, 1024), wi(O_MQ, 256), jnp.tile(kr, (1, MLA_HEADS)), jnp.tile(_swap16(kr), (1, MLA_HEADS)),
        _padc(wi(O_FF, 4), 128), _padc(wi(O_GLOW, 16), 128), wi(O_MKV, 128),
        jnp.zeros((D_MODEL, B_W - B_END), kr.dtype)], axis=1)
    out['in_c'] = wi(O_ZG, 3072)
    uq = w['w_mla_uq'].reshape(MLA_Q_RANK, MLA_HEADS, MLA_NOPE + MLA_ROPE)
    rope = uq[:, :, MLA_NOPE:]
    out['uq'] = jnp.concatenate([uq[:, :, :MLA_NOPE].reshape(MLA_Q_RANK, -1), rope.reshape(MLA_Q_RANK, -1),
                                 _swap16(rope).reshape(MLA_Q_RANK, -1)], axis=1)
    ukv = w['w_mla_ukv'].reshape(MLA_KV_RANK, MLA_HEADS, MLA_NOPE + MLA_VD)
    out['ukv'] = jnp.concatenate([ukv[:, :, :MLA_NOPE].reshape(MLA_KV_RANK, -1),
                                  ukv[:, :, MLA_NOPE:].reshape(MLA_KV_RANK, -1)], axis=1)
    out['gate'] = jnp.pad(w['w_gla_gate'], ((0, 128 - GLA_RANK), (0, 0)))
    return out


def _unpack_layer_grads(g):
    a, b, c = g['in_a'], g['in_b'], g['in_c']
    a = jnp.concatenate([a[:, :256] * FOX_SCALE, a[:, 256:]], axis=1)
    fold = lambda o: sum(b[:, o + MLA_ROPE * q:o + MLA_ROPE * (q + 1)] for q in range(MLA_HEADS))
    kr = fold(B_MKR) + _swap16(fold(B_MKRS))
    w_in = jnp.concatenate([a, b[:, B_FF:B_FF + 4], b[:, B_GQ:B_GQ + 1024], b[:, B_GLOW:B_GLOW + 16],
                            b[:, B_GR:B_GR + 512], b[:, B_MQ:B_MQ + 256], b[:, B_MKV:B_MKV + 128], kr, c], axis=1)
    uq = g['uq']
    nope = uq[:, :256].reshape(MLA_Q_RANK, MLA_HEADS, MLA_NOPE)
    rope = (uq[:, 256:384].reshape(MLA_Q_RANK, MLA_HEADS, MLA_ROPE)
            + _swap16(uq[:, 384:512].reshape(MLA_Q_RANK, MLA_HEADS, MLA_ROPE)))
    w_uq = jnp.concatenate([nope, rope], axis=2).reshape(MLA_Q_RANK, -1)
    ukv = g['ukv']
    w_ukv = jnp.concatenate([ukv[:, :256].reshape(MLA_KV_RANK, MLA_HEADS, MLA_NOPE),
                             ukv[:, 256:].reshape(MLA_KV_RANK, MLA_HEADS, MLA_VD)], axis=2).reshape(MLA_KV_RANK, -1)
    out = {'w_in': w_in, 'w_mla_uq': w_uq, 'w_mla_ukv': w_ukv, 'w_gla_gate': g['gate'][:GLA_RANK]}
    for nm in ('w_up_fox', 'w_up_gla', 'w_up_mla', 'w_out', 'w_xq', 'w_xkv', 'w_xo', 'w_mlp1', 'w_mlp2'):
        out[nm] = g[nm]
    return out


def _rope_tables(s):
    half = MLA_ROPE // 2
    inv = ROPE_BASE ** (-jnp.arange(half, dtype=F32) / half)
    ang = jnp.arange(s).astype(F32)[:, None] * inv[None, :]
    cos, sin = jnp.cos(ang), jnp.sin(ang)
    c1 = jnp.concatenate([cos, cos], axis=1)
    s1 = jnp.concatenate([-sin, sin], axis=1)
    return jnp.tile(c1, (1, MLA_HEADS)), jnp.tile(s1, (1, MLA_HEADS))


def _rms_bwd(x, dh, g):
    r = lax.rsqrt(jnp.mean(x * x, axis=-1, keepdims=True) + EPS)
    xh = x * r
    gd = dh * g
    return r * (gd - xh * jnp.mean(gd * xh, axis=-1, keepdims=True)), dh * xh


def _norm_bwd_epilogue(dh, x, dres, g):
    dx, dg = _rms_bwd(x, dh, g)
    return dres + dx, dg


def _norm_bwd_call(x, dh, g, dres, name):
    w = x.width if isinstance(x, Cols) else x.shape[1]

    def with_res(xv, dv, rv, gv):
        dx, dg = _rms_bwd(xv, dv.astype(F32), gv)
        return rv + dx, dg

    def plain(xv, dv, gv):
        return _rms_bwd(xv, dv.astype(F32), gv)

    if dres is None:
        return _rowwise(plain, [x, dh], [g], [(w, F32)], [w], name=name)
    return _rowwise(with_res, [x, dh, dres], [g], [(w, F32)], [w], name=name)


def _gla_out_fwd(oraw, gr, g_out):
    outs = []
    for hh in range(GLA_HEADS):
        sl = slice(hh * GLA_DV, (hh + 1) * GLA_DV)
        oh = oraw[:, sl]
        n = oh * lax.rsqrt(jnp.mean(oh * oh, axis=-1, keepdims=True) + EPS) * g_out
        r = gr[:, sl]
        outs.append(n * (r * _sig(r)))
    return (jnp.concatenate(outs, axis=1),)


def _gla_out_bwd(oraw, gr, dout, g_out):
    d_o, d_r, dg = [], [], 0.0
    for hh in range(GLA_HEADS):
        sl = slice(hh * GLA_DV, (hh + 1) * GLA_DV)
        oh, r, do = oraw[:, sl], gr[:, sl], dout[:, sl].astype(F32)
        rs = lax.rsqrt(jnp.mean(oh * oh, axis=-1, keepdims=True) + EPS)
        sg = _sig(r)
        dn = do * (r * sg)
        d_r.append(do * (oh * rs * g_out) * (sg + r * sg * (1.0 - sg)))
        dx, dgh = _rms_bwd(oh, dn, g_out)
        d_o.append(dx)
        dg = dg + dgh
    return jnp.concatenate(d_o, axis=1), jnp.concatenate(d_r, axis=1), dg


def _adam(w, g, m, v):
    m = ADAM_B1 * m + (1.0 - ADAM_B1) * g
    v = ADAM_B2 * v + (1.0 - ADAM_B2) * (g * g)
    m_hat = m / (1.0 - ADAM_B1 ** ADAM_STEP)
    v_hat = v / (1.0 - ADAM_B2 ** ADAM_STEP)
    return -ADAM_LR * (m_hat / (jnp.sqrt(v_hat) + ADAM_EPS) + ADAM_WD * w), m, v


def _layer_fwd(x, mem, w, p, tabs, tag, carry_fox=None, after_fox=None, carry_mla=None):
    c4, s4 = tabs
    sv = {'x0': x}
    nm = lambda t: f'{t}_{tag}'
    za, h = _mm(x, w['in_a'], mode='nn', out_dtype=BF16, norm_g=p['g_mix'], emit_norm=True, name=nm('in_a'))
    zb = _mm(h, w['in_b'], mode='nn', out_dtype=F32, name=nm('in_b'))
    zc = _mm(h, w['in_c'], mode='nn', out_dtype=BF16, name=nm('in_c'))
    sv.update(h=h, zc=zc)
    ff = Cols(zb, 128, B_FF // 128)
    (lf,) = _rowwise(lambda f, b: (_logsig(f + b),), [ff], [p['b_fox']], [(128, F32)], name=nm('fox_lf'))
    cum = _cumsum_rows(lf, reverse=False, name=nm('fox_cum'))
    ckf = jnp.pad(cum[:, :FOX_HEADS].T.reshape(2, 2, x.shape[0]), ((0, 0), (0, 6), (0, 0)))
    fox = dict(qc=0, kc=2, vc=4, nb=2, g=2, mode='causal', ck=ckf)
    o_fox, lse_fox, *carried = _mattn_fwd(za, za, za, name=nm('fox_attn'), comm=carry_fox, **fox)
    if after_fox is not None:
        w = {**w, **after_fox(carried[0])}
    sv.update(ff=ff, za=za, fox=fox, o_fox=o_fox, lse_fox=lse_fox)
    glow = Cols(zb, 128, B_GLOW // 128)
    gr = Cols(zb, 512, B_GR // 512)

    def gate_fn(gl, wg, bg):
        return (_logsig(_dot(gl.astype(BF16), wg) + bg) / GLA_TAU,)

    (la,) = _rowwise(gate_fn, [glow], [w['gate'], p['b_gla']], [(256, F32)], name=nm('gla_gate'))
    gla = dict(qc=B_GQ // LANES, kc=B_GK // LANES, vc=B_GV // LANES)
    oraw, states = _gla_fwd(zb, la, name=nm('gla'), **gla)
    (o_gla,) = _rowwise(_gla_out_fwd, [oraw, gr], [p['g_gla_out']], [(512, BF16)], name=nm('gla_out'))
    sv.update(glow=glow, gr=gr, zb=zb, la=la, gla=gla, states=states, oraw=oraw, o_gla=o_gla)
    mq = Cols(zb, 256, B_MQ // 256)
    mkv = Cols(zb, 128, B_MKV // 128)
    mkr2 = Cols(zb, 256, B_MKR // 256)
    qp, cqn = _mm(mq, w['uq'], mode='nn', out_dtype=F32, norm_g=p['g_mla_q'], emit_norm=True, name=nm('mla_uq'))
    kvp, ckvn = _mm(mkv, w['ukv'], mode='nn', out_dtype=BF16, norm_g=p['g_mla_kv'], emit_norm=True,
                    name=nm('mla_ukv'))

    def rope_fn(qv, kr, c4v, s4v):
        q_rope = qv[:, 256:384] * c4v + qv[:, 384:512] * s4v
        q_scaled = jnp.concatenate([qv[:, 0:256], q_rope], axis=1) * MLA_SCALE
        return q_scaled, kr[:, 0:128] * c4v + kr[:, 128:256] * s4v

    qall, kr4 = _rowwise(rope_fn, [qp, mkr2, c4, s4], [], [(384, BF16), (128, BF16)], name=nm('rope'))
    mla = dict(qc=0, kc=0, vc=2, nb=2, g=2, dq_scale=MLA_SCALE, mode='chunk', qr=qall, qrc=2, kr=kr4)
    o_mla, lse_mla, *carried = _mattn_fwd(qall, kvp, kvp, name=nm('mla_attn'), comm=carry_mla, **mla)
    if carry_mla is not None:
        sv['carried_mla'] = carried[0]
    sv.update(mq=mq, mkv=mkv, cqn=cqn, ckvn=ckvn, qall=qall, kvp=kvp, mla=mla, o_mla=o_mla, lse_mla=lse_mla)
    of_m, om_m = o_fox, o_mla
    sv.update(of_m=of_m, om_m=om_m)
    b_br = p['b_branch']

    def first(acc, zg, bb):
        return _sig(zg + bb) * acc

    def more(acc, zg, bb, prev):
        return prev + _sig(zg + bb) * acc

    y = _mm(of_m, w['w_up_fox'], mode='nn', out_dtype=F32, name=nm('up_fox'), epilogue=first,
            extras=[(zc, *_mn(col_off=0)), (b_br, *_nvec(col_off=0))])
    y = _mm(o_gla, w['w_up_gla'], mode='nn', out_dtype=F32, name=nm('up_gla'), epilogue=more,
            extras=[(zc, *_mn(col_off=1024)), (b_br, *_nvec(col_off=1024)), (y, *_mn())])
    y = _mm(om_m, w['w_up_mla'], mode='nn', out_dtype=BF16, name=nm('up_mla'), epilogue=more,
            extras=[(zc, *_mn(col_off=2048)), (b_br, *_nvec(col_off=2048)), (y, *_mn())])
    add = lambda acc, res: res + acc
    x1 = _mm(y, w['w_out'], mode='nn', out_dtype=F32, name=nm('out'), epilogue=add, extras=[(x, *_mn())])
    sv.update(y=y, x1=x1)
    qx, hx = _mm(x1, w['w_xq'], mode='nn', out_dtype=BF16, norm_g=p['g_xa'], emit_norm=True, name=nm('xq'),
                 epilogue=lambda acc: acc * XA_SCALE)
    kvx, mn = _mm(mem, w['w_xkv'], mode='nn', out_dtype=BF16, norm_g=p['g_mem'], emit_norm=True, name=nm('xkv'))
    xa = dict(qc=0, kc=0, vc=4, nb=4, g=1, dq_scale=XA_SCALE, mode='full')
    ox_m, lse_x = _mattn_fwd(qx, kvx, kvx, name=nm('xa_attn'), **xa)
    x2 = _mm(ox_m, w['w_xo'], mode='nn', out_dtype=F32, name=nm('xo'), epilogue=add, extras=[(x1, *_mn())])
    sv.update(hx=hx, mn=mn, qx=qx, kvx=kvx, xa=xa, lse_x=lse_x, ox_m=ox_m, x2=x2)
    hpre, hm = _mm(x2, w['w_mlp1'], mode='nn', out_dtype=BF16, norm_g=p['g_mlp'], emit_norm=True, name=nm('mlp1'))
    relu2 = lambda t: jnp.square(jnp.maximum(t.astype(F32), 0.0))
    x3 = _mm(hpre, w['w_mlp2'], mode='nn', out_dtype=F32, name=nm('mlp2'), a_fn=relu2, epilogue=add,
             extras=[(x2, *_mn())])
    sv.update(hpre=hpre, hm=hm, w=w)
    return x3, sv


EARLY = ('w_mlp1', 'w_mlp2', 'w_xo', 'w_xq', 'w_xkv', 'w_out', 'w_up_fox', 'w_up_gla', 'w_up_mla')
LATE = ('w_in', 'w_gla_gate', 'w_mla_uq', 'w_mla_ukv')


def _layer_bwd(dx3, mem, w, p, tabs, sv, tag, carry_mla=None, early=None):
    c4, s4 = tabs
    nm = lambda t: f'{t}_{tag}'
    s = dx3.shape[0]
    gw, gs = {}, {}
    relu2 = lambda t: jnp.square(jnp.maximum(t.astype(F32), 0.0))
    gw['w_mlp2'] = _mm(sv['hpre'], dx3, mode='tn', out_dtype=F32, name=nm('d_mlp2'), a_fn=relu2)
    dact = lambda acc, hp: acc * (2.0 * jnp.maximum(hp.astype(F32), 0.0))
    dhpre = _mm(dx3, w['w_mlp2'], mode='nt', out_dtype=BF16, name=nm('d_act'), epilogue=dact,
                extras=[(sv['hpre'], *_mn())])
    gw['w_mlp1'] = _mm(sv['hm'], dhpre, mode='tn', out_dtype=F32, name=nm('d_mlp1'))
    dx2, gs['g_mlp'] = _mm(dhpre, w['w_mlp1'], mode='nt', out_dtype=F32, name=nm('d_hm'), epilogue=_norm_bwd_epilogue,
                           col_sums=True, full_rows=True,
                           extras=[(sv['x2'], *_mn()), (dx3, *_mn()), (p['g_mlp'], *_nvec())])
    gw['w_xo'] = _mm(sv['ox_m'], dx2, mode='tn', out_dtype=F32, name=nm('d_xo'))
    dox = _mm(dx2, w['w_xo'], mode='nt', out_dtype=BF16, name=nm('d_ox'))
    dqx_m, dkx, dvx = _mattn_bwd(sv['qx'], sv['kvx'], sv['kvx'], sv['ox_m'], dox, sv['lse_x'], name=nm('xa_bwd'),
                                 **sv['xa'])
    dkvx = jnp.concatenate([dkx, dvx], axis=1).astype(BF16)
    gw['w_xq'] = _mm(sv['hx'], dqx_m, mode='tn', out_dtype=F32, name=nm('d_xq'))
    dx1, gs['g_xa'] = _mm(dqx_m, w['w_xq'], mode='nt', out_dtype=F32, name=nm('d_hx'), epilogue=_norm_bwd_epilogue,
                          col_sums=True, full_rows=True,
                          extras=[(sv['x1'], *_mn()), (dx2, *_mn()), (p['g_xa'], *_nvec())])
    gw['w_xkv'] = _mm(sv['mn'], dkvx, mode='tn', out_dtype=F32, name=nm('d_xkv'))
    dmn = _mm(dkvx, w['w_xkv'], mode='nt', out_dtype=F32, name=nm('d_mn'))
    _, gs['g_mem'] = _norm_bwd_call(mem, dmn, p['g_mem'], None, nm('d_norm_mem'))
    gw['w_out'] = _mm(sv['y'], dx1, mode='tn', out_dtype=F32, name=nm('d_out'))
    dy = _mm(dx1, w['w_out'], mode='nt', out_dtype=BF16, name=nm('d_y'))
    zc, b_br = sv['zc'], p['b_branch']

    def du_fn(dyv, zg, bb):
        g = _sig(zg + bb)
        d = dyv.astype(F32)
        return d * g[:, 0:1024], d * g[:, 1024:2048], d * g[:, 2048:3072]

    du = _rowwise(du_fn, [dy, zc], [b_br], [(D_MODEL, BF16)] * 3, name=nm('d_u'))

    def dgate(acc, dyv, zg, bb):
        g = _sig(zg + bb)
        return dyv.astype(F32) * acc * g * (1.0 - g)

    dzc, db_br, do_br = [], [], []
    for q, (o_m, wn) in enumerate(((sv['of_m'], 'w_up_fox'), (sv['o_gla'], 'w_up_gla'), (sv['om_m'], 'w_up_mla'))):
        dz, db = _mm(o_m, w[wn], mode='nn', out_dtype=BF16, name=nm(f'd_zg{q}'), epilogue=dgate, col_sums=True,
                     extras=[(dy, *_mn()), (zc, *_mn(col_off=1024 * q)), (b_br, *_nvec(col_off=1024 * q))])
        dzc.append(dz)
        db_br.append(db)
        gw[wn] = _mm(o_m, du[q], mode='tn', out_dtype=F32, name=nm(f'd_up{q}'))
        do_br.append(_mm(du[q], w[wn], mode='nt', out_dtype=F32 if q == 1 else BF16, name=nm(f'd_o{q}')))
    dzc = jnp.concatenate(dzc, axis=1)
    gs['b_branch'] = jnp.concatenate(db_br, axis=1)
    za = sv['za']
    carry_fox = None if early is None else early({nm_: gw[nm_] for nm_ in EARLY})
    dfq, dfk, dfv, dck, dcq, *carried_fox = _mattn_bwd(za, za, za, sv['o_fox'], do_br[0], sv['lse_fox'],
                                                       name=nm('fox_bwd'), comm=carry_fox, **sv['fox'])
    dcum = _padc(dck[:, :2, :].reshape(FOX_HEADS, s).T + dcq.reshape(s, 2, LANES)[:, :, :2].reshape(s, FOX_HEADS), 128)
    dlf = _cumsum_rows(dcum, reverse=True, name=nm('fox_dcum'))

    def dff_fn(dl, f, b):
        d = dl * _sig(-(f + b))
        return d, d

    dff, db_fox = _rowwise(dff_fn, [dlf, sv['ff']], [p['b_fox']], [(128, F32)], [128], name=nm('fox_dff'))
    gs['b_fox'] = db_fox
    dza = jnp.concatenate([dfq, dfk, dfv], axis=1).astype(BF16)
    dqn, dkn, dvv, dq_rope, dk_rope, *carried_mla = _mattn_bwd(sv['qall'], sv['kvp'], sv['kvp'], sv['o_mla'], do_br[2],
                                                               sv['lse_mla'], name=nm('mla_bwd'), comm=carry_mla,
                                                               **sv['mla'])

    def drope_fn(dn, dq, dk, c4v, s4v):
        return jnp.concatenate([dn, dq * c4v, dq * s4v], axis=1), jnp.concatenate([dk * c4v, dk * s4v], axis=1)

    dqp, dmkr2 = _rowwise(drope_fn, [dqn, dq_rope, dk_rope, c4, s4], [], [(512, BF16), (256, BF16)], name=nm('d_rope'))
    dkvp = jnp.concatenate([dkn, dvv], axis=1).astype(BF16)
    gw['uq'] = _mm(sv['cqn'], dqp, mode='tn', out_dtype=F32, name=nm('d_uq'))
    dcqn = _mm(dqp, w['uq'], mode='nt', out_dtype=F32, name=nm('d_cqn'))
    gw['ukv'] = _mm(sv['ckvn'], dkvp, mode='tn', out_dtype=F32, name=nm('d_ukv'))
    dckvn = _mm(dkvp, w['ukv'], mode='nt', out_dtype=F32, name=nm('d_ckvn'))
    dmq, gs['g_mla_q'] = _norm_bwd_call(sv['mq'], dcqn, p['g_mla_q'], None, nm('d_norm_q'))
    dmkv, gs['g_mla_kv'] = _norm_bwd_call(sv['mkv'], dckvn, p['g_mla_kv'], None, nm('d_norm_kv'))
    doraw, dgr, gs['g_gla_out'] = _rowwise(_gla_out_bwd, [sv['oraw'], sv['gr'], do_br[1]], [p['g_gla_out']],
                                           [(512, F32), (512, BF16)], [128], name=nm('d_gla_out'))
    st = sv['states']
    st_prev = jnp.concatenate([jnp.zeros_like(st[:, :1]), st[:, :-1]], axis=1)
    dgq, dgk, dgv, dla = _gla_bwd(sv['zb'], sv['la'], st, st_prev, doraw, name=nm('gla_bwd'), **sv['gla'])

    def dgate_fn(dl, gl, wg, bg):
        pre = _dot(gl.astype(BF16), wg) + bg
        dpre = dl * (1.0 / GLA_TAU) * _sig(-pre)
        return dpre, _dot(dpre.astype(BF16), wg, NT), dpre

    dpre, dglow, gs['b_gla'] = _rowwise(dgate_fn, [dla, sv['glow']], [w['gate'], p['b_gla']],
                                        [(256, BF16), (128, BF16)], [256], name=nm('d_gla_gate'))
    gw['gate'] = _mm(sv['glow'], dpre, mode='tn', out_dtype=F32, name=nm('d_wgate'))
    bf = lambda t: t.astype(BF16)
    dzb = jnp.concatenate([dgr, bf(dgq), bf(dgk), bf(dgv), bf(dmq), dmkr2, bf(dff), dglow, bf(dmkv),
                           jnp.zeros((s, B_W - B_END), BF16)], axis=1)
    h = sv['h']
    gw['in_a'] = _mm(h, dza, mode='tn', out_dtype=F32, name=nm('d_in_a'))
    gw['in_b'] = _mm(h, dzb, mode='tn', out_dtype=F32, name=nm('d_in_b'))
    gw['in_c'] = _mm(h, dzc, mode='tn', out_dtype=F32, name=nm('d_in_c'))
    add = lambda acc, prev: prev + acc
    dh = _mm(dza, w['in_a'], mode='nt', out_dtype=F32, name=nm('d_h_a'))
    dh = _mm(dzb, w['in_b'], mode='nt', out_dtype=F32, name=nm('d_h_b'), epilogue=add, extras=[(dh, *_mn())])
    dx0, gs['g_mix'] = _mm(dzc, w['in_c'], mode='nt', out_dtype=F32, name=nm('d_h_c'), col_sums=True, full_rows=True,
                           epilogue=lambda acc, prev, xv, rv, gv: _norm_bwd_epilogue(prev + acc, xv, rv, gv),
                           extras=[(dh, *_mn()), (sv['x0'], *_mn()), (dx1, *_mn()), (p['g_mix'], *_nvec())])
    return dx0, gw, gs, (carried_mla or [None])[0], (carried_fox or [None])[0]


def _loss_head(x, target, g_final):
    d = x.shape[1]

    def fn(xv, tv, gv):
        r = lax.rsqrt(jnp.mean(xv * xv, axis=-1, keepdims=True) + EPS)
        xh = xv * r
        e = xh * gv - tv
        dy = e * (1.0 / d)
        gd = dy * gv
        dx = r * (gd - xh * jnp.mean(gd * xh, axis=-1, keepdims=True))
        row_loss = 0.5 * jnp.mean(e * e, axis=-1, keepdims=True)
        return dx, dy * xh, jnp.broadcast_to(row_loss, (xv.shape[0], LANES))

    return _rowwise(fn, [x, target], [g_final], [(d, F32)], [d, LANES], name='loss_head')


def _small_sizes(shapes):
    return [math.prod(shapes[nm]) for nm in SMALL]


def _step(args):
    shapes = {nm: args[nm].shape for nm in ORDER}
    x, mem, target = args['x'][0], args['mem'][0], args['loss_target'][0]
    s = x.shape[0]

    def wire(nm, l):
        w = args[nm][l].astype(BF16)
        if nm == 'w_in':
            w = jnp.pad(w, ((0, 0), (0, WIN_PAD - WIN_SHARD)))
        if nm == 'w_gla_gate':
            w = jnp.pad(w, ((0, GATE_WIRE_ROWS - GLA_RANK), (0, 0)))
        return w

    axis_of = dict(BIG)
    names = tuple(nm for nm, _ in BIG)
    wires = lambda l, nms: [wire(nm, l) for nm in nms]
    width = lambda nm: WIN_PAD if nm == 'w_in' else args[nm].shape[2]
    side_by_side = lambda nms: [axis_of[nm] == 2 and width(nm) % LANES == 0 for nm in nms]
    over_ici = lambda l, nms: _gather_over_ici(wires(l, nms), side_by_side(nms))

    def whole(parts, nms, tag):
        side = side_by_side(nms)
        parts = _run_comm(_gather_over_d2d(parts, side), name=f'gather_d2d_{tag}', alias=True)
        full = {nm: p if sd else _full_layer(p, axis_of[nm]) for nm, p, sd in zip(nms, parts, side)}
        if 'w_gla_gate' in full:
            full['w_gla_gate'] = full['w_gla_gate'][:GLA_RANK]
        return full

    tabs = _rope_tables(s)
    layers_p = []
    for l in range(DEPTH):
        layers_p.append({
            'g_mix': args['g_mix'][l][None], 'b_fox': _padc(args['b_fox_forget'][l][None], 128),
            'b_gla': args['b_gla_gate'][l][None], 'g_gla_out': args['g_gla_out'][l][None],
            'g_mla_q': args['g_mla_q'][l][None], 'g_mla_kv': args['g_mla_kv'][l][None],
            'b_branch': args['b_branch_gate'][l][None], 'g_xa': args['g_xa'][l][None],
            'g_mem': args['g_mem'][l][None], 'g_mlp': args['g_mlp'][l][None]})

    first = _run_comm(over_ici(0, LATE), name='gather_ici_first_l0')
    w_now = _repack_layer_weights(whole(first, LATE, 'first_l0'))
    saved = []
    xl = x
    for l in range(DEPTH):
        carry_fox = over_ici(0, EARLY) if l == 0 else None
        after_fox = (lambda parts: whole(parts, EARLY, 'rest_l0')) if l == 0 else None
        carry_mla = over_ici(l + 1, names) if l + 1 < DEPTH else None
        xl, sv = _layer_fwd(xl, mem, w_now, layers_p[l], tabs, f'l{l}', carry_fox=carry_fox, after_fox=after_fox,
                            carry_mla=carry_mla)
        saved.append(sv)
        if carry_mla is not None:
            w_now = _repack_layer_weights(whole(sv.pop('carried_mla'), names, f'l{l + 1}'))
    dx, dg_final, loss_lanes = _loss_head(xl, target, args['g_final'][None])
    cidx = lax.axis_index('c')
    chip = 2 * lax.axis_index('x') + lax.axis_index('y')

    def pair_sums(gw, nms, tag):
        mine, theirs = [], []
        for nm in nms:
            shards = _split_full(gw[nm], axis_of[nm]).astype(BF16)
            h = shards.shape[1] // 2
            mine.append(lax.dynamic_slice_in_dim(shards, cidx * h, h, axis=1))
            theirs.append(lax.dynamic_slice_in_dim(shards, (1 - cidx) * h, h, axis=1))
        got = _to_sibling(theirs, name=f'grads_swap_{tag}')
        pairs = []
        for nm, a, b in zip(nms, mine, got):
            _, h, n = a.shape
            (p,) = _rowwise(lambda u, v: (u.astype(F32) + v.astype(F32),),
                            [a.reshape(N_CHIPS * h, n), b.reshape(N_CHIPS * h, n)], [], [(n, BF16)],
                            name=f'pair_sum_{nm}_{tag}')
            pairs.append(p.reshape(N_CHIPS, h, n))
        return pairs

    def finish(pairs, from_chips, nms, tag):
        own = [lax.dynamic_index_in_dim(p, chip, axis=0, keepdims=False) for p in pairs]
        mine = [_sum_chips(o, r, name=f'chip_sum_{nm}_{tag}') for nm, o, r in zip(nms, own, from_chips)]
        theirs = _to_sibling(mine, name=f'grads_join_{tag}')
        return {nm: jnp.where(cidx == 0, jnp.concatenate([a, b]), jnp.concatenate([b, a]))
                for nm, a, b in zip(nms, mine, theirs)}

    gs_layers, done = [None] * DEPTH, [{} for _ in range(DEPTH)]
    above = None
    for l in reversed(range(DEPTH)):
        lowest, early_pairs = l == 0, []

        def early(gw_early, l=l, early_pairs=early_pairs):
            early_pairs.extend(pair_sums(gw_early, EARLY, f'early_l{l}'))
            return _chip_exchange(early_pairs)

        carry_mla = None if above is None else _chip_exchange(above[1])
        dx, gw, gs_layers[l], got_mla, got_fox = _layer_bwd(
            dx, mem, saved[l]['w'], layers_p[l], tabs, saved[l], f'l{l}', carry_mla=carry_mla,
            early=early if lowest else None)
        if above is not None:
            done[above[0]].update(finish(above[1], got_mla, names, f'l{above[0]}'))
        grads = _unpack_layer_grads(gw)
        if lowest:
            done[l].update(finish(early_pairs, got_fox, EARLY, f'early_l{l}'))
            late_pairs = pair_sums(grads, LATE, f'late_l{l}')
            from_late = _run_comm(_chip_exchange(late_pairs), name=f'grads_exchange_late_l{l}')
            done[l].update(finish(late_pairs, from_late, LATE, f'late_l{l}'))
        else:
            above = (l, pair_sums(grads, names, f'l{l}'))
    grad_x = dx[None]
    gshard = {nm: jnp.stack([done[l][nm] for l in range(DEPTH)]) for nm in names}

    small_g = []
    for nm, key in (('g_mix', 'g_mix'), ('b_fox_forget', 'b_fox'), ('b_gla_gate', 'b_gla'),
                    ('g_gla_out', 'g_gla_out'), ('g_mla_q', 'g_mla_q'), ('g_mla_kv', 'g_mla_kv'),
                    ('b_branch_gate', 'b_branch'), ('g_xa', 'g_xa'), ('g_mem', 'g_mem'), ('g_mlp', 'g_mlp')):
        width = shapes[nm][1]
        small_g.append(jnp.concatenate([gs_layers[l][key][0, :width] for l in range(DEPTH)]))
    small_g.append(dg_final[0])
    small_g.append(loss_lanes[0, :1])
    flat = jnp.concatenate(small_g)
    n_small = flat.shape[0]
    srows = -(-n_small // (8 * LANES)) * 8
    pad = lambda v: jnp.pad(v, (0, srows * LANES - v.shape[0])).reshape(srows, LANES)
    all_small = _all_gather8(pad(flat), name='gather_small')
    sw, sm, svv = (pad(jnp.concatenate([args[pre + nm].reshape(-1) for nm in SMALL] + [jnp.zeros((1,), F32)]))
                   for pre in ('', 'm_', 'v_'))

    def small_body(g_ref, w_ref, m_ref, v_ref, go_ref, d_ref, mo_ref, vo_ref):
        g = g_ref[0]
        for q in range(1, N_DEV):
            g = g + g_ref[q]
        go_ref[...] = g
        d_ref[...], mo_ref[...], vo_ref[...] = _adam(w_ref[...], g, m_ref[...], v_ref[...])

    sg, sd, snm, snv = pl.pallas_call(
        small_body, name='small_sum_adam', out_shape=[jax.ShapeDtypeStruct((srows, LANES), F32)] * 4,
        compiler_params=pltpu.CompilerParams(vmem_limit_bytes=VMEM_LIMIT))(all_small, sw, sm, svv)

    def unsmall(buf):
        v, out, off = buf.reshape(-1), {}, 0
        for nm in SMALL:
            nel = math.prod(shapes[nm])
            out[nm] = v[off:off + nel].reshape(shapes[nm])
            off += nel
        return out, v[off]

    res = {}
    (res['grad'], loss), (res['delta'], _), (res['m'], _), (res['v'], _) = (unsmall(t) for t in (sg, sd, snm, snv))

    for nm, _ in BIG:
        shp = args[nm].shape
        view = lambda t: t.reshape(shp[0] * shp[1], shp[2])
        d, m2, v2 = _rowwise(_adam, [view(args[nm]), view(gshard[nm]), view(args['m_' + nm]), view(args['v_' + nm])],
                             [], [(shp[2], F32)] * 3, name=f'adam_{nm}')
        res['grad'][nm], res['delta'][nm], res['m'][nm], res['v'][nm] = (
            gshard[nm], d.reshape(shp), m2.reshape(shp), v2.reshape(shp))

    return (loss, grad_x, *[res['grad'][nm] for nm in ORDER], *[res['delta'][nm] for nm in ORDER],
            *[res['m'][nm] for nm in ORDER], *[res['v'][nm] for nm in ORDER])


def kernel(x, mem, g_mix, w_in, b_fox_forget, w_gla_gate, b_gla_gate, g_gla_out, g_mla_q, w_mla_uq, g_mla_kv, w_mla_ukv, b_branch_gate, w_up_fox, w_up_gla, w_up_mla, w_out, g_xa, g_mem, w_xq, w_xkv, w_xo, g_mlp, w_mlp1, w_mlp2, g_final, loss_target, m_g_mix, m_w_in, m_b_fox_forget, m_w_gla_gate, m_b_gla_gate, m_g_gla_out, m_g_mla_q, m_w_mla_uq, m_g_mla_kv, m_w_mla_ukv, m_b_branch_gate, m_w_up_fox, m_w_up_gla, m_w_up_mla, m_w_out, m_g_xa, m_g_mem, m_w_xq, m_w_xkv, m_w_xo, m_g_mlp, m_w_mlp1, m_w_mlp2, m_g_final, v_g_mix, v_w_in, v_b_fox_forget, v_w_gla_gate, v_b_gla_gate, v_g_gla_out, v_g_mla_q, v_w_mla_uq, v_g_mla_kv, v_w_mla_ukv, v_b_branch_gate, v_w_up_fox, v_w_up_gla, v_w_up_mla, v_w_out, v_g_xa, v_g_mem, v_w_xq, v_w_xkv, v_w_xo, v_g_mlp, v_w_mlp1, v_w_mlp2, v_g_final):
    return _step(dict(locals()))
```

```python
import functools
import math
import typing

import jax
import jax.numpy as jnp
from jax import lax
from jax.experimental import pallas as pl
from jax.experimental.pallas import tpu as pltpu

F32 = jnp.float32
BF16 = jnp.bfloat16
MESH = pl.DeviceIdType.MESH

D_MODEL = 1024
DEPTH = 2
CHUNK = 64
EPS = 1e-6
FOX_HEADS, FOX_HD = 4, 64
GLA_HEADS, GLA_DK, GLA_DV, GLA_RANK, GLA_TAU = 4, 64, 128, 16, 16.0
MLA_HEADS, MLA_Q_RANK, MLA_KV_RANK, MLA_NOPE, MLA_ROPE, MLA_VD = 4, 256, 128, 64, 32, 64
ROPE_BASE = 10000.0
XA_HEADS, XA_HD = 4, 128
D_FF = 4 * D_MODEL
IN_SIZES = (256, 256, 256, 4, 256, 256, 512, 16, 512, 256, 128, 32, 3072)
N_IN = sum(IN_SIZES)

ADAM_LR, ADAM_B1, ADAM_B2, ADAM_EPS, ADAM_WD, ADAM_STEP = 0.001, 0.9, 0.999, 1e-08, 0.01, 10

N_CHIPS = 4
N_DEV = 8
LANES = 128
VMEM_LIMIT = 48 * 1024 * 1024
MASK_VALUE = -1e30

BIG = (('w_in', 2), ('w_gla_gate', 2), ('w_mla_uq', 2), ('w_mla_ukv', 2), ('w_up_fox', 2), ('w_up_gla', 2),
       ('w_up_mla', 2), ('w_out', 1), ('w_xq', 1), ('w_xkv', 1), ('w_xo', 2), ('w_mlp1', 2), ('w_mlp2', 1))
SMALL = ('g_mix', 'b_fox_forget', 'b_gla_gate', 'g_gla_out', 'g_mla_q', 'g_mla_kv', 'b_branch_gate',
         'g_xa', 'g_mem', 'g_mlp', 'g_final')
ORDER = ('g_mix', 'w_in', 'b_fox_forget', 'w_gla_gate', 'b_gla_gate', 'g_gla_out', 'g_mla_q', 'w_mla_uq',
         'g_mla_kv', 'w_mla_ukv', 'b_branch_gate', 'w_up_fox', 'w_up_gla', 'w_up_mla', 'w_out', 'g_xa', 'g_mem',
         'w_xq', 'w_xkv', 'w_xo', 'g_mlp', 'w_mlp1', 'w_mlp2', 'g_final')


def _params(*sem):
    return pltpu.CompilerParams(dimension_semantics=sem, vmem_limit_bytes=VMEM_LIMIT)


def _sig(x):
    return 1.0 / (1.0 + jnp.exp(-x))


def _logsig(x):
    return jnp.minimum(x, 0.0) - jnp.log(1.0 + jnp.exp(-jnp.abs(x)))


NN = (((1,), (0,)), ((), ()))
NT = (((1,), (1,)), ((), ()))
TN = (((0,), (0,)), ((), ()))


def _dot(a, b, dims=NN):
    return lax.dot_general(a, b, dims, preferred_element_type=F32)


class Cols(typing.NamedTuple):
    arr: jax.Array
    width: int
    blk: int


def _tri_dot(tri, x):
    hi = x.astype(BF16)
    r1 = x - hi.astype(F32)
    mid = r1.astype(BF16)
    lo = (r1 - mid.astype(F32)).astype(BF16)
    return _dot(tri, hi) + _dot(tri, mid) + _dot(tri, lo)


MM_TILES = ((1024, 1024), (1024, 512), (512, 1024), (512, 512), (256, 1024), (512, 256), (256, 512), (256, 256),
            (128, 1024), (128, 128))
MM_VMEM_BUDGET = 38 * 1024 * 1024


def _mm_tiles(m, n, k, a_bytes, b_bytes, out_bytes, ex_bytes, has_norm, emit_norm, has_fn, full_rows):
    for tm, tn in MM_TILES:
        tm, tn = min(tm, m), min(tn, n)
        if m % tm or n % tn or (full_rows and tn != n):
            continue
        blocks = tm * k * a_bytes + k * tn * b_bytes + tm * tn * (out_bytes + ex_bytes) + (tm * k * 2 if emit_norm else 0)
        temps = tm * tn * 4 + (tm * k * 2 if has_norm else 0) + (tm * k * 6 if has_fn or has_norm else 0)
        if 2 * blocks + temps <= MM_VMEM_BUDGET:
            return tm, tn
    raise ValueError((m, n, k))


def _mm(a, b, *, mode, out_dtype, name, norm_g=None, emit_norm=False, a_fn=None, extras=(), epilogue=None,
        col_sums=False, full_rows=False):
    a_blk = 0
    if isinstance(a, Cols):
        a, width, a_blk = a
        a_shape = (a.shape[0], width)
    else:
        a_shape = a.shape
    if mode == 'tn':
        k, m = a_shape
    else:
        m, k = a_shape
    n = b.shape[0] if mode == 'nt' else b.shape[1]
    assert (b.shape[1] if mode == 'nt' else b.shape[0]) == k, (name, a.shape, b.shape)
    has_norm = norm_g is not None
    ex_bytes = sum(arr.dtype.itemsize for arr, kind, _ in extras if kind == 'mn')
    tm, tn = _mm_tiles(m, n, k, a.dtype.itemsize, b.dtype.itemsize, jnp.dtype(out_dtype).itemsize, ex_bytes, has_norm,
                       emit_norm, a_fn is not None, full_rows)
    assert all(col % tn == 0 for _, _, col in extras), (name, tn)
    assert a_blk == 0 or (mode == 'nn') or (mode == 'tn' and tm == m)
    assert not (col_sums and (has_norm or emit_norm))
    ij = (lambda f: lambda g0, g1: f(g1, g0)) if col_sums else (lambda f: f)
    spec = lambda blk, f: pl.BlockSpec(blk, ij(f))
    if mode == 'tn':
        a_spec = spec((k, tm), lambda i, j: (0, i + a_blk))
    else:
        a_spec = spec((tm, k), lambda i, j: (i, a_blk))
    b_spec = spec((tn, k), lambda i, j: (j, 0)) if mode == 'nt' else spec((k, tn), lambda i, j: (0, j))
    dims = {'nn': NN, 'nt': NT, 'tn': TN}[mode]
    assert not (has_norm and mode != 'nn')
    n_ex = len(extras)

    def body(*refs):
        a_ref, b_ref = refs[0], refs[1]
        pos = 2
        g_ref = None
        if has_norm:
            g_ref = refs[pos]
            pos += 1
        ex_refs = refs[pos:pos + n_ex]
        pos += n_ex
        o_ref = refs[pos]
        pos += 1
        h_ref = None
        if emit_norm:
            h_ref = refs[pos]
            pos += 1
        if has_norm:
            an_ref = refs[pos]

            @pl.when(pl.program_id(1) == 0)
            def _():
                xf = a_ref[...].astype(F32)
                y = xf * lax.rsqrt(jnp.mean(xf * xf, axis=-1, keepdims=True) + EPS) * g_ref[...]
                an_ref[...] = y.astype(BF16)
                if emit_norm:
                    h_ref[...] = y.astype(BF16)

            av = an_ref[...]
        else:
            av = a_ref[...]
            if a_fn is not None:
                av = a_fn(av)
            av = av.astype(BF16)
        acc = _dot(av, b_ref[...].astype(BF16), dims)
        if epilogue is not None:
            acc = epilogue(acc, *[r[...] for r in ex_refs])
        acc, to_sum = acc if isinstance(acc, tuple) else (acc, acc)
        o_ref[...] = acc.astype(out_dtype)
        if col_sums:
            sum_ref = refs[pos]

            @pl.when(pl.program_id(1) == 0)
            def _():
                sum_ref[...] = jnp.zeros_like(sum_ref)

            sum_ref[...] += jnp.sum(to_sum, axis=0, keepdims=True)

    in_specs = [a_spec, b_spec]
    args = [a, b]
    if has_norm:
        in_specs.append(pl.BlockSpec((1, k), lambda i, j: (0, 0)))
        args.append(norm_g)
    for arr, kind, col in extras:
        if kind == 'mn':
            in_specs.append(spec((tm, tn), lambda i, j, o=col // tn: (i, j + o)))
        else:
            in_specs.append(spec((1, tn), lambda i, j, o=col // tn: (0, j + o)))
        args.append(arr)
    out_shape = [jax.ShapeDtypeStruct((m, n), out_dtype)]
    out_specs = [spec((tm, tn), lambda i, j: (i, j))]
    if emit_norm:
        out_shape.append(jax.ShapeDtypeStruct((m, k), BF16))
        out_specs.append(pl.BlockSpec((tm, k), lambda i, j: (i, 0)))
    if col_sums:
        out_shape.append(jax.ShapeDtypeStruct((1, n), F32))
        out_specs.append(spec((1, tn), lambda i, j: (0, j)))
    scratch = [pltpu.VMEM((tm, k), BF16)] if has_norm else []
    grid = (n // tn, m // tm) if col_sums else (m // tm, n // tn)
    res = pl.pallas_call(
        body, name=name, grid=grid, in_specs=in_specs, out_specs=out_specs, out_shape=out_shape,
        scratch_shapes=scratch, compiler_params=_params('arbitrary', 'arbitrary'))(*args)
    return res if emit_norm or col_sums else res[0]


def _gated_merge(outs, ups, zg, bias, *, name, tm=1024, tn=512):
    s, n, nq = zg.shape[0], ups[0].shape[1], len(outs)
    tm, tn = min(tm, s), min(tn, n)
    per = n // tn

    def body(*refs):
        y = None
        for q in range(nq):
            o_ref, w_ref, z_ref, b_ref = refs[q], refs[nq + q], refs[2 * nq + q], refs[3 * nq + q]
            term = _sig(z_ref[...].astype(F32) + b_ref[...]) * _dot(o_ref[...], w_ref[...])
            y = term if y is None else y + term
        refs[4 * nq][...] = y.astype(BF16)

    in_specs = [pl.BlockSpec((tm, o.shape[1]), lambda i, j: (i, 0)) for o in outs]
    in_specs += [pl.BlockSpec((u.shape[0], tn), lambda i, j: (0, j)) for u in ups]
    in_specs += [pl.BlockSpec((tm, tn), lambda i, j, q=q: (i, j + q * per)) for q in range(nq)]
    in_specs += [pl.BlockSpec((1, tn), lambda i, j, q=q: (0, j + q * per)) for q in range(nq)]
    return pl.pallas_call(body, name=name, grid=(s // tm, per), in_specs=in_specs,
                          out_specs=pl.BlockSpec((tm, tn), lambda i, j: (i, j)),
                          out_shape=jax.ShapeDtypeStruct((s, n), BF16),
                          compiler_params=_params('arbitrary', 'arbitrary'))(*outs, *ups, *[zg] * nq, *[bias] * nq)


def _mn(col_off=0):
    return 'mn', col_off


def _nvec(col_off=0):
    return 'n', col_off


def _rowwise(fn, rows, consts, outs, sums=(), *, name, ts=256):
    views = [x if isinstance(x, Cols) else Cols(x, x.shape[1], 0) for x in rows]
    rows = [v.arr for v in views]
    r = rows[0].shape[0]
    ts = min(ts, r)
    assert r % ts == 0, (name, r, ts)
    nr, nc, no, ns = len(rows), len(consts), len(outs), len(sums)

    def body(*refs):
        vals = fn(*[x[...] for x in refs[:nr + nc]])
        for q in range(no):
            refs[nr + nc + q][...] = vals[q].astype(outs[q][1])
        if ns:
            @pl.when(pl.program_id(0) == 0)
            def _():
                for q in range(ns):
                    refs[nr + nc + no + q][...] = jnp.zeros((1, sums[q]), F32)

            for q in range(ns):
                refs[nr + nc + no + q][...] += jnp.sum(vals[no + q].astype(F32), axis=0, keepdims=True)

    in_specs = [pl.BlockSpec((ts, v.width), lambda i, blk=v.blk: (i, blk)) for v in views]
    in_specs += [pl.BlockSpec(x.shape, lambda i, nd=x.ndim: (0,) * nd) for x in consts]
    out_specs = [pl.BlockSpec((ts, w), lambda i: (i, 0)) for w, _ in outs]
    out_specs += [pl.BlockSpec((1, w), lambda i: (0, 0)) for w in sums]
    out_shape = [jax.ShapeDtypeStruct((r, w), dt) for w, dt in outs]
    out_shape += [jax.ShapeDtypeStruct((1, w), F32) for w in sums]
    return pl.pallas_call(body, name=name, grid=(r // ts,), in_specs=in_specs, out_specs=out_specs,
                          out_shape=out_shape, compiler_params=_params('arbitrary'))(*rows, *consts)


def _cumsum_rows(x, *, reverse, name, bs=256):
    s, w = x.shape
    bs = min(bs, s)
    nb = s // bs

    def body(x_ref, o_ref, carry):
        @pl.when(pl.program_id(0) == 0)
        def _():
            carry[...] = jnp.zeros_like(carry)

        r = lax.broadcasted_iota(jnp.int32, (bs, bs), 0)
        c = lax.broadcasted_iota(jnp.int32, (bs, bs), 1)
        tri = jnp.where((c >= r) if reverse else (c <= r), 1.0, 0.0).astype(BF16)
        xv = x_ref[...]
        o_ref[...] = _tri_dot(tri, xv) + carry[...]
        carry[...] += jnp.sum(xv, axis=0, keepdims=True)

    imap = (lambda i: (nb - 1 - i, 0)) if reverse else (lambda i: (i, 0))
    return pl.pallas_call(body, name=name, grid=(nb,), in_specs=[pl.BlockSpec((bs, w), imap)],
                          out_specs=pl.BlockSpec((bs, w), imap), out_shape=jax.ShapeDtypeStruct((s, w), F32),
                          scratch_shapes=[pltpu.VMEM((1, w), F32)], compiler_params=_params('arbitrary'))(x)


def _mask(mode, q0, k0, bq, bk):
    qpos = q0 + lax.broadcasted_iota(jnp.int32, (bq, bk), 0)
    kpos = k0 + lax.broadcasted_iota(jnp.int32, (bq, bk), 1)
    if mode == 'causal':
        return kpos <= qpos
    return kpos < (jnp.right_shift(qpos, int(math.log2(CHUNK))) + 1) * CHUNK


ROPE_SHIFT = int(math.log2(MLA_ROPE))
FOX_SCALE, MLA_SCALE, XA_SCALE = FOX_HD ** -0.5, (MLA_NOPE + MLA_ROPE) ** -0.5, XA_HD ** -0.5
ATTN_ROW_SLAB = 512


def _lane_masks(g, b, rope):
    lane = lax.broadcasted_iota(jnp.int32, (1, LANES), 1)
    heads = [None if g == 1 else (lane >= hh * (LANES // g)) & (lane < (hh + 1) * (LANES // g)) for hh in range(g)]
    ropes = [jnp.right_shift(lane, ROPE_SHIFT) == b * g + hh for hh in range(g)] if rope else [None] * g
    return heads, ropes


def _sel(mask, x):
    return x if mask is None else jnp.where(mask, x, jnp.zeros_like(x))


class Step(typing.NamedTuple):
    qi: typing.Any
    kj: typing.Any
    first: typing.Any
    last: typing.Any
    plain: typing.Any
    masked: typing.Any


def _fwd_steps(tri, nq, nk):
    if not tri:
        return (nq, nk), lambda i, j: Step(i, j, j == 0, j == nk - 1, True, False)
    if nq % 2:
        return (nq, nk), lambda i, j: Step(i, jnp.minimum(i, j), j == 0, j == nk - 1, j < i, j == i)

    def at(i, t):
        low = t <= i
        diag = (t == i) | (t == nq)
        return Step(jnp.where(low, i, nq - 1 - i), jnp.where(low, t, t - (i + 1)), (t == 0) | (t == i + 1), diag,
                    jnp.logical_not(diag), diag)

    return (nq // 2, nq + 1), at


def _bwd_steps(tri, nq, nk):
    if not tri:
        return (nk, nq), lambda j, i: Step(i, j, i == 0, i == nq - 1, True, False)
    if nk % 2:
        return (nk, nq), lambda j, i: Step(jnp.maximum(i, j), j, i == 0, i == nq - 1, i > j, i == j)

    def at(j, t):
        n1 = nq - j
        low = t < n1
        diag = (t == 0) | (t == n1)
        return Step(jnp.where(low, j + t, nk - 1 - j + t - n1), jnp.where(low, j, nk - 1 - j), diag,
                    (t == n1 - 1) | (t == nq), jnp.logical_not(diag), diag)

    return (nk // 2, nq + 1), at


def _carried(comm, refs, n_in, n_out):
    ci, co = len(comm.ins), len(comm.out_shapes)
    ins = refs[n_in:n_in + ci]
    outs = refs[n_in + ci + n_out:n_in + ci + n_out + co]
    rest = refs[:n_in] + refs[n_in + ci:n_in + ci + n_out] + refs[n_in + ci + n_out + co:-2]
    return rest, (ins, outs, refs[-2], refs[-1])


def _mattn_fwd(q, k, v, *, qc, kc, vc, nb, g, mode, name, dq_scale=1.0, ck=None, qr=None, qrc=0, kr=None, blk=512,
               comm=None):
    s, t = q.shape[0], k.shape[0]
    bq, bk = min(blk, s), min(blk, t)
    nq, nk = s // bq, t // bk
    tri = mode != 'full'
    bias, rope = ck is not None, qr is not None
    assert not tri or (bq == bk and bq % CHUNK == 0)
    rs = min(ATTN_ROW_SLAB, bq)
    n_in = 3 + bias + 2 * rope
    (n1, n2), step_at = _fwd_steps(tri, nq, nk)

    def body(*refs):
        refs = list(refs)
        b, p1, p2 = pl.program_id(0), pl.program_id(1), pl.program_id(2)
        st = step_at(p1, p2)
        i, j = st.qi, st.kj
        if comm is not None:
            refs, comm_refs = _carried(comm, refs, n_in, 2)
            pl.when((b == 0) & (p1 == 0) & (p2 == 0))(lambda: comm.start(*comm_refs))
        q_ref, k_ref, v_ref = refs[:3]
        pos = 3
        ck_ref = qr_ref = kr_ref = None
        if bias:
            ck_ref = refs[pos]
            pos += 1
        if rope:
            qr_ref, kr_ref = refs[pos:pos + 2]
            pos += 2
        o_ref, lse_ref, m_s, l_s, acc_s = refs[pos:]
        heads, ropes = _lane_masks(g, b, rope)

        @pl.when(st.first)
        def _():
            m_s[...] = jnp.full_like(m_s, MASK_VALUE)
            l_s[...] = jnp.zeros_like(l_s)
            acc_s[...] = jnp.zeros_like(acc_s)

        def compute(masked):
            k2, v2 = k_ref[...], v_ref[...]
            for r in range(bq // rs):
                rows = pl.ds(r * rs, rs)
                q2 = q_ref[rows, :]
                alphas, pvs = [], []
                for hh in range(g):
                    sc = _dot(_sel(heads[hh], q2), k2, NT)
                    if rope:
                        sc = sc + _dot(_sel(ropes[hh], qr_ref[rows, :]), kr_ref[...], NT)
                    if bias:
                        sc = sc - ck_ref[0, hh:hh + 1, :]
                    if masked:
                        sc = jnp.where(_mask(mode, i * bq + r * rs, j * bk, rs, bk), sc, MASK_VALUE)
                    m_prev = m_s[hh, rows]
                    m_new = jnp.maximum(m_prev, jnp.max(sc, axis=1, keepdims=True))
                    alpha = jnp.exp(m_prev - m_new)
                    p = jnp.exp(sc - m_new)
                    l_s[hh, rows] = alpha * l_s[hh, rows] + jnp.sum(p, axis=1, keepdims=True)
                    m_s[hh, rows] = m_new
                    alphas.append(alpha)
                    pvs.append(_dot(p.astype(BF16), _sel(heads[hh], v2)))
                alpha = alphas[0]
                for hh in range(1, g):
                    alpha = jnp.where(heads[hh], alphas[hh], alpha)
                acc_s[rows, :] = acc_s[rows, :] * alpha + sum(pvs[1:], pvs[0])

        if tri:
            pl.when(st.plain)(functools.partial(compute, False))
            pl.when(st.masked)(functools.partial(compute, True))
        else:
            compute(False)

        @pl.when(st.last)
        def _():
            lane = lax.broadcasted_iota(jnp.int32, (bq, LANES), 1)
            l_full, lse = l_s[0], jnp.zeros((bq, LANES), F32)
            for hh in range(g):
                if hh:
                    l_full = jnp.where(heads[hh], l_s[hh], l_full)
                lse = jnp.where(lane == hh, m_s[hh] + jnp.log(l_s[hh]), lse)
            o_ref[...] = (acc_s[...] / l_full).astype(o_ref.dtype)
            lse_ref[...] = lse

        if comm is not None:
            pl.when((b == nb - 1) & (p1 == n1 - 1) & (p2 == n2 - 1))(lambda: comm.finish(*comm_refs))

    qi = lambda p1, p2: step_at(p1, p2).qi
    kj = lambda p1, p2: step_at(p1, p2).kj
    in_specs = [pl.BlockSpec((bq, LANES), lambda b, p1, p2: (qi(p1, p2), qc + b)),
                pl.BlockSpec((bk, LANES), lambda b, p1, p2: (kj(p1, p2), kc + b)),
                pl.BlockSpec((bk, LANES), lambda b, p1, p2: (kj(p1, p2), vc + b))]
    args = [q, k, v]
    if bias:
        in_specs.append(pl.BlockSpec((1, 8, bk), lambda b, p1, p2: (b, 0, kj(p1, p2))))
        args.append(ck)
    if rope:
        in_specs += [pl.BlockSpec((bq, LANES), lambda b, p1, p2: (qi(p1, p2), qrc)),
                     pl.BlockSpec((bk, LANES), lambda b, p1, p2: (kj(p1, p2), 0))]
        args += [qr, kr]
    out = pl.BlockSpec((bq, LANES), lambda b, p1, p2: (qi(p1, p2), b))
    out_specs = [out, out]
    out_shape = [jax.ShapeDtypeStruct((s, LANES * nb), BF16), jax.ShapeDtypeStruct((s, LANES * nb), F32)]
    scratch = [pltpu.VMEM((g, bq, 1), F32), pltpu.VMEM((g, bq, 1), F32), pltpu.VMEM((bq, LANES), F32)]
    if comm is not None:
        in_specs += [ANY] * len(comm.ins)
        args += comm.ins
        out_specs += [ANY] * len(comm.out_shapes)
        out_shape += comm.out_shapes
        scratch += _sems(comm.n_sems, comm.n_sems)
    res = pl.pallas_call(body, name=name, grid=(nb, n1, n2), in_specs=in_specs, out_specs=out_specs, out_shape=out_shape,
                         scratch_shapes=scratch, compiler_params=_params('arbitrary', 'arbitrary', 'arbitrary'))(*args)
    return res if comm is None else (res[0], res[1], res[2:])


def _mattn_bwd(q, k, v, o, do, lse, *, qc, kc, vc, nb, g, mode, name, dq_scale=1.0, ck=None, qr=None, qrc=0, kr=None,
               blk=512, comm=None):
    s, t = q.shape[0], k.shape[0]
    bq, bk = min(blk, s), min(blk, t)
    nq, nk = s // bq, t // bk
    tri = mode != 'full'
    bias, rope = ck is not None, qr is not None
    rs = min(ATTN_ROW_SLAB, bq)
    n_in, n_out = 6 + bias + 2 * rope, 3 + 2 * bias + 2 * rope
    (n1, n2), step_at = _bwd_steps(tri, nq, nk)

    def body(*refs):
        refs = list(refs)
        if comm is not None:
            refs, comm_refs = _carried(comm, refs, n_in, n_out)
            first = (pl.program_id(0) == 0) & (pl.program_id(1) == 0) & (pl.program_id(2) == 0)
            pl.when(first)(lambda: comm.start(*comm_refs))
        q_ref, k_ref, v_ref, o_ref, do_ref, lse_ref = refs[:6]
        pos = 6
        ck_ref = qr_ref = kr_ref = dck_ref = dcq_ref = dqr_ref = dkr_ref = dck_s = None
        if bias:
            ck_ref = refs[pos]
            pos += 1
        if rope:
            qr_ref, kr_ref = refs[pos:pos + 2]
            pos += 2
        dq_ref, dk_ref, dv_ref = refs[pos:pos + 3]
        pos += 3
        if bias:
            dck_ref, dcq_ref = refs[pos:pos + 2]
            pos += 2
        if rope:
            dqr_ref, dkr_ref = refs[pos:pos + 2]
            pos += 2
        dk_s, dv_s = refs[pos:pos + 2]
        if bias:
            dck_s = refs[pos + 2]
        b, p1, p2 = pl.program_id(0), pl.program_id(1), pl.program_id(2)
        st = step_at(p1, p2)
        i, j = st.qi, st.kj
        heads, ropes = _lane_masks(g, b, rope)

        @pl.when((p1 == 0) & (p2 == 0))
        def _():
            dq_ref[...] = jnp.zeros_like(dq_ref)
            if bias:
                dcq_ref[...] = jnp.zeros_like(dcq_ref)

        if rope:
            @pl.when((b == 0) & (p1 == 0) & (p2 == 0))
            def _():
                dqr_ref[...] = jnp.zeros_like(dqr_ref)
                dkr_ref[...] = jnp.zeros_like(dkr_ref)

        @pl.when(st.first)
        def _():
            dk_s[...] = jnp.zeros_like(dk_s)
            dv_s[...] = jnp.zeros_like(dv_s)
            if bias:
                dck_s[...] = jnp.zeros_like(dck_s)

        def compute(masked):
            k2, v2 = k_ref[...], v_ref[...]
            lane = lax.broadcasted_iota(jnp.int32, (rs, LANES), 1)
            rk = pl.ds(pl.multiple_of(j * bk, bk), bk)
            add = lambda tot, x: x if tot is None else tot + x
            dv_t = dk_t = dkr_t = None
            dck_t = [None] * g
            for r in range(bq // rs):
                rows = pl.ds(r * rs, rs)
                rq = pl.ds(pl.multiple_of(i * bq + r * rs, rs), rs)
                q2, do2, lse2 = q_ref[rows, :], do_ref[rows, :], lse_ref[rows, :]
                dd = do2.astype(F32) * o_ref[rows, :].astype(F32)
                dq_t = dqr_t = dcq_t = None
                for hh in range(g):
                    qm = _sel(heads[hh], q2)
                    sc = _dot(qm, k2, NT)
                    if rope:
                        qrm = _sel(ropes[hh], qr_ref[rows, :])
                        sc = sc + _dot(qrm, kr_ref[...], NT)
                    if bias:
                        sc = sc - ck_ref[0, hh:hh + 1, :]
                    if masked:
                        sc = jnp.where(_mask(mode, i * bq + r * rs, j * bk, rs, bk), sc, MASK_VALUE)
                    p = jnp.exp(sc - jnp.sum(jnp.where(lane == hh, lse2, 0.0), axis=1, keepdims=True))
                    dom = _sel(heads[hh], do2)
                    dp = _dot(dom, v2, NT)
                    delta = jnp.sum(_sel(heads[hh], dd), axis=1, keepdims=True)
                    ds = p * (dp - delta)
                    dsb = ds.astype(BF16)
                    dv_t = add(dv_t, _dot(p.astype(BF16), dom, TN))
                    dk_t = add(dk_t, _dot(dsb, qm, TN))
                    dq_t = add(dq_t, _dot(dsb, _sel(heads[hh], k2)))
                    if rope:
                        dqr_t = add(dqr_t, _dot(dsb, _sel(ropes[hh], kr_ref[...])))
                        dkr_t = add(dkr_t, _dot(dsb, qrm, TN))
                    if bias:
                        dck_t[hh] = add(dck_t[hh], jnp.sum(ds, axis=0, keepdims=True))
                        dcq_t = add(dcq_t, jnp.where(lane == hh, jnp.sum(ds, axis=1, keepdims=True), 0.0))
                dq_ref[rq, :] += dq_t if dq_scale == 1.0 else dq_scale * dq_t
                if rope:
                    dqr_ref[rq, :] += dq_scale * dqr_t
                if bias:
                    dcq_ref[rq, :] += dcq_t
            dv_s[...] += dv_t
            dk_s[...] += dk_t
            if rope:
                dkr_ref[rk, :] += dkr_t
            if bias:
                for hh in range(g):
                    dck_s[hh:hh + 1, :] -= dck_t[hh]

        if tri:
            pl.when(st.plain)(functools.partial(compute, False))
            pl.when(st.masked)(functools.partial(compute, True))
        else:
            compute(False)

        @pl.when(st.last)
        def _():
            dk_ref[...] = dk_s[...]
            dv_ref[...] = dv_s[...]
            if bias:
                dck_ref[0] = dck_s[...]

        if comm is not None:
            pl.when((b == nb - 1) & (p1 == n1 - 1) & (p2 == n2 - 1))(lambda: comm.finish(*comm_refs))

    qrow = lambda col: pl.BlockSpec((bq, LANES), lambda b, p1, p2: (step_at(p1, p2).qi, col(b)))
    krow = lambda col: pl.BlockSpec((bk, LANES), lambda b, p1, p2: (step_at(p1, p2).kj, col(b)))
    in_specs = [qrow(lambda b: qc + b), krow(lambda b: kc + b), krow(lambda b: vc + b), qrow(lambda b: b),
                qrow(lambda b: b), qrow(lambda b: b)]
    args = [q, k, v, o, do, lse]
    whole = lambda rows: pl.BlockSpec((rows, LANES), lambda b, j, i: (0, b))
    out_specs = [whole(s), krow(lambda b: b), krow(lambda b: b)]
    out_shape = [jax.ShapeDtypeStruct((s, LANES * nb), F32), jax.ShapeDtypeStruct((t, LANES * nb), F32),
                 jax.ShapeDtypeStruct((t, LANES * nb), F32)]
    scratch = [pltpu.VMEM((bk, LANES), F32), pltpu.VMEM((bk, LANES), F32)]
    if bias:
        ckj = pl.BlockSpec((1, 8, bk), lambda b, p1, p2: (b, 0, step_at(p1, p2).kj))
        in_specs.append(ckj)
        args.append(ck)
        out_specs += [ckj, whole(s)]
        out_shape += [jax.ShapeDtypeStruct((nb, 8, t), F32), jax.ShapeDtypeStruct((s, LANES * nb), F32)]
    if rope:
        in_specs += [qrow(lambda b: qrc), krow(lambda b: 0)]
        args += [qr, kr]
        out_specs += [pl.BlockSpec((s, LANES), lambda b, j, i: (0, 0)), pl.BlockSpec((t, LANES), lambda b, j, i: (0, 0))]
        out_shape += [jax.ShapeDtypeStruct((s, LANES), F32), jax.ShapeDtypeStruct((t, LANES), F32)]
    if bias:
        scratch.append(pltpu.VMEM((8, bk), F32))
    if comm is not None:
        in_specs += [ANY] * len(comm.ins)
        args += comm.ins
        out_specs += [ANY] * len(comm.out_shapes)
        out_shape += comm.out_shapes
        scratch += _sems(comm.n_sems, comm.n_sems)
    res = pl.pallas_call(body, name=name, grid=(nb, n1, n2), in_specs=in_specs, out_specs=out_specs,
                         out_shape=out_shape, scratch_shapes=scratch,
                         compiler_params=_params('arbitrary', 'arbitrary', 'arbitrary'))(*args)
    return res if comm is None else (*res[:n_out], res[n_out:])


def _gla_chunk(la_c, k_c):
    r = lax.broadcasted_iota(jnp.int32, (CHUNK, CHUNK), 0)
    c = lax.broadcasted_iota(jnp.int32, (CHUNK, CHUNK), 1)
    tri = jnp.where(c <= r, 1.0, 0.0).astype(BF16)
    cum = _tri_dot(tri, la_c)
    end = jnp.sum(la_c, axis=0, keepdims=True)
    dec = jnp.exp(end - cum)
    return dec, k_c * dec, jnp.exp(end)


GLA_PAIRS = GLA_HEADS // 2


def _gla_fwd(z, la, *, qc, kc, vc, name, blk=512):
    s = z.shape[0]
    bs = min(blk, s)
    ncb = bs // CHUNK
    nblk = s // bs

    def body(q_ref, k_ref, va_ref, vb_ref, la_ref, o_ref, st_ref, st):
        @pl.when(pl.program_id(1) == 0)
        def _():
            st[...] = jnp.zeros_like(st)

        heads, _ = _lane_masks(2, 0, False)
        v_refs = (va_ref, vb_ref)
        for c in range(ncb):
            sl = pl.ds(c * CHUNK, CHUNK)
            _, kf, a = _gla_chunk(la_ref[sl, :], k_ref[sl, :])
            qs = q_ref[sl, :] * (GLA_DK ** -0.5)
            for hh in range(2):
                ut = _dot(v_refs[hh][sl, :].astype(BF16), _sel(heads[hh], kf).astype(BF16), TN)
                new = a * st[hh] + ut
                st[hh] = new
                st_ref[0, c, hh] = new
                o_ref[sl, hh * GLA_DV:(hh + 1) * GLA_DV] = _dot(_sel(heads[hh], qs).astype(BF16), new.astype(BF16), NT)

    col = lambda c0, m=1: pl.BlockSpec((bs, LANES), lambda b, i: (i, c0 + m * b))
    return pl.pallas_call(
        body, name=name, grid=(GLA_PAIRS, nblk),
        in_specs=[col(qc), col(kc), col(vc, 2), col(vc + 1, 2), col(0)],
        out_specs=[pl.BlockSpec((bs, 2 * GLA_DV), lambda b, i: (i, b)),
                   pl.BlockSpec((1, ncb, 2, GLA_DV, LANES), lambda b, i: (b, i, 0, 0, 0))],
        out_shape=[jax.ShapeDtypeStruct((s, GLA_HEADS * GLA_DV), F32),
                   jax.ShapeDtypeStruct((GLA_PAIRS, s // CHUNK, 2, GLA_DV, LANES), F32)],
        scratch_shapes=[pltpu.VMEM((2, GLA_DV, LANES), F32)],
        compiler_params=_params('arbitrary', 'arbitrary'))(z, z, z, z, la)


def _gla_bwd(z, la, st_all, st_prev, do, *, qc, kc, vc, name, blk=512):
    s = z.shape[0]
    bs = min(blk, s)
    ncb = bs // CHUNK
    nblk = s // bs

    def body(q_ref, k_ref, va_ref, vb_ref, la_ref, st_ref, sp_ref, do_ref, dq_ref, dk_ref, dv_ref, dla_ref, ga):
        @pl.when(pl.program_id(1) == 0)
        def _():
            ga[...] = jnp.zeros_like(ga)

        r = lax.broadcasted_iota(jnp.int32, (CHUNK, CHUNK), 0)
        cc = lax.broadcasted_iota(jnp.int32, (CHUNK, CHUNK), 1)
        tri_rev = jnp.where(cc >= r, 1.0, 0.0).astype(BF16)
        heads, _ = _lane_masks(2, 0, False)
        v_refs = (va_ref, vb_ref)
        for c in reversed(range(ncb)):
            sl = pl.ds(c * CHUNK, CHUNK)
            dec, kf, a = _gla_chunk(la_ref[sl, :], k_ref[sl, :])
            qs = q_ref[sl, :] * (GLA_DK ** -0.5)
            dq2 = jnp.zeros((CHUNK, LANES), F32)
            dkd = jnp.zeros((CHUNK, LANES), F32)
            da = jnp.zeros((1, LANES), F32)
            for hh in range(2):
                hv = slice(hh * GLA_DV, (hh + 1) * GLA_DV)
                dob = do_ref[sl, hv].astype(BF16)
                g = _dot(dob, _sel(heads[hh], qs).astype(BF16), TN) + ga[hh]
                gb = g.astype(BF16)
                dq2 = dq2 + _dot(dob, st_ref[0, c, hh].astype(BF16))
                dv_ref[sl, hv] = _dot(_sel(heads[hh], kf).astype(BF16), gb, NT)
                dkd = dkd + _dot(v_refs[hh][sl, :].astype(BF16), gb)
                da = da + jnp.sum(g * sp_ref[0, c, hh], axis=0, keepdims=True)
                ga[hh] = a * g
            dq_ref[sl, :] = (GLA_DK ** -0.5) * dq2
            dk_ref[sl, :] = dkd * dec
            e = dkd * kf
            dend = jnp.sum(e, axis=0, keepdims=True) + da * a
            dla_ref[sl, :] = dend - _tri_dot(tri_rev, e)

    rev = lambda i: nblk - 1 - i
    col = lambda c0, m=1: pl.BlockSpec((bs, LANES), lambda b, i: (rev(i), c0 + m * b))
    wide = pl.BlockSpec((bs, 2 * GLA_DV), lambda b, i: (rev(i), b))
    stspec = pl.BlockSpec((1, ncb, 2, GLA_DV, LANES), lambda b, i: (b, rev(i), 0, 0, 0))
    return pl.pallas_call(
        body, name=name, grid=(GLA_PAIRS, nblk),
        in_specs=[col(qc), col(kc), col(vc, 2), col(vc + 1, 2), col(0), stspec, stspec, wide],
        out_specs=[col(0), col(0), wide, col(0)],
        out_shape=[jax.ShapeDtypeStruct((s, GLA_HEADS * GLA_DK), F32), jax.ShapeDtypeStruct((s, GLA_HEADS * GLA_DK), F32),
                   jax.ShapeDtypeStruct((s, GLA_HEADS * GLA_DV), F32), jax.ShapeDtypeStruct((s, GLA_HEADS * GLA_DK), F32)],
        scratch_shapes=[pltpu.VMEM((2, GLA_DV, LANES), F32)],
        compiler_params=_params('arbitrary', 'arbitrary'))(z, z, z, z, la, st_all, st_prev, do)


def _place():
    return lax.axis_index('x'), lax.axis_index('y'), lax.axis_index('c')


ANY = pl.BlockSpec(memory_space=pl.ANY)


def _all_gather8(blk, *, name):
    m, n = blk.shape

    def body(x_ref, out_ref, send_sems, recv_sems, local_sem):
        x, y, c = _place()
        me, sibling = (x, y, c), (x, y, 1 - c)
        chips = [(1 - x, y), (x, 1 - y), (1 - x, 1 - y)]

        def slot(px, py, pc):
            return out_ref.at[4 * px + 2 * py + pc]

        def copy(q, block, to, src=None):
            return pltpu.make_async_remote_copy(
                src_ref=slot(*block) if src is None else src, dst_ref=slot(*block), send_sem=send_sems.at[q],
                recv_sem=recv_sems.at[q], device_id=to, device_id_type=MESH)

        mine = pltpu.make_async_copy(x_ref, slot(*me), local_sem)
        mine.start()
        first = [copy(0, me, sibling, src=x_ref)]
        first += [copy(1 + q, me, (*chip, c), src=x_ref) for q, chip in enumerate(chips)]
        for cp in first:
            cp.start()
        passed = [copy(4 + q, (*chip, c), sibling) for q, chip in enumerate(chips)]
        for q, chip in enumerate(chips):
            copy(1 + q, (*chip, c), me).wait_recv()
            passed[q].start()
        copy(0, sibling, me).wait_recv()
        for q, chip in enumerate(chips):
            copy(4 + q, (*chip, 1 - c), me).wait_recv()
        for cp in first + passed:
            cp.wait_send()
        mine.wait()

    return pl.pallas_call(
        body, name=name, in_specs=[ANY], out_specs=ANY, out_shape=jax.ShapeDtypeStruct((N_DEV, m, n), blk.dtype),
        scratch_shapes=[pltpu.SemaphoreType.DMA((7,)), pltpu.SemaphoreType.DMA((7,)), pltpu.SemaphoreType.DMA(())],
    )(blk)


def _sems(*counts):
    return [pltpu.SemaphoreType.DMA((n,)) for n in counts]


class Comm(typing.NamedTuple):
    ins: list
    out_shapes: list
    n_sems: int
    start: typing.Callable
    finish: typing.Callable


def _remote(src, dst, send_sems, recv_sems, idx, to):
    return lambda: pltpu.make_async_remote_copy(src_ref=src, dst_ref=dst, send_sem=send_sems.at[idx],
                                                recv_sem=recv_sems.at[idx], device_id=to, device_id_type=MESH)


def _comm_from(copies, ins, out_shapes, n_sems):
    def start(*refs):
        for cp in copies(*refs)[0]:
            cp().start()

    def finish(*refs):
        sent, received = copies(*refs)
        for cp in received:
            cp().wait_recv()
        for cp in sent:
            cp().wait_send()

    return Comm(list(ins), list(out_shapes), n_sems, start, finish)


def _run_comm(comm, *, name, alias=False):
    n_in, n_out = len(comm.ins), len(comm.out_shapes)

    def body(*refs):
        ins, outs, sems = refs[:n_in], refs[n_in:n_in + n_out], refs[n_in + n_out:]
        comm.start(ins, outs, *sems)
        comm.finish(ins, outs, *sems)

    return pl.pallas_call(body, name=name, in_specs=[ANY] * n_in, out_specs=[ANY] * n_out, out_shape=comm.out_shapes,
                          input_output_aliases={q: q for q in range(n_in)} if alias else {},
                          scratch_shapes=_sems(comm.n_sems, comm.n_sems))(*comm.ins)


def _half(rows, c):
    h = rows // 2
    return pl.ds(pl.multiple_of(c * h, h), h)


def _gathered(ref, chip, rows, side):
    if not side:
        return ref.at[chip, rows]
    n = ref.shape[1] // N_CHIPS
    return ref.at[rows, pl.ds(pl.multiple_of(chip * n, n), n)]


def _gather_over_ici(ws, side):
    def copies(ins, outs, send_sems, recv_sems):
        x, y, c = _place()
        me_chip = 2 * x + y
        sent, received = [], []
        for q, w in enumerate(ws):
            half, every = _half(w.shape[0], c), pl.ds(0, w.shape[0])
            for k, (px, py) in enumerate([(1 - x, y), (x, 1 - y), (1 - x, 1 - y)]):
                sent.append(_remote(ins[q].at[half], _gathered(outs[q], me_chip, half, side[q]), send_sems, recv_sems,
                                    4 * q + k, (px, py, c)))
                slot = _gathered(outs[q], 2 * px + py, half, side[q])
                received.append(_remote(slot, slot, send_sems, recv_sems, 4 * q + k, (px, py, c)))
            whole = _remote(ins[q], _gathered(outs[q], me_chip, every, side[q]), send_sems, recv_sems, 4 * q + 3,
                            (x, y, 1 - c))
            sent.append(whole)
            received.append(whole)
        return sent, received

    shapes = [jax.ShapeDtypeStruct((w.shape[0], N_CHIPS * w.shape[1]) if sd else (N_CHIPS,) + w.shape, w.dtype)
              for w, sd in zip(ws, side)]
    return _comm_from(copies, ws, shapes, 4 * len(ws))


def _gather_over_d2d(parts, side):
    def copies(ins, outs, send_sems, recv_sems):
        x, y, c = _place()
        sent, received = [], []
        for q, w in enumerate(parts):
            rows = w.shape[0] if side[q] else w.shape[1]
            for k, (px, py) in enumerate([(1 - x, y), (x, 1 - y), (1 - x, 1 - y)]):
                mine = _gathered(outs[q], 2 * px + py, _half(rows, c), side[q])
                theirs = _gathered(outs[q], 2 * px + py, _half(rows, 1 - c), side[q])
                sent.append(_remote(mine, mine, send_sems, recv_sems, 3 * q + k, (x, y, 1 - c)))
                received.append(_remote(theirs, theirs, send_sems, recv_sems, 3 * q + k, (x, y, 1 - c)))
        return sent, received

    return _comm_from(copies, parts, [jax.ShapeDtypeStruct(w.shape, w.dtype) for w in parts], 3 * len(parts))


def _to_sibling(gs, *, name):
    n = len(gs)

    def body(*refs):
        ins, outs = refs[:n], refs[n:2 * n]
        send_sems, recv_sems = refs[2 * n:]
        x, y, c = _place()
        cps = [pltpu.make_async_remote_copy(
            src_ref=ins[q], dst_ref=outs[q], send_sem=send_sems.at[q], recv_sem=recv_sems.at[q],
            device_id=(x, y, 1 - c), device_id_type=MESH) for q in range(n)]
        for cp in cps:
            cp.start()
        for cp in cps:
            cp.wait()

    return pl.pallas_call(body, name=name, in_specs=[ANY] * n, out_specs=[ANY] * n,
                          out_shape=[jax.ShapeDtypeStruct(g.shape, g.dtype) for g in gs],
                          scratch_shapes=_sems(n, n))(*gs)


def _chip_exchange(ps):
    def copies(ins, outs, send_sems, recv_sems):
        x, y, c = _place()
        cps = [_remote(ins[q].at[2 * px + py], outs[q].at[k], send_sems, recv_sems, 3 * q + k, (px, py, c))
               for q in range(len(ps)) for k, (px, py) in enumerate([(1 - x, y), (x, 1 - y), (1 - x, 1 - y)])]
        return cps, cps

    return _comm_from(copies, ps, [jax.ShapeDtypeStruct((3,) + p.shape[1:], p.dtype) for p in ps], 3 * len(ps))


def _sum_chips(own, r, *, name, ts=256):
    k, n = own.shape
    ts = min(ts, k)

    def body(own_ref, r_ref, o_ref):
        f = lambda q: r_ref[q].astype(F32)
        o_ref[...] = ((own_ref[...].astype(F32) + f(0)) + f(1)) + f(2)

    return pl.pallas_call(
        body, name=name, grid=(k // ts,),
        in_specs=[pl.BlockSpec((ts, n), lambda i: (i, 0)), pl.BlockSpec((3, ts, n), lambda i: (0, i, 0))],
        out_specs=pl.BlockSpec((ts, n), lambda i: (i, 0)), out_shape=jax.ShapeDtypeStruct((k, n), F32),
        compiler_params=_params('arbitrary'))(own, r)


WIN_SHARD = N_IN // N_CHIPS
WIN_PAD = -(-WIN_SHARD // LANES) * LANES
GATE_WIRE_ROWS = 32


def _full_layer(sh, axis):
    _, k, n = sh.shape
    if axis == 2:
        return sh.transpose(1, 0, 2).reshape(k, N_CHIPS * n)
    return sh.reshape(N_CHIPS * k, n)


def _win_cols(wp, o, n):
    parts = []
    while n > 0:
        j, r = divmod(o, WIN_SHARD)
        take = min(n, WIN_SHARD - r)
        parts.append(wp[:, j * WIN_PAD + r:j * WIN_PAD + r + take])
        o, n = o + take, n - take
    return parts[0] if len(parts) == 1 else jnp.concatenate(parts, axis=1)


def _split_full(full, axis):
    k, n = full.shape
    if axis == 2:
        return jnp.stack([full[:, j * (n // N_CHIPS):(j + 1) * (n // N_CHIPS)] for j in range(N_CHIPS)])
    return full.reshape(N_CHIPS, k // N_CHIPS, n)


def _padc(a, w):
    return jnp.pad(a, ((0, 0), (0, w - a.shape[1])))


def _swap16(a):
    return jnp.concatenate([a[..., 16:32], a[..., 0:16]], axis=-1)


B_GR, B_GQ, B_GK, B_GV, B_MQ, B_MKR, B_MKRS, B_FF, B_GLOW, B_MKV, B_END = (
    0, 512, 768, 1024, 1536, 1792, 1920, 2048, 2176, 2304, 2432)
B_W = 2560
O_FQ, O_FF, O_GQ, O_GLOW, O_GR, O_MQ, O_MKV, O_MKR, O_ZG = 0, 768, 772, 1796, 1812, 2324, 2580, 2708, 2740


def _repack_layer_weights(w):
    wi = functools.partial(_win_cols, w['w_in'])
    out = dict(w)
    out['in_a'] = jnp.concatenate([wi(O_FQ, 256) * FOX_SCALE, wi(O_FQ + 256, 512)], axis=1)
    kr = wi(O_MKR, 32)
    out['in_b'] = jnp.concatenate([
        wi(O_GR, 512), wi(O_GQ, 1024), wi(O_MQ, 256), jnp.tile(kr, (1, MLA_HEADS)), jnp.tile(_swap16(kr), (1, MLA_HEADS)),
        _padc(wi(O_FF, 4), 128), _padc(wi(O_GLOW, 16), 128), wi(O_MKV, 128),
        jnp.zeros((D_MODEL, B_W - B_END), kr.dtype)], axis=1)
    out['in_c'] = wi(O_ZG, 3072)
    uq = w['w_mla_uq'].reshape(MLA_Q_RANK, MLA_HEADS, MLA_NOPE + MLA_ROPE)
    rope = uq[:, :, MLA_NOPE:]
    out['uq'] = jnp.concatenate([uq[:, :, :MLA_NOPE].reshape(MLA_Q_RANK, -1), rope.reshape(MLA_Q_RANK, -1),
                                 _swap16(rope).reshape(MLA_Q_RANK, -1)], axis=1)
    ukv = w['w_mla_ukv'].reshape(MLA_KV_RANK, MLA_HEADS, MLA_NOPE + MLA_VD)
    out['ukv'] = jnp.concatenate([ukv[:, :, :MLA_NOPE].reshape(MLA_KV_RANK, -1),
                                  ukv[:, :, MLA_NOPE:].reshape(MLA_KV_RANK, -1)], axis=1)
    out['gate'] = jnp.pad(w['w_gla_gate'], ((0, 128 - GLA_RANK), (0, 0)))
    return out


def _unpack_layer_grads(g):
    a, b, c = g['in_a'], g['in_b'], g['in_c']
    a = jnp.concatenate([a[:, :256] * FOX_SCALE, a[:, 256:]], axis=1)
    fold = lambda o: sum(b[:, o + MLA_ROPE * q:o + MLA_ROPE * (q + 1)] for q in range(MLA_HEADS))
    kr = fold(B_MKR) + _swap16(fold(B_MKRS))
    w_in = jnp.concatenate([a, b[:, B_FF:B_FF + 4], b[:, B_GQ:B_GQ + 1024], b[:, B_GLOW:B_GLOW + 16],
                            b[:, B_GR:B_GR + 512], b[:, B_MQ:B_MQ + 256], b[:, B_MKV:B_MKV + 128], kr, c], axis=1)
    uq = g['uq']
    nope = uq[:, :256].reshape(MLA_Q_RANK, MLA_HEADS, MLA_NOPE)
    rope = (uq[:, 256:384].reshape(MLA_Q_RANK, MLA_HEADS, MLA_ROPE)
            + _swap16(uq[:, 384:512].reshape(MLA_Q_RANK, MLA_HEADS, MLA_ROPE)))
    w_uq = jnp.concatenate([nope, rope], axis=2).reshape(MLA_Q_RANK, -1)
    ukv = g['ukv']
    w_ukv = jnp.concatenate([ukv[:, :256].reshape(MLA_KV_RANK, MLA_HEADS, MLA_NOPE),
                             ukv[:, 256:].reshape(MLA_KV_RANK, MLA_HEADS, MLA_VD)], axis=2).reshape(MLA_KV_RANK, -1)
    out = {'w_in': w_in, 'w_mla_uq': w_uq, 'w_mla_ukv': w_ukv, 'w_gla_gate': g['gate'][:GLA_RANK]}
    for nm in ('w_up_fox', 'w_up_gla', 'w_up_mla', 'w_out', 'w_xq', 'w_xkv', 'w_xo', 'w_mlp1', 'w_mlp2'):
        out[nm] = g[nm]
    return out


def _rope_tables(s):
    half = MLA_ROPE // 2
    inv = ROPE_BASE ** (-jnp.arange(half, dtype=F32) / half)
    ang = jnp.arange(s).astype(F32)[:, None] * inv[None, :]
    cos, sin = jnp.cos(ang), jnp.sin(ang)
    c1 = jnp.concatenate([cos, cos], axis=1)
    s1 = jnp.concatenate([-sin, sin], axis=1)
    return jnp.tile(c1, (1, MLA_HEADS)), jnp.tile(s1, (1, MLA_HEADS))


def _rms_bwd(x, dh, g):
    r = lax.rsqrt(jnp.mean(x * x, axis=-1, keepdims=True) + EPS)
    xh = x * r
    gd = dh * g
    return r * (gd - xh * jnp.mean(gd * xh, axis=-1, keepdims=True)), dh * xh


def _norm_bwd_epilogue(dh, x, dres, g):
    dx, dg = _rms_bwd(x, dh, g)
    return dres + dx, dg


def _norm_bwd_call(x, dh, g, dres, name):
    w = x.width if isinstance(x, Cols) else x.shape[1]

    def with_res(xv, dv, rv, gv):
        dx, dg = _rms_bwd(xv, dv.astype(F32), gv)
        return rv + dx, dg

    def plain(xv, dv, gv):
        return _rms_bwd(xv, dv.astype(F32), gv)

    if dres is None:
        return _rowwise(plain, [x, dh], [g], [(w, F32)], [w], name=name)
    return _rowwise(with_res, [x, dh, dres], [g], [(w, F32)], [w], name=name)


def _gla_out_fwd(oraw, gr, g_out):
    outs = []
    for hh in range(GLA_HEADS):
        sl = slice(hh * GLA_DV, (hh + 1) * GLA_DV)
        oh = oraw[:, sl]
        n = oh * lax.rsqrt(jnp.mean(oh * oh, axis=-1, keepdims=True) + EPS) * g_out
        r = gr[:, sl]
        outs.append(n * (r * _sig(r)))
    return (jnp.concatenate(outs, axis=1),)


def _gla_out_bwd(oraw, gr, dout, g_out):
    d_o, d_r, dg = [], [], 0.0
    for hh in range(GLA_HEADS):
        sl = slice(hh * GLA_DV, (hh + 1) * GLA_DV)
        oh, r, do = oraw[:, sl], gr[:, sl], dout[:, sl].astype(F32)
        rs = lax.rsqrt(jnp.mean(oh * oh, axis=-1, keepdims=True) + EPS)
        sg = _sig(r)
        dn = do * (r * sg)
        d_r.append(do * (oh * rs * g_out) * (sg + r * sg * (1.0 - sg)))
        dx, dgh = _rms_bwd(oh, dn, g_out)
        d_o.append(dx)
        dg = dg + dgh
    return jnp.concatenate(d_o, axis=1), jnp.concatenate(d_r, axis=1), dg


def _adam(w, g, m, v):
    m = ADAM_B1 * m + (1.0 - ADAM_B1) * g
    v = ADAM_B2 * v + (1.0 - ADAM_B2) * (g * g)
    m_hat = m / (1.0 - ADAM_B1 ** ADAM_STEP)
    v_hat = v / (1.0 - ADAM_B2 ** ADAM_STEP)
    return -ADAM_LR * (m_hat / (jnp.sqrt(v_hat) + ADAM_EPS) + ADAM_WD * w), m, v


def _layer_fwd(x, mem, w, p, tabs, tag, carry_fox=None, after_fox=None, carry_mla=None):
    c4, s4 = tabs
    sv = {'x0': x}
    nm = lambda t: f'{t}_{tag}'
    za, h = _mm(x, w['in_a'], mode='nn', out_dtype=BF16, norm_g=p['g_mix'], emit_norm=True, name=nm('in_a'))
    zb = _mm(h, w['in_b'], mode='nn', out_dtype=F32, name=nm('in_b'))
    zc = _mm(h, w['in_c'], mode='nn', out_dtype=F32, name=nm('in_c'))
    sv.update(h=h, zc=zc)
    ff = Cols(zb, 128, B_FF // 128)
    (lf,) = _rowwise(lambda f, b: (_logsig(f + b),), [ff], [p['b_fox']], [(128, F32)], name=nm('fox_lf'))
    cum = _cumsum_rows(lf, reverse=False, name=nm('fox_cum'))
    ckf = jnp.pad(cum[:, :FOX_HEADS].T.reshape(2, 2, x.shape[0]), ((0, 0), (0, 6), (0, 0)))
    fox = dict(qc=0, kc=2, vc=4, nb=2, g=2, mode='causal', ck=ckf)
    o_fox, lse_fox, *carried = _mattn_fwd(za, za, za, name=nm('fox_attn'), comm=carry_fox, **fox)
    if after_fox is not None:
        w = {**w, **after_fox(carried[0])}
    sv.update(ff=ff, za=za, fox=fox, o_fox=o_fox, lse_fox=lse_fox)
    glow = Cols(zb, 128, B_GLOW // 128)
    gr = Cols(zb, 512, B_GR // 512)

    def gate_fn(gl, wg, bg):
        return (_logsig(_dot(gl.astype(BF16), wg) + bg) / GLA_TAU,)

    (la,) = _rowwise(gate_fn, [glow], [w['gate'], p['b_gla']], [(256, F32)], name=nm('gla_gate'))
    gla = dict(qc=B_GQ // LANES, kc=B_GK // LANES, vc=B_GV // LANES)
    oraw, states = _gla_fwd(zb, la, name=nm('gla'), **gla)
    (o_gla,) = _rowwise(_gla_out_fwd, [oraw, gr], [p['g_gla_out']], [(512, BF16)], name=nm('gla_out'))
    sv.update(glow=glow, gr=gr, zb=zb, la=la, gla=gla, states=states, oraw=oraw, o_gla=o_gla)
    mq = Cols(zb, 256, B_MQ // 256)
    mkv = Cols(zb, 128, B_MKV // 128)
    mkr2 = Cols(zb, 256, B_MKR // 256)
    qp, cqn = _mm(mq, w['uq'], mode='nn', out_dtype=F32, norm_g=p['g_mla_q'], emit_norm=True, name=nm('mla_uq'))
    kvp, ckvn = _mm(mkv, w['ukv'], mode='nn', out_dtype=BF16, norm_g=p['g_mla_kv'], emit_norm=True,
                    name=nm('mla_ukv'))

    def rope_fn(qv, kr, c4v, s4v):
        q_rope = qv[:, 256:384] * c4v + qv[:, 384:512] * s4v
        q_scaled = jnp.concatenate([qv[:, 0:256], q_rope], axis=1) * MLA_SCALE
        return q_scaled, kr[:, 0:128] * c4v + kr[:, 128:256] * s4v

    qall, kr4 = _rowwise(rope_fn, [qp, mkr2, c4, s4], [], [(384, BF16), (128, BF16)], name=nm('rope'))
    mla = dict(qc=0, kc=0, vc=2, nb=2, g=2, dq_scale=MLA_SCALE, mode='chunk', qr=qall, qrc=2, kr=kr4)
    o_mla, lse_mla, *carried = _mattn_fwd(qall, kvp, kvp, name=nm('mla_attn'), comm=carry_mla, **mla)
    if carry_mla is not None:
        sv['carried_mla'] = carried[0]
    sv.update(mq=mq, mkv=mkv, cqn=cqn, ckvn=ckvn, qall=qall, kvp=kvp, mla=mla, o_mla=o_mla, lse_mla=lse_mla)
    of_m, om_m = o_fox, o_mla
    sv.update(of_m=of_m, om_m=om_m)
    b_br = p['b_branch']

    y = _gated_merge([of_m, o_gla, om_m], [w['w_up_fox'], w['w_up_gla'], w['w_up_mla']], zc, b_br, name=nm('up_merge'))
    add = lambda acc, res: res + acc
    x1 = _mm(y, w['w_out'], mode='nn', out_dtype=F32, name=nm('out'), epilogue=add, extras=[(x, *_mn())])
    sv.update(y=y, x1=x1)
    qx, hx = _mm(x1, w['w_xq'], mode='nn', out_dtype=BF16, norm_g=p['g_xa'], emit_norm=True, name=nm('xq'),
                 epilogue=lambda acc: acc * XA_SCALE)
    kvx, mn = _mm(mem, w['w_xkv'], mode='nn', out_dtype=BF16, norm_g=p['g_mem'], emit_norm=True, name=nm('xkv'))
    xa = dict(qc=0, kc=0, vc=4, nb=4, g=1, dq_scale=XA_SCALE, mode='full')
    ox_m, lse_x = _mattn_fwd(qx, kvx, kvx, name=nm('xa_attn'), **xa)
    x2 = _mm(ox_m, w['w_xo'], mode='nn', out_dtype=F32, name=nm('xo'), epilogue=add, extras=[(x1, *_mn())])
    sv.update(hx=hx, mn=mn, qx=qx, kvx=kvx, xa=xa, lse_x=lse_x, ox_m=ox_m, x2=x2)
    hpre, hm = _mm(x2, w['w_mlp1'], mode='nn', out_dtype=BF16, norm_g=p['g_mlp'], emit_norm=True, name=nm('mlp1'))
    relu2 = lambda t: jnp.square(jnp.maximum(t.astype(F32), 0.0))
    x3 = _mm(hpre, w['w_mlp2'], mode='nn', out_dtype=F32, name=nm('mlp2'), a_fn=relu2, epilogue=add,
             extras=[(x2, *_mn())])
    sv.update(hpre=hpre, hm=hm, w=w)
    return x3, sv


EARLY = ('w_mlp1', 'w_mlp2', 'w_xo', 'w_xq', 'w_xkv', 'w_out', 'w_up_fox', 'w_up_gla', 'w_up_mla')
LATE = ('w_in', 'w_gla_gate', 'w_mla_uq', 'w_mla_ukv')


def _layer_bwd(dx3, mem, w, p, tabs, sv, tag, carry_mla=None, early=None):
    c4, s4 = tabs
    nm = lambda t: f'{t}_{tag}'
    s = dx3.shape[0]
    gw, gs = {}, {}
    relu2 = lambda t: jnp.square(jnp.maximum(t.astype(F32), 0.0))
    gw['w_mlp2'] = _mm(sv['hpre'], dx3, mode='tn', out_dtype=F32, name=nm('d_mlp2'), a_fn=relu2)
    dact = lambda acc, hp: acc * (2.0 * jnp.maximum(hp.astype(F32), 0.0))
    dhpre = _mm(dx3, w['w_mlp2'], mode='nt', out_dtype=BF16, name=nm('d_act'), epilogue=dact,
                extras=[(sv['hpre'], *_mn())])
    gw['w_mlp1'] = _mm(sv['hm'], dhpre, mode='tn', out_dtype=F32, name=nm('d_mlp1'))
    dx2, gs['g_mlp'] = _mm(dhpre, w['w_mlp1'], mode='nt', out_dtype=F32, name=nm('d_hm'), epilogue=_norm_bwd_epilogue,
                           col_sums=True, full_rows=True,
                           extras=[(sv['x2'], *_mn()), (dx3, *_mn()), (p['g_mlp'], *_nvec())])
    gw['w_xo'] = _mm(sv['ox_m'], dx2, mode='tn', out_dtype=F32, name=nm('d_xo'))
    dox = _mm(dx2, w['w_xo'], mode='nt', out_dtype=BF16, name=nm('d_ox'))
    dqx_m, dkx, dvx = _mattn_bwd(sv['qx'], sv['kvx'], sv['kvx'], sv['ox_m'], dox, sv['lse_x'], name=nm('xa_bwd'),
                                 **sv['xa'])
    dkvx = jnp.concatenate([dkx, dvx], axis=1).astype(BF16)
    gw['w_xq'] = _mm(sv['hx'], dqx_m, mode='tn', out_dtype=F32, name=nm('d_xq'))
    dx1, gs['g_xa'] = _mm(dqx_m, w['w_xq'], mode='nt', out_dtype=F32, name=nm('d_hx'), epilogue=_norm_bwd_epilogue,
                          col_sums=True, full_rows=True,
                          extras=[(sv['x1'], *_mn()), (dx2, *_mn()), (p['g_xa'], *_nvec())])
    gw['w_xkv'] = _mm(sv['mn'], dkvx, mode='tn', out_dtype=F32, name=nm('d_xkv'))
    dmn = _mm(dkvx, w['w_xkv'], mode='nt', out_dtype=F32, name=nm('d_mn'))
    _, gs['g_mem'] = _norm_bwd_call(mem, dmn, p['g_mem'], None, nm('d_norm_mem'))
    gw['w_out'] = _mm(sv['y'], dx1, mode='tn', out_dtype=F32, name=nm('d_out'))
    dy = _mm(dx1, w['w_out'], mode='nt', out_dtype=BF16, name=nm('d_y'))
    zc, b_br = sv['zc'], p['b_branch']

    def du_fn(dyv, zg, bb):
        g = _sig(zg + bb)
        d = dyv.astype(F32)
        return d * g[:, 0:1024], d * g[:, 1024:2048], d * g[:, 2048:3072]

    du = _rowwise(du_fn, [dy, zc], [b_br], [(D_MODEL, BF16)] * 3, name=nm('d_u'))

    def dgate(acc, dyv, zg, bb):
        g = _sig(zg + bb)
        return dyv.astype(F32) * acc * g * (1.0 - g)

    dzc, db_br, do_br = [], [], []
    for q, (o_m, wn) in enumerate(((sv['of_m'], 'w_up_fox'), (sv['o_gla'], 'w_up_gla'), (sv['om_m'], 'w_up_mla'))):
        dz, db = _mm(o_m, w[wn], mode='nn', out_dtype=BF16, name=nm(f'd_zg{q}'), epilogue=dgate, col_sums=True,
                     extras=[(dy, *_mn()), (zc, *_mn(col_off=1024 * q)), (b_br, *_nvec(col_off=1024 * q))])
        dzc.append(dz)
        db_br.append(db)
        gw[wn] = _mm(o_m, du[q], mode='tn', out_dtype=F32, name=nm(f'd_up{q}'))
        do_br.append(_mm(du[q], w[wn], mode='nt', out_dtype=F32 if q == 1 else BF16, name=nm(f'd_o{q}')))
    dzc = jnp.concatenate(dzc, axis=1)
    gs['b_branch'] = jnp.concatenate(db_br, axis=1)
    za = sv['za']
    carry_fox = None if early is None else early({nm_: gw[nm_] for nm_ in EARLY})
    dfq, dfk, dfv, dck, dcq, *carried_fox = _mattn_bwd(za, za, za, sv['o_fox'], do_br[0], sv['lse_fox'],
                                                       name=nm('fox_bwd'), comm=carry_fox, **sv['fox'])
    dcum = _padc(dck[:, :2, :].reshape(FOX_HEADS, s).T + dcq.reshape(s, 2, LANES)[:, :, :2].reshape(s, FOX_HEADS), 128)
    dlf = _cumsum_rows(dcum, reverse=True, name=nm('fox_dcum'))

    def dff_fn(dl, f, b):
        d = dl * _sig(-(f + b))
        return d, d

    dff, db_fox = _rowwise(dff_fn, [dlf, sv['ff']], [p['b_fox']], [(128, F32)], [128], name=nm('fox_dff'))
    gs['b_fox'] = db_fox
    dza = jnp.concatenate([dfq, dfk, dfv], axis=1).astype(BF16)
    dqn, dkn, dvv, dq_rope, dk_rope, *carried_mla = _mattn_bwd(sv['qall'], sv['kvp'], sv['kvp'], sv['o_mla'], do_br[2],
                                                               sv['lse_mla'], name=nm('mla_bwd'), comm=carry_mla,
                                                               **sv['mla'])

    def drope_fn(dn, dq, dk, c4v, s4v):
        return jnp.concatenate([dn, dq * c4v, dq * s4v], axis=1), jnp.concatenate([dk * c4v, dk * s4v], axis=1)

    dqp, dmkr2 = _rowwise(drope_fn, [dqn, dq_rope, dk_rope, c4, s4], [], [(512, BF16), (256, BF16)], name=nm('d_rope'))
    dkvp = jnp.concatenate([dkn, dvv], axis=1).astype(BF16)
    gw['uq'] = _mm(sv['cqn'], dqp, mode='tn', out_dtype=F32, name=nm('d_uq'))
    dcqn = _mm(dqp, w['uq'], mode='nt', out_dtype=F32, name=nm('d_cqn'))
    gw['ukv'] = _mm(sv['ckvn'], dkvp, mode='tn', out_dtype=F32, name=nm('d_ukv'))
    dckvn = _mm(dkvp, w['ukv'], mode='nt', out_dtype=F32, name=nm('d_ckvn'))
    dmq, gs['g_mla_q'] = _norm_bwd_call(sv['mq'], dcqn, p['g_mla_q'], None, nm('d_norm_q'))
    dmkv, gs['g_mla_kv'] = _norm_bwd_call(sv['mkv'], dckvn, p['g_mla_kv'], None, nm('d_norm_kv'))
    doraw, dgr, gs['g_gla_out'] = _rowwise(_gla_out_bwd, [sv['oraw'], sv['gr'], do_br[1]], [p['g_gla_out']],
                                           [(512, F32), (512, BF16)], [128], name=nm('d_gla_out'))
    st = sv['states']
    st_prev = jnp.concatenate([jnp.zeros_like(st[:, :1]), st[:, :-1]], axis=1)
    dgq, dgk, dgv, dla = _gla_bwd(sv['zb'], sv['la'], st, st_prev, doraw, name=nm('gla_bwd'), **sv['gla'])

    def dgate_fn(dl, gl, wg, bg):
        pre = _dot(gl.astype(BF16), wg) + bg
        dpre = dl * (1.0 / GLA_TAU) * _sig(-pre)
        return dpre, _dot(dpre.astype(BF16), wg, NT), dpre

    dpre, dglow, gs['b_gla'] = _rowwise(dgate_fn, [dla, sv['glow']], [w['gate'], p['b_gla']],
                                        [(256, BF16), (128, BF16)], [256], name=nm('d_gla_gate'))
    gw['gate'] = _mm(sv['glow'], dpre, mode='tn', out_dtype=F32, name=nm('d_wgate'))
    bf = lambda t: t.astype(BF16)
    dzb = jnp.concatenate([dgr, bf(dgq), bf(dgk), bf(dgv), bf(dmq), dmkr2, bf(dff), dglow, bf(dmkv),
                           jnp.zeros((s, B_W - B_END), BF16)], axis=1)
    h = sv['h']
    gw['in_a'] = _mm(h, dza, mode='tn', out_dtype=F32, name=nm('d_in_a'))
    gw['in_b'] = _mm(h, dzb, mode='tn', out_dtype=F32, name=nm('d_in_b'))
    gw['in_c'] = _mm(h, dzc, mode='tn', out_dtype=F32, name=nm('d_in_c'))
    add = lambda acc, prev: prev + acc
    dh = _mm(dza, w['in_a'], mode='nt', out_dtype=F32, name=nm('d_h_a'))
    dh = _mm(dzb, w['in_b'], mode='nt', out_dtype=F32, name=nm('d_h_b'), epilogue=add, extras=[(dh, *_mn())])
    dx0, gs['g_mix'] = _mm(dzc, w['in_c'], mode='nt', out_dtype=F32, name=nm('d_h_c'), col_sums=True, full_rows=True,
                           epilogue=lambda acc, prev, xv, rv, gv: _norm_bwd_epilogue(prev + acc, xv, rv, gv),
                           extras=[(dh, *_mn()), (sv['x0'], *_mn()), (dx1, *_mn()), (p['g_mix'], *_nvec())])
    return dx0, gw, gs, (carried_mla or [None])[0], (carried_fox or [None])[0]


def _loss_head(x, target, g_final):
    d = x.shape[1]

    def fn(xv, tv, gv):
        r = lax.rsqrt(jnp.mean(xv * xv, axis=-1, keepdims=True) + EPS)
        xh = xv * r
        e = xh * gv - tv
        dy = e * (1.0 / d)
        gd = dy * gv
        dx = r * (gd - xh * jnp.mean(gd * xh, axis=-1, keepdims=True))
        row_loss = 0.5 * jnp.mean(e * e, axis=-1, keepdims=True)
        return dx, dy * xh, jnp.broadcast_to(row_loss, (xv.shape[0], LANES))

    return _rowwise(fn, [x, target], [g_final], [(d, F32)], [d, LANES], name='loss_head')


def _small_sizes(shapes):
    return [math.prod(shapes[nm]) for nm in SMALL]


def _step(args):
    shapes = {nm: args[nm].shape for nm in ORDER}
    x, mem, target = args['x'][0], args['mem'][0], args['loss_target'][0]
    s = x.shape[0]

    def wire(nm, l):
        w = args[nm][l].astype(BF16)
        if nm == 'w_in':
            w = jnp.pad(w, ((0, 0), (0, WIN_PAD - WIN_SHARD)))
        if nm == 'w_gla_gate':
            w = jnp.pad(w, ((0, GATE_WIRE_ROWS - GLA_RANK), (0, 0)))
        return w

    axis_of = dict(BIG)
    names = tuple(nm for nm, _ in BIG)
    wires = lambda l, nms: [wire(nm, l) for nm in nms]
    width = lambda nm: WIN_PAD if nm == 'w_in' else args[nm].shape[2]
    side_by_side = lambda nms: [axis_of[nm] == 2 and width(nm) % LANES == 0 for nm in nms]
    over_ici = lambda l, nms: _gather_over_ici(wires(l, nms), side_by_side(nms))

    def whole(parts, nms, tag):
        side = side_by_side(nms)
        parts = _run_comm(_gather_over_d2d(parts, side), name=f'gather_d2d_{tag}', alias=True)
        full = {nm: p if sd else _full_layer(p, axis_of[nm]) for nm, p, sd in zip(nms, parts, side)}
        if 'w_gla_gate' in full:
            full['w_gla_gate'] = full['w_gla_gate'][:GLA_RANK]
        return full

    tabs = _rope_tables(s)
    layers_p = []
    for l in range(DEPTH):
        layers_p.append({
            'g_mix': args['g_mix'][l][None], 'b_fox': _padc(args['b_fox_forget'][l][None], 128),
            'b_gla': args['b_gla_gate'][l][None], 'g_gla_out': args['g_gla_out'][l][None],
            'g_mla_q': args['g_mla_q'][l][None], 'g_mla_kv': args['g_mla_kv'][l][None],
            'b_branch': args['b_branch_gate'][l][None], 'g_xa': args['g_xa'][l][None],
            'g_mem': args['g_mem'][l][None], 'g_mlp': args['g_mlp'][l][None]})

    first = _run_comm(over_ici(0, LATE), name='gather_ici_first_l0')
    w_now = _repack_layer_weights(whole(first, LATE, 'first_l0'))
    saved = []
    xl = x
    for l in range(DEPTH):
        carry_fox = over_ici(0, EARLY) if l == 0 else None
        after_fox = (lambda parts: whole(parts, EARLY, 'rest_l0')) if l == 0 else None
        carry_mla = over_ici(l + 1, names) if l + 1 < DEPTH else None
        xl, sv = _layer_fwd(xl, mem, w_now, layers_p[l], tabs, f'l{l}', carry_fox=carry_fox, after_fox=after_fox,
                            carry_mla=carry_mla)
        saved.append(sv)
        if carry_mla is not None:
            w_now = _repack_layer_weights(whole(sv.pop('carried_mla'), names, f'l{l + 1}'))
    dx, dg_final, loss_lanes = _loss_head(xl, target, args['g_final'][None])
    cidx = lax.axis_index('c')
    chip = 2 * lax.axis_index('x') + lax.axis_index('y')

    def pair_sums(gw, nms, tag):
        mine, theirs = [], []
        for nm in nms:
            shards = _split_full(gw[nm], axis_of[nm]).astype(BF16)
            h = shards.shape[1] // 2
            mine.append(lax.dynamic_slice_in_dim(shards, cidx * h, h, axis=1))
            theirs.append(lax.dynamic_slice_in_dim(shards, (1 - cidx) * h, h, axis=1))
        got = _to_sibling(theirs, name=f'grads_swap_{tag}')
        pairs = []
        for nm, a, b in zip(nms, mine, got):
            _, h, n = a.shape
            (p,) = _rowwise(lambda u, v: (u.astype(F32) + v.astype(F32),),
                            [a.reshape(N_CHIPS * h, n), b.reshape(N_CHIPS * h, n)], [], [(n, BF16)],
                            name=f'pair_sum_{nm}_{tag}')
            pairs.append(p.reshape(N_CHIPS, h, n))
        return pairs

    def finish(pairs, from_chips, nms, tag):
        own = [lax.dynamic_index_in_dim(p, chip, axis=0, keepdims=False) for p in pairs]
        mine = [_sum_chips(o, r, name=f'chip_sum_{nm}_{tag}') for nm, o, r in zip(nms, own, from_chips)]
        theirs = _to_sibling(mine, name=f'grads_join_{tag}')
        return {nm: jnp.where(cidx == 0, jnp.concatenate([a, b]), jnp.concatenate([b, a]))
                for nm, a, b in zip(nms, mine, theirs)}

    gs_layers, done = [None] * DEPTH, [{} for _ in range(DEPTH)]
    above = None
    for l in reversed(range(DEPTH)):
        lowest, early_pairs = l == 0, []

        def early(gw_early, l=l, early_pairs=early_pairs):
            early_pairs.extend(pair_sums(gw_early, EARLY, f'early_l{l}'))
            return _chip_exchange(early_pairs)

        carry_mla = None if above is None else _chip_exchange(above[1])
        dx, gw, gs_layers[l], got_mla, got_fox = _layer_bwd(
            dx, mem, saved[l]['w'], layers_p[l], tabs, saved[l], f'l{l}', carry_mla=carry_mla,
            early=early if lowest else None)
        if above is not None:
            done[above[0]].update(finish(above[1], got_mla, names, f'l{above[0]}'))
        grads = _unpack_layer_grads(gw)
        if lowest:
            done[l].update(finish(early_pairs, got_fox, EARLY, f'early_l{l}'))
            late_pairs = pair_sums(grads, LATE, f'late_l{l}')
            from_late = _run_comm(_chip_exchange(late_pairs), name=f'grads_exchange_late_l{l}')
            done[l].update(finish(late_pairs, from_late, LATE, f'late_l{l}'))
        else:
            above = (l, pair_sums(grads, names, f'l{l}'))
    grad_x = dx[None]
    gshard = {nm: jnp.stack([done[l][nm] for l in range(DEPTH)]) for nm in names}

    small_g = []
    for nm, key in (('g_mix', 'g_mix'), ('b_fox_forget', 'b_fox'), ('b_gla_gate', 'b_gla'),
                    ('g_gla_out', 'g_gla_out'), ('g_mla_q', 'g_mla_q'), ('g_mla_kv', 'g_mla_kv'),
                    ('b_branch_gate', 'b_branch'), ('g_xa', 'g_xa'), ('g_mem', 'g_mem'), ('g_mlp', 'g_mlp')):
        width = shapes[nm][1]
        small_g.append(jnp.concatenate([gs_layers[l][key][0, :width] for l in range(DEPTH)]))
    small_g.append(dg_final[0])
    small_g.append(loss_lanes[0, :1])
    flat = jnp.concatenate(small_g)
    n_small = flat.shape[0]
    srows = -(-n_small // (8 * LANES)) * 8
    pad = lambda v: jnp.pad(v, (0, srows * LANES - v.shape[0])).reshape(srows, LANES)
    all_small = _all_gather8(pad(flat), name='gather_small')
    sw, sm, svv = (pad(jnp.concatenate([args[pre + nm].reshape(-1) for nm in SMALL] + [jnp.zeros((1,), F32)]))
                   for pre in ('', 'm_', 'v_'))

    def small_body(g_ref, w_ref, m_ref, v_ref, go_ref, d_ref, mo_ref, vo_ref):
        g = g_ref[0]
        for q in range(1, N_DEV):
            g = g + g_ref[q]
        go_ref[...] = g
        d_ref[...], mo_ref[...], vo_ref[...] = _adam(w_ref[...], g, m_ref[...], v_ref[...])

    sg, sd, snm, snv = pl.pallas_call(
        small_body, name='small_sum_adam', out_shape=[jax.ShapeDtypeStruct((srows, LANES), F32)] * 4,
        compiler_params=pltpu.CompilerParams(vmem_limit_bytes=VMEM_LIMIT))(all_small, sw, sm, svv)

    def unsmall(buf):
        v, out, off = buf.reshape(-1), {}, 0
        for nm in SMALL:
            nel = math.prod(shapes[nm])
            out[nm] = v[off:off + nel].reshape(shapes[nm])
            off += nel
        return out, v[off]

    res = {}
    (res['grad'], loss), (res['delta'], _), (res['m'], _), (res['v'], _) = (unsmall(t) for t in (sg, sd, snm, snv))

    for nm, _ in BIG:
        shp = args[nm].shape
        view = lambda t: t.reshape(shp[0] * shp[1], shp[2])
        d, m2, v2 = _rowwise(_adam, [view(args[nm]), view(gshard[nm]), view(args['m_' + nm]), view(args['v_' + nm])],
                             [], [(shp[2], F32)] * 3, name=f'adam_{nm}')
        res['grad'][nm], res['delta'][nm], res['m'][nm], res['v'][nm] = (
            gshard[nm], d.reshape(shp), m2.reshape(shp), v2.reshape(shp))

    return (loss, grad_x, *[res['grad'][nm] for nm in ORDER], *[res['delta'][nm] for nm in ORDER],
            *[res['m'][nm] for nm in ORDER], *[res['v'][nm] for nm in ORDER])


def kernel(x, mem, g_mix, w_in, b_fox_forget, w_gla_gate, b_gla_gate, g_gla_out, g_mla_q, w_mla_uq, g_mla_kv, w_mla_ukv, b_branch_gate, w_up_fox, w_up_gla, w_up_mla, w_out, g_xa, g_mem, w_xq, w_xkv, w_xo, g_mlp, w_mlp1, w_mlp2, g_final, loss_target, m_g_mix, m_w_in, m_b_fox_forget, m_w_gla_gate, m_b_gla_gate, m_g_gla_out, m_g_mla_q, m_w_mla_uq, m_g_mla_kv, m_w_mla_ukv, m_b_branch_gate, m_w_up_fox, m_w_up_gla, m_w_up_mla, m_w_out, m_g_xa, m_g_mem, m_w_xq, m_w_xkv, m_w_xo, m_g_mlp, m_w_mlp1, m_w_mlp2, m_g_final, v_g_mix, v_w_in, v_b_fox_forget, v_w_gla_gate, v_b_gla_gate, v_g_gla_out, v_g_mla_q, v_w_mla_uq, v_g_mla_kv, v_w_mla_ukv, v_b_branch_gate, v_w_up_fox, v_w_up_gla, v_w_up_mla, v_w_out, v_g_xa, v_g_mem, v_w_xq, v_w_xkv, v_w_xo, v_g_mlp, v_w_mlp1, v_w_mlp2, v_g_final):
    return _step(dict(locals()))
```

```python
import functools
import math
import typing

import jax
import jax.numpy as jnp
from jax import lax
from jax.experimental import pallas as pl
from jax.experimental.pallas import tpu as pltpu

F32 = jnp.float32
BF16 = jnp.bfloat16
MESH = pl.DeviceIdType.MESH

D_MODEL = 1024
DEPTH = 2
CHUNK = 64
EPS = 1e-6
FOX_HEADS, FOX_HD = 4, 64
GLA_HEADS, GLA_DK, GLA_DV, GLA_RANK, GLA_TAU = 4, 64, 128, 16, 16.0
MLA_HEADS, MLA_Q_RANK, MLA_KV_RANK, MLA_NOPE, MLA_ROPE, MLA_VD = 4, 256, 128, 64, 32, 64
ROPE_BASE = 10000.0
XA_HEADS, XA_HD = 4, 128
D_FF = 4 * D_MODEL
IN_SIZES = (256, 256, 256, 4, 256, 256, 512, 16, 512, 256, 128, 32, 3072)
N_IN = sum(IN_SIZES)

ADAM_LR, ADAM_B1, ADAM_B2, ADAM_EPS, ADAM_WD, ADAM_STEP = 0.001, 0.9, 0.999, 1e-08, 0.01, 10

N_CHIPS = 4
N_DEV = 8
LANES = 128
VMEM_LIMIT = 48 * 1024 * 1024
MASK_VALUE = -1e30

BIG = (('w_in', 2), ('w_gla_gate', 2), ('w_mla_uq', 2), ('w_mla_ukv', 2), ('w_up_fox', 2), ('w_up_gla', 2),
       ('w_up_mla', 2), ('w_out', 1), ('w_xq', 1), ('w_xkv', 1), ('w_xo', 2), ('w_mlp1', 2), ('w_mlp2', 1))
SMALL = ('g_mix', 'b_fox_forget', 'b_gla_gate', 'g_gla_out', 'g_mla_q', 'g_mla_kv', 'b_branch_gate',
         'g_xa', 'g_mem', 'g_mlp', 'g_final')
ORDER = ('g_mix', 'w_in', 'b_fox_forget', 'w_gla_gate', 'b_gla_gate', 'g_gla_out', 'g_mla_q', 'w_mla_uq',
         'g_mla_kv', 'w_mla_ukv', 'b_branch_gate', 'w_up_fox', 'w_up_gla', 'w_up_mla', 'w_out', 'g_xa', 'g_mem',
         'w_xq', 'w_xkv', 'w_xo', 'g_mlp', 'w_mlp1', 'w_mlp2', 'g_final')


def _params(*sem):
    return pltpu.CompilerParams(dimension_semantics=sem, vmem_limit_bytes=VMEM_LIMIT)


def _sig(x):
    return 1.0 / (1.0 + jnp.exp(-x))


def _logsig(x):
    return jnp.minimum(x, 0.0) - jnp.log(1.0 + jnp.exp(-jnp.abs(x)))


NN = (((1,), (0,)), ((), ()))
NT = (((1,), (1,)), ((), ()))
TN = (((0,), (0,)), ((), ()))


def _dot(a, b, dims=NN):
    return lax.dot_general(a, b, dims, preferred_element_type=F32)


class Cols(typing.NamedTuple):
    arr: jax.Array
    width: int
    blk: int


def _tri_dot(tri, x):
    hi = x.astype(BF16)
    r1 = x - hi.astype(F32)
    mid = r1.astype(BF16)
    lo = (r1 - mid.astype(F32)).astype(BF16)
    return _dot(tri, hi) + _dot(tri, mid) + _dot(tri, lo)


MM_TILES = ((1024, 1024), (1024, 512), (512, 1024), (512, 512), (256, 1024), (512, 256), (256, 512), (256, 256),
            (128, 1024), (128, 128))
MM_VMEM_BUDGET = 38 * 1024 * 1024


def _mm_tiles(m, n, k, a_bytes, b_bytes, out_bytes, ex_bytes, has_norm, emit_norm, has_fn, full_rows):
    for tm, tn in MM_TILES:
        tm, tn = min(tm, m), min(tn, n)
        if m % tm or n % tn or (full_rows and tn != n):
            continue
        blocks = tm * k * a_bytes + k * tn * b_bytes + tm * tn * (out_bytes + ex_bytes) + (tm * k * 2 if emit_norm else 0)
        temps = tm * tn * 4 + (tm * k * 2 if has_norm else 0) + (tm * k * 6 if has_fn or has_norm else 0)
        if 2 * blocks + temps <= MM_VMEM_BUDGET:
            return tm, tn
    raise ValueError((m, n, k))


def _mm(a, b, *, mode, out_dtype, name, norm_g=None, emit_norm=False, a_fn=None, extras=(), epilogue=None,
        col_sums=False, full_rows=False):
    a_blk = 0
    if isinstance(a, Cols):
        a, width, a_blk = a
        a_shape = (a.shape[0], width)
    else:
        a_shape = a.shape
    if mode == 'tn':
        k, m = a_shape
    else:
        m, k = a_shape
    n = b.shape[0] if mode == 'nt' else b.shape[1]
    assert (b.shape[1] if mode == 'nt' else b.shape[0]) == k, (name, a.shape, b.shape)
    has_norm = norm_g is not None
    ex_bytes = sum(arr.dtype.itemsize for arr, kind, _ in extras if kind == 'mn')
    tm, tn = _mm_tiles(m, n, k, a.dtype.itemsize, b.dtype.itemsize, jnp.dtype(out_dtype).itemsize, ex_bytes, has_norm,
                       emit_norm, a_fn is not None, full_rows)
    assert all(col % tn == 0 for _, _, col in extras), (name, tn)
    assert a_blk == 0 or (mode == 'nn') or (mode == 'tn' and tm == m)
    assert not (col_sums and (has_norm or emit_norm))
    ij = (lambda f: lambda g0, g1: f(g1, g0)) if col_sums else (lambda f: f)
    spec = lambda blk, f: pl.BlockSpec(blk, ij(f))
    if mode == 'tn':
        a_spec = spec((k, tm), lambda i, j: (0, i + a_blk))
    else:
        a_spec = spec((tm, k), lambda i, j: (i, a_blk))
    b_spec = spec((tn, k), lambda i, j: (j, 0)) if mode == 'nt' else spec((k, tn), lambda i, j: (0, j))
    dims = {'nn': NN, 'nt': NT, 'tn': TN}[mode]
    assert not (has_norm and mode != 'nn')
    n_ex = len(extras)

    def body(*refs):
        a_ref, b_ref = refs[0], refs[1]
        pos = 2
        g_ref = None
        if has_norm:
            g_ref = refs[pos]
            pos += 1
        ex_refs = refs[pos:pos + n_ex]
        pos += n_ex
        o_ref = refs[pos]
        pos += 1
        h_ref = None
        if emit_norm:
            h_ref = refs[pos]
            pos += 1
        if has_norm:
            an_ref = refs[pos]

            @pl.when(pl.program_id(1) == 0)
            def _():
                xf = a_ref[...].astype(F32)
                y = xf * lax.rsqrt(jnp.mean(xf * xf, axis=-1, keepdims=True) + EPS) * g_ref[...]
                an_ref[...] = y.astype(BF16)
                if emit_norm:
                    h_ref[...] = y.astype(BF16)

            av = an_ref[...]
        else:
            av = a_ref[...]
            if a_fn is not None:
                av = a_fn(av)
            av = av.astype(BF16)
        acc = _dot(av, b_ref[...].astype(BF16), dims)
        if epilogue is not None:
            acc = epilogue(acc, *[r[...] for r in ex_refs])
        acc, to_sum = acc if isinstance(acc, tuple) else (acc, acc)
        o_ref[...] = acc.astype(out_dtype)
        if col_sums:
            sum_ref = refs[pos]

            @pl.when(pl.program_id(1) == 0)
            def _():
                sum_ref[...] = jnp.zeros_like(sum_ref)

            sum_ref[...] += jnp.sum(to_sum, axis=0, keepdims=True)

    in_specs = [a_spec, b_spec]
    args = [a, b]
    if has_norm:
        in_specs.append(pl.BlockSpec((1, k), lambda i, j: (0, 0)))
        args.append(norm_g)
    for arr, kind, col in extras:
        if kind == 'mn':
            in_specs.append(spec((tm, tn), lambda i, j, o=col // tn: (i, j + o)))
        else:
            in_specs.append(spec((1, tn), lambda i, j, o=col // tn: (0, j + o)))
        args.append(arr)
    out_shape = [jax.ShapeDtypeStruct((m, n), out_dtype)]
    out_specs = [spec((tm, tn), lambda i, j: (i, j))]
    if emit_norm:
        out_shape.append(jax.ShapeDtypeStruct((m, k), BF16))
        out_specs.append(pl.BlockSpec((tm, k), lambda i, j: (i, 0)))
    if col_sums:
        out_shape.append(jax.ShapeDtypeStruct((1, n), F32))
        out_specs.append(spec((1, tn), lambda i, j: (0, j)))
    scratch = [pltpu.VMEM((tm, k), BF16)] if has_norm else []
    grid = (n // tn, m // tm) if col_sums else (m // tm, n // tn)
    res = pl.pallas_call(
        body, name=name, grid=grid, in_specs=in_specs, out_specs=out_specs, out_shape=out_shape,
        scratch_shapes=scratch, compiler_params=_params('arbitrary', 'arbitrary'))(*args)
    return res if emit_norm or col_sums else res[0]


def _gated_merge(outs, ups, zg, bias, *, name, tm=1024, tn=512):
    s, n, nq = zg.shape[0], ups[0].shape[1], len(outs)
    tm, tn = min(tm, s), min(tn, n)
    per = n // tn

    def body(*refs):
        y = None
        for q in range(nq):
            o_ref, w_ref, z_ref, b_ref = refs[q], refs[nq + q], refs[2 * nq + q], refs[3 * nq + q]
            term = _sig(z_ref[...].astype(F32) + b_ref[...]) * _dot(o_ref[...], w_ref[...])
            y = term if y is None else y + term
        refs[4 * nq][...] = y.astype(BF16)

    in_specs = [pl.BlockSpec((tm, o.shape[1]), lambda i, j: (i, 0)) for o in outs]
    in_specs += [pl.BlockSpec((u.shape[0], tn), lambda i, j: (0, j)) for u in ups]
    in_specs += [pl.BlockSpec((tm, tn), lambda i, j, q=q: (i, j + q * per)) for q in range(nq)]
    in_specs += [pl.BlockSpec((1, tn), lambda i, j, q=q: (0, j + q * per)) for q in range(nq)]
    return pl.pallas_call(body, name=name, grid=(s // tm, per), in_specs=in_specs,
                          out_specs=pl.BlockSpec((tm, tn), lambda i, j: (i, j)),
                          out_shape=jax.ShapeDtypeStruct((s, n), BF16),
                          compiler_params=_params('arbitrary', 'arbitrary'))(*outs, *ups, *[zg] * nq, *[bias] * nq)


def _gated_merge_bwd(dy, zg, bias, outs, ups, do_dtypes, *, name, tm=512):
    s, n = dy.shape
    nq = len(outs)
    tm = min(tm, s)

    def body(*refs):
        dy_ref, zg_ref, b_ref = refs[:3]
        o_refs, w_refs = refs[3:3 + nq], refs[3 + nq:3 + 2 * nq]
        du_refs, do_refs = refs[3 + 2 * nq:3 + 3 * nq], refs[3 + 3 * nq:3 + 4 * nq]
        dz_ref, db_ref = refs[3 + 4 * nq:]

        @pl.when(pl.program_id(0) == 0)
        def _():
            db_ref[...] = jnp.zeros_like(db_ref)

        d = dy_ref[...].astype(F32)
        for q in range(nq):
            cols = slice(q * n, (q + 1) * n)
            g = _sig(zg_ref[:, cols].astype(F32) + b_ref[:, cols])
            du = (d * g).astype(BF16)
            du_refs[q][...] = du
            do_refs[q][...] = _dot(du, w_refs[q][...], NT).astype(do_dtypes[q])
            dz = d * _dot(o_refs[q][...], w_refs[q][...]) * g * (1.0 - g)
            dz_ref[:, cols] = dz.astype(BF16)
            db_ref[:, cols] += jnp.sum(dz, axis=0, keepdims=True)

    row = lambda w: pl.BlockSpec((tm, w), lambda i: (i, 0))
    whole = lambda a: pl.BlockSpec(a.shape, lambda i: (0, 0))
    in_specs = [row(n), row(nq * n), whole(bias)] + [row(o.shape[1]) for o in outs] + [whole(u) for u in ups]
    out_specs = [row(n)] * nq + [row(o.shape[1]) for o in outs] + [row(nq * n), pl.BlockSpec((1, nq * n), lambda i: (0, 0))]
    out_shape = ([jax.ShapeDtypeStruct((s, n), BF16)] * nq
                 + [jax.ShapeDtypeStruct((s, o.shape[1]), dt) for o, dt in zip(outs, do_dtypes)]
                 + [jax.ShapeDtypeStruct((s, nq * n), BF16), jax.ShapeDtypeStruct((1, nq * n), F32)])
    res = pl.pallas_call(body, name=name, grid=(s // tm,), in_specs=in_specs, out_specs=out_specs, out_shape=out_shape,
                         compiler_params=_params('arbitrary'))(dy, zg, bias, *outs, *ups)
    return res[:nq], res[nq:2 * nq], res[2 * nq], res[2 * nq + 1]


def _mn(col_off=0):
    return 'mn', col_off


def _nvec(col_off=0):
    return 'n', col_off


def _rowwise(fn, rows, consts, outs, sums=(), *, name, ts=256):
    views = [x if isinstance(x, Cols) else Cols(x, x.shape[1], 0) for x in rows]
    rows = [v.arr for v in views]
    r = rows[0].shape[0]
    ts = min(ts, r)
    assert r % ts == 0, (name, r, ts)
    nr, nc, no, ns = len(rows), len(consts), len(outs), len(sums)

    def body(*refs):
        vals = fn(*[x[...] for x in refs[:nr + nc]])
        for q in range(no):
            refs[nr + nc + q][...] = vals[q].astype(outs[q][1])
        if ns:
            @pl.when(pl.program_id(0) == 0)
            def _():
                for q in range(ns):
                    refs[nr + nc + no + q][...] = jnp.zeros((1, sums[q]), F32)

            for q in range(ns):
                refs[nr + nc + no + q][...] += jnp.sum(vals[no + q].astype(F32), axis=0, keepdims=True)

    in_specs = [pl.BlockSpec((ts, v.width), lambda i, blk=v.blk: (i, blk)) for v in views]
    in_specs += [pl.BlockSpec(x.shape, lambda i, nd=x.ndim: (0,) * nd) for x in consts]
    out_specs = [pl.BlockSpec((ts, w), lambda i: (i, 0)) for w, _ in outs]
    out_specs += [pl.BlockSpec((1, w), lambda i: (0, 0)) for w in sums]
    out_shape = [jax.ShapeDtypeStruct((r, w), dt) for w, dt in outs]
    out_shape += [jax.ShapeDtypeStruct((1, w), F32) for w in sums]
    return pl.pallas_call(body, name=name, grid=(r // ts,), in_specs=in_specs, out_specs=out_specs,
                          out_shape=out_shape, compiler_params=_params('arbitrary'))(*rows, *consts)


def _cumsum_rows(x, *, reverse, name, bs=256):
    s, w = x.shape
    bs = min(bs, s)
    nb = s // bs

    def body(x_ref, o_ref, carry):
        @pl.when(pl.program_id(0) == 0)
        def _():
            carry[...] = jnp.zeros_like(carry)

        r = lax.broadcasted_iota(jnp.int32, (bs, bs), 0)
        c = lax.broadcasted_iota(jnp.int32, (bs, bs), 1)
        tri = jnp.where((c >= r) if reverse else (c <= r), 1.0, 0.0).astype(BF16)
        xv = x_ref[...]
        o_ref[...] = _tri_dot(tri, xv) + carry[...]
        carry[...] += jnp.sum(xv, axis=0, keepdims=True)

    imap = (lambda i: (nb - 1 - i, 0)) if reverse else (lambda i: (i, 0))
    return pl.pallas_call(body, name=name, grid=(nb,), in_specs=[pl.BlockSpec((bs, w), imap)],
                          out_specs=pl.BlockSpec((bs, w), imap), out_shape=jax.ShapeDtypeStruct((s, w), F32),
                          scratch_shapes=[pltpu.VMEM((1, w), F32)], compiler_params=_params('arbitrary'))(x)


def _mask(mode, q0, k0, bq, bk):
    qpos = q0 + lax.broadcasted_iota(jnp.int32, (bq, bk), 0)
    kpos = k0 + lax.broadcasted_iota(jnp.int32, (bq, bk), 1)
    if mode == 'causal':
        return kpos <= qpos
    return kpos < (jnp.right_shift(qpos, int(math.log2(CHUNK))) + 1) * CHUNK


ROPE_SHIFT = int(math.log2(MLA_ROPE))
FOX_SCALE, MLA_SCALE, XA_SCALE = FOX_HD ** -0.5, (MLA_NOPE + MLA_ROPE) ** -0.5, XA_HD ** -0.5
ATTN_ROW_SLAB = 512


def _lane_masks(g, b, rope):
    lane = lax.broadcasted_iota(jnp.int32, (1, LANES), 1)
    heads = [None if g == 1 else (lane >= hh * (LANES // g)) & (lane < (hh + 1) * (LANES // g)) for hh in range(g)]
    ropes = [jnp.right_shift(lane, ROPE_SHIFT) == b * g + hh for hh in range(g)] if rope else [None] * g
    return heads, ropes


def _sel(mask, x):
    return x if mask is None else jnp.where(mask, x, jnp.zeros_like(x))


class Step(typing.NamedTuple):
    qi: typing.Any
    kj: typing.Any
    first: typing.Any
    last: typing.Any
    plain: typing.Any
    masked: typing.Any


def _fwd_steps(tri, nq, nk):
    if not tri:
        return (nq, nk), lambda i, j: Step(i, j, j == 0, j == nk - 1, True, False)
    if nq % 2:
        return (nq, nk), lambda i, j: Step(i, jnp.minimum(i, j), j == 0, j == nk - 1, j < i, j == i)

    def at(i, t):
        low = t <= i
        diag = (t == i) | (t == nq)
        return Step(jnp.where(low, i, nq - 1 - i), jnp.where(low, t, t - (i + 1)), (t == 0) | (t == i + 1), diag,
                    jnp.logical_not(diag), diag)

    return (nq // 2, nq + 1), at


def _bwd_steps(tri, nq, nk):
    if not tri:
        return (nk, nq), lambda j, i: Step(i, j, i == 0, i == nq - 1, True, False)
    if nk % 2:
        return (nk, nq), lambda j, i: Step(jnp.maximum(i, j), j, i == 0, i == nq - 1, i > j, i == j)

    def at(j, t):
        n1 = nq - j
        low = t < n1
        diag = (t == 0) | (t == n1)
        return Step(jnp.where(low, j + t, nk - 1 - j + t - n1), jnp.where(low, j, nk - 1 - j), diag,
                    (t == n1 - 1) | (t == nq), jnp.logical_not(diag), diag)

    return (nk // 2, nq + 1), at


def _carried(comm, refs, n_in, n_out):
    ci, co = len(comm.ins), len(comm.out_shapes)
    ins = refs[n_in:n_in + ci]
    outs = refs[n_in + ci + n_out:n_in + ci + n_out + co]
    rest = refs[:n_in] + refs[n_in + ci:n_in + ci + n_out] + refs[n_in + ci + n_out + co:-2]
    return rest, (ins, outs, refs[-2], refs[-1])


def _mattn_fwd(q, k, v, *, qc, kc, vc, nb, g, mode, name, dq_scale=1.0, ck=None, qr=None, qrc=0, kr=None, blk=512,
               comm=None):
    s, t = q.shape[0], k.shape[0]
    bq, bk = min(blk, s), min(blk, t)
    nq, nk = s // bq, t // bk
    tri = mode != 'full'
    bias, rope = ck is not None, qr is not None
    assert not tri or (bq == bk and bq % CHUNK == 0)
    rs = min(ATTN_ROW_SLAB, bq)
    n_in = 3 + bias + 2 * rope
    (n1, n2), step_at = _fwd_steps(tri, nq, nk)

    def body(*refs):
        refs = list(refs)
        b, p1, p2 = pl.program_id(0), pl.program_id(1), pl.program_id(2)
        st = step_at(p1, p2)
        i, j = st.qi, st.kj
        if comm is not None:
            refs, comm_refs = _carried(comm, refs, n_in, 2)
            pl.when((b == 0) & (p1 == 0) & (p2 == 0))(lambda: comm.start(*comm_refs))
        q_ref, k_ref, v_ref = refs[:3]
        pos = 3
        ck_ref = qr_ref = kr_ref = None
        if bias:
            ck_ref = refs[pos]
            pos += 1
        if rope:
            qr_ref, kr_ref = refs[pos:pos + 2]
            pos += 2
        o_ref, lse_ref, m_s, l_s, acc_s = refs[pos:]
        heads, ropes = _lane_masks(g, b, rope)

        @pl.when(st.first)
        def _():
            m_s[...] = jnp.full_like(m_s, MASK_VALUE)
            l_s[...] = jnp.zeros_like(l_s)
            acc_s[...] = jnp.zeros_like(acc_s)

        def compute(masked):
            k2, v2 = k_ref[...], v_ref[...]
            for r in range(bq // rs):
                rows = pl.ds(r * rs, rs)
                q2 = q_ref[rows, :]
                alphas, pvs = [], []
                for hh in range(g):
                    sc = _dot(_sel(heads[hh], q2), k2, NT)
                    if rope:
                        sc = sc + _dot(_sel(ropes[hh], qr_ref[rows, :]), kr_ref[...], NT)
                    if bias:
                        sc = sc - ck_ref[0, hh:hh + 1, :]
                    if masked:
                        sc = jnp.where(_mask(mode, i * bq + r * rs, j * bk, rs, bk), sc, MASK_VALUE)
                    m_prev = m_s[hh, rows]
                    m_new = jnp.maximum(m_prev, jnp.max(sc, axis=1, keepdims=True))
                    alpha = jnp.exp(m_prev - m_new)
                    p = jnp.exp(sc - m_new)
                    l_s[hh, rows] = alpha * l_s[hh, rows] + jnp.sum(p, axis=1, keepdims=True)
                    m_s[hh, rows] = m_new
                    alphas.append(alpha)
                    pvs.append(_dot(p.astype(BF16), _sel(heads[hh], v2)))
                alpha = alphas[0]
                for hh in range(1, g):
                    alpha = jnp.where(heads[hh], alphas[hh], alpha)
                acc_s[rows, :] = acc_s[rows, :] * alpha + sum(pvs[1:], pvs[0])

        if tri:
            pl.when(st.plain)(functools.partial(compute, False))
            pl.when(st.masked)(functools.partial(compute, True))
        else:
            compute(False)

        @pl.when(st.last)
        def _():
            lane = lax.broadcasted_iota(jnp.int32, (bq, LANES), 1)
            l_full, lse = l_s[0], jnp.zeros((bq, LANES), F32)
            for hh in range(g):
                if hh:
                    l_full = jnp.where(heads[hh], l_s[hh], l_full)
                lse = jnp.where(lane == hh, m_s[hh] + jnp.log(l_s[hh]), lse)
            o_ref[...] = (acc_s[...] / l_full).astype(o_ref.dtype)
            lse_ref[...] = lse

        if comm is not None:
            pl.when((b == nb - 1) & (p1 == n1 - 1) & (p2 == n2 - 1))(lambda: comm.finish(*comm_refs))

    qi = lambda p1, p2: step_at(p1, p2).qi
    kj = lambda p1, p2: step_at(p1, p2).kj
    in_specs = [pl.BlockSpec((bq, LANES), lambda b, p1, p2: (qi(p1, p2), qc + b)),
                pl.BlockSpec((bk, LANES), lambda b, p1, p2: (kj(p1, p2), kc + b)),
                pl.BlockSpec((bk, LANES), lambda b, p1, p2: (kj(p1, p2), vc + b))]
    args = [q, k, v]
    if bias:
        in_specs.append(pl.BlockSpec((1, 8, bk), lambda b, p1, p2: (b, 0, kj(p1, p2))))
        args.append(ck)
    if rope:
        in_specs += [pl.BlockSpec((bq, LANES), lambda b, p1, p2: (qi(p1, p2), qrc)),
                     pl.BlockSpec((bk, LANES), lambda b, p1, p2: (kj(p1, p2), 0))]
        args += [qr, kr]
    out = pl.BlockSpec((bq, LANES), lambda b, p1, p2: (qi(p1, p2), b))
    out_specs = [out, out]
    out_shape = [jax.ShapeDtypeStruct((s, LANES * nb), BF16), jax.ShapeDtypeStruct((s, LANES * nb), F32)]
    scratch = [pltpu.VMEM((g, bq, 1), F32), pltpu.VMEM((g, bq, 1), F32), pltpu.VMEM((bq, LANES), F32)]
    if comm is not None:
        in_specs += [ANY] * len(comm.ins)
        args += comm.ins
        out_specs += [ANY] * len(comm.out_shapes)
        out_shape += comm.out_shapes
        scratch += _sems(comm.n_sems, comm.n_sems)
    res = pl.pallas_call(body, name=name, grid=(nb, n1, n2), in_specs=in_specs, out_specs=out_specs, out_shape=out_shape,
                         scratch_shapes=scratch, compiler_params=_params('arbitrary', 'arbitrary', 'arbitrary'))(*args)
    return res if comm is None else (res[0], res[1], res[2:])


def _mattn_bwd(q, k, v, o, do, lse, *, qc, kc, vc, nb, g, mode, name, dq_scale=1.0, ck=None, qr=None, qrc=0, kr=None,
               blk=512, comm=None):
    s, t = q.shape[0], k.shape[0]
    bq, bk = min(blk, s), min(blk, t)
    nq, nk = s // bq, t // bk
    tri = mode != 'full'
    bias, rope = ck is not None, qr is not None
    rs = min(ATTN_ROW_SLAB, bq)
    n_in, n_out = 6 + bias + 2 * rope, 3 + 2 * bias + 2 * rope
    (n1, n2), step_at = _bwd_steps(tri, nq, nk)

    def body(*refs):
        refs = list(refs)
        if comm is not None:
            refs, comm_refs = _carried(comm, refs, n_in, n_out)
            first = (pl.program_id(0) == 0) & (pl.program_id(1) == 0) & (pl.program_id(2) == 0)
            pl.when(first)(lambda: comm.start(*comm_refs))
        q_ref, k_ref, v_ref, o_ref, do_ref, lse_ref = refs[:6]
        pos = 6
        ck_ref = qr_ref = kr_ref = dck_ref = dcq_ref = dqr_ref = dkr_ref = dck_s = None
        if bias:
            ck_ref = refs[pos]
            pos += 1
        if rope:
            qr_ref, kr_ref = refs[pos:pos + 2]
            pos += 2
        dq_ref, dk_ref, dv_ref = refs[pos:pos + 3]
        pos += 3
        if bias:
            dck_ref, dcq_ref = refs[pos:pos + 2]
            pos += 2
        if rope:
            dqr_ref, dkr_ref = refs[pos:pos + 2]
            pos += 2
        dk_s, dv_s = refs[pos:pos + 2]
        if bias:
            dck_s = refs[pos + 2]
        b, p1, p2 = pl.program_id(0), pl.program_id(1), pl.program_id(2)
        st = step_at(p1, p2)
        i, j = st.qi, st.kj
        heads, ropes = _lane_masks(g, b, rope)

        @pl.when((p1 == 0) & (p2 == 0))
        def _():
            dq_ref[...] = jnp.zeros_like(dq_ref)
            if bias:
                dcq_ref[...] = jnp.zeros_like(dcq_ref)

        if rope:
            @pl.when((b == 0) & (p1 == 0) & (p2 == 0))
            def _():
                dqr_ref[...] = jnp.zeros_like(dqr_ref)
                dkr_ref[...] = jnp.zeros_like(dkr_ref)

        @pl.when(st.first)
        def _():
            dk_s[...] = jnp.zeros_like(dk_s)
            dv_s[...] = jnp.zeros_like(dv_s)
            if bias:
                dck_s[...] = jnp.zeros_like(dck_s)

        def compute(masked):
            k2, v2 = k_ref[...], v_ref[...]
            lane = lax.broadcasted_iota(jnp.int32, (rs, LANES), 1)
            rk = pl.ds(pl.multiple_of(j * bk, bk), bk)
            add = lambda tot, x: x if tot is None else tot + x
            dv_t = dk_t = dkr_t = None
            dck_t = [None] * g
            for r in range(bq // rs):
                rows = pl.ds(r * rs, rs)
                rq = pl.ds(pl.multiple_of(i * bq + r * rs, rs), rs)
                q2, do2, lse2 = q_ref[rows, :], do_ref[rows, :], lse_ref[rows, :]
                dd = do2.astype(F32) * o_ref[rows, :].astype(F32)
                dq_t = dqr_t = dcq_t = None
                for hh in range(g):
                    qm = _sel(heads[hh], q2)
                    sc = _dot(qm, k2, NT)
                    if rope:
                        qrm = _sel(ropes[hh], qr_ref[rows, :])
                        sc = sc + _dot(qrm, kr_ref[...], NT)
                    if bias:
                        sc = sc - ck_ref[0, hh:hh + 1, :]
                    if masked:
                        sc = jnp.where(_mask(mode, i * bq + r * rs, j * bk, rs, bk), sc, MASK_VALUE)
                    p = jnp.exp(sc - jnp.sum(jnp.where(lane == hh, lse2, 0.0), axis=1, keepdims=True))
                    dom = _sel(heads[hh], do2)
                    dp = _dot(dom, v2, NT)
                    delta = jnp.sum(_sel(heads[hh], dd), axis=1, keepdims=True)
                    ds = p * (dp - delta)
                    dsb = ds.astype(BF16)
                    dv_t = add(dv_t, _dot(p.astype(BF16), dom, TN))
                    dk_t = add(dk_t, _dot(dsb, qm, TN))
                    dq_t = add(dq_t, _dot(dsb, _sel(heads[hh], k2)))
                    if rope:
                        dqr_t = add(dqr_t, _dot(dsb, _sel(ropes[hh], kr_ref[...])))
                        dkr_t = add(dkr_t, _dot(dsb, qrm, TN))
                    if bias:
                        dck_t[hh] = add(dck_t[hh], jnp.sum(ds, axis=0, keepdims=True))
                        dcq_t = add(dcq_t, jnp.where(lane == hh, jnp.sum(ds, axis=1, keepdims=True), 0.0))
                dq_ref[rq, :] += dq_t if dq_scale == 1.0 else dq_scale * dq_t
                if rope:
                    dqr_ref[rq, :] += dq_scale * dqr_t
                if bias:
                    dcq_ref[rq, :] += dcq_t
            dv_s[...] += dv_t
            dk_s[...] += dk_t
            if rope:
                dkr_ref[rk, :] += dkr_t
            if bias:
                for hh in range(g):
                    dck_s[hh:hh + 1, :] -= dck_t[hh]

        if tri:
            pl.when(st.plain)(functools.partial(compute, False))
            pl.when(st.masked)(functools.partial(compute, True))
        else:
            compute(False)

        @pl.when(st.last)
        def _():
            dk_ref[...] = dk_s[...]
            dv_ref[...] = dv_s[...]
            if bias:
                dck_ref[0] = dck_s[...]

        if comm is not None:
            pl.when((b == nb - 1) & (p1 == n1 - 1) & (p2 == n2 - 1))(lambda: comm.finish(*comm_refs))

    qrow = lambda col: pl.BlockSpec((bq, LANES), lambda b, p1, p2: (step_at(p1, p2).qi, col(b)))
    krow = lambda col: pl.BlockSpec((bk, LANES), lambda b, p1, p2: (step_at(p1, p2).kj, col(b)))
    in_specs = [qrow(lambda b: qc + b), krow(lambda b: kc + b), krow(lambda b: vc + b), qrow(lambda b: b),
                qrow(lambda b: b), qrow(lambda b: b)]
    args = [q, k, v, o, do, lse]
    whole = lambda rows: pl.BlockSpec((rows, LANES), lambda b, j, i: (0, b))
    out_specs = [whole(s), krow(lambda b: b), krow(lambda b: b)]
    out_shape = [jax.ShapeDtypeStruct((s, LANES * nb), F32), jax.ShapeDtypeStruct((t, LANES * nb), F32),
                 jax.ShapeDtypeStruct((t, LANES * nb), F32)]
    scratch = [pltpu.VMEM((bk, LANES), F32), pltpu.VMEM((bk, LANES), F32)]
    if bias:
        ckj = pl.BlockSpec((1, 8, bk), lambda b, p1, p2: (b, 0, step_at(p1, p2).kj))
        in_specs.append(ckj)
        args.append(ck)
        out_specs += [ckj, whole(s)]
        out_shape += [jax.ShapeDtypeStruct((nb, 8, t), F32), jax.ShapeDtypeStruct((s, LANES * nb), F32)]
    if rope:
        in_specs += [qrow(lambda b: qrc), krow(lambda b: 0)]
        args += [qr, kr]
        out_specs += [pl.BlockSpec((s, LANES), lambda b, j, i: (0, 0)), pl.BlockSpec((t, LANES), lambda b, j, i: (0, 0))]
        out_shape += [jax.ShapeDtypeStruct((s, LANES), F32), jax.ShapeDtypeStruct((t, LANES), F32)]
    if bias:
        scratch.append(pltpu.VMEM((8, bk), F32))
    if comm is not None:
        in_specs += [ANY] * len(comm.ins)
        args += comm.ins
        out_specs += [ANY] * len(comm.out_shapes)
        out_shape += comm.out_shapes
        scratch += _sems(comm.n_sems, comm.n_sems)
    res = pl.pallas_call(body, name=name, grid=(nb, n1, n2), in_specs=in_specs, out_specs=out_specs,
                         out_shape=out_shape, scratch_shapes=scratch,
                         compiler_params=_params('arbitrary', 'arbitrary', 'arbitrary'))(*args)
    return res if comm is None else (*res[:n_out], res[n_out:])


def _gla_chunk(la_c, k_c):
    r = lax.broadcasted_iota(jnp.int32, (CHUNK, CHUNK), 0)
    c = lax.broadcasted_iota(jnp.int32, (CHUNK, CHUNK), 1)
    tri = jnp.where(c <= r, 1.0, 0.0).astype(BF16)
    cum = _tri_dot(tri, la_c)
    end = jnp.sum(la_c, axis=0, keepdims=True)
    dec = jnp.exp(end - cum)
    return dec, k_c * dec, jnp.exp(end)


GLA_PAIRS = GLA_HEADS // 2


def _gla_fwd(z, la, *, qc, kc, vc, name, blk=512):
    s = z.shape[0]
    bs = min(blk, s)
    ncb = bs // CHUNK
    nblk = s // bs

    def body(q_ref, k_ref, va_ref, vb_ref, la_ref, o_ref, st_ref, st):
        @pl.when(pl.program_id(1) == 0)
        def _():
            st[...] = jnp.zeros_like(st)

        heads, _ = _lane_masks(2, 0, False)
        v_refs = (va_ref, vb_ref)
        for c in range(ncb):
            sl = pl.ds(c * CHUNK, CHUNK)
            _, kf, a = _gla_chunk(la_ref[sl, :], k_ref[sl, :])
            qs = q_ref[sl, :] * (GLA_DK ** -0.5)
            for hh in range(2):
                ut = _dot(v_refs[hh][sl, :].astype(BF16), _sel(heads[hh], kf).astype(BF16), TN)
                new = a * st[hh] + ut
                st[hh] = new
                st_ref[0, c, hh] = new
                o_ref[sl, hh * GLA_DV:(hh + 1) * GLA_DV] = _dot(_sel(heads[hh], qs).astype(BF16), new.astype(BF16), NT)

    col = lambda c0, m=1: pl.BlockSpec((bs, LANES), lambda b, i: (i, c0 + m * b))
    return pl.pallas_call(
        body, name=name, grid=(GLA_PAIRS, nblk),
        in_specs=[col(qc), col(kc), col(vc, 2), col(vc + 1, 2), col(0)],
        out_specs=[pl.BlockSpec((bs, 2 * GLA_DV), lambda b, i: (i, b)),
                   pl.BlockSpec((1, ncb, 2, GLA_DV, LANES), lambda b, i: (b, i, 0, 0, 0))],
        out_shape=[jax.ShapeDtypeStruct((s, GLA_HEADS * GLA_DV), F32),
                   jax.ShapeDtypeStruct((GLA_PAIRS, s // CHUNK, 2, GLA_DV, LANES), F32)],
        scratch_shapes=[pltpu.VMEM((2, GLA_DV, LANES), F32)],
        compiler_params=_params('arbitrary', 'arbitrary'))(z, z, z, z, la)


def _gla_bwd(z, la, st_all, st_prev, do, *, qc, kc, vc, name, blk=512):
    s = z.shape[0]
    bs = min(blk, s)
    ncb = bs // CHUNK
    nblk = s // bs

    def body(q_ref, k_ref, va_ref, vb_ref, la_ref, st_ref, sp_ref, do_ref, dq_ref, dk_ref, dv_ref, dla_ref, ga):
        @pl.when(pl.program_id(1) == 0)
        def _():
            ga[...] = jnp.zeros_like(ga)

        r = lax.broadcasted_iota(jnp.int32, (CHUNK, CHUNK), 0)
        cc = lax.broadcasted_iota(jnp.int32, (CHUNK, CHUNK), 1)
        tri_rev = jnp.where(cc >= r, 1.0, 0.0).astype(BF16)
        heads, _ = _lane_masks(2, 0, False)
        v_refs = (va_ref, vb_ref)
        for c in reversed(range(ncb)):
            sl = pl.ds(c * CHUNK, CHUNK)
            dec, kf, a = _gla_chunk(la_ref[sl, :], k_ref[sl, :])
            qs = q_ref[sl, :] * (GLA_DK ** -0.5)
            dq2 = jnp.zeros((CHUNK, LANES), F32)
            dkd = jnp.zeros((CHUNK, LANES), F32)
            da = jnp.zeros((1, LANES), F32)
            for hh in range(2):
                hv = slice(hh * GLA_DV, (hh + 1) * GLA_DV)
                dob = do_ref[sl, hv].astype(BF16)
                g = _dot(dob, _sel(heads[hh], qs).astype(BF16), TN) + ga[hh]
                gb = g.astype(BF16)
                dq2 = dq2 + _dot(dob, st_ref[0, c, hh].astype(BF16))
                dv_ref[sl, hv] = _dot(_sel(heads[hh], kf).astype(BF16), gb, NT)
                dkd = dkd + _dot(v_refs[hh][sl, :].astype(BF16), gb)
                da = da + jnp.sum(g * sp_ref[0, c, hh], axis=0, keepdims=True)
                ga[hh] = a * g
            dq_ref[sl, :] = (GLA_DK ** -0.5) * dq2
            dk_ref[sl, :] = dkd * dec
            e = dkd * kf
            dend = jnp.sum(e, axis=0, keepdims=True) + da * a
            dla_ref[sl, :] = dend - _tri_dot(tri_rev, e)

    rev = lambda i: nblk - 1 - i
    col = lambda c0, m=1: pl.BlockSpec((bs, LANES), lambda b, i: (rev(i), c0 + m * b))
    wide = pl.BlockSpec((bs, 2 * GLA_DV), lambda b, i: (rev(i), b))
    stspec = pl.BlockSpec((1, ncb, 2, GLA_DV, LANES), lambda b, i: (b, rev(i), 0, 0, 0))
    return pl.pallas_call(
        body, name=name, grid=(GLA_PAIRS, nblk),
        in_specs=[col(qc), col(kc), col(vc, 2), col(vc + 1, 2), col(0), stspec, stspec, wide],
        out_specs=[col(0), col(0), wide, col(0)],
        out_shape=[jax.ShapeDtypeStruct((s, GLA_HEADS * GLA_DK), F32), jax.ShapeDtypeStruct((s, GLA_HEADS * GLA_DK), F32),
                   jax.ShapeDtypeStruct((s, GLA_HEADS * GLA_DV), F32), jax.ShapeDtypeStruct((s, GLA_HEADS * GLA_DK), F32)],
        scratch_shapes=[pltpu.VMEM((2, GLA_DV, LANES), F32)],
        compiler_params=_params('arbitrary', 'arbitrary'))(z, z, z, z, la, st_all, st_prev, do)


def _place():
    return lax.axis_index('x'), lax.axis_index('y'), lax.axis_index('c')


ANY = pl.BlockSpec(memory_space=pl.ANY)


def _all_gather8(blk, *, name):
    m, n = blk.shape

    def body(x_ref, out_ref, send_sems, recv_sems, local_sem):
        x, y, c = _place()
        me, sibling = (x, y, c), (x, y, 1 - c)
        chips = [(1 - x, y), (x, 1 - y), (1 - x, 1 - y)]

        def slot(px, py, pc):
            return out_ref.at[4 * px + 2 * py + pc]

        def copy(q, block, to, src=None):
            return pltpu.make_async_remote_copy(
                src_ref=slot(*block) if src is None else src, dst_ref=slot(*block), send_sem=send_sems.at[q],
                recv_sem=recv_sems.at[q], device_id=to, device_id_type=MESH)

        mine = pltpu.make_async_copy(x_ref, slot(*me), local_sem)
        mine.start()
        first = [copy(0, me, sibling, src=x_ref)]
        first += [copy(1 + q, me, (*chip, c), src=x_ref) for q, chip in enumerate(chips)]
        for cp in first:
            cp.start()
        passed = [copy(4 + q, (*chip, c), sibling) for q, chip in enumerate(chips)]
        for q, chip in enumerate(chips):
            copy(1 + q, (*chip, c), me).wait_recv()
            passed[q].start()
        copy(0, sibling, me).wait_recv()
        for q, chip in enumerate(chips):
            copy(4 + q, (*chip, 1 - c), me).wait_recv()
        for cp in first + passed:
            cp.wait_send()
        mine.wait()

    return pl.pallas_call(
        body, name=name, in_specs=[ANY], out_specs=ANY, out_shape=jax.ShapeDtypeStruct((N_DEV, m, n), blk.dtype),
        scratch_shapes=[pltpu.SemaphoreType.DMA((7,)), pltpu.SemaphoreType.DMA((7,)), pltpu.SemaphoreType.DMA(())],
    )(blk)


def _sems(*counts):
    return [pltpu.SemaphoreType.DMA((n,)) for n in counts]


class Comm(typing.NamedTuple):
    ins: list
    out_shapes: list
    n_sems: int
    start: typing.Callable
    finish: typing.Callable


def _remote(src, dst, send_sems, recv_sems, idx, to):
    return lambda: pltpu.make_async_remote_copy(src_ref=src, dst_ref=dst, send_sem=send_sems.at[idx],
                                                recv_sem=recv_sems.at[idx], device_id=to, device_id_type=MESH)


def _comm_from(copies, ins, out_shapes, n_sems):
    def start(*refs):
        for cp in copies(*refs)[0]:
            cp().start()

    def finish(*refs):
        sent, received = copies(*refs)
        for cp in received:
            cp().wait_recv()
        for cp in sent:
            cp().wait_send()

    return Comm(list(ins), list(out_shapes), n_sems, start, finish)


def _run_comm(comm, *, name, alias=False):
    n_in, n_out = len(comm.ins), len(comm.out_shapes)

    def body(*refs):
        ins, outs, sems = refs[:n_in], refs[n_in:n_in + n_out], refs[n_in + n_out:]
        comm.start(ins, outs, *sems)
        comm.finish(ins, outs, *sems)

    return pl.pallas_call(body, name=name, in_specs=[ANY] * n_in, out_specs=[ANY] * n_out, out_shape=comm.out_shapes,
                          input_output_aliases={q: q for q in range(n_in)} if alias else {},
                          scratch_shapes=_sems(comm.n_sems, comm.n_sems))(*comm.ins)


def _half(rows, c):
    h = rows // 2
    return pl.ds(pl.multiple_of(c * h, h), h)


def _gathered(ref, chip, rows, side):
    if not side:
        return ref.at[chip, rows]
    n = ref.shape[1] // N_CHIPS
    return ref.at[rows, pl.ds(pl.multiple_of(chip * n, n), n)]


def _gather_over_ici(ws, side):
    def copies(ins, outs, send_sems, recv_sems):
        x, y, c = _place()
        me_chip = 2 * x + y
        sent, received = [], []
        for q, w in enumerate(ws):
            half, every = _half(w.shape[0], c), pl.ds(0, w.shape[0])
            for k, (px, py) in enumerate([(1 - x, y), (x, 1 - y), (1 - x, 1 - y)]):
                sent.append(_remote(ins[q].at[half], _gathered(outs[q], me_chip, half, side[q]), send_sems, recv_sems,
                                    4 * q + k, (px, py, c)))
                slot = _gathered(outs[q], 2 * px + py, half, side[q])
                received.append(_remote(slot, slot, send_sems, recv_sems, 4 * q + k, (px, py, c)))
            whole = _remote(ins[q], _gathered(outs[q], me_chip, every, side[q]), send_sems, recv_sems, 4 * q + 3,
                            (x, y, 1 - c))
            sent.append(whole)
            received.append(whole)
        return sent, received

    shapes = [jax.ShapeDtypeStruct((w.shape[0], N_CHIPS * w.shape[1]) if sd else (N_CHIPS,) + w.shape, w.dtype)
              for w, sd in zip(ws, side)]
    return _comm_from(copies, ws, shapes, 4 * len(ws))


def _gather_over_d2d(parts, side):
    def copies(ins, outs, send_sems, recv_sems):
        x, y, c = _place()
        sent, received = [], []
        for q, w in enumerate(parts):
            rows = w.shape[0] if side[q] else w.shape[1]
            for k, (px, py) in enumerate([(1 - x, y), (x, 1 - y), (1 - x, 1 - y)]):
                mine = _gathered(outs[q], 2 * px + py, _half(rows, c), side[q])
                theirs = _gathered(outs[q], 2 * px + py, _half(rows, 1 - c), side[q])
                sent.append(_remote(mine, mine, send_sems, recv_sems, 3 * q + k, (x, y, 1 - c)))
                received.append(_remote(theirs, theirs, send_sems, recv_sems, 3 * q + k, (x, y, 1 - c)))
        return sent, received

    return _comm_from(copies, parts, [jax.ShapeDtypeStruct(w.shape, w.dtype) for w in parts], 3 * len(parts))


def _to_sibling(gs, *, name):
    n = len(gs)

    def body(*refs):
        ins, outs = refs[:n], refs[n:2 * n]
        send_sems, recv_sems = refs[2 * n:]
        x, y, c = _place()
        cps = [pltpu.make_async_remote_copy(
            src_ref=ins[q], dst_ref=outs[q], send_sem=send_sems.at[q], recv_sem=recv_sems.at[q],
            device_id=(x, y, 1 - c), device_id_type=MESH) for q in range(n)]
        for cp in cps:
            cp.start()
        for cp in cps:
            cp.wait()

    return pl.pallas_call(body, name=name, in_specs=[ANY] * n, out_specs=[ANY] * n,
                          out_shape=[jax.ShapeDtypeStruct(g.shape, g.dtype) for g in gs],
                          scratch_shapes=_sems(n, n))(*gs)


def _chip_exchange(ps):
    def copies(ins, outs, send_sems, recv_sems):
        x, y, c = _place()
        cps = [_remote(ins[q].at[2 * px + py], outs[q].at[k], send_sems, recv_sems, 3 * q + k, (px, py, c))
               for q in range(len(ps)) for k, (px, py) in enumerate([(1 - x, y), (x, 1 - y), (1 - x, 1 - y)])]
        return cps, cps

    return _comm_from(copies, ps, [jax.ShapeDtypeStruct((3,) + p.shape[1:], p.dtype) for p in ps], 3 * len(ps))


def _sum_chips(own, r, *, name, ts=256):
    k, n = own.shape
    ts = min(ts, k)

    def body(own_ref, r_ref, o_ref):
        f = lambda q: r_ref[q].astype(F32)
        o_ref[...] = ((own_ref[...].astype(F32) + f(0)) + f(1)) + f(2)

    return pl.pallas_call(
        body, name=name, grid=(k // ts,),
        in_specs=[pl.BlockSpec((ts, n), lambda i: (i, 0)), pl.BlockSpec((3, ts, n), lambda i: (0, i, 0))],
        out_specs=pl.BlockSpec((ts, n), lambda i: (i, 0)), out_shape=jax.ShapeDtypeStruct((k, n), F32),
        compiler_params=_params('arbitrary'))(own, r)


WIN_SHARD = N_IN // N_CHIPS
WIN_PAD = -(-WIN_SHARD // LANES) * LANES
GATE_WIRE_ROWS = 32


def _full_layer(sh, axis):
    _, k, n = sh.shape
    if axis == 2:
        return sh.transpose(1, 0, 2).reshape(k, N_CHIPS * n)
    return sh.reshape(N_CHIPS * k, n)


def _win_cols(wp, o, n):
    parts = []
    while n > 0:
        j, r = divmod(o, WIN_SHARD)
        take = min(n, WIN_SHARD - r)
        parts.append(wp[:, j * WIN_PAD + r:j * WIN_PAD + r + take])
        o, n = o + take, n - take
    return parts[0] if len(parts) == 1 else jnp.concatenate(parts, axis=1)


def _split_full(full, axis):
    k, n = full.shape
    if axis == 2:
        return jnp.stack([full[:, j * (n // N_CHIPS):(j + 1) * (n // N_CHIPS)] for j in range(N_CHIPS)])
    return full.reshape(N_CHIPS, k // N_CHIPS, n)


def _padc(a, w):
    return jnp.pad(a, ((0, 0), (0, w - a.shape[1])))


def _swap16(a):
    return jnp.concatenate([a[..., 16:32], a[..., 0:16]], axis=-1)


B_GR, B_GQ, B_GK, B_GV, B_MQ, B_MKR, B_MKRS, B_FF, B_GLOW, B_MKV, B_END = (
    0, 512, 768, 1024, 1536, 1792, 1920, 2048, 2176, 2304, 2432)
B_W = 2560
O_FQ, O_FF, O_GQ, O_GLOW, O_GR, O_MQ, O_MKV, O_MKR, O_ZG = 0, 768, 772, 1796, 1812, 2324, 2580, 2708, 2740


def _repack_layer_weights(w):
    wi = functools.partial(_win_cols, w['w_in'])
    out = dict(w)
    out['in_a'] = jnp.concatenate([wi(O_FQ, 256) * FOX_SCALE, wi(O_FQ + 256, 512)], axis=1)
    kr = wi(O_MKR, 32)
    out['in_b'] = jnp.concatenate([
        wi(O_GR, 512), wi(O_GQ, 1024), wi(O_MQ, 256), jnp.tile(kr, (1, MLA_HEADS)), jnp.tile(_swap16(kr), (1, MLA_HEADS)),
        _padc(wi(O_FF, 4), 128), _padc(wi(O_GLOW, 16), 128), wi(O_MKV, 128),
        jnp.zeros((D_MODEL, B_W - B_END), kr.dtype)], axis=1)
    out['in_c'] = wi(O_ZG, 3072)
    uq = w['w_mla_uq'].reshape(MLA_Q_RANK, MLA_HEADS, MLA_NOPE + MLA_ROPE)
    rope = uq[:, :, MLA_NOPE:]
    out['uq'] = jnp.concatenate([uq[:, :, :MLA_NOPE].reshape(MLA_Q_RANK, -1), rope.reshape(MLA_Q_RANK, -1),
                                 _swap16(rope).reshape(MLA_Q_RANK, -1)], axis=1)
    ukv = w['w_mla_ukv'].reshape(MLA_KV_RANK, MLA_HEADS, MLA_NOPE + MLA_VD)
    out['ukv'] = jnp.concatenate([ukv[:, :, :MLA_NOPE].reshape(MLA_KV_RANK, -1),
                                  ukv[:, :, MLA_NOPE:].reshape(MLA_KV_RANK, -1)], axis=1)
    out['gate'] = jnp.pad(w['w_gla_gate'], ((0, 128 - GLA_RANK), (0, 0)))
    return out


def _unpack_layer_grads(g):
    a, b, c = g['in_a'], g['in_b'], g['in_c']
    a = jnp.concatenate([a[:, :256] * FOX_SCALE, a[:, 256:]], axis=1)
    fold = lambda o: sum(b[:, o + MLA_ROPE * q:o + MLA_ROPE * (q + 1)] for q in range(MLA_HEADS))
    kr = fold(B_MKR) + _swap16(fold(B_MKRS))
    w_in = jnp.concatenate([a, b[:, B_FF:B_FF + 4], b[:, B_GQ:B_GQ + 1024], b[:, B_GLOW:B_GLOW + 16],
                            b[:, B_GR:B_GR + 512], b[:, B_MQ:B_MQ + 256], b[:, B_MKV:B_MKV + 128], kr, c], axis=1)
    uq = g['uq']
    nope = uq[:, :256].reshape(MLA_Q_RANK, MLA_HEADS, MLA_NOPE)
    rope = (uq[:, 256:384].reshape(MLA_Q_RANK, MLA_HEADS, MLA_ROPE)
            + _swap16(uq[:, 384:512].reshape(MLA_Q_RANK, MLA_HEADS, MLA_ROPE)))
    w_uq = jnp.concatenate([nope, rope], axis=2).reshape(MLA_Q_RANK, -1)
    ukv = g['ukv']
    w_ukv = jnp.concatenate([ukv[:, :256].reshape(MLA_KV_RANK, MLA_HEADS, MLA_NOPE),
                             ukv[:, 256:].reshape(MLA_KV_RANK, MLA_HEADS, MLA_VD)], axis=2).reshape(MLA_KV_RANK, -1)
    out = {'w_in': w_in, 'w_mla_uq': w_uq, 'w_mla_ukv': w_ukv, 'w_gla_gate': g['gate'][:GLA_RANK]}
    for nm in ('w_up_fox', 'w_up_gla', 'w_up_mla', 'w_out', 'w_xq', 'w_xkv', 'w_xo', 'w_mlp1', 'w_mlp2'):
        out[nm] = g[nm]
    return out


def _rope_tables(s):
    half = MLA_ROPE // 2
    inv = ROPE_BASE ** (-jnp.arange(half, dtype=F32) / half)
    ang = jnp.arange(s).astype(F32)[:, None] * inv[None, :]
    cos, sin = jnp.cos(ang), jnp.sin(ang)
    c1 = jnp.concatenate([cos, cos], axis=1)
    s1 = jnp.concatenate([-sin, sin], axis=1)
    return jnp.tile(c1, (1, MLA_HEADS)), jnp.tile(s1, (1, MLA_HEADS))


def _rms_bwd(x, dh, g):
    r = lax.rsqrt(jnp.mean(x * x, axis=-1, keepdims=True) + EPS)
    xh = x * r
    gd = dh * g
    return r * (gd - xh * jnp.mean(gd * xh, axis=-1, keepdims=True)), dh * xh


def _norm_bwd_epilogue(dh, x, dres, g):
    dx, dg = _rms_bwd(x, dh, g)
    return dres + dx, dg


def _norm_bwd_call(x, dh, g, dres, name):
    w = x.width if isinstance(x, Cols) else x.shape[1]

    def with_res(xv, dv, rv, gv):
        dx, dg = _rms_bwd(xv, dv.astype(F32), gv)
        return rv + dx, dg

    def plain(xv, dv, gv):
        return _rms_bwd(xv, dv.astype(F32), gv)

    if dres is None:
        return _rowwise(plain, [x, dh], [g], [(w, F32)], [w], name=name)
    return _rowwise(with_res, [x, dh, dres], [g], [(w, F32)], [w], name=name)


def _gla_out_fwd(oraw, gr, g_out):
    outs = []
    for hh in range(GLA_HEADS):
        sl = slice(hh * GLA_DV, (hh + 1) * GLA_DV)
        oh = oraw[:, sl]
        n = oh * lax.rsqrt(jnp.mean(oh * oh, axis=-1, keepdims=True) + EPS) * g_out
        r = gr[:, sl]
        outs.append(n * (r * _sig(r)))
    return (jnp.concatenate(outs, axis=1),)


def _gla_out_bwd(oraw, gr, dout, g_out):
    d_o, d_r, dg = [], [], 0.0
    for hh in range(GLA_HEADS):
        sl = slice(hh * GLA_DV, (hh + 1) * GLA_DV)
        oh, r, do = oraw[:, sl], gr[:, sl], dout[:, sl].astype(F32)
        rs = lax.rsqrt(jnp.mean(oh * oh, axis=-1, keepdims=True) + EPS)
        sg = _sig(r)
        dn = do * (r * sg)
        d_r.append(do * (oh * rs * g_out) * (sg + r * sg * (1.0 - sg)))
        dx, dgh = _rms_bwd(oh, dn, g_out)
        d_o.append(dx)
        dg = dg + dgh
    return jnp.concatenate(d_o, axis=1), jnp.concatenate(d_r, axis=1), dg


def _adam(w, g, m, v):
    m = ADAM_B1 * m + (1.0 - ADAM_B1) * g
    v = ADAM_B2 * v + (1.0 - ADAM_B2) * (g * g)
    m_hat = m / (1.0 - ADAM_B1 ** ADAM_STEP)
    v_hat = v / (1.0 - ADAM_B2 ** ADAM_STEP)
    return -ADAM_LR * (m_hat / (jnp.sqrt(v_hat) + ADAM_EPS) + ADAM_WD * w), m, v


def _layer_fwd(x, mem, w, p, tabs, tag, carry_fox=None, after_fox=None, carry_mla=None):
    c4, s4 = tabs
    sv = {'x0': x}
    nm = lambda t: f'{t}_{tag}'
    za, h = _mm(x, w['in_a'], mode='nn', out_dtype=BF16, norm_g=p['g_mix'], emit_norm=True, name=nm('in_a'))
    zb = _mm(h, w['in_b'], mode='nn', out_dtype=F32, name=nm('in_b'))
    zc = _mm(h, w['in_c'], mode='nn', out_dtype=F32, name=nm('in_c'))
    sv.update(h=h, zc=zc)
    ff = Cols(zb, 128, B_FF // 128)
    (lf,) = _rowwise(lambda f, b: (_logsig(f + b),), [ff], [p['b_fox']], [(128, F32)], name=nm('fox_lf'))
    cum = _cumsum_rows(lf, reverse=False, name=nm('fox_cum'))
    ckf = jnp.pad(cum[:, :FOX_HEADS].T.reshape(2, 2, x.shape[0]), ((0, 0), (0, 6), (0, 0)))
    fox = dict(qc=0, kc=2, vc=4, nb=2, g=2, mode='causal', ck=ckf)
    o_fox, lse_fox, *carried = _mattn_fwd(za, za, za, name=nm('fox_attn'), comm=carry_fox, **fox)
    if after_fox is not None:
        w = {**w, **after_fox(carried[0])}
    sv.update(ff=ff, za=za, fox=fox, o_fox=o_fox, lse_fox=lse_fox)
    glow = Cols(zb, 128, B_GLOW // 128)
    gr = Cols(zb, 512, B_GR // 512)

    def gate_fn(gl, wg, bg):
        return (_logsig(_dot(gl.astype(BF16), wg) + bg) / GLA_TAU,)

    (la,) = _rowwise(gate_fn, [glow], [w['gate'], p['b_gla']], [(256, F32)], name=nm('gla_gate'))
    gla = dict(qc=B_GQ // LANES, kc=B_GK // LANES, vc=B_GV // LANES)
    oraw, states = _gla_fwd(zb, la, name=nm('gla'), **gla)
    (o_gla,) = _rowwise(_gla_out_fwd, [oraw, gr], [p['g_gla_out']], [(512, BF16)], name=nm('gla_out'))
    sv.update(glow=glow, gr=gr, zb=zb, la=la, gla=gla, states=states, oraw=oraw, o_gla=o_gla)
    mq = Cols(zb, 256, B_MQ // 256)
    mkv = Cols(zb, 128, B_MKV // 128)
    mkr2 = Cols(zb, 256, B_MKR // 256)
    qp, cqn = _mm(mq, w['uq'], mode='nn', out_dtype=F32, norm_g=p['g_mla_q'], emit_norm=True, name=nm('mla_uq'))
    kvp, ckvn = _mm(mkv, w['ukv'], mode='nn', out_dtype=BF16, norm_g=p['g_mla_kv'], emit_norm=True,
                    name=nm('mla_ukv'))

    def rope_fn(qv, kr, c4v, s4v):
        q_rope = qv[:, 256:384] * c4v + qv[:, 384:512] * s4v
        q_scaled = jnp.concatenate([qv[:, 0:256], q_rope], axis=1) * MLA_SCALE
        return q_scaled, kr[:, 0:128] * c4v + kr[:, 128:256] * s4v

    qall, kr4 = _rowwise(rope_fn, [qp, mkr2, c4, s4], [], [(384, BF16), (128, BF16)], name=nm('rope'))
    mla = dict(qc=0, kc=0, vc=2, nb=2, g=2, dq_scale=MLA_SCALE, mode='chunk', qr=qall, qrc=2, kr=kr4)
    o_mla, lse_mla, *carried = _mattn_fwd(qall, kvp, kvp, name=nm('mla_attn'), comm=carry_mla, **mla)
    if carry_mla is not None:
        sv['carried_mla'] = carried[0]
    sv.update(mq=mq, mkv=mkv, cqn=cqn, ckvn=ckvn, qall=qall, kvp=kvp, mla=mla, o_mla=o_mla, lse_mla=lse_mla)
    of_m, om_m = o_fox, o_mla
    sv.update(of_m=of_m, om_m=om_m)
    b_br = p['b_branch']

    y = _gated_merge([of_m, o_gla, om_m], [w['w_up_fox'], w['w_up_gla'], w['w_up_mla']], zc, b_br, name=nm('up_merge'))
    add = lambda acc, res: res + acc
    x1 = _mm(y, w['w_out'], mode='nn', out_dtype=F32, name=nm('out'), epilogue=add, extras=[(x, *_mn())])
    sv.update(y=y, x1=x1)
    qx, hx = _mm(x1, w['w_xq'], mode='nn', out_dtype=BF16, norm_g=p['g_xa'], emit_norm=True, name=nm('xq'),
                 epilogue=lambda acc: acc * XA_SCALE)
    kvx, mn = _mm(mem, w['w_xkv'], mode='nn', out_dtype=BF16, norm_g=p['g_mem'], emit_norm=True, name=nm('xkv'))
    xa = dict(qc=0, kc=0, vc=4, nb=4, g=1, dq_scale=XA_SCALE, mode='full')
    ox_m, lse_x = _mattn_fwd(qx, kvx, kvx, name=nm('xa_attn'), **xa)
    x2 = _mm(ox_m, w['w_xo'], mode='nn', out_dtype=F32, name=nm('xo'), epilogue=add, extras=[(x1, *_mn())])
    sv.update(hx=hx, mn=mn, qx=qx, kvx=kvx, xa=xa, lse_x=lse_x, ox_m=ox_m, x2=x2)
    hpre, hm = _mm(x2, w['w_mlp1'], mode='nn', out_dtype=BF16, norm_g=p['g_mlp'], emit_norm=True, name=nm('mlp1'))
    relu2 = lambda t: jnp.square(jnp.maximum(t.astype(F32), 0.0))
    x3 = _mm(hpre, w['w_mlp2'], mode='nn', out_dtype=F32, name=nm('mlp2'), a_fn=relu2, epilogue=add,
             extras=[(x2, *_mn())])
    sv.update(hpre=hpre, hm=hm, w=w)
    return x3, sv


EARLY = ('w_mlp1', 'w_mlp2', 'w_xo', 'w_xq', 'w_xkv', 'w_out', 'w_up_fox', 'w_up_gla', 'w_up_mla')
LATE = ('w_in', 'w_gla_gate', 'w_mla_uq', 'w_mla_ukv')


def _layer_bwd(dx3, mem, w, p, tabs, sv, tag, carry_mla=None, early=None):
    c4, s4 = tabs
    nm = lambda t: f'{t}_{tag}'
    s = dx3.shape[0]
    gw, gs = {}, {}
    relu2 = lambda t: jnp.square(jnp.maximum(t.astype(F32), 0.0))
    gw['w_mlp2'] = _mm(sv['hpre'], dx3, mode='tn', out_dtype=F32, name=nm('d_mlp2'), a_fn=relu2)
    dact = lambda acc, hp: acc * (2.0 * jnp.maximum(hp.astype(F32), 0.0))
    dhpre = _mm(dx3, w['w_mlp2'], mode='nt', out_dtype=BF16, name=nm('d_act'), epilogue=dact,
                extras=[(sv['hpre'], *_mn())])
    gw['w_mlp1'] = _mm(sv['hm'], dhpre, mode='tn', out_dtype=F32, name=nm('d_mlp1'))
    dx2, gs['g_mlp'] = _mm(dhpre, w['w_mlp1'], mode='nt', out_dtype=F32, name=nm('d_hm'), epilogue=_norm_bwd_epilogue,
                           col_sums=True, full_rows=True,
                           extras=[(sv['x2'], *_mn()), (dx3, *_mn()), (p['g_mlp'], *_nvec())])
    gw['w_xo'] = _mm(sv['ox_m'], dx2, mode='tn', out_dtype=F32, name=nm('d_xo'))
    dox = _mm(dx2, w['w_xo'], mode='nt', out_dtype=BF16, name=nm('d_ox'))
    dqx_m, dkx, dvx = _mattn_bwd(sv['qx'], sv['kvx'], sv['kvx'], sv['ox_m'], dox, sv['lse_x'], name=nm('xa_bwd'),
                                 **sv['xa'])
    dkvx = jnp.concatenate([dkx, dvx], axis=1).astype(BF16)
    gw['w_xq'] = _mm(sv['hx'], dqx_m, mode='tn', out_dtype=F32, name=nm('d_xq'))
    dx1, gs['g_xa'] = _mm(dqx_m, w['w_xq'], mode='nt', out_dtype=F32, name=nm('d_hx'), epilogue=_norm_bwd_epilogue,
                          col_sums=True, full_rows=True,
                          extras=[(sv['x1'], *_mn()), (dx2, *_mn()), (p['g_xa'], *_nvec())])
    gw['w_xkv'] = _mm(sv['mn'], dkvx, mode='tn', out_dtype=F32, name=nm('d_xkv'))
    dmn = _mm(dkvx, w['w_xkv'], mode='nt', out_dtype=F32, name=nm('d_mn'))
    _, gs['g_mem'] = _norm_bwd_call(mem, dmn, p['g_mem'], None, nm('d_norm_mem'))
    gw['w_out'] = _mm(sv['y'], dx1, mode='tn', out_dtype=F32, name=nm('d_out'))
    dy = _mm(dx1, w['w_out'], mode='nt', out_dtype=BF16, name=nm('d_y'))
    zc, b_br = sv['zc'], p['b_branch']

    branches = (('w_up_fox', sv['of_m'], BF16), ('w_up_gla', sv['o_gla'], F32), ('w_up_mla', sv['om_m'], BF16))
    du, do_br, dzc, gs['b_branch'] = _gated_merge_bwd(dy, zc, b_br, [o for _, o, _ in branches],
                                                      [w[wn] for wn, _, _ in branches], [dt for _, _, dt in branches],
                                                      name=nm('d_merge'))
    for q, (wn, o_m, _) in enumerate(branches):
        gw[wn] = _mm(o_m, du[q], mode='tn', out_dtype=F32, name=nm(f'd_up{q}'))
    za = sv['za']
    carry_fox = None if early is None else early({nm_: gw[nm_] for nm_ in EARLY})
    dfq, dfk, dfv, dck, dcq, *carried_fox = _mattn_bwd(za, za, za, sv['o_fox'], do_br[0], sv['lse_fox'],
                                                       name=nm('fox_bwd'), comm=carry_fox, **sv['fox'])
    dcum = _padc(dck[:, :2, :].reshape(FOX_HEADS, s).T + dcq.reshape(s, 2, LANES)[:, :, :2].reshape(s, FOX_HEADS), 128)
    dlf = _cumsum_rows(dcum, reverse=True, name=nm('fox_dcum'))

    def dff_fn(dl, f, b):
        d = dl * _sig(-(f + b))
        return d, d

    dff, db_fox = _rowwise(dff_fn, [dlf, sv['ff']], [p['b_fox']], [(128, F32)], [128], name=nm('fox_dff'))
    gs['b_fox'] = db_fox
    dza = jnp.concatenate([dfq, dfk, dfv], axis=1).astype(BF16)
    dqn, dkn, dvv, dq_rope, dk_rope, *carried_mla = _mattn_bwd(sv['qall'], sv['kvp'], sv['kvp'], sv['o_mla'], do_br[2],
                                                               sv['lse_mla'], name=nm('mla_bwd'), comm=carry_mla,
                                                               **sv['mla'])

    def drope_fn(dn, dq, dk, c4v, s4v):
        return jnp.concatenate([dn, dq * c4v, dq * s4v], axis=1), jnp.concatenate([dk * c4v, dk * s4v], axis=1)

    dqp, dmkr2 = _rowwise(drope_fn, [dqn, dq_rope, dk_rope, c4, s4], [], [(512, BF16), (256, BF16)], name=nm('d_rope'))
    dkvp = jnp.concatenate([dkn, dvv], axis=1).astype(BF16)
    gw['uq'] = _mm(sv['cqn'], dqp, mode='tn', out_dtype=F32, name=nm('d_uq'))
    dcqn = _mm(dqp, w['uq'], mode='nt', out_dtype=F32, name=nm('d_cqn'))
    gw['ukv'] = _mm(sv['ckvn'], dkvp, mode='tn', out_dtype=F32, name=nm('d_ukv'))
    dckvn = _mm(dkvp, w['ukv'], mode='nt', out_dtype=F32, name=nm('d_ckvn'))
    dmq, gs['g_mla_q'] = _norm_bwd_call(sv['mq'], dcqn, p['g_mla_q'], None, nm('d_norm_q'))
    dmkv, gs['g_mla_kv'] = _norm_bwd_call(sv['mkv'], dckvn, p['g_mla_kv'], None, nm('d_norm_kv'))
    doraw, dgr, gs['g_gla_out'] = _rowwise(_gla_out_bwd, [sv['oraw'], sv['gr'], do_br[1]], [p['g_gla_out']],
                                           [(512, F32), (512, BF16)], [128], name=nm('d_gla_out'))
    st = sv['states']
    st_prev = jnp.concatenate([jnp.zeros_like(st[:, :1]), st[:, :-1]], axis=1)
    dgq, dgk, dgv, dla = _gla_bwd(sv['zb'], sv['la'], st, st_prev, doraw, name=nm('gla_bwd'), **sv['gla'])

    def dgate_fn(dl, gl, wg, bg):
        pre = _dot(gl.astype(BF16), wg) + bg
        dpre = dl * (1.0 / GLA_TAU) * _sig(-pre)
        return dpre, _dot(dpre.astype(BF16), wg, NT), dpre

    dpre, dglow, gs['b_gla'] = _rowwise(dgate_fn, [dla, sv['glow']], [w['gate'], p['b_gla']],
                                        [(256, BF16), (128, BF16)], [256], name=nm('d_gla_gate'))
    gw['gate'] = _mm(sv['glow'], dpre, mode='tn', out_dtype=F32, name=nm('d_wgate'))
    bf = lambda t: t.astype(BF16)
    dzb = jnp.concatenate([dgr, bf(dgq), bf(dgk), bf(dgv), bf(dmq), dmkr2, bf(dff), dglow, bf(dmkv),
                           jnp.zeros((s, B_W - B_END), BF16)], axis=1)
    h = sv['h']
    gw['in_a'] = _mm(h, dza, mode='tn', out_dtype=F32, name=nm('d_in_a'))
    gw['in_b'] = _mm(h, dzb, mode='tn', out_dtype=F32, name=nm('d_in_b'))
    gw['in_c'] = _mm(h, dzc, mode='tn', out_dtype=F32, name=nm('d_in_c'))
    add = lambda acc, prev: prev + acc
    dh = _mm(dza, w['in_a'], mode='nt', out_dtype=F32, name=nm('d_h_a'))
    dh = _mm(dzb, w['in_b'], mode='nt', out_dtype=F32, name=nm('d_h_b'), epilogue=add, extras=[(dh, *_mn())])
    dx0, gs['g_mix'] = _mm(dzc, w['in_c'], mode='nt', out_dtype=F32, name=nm('d_h_c'), col_sums=True, full_rows=True,
                           epilogue=lambda acc, prev, xv, rv, gv: _norm_bwd_epilogue(prev + acc, xv, rv, gv),
                           extras=[(dh, *_mn()), (sv['x0'], *_mn()), (dx1, *_mn()), (p['g_mix'], *_nvec())])
    return dx0, gw, gs, (carried_mla or [None])[0], (carried_fox or [None])[0]


def _loss_head(x, target, g_final):
    d = x.shape[1]

    def fn(xv, tv, gv):
        r = lax.rsqrt(jnp.mean(xv * xv, axis=-1, keepdims=True) + EPS)
        xh = xv * r
        e = xh * gv - tv
        dy = e * (1.0 / d)
        gd = dy * gv
        dx = r * (gd - xh * jnp.mean(gd * xh, axis=-1, keepdims=True))
        row_loss = 0.5 * jnp.mean(e * e, axis=-1, keepdims=True)
        return dx, dy * xh, jnp.broadcast_to(row_loss, (xv.shape[0], LANES))

    return _rowwise(fn, [x, target], [g_final], [(d, F32)], [d, LANES], name='loss_head')


def _small_sizes(shapes):
    return [math.prod(shapes[nm]) for nm in SMALL]


def _step(args):
    shapes = {nm: args[nm].shape for nm in ORDER}
    x, mem, target = args['x'][0], args['mem'][0], args['loss_target'][0]
    s = x.shape[0]

    def wire(nm, l):
        w = args[nm][l].astype(BF16)
        if nm == 'w_in':
            w = jnp.pad(w, ((0, 0), (0, WIN_PAD - WIN_SHARD)))
        if nm == 'w_gla_gate':
            w = jnp.pad(w, ((0, GATE_WIRE_ROWS - GLA_RANK), (0, 0)))
        return w

    axis_of = dict(BIG)
    names = tuple(nm for nm, _ in BIG)
    wires = lambda l, nms: [wire(nm, l) for nm in nms]
    width = lambda nm: WIN_PAD if nm == 'w_in' else args[nm].shape[2]
    side_by_side = lambda nms: [axis_of[nm] == 2 and width(nm) % LANES == 0 for nm in nms]
    over_ici = lambda l, nms: _gather_over_ici(wires(l, nms), side_by_side(nms))

    def whole(parts, nms, tag):
        side = side_by_side(nms)
        parts = _run_comm(_gather_over_d2d(parts, side), name=f'gather_d2d_{tag}', alias=True)
        full = {nm: p if sd else _full_layer(p, axis_of[nm]) for nm, p, sd in zip(nms, parts, side)}
        if 'w_gla_gate' in full:
            full['w_gla_gate'] = full['w_gla_gate'][:GLA_RANK]
        return full

    tabs = _rope_tables(s)
    layers_p = []
    for l in range(DEPTH):
        layers_p.append({
            'g_mix': args['g_mix'][l][None], 'b_fox': _padc(args['b_fox_forget'][l][None], 128),
            'b_gla': args['b_gla_gate'][l][None], 'g_gla_out': args['g_gla_out'][l][None],
            'g_mla_q': args['g_mla_q'][l][None], 'g_mla_kv': args['g_mla_kv'][l][None],
            'b_branch': args['b_branch_gate'][l][None], 'g_xa': args['g_xa'][l][None],
            'g_mem': args['g_mem'][l][None], 'g_mlp': args['g_mlp'][l][None]})

    first = _run_comm(over_ici(0, LATE), name='gather_ici_first_l0')
    w_now = _repack_layer_weights(whole(first, LATE, 'first_l0'))
    saved = []
    xl = x
    for l in range(DEPTH):
        carry_fox = over_ici(0, EARLY) if l == 0 else None
        after_fox = (lambda parts: whole(parts, EARLY, 'rest_l0')) if l == 0 else None
        carry_mla = over_ici(l + 1, names) if l + 1 < DEPTH else None
        xl, sv = _layer_fwd(xl, mem, w_now, layers_p[l], tabs, f'l{l}', carry_fox=carry_fox, after_fox=after_fox,
                            carry_mla=carry_mla)
        saved.append(sv)
        if carry_mla is not None:
            w_now = _repack_layer_weights(whole(sv.pop('carried_mla'), names, f'l{l + 1}'))
    dx, dg_final, loss_lanes = _loss_head(xl, target, args['g_final'][None])
    cidx = lax.axis_index('c')
    chip = 2 * lax.axis_index('x') + lax.axis_index('y')

    def pair_sums(gw, nms, tag):
        mine, theirs = [], []
        for nm in nms:
            shards = _split_full(gw[nm], axis_of[nm]).astype(BF16)
            h = shards.shape[1] // 2
            mine.append(lax.dynamic_slice_in_dim(shards, cidx * h, h, axis=1))
            theirs.append(lax.dynamic_slice_in_dim(shards, (1 - cidx) * h, h, axis=1))
        got = _to_sibling(theirs, name=f'grads_swap_{tag}')
        pairs = []
        for nm, a, b in zip(nms, mine, got):
            _, h, n = a.shape
            (p,) = _rowwise(lambda u, v: (u.astype(F32) + v.astype(F32),),
                            [a.reshape(N_CHIPS * h, n), b.reshape(N_CHIPS * h, n)], [], [(n, BF16)],
                            name=f'pair_sum_{nm}_{tag}')
            pairs.append(p.reshape(N_CHIPS, h, n))
        return pairs

    def finish(pairs, from_chips, nms, tag):
        own = [lax.dynamic_index_in_dim(p, chip, axis=0, keepdims=False) for p in pairs]
        mine = [_sum_chips(o, r, name=f'chip_sum_{nm}_{tag}') for nm, o, r in zip(nms, own, from_chips)]
        theirs = _to_sibling(mine, name=f'grads_join_{tag}')
        return {nm: jnp.where(cidx == 0, jnp.concatenate([a, b]), jnp.concatenate([b, a]))
                for nm, a, b in zip(nms, mine, theirs)}

    gs_layers, done = [None] * DEPTH, [{} for _ in range(DEPTH)]
    above = None
    for l in reversed(range(DEPTH)):
        lowest, early_pairs = l == 0, []

        def early(gw_early, l=l, early_pairs=early_pairs):
            early_pairs.extend(pair_sums(gw_early, EARLY, f'early_l{l}'))
            return _chip_exchange(early_pairs)

        carry_mla = None if above is None else _chip_exchange(above[1])
        dx, gw, gs_layers[l], got_mla, got_fox = _layer_bwd(
            dx, mem, saved[l]['w'], layers_p[l], tabs, saved[l], f'l{l}', carry_mla=carry_mla,
            early=early if lowest else None)
        if above is not None:
            done[above[0]].update(finish(above[1], got_mla, names, f'l{above[0]}'))
        grads = _unpack_layer_grads(gw)
        if lowest:
            done[l].update(finish(early_pairs, got_fox, EARLY, f'early_l{l}'))
            late_pairs = pair_sums(grads, LATE, f'late_l{l}')
            from_late = _run_comm(_chip_exchange(late_pairs), name=f'grads_exchange_late_l{l}')
            done[l].update(finish(late_pairs, from_late, LATE, f'late_l{l}'))
        else:
            above = (l, pair_sums(grads, names, f'l{l}'))
    grad_x = dx[None]
    gshard = {nm: jnp.stack([done[l][nm] for l in range(DEPTH)]) for nm in names}

    small_g = []
    for nm, key in (('g_mix', 'g_mix'), ('b_fox_forget', 'b_fox'), ('b_gla_gate', 'b_gla'),
                    ('g_gla_out', 'g_gla_out'), ('g_mla_q', 'g_mla_q'), ('g_mla_kv', 'g_mla_kv'),
                    ('b_branch_gate', 'b_branch'), ('g_xa', 'g_xa'), ('g_mem', 'g_mem'), ('g_mlp', 'g_mlp')):
        width = shapes[nm][1]
        small_g.append(jnp.concatenate([gs_layers[l][key][0, :width] for l in range(DEPTH)]))
    small_g.append(dg_final[0])
    small_g.append(loss_lanes[0, :1])
    flat = jnp.concatenate(small_g)
    n_small = flat.shape[0]
    srows = -(-n_small // (8 * LANES)) * 8
    pad = lambda v: jnp.pad(v, (0, srows * LANES - v.shape[0])).reshape(srows, LANES)
    all_small = _all_gather8(pad(flat), name='gather_small')
    sw, sm, svv = (pad(jnp.concatenate([args[pre + nm].reshape(-1) for nm in SMALL] + [jnp.zeros((1,), F32)]))
                   for pre in ('', 'm_', 'v_'))

    def small_body(g_ref, w_ref, m_ref, v_ref, go_ref, d_ref, mo_ref, vo_ref):
        g = g_ref[0]
        for q in range(1, N_DEV):
            g = g + g_ref[q]
        go_ref[...] = g
        d_ref[...], mo_ref[...], vo_ref[...] = _adam(w_ref[...], g, m_ref[...], v_ref[...])

    sg, sd, snm, snv = pl.pallas_call(
        small_body, name='small_sum_adam', out_shape=[jax.ShapeDtypeStruct((srows, LANES), F32)] * 4,
        compiler_params=pltpu.CompilerParams(vmem_limit_bytes=VMEM_LIMIT))(all_small, sw, sm, svv)

    def unsmall(buf):
        v, out, off = buf.reshape(-1), {}, 0
        for nm in SMALL:
            nel = math.prod(shapes[nm])
            out[nm] = v[off:off + nel].reshape(shapes[nm])
            off += nel
        return out, v[off]

    res = {}
    (res['grad'], loss), (res['delta'], _), (res['m'], _), (res['v'], _) = (unsmall(t) for t in (sg, sd, snm, snv))

    for nm, _ in BIG:
        shp = args[nm].shape
        view = lambda t: t.reshape(shp[0] * shp[1], shp[2])
        d, m2, v2 = _rowwise(_adam, [view(args[nm]), view(gshard[nm]), view(args['m_' + nm]), view(args['v_' + nm])],
                             [], [(shp[2], F32)] * 3, name=f'adam_{nm}')
        res['grad'][nm], res['delta'][nm], res['m'][nm], res['v'][nm] = (
            gshard[nm], d.reshape(shp), m2.reshape(shp), v2.reshape(shp))

    return (loss, grad_x, *[res['grad'][nm] for nm in ORDER], *[res['delta'][nm] for nm in ORDER],
            *[res['m'][nm] for nm in ORDER], *[res['v'][nm] for nm in ORDER])


def kernel(x, mem, g_mix, w_in, b_fox_forget, w_gla_gate, b_gla_gate, g_gla_out, g_mla_q, w_mla_uq, g_mla_kv, w_mla_ukv, b_branch_gate, w_up_fox, w_up_gla, w_up_mla, w_out, g_xa, g_mem, w_xq, w_xkv, w_xo, g_mlp, w_mlp1, w_mlp2, g_final, loss_target, m_g_mix, m_w_in, m_b_fox_forget, m_w_gla_gate, m_b_gla_gate, m_g_gla_out, m_g_mla_q, m_w_mla_uq, m_g_mla_kv, m_w_mla_ukv, m_b_branch_gate, m_w_up_fox, m_w_up_gla, m_w_up_mla, m_w_out, m_g_xa, m_g_mem, m_w_xq, m_w_xkv, m_w_xo, m_g_mlp, m_w_mlp1, m_w_mlp2, m_g_final, v_g_mix, v_w_in, v_b_fox_forget, v_w_gla_gate, v_b_gla_gate, v_g_gla_out, v_g_mla_q, v_w_mla_uq, v_g_mla_kv, v_w_mla_ukv, v_b_branch_gate, v_w_up_fox, v_w_up_gla, v_w_up_mla, v_w_out, v_g_xa, v_g_mem, v_w_xq, v_w_xkv, v_w_xo, v_g_mlp, v_w_mlp1, v_w_mlp2, v_g_final):
    return _step(dict(locals()))
```

```python
import functools
import math
import typing

import jax
import jax.numpy as jnp
from jax import lax
from jax.experimental import pallas as pl
from jax.experimental.pallas import tpu as pltpu

F32 = jnp.float32
BF16 = jnp.bfloat16
MESH = pl.DeviceIdType.MESH

D_MODEL = 1024
DEPTH = 2
CHUNK = 64
EPS = 1e-6
FOX_HEADS, FOX_HD = 4, 64
GLA_HEADS, GLA_DK, GLA_DV, GLA_RANK, GLA_TAU = 4, 64, 128, 16, 16.0
MLA_HEADS, MLA_Q_RANK, MLA_KV_RANK, MLA_NOPE, MLA_ROPE, MLA_VD = 4, 256, 128, 64, 32, 64
ROPE_BASE = 10000.0
XA_HEADS, XA_HD = 4, 128
D_FF = 4 * D_MODEL
IN_SIZES = (256, 256, 256, 4, 256, 256, 512, 16, 512, 256, 128, 32, 3072)
N_IN = sum(IN_SIZES)

ADAM_LR, ADAM_B1, ADAM_B2, ADAM_EPS, ADAM_WD, ADAM_STEP = 0.001, 0.9, 0.999, 1e-08, 0.01, 10

N_CHIPS = 4
N_DEV = 8
LANES = 128
VMEM_LIMIT = 48 * 1024 * 1024
MASK_VALUE = -1e30

BIG = (('w_in', 2), ('w_gla_gate', 2), ('w_mla_uq', 2), ('w_mla_ukv', 2), ('w_up_fox', 2), ('w_up_gla', 2),
       ('w_up_mla', 2), ('w_out', 1), ('w_xq', 1), ('w_xkv', 1), ('w_xo', 2), ('w_mlp1', 2), ('w_mlp2', 1))
SMALL = ('g_mix', 'b_fox_forget', 'b_gla_gate', 'g_gla_out', 'g_mla_q', 'g_mla_kv', 'b_branch_gate',
         'g_xa', 'g_mem', 'g_mlp', 'g_final')
ORDER = ('g_mix', 'w_in', 'b_fox_forget', 'w_gla_gate', 'b_gla_gate', 'g_gla_out', 'g_mla_q', 'w_mla_uq',
         'g_mla_kv', 'w_mla_ukv', 'b_branch_gate', 'w_up_fox', 'w_up_gla', 'w_up_mla', 'w_out', 'g_xa', 'g_mem',
         'w_xq', 'w_xkv', 'w_xo', 'g_mlp', 'w_mlp1', 'w_mlp2', 'g_final')


def _params(*sem):
    return pltpu.CompilerParams(dimension_semantics=sem, vmem_limit_bytes=VMEM_LIMIT)


def _sig(x):
    return 1.0 / (1.0 + jnp.exp(-x))


def _logsig(x):
    return jnp.minimum(x, 0.0) - jnp.log(1.0 + jnp.exp(-jnp.abs(x)))


NN = (((1,), (0,)), ((), ()))
NT = (((1,), (1,)), ((), ()))
TN = (((0,), (0,)), ((), ()))


def _dot(a, b, dims=NN):
    return lax.dot_general(a, b, dims, preferred_element_type=F32)


class Cols(typing.NamedTuple):
    arr: jax.Array
    width: int
    blk: int


def _tri_dot(tri, x):
    hi = x.astype(BF16)
    r1 = x - hi.astype(F32)
    mid = r1.astype(BF16)
    lo = (r1 - mid.astype(F32)).astype(BF16)
    return _dot(tri, hi) + _dot(tri, mid) + _dot(tri, lo)


MM_TILES = ((1024, 1024), (1024, 512), (512, 1024), (512, 512), (256, 1024), (512, 256), (256, 512), (256, 256),
            (128, 1024), (128, 128))
MM_VMEM_BUDGET = 38 * 1024 * 1024


def _mm_tiles(m, n, k, a_bytes, b_bytes, out_bytes, ex_bytes, has_norm, emit_norm, has_fn, full_rows):
    for tm, tn in MM_TILES:
        tm, tn = min(tm, m), min(tn, n)
        if m % tm or n % tn or (full_rows and tn != n):
            continue
        blocks = tm * k * a_bytes + k * tn * b_bytes + tm * tn * (out_bytes + ex_bytes) + (tm * k * 2 if emit_norm else 0)
        temps = tm * tn * 4 + (tm * k * 2 if has_norm else 0) + (tm * k * 6 if has_fn or has_norm else 0)
        if 2 * blocks + temps <= MM_VMEM_BUDGET:
            return tm, tn
    raise ValueError((m, n, k))


def _mm(a, b, *, mode, out_dtype, name, norm_g=None, emit_norm=False, a_fn=None, extras=(), epilogue=None,
        col_sums=False, full_rows=False):
    a_blk = 0
    if isinstance(a, Cols):
        a, width, a_blk = a
        a_shape = (a.shape[0], width)
    else:
        a_shape = a.shape
    if mode == 'tn':
        k, m = a_shape
    else:
        m, k = a_shape
    n = b.shape[0] if mode == 'nt' else b.shape[1]
    assert (b.shape[1] if mode == 'nt' else b.shape[0]) == k, (name, a.shape, b.shape)
    has_norm = norm_g is not None
    ex_bytes = sum(arr.dtype.itemsize for arr, kind, _ in extras if kind == 'mn')
    tm, tn = _mm_tiles(m, n, k, a.dtype.itemsize, b.dtype.itemsize, jnp.dtype(out_dtype).itemsize, ex_bytes, has_norm,
                       emit_norm, a_fn is not None, full_rows)
    assert all(col % tn == 0 for _, _, col in extras), (name, tn)
    assert a_blk == 0 or (mode == 'nn') or (mode == 'tn' and tm == m)
    assert not (col_sums and (has_norm or emit_norm))
    ij = (lambda f: lambda g0, g1: f(g1, g0)) if col_sums else (lambda f: f)
    spec = lambda blk, f: pl.BlockSpec(blk, ij(f))
    if mode == 'tn':
        a_spec = spec((k, tm), lambda i, j: (0, i + a_blk))
    else:
        a_spec = spec((tm, k), lambda i, j: (i, a_blk))
    b_spec = spec((tn, k), lambda i, j: (j, 0)) if mode == 'nt' else spec((k, tn), lambda i, j: (0, j))
    dims = {'nn': NN, 'nt': NT, 'tn': TN}[mode]
    assert not (has_norm and mode != 'nn')
    n_ex = len(extras)

    def body(*refs):
        a_ref, b_ref = refs[0], refs[1]
        pos = 2
        g_ref = None
        if has_norm:
            g_ref = refs[pos]
            pos += 1
        ex_refs = refs[pos:pos + n_ex]
        pos += n_ex
        o_ref = refs[pos]
        pos += 1
        h_ref = None
        if emit_norm:
            h_ref = refs[pos]
            pos += 1
        if has_norm:
            an_ref = refs[pos]

            @pl.when(pl.program_id(1) == 0)
            def _():
                xf = a_ref[...].astype(F32)
                y = xf * lax.rsqrt(jnp.mean(xf * xf, axis=-1, keepdims=True) + EPS) * g_ref[...]
                an_ref[...] = y.astype(BF16)
                if emit_norm:
                    h_ref[...] = y.astype(BF16)

            av = an_ref[...]
        else:
            av = a_ref[...]
            if a_fn is not None:
                av = a_fn(av)
            av = av.astype(BF16)
        acc = _dot(av, b_ref[...].astype(BF16), dims)
        if epilogue is not None:
            acc = epilogue(acc, *[r[...] for r in ex_refs])
        acc, to_sum = acc if isinstance(acc, tuple) else (acc, acc)
        o_ref[...] = acc.astype(out_dtype)
        if col_sums:
            sum_ref = refs[pos]

            @pl.when(pl.program_id(1) == 0)
            def _():
                sum_ref[...] = jnp.zeros_like(sum_ref)

            sum_ref[...] += jnp.sum(to_sum, axis=0, keepdims=True)

    in_specs = [a_spec, b_spec]
    args = [a, b]
    if has_norm:
        in_specs.append(pl.BlockSpec((1, k), lambda i, j: (0, 0)))
        args.append(norm_g)
    for arr, kind, col in extras:
        if kind == 'mn':
            in_specs.append(spec((tm, tn), lambda i, j, o=col // tn: (i, j + o)))
        else:
            in_specs.append(spec((1, tn), lambda i, j, o=col // tn: (0, j + o)))
        args.append(arr)
    out_shape = [jax.ShapeDtypeStruct((m, n), out_dtype)]
    out_specs = [spec((tm, tn), lambda i, j: (i, j))]
    if emit_norm:
        out_shape.append(jax.ShapeDtypeStruct((m, k), BF16))
        out_specs.append(pl.BlockSpec((tm, k), lambda i, j: (i, 0)))
    if col_sums:
        out_shape.append(jax.ShapeDtypeStruct((1, n), F32))
        out_specs.append(spec((1, tn), lambda i, j: (0, j)))
    scratch = [pltpu.VMEM((tm, k), BF16)] if has_norm else []
    grid = (n // tn, m // tm) if col_sums else (m // tm, n // tn)
    res = pl.pallas_call(
        body, name=name, grid=grid, in_specs=in_specs, out_specs=out_specs, out_shape=out_shape,
        scratch_shapes=scratch, compiler_params=_params('arbitrary', 'arbitrary'))(*args)
    return res if emit_norm or col_sums else res[0]


def _gated_merge(outs, ups, zg, bias, *, name, tm=1024, tn=512):
    s, n, nq = zg.shape[0], ups[0].shape[1], len(outs)
    tm, tn = min(tm, s), min(tn, n)
    per = n // tn

    def body(*refs):
        y = None
        for q in range(nq):
            o_ref, w_ref, z_ref, b_ref = refs[q], refs[nq + q], refs[2 * nq + q], refs[3 * nq + q]
            term = _sig(z_ref[...].astype(F32) + b_ref[...]) * _dot(o_ref[...], w_ref[...])
            y = term if y is None else y + term
        refs[4 * nq][...] = y.astype(BF16)

    in_specs = [pl.BlockSpec((tm, o.shape[1]), lambda i, j: (i, 0)) for o in outs]
    in_specs += [pl.BlockSpec((u.shape[0], tn), lambda i, j: (0, j)) for u in ups]
    in_specs += [pl.BlockSpec((tm, tn), lambda i, j, q=q: (i, j + q * per)) for q in range(nq)]
    in_specs += [pl.BlockSpec((1, tn), lambda i, j, q=q: (0, j + q * per)) for q in range(nq)]
    return pl.pallas_call(body, name=name, grid=(s // tm, per), in_specs=in_specs,
                          out_specs=pl.BlockSpec((tm, tn), lambda i, j: (i, j)),
                          out_shape=jax.ShapeDtypeStruct((s, n), BF16),
                          compiler_params=_params('arbitrary', 'arbitrary'))(*outs, *ups, *[zg] * nq, *[bias] * nq)


def _gated_merge_bwd(dy, zg, bias, outs, ups, do_dtypes, *, name, tm=512):
    s, n = dy.shape
    nq = len(outs)
    tm = min(tm, s)

    def body(*refs):
        dy_ref, zg_ref, b_ref = refs[:3]
        o_refs, w_refs = refs[3:3 + nq], refs[3 + nq:3 + 2 * nq]
        du_refs, do_refs = refs[3 + 2 * nq:3 + 3 * nq], refs[3 + 3 * nq:3 + 4 * nq]
        dz_ref, db_ref = refs[3 + 4 * nq:]

        @pl.when(pl.program_id(0) == 0)
        def _():
            db_ref[...] = jnp.zeros_like(db_ref)

        d = dy_ref[...].astype(F32)
        for q in range(nq):
            cols = slice(q * n, (q + 1) * n)
            g = _sig(zg_ref[:, cols].astype(F32) + b_ref[:, cols])
            du = (d * g).astype(BF16)
            du_refs[q][...] = du
            do_refs[q][...] = _dot(du, w_refs[q][...], NT).astype(do_dtypes[q])
            dz = d * _dot(o_refs[q][...], w_refs[q][...]) * g * (1.0 - g)
            dz_ref[:, cols] = dz.astype(BF16)
            db_ref[:, cols] += jnp.sum(dz, axis=0, keepdims=True)

    row = lambda w: pl.BlockSpec((tm, w), lambda i: (i, 0))
    whole = lambda a: pl.BlockSpec(a.shape, lambda i: (0, 0))
    in_specs = [row(n), row(nq * n), whole(bias)] + [row(o.shape[1]) for o in outs] + [whole(u) for u in ups]
    out_specs = [row(n)] * nq + [row(o.shape[1]) for o in outs] + [row(nq * n), pl.BlockSpec((1, nq * n), lambda i: (0, 0))]
    out_shape = ([jax.ShapeDtypeStruct((s, n), BF16)] * nq
                 + [jax.ShapeDtypeStruct((s, o.shape[1]), dt) for o, dt in zip(outs, do_dtypes)]
                 + [jax.ShapeDtypeStruct((s, nq * n), BF16), jax.ShapeDtypeStruct((1, nq * n), F32)])
    res = pl.pallas_call(body, name=name, grid=(s // tm,), in_specs=in_specs, out_specs=out_specs, out_shape=out_shape,
                         compiler_params=_params('arbitrary'))(dy, zg, bias, *outs, *ups)
    return res[:nq], res[nq:2 * nq], res[2 * nq], res[2 * nq + 1]


def _mn(col_off=0):
    return 'mn', col_off


def _nvec(col_off=0):
    return 'n', col_off


def _rowwise(fn, rows, consts, outs, sums=(), *, name, ts=256):
    views = [x if isinstance(x, Cols) else Cols(x, x.shape[1], 0) for x in rows]
    rows = [v.arr for v in views]
    r = rows[0].shape[0]
    ts = min(ts, r)
    assert r % ts == 0, (name, r, ts)
    nr, nc, no, ns = len(rows), len(consts), len(outs), len(sums)

    def body(*refs):
        vals = fn(*[x[...] for x in refs[:nr + nc]])
        for q in range(no):
            refs[nr + nc + q][...] = vals[q].astype(outs[q][1])
        if ns:
            @pl.when(pl.program_id(0) == 0)
            def _():
                for q in range(ns):
                    refs[nr + nc + no + q][...] = jnp.zeros((1, sums[q]), F32)

            for q in range(ns):
                refs[nr + nc + no + q][...] += jnp.sum(vals[no + q].astype(F32), axis=0, keepdims=True)

    in_specs = [pl.BlockSpec((ts, v.width), lambda i, blk=v.blk: (i, blk)) for v in views]
    in_specs += [pl.BlockSpec(x.shape, lambda i, nd=x.ndim: (0,) * nd) for x in consts]
    out_specs = [pl.BlockSpec((ts, w), lambda i: (i, 0)) for w, _ in outs]
    out_specs += [pl.BlockSpec((1, w), lambda i: (0, 0)) for w in sums]
    out_shape = [jax.ShapeDtypeStruct((r, w), dt) for w, dt in outs]
    out_shape += [jax.ShapeDtypeStruct((1, w), F32) for w in sums]
    return pl.pallas_call(body, name=name, grid=(r // ts,), in_specs=in_specs, out_specs=out_specs,
                          out_shape=out_shape, compiler_params=_params('arbitrary'))(*rows, *consts)


def _cumsum_rows(x, *, reverse, name, bs=256):
    s, w = x.shape
    bs = min(bs, s)
    nb = s // bs

    def body(x_ref, o_ref, carry):
        @pl.when(pl.program_id(0) == 0)
        def _():
            carry[...] = jnp.zeros_like(carry)

        r = lax.broadcasted_iota(jnp.int32, (bs, bs), 0)
        c = lax.broadcasted_iota(jnp.int32, (bs, bs), 1)
        tri = jnp.where((c >= r) if reverse else (c <= r), 1.0, 0.0).astype(BF16)
        xv = x_ref[...]
        o_ref[...] = _tri_dot(tri, xv) + carry[...]
        carry[...] += jnp.sum(xv, axis=0, keepdims=True)

    imap = (lambda i: (nb - 1 - i, 0)) if reverse else (lambda i: (i, 0))
    return pl.pallas_call(body, name=name, grid=(nb,), in_specs=[pl.BlockSpec((bs, w), imap)],
                          out_specs=pl.BlockSpec((bs, w), imap), out_shape=jax.ShapeDtypeStruct((s, w), F32),
                          scratch_shapes=[pltpu.VMEM((1, w), F32)], compiler_params=_params('arbitrary'))(x)


def _mask(mode, q0, k0, bq, bk):
    qpos = q0 + lax.broadcasted_iota(jnp.int32, (bq, bk), 0)
    kpos = k0 + lax.broadcasted_iota(jnp.int32, (bq, bk), 1)
    if mode == 'causal':
        return kpos <= qpos
    return kpos < (jnp.right_shift(qpos, int(math.log2(CHUNK))) + 1) * CHUNK


ROPE_SHIFT = int(math.log2(MLA_ROPE))
FOX_SCALE, MLA_SCALE, XA_SCALE = FOX_HD ** -0.5, (MLA_NOPE + MLA_ROPE) ** -0.5, XA_HD ** -0.5
ATTN_ROW_SLAB = 512


def _lane_masks(g, b, rope):
    lane = lax.broadcasted_iota(jnp.int32, (1, LANES), 1)
    heads = [None if g == 1 else (lane >= hh * (LANES // g)) & (lane < (hh + 1) * (LANES // g)) for hh in range(g)]
    ropes = [jnp.right_shift(lane, ROPE_SHIFT) == b * g + hh for hh in range(g)] if rope else [None] * g
    return heads, ropes


def _sel(mask, x):
    return x if mask is None else jnp.where(mask, x, jnp.zeros_like(x))


class Step(typing.NamedTuple):
    qi: typing.Any
    kj: typing.Any
    first: typing.Any
    last: typing.Any
    plain: typing.Any
    masked: typing.Any


def _fwd_steps(tri, nq, nk):
    if not tri:
        return (nq, nk), lambda i, j: Step(i, j, j == 0, j == nk - 1, True, False)
    if nq % 2:
        return (nq, nk), lambda i, j: Step(i, jnp.minimum(i, j), j == 0, j == nk - 1, j < i, j == i)

    def at(i, t):
        low = t <= i
        diag = (t == i) | (t == nq)
        return Step(jnp.where(low, i, nq - 1 - i), jnp.where(low, t, t - (i + 1)), (t == 0) | (t == i + 1), diag,
                    jnp.logical_not(diag), diag)

    return (nq // 2, nq + 1), at


def _bwd_steps(tri, nq, nk):
    if not tri:
        return (nk, nq), lambda j, i: Step(i, j, i == 0, i == nq - 1, True, False)
    if nk % 2:
        return (nk, nq), lambda j, i: Step(jnp.maximum(i, j), j, i == 0, i == nq - 1, i > j, i == j)

    def at(j, t):
        n1 = nq - j
        low = t < n1
        diag = (t == 0) | (t == n1)
        return Step(jnp.where(low, j + t, nk - 1 - j + t - n1), jnp.where(low, j, nk - 1 - j), diag,
                    (t == n1 - 1) | (t == nq), jnp.logical_not(diag), diag)

    return (nk // 2, nq + 1), at


def _carried(comm, refs, n_in, n_out):
    ci, co = len(comm.ins), len(comm.out_shapes)
    ins = refs[n_in:n_in + ci]
    outs = refs[n_in + ci + n_out:n_in + ci + n_out + co]
    rest = refs[:n_in] + refs[n_in + ci:n_in + ci + n_out] + refs[n_in + ci + n_out + co:-2]
    return rest, (ins, outs, refs[-2], refs[-1])


def _mattn_fwd(q, k, v, *, qc, kc, vc, nb, g, mode, name, dq_scale=1.0, ck=None, qr=None, qrc=0, kr=None, blk=512,
               comm=None):
    s, t = q.shape[0], k.shape[0]
    bq, bk = min(blk, s), min(blk, t)
    nq, nk = s // bq, t // bk
    tri = mode != 'full'
    bias, rope = ck is not None, qr is not None
    assert not tri or (bq == bk and bq % CHUNK == 0)
    rs = min(ATTN_ROW_SLAB, bq)
    n_in = 3 + bias + 2 * rope
    (n1, n2), step_at = _fwd_steps(tri, nq, nk)

    def body(*refs):
        refs = list(refs)
        b, p1, p2 = pl.program_id(0), pl.program_id(1), pl.program_id(2)
        st = step_at(p1, p2)
        i, j = st.qi, st.kj
        if comm is not None:
            refs, comm_refs = _carried(comm, refs, n_in, 2)
            pl.when((b == 0) & (p1 == 0) & (p2 == 0))(lambda: comm.start(*comm_refs))
        q_ref, k_ref, v_ref = refs[:3]
        pos = 3
        ck_ref = qr_ref = kr_ref = None
        if bias:
            ck_ref = refs[pos]
            pos += 1
        if rope:
            qr_ref, kr_ref = refs[pos:pos + 2]
            pos += 2
        o_ref, lse_ref, m_s, l_s, acc_s = refs[pos:]
        heads, ropes = _lane_masks(g, b, rope)

        @pl.when(st.first)
        def _():
            m_s[...] = jnp.full_like(m_s, MASK_VALUE)
            l_s[...] = jnp.zeros_like(l_s)
            acc_s[...] = jnp.zeros_like(acc_s)

        def compute(masked):
            k2, v2 = k_ref[...], v_ref[...]
            for r in range(bq // rs):
                rows = pl.ds(r * rs, rs)
                q2 = q_ref[rows, :]
                alphas, pvs = [], []
                for hh in range(g):
                    sc = _dot(_sel(heads[hh], q2), k2, NT)
                    if rope:
                        sc = sc + _dot(_sel(ropes[hh], qr_ref[rows, :]), kr_ref[...], NT)
                    if bias:
                        sc = sc - ck_ref[0, hh:hh + 1, :]
                    if masked:
                        sc = jnp.where(_mask(mode, i * bq + r * rs, j * bk, rs, bk), sc, MASK_VALUE)
                    m_prev = m_s[hh, rows]
                    m_new = jnp.maximum(m_prev, jnp.max(sc, axis=1, keepdims=True))
                    alpha = jnp.exp(m_prev - m_new)
                    p = jnp.exp(sc - m_new)
                    l_s[hh, rows] = alpha * l_s[hh, rows] + jnp.sum(p, axis=1, keepdims=True)
                    m_s[hh, rows] = m_new
                    alphas.append(alpha)
                    pvs.append(_dot(p.astype(BF16), _sel(heads[hh], v2)))
                alpha = alphas[0]
                for hh in range(1, g):
                    alpha = jnp.where(heads[hh], alphas[hh], alpha)
                acc_s[rows, :] = acc_s[rows, :] * alpha + sum(pvs[1:], pvs[0])

        if tri:
            pl.when(st.plain)(functools.partial(compute, False))
            pl.when(st.masked)(functools.partial(compute, True))
        else:
            compute(False)

        @pl.when(st.last)
        def _():
            lane = lax.broadcasted_iota(jnp.int32, (bq, LANES), 1)
            l_full, lse = l_s[0], jnp.zeros((bq, LANES), F32)
            for hh in range(g):
                if hh:
                    l_full = jnp.where(heads[hh], l_s[hh], l_full)
                lse = jnp.where(lane == hh, m_s[hh] + jnp.log(l_s[hh]), lse)
            o_ref[...] = (acc_s[...] / l_full).astype(o_ref.dtype)
            lse_ref[...] = lse

        if comm is not None:
            pl.when((b == nb - 1) & (p1 == n1 - 1) & (p2 == n2 - 1))(lambda: comm.finish(*comm_refs))

    qi = lambda p1, p2: step_at(p1, p2).qi
    kj = lambda p1, p2: step_at(p1, p2).kj
    in_specs = [pl.BlockSpec((bq, LANES), lambda b, p1, p2: (qi(p1, p2), qc + b)),
                pl.BlockSpec((bk, LANES), lambda b, p1, p2: (kj(p1, p2), kc + b)),
                pl.BlockSpec((bk, LANES), lambda b, p1, p2: (kj(p1, p2), vc + b))]
    args = [q, k, v]
    if bias:
        in_specs.append(pl.BlockSpec((1, 8, bk), lambda b, p1, p2: (b, 0, kj(p1, p2))))
        args.append(ck)
    if rope:
        in_specs += [pl.BlockSpec((bq, LANES), lambda b, p1, p2: (qi(p1, p2), qrc)),
                     pl.BlockSpec((bk, LANES), lambda b, p1, p2: (kj(p1, p2), 0))]
        args += [qr, kr]
    out = pl.BlockSpec((bq, LANES), lambda b, p1, p2: (qi(p1, p2), b))
    out_specs = [out, out]
    out_shape = [jax.ShapeDtypeStruct((s, LANES * nb), BF16), jax.ShapeDtypeStruct((s, LANES * nb), F32)]
    scratch = [pltpu.VMEM((g, bq, 1), F32), pltpu.VMEM((g, bq, 1), F32), pltpu.VMEM((bq, LANES), F32)]
    if comm is not None:
        in_specs += [ANY] * len(comm.ins)
        args += comm.ins
        out_specs += [ANY] * len(comm.out_shapes)
        out_shape += comm.out_shapes
        scratch += _sems(comm.n_sems, comm.n_sems)
    res = pl.pallas_call(body, name=name, grid=(nb, n1, n2), in_specs=in_specs, out_specs=out_specs, out_shape=out_shape,
                         scratch_shapes=scratch, compiler_params=_params('arbitrary', 'arbitrary', 'arbitrary'))(*args)
    return res if comm is None else (res[0], res[1], res[2:])


def _mattn_bwd(q, k, v, o, do, lse, *, qc, kc, vc, nb, g, mode, name, dq_scale=1.0, ck=None, qr=None, qrc=0, kr=None,
               blk=512, comm=None):
    s, t = q.shape[0], k.shape[0]
    bq, bk = min(blk, s), min(blk, t)
    nq, nk = s // bq, t // bk
    tri = mode != 'full'
    bias, rope = ck is not None, qr is not None
    rs = min(ATTN_ROW_SLAB, bq)
    n_in, n_out = 6 + bias + 2 * rope, 3 + 2 * bias + 2 * rope
    (n1, n2), step_at = _bwd_steps(tri, nq, nk)

    def body(*refs):
        refs = list(refs)
        if comm is not None:
            refs, comm_refs = _carried(comm, refs, n_in, n_out)
            first = (pl.program_id(0) == 0) & (pl.program_id(1) == 0) & (pl.program_id(2) == 0)
            pl.when(first)(lambda: comm.start(*comm_refs))
        q_ref, k_ref, v_ref, o_ref, do_ref, lse_ref = refs[:6]
        pos = 6
        ck_ref = qr_ref = kr_ref = dck_ref = dcq_ref = dqr_ref = dkr_ref = dck_s = None
        if bias:
            ck_ref = refs[pos]
            pos += 1
        if rope:
            qr_ref, kr_ref = refs[pos:pos + 2]
            pos += 2
        dq_ref, dk_ref, dv_ref = refs[pos:pos + 3]
        pos += 3
        if bias:
            dck_ref, dcq_ref = refs[pos:pos + 2]
            pos += 2
        if rope:
            dqr_ref, dkr_ref = refs[pos:pos + 2]
            pos += 2
        dk_s, dv_s = refs[pos:pos + 2]
        if bias:
            dck_s = refs[pos + 2]
        b, p1, p2 = pl.program_id(0), pl.program_id(1), pl.program_id(2)
        st = step_at(p1, p2)
        i, j = st.qi, st.kj
        heads, ropes = _lane_masks(g, b, rope)

        @pl.when((p1 == 0) & (p2 == 0))
        def _():
            dq_ref[...] = jnp.zeros_like(dq_ref)
            if bias:
                dcq_ref[...] = jnp.zeros_like(dcq_ref)

        if rope:
            @pl.when((b == 0) & (p1 == 0) & (p2 == 0))
            def _():
                dqr_ref[...] = jnp.zeros_like(dqr_ref)
                dkr_ref[...] = jnp.zeros_like(dkr_ref)

        @pl.when(st.first)
        def _():
            dk_s[...] = jnp.zeros_like(dk_s)
            dv_s[...] = jnp.zeros_like(dv_s)
            if bias:
                dck_s[...] = jnp.zeros_like(dck_s)

        def compute(masked):
            k2, v2 = k_ref[...], v_ref[...]
            lane = lax.broadcasted_iota(jnp.int32, (rs, LANES), 1)
            rk = pl.ds(pl.multiple_of(j * bk, bk), bk)
            add = lambda tot, x: x if tot is None else tot + x
            dv_t = dk_t = dkr_t = None
            dck_t = [None] * g
            for r in range(bq // rs):
                rows = pl.ds(r * rs, rs)
                rq = pl.ds(pl.multiple_of(i * bq + r * rs, rs), rs)
                q2, do2, lse2 = q_ref[rows, :], do_ref[rows, :], lse_ref[rows, :]
                dd = do2.astype(F32) * o_ref[rows, :].astype(F32)
                dq_t = dqr_t = dcq_t = None
                for hh in range(g):
                    qm = _sel(heads[hh], q2)
                    sc = _dot(qm, k2, NT)
                    if rope:
                        qrm = _sel(ropes[hh], qr_ref[rows, :])
                        sc = sc + _dot(qrm, kr_ref[...], NT)
                    if bias:
                        sc = sc - ck_ref[0, hh:hh + 1, :]
                    if masked:
                        sc = jnp.where(_mask(mode, i * bq + r * rs, j * bk, rs, bk), sc, MASK_VALUE)
                    p = jnp.exp(sc - jnp.sum(jnp.where(lane == hh, lse2, 0.0), axis=1, keepdims=True))
                    dom = _sel(heads[hh], do2)
                    dp = _dot(dom, v2, NT)
                    delta = jnp.sum(_sel(heads[hh], dd), axis=1, keepdims=True)
                    ds = p * (dp - delta)
                    dsb = ds.astype(BF16)
                    dv_t = add(dv_t, _dot(p.astype(BF16), dom, TN))
                    dk_t = add(dk_t, _dot(dsb, qm, TN))
                    dq_t = add(dq_t, _dot(dsb, _sel(heads[hh], k2)))
                    if rope:
                        dqr_t = add(dqr_t, _dot(dsb, _sel(ropes[hh], kr_ref[...])))
                        dkr_t = add(dkr_t, _dot(dsb, qrm, TN))
                    if bias:
                        dck_t[hh] = add(dck_t[hh], jnp.sum(ds, axis=0, keepdims=True))
                        dcq_t = add(dcq_t, jnp.where(lane == hh, jnp.sum(ds, axis=1, keepdims=True), 0.0))
                dq_ref[rq, :] += dq_t if dq_scale == 1.0 else dq_scale * dq_t
                if rope:
                    dqr_ref[rq, :] += dq_scale * dqr_t
                if bias:
                    dcq_ref[rq, :] += dcq_t
            dv_s[...] += dv_t
            dk_s[...] += dk_t
            if rope:
                dkr_ref[rk, :] += dkr_t
            if bias:
                for hh in range(g):
                    dck_s[hh:hh + 1, :] -= dck_t[hh]

        if tri:
            pl.when(st.plain)(functools.partial(compute, False))
            pl.when(st.masked)(functools.partial(compute, True))
        else:
            compute(False)

        @pl.when(st.last)
        def _():
            dk_ref[...] = dk_s[...]
            dv_ref[...] = dv_s[...]
            if bias:
                dck_ref[0] = dck_s[...]

        if comm is not None:
            pl.when((b == nb - 1) & (p1 == n1 - 1) & (p2 == n2 - 1))(lambda: comm.finish(*comm_refs))

    qrow = lambda col: pl.BlockSpec((bq, LANES), lambda b, p1, p2: (step_at(p1, p2).qi, col(b)))
    krow = lambda col: pl.BlockSpec((bk, LANES), lambda b, p1, p2: (step_at(p1, p2).kj, col(b)))
    in_specs = [qrow(lambda b: qc + b), krow(lambda b: kc + b), krow(lambda b: vc + b), qrow(lambda b: b),
                qrow(lambda b: b), qrow(lambda b: b)]
    args = [q, k, v, o, do, lse]
    whole = lambda rows: pl.BlockSpec((rows, LANES), lambda b, j, i: (0, b))
    out_specs = [whole(s), krow(lambda b: b), krow(lambda b: b)]
    out_shape = [jax.ShapeDtypeStruct((s, LANES * nb), F32), jax.ShapeDtypeStruct((t, LANES * nb), F32),
                 jax.ShapeDtypeStruct((t, LANES * nb), F32)]
    scratch = [pltpu.VMEM((bk, LANES), F32), pltpu.VMEM((bk, LANES), F32)]
    if bias:
        ckj = pl.BlockSpec((1, 8, bk), lambda b, p1, p2: (b, 0, step_at(p1, p2).kj))
        in_specs.append(ckj)
        args.append(ck)
        out_specs += [ckj, whole(s)]
        out_shape += [jax.ShapeDtypeStruct((nb, 8, t), F32), jax.ShapeDtypeStruct((s, LANES * nb), F32)]
    if rope:
        in_specs += [qrow(lambda b: qrc), krow(lambda b: 0)]
        args += [qr, kr]
        out_specs += [pl.BlockSpec((s, LANES), lambda b, j, i: (0, 0)), pl.BlockSpec((t, LANES), lambda b, j, i: (0, 0))]
        out_shape += [jax.ShapeDtypeStruct((s, LANES), F32), jax.ShapeDtypeStruct((t, LANES), F32)]
    if bias:
        scratch.append(pltpu.VMEM((8, bk), F32))
    if comm is not None:
        in_specs += [ANY] * len(comm.ins)
        args += comm.ins
        out_specs += [ANY] * len(comm.out_shapes)
        out_shape += comm.out_shapes
        scratch += _sems(comm.n_sems, comm.n_sems)
    res = pl.pallas_call(body, name=name, grid=(nb, n1, n2), in_specs=in_specs, out_specs=out_specs,
                         out_shape=out_shape, scratch_shapes=scratch,
                         compiler_params=_params('arbitrary', 'arbitrary', 'arbitrary'))(*args)
    return res if comm is None else (*res[:n_out], res[n_out:])


def _gla_chunk(la_c, k_c):
    r = lax.broadcasted_iota(jnp.int32, (CHUNK, CHUNK), 0)
    c = lax.broadcasted_iota(jnp.int32, (CHUNK, CHUNK), 1)
    tri = jnp.where(c <= r, 1.0, 0.0).astype(BF16)
    cum = _tri_dot(tri, la_c)
    end = jnp.sum(la_c, axis=0, keepdims=True)
    dec = jnp.exp(end - cum)
    return dec, k_c * dec, jnp.exp(end)


GLA_PAIRS = GLA_HEADS // 2


def _gla_fwd(z, la, *, qc, kc, vc, name, blk=512):
    s = z.shape[0]
    bs = min(blk, s)
    ncb = bs // CHUNK
    nblk = s // bs

    def body(q_ref, k_ref, va_ref, vb_ref, la_ref, o_ref, st_ref, st):
        @pl.when(pl.program_id(1) == 0)
        def _():
            st[...] = jnp.zeros_like(st)

        heads, _ = _lane_masks(2, 0, False)
        v_refs = (va_ref, vb_ref)
        for c in range(ncb):
            sl = pl.ds(c * CHUNK, CHUNK)
            _, kf, a = _gla_chunk(la_ref[sl, :], k_ref[sl, :])
            qs = q_ref[sl, :] * (GLA_DK ** -0.5)
            for hh in range(2):
                ut = _dot(v_refs[hh][sl, :].astype(BF16), _sel(heads[hh], kf).astype(BF16), TN)
                new = a * st[hh] + ut
                st[hh] = new
                st_ref[0, c, hh] = new
                o_ref[sl, hh * GLA_DV:(hh + 1) * GLA_DV] = _dot(_sel(heads[hh], qs).astype(BF16), new.astype(BF16), NT)

    col = lambda c0, m=1: pl.BlockSpec((bs, LANES), lambda b, i: (i, c0 + m * b))
    return pl.pallas_call(
        body, name=name, grid=(GLA_PAIRS, nblk),
        in_specs=[col(qc), col(kc), col(vc, 2), col(vc + 1, 2), col(0)],
        out_specs=[pl.BlockSpec((bs, 2 * GLA_DV), lambda b, i: (i, b)),
                   pl.BlockSpec((1, ncb, 2, GLA_DV, LANES), lambda b, i: (b, i, 0, 0, 0))],
        out_shape=[jax.ShapeDtypeStruct((s, GLA_HEADS * GLA_DV), F32),
                   jax.ShapeDtypeStruct((GLA_PAIRS, s // CHUNK, 2, GLA_DV, LANES), F32)],
        scratch_shapes=[pltpu.VMEM((2, GLA_DV, LANES), F32)],
        compiler_params=_params('arbitrary', 'arbitrary'))(z, z, z, z, la)


def _gla_bwd(z, la, st_all, st_prev, do, *, qc, kc, vc, name, blk=512):
    s = z.shape[0]
    bs = min(blk, s)
    ncb = bs // CHUNK
    nblk = s // bs

    def body(q_ref, k_ref, va_ref, vb_ref, la_ref, st_ref, sp_ref, do_ref, dq_ref, dk_ref, dv_ref, dla_ref, ga):
        @pl.when(pl.program_id(1) == 0)
        def _():
            ga[...] = jnp.zeros_like(ga)

        r = lax.broadcasted_iota(jnp.int32, (CHUNK, CHUNK), 0)
        cc = lax.broadcasted_iota(jnp.int32, (CHUNK, CHUNK), 1)
        tri_rev = jnp.where(cc >= r, 1.0, 0.0).astype(BF16)
        heads, _ = _lane_masks(2, 0, False)
        v_refs = (va_ref, vb_ref)
        for c in reversed(range(ncb)):
            sl = pl.ds(c * CHUNK, CHUNK)
            dec, kf, a = _gla_chunk(la_ref[sl, :], k_ref[sl, :])
            qs = q_ref[sl, :] * (GLA_DK ** -0.5)
            dq2 = jnp.zeros((CHUNK, LANES), F32)
            dkd = jnp.zeros((CHUNK, LANES), F32)
            da = jnp.zeros((1, LANES), F32)
            for hh in range(2):
                hv = slice(hh * GLA_DV, (hh + 1) * GLA_DV)
                dob = do_ref[sl, hv].astype(BF16)
                g = _dot(dob, _sel(heads[hh], qs).astype(BF16), TN) + ga[hh]
                gb = g.astype(BF16)
                dq2 = dq2 + _dot(dob, st_ref[0, c, hh].astype(BF16))
                dv_ref[sl, hv] = _dot(_sel(heads[hh], kf).astype(BF16), gb, NT)
                dkd = dkd + _dot(v_refs[hh][sl, :].astype(BF16), gb)
                da = da + jnp.sum(g * sp_ref[0, c, hh], axis=0, keepdims=True)
                ga[hh] = a * g
            dq_ref[sl, :] = (GLA_DK ** -0.5) * dq2
            dk_ref[sl, :] = dkd * dec
            e = dkd * kf
            dend = jnp.sum(e, axis=0, keepdims=True) + da * a
            dla_ref[sl, :] = dend - _tri_dot(tri_rev, e)

    rev = lambda i: nblk - 1 - i
    col = lambda c0, m=1: pl.BlockSpec((bs, LANES), lambda b, i: (rev(i), c0 + m * b))
    wide = pl.BlockSpec((bs, 2 * GLA_DV), lambda b, i: (rev(i), b))
    stspec = pl.BlockSpec((1, ncb, 2, GLA_DV, LANES), lambda b, i: (b, rev(i), 0, 0, 0))
    return pl.pallas_call(
        body, name=name, grid=(GLA_PAIRS, nblk),
        in_specs=[col(qc), col(kc), col(vc, 2), col(vc + 1, 2), col(0), stspec, stspec, wide],
        out_specs=[col(0), col(0), wide, col(0)],
        out_shape=[jax.ShapeDtypeStruct((s, GLA_HEADS * GLA_DK), F32), jax.ShapeDtypeStruct((s, GLA_HEADS * GLA_DK), F32),
                   jax.ShapeDtypeStruct((s, GLA_HEADS * GLA_DV), F32), jax.ShapeDtypeStruct((s, GLA_HEADS * GLA_DK), F32)],
        scratch_shapes=[pltpu.VMEM((2, GLA_DV, LANES), F32)],
        compiler_params=_params('arbitrary', 'arbitrary'))(z, z, z, z, la, st_all, st_prev, do)


def _place():
    return lax.axis_index('x'), lax.axis_index('y'), lax.axis_index('c')


ANY = pl.BlockSpec(memory_space=pl.ANY)


def _all_gather8(blk, *, name):
    m, n = blk.shape

    def body(x_ref, out_ref, send_sems, recv_sems, local_sem):
        x, y, c = _place()
        me, sibling = (x, y, c), (x, y, 1 - c)
        chips = [(1 - x, y), (x, 1 - y), (1 - x, 1 - y)]

        def slot(px, py, pc):
            return out_ref.at[4 * px + 2 * py + pc]

        def copy(q, block, to, src=None):
            return pltpu.make_async_remote_copy(
                src_ref=slot(*block) if src is None else src, dst_ref=slot(*block), send_sem=send_sems.at[q],
                recv_sem=recv_sems.at[q], device_id=to, device_id_type=MESH)

        mine = pltpu.make_async_copy(x_ref, slot(*me), local_sem)
        mine.start()
        first = [copy(0, me, sibling, src=x_ref)]
        first += [copy(1 + q, me, (*chip, c), src=x_ref) for q, chip in enumerate(chips)]
        for cp in first:
            cp.start()
        passed = [copy(4 + q, (*chip, c), sibling) for q, chip in enumerate(chips)]
        for q, chip in enumerate(chips):
            copy(1 + q, (*chip, c), me).wait_recv()
            passed[q].start()
        copy(0, sibling, me).wait_recv()
        for q, chip in enumerate(chips):
            copy(4 + q, (*chip, 1 - c), me).wait_recv()
        for cp in first + passed:
            cp.wait_send()
        mine.wait()

    return pl.pallas_call(
        body, name=name, in_specs=[ANY], out_specs=ANY, out_shape=jax.ShapeDtypeStruct((N_DEV, m, n), blk.dtype),
        scratch_shapes=[pltpu.SemaphoreType.DMA((7,)), pltpu.SemaphoreType.DMA((7,)), pltpu.SemaphoreType.DMA(())],
    )(blk)


def _sems(*counts):
    return [pltpu.SemaphoreType.DMA((n,)) for n in counts]


class Comm(typing.NamedTuple):
    ins: list
    out_shapes: list
    n_sems: int
    start: typing.Callable
    finish: typing.Callable


def _remote(src, dst, send_sems, recv_sems, idx, to):
    return lambda: pltpu.make_async_remote_copy(src_ref=src, dst_ref=dst, send_sem=send_sems.at[idx],
                                                recv_sem=recv_sems.at[idx], device_id=to, device_id_type=MESH)


def _comm_from(copies, ins, out_shapes, n_sems):
    def start(*refs):
        for cp in copies(*refs)[0]:
            cp().start()

    def finish(*refs):
        sent, received = copies(*refs)
        for cp in received:
            cp().wait_recv()
        for cp in sent:
            cp().wait_send()

    return Comm(list(ins), list(out_shapes), n_sems, start, finish)


def _run_comm(comm, *, name, alias=False):
    n_in, n_out = len(comm.ins), len(comm.out_shapes)

    def body(*refs):
        ins, outs, sems = refs[:n_in], refs[n_in:n_in + n_out], refs[n_in + n_out:]
        comm.start(ins, outs, *sems)
        comm.finish(ins, outs, *sems)

    return pl.pallas_call(body, name=name, in_specs=[ANY] * n_in, out_specs=[ANY] * n_out, out_shape=comm.out_shapes,
                          input_output_aliases={q: q for q in range(n_in)} if alias else {},
                          scratch_shapes=_sems(comm.n_sems, comm.n_sems))(*comm.ins)


def _half(rows, c):
    h = rows // 2
    return pl.ds(pl.multiple_of(c * h, h), h)


def _gathered(ref, chip, rows, side):
    if not side:
        return ref.at[chip, rows]
    n = ref.shape[1] // N_CHIPS
    return ref.at[rows, pl.ds(pl.multiple_of(chip * n, n), n)]


def _gather_over_ici(ws, side):
    def copies(ins, outs, send_sems, recv_sems):
        x, y, c = _place()
        me_chip = 2 * x + y
        sent, received = [], []
        for q, w in enumerate(ws):
            half, every = _half(w.shape[0], c), pl.ds(0, w.shape[0])
            for k, (px, py) in enumerate([(1 - x, y), (x, 1 - y), (1 - x, 1 - y)]):
                sent.append(_remote(ins[q].at[half], _gathered(outs[q], me_chip, half, side[q]), send_sems, recv_sems,
                                    4 * q + k, (px, py, c)))
                slot = _gathered(outs[q], 2 * px + py, half, side[q])
                received.append(_remote(slot, slot, send_sems, recv_sems, 4 * q + k, (px, py, c)))
            whole = _remote(ins[q], _gathered(outs[q], me_chip, every, side[q]), send_sems, recv_sems, 4 * q + 3,
                            (x, y, 1 - c))
            sent.append(whole)
            received.append(whole)
        return sent, received

    shapes = [jax.ShapeDtypeStruct((w.shape[0], N_CHIPS * w.shape[1]) if sd else (N_CHIPS,) + w.shape, w.dtype)
              for w, sd in zip(ws, side)]
    return _comm_from(copies, ws, shapes, 4 * len(ws))


def _gather_over_d2d(parts, side):
    def copies(ins, outs, send_sems, recv_sems):
        x, y, c = _place()
        sent, received = [], []
        for q, w in enumerate(parts):
            rows = w.shape[0] if side[q] else w.shape[1]
            for k, (px, py) in enumerate([(1 - x, y), (x, 1 - y), (1 - x, 1 - y)]):
                mine = _gathered(outs[q], 2 * px + py, _half(rows, c), side[q])
                theirs = _gathered(outs[q], 2 * px + py, _half(rows, 1 - c), side[q])
                sent.append(_remote(mine, mine, send_sems, recv_sems, 3 * q + k, (x, y, 1 - c)))
                received.append(_remote(theirs, theirs, send_sems, recv_sems, 3 * q + k, (x, y, 1 - c)))
        return sent, received

    return _comm_from(copies, parts, [jax.ShapeDtypeStruct(w.shape, w.dtype) for w in parts], 3 * len(parts))


def _to_sibling(gs, *, name):
    n = len(gs)

    def body(*refs):
        ins, outs = refs[:n], refs[n:2 * n]
        send_sems, recv_sems = refs[2 * n:]
        x, y, c = _place()
        cps = [pltpu.make_async_remote_copy(
            src_ref=ins[q], dst_ref=outs[q], send_sem=send_sems.at[q], recv_sem=recv_sems.at[q],
            device_id=(x, y, 1 - c), device_id_type=MESH) for q in range(n)]
        for cp in cps:
            cp.start()
        for cp in cps:
            cp.wait()

    return pl.pallas_call(body, name=name, in_specs=[ANY] * n, out_specs=[ANY] * n,
                          out_shape=[jax.ShapeDtypeStruct(g.shape, g.dtype) for g in gs],
                          scratch_shapes=_sems(n, n))(*gs)


def _chip_exchange(ps):
    def copies(ins, outs, send_sems, recv_sems):
        x, y, c = _place()
        cps = [_remote(ins[q].at[2 * px + py], outs[q].at[k], send_sems, recv_sems, 3 * q + k, (px, py, c))
               for q in range(len(ps)) for k, (px, py) in enumerate([(1 - x, y), (x, 1 - y), (1 - x, 1 - y)])]
        return cps, cps

    return _comm_from(copies, ps, [jax.ShapeDtypeStruct((3,) + p.shape[1:], p.dtype) for p in ps], 3 * len(ps))


def _sum_chips(own, r, *, name, ts=256):
    k, n = own.shape
    ts = min(ts, k)

    def body(own_ref, r_ref, o_ref):
        f = lambda q: r_ref[q].astype(F32)
        o_ref[...] = ((own_ref[...].astype(F32) + f(0)) + f(1)) + f(2)

    return pl.pallas_call(
        body, name=name, grid=(k // ts,),
        in_specs=[pl.BlockSpec((ts, n), lambda i: (i, 0)), pl.BlockSpec((3, ts, n), lambda i: (0, i, 0))],
        out_specs=pl.BlockSpec((ts, n), lambda i: (i, 0)), out_shape=jax.ShapeDtypeStruct((k, n), F32),
        compiler_params=_params('arbitrary'))(own, r)


WIN_SHARD = N_IN // N_CHIPS
WIN_PAD = -(-WIN_SHARD // LANES) * LANES
GATE_WIRE_ROWS = 32


def _full_layer(sh, axis):
    _, k, n = sh.shape
    if axis == 2:
        return sh.transpose(1, 0, 2).reshape(k, N_CHIPS * n)
    return sh.reshape(N_CHIPS * k, n)


def _win_cols(wp, o, n):
    parts = []
    while n > 0:
        j, r = divmod(o, WIN_SHARD)
        take = min(n, WIN_SHARD - r)
        parts.append(wp[:, j * WIN_PAD + r:j * WIN_PAD + r + take])
        o, n = o + take, n - take
    return parts[0] if len(parts) == 1 else jnp.concatenate(parts, axis=1)


def _split_full(full, axis):
    k, n = full.shape
    if axis == 2:
        return jnp.stack([full[:, j * (n // N_CHIPS):(j + 1) * (n // N_CHIPS)] for j in range(N_CHIPS)])
    return full.reshape(N_CHIPS, k // N_CHIPS, n)


def _padc(a, w):
    return jnp.pad(a, ((0, 0), (0, w - a.shape[1])))


def _swap16(a):
    return jnp.concatenate([a[..., 16:32], a[..., 0:16]], axis=-1)


B_GR, B_GQ, B_GK, B_GV, B_MQ, B_MKR, B_MKRS, B_FF, B_GLOW, B_MKV, B_END = (
    0, 512, 768, 1024, 1536, 1792, 1920, 2048, 2176, 2304, 2432)
B_W = 2560
O_FQ, O_FF, O_GQ, O_GLOW, O_GR, O_MQ, O_MKV, O_MKR, O_ZG = 0, 768, 772, 1796, 1812, 2324, 2580, 2708, 2740


def _repack_layer_weights(w):
    wi = functools.partial(_win_cols, w['w_in'])
    out = dict(w)
    out['in_a'] = jnp.concatenate([wi(O_FQ, 256) * FOX_SCALE, wi(O_FQ + 256, 512)], axis=1)
    kr = wi(O_MKR, 32)
    out['in_b'] = jnp.concatenate([
        wi(O_GR, 512), wi(O_GQ, 1024), wi(O_MQ, 256), jnp.tile(kr, (1, MLA_HEADS)), jnp.tile(_swap16(kr), (1, MLA_HEADS)),
        _padc(wi(O_FF, 4), 128), _padc(wi(O_GLOW, 16), 128), wi(O_MKV, 128),
        jnp.zeros((D_MODEL, B_W - B_END), kr.dtype)], axis=1)
    out['in_c'] = wi(O_ZG, 3072)
    uq = w['w_mla_uq'].reshape(MLA_Q_RANK, MLA_HEADS, MLA_NOPE + MLA_ROPE)
    rope = uq[:, :, MLA_NOPE:]
    out['uq'] = jnp.concatenate([uq[:, :, :MLA_NOPE].reshape(MLA_Q_RANK, -1), rope.reshape(MLA_Q_RANK, -1),
                                 _swap16(rope).reshape(MLA_Q_RANK, -1)], axis=1)
    ukv = w['w_mla_ukv'].reshape(MLA_KV_RANK, MLA_HEADS, MLA_NOPE + MLA_VD)
    out['ukv'] = jnp.concatenate([ukv[:, :, :MLA_NOPE].reshape(MLA_KV_RANK, -1),
                                  ukv[:, :, MLA_NOPE:].reshape(MLA_KV_RANK, -1)], axis=1)
    out['gate'] = jnp.pad(w['w_gla_gate'], ((0, 128 - GLA_RANK), (0, 0)))
    return out


def _unpack_layer_grads(g):
    a, b, c = g['in_a'], g['in_b'], g['in_c']
    a = jnp.concatenate([a[:, :256] * FOX_SCALE, a[:, 256:]], axis=1)
    fold = lambda o: sum(b[:, o + MLA_ROPE * q:o + MLA_ROPE * (q + 1)] for q in range(MLA_HEADS))
    kr = fold(B_MKR) + _swap16(fold(B_MKRS))
    w_in = jnp.concatenate([a, b[:, B_FF:B_FF + 4], b[:, B_GQ:B_GQ + 1024], b[:, B_GLOW:B_GLOW + 16],
                            b[:, B_GR:B_GR + 512], b[:, B_MQ:B_MQ + 256], b[:, B_MKV:B_MKV + 128], kr, c], axis=1)
    uq = g['uq']
    nope = uq[:, :256].reshape(MLA_Q_RANK, MLA_HEADS, MLA_NOPE)
    rope = (uq[:, 256:384].reshape(MLA_Q_RANK, MLA_HEADS, MLA_ROPE)
            + _swap16(uq[:, 384:512].reshape(MLA_Q_RANK, MLA_HEADS, MLA_ROPE)))
    w_uq = jnp.concatenate([nope, rope], axis=2).reshape(MLA_Q_RANK, -1)
    ukv = g['ukv']
    w_ukv = jnp.concatenate([ukv[:, :256].reshape(MLA_KV_RANK, MLA_HEADS, MLA_NOPE),
                             ukv[:, 256:].reshape(MLA_KV_RANK, MLA_HEADS, MLA_VD)], axis=2).reshape(MLA_KV_RANK, -1)
    out = {'w_in': w_in, 'w_mla_uq': w_uq, 'w_mla_ukv': w_ukv, 'w_gla_gate': g['gate'][:GLA_RANK]}
    for nm in ('w_up_fox', 'w_up_gla', 'w_up_mla', 'w_out', 'w_xq', 'w_xkv', 'w_xo', 'w_mlp1', 'w_mlp2'):
        out[nm] = g[nm]
    return out


def _rope_tables(s):
    half = MLA_ROPE // 2
    inv = ROPE_BASE ** (-jnp.arange(half, dtype=F32) / half)
    ang = jnp.arange(s).astype(F32)[:, None] * inv[None, :]
    cos, sin = jnp.cos(ang), jnp.sin(ang)
    c1 = jnp.concatenate([cos, cos], axis=1)
    s1 = jnp.concatenate([-sin, sin], axis=1)
    return jnp.tile(c1, (1, MLA_HEADS)), jnp.tile(s1, (1, MLA_HEADS))


def _rms_bwd(x, dh, g):
    r = lax.rsqrt(jnp.mean(x * x, axis=-1, keepdims=True) + EPS)
    xh = x * r
    gd = dh * g
    return r * (gd - xh * jnp.mean(gd * xh, axis=-1, keepdims=True)), dh * xh


def _norm_bwd_epilogue(dh, x, dres, g):
    dx, dg = _rms_bwd(x, dh, g)
    return dres + dx, dg


def _norm_bwd_call(x, dh, g, dres, name):
    w = x.width if isinstance(x, Cols) else x.shape[1]

    def with_res(xv, dv, rv, gv):
        dx, dg = _rms_bwd(xv, dv.astype(F32), gv)
        return rv + dx, dg

    def plain(xv, dv, gv):
        return _rms_bwd(xv, dv.astype(F32), gv)

    if dres is None:
        return _rowwise(plain, [x, dh], [g], [(w, F32)], [w], name=name)
    return _rowwise(with_res, [x, dh, dres], [g], [(w, F32)], [w], name=name)


def _gla_out_fwd(oraw, gr, g_out):
    outs = []
    for hh in range(GLA_HEADS):
        sl = slice(hh * GLA_DV, (hh + 1) * GLA_DV)
        oh = oraw[:, sl]
        n = oh * lax.rsqrt(jnp.mean(oh * oh, axis=-1, keepdims=True) + EPS) * g_out
        r = gr[:, sl]
        outs.append(n * (r * _sig(r)))
    return (jnp.concatenate(outs, axis=1),)


def _gla_out_bwd(oraw, gr, dout, g_out):
    d_o, d_r, dg = [], [], 0.0
    for hh in range(GLA_HEADS):
        sl = slice(hh * GLA_DV, (hh + 1) * GLA_DV)
        oh, r, do = oraw[:, sl], gr[:, sl], dout[:, sl].astype(F32)
        rs = lax.rsqrt(jnp.mean(oh * oh, axis=-1, keepdims=True) + EPS)
        sg = _sig(r)
        dn = do * (r * sg)
        d_r.append(do * (oh * rs * g_out) * (sg + r * sg * (1.0 - sg)))
        dx, dgh = _rms_bwd(oh, dn, g_out)
        d_o.append(dx)
        dg = dg + dgh
    return jnp.concatenate(d_o, axis=1), jnp.concatenate(d_r, axis=1), dg


def _adam(w, g, m, v):
    m = ADAM_B1 * m + (1.0 - ADAM_B1) * g
    v = ADAM_B2 * v + (1.0 - ADAM_B2) * (g * g)
    m_hat = m / (1.0 - ADAM_B1 ** ADAM_STEP)
    v_hat = v / (1.0 - ADAM_B2 ** ADAM_STEP)
    return -ADAM_LR * (m_hat / (jnp.sqrt(v_hat) + ADAM_EPS) + ADAM_WD * w), m, v


def _layer_fwd(x, mem, w, p, tabs, tag, carry_fox=None, after_fox=None, carry_mla=None):
    c4, s4 = tabs
    sv = {'x0': x}
    nm = lambda t: f'{t}_{tag}'
    za, h = _mm(x, w['in_a'], mode='nn', out_dtype=BF16, norm_g=p['g_mix'], emit_norm=True, name=nm('in_a'))
    zb = _mm(h, w['in_b'], mode='nn', out_dtype=F32, name=nm('in_b'))
    zc = _mm(h, w['in_c'], mode='nn', out_dtype=F32, name=nm('in_c'))
    sv.update(h=h, zc=zc)
    ff = Cols(zb, 128, B_FF // 128)
    (lf,) = _rowwise(lambda f, b: (_logsig(f + b),), [ff], [p['b_fox']], [(128, F32)], name=nm('fox_lf'))
    cum = _cumsum_rows(lf, reverse=False, name=nm('fox_cum'))
    ckf = jnp.pad(cum[:, :FOX_HEADS].T.reshape(2, 2, x.shape[0]), ((0, 0), (0, 6), (0, 0)))
    fox = dict(qc=0, kc=2, vc=4, nb=2, g=2, mode='causal', ck=ckf)
    o_fox, lse_fox, *carried = _mattn_fwd(za, za, za, name=nm('fox_attn'), comm=carry_fox, **fox)
    if after_fox is not None:
        w = {**w, **after_fox(carried[0])}
    sv.update(ff=ff, za=za, fox=fox, o_fox=o_fox, lse_fox=lse_fox)
    glow = Cols(zb, 128, B_GLOW // 128)
    gr = Cols(zb, 512, B_GR // 512)

    def gate_fn(gl, wg, bg):
        return (_logsig(_dot(gl.astype(BF16), wg) + bg) / GLA_TAU,)

    (la,) = _rowwise(gate_fn, [glow], [w['gate'], p['b_gla']], [(256, F32)], name=nm('gla_gate'))
    gla = dict(qc=B_GQ // LANES, kc=B_GK // LANES, vc=B_GV // LANES)
    oraw, states = _gla_fwd(zb, la, name=nm('gla'), **gla)
    (o_gla,) = _rowwise(_gla_out_fwd, [oraw, gr], [p['g_gla_out']], [(512, BF16)], name=nm('gla_out'))
    sv.update(glow=glow, gr=gr, zb=zb, la=la, gla=gla, states=states, oraw=oraw, o_gla=o_gla)
    mq = Cols(zb, 256, B_MQ // 256)
    mkv = Cols(zb, 128, B_MKV // 128)
    mkr2 = Cols(zb, 256, B_MKR // 256)
    qp, cqn = _mm(mq, w['uq'], mode='nn', out_dtype=F32, norm_g=p['g_mla_q'], emit_norm=True, name=nm('mla_uq'))
    kvp, ckvn = _mm(mkv, w['ukv'], mode='nn', out_dtype=BF16, norm_g=p['g_mla_kv'], emit_norm=True,
                    name=nm('mla_ukv'))

    def rope_fn(qv, kr, c4v, s4v):
        q_rope = qv[:, 256:384] * c4v + qv[:, 384:512] * s4v
        q_scaled = jnp.concatenate([qv[:, 0:256], q_rope], axis=1) * MLA_SCALE
        return q_scaled, kr[:, 0:128] * c4v + kr[:, 128:256] * s4v

    qall, kr4 = _rowwise(rope_fn, [qp, mkr2, c4, s4], [], [(384, BF16), (128, BF16)], name=nm('rope'))
    mla = dict(qc=0, kc=0, vc=2, nb=2, g=2, dq_scale=MLA_SCALE, mode='chunk', qr=qall, qrc=2, kr=kr4)
    o_mla, lse_mla, *carried = _mattn_fwd(qall, kvp, kvp, name=nm('mla_attn'), comm=carry_mla, **mla)
    if carry_mla is not None:
        sv['carried_mla'] = carried[0]
    sv.update(mq=mq, mkv=mkv, cqn=cqn, ckvn=ckvn, qall=qall, kvp=kvp, mla=mla, o_mla=o_mla, lse_mla=lse_mla)
    of_m, om_m = o_fox, o_mla
    sv.update(of_m=of_m, om_m=om_m)
    b_br = p['b_branch']

    y = _gated_merge([of_m, o_gla, om_m], [w['w_up_fox'], w['w_up_gla'], w['w_up_mla']], zc, b_br, name=nm('up_merge'))
    add = lambda acc, res: res + acc
    x1 = _mm(y, w['w_out'], mode='nn', out_dtype=F32, name=nm('out'), epilogue=add, extras=[(x, *_mn())])
    sv.update(y=y, x1=x1)
    qx, hx = _mm(x1, w['w_xq'], mode='nn', out_dtype=BF16, norm_g=p['g_xa'], emit_norm=True, name=nm('xq'),
                 epilogue=lambda acc: acc * XA_SCALE)
    kvx, mn = _mm(mem, w['w_xkv'], mode='nn', out_dtype=BF16, norm_g=p['g_mem'], emit_norm=True, name=nm('xkv'))
    xa = dict(qc=0, kc=0, vc=4, nb=4, g=1, dq_scale=XA_SCALE, mode='full')
    ox_m, lse_x = _mattn_fwd(qx, kvx, kvx, name=nm('xa_attn'), **xa)
    x2 = _mm(ox_m, w['w_xo'], mode='nn', out_dtype=F32, name=nm('xo'), epilogue=add, extras=[(x1, *_mn())])
    sv.update(hx=hx, mn=mn, qx=qx, kvx=kvx, xa=xa, lse_x=lse_x, ox_m=ox_m, x2=x2)
    hpre, hm = _mm(x2, w['w_mlp1'], mode='nn', out_dtype=BF16, norm_g=p['g_mlp'], emit_norm=True, name=nm('mlp1'))
    relu2 = lambda t: jnp.square(jnp.maximum(t.astype(F32), 0.0))
    x3 = _mm(hpre, w['w_mlp2'], mode='nn', out_dtype=F32, name=nm('mlp2'), a_fn=relu2, epilogue=add,
             extras=[(x2, *_mn())])
    sv.update(hpre=hpre, hm=hm, w=w)
    return x3, sv


EARLY = ('w_mlp1', 'w_mlp2', 'w_xo', 'w_xq', 'w_xkv', 'w_out', 'w_up_fox', 'w_up_gla', 'w_up_mla')
LATE = ('w_in', 'w_gla_gate', 'w_mla_uq', 'w_mla_ukv')


def _layer_bwd(dx3, mem, w, p, tabs, sv, tag, carry_mla=None, early=None):
    c4, s4 = tabs
    nm = lambda t: f'{t}_{tag}'
    s = dx3.shape[0]
    gw, gs = {}, {}
    relu2 = lambda t: jnp.square(jnp.maximum(t.astype(F32), 0.0))
    gw['w_mlp2'] = _mm(sv['hpre'], dx3.astype(BF16), mode='tn', out_dtype=F32, name=nm('d_mlp2'), a_fn=relu2)
    dact = lambda acc, hp: acc * (2.0 * jnp.maximum(hp.astype(F32), 0.0))
    dhpre = _mm(dx3, w['w_mlp2'], mode='nt', out_dtype=BF16, name=nm('d_act'), epilogue=dact,
                extras=[(sv['hpre'], *_mn())])
    gw['w_mlp1'] = _mm(sv['hm'], dhpre, mode='tn', out_dtype=F32, name=nm('d_mlp1'))
    dx2, gs['g_mlp'] = _mm(dhpre, w['w_mlp1'], mode='nt', out_dtype=F32, name=nm('d_hm'), epilogue=_norm_bwd_epilogue,
                           col_sums=True, full_rows=True,
                           extras=[(sv['x2'], *_mn()), (dx3, *_mn()), (p['g_mlp'], *_nvec())])
    gw['w_xo'] = _mm(sv['ox_m'], dx2.astype(BF16), mode='tn', out_dtype=F32, name=nm('d_xo'))
    dox = _mm(dx2, w['w_xo'], mode='nt', out_dtype=BF16, name=nm('d_ox'))
    dqx_m, dkx, dvx = _mattn_bwd(sv['qx'], sv['kvx'], sv['kvx'], sv['ox_m'], dox, sv['lse_x'], name=nm('xa_bwd'),
                                 **sv['xa'])
    dkvx = jnp.concatenate([dkx, dvx], axis=1).astype(BF16)
    gw['w_xq'] = _mm(sv['hx'], dqx_m, mode='tn', out_dtype=F32, name=nm('d_xq'))
    dx1, gs['g_xa'] = _mm(dqx_m, w['w_xq'], mode='nt', out_dtype=F32, name=nm('d_hx'), epilogue=_norm_bwd_epilogue,
                          col_sums=True, full_rows=True,
                          extras=[(sv['x1'], *_mn()), (dx2, *_mn()), (p['g_xa'], *_nvec())])
    gw['w_xkv'] = _mm(sv['mn'], dkvx, mode='tn', out_dtype=F32, name=nm('d_xkv'))
    dmn = _mm(dkvx, w['w_xkv'], mode='nt', out_dtype=F32, name=nm('d_mn'))
    _, gs['g_mem'] = _norm_bwd_call(mem, dmn, p['g_mem'], None, nm('d_norm_mem'))
    gw['w_out'] = _mm(sv['y'], dx1.astype(BF16), mode='tn', out_dtype=F32, name=nm('d_out'))
    dy = _mm(dx1, w['w_out'], mode='nt', out_dtype=BF16, name=nm('d_y'))
    zc, b_br = sv['zc'], p['b_branch']

    branches = (('w_up_fox', sv['of_m'], BF16), ('w_up_gla', sv['o_gla'], F32), ('w_up_mla', sv['om_m'], BF16))
    du, do_br, dzc, gs['b_branch'] = _gated_merge_bwd(dy, zc, b_br, [o for _, o, _ in branches],
                                                      [w[wn] for wn, _, _ in branches], [dt for _, _, dt in branches],
                                                      name=nm('d_merge'))
    for q, (wn, o_m, _) in enumerate(branches):
        gw[wn] = _mm(o_m, du[q], mode='tn', out_dtype=F32, name=nm(f'd_up{q}'))
    za = sv['za']
    carry_fox = None if early is None else early({nm_: gw[nm_] for nm_ in EARLY})
    dfq, dfk, dfv, dck, dcq, *carried_fox = _mattn_bwd(za, za, za, sv['o_fox'], do_br[0], sv['lse_fox'],
                                                       name=nm('fox_bwd'), comm=carry_fox, **sv['fox'])
    dcum = _padc(dck[:, :2, :].reshape(FOX_HEADS, s).T + dcq.reshape(s, 2, LANES)[:, :, :2].reshape(s, FOX_HEADS), 128)
    dlf = _cumsum_rows(dcum, reverse=True, name=nm('fox_dcum'))

    def dff_fn(dl, f, b):
        d = dl * _sig(-(f + b))
        return d, d

    dff, db_fox = _rowwise(dff_fn, [dlf, sv['ff']], [p['b_fox']], [(128, F32)], [128], name=nm('fox_dff'))
    gs['b_fox'] = db_fox
    dza = jnp.concatenate([dfq, dfk, dfv], axis=1).astype(BF16)
    dqn, dkn, dvv, dq_rope, dk_rope, *carried_mla = _mattn_bwd(sv['qall'], sv['kvp'], sv['kvp'], sv['o_mla'], do_br[2],
                                                               sv['lse_mla'], name=nm('mla_bwd'), comm=carry_mla,
                                                               **sv['mla'])

    def drope_fn(dn, dq, dk, c4v, s4v):
        return jnp.concatenate([dn, dq * c4v, dq * s4v], axis=1), jnp.concatenate([dk * c4v, dk * s4v], axis=1)

    dqp, dmkr2 = _rowwise(drope_fn, [dqn, dq_rope, dk_rope, c4, s4], [], [(512, BF16), (256, BF16)], name=nm('d_rope'))
    dkvp = jnp.concatenate([dkn, dvv], axis=1).astype(BF16)
    gw['uq'] = _mm(sv['cqn'], dqp, mode='tn', out_dtype=F32, name=nm('d_uq'))
    dcqn = _mm(dqp, w['uq'], mode='nt', out_dtype=F32, name=nm('d_cqn'))
    gw['ukv'] = _mm(sv['ckvn'], dkvp, mode='tn', out_dtype=F32, name=nm('d_ukv'))
    dckvn = _mm(dkvp, w['ukv'], mode='nt', out_dtype=F32, name=nm('d_ckvn'))
    dmq, gs['g_mla_q'] = _norm_bwd_call(sv['mq'], dcqn, p['g_mla_q'], None, nm('d_norm_q'))
    dmkv, gs['g_mla_kv'] = _norm_bwd_call(sv['mkv'], dckvn, p['g_mla_kv'], None, nm('d_norm_kv'))
    doraw, dgr, gs['g_gla_out'] = _rowwise(_gla_out_bwd, [sv['oraw'], sv['gr'], do_br[1]], [p['g_gla_out']],
                                           [(512, F32), (512, BF16)], [128], name=nm('d_gla_out'))
    st = sv['states']
    st_prev = jnp.concatenate([jnp.zeros_like(st[:, :1]), st[:, :-1]], axis=1)
    dgq, dgk, dgv, dla = _gla_bwd(sv['zb'], sv['la'], st, st_prev, doraw, name=nm('gla_bwd'), **sv['gla'])

    def dgate_fn(dl, gl, wg, bg):
        pre = _dot(gl.astype(BF16), wg) + bg
        dpre = dl * (1.0 / GLA_TAU) * _sig(-pre)
        return dpre, _dot(dpre.astype(BF16), wg, NT), dpre

    dpre, dglow, gs['b_gla'] = _rowwise(dgate_fn, [dla, sv['glow']], [w['gate'], p['b_gla']],
                                        [(256, BF16), (128, BF16)], [256], name=nm('d_gla_gate'))
    gw['gate'] = _mm(sv['glow'], dpre, mode='tn', out_dtype=F32, name=nm('d_wgate'))
    bf = lambda t: t.astype(BF16)
    dzb = jnp.concatenate([dgr, bf(dgq), bf(dgk), bf(dgv), bf(dmq), dmkr2, bf(dff), dglow, bf(dmkv),
                           jnp.zeros((s, B_W - B_END), BF16)], axis=1)
    h = sv['h']
    gw['in_a'] = _mm(h, dza, mode='tn', out_dtype=F32, name=nm('d_in_a'))
    gw['in_b'] = _mm(h, dzb, mode='tn', out_dtype=F32, name=nm('d_in_b'))
    gw['in_c'] = _mm(h, dzc, mode='tn', out_dtype=F32, name=nm('d_in_c'))
    add = lambda acc, prev: prev + acc
    dh = _mm(dza, w['in_a'], mode='nt', out_dtype=F32, name=nm('d_h_a'))
    dh = _mm(dzb, w['in_b'], mode='nt', out_dtype=F32, name=nm('d_h_b'), epilogue=add, extras=[(dh, *_mn())])
    dx0, gs['g_mix'] = _mm(dzc, w['in_c'], mode='nt', out_dtype=F32, name=nm('d_h_c'), col_sums=True, full_rows=True,
                           epilogue=lambda acc, prev, xv, rv, gv: _norm_bwd_epilogue(prev + acc, xv, rv, gv),
                           extras=[(dh, *_mn()), (sv['x0'], *_mn()), (dx1, *_mn()), (p['g_mix'], *_nvec())])
    return dx0, gw, gs, (carried_mla or [None])[0], (carried_fox or [None])[0]


def _loss_head(x, target, g_final):
    d = x.shape[1]

    def fn(xv, tv, gv):
        r = lax.rsqrt(jnp.mean(xv * xv, axis=-1, keepdims=True) + EPS)
        xh = xv * r
        e = xh * gv - tv
        dy = e * (1.0 / d)
        gd = dy * gv
        dx = r * (gd - xh * jnp.mean(gd * xh, axis=-1, keepdims=True))
        row_loss = 0.5 * jnp.mean(e * e, axis=-1, keepdims=True)
        return dx, dy * xh, jnp.broadcast_to(row_loss, (xv.shape[0], LANES))

    return _rowwise(fn, [x, target], [g_final], [(d, F32)], [d, LANES], name='loss_head')


def _small_sizes(shapes):
    return [math.prod(shapes[nm]) for nm in SMALL]


def _step(args):
    shapes = {nm: args[nm].shape for nm in ORDER}
    x, mem, target = args['x'][0], args['mem'][0], args['loss_target'][0]
    s = x.shape[0]

    def wire(nm, l):
        w = args[nm][l].astype(BF16)
        if nm == 'w_in':
            w = jnp.pad(w, ((0, 0), (0, WIN_PAD - WIN_SHARD)))
        if nm == 'w_gla_gate':
            w = jnp.pad(w, ((0, GATE_WIRE_ROWS - GLA_RANK), (0, 0)))
        return w

    axis_of = dict(BIG)
    names = tuple(nm for nm, _ in BIG)
    wires = lambda l, nms: [wire(nm, l) for nm in nms]
    width = lambda nm: WIN_PAD if nm == 'w_in' else args[nm].shape[2]
    side_by_side = lambda nms: [axis_of[nm] == 2 and width(nm) % LANES == 0 for nm in nms]
    over_ici = lambda l, nms: _gather_over_ici(wires(l, nms), side_by_side(nms))

    def whole(parts, nms, tag):
        side = side_by_side(nms)
        parts = _run_comm(_gather_over_d2d(parts, side), name=f'gather_d2d_{tag}', alias=True)
        full = {nm: p if sd else _full_layer(p, axis_of[nm]) for nm, p, sd in zip(nms, parts, side)}
        if 'w_gla_gate' in full:
            full['w_gla_gate'] = full['w_gla_gate'][:GLA_RANK]
        return full

    tabs = _rope_tables(s)
    layers_p = []
    for l in range(DEPTH):
        layers_p.append({
            'g_mix': args['g_mix'][l][None], 'b_fox': _padc(args['b_fox_forget'][l][None], 128),
            'b_gla': args['b_gla_gate'][l][None], 'g_gla_out': args['g_gla_out'][l][None],
            'g_mla_q': args['g_mla_q'][l][None], 'g_mla_kv': args['g_mla_kv'][l][None],
            'b_branch': args['b_branch_gate'][l][None], 'g_xa': args['g_xa'][l][None],
            'g_mem': args['g_mem'][l][None], 'g_mlp': args['g_mlp'][l][None]})

    first = _run_comm(over_ici(0, LATE), name='gather_ici_first_l0')
    w_now = _repack_layer_weights(whole(first, LATE, 'first_l0'))
    saved = []
    xl = x
    for l in range(DEPTH):
        carry_fox = over_ici(0, EARLY) if l == 0 else None
        after_fox = (lambda parts: whole(parts, EARLY, 'rest_l0')) if l == 0 else None
        carry_mla = over_ici(l + 1, names) if l + 1 < DEPTH else None
        xl, sv = _layer_fwd(xl, mem, w_now, layers_p[l], tabs, f'l{l}', carry_fox=carry_fox, after_fox=after_fox,
                            carry_mla=carry_mla)
        saved.append(sv)
        if carry_mla is not None:
            w_now = _repack_layer_weights(whole(sv.pop('carried_mla'), names, f'l{l + 1}'))
    dx, dg_final, loss_lanes = _loss_head(xl, target, args['g_final'][None])
    cidx = lax.axis_index('c')
    chip = 2 * lax.axis_index('x') + lax.axis_index('y')

    def pair_sums(gw, nms, tag):
        mine, theirs = [], []
        for nm in nms:
            shards = _split_full(gw[nm], axis_of[nm]).astype(BF16)
            h = shards.shape[1] // 2
            mine.append(lax.dynamic_slice_in_dim(shards, cidx * h, h, axis=1))
            theirs.append(lax.dynamic_slice_in_dim(shards, (1 - cidx) * h, h, axis=1))
        got = _to_sibling(theirs, name=f'grads_swap_{tag}')
        pairs = []
        for nm, a, b in zip(nms, mine, got):
            _, h, n = a.shape
            (p,) = _rowwise(lambda u, v: (u.astype(F32) + v.astype(F32),),
                            [a.reshape(N_CHIPS * h, n), b.reshape(N_CHIPS * h, n)], [], [(n, BF16)],
                            name=f'pair_sum_{nm}_{tag}')
            pairs.append(p.reshape(N_CHIPS, h, n))
        return pairs

    def finish(pairs, from_chips, nms, tag):
        own = [lax.dynamic_index_in_dim(p, chip, axis=0, keepdims=False) for p in pairs]
        mine = [_sum_chips(o, r, name=f'chip_sum_{nm}_{tag}') for nm, o, r in zip(nms, own, from_chips)]
        theirs = _to_sibling(mine, name=f'grads_join_{tag}')
        return {nm: jnp.where(cidx == 0, jnp.concatenate([a, b]), jnp.concatenate([b, a]))
                for nm, a, b in zip(nms, mine, theirs)}

    gs_layers, done = [None] * DEPTH, [{} for _ in range(DEPTH)]
    above = None
    for l in reversed(range(DEPTH)):
        lowest, early_pairs = l == 0, []

        def early(gw_early, l=l, early_pairs=early_pairs):
            early_pairs.extend(pair_sums(gw_early, EARLY, f'early_l{l}'))
            return _chip_exchange(early_pairs)

        carry_mla = None if above is None else _chip_exchange(above[1])
        dx, gw, gs_layers[l], got_mla, got_fox = _layer_bwd(
            dx, mem, saved[l]['w'], layers_p[l], tabs, saved[l], f'l{l}', carry_mla=carry_mla,
            early=early if lowest else None)
        if above is not None:
            done[above[0]].update(finish(above[1], got_mla, names, f'l{above[0]}'))
        grads = _unpack_layer_grads(gw)
        if lowest:
            done[l].update(finish(early_pairs, got_fox, EARLY, f'early_l{l}'))
            late_pairs = pair_sums(grads, LATE, f'late_l{l}')
            from_late = _run_comm(_chip_exchange(late_pairs), name=f'grads_exchange_late_l{l}')
            done[l].update(finish(late_pairs, from_late, LATE, f'late_l{l}'))
        else:
            above = (l, pair_sums(grads, names, f'l{l}'))
    grad_x = dx[None]
    gshard = {nm: jnp.stack([done[l][nm] for l in range(DEPTH)]) for nm in names}

    small_g = []
    for nm, key in (('g_mix', 'g_mix'), ('b_fox_forget', 'b_fox'), ('b_gla_gate', 'b_gla'),
                    ('g_gla_out', 'g_gla_out'), ('g_mla_q', 'g_mla_q'), ('g_mla_kv', 'g_mla_kv'),
                    ('b_branch_gate', 'b_branch'), ('g_xa', 'g_xa'), ('g_mem', 'g_mem'), ('g_mlp', 'g_mlp')):
        width = shapes[nm][1]
        small_g.append(jnp.concatenate([gs_layers[l][key][0, :width] for l in range(DEPTH)]))
    small_g.append(dg_final[0])
    small_g.append(loss_lanes[0, :1])
    flat = jnp.concatenate(small_g)
    n_small = flat.shape[0]
    srows = -(-n_small // (8 * LANES)) * 8
    pad = lambda v: jnp.pad(v, (0, srows * LANES - v.shape[0])).reshape(srows, LANES)
    all_small = _all_gather8(pad(flat), name='gather_small')
    sw, sm, svv = (pad(jnp.concatenate([args[pre + nm].reshape(-1) for nm in SMALL] + [jnp.zeros((1,), F32)]))
                   for pre in ('', 'm_', 'v_'))

    def small_body(g_ref, w_ref, m_ref, v_ref, go_ref, d_ref, mo_ref, vo_ref):
        g = g_ref[0]
        for q in range(1, N_DEV):
            g = g + g_ref[q]
        go_ref[...] = g
        d_ref[...], mo_ref[...], vo_ref[...] = _adam(w_ref[...], g, m_ref[...], v_ref[...])

    sg, sd, snm, snv = pl.pallas_call(
        small_body, name='small_sum_adam', out_shape=[jax.ShapeDtypeStruct((srows, LANES), F32)] * 4,
        compiler_params=pltpu.CompilerParams(vmem_limit_bytes=VMEM_LIMIT))(all_small, sw, sm, svv)

    def unsmall(buf):
        v, out, off = buf.reshape(-1), {}, 0
        for nm in SMALL:
            nel = math.prod(shapes[nm])
            out[nm] = v[off:off + nel].reshape(shapes[nm])
            off += nel
        return out, v[off]

    res = {}
    (res['grad'], loss), (res['delta'], _), (res['m'], _), (res['v'], _) = (unsmall(t) for t in (sg, sd, snm, snv))

    for nm, _ in BIG:
        shp = args[nm].shape
        view = lambda t: t.reshape(shp[0] * shp[1], shp[2])
        d, m2, v2 = _rowwise(_adam, [view(args[nm]), view(gshard[nm]), view(args['m_' + nm]), view(args['v_' + nm])],
                             [], [(shp[2], F32)] * 3, name=f'adam_{nm}')
        res['grad'][nm], res['delta'][nm], res['m'][nm], res['v'][nm] = (
            gshard[nm], d.reshape(shp), m2.reshape(shp), v2.reshape(shp))

    return (loss, grad_x, *[res['grad'][nm] for nm in ORDER], *[res['delta'][nm] for nm in ORDER],
            *[res['m'][nm] for nm in ORDER], *[res['v'][nm] for nm in ORDER])


def kernel(x, mem, g_mix, w_in, b_fox_forget, w_gla_gate, b_gla_gate, g_gla_out, g_mla_q, w_mla_uq, g_mla_kv, w_mla_ukv, b_branch_gate, w_up_fox, w_up_gla, w_up_mla, w_out, g_xa, g_mem, w_xq, w_xkv, w_xo, g_mlp, w_mlp1, w_mlp2, g_final, loss_target, m_g_mix, m_w_in, m_b_fox_forget, m_w_gla_gate, m_b_gla_gate, m_g_gla_out, m_g_mla_q, m_w_mla_uq, m_g_mla_kv, m_w_mla_ukv, m_b_branch_gate, m_w_up_fox, m_w_up_gla, m_w_up_mla, m_w_out, m_g_xa, m_g_mem, m_w_xq, m_w_xkv, m_w_xo, m_g_mlp, m_w_mlp1, m_w_mlp2, m_g_final, v_g_mix, v_w_in, v_b_fox_forget, v_w_gla_gate, v_b_gla_gate, v_g_gla_out, v_g_mla_q, v_w_mla_uq, v_g_mla_kv, v_w_mla_ukv, v_b_branch_gate, v_w_up_fox, v_w_up_gla, v_w_up_mla, v_w_out, v_g_xa, v_g_mem, v_w_xq, v_w_xkv, v_w_xo, v_g_mlp, v_w_mlp1, v_w_mlp2, v_g_final):
    return _step(dict(locals()))
```

```python
import functools
import math
import typing

import jax
import jax.numpy as jnp
from jax import lax
from jax.experimental import pallas as pl
from jax.experimental.pallas import tpu as pltpu

F32 = jnp.float32
BF16 = jnp.bfloat16
MESH = pl.DeviceIdType.MESH

D_MODEL = 1024
DEPTH = 2
CHUNK = 64
EPS = 1e-6
FOX_HEADS, FOX_HD = 4, 64
GLA_HEADS, GLA_DK, GLA_DV, GLA_RANK, GLA_TAU = 4, 64, 128, 16, 16.0
MLA_HEADS, MLA_Q_RANK, MLA_KV_RANK, MLA_NOPE, MLA_ROPE, MLA_VD = 4, 256, 128, 64, 32, 64
ROPE_BASE = 10000.0
XA_HEADS, XA_HD = 4, 128
D_FF = 4 * D_MODEL
IN_SIZES = (256, 256, 256, 4, 256, 256, 512, 16, 512, 256, 128, 32, 3072)
N_IN = sum(IN_SIZES)

ADAM_LR, ADAM_B1, ADAM_B2, ADAM_EPS, ADAM_WD, ADAM_STEP = 0.001, 0.9, 0.999, 1e-08, 0.01, 10

N_CHIPS = 4
N_DEV = 8
LANES = 128
VMEM_LIMIT = 48 * 1024 * 1024
MASK_VALUE = -1e30

BIG = (('w_in', 2), ('w_gla_gate', 2), ('w_mla_uq', 2), ('w_mla_ukv', 2), ('w_up_fox', 2), ('w_up_gla', 2),
       ('w_up_mla', 2), ('w_out', 1), ('w_xq', 1), ('w_xkv', 1), ('w_xo', 2), ('w_mlp1', 2), ('w_mlp2', 1))
SMALL = ('g_mix', 'b_fox_forget', 'b_gla_gate', 'g_gla_out', 'g_mla_q', 'g_mla_kv', 'b_branch_gate',
         'g_xa', 'g_mem', 'g_mlp', 'g_final')
ORDER = ('g_mix', 'w_in', 'b_fox_forget', 'w_gla_gate', 'b_gla_gate', 'g_gla_out', 'g_mla_q', 'w_mla_uq',
         'g_mla_kv', 'w_mla_ukv', 'b_branch_gate', 'w_up_fox', 'w_up_gla', 'w_up_mla', 'w_out', 'g_xa', 'g_mem',
         'w_xq', 'w_xkv', 'w_xo', 'g_mlp', 'w_mlp1', 'w_mlp2', 'g_final')


def _params(*sem):
    return pltpu.CompilerParams(dimension_semantics=sem, vmem_limit_bytes=VMEM_LIMIT)


def _sig(x):
    return 1.0 / (1.0 + jnp.exp(-x))


def _logsig(x):
    return jnp.minimum(x, 0.0) - jnp.log(1.0 + jnp.exp(-jnp.abs(x)))


NN = (((1,), (0,)), ((), ()))
NT = (((1,), (1,)), ((), ()))
TN = (((0,), (0,)), ((), ()))


def _dot(a, b, dims=NN):
    return lax.dot_general(a, b, dims, preferred_element_type=F32)


class Cols(typing.NamedTuple):
    arr: jax.Array
    width: int
    blk: int


def _tri_dot(tri, x):
    hi = x.astype(BF16)
    r1 = x - hi.astype(F32)
    mid = r1.astype(BF16)
    lo = (r1 - mid.astype(F32)).astype(BF16)
    return _dot(tri, hi) + _dot(tri, mid) + _dot(tri, lo)


MM_TILES = ((1024, 1024), (1024, 512), (512, 1024), (512, 512), (256, 1024), (512, 256), (256, 512), (256, 256),
            (128, 1024), (128, 128))
MM_VMEM_BUDGET = 38 * 1024 * 1024


def _mm_tiles(m, n, k, a_bytes, b_bytes, out_bytes, ex_bytes, has_norm, emit_norm, has_fn, full_rows):
    for tm, tn in MM_TILES:
        tm, tn = min(tm, m), min(tn, n)
        if m % tm or n % tn or (full_rows and tn != n):
            continue
        blocks = tm * k * a_bytes + k * tn * b_bytes + tm * tn * (out_bytes + ex_bytes) + (tm * k * 2 if emit_norm else 0)
        temps = tm * tn * 4 + (tm * k * 2 if has_norm else 0) + (tm * k * 6 if has_fn or has_norm else 0)
        if 2 * blocks + temps <= MM_VMEM_BUDGET:
            return tm, tn
    raise ValueError((m, n, k))


def _mm(a, b, *, mode, out_dtype, name, norm_g=None, emit_norm=False, a_fn=None, extras=(), epilogue=None,
        col_sums=False, full_rows=False):
    a_blk = 0
    if isinstance(a, Cols):
        a, width, a_blk = a
        a_shape = (a.shape[0], width)
    else:
        a_shape = a.shape
    if mode == 'tn':
        k, m = a_shape
    else:
        m, k = a_shape
    n = b.shape[0] if mode == 'nt' else b.shape[1]
    assert (b.shape[1] if mode == 'nt' else b.shape[0]) == k, (name, a.shape, b.shape)
    has_norm = norm_g is not None
    ex_bytes = sum(arr.dtype.itemsize for arr, kind, _ in extras if kind == 'mn')
    tm, tn = _mm_tiles(m, n, k, a.dtype.itemsize, b.dtype.itemsize, jnp.dtype(out_dtype).itemsize, ex_bytes, has_norm,
                       emit_norm, a_fn is not None, full_rows)
    assert all(col % tn == 0 for _, _, col in extras), (name, tn)
    assert a_blk == 0 or (mode == 'nn') or (mode == 'tn' and tm == m)
    assert not (col_sums and (has_norm or emit_norm))
    ij = (lambda f: lambda g0, g1: f(g1, g0)) if col_sums else (lambda f: f)
    spec = lambda blk, f: pl.BlockSpec(blk, ij(f))
    if mode == 'tn':
        a_spec = spec((k, tm), lambda i, j: (0, i + a_blk))
    else:
        a_spec = spec((tm, k), lambda i, j: (i, a_blk))
    b_spec = spec((tn, k), lambda i, j: (j, 0)) if mode == 'nt' else spec((k, tn), lambda i, j: (0, j))
    dims = {'nn': NN, 'nt': NT, 'tn': TN}[mode]
    assert not (has_norm and mode != 'nn')
    n_ex = len(extras)

    def body(*refs):
        a_ref, b_ref = refs[0], refs[1]
        pos = 2
        g_ref = None
        if has_norm:
            g_ref = refs[pos]
            pos += 1
        ex_refs = refs[pos:pos + n_ex]
        pos += n_ex
        o_ref = refs[pos]
        pos += 1
        h_ref = None
        if emit_norm:
            h_ref = refs[pos]
            pos += 1
        if has_norm:
            an_ref = refs[pos]

            @pl.when(pl.program_id(1) == 0)
            def _():
                xf = a_ref[...].astype(F32)
                y = xf * lax.rsqrt(jnp.mean(xf * xf, axis=-1, keepdims=True) + EPS) * g_ref[...]
                an_ref[...] = y.astype(BF16)
                if emit_norm:
                    h_ref[...] = y.astype(BF16)

            av = an_ref[...]
        else:
            av = a_ref[...]
            if a_fn is not None:
                av = a_fn(av)
            av = av.astype(BF16)
        acc = _dot(av, b_ref[...].astype(BF16), dims)
        if epilogue is not None:
            acc = epilogue(acc, *[r[...] for r in ex_refs])
        acc, to_sum = acc if isinstance(acc, tuple) else (acc, acc)
        o_ref[...] = acc.astype(out_dtype)
        if col_sums:
            sum_ref = refs[pos]

            @pl.when(pl.program_id(1) == 0)
            def _():
                sum_ref[...] = jnp.zeros_like(sum_ref)

            sum_ref[...] += jnp.sum(to_sum, axis=0, keepdims=True)

    in_specs = [a_spec, b_spec]
    args = [a, b]
    if has_norm:
        in_specs.append(pl.BlockSpec((1, k), lambda i, j: (0, 0)))
        args.append(norm_g)
    for arr, kind, col in extras:
        if kind == 'mn':
            in_specs.append(spec((tm, tn), lambda i, j, o=col // tn: (i, j + o)))
        else:
            in_specs.append(spec((1, tn), lambda i, j, o=col // tn: (0, j + o)))
        args.append(arr)
    out_shape = [jax.ShapeDtypeStruct((m, n), out_dtype)]
    out_specs = [spec((tm, tn), lambda i, j: (i, j))]
    if emit_norm:
        out_shape.append(jax.ShapeDtypeStruct((m, k), BF16))
        out_specs.append(pl.BlockSpec((tm, k), lambda i, j: (i, 0)))
    if col_sums:
        out_shape.append(jax.ShapeDtypeStruct((1, n), F32))
        out_specs.append(spec((1, tn), lambda i, j: (0, j)))
    scratch = [pltpu.VMEM((tm, k), BF16)] if has_norm else []
    grid = (n // tn, m // tm) if col_sums else (m // tm, n // tn)
    res = pl.pallas_call(
        body, name=name, grid=grid, in_specs=in_specs, out_specs=out_specs, out_shape=out_shape,
        scratch_shapes=scratch, compiler_params=_params('arbitrary', 'arbitrary'))(*args)
    return res if emit_norm or col_sums else res[0]


def _gated_merge(outs, ups, zg, bias, *, name, tm=1024, tn=512):
    s, n, nq = zg.shape[0], ups[0].shape[1], len(outs)
    tm, tn = min(tm, s), min(tn, n)
    per = n // tn

    def body(*refs):
        y = None
        for q in range(nq):
            o_ref, w_ref, z_ref, b_ref = refs[q], refs[nq + q], refs[2 * nq + q], refs[3 * nq + q]
            term = _sig(z_ref[...].astype(F32) + b_ref[...]) * _dot(o_ref[...], w_ref[...])
            y = term if y is None else y + term
        refs[4 * nq][...] = y.astype(BF16)

    in_specs = [pl.BlockSpec((tm, o.shape[1]), lambda i, j: (i, 0)) for o in outs]
    in_specs += [pl.BlockSpec((u.shape[0], tn), lambda i, j: (0, j)) for u in ups]
    in_specs += [pl.BlockSpec((tm, tn), lambda i, j, q=q: (i, j + q * per)) for q in range(nq)]
    in_specs += [pl.BlockSpec((1, tn), lambda i, j, q=q: (0, j + q * per)) for q in range(nq)]
    return pl.pallas_call(body, name=name, grid=(s // tm, per), in_specs=in_specs,
                          out_specs=pl.BlockSpec((tm, tn), lambda i, j: (i, j)),
                          out_shape=jax.ShapeDtypeStruct((s, n), BF16),
                          compiler_params=_params('arbitrary', 'arbitrary'))(*outs, *ups, *[zg] * nq, *[bias] * nq)


def _gated_merge_bwd(dy, zg, bias, outs, ups, do_dtypes, *, name, tm=512):
    s, n = dy.shape
    nq = len(outs)
    tm = min(tm, s)

    def body(*refs):
        dy_ref, zg_ref, b_ref = refs[:3]
        o_refs, w_refs = refs[3:3 + nq], refs[3 + nq:3 + 2 * nq]
        du_refs, do_refs = refs[3 + 2 * nq:3 + 3 * nq], refs[3 + 3 * nq:3 + 4 * nq]
        dz_ref, db_ref = refs[3 + 4 * nq:]

        @pl.when(pl.program_id(0) == 0)
        def _():
            db_ref[...] = jnp.zeros_like(db_ref)

        d = dy_ref[...].astype(F32)
        for q in range(nq):
            cols = slice(q * n, (q + 1) * n)
            g = _sig(zg_ref[:, cols].astype(F32) + b_ref[:, cols])
            du = (d * g).astype(BF16)
            du_refs[q][...] = du
            do_refs[q][...] = _dot(du, w_refs[q][...], NT).astype(do_dtypes[q])
            dz = d * _dot(o_refs[q][...], w_refs[q][...]) * g * (1.0 - g)
            dz_ref[:, cols] = dz.astype(BF16)
            db_ref[:, cols] += jnp.sum(dz, axis=0, keepdims=True)

    row = lambda w: pl.BlockSpec((tm, w), lambda i: (i, 0))
    whole = lambda a: pl.BlockSpec(a.shape, lambda i: (0, 0))
    in_specs = [row(n), row(nq * n), whole(bias)] + [row(o.shape[1]) for o in outs] + [whole(u) for u in ups]
    out_specs = [row(n)] * nq + [row(o.shape[1]) for o in outs] + [row(nq * n), pl.BlockSpec((1, nq * n), lambda i: (0, 0))]
    out_shape = ([jax.ShapeDtypeStruct((s, n), BF16)] * nq
                 + [jax.ShapeDtypeStruct((s, o.shape[1]), dt) for o, dt in zip(outs, do_dtypes)]
                 + [jax.ShapeDtypeStruct((s, nq * n), BF16), jax.ShapeDtypeStruct((1, nq * n), F32)])
    res = pl.pallas_call(body, name=name, grid=(s // tm,), in_specs=in_specs, out_specs=out_specs, out_shape=out_shape,
                         compiler_params=_params('arbitrary'))(dy, zg, bias, *outs, *ups)
    return res[:nq], res[nq:2 * nq], res[2 * nq], res[2 * nq + 1]


def _mn(col_off=0):
    return 'mn', col_off


def _nvec(col_off=0):
    return 'n', col_off


def _rowwise(fn, rows, consts, outs, sums=(), *, name, ts=256):
    views = [x if isinstance(x, Cols) else Cols(x, x.shape[1], 0) for x in rows]
    rows = [v.arr for v in views]
    r = rows[0].shape[0]
    ts = min(ts, r)
    assert r % ts == 0, (name, r, ts)
    nr, nc, no, ns = len(rows), len(consts), len(outs), len(sums)

    def body(*refs):
        vals = fn(*[x[...] for x in refs[:nr + nc]])
        for q in range(no):
            refs[nr + nc + q][...] = vals[q].astype(outs[q][1])
        if ns:
            @pl.when(pl.program_id(0) == 0)
            def _():
                for q in range(ns):
                    refs[nr + nc + no + q][...] = jnp.zeros((1, sums[q]), F32)

            for q in range(ns):
                refs[nr + nc + no + q][...] += jnp.sum(vals[no + q].astype(F32), axis=0, keepdims=True)

    in_specs = [pl.BlockSpec((ts, v.width), lambda i, blk=v.blk: (i, blk)) for v in views]
    in_specs += [pl.BlockSpec(x.shape, lambda i, nd=x.ndim: (0,) * nd) for x in consts]
    out_specs = [pl.BlockSpec((ts, w), lambda i: (i, 0)) for w, _ in outs]
    out_specs += [pl.BlockSpec((1, w), lambda i: (0, 0)) for w in sums]
    out_shape = [jax.ShapeDtypeStruct((r, w), dt) for w, dt in outs]
    out_shape += [jax.ShapeDtypeStruct((1, w), F32) for w in sums]
    return pl.pallas_call(body, name=name, grid=(r // ts,), in_specs=in_specs, out_specs=out_specs,
                          out_shape=out_shape, compiler_params=_params('arbitrary'))(*rows, *consts)


def _cumsum_rows(x, *, reverse, name, bs=256):
    s, w = x.shape
    bs = min(bs, s)
    nb = s // bs

    def body(x_ref, o_ref, carry):
        @pl.when(pl.program_id(0) == 0)
        def _():
            carry[...] = jnp.zeros_like(carry)

        r = lax.broadcasted_iota(jnp.int32, (bs, bs), 0)
        c = lax.broadcasted_iota(jnp.int32, (bs, bs), 1)
        tri = jnp.where((c >= r) if reverse else (c <= r), 1.0, 0.0).astype(BF16)
        xv = x_ref[...]
        o_ref[...] = _tri_dot(tri, xv) + carry[...]
        carry[...] += jnp.sum(xv, axis=0, keepdims=True)

    imap = (lambda i: (nb - 1 - i, 0)) if reverse else (lambda i: (i, 0))
    return pl.pallas_call(body, name=name, grid=(nb,), in_specs=[pl.BlockSpec((bs, w), imap)],
                          out_specs=pl.BlockSpec((bs, w), imap), out_shape=jax.ShapeDtypeStruct((s, w), F32),
                          scratch_shapes=[pltpu.VMEM((1, w), F32)], compiler_params=_params('arbitrary'))(x)


def _mask(mode, q0, k0, bq, bk):
    qpos = q0 + lax.broadcasted_iota(jnp.int32, (bq, bk), 0)
    kpos = k0 + lax.broadcasted_iota(jnp.int32, (bq, bk), 1)
    if mode == 'causal':
        return kpos <= qpos
    return kpos < (jnp.right_shift(qpos, int(math.log2(CHUNK))) + 1) * CHUNK


ROPE_SHIFT = int(math.log2(MLA_ROPE))
FOX_SCALE, MLA_SCALE, XA_SCALE = FOX_HD ** -0.5, (MLA_NOPE + MLA_ROPE) ** -0.5, XA_HD ** -0.5
ATTN_ROW_SLAB = 512


def _lane_masks(g, b, rope):
    lane = lax.broadcasted_iota(jnp.int32, (1, LANES), 1)
    heads = [None if g == 1 else (lane >= hh * (LANES // g)) & (lane < (hh + 1) * (LANES // g)) for hh in range(g)]
    ropes = [jnp.right_shift(lane, ROPE_SHIFT) == b * g + hh for hh in range(g)] if rope else [None] * g
    return heads, ropes


def _sel(mask, x):
    return x if mask is None else jnp.where(mask, x, jnp.zeros_like(x))


class Step(typing.NamedTuple):
    qi: typing.Any
    kj: typing.Any
    first: typing.Any
    last: typing.Any
    plain: typing.Any
    masked: typing.Any


def _fwd_steps(tri, nq, nk):
    if not tri:
        return (nq, nk), lambda i, j: Step(i, j, j == 0, j == nk - 1, True, False)
    if nq % 2:
        return (nq, nk), lambda i, j: Step(i, jnp.minimum(i, j), j == 0, j == nk - 1, j < i, j == i)

    def at(i, t):
        low = t <= i
        diag = (t == i) | (t == nq)
        return Step(jnp.where(low, i, nq - 1 - i), jnp.where(low, t, t - (i + 1)), (t == 0) | (t == i + 1), diag,
                    jnp.logical_not(diag), diag)

    return (nq // 2, nq + 1), at


def _bwd_steps(tri, nq, nk):
    if not tri:
        return (nk, nq), lambda j, i: Step(i, j, i == 0, i == nq - 1, True, False)
    if nk % 2:
        return (nk, nq), lambda j, i: Step(jnp.maximum(i, j), j, i == 0, i == nq - 1, i > j, i == j)

    def at(j, t):
        n1 = nq - j
        low = t < n1
        diag = (t == 0) | (t == n1)
        return Step(jnp.where(low, j + t, nk - 1 - j + t - n1), jnp.where(low, j, nk - 1 - j), diag,
                    (t == n1 - 1) | (t == nq), jnp.logical_not(diag), diag)

    return (nk // 2, nq + 1), at


def _carried(comm, refs, n_in, n_out):
    ci, co = len(comm.ins), len(comm.out_shapes)
    ins = refs[n_in:n_in + ci]
    outs = refs[n_in + ci + n_out:n_in + ci + n_out + co]
    rest = refs[:n_in] + refs[n_in + ci:n_in + ci + n_out] + refs[n_in + ci + n_out + co:-2]
    return rest, (ins, outs, refs[-2], refs[-1])


def _mattn_fwd(q, k, v, *, qc, kc, vc, nb, g, mode, name, dq_scale=1.0, ck=None, qr=None, qrc=0, kr=None, blk=512,
               comm=None):
    s, t = q.shape[0], k.shape[0]
    bq, bk = min(blk, s), min(blk, t)
    nq, nk = s // bq, t // bk
    tri = mode != 'full'
    bias, rope = ck is not None, qr is not None
    assert not tri or (bq == bk and bq % CHUNK == 0)
    rs = min(ATTN_ROW_SLAB, bq)
    n_in = 3 + bias + 2 * rope
    (n1, n2), step_at = _fwd_steps(tri, nq, nk)

    def body(*refs):
        refs = list(refs)
        b, p1, p2 = pl.program_id(0), pl.program_id(1), pl.program_id(2)
        st = step_at(p1, p2)
        i, j = st.qi, st.kj
        if comm is not None:
            refs, comm_refs = _carried(comm, refs, n_in, 2)
            pl.when((b == 0) & (p1 == 0) & (p2 == 0))(lambda: comm.start(*comm_refs))
        q_ref, k_ref, v_ref = refs[:3]
        pos = 3
        ck_ref = qr_ref = kr_ref = None
        if bias:
            ck_ref = refs[pos]
            pos += 1
        if rope:
            qr_ref, kr_ref = refs[pos:pos + 2]
            pos += 2
        o_ref, lse_ref, m_s, l_s, acc_s = refs[pos:]
        heads, ropes = _lane_masks(g, b, rope)

        @pl.when(st.first)
        def _():
            m_s[...] = jnp.full_like(m_s, MASK_VALUE)
            l_s[...] = jnp.zeros_like(l_s)
            acc_s[...] = jnp.zeros_like(acc_s)

        def compute(masked):
            k2, v2 = k_ref[...], v_ref[...]
            for r in range(bq // rs):
                rows = pl.ds(r * rs, rs)
                q2 = q_ref[rows, :]
                alphas, pvs = [], []
                for hh in range(g):
                    sc = _dot(_sel(heads[hh], q2), k2, NT)
                    if rope:
                        sc = sc + _dot(_sel(ropes[hh], qr_ref[rows, :]), kr_ref[...], NT)
                    if bias:
                        sc = sc - ck_ref[0, hh:hh + 1, :]
                    if masked:
                        sc = jnp.where(_mask(mode, i * bq + r * rs, j * bk, rs, bk), sc, MASK_VALUE)
                    m_prev = m_s[hh, rows]
                    m_new = jnp.maximum(m_prev, jnp.max(sc, axis=1, keepdims=True))
                    alpha = jnp.exp(m_prev - m_new)
                    p = jnp.exp(sc - m_new)
                    l_s[hh, rows] = alpha * l_s[hh, rows] + jnp.sum(p, axis=1, keepdims=True)
                    m_s[hh, rows] = m_new
                    alphas.append(alpha)
                    pvs.append(_dot(p.astype(BF16), _sel(heads[hh], v2)))
                alpha = alphas[0]
                for hh in range(1, g):
                    alpha = jnp.where(heads[hh], alphas[hh], alpha)
                acc_s[rows, :] = acc_s[rows, :] * alpha + sum(pvs[1:], pvs[0])

        if tri:
            pl.when(st.plain)(functools.partial(compute, False))
            pl.when(st.masked)(functools.partial(compute, True))
        else:
            compute(False)

        @pl.when(st.last)
        def _():
            lane = lax.broadcasted_iota(jnp.int32, (bq, LANES), 1)
            l_full, lse = l_s[0], jnp.zeros((bq, LANES), F32)
            for hh in range(g):
                if hh:
                    l_full = jnp.where(heads[hh], l_s[hh], l_full)
                lse = jnp.where(lane == hh, m_s[hh] + jnp.log(l_s[hh]), lse)
            o_ref[...] = (acc_s[...] / l_full).astype(o_ref.dtype)
            lse_ref[...] = lse

        if comm is not None:
            pl.when((b == nb - 1) & (p1 == n1 - 1) & (p2 == n2 - 1))(lambda: comm.finish(*comm_refs))

    qi = lambda p1, p2: step_at(p1, p2).qi
    kj = lambda p1, p2: step_at(p1, p2).kj
    in_specs = [pl.BlockSpec((bq, LANES), lambda b, p1, p2: (qi(p1, p2), qc + b)),
                pl.BlockSpec((bk, LANES), lambda b, p1, p2: (kj(p1, p2), kc + b)),
                pl.BlockSpec((bk, LANES), lambda b, p1, p2: (kj(p1, p2), vc + b))]
    args = [q, k, v]
    if bias:
        in_specs.append(pl.BlockSpec((1, 8, bk), lambda b, p1, p2: (b, 0, kj(p1, p2))))
        args.append(ck)
    if rope:
        in_specs += [pl.BlockSpec((bq, LANES), lambda b, p1, p2: (qi(p1, p2), qrc)),
                     pl.BlockSpec((bk, LANES), lambda b, p1, p2: (kj(p1, p2), 0))]
        args += [qr, kr]
    out = pl.BlockSpec((bq, LANES), lambda b, p1, p2: (qi(p1, p2), b))
    out_specs = [out, out]
    out_shape = [jax.ShapeDtypeStruct((s, LANES * nb), BF16), jax.ShapeDtypeStruct((s, LANES * nb), F32)]
    scratch = [pltpu.VMEM((g, bq, 1), F32), pltpu.VMEM((g, bq, 1), F32), pltpu.VMEM((bq, LANES), F32)]
    if comm is not None:
        in_specs += [ANY] * len(comm.ins)
        args += comm.ins
        out_specs += [ANY] * len(comm.out_shapes)
        out_shape += comm.out_shapes
        scratch += _sems(comm.n_sems, comm.n_sems)
    res = pl.pallas_call(body, name=name, grid=(nb, n1, n2), in_specs=in_specs, out_specs=out_specs, out_shape=out_shape,
                         scratch_shapes=scratch, compiler_params=_params('arbitrary', 'arbitrary', 'arbitrary'))(*args)
    return res if comm is None else (res[0], res[1], res[2:])


def _mattn_bwd(q, k, v, o, do, lse, *, qc, kc, vc, nb, g, mode, name, dq_scale=1.0, ck=None, qr=None, qrc=0, kr=None,
               blk=512, comm=None):
    s, t = q.shape[0], k.shape[0]
    bq, bk = min(blk, s), min(blk, t)
    nq, nk = s // bq, t // bk
    tri = mode != 'full'
    bias, rope = ck is not None, qr is not None
    rs = min(ATTN_ROW_SLAB, bq)
    n_in, n_out = 6 + bias + 2 * rope, 3 + 2 * bias + 2 * rope
    (n1, n2), step_at = _bwd_steps(tri, nq, nk)

    def body(*refs):
        refs = list(refs)
        if comm is not None:
            refs, comm_refs = _carried(comm, refs, n_in, n_out)
            first = (pl.program_id(0) == 0) & (pl.program_id(1) == 0) & (pl.program_id(2) == 0)
            pl.when(first)(lambda: comm.start(*comm_refs))
        q_ref, k_ref, v_ref, o_ref, do_ref, lse_ref = refs[:6]
        pos = 6
        ck_ref = qr_ref = kr_ref = dck_ref = dcq_ref = dqr_ref = dkr_ref = dck_s = None
        if bias:
            ck_ref = refs[pos]
            pos += 1
        if rope:
            qr_ref, kr_ref = refs[pos:pos + 2]
            pos += 2
        dq_ref, dk_ref, dv_ref = refs[pos:pos + 3]
        pos += 3
        if bias:
            dck_ref, dcq_ref = refs[pos:pos + 2]
            pos += 2
        if rope:
            dqr_ref, dkr_ref = refs[pos:pos + 2]
            pos += 2
        dk_s, dv_s = refs[pos:pos + 2]
        if bias:
            dck_s = refs[pos + 2]
        b, p1, p2 = pl.program_id(0), pl.program_id(1), pl.program_id(2)
        st = step_at(p1, p2)
        i, j = st.qi, st.kj
        heads, ropes = _lane_masks(g, b, rope)

        @pl.when((p1 == 0) & (p2 == 0))
        def _():
            dq_ref[...] = jnp.zeros_like(dq_ref)
            if bias:
                dcq_ref[...] = jnp.zeros_like(dcq_ref)

        if rope:
            @pl.when((b == 0) & (p1 == 0) & (p2 == 0))
            def _():
                dqr_ref[...] = jnp.zeros_like(dqr_ref)
                dkr_ref[...] = jnp.zeros_like(dkr_ref)

        @pl.when(st.first)
        def _():
            dk_s[...] = jnp.zeros_like(dk_s)
            dv_s[...] = jnp.zeros_like(dv_s)
            if bias:
                dck_s[...] = jnp.zeros_like(dck_s)

        def compute(masked):
            k2, v2 = k_ref[...], v_ref[...]
            lane = lax.broadcasted_iota(jnp.int32, (rs, LANES), 1)
            rk = pl.ds(pl.multiple_of(j * bk, bk), bk)
            add = lambda tot, x: x if tot is None else tot + x
            dv_t = dk_t = dkr_t = None
            dck_t = [None] * g
            for r in range(bq // rs):
                rows = pl.ds(r * rs, rs)
                rq = pl.ds(pl.multiple_of(i * bq + r * rs, rs), rs)
                q2, do2, lse2 = q_ref[rows, :], do_ref[rows, :], lse_ref[rows, :]
                dd = do2.astype(F32) * o_ref[rows, :].astype(F32)
                dq_t = dqr_t = dcq_t = None
                for hh in range(g):
                    qm = _sel(heads[hh], q2)
                    sc = _dot(qm, k2, NT)
                    if rope:
                        qrm = _sel(ropes[hh], qr_ref[rows, :])
                        sc = sc + _dot(qrm, kr_ref[...], NT)
                    if bias:
                        sc = sc - ck_ref[0, hh:hh + 1, :]
                    if masked:
                        sc = jnp.where(_mask(mode, i * bq + r * rs, j * bk, rs, bk), sc, MASK_VALUE)
                    p = jnp.exp(sc - jnp.sum(jnp.where(lane == hh, lse2, 0.0), axis=1, keepdims=True))
                    dom = _sel(heads[hh], do2)
                    dp = _dot(dom, v2, NT)
                    delta = jnp.sum(_sel(heads[hh], dd), axis=1, keepdims=True)
                    ds = p * (dp - delta)
                    dsb = ds.astype(BF16)
                    dv_t = add(dv_t, _dot(p.astype(BF16), dom, TN))
                    dk_t = add(dk_t, _dot(dsb, qm, TN))
                    dq_t = add(dq_t, _dot(dsb, _sel(heads[hh], k2)))
                    if rope:
                        dqr_t = add(dqr_t, _dot(dsb, _sel(ropes[hh], kr_ref[...])))
                        dkr_t = add(dkr_t, _dot(dsb, qrm, TN))
                    if bias:
                        dck_t[hh] = add(dck_t[hh], jnp.sum(ds, axis=0, keepdims=True))
                        dcq_t = add(dcq_t, jnp.where(lane == hh, jnp.sum(ds, axis=1, keepdims=True), 0.0))
                dq_ref[rq, :] += dq_t if dq_scale == 1.0 else dq_scale * dq_t
                if rope:
                    dqr_ref[rq, :] += dq_scale * dqr_t
                if bias:
                    dcq_ref[rq, :] += dcq_t
            dv_s[...] += dv_t
            dk_s[...] += dk_t
            if rope:
                dkr_ref[rk, :] += dkr_t
            if bias:
                for hh in range(g):
                    dck_s[hh:hh + 1, :] -= dck_t[hh]

        if tri:
            pl.when(st.plain)(functools.partial(compute, False))
            pl.when(st.masked)(functools.partial(compute, True))
        else:
            compute(False)

        @pl.when(st.last)
        def _():
            dk_ref[...] = dk_s[...]
            dv_ref[...] = dv_s[...]
            if bias:
                dck_ref[0] = dck_s[...]

        if comm is not None:
            pl.when((b == nb - 1) & (p1 == n1 - 1) & (p2 == n2 - 1))(lambda: comm.finish(*comm_refs))

    qrow = lambda col: pl.BlockSpec((bq, LANES), lambda b, p1, p2: (step_at(p1, p2).qi, col(b)))
    krow = lambda col: pl.BlockSpec((bk, LANES), lambda b, p1, p2: (step_at(p1, p2).kj, col(b)))
    in_specs = [qrow(lambda b: qc + b), krow(lambda b: kc + b), krow(lambda b: vc + b), qrow(lambda b: b),
                qrow(lambda b: b), qrow(lambda b: b)]
    args = [q, k, v, o, do, lse]
    whole = lambda rows: pl.BlockSpec((rows, LANES), lambda b, j, i: (0, b))
    out_specs = [whole(s), krow(lambda b: b), krow(lambda b: b)]
    out_shape = [jax.ShapeDtypeStruct((s, LANES * nb), F32), jax.ShapeDtypeStruct((t, LANES * nb), F32),
                 jax.ShapeDtypeStruct((t, LANES * nb), F32)]
    scratch = [pltpu.VMEM((bk, LANES), F32), pltpu.VMEM((bk, LANES), F32)]
    if bias:
        ckj = pl.BlockSpec((1, 8, bk), lambda b, p1, p2: (b, 0, step_at(p1, p2).kj))
        in_specs.append(ckj)
        args.append(ck)
        out_specs += [ckj, whole(s)]
        out_shape += [jax.ShapeDtypeStruct((nb, 8, t), F32), jax.ShapeDtypeStruct((s, LANES * nb), F32)]
    if rope:
        in_specs += [qrow(lambda b: qrc), krow(lambda b: 0)]
        args += [qr, kr]
        out_specs += [pl.BlockSpec((s, LANES), lambda b, j, i: (0, 0)), pl.BlockSpec((t, LANES), lambda b, j, i: (0, 0))]
        out_shape += [jax.ShapeDtypeStruct((s, LANES), F32), jax.ShapeDtypeStruct((t, LANES), F32)]
    if bias:
        scratch.append(pltpu.VMEM((8, bk), F32))
    if comm is not None:
        in_specs += [ANY] * len(comm.ins)
        args += comm.ins
        out_specs += [ANY] * len(comm.out_shapes)
        out_shape += comm.out_shapes
        scratch += _sems(comm.n_sems, comm.n_sems)
    res = pl.pallas_call(body, name=name, grid=(nb, n1, n2), in_specs=in_specs, out_specs=out_specs,
                         out_shape=out_shape, scratch_shapes=scratch,
                         compiler_params=_params('arbitrary', 'arbitrary', 'arbitrary'))(*args)
    return res if comm is None else (*res[:n_out], res[n_out:])


def _gla_chunk(la_c, k_c):
    r = lax.broadcasted_iota(jnp.int32, (CHUNK, CHUNK), 0)
    c = lax.broadcasted_iota(jnp.int32, (CHUNK, CHUNK), 1)
    tri = jnp.where(c <= r, 1.0, 0.0).astype(BF16)
    cum = _tri_dot(tri, la_c)
    end = jnp.sum(la_c, axis=0, keepdims=True)
    dec = jnp.exp(end - cum)
    return dec, k_c * dec, jnp.exp(end)


GLA_PAIRS = GLA_HEADS // 2


def _gla_fwd(z, la, *, qc, kc, vc, name, blk=512):
    s = z.shape[0]
    bs = min(blk, s)
    ncb = bs // CHUNK
    nblk = s // bs

    def body(q_ref, k_ref, va_ref, vb_ref, la_ref, o_ref, st_ref, st):
        @pl.when(pl.program_id(1) == 0)
        def _():
            st[...] = jnp.zeros_like(st)

        heads, _ = _lane_masks(2, 0, False)
        v_refs = (va_ref, vb_ref)
        for c in range(ncb):
            sl = pl.ds(c * CHUNK, CHUNK)
            _, kf, a = _gla_chunk(la_ref[sl, :], k_ref[sl, :])
            qs = q_ref[sl, :] * (GLA_DK ** -0.5)
            for hh in range(2):
                ut = _dot(v_refs[hh][sl, :].astype(BF16), _sel(heads[hh], kf).astype(BF16), TN)
                new = a * st[hh] + ut
                st[hh] = new
                st_ref[0, c, hh] = new
                o_ref[sl, hh * GLA_DV:(hh + 1) * GLA_DV] = _dot(_sel(heads[hh], qs).astype(BF16), new.astype(BF16), NT)

    col = lambda c0, m=1: pl.BlockSpec((bs, LANES), lambda b, i: (i, c0 + m * b))
    return pl.pallas_call(
        body, name=name, grid=(GLA_PAIRS, nblk),
        in_specs=[col(qc), col(kc), col(vc, 2), col(vc + 1, 2), col(0)],
        out_specs=[pl.BlockSpec((bs, 2 * GLA_DV), lambda b, i: (i, b)),
                   pl.BlockSpec((1, ncb, 2, GLA_DV, LANES), lambda b, i: (b, i, 0, 0, 0))],
        out_shape=[jax.ShapeDtypeStruct((s, GLA_HEADS * GLA_DV), F32),
                   jax.ShapeDtypeStruct((GLA_PAIRS, s // CHUNK, 2, GLA_DV, LANES), F32)],
        scratch_shapes=[pltpu.VMEM((2, GLA_DV, LANES), F32)],
        compiler_params=_params('arbitrary', 'arbitrary'))(z, z, z, z, la)


def _gla_bwd(z, la, st_all, st_prev, do, *, qc, kc, vc, name, blk=512):
    s = z.shape[0]
    bs = min(blk, s)
    ncb = bs // CHUNK
    nblk = s // bs

    def body(q_ref, k_ref, va_ref, vb_ref, la_ref, st_ref, sp_ref, do_ref, dq_ref, dk_ref, dv_ref, dla_ref, ga):
        @pl.when(pl.program_id(1) == 0)
        def _():
            ga[...] = jnp.zeros_like(ga)

        r = lax.broadcasted_iota(jnp.int32, (CHUNK, CHUNK), 0)
        cc = lax.broadcasted_iota(jnp.int32, (CHUNK, CHUNK), 1)
        tri_rev = jnp.where(cc >= r, 1.0, 0.0).astype(BF16)
        heads, _ = _lane_masks(2, 0, False)
        v_refs = (va_ref, vb_ref)
        for c in reversed(range(ncb)):
            sl = pl.ds(c * CHUNK, CHUNK)
            dec, kf, a = _gla_chunk(la_ref[sl, :], k_ref[sl, :])
            qs = q_ref[sl, :] * (GLA_DK ** -0.5)
            dq2 = jnp.zeros((CHUNK, LANES), F32)
            dkd = jnp.zeros((CHUNK, LANES), F32)
            da = jnp.zeros((1, LANES), F32)
            for hh in range(2):
                hv = slice(hh * GLA_DV, (hh + 1) * GLA_DV)
                dob = do_ref[sl, hv].astype(BF16)
                g = _dot(dob, _sel(heads[hh], qs).astype(BF16), TN) + ga[hh]
                gb = g.astype(BF16)
                dq2 = dq2 + _dot(dob, st_ref[0, c, hh].astype(BF16))
                dv_ref[sl, hv] = _dot(_sel(heads[hh], kf).astype(BF16), gb, NT)
                dkd = dkd + _dot(v_refs[hh][sl, :].astype(BF16), gb)
                da = da + jnp.sum(g * sp_ref[0, c, hh], axis=0, keepdims=True)
                ga[hh] = a * g
            dq_ref[sl, :] = (GLA_DK ** -0.5) * dq2
            dk_ref[sl, :] = dkd * dec
            e = dkd * kf
            dend = jnp.sum(e, axis=0, keepdims=True) + da * a
            dla_ref[sl, :] = dend - _tri_dot(tri_rev, e)

    rev = lambda i: nblk - 1 - i
    col = lambda c0, m=1: pl.BlockSpec((bs, LANES), lambda b, i: (rev(i), c0 + m * b))
    wide = pl.BlockSpec((bs, 2 * GLA_DV), lambda b, i: (rev(i), b))
    stspec = pl.BlockSpec((1, ncb, 2, GLA_DV, LANES), lambda b, i: (b, rev(i), 0, 0, 0))
    return pl.pallas_call(
        body, name=name, grid=(GLA_PAIRS, nblk),
        in_specs=[col(qc), col(kc), col(vc, 2), col(vc + 1, 2), col(0), stspec, stspec, wide],
        out_specs=[col(0), col(0), wide, col(0)],
        out_shape=[jax.ShapeDtypeStruct((s, GLA_HEADS * GLA_DK), F32), jax.ShapeDtypeStruct((s, GLA_HEADS * GLA_DK), F32),
                   jax.ShapeDtypeStruct((s, GLA_HEADS * GLA_DV), F32), jax.ShapeDtypeStruct((s, GLA_HEADS * GLA_DK), F32)],
        scratch_shapes=[pltpu.VMEM((2, GLA_DV, LANES), F32)],
        compiler_params=_params('arbitrary', 'arbitrary'))(z, z, z, z, la, st_all, st_prev, do)


def _place():
    return lax.axis_index('x'), lax.axis_index('y'), lax.axis_index('c')


ANY = pl.BlockSpec(memory_space=pl.ANY)


def _all_gather8(blk, *, name):
    m, n = blk.shape

    def body(x_ref, out_ref, send_sems, recv_sems, local_sem):
        x, y, c = _place()
        me, sibling = (x, y, c), (x, y, 1 - c)
        chips = [(1 - x, y), (x, 1 - y), (1 - x, 1 - y)]

        def slot(px, py, pc):
            return out_ref.at[4 * px + 2 * py + pc]

        def copy(q, block, to, src=None):
            return pltpu.make_async_remote_copy(
                src_ref=slot(*block) if src is None else src, dst_ref=slot(*block), send_sem=send_sems.at[q],
                recv_sem=recv_sems.at[q], device_id=to, device_id_type=MESH)

        mine = pltpu.make_async_copy(x_ref, slot(*me), local_sem)
        mine.start()
        first = [copy(0, me, sibling, src=x_ref)]
        first += [copy(1 + q, me, (*chip, c), src=x_ref) for q, chip in enumerate(chips)]
        for cp in first:
            cp.start()
        passed = [copy(4 + q, (*chip, c), sibling) for q, chip in enumerate(chips)]
        for q, chip in enumerate(chips):
            copy(1 + q, (*chip, c), me).wait_recv()
            passed[q].start()
        copy(0, sibling, me).wait_recv()
        for q, chip in enumerate(chips):
            copy(4 + q, (*chip, 1 - c), me).wait_recv()
        for cp in first + passed:
            cp.wait_send()
        mine.wait()

    return pl.pallas_call(
        body, name=name, in_specs=[ANY], out_specs=ANY, out_shape=jax.ShapeDtypeStruct((N_DEV, m, n), blk.dtype),
        scratch_shapes=[pltpu.SemaphoreType.DMA((7,)), pltpu.SemaphoreType.DMA((7,)), pltpu.SemaphoreType.DMA(())],
    )(blk)


def _sems(*counts):
    return [pltpu.SemaphoreType.DMA((n,)) for n in counts]


class Comm(typing.NamedTuple):
    ins: list
    out_shapes: list
    n_sems: int
    start: typing.Callable
    finish: typing.Callable


def _remote(src, dst, send_sems, recv_sems, idx, to):
    return lambda: pltpu.make_async_remote_copy(src_ref=src, dst_ref=dst, send_sem=send_sems.at[idx],
                                                recv_sem=recv_sems.at[idx], device_id=to, device_id_type=MESH)


def _comm_from(copies, ins, out_shapes, n_sems):
    def start(*refs):
        for cp in copies(*refs)[0]:
            cp().start()

    def finish(*refs):
        sent, received = copies(*refs)
        for cp in received:
            cp().wait_recv()
        for cp in sent:
            cp().wait_send()

    return Comm(list(ins), list(out_shapes), n_sems, start, finish)


def _run_comm(comm, *, name, alias=False):
    n_in, n_out = len(comm.ins), len(comm.out_shapes)

    def body(*refs):
        ins, outs, sems = refs[:n_in], refs[n_in:n_in + n_out], refs[n_in + n_out:]
        comm.start(ins, outs, *sems)
        comm.finish(ins, outs, *sems)

    return pl.pallas_call(body, name=name, in_specs=[ANY] * n_in, out_specs=[ANY] * n_out, out_shape=comm.out_shapes,
                          input_output_aliases={q: q for q in range(n_in)} if alias else {},
                          scratch_shapes=_sems(comm.n_sems, comm.n_sems))(*comm.ins)


def _half(rows, c):
    h = rows // 2
    return pl.ds(pl.multiple_of(c * h, h), h)


def _gathered(ref, chip, rows, side):
    if not side:
        return ref.at[chip, rows]
    n = ref.shape[1] // N_CHIPS
    return ref.at[rows, pl.ds(pl.multiple_of(chip * n, n), n)]


def _gather_over_ici(ws, side):
    def copies(ins, outs, send_sems, recv_sems):
        x, y, c = _place()
        me_chip = 2 * x + y
        sent, received = [], []
        for q, w in enumerate(ws):
            half, every = _half(w.shape[0], c), pl.ds(0, w.shape[0])
            for k, (px, py) in enumerate([(1 - x, y), (x, 1 - y), (1 - x, 1 - y)]):
                sent.append(_remote(ins[q].at[half], _gathered(outs[q], me_chip, half, side[q]), send_sems, recv_sems,
                                    4 * q + k, (px, py, c)))
                slot = _gathered(outs[q], 2 * px + py, half, side[q])
                received.append(_remote(slot, slot, send_sems, recv_sems, 4 * q + k, (px, py, c)))
            whole = _remote(ins[q], _gathered(outs[q], me_chip, every, side[q]), send_sems, recv_sems, 4 * q + 3,
                            (x, y, 1 - c))
            sent.append(whole)
            received.append(whole)
        return sent, received

    shapes = [jax.ShapeDtypeStruct((w.shape[0], N_CHIPS * w.shape[1]) if sd else (N_CHIPS,) + w.shape, w.dtype)
              for w, sd in zip(ws, side)]
    return _comm_from(copies, ws, shapes, 4 * len(ws))


def _gather_over_d2d(parts, side):
    def copies(ins, outs, send_sems, recv_sems):
        x, y, c = _place()
        sent, received = [], []
        for q, w in enumerate(parts):
            rows = w.shape[0] if side[q] else w.shape[1]
            for k, (px, py) in enumerate([(1 - x, y), (x, 1 - y), (1 - x, 1 - y)]):
                mine = _gathered(outs[q], 2 * px + py, _half(rows, c), side[q])
                theirs = _gathered(outs[q], 2 * px + py, _half(rows, 1 - c), side[q])
                sent.append(_remote(mine, mine, send_sems, recv_sems, 3 * q + k, (x, y, 1 - c)))
                received.append(_remote(theirs, theirs, send_sems, recv_sems, 3 * q + k, (x, y, 1 - c)))
        return sent, received

    return _comm_from(copies, parts, [jax.ShapeDtypeStruct(w.shape, w.dtype) for w in parts], 3 * len(parts))


def _to_sibling(gs, *, name):
    n = len(gs)

    def body(*refs):
        ins, outs = refs[:n], refs[n:2 * n]
        send_sems, recv_sems = refs[2 * n:]
        x, y, c = _place()
        cps = [pltpu.make_async_remote_copy(
            src_ref=ins[q], dst_ref=outs[q], send_sem=send_sems.at[q], recv_sem=recv_sems.at[q],
            device_id=(x, y, 1 - c), device_id_type=MESH) for q in range(n)]
        for cp in cps:
            cp.start()
        for cp in cps:
            cp.wait()

    return pl.pallas_call(body, name=name, in_specs=[ANY] * n, out_specs=[ANY] * n,
                          out_shape=[jax.ShapeDtypeStruct(g.shape, g.dtype) for g in gs],
                          scratch_shapes=_sems(n, n))(*gs)


def _chip_exchange(ps):
    def copies(ins, outs, send_sems, recv_sems):
        x, y, c = _place()
        cps = [_remote(ins[q].at[2 * px + py], outs[q].at[k], send_sems, recv_sems, 3 * q + k, (px, py, c))
               for q in range(len(ps)) for k, (px, py) in enumerate([(1 - x, y), (x, 1 - y), (1 - x, 1 - y)])]
        return cps, cps

    return _comm_from(copies, ps, [jax.ShapeDtypeStruct((3,) + p.shape[1:], p.dtype) for p in ps], 3 * len(ps))


def _sum_chips(own, r, *, name, ts=256):
    k, n = own.shape
    ts = min(ts, k)

    def body(own_ref, r_ref, o_ref):
        f = lambda q: r_ref[q].astype(F32)
        o_ref[...] = ((own_ref[...].astype(F32) + f(0)) + f(1)) + f(2)

    return pl.pallas_call(
        body, name=name, grid=(k // ts,),
        in_specs=[pl.BlockSpec((ts, n), lambda i: (i, 0)), pl.BlockSpec((3, ts, n), lambda i: (0, i, 0))],
        out_specs=pl.BlockSpec((ts, n), lambda i: (i, 0)), out_shape=jax.ShapeDtypeStruct((k, n), F32),
        compiler_params=_params('arbitrary'))(own, r)


WIN_SHARD = N_IN // N_CHIPS
WIN_PAD = -(-WIN_SHARD // LANES) * LANES
GATE_WIRE_ROWS = 32


def _full_layer(sh, axis):
    _, k, n = sh.shape
    if axis == 2:
        return sh.transpose(1, 0, 2).reshape(k, N_CHIPS * n)
    return sh.reshape(N_CHIPS * k, n)


def _win_cols(wp, o, n):
    parts = []
    while n > 0:
        j, r = divmod(o, WIN_SHARD)
        take = min(n, WIN_SHARD - r)
        parts.append(wp[:, j * WIN_PAD + r:j * WIN_PAD + r + take])
        o, n = o + take, n - take
    return parts[0] if len(parts) == 1 else jnp.concatenate(parts, axis=1)


def _split_full(full, axis):
    if full.ndim == 3:
        return full
    k, n = full.shape
    if axis == 2:
        return jnp.stack([full[:, j * (n // N_CHIPS):(j + 1) * (n // N_CHIPS)] for j in range(N_CHIPS)])
    return full.reshape(N_CHIPS, k // N_CHIPS, n)


def _padc(a, w):
    return jnp.pad(a, ((0, 0), (0, w - a.shape[1])))


def _swap16(a):
    return jnp.concatenate([a[..., 16:32], a[..., 0:16]], axis=-1)


B_GR, B_GQ, B_GK, B_GV, B_MQ, B_MKR, B_MKRS, B_FF, B_GLOW, B_MKV, B_END = (
    0, 512, 768, 1024, 1536, 1792, 1920, 2048, 2176, 2304, 2432)
B_W = 2560
O_FQ, O_FF, O_GQ, O_GLOW, O_GR, O_MQ, O_MKV, O_MKR, O_ZG = 0, 768, 772, 1796, 1812, 2324, 2580, 2708, 2740


def _repack_layer_weights(w):
    wi = functools.partial(_win_cols, w['w_in'])
    out = dict(w)
    out['in_a'] = jnp.concatenate([wi(O_FQ, 256) * FOX_SCALE, wi(O_FQ + 256, 512)], axis=1)
    kr = wi(O_MKR, 32)
    out['in_b'] = jnp.concatenate([
        wi(O_GR, 512), wi(O_GQ, 1024), wi(O_MQ, 256), jnp.tile(kr, (1, MLA_HEADS)), jnp.tile(_swap16(kr), (1, MLA_HEADS)),
        _padc(wi(O_FF, 4), 128), _padc(wi(O_GLOW, 16), 128), wi(O_MKV, 128),
        jnp.zeros((D_MODEL, B_W - B_END), kr.dtype)], axis=1)
    out['in_c'] = wi(O_ZG, 3072)
    uq = w['w_mla_uq'].reshape(MLA_Q_RANK, MLA_HEADS, MLA_NOPE + MLA_ROPE)
    rope = uq[:, :, MLA_NOPE:]
    out['uq'] = jnp.concatenate([uq[:, :, :MLA_NOPE].reshape(MLA_Q_RANK, -1), rope.reshape(MLA_Q_RANK, -1),
                                 _swap16(rope).reshape(MLA_Q_RANK, -1)], axis=1)
    ukv = w['w_mla_ukv'].reshape(MLA_KV_RANK, MLA_HEADS, MLA_NOPE + MLA_VD)
    out['ukv'] = jnp.concatenate([ukv[:, :, :MLA_NOPE].reshape(MLA_KV_RANK, -1),
                                  ukv[:, :, MLA_NOPE:].reshape(MLA_KV_RANK, -1)], axis=1)
    out['gate'] = jnp.pad(w['w_gla_gate'], ((0, 128 - GLA_RANK), (0, 0)))
    return out


def _unpack_layer_grads(g):
    a, b, c = g['in_a'], g['in_b'], g['in_c']
    fold = lambda o: sum(b[:, o + MLA_ROPE * q:o + MLA_ROPE * (q + 1)] for q in range(MLA_HEADS))
    kr = fold(B_MKR) + _swap16(fold(B_MKRS))
    pieces = [(a[:, :256] * FOX_SCALE, 0, 256), (a, 256, 512), (b, B_FF, 4), (b, B_GQ, 1024), (b, B_GLOW, 16),
              (b, B_GR, 512), (b, B_MQ, 256), (b, B_MKV, 128), (kr, 0, 32), (c, 0, 3072)]
    shards = []
    for j in range(N_CHIPS):
        lo, hi, cut, at = j * WIN_SHARD, (j + 1) * WIN_SHARD, [], 0
        for arr, first, width in pieces:
            l, h = max(lo, at), min(hi, at + width)
            if l < h:
                cut.append(arr[:, first + l - at:first + h - at])
            at += width
        shards.append(jnp.concatenate(cut, axis=1))
    w_in = jnp.stack(shards)
    uq = g['uq']
    nope = uq[:, :256].reshape(MLA_Q_RANK, MLA_HEADS, MLA_NOPE)
    rope = (uq[:, 256:384].reshape(MLA_Q_RANK, MLA_HEADS, MLA_ROPE)
            + _swap16(uq[:, 384:512].reshape(MLA_Q_RANK, MLA_HEADS, MLA_ROPE)))
    w_uq = jnp.concatenate([nope, rope], axis=2).reshape(MLA_Q_RANK, -1)
    ukv = g['ukv']
    w_ukv = jnp.concatenate([ukv[:, :256].reshape(MLA_KV_RANK, MLA_HEADS, MLA_NOPE),
                             ukv[:, 256:].reshape(MLA_KV_RANK, MLA_HEADS, MLA_VD)], axis=2).reshape(MLA_KV_RANK, -1)
    out = {'w_in': w_in, 'w_mla_uq': w_uq, 'w_mla_ukv': w_ukv, 'w_gla_gate': g['gate'][:GLA_RANK]}
    for nm in ('w_up_fox', 'w_up_gla', 'w_up_mla', 'w_out', 'w_xq', 'w_xkv', 'w_xo', 'w_mlp1', 'w_mlp2'):
        out[nm] = g[nm]
    return out


def _rope_tables(s):
    half = MLA_ROPE // 2
    inv = ROPE_BASE ** (-jnp.arange(half, dtype=F32) / half)
    ang = jnp.arange(s).astype(F32)[:, None] * inv[None, :]
    cos, sin = jnp.cos(ang), jnp.sin(ang)
    c1 = jnp.concatenate([cos, cos], axis=1)
    s1 = jnp.concatenate([-sin, sin], axis=1)
    return jnp.tile(c1, (1, MLA_HEADS)), jnp.tile(s1, (1, MLA_HEADS))


def _rms_bwd(x, dh, g):
    r = lax.rsqrt(jnp.mean(x * x, axis=-1, keepdims=True) + EPS)
    xh = x * r
    gd = dh * g
    return r * (gd - xh * jnp.mean(gd * xh, axis=-1, keepdims=True)), dh * xh


def _norm_bwd_epilogue(dh, x, dres, g):
    dx, dg = _rms_bwd(x, dh, g)
    return dres + dx, dg


def _norm_bwd_call(x, dh, g, dres, name):
    w = x.width if isinstance(x, Cols) else x.shape[1]

    def with_res(xv, dv, rv, gv):
        dx, dg = _rms_bwd(xv, dv.astype(F32), gv)
        return rv + dx, dg

    def plain(xv, dv, gv):
        return _rms_bwd(xv, dv.astype(F32), gv)

    if dres is None:
        return _rowwise(plain, [x, dh], [g], [(w, F32)], [w], name=name)
    return _rowwise(with_res, [x, dh, dres], [g], [(w, F32)], [w], name=name)


def _gla_out_fwd(oraw, gr, g_out):
    outs = []
    for hh in range(GLA_HEADS):
        sl = slice(hh * GLA_DV, (hh + 1) * GLA_DV)
        oh = oraw[:, sl]
        n = oh * lax.rsqrt(jnp.mean(oh * oh, axis=-1, keepdims=True) + EPS) * g_out
        r = gr[:, sl]
        outs.append(n * (r * _sig(r)))
    return (jnp.concatenate(outs, axis=1),)


def _gla_out_bwd(oraw, gr, dout, g_out):
    d_o, d_r, dg = [], [], 0.0
    for hh in range(GLA_HEADS):
        sl = slice(hh * GLA_DV, (hh + 1) * GLA_DV)
        oh, r, do = oraw[:, sl], gr[:, sl], dout[:, sl].astype(F32)
        rs = lax.rsqrt(jnp.mean(oh * oh, axis=-1, keepdims=True) + EPS)
        sg = _sig(r)
        dn = do * (r * sg)
        d_r.append(do * (oh * rs * g_out) * (sg + r * sg * (1.0 - sg)))
        dx, dgh = _rms_bwd(oh, dn, g_out)
        d_o.append(dx)
        dg = dg + dgh
    return jnp.concatenate(d_o, axis=1), jnp.concatenate(d_r, axis=1), dg


def _adam(w, g, m, v):
    m = ADAM_B1 * m + (1.0 - ADAM_B1) * g
    v = ADAM_B2 * v + (1.0 - ADAM_B2) * (g * g)
    m_hat = m / (1.0 - ADAM_B1 ** ADAM_STEP)
    v_hat = v / (1.0 - ADAM_B2 ** ADAM_STEP)
    return -ADAM_LR * (m_hat / (jnp.sqrt(v_hat) + ADAM_EPS) + ADAM_WD * w), m, v


def _layer_fwd(x, mem, w, p, tabs, tag, carry_fox=None, after_fox=None, carry_mla=None):
    c4, s4 = tabs
    sv = {'x0': x}
    nm = lambda t: f'{t}_{tag}'
    za, h = _mm(x, w['in_a'], mode='nn', out_dtype=BF16, norm_g=p['g_mix'], emit_norm=True, name=nm('in_a'))
    zb = _mm(h, w['in_b'], mode='nn', out_dtype=F32, name=nm('in_b'))
    zc = _mm(h, w['in_c'], mode='nn', out_dtype=F32, name=nm('in_c'))
    sv.update(h=h, zc=zc)
    ff = Cols(zb, 128, B_FF // 128)
    (lf,) = _rowwise(lambda f, b: (_logsig(f + b),), [ff], [p['b_fox']], [(128, F32)], name=nm('fox_lf'))
    cum = _cumsum_rows(lf, reverse=False, name=nm('fox_cum'))
    ckf = jnp.pad(cum[:, :FOX_HEADS].T.reshape(2, 2, x.shape[0]), ((0, 0), (0, 6), (0, 0)))
    fox = dict(qc=0, kc=2, vc=4, nb=2, g=2, mode='causal', ck=ckf)
    o_fox, lse_fox, *carried = _mattn_fwd(za, za, za, name=nm('fox_attn'), comm=carry_fox, **fox)
    if after_fox is not None:
        w = {**w, **after_fox(carried[0])}
    sv.update(ff=ff, za=za, fox=fox, o_fox=o_fox, lse_fox=lse_fox)
    glow = Cols(zb, 128, B_GLOW // 128)
    gr = Cols(zb, 512, B_GR // 512)

    def gate_fn(gl, wg, bg):
        return (_logsig(_dot(gl.astype(BF16), wg) + bg) / GLA_TAU,)

    (la,) = _rowwise(gate_fn, [glow], [w['gate'], p['b_gla']], [(256, F32)], name=nm('gla_gate'))
    gla = dict(qc=B_GQ // LANES, kc=B_GK // LANES, vc=B_GV // LANES)
    oraw, states = _gla_fwd(zb, la, name=nm('gla'), **gla)
    (o_gla,) = _rowwise(_gla_out_fwd, [oraw, gr], [p['g_gla_out']], [(512, BF16)], name=nm('gla_out'))
    sv.update(glow=glow, gr=gr, zb=zb, la=la, gla=gla, states=states, oraw=oraw, o_gla=o_gla)
    mq = Cols(zb, 256, B_MQ // 256)
    mkv = Cols(zb, 128, B_MKV // 128)
    mkr2 = Cols(zb, 256, B_MKR // 256)
    qp, cqn = _mm(mq, w['uq'], mode='nn', out_dtype=F32, norm_g=p['g_mla_q'], emit_norm=True, name=nm('mla_uq'))
    kvp, ckvn = _mm(mkv, w['ukv'], mode='nn', out_dtype=BF16, norm_g=p['g_mla_kv'], emit_norm=True,
                    name=nm('mla_ukv'))

    def rope_fn(qv, kr, c4v, s4v):
        q_rope = qv[:, 256:384] * c4v + qv[:, 384:512] * s4v
        q_scaled = jnp.concatenate([qv[:, 0:256], q_rope], axis=1) * MLA_SCALE
        return q_scaled, kr[:, 0:128] * c4v + kr[:, 128:256] * s4v

    qall, kr4 = _rowwise(rope_fn, [qp, mkr2, c4, s4], [], [(384, BF16), (128, BF16)], name=nm('rope'))
    mla = dict(qc=0, kc=0, vc=2, nb=2, g=2, dq_scale=MLA_SCALE, mode='chunk', qr=qall, qrc=2, kr=kr4)
    o_mla, lse_mla, *carried = _mattn_fwd(qall, kvp, kvp, name=nm('mla_attn'), comm=carry_mla, **mla)
    if carry_mla is not None:
        sv['carried_mla'] = carried[0]
    sv.update(mq=mq, mkv=mkv, cqn=cqn, ckvn=ckvn, qall=qall, kvp=kvp, mla=mla, o_mla=o_mla, lse_mla=lse_mla)
    of_m, om_m = o_fox, o_mla
    sv.update(of_m=of_m, om_m=om_m)
    b_br = p['b_branch']

    y = _gated_merge([of_m, o_gla, om_m], [w['w_up_fox'], w['w_up_gla'], w['w_up_mla']], zc, b_br, name=nm('up_merge'))
    add = lambda acc, res: res + acc
    x1 = _mm(y, w['w_out'], mode='nn', out_dtype=F32, name=nm('out'), epilogue=add, extras=[(x, *_mn())])
    sv.update(y=y, x1=x1)
    qx, hx = _mm(x1, w['w_xq'], mode='nn', out_dtype=BF16, norm_g=p['g_xa'], emit_norm=True, name=nm('xq'),
                 epilogue=lambda acc: acc * XA_SCALE)
    kvx, mn = _mm(mem, w['w_xkv'], mode='nn', out_dtype=BF16, norm_g=p['g_mem'], emit_norm=True, name=nm('xkv'))
    xa = dict(qc=0, kc=0, vc=4, nb=4, g=1, dq_scale=XA_SCALE, mode='full')
    ox_m, lse_x = _mattn_fwd(qx, kvx, kvx, name=nm('xa_attn'), **xa)
    x2 = _mm(ox_m, w['w_xo'], mode='nn', out_dtype=F32, name=nm('xo'), epilogue=add, extras=[(x1, *_mn())])
    sv.update(hx=hx, mn=mn, qx=qx, kvx=kvx, xa=xa, lse_x=lse_x, ox_m=ox_m, x2=x2)
    hpre, hm = _mm(x2, w['w_mlp1'], mode='nn', out_dtype=BF16, norm_g=p['g_mlp'], emit_norm=True, name=nm('mlp1'))
    relu2 = lambda t: jnp.square(jnp.maximum(t.astype(F32), 0.0))
    x3 = _mm(hpre, w['w_mlp2'], mode='nn', out_dtype=F32, name=nm('mlp2'), a_fn=relu2, epilogue=add,
             extras=[(x2, *_mn())])
    sv.update(hpre=hpre, hm=hm, w=w)
    return x3, sv


EARLY = ('w_mlp1', 'w_mlp2', 'w_xo', 'w_xq', 'w_xkv', 'w_out', 'w_up_fox', 'w_up_gla', 'w_up_mla')
LATE = ('w_in', 'w_gla_gate', 'w_mla_uq', 'w_mla_ukv')


def _layer_bwd(dx3, mem, w, p, tabs, sv, tag, carry_mla=None, early=None):
    c4, s4 = tabs
    nm = lambda t: f'{t}_{tag}'
    s = dx3.shape[0]
    gw, gs = {}, {}
    relu2 = lambda t: jnp.square(jnp.maximum(t.astype(F32), 0.0))
    gw['w_mlp2'] = _mm(sv['hpre'], dx3, mode='tn', out_dtype=F32, name=nm('d_mlp2'), a_fn=relu2)
    dact = lambda acc, hp: acc * (2.0 * jnp.maximum(hp.astype(F32), 0.0))
    dhpre = _mm(dx3, w['w_mlp2'], mode='nt', out_dtype=BF16, name=nm('d_act'), epilogue=dact,
                extras=[(sv['hpre'], *_mn())])
    gw['w_mlp1'] = _mm(sv['hm'], dhpre, mode='tn', out_dtype=F32, name=nm('d_mlp1'))
    dx2, gs['g_mlp'] = _mm(dhpre, w['w_mlp1'], mode='nt', out_dtype=F32, name=nm('d_hm'), epilogue=_norm_bwd_epilogue,
                           col_sums=True, full_rows=True,
                           extras=[(sv['x2'], *_mn()), (dx3, *_mn()), (p['g_mlp'], *_nvec())])
    gw['w_xo'] = _mm(sv['ox_m'], dx2, mode='tn', out_dtype=F32, name=nm('d_xo'))
    dox = _mm(dx2, w['w_xo'], mode='nt', out_dtype=BF16, name=nm('d_ox'))
    dqx_m, dkx, dvx = _mattn_bwd(sv['qx'], sv['kvx'], sv['kvx'], sv['ox_m'], dox, sv['lse_x'], name=nm('xa_bwd'),
                                 **sv['xa'])
    dkvx = jnp.concatenate([dkx, dvx], axis=1).astype(BF16)
    gw['w_xq'] = _mm(sv['hx'], dqx_m, mode='tn', out_dtype=F32, name=nm('d_xq'))
    dx1, gs['g_xa'] = _mm(dqx_m, w['w_xq'], mode='nt', out_dtype=F32, name=nm('d_hx'), epilogue=_norm_bwd_epilogue,
                          col_sums=True, full_rows=True,
                          extras=[(sv['x1'], *_mn()), (dx2, *_mn()), (p['g_xa'], *_nvec())])
    gw['w_xkv'] = _mm(sv['mn'], dkvx, mode='tn', out_dtype=F32, name=nm('d_xkv'))
    dmn = _mm(dkvx, w['w_xkv'], mode='nt', out_dtype=F32, name=nm('d_mn'))
    _, gs['g_mem'] = _norm_bwd_call(mem, dmn, p['g_mem'], None, nm('d_norm_mem'))
    gw['w_out'] = _mm(sv['y'], dx1, mode='tn', out_dtype=F32, name=nm('d_out'))
    dy = _mm(dx1, w['w_out'], mode='nt', out_dtype=BF16, name=nm('d_y'))
    zc, b_br = sv['zc'], p['b_branch']

    branches = (('w_up_fox', sv['of_m'], BF16), ('w_up_gla', sv['o_gla'], F32), ('w_up_mla', sv['om_m'], BF16))
    du, do_br, dzc, gs['b_branch'] = _gated_merge_bwd(dy, zc, b_br, [o for _, o, _ in branches],
                                                      [w[wn] for wn, _, _ in branches], [dt for _, _, dt in branches],
                                                      name=nm('d_merge'))
    for q, (wn, o_m, _) in enumerate(branches):
        gw[wn] = _mm(o_m, du[q], mode='tn', out_dtype=F32, name=nm(f'd_up{q}'))
    za = sv['za']
    carry_fox = None if early is None else early({nm_: gw[nm_] for nm_ in EARLY})
    dfq, dfk, dfv, dck, dcq, *carried_fox = _mattn_bwd(za, za, za, sv['o_fox'], do_br[0], sv['lse_fox'],
                                                       name=nm('fox_bwd'), comm=carry_fox, **sv['fox'])
    dcum = _padc(dck[:, :2, :].reshape(FOX_HEADS, s).T + dcq.reshape(s, 2, LANES)[:, :, :2].reshape(s, FOX_HEADS), 128)
    dlf = _cumsum_rows(dcum, reverse=True, name=nm('fox_dcum'))

    def dff_fn(dl, f, b):
        d = dl * _sig(-(f + b))
        return d, d

    dff, db_fox = _rowwise(dff_fn, [dlf, sv['ff']], [p['b_fox']], [(128, F32)], [128], name=nm('fox_dff'))
    gs['b_fox'] = db_fox
    dza = jnp.concatenate([dfq, dfk, dfv], axis=1).astype(BF16)
    dqn, dkn, dvv, dq_rope, dk_rope, *carried_mla = _mattn_bwd(sv['qall'], sv['kvp'], sv['kvp'], sv['o_mla'], do_br[2],
                                                               sv['lse_mla'], name=nm('mla_bwd'), comm=carry_mla,
                                                               **sv['mla'])

    def drope_fn(dn, dq, dk, c4v, s4v):
        return jnp.concatenate([dn, dq * c4v, dq * s4v], axis=1), jnp.concatenate([dk * c4v, dk * s4v], axis=1)

    dqp, dmkr2 = _rowwise(drope_fn, [dqn, dq_rope, dk_rope, c4, s4], [], [(512, BF16), (256, BF16)], name=nm('d_rope'))
    dkvp = jnp.concatenate([dkn, dvv], axis=1).astype(BF16)
    gw['uq'] = _mm(sv['cqn'], dqp, mode='tn', out_dtype=F32, name=nm('d_uq'))
    dcqn = _mm(dqp, w['uq'], mode='nt', out_dtype=F32, name=nm('d_cqn'))
    gw['ukv'] = _mm(sv['ckvn'], dkvp, mode='tn', out_dtype=F32, name=nm('d_ukv'))
    dckvn = _mm(dkvp, w['ukv'], mode='nt', out_dtype=F32, name=nm('d_ckvn'))
    dmq, gs['g_mla_q'] = _norm_bwd_call(sv['mq'], dcqn, p['g_mla_q'], None, nm('d_norm_q'))
    dmkv, gs['g_mla_kv'] = _norm_bwd_call(sv['mkv'], dckvn, p['g_mla_kv'], None, nm('d_norm_kv'))
    doraw, dgr, gs['g_gla_out'] = _rowwise(_gla_out_bwd, [sv['oraw'], sv['gr'], do_br[1]], [p['g_gla_out']],
                                           [(512, F32), (512, BF16)], [128], name=nm('d_gla_out'))
    st = sv['states']
    st_prev = jnp.concatenate([jnp.zeros_like(st[:, :1]), st[:, :-1]], axis=1)
    dgq, dgk, dgv, dla = _gla_bwd(sv['zb'], sv['la'], st, st_prev, doraw, name=nm('gla_bwd'), **sv['gla'])

    def dgate_fn(dl, gl, wg, bg):
        pre = _dot(gl.astype(BF16), wg) + bg
        dpre = dl * (1.0 / GLA_TAU) * _sig(-pre)
        return dpre, _dot(dpre.astype(BF16), wg, NT), dpre

    dpre, dglow, gs['b_gla'] = _rowwise(dgate_fn, [dla, sv['glow']], [w['gate'], p['b_gla']],
                                        [(256, BF16), (128, BF16)], [256], name=nm('d_gla_gate'))
    gw['gate'] = _mm(sv['glow'], dpre, mode='tn', out_dtype=F32, name=nm('d_wgate'))
    bf = lambda t: t.astype(BF16)
    dzb = jnp.concatenate([dgr, bf(dgq), bf(dgk), bf(dgv), bf(dmq), dmkr2, bf(dff), dglow, bf(dmkv),
                           jnp.zeros((s, B_W - B_END), BF16)], axis=1)
    h = sv['h']
    gw['in_a'] = _mm(h, dza, mode='tn', out_dtype=F32, name=nm('d_in_a'))
    gw['in_b'] = _mm(h, dzb, mode='tn', out_dtype=F32, name=nm('d_in_b'))
    gw['in_c'] = _mm(h, dzc, mode='tn', out_dtype=F32, name=nm('d_in_c'))
    add = lambda acc, prev: prev + acc
    dh = _mm(dza, w['in_a'], mode='nt', out_dtype=F32, name=nm('d_h_a'))
    dh = _mm(dzb, w['in_b'], mode='nt', out_dtype=F32, name=nm('d_h_b'), epilogue=add, extras=[(dh, *_mn())])
    dx0, gs['g_mix'] = _mm(dzc, w['in_c'], mode='nt', out_dtype=F32, name=nm('d_h_c'), col_sums=True, full_rows=True,
                           epilogue=lambda acc, prev, xv, rv, gv: _norm_bwd_epilogue(prev + acc, xv, rv, gv),
                           extras=[(dh, *_mn()), (sv['x0'], *_mn()), (dx1, *_mn()), (p['g_mix'], *_nvec())])
    return dx0, gw, gs, (carried_mla or [None])[0], (carried_fox or [None])[0]


def _loss_head(x, target, g_final):
    d = x.shape[1]

    def fn(xv, tv, gv):
        r = lax.rsqrt(jnp.mean(xv * xv, axis=-1, keepdims=True) + EPS)
        xh = xv * r
        e = xh * gv - tv
        dy = e * (1.0 / d)
        gd = dy * gv
        dx = r * (gd - xh * jnp.mean(gd * xh, axis=-1, keepdims=True))
        row_loss = 0.5 * jnp.mean(e * e, axis=-1, keepdims=True)
        return dx, dy * xh, jnp.broadcast_to(row_loss, (xv.shape[0], LANES))

    return _rowwise(fn, [x, target], [g_final], [(d, F32)], [d, LANES], name='loss_head')


def _step(args):
    shapes = {nm: args[nm].shape for nm in ORDER}
    x, mem, target = args['x'][0], args['mem'][0], args['loss_target'][0]
    s = x.shape[0]

    def wire(nm, l):
        w = args[nm][l].astype(BF16)
        if nm == 'w_in':
            w = jnp.pad(w, ((0, 0), (0, WIN_PAD - WIN_SHARD)))
        if nm == 'w_gla_gate':
            w = jnp.pad(w, ((0, GATE_WIRE_ROWS - GLA_RANK), (0, 0)))
        return w

    axis_of = dict(BIG)
    names = tuple(nm for nm, _ in BIG)
    wires = lambda l, nms: [wire(nm, l) for nm in nms]
    width = lambda nm: WIN_PAD if nm == 'w_in' else args[nm].shape[2]
    side_by_side = lambda nms: [axis_of[nm] == 2 and width(nm) % LANES == 0 for nm in nms]
    over_ici = lambda l, nms: _gather_over_ici(wires(l, nms), side_by_side(nms))

    def whole(parts, nms, tag):
        side = side_by_side(nms)
        parts = _run_comm(_gather_over_d2d(parts, side), name=f'gather_d2d_{tag}', alias=True)
        full = {nm: p if sd else _full_layer(p, axis_of[nm]) for nm, p, sd in zip(nms, parts, side)}
        if 'w_gla_gate' in full:
            full['w_gla_gate'] = full['w_gla_gate'][:GLA_RANK]
        return full

    tabs = _rope_tables(s)
    layers_p = []
    for l in range(DEPTH):
        layers_p.append({
            'g_mix': args['g_mix'][l][None], 'b_fox': _padc(args['b_fox_forget'][l][None], 128),
            'b_gla': args['b_gla_gate'][l][None], 'g_gla_out': args['g_gla_out'][l][None],
            'g_mla_q': args['g_mla_q'][l][None], 'g_mla_kv': args['g_mla_kv'][l][None],
            'b_branch': args['b_branch_gate'][l][None], 'g_xa': args['g_xa'][l][None],
            'g_mem': args['g_mem'][l][None], 'g_mlp': args['g_mlp'][l][None]})

    first = _run_comm(over_ici(0, LATE), name='gather_ici_first_l0')
    w_now = _repack_layer_weights(whole(first, LATE, 'first_l0'))
    saved = []
    xl = x
    for l in range(DEPTH):
        carry_fox = over_ici(0, EARLY) if l == 0 else None
        after_fox = (lambda parts: whole(parts, EARLY, 'rest_l0')) if l == 0 else None
        carry_mla = over_ici(l + 1, names) if l + 1 < DEPTH else None
        xl, sv = _layer_fwd(xl, mem, w_now, layers_p[l], tabs, f'l{l}', carry_fox=carry_fox, after_fox=after_fox,
                            carry_mla=carry_mla)
        saved.append(sv)
        if carry_mla is not None:
            w_now = _repack_layer_weights(whole(sv.pop('carried_mla'), names, f'l{l + 1}'))
    dx, dg_final, loss_lanes = _loss_head(xl, target, args['g_final'][None])
    cidx = lax.axis_index('c')
    chip = 2 * lax.axis_index('x') + lax.axis_index('y')

    def pair_sums(gw, nms, tag):
        mine, theirs = [], []
        for nm in nms:
            shards = _split_full(gw[nm], axis_of[nm]).astype(BF16)
            h = shards.shape[1] // 2
            mine.append(lax.dynamic_slice_in_dim(shards, cidx * h, h, axis=1))
            theirs.append(lax.dynamic_slice_in_dim(shards, (1 - cidx) * h, h, axis=1))
        got = _to_sibling(theirs, name=f'grads_swap_{tag}')
        pairs = []
        for nm, a, b in zip(nms, mine, got):
            _, h, n = a.shape
            (p,) = _rowwise(lambda u, v: (u.astype(F32) + v.astype(F32),),
                            [a.reshape(N_CHIPS * h, n), b.reshape(N_CHIPS * h, n)], [], [(n, BF16)],
                            name=f'pair_sum_{nm}_{tag}')
            pairs.append(p.reshape(N_CHIPS, h, n))
        return pairs

    def finish(pairs, from_chips, nms, tag):
        own = [lax.dynamic_index_in_dim(p, chip, axis=0, keepdims=False) for p in pairs]
        mine = [_sum_chips(o, r, name=f'chip_sum_{nm}_{tag}') for nm, o, r in zip(nms, own, from_chips)]
        theirs = _to_sibling(mine, name=f'grads_join_{tag}')
        return {nm: jnp.where(cidx == 0, jnp.concatenate([a, b]), jnp.concatenate([b, a]))
                for nm, a, b in zip(nms, mine, theirs)}

    gs_layers, done = [None] * DEPTH, [{} for _ in range(DEPTH)]
    above = None
    for l in reversed(range(DEPTH)):
        lowest, early_pairs = l == 0, []

        def early(gw_early, l=l, early_pairs=early_pairs):
            early_pairs.extend(pair_sums(gw_early, EARLY, f'early_l{l}'))
            return _chip_exchange(early_pairs)

        carry_mla = None if above is None else _chip_exchange(above[1])
        dx, gw, gs_layers[l], got_mla, got_fox = _layer_bwd(
            dx, mem, saved[l]['w'], layers_p[l], tabs, saved[l], f'l{l}', carry_mla=carry_mla,
            early=early if lowest else None)
        if above is not None:
            done[above[0]].update(finish(above[1], got_mla, names, f'l{above[0]}'))
        grads = _unpack_layer_grads(gw)
        if lowest:
            done[l].update(finish(early_pairs, got_fox, EARLY, f'early_l{l}'))
            late_pairs = pair_sums(grads, LATE, f'late_l{l}')
            from_late = _run_comm(_chip_exchange(late_pairs), name=f'grads_exchange_late_l{l}')
            done[l].update(finish(late_pairs, from_late, LATE, f'late_l{l}'))
        else:
            above = (l, pair_sums(grads, names, f'l{l}'))
    grad_x = dx[None]
    gshard = {nm: jnp.stack([done[l][nm] for l in range(DEPTH)]) for nm in names}

    small_g = []
    for nm, key in (('g_mix', 'g_mix'), ('b_fox_forget', 'b_fox'), ('b_gla_gate', 'b_gla'),
                    ('g_gla_out', 'g_gla_out'), ('g_mla_q', 'g_mla_q'), ('g_mla_kv', 'g_mla_kv'),
                    ('b_branch_gate', 'b_branch'), ('g_xa', 'g_xa'), ('g_mem', 'g_mem'), ('g_mlp', 'g_mlp')):
        width = shapes[nm][1]
        small_g.append(jnp.concatenate([gs_layers[l][key][0, :width] for l in range(DEPTH)]))
    small_g.append(dg_final[0])
    small_g.append(loss_lanes[0, :1])
    flat = jnp.concatenate(small_g)
    n_small = flat.shape[0]
    srows = -(-n_small // (8 * LANES)) * 8
    pad = lambda v: jnp.pad(v, (0, srows * LANES - v.shape[0])).reshape(srows, LANES)
    all_small = _all_gather8(pad(flat), name='gather_small')
    sw, sm, svv = (pad(jnp.concatenate([args[pre + nm].reshape(-1) for nm in SMALL] + [jnp.zeros((1,), F32)]))
                   for pre in ('', 'm_', 'v_'))

    def small_body(g_ref, w_ref, m_ref, v_ref, go_ref, d_ref, mo_ref, vo_ref):
        g = g_ref[0]
        for q in range(1, N_DEV):
            g = g + g_ref[q]
        go_ref[...] = g
        d_ref[...], mo_ref[...], vo_ref[...] = _adam(w_ref[...], g, m_ref[...], v_ref[...])

    sg, sd, snm, snv = pl.pallas_call(
        small_body, name='small_sum_adam', out_shape=[jax.ShapeDtypeStruct((srows, LANES), F32)] * 4,
        compiler_params=pltpu.CompilerParams(vmem_limit_bytes=VMEM_LIMIT))(all_small, sw, sm, svv)

    def unsmall(buf):
        v, out, off = buf.reshape(-1), {}, 0
        for nm in SMALL:
            nel = math.prod(shapes[nm])
            out[nm] = v[off:off + nel].reshape(shapes[nm])
            off += nel
        return out, v[off]

    res = {}
    (res['grad'], loss), (res['delta'], _), (res['m'], _), (res['v'], _) = (unsmall(t) for t in (sg, sd, snm, snv))

    for nm, _ in BIG:
        shp = args[nm].shape
        view = lambda t: t.reshape(shp[0] * shp[1], shp[2])
        d, m2, v2 = _rowwise(_adam, [view(args[nm]), view(gshard[nm]), view(args['m_' + nm]), view(args['v_' + nm])],
                             [], [(shp[2], F32)] * 3, name=f'adam_{nm}')
        res['grad'][nm], res['delta'][nm], res['m'][nm], res['v'][nm] = (
            gshard[nm], d.reshape(shp), m2.reshape(shp), v2.reshape(shp))

    return (loss, grad_x, *[res['grad'][nm] for nm in ORDER], *[res['delta'][nm] for nm in ORDER],
            *[res['m'][nm] for nm in ORDER], *[res['v'][nm] for nm in ORDER])


def kernel(x, mem, g_mix, w_in, b_fox_forget, w_gla_gate, b_gla_gate, g_gla_out, g_mla_q, w_mla_uq, g_mla_kv, w_mla_ukv, b_branch_gate, w_up_fox, w_up_gla, w_up_mla, w_out, g_xa, g_mem, w_xq, w_xkv, w_xo, g_mlp, w_mlp1, w_mlp2, g_final, loss_target, m_g_mix, m_w_in, m_b_fox_forget, m_w_gla_gate, m_b_gla_gate, m_g_gla_out, m_g_mla_q, m_w_mla_uq, m_g_mla_kv, m_w_mla_ukv, m_b_branch_gate, m_w_up_fox, m_w_up_gla, m_w_up_mla, m_w_out, m_g_xa, m_g_mem, m_w_xq, m_w_xkv, m_w_xo, m_g_mlp, m_w_mlp1, m_w_mlp2, m_g_final, v_g_mix, v_w_in, v_b_fox_forget, v_w_gla_gate, v_b_gla_gate, v_g_gla_out, v_g_mla_q, v_w_mla_uq, v_g_mla_kv, v_w_mla_ukv, v_b_branch_gate, v_w_up_fox, v_w_up_gla, v_w_up_mla, v_w_out, v_g_xa, v_g_mem, v_w_xq, v_w_xkv, v_w_xo, v_g_mlp, v_w_mlp1, v_w_mlp2, v_g_final):
    return _step(dict(locals()))
```

```python
import functools
import math
import typing

import jax
import jax.numpy as jnp
from jax import lax
from jax.experimental import pallas as pl
from jax.experimental.pallas import tpu as pltpu

F32 = jnp.float32
BF16 = jnp.bfloat16
MESH = pl.DeviceIdType.MESH

D_MODEL = 1024
DEPTH = 2
CHUNK = 64
EPS = 1e-6
FOX_HEADS, FOX_HD = 4, 64
GLA_HEADS, GLA_DK, GLA_DV, GLA_RANK, GLA_TAU = 4, 64, 128, 16, 16.0
MLA_HEADS, MLA_Q_RANK, MLA_KV_RANK, MLA_NOPE, MLA_ROPE, MLA_VD = 4, 256, 128, 64, 32, 64
ROPE_BASE = 10000.0
XA_HEADS, XA_HD = 4, 128
D_FF = 4 * D_MODEL
IN_SIZES = (256, 256, 256, 4, 256, 256, 512, 16, 512, 256, 128, 32, 3072)
N_IN = sum(IN_SIZES)

ADAM_LR, ADAM_B1, ADAM_B2, ADAM_EPS, ADAM_WD, ADAM_STEP = 0.001, 0.9, 0.999, 1e-08, 0.01, 10

N_CHIPS = 4
N_DEV = 8
LANES = 128
VMEM_LIMIT = 56 * 1024 * 1024
MASK_VALUE = -1e30

BIG = (('w_in', 2), ('w_gla_gate', 2), ('w_mla_uq', 2), ('w_mla_ukv', 2), ('w_up_fox', 2), ('w_up_gla', 2),
       ('w_up_mla', 2), ('w_out', 1), ('w_xq', 1), ('w_xkv', 1), ('w_xo', 2), ('w_mlp1', 2), ('w_mlp2', 1))
SMALL = ('g_mix', 'b_fox_forget', 'b_gla_gate', 'g_gla_out', 'g_mla_q', 'g_mla_kv', 'b_branch_gate',
         'g_xa', 'g_mem', 'g_mlp', 'g_final')
ORDER = ('g_mix', 'w_in', 'b_fox_forget', 'w_gla_gate', 'b_gla_gate', 'g_gla_out', 'g_mla_q', 'w_mla_uq',
         'g_mla_kv', 'w_mla_ukv', 'b_branch_gate', 'w_up_fox', 'w_up_gla', 'w_up_mla', 'w_out', 'g_xa', 'g_mem',
         'w_xq', 'w_xkv', 'w_xo', 'g_mlp', 'w_mlp1', 'w_mlp2', 'g_final')


def _params(*sem):
    return pltpu.CompilerParams(dimension_semantics=sem, vmem_limit_bytes=VMEM_LIMIT)


def _sig(x):
    return 1.0 / (1.0 + jnp.exp(-x))


def _logsig(x):
    return jnp.minimum(x, 0.0) - jnp.log(1.0 + jnp.exp(-jnp.abs(x)))


NN = (((1,), (0,)), ((), ()))
NT = (((1,), (1,)), ((), ()))
TN = (((0,), (0,)), ((), ()))


def _dot(a, b, dims=NN):
    return lax.dot_general(a, b, dims, preferred_element_type=F32)


class Cols(typing.NamedTuple):
    arr: jax.Array
    width: int
    blk: int


def _tri_dot(tri, x):
    hi = x.astype(BF16)
    r1 = x - hi.astype(F32)
    mid = r1.astype(BF16)
    lo = (r1 - mid.astype(F32)).astype(BF16)
    return _dot(tri, hi) + _dot(tri, mid) + _dot(tri, lo)


MM_TILES = ((1024, 1024), (1024, 512), (512, 1024), (512, 512), (256, 1024), (512, 256), (256, 512), (256, 256),
            (128, 1024), (128, 128))
MM_VMEM_BUDGET = 46 * 1024 * 1024


def _mm_tiles(m, n, k, a_bytes, b_bytes, out_bytes, ex_bytes, has_norm, emit_norm, has_fn, full_rows):
    for tm, tn in MM_TILES:
        tm, tn = min(tm, m), min(tn, n)
        if m % tm or n % tn or (full_rows and tn != n):
            continue
        blocks = tm * k * a_bytes + k * tn * b_bytes + tm * tn * (out_bytes + ex_bytes) + (tm * k * 2 if emit_norm else 0)
        temps = tm * tn * 4 + (tm * k * 2 if has_norm else 0) + (tm * k * 6 if has_fn or has_norm else 0)
        if 2 * blocks + temps <= MM_VMEM_BUDGET:
            return tm, tn
    raise ValueError((m, n, k))


def _mm(a, b, *, mode, out_dtype, name, norm_g=None, emit_norm=False, a_fn=None, extras=(), epilogue=None,
        col_sums=False, full_rows=False):
    a_blk = 0
    if isinstance(a, Cols):
        a, width, a_blk = a
        a_shape = (a.shape[0], width)
    else:
        a_shape = a.shape
    if mode == 'tn':
        k, m = a_shape
    else:
        m, k = a_shape
    n = b.shape[0] if mode == 'nt' else b.shape[1]
    assert (b.shape[1] if mode == 'nt' else b.shape[0]) == k, (name, a.shape, b.shape)
    has_norm = norm_g is not None
    ex_bytes = sum(arr.dtype.itemsize for arr, kind, _ in extras if kind == 'mn')
    tm, tn = _mm_tiles(m, n, k, a.dtype.itemsize, b.dtype.itemsize, jnp.dtype(out_dtype).itemsize, ex_bytes, has_norm,
                       emit_norm, a_fn is not None, full_rows)
    assert all(col % tn == 0 for _, _, col in extras), (name, tn)
    assert a_blk == 0 or (mode == 'nn') or (mode == 'tn' and tm == m)
    assert not (col_sums and (has_norm or emit_norm))
    ij = (lambda f: lambda g0, g1: f(g1, g0)) if col_sums else (lambda f: f)
    spec = lambda blk, f: pl.BlockSpec(blk, ij(f))
    if mode == 'tn':
        a_spec = spec((k, tm), lambda i, j: (0, i + a_blk))
    else:
        a_spec = spec((tm, k), lambda i, j: (i, a_blk))
    b_spec = spec((tn, k), lambda i, j: (j, 0)) if mode == 'nt' else spec((k, tn), lambda i, j: (0, j))
    dims = {'nn': NN, 'nt': NT, 'tn': TN}[mode]
    assert not (has_norm and mode != 'nn')
    n_ex = len(extras)

    def body(*refs):
        a_ref, b_ref = refs[0], refs[1]
        pos = 2
        g_ref = None
        if has_norm:
            g_ref = refs[pos]
            pos += 1
        ex_refs = refs[pos:pos + n_ex]
        pos += n_ex
        o_ref = refs[pos]
        pos += 1
        h_ref = None
        if emit_norm:
            h_ref = refs[pos]
            pos += 1
        if has_norm:
            an_ref = refs[pos]

            @pl.when(pl.program_id(1) == 0)
            def _():
                xf = a_ref[...].astype(F32)
                y = xf * lax.rsqrt(jnp.mean(xf * xf, axis=-1, keepdims=True) + EPS) * g_ref[...]
                an_ref[...] = y.astype(BF16)
                if emit_norm:
                    h_ref[...] = y.astype(BF16)

            av = an_ref[...]
        else:
            av = a_ref[...]
            if a_fn is not None:
                av = a_fn(av)
            av = av.astype(BF16)
        acc = _dot(av, b_ref[...].astype(BF16), dims)
        if epilogue is not None:
            acc = epilogue(acc, *[r[...] for r in ex_refs])
        acc, to_sum = acc if isinstance(acc, tuple) else (acc, acc)
        o_ref[...] = acc.astype(out_dtype)
        if col_sums:
            sum_ref = refs[pos]

            @pl.when(pl.program_id(1) == 0)
            def _():
                sum_ref[...] = jnp.zeros_like(sum_ref)

            sum_ref[...] += jnp.sum(to_sum, axis=0, keepdims=True)

    in_specs = [a_spec, b_spec]
    args = [a, b]
    if has_norm:
        in_specs.append(pl.BlockSpec((1, k), lambda i, j: (0, 0)))
        args.append(norm_g)
    for arr, kind, col in extras:
        if kind == 'mn':
            in_specs.append(spec((tm, tn), lambda i, j, o=col // tn: (i, j + o)))
        else:
            in_specs.append(spec((1, tn), lambda i, j, o=col // tn: (0, j + o)))
        args.append(arr)
    out_shape = [jax.ShapeDtypeStruct((m, n), out_dtype)]
    out_specs = [spec((tm, tn), lambda i, j: (i, j))]
    if emit_norm:
        out_shape.append(jax.ShapeDtypeStruct((m, k), BF16))
        out_specs.append(pl.BlockSpec((tm, k), lambda i, j: (i, 0)))
    if col_sums:
        out_shape.append(jax.ShapeDtypeStruct((1, n), F32))
        out_specs.append(spec((1, tn), lambda i, j: (0, j)))
    scratch = [pltpu.VMEM((tm, k), BF16)] if has_norm else []
    grid = (n // tn, m // tm) if col_sums else (m // tm, n // tn)
    res = pl.pallas_call(
        body, name=name, grid=grid, in_specs=in_specs, out_specs=out_specs, out_shape=out_shape,
        scratch_shapes=scratch, compiler_params=_params('arbitrary', 'arbitrary'))(*args)
    return res if emit_norm or col_sums else res[0]


def _gated_merge(outs, ups, zg, bias, *, name, tm=1024, tn=512):
    s, n, nq = zg.shape[0], ups[0].shape[1], len(outs)
    tm, tn = min(tm, s), min(tn, n)
    per = n // tn

    def body(*refs):
        y = None
        for q in range(nq):
            o_ref, w_ref, z_ref, b_ref = refs[q], refs[nq + q], refs[2 * nq + q], refs[3 * nq + q]
            term = _sig(z_ref[...].astype(F32) + b_ref[...]) * _dot(o_ref[...], w_ref[...])
            y = term if y is None else y + term
        refs[4 * nq][...] = y.astype(BF16)

    in_specs = [pl.BlockSpec((tm, o.shape[1]), lambda i, j: (i, 0)) for o in outs]
    in_specs += [pl.BlockSpec((u.shape[0], tn), lambda i, j: (0, j)) for u in ups]
    in_specs += [pl.BlockSpec((tm, tn), lambda i, j, q=q: (i, j + q * per)) for q in range(nq)]
    in_specs += [pl.BlockSpec((1, tn), lambda i, j, q=q: (0, j + q * per)) for q in range(nq)]
    return pl.pallas_call(body, name=name, grid=(s // tm, per), in_specs=in_specs,
                          out_specs=pl.BlockSpec((tm, tn), lambda i, j: (i, j)),
                          out_shape=jax.ShapeDtypeStruct((s, n), BF16),
                          compiler_params=_params('arbitrary', 'arbitrary'))(*outs, *ups, *[zg] * nq, *[bias] * nq)


def _gated_merge_bwd(dy, zg, bias, outs, ups, do_dtypes, *, name, tm=512):
    s, n = dy.shape
    nq = len(outs)
    tm = min(tm, s)

    def body(*refs):
        dy_ref, zg_ref, b_ref = refs[:3]
        o_refs, w_refs = refs[3:3 + nq], refs[3 + nq:3 + 2 * nq]
        du_refs, do_refs = refs[3 + 2 * nq:3 + 3 * nq], refs[3 + 3 * nq:3 + 4 * nq]
        dz_ref, db_ref = refs[3 + 4 * nq:]

        @pl.when(pl.program_id(0) == 0)
        def _():
            db_ref[...] = jnp.zeros_like(db_ref)

        d = dy_ref[...].astype(F32)
        for q in range(nq):
            cols = slice(q * n, (q + 1) * n)
            g = _sig(zg_ref[:, cols].astype(F32) + b_ref[:, cols])
            du = (d * g).astype(BF16)
            du_refs[q][...] = du
            do_refs[q][...] = _dot(du, w_refs[q][...], NT).astype(do_dtypes[q])
            dz = d * _dot(o_refs[q][...], w_refs[q][...]) * g * (1.0 - g)
            dz_ref[:, cols] = dz.astype(BF16)
            db_ref[:, cols] += jnp.sum(dz, axis=0, keepdims=True)

    row = lambda w: pl.BlockSpec((tm, w), lambda i: (i, 0))
    whole = lambda a: pl.BlockSpec(a.shape, lambda i: (0, 0))
    in_specs = [row(n), row(nq * n), whole(bias)] + [row(o.shape[1]) for o in outs] + [whole(u) for u in ups]
    out_specs = [row(n)] * nq + [row(o.shape[1]) for o in outs] + [row(nq * n), pl.BlockSpec((1, nq * n), lambda i: (0, 0))]
    out_shape = ([jax.ShapeDtypeStruct((s, n), BF16)] * nq
                 + [jax.ShapeDtypeStruct((s, o.shape[1]), dt) for o, dt in zip(outs, do_dtypes)]
                 + [jax.ShapeDtypeStruct((s, nq * n), BF16), jax.ShapeDtypeStruct((1, nq * n), F32)])
    res = pl.pallas_call(body, name=name, grid=(s // tm,), in_specs=in_specs, out_specs=out_specs, out_shape=out_shape,
                         compiler_params=_params('arbitrary'))(dy, zg, bias, *outs, *ups)
    return res[:nq], res[nq:2 * nq], res[2 * nq], res[2 * nq + 1]


def _mn(col_off=0):
    return 'mn', col_off


def _nvec(col_off=0):
    return 'n', col_off


def _rowwise(fn, rows, consts, outs, sums=(), *, name, ts=256):
    views = [x if isinstance(x, Cols) else Cols(x, x.shape[1], 0) for x in rows]
    rows = [v.arr for v in views]
    r = rows[0].shape[0]
    ts = min(ts, r)
    assert r % ts == 0, (name, r, ts)
    nr, nc, no, ns = len(rows), len(consts), len(outs), len(sums)

    def body(*refs):
        vals = fn(*[x[...] for x in refs[:nr + nc]])
        for q in range(no):
            refs[nr + nc + q][...] = vals[q].astype(outs[q][1])
        if ns:
            @pl.when(pl.program_id(0) == 0)
            def _():
                for q in range(ns):
                    refs[nr + nc + no + q][...] = jnp.zeros((1, sums[q]), F32)

            for q in range(ns):
                refs[nr + nc + no + q][...] += jnp.sum(vals[no + q].astype(F32), axis=0, keepdims=True)

    in_specs = [pl.BlockSpec((ts, v.width), lambda i, blk=v.blk: (i, blk)) for v in views]
    in_specs += [pl.BlockSpec(x.shape, lambda i, nd=x.ndim: (0,) * nd) for x in consts]
    out_specs = [pl.BlockSpec((ts, w), lambda i: (i, 0)) for w, _ in outs]
    out_specs += [pl.BlockSpec((1, w), lambda i: (0, 0)) for w in sums]
    out_shape = [jax.ShapeDtypeStruct((r, w), dt) for w, dt in outs]
    out_shape += [jax.ShapeDtypeStruct((1, w), F32) for w in sums]
    return pl.pallas_call(body, name=name, grid=(r // ts,), in_specs=in_specs, out_specs=out_specs,
                          out_shape=out_shape, compiler_params=_params('arbitrary'))(*rows, *consts)


def _cumsum_rows(x, *, reverse, name, bs=256):
    s, w = x.shape
    bs = min(bs, s)
    nb = s // bs

    def body(x_ref, o_ref, carry):
        @pl.when(pl.program_id(0) == 0)
        def _():
            carry[...] = jnp.zeros_like(carry)

        r = lax.broadcasted_iota(jnp.int32, (bs, bs), 0)
        c = lax.broadcasted_iota(jnp.int32, (bs, bs), 1)
        tri = jnp.where((c >= r) if reverse else (c <= r), 1.0, 0.0).astype(BF16)
        xv = x_ref[...]
        o_ref[...] = _tri_dot(tri, xv) + carry[...]
        carry[...] += jnp.sum(xv, axis=0, keepdims=True)

    imap = (lambda i: (nb - 1 - i, 0)) if reverse else (lambda i: (i, 0))
    return pl.pallas_call(body, name=name, grid=(nb,), in_specs=[pl.BlockSpec((bs, w), imap)],
                          out_specs=pl.BlockSpec((bs, w), imap), out_shape=jax.ShapeDtypeStruct((s, w), F32),
                          scratch_shapes=[pltpu.VMEM((1, w), F32)], compiler_params=_params('arbitrary'))(x)


def _mask(mode, q0, k0, bq, bk):
    qpos = q0 + lax.broadcasted_iota(jnp.int32, (bq, bk), 0)
    kpos = k0 + lax.broadcasted_iota(jnp.int32, (bq, bk), 1)
    if mode == 'causal':
        return kpos <= qpos
    return kpos < (jnp.right_shift(qpos, int(math.log2(CHUNK))) + 1) * CHUNK


ROPE_SHIFT = int(math.log2(MLA_ROPE))
FOX_SCALE, MLA_SCALE, XA_SCALE = FOX_HD ** -0.5, (MLA_NOPE + MLA_ROPE) ** -0.5, XA_HD ** -0.5
ATTN_ROW_SLAB = 512


def _lane_masks(g, b, rope):
    lane = lax.broadcasted_iota(jnp.int32, (1, LANES), 1)
    heads = [None if g == 1 else (lane >= hh * (LANES // g)) & (lane < (hh + 1) * (LANES // g)) for hh in range(g)]
    ropes = [jnp.right_shift(lane, ROPE_SHIFT) == b * g + hh for hh in range(g)] if rope else [None] * g
    return heads, ropes


def _sel(mask, x):
    return x if mask is None else jnp.where(mask, x, jnp.zeros_like(x))


class Step(typing.NamedTuple):
    qi: typing.Any
    kj: typing.Any
    first: typing.Any
    last: typing.Any
    plain: typing.Any
    masked: typing.Any


def _fwd_steps(tri, nq, nk):
    if not tri:
        return (nq, nk), lambda i, j: Step(i, j, j == 0, j == nk - 1, True, False)
    if nq % 2:
        return (nq, nk), lambda i, j: Step(i, jnp.minimum(i, j), j == 0, j == nk - 1, j < i, j == i)

    def at(i, t):
        low = t <= i
        diag = (t == i) | (t == nq)
        return Step(jnp.where(low, i, nq - 1 - i), jnp.where(low, t, t - (i + 1)), (t == 0) | (t == i + 1), diag,
                    jnp.logical_not(diag), diag)

    return (nq // 2, nq + 1), at


def _bwd_steps(tri, nq, nk):
    if not tri:
        return (nk, nq), lambda j, i: Step(i, j, i == 0, i == nq - 1, True, False)
    if nk % 2:
        return (nk, nq), lambda j, i: Step(jnp.maximum(i, j), j, i == 0, i == nq - 1, i > j, i == j)

    def at(j, t):
        n1 = nq - j
        low = t < n1
        diag = (t == 0) | (t == n1)
        return Step(jnp.where(low, j + t, nk - 1 - j + t - n1), jnp.where(low, j, nk - 1 - j), diag,
                    (t == n1 - 1) | (t == nq), jnp.logical_not(diag), diag)

    return (nk // 2, nq + 1), at


def _carried(comm, refs, n_in, n_out):
    ci, co = len(comm.ins), len(comm.out_shapes)
    ins = refs[n_in:n_in + ci]
    outs = refs[n_in + ci + n_out:n_in + ci + n_out + co]
    rest = refs[:n_in] + refs[n_in + ci:n_in + ci + n_out] + refs[n_in + ci + n_out + co:-2]
    return rest, (ins, outs, refs[-2], refs[-1])


def _mattn_fwd(q, k, v, *, qc, kc, vc, nb, g, mode, name, dq_scale=1.0, ck=None, qr=None, qrc=0, kr=None, blk=512,
               comm=None):
    s, t = q.shape[0], k.shape[0]
    bq, bk = min(blk, s), min(blk, t)
    nq, nk = s // bq, t // bk
    tri = mode != 'full'
    bias, rope = ck is not None, qr is not None
    assert not tri or (bq == bk and bq % CHUNK == 0)
    rs = min(ATTN_ROW_SLAB, bq)
    n_in = 3 + bias + 2 * rope
    (n1, n2), step_at = _fwd_steps(tri, nq, nk)

    def body(*refs):
        refs = list(refs)
        b, p1, p2 = pl.program_id(0), pl.program_id(1), pl.program_id(2)
        st = step_at(p1, p2)
        i, j = st.qi, st.kj
        if comm is not None:
            refs, comm_refs = _carried(comm, refs, n_in, 2)
            pl.when((b == 0) & (p1 == 0) & (p2 == 0))(lambda: comm.start(*comm_refs))
        q_ref, k_ref, v_ref = refs[:3]
        pos = 3
        ck_ref = qr_ref = kr_ref = None
        if bias:
            ck_ref = refs[pos]
            pos += 1
        if rope:
            qr_ref, kr_ref = refs[pos:pos + 2]
            pos += 2
        o_ref, lse_ref, m_s, l_s, acc_s = refs[pos:]
        heads, ropes = _lane_masks(g, b, rope)

        @pl.when(st.first)
        def _():
            m_s[...] = jnp.full_like(m_s, MASK_VALUE)
            l_s[...] = jnp.zeros_like(l_s)
            acc_s[...] = jnp.zeros_like(acc_s)

        def compute(masked):
            k2, v2 = k_ref[...], v_ref[...]
            for r in range(bq // rs):
                rows = pl.ds(r * rs, rs)
                q2 = q_ref[rows, :]
                alphas, pvs = [], []
                for hh in range(g):
                    sc = _dot(_sel(heads[hh], q2), k2, NT)
                    if rope:
                        sc = sc + _dot(_sel(ropes[hh], qr_ref[rows, :]), kr_ref[...], NT)
                    if bias:
                        sc = sc - ck_ref[0, hh:hh + 1, :]
                    if masked:
                        sc = jnp.where(_mask(mode, i * bq + r * rs, j * bk, rs, bk), sc, MASK_VALUE)
                    m_prev = m_s[hh, rows]
                    m_new = jnp.maximum(m_prev, jnp.max(sc, axis=1, keepdims=True))
                    alpha = jnp.exp(m_prev - m_new)
                    p = jnp.exp(sc - m_new)
                    l_s[hh, rows] = alpha * l_s[hh, rows] + jnp.sum(p, axis=1, keepdims=True)
                    m_s[hh, rows] = m_new
                    alphas.append(alpha)
                    pvs.append(_dot(p.astype(BF16), _sel(heads[hh], v2)))
                alpha = alphas[0]
                for hh in range(1, g):
                    alpha = jnp.where(heads[hh], alphas[hh], alpha)
                acc_s[rows, :] = acc_s[rows, :] * alpha + sum(pvs[1:], pvs[0])

        if tri:
            pl.when(st.plain)(functools.partial(compute, False))
            pl.when(st.masked)(functools.partial(compute, True))
        else:
            compute(False)

        @pl.when(st.last)
        def _():
            lane = lax.broadcasted_iota(jnp.int32, (bq, LANES), 1)
            l_full, lse = l_s[0], jnp.zeros((bq, LANES), F32)
            for hh in range(g):
                if hh:
                    l_full = jnp.where(heads[hh], l_s[hh], l_full)
                lse = jnp.where(lane == hh, m_s[hh] + jnp.log(l_s[hh]), lse)
            o_ref[...] = (acc_s[...] / l_full).astype(o_ref.dtype)
            lse_ref[...] = lse

        if comm is not None:
            pl.when((b == nb - 1) & (p1 == n1 - 1) & (p2 == n2 - 1))(lambda: comm.finish(*comm_refs))

    qi = lambda p1, p2: step_at(p1, p2).qi
    kj = lambda p1, p2: step_at(p1, p2).kj
    in_specs = [pl.BlockSpec((bq, LANES), lambda b, p1, p2: (qi(p1, p2), qc + b)),
                pl.BlockSpec((bk, LANES), lambda b, p1, p2: (kj(p1, p2), kc + b)),
                pl.BlockSpec((bk, LANES), lambda b, p1, p2: (kj(p1, p2), vc + b))]
    args = [q, k, v]
    if bias:
        in_specs.append(pl.BlockSpec((1, 8, bk), lambda b, p1, p2: (b, 0, kj(p1, p2))))
        args.append(ck)
    if rope:
        in_specs += [pl.BlockSpec((bq, LANES), lambda b, p1, p2: (qi(p1, p2), qrc)),
                     pl.BlockSpec((bk, LANES), lambda b, p1, p2: (kj(p1, p2), 0))]
        args += [qr, kr]
    out = pl.BlockSpec((bq, LANES), lambda b, p1, p2: (qi(p1, p2), b))
    out_specs = [out, out]
    out_shape = [jax.ShapeDtypeStruct((s, LANES * nb), BF16), jax.ShapeDtypeStruct((s, LANES * nb), F32)]
    scratch = [pltpu.VMEM((g, bq, 1), F32), pltpu.VMEM((g, bq, 1), F32), pltpu.VMEM((bq, LANES), F32)]
    if comm is not None:
        in_specs += [ANY] * len(comm.ins)
        args += comm.ins
        out_specs += [ANY] * len(comm.out_shapes)
        out_shape += comm.out_shapes
        scratch += _sems(comm.n_sems, comm.n_sems)
    res = pl.pallas_call(body, name=name, grid=(nb, n1, n2), in_specs=in_specs, out_specs=out_specs, out_shape=out_shape,
                         scratch_shapes=scratch, compiler_params=_params('arbitrary', 'arbitrary', 'arbitrary'))(*args)
    return res if comm is None else (res[0], res[1], res[2:])


def _mattn_bwd(q, k, v, o, do, lse, *, qc, kc, vc, nb, g, mode, name, dq_scale=1.0, ck=None, qr=None, qrc=0, kr=None,
               blk=512, comm=None):
    s, t = q.shape[0], k.shape[0]
    bq, bk = min(blk, s), min(blk, t)
    nq, nk = s // bq, t // bk
    tri = mode != 'full'
    bias, rope = ck is not None, qr is not None
    rs = min(ATTN_ROW_SLAB, bq)
    n_in, n_out = 6 + bias + 2 * rope, 3 + 2 * bias + 2 * rope
    (n1, n2), step_at = _bwd_steps(tri, nq, nk)

    def body(*refs):
        refs = list(refs)
        if comm is not None:
            refs, comm_refs = _carried(comm, refs, n_in, n_out)
            first = (pl.program_id(0) == 0) & (pl.program_id(1) == 0) & (pl.program_id(2) == 0)
            pl.when(first)(lambda: comm.start(*comm_refs))
        q_ref, k_ref, v_ref, o_ref, do_ref, lse_ref = refs[:6]
        pos = 6
        ck_ref = qr_ref = kr_ref = dck_ref = dcq_ref = dqr_ref = dkr_ref = dck_s = None
        if bias:
            ck_ref = refs[pos]
            pos += 1
        if rope:
            qr_ref, kr_ref = refs[pos:pos + 2]
            pos += 2
        dq_ref, dk_ref, dv_ref = refs[pos:pos + 3]
        pos += 3
        if bias:
            dck_ref, dcq_ref = refs[pos:pos + 2]
            pos += 2
        if rope:
            dqr_ref, dkr_ref = refs[pos:pos + 2]
            pos += 2
        dk_s, dv_s = refs[pos:pos + 2]
        if bias:
            dck_s = refs[pos + 2]
        b, p1, p2 = pl.program_id(0), pl.program_id(1), pl.program_id(2)
        st = step_at(p1, p2)
        i, j = st.qi, st.kj
        heads, ropes = _lane_masks(g, b, rope)

        @pl.when((p1 == 0) & (p2 == 0))
        def _():
            dq_ref[...] = jnp.zeros_like(dq_ref)
            if bias:
                dcq_ref[...] = jnp.zeros_like(dcq_ref)

        if rope:
            @pl.when((b == 0) & (p1 == 0) & (p2 == 0))
            def _():
                dqr_ref[...] = jnp.zeros_like(dqr_ref)
                dkr_ref[...] = jnp.zeros_like(dkr_ref)

        @pl.when(st.first)
        def _():
            dk_s[...] = jnp.zeros_like(dk_s)
            dv_s[...] = jnp.zeros_like(dv_s)
            if bias:
                dck_s[...] = jnp.zeros_like(dck_s)

        def compute(masked):
            k2, v2 = k_ref[...], v_ref[...]
            lane = lax.broadcasted_iota(jnp.int32, (rs, LANES), 1)
            rk = pl.ds(pl.multiple_of(j * bk, bk), bk)
            add = lambda tot, x: x if tot is None else tot + x
            dv_t = dk_t = dkr_t = None
            dck_t = [None] * g
            for r in range(bq // rs):
                rows = pl.ds(r * rs, rs)
                rq = pl.ds(pl.multiple_of(i * bq + r * rs, rs), rs)
                q2, do2, lse2 = q_ref[rows, :], do_ref[rows, :], lse_ref[rows, :]
                dd = do2.astype(F32) * o_ref[rows, :].astype(F32)
                dq_t = dqr_t = dcq_t = None
                for hh in range(g):
                    qm = _sel(heads[hh], q2)
                    sc = _dot(qm, k2, NT)
                    if rope:
                        qrm = _sel(ropes[hh], qr_ref[rows, :])
                        sc = sc + _dot(qrm, kr_ref[...], NT)
                    if bias:
                        sc = sc - ck_ref[0, hh:hh + 1, :]
                    if masked:
                        sc = jnp.where(_mask(mode, i * bq + r * rs, j * bk, rs, bk), sc, MASK_VALUE)
                    p = jnp.exp(sc - jnp.sum(jnp.where(lane == hh, lse2, 0.0), axis=1, keepdims=True))
                    dom = _sel(heads[hh], do2)
                    dp = _dot(dom, v2, NT)
                    delta = jnp.sum(_sel(heads[hh], dd), axis=1, keepdims=True)
                    ds = p * (dp - delta)
                    dsb = ds.astype(BF16)
                    dv_t = add(dv_t, _dot(p.astype(BF16), dom, TN))
                    dk_t = add(dk_t, _dot(dsb, qm, TN))
                    dq_t = add(dq_t, _dot(dsb, _sel(heads[hh], k2)))
                    if rope:
                        dqr_t = add(dqr_t, _dot(dsb, _sel(ropes[hh], kr_ref[...])))
                        dkr_t = add(dkr_t, _dot(dsb, qrm, TN))
                    if bias:
                        dck_t[hh] = add(dck_t[hh], jnp.sum(ds, axis=0, keepdims=True))
                        dcq_t = add(dcq_t, jnp.where(lane == hh, jnp.sum(ds, axis=1, keepdims=True), 0.0))
                dq_ref[rq, :] += dq_t if dq_scale == 1.0 else dq_scale * dq_t
                if rope:
                    dqr_ref[rq, :] += dq_scale * dqr_t
                if bias:
                    dcq_ref[rq, :] += dcq_t
            dv_s[...] += dv_t
            dk_s[...] += dk_t
            if rope:
                dkr_ref[rk, :] += dkr_t
            if bias:
                for hh in range(g):
                    dck_s[hh:hh + 1, :] -= dck_t[hh]

        if tri:
            pl.when(st.plain)(functools.partial(compute, False))
            pl.when(st.masked)(functools.partial(compute, True))
        else:
            compute(False)

        @pl.when(st.last)
        def _():
            dk_ref[...] = dk_s[...]
            dv_ref[...] = dv_s[...]
            if bias:
                dck_ref[0] = dck_s[...]

        if comm is not None:
            pl.when((b == nb - 1) & (p1 == n1 - 1) & (p2 == n2 - 1))(lambda: comm.finish(*comm_refs))

    qrow = lambda col: pl.BlockSpec((bq, LANES), lambda b, p1, p2: (step_at(p1, p2).qi, col(b)))
    krow = lambda col: pl.BlockSpec((bk, LANES), lambda b, p1, p2: (step_at(p1, p2).kj, col(b)))
    in_specs = [qrow(lambda b: qc + b), krow(lambda b: kc + b), krow(lambda b: vc + b), qrow(lambda b: b),
                qrow(lambda b: b), qrow(lambda b: b)]
    args = [q, k, v, o, do, lse]
    whole = lambda rows: pl.BlockSpec((rows, LANES), lambda b, j, i: (0, b))
    out_specs = [whole(s), krow(lambda b: b), krow(lambda b: b)]
    out_shape = [jax.ShapeDtypeStruct((s, LANES * nb), F32), jax.ShapeDtypeStruct((t, LANES * nb), F32),
                 jax.ShapeDtypeStruct((t, LANES * nb), F32)]
    scratch = [pltpu.VMEM((bk, LANES), F32), pltpu.VMEM((bk, LANES), F32)]
    if bias:
        ckj = pl.BlockSpec((1, 8, bk), lambda b, p1, p2: (b, 0, step_at(p1, p2).kj))
        in_specs.append(ckj)
        args.append(ck)
        out_specs += [ckj, whole(s)]
        out_shape += [jax.ShapeDtypeStruct((nb, 8, t), F32), jax.ShapeDtypeStruct((s, LANES * nb), F32)]
    if rope:
        in_specs += [qrow(lambda b: qrc), krow(lambda b: 0)]
        args += [qr, kr]
        out_specs += [pl.BlockSpec((s, LANES), lambda b, j, i: (0, 0)), pl.BlockSpec((t, LANES), lambda b, j, i: (0, 0))]
        out_shape += [jax.ShapeDtypeStruct((s, LANES), F32), jax.ShapeDtypeStruct((t, LANES), F32)]
    if bias:
        scratch.append(pltpu.VMEM((8, bk), F32))
    if comm is not None:
        in_specs += [ANY] * len(comm.ins)
        args += comm.ins
        out_specs += [ANY] * len(comm.out_shapes)
        out_shape += comm.out_shapes
        scratch += _sems(comm.n_sems, comm.n_sems)
    res = pl.pallas_call(body, name=name, grid=(nb, n1, n2), in_specs=in_specs, out_specs=out_specs,
                         out_shape=out_shape, scratch_shapes=scratch,
                         compiler_params=_params('arbitrary', 'arbitrary', 'arbitrary'))(*args)
    return res if comm is None else (*res[:n_out], res[n_out:])


def _gla_chunk(la_c, k_c):
    r = lax.broadcasted_iota(jnp.int32, (CHUNK, CHUNK), 0)
    c = lax.broadcasted_iota(jnp.int32, (CHUNK, CHUNK), 1)
    tri = jnp.where(c <= r, 1.0, 0.0).astype(BF16)
    cum = _tri_dot(tri, la_c)
    end = jnp.sum(la_c, axis=0, keepdims=True)
    dec = jnp.exp(end - cum)
    return dec, k_c * dec, jnp.exp(end)


GLA_PAIRS = GLA_HEADS // 2


def _gla_fwd(z, la, *, qc, kc, vc, name, blk=512):
    s = z.shape[0]
    bs = min(blk, s)
    ncb = bs // CHUNK
    nblk = s // bs

    def body(q_ref, k_ref, va_ref, vb_ref, la_ref, o_ref, st_ref, st):
        @pl.when(pl.program_id(1) == 0)
        def _():
            st[...] = jnp.zeros_like(st)

        heads, _ = _lane_masks(2, 0, False)
        v_refs = (va_ref, vb_ref)
        for c in range(ncb):
            sl = pl.ds(c * CHUNK, CHUNK)
            _, kf, a = _gla_chunk(la_ref[sl, :], k_ref[sl, :])
            qs = q_ref[sl, :] * (GLA_DK ** -0.5)
            for hh in range(2):
                ut = _dot(v_refs[hh][sl, :].astype(BF16), _sel(heads[hh], kf).astype(BF16), TN)
                new = a * st[hh] + ut
                st[hh] = new
                st_ref[0, c, hh] = new
                o_ref[sl, hh * GLA_DV:(hh + 1) * GLA_DV] = _dot(_sel(heads[hh], qs).astype(BF16), new.astype(BF16), NT)

    col = lambda c0, m=1: pl.BlockSpec((bs, LANES), lambda b, i: (i, c0 + m * b))
    return pl.pallas_call(
        body, name=name, grid=(GLA_PAIRS, nblk),
        in_specs=[col(qc), col(kc), col(vc, 2), col(vc + 1, 2), col(0)],
        out_specs=[pl.BlockSpec((bs, 2 * GLA_DV), lambda b, i: (i, b)),
                   pl.BlockSpec((1, ncb, 2, GLA_DV, LANES), lambda b, i: (b, i, 0, 0, 0))],
        out_shape=[jax.ShapeDtypeStruct((s, GLA_HEADS * GLA_DV), F32),
                   jax.ShapeDtypeStruct((GLA_PAIRS, s // CHUNK, 2, GLA_DV, LANES), F32)],
        scratch_shapes=[pltpu.VMEM((2, GLA_DV, LANES), F32)],
        compiler_params=_params('arbitrary', 'arbitrary'))(z, z, z, z, la)


def _gla_bwd(z, la, st_all, st_prev, do, *, qc, kc, vc, name, blk=512):
    s = z.shape[0]
    bs = min(blk, s)
    ncb = bs // CHUNK
    nblk = s // bs

    def body(q_ref, k_ref, va_ref, vb_ref, la_ref, st_ref, sp_ref, do_ref, dq_ref, dk_ref, dv_ref, dla_ref, ga):
        @pl.when(pl.program_id(1) == 0)
        def _():
            ga[...] = jnp.zeros_like(ga)

        r = lax.broadcasted_iota(jnp.int32, (CHUNK, CHUNK), 0)
        cc = lax.broadcasted_iota(jnp.int32, (CHUNK, CHUNK), 1)
        tri_rev = jnp.where(cc >= r, 1.0, 0.0).astype(BF16)
        heads, _ = _lane_masks(2, 0, False)
        v_refs = (va_ref, vb_ref)
        for c in reversed(range(ncb)):
            sl = pl.ds(c * CHUNK, CHUNK)
            dec, kf, a = _gla_chunk(la_ref[sl, :], k_ref[sl, :])
            qs = q_ref[sl, :] * (GLA_DK ** -0.5)
            dq2 = jnp.zeros((CHUNK, LANES), F32)
            dkd = jnp.zeros((CHUNK, LANES), F32)
            da = jnp.zeros((1, LANES), F32)
            for hh in range(2):
                hv = slice(hh * GLA_DV, (hh + 1) * GLA_DV)
                dob = do_ref[sl, hv].astype(BF16)
                g = _dot(dob, _sel(heads[hh], qs).astype(BF16), TN) + ga[hh]
                gb = g.astype(BF16)
                dq2 = dq2 + _dot(dob, st_ref[0, c, hh].astype(BF16))
                dv_ref[sl, hv] = _dot(_sel(heads[hh], kf).astype(BF16), gb, NT)
                dkd = dkd + _dot(v_refs[hh][sl, :].astype(BF16), gb)
                da = da + jnp.sum(g * sp_ref[0, c, hh], axis=0, keepdims=True)
                ga[hh] = a * g
            dq_ref[sl, :] = (GLA_DK ** -0.5) * dq2
            dk_ref[sl, :] = dkd * dec
            e = dkd * kf
            dend = jnp.sum(e, axis=0, keepdims=True) + da * a
            dla_ref[sl, :] = dend - _tri_dot(tri_rev, e)

    rev = lambda i: nblk - 1 - i
    col = lambda c0, m=1: pl.BlockSpec((bs, LANES), lambda b, i: (rev(i), c0 + m * b))
    wide = pl.BlockSpec((bs, 2 * GLA_DV), lambda b, i: (rev(i), b))
    stspec = pl.BlockSpec((1, ncb, 2, GLA_DV, LANES), lambda b, i: (b, rev(i), 0, 0, 0))
    return pl.pallas_call(
        body, name=name, grid=(GLA_PAIRS, nblk),
        in_specs=[col(qc), col(kc), col(vc, 2), col(vc + 1, 2), col(0), stspec, stspec, wide],
        out_specs=[col(0), col(0), wide, col(0)],
        out_shape=[jax.ShapeDtypeStruct((s, GLA_HEADS * GLA_DK), F32), jax.ShapeDtypeStruct((s, GLA_HEADS * GLA_DK), F32),
                   jax.ShapeDtypeStruct((s, GLA_HEADS * GLA_DV), F32), jax.ShapeDtypeStruct((s, GLA_HEADS * GLA_DK), F32)],
        scratch_shapes=[pltpu.VMEM((2, GLA_DV, LANES), F32)],
        compiler_params=_params('arbitrary', 'arbitrary'))(z, z, z, z, la, st_all, st_prev, do)


def _place():
    return lax.axis_index('x'), lax.axis_index('y'), lax.axis_index('c')


ANY = pl.BlockSpec(memory_space=pl.ANY)


def _all_gather8(blk, *, name):
    m, n = blk.shape

    def body(x_ref, out_ref, send_sems, recv_sems, local_sem):
        x, y, c = _place()
        me, sibling = (x, y, c), (x, y, 1 - c)
        chips = [(1 - x, y), (x, 1 - y), (1 - x, 1 - y)]

        def slot(px, py, pc):
            return out_ref.at[4 * px + 2 * py + pc]

        def copy(q, block, to, src=None):
            return pltpu.make_async_remote_copy(
                src_ref=slot(*block) if src is None else src, dst_ref=slot(*block), send_sem=send_sems.at[q],
                recv_sem=recv_sems.at[q], device_id=to, device_id_type=MESH)

        mine = pltpu.make_async_copy(x_ref, slot(*me), local_sem)
        mine.start()
        first = [copy(0, me, sibling, src=x_ref)]
        first += [copy(1 + q, me, (*chip, c), src=x_ref) for q, chip in enumerate(chips)]
        for cp in first:
            cp.start()
        passed = [copy(4 + q, (*chip, c), sibling) for q, chip in enumerate(chips)]
        for q, chip in enumerate(chips):
            copy(1 + q, (*chip, c), me).wait_recv()
            passed[q].start()
        copy(0, sibling, me).wait_recv()
        for q, chip in enumerate(chips):
            copy(4 + q, (*chip, 1 - c), me).wait_recv()
        for cp in first + passed:
            cp.wait_send()
        mine.wait()

    return pl.pallas_call(
        body, name=name, in_specs=[ANY], out_specs=ANY, out_shape=jax.ShapeDtypeStruct((N_DEV, m, n), blk.dtype),
        scratch_shapes=[pltpu.SemaphoreType.DMA((7,)), pltpu.SemaphoreType.DMA((7,)), pltpu.SemaphoreType.DMA(())],
    )(blk)


def _sems(*counts):
    return [pltpu.SemaphoreType.DMA((n,)) for n in counts]


class Comm(typing.NamedTuple):
    ins: list
    out_shapes: list
    n_sems: int
    start: typing.Callable
    finish: typing.Callable


def _remote(src, dst, send_sems, recv_sems, idx, to):
    return lambda: pltpu.make_async_remote_copy(src_ref=src, dst_ref=dst, send_sem=send_sems.at[idx],
                                                recv_sem=recv_sems.at[idx], device_id=to, device_id_type=MESH)


def _comm_from(copies, ins, out_shapes, n_sems):
    def start(*refs):
        for cp in copies(*refs)[0]:
            cp().start()

    def finish(*refs):
        sent, received = copies(*refs)
        for cp in received:
            cp().wait_recv()
        for cp in sent:
            cp().wait_send()

    return Comm(list(ins), list(out_shapes), n_sems, start, finish)


def _run_comm(comm, *, name, alias=False):
    n_in, n_out = len(comm.ins), len(comm.out_shapes)

    def body(*refs):
        ins, outs, sems = refs[:n_in], refs[n_in:n_in + n_out], refs[n_in + n_out:]
        comm.start(ins, outs, *sems)
        comm.finish(ins, outs, *sems)

    return pl.pallas_call(body, name=name, in_specs=[ANY] * n_in, out_specs=[ANY] * n_out, out_shape=comm.out_shapes,
                          input_output_aliases={q: q for q in range(n_in)} if alias else {},
                          scratch_shapes=_sems(comm.n_sems, comm.n_sems))(*comm.ins)


def _half(rows, c):
    h = rows // 2
    return pl.ds(pl.multiple_of(c * h, h), h)


def _gathered(ref, chip, rows, side):
    if not side:
        return ref.at[chip, rows]
    n = ref.shape[1] // N_CHIPS
    return ref.at[rows, pl.ds(pl.multiple_of(chip * n, n), n)]


def _gather_over_ici(ws, side):
    def copies(ins, outs, send_sems, recv_sems):
        x, y, c = _place()
        me_chip = 2 * x + y
        sent, received = [], []
        for q, w in enumerate(ws):
            half, every = _half(w.shape[0], c), pl.ds(0, w.shape[0])
            for k, (px, py) in enumerate([(1 - x, y), (x, 1 - y), (1 - x, 1 - y)]):
                sent.append(_remote(ins[q].at[half], _gathered(outs[q], me_chip, half, side[q]), send_sems, recv_sems,
                                    4 * q + k, (px, py, c)))
                slot = _gathered(outs[q], 2 * px + py, half, side[q])
                received.append(_remote(slot, slot, send_sems, recv_sems, 4 * q + k, (px, py, c)))
            whole = _remote(ins[q], _gathered(outs[q], me_chip, every, side[q]), send_sems, recv_sems, 4 * q + 3,
                            (x, y, 1 - c))
            sent.append(whole)
            received.append(whole)
        return sent, received

    shapes = [jax.ShapeDtypeStruct((w.shape[0], N_CHIPS * w.shape[1]) if sd else (N_CHIPS,) + w.shape, w.dtype)
              for w, sd in zip(ws, side)]
    return _comm_from(copies, ws, shapes, 4 * len(ws))


def _gather_over_d2d(parts, side):
    def copies(ins, outs, send_sems, recv_sems):
        x, y, c = _place()
        sent, received = [], []
        for q, w in enumerate(parts):
            rows = w.shape[0] if side[q] else w.shape[1]
            for k, (px, py) in enumerate([(1 - x, y), (x, 1 - y), (1 - x, 1 - y)]):
                mine = _gathered(outs[q], 2 * px + py, _half(rows, c), side[q])
                theirs = _gathered(outs[q], 2 * px + py, _half(rows, 1 - c), side[q])
                sent.append(_remote(mine, mine, send_sems, recv_sems, 3 * q + k, (x, y, 1 - c)))
                received.append(_remote(theirs, theirs, send_sems, recv_sems, 3 * q + k, (x, y, 1 - c)))
        return sent, received

    return _comm_from(copies, parts, [jax.ShapeDtypeStruct(w.shape, w.dtype) for w in parts], 3 * len(parts))


def _to_sibling(gs, *, name):
    n = len(gs)

    def body(*refs):
        ins, outs = refs[:n], refs[n:2 * n]
        send_sems, recv_sems = refs[2 * n:]
        x, y, c = _place()
        cps = [pltpu.make_async_remote_copy(
            src_ref=ins[q], dst_ref=outs[q], send_sem=send_sems.at[q], recv_sem=recv_sems.at[q],
            device_id=(x, y, 1 - c), device_id_type=MESH) for q in range(n)]
        for cp in cps:
            cp.start()
        for cp in cps:
            cp.wait()

    return pl.pallas_call(body, name=name, in_specs=[ANY] * n, out_specs=[ANY] * n,
                          out_shape=[jax.ShapeDtypeStruct(g.shape, g.dtype) for g in gs],
                          scratch_shapes=_sems(n, n))(*gs)


def _chip_exchange(ps):
    def copies(ins, outs, send_sems, recv_sems):
        x, y, c = _place()
        cps = [_remote(ins[q].at[2 * px + py], outs[q].at[k], send_sems, recv_sems, 3 * q + k, (px, py, c))
               for q in range(len(ps)) for k, (px, py) in enumerate([(1 - x, y), (x, 1 - y), (1 - x, 1 - y)])]
        return cps, cps

    return _comm_from(copies, ps, [jax.ShapeDtypeStruct((3,) + p.shape[1:], p.dtype) for p in ps], 3 * len(ps))


def _sum_chips(own, r, *, name, ts=256):
    k, n = own.shape
    ts = min(ts, k)

    def body(own_ref, r_ref, o_ref):
        f = lambda q: r_ref[q].astype(F32)
        o_ref[...] = ((own_ref[...].astype(F32) + f(0)) + f(1)) + f(2)

    return pl.pallas_call(
        body, name=name, grid=(k // ts,),
        in_specs=[pl.BlockSpec((ts, n), lambda i: (i, 0)), pl.BlockSpec((3, ts, n), lambda i: (0, i, 0))],
        out_specs=pl.BlockSpec((ts, n), lambda i: (i, 0)), out_shape=jax.ShapeDtypeStruct((k, n), F32),
        compiler_params=_params('arbitrary'))(own, r)


WIN_SHARD = N_IN // N_CHIPS
WIN_PAD = -(-WIN_SHARD // LANES) * LANES
GATE_WIRE_ROWS = 32


def _full_layer(sh, axis):
    _, k, n = sh.shape
    if axis == 2:
        return sh.transpose(1, 0, 2).reshape(k, N_CHIPS * n)
    return sh.reshape(N_CHIPS * k, n)


def _win_cols(wp, o, n):
    parts = []
    while n > 0:
        j, r = divmod(o, WIN_SHARD)
        take = min(n, WIN_SHARD - r)
        parts.append(wp[:, j * WIN_PAD + r:j * WIN_PAD + r + take])
        o, n = o + take, n - take
    return parts[0] if len(parts) == 1 else jnp.concatenate(parts, axis=1)


def _split_full(full, axis):
    if full.ndim == 3:
        return full
    k, n = full.shape
    if axis == 2:
        return jnp.stack([full[:, j * (n // N_CHIPS):(j + 1) * (n // N_CHIPS)] for j in range(N_CHIPS)])
    return full.reshape(N_CHIPS, k // N_CHIPS, n)


def _padc(a, w):
    return jnp.pad(a, ((0, 0), (0, w - a.shape[1])))


def _swap16(a):
    return jnp.concatenate([a[..., 16:32], a[..., 0:16]], axis=-1)


B_GR, B_GQ, B_GK, B_GV, B_MQ, B_MKR, B_MKRS, B_FF, B_GLOW, B_MKV, B_END = (
    0, 512, 768, 1024, 1536, 1792, 1920, 2048, 2176, 2304, 2432)
B_W = 2560
O_FQ, O_FF, O_GQ, O_GLOW, O_GR, O_MQ, O_MKV, O_MKR, O_ZG = 0, 768, 772, 1796, 1812, 2324, 2580, 2708, 2740


def _repack_layer_weights(w):
    wi = functools.partial(_win_cols, w['w_in'])
    out = dict(w)
    out['in_a'] = jnp.concatenate([wi(O_FQ, 256) * FOX_SCALE, wi(O_FQ + 256, 512)], axis=1)
    kr = wi(O_MKR, 32)
    out['in_b'] = jnp.concatenate([
        wi(O_GR, 512), wi(O_GQ, 1024), wi(O_MQ, 256), jnp.tile(kr, (1, MLA_HEADS)), jnp.tile(_swap16(kr), (1, MLA_HEADS)),
        _padc(wi(O_FF, 4), 128), _padc(wi(O_GLOW, 16), 128), wi(O_MKV, 128),
        jnp.zeros((D_MODEL, B_W - B_END), kr.dtype)], axis=1)
    out['in_c'] = wi(O_ZG, 3072)
    uq = w['w_mla_uq'].reshape(MLA_Q_RANK, MLA_HEADS, MLA_NOPE + MLA_ROPE)
    rope = uq[:, :, MLA_NOPE:]
    out['uq'] = jnp.concatenate([uq[:, :, :MLA_NOPE].reshape(MLA_Q_RANK, -1), rope.reshape(MLA_Q_RANK, -1),
                                 _swap16(rope).reshape(MLA_Q_RANK, -1)], axis=1)
    ukv = w['w_mla_ukv'].reshape(MLA_KV_RANK, MLA_HEADS, MLA_NOPE + MLA_VD)
    out['ukv'] = jnp.concatenate([ukv[:, :, :MLA_NOPE].reshape(MLA_KV_RANK, -1),
                                  ukv[:, :, MLA_NOPE:].reshape(MLA_KV_RANK, -1)], axis=1)
    out['gate'] = jnp.pad(w['w_gla_gate'], ((0, 128 - GLA_RANK), (0, 0)))
    return out


def _unpack_layer_grads(g):
    a, b, c = g['in_a'], g['in_b'], g['in_c']
    fold = lambda o: sum(b[:, o + MLA_ROPE * q:o + MLA_ROPE * (q + 1)] for q in range(MLA_HEADS))
    kr = fold(B_MKR) + _swap16(fold(B_MKRS))
    pieces = [(a[:, :256] * FOX_SCALE, 0, 256), (a, 256, 512), (b, B_FF, 4), (b, B_GQ, 1024), (b, B_GLOW, 16),
              (b, B_GR, 512), (b, B_MQ, 256), (b, B_MKV, 128), (kr, 0, 32), (c, 0, 3072)]
    shards = []
    for j in range(N_CHIPS):
        lo, hi, cut, at = j * WIN_SHARD, (j + 1) * WIN_SHARD, [], 0
        for arr, first, width in pieces:
            l, h = max(lo, at), min(hi, at + width)
            if l < h:
                cut.append(arr[:, first + l - at:first + h - at])
            at += width
        shards.append(jnp.concatenate(cut, axis=1))
    w_in = jnp.stack(shards)
    uq = g['uq']
    nope = uq[:, :256].reshape(MLA_Q_RANK, MLA_HEADS, MLA_NOPE)
    rope = (uq[:, 256:384].reshape(MLA_Q_RANK, MLA_HEADS, MLA_ROPE)
            + _swap16(uq[:, 384:512].reshape(MLA_Q_RANK, MLA_HEADS, MLA_ROPE)))
    w_uq = jnp.concatenate([nope, rope], axis=2).reshape(MLA_Q_RANK, -1)
    ukv = g['ukv']
    w_ukv = jnp.concatenate([ukv[:, :256].reshape(MLA_KV_RANK, MLA_HEADS, MLA_NOPE),
                             ukv[:, 256:].reshape(MLA_KV_RANK, MLA_HEADS, MLA_VD)], axis=2).reshape(MLA_KV_RANK, -1)
    out = {'w_in': w_in, 'w_mla_uq': w_uq, 'w_mla_ukv': w_ukv, 'w_gla_gate': g['gate'][:GLA_RANK]}
    for nm in ('w_up_fox', 'w_up_gla', 'w_up_mla', 'w_out', 'w_xq', 'w_xkv', 'w_xo', 'w_mlp1', 'w_mlp2'):
        out[nm] = g[nm]
    return out


def _rope_tables(s):
    half = MLA_ROPE // 2
    inv = ROPE_BASE ** (-jnp.arange(half, dtype=F32) / half)
    ang = jnp.arange(s).astype(F32)[:, None] * inv[None, :]
    cos, sin = jnp.cos(ang), jnp.sin(ang)
    c1 = jnp.concatenate([cos, cos], axis=1)
    s1 = jnp.concatenate([-sin, sin], axis=1)
    return jnp.tile(c1, (1, MLA_HEADS)), jnp.tile(s1, (1, MLA_HEADS))


def _rms_bwd(x, dh, g):
    r = lax.rsqrt(jnp.mean(x * x, axis=-1, keepdims=True) + EPS)
    xh = x * r
    gd = dh * g
    return r * (gd - xh * jnp.mean(gd * xh, axis=-1, keepdims=True)), dh * xh


def _norm_bwd_epilogue(dh, x, dres, g):
    dx, dg = _rms_bwd(x, dh, g)
    return dres + dx, dg


def _norm_bwd_call(x, dh, g, dres, name):
    w = x.width if isinstance(x, Cols) else x.shape[1]

    def with_res(xv, dv, rv, gv):
        dx, dg = _rms_bwd(xv, dv.astype(F32), gv)
        return rv + dx, dg

    def plain(xv, dv, gv):
        return _rms_bwd(xv, dv.astype(F32), gv)

    if dres is None:
        return _rowwise(plain, [x, dh], [g], [(w, F32)], [w], name=name)
    return _rowwise(with_res, [x, dh, dres], [g], [(w, F32)], [w], name=name)


def _gla_out_fwd(oraw, gr, g_out):
    outs = []
    for hh in range(GLA_HEADS):
        sl = slice(hh * GLA_DV, (hh + 1) * GLA_DV)
        oh = oraw[:, sl]
        n = oh * lax.rsqrt(jnp.mean(oh * oh, axis=-1, keepdims=True) + EPS) * g_out
        r = gr[:, sl]
        outs.append(n * (r * _sig(r)))
    return (jnp.concatenate(outs, axis=1),)


def _gla_out_bwd(oraw, gr, dout, g_out):
    d_o, d_r, dg = [], [], 0.0
    for hh in range(GLA_HEADS):
        sl = slice(hh * GLA_DV, (hh + 1) * GLA_DV)
        oh, r, do = oraw[:, sl], gr[:, sl], dout[:, sl].astype(F32)
        rs = lax.rsqrt(jnp.mean(oh * oh, axis=-1, keepdims=True) + EPS)
        sg = _sig(r)
        dn = do * (r * sg)
        d_r.append(do * (oh * rs * g_out) * (sg + r * sg * (1.0 - sg)))
        dx, dgh = _rms_bwd(oh, dn, g_out)
        d_o.append(dx)
        dg = dg + dgh
    return jnp.concatenate(d_o, axis=1), jnp.concatenate(d_r, axis=1), dg


def _adam(w, g, m, v):
    m = ADAM_B1 * m + (1.0 - ADAM_B1) * g
    v = ADAM_B2 * v + (1.0 - ADAM_B2) * (g * g)
    m_hat = m / (1.0 - ADAM_B1 ** ADAM_STEP)
    v_hat = v / (1.0 - ADAM_B2 ** ADAM_STEP)
    return -ADAM_LR * (m_hat / (jnp.sqrt(v_hat) + ADAM_EPS) + ADAM_WD * w), m, v


def _layer_fwd(x, mem, w, p, tabs, tag, carry_fox=None, after_fox=None, carry_mla=None):
    c4, s4 = tabs
    sv = {'x0': x}
    nm = lambda t: f'{t}_{tag}'
    za, h = _mm(x, w['in_a'], mode='nn', out_dtype=BF16, norm_g=p['g_mix'], emit_norm=True, name=nm('in_a'))
    zb = _mm(h, w['in_b'], mode='nn', out_dtype=F32, name=nm('in_b'))
    zc = _mm(h, w['in_c'], mode='nn', out_dtype=F32, name=nm('in_c'))
    sv.update(h=h, zc=zc)
    ff = Cols(zb, 128, B_FF // 128)
    (lf,) = _rowwise(lambda f, b: (_logsig(f + b),), [ff], [p['b_fox']], [(128, F32)], name=nm('fox_lf'))
    cum = _cumsum_rows(lf, reverse=False, name=nm('fox_cum'))
    ckf = jnp.pad(cum[:, :FOX_HEADS].T.reshape(2, 2, x.shape[0]), ((0, 0), (0, 6), (0, 0)))
    fox = dict(qc=0, kc=2, vc=4, nb=2, g=2, mode='causal', ck=ckf)
    o_fox, lse_fox, *carried = _mattn_fwd(za, za, za, name=nm('fox_attn'), comm=carry_fox, **fox)
    if after_fox is not None:
        w = {**w, **after_fox(carried[0])}
    sv.update(ff=ff, za=za, fox=fox, o_fox=o_fox, lse_fox=lse_fox)
    glow = Cols(zb, 128, B_GLOW // 128)
    gr = Cols(zb, 512, B_GR // 512)

    def gate_fn(gl, wg, bg):
        return (_logsig(_dot(gl.astype(BF16), wg) + bg) / GLA_TAU,)

    (la,) = _rowwise(gate_fn, [glow], [w['gate'], p['b_gla']], [(256, F32)], name=nm('gla_gate'))
    gla = dict(qc=B_GQ // LANES, kc=B_GK // LANES, vc=B_GV // LANES)
    oraw, states = _gla_fwd(zb, la, name=nm('gla'), **gla)
    (o_gla,) = _rowwise(_gla_out_fwd, [oraw, gr], [p['g_gla_out']], [(512, BF16)], name=nm('gla_out'))
    sv.update(glow=glow, gr=gr, zb=zb, la=la, gla=gla, states=states, oraw=oraw, o_gla=o_gla)
    mq = Cols(zb, 256, B_MQ // 256)
    mkv = Cols(zb, 128, B_MKV // 128)
    mkr2 = Cols(zb, 256, B_MKR // 256)
    qp, cqn = _mm(mq, w['uq'], mode='nn', out_dtype=F32, norm_g=p['g_mla_q'], emit_norm=True, name=nm('mla_uq'))
    kvp, ckvn = _mm(mkv, w['ukv'], mode='nn', out_dtype=BF16, norm_g=p['g_mla_kv'], emit_norm=True,
                    name=nm('mla_ukv'))

    def rope_fn(qv, kr, c4v, s4v):
        q_rope = qv[:, 256:384] * c4v + qv[:, 384:512] * s4v
        q_scaled = jnp.concatenate([qv[:, 0:256], q_rope], axis=1) * MLA_SCALE
        return q_scaled, kr[:, 0:128] * c4v + kr[:, 128:256] * s4v

    qall, kr4 = _rowwise(rope_fn, [qp, mkr2, c4, s4], [], [(384, BF16), (128, BF16)], name=nm('rope'))
    mla = dict(qc=0, kc=0, vc=2, nb=2, g=2, dq_scale=MLA_SCALE, mode='chunk', qr=qall, qrc=2, kr=kr4)
    o_mla, lse_mla, *carried = _mattn_fwd(qall, kvp, kvp, name=nm('mla_attn'), comm=carry_mla, **mla)
    if carry_mla is not None:
        sv['carried_mla'] = carried[0]
    sv.update(mq=mq, mkv=mkv, cqn=cqn, ckvn=ckvn, qall=qall, kvp=kvp, mla=mla, o_mla=o_mla, lse_mla=lse_mla)
    of_m, om_m = o_fox, o_mla
    sv.update(of_m=of_m, om_m=om_m)
    b_br = p['b_branch']

    y = _gated_merge([of_m, o_gla, om_m], [w['w_up_fox'], w['w_up_gla'], w['w_up_mla']], zc, b_br, name=nm('up_merge'))
    add = lambda acc, res: res + acc
    x1 = _mm(y, w['w_out'], mode='nn', out_dtype=F32, name=nm('out'), epilogue=add, extras=[(x, *_mn())])
    sv.update(y=y, x1=x1)
    qx, hx = _mm(x1, w['w_xq'], mode='nn', out_dtype=BF16, norm_g=p['g_xa'], emit_norm=True, name=nm('xq'),
                 epilogue=lambda acc: acc * XA_SCALE)
    kvx, mn = _mm(mem, w['w_xkv'], mode='nn', out_dtype=BF16, norm_g=p['g_mem'], emit_norm=True, name=nm('xkv'))
    xa = dict(qc=0, kc=0, vc=4, nb=4, g=1, dq_scale=XA_SCALE, mode='full')
    ox_m, lse_x = _mattn_fwd(qx, kvx, kvx, name=nm('xa_attn'), **xa)
    x2 = _mm(ox_m, w['w_xo'], mode='nn', out_dtype=F32, name=nm('xo'), epilogue=add, extras=[(x1, *_mn())])
    sv.update(hx=hx, mn=mn, qx=qx, kvx=kvx, xa=xa, lse_x=lse_x, ox_m=ox_m, x2=x2)
    hpre, hm = _mm(x2, w['w_mlp1'], mode='nn', out_dtype=BF16, norm_g=p['g_mlp'], emit_norm=True, name=nm('mlp1'))
    relu2 = lambda t: jnp.square(jnp.maximum(t.astype(F32), 0.0))
    x3 = _mm(hpre, w['w_mlp2'], mode='nn', out_dtype=F32, name=nm('mlp2'), a_fn=relu2, epilogue=add,
             extras=[(x2, *_mn())])
    sv.update(hpre=hpre, hm=hm, w=w)
    return x3, sv


EARLY = ('w_mlp1', 'w_mlp2', 'w_xo', 'w_xq', 'w_xkv', 'w_out', 'w_up_fox', 'w_up_gla', 'w_up_mla')
LATE = ('w_in', 'w_gla_gate', 'w_mla_uq', 'w_mla_ukv')


def _layer_bwd(dx3, mem, w, p, tabs, sv, tag, carry_mla=None, early=None):
    c4, s4 = tabs
    nm = lambda t: f'{t}_{tag}'
    s = dx3.shape[0]
    gw, gs = {}, {}
    relu2 = lambda t: jnp.square(jnp.maximum(t.astype(F32), 0.0))
    gw['w_mlp2'] = _mm(sv['hpre'], dx3, mode='tn', out_dtype=F32, name=nm('d_mlp2'), a_fn=relu2)
    dact = lambda acc, hp: acc * (2.0 * jnp.maximum(hp.astype(F32), 0.0))
    dhpre = _mm(dx3, w['w_mlp2'], mode='nt', out_dtype=BF16, name=nm('d_act'), epilogue=dact,
                extras=[(sv['hpre'], *_mn())])
    gw['w_mlp1'] = _mm(sv['hm'], dhpre, mode='tn', out_dtype=F32, name=nm('d_mlp1'))
    dx2, gs['g_mlp'] = _mm(dhpre, w['w_mlp1'], mode='nt', out_dtype=F32, name=nm('d_hm'), epilogue=_norm_bwd_epilogue,
                           col_sums=True, full_rows=True,
                           extras=[(sv['x2'], *_mn()), (dx3, *_mn()), (p['g_mlp'], *_nvec())])
    gw['w_xo'] = _mm(sv['ox_m'], dx2, mode='tn', out_dtype=F32, name=nm('d_xo'))
    dox = _mm(dx2, w['w_xo'], mode='nt', out_dtype=BF16, name=nm('d_ox'))
    dqx_m, dkx, dvx = _mattn_bwd(sv['qx'], sv['kvx'], sv['kvx'], sv['ox_m'], dox, sv['lse_x'], name=nm('xa_bwd'),
                                 **sv['xa'])
    dkvx = jnp.concatenate([dkx, dvx], axis=1).astype(BF16)
    gw['w_xq'] = _mm(sv['hx'], dqx_m, mode='tn', out_dtype=F32, name=nm('d_xq'))
    dx1, gs['g_xa'] = _mm(dqx_m, w['w_xq'], mode='nt', out_dtype=F32, name=nm('d_hx'), epilogue=_norm_bwd_epilogue,
                          col_sums=True, full_rows=True,
                          extras=[(sv['x1'], *_mn()), (dx2, *_mn()), (p['g_xa'], *_nvec())])
    gw['w_xkv'] = _mm(sv['mn'], dkvx, mode='tn', out_dtype=F32, name=nm('d_xkv'))
    dmn = _mm(dkvx, w['w_xkv'], mode='nt', out_dtype=F32, name=nm('d_mn'))
    _, gs['g_mem'] = _norm_bwd_call(mem, dmn, p['g_mem'], None, nm('d_norm_mem'))
    gw['w_out'] = _mm(sv['y'], dx1, mode='tn', out_dtype=F32, name=nm('d_out'))
    dy = _mm(dx1, w['w_out'], mode='nt', out_dtype=BF16, name=nm('d_y'))
    zc, b_br = sv['zc'], p['b_branch']

    branches = (('w_up_fox', sv['of_m'], BF16), ('w_up_gla', sv['o_gla'], F32), ('w_up_mla', sv['om_m'], BF16))
    du, do_br, dzc, gs['b_branch'] = _gated_merge_bwd(dy, zc, b_br, [o for _, o, _ in branches],
                                                      [w[wn] for wn, _, _ in branches], [dt for _, _, dt in branches],
                                                      name=nm('d_merge'))
    for q, (wn, o_m, _) in enumerate(branches):
        gw[wn] = _mm(o_m, du[q], mode='tn', out_dtype=F32, name=nm(f'd_up{q}'))
    za = sv['za']
    carry_fox = None if early is None else early({nm_: gw[nm_] for nm_ in EARLY})
    dfq, dfk, dfv, dck, dcq, *carried_fox = _mattn_bwd(za, za, za, sv['o_fox'], do_br[0], sv['lse_fox'],
                                                       name=nm('fox_bwd'), comm=carry_fox, **sv['fox'])
    dcum = _padc(dck[:, :2, :].reshape(FOX_HEADS, s).T + dcq.reshape(s, 2, LANES)[:, :, :2].reshape(s, FOX_HEADS), 128)
    dlf = _cumsum_rows(dcum, reverse=True, name=nm('fox_dcum'))

    def dff_fn(dl, f, b):
        d = dl * _sig(-(f + b))
        return d, d

    dff, db_fox = _rowwise(dff_fn, [dlf, sv['ff']], [p['b_fox']], [(128, F32)], [128], name=nm('fox_dff'))
    gs['b_fox'] = db_fox
    dza = jnp.concatenate([dfq, dfk, dfv], axis=1).astype(BF16)
    dqn, dkn, dvv, dq_rope, dk_rope, *carried_mla = _mattn_bwd(sv['qall'], sv['kvp'], sv['kvp'], sv['o_mla'], do_br[2],
                                                               sv['lse_mla'], name=nm('mla_bwd'), comm=carry_mla,
                                                               **sv['mla'])

    def drope_fn(dn, dq, dk, c4v, s4v):
        return jnp.concatenate([dn, dq * c4v, dq * s4v], axis=1), jnp.concatenate([dk * c4v, dk * s4v], axis=1)

    dqp, dmkr2 = _rowwise(drope_fn, [dqn, dq_rope, dk_rope, c4, s4], [], [(512, BF16), (256, BF16)], name=nm('d_rope'))
    dkvp = jnp.concatenate([dkn, dvv], axis=1).astype(BF16)
    gw['uq'] = _mm(sv['cqn'], dqp, mode='tn', out_dtype=F32, name=nm('d_uq'))
    dcqn = _mm(dqp, w['uq'], mode='nt', out_dtype=F32, name=nm('d_cqn'))
    gw['ukv'] = _mm(sv['ckvn'], dkvp, mode='tn', out_dtype=F32, name=nm('d_ukv'))
    dckvn = _mm(dkvp, w['ukv'], mode='nt', out_dtype=F32, name=nm('d_ckvn'))
    dmq, gs['g_mla_q'] = _norm_bwd_call(sv['mq'], dcqn, p['g_mla_q'], None, nm('d_norm_q'))
    dmkv, gs['g_mla_kv'] = _norm_bwd_call(sv['mkv'], dckvn, p['g_mla_kv'], None, nm('d_norm_kv'))
    doraw, dgr, gs['g_gla_out'] = _rowwise(_gla_out_bwd, [sv['oraw'], sv['gr'], do_br[1]], [p['g_gla_out']],
                                           [(512, F32), (512, BF16)], [128], name=nm('d_gla_out'))
    st = sv['states']
    st_prev = jnp.concatenate([jnp.zeros_like(st[:, :1]), st[:, :-1]], axis=1)
    dgq, dgk, dgv, dla = _gla_bwd(sv['zb'], sv['la'], st, st_prev, doraw, name=nm('gla_bwd'), **sv['gla'])

    def dgate_fn(dl, gl, wg, bg):
        pre = _dot(gl.astype(BF16), wg) + bg
        dpre = dl * (1.0 / GLA_TAU) * _sig(-pre)
        return dpre, _dot(dpre.astype(BF16), wg, NT), dpre

    dpre, dglow, gs['b_gla'] = _rowwise(dgate_fn, [dla, sv['glow']], [w['gate'], p['b_gla']],
                                        [(256, BF16), (128, BF16)], [256], name=nm('d_gla_gate'))
    gw['gate'] = _mm(sv['glow'], dpre, mode='tn', out_dtype=F32, name=nm('d_wgate'))
    bf = lambda t: t.astype(BF16)
    dzb = jnp.concatenate([dgr, bf(dgq), bf(dgk), bf(dgv), bf(dmq), dmkr2, bf(dff), dglow, bf(dmkv),
                           jnp.zeros((s, B_W - B_END), BF16)], axis=1)
    h = sv['h']
    gw['in_a'] = _mm(h, dza, mode='tn', out_dtype=F32, name=nm('d_in_a'))
    gw['in_b'] = _mm(h, dzb, mode='tn', out_dtype=F32, name=nm('d_in_b'))
    gw['in_c'] = _mm(h, dzc, mode='tn', out_dtype=F32, name=nm('d_in_c'))
    add = lambda acc, prev: prev + acc
    dh = _mm(dza, w['in_a'], mode='nt', out_dtype=F32, name=nm('d_h_a'))
    dh = _mm(dzb, w['in_b'], mode='nt', out_dtype=F32, name=nm('d_h_b'), epilogue=add, extras=[(dh, *_mn())])
    dx0, gs['g_mix'] = _mm(dzc, w['in_c'], mode='nt', out_dtype=F32, name=nm('d_h_c'), col_sums=True, full_rows=True,
                           epilogue=lambda acc, prev, xv, rv, gv: _norm_bwd_epilogue(prev + acc, xv, rv, gv),
                           extras=[(dh, *_mn()), (sv['x0'], *_mn()), (dx1, *_mn()), (p['g_mix'], *_nvec())])
    return dx0, gw, gs, (carried_mla or [None])[0], (carried_fox or [None])[0]


def _loss_head(x, target, g_final):
    d = x.shape[1]

    def fn(xv, tv, gv):
        r = lax.rsqrt(jnp.mean(xv * xv, axis=-1, keepdims=True) + EPS)
        xh = xv * r
        e = xh * gv - tv
        dy = e * (1.0 / d)
        gd = dy * gv
        dx = r * (gd - xh * jnp.mean(gd * xh, axis=-1, keepdims=True))
        row_loss = 0.5 * jnp.mean(e * e, axis=-1, keepdims=True)
        return dx, dy * xh, jnp.broadcast_to(row_loss, (xv.shape[0], LANES))

    return _rowwise(fn, [x, target], [g_final], [(d, F32)], [d, LANES], name='loss_head')


def _step(args):
    shapes = {nm: args[nm].shape for nm in ORDER}
    x, mem, target = args['x'][0], args['mem'][0], args['loss_target'][0]
    s = x.shape[0]

    def wire(nm, l):
        w = args[nm][l].astype(BF16)
        if nm == 'w_in':
            w = jnp.pad(w, ((0, 0), (0, WIN_PAD - WIN_SHARD)))
        if nm == 'w_gla_gate':
            w = jnp.pad(w, ((0, GATE_WIRE_ROWS - GLA_RANK), (0, 0)))
        return w

    axis_of = dict(BIG)
    names = tuple(nm for nm, _ in BIG)
    wires = lambda l, nms: [wire(nm, l) for nm in nms]
    width = lambda nm: WIN_PAD if nm == 'w_in' else args[nm].shape[2]
    side_by_side = lambda nms: [axis_of[nm] == 2 and width(nm) % LANES == 0 for nm in nms]
    over_ici = lambda l, nms: _gather_over_ici(wires(l, nms), side_by_side(nms))

    def whole(parts, nms, tag):
        side = side_by_side(nms)
        parts = _run_comm(_gather_over_d2d(parts, side), name=f'gather_d2d_{tag}', alias=True)
        full = {nm: p if sd else _full_layer(p, axis_of[nm]) for nm, p, sd in zip(nms, parts, side)}
        if 'w_gla_gate' in full:
            full['w_gla_gate'] = full['w_gla_gate'][:GLA_RANK]
        return full

    tabs = _rope_tables(s)
    layers_p = []
    for l in range(DEPTH):
        layers_p.append({
            'g_mix': args['g_mix'][l][None], 'b_fox': _padc(args['b_fox_forget'][l][None], 128),
            'b_gla': args['b_gla_gate'][l][None], 'g_gla_out': args['g_gla_out'][l][None],
            'g_mla_q': args['g_mla_q'][l][None], 'g_mla_kv': args['g_mla_kv'][l][None],
            'b_branch': args['b_branch_gate'][l][None], 'g_xa': args['g_xa'][l][None],
            'g_mem': args['g_mem'][l][None], 'g_mlp': args['g_mlp'][l][None]})

    first = _run_comm(over_ici(0, LATE), name='gather_ici_first_l0')
    w_now = _repack_layer_weights(whole(first, LATE, 'first_l0'))
    saved = []
    xl = x
    for l in range(DEPTH):
        carry_fox = over_ici(0, EARLY) if l == 0 else None
        after_fox = (lambda parts: whole(parts, EARLY, 'rest_l0')) if l == 0 else None
        carry_mla = over_ici(l + 1, names) if l + 1 < DEPTH else None
        xl, sv = _layer_fwd(xl, mem, w_now, layers_p[l], tabs, f'l{l}', carry_fox=carry_fox, after_fox=after_fox,
                            carry_mla=carry_mla)
        saved.append(sv)
        if carry_mla is not None:
            w_now = _repack_layer_weights(whole(sv.pop('carried_mla'), names, f'l{l + 1}'))
    dx, dg_final, loss_lanes = _loss_head(xl, target, args['g_final'][None])
    cidx = lax.axis_index('c')
    chip = 2 * lax.axis_index('x') + lax.axis_index('y')

    def pair_sums(gw, nms, tag):
        mine, theirs = [], []
        for nm in nms:
            shards = _split_full(gw[nm], axis_of[nm]).astype(BF16)
            h = shards.shape[1] // 2
            mine.append(lax.dynamic_slice_in_dim(shards, cidx * h, h, axis=1))
            theirs.append(lax.dynamic_slice_in_dim(shards, (1 - cidx) * h, h, axis=1))
        got = _to_sibling(theirs, name=f'grads_swap_{tag}')
        pairs = []
        for nm, a, b in zip(nms, mine, got):
            _, h, n = a.shape
            (p,) = _rowwise(lambda u, v: (u.astype(F32) + v.astype(F32),),
                            [a.reshape(N_CHIPS * h, n), b.reshape(N_CHIPS * h, n)], [], [(n, BF16)],
                            name=f'pair_sum_{nm}_{tag}')
            pairs.append(p.reshape(N_CHIPS, h, n))
        return pairs

    def finish(pairs, from_chips, nms, tag):
        own = [lax.dynamic_index_in_dim(p, chip, axis=0, keepdims=False) for p in pairs]
        mine = [_sum_chips(o, r, name=f'chip_sum_{nm}_{tag}') for nm, o, r in zip(nms, own, from_chips)]
        theirs = _to_sibling(mine, name=f'grads_join_{tag}')
        return {nm: jnp.where(cidx == 0, jnp.concatenate([a, b]), jnp.concatenate([b, a]))
                for nm, a, b in zip(nms, mine, theirs)}

    gs_layers, done = [None] * DEPTH, [{} for _ in range(DEPTH)]
    above = None
    for l in reversed(range(DEPTH)):
        lowest, early_pairs = l == 0, []

        def early(gw_early, l=l, early_pairs=early_pairs):
            early_pairs.extend(pair_sums(gw_early, EARLY, f'early_l{l}'))
            return _chip_exchange(early_pairs)

        carry_mla = None if above is None else _chip_exchange(above[1])
        dx, gw, gs_layers[l], got_mla, got_fox = _layer_bwd(
            dx, mem, saved[l]['w'], layers_p[l], tabs, saved[l], f'l{l}', carry_mla=carry_mla,
            early=early if lowest else None)
        if above is not None:
            done[above[0]].update(finish(above[1], got_mla, names, f'l{above[0]}'))
        grads = _unpack_layer_grads(gw)
        if lowest:
            done[l].update(finish(early_pairs, got_fox, EARLY, f'early_l{l}'))
            late_pairs = pair_sums(grads, LATE, f'late_l{l}')
            from_late = _run_comm(_chip_exchange(late_pairs), name=f'grads_exchange_late_l{l}')
            done[l].update(finish(late_pairs, from_late, LATE, f'late_l{l}'))
        else:
            above = (l, pair_sums(grads, names, f'l{l}'))
    grad_x = dx[None]
    gshard = {nm: jnp.stack([done[l][nm] for l in range(DEPTH)]) for nm in names}

    small_g = []
    for nm, key in (('g_mix', 'g_mix'), ('b_fox_forget', 'b_fox'), ('b_gla_gate', 'b_gla'),
                    ('g_gla_out', 'g_gla_out'), ('g_mla_q', 'g_mla_q'), ('g_mla_kv', 'g_mla_kv'),
                    ('b_branch_gate', 'b_branch'), ('g_xa', 'g_xa'), ('g_mem', 'g_mem'), ('g_mlp', 'g_mlp')):
        width = shapes[nm][1]
        small_g.append(jnp.concatenate([gs_layers[l][key][0, :width] for l in range(DEPTH)]))
    small_g.append(dg_final[0])
    small_g.append(loss_lanes[0, :1])
    flat = jnp.concatenate(small_g)
    n_small = flat.shape[0]
    srows = -(-n_small // (8 * LANES)) * 8
    pad = lambda v: jnp.pad(v, (0, srows * LANES - v.shape[0])).reshape(srows, LANES)
    all_small = _all_gather8(pad(flat), name='gather_small')
    sw, sm, svv = (pad(jnp.concatenate([args[pre + nm].reshape(-1) for nm in SMALL] + [jnp.zeros((1,), F32)]))
                   for pre in ('', 'm_', 'v_'))

    def small_body(g_ref, w_ref, m_ref, v_ref, go_ref, d_ref, mo_ref, vo_ref):
        g = g_ref[0]
        for q in range(1, N_DEV):
            g = g + g_ref[q]
        go_ref[...] = g
        d_ref[...], mo_ref[...], vo_ref[...] = _adam(w_ref[...], g, m_ref[...], v_ref[...])

    sg, sd, snm, snv = pl.pallas_call(
        small_body, name='small_sum_adam', out_shape=[jax.ShapeDtypeStruct((srows, LANES), F32)] * 4,
        compiler_params=pltpu.CompilerParams(vmem_limit_bytes=VMEM_LIMIT))(all_small, sw, sm, svv)

    def unsmall(buf):
        v, out, off = buf.reshape(-1), {}, 0
        for nm in SMALL:
            nel = math.prod(shapes[nm])
            out[nm] = v[off:off + nel].reshape(shapes[nm])
            off += nel
        return out, v[off]

    res = {}
    (res['grad'], loss), (res['delta'], _), (res['m'], _), (res['v'], _) = (unsmall(t) for t in (sg, sd, snm, snv))

    for nm, _ in BIG:
        shp = args[nm].shape
        view = lambda t: t.reshape(shp[0] * shp[1], shp[2])
        d, m2, v2 = _rowwise(_adam, [view(args[nm]), view(gshard[nm]), view(args['m_' + nm]), view(args['v_' + nm])],
                             [], [(shp[2], F32)] * 3, name=f'adam_{nm}')
        res['grad'][nm], res['delta'][nm], res['m'][nm], res['v'][nm] = (
            gshard[nm], d.reshape(shp), m2.reshape(shp), v2.reshape(shp))

    return (loss, grad_x, *[res['grad'][nm] for nm in ORDER], *[res['delta'][nm] for nm in ORDER],
            *[res['m'][nm] for nm in ORDER], *[res['v'][nm] for nm in ORDER])


def kernel(x, mem, g_mix, w_in, b_fox_forget, w_gla_gate, b_gla_gate, g_gla_out, g_mla_q, w_mla_uq, g_mla_kv, w_mla_ukv, b_branch_gate, w_up_fox, w_up_gla, w_up_mla, w_out, g_xa, g_mem, w_xq, w_xkv, w_xo, g_mlp, w_mlp1, w_mlp2, g_final, loss_target, m_g_mix, m_w_in, m_b_fox_forget, m_w_gla_gate, m_b_gla_gate, m_g_gla_out, m_g_mla_q, m_w_mla_uq, m_g_mla_kv, m_w_mla_ukv, m_b_branch_gate, m_w_up_fox, m_w_up_gla, m_w_up_mla, m_w_out, m_g_xa, m_g_mem, m_w_xq, m_w_xkv, m_w_xo, m_g_mlp, m_w_mlp1, m_w_mlp2, m_g_final, v_g_mix, v_w_in, v_b_fox_forget, v_w_gla_gate, v_b_gla_gate, v_g_gla_out, v_g_mla_q, v_w_mla_uq, v_g_mla_kv, v_w_mla_ukv, v_b_branch_gate, v_w_up_fox, v_w_up_gla, v_w_up_mla, v_w_out, v_g_xa, v_g_mem, v_w_xq, v_w_xkv, v_w_xo, v_g_mlp, v_w_mlp1, v_w_mlp2, v_g_final):
    return _step(dict(locals()))
```

```python
import functools
import math
import typing

import jax
import jax.numpy as jnp
from jax import lax
from jax.experimental import pallas as pl
from jax.experimental.pallas import tpu as pltpu

F32 = jnp.float32
BF16 = jnp.bfloat16
MESH = pl.DeviceIdType.MESH

D_MODEL = 1024
DEPTH = 2
CHUNK = 64
EPS = 1e-6
FOX_HEADS, FOX_HD = 4, 64
GLA_HEADS, GLA_DK, GLA_DV, GLA_RANK, GLA_TAU = 4, 64, 128, 16, 16.0
MLA_HEADS, MLA_Q_RANK, MLA_KV_RANK, MLA_NOPE, MLA_ROPE, MLA_VD = 4, 256, 128, 64, 32, 64
ROPE_BASE = 10000.0
XA_HEADS, XA_HD = 4, 128
D_FF = 4 * D_MODEL
IN_SIZES = (256, 256, 256, 4, 256, 256, 512, 16, 512, 256, 128, 32, 3072)
N_IN = sum(IN_SIZES)

ADAM_LR, ADAM_B1, ADAM_B2, ADAM_EPS, ADAM_WD, ADAM_STEP = 0.001, 0.9, 0.999, 1e-08, 0.01, 10

N_CHIPS = 4
N_DEV = 8
LANES = 128
VMEM_LIMIT = 48 * 1024 * 1024
MASK_VALUE = -1e30

BIG = (('w_in', 2), ('w_gla_gate', 2), ('w_mla_uq', 2), ('w_mla_ukv', 2), ('w_up_fox', 2), ('w_up_gla', 2),
       ('w_up_mla', 2), ('w_out', 1), ('w_xq', 1), ('w_xkv', 1), ('w_xo', 2), ('w_mlp1', 2), ('w_mlp2', 1))
SMALL = ('g_mix', 'b_fox_forget', 'b_gla_gate', 'g_gla_out', 'g_mla_q', 'g_mla_kv', 'b_branch_gate',
         'g_xa', 'g_mem', 'g_mlp', 'g_final')
ORDER = ('g_mix', 'w_in', 'b_fox_forget', 'w_gla_gate', 'b_gla_gate', 'g_gla_out', 'g_mla_q', 'w_mla_uq',
         'g_mla_kv', 'w_mla_ukv', 'b_branch_gate', 'w_up_fox', 'w_up_gla', 'w_up_mla', 'w_out', 'g_xa', 'g_mem',
         'w_xq', 'w_xkv', 'w_xo', 'g_mlp', 'w_mlp1', 'w_mlp2', 'g_final')


def _params(*sem, extra_vmem=0):
    return pltpu.CompilerParams(dimension_semantics=sem, vmem_limit_bytes=VMEM_LIMIT + extra_vmem)


def _sig(x):
    return 1.0 / (1.0 + jnp.exp(-x))


def _logsig(x):
    return jnp.minimum(x, 0.0) - jnp.log(1.0 + jnp.exp(-jnp.abs(x)))


NN = (((1,), (0,)), ((), ()))
NT = (((1,), (1,)), ((), ()))
TN = (((0,), (0,)), ((), ()))


def _dot(a, b, dims=NN):
    return lax.dot_general(a, b, dims, preferred_element_type=F32)


class Cols(typing.NamedTuple):
    arr: jax.Array
    width: int
    blk: int


def _tri_dot(tri, x):
    hi = x.astype(BF16)
    r1 = x - hi.astype(F32)
    mid = r1.astype(BF16)
    lo = (r1 - mid.astype(F32)).astype(BF16)
    return _dot(tri, hi) + _dot(tri, mid) + _dot(tri, lo)


MM_TILES = ((1024, 1024), (1024, 512), (512, 1024), (512, 512), (256, 1024), (512, 256), (256, 512), (256, 256),
            (128, 1024), (128, 128))
MM_VMEM_BUDGET = 38 * 1024 * 1024
MM_VMEM_EXTRA = 8 * 1024 * 1024


def _mm_tiles(m, n, k, a_bytes, b_bytes, out_bytes, ex_bytes, has_norm, emit_norm, has_fn, full_rows):
    for tm, tn in MM_TILES:
        tm, tn = min(tm, m), min(tn, n)
        if m % tm or n % tn or (full_rows and tn != n):
            continue
        blocks = tm * k * a_bytes + k * tn * b_bytes + tm * tn * (out_bytes + ex_bytes) + (tm * k * 2 if emit_norm else 0)
        temps = tm * tn * 4 + (tm * k * 2 if has_norm else 0) + (tm * k * 6 if has_fn or has_norm else 0)
        if 2 * blocks + temps <= MM_VMEM_BUDGET + (MM_VMEM_EXTRA if has_fn else 0):
            return tm, tn
    raise ValueError((m, n, k))


def _mm(a, b, *, mode, out_dtype, name, norm_g=None, emit_norm=False, a_fn=None, extras=(), epilogue=None,
        col_sums=False, full_rows=False):
    a_blk = 0
    if isinstance(a, Cols):
        a, width, a_blk = a
        a_shape = (a.shape[0], width)
    else:
        a_shape = a.shape
    if mode == 'tn':
        k, m = a_shape
    else:
        m, k = a_shape
    n = b.shape[0] if mode == 'nt' else b.shape[1]
    assert (b.shape[1] if mode == 'nt' else b.shape[0]) == k, (name, a.shape, b.shape)
    has_norm = norm_g is not None
    ex_bytes = sum(arr.dtype.itemsize for arr, kind, _ in extras if kind == 'mn')
    tm, tn = _mm_tiles(m, n, k, a.dtype.itemsize, b.dtype.itemsize, jnp.dtype(out_dtype).itemsize, ex_bytes, has_norm,
                       emit_norm, a_fn is not None, full_rows)
    assert all(col % tn == 0 for _, _, col in extras), (name, tn)
    assert a_blk == 0 or (mode == 'nn') or (mode == 'tn' and tm == m)
    assert not (col_sums and (has_norm or emit_norm))
    ij = (lambda f: lambda g0, g1: f(g1, g0)) if col_sums else (lambda f: f)
    spec = lambda blk, f: pl.BlockSpec(blk, ij(f))
    if mode == 'tn':
        a_spec = spec((k, tm), lambda i, j: (0, i + a_blk))
    else:
        a_spec = spec((tm, k), lambda i, j: (i, a_blk))
    b_spec = spec((tn, k), lambda i, j: (j, 0)) if mode == 'nt' else spec((k, tn), lambda i, j: (0, j))
    dims = {'nn': NN, 'nt': NT, 'tn': TN}[mode]
    assert not (has_norm and mode != 'nn')
    n_ex = len(extras)

    def body(*refs):
        a_ref, b_ref = refs[0], refs[1]
        pos = 2
        g_ref = None
        if has_norm:
            g_ref = refs[pos]
            pos += 1
        ex_refs = refs[pos:pos + n_ex]
        pos += n_ex
        o_ref = refs[pos]
        pos += 1
        h_ref = None
        if emit_norm:
            h_ref = refs[pos]
            pos += 1
        if has_norm:
            an_ref = refs[pos]

            @pl.when(pl.program_id(1) == 0)
            def _():
                xf = a_ref[...].astype(F32)
                y = xf * lax.rsqrt(jnp.mean(xf * xf, axis=-1, keepdims=True) + EPS) * g_ref[...]
                an_ref[...] = y.astype(BF16)
                if emit_norm:
                    h_ref[...] = y.astype(BF16)

            av = an_ref[...]
        else:
            av = a_ref[...]
            if a_fn is not None:
                av = a_fn(av)
            av = av.astype(BF16)
        acc = _dot(av, b_ref[...].astype(BF16), dims)
        if epilogue is not None:
            acc = epilogue(acc, *[r[...] for r in ex_refs])
        acc, to_sum = acc if isinstance(acc, tuple) else (acc, acc)
        o_ref[...] = acc.astype(out_dtype)
        if col_sums:
            sum_ref = refs[pos]

            @pl.when(pl.program_id(1) == 0)
            def _():
                sum_ref[...] = jnp.zeros_like(sum_ref)

            sum_ref[...] += jnp.sum(to_sum, axis=0, keepdims=True)

    in_specs = [a_spec, b_spec]
    args = [a, b]
    if has_norm:
        in_specs.append(pl.BlockSpec((1, k), lambda i, j: (0, 0)))
        args.append(norm_g)
    for arr, kind, col in extras:
        if kind == 'mn':
            in_specs.append(spec((tm, tn), lambda i, j, o=col // tn: (i, j + o)))
        else:
            in_specs.append(spec((1, tn), lambda i, j, o=col // tn: (0, j + o)))
        args.append(arr)
    out_shape = [jax.ShapeDtypeStruct((m, n), out_dtype)]
    out_specs = [spec((tm, tn), lambda i, j: (i, j))]
    if emit_norm:
        out_shape.append(jax.ShapeDtypeStruct((m, k), BF16))
        out_specs.append(pl.BlockSpec((tm, k), lambda i, j: (i, 0)))
    if col_sums:
        out_shape.append(jax.ShapeDtypeStruct((1, n), F32))
        out_specs.append(spec((1, tn), lambda i, j: (0, j)))
    scratch = [pltpu.VMEM((tm, k), BF16)] if has_norm else []
    grid = (n // tn, m // tm) if col_sums else (m // tm, n // tn)
    res = pl.pallas_call(
        body, name=name, grid=grid, in_specs=in_specs, out_specs=out_specs, out_shape=out_shape,
        scratch_shapes=scratch,
        compiler_params=_params('arbitrary', 'arbitrary', extra_vmem=MM_VMEM_EXTRA if a_fn is not None else 0))(*args)
    return res if emit_norm or col_sums else res[0]


def _gated_merge(outs, ups, zg, bias, *, name, tm=1024, tn=512):
    s, n, nq = zg.shape[0], ups[0].shape[1], len(outs)
    tm, tn = min(tm, s), min(tn, n)
    per = n // tn

    def body(*refs):
        y = None
        for q in range(nq):
            o_ref, w_ref, z_ref, b_ref = refs[q], refs[nq + q], refs[2 * nq + q], refs[3 * nq + q]
            term = _sig(z_ref[...].astype(F32) + b_ref[...]) * _dot(o_ref[...], w_ref[...])
            y = term if y is None else y + term
        refs[4 * nq][...] = y.astype(BF16)

    in_specs = [pl.BlockSpec((tm, o.shape[1]), lambda i, j: (i, 0)) for o in outs]
    in_specs += [pl.BlockSpec((u.shape[0], tn), lambda i, j: (0, j)) for u in ups]
    in_specs += [pl.BlockSpec((tm, tn), lambda i, j, q=q: (i, j + q * per)) for q in range(nq)]
    in_specs += [pl.BlockSpec((1, tn), lambda i, j, q=q: (0, j + q * per)) for q in range(nq)]
    return pl.pallas_call(body, name=name, grid=(s // tm, per), in_specs=in_specs,
                          out_specs=pl.BlockSpec((tm, tn), lambda i, j: (i, j)),
                          out_shape=jax.ShapeDtypeStruct((s, n), BF16),
                          compiler_params=_params('arbitrary', 'arbitrary'))(*outs, *ups, *[zg] * nq, *[bias] * nq)


def _gated_merge_bwd(dy, zg, bias, outs, ups, do_dtypes, *, name, tm=512):
    s, n = dy.shape
    nq = len(outs)
    tm = min(tm, s)

    def body(*refs):
        dy_ref, zg_ref, b_ref = refs[:3]
        o_refs, w_refs = refs[3:3 + nq], refs[3 + nq:3 + 2 * nq]
        du_refs, do_refs = refs[3 + 2 * nq:3 + 3 * nq], refs[3 + 3 * nq:3 + 4 * nq]
        dz_ref, db_ref = refs[3 + 4 * nq:]

        @pl.when(pl.program_id(0) == 0)
        def _():
            db_ref[...] = jnp.zeros_like(db_ref)

        d = dy_ref[...].astype(F32)
        for q in range(nq):
            cols = slice(q * n, (q + 1) * n)
            g = _sig(zg_ref[:, cols].astype(F32) + b_ref[:, cols])
            du = (d * g).astype(BF16)
            du_refs[q][...] = du
            do_refs[q][...] = _dot(du, w_refs[q][...], NT).astype(do_dtypes[q])
            dz = d * _dot(o_refs[q][...], w_refs[q][...]) * g * (1.0 - g)
            dz_ref[:, cols] = dz.astype(BF16)
            db_ref[:, cols] += jnp.sum(dz, axis=0, keepdims=True)

    row = lambda w: pl.BlockSpec((tm, w), lambda i: (i, 0))
    whole = lambda a: pl.BlockSpec(a.shape, lambda i: (0, 0))
    in_specs = [row(n), row(nq * n), whole(bias)] + [row(o.shape[1]) for o in outs] + [whole(u) for u in ups]
    out_specs = [row(n)] * nq + [row(o.shape[1]) for o in outs] + [row(nq * n), pl.BlockSpec((1, nq * n), lambda i: (0, 0))]
    out_shape = ([jax.ShapeDtypeStruct((s, n), BF16)] * nq
                 + [jax.ShapeDtypeStruct((s, o.shape[1]), dt) for o, dt in zip(outs, do_dtypes)]
                 + [jax.ShapeDtypeStruct((s, nq * n), BF16), jax.ShapeDtypeStruct((1, nq * n), F32)])
    res = pl.pallas_call(body, name=name, grid=(s // tm,), in_specs=in_specs, out_specs=out_specs, out_shape=out_shape,
                         compiler_params=_params('arbitrary'))(dy, zg, bias, *outs, *ups)
    return res[:nq], res[nq:2 * nq], res[2 * nq], res[2 * nq + 1]


def _mn(col_off=0):
    return 'mn', col_off


def _nvec(col_off=0):
    return 'n', col_off


def _rowwise(fn, rows, consts, outs, sums=(), *, name, ts=256):
    views = [x if isinstance(x, Cols) else Cols(x, x.shape[1], 0) for x in rows]
    rows = [v.arr for v in views]
    r = rows[0].shape[0]
    ts = min(ts, r)
    assert r % ts == 0, (name, r, ts)
    nr, nc, no, ns = len(rows), len(consts), len(outs), len(sums)

    def body(*refs):
        vals = fn(*[x[...] for x in refs[:nr + nc]])
        for q in range(no):
            refs[nr + nc + q][...] = vals[q].astype(outs[q][1])
        if ns:
            @pl.when(pl.program_id(0) == 0)
            def _():
                for q in range(ns):
                    refs[nr + nc + no + q][...] = jnp.zeros((1, sums[q]), F32)

            for q in range(ns):
                refs[nr + nc + no + q][...] += jnp.sum(vals[no + q].astype(F32), axis=0, keepdims=True)

    in_specs = [pl.BlockSpec((ts, v.width), lambda i, blk=v.blk: (i, blk)) for v in views]
    in_specs += [pl.BlockSpec(x.shape, lambda i, nd=x.ndim: (0,) * nd) for x in consts]
    out_specs = [pl.BlockSpec((ts, w), lambda i: (i, 0)) for w, _ in outs]
    out_specs += [pl.BlockSpec((1, w), lambda i: (0, 0)) for w in sums]
    out_shape = [jax.ShapeDtypeStruct((r, w), dt) for w, dt in outs]
    out_shape += [jax.ShapeDtypeStruct((1, w), F32) for w in sums]
    return pl.pallas_call(body, name=name, grid=(r // ts,), in_specs=in_specs, out_specs=out_specs,
                          out_shape=out_shape, compiler_params=_params('arbitrary'))(*rows, *consts)


def _cumsum_rows(x, *, reverse, name, bs=256):
    s, w = x.shape
    bs = min(bs, s)
    nb = s // bs

    def body(x_ref, o_ref, carry):
        @pl.when(pl.program_id(0) == 0)
        def _():
            carry[...] = jnp.zeros_like(carry)

        r = lax.broadcasted_iota(jnp.int32, (bs, bs), 0)
        c = lax.broadcasted_iota(jnp.int32, (bs, bs), 1)
        tri = jnp.where((c >= r) if reverse else (c <= r), 1.0, 0.0).astype(BF16)
        xv = x_ref[...]
        o_ref[...] = _tri_dot(tri, xv) + carry[...]
        carry[...] += jnp.sum(xv, axis=0, keepdims=True)

    imap = (lambda i: (nb - 1 - i, 0)) if reverse else (lambda i: (i, 0))
    return pl.pallas_call(body, name=name, grid=(nb,), in_specs=[pl.BlockSpec((bs, w), imap)],
                          out_specs=pl.BlockSpec((bs, w), imap), out_shape=jax.ShapeDtypeStruct((s, w), F32),
                          scratch_shapes=[pltpu.VMEM((1, w), F32)], compiler_params=_params('arbitrary'))(x)


def _mask(mode, q0, k0, bq, bk):
    qpos = q0 + lax.broadcasted_iota(jnp.int32, (bq, bk), 0)
    kpos = k0 + lax.broadcasted_iota(jnp.int32, (bq, bk), 1)
    if mode == 'causal':
        return kpos <= qpos
    return kpos < (jnp.right_shift(qpos, int(math.log2(CHUNK))) + 1) * CHUNK


ROPE_SHIFT = int(math.log2(MLA_ROPE))
FOX_SCALE, MLA_SCALE, XA_SCALE = FOX_HD ** -0.5, (MLA_NOPE + MLA_ROPE) ** -0.5, XA_HD ** -0.5
ATTN_ROW_SLAB = 512


def _lane_masks(g, b, rope):
    lane = lax.broadcasted_iota(jnp.int32, (1, LANES), 1)
    heads = [None if g == 1 else (lane >= hh * (LANES // g)) & (lane < (hh + 1) * (LANES // g)) for hh in range(g)]
    ropes = [jnp.right_shift(lane, ROPE_SHIFT) == b * g + hh for hh in range(g)] if rope else [None] * g
    return heads, ropes


def _sel(mask, x):
    return x if mask is None else jnp.where(mask, x, jnp.zeros_like(x))


class Step(typing.NamedTuple):
    qi: typing.Any
    kj: typing.Any
    first: typing.Any
    last: typing.Any
    plain: typing.Any
    masked: typing.Any


def _fwd_steps(tri, nq, nk):
    if not tri:
        return (nq, nk), lambda i, j: Step(i, j, j == 0, j == nk - 1, True, False)
    if nq % 2:
        return (nq, nk), lambda i, j: Step(i, jnp.minimum(i, j), j == 0, j == nk - 1, j < i, j == i)

    def at(i, t):
        low = t <= i
        diag = (t == i) | (t == nq)
        return Step(jnp.where(low, i, nq - 1 - i), jnp.where(low, t, t - (i + 1)), (t == 0) | (t == i + 1), diag,
                    jnp.logical_not(diag), diag)

    return (nq // 2, nq + 1), at


def _bwd_steps(tri, nq, nk):
    if not tri:
        return (nk, nq), lambda j, i: Step(i, j, i == 0, i == nq - 1, True, False)
    if nk % 2:
        return (nk, nq), lambda j, i: Step(jnp.maximum(i, j), j, i == 0, i == nq - 1, i > j, i == j)

    def at(j, t):
        n1 = nq - j
        low = t < n1
        diag = (t == 0) | (t == n1)
        return Step(jnp.where(low, j + t, nk - 1 - j + t - n1), jnp.where(low, j, nk - 1 - j), diag,
                    (t == n1 - 1) | (t == nq), jnp.logical_not(diag), diag)

    return (nk // 2, nq + 1), at


def _carried(comm, refs, n_in, n_out):
    ci, co = len(comm.ins), len(comm.out_shapes)
    ins = refs[n_in:n_in + ci]
    outs = refs[n_in + ci + n_out:n_in + ci + n_out + co]
    rest = refs[:n_in] + refs[n_in + ci:n_in + ci + n_out] + refs[n_in + ci + n_out + co:-2]
    return rest, (ins, outs, refs[-2], refs[-1])


def _mattn_fwd(q, k, v, *, qc, kc, vc, nb, g, mode, name, dq_scale=1.0, ck=None, qr=None, qrc=0, kr=None, blk=512,
               comm=None):
    s, t = q.shape[0], k.shape[0]
    bq, bk = min(blk, s), min(blk, t)
    nq, nk = s // bq, t // bk
    tri = mode != 'full'
    bias, rope = ck is not None, qr is not None
    assert not tri or (bq == bk and bq % CHUNK == 0)
    rs = min(ATTN_ROW_SLAB, bq)
    n_in = 3 + bias + 2 * rope
    (n1, n2), step_at = _fwd_steps(tri, nq, nk)

    def body(*refs):
        refs = list(refs)
        b, p1, p2 = pl.program_id(0), pl.program_id(1), pl.program_id(2)
        st = step_at(p1, p2)
        i, j = st.qi, st.kj
        if comm is not None:
            refs, comm_refs = _carried(comm, refs, n_in, 2)
            pl.when((b == 0) & (p1 == 0) & (p2 == 0))(lambda: comm.start(*comm_refs))
        q_ref, k_ref, v_ref = refs[:3]
        pos = 3
        ck_ref = qr_ref = kr_ref = None
        if bias:
            ck_ref = refs[pos]
            pos += 1
        if rope:
            qr_ref, kr_ref = refs[pos:pos + 2]
            pos += 2
        o_ref, lse_ref, m_s, l_s, acc_s = refs[pos:]
        heads, ropes = _lane_masks(g, b, rope)

        @pl.when(st.first)
        def _():
            m_s[...] = jnp.full_like(m_s, MASK_VALUE)
            l_s[...] = jnp.zeros_like(l_s)
            acc_s[...] = jnp.zeros_like(acc_s)

        def compute(masked):
            k2, v2 = k_ref[...], v_ref[...]
            for r in range(bq // rs):
                rows = pl.ds(r * rs, rs)
                q2 = q_ref[rows, :]
                alphas, pvs = [], []
                for hh in range(g):
                    sc = _dot(_sel(heads[hh], q2), k2, NT)
                    if rope:
                        sc = sc + _dot(_sel(ropes[hh], qr_ref[rows, :]), kr_ref[...], NT)
                    if bias:
                        sc = sc - ck_ref[0, hh:hh + 1, :]
                    if masked:
                        sc = jnp.where(_mask(mode, i * bq + r * rs, j * bk, rs, bk), sc, MASK_VALUE)
                    m_prev = m_s[hh, rows]
                    m_new = jnp.maximum(m_prev, jnp.max(sc, axis=1, keepdims=True))
                    alpha = jnp.exp(m_prev - m_new)
                    p = jnp.exp(sc - m_new)
                    l_s[hh, rows] = alpha * l_s[hh, rows] + jnp.sum(p, axis=1, keepdims=True)
                    m_s[hh, rows] = m_new
                    alphas.append(alpha)
                    pvs.append(_dot(p.astype(BF16), _sel(heads[hh], v2)))
                alpha = alphas[0]
                for hh in range(1, g):
                    alpha = jnp.where(heads[hh], alphas[hh], alpha)
                acc_s[rows, :] = acc_s[rows, :] * alpha + sum(pvs[1:], pvs[0])

        if tri:
            pl.when(st.plain)(functools.partial(compute, False))
            pl.when(st.masked)(functools.partial(compute, True))
        else:
            compute(False)

        @pl.when(st.last)
        def _():
            lane = lax.broadcasted_iota(jnp.int32, (bq, LANES), 1)
            l_full, lse = l_s[0], jnp.zeros((bq, LANES), F32)
            for hh in range(g):
                if hh:
                    l_full = jnp.where(heads[hh], l_s[hh], l_full)
                lse = jnp.where(lane == hh, m_s[hh] + jnp.log(l_s[hh]), lse)
            o_ref[...] = (acc_s[...] / l_full).astype(o_ref.dtype)
            lse_ref[...] = lse

        if comm is not None:
            pl.when((b == nb - 1) & (p1 == n1 - 1) & (p2 == n2 - 1))(lambda: comm.finish(*comm_refs))

    qi = lambda p1, p2: step_at(p1, p2).qi
    kj = lambda p1, p2: step_at(p1, p2).kj
    in_specs = [pl.BlockSpec((bq, LANES), lambda b, p1, p2: (qi(p1, p2), qc + b)),
                pl.BlockSpec((bk, LANES), lambda b, p1, p2: (kj(p1, p2), kc + b)),
                pl.BlockSpec((bk, LANES), lambda b, p1, p2: (kj(p1, p2), vc + b))]
    args = [q, k, v]
    if bias:
        in_specs.append(pl.BlockSpec((1, 8, bk), lambda b, p1, p2: (b, 0, kj(p1, p2))))
        args.append(ck)
    if rope:
        in_specs += [pl.BlockSpec((bq, LANES), lambda b, p1, p2: (qi(p1, p2), qrc)),
                     pl.BlockSpec((bk, LANES), lambda b, p1, p2: (kj(p1, p2), 0))]
        args += [qr, kr]
    out = pl.BlockSpec((bq, LANES), lambda b, p1, p2: (qi(p1, p2), b))
    out_specs = [out, out]
    out_shape = [jax.ShapeDtypeStruct((s, LANES * nb), BF16), jax.ShapeDtypeStruct((s, LANES * nb), F32)]
    scratch = [pltpu.VMEM((g, bq, 1), F32), pltpu.VMEM((g, bq, 1), F32), pltpu.VMEM((bq, LANES), F32)]
    if comm is not None:
        in_specs += [ANY] * len(comm.ins)
        args += comm.ins
        out_specs += [ANY] * len(comm.out_shapes)
        out_shape += comm.out_shapes
        scratch += _sems(comm.n_sems, comm.n_sems)
    res = pl.pallas_call(body, name=name, grid=(nb, n1, n2), in_specs=in_specs, out_specs=out_specs, out_shape=out_shape,
                         scratch_shapes=scratch, compiler_params=_params('arbitrary', 'arbitrary', 'arbitrary'))(*args)
    return res if comm is None else (res[0], res[1], res[2:])


def _mattn_bwd(q, k, v, o, do, lse, *, qc, kc, vc, nb, g, mode, name, dq_scale=1.0, ck=None, qr=None, qrc=0, kr=None,
               blk=512, comm=None):
    s, t = q.shape[0], k.shape[0]
    bq, bk = min(blk, s), min(blk, t)
    nq, nk = s // bq, t // bk
    tri = mode != 'full'
    bias, rope = ck is not None, qr is not None
    rs = min(ATTN_ROW_SLAB, bq)
    n_in, n_out = 6 + bias + 2 * rope, 3 + 2 * bias + 2 * rope
    (n1, n2), step_at = _bwd_steps(tri, nq, nk)

    def body(*refs):
        refs = list(refs)
        if comm is not None:
            refs, comm_refs = _carried(comm, refs, n_in, n_out)
            first = (pl.program_id(0) == 0) & (pl.program_id(1) == 0) & (pl.program_id(2) == 0)
            pl.when(first)(lambda: comm.start(*comm_refs))
        q_ref, k_ref, v_ref, o_ref, do_ref, lse_ref = refs[:6]
        pos = 6
        ck_ref = qr_ref = kr_ref = dck_ref = dcq_ref = dqr_ref = dkr_ref = dck_s = None
        if bias:
            ck_ref = refs[pos]
            pos += 1
        if rope:
            qr_ref, kr_ref = refs[pos:pos + 2]
            pos += 2
        dq_ref, dk_ref, dv_ref = refs[pos:pos + 3]
        pos += 3
        if bias:
            dck_ref, dcq_ref = refs[pos:pos + 2]
            pos += 2
        if rope:
            dqr_ref, dkr_ref = refs[pos:pos + 2]
            pos += 2
        dk_s, dv_s = refs[pos:pos + 2]
        if bias:
            dck_s = refs[pos + 2]
        b, p1, p2 = pl.program_id(0), pl.program_id(1), pl.program_id(2)
        st = step_at(p1, p2)
        i, j = st.qi, st.kj
        heads, ropes = _lane_masks(g, b, rope)

        @pl.when((p1 == 0) & (p2 == 0))
        def _():
            dq_ref[...] = jnp.zeros_like(dq_ref)
            if bias:
                dcq_ref[...] = jnp.zeros_like(dcq_ref)

        if rope:
            @pl.when((b == 0) & (p1 == 0) & (p2 == 0))
            def _():
                dqr_ref[...] = jnp.zeros_like(dqr_ref)
                dkr_ref[...] = jnp.zeros_like(dkr_ref)

        @pl.when(st.first)
        def _():
            dk_s[...] = jnp.zeros_like(dk_s)
            dv_s[...] = jnp.zeros_like(dv_s)
            if bias:
                dck_s[...] = jnp.zeros_like(dck_s)

        def compute(masked):
            k2, v2 = k_ref[...], v_ref[...]
            lane = lax.broadcasted_iota(jnp.int32, (rs, LANES), 1)
            rk = pl.ds(pl.multiple_of(j * bk, bk), bk)
            add = lambda tot, x: x if tot is None else tot + x
            dv_t = dk_t = dkr_t = None
            dck_t = [None] * g
            for r in range(bq // rs):
                rows = pl.ds(r * rs, rs)
                rq = pl.ds(pl.multiple_of(i * bq + r * rs, rs), rs)
                q2, do2, lse2 = q_ref[rows, :], do_ref[rows, :], lse_ref[rows, :]
                dd = do2.astype(F32) * o_ref[rows, :].astype(F32)
                dq_t = dqr_t = dcq_t = None
                for hh in range(g):
                    qm = _sel(heads[hh], q2)
                    sc = _dot(qm, k2, NT)
                    if rope:
                        qrm = _sel(ropes[hh], qr_ref[rows, :])
                        sc = sc + _dot(qrm, kr_ref[...], NT)
                    if bias:
                        sc = sc - ck_ref[0, hh:hh + 1, :]
                    if masked:
                        sc = jnp.where(_mask(mode, i * bq + r * rs, j * bk, rs, bk), sc, MASK_VALUE)
                    p = jnp.exp(sc - jnp.sum(jnp.where(lane == hh, lse2, 0.0), axis=1, keepdims=True))
                    dom = _sel(heads[hh], do2)
                    dp = _dot(dom, v2, NT)
                    delta = jnp.sum(_sel(heads[hh], dd), axis=1, keepdims=True)
                    ds = p * (dp - delta)
                    dsb = ds.astype(BF16)
                    dv_t = add(dv_t, _dot(p.astype(BF16), dom, TN))
                    dk_t = add(dk_t, _dot(dsb, qm, TN))
                    dq_t = add(dq_t, _dot(dsb, _sel(heads[hh], k2)))
                    if rope:
                        dqr_t = add(dqr_t, _dot(dsb, _sel(ropes[hh], kr_ref[...])))
                        dkr_t = add(dkr_t, _dot(dsb, qrm, TN))
                    if bias:
                        dck_t[hh] = add(dck_t[hh], jnp.sum(ds, axis=0, keepdims=True))
                        dcq_t = add(dcq_t, jnp.where(lane == hh, jnp.sum(ds, axis=1, keepdims=True), 0.0))
                dq_ref[rq, :] += dq_t if dq_scale == 1.0 else dq_scale * dq_t
                if rope:
                    dqr_ref[rq, :] += dq_scale * dqr_t
                if bias:
                    dcq_ref[rq, :] += dcq_t
            dv_s[...] += dv_t
            dk_s[...] += dk_t
            if rope:
                dkr_ref[rk, :] += dkr_t
            if bias:
                for hh in range(g):
                    dck_s[hh:hh + 1, :] -= dck_t[hh]

        if tri:
            pl.when(st.plain)(functools.partial(compute, False))
            pl.when(st.masked)(functools.partial(compute, True))
        else:
            compute(False)

        @pl.when(st.last)
        def _():
            dk_ref[...] = dk_s[...]
            dv_ref[...] = dv_s[...]
            if bias:
                dck_ref[0] = dck_s[...]

        if comm is not None:
            pl.when((b == nb - 1) & (p1 == n1 - 1) & (p2 == n2 - 1))(lambda: comm.finish(*comm_refs))

    qrow = lambda col: pl.BlockSpec((bq, LANES), lambda b, p1, p2: (step_at(p1, p2).qi, col(b)))
    krow = lambda col: pl.BlockSpec((bk, LANES), lambda b, p1, p2: (step_at(p1, p2).kj, col(b)))
    in_specs = [qrow(lambda b: qc + b), krow(lambda b: kc + b), krow(lambda b: vc + b), qrow(lambda b: b),
                qrow(lambda b: b), qrow(lambda b: b)]
    args = [q, k, v, o, do, lse]
    whole = lambda rows: pl.BlockSpec((rows, LANES), lambda b, j, i: (0, b))
    out_specs = [whole(s), krow(lambda b: b), krow(lambda b: b)]
    out_shape = [jax.ShapeDtypeStruct((s, LANES * nb), F32), jax.ShapeDtypeStruct((t, LANES * nb), F32),
                 jax.ShapeDtypeStruct((t, LANES * nb), F32)]
    scratch = [pltpu.VMEM((bk, LANES), F32), pltpu.VMEM((bk, LANES), F32)]
    if bias:
        ckj = pl.BlockSpec((1, 8, bk), lambda b, p1, p2: (b, 0, step_at(p1, p2).kj))
        in_specs.append(ckj)
        args.append(ck)
        out_specs += [ckj, whole(s)]
        out_shape += [jax.ShapeDtypeStruct((nb, 8, t), F32), jax.ShapeDtypeStruct((s, LANES * nb), F32)]
    if rope:
        in_specs += [qrow(lambda b: qrc), krow(lambda b: 0)]
        args += [qr, kr]
        out_specs += [pl.BlockSpec((s, LANES), lambda b, j, i: (0, 0)), pl.BlockSpec((t, LANES), lambda b, j, i: (0, 0))]
        out_shape += [jax.ShapeDtypeStruct((s, LANES), F32), jax.ShapeDtypeStruct((t, LANES), F32)]
    if bias:
        scratch.append(pltpu.VMEM((8, bk), F32))
    if comm is not None:
        in_specs += [ANY] * len(comm.ins)
        args += comm.ins
        out_specs += [ANY] * len(comm.out_shapes)
        out_shape += comm.out_shapes
        scratch += _sems(comm.n_sems, comm.n_sems)
    res = pl.pallas_call(body, name=name, grid=(nb, n1, n2), in_specs=in_specs, out_specs=out_specs,
                         out_shape=out_shape, scratch_shapes=scratch,
                         compiler_params=_params('arbitrary', 'arbitrary', 'arbitrary'))(*args)
    return res if comm is None else (*res[:n_out], res[n_out:])


def _gla_chunk(la_c, k_c):
    r = lax.broadcasted_iota(jnp.int32, (CHUNK, CHUNK), 0)
    c = lax.broadcasted_iota(jnp.int32, (CHUNK, CHUNK), 1)
    tri = jnp.where(c <= r, 1.0, 0.0).astype(BF16)
    cum = _tri_dot(tri, la_c)
    end = jnp.sum(la_c, axis=0, keepdims=True)
    dec = jnp.exp(end - cum)
    return dec, k_c * dec, jnp.exp(end)


GLA_PAIRS = GLA_HEADS // 2


def _gla_fwd(z, la, *, qc, kc, vc, name, blk=512):
    s = z.shape[0]
    bs = min(blk, s)
    ncb = bs // CHUNK
    nblk = s // bs

    def body(q_ref, k_ref, va_ref, vb_ref, la_ref, o_ref, st_ref, st):
        @pl.when(pl.program_id(1) == 0)
        def _():
            st[...] = jnp.zeros_like(st)

        heads, _ = _lane_masks(2, 0, False)
        v_refs = (va_ref, vb_ref)
        for c in range(ncb):
            sl = pl.ds(c * CHUNK, CHUNK)
            _, kf, a = _gla_chunk(la_ref[sl, :], k_ref[sl, :])
            qs = q_ref[sl, :] * (GLA_DK ** -0.5)
            for hh in range(2):
                ut = _dot(v_refs[hh][sl, :].astype(BF16), _sel(heads[hh], kf).astype(BF16), TN)
                new = a * st[hh] + ut
                st[hh] = new
                st_ref[0, c, hh] = new
                o_ref[sl, hh * GLA_DV:(hh + 1) * GLA_DV] = _dot(_sel(heads[hh], qs).astype(BF16), new.astype(BF16), NT)

    col = lambda c0, m=1: pl.BlockSpec((bs, LANES), lambda b, i: (i, c0 + m * b))
    return pl.pallas_call(
        body, name=name, grid=(GLA_PAIRS, nblk),
        in_specs=[col(qc), col(kc), col(vc, 2), col(vc + 1, 2), col(0)],
        out_specs=[pl.BlockSpec((bs, 2 * GLA_DV), lambda b, i: (i, b)),
                   pl.BlockSpec((1, ncb, 2, GLA_DV, LANES), lambda b, i: (b, i, 0, 0, 0))],
        out_shape=[jax.ShapeDtypeStruct((s, GLA_HEADS * GLA_DV), F32),
                   jax.ShapeDtypeStruct((GLA_PAIRS, s // CHUNK, 2, GLA_DV, LANES), F32)],
        scratch_shapes=[pltpu.VMEM((2, GLA_DV, LANES), F32)],
        compiler_params=_params('arbitrary', 'arbitrary'))(z, z, z, z, la)


def _gla_bwd(z, la, st_all, st_prev, do, *, qc, kc, vc, name, blk=512):
    s = z.shape[0]
    bs = min(blk, s)
    ncb = bs // CHUNK
    nblk = s // bs

    def body(q_ref, k_ref, va_ref, vb_ref, la_ref, st_ref, sp_ref, do_ref, dq_ref, dk_ref, dv_ref, dla_ref, ga):
        @pl.when(pl.program_id(1) == 0)
        def _():
            ga[...] = jnp.zeros_like(ga)

        r = lax.broadcasted_iota(jnp.int32, (CHUNK, CHUNK), 0)
        cc = lax.broadcasted_iota(jnp.int32, (CHUNK, CHUNK), 1)
        tri_rev = jnp.where(cc >= r, 1.0, 0.0).astype(BF16)
        heads, _ = _lane_masks(2, 0, False)
        v_refs = (va_ref, vb_ref)
        for c in reversed(range(ncb)):
            sl = pl.ds(c * CHUNK, CHUNK)
            dec, kf, a = _gla_chunk(la_ref[sl, :], k_ref[sl, :])
            qs = q_ref[sl, :] * (GLA_DK ** -0.5)
            dq2 = jnp.zeros((CHUNK, LANES), F32)
            dkd = jnp.zeros((CHUNK, LANES), F32)
            da = jnp.zeros((1, LANES), F32)
            for hh in range(2):
                hv = slice(hh * GLA_DV, (hh + 1) * GLA_DV)
                dob = do_ref[sl, hv].astype(BF16)
                g = _dot(dob, _sel(heads[hh], qs).astype(BF16), TN) + ga[hh]
                gb = g.astype(BF16)
                dq2 = dq2 + _dot(dob, st_ref[0, c, hh].astype(BF16))
                dv_ref[sl, hv] = _dot(_sel(heads[hh], kf).astype(BF16), gb, NT)
                dkd = dkd + _dot(v_refs[hh][sl, :].astype(BF16), gb)
                da = da + jnp.sum(g * sp_ref[0, c, hh], axis=0, keepdims=True)
                ga[hh] = a * g
            dq_ref[sl, :] = (GLA_DK ** -0.5) * dq2
            dk_ref[sl, :] = dkd * dec
            e = dkd * kf
            dend = jnp.sum(e, axis=0, keepdims=True) + da * a
            dla_ref[sl, :] = dend - _tri_dot(tri_rev, e)

    rev = lambda i: nblk - 1 - i
    col = lambda c0, m=1: pl.BlockSpec((bs, LANES), lambda b, i: (rev(i), c0 + m * b))
    wide = pl.BlockSpec((bs, 2 * GLA_DV), lambda b, i: (rev(i), b))
    stspec = pl.BlockSpec((1, ncb, 2, GLA_DV, LANES), lambda b, i: (b, rev(i), 0, 0, 0))
    return pl.pallas_call(
        body, name=name, grid=(GLA_PAIRS, nblk),
        in_specs=[col(qc), col(kc), col(vc, 2), col(vc + 1, 2), col(0), stspec, stspec, wide],
        out_specs=[col(0), col(0), wide, col(0)],
        out_shape=[jax.ShapeDtypeStruct((s, GLA_HEADS * GLA_DK), F32), jax.ShapeDtypeStruct((s, GLA_HEADS * GLA_DK), F32),
                   jax.ShapeDtypeStruct((s, GLA_HEADS * GLA_DV), F32), jax.ShapeDtypeStruct((s, GLA_HEADS * GLA_DK), F32)],
        scratch_shapes=[pltpu.VMEM((2, GLA_DV, LANES), F32)],
        compiler_params=_params('arbitrary', 'arbitrary'))(z, z, z, z, la, st_all, st_prev, do)


def _place():
    return lax.axis_index('x'), lax.axis_index('y'), lax.axis_index('c')


ANY = pl.BlockSpec(memory_space=pl.ANY)


def _all_gather8(blk, *, name):
    m, n = blk.shape

    def body(x_ref, out_ref, send_sems, recv_sems, local_sem):
        x, y, c = _place()
        me, sibling = (x, y, c), (x, y, 1 - c)
        chips = [(1 - x, y), (x, 1 - y), (1 - x, 1 - y)]

        def slot(px, py, pc):
            return out_ref.at[4 * px + 2 * py + pc]

        def copy(q, block, to, src=None):
            return pltpu.make_async_remote_copy(
                src_ref=slot(*block) if src is None else src, dst_ref=slot(*block), send_sem=send_sems.at[q],
                recv_sem=recv_sems.at[q], device_id=to, device_id_type=MESH)

        mine = pltpu.make_async_copy(x_ref, slot(*me), local_sem)
        mine.start()
        first = [copy(0, me, sibling, src=x_ref)]
        first += [copy(1 + q, me, (*chip, c), src=x_ref) for q, chip in enumerate(chips)]
        for cp in first:
            cp.start()
        passed = [copy(4 + q, (*chip, c), sibling) for q, chip in enumerate(chips)]
        for q, chip in enumerate(chips):
            copy(1 + q, (*chip, c), me).wait_recv()
            passed[q].start()
        copy(0, sibling, me).wait_recv()
        for q, chip in enumerate(chips):
            copy(4 + q, (*chip, 1 - c), me).wait_recv()
        for cp in first + passed:
            cp.wait_send()
        mine.wait()

    return pl.pallas_call(
        body, name=name, in_specs=[ANY], out_specs=ANY, out_shape=jax.ShapeDtypeStruct((N_DEV, m, n), blk.dtype),
        scratch_shapes=[pltpu.SemaphoreType.DMA((7,)), pltpu.SemaphoreType.DMA((7,)), pltpu.SemaphoreType.DMA(())],
    )(blk)


def _sems(*counts):
    return [pltpu.SemaphoreType.DMA((n,)) for n in counts]


class Comm(typing.NamedTuple):
    ins: list
    out_shapes: list
    n_sems: int
    start: typing.Callable
    finish: typing.Callable


def _remote(src, dst, send_sems, recv_sems, idx, to):
    return lambda: pltpu.make_async_remote_copy(src_ref=src, dst_ref=dst, send_sem=send_sems.at[idx],
                                                recv_sem=recv_sems.at[idx], device_id=to, device_id_type=MESH)


def _comm_from(copies, ins, out_shapes, n_sems):
    def start(*refs):
        for cp in copies(*refs)[0]:
            cp().start()

    def finish(*refs):
        sent, received = copies(*refs)
        for cp in received:
            cp().wait_recv()
        for cp in sent:
            cp().wait_send()

    return Comm(list(ins), list(out_shapes), n_sems, start, finish)


def _run_comm(comm, *, name, alias=False):
    n_in, n_out = len(comm.ins), len(comm.out_shapes)

    def body(*refs):
        ins, outs, sems = refs[:n_in], refs[n_in:n_in + n_out], refs[n_in + n_out:]
        comm.start(ins, outs, *sems)
        comm.finish(ins, outs, *sems)

    return pl.pallas_call(body, name=name, in_specs=[ANY] * n_in, out_specs=[ANY] * n_out, out_shape=comm.out_shapes,
                          input_output_aliases={q: q for q in range(n_in)} if alias else {},
                          scratch_shapes=_sems(comm.n_sems, comm.n_sems))(*comm.ins)


def _half(rows, c):
    h = rows // 2
    return pl.ds(pl.multiple_of(c * h, h), h)


def _gathered(ref, chip, rows, side):
    if not side:
        return ref.at[chip, rows]
    n = ref.shape[1] // N_CHIPS
    return ref.at[rows, pl.ds(pl.multiple_of(chip * n, n), n)]


def _gather_over_ici(ws, side):
    def copies(ins, outs, send_sems, recv_sems):
        x, y, c = _place()
        me_chip = 2 * x + y
        sent, received = [], []
        for q, w in enumerate(ws):
            half, every = _half(w.shape[0], c), pl.ds(0, w.shape[0])
            for k, (px, py) in enumerate([(1 - x, y), (x, 1 - y), (1 - x, 1 - y)]):
                sent.append(_remote(ins[q].at[half], _gathered(outs[q], me_chip, half, side[q]), send_sems, recv_sems,
                                    4 * q + k, (px, py, c)))
                slot = _gathered(outs[q], 2 * px + py, half, side[q])
                received.append(_remote(slot, slot, send_sems, recv_sems, 4 * q + k, (px, py, c)))
            whole = _remote(ins[q], _gathered(outs[q], me_chip, every, side[q]), send_sems, recv_sems, 4 * q + 3,
                            (x, y, 1 - c))
            sent.append(whole)
            received.append(whole)
        return sent, received

    shapes = [jax.ShapeDtypeStruct((w.shape[0], N_CHIPS * w.shape[1]) if sd else (N_CHIPS,) + w.shape, w.dtype)
              for w, sd in zip(ws, side)]
    return _comm_from(copies, ws, shapes, 4 * len(ws))


def _gather_over_d2d(parts, side):
    def copies(ins, outs, send_sems, recv_sems):
        x, y, c = _place()
        sent, received = [], []
        for q, w in enumerate(parts):
            rows = w.shape[0] if side[q] else w.shape[1]
            for k, (px, py) in enumerate([(1 - x, y), (x, 1 - y), (1 - x, 1 - y)]):
                mine = _gathered(outs[q], 2 * px + py, _half(rows, c), side[q])
                theirs = _gathered(outs[q], 2 * px + py, _half(rows, 1 - c), side[q])
                sent.append(_remote(mine, mine, send_sems, recv_sems, 3 * q + k, (x, y, 1 - c)))
                received.append(_remote(theirs, theirs, send_sems, recv_sems, 3 * q + k, (x, y, 1 - c)))
        return sent, received

    return _comm_from(copies, parts, [jax.ShapeDtypeStruct(w.shape, w.dtype) for w in parts], 3 * len(parts))


def _to_sibling(gs, *, name):
    n = len(gs)

    def body(*refs):
        ins, outs = refs[:n], refs[n:2 * n]
        send_sems, recv_sems = refs[2 * n:]
        x, y, c = _place()
        cps = [pltpu.make_async_remote_copy(
            src_ref=ins[q], dst_ref=outs[q], send_sem=send_sems.at[q], recv_sem=recv_sems.at[q],
            device_id=(x, y, 1 - c), device_id_type=MESH) for q in range(n)]
        for cp in cps:
            cp.start()
        for cp in cps:
            cp.wait()

    return pl.pallas_call(body, name=name, in_specs=[ANY] * n, out_specs=[ANY] * n,
                          out_shape=[jax.ShapeDtypeStruct(g.shape, g.dtype) for g in gs],
                          scratch_shapes=_sems(n, n))(*gs)


def _chip_exchange(ps):
    def copies(ins, outs, send_sems, recv_sems):
        x, y, c = _place()
        cps = [_remote(ins[q].at[2 * px + py], outs[q].at[k], send_sems, recv_sems, 3 * q + k, (px, py, c))
               for q in range(len(ps)) for k, (px, py) in enumerate([(1 - x, y), (x, 1 - y), (1 - x, 1 - y)])]
        return cps, cps

    return _comm_from(copies, ps, [jax.ShapeDtypeStruct((3,) + p.shape[1:], p.dtype) for p in ps], 3 * len(ps))


def _sum_chips(own, r, *, name, ts=256):
    k, n = own.shape
    ts = min(ts, k)

    def body(own_ref, r_ref, o_ref):
        f = lambda q: r_ref[q].astype(F32)
        o_ref[...] = ((own_ref[...].astype(F32) + f(0)) + f(1)) + f(2)

    return pl.pallas_call(
        body, name=name, grid=(k // ts,),
        in_specs=[pl.BlockSpec((ts, n), lambda i: (i, 0)), pl.BlockSpec((3, ts, n), lambda i: (0, i, 0))],
        out_specs=pl.BlockSpec((ts, n), lambda i: (i, 0)), out_shape=jax.ShapeDtypeStruct((k, n), F32),
        compiler_params=_params('arbitrary'))(own, r)


WIN_SHARD = N_IN // N_CHIPS
WIN_PAD = -(-WIN_SHARD // LANES) * LANES
GATE_WIRE_ROWS = 32


def _full_layer(sh, axis):
    _, k, n = sh.shape
    if axis == 2:
        return sh.transpose(1, 0, 2).reshape(k, N_CHIPS * n)
    return sh.reshape(N_CHIPS * k, n)


def _win_cols(wp, o, n):
    parts = []
    while n > 0:
        j, r = divmod(o, WIN_SHARD)
        take = min(n, WIN_SHARD - r)
        parts.append(wp[:, j * WIN_PAD + r:j * WIN_PAD + r + take])
        o, n = o + take, n - take
    return parts[0] if len(parts) == 1 else jnp.concatenate(parts, axis=1)


def _split_full(full, axis):
    if full.ndim == 3:
        return full
    k, n = full.shape
    if axis == 2:
        return jnp.stack([full[:, j * (n // N_CHIPS):(j + 1) * (n // N_CHIPS)] for j in range(N_CHIPS)])
    return full.reshape(N_CHIPS, k // N_CHIPS, n)


def _padc(a, w):
    return jnp.pad(a, ((0, 0), (0, w - a.shape[1])))


def _swap16(a):
    return jnp.concatenate([a[..., 16:32], a[..., 0:16]], axis=-1)


B_GR, B_GQ, B_GK, B_GV, B_MQ, B_MKR, B_MKRS, B_FF, B_GLOW, B_MKV, B_END = (
    0, 512, 768, 1024, 1536, 1792, 1920, 2048, 2176, 2304, 2432)
B_W = 2560
O_FQ, O_FF, O_GQ, O_GLOW, O_GR, O_MQ, O_MKV, O_MKR, O_ZG = 0, 768, 772, 1796, 1812, 2324, 2580, 2708, 2740


def _repack_layer_weights(w):
    wi = functools.partial(_win_cols, w['w_in'])
    out = dict(w)
    out['in_a'] = jnp.concatenate([wi(O_FQ, 256) * FOX_SCALE, wi(O_FQ + 256, 512)], axis=1)
    kr = wi(O_MKR, 32)
    out['in_b'] = jnp.concatenate([
        wi(O_GR, 512), wi(O_GQ, 1024), wi(O_MQ, 256), jnp.tile(kr, (1, MLA_HEADS)), jnp.tile(_swap16(kr), (1, MLA_HEADS)),
        _padc(wi(O_FF, 4), 128), _padc(wi(O_GLOW, 16), 128), wi(O_MKV, 128),
        jnp.zeros((D_MODEL, B_W - B_END), kr.dtype)], axis=1)
    out['in_c'] = wi(O_ZG, 3072)
    uq = w['w_mla_uq'].reshape(MLA_Q_RANK, MLA_HEADS, MLA_NOPE + MLA_ROPE)
    rope = uq[:, :, MLA_NOPE:]
    out['uq'] = jnp.concatenate([uq[:, :, :MLA_NOPE].reshape(MLA_Q_RANK, -1), rope.reshape(MLA_Q_RANK, -1),
                                 _swap16(rope).reshape(MLA_Q_RANK, -1)], axis=1)
    ukv = w['w_mla_ukv'].reshape(MLA_KV_RANK, MLA_HEADS, MLA_NOPE + MLA_VD)
    out['ukv'] = jnp.concatenate([ukv[:, :, :MLA_NOPE].reshape(MLA_KV_RANK, -1),
                                  ukv[:, :, MLA_NOPE:].reshape(MLA_KV_RANK, -1)], axis=1)
    out['gate'] = jnp.pad(w['w_gla_gate'], ((0, 128 - GLA_RANK), (0, 0)))
    return out


def _unpack_layer_grads(g):
    a, b, c = g['in_a'], g['in_b'], g['in_c']
    fold = lambda o: sum(b[:, o + MLA_ROPE * q:o + MLA_ROPE * (q + 1)] for q in range(MLA_HEADS))
    kr = fold(B_MKR) + _swap16(fold(B_MKRS))
    pieces = [(a[:, :256] * FOX_SCALE, 0, 256), (a, 256, 512), (b, B_FF, 4), (b, B_GQ, 1024), (b, B_GLOW, 16),
              (b, B_GR, 512), (b, B_MQ, 256), (b, B_MKV, 128), (kr, 0, 32), (c, 0, 3072)]
    shards = []
    for j in range(N_CHIPS):
        lo, hi, cut, at = j * WIN_SHARD, (j + 1) * WIN_SHARD, [], 0
        for arr, first, width in pieces:
            l, h = max(lo, at), min(hi, at + width)
            if l < h:
                cut.append(arr[:, first + l - at:first + h - at])
            at += width
        shards.append(jnp.concatenate(cut, axis=1))
    w_in = jnp.stack(shards)
    uq = g['uq']
    nope = uq[:, :256].reshape(MLA_Q_RANK, MLA_HEADS, MLA_NOPE)
    rope = (uq[:, 256:384].reshape(MLA_Q_RANK, MLA_HEADS, MLA_ROPE)
            + _swap16(uq[:, 384:512].reshape(MLA_Q_RANK, MLA_HEADS, MLA_ROPE)))
    w_uq = jnp.concatenate([nope, rope], axis=2).reshape(MLA_Q_RANK, -1)
    ukv = g['ukv']
    w_ukv = jnp.concatenate([ukv[:, :256].reshape(MLA_KV_RANK, MLA_HEADS, MLA_NOPE),
                             ukv[:, 256:].reshape(MLA_KV_RANK, MLA_HEADS, MLA_VD)], axis=2).reshape(MLA_KV_RANK, -1)
    out = {'w_in': w_in, 'w_mla_uq': w_uq, 'w_mla_ukv': w_ukv, 'w_gla_gate': g['gate'][:GLA_RANK]}
    for nm in ('w_up_fox', 'w_up_gla', 'w_up_mla', 'w_out', 'w_xq', 'w_xkv', 'w_xo', 'w_mlp1', 'w_mlp2'):
        out[nm] = g[nm]
    return out


def _rope_tables(s):
    half = MLA_ROPE // 2
    inv = ROPE_BASE ** (-jnp.arange(half, dtype=F32) / half)
    ang = jnp.arange(s).astype(F32)[:, None] * inv[None, :]
    cos, sin = jnp.cos(ang), jnp.sin(ang)
    c1 = jnp.concatenate([cos, cos], axis=1)
    s1 = jnp.concatenate([-sin, sin], axis=1)
    return jnp.tile(c1, (1, MLA_HEADS)), jnp.tile(s1, (1, MLA_HEADS))


def _rms_bwd(x, dh, g):
    r = lax.rsqrt(jnp.mean(x * x, axis=-1, keepdims=True) + EPS)
    xh = x * r
    gd = dh * g
    return r * (gd - xh * jnp.mean(gd * xh, axis=-1, keepdims=True)), dh * xh


def _norm_bwd_epilogue(dh, x, dres, g):
    dx, dg = _rms_bwd(x, dh, g)
    return dres + dx, dg


def _norm_bwd_call(x, dh, g, dres, name):
    w = x.width if isinstance(x, Cols) else x.shape[1]

    def with_res(xv, dv, rv, gv):
        dx, dg = _rms_bwd(xv, dv.astype(F32), gv)
        return rv + dx, dg

    def plain(xv, dv, gv):
        return _rms_bwd(xv, dv.astype(F32), gv)

    if dres is None:
        return _rowwise(plain, [x, dh], [g], [(w, F32)], [w], name=name)
    return _rowwise(with_res, [x, dh, dres], [g], [(w, F32)], [w], name=name)


def _gla_out_fwd(oraw, gr, g_out):
    outs = []
    for hh in range(GLA_HEADS):
        sl = slice(hh * GLA_DV, (hh + 1) * GLA_DV)
        oh = oraw[:, sl]
        n = oh * lax.rsqrt(jnp.mean(oh * oh, axis=-1, keepdims=True) + EPS) * g_out
        r = gr[:, sl]
        outs.append(n * (r * _sig(r)))
    return (jnp.concatenate(outs, axis=1),)


def _gla_out_bwd(oraw, gr, dout, g_out):
    d_o, d_r, dg = [], [], 0.0
    for hh in range(GLA_HEADS):
        sl = slice(hh * GLA_DV, (hh + 1) * GLA_DV)
        oh, r, do = oraw[:, sl], gr[:, sl], dout[:, sl].astype(F32)
        rs = lax.rsqrt(jnp.mean(oh * oh, axis=-1, keepdims=True) + EPS)
        sg = _sig(r)
        dn = do * (r * sg)
        d_r.append(do * (oh * rs * g_out) * (sg + r * sg * (1.0 - sg)))
        dx, dgh = _rms_bwd(oh, dn, g_out)
        d_o.append(dx)
        dg = dg + dgh
    return jnp.concatenate(d_o, axis=1), jnp.concatenate(d_r, axis=1), dg


def _adam(w, g, m, v):
    m = ADAM_B1 * m + (1.0 - ADAM_B1) * g
    v = ADAM_B2 * v + (1.0 - ADAM_B2) * (g * g)
    m_hat = m / (1.0 - ADAM_B1 ** ADAM_STEP)
    v_hat = v / (1.0 - ADAM_B2 ** ADAM_STEP)
    return -ADAM_LR * (m_hat / (jnp.sqrt(v_hat) + ADAM_EPS) + ADAM_WD * w), m, v


def _layer_fwd(x, mem, w, p, tabs, tag, carry_fox=None, after_fox=None, carry_mla=None):
    c4, s4 = tabs
    sv = {'x0': x}
    nm = lambda t: f'{t}_{tag}'
    za, h = _mm(x, w['in_a'], mode='nn', out_dtype=BF16, norm_g=p['g_mix'], emit_norm=True, name=nm('in_a'))
    zb = _mm(h, w['in_b'], mode='nn', out_dtype=F32, name=nm('in_b'))
    zc = _mm(h, w['in_c'], mode='nn', out_dtype=F32, name=nm('in_c'))
    sv.update(h=h, zc=zc)
    ff = Cols(zb, 128, B_FF // 128)
    (lf,) = _rowwise(lambda f, b: (_logsig(f + b),), [ff], [p['b_fox']], [(128, F32)], name=nm('fox_lf'))
    cum = _cumsum_rows(lf, reverse=False, name=nm('fox_cum'))
    ckf = jnp.pad(cum[:, :FOX_HEADS].T.reshape(2, 2, x.shape[0]), ((0, 0), (0, 6), (0, 0)))
    fox = dict(qc=0, kc=2, vc=4, nb=2, g=2, mode='causal', ck=ckf)
    o_fox, lse_fox, *carried = _mattn_fwd(za, za, za, name=nm('fox_attn'), comm=carry_fox, **fox)
    if after_fox is not None:
        w = {**w, **after_fox(carried[0])}
    sv.update(ff=ff, za=za, fox=fox, o_fox=o_fox, lse_fox=lse_fox)
    glow = Cols(zb, 128, B_GLOW // 128)
    gr = Cols(zb, 512, B_GR // 512)

    def gate_fn(gl, wg, bg):
        return (_logsig(_dot(gl.astype(BF16), wg) + bg) / GLA_TAU,)

    (la,) = _rowwise(gate_fn, [glow], [w['gate'], p['b_gla']], [(256, F32)], name=nm('gla_gate'))
    gla = dict(qc=B_GQ // LANES, kc=B_GK // LANES, vc=B_GV // LANES)
    oraw, states = _gla_fwd(zb, la, name=nm('gla'), **gla)
    (o_gla,) = _rowwise(_gla_out_fwd, [oraw, gr], [p['g_gla_out']], [(512, BF16)], name=nm('gla_out'))
    sv.update(glow=glow, gr=gr, zb=zb, la=la, gla=gla, states=states, oraw=oraw, o_gla=o_gla)
    mq = Cols(zb, 256, B_MQ // 256)
    mkv = Cols(zb, 128, B_MKV // 128)
    mkr2 = Cols(zb, 256, B_MKR // 256)
    qp, cqn = _mm(mq, w['uq'], mode='nn', out_dtype=F32, norm_g=p['g_mla_q'], emit_norm=True, name=nm('mla_uq'))
    kvp, ckvn = _mm(mkv, w['ukv'], mode='nn', out_dtype=BF16, norm_g=p['g_mla_kv'], emit_norm=True,
                    name=nm('mla_ukv'))

    def rope_fn(qv, kr, c4v, s4v):
        q_rope = qv[:, 256:384] * c4v + qv[:, 384:512] * s4v
        q_scaled = jnp.concatenate([qv[:, 0:256], q_rope], axis=1) * MLA_SCALE
        return q_scaled, kr[:, 0:128] * c4v + kr[:, 128:256] * s4v

    qall, kr4 = _rowwise(rope_fn, [qp, mkr2, c4, s4], [], [(384, BF16), (128, BF16)], name=nm('rope'))
    mla = dict(qc=0, kc=0, vc=2, nb=2, g=2, dq_scale=MLA_SCALE, mode='chunk', qr=qall, qrc=2, kr=kr4)
    o_mla, lse_mla, *carried = _mattn_fwd(qall, kvp, kvp, name=nm('mla_attn'), comm=carry_mla, **mla)
    if carry_mla is not None:
        sv['carried_mla'] = carried[0]
    sv.update(mq=mq, mkv=mkv, cqn=cqn, ckvn=ckvn, qall=qall, kvp=kvp, mla=mla, o_mla=o_mla, lse_mla=lse_mla)
    of_m, om_m = o_fox, o_mla
    sv.update(of_m=of_m, om_m=om_m)
    b_br = p['b_branch']

    y = _gated_merge([of_m, o_gla, om_m], [w['w_up_fox'], w['w_up_gla'], w['w_up_mla']], zc, b_br, name=nm('up_merge'))
    add = lambda acc, res: res + acc
    x1 = _mm(y, w['w_out'], mode='nn', out_dtype=F32, name=nm('out'), epilogue=add, extras=[(x, *_mn())])
    sv.update(y=y, x1=x1)
    qx, hx = _mm(x1, w['w_xq'], mode='nn', out_dtype=BF16, norm_g=p['g_xa'], emit_norm=True, name=nm('xq'),
                 epilogue=lambda acc: acc * XA_SCALE)
    kvx, mn = _mm(mem, w['w_xkv'], mode='nn', out_dtype=BF16, norm_g=p['g_mem'], emit_norm=True, name=nm('xkv'))
    xa = dict(qc=0, kc=0, vc=4, nb=4, g=1, dq_scale=XA_SCALE, mode='full')
    ox_m, lse_x = _mattn_fwd(qx, kvx, kvx, name=nm('xa_attn'), **xa)
    x2 = _mm(ox_m, w['w_xo'], mode='nn', out_dtype=F32, name=nm('xo'), epilogue=add, extras=[(x1, *_mn())])
    sv.update(hx=hx, mn=mn, qx=qx, kvx=kvx, xa=xa, lse_x=lse_x, ox_m=ox_m, x2=x2)
    hpre, hm = _mm(x2, w['w_mlp1'], mode='nn', out_dtype=BF16, norm_g=p['g_mlp'], emit_norm=True, name=nm('mlp1'))
    relu2 = lambda t: jnp.square(jnp.maximum(t.astype(F32), 0.0))
    x3 = _mm(hpre, w['w_mlp2'], mode='nn', out_dtype=F32, name=nm('mlp2'), a_fn=relu2, epilogue=add,
             extras=[(x2, *_mn())])
    sv.update(hpre=hpre, hm=hm, w=w)
    return x3, sv


EARLY = ('w_mlp1', 'w_mlp2', 'w_xo', 'w_xq', 'w_xkv', 'w_out', 'w_up_fox', 'w_up_gla', 'w_up_mla')
LATE = ('w_in', 'w_gla_gate', 'w_mla_uq', 'w_mla_ukv')


def _layer_bwd(dx3, mem, w, p, tabs, sv, tag, carry_mla=None, early=None):
    c4, s4 = tabs
    nm = lambda t: f'{t}_{tag}'
    s = dx3.shape[0]
    gw, gs = {}, {}
    relu2 = lambda t: jnp.square(jnp.maximum(t.astype(F32), 0.0))
    gw['w_mlp2'] = _mm(sv['hpre'], dx3, mode='tn', out_dtype=F32, name=nm('d_mlp2'), a_fn=relu2)
    dact = lambda acc, hp: acc * (2.0 * jnp.maximum(hp.astype(F32), 0.0))
    dhpre = _mm(dx3, w['w_mlp2'], mode='nt', out_dtype=BF16, name=nm('d_act'), epilogue=dact,
                extras=[(sv['hpre'], *_mn())])
    gw['w_mlp1'] = _mm(sv['hm'], dhpre, mode='tn', out_dtype=F32, name=nm('d_mlp1'))
    dx2, gs['g_mlp'] = _mm(dhpre, w['w_mlp1'], mode='nt', out_dtype=F32, name=nm('d_hm'), epilogue=_norm_bwd_epilogue,
                           col_sums=True, full_rows=True,
                           extras=[(sv['x2'], *_mn()), (dx3, *_mn()), (p['g_mlp'], *_nvec())])
    gw['w_xo'] = _mm(sv['ox_m'], dx2, mode='tn', out_dtype=F32, name=nm('d_xo'))
    dox = _mm(dx2, w['w_xo'], mode='nt', out_dtype=BF16, name=nm('d_ox'))
    dqx_m, dkx, dvx = _mattn_bwd(sv['qx'], sv['kvx'], sv['kvx'], sv['ox_m'], dox, sv['lse_x'], name=nm('xa_bwd'),
                                 **sv['xa'])
    dkvx = jnp.concatenate([dkx, dvx], axis=1).astype(BF16)
    gw['w_xq'] = _mm(sv['hx'], dqx_m, mode='tn', out_dtype=F32, name=nm('d_xq'))
    dx1, gs['g_xa'] = _mm(dqx_m, w['w_xq'], mode='nt', out_dtype=F32, name=nm('d_hx'), epilogue=_norm_bwd_epilogue,
                          col_sums=True, full_rows=True,
                          extras=[(sv['x1'], *_mn()), (dx2, *_mn()), (p['g_xa'], *_nvec())])
    gw['w_xkv'] = _mm(sv['mn'], dkvx, mode='tn', out_dtype=F32, name=nm('d_xkv'))
    dmn = _mm(dkvx, w['w_xkv'], mode='nt', out_dtype=F32, name=nm('d_mn'))
    _, gs['g_mem'] = _norm_bwd_call(mem, dmn, p['g_mem'], None, nm('d_norm_mem'))
    gw['w_out'] = _mm(sv['y'], dx1, mode='tn', out_dtype=F32, name=nm('d_out'))
    dy = _mm(dx1, w['w_out'], mode='nt', out_dtype=BF16, name=nm('d_y'))
    zc, b_br = sv['zc'], p['b_branch']

    branches = (('w_up_fox', sv['of_m'], BF16), ('w_up_gla', sv['o_gla'], F32), ('w_up_mla', sv['om_m'], BF16))
    du, do_br, dzc, gs['b_branch'] = _gated_merge_bwd(dy, zc, b_br, [o for _, o, _ in branches],
                                                      [w[wn] for wn, _, _ in branches], [dt for _, _, dt in branches],
                                                      name=nm('d_merge'))
    for q, (wn, o_m, _) in enumerate(branches):
        gw[wn] = _mm(o_m, du[q], mode='tn', out_dtype=F32, name=nm(f'd_up{q}'))
    za = sv['za']
    carry_fox = None if early is None else early({nm_: gw[nm_] for nm_ in EARLY})
    dfq, dfk, dfv, dck, dcq, *carried_fox = _mattn_bwd(za, za, za, sv['o_fox'], do_br[0], sv['lse_fox'],
                                                       name=nm('fox_bwd'), comm=carry_fox, **sv['fox'])
    dcum = _padc(dck[:, :2, :].reshape(FOX_HEADS, s).T + dcq.reshape(s, 2, LANES)[:, :, :2].reshape(s, FOX_HEADS), 128)
    dlf = _cumsum_rows(dcum, reverse=True, name=nm('fox_dcum'))

    def dff_fn(dl, f, b):
        d = dl * _sig(-(f + b))
        return d, d

    dff, db_fox = _rowwise(dff_fn, [dlf, sv['ff']], [p['b_fox']], [(128, F32)], [128], name=nm('fox_dff'))
    gs['b_fox'] = db_fox
    dza = jnp.concatenate([dfq, dfk, dfv], axis=1).astype(BF16)
    dqn, dkn, dvv, dq_rope, dk_rope, *carried_mla = _mattn_bwd(sv['qall'], sv['kvp'], sv['kvp'], sv['o_mla'], do_br[2],
                                                               sv['lse_mla'], name=nm('mla_bwd'), comm=carry_mla,
                                                               **sv['mla'])

    def drope_fn(dn, dq, dk, c4v, s4v):
        return jnp.concatenate([dn, dq * c4v, dq * s4v], axis=1), jnp.concatenate([dk * c4v, dk * s4v], axis=1)

    dqp, dmkr2 = _rowwise(drope_fn, [dqn, dq_rope, dk_rope, c4, s4], [], [(512, BF16), (256, BF16)], name=nm('d_rope'))
    dkvp = jnp.concatenate([dkn, dvv], axis=1).astype(BF16)
    gw['uq'] = _mm(sv['cqn'], dqp, mode='tn', out_dtype=F32, name=nm('d_uq'))
    dcqn = _mm(dqp, w['uq'], mode='nt', out_dtype=F32, name=nm('d_cqn'))
    gw['ukv'] = _mm(sv['ckvn'], dkvp, mode='tn', out_dtype=F32, name=nm('d_ukv'))
    dckvn = _mm(dkvp, w['ukv'], mode='nt', out_dtype=F32, name=nm('d_ckvn'))
    dmq, gs['g_mla_q'] = _norm_bwd_call(sv['mq'], dcqn, p['g_mla_q'], None, nm('d_norm_q'))
    dmkv, gs['g_mla_kv'] = _norm_bwd_call(sv['mkv'], dckvn, p['g_mla_kv'], None, nm('d_norm_kv'))
    doraw, dgr, gs['g_gla_out'] = _rowwise(_gla_out_bwd, [sv['oraw'], sv['gr'], do_br[1]], [p['g_gla_out']],
                                           [(512, F32), (512, BF16)], [128], name=nm('d_gla_out'))
    st = sv['states']
    st_prev = jnp.concatenate([jnp.zeros_like(st[:, :1]), st[:, :-1]], axis=1)
    dgq, dgk, dgv, dla = _gla_bwd(sv['zb'], sv['la'], st, st_prev, doraw, name=nm('gla_bwd'), **sv['gla'])

    def dgate_fn(dl, gl, wg, bg):
        pre = _dot(gl.astype(BF16), wg) + bg
        dpre = dl * (1.0 / GLA_TAU) * _sig(-pre)
        return dpre, _dot(dpre.astype(BF16), wg, NT), dpre

    dpre, dglow, gs['b_gla'] = _rowwise(dgate_fn, [dla, sv['glow']], [w['gate'], p['b_gla']],
                                        [(256, BF16), (128, BF16)], [256], name=nm('d_gla_gate'))
    gw['gate'] = _mm(sv['glow'], dpre, mode='tn', out_dtype=F32, name=nm('d_wgate'))
    bf = lambda t: t.astype(BF16)
    dzb = jnp.concatenate([dgr, bf(dgq), bf(dgk), bf(dgv), bf(dmq), dmkr2, bf(dff), dglow, bf(dmkv),
                           jnp.zeros((s, B_W - B_END), BF16)], axis=1)
    h = sv['h']
    gw['in_a'] = _mm(h, dza, mode='tn', out_dtype=F32, name=nm('d_in_a'))
    gw['in_b'] = _mm(h, dzb, mode='tn', out_dtype=F32, name=nm('d_in_b'))
    gw['in_c'] = _mm(h, dzc, mode='tn', out_dtype=F32, name=nm('d_in_c'))
    add = lambda acc, prev: prev + acc
    dh = _mm(dza, w['in_a'], mode='nt', out_dtype=F32, name=nm('d_h_a'))
    dh = _mm(dzb, w['in_b'], mode='nt', out_dtype=F32, name=nm('d_h_b'), epilogue=add, extras=[(dh, *_mn())])
    dx0, gs['g_mix'] = _mm(dzc, w['in_c'], mode='nt', out_dtype=F32, name=nm('d_h_c'), col_sums=True, full_rows=True,
                           epilogue=lambda acc, prev, xv, rv, gv: _norm_bwd_epilogue(prev + acc, xv, rv, gv),
                           extras=[(dh, *_mn()), (sv['x0'], *_mn()), (dx1, *_mn()), (p['g_mix'], *_nvec())])
    return dx0, gw, gs, (carried_mla or [None])[0], (carried_fox or [None])[0]


def _loss_head(x, target, g_final):
    d = x.shape[1]

    def fn(xv, tv, gv):
        r = lax.rsqrt(jnp.mean(xv * xv, axis=-1, keepdims=True) + EPS)
        xh = xv * r
        e = xh * gv - tv
        dy = e * (1.0 / d)
        gd = dy * gv
        dx = r * (gd - xh * jnp.mean(gd * xh, axis=-1, keepdims=True))
        row_loss = 0.5 * jnp.mean(e * e, axis=-1, keepdims=True)
        return dx, dy * xh, jnp.broadcast_to(row_loss, (xv.shape[0], LANES))

    return _rowwise(fn, [x, target], [g_final], [(d, F32)], [d, LANES], name='loss_head')


def _step(args):
    shapes = {nm: args[nm].shape for nm in ORDER}
    x, mem, target = args['x'][0], args['mem'][0], args['loss_target'][0]
    s = x.shape[0]

    def wire(nm, l):
        w = args[nm][l].astype(BF16)
        if nm == 'w_in':
            w = jnp.pad(w, ((0, 0), (0, WIN_PAD - WIN_SHARD)))
        if nm == 'w_gla_gate':
            w = jnp.pad(w, ((0, GATE_WIRE_ROWS - GLA_RANK), (0, 0)))
        return w

    axis_of = dict(BIG)
    names = tuple(nm for nm, _ in BIG)
    wires = lambda l, nms: [wire(nm, l) for nm in nms]
    width = lambda nm: WIN_PAD if nm == 'w_in' else args[nm].shape[2]
    side_by_side = lambda nms: [axis_of[nm] == 2 and width(nm) % LANES == 0 for nm in nms]
    over_ici = lambda l, nms: _gather_over_ici(wires(l, nms), side_by_side(nms))

    def whole(parts, nms, tag):
        side = side_by_side(nms)
        parts = _run_comm(_gather_over_d2d(parts, side), name=f'gather_d2d_{tag}', alias=True)
        full = {nm: p if sd else _full_layer(p, axis_of[nm]) for nm, p, sd in zip(nms, parts, side)}
        if 'w_gla_gate' in full:
            full['w_gla_gate'] = full['w_gla_gate'][:GLA_RANK]
        return full

    tabs = _rope_tables(s)
    layers_p = []
    for l in range(DEPTH):
        layers_p.append({
            'g_mix': args['g_mix'][l][None], 'b_fox': _padc(args['b_fox_forget'][l][None], 128),
            'b_gla': args['b_gla_gate'][l][None], 'g_gla_out': args['g_gla_out'][l][None],
            'g_mla_q': args['g_mla_q'][l][None], 'g_mla_kv': args['g_mla_kv'][l][None],
            'b_branch': args['b_branch_gate'][l][None], 'g_xa': args['g_xa'][l][None],
            'g_mem': args['g_mem'][l][None], 'g_mlp': args['g_mlp'][l][None]})

    first = _run_comm(over_ici(0, LATE), name='gather_ici_first_l0')
    w_now = _repack_layer_weights(whole(first, LATE, 'first_l0'))
    saved = []
    xl = x
    for l in range(DEPTH):
        carry_fox = over_ici(0, EARLY) if l == 0 else None
        after_fox = (lambda parts: whole(parts, EARLY, 'rest_l0')) if l == 0 else None
        carry_mla = over_ici(l + 1, names) if l + 1 < DEPTH else None
        xl, sv = _layer_fwd(xl, mem, w_now, layers_p[l], tabs, f'l{l}', carry_fox=carry_fox, after_fox=after_fox,
                            carry_mla=carry_mla)
        saved.append(sv)
        if carry_mla is not None:
            w_now = _repack_layer_weights(whole(sv.pop('carried_mla'), names, f'l{l + 1}'))
    dx, dg_final, loss_lanes = _loss_head(xl, target, args['g_final'][None])
    cidx = lax.axis_index('c')
    chip = 2 * lax.axis_index('x') + lax.axis_index('y')

    def pair_sums(gw, nms, tag):
        mine, theirs = [], []
        for nm in nms:
            shards = _split_full(gw[nm], axis_of[nm]).astype(BF16)
            h = shards.shape[1] // 2
            mine.append(lax.dynamic_slice_in_dim(shards, cidx * h, h, axis=1))
            theirs.append(lax.dynamic_slice_in_dim(shards, (1 - cidx) * h, h, axis=1))
        got = _to_sibling(theirs, name=f'grads_swap_{tag}')
        pairs = []
        for nm, a, b in zip(nms, mine, got):
            _, h, n = a.shape
            (p,) = _rowwise(lambda u, v: (u.astype(F32) + v.astype(F32),),
                            [a.reshape(N_CHIPS * h, n), b.reshape(N_CHIPS * h, n)], [], [(n, BF16)],
                            name=f'pair_sum_{nm}_{tag}')
            pairs.append(p.reshape(N_CHIPS, h, n))
        return pairs

    def finish(pairs, from_chips, nms, tag):
        own = [lax.dynamic_index_in_dim(p, chip, axis=0, keepdims=False) for p in pairs]
        mine = [_sum_chips(o, r, name=f'chip_sum_{nm}_{tag}') for nm, o, r in zip(nms, own, from_chips)]
        theirs = _to_sibling(mine, name=f'grads_join_{tag}')
        return {nm: jnp.where(cidx == 0, jnp.concatenate([a, b]), jnp.concatenate([b, a]))
                for nm, a, b in zip(nms, mine, theirs)}

    gs_layers, done = [None] * DEPTH, [{} for _ in range(DEPTH)]
    above = None
    for l in reversed(range(DEPTH)):
        lowest, early_pairs = l == 0, []

        def early(gw_early, l=l, early_pairs=early_pairs):
            early_pairs.extend(pair_sums(gw_early, EARLY, f'early_l{l}'))
            return _chip_exchange(early_pairs)

        carry_mla = None if above is None else _chip_exchange(above[1])
        dx, gw, gs_layers[l], got_mla, got_fox = _layer_bwd(
            dx, mem, saved[l]['w'], layers_p[l], tabs, saved[l], f'l{l}', carry_mla=carry_mla,
            early=early if lowest else None)
        if above is not None:
            done[above[0]].update(finish(above[1], got_mla, names, f'l{above[0]}'))
        grads = _unpack_layer_grads(gw)
        if lowest:
            done[l].update(finish(early_pairs, got_fox, EARLY, f'early_l{l}'))
            late_pairs = pair_sums(grads, LATE, f'late_l{l}')
            from_late = _run_comm(_chip_exchange(late_pairs), name=f'grads_exchange_late_l{l}')
            done[l].update(finish(late_pairs, from_late, LATE, f'late_l{l}'))
        else:
            above = (l, pair_sums(grads, names, f'l{l}'))
    grad_x = dx[None]
    gshard = {nm: jnp.stack([done[l][nm] for l in range(DEPTH)]) for nm in names}

    small_g = []
    for nm, key in (('g_mix', 'g_mix'), ('b_fox_forget', 'b_fox'), ('b_gla_gate', 'b_gla'),
                    ('g_gla_out', 'g_gla_out'), ('g_mla_q', 'g_mla_q'), ('g_mla_kv', 'g_mla_kv'),
                    ('b_branch_gate', 'b_branch'), ('g_xa', 'g_xa'), ('g_mem', 'g_mem'), ('g_mlp', 'g_mlp')):
        width = shapes[nm][1]
        small_g.append(jnp.concatenate([gs_layers[l][key][0, :width] for l in range(DEPTH)]))
    small_g.append(dg_final[0])
    small_g.append(loss_lanes[0, :1])
    flat = jnp.concatenate(small_g)
    n_small = flat.shape[0]
    srows = -(-n_small // (8 * LANES)) * 8
    pad = lambda v: jnp.pad(v, (0, srows * LANES - v.shape[0])).reshape(srows, LANES)
    all_small = _all_gather8(pad(flat), name='gather_small')
    sw, sm, svv = (pad(jnp.concatenate([args[pre + nm].reshape(-1) for nm in SMALL] + [jnp.zeros((1,), F32)]))
                   for pre in ('', 'm_', 'v_'))

    def small_body(g_ref, w_ref, m_ref, v_ref, go_ref, d_ref, mo_ref, vo_ref):
        g = g_ref[0]
        for q in range(1, N_DEV):
            g = g + g_ref[q]
        go_ref[...] = g
        d_ref[...], mo_ref[...], vo_ref[...] = _adam(w_ref[...], g, m_ref[...], v_ref[...])

    sg, sd, snm, snv = pl.pallas_call(
        small_body, name='small_sum_adam', out_shape=[jax.ShapeDtypeStruct((srows, LANES), F32)] * 4,
        compiler_params=pltpu.CompilerParams(vmem_limit_bytes=VMEM_LIMIT))(all_small, sw, sm, svv)

    def unsmall(buf):
        v, out, off = buf.reshape(-1), {}, 0
        for nm in SMALL:
            nel = math.prod(shapes[nm])
            out[nm] = v[off:off + nel].reshape(shapes[nm])
            off += nel
        return out, v[off]

    res = {}
    (res['grad'], loss), (res['delta'], _), (res['m'], _), (res['v'], _) = (unsmall(t) for t in (sg, sd, snm, snv))

    for nm, _ in BIG:
        shp = args[nm].shape
        view = lambda t: t.reshape(shp[0] * shp[1], shp[2])
        d, m2, v2 = _rowwise(_adam, [view(args[nm]), view(gshard[nm]), view(args['m_' + nm]), view(args['v_' + nm])],
                             [], [(shp[2], F32)] * 3, name=f'adam_{nm}')
        res['grad'][nm], res['delta'][nm], res['m'][nm], res['v'][nm] = (
            gshard[nm], d.reshape(shp), m2.reshape(shp), v2.reshape(shp))

    return (loss, grad_x, *[res['grad'][nm] for nm in ORDER], *[res['delta'][nm] for nm in ORDER],
            *[res['m'][nm] for nm in ORDER], *[res['v'][nm] for nm in ORDER])


def kernel(x, mem, g_mix, w_in, b_fox_forget, w_gla_gate, b_gla_gate, g_gla_out, g_mla_q, w_mla_uq, g_mla_kv, w_mla_ukv, b_branch_gate, w_up_fox, w_up_gla, w_up_mla, w_out, g_xa, g_mem, w_xq, w_xkv, w_xo, g_mlp, w_mlp1, w_mlp2, g_final, loss_target, m_g_mix, m_w_in, m_b_fox_forget, m_w_gla_gate, m_b_gla_gate, m_g_gla_out, m_g_mla_q, m_w_mla_uq, m_g_mla_kv, m_w_mla_ukv, m_b_branch_gate, m_w_up_fox, m_w_up_gla, m_w_up_mla, m_w_out, m_g_xa, m_g_mem, m_w_xq, m_w_xkv, m_w_xo, m_g_mlp, m_w_mlp1, m_w_mlp2, m_g_final, v_g_mix, v_w_in, v_b_fox_forget, v_w_gla_gate, v_b_gla_gate, v_g_gla_out, v_g_mla_q, v_w_mla_uq, v_g_mla_kv, v_w_mla_ukv, v_b_branch_gate, v_w_up_fox, v_w_up_gla, v_w_up_mla, v_w_out, v_g_xa, v_g_mem, v_w_xq, v_w_xkv, v_w_xo, v_g_mlp, v_w_mlp1, v_w_mlp2, v_g_final):
    return _step(dict(locals()))
```

```python
import functools
import math
import typing

import jax
import jax.numpy as jnp
from jax import lax
from jax.experimental import pallas as pl
from jax.experimental.pallas import tpu as pltpu

F32 = jnp.float32
BF16 = jnp.bfloat16
MESH = pl.DeviceIdType.MESH

D_MODEL = 1024
DEPTH = 2
CHUNK = 64
EPS = 1e-6
FOX_HEADS, FOX_HD = 4, 64
GLA_HEADS, GLA_DK, GLA_DV, GLA_RANK, GLA_TAU = 4, 64, 128, 16, 16.0
MLA_HEADS, MLA_Q_RANK, MLA_KV_RANK, MLA_NOPE, MLA_ROPE, MLA_VD = 4, 256, 128, 64, 32, 64
ROPE_BASE = 10000.0
XA_HEADS, XA_HD = 4, 128
D_FF = 4 * D_MODEL
IN_SIZES = (256, 256, 256, 4, 256, 256, 512, 16, 512, 256, 128, 32, 3072)
N_IN = sum(IN_SIZES)

ADAM_LR, ADAM_B1, ADAM_B2, ADAM_EPS, ADAM_WD, ADAM_STEP = 0.001, 0.9, 0.999, 1e-08, 0.01, 10

N_CHIPS = 4
N_DEV = 8
LANES = 128
VMEM_LIMIT = 48 * 1024 * 1024
MASK_VALUE = -1e30

BIG = (('w_in', 2), ('w_gla_gate', 2), ('w_mla_uq', 2), ('w_mla_ukv', 2), ('w_up_fox', 2), ('w_up_gla', 2),
       ('w_up_mla', 2), ('w_out', 1), ('w_xq', 1), ('w_xkv', 1), ('w_xo', 2), ('w_mlp1', 2), ('w_mlp2', 1))
SMALL = ('g_mix', 'b_fox_forget', 'b_gla_gate', 'g_gla_out', 'g_mla_q', 'g_mla_kv', 'b_branch_gate',
         'g_xa', 'g_mem', 'g_mlp', 'g_final')
ORDER = ('g_mix', 'w_in', 'b_fox_forget', 'w_gla_gate', 'b_gla_gate', 'g_gla_out', 'g_mla_q', 'w_mla_uq',
         'g_mla_kv', 'w_mla_ukv', 'b_branch_gate', 'w_up_fox', 'w_up_gla', 'w_up_mla', 'w_out', 'g_xa', 'g_mem',
         'w_xq', 'w_xkv', 'w_xo', 'g_mlp', 'w_mlp1', 'w_mlp2', 'g_final')


def _params(*sem, extra_vmem=0):
    return pltpu.CompilerParams(dimension_semantics=sem, vmem_limit_bytes=VMEM_LIMIT + extra_vmem)


def _sig(x):
    return 1.0 / (1.0 + jnp.exp(-x))


def _logsig(x):
    return jnp.minimum(x, 0.0) - jnp.log(1.0 + jnp.exp(-jnp.abs(x)))


NN = (((1,), (0,)), ((), ()))
NT = (((1,), (1,)), ((), ()))
TN = (((0,), (0,)), ((), ()))


def _dot(a, b, dims=NN):
    return lax.dot_general(a, b, dims, preferred_element_type=F32)


class Cols(typing.NamedTuple):
    arr: jax.Array
    width: int
    blk: int


def _tri_dot(tri, x):
    hi = x.astype(BF16)
    r1 = x - hi.astype(F32)
    mid = r1.astype(BF16)
    lo = (r1 - mid.astype(F32)).astype(BF16)
    return _dot(tri, hi) + _dot(tri, mid) + _dot(tri, lo)


MM_TILES = ((1024, 1024), (1024, 512), (512, 1024), (512, 512), (256, 1024), (512, 256), (256, 512), (256, 256),
            (128, 1024), (128, 128))
MM_VMEM_BUDGET = 38 * 1024 * 1024
MM_VMEM_EXTRA = 8 * 1024 * 1024


def _mm_tiles(m, n, k, a_bytes, b_bytes, out_bytes, ex_bytes, has_norm, emit_norm, has_fn, full_rows):
    for tm, tn in MM_TILES:
        tm, tn = min(tm, m), min(tn, n)
        if m % tm or n % tn or (full_rows and tn != n):
            continue
        blocks = tm * k * a_bytes + k * tn * b_bytes + tm * tn * (out_bytes + ex_bytes) + (tm * k * 2 if emit_norm else 0)
        temps = tm * tn * 4 + (tm * k * 2 if has_norm else 0) + (tm * k * 6 if has_fn or has_norm else 0)
        if 2 * blocks + temps <= MM_VMEM_BUDGET + (MM_VMEM_EXTRA if has_fn else 0):
            return tm, tn
    raise ValueError((m, n, k))


def _mm(a, b, *, mode, out_dtype, name, norm_g=None, emit_norm=False, a_fn=None, extras=(), epilogue=None,
        col_sums=False, full_rows=False):
    a_blk = 0
    if isinstance(a, Cols):
        a, width, a_blk = a
        a_shape = (a.shape[0], width)
    else:
        a_shape = a.shape
    if mode == 'tn':
        k, m = a_shape
    else:
        m, k = a_shape
    n = b.shape[0] if mode == 'nt' else b.shape[1]
    assert (b.shape[1] if mode == 'nt' else b.shape[0]) == k, (name, a.shape, b.shape)
    has_norm = norm_g is not None
    ex_bytes = sum(arr.dtype.itemsize for arr, kind, _ in extras if kind == 'mn')
    tm, tn = _mm_tiles(m, n, k, a.dtype.itemsize, b.dtype.itemsize, jnp.dtype(out_dtype).itemsize, ex_bytes, has_norm,
                       emit_norm, a_fn is not None, full_rows)
    assert all(col % tn == 0 for _, _, col in extras), (name, tn)
    assert a_blk == 0 or (mode == 'nn') or (mode == 'tn' and tm == m)
    assert not (col_sums and (has_norm or emit_norm))
    ij = (lambda f: lambda g0, g1: f(g1, g0)) if col_sums else (lambda f: f)
    spec = lambda blk, f: pl.BlockSpec(blk, ij(f))
    if mode == 'tn':
        a_spec = spec((k, tm), lambda i, j: (0, i + a_blk))
    else:
        a_spec = spec((tm, k), lambda i, j: (i, a_blk))
    b_spec = spec((tn, k), lambda i, j: (j, 0)) if mode == 'nt' else spec((k, tn), lambda i, j: (0, j))
    dims = {'nn': NN, 'nt': NT, 'tn': TN}[mode]
    assert not (has_norm and mode != 'nn')
    n_ex = len(extras)

    def body(*refs):
        a_ref, b_ref = refs[0], refs[1]
        pos = 2
        g_ref = None
        if has_norm:
            g_ref = refs[pos]
            pos += 1
        ex_refs = refs[pos:pos + n_ex]
        pos += n_ex
        o_ref = refs[pos]
        pos += 1
        h_ref = None
        if emit_norm:
            h_ref = refs[pos]
            pos += 1
        if has_norm:
            an_ref = refs[pos]

            @pl.when(pl.program_id(1) == 0)
            def _():
                xf = a_ref[...].astype(F32)
                y = xf * lax.rsqrt(jnp.mean(xf * xf, axis=-1, keepdims=True) + EPS) * g_ref[...]
                an_ref[...] = y.astype(BF16)
                if emit_norm:
                    h_ref[...] = y.astype(BF16)

            av = an_ref[...]
        else:
            av = a_ref[...]
            if a_fn is not None:
                av = a_fn(av)
            av = av.astype(BF16)
        acc = _dot(av, b_ref[...].astype(BF16), dims)
        if epilogue is not None:
            acc = epilogue(acc, *[r[...] for r in ex_refs])
        acc, to_sum = acc if isinstance(acc, tuple) else (acc, acc)
        o_ref[...] = acc.astype(out_dtype)
        if col_sums:
            sum_ref = refs[pos]

            @pl.when(pl.program_id(1) == 0)
            def _():
                sum_ref[...] = jnp.zeros_like(sum_ref)

            sum_ref[...] += jnp.sum(to_sum, axis=0, keepdims=True)

    in_specs = [a_spec, b_spec]
    args = [a, b]
    if has_norm:
        in_specs.append(pl.BlockSpec((1, k), lambda i, j: (0, 0)))
        args.append(norm_g)
    for arr, kind, col in extras:
        if kind == 'mn':
            in_specs.append(spec((tm, tn), lambda i, j, o=col // tn: (i, j + o)))
        else:
            in_specs.append(spec((1, tn), lambda i, j, o=col // tn: (0, j + o)))
        args.append(arr)
    out_shape = [jax.ShapeDtypeStruct((m, n), out_dtype)]
    out_specs = [spec((tm, tn), lambda i, j: (i, j))]
    if emit_norm:
        out_shape.append(jax.ShapeDtypeStruct((m, k), BF16))
        out_specs.append(pl.BlockSpec((tm, k), lambda i, j: (i, 0)))
    if col_sums:
        out_shape.append(jax.ShapeDtypeStruct((1, n), F32))
        out_specs.append(spec((1, tn), lambda i, j: (0, j)))
    scratch = [pltpu.VMEM((tm, k), BF16)] if has_norm else []
    grid = (n // tn, m // tm) if col_sums else (m // tm, n // tn)
    res = pl.pallas_call(
        body, name=name, grid=grid, in_specs=in_specs, out_specs=out_specs, out_shape=out_shape,
        scratch_shapes=scratch,
        compiler_params=_params('arbitrary', 'arbitrary', extra_vmem=MM_VMEM_EXTRA if a_fn is not None else 0))(*args)
    return res if emit_norm or col_sums else res[0]


def _gated_merge(outs, ups, zg, bias, *, name, tm=1024, tn=512):
    s, n, nq = zg.shape[0], ups[0].shape[1], len(outs)
    tm, tn = min(tm, s), min(tn, n)
    per = n // tn

    def body(*refs):
        y = None
        for q in range(nq):
            o_ref, w_ref, z_ref, b_ref = refs[q], refs[nq + q], refs[2 * nq + q], refs[3 * nq + q]
            term = _sig(z_ref[...].astype(F32) + b_ref[...]) * _dot(o_ref[...], w_ref[...])
            y = term if y is None else y + term
        refs[4 * nq][...] = y.astype(BF16)

    in_specs = [pl.BlockSpec((tm, o.shape[1]), lambda i, j: (i, 0)) for o in outs]
    in_specs += [pl.BlockSpec((u.shape[0], tn), lambda i, j: (0, j)) for u in ups]
    in_specs += [pl.BlockSpec((tm, tn), lambda i, j, q=q: (i, j + q * per)) for q in range(nq)]
    in_specs += [pl.BlockSpec((1, tn), lambda i, j, q=q: (0, j + q * per)) for q in range(nq)]
    return pl.pallas_call(body, name=name, grid=(s // tm, per), in_specs=in_specs,
                          out_specs=pl.BlockSpec((tm, tn), lambda i, j: (i, j)),
                          out_shape=jax.ShapeDtypeStruct((s, n), BF16),
                          compiler_params=_params('arbitrary', 'arbitrary'))(*outs, *ups, *[zg] * nq, *[bias] * nq)


def _gated_merge_bwd(dy, zg, bias, outs, ups, do_dtypes, *, name, tm=512):
    s, n = dy.shape
    nq = len(outs)
    tm = min(tm, s)

    def body(*refs):
        dy_ref, zg_ref, b_ref = refs[:3]
        o_refs, w_refs = refs[3:3 + nq], refs[3 + nq:3 + 2 * nq]
        du_refs, do_refs = refs[3 + 2 * nq:3 + 3 * nq], refs[3 + 3 * nq:3 + 4 * nq]
        dz_ref, db_ref = refs[3 + 4 * nq:]

        @pl.when(pl.program_id(0) == 0)
        def _():
            db_ref[...] = jnp.zeros_like(db_ref)

        d = dy_ref[...].astype(F32)
        for q in range(nq):
            cols = slice(q * n, (q + 1) * n)
            g = _sig(zg_ref[:, cols].astype(F32) + b_ref[:, cols])
            du = (d * g).astype(BF16)
            du_refs[q][...] = du
            do_refs[q][...] = _dot(du, w_refs[q][...], NT).astype(do_dtypes[q])
            dz = d * _dot(o_refs[q][...], w_refs[q][...]) * g * (1.0 - g)
            dz_ref[:, cols] = dz.astype(BF16)
            db_ref[:, cols] += jnp.sum(dz, axis=0, keepdims=True)

    row = lambda w: pl.BlockSpec((tm, w), lambda i: (i, 0))
    whole = lambda a: pl.BlockSpec(a.shape, lambda i: (0, 0))
    in_specs = [row(n), row(nq * n), whole(bias)] + [row(o.shape[1]) for o in outs] + [whole(u) for u in ups]
    out_specs = [row(n)] * nq + [row(o.shape[1]) for o in outs] + [row(nq * n), pl.BlockSpec((1, nq * n), lambda i: (0, 0))]
    out_shape = ([jax.ShapeDtypeStruct((s, n), BF16)] * nq
                 + [jax.ShapeDtypeStruct((s, o.shape[1]), dt) for o, dt in zip(outs, do_dtypes)]
                 + [jax.ShapeDtypeStruct((s, nq * n), BF16), jax.ShapeDtypeStruct((1, nq * n), F32)])
    res = pl.pallas_call(body, name=name, grid=(s // tm,), in_specs=in_specs, out_specs=out_specs, out_shape=out_shape,
                         compiler_params=_params('arbitrary'))(dy, zg, bias, *outs, *ups)
    return res[:nq], res[nq:2 * nq], res[2 * nq], res[2 * nq + 1]


def _mn(col_off=0):
    return 'mn', col_off


def _nvec(col_off=0):
    return 'n', col_off


def _rowwise(fn, rows, consts, outs, sums=(), *, name, ts=256):
    views = [x if isinstance(x, Cols) else Cols(x, x.shape[1], 0) for x in rows]
    rows = [v.arr for v in views]
    r = rows[0].shape[0]
    ts = min(ts, r)
    assert r % ts == 0, (name, r, ts)
    nr, nc, no, ns = len(rows), len(consts), len(outs), len(sums)

    def body(*refs):
        vals = fn(*[x[...] for x in refs[:nr + nc]])
        for q in range(no):
            refs[nr + nc + q][...] = vals[q].astype(outs[q][1])
        if ns:
            @pl.when(pl.program_id(0) == 0)
            def _():
                for q in range(ns):
                    refs[nr + nc + no + q][...] = jnp.zeros((1, sums[q]), F32)

            for q in range(ns):
                refs[nr + nc + no + q][...] += jnp.sum(vals[no + q].astype(F32), axis=0, keepdims=True)

    in_specs = [pl.BlockSpec((ts, v.width), lambda i, blk=v.blk: (i, blk)) for v in views]
    in_specs += [pl.BlockSpec(x.shape, lambda i, nd=x.ndim: (0,) * nd) for x in consts]
    out_specs = [pl.BlockSpec((ts, w), lambda i: (i, 0)) for w, _ in outs]
    out_specs += [pl.BlockSpec((1, w), lambda i: (0, 0)) for w in sums]
    out_shape = [jax.ShapeDtypeStruct((r, w), dt) for w, dt in outs]
    out_shape += [jax.ShapeDtypeStruct((1, w), F32) for w in sums]
    return pl.pallas_call(body, name=name, grid=(r // ts,), in_specs=in_specs, out_specs=out_specs,
                          out_shape=out_shape, compiler_params=_params('arbitrary'))(*rows, *consts)


def _cumsum_rows(x, *, reverse, name, bs=256):
    s, w = x.shape
    bs = min(bs, s)
    nb = s // bs

    def body(x_ref, o_ref, carry):
        @pl.when(pl.program_id(0) == 0)
        def _():
            carry[...] = jnp.zeros_like(carry)

        r = lax.broadcasted_iota(jnp.int32, (bs, bs), 0)
        c = lax.broadcasted_iota(jnp.int32, (bs, bs), 1)
        tri = jnp.where((c >= r) if reverse else (c <= r), 1.0, 0.0).astype(BF16)
        xv = x_ref[...]
        o_ref[...] = _tri_dot(tri, xv) + carry[...]
        carry[...] += jnp.sum(xv, axis=0, keepdims=True)

    imap = (lambda i: (nb - 1 - i, 0)) if reverse else (lambda i: (i, 0))
    return pl.pallas_call(body, name=name, grid=(nb,), in_specs=[pl.BlockSpec((bs, w), imap)],
                          out_specs=pl.BlockSpec((bs, w), imap), out_shape=jax.ShapeDtypeStruct((s, w), F32),
                          scratch_shapes=[pltpu.VMEM((1, w), F32)], compiler_params=_params('arbitrary'))(x)


def _mask(mode, q0, k0, bq, bk):
    qpos = q0 + lax.broadcasted_iota(jnp.int32, (bq, bk), 0)
    kpos = k0 + lax.broadcasted_iota(jnp.int32, (bq, bk), 1)
    if mode == 'causal':
        return kpos <= qpos
    return kpos < (jnp.right_shift(qpos, int(math.log2(CHUNK))) + 1) * CHUNK


ROPE_SHIFT = int(math.log2(MLA_ROPE))
FOX_SCALE, MLA_SCALE, XA_SCALE = FOX_HD ** -0.5, (MLA_NOPE + MLA_ROPE) ** -0.5, XA_HD ** -0.5
ATTN_ROW_SLAB = 512
ATTN_BWD_ROW_SLAB = 256


def _lane_masks(g, b, rope):
    lane = lax.broadcasted_iota(jnp.int32, (1, LANES), 1)
    heads = [None if g == 1 else (lane >= hh * (LANES // g)) & (lane < (hh + 1) * (LANES // g)) for hh in range(g)]
    ropes = [jnp.right_shift(lane, ROPE_SHIFT) == b * g + hh for hh in range(g)] if rope else [None] * g
    return heads, ropes


def _sel(mask, x):
    return x if mask is None else jnp.where(mask, x, jnp.zeros_like(x))


class Step(typing.NamedTuple):
    qi: typing.Any
    kj: typing.Any
    first: typing.Any
    last: typing.Any
    plain: typing.Any
    masked: typing.Any


def _fwd_steps(tri, nq, nk):
    if not tri:
        return (nq, nk), lambda i, j: Step(i, j, j == 0, j == nk - 1, True, False)
    if nq % 2:
        return (nq, nk), lambda i, j: Step(i, jnp.minimum(i, j), j == 0, j == nk - 1, j < i, j == i)

    def at(i, t):
        low = t <= i
        diag = (t == i) | (t == nq)
        return Step(jnp.where(low, i, nq - 1 - i), jnp.where(low, t, t - (i + 1)), (t == 0) | (t == i + 1), diag,
                    jnp.logical_not(diag), diag)

    return (nq // 2, nq + 1), at


def _bwd_steps(tri, nq, nk):
    if not tri:
        return (nk, nq), lambda j, i: Step(i, j, i == 0, i == nq - 1, True, False)
    if nk % 2:
        return (nk, nq), lambda j, i: Step(jnp.maximum(i, j), j, i == 0, i == nq - 1, i > j, i == j)

    def at(j, t):
        n1 = nq - j
        low = t < n1
        diag = (t == 0) | (t == n1)
        return Step(jnp.where(low, j + t, nk - 1 - j + t - n1), jnp.where(low, j, nk - 1 - j), diag,
                    (t == n1 - 1) | (t == nq), jnp.logical_not(diag), diag)

    return (nk // 2, nq + 1), at


def _carried(comm, refs, n_in, n_out):
    ci, co = len(comm.ins), len(comm.out_shapes)
    ins = refs[n_in:n_in + ci]
    outs = refs[n_in + ci + n_out:n_in + ci + n_out + co]
    rest = refs[:n_in] + refs[n_in + ci:n_in + ci + n_out] + refs[n_in + ci + n_out + co:-2]
    return rest, (ins, outs, refs[-2], refs[-1])


def _mattn_fwd(q, k, v, *, qc, kc, vc, nb, g, mode, name, dq_scale=1.0, ck=None, qr=None, qrc=0, kr=None, blk=512,
               comm=None):
    s, t = q.shape[0], k.shape[0]
    bq, bk = min(blk, s), min(blk, t)
    nq, nk = s // bq, t // bk
    tri = mode != 'full'
    bias, rope = ck is not None, qr is not None
    assert not tri or (bq == bk and bq % CHUNK == 0)
    rs = min(ATTN_ROW_SLAB, bq)
    n_in = 3 + bias + 2 * rope
    (n1, n2), step_at = _fwd_steps(tri, nq, nk)

    def body(*refs):
        refs = list(refs)
        b, p1, p2 = pl.program_id(0), pl.program_id(1), pl.program_id(2)
        st = step_at(p1, p2)
        i, j = st.qi, st.kj
        if comm is not None:
            refs, comm_refs = _carried(comm, refs, n_in, 2)
            pl.when((b == 0) & (p1 == 0) & (p2 == 0))(lambda: comm.start(*comm_refs))
        q_ref, k_ref, v_ref = refs[:3]
        pos = 3
        ck_ref = qr_ref = kr_ref = None
        if bias:
            ck_ref = refs[pos]
            pos += 1
        if rope:
            qr_ref, kr_ref = refs[pos:pos + 2]
            pos += 2
        o_ref, lse_ref, m_s, l_s, acc_s = refs[pos:]
        heads, ropes = _lane_masks(g, b, rope)

        @pl.when(st.first)
        def _():
            m_s[...] = jnp.full_like(m_s, MASK_VALUE)
            l_s[...] = jnp.zeros_like(l_s)
            acc_s[...] = jnp.zeros_like(acc_s)

        def compute(masked):
            k2, v2 = k_ref[...], v_ref[...]
            for r in range(bq // rs):
                rows = pl.ds(r * rs, rs)
                q2 = q_ref[rows, :]
                alphas, pvs = [], []
                for hh in range(g):
                    sc = _dot(_sel(heads[hh], q2), k2, NT)
                    if rope:
                        sc = sc + _dot(_sel(ropes[hh], qr_ref[rows, :]), kr_ref[...], NT)
                    if bias:
                        sc = sc - ck_ref[0, hh:hh + 1, :]
                    if masked:
                        sc = jnp.where(_mask(mode, i * bq + r * rs, j * bk, rs, bk), sc, MASK_VALUE)
                    m_prev = m_s[hh, rows]
                    m_new = jnp.maximum(m_prev, jnp.max(sc, axis=1, keepdims=True))
                    alpha = jnp.exp(m_prev - m_new)
                    p = jnp.exp(sc - m_new)
                    l_s[hh, rows] = alpha * l_s[hh, rows] + jnp.sum(p, axis=1, keepdims=True)
                    m_s[hh, rows] = m_new
                    alphas.append(alpha)
                    pvs.append(_dot(p.astype(BF16), _sel(heads[hh], v2)))
                alpha = alphas[0]
                for hh in range(1, g):
                    alpha = jnp.where(heads[hh], alphas[hh], alpha)
                acc_s[rows, :] = acc_s[rows, :] * alpha + sum(pvs[1:], pvs[0])

        if tri:
            pl.when(st.plain)(functools.partial(compute, False))
            pl.when(st.masked)(functools.partial(compute, True))
        else:
            compute(False)

        @pl.when(st.last)
        def _():
            lane = lax.broadcasted_iota(jnp.int32, (bq, LANES), 1)
            l_full, lse = l_s[0], jnp.zeros((bq, LANES), F32)
            for hh in range(g):
                if hh:
                    l_full = jnp.where(heads[hh], l_s[hh], l_full)
                lse = jnp.where(lane == hh, m_s[hh] + jnp.log(l_s[hh]), lse)
            o_ref[...] = (acc_s[...] / l_full).astype(o_ref.dtype)
            lse_ref[...] = lse

        if comm is not None:
            pl.when((b == nb - 1) & (p1 == n1 - 1) & (p2 == n2 - 1))(lambda: comm.finish(*comm_refs))

    qi = lambda p1, p2: step_at(p1, p2).qi
    kj = lambda p1, p2: step_at(p1, p2).kj
    in_specs = [pl.BlockSpec((bq, LANES), lambda b, p1, p2: (qi(p1, p2), qc + b)),
                pl.BlockSpec((bk, LANES), lambda b, p1, p2: (kj(p1, p2), kc + b)),
                pl.BlockSpec((bk, LANES), lambda b, p1, p2: (kj(p1, p2), vc + b))]
    args = [q, k, v]
    if bias:
        in_specs.append(pl.BlockSpec((1, 8, bk), lambda b, p1, p2: (b, 0, kj(p1, p2))))
        args.append(ck)
    if rope:
        in_specs += [pl.BlockSpec((bq, LANES), lambda b, p1, p2: (qi(p1, p2), qrc)),
                     pl.BlockSpec((bk, LANES), lambda b, p1, p2: (kj(p1, p2), 0))]
        args += [qr, kr]
    out = pl.BlockSpec((bq, LANES), lambda b, p1, p2: (qi(p1, p2), b))
    out_specs = [out, out]
    out_shape = [jax.ShapeDtypeStruct((s, LANES * nb), BF16), jax.ShapeDtypeStruct((s, LANES * nb), F32)]
    scratch = [pltpu.VMEM((g, bq, 1), F32), pltpu.VMEM((g, bq, 1), F32), pltpu.VMEM((bq, LANES), F32)]
    if comm is not None:
        in_specs += [ANY] * len(comm.ins)
        args += comm.ins
        out_specs += [ANY] * len(comm.out_shapes)
        out_shape += comm.out_shapes
        scratch += _sems(comm.n_sems, comm.n_sems)
    res = pl.pallas_call(body, name=name, grid=(nb, n1, n2), in_specs=in_specs, out_specs=out_specs, out_shape=out_shape,
                         scratch_shapes=scratch, compiler_params=_params('arbitrary', 'arbitrary', 'arbitrary'))(*args)
    return res if comm is None else (res[0], res[1], res[2:])


def _mattn_bwd(q, k, v, o, do, lse, *, qc, kc, vc, nb, g, mode, name, dq_scale=1.0, ck=None, qr=None, qrc=0, kr=None,
               blk=512, comm=None):
    s, t = q.shape[0], k.shape[0]
    bq, bk = min(blk, s), min(blk, t)
    nq, nk = s // bq, t // bk
    tri = mode != 'full'
    bias, rope = ck is not None, qr is not None
    rs = min(ATTN_BWD_ROW_SLAB, bq)
    n_in, n_out = 6 + bias + 2 * rope, 3 + 2 * bias + 2 * rope
    (n1, n2), step_at = _bwd_steps(tri, nq, nk)

    def body(*refs):
        refs = list(refs)
        if comm is not None:
            refs, comm_refs = _carried(comm, refs, n_in, n_out)
            first = (pl.program_id(0) == 0) & (pl.program_id(1) == 0) & (pl.program_id(2) == 0)
            pl.when(first)(lambda: comm.start(*comm_refs))
        q_ref, k_ref, v_ref, o_ref, do_ref, lse_ref = refs[:6]
        pos = 6
        ck_ref = qr_ref = kr_ref = dck_ref = dcq_ref = dqr_ref = dkr_ref = dck_s = None
        if bias:
            ck_ref = refs[pos]
            pos += 1
        if rope:
            qr_ref, kr_ref = refs[pos:pos + 2]
            pos += 2
        dq_ref, dk_ref, dv_ref = refs[pos:pos + 3]
        pos += 3
        if bias:
            dck_ref, dcq_ref = refs[pos:pos + 2]
            pos += 2
        if rope:
            dqr_ref, dkr_ref = refs[pos:pos + 2]
            pos += 2
        dk_s, dv_s = refs[pos:pos + 2]
        if bias:
            dck_s = refs[pos + 2]
        b, p1, p2 = pl.program_id(0), pl.program_id(1), pl.program_id(2)
        st = step_at(p1, p2)
        i, j = st.qi, st.kj
        heads, ropes = _lane_masks(g, b, rope)

        @pl.when((p1 == 0) & (p2 == 0))
        def _():
            dq_ref[...] = jnp.zeros_like(dq_ref)
            if bias:
                dcq_ref[...] = jnp.zeros_like(dcq_ref)

        if rope:
            @pl.when((b == 0) & (p1 == 0) & (p2 == 0))
            def _():
                dqr_ref[...] = jnp.zeros_like(dqr_ref)
                dkr_ref[...] = jnp.zeros_like(dkr_ref)

        @pl.when(st.first)
        def _():
            dk_s[...] = jnp.zeros_like(dk_s)
            dv_s[...] = jnp.zeros_like(dv_s)
            if bias:
                dck_s[...] = jnp.zeros_like(dck_s)

        def compute(masked):
            k2, v2 = k_ref[...], v_ref[...]
            lane = lax.broadcasted_iota(jnp.int32, (rs, LANES), 1)
            rk = pl.ds(pl.multiple_of(j * bk, bk), bk)
            add = lambda tot, x: x if tot is None else tot + x
            dv_t = dk_t = dkr_t = None
            dck_t = [None] * g
            for r in range(bq // rs):
                rows = pl.ds(r * rs, rs)
                rq = pl.ds(pl.multiple_of(i * bq + r * rs, rs), rs)
                q2, do2, lse2 = q_ref[rows, :], do_ref[rows, :], lse_ref[rows, :]
                dd = do2.astype(F32) * o_ref[rows, :].astype(F32)
                dq_t = dqr_t = dcq_t = None
                for hh in range(g):
                    qm = _sel(heads[hh], q2)
                    sc = _dot(qm, k2, NT)
                    if rope:
                        qrm = _sel(ropes[hh], qr_ref[rows, :])
                        sc = sc + _dot(qrm, kr_ref[...], NT)
                    if bias:
                        sc = sc - ck_ref[0, hh:hh + 1, :]
                    if masked:
                        sc = jnp.where(_mask(mode, i * bq + r * rs, j * bk, rs, bk), sc, MASK_VALUE)
                    p = jnp.exp(sc - jnp.sum(jnp.where(lane == hh, lse2, 0.0), axis=1, keepdims=True))
                    dom = _sel(heads[hh], do2)
                    dp = _dot(dom, v2, NT)
                    delta = jnp.sum(_sel(heads[hh], dd), axis=1, keepdims=True)
                    ds = p * (dp - delta)
                    dsb = ds.astype(BF16)
                    dv_t = add(dv_t, _dot(p.astype(BF16), dom, TN))
                    dk_t = add(dk_t, _dot(dsb, qm, TN))
                    dq_t = add(dq_t, _dot(dsb, _sel(heads[hh], k2)))
                    if rope:
                        dqr_t = add(dqr_t, _dot(dsb, _sel(ropes[hh], kr_ref[...])))
                        dkr_t = add(dkr_t, _dot(dsb, qrm, TN))
                    if bias:
                        dck_t[hh] = add(dck_t[hh], jnp.sum(ds, axis=0, keepdims=True))
                        dcq_t = add(dcq_t, jnp.where(lane == hh, jnp.sum(ds, axis=1, keepdims=True), 0.0))
                dq_ref[rq, :] += dq_t if dq_scale == 1.0 else dq_scale * dq_t
                if rope:
                    dqr_ref[rq, :] += dq_scale * dqr_t
                if bias:
                    dcq_ref[rq, :] += dcq_t
            dv_s[...] += dv_t
            dk_s[...] += dk_t
            if rope:
                dkr_ref[rk, :] += dkr_t
            if bias:
                for hh in range(g):
                    dck_s[hh:hh + 1, :] -= dck_t[hh]

        if tri:
            pl.when(st.plain)(functools.partial(compute, False))
            pl.when(st.masked)(functools.partial(compute, True))
        else:
            compute(False)

        @pl.when(st.last)
        def _():
            dk_ref[...] = dk_s[...]
            dv_ref[...] = dv_s[...]
            if bias:
                dck_ref[0] = dck_s[...]

        if comm is not None:
            pl.when((b == nb - 1) & (p1 == n1 - 1) & (p2 == n2 - 1))(lambda: comm.finish(*comm_refs))

    qrow = lambda col: pl.BlockSpec((bq, LANES), lambda b, p1, p2: (step_at(p1, p2).qi, col(b)))
    krow = lambda col: pl.BlockSpec((bk, LANES), lambda b, p1, p2: (step_at(p1, p2).kj, col(b)))
    in_specs = [qrow(lambda b: qc + b), krow(lambda b: kc + b), krow(lambda b: vc + b), qrow(lambda b: b),
                qrow(lambda b: b), qrow(lambda b: b)]
    args = [q, k, v, o, do, lse]
    whole = lambda rows: pl.BlockSpec((rows, LANES), lambda b, j, i: (0, b))
    out_specs = [whole(s), krow(lambda b: b), krow(lambda b: b)]
    out_shape = [jax.ShapeDtypeStruct((s, LANES * nb), F32), jax.ShapeDtypeStruct((t, LANES * nb), F32),
                 jax.ShapeDtypeStruct((t, LANES * nb), F32)]
    scratch = [pltpu.VMEM((bk, LANES), F32), pltpu.VMEM((bk, LANES), F32)]
    if bias:
        ckj = pl.BlockSpec((1, 8, bk), lambda b, p1, p2: (b, 0, step_at(p1, p2).kj))
        in_specs.append(ckj)
        args.append(ck)
        out_specs += [ckj, whole(s)]
        out_shape += [jax.ShapeDtypeStruct((nb, 8, t), F32), jax.ShapeDtypeStruct((s, LANES * nb), F32)]
    if rope:
        in_specs += [qrow(lambda b: qrc), krow(lambda b: 0)]
        args += [qr, kr]
        out_specs += [pl.BlockSpec((s, LANES), lambda b, j, i: (0, 0)), pl.BlockSpec((t, LANES), lambda b, j, i: (0, 0))]
        out_shape += [jax.ShapeDtypeStruct((s, LANES), F32), jax.ShapeDtypeStruct((t, LANES), F32)]
    if bias:
        scratch.append(pltpu.VMEM((8, bk), F32))
    if comm is not None:
        in_specs += [ANY] * len(comm.ins)
        args += comm.ins
        out_specs += [ANY] * len(comm.out_shapes)
        out_shape += comm.out_shapes
        scratch += _sems(comm.n_sems, comm.n_sems)
    res = pl.pallas_call(body, name=name, grid=(nb, n1, n2), in_specs=in_specs, out_specs=out_specs,
                         out_shape=out_shape, scratch_shapes=scratch,
                         compiler_params=_params('arbitrary', 'arbitrary', 'arbitrary'))(*args)
    return res if comm is None else (*res[:n_out], res[n_out:])


def _gla_chunk(la_c, k_c):
    r = lax.broadcasted_iota(jnp.int32, (CHUNK, CHUNK), 0)
    c = lax.broadcasted_iota(jnp.int32, (CHUNK, CHUNK), 1)
    tri = jnp.where(c <= r, 1.0, 0.0).astype(BF16)
    cum = _tri_dot(tri, la_c)
    end = jnp.sum(la_c, axis=0, keepdims=True)
    dec = jnp.exp(end - cum)
    return dec, k_c * dec, jnp.exp(end)


GLA_PAIRS = GLA_HEADS // 2


def _gla_fwd(z, la, *, qc, kc, vc, name, blk=512):
    s = z.shape[0]
    bs = min(blk, s)
    ncb = bs // CHUNK
    nblk = s // bs

    def body(q_ref, k_ref, va_ref, vb_ref, la_ref, o_ref, st_ref, st):
        @pl.when(pl.program_id(1) == 0)
        def _():
            st[...] = jnp.zeros_like(st)

        heads, _ = _lane_masks(2, 0, False)
        v_refs = (va_ref, vb_ref)
        for c in range(ncb):
            sl = pl.ds(c * CHUNK, CHUNK)
            _, kf, a = _gla_chunk(la_ref[sl, :], k_ref[sl, :])
            qs = q_ref[sl, :] * (GLA_DK ** -0.5)
            for hh in range(2):
                ut = _dot(v_refs[hh][sl, :].astype(BF16), _sel(heads[hh], kf).astype(BF16), TN)
                new = a * st[hh] + ut
                st[hh] = new
                st_ref[0, c, hh] = new
                o_ref[sl, hh * GLA_DV:(hh + 1) * GLA_DV] = _dot(_sel(heads[hh], qs).astype(BF16), new.astype(BF16), NT)

    col = lambda c0, m=1: pl.BlockSpec((bs, LANES), lambda b, i: (i, c0 + m * b))
    return pl.pallas_call(
        body, name=name, grid=(GLA_PAIRS, nblk),
        in_specs=[col(qc), col(kc), col(vc, 2), col(vc + 1, 2), col(0)],
        out_specs=[pl.BlockSpec((bs, 2 * GLA_DV), lambda b, i: (i, b)),
                   pl.BlockSpec((1, ncb, 2, GLA_DV, LANES), lambda b, i: (b, i, 0, 0, 0))],
        out_shape=[jax.ShapeDtypeStruct((s, GLA_HEADS * GLA_DV), F32),
                   jax.ShapeDtypeStruct((GLA_PAIRS, s // CHUNK, 2, GLA_DV, LANES), F32)],
        scratch_shapes=[pltpu.VMEM((2, GLA_DV, LANES), F32)],
        compiler_params=_params('arbitrary', 'arbitrary'))(z, z, z, z, la)


def _gla_bwd(z, la, st_all, st_prev, do, *, qc, kc, vc, name, blk=512):
    s = z.shape[0]
    bs = min(blk, s)
    ncb = bs // CHUNK
    nblk = s // bs

    def body(q_ref, k_ref, va_ref, vb_ref, la_ref, st_ref, sp_ref, do_ref, dq_ref, dk_ref, dv_ref, dla_ref, ga):
        @pl.when(pl.program_id(1) == 0)
        def _():
            ga[...] = jnp.zeros_like(ga)

        r = lax.broadcasted_iota(jnp.int32, (CHUNK, CHUNK), 0)
        cc = lax.broadcasted_iota(jnp.int32, (CHUNK, CHUNK), 1)
        tri_rev = jnp.where(cc >= r, 1.0, 0.0).astype(BF16)
        heads, _ = _lane_masks(2, 0, False)
        v_refs = (va_ref, vb_ref)
        for c in reversed(range(ncb)):
            sl = pl.ds(c * CHUNK, CHUNK)
            dec, kf, a = _gla_chunk(la_ref[sl, :], k_ref[sl, :])
            qs = q_ref[sl, :] * (GLA_DK ** -0.5)
            dq2 = jnp.zeros((CHUNK, LANES), F32)
            dkd = jnp.zeros((CHUNK, LANES), F32)
            da = jnp.zeros((1, LANES), F32)
            for hh in range(2):
                hv = slice(hh * GLA_DV, (hh + 1) * GLA_DV)
                dob = do_ref[sl, hv].astype(BF16)
                g = _dot(dob, _sel(heads[hh], qs).astype(BF16), TN) + ga[hh]
                gb = g.astype(BF16)
                dq2 = dq2 + _dot(dob, st_ref[0, c, hh].astype(BF16))
                dv_ref[sl, hv] = _dot(_sel(heads[hh], kf).astype(BF16), gb, NT)
                dkd = dkd + _dot(v_refs[hh][sl, :].astype(BF16), gb)
                da = da + jnp.sum(g * sp_ref[0, c, hh], axis=0, keepdims=True)
                ga[hh] = a * g
            dq_ref[sl, :] = (GLA_DK ** -0.5) * dq2
            dk_ref[sl, :] = dkd * dec
            e = dkd * kf
            dend = jnp.sum(e, axis=0, keepdims=True) + da * a
            dla_ref[sl, :] = dend - _tri_dot(tri_rev, e)

    rev = lambda i: nblk - 1 - i
    col = lambda c0, m=1: pl.BlockSpec((bs, LANES), lambda b, i: (rev(i), c0 + m * b))
    wide = pl.BlockSpec((bs, 2 * GLA_DV), lambda b, i: (rev(i), b))
    stspec = pl.BlockSpec((1, ncb, 2, GLA_DV, LANES), lambda b, i: (b, rev(i), 0, 0, 0))
    return pl.pallas_call(
        body, name=name, grid=(GLA_PAIRS, nblk),
        in_specs=[col(qc), col(kc), col(vc, 2), col(vc + 1, 2), col(0), stspec, stspec, wide],
        out_specs=[col(0), col(0), wide, col(0)],
        out_shape=[jax.ShapeDtypeStruct((s, GLA_HEADS * GLA_DK), F32), jax.ShapeDtypeStruct((s, GLA_HEADS * GLA_DK), F32),
                   jax.ShapeDtypeStruct((s, GLA_HEADS * GLA_DV), F32), jax.ShapeDtypeStruct((s, GLA_HEADS * GLA_DK), F32)],
        scratch_shapes=[pltpu.VMEM((2, GLA_DV, LANES), F32)],
        compiler_params=_params('arbitrary', 'arbitrary'))(z, z, z, z, la, st_all, st_prev, do)


def _place():
    return lax.axis_index('x'), lax.axis_index('y'), lax.axis_index('c')


ANY = pl.BlockSpec(memory_space=pl.ANY)


def _all_gather8(blk, *, name):
    m, n = blk.shape

    def body(x_ref, out_ref, send_sems, recv_sems, local_sem):
        x, y, c = _place()
        me, sibling = (x, y, c), (x, y, 1 - c)
        chips = [(1 - x, y), (x, 1 - y), (1 - x, 1 - y)]

        def slot(px, py, pc):
            return out_ref.at[4 * px + 2 * py + pc]

        def copy(q, block, to, src=None):
            return pltpu.make_async_remote_copy(
                src_ref=slot(*block) if src is None else src, dst_ref=slot(*block), send_sem=send_sems.at[q],
                recv_sem=recv_sems.at[q], device_id=to, device_id_type=MESH)

        mine = pltpu.make_async_copy(x_ref, slot(*me), local_sem)
        mine.start()
        first = [copy(0, me, sibling, src=x_ref)]
        first += [copy(1 + q, me, (*chip, c), src=x_ref) for q, chip in enumerate(chips)]
        for cp in first:
            cp.start()
        passed = [copy(4 + q, (*chip, c), sibling) for q, chip in enumerate(chips)]
        for q, chip in enumerate(chips):
            copy(1 + q, (*chip, c), me).wait_recv()
            passed[q].start()
        copy(0, sibling, me).wait_recv()
        for q, chip in enumerate(chips):
            copy(4 + q, (*chip, 1 - c), me).wait_recv()
        for cp in first + passed:
            cp.wait_send()
        mine.wait()

    return pl.pallas_call(
        body, name=name, in_specs=[ANY], out_specs=ANY, out_shape=jax.ShapeDtypeStruct((N_DEV, m, n), blk.dtype),
        scratch_shapes=[pltpu.SemaphoreType.DMA((7,)), pltpu.SemaphoreType.DMA((7,)), pltpu.SemaphoreType.DMA(())],
    )(blk)


def _sems(*counts):
    return [pltpu.SemaphoreType.DMA((n,)) for n in counts]


class Comm(typing.NamedTuple):
    ins: list
    out_shapes: list
    n_sems: int
    start: typing.Callable
    finish: typing.Callable


def _remote(src, dst, send_sems, recv_sems, idx, to):
    return lambda: pltpu.make_async_remote_copy(src_ref=src, dst_ref=dst, send_sem=send_sems.at[idx],
                                                recv_sem=recv_sems.at[idx], device_id=to, device_id_type=MESH)


def _comm_from(copies, ins, out_shapes, n_sems):
    def start(*refs):
        for cp in copies(*refs)[0]:
            cp().start()

    def finish(*refs):
        sent, received = copies(*refs)
        for cp in received:
            cp().wait_recv()
        for cp in sent:
            cp().wait_send()

    return Comm(list(ins), list(out_shapes), n_sems, start, finish)


def _run_comm(comm, *, name, alias=False):
    n_in, n_out = len(comm.ins), len(comm.out_shapes)

    def body(*refs):
        ins, outs, sems = refs[:n_in], refs[n_in:n_in + n_out], refs[n_in + n_out:]
        comm.start(ins, outs, *sems)
        comm.finish(ins, outs, *sems)

    return pl.pallas_call(body, name=name, in_specs=[ANY] * n_in, out_specs=[ANY] * n_out, out_shape=comm.out_shapes,
                          input_output_aliases={q: q for q in range(n_in)} if alias else {},
                          scratch_shapes=_sems(comm.n_sems, comm.n_sems))(*comm.ins)


def _half(rows, c):
    h = rows // 2
    return pl.ds(pl.multiple_of(c * h, h), h)


def _gathered(ref, chip, rows, side):
    if not side:
        return ref.at[chip, rows]
    n = ref.shape[1] // N_CHIPS
    return ref.at[rows, pl.ds(pl.multiple_of(chip * n, n), n)]


def _gather_over_ici(ws, side):
    def copies(ins, outs, send_sems, recv_sems):
        x, y, c = _place()
        me_chip = 2 * x + y
        sent, received = [], []
        for q, w in enumerate(ws):
            half, every = _half(w.shape[0], c), pl.ds(0, w.shape[0])
            for k, (px, py) in enumerate([(1 - x, y), (x, 1 - y), (1 - x, 1 - y)]):
                sent.append(_remote(ins[q].at[half], _gathered(outs[q], me_chip, half, side[q]), send_sems, recv_sems,
                                    4 * q + k, (px, py, c)))
                slot = _gathered(outs[q], 2 * px + py, half, side[q])
                received.append(_remote(slot, slot, send_sems, recv_sems, 4 * q + k, (px, py, c)))
            whole = _remote(ins[q], _gathered(outs[q], me_chip, every, side[q]), send_sems, recv_sems, 4 * q + 3,
                            (x, y, 1 - c))
            sent.append(whole)
            received.append(whole)
        return sent, received

    shapes = [jax.ShapeDtypeStruct((w.shape[0], N_CHIPS * w.shape[1]) if sd else (N_CHIPS,) + w.shape, w.dtype)
              for w, sd in zip(ws, side)]
    return _comm_from(copies, ws, shapes, 4 * len(ws))


def _gather_over_d2d(parts, side):
    def copies(ins, outs, send_sems, recv_sems):
        x, y, c = _place()
        sent, received = [], []
        for q, w in enumerate(parts):
            rows = w.shape[0] if side[q] else w.shape[1]
            for k, (px, py) in enumerate([(1 - x, y), (x, 1 - y), (1 - x, 1 - y)]):
                mine = _gathered(outs[q], 2 * px + py, _half(rows, c), side[q])
                theirs = _gathered(outs[q], 2 * px + py, _half(rows, 1 - c), side[q])
                sent.append(_remote(mine, mine, send_sems, recv_sems, 3 * q + k, (x, y, 1 - c)))
                received.append(_remote(theirs, theirs, send_sems, recv_sems, 3 * q + k, (x, y, 1 - c)))
        return sent, received

    return _comm_from(copies, parts, [jax.ShapeDtypeStruct(w.shape, w.dtype) for w in parts], 3 * len(parts))


def _to_sibling(gs, *, name):
    n = len(gs)

    def body(*refs):
        ins, outs = refs[:n], refs[n:2 * n]
        send_sems, recv_sems = refs[2 * n:]
        x, y, c = _place()
        cps = [pltpu.make_async_remote_copy(
            src_ref=ins[q], dst_ref=outs[q], send_sem=send_sems.at[q], recv_sem=recv_sems.at[q],
            device_id=(x, y, 1 - c), device_id_type=MESH) for q in range(n)]
        for cp in cps:
            cp.start()
        for cp in cps:
            cp.wait()

    return pl.pallas_call(body, name=name, in_specs=[ANY] * n, out_specs=[ANY] * n,
                          out_shape=[jax.ShapeDtypeStruct(g.shape, g.dtype) for g in gs],
                          scratch_shapes=_sems(n, n))(*gs)


def _chip_exchange(ps):
    def copies(ins, outs, send_sems, recv_sems):
        x, y, c = _place()
        cps = [_remote(ins[q].at[2 * px + py], outs[q].at[k], send_sems, recv_sems, 3 * q + k, (px, py, c))
               for q in range(len(ps)) for k, (px, py) in enumerate([(1 - x, y), (x, 1 - y), (1 - x, 1 - y)])]
        return cps, cps

    return _comm_from(copies, ps, [jax.ShapeDtypeStruct((3,) + p.shape[1:], p.dtype) for p in ps], 3 * len(ps))


def _sum_chips(own, r, *, name, ts=256):
    k, n = own.shape
    ts = min(ts, k)

    def body(own_ref, r_ref, o_ref):
        f = lambda q: r_ref[q].astype(F32)
        o_ref[...] = ((own_ref[...].astype(F32) + f(0)) + f(1)) + f(2)

    return pl.pallas_call(
        body, name=name, grid=(k // ts,),
        in_specs=[pl.BlockSpec((ts, n), lambda i: (i, 0)), pl.BlockSpec((3, ts, n), lambda i: (0, i, 0))],
        out_specs=pl.BlockSpec((ts, n), lambda i: (i, 0)), out_shape=jax.ShapeDtypeStruct((k, n), F32),
        compiler_params=_params('arbitrary'))(own, r)


WIN_SHARD = N_IN // N_CHIPS
WIN_PAD = -(-WIN_SHARD // LANES) * LANES
GATE_WIRE_ROWS = 32


def _full_layer(sh, axis):
    _, k, n = sh.shape
    if axis == 2:
        return sh.transpose(1, 0, 2).reshape(k, N_CHIPS * n)
    return sh.reshape(N_CHIPS * k, n)


def _win_cols(wp, o, n):
    parts = []
    while n > 0:
        j, r = divmod(o, WIN_SHARD)
        take = min(n, WIN_SHARD - r)
        parts.append(wp[:, j * WIN_PAD + r:j * WIN_PAD + r + take])
        o, n = o + take, n - take
    return parts[0] if len(parts) == 1 else jnp.concatenate(parts, axis=1)


def _split_full(full, axis):
    if full.ndim == 3:
        return full
    k, n = full.shape
    if axis == 2:
        return jnp.stack([full[:, j * (n // N_CHIPS):(j + 1) * (n // N_CHIPS)] for j in range(N_CHIPS)])
    return full.reshape(N_CHIPS, k // N_CHIPS, n)


def _padc(a, w):
    return jnp.pad(a, ((0, 0), (0, w - a.shape[1])))


def _swap16(a):
    return jnp.concatenate([a[..., 16:32], a[..., 0:16]], axis=-1)


B_GR, B_GQ, B_GK, B_GV, B_MQ, B_MKR, B_MKRS, B_FF, B_GLOW, B_MKV, B_END = (
    0, 512, 768, 1024, 1536, 1792, 1920, 2048, 2176, 2304, 2432)
B_W = 2560
O_FQ, O_FF, O_GQ, O_GLOW, O_GR, O_MQ, O_MKV, O_MKR, O_ZG = 0, 768, 772, 1796, 1812, 2324, 2580, 2708, 2740


def _repack_layer_weights(w):
    wi = functools.partial(_win_cols, w['w_in'])
    out = dict(w)
    out['in_a'] = jnp.concatenate([wi(O_FQ, 256) * FOX_SCALE, wi(O_FQ + 256, 512)], axis=1)
    kr = wi(O_MKR, 32)
    out['in_b'] = jnp.concatenate([
        wi(O_GR, 512), wi(O_GQ, 1024), wi(O_MQ, 256), jnp.tile(kr, (1, MLA_HEADS)), jnp.tile(_swap16(kr), (1, MLA_HEADS)),
        _padc(wi(O_FF, 4), 128), _padc(wi(O_GLOW, 16), 128), wi(O_MKV, 128),
        jnp.zeros((D_MODEL, B_W - B_END), kr.dtype)], axis=1)
    out['in_c'] = wi(O_ZG, 3072)
    uq = w['w_mla_uq'].reshape(MLA_Q_RANK, MLA_HEADS, MLA_NOPE + MLA_ROPE)
    rope = uq[:, :, MLA_NOPE:]
    out['uq'] = jnp.concatenate([uq[:, :, :MLA_NOPE].reshape(MLA_Q_RANK, -1), rope.reshape(MLA_Q_RANK, -1),
                                 _swap16(rope).reshape(MLA_Q_RANK, -1)], axis=1)
    ukv = w['w_mla_ukv'].reshape(MLA_KV_RANK, MLA_HEADS, MLA_NOPE + MLA_VD)
    out['ukv'] = jnp.concatenate([ukv[:, :, :MLA_NOPE].reshape(MLA_KV_RANK, -1),
                                  ukv[:, :, MLA_NOPE:].reshape(MLA_KV_RANK, -1)], axis=1)
    out['gate'] = jnp.pad(w['w_gla_gate'], ((0, 128 - GLA_RANK), (0, 0)))
    return out


def _unpack_layer_grads(g):
    a, b, c = g['in_a'], g['in_b'], g['in_c']
    fold = lambda o: sum(b[:, o + MLA_ROPE * q:o + MLA_ROPE * (q + 1)] for q in range(MLA_HEADS))
    kr = fold(B_MKR) + _swap16(fold(B_MKRS))
    pieces = [(a[:, :256] * FOX_SCALE, 0, 256), (a, 256, 512), (b, B_FF, 4), (b, B_GQ, 1024), (b, B_GLOW, 16),
              (b, B_GR, 512), (b, B_MQ, 256), (b, B_MKV, 128), (kr, 0, 32), (c, 0, 3072)]
    shards = []
    for j in range(N_CHIPS):
        lo, hi, cut, at = j * WIN_SHARD, (j + 1) * WIN_SHARD, [], 0
        for arr, first, width in pieces:
            l, h = max(lo, at), min(hi, at + width)
            if l < h:
                cut.append(arr[:, first + l - at:first + h - at])
            at += width
        shards.append(jnp.concatenate(cut, axis=1))
    w_in = jnp.stack(shards)
    uq = g['uq']
    nope = uq[:, :256].reshape(MLA_Q_RANK, MLA_HEADS, MLA_NOPE)
    rope = (uq[:, 256:384].reshape(MLA_Q_RANK, MLA_HEADS, MLA_ROPE)
            + _swap16(uq[:, 384:512].reshape(MLA_Q_RANK, MLA_HEADS, MLA_ROPE)))
    w_uq = jnp.concatenate([nope, rope], axis=2).reshape(MLA_Q_RANK, -1)
    ukv = g['ukv']
    w_ukv = jnp.concatenate([ukv[:, :256].reshape(MLA_KV_RANK, MLA_HEADS, MLA_NOPE),
                             ukv[:, 256:].reshape(MLA_KV_RANK, MLA_HEADS, MLA_VD)], axis=2).reshape(MLA_KV_RANK, -1)
    out = {'w_in': w_in, 'w_mla_uq': w_uq, 'w_mla_ukv': w_ukv, 'w_gla_gate': g['gate'][:GLA_RANK]}
    for nm in ('w_up_fox', 'w_up_gla', 'w_up_mla', 'w_out', 'w_xq', 'w_xkv', 'w_xo', 'w_mlp1', 'w_mlp2'):
        out[nm] = g[nm]
    return out


def _rope_tables(s):
    half = MLA_ROPE // 2
    inv = ROPE_BASE ** (-jnp.arange(half, dtype=F32) / half)
    ang = jnp.arange(s).astype(F32)[:, None] * inv[None, :]
    cos, sin = jnp.cos(ang), jnp.sin(ang)
    c1 = jnp.concatenate([cos, cos], axis=1)
    s1 = jnp.concatenate([-sin, sin], axis=1)
    return jnp.tile(c1, (1, MLA_HEADS)), jnp.tile(s1, (1, MLA_HEADS))


def _rms_bwd(x, dh, g):
    r = lax.rsqrt(jnp.mean(x * x, axis=-1, keepdims=True) + EPS)
    xh = x * r
    gd = dh * g
    return r * (gd - xh * jnp.mean(gd * xh, axis=-1, keepdims=True)), dh * xh


def _norm_bwd_epilogue(dh, x, dres, g):
    dx, dg = _rms_bwd(x, dh, g)
    return dres + dx, dg


def _norm_bwd_call(x, dh, g, dres, name):
    w = x.width if isinstance(x, Cols) else x.shape[1]

    def with_res(xv, dv, rv, gv):
        dx, dg = _rms_bwd(xv, dv.astype(F32), gv)
        return rv + dx, dg

    def plain(xv, dv, gv):
        return _rms_bwd(xv, dv.astype(F32), gv)

    if dres is None:
        return _rowwise(plain, [x, dh], [g], [(w, F32)], [w], name=name)
    return _rowwise(with_res, [x, dh, dres], [g], [(w, F32)], [w], name=name)


def _gla_out_fwd(oraw, gr, g_out):
    outs = []
    for hh in range(GLA_HEADS):
        sl = slice(hh * GLA_DV, (hh + 1) * GLA_DV)
        oh = oraw[:, sl]
        n = oh * lax.rsqrt(jnp.mean(oh * oh, axis=-1, keepdims=True) + EPS) * g_out
        r = gr[:, sl]
        outs.append(n * (r * _sig(r)))
    return (jnp.concatenate(outs, axis=1),)


def _gla_out_bwd(oraw, gr, dout, g_out):
    d_o, d_r, dg = [], [], 0.0
    for hh in range(GLA_HEADS):
        sl = slice(hh * GLA_DV, (hh + 1) * GLA_DV)
        oh, r, do = oraw[:, sl], gr[:, sl], dout[:, sl].astype(F32)
        rs = lax.rsqrt(jnp.mean(oh * oh, axis=-1, keepdims=True) + EPS)
        sg = _sig(r)
        dn = do * (r * sg)
        d_r.append(do * (oh * rs * g_out) * (sg + r * sg * (1.0 - sg)))
        dx, dgh = _rms_bwd(oh, dn, g_out)
        d_o.append(dx)
        dg = dg + dgh
    return jnp.concatenate(d_o, axis=1), jnp.concatenate(d_r, axis=1), dg


def _adam(w, g, m, v):
    m = ADAM_B1 * m + (1.0 - ADAM_B1) * g
    v = ADAM_B2 * v + (1.0 - ADAM_B2) * (g * g)
    m_hat = m / (1.0 - ADAM_B1 ** ADAM_STEP)
    v_hat = v / (1.0 - ADAM_B2 ** ADAM_STEP)
    return -ADAM_LR * (m_hat / (jnp.sqrt(v_hat) + ADAM_EPS) + ADAM_WD * w), m, v


def _layer_fwd(x, mem, w, p, tabs, tag, carry_fox=None, after_fox=None, carry_mla=None):
    c4, s4 = tabs
    sv = {'x0': x}
    nm = lambda t: f'{t}_{tag}'
    za, h = _mm(x, w['in_a'], mode='nn', out_dtype=BF16, norm_g=p['g_mix'], emit_norm=True, name=nm('in_a'))
    zb = _mm(h, w['in_b'], mode='nn', out_dtype=F32, name=nm('in_b'))
    zc = _mm(h, w['in_c'], mode='nn', out_dtype=F32, name=nm('in_c'))
    sv.update(h=h, zc=zc)
    ff = Cols(zb, 128, B_FF // 128)
    (lf,) = _rowwise(lambda f, b: (_logsig(f + b),), [ff], [p['b_fox']], [(128, F32)], name=nm('fox_lf'))
    cum = _cumsum_rows(lf, reverse=False, name=nm('fox_cum'))
    ckf = jnp.pad(cum[:, :FOX_HEADS].T.reshape(2, 2, x.shape[0]), ((0, 0), (0, 6), (0, 0)))
    fox = dict(qc=0, kc=2, vc=4, nb=2, g=2, mode='causal', ck=ckf)
    o_fox, lse_fox, *carried = _mattn_fwd(za, za, za, name=nm('fox_attn'), comm=carry_fox, **fox)
    if after_fox is not None:
        w = {**w, **after_fox(carried[0])}
    sv.update(ff=ff, za=za, fox=fox, o_fox=o_fox, lse_fox=lse_fox)
    glow = Cols(zb, 128, B_GLOW // 128)
    gr = Cols(zb, 512, B_GR // 512)

    def gate_fn(gl, wg, bg):
        return (_logsig(_dot(gl.astype(BF16), wg) + bg) / GLA_TAU,)

    (la,) = _rowwise(gate_fn, [glow], [w['gate'], p['b_gla']], [(256, F32)], name=nm('gla_gate'))
    gla = dict(qc=B_GQ // LANES, kc=B_GK // LANES, vc=B_GV // LANES)
    oraw, states = _gla_fwd(zb, la, name=nm('gla'), **gla)
    (o_gla,) = _rowwise(_gla_out_fwd, [oraw, gr], [p['g_gla_out']], [(512, BF16)], name=nm('gla_out'))
    sv.update(glow=glow, gr=gr, zb=zb, la=la, gla=gla, states=states, oraw=oraw, o_gla=o_gla)
    mq = Cols(zb, 256, B_MQ // 256)
    mkv = Cols(zb, 128, B_MKV // 128)
    mkr2 = Cols(zb, 256, B_MKR // 256)
    qp, cqn = _mm(mq, w['uq'], mode='nn', out_dtype=F32, norm_g=p['g_mla_q'], emit_norm=True, name=nm('mla_uq'))
    kvp, ckvn = _mm(mkv, w['ukv'], mode='nn', out_dtype=BF16, norm_g=p['g_mla_kv'], emit_norm=True,
                    name=nm('mla_ukv'))

    def rope_fn(qv, kr, c4v, s4v):
        q_rope = qv[:, 256:384] * c4v + qv[:, 384:512] * s4v
        q_scaled = jnp.concatenate([qv[:, 0:256], q_rope], axis=1) * MLA_SCALE
        return q_scaled, kr[:, 0:128] * c4v + kr[:, 128:256] * s4v

    qall, kr4 = _rowwise(rope_fn, [qp, mkr2, c4, s4], [], [(384, BF16), (128, BF16)], name=nm('rope'))
    mla = dict(qc=0, kc=0, vc=2, nb=2, g=2, dq_scale=MLA_SCALE, mode='chunk', qr=qall, qrc=2, kr=kr4)
    o_mla, lse_mla, *carried = _mattn_fwd(qall, kvp, kvp, name=nm('mla_attn'), comm=carry_mla, **mla)
    if carry_mla is not None:
        sv['carried_mla'] = carried[0]
    sv.update(mq=mq, mkv=mkv, cqn=cqn, ckvn=ckvn, qall=qall, kvp=kvp, mla=mla, o_mla=o_mla, lse_mla=lse_mla)
    of_m, om_m = o_fox, o_mla
    sv.update(of_m=of_m, om_m=om_m)
    b_br = p['b_branch']

    y = _gated_merge([of_m, o_gla, om_m], [w['w_up_fox'], w['w_up_gla'], w['w_up_mla']], zc, b_br, name=nm('up_merge'))
    add = lambda acc, res: res + acc
    x1 = _mm(y, w['w_out'], mode='nn', out_dtype=F32, name=nm('out'), epilogue=add, extras=[(x, *_mn())])
    sv.update(y=y, x1=x1)
    qx, hx = _mm(x1, w['w_xq'], mode='nn', out_dtype=BF16, norm_g=p['g_xa'], emit_norm=True, name=nm('xq'),
                 epilogue=lambda acc: acc * XA_SCALE)
    kvx, mn = _mm(mem, w['w_xkv'], mode='nn', out_dtype=BF16, norm_g=p['g_mem'], emit_norm=True, name=nm('xkv'))
    xa = dict(qc=0, kc=0, vc=4, nb=4, g=1, dq_scale=XA_SCALE, mode='full')
    ox_m, lse_x = _mattn_fwd(qx, kvx, kvx, name=nm('xa_attn'), **xa)
    x2 = _mm(ox_m, w['w_xo'], mode='nn', out_dtype=F32, name=nm('xo'), epilogue=add, extras=[(x1, *_mn())])
    sv.update(hx=hx, mn=mn, qx=qx, kvx=kvx, xa=xa, lse_x=lse_x, ox_m=ox_m, x2=x2)
    hpre, hm = _mm(x2, w['w_mlp1'], mode='nn', out_dtype=BF16, norm_g=p['g_mlp'], emit_norm=True, name=nm('mlp1'))
    relu2 = lambda t: jnp.square(jnp.maximum(t.astype(F32), 0.0))
    x3 = _mm(hpre, w['w_mlp2'], mode='nn', out_dtype=F32, name=nm('mlp2'), a_fn=relu2, epilogue=add,
             extras=[(x2, *_mn())])
    sv.update(hpre=hpre, hm=hm, w=w)
    return x3, sv


EARLY = ('w_mlp1', 'w_mlp2', 'w_xo', 'w_xq', 'w_xkv', 'w_out', 'w_up_fox', 'w_up_gla', 'w_up_mla')
LATE = ('w_in', 'w_gla_gate', 'w_mla_uq', 'w_mla_ukv')


def _layer_bwd(dx3, mem, w, p, tabs, sv, tag, carry_mla=None, early=None):
    c4, s4 = tabs
    nm = lambda t: f'{t}_{tag}'
    s = dx3.shape[0]
    gw, gs = {}, {}
    relu2 = lambda t: jnp.square(jnp.maximum(t.astype(F32), 0.0))
    gw['w_mlp2'] = _mm(sv['hpre'], dx3, mode='tn', out_dtype=F32, name=nm('d_mlp2'), a_fn=relu2)
    dact = lambda acc, hp: acc * (2.0 * jnp.maximum(hp.astype(F32), 0.0))
    dhpre = _mm(dx3, w['w_mlp2'], mode='nt', out_dtype=BF16, name=nm('d_act'), epilogue=dact,
                extras=[(sv['hpre'], *_mn())])
    gw['w_mlp1'] = _mm(sv['hm'], dhpre, mode='tn', out_dtype=F32, name=nm('d_mlp1'))
    dx2, gs['g_mlp'] = _mm(dhpre, w['w_mlp1'], mode='nt', out_dtype=F32, name=nm('d_hm'), epilogue=_norm_bwd_epilogue,
                           col_sums=True, full_rows=True,
                           extras=[(sv['x2'], *_mn()), (dx3, *_mn()), (p['g_mlp'], *_nvec())])
    gw['w_xo'] = _mm(sv['ox_m'], dx2, mode='tn', out_dtype=F32, name=nm('d_xo'))
    dox = _mm(dx2, w['w_xo'], mode='nt', out_dtype=BF16, name=nm('d_ox'))
    dqx_m, dkx, dvx = _mattn_bwd(sv['qx'], sv['kvx'], sv['kvx'], sv['ox_m'], dox, sv['lse_x'], name=nm('xa_bwd'),
                                 **sv['xa'])
    dkvx = jnp.concatenate([dkx, dvx], axis=1).astype(BF16)
    gw['w_xq'] = _mm(sv['hx'], dqx_m, mode='tn', out_dtype=F32, name=nm('d_xq'))
    dx1, gs['g_xa'] = _mm(dqx_m, w['w_xq'], mode='nt', out_dtype=F32, name=nm('d_hx'), epilogue=_norm_bwd_epilogue,
                          col_sums=True, full_rows=True,
                          extras=[(sv['x1'], *_mn()), (dx2, *_mn()), (p['g_xa'], *_nvec())])
    gw['w_xkv'] = _mm(sv['mn'], dkvx, mode='tn', out_dtype=F32, name=nm('d_xkv'))
    dmn = _mm(dkvx, w['w_xkv'], mode='nt', out_dtype=F32, name=nm('d_mn'))
    _, gs['g_mem'] = _norm_bwd_call(mem, dmn, p['g_mem'], None, nm('d_norm_mem'))
    gw['w_out'] = _mm(sv['y'], dx1, mode='tn', out_dtype=F32, name=nm('d_out'))
    dy = _mm(dx1, w['w_out'], mode='nt', out_dtype=BF16, name=nm('d_y'))
    zc, b_br = sv['zc'], p['b_branch']

    branches = (('w_up_fox', sv['of_m'], BF16), ('w_up_gla', sv['o_gla'], F32), ('w_up_mla', sv['om_m'], BF16))
    du, do_br, dzc, gs['b_branch'] = _gated_merge_bwd(dy, zc, b_br, [o for _, o, _ in branches],
                                                      [w[wn] for wn, _, _ in branches], [dt for _, _, dt in branches],
                                                      name=nm('d_merge'))
    for q, (wn, o_m, _) in enumerate(branches):
        gw[wn] = _mm(o_m, du[q], mode='tn', out_dtype=F32, name=nm(f'd_up{q}'))
    za = sv['za']
    carry_fox = None if early is None else early({nm_: gw[nm_] for nm_ in EARLY})
    dfq, dfk, dfv, dck, dcq, *carried_fox = _mattn_bwd(za, za, za, sv['o_fox'], do_br[0], sv['lse_fox'],
                                                       name=nm('fox_bwd'), comm=carry_fox, **sv['fox'])
    dcum = _padc(dck[:, :2, :].reshape(FOX_HEADS, s).T + dcq.reshape(s, 2, LANES)[:, :, :2].reshape(s, FOX_HEADS), 128)
    dlf = _cumsum_rows(dcum, reverse=True, name=nm('fox_dcum'))

    def dff_fn(dl, f, b):
        d = dl * _sig(-(f + b))
        return d, d

    dff, db_fox = _rowwise(dff_fn, [dlf, sv['ff']], [p['b_fox']], [(128, F32)], [128], name=nm('fox_dff'))
    gs['b_fox'] = db_fox
    dza = jnp.concatenate([dfq, dfk, dfv], axis=1).astype(BF16)
    dqn, dkn, dvv, dq_rope, dk_rope, *carried_mla = _mattn_bwd(sv['qall'], sv['kvp'], sv['kvp'], sv['o_mla'], do_br[2],
                                                               sv['lse_mla'], name=nm('mla_bwd'), comm=carry_mla,
                                                               **sv['mla'])

    def drope_fn(dn, dq, dk, c4v, s4v):
        return jnp.concatenate([dn, dq * c4v, dq * s4v], axis=1), jnp.concatenate([dk * c4v, dk * s4v], axis=1)

    dqp, dmkr2 = _rowwise(drope_fn, [dqn, dq_rope, dk_rope, c4, s4], [], [(512, BF16), (256, BF16)], name=nm('d_rope'))
    dkvp = jnp.concatenate([dkn, dvv], axis=1).astype(BF16)
    gw['uq'] = _mm(sv['cqn'], dqp, mode='tn', out_dtype=F32, name=nm('d_uq'))
    dcqn = _mm(dqp, w['uq'], mode='nt', out_dtype=F32, name=nm('d_cqn'))
    gw['ukv'] = _mm(sv['ckvn'], dkvp, mode='tn', out_dtype=F32, name=nm('d_ukv'))
    dckvn = _mm(dkvp, w['ukv'], mode='nt', out_dtype=F32, name=nm('d_ckvn'))
    dmq, gs['g_mla_q'] = _norm_bwd_call(sv['mq'], dcqn, p['g_mla_q'], None, nm('d_norm_q'))
    dmkv, gs['g_mla_kv'] = _norm_bwd_call(sv['mkv'], dckvn, p['g_mla_kv'], None, nm('d_norm_kv'))
    doraw, dgr, gs['g_gla_out'] = _rowwise(_gla_out_bwd, [sv['oraw'], sv['gr'], do_br[1]], [p['g_gla_out']],
                                           [(512, F32), (512, BF16)], [128], name=nm('d_gla_out'))
    st = sv['states']
    st_prev = jnp.concatenate([jnp.zeros_like(st[:, :1]), st[:, :-1]], axis=1)
    dgq, dgk, dgv, dla = _gla_bwd(sv['zb'], sv['la'], st, st_prev, doraw, name=nm('gla_bwd'), **sv['gla'])

    def dgate_fn(dl, gl, wg, bg):
        pre = _dot(gl.astype(BF16), wg) + bg
        dpre = dl * (1.0 / GLA_TAU) * _sig(-pre)
        return dpre, _dot(dpre.astype(BF16), wg, NT), dpre

    dpre, dglow, gs['b_gla'] = _rowwise(dgate_fn, [dla, sv['glow']], [w['gate'], p['b_gla']],
                                        [(256, BF16), (128, BF16)], [256], name=nm('d_gla_gate'))
    gw['gate'] = _mm(sv['glow'], dpre, mode='tn', out_dtype=F32, name=nm('d_wgate'))
    bf = lambda t: t.astype(BF16)
    dzb = jnp.concatenate([dgr, bf(dgq), bf(dgk), bf(dgv), bf(dmq), dmkr2, bf(dff), dglow, bf(dmkv),
                           jnp.zeros((s, B_W - B_END), BF16)], axis=1)
    h = sv['h']
    gw['in_a'] = _mm(h, dza, mode='tn', out_dtype=F32, name=nm('d_in_a'))
    gw['in_b'] = _mm(h, dzb, mode='tn', out_dtype=F32, name=nm('d_in_b'))
    gw['in_c'] = _mm(h, dzc, mode='tn', out_dtype=F32, name=nm('d_in_c'))
    add = lambda acc, prev: prev + acc
    dh = _mm(dza, w['in_a'], mode='nt', out_dtype=F32, name=nm('d_h_a'))
    dh = _mm(dzb, w['in_b'], mode='nt', out_dtype=F32, name=nm('d_h_b'), epilogue=add, extras=[(dh, *_mn())])
    dx0, gs['g_mix'] = _mm(dzc, w['in_c'], mode='nt', out_dtype=F32, name=nm('d_h_c'), col_sums=True, full_rows=True,
                           epilogue=lambda acc, prev, xv, rv, gv: _norm_bwd_epilogue(prev + acc, xv, rv, gv),
                           extras=[(dh, *_mn()), (sv['x0'], *_mn()), (dx1, *_mn()), (p['g_mix'], *_nvec())])
    return dx0, gw, gs, (carried_mla or [None])[0], (carried_fox or [None])[0]


def _loss_head(x, target, g_final):
    d = x.shape[1]

    def fn(xv, tv, gv):
        r = lax.rsqrt(jnp.mean(xv * xv, axis=-1, keepdims=True) + EPS)
        xh = xv * r
        e = xh * gv - tv
        dy = e * (1.0 / d)
        gd = dy * gv
        dx = r * (gd - xh * jnp.mean(gd * xh, axis=-1, keepdims=True))
        row_loss = 0.5 * jnp.mean(e * e, axis=-1, keepdims=True)
        return dx, dy * xh, jnp.broadcast_to(row_loss, (xv.shape[0], LANES))

    return _rowwise(fn, [x, target], [g_final], [(d, F32)], [d, LANES], name='loss_head')


def _step(args):
    shapes = {nm: args[nm].shape for nm in ORDER}
    x, mem, target = args['x'][0], args['mem'][0], args['loss_target'][0]
    s = x.shape[0]

    def wire(nm, l):
        w = args[nm][l].astype(BF16)
        if nm == 'w_in':
            w = jnp.pad(w, ((0, 0), (0, WIN_PAD - WIN_SHARD)))
        if nm == 'w_gla_gate':
            w = jnp.pad(w, ((0, GATE_WIRE_ROWS - GLA_RANK), (0, 0)))
        return w

    axis_of = dict(BIG)
    names = tuple(nm for nm, _ in BIG)
    wires = lambda l, nms: [wire(nm, l) for nm in nms]
    width = lambda nm: WIN_PAD if nm == 'w_in' else args[nm].shape[2]
    side_by_side = lambda nms: [axis_of[nm] == 2 and width(nm) % LANES == 0 for nm in nms]
    over_ici = lambda l, nms: _gather_over_ici(wires(l, nms), side_by_side(nms))

    def whole(parts, nms, tag):
        side = side_by_side(nms)
        parts = _run_comm(_gather_over_d2d(parts, side), name=f'gather_d2d_{tag}', alias=True)
        full = {nm: p if sd else _full_layer(p, axis_of[nm]) for nm, p, sd in zip(nms, parts, side)}
        if 'w_gla_gate' in full:
            full['w_gla_gate'] = full['w_gla_gate'][:GLA_RANK]
        return full

    tabs = _rope_tables(s)
    layers_p = []
    for l in range(DEPTH):
        layers_p.append({
            'g_mix': args['g_mix'][l][None], 'b_fox': _padc(args['b_fox_forget'][l][None], 128),
            'b_gla': args['b_gla_gate'][l][None], 'g_gla_out': args['g_gla_out'][l][None],
            'g_mla_q': args['g_mla_q'][l][None], 'g_mla_kv': args['g_mla_kv'][l][None],
            'b_branch': args['b_branch_gate'][l][None], 'g_xa': args['g_xa'][l][None],
            'g_mem': args['g_mem'][l][None], 'g_mlp': args['g_mlp'][l][None]})

    first = _run_comm(over_ici(0, LATE), name='gather_ici_first_l0')
    w_now = _repack_layer_weights(whole(first, LATE, 'first_l0'))
    saved = []
    xl = x
    for l in range(DEPTH):
        carry_fox = over_ici(0, EARLY) if l == 0 else None
        after_fox = (lambda parts: whole(parts, EARLY, 'rest_l0')) if l == 0 else None
        carry_mla = over_ici(l + 1, names) if l + 1 < DEPTH else None
        xl, sv = _layer_fwd(xl, mem, w_now, layers_p[l], tabs, f'l{l}', carry_fox=carry_fox, after_fox=after_fox,
                            carry_mla=carry_mla)
        saved.append(sv)
        if carry_mla is not None:
            w_now = _repack_layer_weights(whole(sv.pop('carried_mla'), names, f'l{l + 1}'))
    dx, dg_final, loss_lanes = _loss_head(xl, target, args['g_final'][None])
    cidx = lax.axis_index('c')
    chip = 2 * lax.axis_index('x') + lax.axis_index('y')

    def pair_sums(gw, nms, tag):
        mine, theirs = [], []
        for nm in nms:
            shards = _split_full(gw[nm], axis_of[nm]).astype(BF16)
            h = shards.shape[1] // 2
            mine.append(lax.dynamic_slice_in_dim(shards, cidx * h, h, axis=1))
            theirs.append(lax.dynamic_slice_in_dim(shards, (1 - cidx) * h, h, axis=1))
        got = _to_sibling(theirs, name=f'grads_swap_{tag}')
        pairs = []
        for nm, a, b in zip(nms, mine, got):
            _, h, n = a.shape
            (p,) = _rowwise(lambda u, v: (u.astype(F32) + v.astype(F32),),
                            [a.reshape(N_CHIPS * h, n), b.reshape(N_CHIPS * h, n)], [], [(n, BF16)],
                            name=f'pair_sum_{nm}_{tag}')
            pairs.append(p.reshape(N_CHIPS, h, n))
        return pairs

    def finish(pairs, from_chips, nms, tag):
        own = [lax.dynamic_index_in_dim(p, chip, axis=0, keepdims=False) for p in pairs]
        mine = [_sum_chips(o, r, name=f'chip_sum_{nm}_{tag}') for nm, o, r in zip(nms, own, from_chips)]
        theirs = _to_sibling(mine, name=f'grads_join_{tag}')
        return {nm: jnp.where(cidx == 0, jnp.concatenate([a, b]), jnp.concatenate([b, a]))
                for nm, a, b in zip(nms, mine, theirs)}

    gs_layers, done = [None] * DEPTH, [{} for _ in range(DEPTH)]
    above = None
    for l in reversed(range(DEPTH)):
        lowest, early_pairs = l == 0, []

        def early(gw_early, l=l, early_pairs=early_pairs):
            early_pairs.extend(pair_sums(gw_early, EARLY, f'early_l{l}'))
            return _chip_exchange(early_pairs)

        carry_mla = None if above is None else _chip_exchange(above[1])
        dx, gw, gs_layers[l], got_mla, got_fox = _layer_bwd(
            dx, mem, saved[l]['w'], layers_p[l], tabs, saved[l], f'l{l}', carry_mla=carry_mla,
            early=early if lowest else None)
        if above is not None:
            done[above[0]].update(finish(above[1], got_mla, names, f'l{above[0]}'))
        grads = _unpack_layer_grads(gw)
        if lowest:
            done[l].update(finish(early_pairs, got_fox, EARLY, f'early_l{l}'))
            late_pairs = pair_sums(grads, LATE, f'late_l{l}')
            from_late = _run_comm(_chip_exchange(late_pairs), name=f'grads_exchange_late_l{l}')
            done[l].update(finish(late_pairs, from_late, LATE, f'late_l{l}'))
        else:
            above = (l, pair_sums(grads, names, f'l{l}'))
    grad_x = dx[None]
    gshard = {nm: jnp.stack([done[l][nm] for l in range(DEPTH)]) for nm in names}

    small_g = []
    for nm, key in (('g_mix', 'g_mix'), ('b_fox_forget', 'b_fox'), ('b_gla_gate', 'b_gla'),
                    ('g_gla_out', 'g_gla_out'), ('g_mla_q', 'g_mla_q'), ('g_mla_kv', 'g_mla_kv'),
                    ('b_branch_gate', 'b_branch'), ('g_xa', 'g_xa'), ('g_mem', 'g_mem'), ('g_mlp', 'g_mlp')):
        width = shapes[nm][1]
        small_g.append(jnp.concatenate([gs_layers[l][key][0, :width] for l in range(DEPTH)]))
    small_g.append(dg_final[0])
    small_g.append(loss_lanes[0, :1])
    flat = jnp.concatenate(small_g)
    n_small = flat.shape[0]
    srows = -(-n_small // (8 * LANES)) * 8
    pad = lambda v: jnp.pad(v, (0, srows * LANES - v.shape[0])).reshape(srows, LANES)
    all_small = _all_gather8(pad(flat), name='gather_small')
    sw, sm, svv = (pad(jnp.concatenate([args[pre + nm].reshape(-1) for nm in SMALL] + [jnp.zeros((1,), F32)]))
                   for pre in ('', 'm_', 'v_'))

    def small_body(g_ref, w_ref, m_ref, v_ref, go_ref, d_ref, mo_ref, vo_ref):
        g = g_ref[0]
        for q in range(1, N_DEV):
            g = g + g_ref[q]
        go_ref[...] = g
        d_ref[...], mo_ref[...], vo_ref[...] = _adam(w_ref[...], g, m_ref[...], v_ref[...])

    sg, sd, snm, snv = pl.pallas_call(
        small_body, name='small_sum_adam', out_shape=[jax.ShapeDtypeStruct((srows, LANES), F32)] * 4,
        compiler_params=pltpu.CompilerParams(vmem_limit_bytes=VMEM_LIMIT))(all_small, sw, sm, svv)

    def unsmall(buf):
        v, out, off = buf.reshape(-1), {}, 0
        for nm in SMALL:
            nel = math.prod(shapes[nm])
            out[nm] = v[off:off + nel].reshape(shapes[nm])
            off += nel
        return out, v[off]

    res = {}
    (res['grad'], loss), (res['delta'], _), (res['m'], _), (res['v'], _) = (unsmall(t) for t in (sg, sd, snm, snv))

    for nm, _ in BIG:
        shp = args[nm].shape
        view = lambda t: t.reshape(shp[0] * shp[1], shp[2])
        d, m2, v2 = _rowwise(_adam, [view(args[nm]), view(gshard[nm]), view(args['m_' + nm]), view(args['v_' + nm])],
                             [], [(shp[2], F32)] * 3, name=f'adam_{nm}')
        res['grad'][nm], res['delta'][nm], res['m'][nm], res['v'][nm] = (
            gshard[nm], d.reshape(shp), m2.reshape(shp), v2.reshape(shp))

    return (loss, grad_x, *[res['grad'][nm] for nm in ORDER], *[res['delta'][nm] for nm in ORDER],
            *[res['m'][nm] for nm in ORDER], *[res['v'][nm] for nm in ORDER])


def kernel(x, mem, g_mix, w_in, b_fox_forget, w_gla_gate, b_gla_gate, g_gla_out, g_mla_q, w_mla_uq, g_mla_kv, w_mla_ukv, b_branch_gate, w_up_fox, w_up_gla, w_up_mla, w_out, g_xa, g_mem, w_xq, w_xkv, w_xo, g_mlp, w_mlp1, w_mlp2, g_final, loss_target, m_g_mix, m_w_in, m_b_fox_forget, m_w_gla_gate, m_b_gla_gate, m_g_gla_out, m_g_mla_q, m_w_mla_uq, m_g_mla_kv, m_w_mla_ukv, m_b_branch_gate, m_w_up_fox, m_w_up_gla, m_w_up_mla, m_w_out, m_g_xa, m_g_mem, m_w_xq, m_w_xkv, m_w_xo, m_g_mlp, m_w_mlp1, m_w_mlp2, m_g_final, v_g_mix, v_w_in, v_b_fox_forget, v_w_gla_gate, v_b_gla_gate, v_g_gla_out, v_g_mla_q, v_w_mla_uq, v_g_mla_kv, v_w_mla_ukv, v_b_branch_gate, v_w_up_fox, v_w_up_gla, v_w_up_mla, v_w_out, v_g_xa, v_g_mem, v_w_xq, v_w_xkv, v_w_xo, v_g_mlp, v_w_mlp1, v_w_mlp2, v_g_final):
    return _step(dict(locals()))
```

```python
import functools
import math
import typing

import jax
import jax.numpy as jnp
from jax import lax
from jax.experimental import pallas as pl
from jax.experimental.pallas import tpu as pltpu

F32 = jnp.float32
BF16 = jnp.bfloat16
MESH = pl.DeviceIdType.MESH

D_MODEL = 1024
DEPTH = 2
CHUNK = 64
EPS = 1e-6
FOX_HEADS, FOX_HD = 4, 64
GLA_HEADS, GLA_DK, GLA_DV, GLA_RANK, GLA_TAU = 4, 64, 128, 16, 16.0
MLA_HEADS, MLA_Q_RANK, MLA_KV_RANK, MLA_NOPE, MLA_ROPE, MLA_VD = 4, 256, 128, 64, 32, 64
ROPE_BASE = 10000.0
XA_HEADS, XA_HD = 4, 128
D_FF = 4 * D_MODEL
IN_SIZES = (256, 256, 256, 4, 256, 256, 512, 16, 512, 256, 128, 32, 3072)
N_IN = sum(IN_SIZES)

ADAM_LR, ADAM_B1, ADAM_B2, ADAM_EPS, ADAM_WD, ADAM_STEP = 0.001, 0.9, 0.999, 1e-08, 0.01, 10

N_CHIPS = 4
N_DEV = 8
LANES = 128
VMEM_LIMIT = 48 * 1024 * 1024
MASK_VALUE = -1e30

BIG = (('w_in', 2), ('w_gla_gate', 2), ('w_mla_uq', 2), ('w_mla_ukv', 2), ('w_up_fox', 2), ('w_up_gla', 2),
       ('w_up_mla', 2), ('w_out', 1), ('w_xq', 1), ('w_xkv', 1), ('w_xo', 2), ('w_mlp1', 2), ('w_mlp2', 1))
SMALL = ('g_mix', 'b_fox_forget', 'b_gla_gate', 'g_gla_out', 'g_mla_q', 'g_mla_kv', 'b_branch_gate',
         'g_xa', 'g_mem', 'g_mlp', 'g_final')
ORDER = ('g_mix', 'w_in', 'b_fox_forget', 'w_gla_gate', 'b_gla_gate', 'g_gla_out', 'g_mla_q', 'w_mla_uq',
         'g_mla_kv', 'w_mla_ukv', 'b_branch_gate', 'w_up_fox', 'w_up_gla', 'w_up_mla', 'w_out', 'g_xa', 'g_mem',
         'w_xq', 'w_xkv', 'w_xo', 'g_mlp', 'w_mlp1', 'w_mlp2', 'g_final')


def _params(*sem, extra_vmem=0):
    return pltpu.CompilerParams(dimension_semantics=sem, vmem_limit_bytes=VMEM_LIMIT + extra_vmem)


def _sig(x):
    return 1.0 / (1.0 + jnp.exp(-x))


def _logsig(x):
    return jnp.minimum(x, 0.0) - jnp.log(1.0 + jnp.exp(-jnp.abs(x)))


NN = (((1,), (0,)), ((), ()))
NT = (((1,), (1,)), ((), ()))
TN = (((0,), (0,)), ((), ()))


def _dot(a, b, dims=NN):
    return lax.dot_general(a, b, dims, preferred_element_type=F32)


class Cols(typing.NamedTuple):
    arr: jax.Array
    width: int
    blk: int


def _tri_dot(tri, x):
    hi = x.astype(BF16)
    r1 = x - hi.astype(F32)
    mid = r1.astype(BF16)
    lo = (r1 - mid.astype(F32)).astype(BF16)
    return _dot(tri, hi) + _dot(tri, mid) + _dot(tri, lo)


MM_TILES = ((1024, 1024), (1024, 512), (512, 1024), (512, 512), (256, 1024), (512, 256), (256, 512), (256, 256),
            (128, 1024), (128, 128))
MM_VMEM_BUDGET = 38 * 1024 * 1024
MM_VMEM_EXTRA = 8 * 1024 * 1024


def _mm_tiles(m, n, k, a_bytes, b_bytes, out_bytes, ex_bytes, has_norm, emit_norm, has_fn, full_rows):
    for tm, tn in MM_TILES:
        tm, tn = min(tm, m), min(tn, n)
        if m % tm or n % tn or (full_rows and tn != n):
            continue
        blocks = tm * k * a_bytes + k * tn * b_bytes + tm * tn * (out_bytes + ex_bytes) + (tm * k * 2 if emit_norm else 0)
        temps = tm * tn * 4 + (tm * k * 2 if has_norm else 0) + (tm * k * 6 if has_fn or has_norm else 0)
        if 2 * blocks + temps <= MM_VMEM_BUDGET + (MM_VMEM_EXTRA if has_fn else 0):
            return tm, tn
    raise ValueError((m, n, k))


def _mm(a, b, *, mode, out_dtype, name, norm_g=None, emit_norm=False, a_fn=None, extras=(), epilogue=None,
        col_sums=False, full_rows=False):
    a_blk = 0
    if isinstance(a, Cols):
        a, width, a_blk = a
        a_shape = (a.shape[0], width)
    else:
        a_shape = a.shape
    if mode == 'tn':
        k, m = a_shape
    else:
        m, k = a_shape
    n = b.shape[0] if mode == 'nt' else b.shape[1]
    assert (b.shape[1] if mode == 'nt' else b.shape[0]) == k, (name, a.shape, b.shape)
    has_norm = norm_g is not None
    ex_bytes = sum(arr.dtype.itemsize for arr, kind, _ in extras if kind == 'mn')
    tm, tn = _mm_tiles(m, n, k, a.dtype.itemsize, b.dtype.itemsize, jnp.dtype(out_dtype).itemsize, ex_bytes, has_norm,
                       emit_norm, a_fn is not None, full_rows)
    assert all(col % tn == 0 for _, _, col in extras), (name, tn)
    assert a_blk == 0 or (mode == 'nn') or (mode == 'tn' and tm == m)
    assert not (col_sums and (has_norm or emit_norm))
    ij = (lambda f: lambda g0, g1: f(g1, g0)) if col_sums else (lambda f: f)
    spec = lambda blk, f: pl.BlockSpec(blk, ij(f))
    if mode == 'tn':
        a_spec = spec((k, tm), lambda i, j: (0, i + a_blk))
    else:
        a_spec = spec((tm, k), lambda i, j: (i, a_blk))
    b_spec = spec((tn, k), lambda i, j: (j, 0)) if mode == 'nt' else spec((k, tn), lambda i, j: (0, j))
    dims = {'nn': NN, 'nt': NT, 'tn': TN}[mode]
    assert not (has_norm and mode != 'nn')
    n_ex = len(extras)

    def body(*refs):
        a_ref, b_ref = refs[0], refs[1]
        pos = 2
        g_ref = None
        if has_norm:
            g_ref = refs[pos]
            pos += 1
        ex_refs = refs[pos:pos + n_ex]
        pos += n_ex
        o_ref = refs[pos]
        pos += 1
        h_ref = None
        if emit_norm:
            h_ref = refs[pos]
            pos += 1
        if has_norm:
            an_ref = refs[pos]

            @pl.when(pl.program_id(1) == 0)
            def _():
                xf = a_ref[...].astype(F32)
                y = xf * lax.rsqrt(jnp.mean(xf * xf, axis=-1, keepdims=True) + EPS) * g_ref[...]
                an_ref[...] = y.astype(BF16)
                if emit_norm:
                    h_ref[...] = y.astype(BF16)

            av = an_ref[...]
        else:
            av = a_ref[...]
            if a_fn is not None:
                av = a_fn(av)
            av = av.astype(BF16)
        acc = _dot(av, b_ref[...].astype(BF16), dims)
        if epilogue is not None:
            acc = epilogue(acc, *[r[...] for r in ex_refs])
        acc, to_sum = acc if isinstance(acc, tuple) else (acc, acc)
        o_ref[...] = acc.astype(out_dtype)
        if col_sums:
            sum_ref = refs[pos]

            @pl.when(pl.program_id(1) == 0)
            def _():
                sum_ref[...] = jnp.zeros_like(sum_ref)

            sum_ref[...] += jnp.sum(to_sum, axis=0, keepdims=True)

    in_specs = [a_spec, b_spec]
    args = [a, b]
    if has_norm:
        in_specs.append(pl.BlockSpec((1, k), lambda i, j: (0, 0)))
        args.append(norm_g)
    for arr, kind, col in extras:
        if kind == 'mn':
            in_specs.append(spec((tm, tn), lambda i, j, o=col // tn: (i, j + o)))
        else:
            in_specs.append(spec((1, tn), lambda i, j, o=col // tn: (0, j + o)))
        args.append(arr)
    out_shape = [jax.ShapeDtypeStruct((m, n), out_dtype)]
    out_specs = [spec((tm, tn), lambda i, j: (i, j))]
    if emit_norm:
        out_shape.append(jax.ShapeDtypeStruct((m, k), BF16))
        out_specs.append(pl.BlockSpec((tm, k), lambda i, j: (i, 0)))
    if col_sums:
        out_shape.append(jax.ShapeDtypeStruct((1, n), F32))
        out_specs.append(spec((1, tn), lambda i, j: (0, j)))
    scratch = [pltpu.VMEM((tm, k), BF16)] if has_norm else []
    grid = (n // tn, m // tm) if col_sums else (m // tm, n // tn)
    res = pl.pallas_call(
        body, name=name, grid=grid, in_specs=in_specs, out_specs=out_specs, out_shape=out_shape,
        scratch_shapes=scratch,
        compiler_params=_params('arbitrary', 'arbitrary', extra_vmem=MM_VMEM_EXTRA if a_fn is not None else 0))(*args)
    return res if emit_norm or col_sums else res[0]


def _gated_merge(outs, ups, zg, bias, *, name, tm=1024, tn=512):
    s, n, nq = zg.shape[0], ups[0].shape[1], len(outs)
    tm, tn = min(tm, s), min(tn, n)
    per = n // tn

    def body(*refs):
        y = None
        for q in range(nq):
            o_ref, w_ref, z_ref, b_ref = refs[q], refs[nq + q], refs[2 * nq + q], refs[3 * nq + q]
            term = _sig(z_ref[...].astype(F32) + b_ref[...]) * _dot(o_ref[...], w_ref[...])
            y = term if y is None else y + term
        refs[4 * nq][...] = y.astype(BF16)

    in_specs = [pl.BlockSpec((tm, o.shape[1]), lambda i, j: (i, 0)) for o in outs]
    in_specs += [pl.BlockSpec((u.shape[0], tn), lambda i, j: (0, j)) for u in ups]
    in_specs += [pl.BlockSpec((tm, tn), lambda i, j, q=q: (i, j + q * per)) for q in range(nq)]
    in_specs += [pl.BlockSpec((1, tn), lambda i, j, q=q: (0, j + q * per)) for q in range(nq)]
    return pl.pallas_call(body, name=name, grid=(s // tm, per), in_specs=in_specs,
                          out_specs=pl.BlockSpec((tm, tn), lambda i, j: (i, j)),
                          out_shape=jax.ShapeDtypeStruct((s, n), BF16),
                          compiler_params=_params('arbitrary', 'arbitrary'))(*outs, *ups, *[zg] * nq, *[bias] * nq)


def _gated_merge_bwd(dy, zg, bias, outs, ups, do_dtypes, *, name, tm=512):
    s, n = dy.shape
    nq = len(outs)
    tm = min(tm, s)

    def body(*refs):
        dy_ref, zg_ref, b_ref = refs[:3]
        o_refs, w_refs = refs[3:3 + nq], refs[3 + nq:3 + 2 * nq]
        du_refs, do_refs = refs[3 + 2 * nq:3 + 3 * nq], refs[3 + 3 * nq:3 + 4 * nq]
        dz_ref, db_ref = refs[3 + 4 * nq:]

        @pl.when(pl.program_id(0) == 0)
        def _():
            db_ref[...] = jnp.zeros_like(db_ref)

        d = dy_ref[...].astype(F32)
        for q in range(nq):
            cols = slice(q * n, (q + 1) * n)
            g = _sig(zg_ref[:, cols].astype(F32) + b_ref[:, cols])
            du = (d * g).astype(BF16)
            du_refs[q][...] = du
            do_refs[q][...] = _dot(du, w_refs[q][...], NT).astype(do_dtypes[q])
            dz = d * _dot(o_refs[q][...], w_refs[q][...]) * g * (1.0 - g)
            dz_ref[:, cols] = dz.astype(BF16)
            db_ref[:, cols] += jnp.sum(dz, axis=0, keepdims=True)

    row = lambda w: pl.BlockSpec((tm, w), lambda i: (i, 0))
    whole = lambda a: pl.BlockSpec(a.shape, lambda i: (0, 0))
    in_specs = [row(n), row(nq * n), whole(bias)] + [row(o.shape[1]) for o in outs] + [whole(u) for u in ups]
    out_specs = [row(n)] * nq + [row(o.shape[1]) for o in outs] + [row(nq * n), pl.BlockSpec((1, nq * n), lambda i: (0, 0))]
    out_shape = ([jax.ShapeDtypeStruct((s, n), BF16)] * nq
                 + [jax.ShapeDtypeStruct((s, o.shape[1]), dt) for o, dt in zip(outs, do_dtypes)]
                 + [jax.ShapeDtypeStruct((s, nq * n), BF16), jax.ShapeDtypeStruct((1, nq * n), F32)])
    res = pl.pallas_call(body, name=name, grid=(s // tm,), in_specs=in_specs, out_specs=out_specs, out_shape=out_shape,
                         compiler_params=_params('arbitrary'))(dy, zg, bias, *outs, *ups)
    return res[:nq], res[nq:2 * nq], res[2 * nq], res[2 * nq + 1]


def _mn(col_off=0):
    return 'mn', col_off


def _nvec(col_off=0):
    return 'n', col_off


def _rowwise(fn, rows, consts, outs, sums=(), *, name, ts=256):
    views = [x if isinstance(x, Cols) else Cols(x, x.shape[1], 0) for x in rows]
    rows = [v.arr for v in views]
    r = rows[0].shape[0]
    ts = min(ts, r)
    assert r % ts == 0, (name, r, ts)
    nr, nc, no, ns = len(rows), len(consts), len(outs), len(sums)

    def body(*refs):
        vals = fn(*[x[...] for x in refs[:nr + nc]])
        for q in range(no):
            refs[nr + nc + q][...] = vals[q].astype(outs[q][1])
        if ns:
            @pl.when(pl.program_id(0) == 0)
            def _():
                for q in range(ns):
                    refs[nr + nc + no + q][...] = jnp.zeros((1, sums[q]), F32)

            for q in range(ns):
                refs[nr + nc + no + q][...] += jnp.sum(vals[no + q].astype(F32), axis=0, keepdims=True)

    in_specs = [pl.BlockSpec((ts, v.width), lambda i, blk=v.blk: (i, blk)) for v in views]
    in_specs += [pl.BlockSpec(x.shape, lambda i, nd=x.ndim: (0,) * nd) for x in consts]
    out_specs = [pl.BlockSpec((ts, w), lambda i: (i, 0)) for w, _ in outs]
    out_specs += [pl.BlockSpec((1, w), lambda i: (0, 0)) for w in sums]
    out_shape = [jax.ShapeDtypeStruct((r, w), dt) for w, dt in outs]
    out_shape += [jax.ShapeDtypeStruct((1, w), F32) for w in sums]
    return pl.pallas_call(body, name=name, grid=(r // ts,), in_specs=in_specs, out_specs=out_specs,
                          out_shape=out_shape, compiler_params=_params('arbitrary'))(*rows, *consts)


def _cumsum_rows(x, *, reverse, name, bs=256):
    s, w = x.shape
    bs = min(bs, s)
    nb = s // bs

    def body(x_ref, o_ref, carry):
        @pl.when(pl.program_id(0) == 0)
        def _():
            carry[...] = jnp.zeros_like(carry)

        r = lax.broadcasted_iota(jnp.int32, (bs, bs), 0)
        c = lax.broadcasted_iota(jnp.int32, (bs, bs), 1)
        tri = jnp.where((c >= r) if reverse else (c <= r), 1.0, 0.0).astype(BF16)
        xv = x_ref[...]
        o_ref[...] = _tri_dot(tri, xv) + carry[...]
        carry[...] += jnp.sum(xv, axis=0, keepdims=True)

    imap = (lambda i: (nb - 1 - i, 0)) if reverse else (lambda i: (i, 0))
    return pl.pallas_call(body, name=name, grid=(nb,), in_specs=[pl.BlockSpec((bs, w), imap)],
                          out_specs=pl.BlockSpec((bs, w), imap), out_shape=jax.ShapeDtypeStruct((s, w), F32),
                          scratch_shapes=[pltpu.VMEM((1, w), F32)], compiler_params=_params('arbitrary'))(x)


def _mask(mode, q0, k0, bq, bk):
    qpos = q0 + lax.broadcasted_iota(jnp.int32, (bq, bk), 0)
    kpos = k0 + lax.broadcasted_iota(jnp.int32, (bq, bk), 1)
    if mode == 'causal':
        return kpos <= qpos
    return kpos < (jnp.right_shift(qpos, int(math.log2(CHUNK))) + 1) * CHUNK


ROPE_SHIFT = int(math.log2(MLA_ROPE))
FOX_SCALE, MLA_SCALE, XA_SCALE = FOX_HD ** -0.5, (MLA_NOPE + MLA_ROPE) ** -0.5, XA_HD ** -0.5
ATTN_ROW_SLAB = 1024


def _lane_masks(g, b, rope):
    lane = lax.broadcasted_iota(jnp.int32, (1, LANES), 1)
    heads = [None if g == 1 else (lane >= hh * (LANES // g)) & (lane < (hh + 1) * (LANES // g)) for hh in range(g)]
    ropes = [jnp.right_shift(lane, ROPE_SHIFT) == b * g + hh for hh in range(g)] if rope else [None] * g
    return heads, ropes


def _sel(mask, x):
    return x if mask is None else jnp.where(mask, x, jnp.zeros_like(x))


class Step(typing.NamedTuple):
    qi: typing.Any
    kj: typing.Any
    first: typing.Any
    last: typing.Any
    plain: typing.Any
    masked: typing.Any


def _fwd_steps(tri, nq, nk):
    if not tri:
        return (nq, nk), lambda i, j: Step(i, j, j == 0, j == nk - 1, True, False)
    if nq % 2:
        return (nq, nk), lambda i, j: Step(i, jnp.minimum(i, j), j == 0, j == nk - 1, j < i, j == i)

    def at(i, t):
        low = t <= i
        diag = (t == i) | (t == nq)
        return Step(jnp.where(low, i, nq - 1 - i), jnp.where(low, t, t - (i + 1)), (t == 0) | (t == i + 1), diag,
                    jnp.logical_not(diag), diag)

    return (nq // 2, nq + 1), at


def _bwd_steps(tri, nq, nk):
    if not tri:
        return (nk, nq), lambda j, i: Step(i, j, i == 0, i == nq - 1, True, False)
    if nk % 2:
        return (nk, nq), lambda j, i: Step(jnp.maximum(i, j), j, i == 0, i == nq - 1, i > j, i == j)

    def at(j, t):
        n1 = nq - j
        low = t < n1
        diag = (t == 0) | (t == n1)
        return Step(jnp.where(low, j + t, nk - 1 - j + t - n1), jnp.where(low, j, nk - 1 - j), diag,
                    (t == n1 - 1) | (t == nq), jnp.logical_not(diag), diag)

    return (nk // 2, nq + 1), at


def _carried(comm, refs, n_in, n_out):
    ci, co = len(comm.ins), len(comm.out_shapes)
    ins = refs[n_in:n_in + ci]
    outs = refs[n_in + ci + n_out:n_in + ci + n_out + co]
    rest = refs[:n_in] + refs[n_in + ci:n_in + ci + n_out] + refs[n_in + ci + n_out + co:-2]
    return rest, (ins, outs, refs[-2], refs[-1])


def _mattn_fwd(q, k, v, *, qc, kc, vc, nb, g, mode, name, dq_scale=1.0, ck=None, qr=None, qrc=0, kr=None, blk=512,
               comm=None):
    s, t = q.shape[0], k.shape[0]
    bq, bk = min(blk, s), min(blk, t)
    nq, nk = s // bq, t // bk
    tri = mode != 'full'
    bias, rope = ck is not None, qr is not None
    assert not tri or (bq == bk and bq % CHUNK == 0)
    rs = min(ATTN_ROW_SLAB, bq)
    n_in = 3 + bias + 2 * rope
    (n1, n2), step_at = _fwd_steps(tri, nq, nk)

    def body(*refs):
        refs = list(refs)
        b, p1, p2 = pl.program_id(0), pl.program_id(1), pl.program_id(2)
        st = step_at(p1, p2)
        i, j = st.qi, st.kj
        if comm is not None:
            refs, comm_refs = _carried(comm, refs, n_in, 2)
            pl.when((b == 0) & (p1 == 0) & (p2 == 0))(lambda: comm.start(*comm_refs))
        q_ref, k_ref, v_ref = refs[:3]
        pos = 3
        ck_ref = qr_ref = kr_ref = None
        if bias:
            ck_ref = refs[pos]
            pos += 1
        if rope:
            qr_ref, kr_ref = refs[pos:pos + 2]
            pos += 2
        o_ref, lse_ref, m_s, l_s, acc_s = refs[pos:]
        heads, ropes = _lane_masks(g, b, rope)

        @pl.when(st.first)
        def _():
            m_s[...] = jnp.full_like(m_s, MASK_VALUE)
            l_s[...] = jnp.zeros_like(l_s)
            acc_s[...] = jnp.zeros_like(acc_s)

        def compute(masked):
            k2, v2 = k_ref[...], v_ref[...]
            for r in range(bq // rs):
                rows = pl.ds(r * rs, rs)
                q2 = q_ref[rows, :]
                alphas, pvs = [], []
                for hh in range(g):
                    sc = _dot(_sel(heads[hh], q2), k2, NT)
                    if rope:
                        sc = sc + _dot(_sel(ropes[hh], qr_ref[rows, :]), kr_ref[...], NT)
                    if bias:
                        sc = sc - ck_ref[0, hh:hh + 1, :]
                    if masked:
                        sc = jnp.where(_mask(mode, i * bq + r * rs, j * bk, rs, bk), sc, MASK_VALUE)
                    m_prev = m_s[hh, rows]
                    m_new = jnp.maximum(m_prev, jnp.max(sc, axis=1, keepdims=True))
                    alpha = jnp.exp(m_prev - m_new)
                    p = jnp.exp(sc - m_new)
                    l_s[hh, rows] = alpha * l_s[hh, rows] + jnp.sum(p, axis=1, keepdims=True)
                    m_s[hh, rows] = m_new
                    alphas.append(alpha)
                    pvs.append(_dot(p.astype(BF16), _sel(heads[hh], v2)))
                alpha = alphas[0]
                for hh in range(1, g):
                    alpha = jnp.where(heads[hh], alphas[hh], alpha)
                acc_s[rows, :] = acc_s[rows, :] * alpha + sum(pvs[1:], pvs[0])

        if tri:
            pl.when(st.plain)(functools.partial(compute, False))
            pl.when(st.masked)(functools.partial(compute, True))
        else:
            compute(False)

        @pl.when(st.last)
        def _():
            lane = lax.broadcasted_iota(jnp.int32, (bq, LANES), 1)
            l_full, lse = l_s[0], jnp.zeros((bq, LANES), F32)
            for hh in range(g):
                if hh:
                    l_full = jnp.where(heads[hh], l_s[hh], l_full)
                lse = jnp.where(lane == hh, m_s[hh] + jnp.log(l_s[hh]), lse)
            o_ref[...] = (acc_s[...] / l_full).astype(o_ref.dtype)
            lse_ref[...] = lse

        if comm is not None:
            pl.when((b == nb - 1) & (p1 == n1 - 1) & (p2 == n2 - 1))(lambda: comm.finish(*comm_refs))

    qi = lambda p1, p2: step_at(p1, p2).qi
    kj = lambda p1, p2: step_at(p1, p2).kj
    in_specs = [pl.BlockSpec((bq, LANES), lambda b, p1, p2: (qi(p1, p2), qc + b)),
                pl.BlockSpec((bk, LANES), lambda b, p1, p2: (kj(p1, p2), kc + b)),
                pl.BlockSpec((bk, LANES), lambda b, p1, p2: (kj(p1, p2), vc + b))]
    args = [q, k, v]
    if bias:
        in_specs.append(pl.BlockSpec((1, 8, bk), lambda b, p1, p2: (b, 0, kj(p1, p2))))
        args.append(ck)
    if rope:
        in_specs += [pl.BlockSpec((bq, LANES), lambda b, p1, p2: (qi(p1, p2), qrc)),
                     pl.BlockSpec((bk, LANES), lambda b, p1, p2: (kj(p1, p2), 0))]
        args += [qr, kr]
    out = pl.BlockSpec((bq, LANES), lambda b, p1, p2: (qi(p1, p2), b))
    out_specs = [out, out]
    out_shape = [jax.ShapeDtypeStruct((s, LANES * nb), BF16), jax.ShapeDtypeStruct((s, LANES * nb), F32)]
    scratch = [pltpu.VMEM((g, bq, 1), F32), pltpu.VMEM((g, bq, 1), F32), pltpu.VMEM((bq, LANES), F32)]
    if comm is not None:
        in_specs += [ANY] * len(comm.ins)
        args += comm.ins
        out_specs += [ANY] * len(comm.out_shapes)
        out_shape += comm.out_shapes
        scratch += _sems(comm.n_sems, comm.n_sems)
    res = pl.pallas_call(body, name=name, grid=(nb, n1, n2), in_specs=in_specs, out_specs=out_specs, out_shape=out_shape,
                         scratch_shapes=scratch, compiler_params=_params('arbitrary', 'arbitrary', 'arbitrary'))(*args)
    return res if comm is None else (res[0], res[1], res[2:])


def _mattn_bwd(q, k, v, o, do, lse, *, qc, kc, vc, nb, g, mode, name, dq_scale=1.0, ck=None, qr=None, qrc=0, kr=None,
               blk=1024, comm=None):
    s, t = q.shape[0], k.shape[0]
    bq, bk = min(blk, s), min(blk, t)
    nq, nk = s // bq, t // bk
    tri = mode != 'full'
    bias, rope = ck is not None, qr is not None
    rs = min(ATTN_ROW_SLAB, bq)
    n_in, n_out = 6 + bias + 2 * rope, 3 + 2 * bias + 2 * rope
    (n1, n2), step_at = _bwd_steps(tri, nq, nk)

    def body(*refs):
        refs = list(refs)
        if comm is not None:
            refs, comm_refs = _carried(comm, refs, n_in, n_out)
            first = (pl.program_id(0) == 0) & (pl.program_id(1) == 0) & (pl.program_id(2) == 0)
            pl.when(first)(lambda: comm.start(*comm_refs))
        q_ref, k_ref, v_ref, o_ref, do_ref, lse_ref = refs[:6]
        pos = 6
        ck_ref = qr_ref = kr_ref = dck_ref = dcq_ref = dqr_ref = dkr_ref = dck_s = None
        if bias:
            ck_ref = refs[pos]
            pos += 1
        if rope:
            qr_ref, kr_ref = refs[pos:pos + 2]
            pos += 2
        dq_ref, dk_ref, dv_ref = refs[pos:pos + 3]
        pos += 3
        if bias:
            dck_ref, dcq_ref = refs[pos:pos + 2]
            pos += 2
        if rope:
            dqr_ref, dkr_ref = refs[pos:pos + 2]
            pos += 2
        dk_s, dv_s = refs[pos:pos + 2]
        if bias:
            dck_s = refs[pos + 2]
        b, p1, p2 = pl.program_id(0), pl.program_id(1), pl.program_id(2)
        st = step_at(p1, p2)
        i, j = st.qi, st.kj
        heads, ropes = _lane_masks(g, b, rope)

        @pl.when((p1 == 0) & (p2 == 0))
        def _():
            dq_ref[...] = jnp.zeros_like(dq_ref)
            if bias:
                dcq_ref[...] = jnp.zeros_like(dcq_ref)

        if rope:
            @pl.when((b == 0) & (p1 == 0) & (p2 == 0))
            def _():
                dqr_ref[...] = jnp.zeros_like(dqr_ref)
                dkr_ref[...] = jnp.zeros_like(dkr_ref)

        @pl.when(st.first)
        def _():
            dk_s[...] = jnp.zeros_like(dk_s)
            dv_s[...] = jnp.zeros_like(dv_s)
            if bias:
                dck_s[...] = jnp.zeros_like(dck_s)

        def compute(masked):
            k2, v2 = k_ref[...], v_ref[...]
            lane = lax.broadcasted_iota(jnp.int32, (rs, LANES), 1)
            rk = pl.ds(pl.multiple_of(j * bk, bk), bk)
            add = lambda tot, x: x if tot is None else tot + x
            dv_t = dk_t = dkr_t = None
            dck_t = [None] * g
            for r in range(bq // rs):
                rows = pl.ds(r * rs, rs)
                rq = pl.ds(pl.multiple_of(i * bq + r * rs, rs), rs)
                q2, do2, lse2 = q_ref[rows, :], do_ref[rows, :], lse_ref[rows, :]
                dd = do2.astype(F32) * o_ref[rows, :].astype(F32)
                dq_t = dqr_t = dcq_t = None
                for hh in range(g):
                    qm = _sel(heads[hh], q2)
                    sc = _dot(qm, k2, NT)
                    if rope:
                        qrm = _sel(ropes[hh], qr_ref[rows, :])
                        sc = sc + _dot(qrm, kr_ref[...], NT)
                    if bias:
                        sc = sc - ck_ref[0, hh:hh + 1, :]
                    if masked:
                        sc = jnp.where(_mask(mode, i * bq + r * rs, j * bk, rs, bk), sc, MASK_VALUE)
                    p = jnp.exp(sc - jnp.sum(jnp.where(lane == hh, lse2, 0.0), axis=1, keepdims=True))
                    dom = _sel(heads[hh], do2)
                    dp = _dot(dom, v2, NT)
                    delta = jnp.sum(_sel(heads[hh], dd), axis=1, keepdims=True)
                    ds = p * (dp - delta)
                    dsb = ds.astype(BF16)
                    dv_t = add(dv_t, _dot(p.astype(BF16), dom, TN))
                    dk_t = add(dk_t, _dot(dsb, qm, TN))
                    dq_t = add(dq_t, _dot(dsb, _sel(heads[hh], k2)))
                    if rope:
                        dqr_t = add(dqr_t, _dot(dsb, _sel(ropes[hh], kr_ref[...])))
                        dkr_t = add(dkr_t, _dot(dsb, qrm, TN))
                    if bias:
                        dck_t[hh] = add(dck_t[hh], jnp.sum(ds, axis=0, keepdims=True))
                        dcq_t = add(dcq_t, jnp.where(lane == hh, jnp.sum(ds, axis=1, keepdims=True), 0.0))
                dq_ref[rq, :] += dq_t if dq_scale == 1.0 else dq_scale * dq_t
                if rope:
                    dqr_ref[rq, :] += dq_scale * dqr_t
                if bias:
                    dcq_ref[rq, :] += dcq_t
            dv_s[...] += dv_t
            dk_s[...] += dk_t
            if rope:
                dkr_ref[rk, :] += dkr_t
            if bias:
                for hh in range(g):
                    dck_s[hh:hh + 1, :] -= dck_t[hh]

        if tri:
            pl.when(st.plain)(functools.partial(compute, False))
            pl.when(st.masked)(functools.partial(compute, True))
        else:
            compute(False)

        @pl.when(st.last)
        def _():
            dk_ref[...] = dk_s[...]
            dv_ref[...] = dv_s[...]
            if bias:
                dck_ref[0] = dck_s[...]

        if comm is not None:
            pl.when((b == nb - 1) & (p1 == n1 - 1) & (p2 == n2 - 1))(lambda: comm.finish(*comm_refs))

    qrow = lambda col: pl.BlockSpec((bq, LANES), lambda b, p1, p2: (step_at(p1, p2).qi, col(b)))
    krow = lambda col: pl.BlockSpec((bk, LANES), lambda b, p1, p2: (step_at(p1, p2).kj, col(b)))
    in_specs = [qrow(lambda b: qc + b), krow(lambda b: kc + b), krow(lambda b: vc + b), qrow(lambda b: b),
                qrow(lambda b: b), qrow(lambda b: b)]
    args = [q, k, v, o, do, lse]
    whole = lambda rows: pl.BlockSpec((rows, LANES), lambda b, j, i: (0, b))
    out_specs = [whole(s), krow(lambda b: b), krow(lambda b: b)]
    out_shape = [jax.ShapeDtypeStruct((s, LANES * nb), F32), jax.ShapeDtypeStruct((t, LANES * nb), F32),
                 jax.ShapeDtypeStruct((t, LANES * nb), F32)]
    scratch = [pltpu.VMEM((bk, LANES), F32), pltpu.VMEM((bk, LANES), F32)]
    if bias:
        ckj = pl.BlockSpec((1, 8, bk), lambda b, p1, p2: (b, 0, step_at(p1, p2).kj))
        in_specs.append(ckj)
        args.append(ck)
        out_specs += [ckj, whole(s)]
        out_shape += [jax.ShapeDtypeStruct((nb, 8, t), F32), jax.ShapeDtypeStruct((s, LANES * nb), F32)]
    if rope:
        in_specs += [qrow(lambda b: qrc), krow(lambda b: 0)]
        args += [qr, kr]
        out_specs += [pl.BlockSpec((s, LANES), lambda b, j, i: (0, 0)), pl.BlockSpec((t, LANES), lambda b, j, i: (0, 0))]
        out_shape += [jax.ShapeDtypeStruct((s, LANES), F32), jax.ShapeDtypeStruct((t, LANES), F32)]
    if bias:
        scratch.append(pltpu.VMEM((8, bk), F32))
    if comm is not None:
        in_specs += [ANY] * len(comm.ins)
        args += comm.ins
        out_specs += [ANY] * len(comm.out_shapes)
        out_shape += comm.out_shapes
        scratch += _sems(comm.n_sems, comm.n_sems)
    res = pl.pallas_call(body, name=name, grid=(nb, n1, n2), in_specs=in_specs, out_specs=out_specs,
                         out_shape=out_shape, scratch_shapes=scratch,
                         compiler_params=_params('arbitrary', 'arbitrary', 'arbitrary'))(*args)
    return res if comm is None else (*res[:n_out], res[n_out:])


def _gla_chunk(la_c, k_c):
    r = lax.broadcasted_iota(jnp.int32, (CHUNK, CHUNK), 0)
    c = lax.broadcasted_iota(jnp.int32, (CHUNK, CHUNK), 1)
    tri = jnp.where(c <= r, 1.0, 0.0).astype(BF16)
    cum = _tri_dot(tri, la_c)
    end = jnp.sum(la_c, axis=0, keepdims=True)
    dec = jnp.exp(end - cum)
    return dec, k_c * dec, jnp.exp(end)


GLA_PAIRS = GLA_HEADS // 2


def _gla_fwd(z, la, *, qc, kc, vc, name, blk=512):
    s = z.shape[0]
    bs = min(blk, s)
    ncb = bs // CHUNK
    nblk = s // bs

    def body(q_ref, k_ref, va_ref, vb_ref, la_ref, o_ref, st_ref, st):
        @pl.when(pl.program_id(1) == 0)
        def _():
            st[...] = jnp.zeros_like(st)

        heads, _ = _lane_masks(2, 0, False)
        v_refs = (va_ref, vb_ref)
        for c in range(ncb):
            sl = pl.ds(c * CHUNK, CHUNK)
            _, kf, a = _gla_chunk(la_ref[sl, :], k_ref[sl, :])
            qs = q_ref[sl, :] * (GLA_DK ** -0.5)
            for hh in range(2):
                ut = _dot(v_refs[hh][sl, :].astype(BF16), _sel(heads[hh], kf).astype(BF16), TN)
                new = a * st[hh] + ut
                st[hh] = new
                st_ref[0, c, hh] = new
                o_ref[sl, hh * GLA_DV:(hh + 1) * GLA_DV] = _dot(_sel(heads[hh], qs).astype(BF16), new.astype(BF16), NT)

    col = lambda c0, m=1: pl.BlockSpec((bs, LANES), lambda b, i: (i, c0 + m * b))
    return pl.pallas_call(
        body, name=name, grid=(GLA_PAIRS, nblk),
        in_specs=[col(qc), col(kc), col(vc, 2), col(vc + 1, 2), col(0)],
        out_specs=[pl.BlockSpec((bs, 2 * GLA_DV), lambda b, i: (i, b)),
                   pl.BlockSpec((1, ncb, 2, GLA_DV, LANES), lambda b, i: (b, i, 0, 0, 0))],
        out_shape=[jax.ShapeDtypeStruct((s, GLA_HEADS * GLA_DV), F32),
                   jax.ShapeDtypeStruct((GLA_PAIRS, s // CHUNK, 2, GLA_DV, LANES), F32)],
        scratch_shapes=[pltpu.VMEM((2, GLA_DV, LANES), F32)],
        compiler_params=_params('arbitrary', 'arbitrary'))(z, z, z, z, la)


def _gla_bwd(z, la, st_all, st_prev, do, *, qc, kc, vc, name, blk=512):
    s = z.shape[0]
    bs = min(blk, s)
    ncb = bs // CHUNK
    nblk = s // bs

    def body(q_ref, k_ref, va_ref, vb_ref, la_ref, st_ref, sp_ref, do_ref, dq_ref, dk_ref, dv_ref, dla_ref, ga):
        @pl.when(pl.program_id(1) == 0)
        def _():
            ga[...] = jnp.zeros_like(ga)

        r = lax.broadcasted_iota(jnp.int32, (CHUNK, CHUNK), 0)
        cc = lax.broadcasted_iota(jnp.int32, (CHUNK, CHUNK), 1)
        tri_rev = jnp.where(cc >= r, 1.0, 0.0).astype(BF16)
        heads, _ = _lane_masks(2, 0, False)
        v_refs = (va_ref, vb_ref)
        for c in reversed(range(ncb)):
            sl = pl.ds(c * CHUNK, CHUNK)
            dec, kf, a = _gla_chunk(la_ref[sl, :], k_ref[sl, :])
            qs = q_ref[sl, :] * (GLA_DK ** -0.5)
            dq2 = jnp.zeros((CHUNK, LANES), F32)
            dkd = jnp.zeros((CHUNK, LANES), F32)
            da = jnp.zeros((1, LANES), F32)
            for hh in range(2):
                hv = slice(hh * GLA_DV, (hh + 1) * GLA_DV)
                dob = do_ref[sl, hv].astype(BF16)
                g = _dot(dob, _sel(heads[hh], qs).astype(BF16), TN) + ga[hh]
                gb = g.astype(BF16)
                dq2 = dq2 + _dot(dob, st_ref[0, c, hh].astype(BF16))
                dv_ref[sl, hv] = _dot(_sel(heads[hh], kf).astype(BF16), gb, NT)
                dkd = dkd + _dot(v_refs[hh][sl, :].astype(BF16), gb)
                da = da + jnp.sum(g * sp_ref[0, c, hh], axis=0, keepdims=True)
                ga[hh] = a * g
            dq_ref[sl, :] = (GLA_DK ** -0.5) * dq2
            dk_ref[sl, :] = dkd * dec
            e = dkd * kf
            dend = jnp.sum(e, axis=0, keepdims=True) + da * a
            dla_ref[sl, :] = dend - _tri_dot(tri_rev, e)

    rev = lambda i: nblk - 1 - i
    col = lambda c0, m=1: pl.BlockSpec((bs, LANES), lambda b, i: (rev(i), c0 + m * b))
    wide = pl.BlockSpec((bs, 2 * GLA_DV), lambda b, i: (rev(i), b))
    stspec = pl.BlockSpec((1, ncb, 2, GLA_DV, LANES), lambda b, i: (b, rev(i), 0, 0, 0))
    return pl.pallas_call(
        body, name=name, grid=(GLA_PAIRS, nblk),
        in_specs=[col(qc), col(kc), col(vc, 2), col(vc + 1, 2), col(0), stspec, stspec, wide],
        out_specs=[col(0), col(0), wide, col(0)],
        out_shape=[jax.ShapeDtypeStruct((s, GLA_HEADS * GLA_DK), F32), jax.ShapeDtypeStruct((s, GLA_HEADS * GLA_DK), F32),
                   jax.ShapeDtypeStruct((s, GLA_HEADS * GLA_DV), F32), jax.ShapeDtypeStruct((s, GLA_HEADS * GLA_DK), F32)],
        scratch_shapes=[pltpu.VMEM((2, GLA_DV, LANES), F32)],
        compiler_params=_params('arbitrary', 'arbitrary'))(z, z, z, z, la, st_all, st_prev, do)


def _place():
    return lax.axis_index('x'), lax.axis_index('y'), lax.axis_index('c')


ANY = pl.BlockSpec(memory_space=pl.ANY)


def _all_gather8(blk, *, name):
    m, n = blk.shape

    def body(x_ref, out_ref, send_sems, recv_sems, local_sem):
        x, y, c = _place()
        me, sibling = (x, y, c), (x, y, 1 - c)
        chips = [(1 - x, y), (x, 1 - y), (1 - x, 1 - y)]

        def slot(px, py, pc):
            return out_ref.at[4 * px + 2 * py + pc]

        def copy(q, block, to, src=None):
            return pltpu.make_async_remote_copy(
                src_ref=slot(*block) if src is None else src, dst_ref=slot(*block), send_sem=send_sems.at[q],
                recv_sem=recv_sems.at[q], device_id=to, device_id_type=MESH)

        mine = pltpu.make_async_copy(x_ref, slot(*me), local_sem)
        mine.start()
        first = [copy(0, me, sibling, src=x_ref)]
        first += [copy(1 + q, me, (*chip, c), src=x_ref) for q, chip in enumerate(chips)]
        for cp in first:
            cp.start()
        passed = [copy(4 + q, (*chip, c), sibling) for q, chip in enumerate(chips)]
        for q, chip in enumerate(chips):
            copy(1 + q, (*chip, c), me).wait_recv()
            passed[q].start()
        copy(0, sibling, me).wait_recv()
        for q, chip in enumerate(chips):
            copy(4 + q, (*chip, 1 - c), me).wait_recv()
        for cp in first + passed:
            cp.wait_send()
        mine.wait()

    return pl.pallas_call(
        body, name=name, in_specs=[ANY], out_specs=ANY, out_shape=jax.ShapeDtypeStruct((N_DEV, m, n), blk.dtype),
        scratch_shapes=[pltpu.SemaphoreType.DMA((7,)), pltpu.SemaphoreType.DMA((7,)), pltpu.SemaphoreType.DMA(())],
    )(blk)


def _sems(*counts):
    return [pltpu.SemaphoreType.DMA((n,)) for n in counts]


class Comm(typing.NamedTuple):
    ins: list
    out_shapes: list
    n_sems: int
    start: typing.Callable
    finish: typing.Callable


def _remote(src, dst, send_sems, recv_sems, idx, to):
    return lambda: pltpu.make_async_remote_copy(src_ref=src, dst_ref=dst, send_sem=send_sems.at[idx],
                                                recv_sem=recv_sems.at[idx], device_id=to, device_id_type=MESH)


def _comm_from(copies, ins, out_shapes, n_sems):
    def start(*refs):
        for cp in copies(*refs)[0]:
            cp().start()

    def finish(*refs):
        sent, received = copies(*refs)
        for cp in received:
            cp().wait_recv()
        for cp in sent:
            cp().wait_send()

    return Comm(list(ins), list(out_shapes), n_sems, start, finish)


def _run_comm(comm, *, name, alias=False):
    n_in, n_out = len(comm.ins), len(comm.out_shapes)

    def body(*refs):
        ins, outs, sems = refs[:n_in], refs[n_in:n_in + n_out], refs[n_in + n_out:]
        comm.start(ins, outs, *sems)
        comm.finish(ins, outs, *sems)

    return pl.pallas_call(body, name=name, in_specs=[ANY] * n_in, out_specs=[ANY] * n_out, out_shape=comm.out_shapes,
                          input_output_aliases={q: q for q in range(n_in)} if alias else {},
                          scratch_shapes=_sems(comm.n_sems, comm.n_sems))(*comm.ins)


def _half(rows, c):
    h = rows // 2
    return pl.ds(pl.multiple_of(c * h, h), h)


def _gathered(ref, chip, rows, side):
    if not side:
        return ref.at[chip, rows]
    n = ref.shape[1] // N_CHIPS
    return ref.at[rows, pl.ds(pl.multiple_of(chip * n, n), n)]


def _gather_over_ici(ws, side):
    def copies(ins, outs, send_sems, recv_sems):
        x, y, c = _place()
        me_chip = 2 * x + y
        sent, received = [], []
        for q, w in enumerate(ws):
            half, every = _half(w.shape[0], c), pl.ds(0, w.shape[0])
            for k, (px, py) in enumerate([(1 - x, y), (x, 1 - y), (1 - x, 1 - y)]):
                sent.append(_remote(ins[q].at[half], _gathered(outs[q], me_chip, half, side[q]), send_sems, recv_sems,
                                    4 * q + k, (px, py, c)))
                slot = _gathered(outs[q], 2 * px + py, half, side[q])
                received.append(_remote(slot, slot, send_sems, recv_sems, 4 * q + k, (px, py, c)))
            whole = _remote(ins[q], _gathered(outs[q], me_chip, every, side[q]), send_sems, recv_sems, 4 * q + 3,
                            (x, y, 1 - c))
            sent.append(whole)
            received.append(whole)
        return sent, received

    shapes = [jax.ShapeDtypeStruct((w.shape[0], N_CHIPS * w.shape[1]) if sd else (N_CHIPS,) + w.shape, w.dtype)
              for w, sd in zip(ws, side)]
    return _comm_from(copies, ws, shapes, 4 * len(ws))


def _gather_over_d2d(parts, side):
    def copies(ins, outs, send_sems, recv_sems):
        x, y, c = _place()
        sent, received = [], []
        for q, w in enumerate(parts):
            rows = w.shape[0] if side[q] else w.shape[1]
            for k, (px, py) in enumerate([(1 - x, y), (x, 1 - y), (1 - x, 1 - y)]):
                mine = _gathered(outs[q], 2 * px + py, _half(rows, c), side[q])
                theirs = _gathered(outs[q], 2 * px + py, _half(rows, 1 - c), side[q])
                sent.append(_remote(mine, mine, send_sems, recv_sems, 3 * q + k, (x, y, 1 - c)))
                received.append(_remote(theirs, theirs, send_sems, recv_sems, 3 * q + k, (x, y, 1 - c)))
        return sent, received

    return _comm_from(copies, parts, [jax.ShapeDtypeStruct(w.shape, w.dtype) for w in parts], 3 * len(parts))


def _to_sibling(gs, *, name):
    n = len(gs)

    def body(*refs):
        ins, outs = refs[:n], refs[n:2 * n]
        send_sems, recv_sems = refs[2 * n:]
        x, y, c = _place()
        cps = [pltpu.make_async_remote_copy(
            src_ref=ins[q], dst_ref=outs[q], send_sem=send_sems.at[q], recv_sem=recv_sems.at[q],
            device_id=(x, y, 1 - c), device_id_type=MESH) for q in range(n)]
        for cp in cps:
            cp.start()
        for cp in cps:
            cp.wait()

    return pl.pallas_call(body, name=name, in_specs=[ANY] * n, out_specs=[ANY] * n,
                          out_shape=[jax.ShapeDtypeStruct(g.shape, g.dtype) for g in gs],
                          scratch_shapes=_sems(n, n))(*gs)


def _chip_exchange(ps):
    def copies(ins, outs, send_sems, recv_sems):
        x, y, c = _place()
        cps = [_remote(ins[q].at[2 * px + py], outs[q].at[k], send_sems, recv_sems, 3 * q + k, (px, py, c))
               for q in range(len(ps)) for k, (px, py) in enumerate([(1 - x, y), (x, 1 - y), (1 - x, 1 - y)])]
        return cps, cps

    return _comm_from(copies, ps, [jax.ShapeDtypeStruct((3,) + p.shape[1:], p.dtype) for p in ps], 3 * len(ps))


def _sum_chips(own, r, *, name, ts=256):
    k, n = own.shape
    ts = min(ts, k)

    def body(own_ref, r_ref, o_ref):
        f = lambda q: r_ref[q].astype(F32)
        o_ref[...] = ((own_ref[...].astype(F32) + f(0)) + f(1)) + f(2)

    return pl.pallas_call(
        body, name=name, grid=(k // ts,),
        in_specs=[pl.BlockSpec((ts, n), lambda i: (i, 0)), pl.BlockSpec((3, ts, n), lambda i: (0, i, 0))],
        out_specs=pl.BlockSpec((ts, n), lambda i: (i, 0)), out_shape=jax.ShapeDtypeStruct((k, n), F32),
        compiler_params=_params('arbitrary'))(own, r)


WIN_SHARD = N_IN // N_CHIPS
WIN_PAD = -(-WIN_SHARD // LANES) * LANES
GATE_WIRE_ROWS = 32


def _full_layer(sh, axis):
    _, k, n = sh.shape
    if axis == 2:
        return sh.transpose(1, 0, 2).reshape(k, N_CHIPS * n)
    return sh.reshape(N_CHIPS * k, n)


def _win_cols(wp, o, n):
    parts = []
    while n > 0:
        j, r = divmod(o, WIN_SHARD)
        take = min(n, WIN_SHARD - r)
        parts.append(wp[:, j * WIN_PAD + r:j * WIN_PAD + r + take])
        o, n = o + take, n - take
    return parts[0] if len(parts) == 1 else jnp.concatenate(parts, axis=1)


def _split_full(full, axis):
    if full.ndim == 3:
        return full
    k, n = full.shape
    if axis == 2:
        return jnp.stack([full[:, j * (n // N_CHIPS):(j + 1) * (n // N_CHIPS)] for j in range(N_CHIPS)])
    return full.reshape(N_CHIPS, k // N_CHIPS, n)


def _padc(a, w):
    return jnp.pad(a, ((0, 0), (0, w - a.shape[1])))


def _swap16(a):
    return jnp.concatenate([a[..., 16:32], a[..., 0:16]], axis=-1)


B_GR, B_GQ, B_GK, B_GV, B_MQ, B_MKR, B_MKRS, B_FF, B_GLOW, B_MKV, B_END = (
    0, 512, 768, 1024, 1536, 1792, 1920, 2048, 2176, 2304, 2432)
B_W = 2560
O_FQ, O_FF, O_GQ, O_GLOW, O_GR, O_MQ, O_MKV, O_MKR, O_ZG = 0, 768, 772, 1796, 1812, 2324, 2580, 2708, 2740


def _repack_layer_weights(w):
    wi = functools.partial(_win_cols, w['w_in'])
    out = dict(w)
    out['in_a'] = jnp.concatenate([wi(O_FQ, 256) * FOX_SCALE, wi(O_FQ + 256, 512)], axis=1)
    kr = wi(O_MKR, 32)
    out['in_b'] = jnp.concatenate([
        wi(O_GR, 512), wi(O_GQ, 1024), wi(O_MQ, 256), jnp.tile(kr, (1, MLA_HEADS)), jnp.tile(_swap16(kr), (1, MLA_HEADS)),
        _padc(wi(O_FF, 4), 128), _padc(wi(O_GLOW, 16), 128), wi(O_MKV, 128),
        jnp.zeros((D_MODEL, B_W - B_END), kr.dtype)], axis=1)
    out['in_c'] = wi(O_ZG, 3072)
    uq = w['w_mla_uq'].reshape(MLA_Q_RANK, MLA_HEADS, MLA_NOPE + MLA_ROPE)
    rope = uq[:, :, MLA_NOPE:]
    out['uq'] = jnp.concatenate([uq[:, :, :MLA_NOPE].reshape(MLA_Q_RANK, -1), rope.reshape(MLA_Q_RANK, -1),
                                 _swap16(rope).reshape(MLA_Q_RANK, -1)], axis=1)
    ukv = w['w_mla_ukv'].reshape(MLA_KV_RANK, MLA_HEADS, MLA_NOPE + MLA_VD)
    out['ukv'] = jnp.concatenate([ukv[:, :, :MLA_NOPE].reshape(MLA_KV_RANK, -1),
                                  ukv[:, :, MLA_NOPE:].reshape(MLA_KV_RANK, -1)], axis=1)
    out['gate'] = jnp.pad(w['w_gla_gate'], ((0, 128 - GLA_RANK), (0, 0)))
    return out


def _unpack_layer_grads(g):
    a, b, c = g['in_a'], g['in_b'], g['in_c']
    fold = lambda o: sum(b[:, o + MLA_ROPE * q:o + MLA_ROPE * (q + 1)] for q in range(MLA_HEADS))
    kr = fold(B_MKR) + _swap16(fold(B_MKRS))
    pieces = [(a[:, :256] * FOX_SCALE, 0, 256), (a, 256, 512), (b, B_FF, 4), (b, B_GQ, 1024), (b, B_GLOW, 16),
              (b, B_GR, 512), (b, B_MQ, 256), (b, B_MKV, 128), (kr, 0, 32), (c, 0, 3072)]
    shards = []
    for j in range(N_CHIPS):
        lo, hi, cut, at = j * WIN_SHARD, (j + 1) * WIN_SHARD, [], 0
        for arr, first, width in pieces:
            l, h = max(lo, at), min(hi, at + width)
            if l < h:
                cut.append(arr[:, first + l - at:first + h - at])
            at += width
        shards.append(jnp.concatenate(cut, axis=1))
    w_in = jnp.stack(shards)
    uq = g['uq']
    nope = uq[:, :256].reshape(MLA_Q_RANK, MLA_HEADS, MLA_NOPE)
    rope = (uq[:, 256:384].reshape(MLA_Q_RANK, MLA_HEADS, MLA_ROPE)
            + _swap16(uq[:, 384:512].reshape(MLA_Q_RANK, MLA_HEADS, MLA_ROPE)))
    w_uq = jnp.concatenate([nope, rope], axis=2).reshape(MLA_Q_RANK, -1)
    ukv = g['ukv']
    w_ukv = jnp.concatenate([ukv[:, :256].reshape(MLA_KV_RANK, MLA_HEADS, MLA_NOPE),
                             ukv[:, 256:].reshape(MLA_KV_RANK, MLA_HEADS, MLA_VD)], axis=2).reshape(MLA_KV_RANK, -1)
    out = {'w_in': w_in, 'w_mla_uq': w_uq, 'w_mla_ukv': w_ukv, 'w_gla_gate': g['gate'][:GLA_RANK]}
    for nm in ('w_up_fox', 'w_up_gla', 'w_up_mla', 'w_out', 'w_xq', 'w_xkv', 'w_xo', 'w_mlp1', 'w_mlp2'):
        out[nm] = g[nm]
    return out


def _rope_tables(s):
    half = MLA_ROPE // 2
    inv = ROPE_BASE ** (-jnp.arange(half, dtype=F32) / half)
    ang = jnp.arange(s).astype(F32)[:, None] * inv[None, :]
    cos, sin = jnp.cos(ang), jnp.sin(ang)
    c1 = jnp.concatenate([cos, cos], axis=1)
    s1 = jnp.concatenate([-sin, sin], axis=1)
    return jnp.tile(c1, (1, MLA_HEADS)), jnp.tile(s1, (1, MLA_HEADS))


def _rms_bwd(x, dh, g):
    r = lax.rsqrt(jnp.mean(x * x, axis=-1, keepdims=True) + EPS)
    xh = x * r
    gd = dh * g
    return r * (gd - xh * jnp.mean(gd * xh, axis=-1, keepdims=True)), dh * xh


def _norm_bwd_epilogue(dh, x, dres, g):
    dx, dg = _rms_bwd(x, dh, g)
    return dres + dx, dg


def _norm_bwd_call(x, dh, g, dres, name):
    w = x.width if isinstance(x, Cols) else x.shape[1]

    def with_res(xv, dv, rv, gv):
        dx, dg = _rms_bwd(xv, dv.astype(F32), gv)
        return rv + dx, dg

    def plain(xv, dv, gv):
        return _rms_bwd(xv, dv.astype(F32), gv)

    if dres is None:
        return _rowwise(plain, [x, dh], [g], [(w, F32)], [w], name=name)
    return _rowwise(with_res, [x, dh, dres], [g], [(w, F32)], [w], name=name)


def _gla_out_fwd(oraw, gr, g_out):
    outs = []
    for hh in range(GLA_HEADS):
        sl = slice(hh * GLA_DV, (hh + 1) * GLA_DV)
        oh = oraw[:, sl]
        n = oh * lax.rsqrt(jnp.mean(oh * oh, axis=-1, keepdims=True) + EPS) * g_out
        r = gr[:, sl]
        outs.append(n * (r * _sig(r)))
    return (jnp.concatenate(outs, axis=1),)


def _gla_out_bwd(oraw, gr, dout, g_out):
    d_o, d_r, dg = [], [], 0.0
    for hh in range(GLA_HEADS):
        sl = slice(hh * GLA_DV, (hh + 1) * GLA_DV)
        oh, r, do = oraw[:, sl], gr[:, sl], dout[:, sl].astype(F32)
        rs = lax.rsqrt(jnp.mean(oh * oh, axis=-1, keepdims=True) + EPS)
        sg = _sig(r)
        dn = do * (r * sg)
        d_r.append(do * (oh * rs * g_out) * (sg + r * sg * (1.0 - sg)))
        dx, dgh = _rms_bwd(oh, dn, g_out)
        d_o.append(dx)
        dg = dg + dgh
    return jnp.concatenate(d_o, axis=1), jnp.concatenate(d_r, axis=1), dg


def _adam(w, g, m, v):
    m = ADAM_B1 * m + (1.0 - ADAM_B1) * g
    v = ADAM_B2 * v + (1.0 - ADAM_B2) * (g * g)
    m_hat = m / (1.0 - ADAM_B1 ** ADAM_STEP)
    v_hat = v / (1.0 - ADAM_B2 ** ADAM_STEP)
    return -ADAM_LR * (m_hat / (jnp.sqrt(v_hat) + ADAM_EPS) + ADAM_WD * w), m, v


def _layer_fwd(x, mem, w, p, tabs, tag, carry_fox=None, after_fox=None, carry_mla=None):
    c4, s4 = tabs
    sv = {'x0': x}
    nm = lambda t: f'{t}_{tag}'
    za, h = _mm(x, w['in_a'], mode='nn', out_dtype=BF16, norm_g=p['g_mix'], emit_norm=True, name=nm('in_a'))
    zb = _mm(h, w['in_b'], mode='nn', out_dtype=F32, name=nm('in_b'))
    zc = _mm(h, w['in_c'], mode='nn', out_dtype=F32, name=nm('in_c'))
    sv.update(h=h, zc=zc)
    ff = Cols(zb, 128, B_FF // 128)
    (lf,) = _rowwise(lambda f, b: (_logsig(f + b),), [ff], [p['b_fox']], [(128, F32)], name=nm('fox_lf'))
    cum = _cumsum_rows(lf, reverse=False, name=nm('fox_cum'))
    ckf = jnp.pad(cum[:, :FOX_HEADS].T.reshape(2, 2, x.shape[0]), ((0, 0), (0, 6), (0, 0)))
    fox = dict(qc=0, kc=2, vc=4, nb=2, g=2, mode='causal', ck=ckf)
    o_fox, lse_fox, *carried = _mattn_fwd(za, za, za, name=nm('fox_attn'), comm=carry_fox, **fox)
    if after_fox is not None:
        w = {**w, **after_fox(carried[0])}
    sv.update(ff=ff, za=za, fox=fox, o_fox=o_fox, lse_fox=lse_fox)
    glow = Cols(zb, 128, B_GLOW // 128)
    gr = Cols(zb, 512, B_GR // 512)

    def gate_fn(gl, wg, bg):
        return (_logsig(_dot(gl.astype(BF16), wg) + bg) / GLA_TAU,)

    (la,) = _rowwise(gate_fn, [glow], [w['gate'], p['b_gla']], [(256, F32)], name=nm('gla_gate'))
    gla = dict(qc=B_GQ // LANES, kc=B_GK // LANES, vc=B_GV // LANES)
    oraw, states = _gla_fwd(zb, la, name=nm('gla'), **gla)
    (o_gla,) = _rowwise(_gla_out_fwd, [oraw, gr], [p['g_gla_out']], [(512, BF16)], name=nm('gla_out'))
    sv.update(glow=glow, gr=gr, zb=zb, la=la, gla=gla, states=states, oraw=oraw, o_gla=o_gla)
    mq = Cols(zb, 256, B_MQ // 256)
    mkv = Cols(zb, 128, B_MKV // 128)
    mkr2 = Cols(zb, 256, B_MKR // 256)
    qp, cqn = _mm(mq, w['uq'], mode='nn', out_dtype=F32, norm_g=p['g_mla_q'], emit_norm=True, name=nm('mla_uq'))
    kvp, ckvn = _mm(mkv, w['ukv'], mode='nn', out_dtype=BF16, norm_g=p['g_mla_kv'], emit_norm=True,
                    name=nm('mla_ukv'))

    def rope_fn(qv, kr, c4v, s4v):
        q_rope = qv[:, 256:384] * c4v + qv[:, 384:512] * s4v
        q_scaled = jnp.concatenate([qv[:, 0:256], q_rope], axis=1) * MLA_SCALE
        return q_scaled, kr[:, 0:128] * c4v + kr[:, 128:256] * s4v

    qall, kr4 = _rowwise(rope_fn, [qp, mkr2, c4, s4], [], [(384, BF16), (128, BF16)], name=nm('rope'))
    mla = dict(qc=0, kc=0, vc=2, nb=2, g=2, dq_scale=MLA_SCALE, mode='chunk', qr=qall, qrc=2, kr=kr4)
    o_mla, lse_mla, *carried = _mattn_fwd(qall, kvp, kvp, name=nm('mla_attn'), comm=carry_mla, **mla)
    if carry_mla is not None:
        sv['carried_mla'] = carried[0]
    sv.update(mq=mq, mkv=mkv, cqn=cqn, ckvn=ckvn, qall=qall, kvp=kvp, mla=mla, o_mla=o_mla, lse_mla=lse_mla)
    of_m, om_m = o_fox, o_mla
    sv.update(of_m=of_m, om_m=om_m)
    b_br = p['b_branch']

    y = _gated_merge([of_m, o_gla, om_m], [w['w_up_fox'], w['w_up_gla'], w['w_up_mla']], zc, b_br, name=nm('up_merge'))
    add = lambda acc, res: res + acc
    x1 = _mm(y, w['w_out'], mode='nn', out_dtype=F32, name=nm('out'), epilogue=add, extras=[(x, *_mn())])
    sv.update(y=y, x1=x1)
    qx, hx = _mm(x1, w['w_xq'], mode='nn', out_dtype=BF16, norm_g=p['g_xa'], emit_norm=True, name=nm('xq'),
                 epilogue=lambda acc: acc * XA_SCALE)
    kvx, mn = _mm(mem, w['w_xkv'], mode='nn', out_dtype=BF16, norm_g=p['g_mem'], emit_norm=True, name=nm('xkv'))
    xa = dict(qc=0, kc=0, vc=4, nb=4, g=1, dq_scale=XA_SCALE, mode='full')
    ox_m, lse_x = _mattn_fwd(qx, kvx, kvx, name=nm('xa_attn'), **xa)
    x2 = _mm(ox_m, w['w_xo'], mode='nn', out_dtype=F32, name=nm('xo'), epilogue=add, extras=[(x1, *_mn())])
    sv.update(hx=hx, mn=mn, qx=qx, kvx=kvx, xa=xa, lse_x=lse_x, ox_m=ox_m, x2=x2)
    hpre, hm = _mm(x2, w['w_mlp1'], mode='nn', out_dtype=BF16, norm_g=p['g_mlp'], emit_norm=True, name=nm('mlp1'))
    relu2 = lambda t: jnp.square(jnp.maximum(t.astype(F32), 0.0))
    x3 = _mm(hpre, w['w_mlp2'], mode='nn', out_dtype=F32, name=nm('mlp2'), a_fn=relu2, epilogue=add,
             extras=[(x2, *_mn())])
    sv.update(hpre=hpre, hm=hm, w=w)
    return x3, sv


EARLY = ('w_mlp1', 'w_mlp2', 'w_xo', 'w_xq', 'w_xkv', 'w_out', 'w_up_fox', 'w_up_gla', 'w_up_mla')
LATE = ('w_in', 'w_gla_gate', 'w_mla_uq', 'w_mla_ukv')


def _layer_bwd(dx3, mem, w, p, tabs, sv, tag, carry_mla=None, early=None):
    c4, s4 = tabs
    nm = lambda t: f'{t}_{tag}'
    s = dx3.shape[0]
    gw, gs = {}, {}
    relu2 = lambda t: jnp.square(jnp.maximum(t.astype(F32), 0.0))
    gw['w_mlp2'] = _mm(sv['hpre'], dx3, mode='tn', out_dtype=F32, name=nm('d_mlp2'), a_fn=relu2)
    dact = lambda acc, hp: acc * (2.0 * jnp.maximum(hp.astype(F32), 0.0))
    dhpre = _mm(dx3, w['w_mlp2'], mode='nt', out_dtype=BF16, name=nm('d_act'), epilogue=dact,
                extras=[(sv['hpre'], *_mn())])
    gw['w_mlp1'] = _mm(sv['hm'], dhpre, mode='tn', out_dtype=F32, name=nm('d_mlp1'))
    dx2, gs['g_mlp'] = _mm(dhpre, w['w_mlp1'], mode='nt', out_dtype=F32, name=nm('d_hm'), epilogue=_norm_bwd_epilogue,
                           col_sums=True, full_rows=True,
                           extras=[(sv['x2'], *_mn()), (dx3, *_mn()), (p['g_mlp'], *_nvec())])
    gw['w_xo'] = _mm(sv['ox_m'], dx2, mode='tn', out_dtype=F32, name=nm('d_xo'))
    dox = _mm(dx2, w['w_xo'], mode='nt', out_dtype=BF16, name=nm('d_ox'))
    dqx_m, dkx, dvx = _mattn_bwd(sv['qx'], sv['kvx'], sv['kvx'], sv['ox_m'], dox, sv['lse_x'], name=nm('xa_bwd'),
                                 **sv['xa'])
    dkvx = jnp.concatenate([dkx, dvx], axis=1).astype(BF16)
    gw['w_xq'] = _mm(sv['hx'], dqx_m, mode='tn', out_dtype=F32, name=nm('d_xq'))
    dx1, gs['g_xa'] = _mm(dqx_m, w['w_xq'], mode='nt', out_dtype=F32, name=nm('d_hx'), epilogue=_norm_bwd_epilogue,
                          col_sums=True, full_rows=True,
                          extras=[(sv['x1'], *_mn()), (dx2, *_mn()), (p['g_xa'], *_nvec())])
    gw['w_xkv'] = _mm(sv['mn'], dkvx, mode='tn', out_dtype=F32, name=nm('d_xkv'))
    dmn = _mm(dkvx, w['w_xkv'], mode='nt', out_dtype=F32, name=nm('d_mn'))
    _, gs['g_mem'] = _norm_bwd_call(mem, dmn, p['g_mem'], None, nm('d_norm_mem'))
    gw['w_out'] = _mm(sv['y'], dx1, mode='tn', out_dtype=F32, name=nm('d_out'))
    dy = _mm(dx1, w['w_out'], mode='nt', out_dtype=BF16, name=nm('d_y'))
    zc, b_br = sv['zc'], p['b_branch']

    branches = (('w_up_fox', sv['of_m'], BF16), ('w_up_gla', sv['o_gla'], F32), ('w_up_mla', sv['om_m'], BF16))
    du, do_br, dzc, gs['b_branch'] = _gated_merge_bwd(dy, zc, b_br, [o for _, o, _ in branches],
                                                      [w[wn] for wn, _, _ in branches], [dt for _, _, dt in branches],
                                                      name=nm('d_merge'))
    for q, (wn, o_m, _) in enumerate(branches):
        gw[wn] = _mm(o_m, du[q], mode='tn', out_dtype=F32, name=nm(f'd_up{q}'))
    za = sv['za']
    carry_fox = None if early is None else early({nm_: gw[nm_] for nm_ in EARLY})
    dfq, dfk, dfv, dck, dcq, *carried_fox = _mattn_bwd(za, za, za, sv['o_fox'], do_br[0], sv['lse_fox'],
                                                       name=nm('fox_bwd'), comm=carry_fox, **sv['fox'])
    dcum = _padc(dck[:, :2, :].reshape(FOX_HEADS, s).T + dcq.reshape(s, 2, LANES)[:, :, :2].reshape(s, FOX_HEADS), 128)
    dlf = _cumsum_rows(dcum, reverse=True, name=nm('fox_dcum'))

    def dff_fn(dl, f, b):
        d = dl * _sig(-(f + b))
        return d, d

    dff, db_fox = _rowwise(dff_fn, [dlf, sv['ff']], [p['b_fox']], [(128, F32)], [128], name=nm('fox_dff'))
    gs['b_fox'] = db_fox
    dza = jnp.concatenate([dfq, dfk, dfv], axis=1).astype(BF16)
    dqn, dkn, dvv, dq_rope, dk_rope, *carried_mla = _mattn_bwd(sv['qall'], sv['kvp'], sv['kvp'], sv['o_mla'], do_br[2],
                                                               sv['lse_mla'], name=nm('mla_bwd'), comm=carry_mla,
                                                               **sv['mla'])

    def drope_fn(dn, dq, dk, c4v, s4v):
        return jnp.concatenate([dn, dq * c4v, dq * s4v], axis=1), jnp.concatenate([dk * c4v, dk * s4v], axis=1)

    dqp, dmkr2 = _rowwise(drope_fn, [dqn, dq_rope, dk_rope, c4, s4], [], [(512, BF16), (256, BF16)], name=nm('d_rope'))
    dkvp = jnp.concatenate([dkn, dvv], axis=1).astype(BF16)
    gw['uq'] = _mm(sv['cqn'], dqp, mode='tn', out_dtype=F32, name=nm('d_uq'))
    dcqn = _mm(dqp, w['uq'], mode='nt', out_dtype=F32, name=nm('d_cqn'))
    gw['ukv'] = _mm(sv['ckvn'], dkvp, mode='tn', out_dtype=F32, name=nm('d_ukv'))
    dckvn = _mm(dkvp, w['ukv'], mode='nt', out_dtype=F32, name=nm('d_ckvn'))
    dmq, gs['g_mla_q'] = _norm_bwd_call(sv['mq'], dcqn, p['g_mla_q'], None, nm('d_norm_q'))
    dmkv, gs['g_mla_kv'] = _norm_bwd_call(sv['mkv'], dckvn, p['g_mla_kv'], None, nm('d_norm_kv'))
    doraw, dgr, gs['g_gla_out'] = _rowwise(_gla_out_bwd, [sv['oraw'], sv['gr'], do_br[1]], [p['g_gla_out']],
                                           [(512, F32), (512, BF16)], [128], name=nm('d_gla_out'))
    st = sv['states']
    st_prev = jnp.concatenate([jnp.zeros_like(st[:, :1]), st[:, :-1]], axis=1)
    dgq, dgk, dgv, dla = _gla_bwd(sv['zb'], sv['la'], st, st_prev, doraw, name=nm('gla_bwd'), **sv['gla'])

    def dgate_fn(dl, gl, wg, bg):
        pre = _dot(gl.astype(BF16), wg) + bg
        dpre = dl * (1.0 / GLA_TAU) * _sig(-pre)
        return dpre, _dot(dpre.astype(BF16), wg, NT), dpre

    dpre, dglow, gs['b_gla'] = _rowwise(dgate_fn, [dla, sv['glow']], [w['gate'], p['b_gla']],
                                        [(256, BF16), (128, BF16)], [256], name=nm('d_gla_gate'))
    gw['gate'] = _mm(sv['glow'], dpre, mode='tn', out_dtype=F32, name=nm('d_wgate'))
    bf = lambda t: t.astype(BF16)
    dzb = jnp.concatenate([dgr, bf(dgq), bf(dgk), bf(dgv), bf(dmq), dmkr2, bf(dff), dglow, bf(dmkv),
                           jnp.zeros((s, B_W - B_END), BF16)], axis=1)
    h = sv['h']
    gw['in_a'] = _mm(h, dza, mode='tn', out_dtype=F32, name=nm('d_in_a'))
    gw['in_b'] = _mm(h, dzb, mode='tn', out_dtype=F32, name=nm('d_in_b'))
    gw['in_c'] = _mm(h, dzc, mode='tn', out_dtype=F32, name=nm('d_in_c'))
    add = lambda acc, prev: prev + acc
    dh = _mm(dza, w['in_a'], mode='nt', out_dtype=F32, name=nm('d_h_a'))
    dh = _mm(dzb, w['in_b'], mode='nt', out_dtype=F32, name=nm('d_h_b'), epilogue=add, extras=[(dh, *_mn())])
    dx0, gs['g_mix'] = _mm(dzc, w['in_c'], mode='nt', out_dtype=F32, name=nm('d_h_c'), col_sums=True, full_rows=True,
                           epilogue=lambda acc, prev, xv, rv, gv: _norm_bwd_epilogue(prev + acc, xv, rv, gv),
                           extras=[(dh, *_mn()), (sv['x0'], *_mn()), (dx1, *_mn()), (p['g_mix'], *_nvec())])
    return dx0, gw, gs, (carried_mla or [None])[0], (carried_fox or [None])[0]


def _loss_head(x, target, g_final):
    d = x.shape[1]

    def fn(xv, tv, gv):
        r = lax.rsqrt(jnp.mean(xv * xv, axis=-1, keepdims=True) + EPS)
        xh = xv * r
        e = xh * gv - tv
        dy = e * (1.0 / d)
        gd = dy * gv
        dx = r * (gd - xh * jnp.mean(gd * xh, axis=-1, keepdims=True))
        row_loss = 0.5 * jnp.mean(e * e, axis=-1, keepdims=True)
        return dx, dy * xh, jnp.broadcast_to(row_loss, (xv.shape[0], LANES))

    return _rowwise(fn, [x, target], [g_final], [(d, F32)], [d, LANES], name='loss_head')


def _step(args):
    shapes = {nm: args[nm].shape for nm in ORDER}
    x, mem, target = args['x'][0], args['mem'][0], args['loss_target'][0]
    s = x.shape[0]

    def wire(nm, l):
        w = args[nm][l].astype(BF16)
        if nm == 'w_in':
            w = jnp.pad(w, ((0, 0), (0, WIN_PAD - WIN_SHARD)))
        if nm == 'w_gla_gate':
            w = jnp.pad(w, ((0, GATE_WIRE_ROWS - GLA_RANK), (0, 0)))
        return w

    axis_of = dict(BIG)
    names = tuple(nm for nm, _ in BIG)
    wires = lambda l, nms: [wire(nm, l) for nm in nms]
    width = lambda nm: WIN_PAD if nm == 'w_in' else args[nm].shape[2]
    side_by_side = lambda nms: [axis_of[nm] == 2 and width(nm) % LANES == 0 for nm in nms]
    over_ici = lambda l, nms: _gather_over_ici(wires(l, nms), side_by_side(nms))

    def whole(parts, nms, tag):
        side = side_by_side(nms)
        parts = _run_comm(_gather_over_d2d(parts, side), name=f'gather_d2d_{tag}', alias=True)
        full = {nm: p if sd else _full_layer(p, axis_of[nm]) for nm, p, sd in zip(nms, parts, side)}
        if 'w_gla_gate' in full:
            full['w_gla_gate'] = full['w_gla_gate'][:GLA_RANK]
        return full

    tabs = _rope_tables(s)
    layers_p = []
    for l in range(DEPTH):
        layers_p.append({
            'g_mix': args['g_mix'][l][None], 'b_fox': _padc(args['b_fox_forget'][l][None], 128),
            'b_gla': args['b_gla_gate'][l][None], 'g_gla_out': args['g_gla_out'][l][None],
            'g_mla_q': args['g_mla_q'][l][None], 'g_mla_kv': args['g_mla_kv'][l][None],
            'b_branch': args['b_branch_gate'][l][None], 'g_xa': args['g_xa'][l][None],
            'g_mem': args['g_mem'][l][None], 'g_mlp': args['g_mlp'][l][None]})

    first = _run_comm(over_ici(0, LATE), name='gather_ici_first_l0')
    w_now = _repack_layer_weights(whole(first, LATE, 'first_l0'))
    saved = []
    xl = x
    for l in range(DEPTH):
        carry_fox = over_ici(0, EARLY) if l == 0 else None
        after_fox = (lambda parts: whole(parts, EARLY, 'rest_l0')) if l == 0 else None
        carry_mla = over_ici(l + 1, names) if l + 1 < DEPTH else None
        xl, sv = _layer_fwd(xl, mem, w_now, layers_p[l], tabs, f'l{l}', carry_fox=carry_fox, after_fox=after_fox,
                            carry_mla=carry_mla)
        saved.append(sv)
        if carry_mla is not None:
            w_now = _repack_layer_weights(whole(sv.pop('carried_mla'), names, f'l{l + 1}'))
    dx, dg_final, loss_lanes = _loss_head(xl, target, args['g_final'][None])
    cidx = lax.axis_index('c')
    chip = 2 * lax.axis_index('x') + lax.axis_index('y')

    def pair_sums(gw, nms, tag):
        mine, theirs = [], []
        for nm in nms:
            shards = _split_full(gw[nm], axis_of[nm]).astype(BF16)
            h = shards.shape[1] // 2
            mine.append(lax.dynamic_slice_in_dim(shards, cidx * h, h, axis=1))
            theirs.append(lax.dynamic_slice_in_dim(shards, (1 - cidx) * h, h, axis=1))
        got = _to_sibling(theirs, name=f'grads_swap_{tag}')
        pairs = []
        for nm, a, b in zip(nms, mine, got):
            _, h, n = a.shape
            (p,) = _rowwise(lambda u, v: (u.astype(F32) + v.astype(F32),),
                            [a.reshape(N_CHIPS * h, n), b.reshape(N_CHIPS * h, n)], [], [(n, BF16)],
                            name=f'pair_sum_{nm}_{tag}')
            pairs.append(p.reshape(N_CHIPS, h, n))
        return pairs

    def finish(pairs, from_chips, nms, tag):
        own = [lax.dynamic_index_in_dim(p, chip, axis=0, keepdims=False) for p in pairs]
        mine = [_sum_chips(o, r, name=f'chip_sum_{nm}_{tag}') for nm, o, r in zip(nms, own, from_chips)]
        theirs = _to_sibling(mine, name=f'grads_join_{tag}')
        return {nm: jnp.where(cidx == 0, jnp.concatenate([a, b]), jnp.concatenate([b, a]))
                for nm, a, b in zip(nms, mine, theirs)}

    gs_layers, done = [None] * DEPTH, [{} for _ in range(DEPTH)]
    above = None
    for l in reversed(range(DEPTH)):
        lowest, early_pairs = l == 0, []

        def early(gw_early, l=l, early_pairs=early_pairs):
            early_pairs.extend(pair_sums(gw_early, EARLY, f'early_l{l}'))
            return _chip_exchange(early_pairs)

        carry_mla = None if above is None else _chip_exchange(above[1])
        dx, gw, gs_layers[l], got_mla, got_fox = _layer_bwd(
            dx, mem, saved[l]['w'], layers_p[l], tabs, saved[l], f'l{l}', carry_mla=carry_mla,
            early=early if lowest else None)
        if above is not None:
            done[above[0]].update(finish(above[1], got_mla, names, f'l{above[0]}'))
        grads = _unpack_layer_grads(gw)
        if lowest:
            done[l].update(finish(early_pairs, got_fox, EARLY, f'early_l{l}'))
            late_pairs = pair_sums(grads, LATE, f'late_l{l}')
            from_late = _run_comm(_chip_exchange(late_pairs), name=f'grads_exchange_late_l{l}')
            done[l].update(finish(late_pairs, from_late, LATE, f'late_l{l}'))
        else:
            above = (l, pair_sums(grads, names, f'l{l}'))
    grad_x = dx[None]
    gshard = {nm: jnp.stack([done[l][nm] for l in range(DEPTH)]) for nm in names}

    small_g = []
    for nm, key in (('g_mix', 'g_mix'), ('b_fox_forget', 'b_fox'), ('b_gla_gate', 'b_gla'),
                    ('g_gla_out', 'g_gla_out'), ('g_mla_q', 'g_mla_q'), ('g_mla_kv', 'g_mla_kv'),
                    ('b_branch_gate', 'b_branch'), ('g_xa', 'g_xa'), ('g_mem', 'g_mem'), ('g_mlp', 'g_mlp')):
        width = shapes[nm][1]
        small_g.append(jnp.concatenate([gs_layers[l][key][0, :width] for l in range(DEPTH)]))
    small_g.append(dg_final[0])
    small_g.append(loss_lanes[0, :1])
    flat = jnp.concatenate(small_g)
    n_small = flat.shape[0]
    srows = -(-n_small // (8 * LANES)) * 8
    pad = lambda v: jnp.pad(v, (0, srows * LANES - v.shape[0])).reshape(srows, LANES)
    all_small = _all_gather8(pad(flat), name='gather_small')
    sw, sm, svv = (pad(jnp.concatenate([args[pre + nm].reshape(-1) for nm in SMALL] + [jnp.zeros((1,), F32)]))
                   for pre in ('', 'm_', 'v_'))

    def small_body(g_ref, w_ref, m_ref, v_ref, go_ref, d_ref, mo_ref, vo_ref):
        g = g_ref[0]
        for q in range(1, N_DEV):
            g = g + g_ref[q]
        go_ref[...] = g
        d_ref[...], mo_ref[...], vo_ref[...] = _adam(w_ref[...], g, m_ref[...], v_ref[...])

    sg, sd, snm, snv = pl.pallas_call(
        small_body, name='small_sum_adam', out_shape=[jax.ShapeDtypeStruct((srows, LANES), F32)] * 4,
        compiler_params=pltpu.CompilerParams(vmem_limit_bytes=VMEM_LIMIT))(all_small, sw, sm, svv)

    def unsmall(buf):
        v, out, off = buf.reshape(-1), {}, 0
        for nm in SMALL:
            nel = math.prod(shapes[nm])
            out[nm] = v[off:off + nel].reshape(shapes[nm])
            off += nel
        return out, v[off]

    res = {}
    (res['grad'], loss), (res['delta'], _), (res['m'], _), (res['v'], _) = (unsmall(t) for t in (sg, sd, snm, snv))

    for nm, _ in BIG:
        shp = args[nm].shape
        view = lambda t: t.reshape(shp[0] * shp[1], shp[2])
        d, m2, v2 = _rowwise(_adam, [view(args[nm]), view(gshard[nm]), view(args['m_' + nm]), view(args['v_' + nm])],
                             [], [(shp[2], F32)] * 3, name=f'adam_{nm}')
        res['grad'][nm], res['delta'][nm], res['m'][nm], res['v'][nm] = (
            gshard[nm], d.reshape(shp), m2.reshape(shp), v2.reshape(shp))

    return (loss, grad_x, *[res['grad'][nm] for nm in ORDER], *[res['delta'][nm] for nm in ORDER],
            *[res['m'][nm] for nm in ORDER], *[res['v'][nm] for nm in ORDER])


def kernel(x, mem, g_mix, w_in, b_fox_forget, w_gla_gate, b_gla_gate, g_gla_out, g_mla_q, w_mla_uq, g_mla_kv, w_mla_ukv, b_branch_gate, w_up_fox, w_up_gla, w_up_mla, w_out, g_xa, g_mem, w_xq, w_xkv, w_xo, g_mlp, w_mlp1, w_mlp2, g_final, loss_target, m_g_mix, m_w_in, m_b_fox_forget, m_w_gla_gate, m_b_gla_gate, m_g_gla_out, m_g_mla_q, m_w_mla_uq, m_g_mla_kv, m_w_mla_ukv, m_b_branch_gate, m_w_up_fox, m_w_up_gla, m_w_up_mla, m_w_out, m_g_xa, m_g_mem, m_w_xq, m_w_xkv, m_w_xo, m_g_mlp, m_w_mlp1, m_w_mlp2, m_g_final, v_g_mix, v_w_in, v_b_fox_forget, v_w_gla_gate, v_b_gla_gate, v_g_gla_out, v_g_mla_q, v_w_mla_uq, v_g_mla_kv, v_w_mla_ukv, v_b_branch_gate, v_w_up_fox, v_w_up_gla, v_w_up_mla, v_w_out, v_g_xa, v_g_mem, v_w_xq, v_w_xkv, v_w_xo, v_g_mlp, v_w_mlp1, v_w_mlp2, v_g_final):
    return _step(dict(locals()))
```

```python
import functools
import math
import typing

import jax
import jax.numpy as jnp
from jax import lax
from jax.experimental import pallas as pl
from jax.experimental.pallas import tpu as pltpu

F32 = jnp.float32
BF16 = jnp.bfloat16
MESH = pl.DeviceIdType.MESH

D_MODEL = 1024
DEPTH = 2
CHUNK = 64
EPS = 1e-6
FOX_HEADS, FOX_HD = 4, 64
GLA_HEADS, GLA_DK, GLA_DV, GLA_RANK, GLA_TAU = 4, 64, 128, 16, 16.0
MLA_HEADS, MLA_Q_RANK, MLA_KV_RANK, MLA_NOPE, MLA_ROPE, MLA_VD = 4, 256, 128, 64, 32, 64
ROPE_BASE = 10000.0
XA_HEADS, XA_HD = 4, 128
D_FF = 4 * D_MODEL
IN_SIZES = (256, 256, 256, 4, 256, 256, 512, 16, 512, 256, 128, 32, 3072)
N_IN = sum(IN_SIZES)

ADAM_LR, ADAM_B1, ADAM_B2, ADAM_EPS, ADAM_WD, ADAM_STEP = 0.001, 0.9, 0.999, 1e-08, 0.01, 10

N_CHIPS = 4
N_DEV = 8
LANES = 128
VMEM_LIMIT = 48 * 1024 * 1024
MASK_VALUE = -1e30

BIG = (('w_in', 2), ('w_gla_gate', 2), ('w_mla_uq', 2), ('w_mla_ukv', 2), ('w_up_fox', 2), ('w_up_gla', 2),
       ('w_up_mla', 2), ('w_out', 1), ('w_xq', 1), ('w_xkv', 1), ('w_xo', 2), ('w_mlp1', 2), ('w_mlp2', 1))
SMALL = ('g_mix', 'b_fox_forget', 'b_gla_gate', 'g_gla_out', 'g_mla_q', 'g_mla_kv', 'b_branch_gate',
         'g_xa', 'g_mem', 'g_mlp', 'g_final')
ORDER = ('g_mix', 'w_in', 'b_fox_forget', 'w_gla_gate', 'b_gla_gate', 'g_gla_out', 'g_mla_q', 'w_mla_uq',
         'g_mla_kv', 'w_mla_ukv', 'b_branch_gate', 'w_up_fox', 'w_up_gla', 'w_up_mla', 'w_out', 'g_xa', 'g_mem',
         'w_xq', 'w_xkv', 'w_xo', 'g_mlp', 'w_mlp1', 'w_mlp2', 'g_final')


def _params(*sem, extra_vmem=0):
    return pltpu.CompilerParams(dimension_semantics=sem, vmem_limit_bytes=VMEM_LIMIT + extra_vmem)


def _sig(x):
    return 1.0 / (1.0 + jnp.exp(-x))


def _logsig(x):
    return jnp.minimum(x, 0.0) - jnp.log(1.0 + jnp.exp(-jnp.abs(x)))


NN = (((1,), (0,)), ((), ()))
NT = (((1,), (1,)), ((), ()))
TN = (((0,), (0,)), ((), ()))


def _dot(a, b, dims=NN):
    return lax.dot_general(a, b, dims, preferred_element_type=F32)


class Cols(typing.NamedTuple):
    arr: jax.Array
    width: int
    blk: int


def _tri_dot(tri, x):
    hi = x.astype(BF16)
    r1 = x - hi.astype(F32)
    mid = r1.astype(BF16)
    lo = (r1 - mid.astype(F32)).astype(BF16)
    return _dot(tri, hi) + _dot(tri, mid) + _dot(tri, lo)


MM_TILES = ((1024, 1024), (1024, 512), (512, 1024), (512, 512), (256, 1024), (512, 256), (256, 512), (256, 256),
            (128, 1024), (128, 128))
MM_VMEM_BUDGET = 38 * 1024 * 1024
MM_VMEM_EXTRA = 8 * 1024 * 1024


def _mm_tiles(m, n, k, a_bytes, b_bytes, out_bytes, ex_bytes, has_norm, emit_norm, has_fn, full_rows):
    for tm, tn in MM_TILES:
        tm, tn = min(tm, m), min(tn, n)
        if m % tm or n % tn or (full_rows and tn != n):
            continue
        blocks = tm * k * a_bytes + k * tn * b_bytes + tm * tn * (out_bytes + ex_bytes) + (tm * k * 2 if emit_norm else 0)
        temps = tm * tn * 4 + (tm * k * 2 if has_norm else 0) + (tm * k * 6 if has_fn or has_norm else 0)
        if 2 * blocks + temps <= MM_VMEM_BUDGET + (MM_VMEM_EXTRA if has_fn else 0):
            return tm, tn
    raise ValueError((m, n, k))


def _mm(a, b, *, mode, out_dtype, name, norm_g=None, emit_norm=False, a_fn=None, extras=(), epilogue=None,
        col_sums=False, full_rows=False):
    a_blk = 0
    if isinstance(a, Cols):
        a, width, a_blk = a
        a_shape = (a.shape[0], width)
    else:
        a_shape = a.shape
    if mode == 'tn':
        k, m = a_shape
    else:
        m, k = a_shape
    n = b.shape[0] if mode == 'nt' else b.shape[1]
    assert (b.shape[1] if mode == 'nt' else b.shape[0]) == k, (name, a.shape, b.shape)
    has_norm = norm_g is not None
    ex_bytes = sum(arr.dtype.itemsize for arr, kind, _ in extras if kind == 'mn')
    tm, tn = _mm_tiles(m, n, k, a.dtype.itemsize, b.dtype.itemsize, jnp.dtype(out_dtype).itemsize, ex_bytes, has_norm,
                       emit_norm, a_fn is not None, full_rows)
    assert all(col % tn == 0 for _, _, col in extras), (name, tn)
    assert a_blk == 0 or (mode == 'nn') or (mode == 'tn' and tm == m)
    assert not (col_sums and (has_norm or emit_norm))
    ij = (lambda f: lambda g0, g1: f(g1, g0)) if col_sums else (lambda f: f)
    spec = lambda blk, f: pl.BlockSpec(blk, ij(f))
    if mode == 'tn':
        a_spec = spec((k, tm), lambda i, j: (0, i + a_blk))
    else:
        a_spec = spec((tm, k), lambda i, j: (i, a_blk))
    b_spec = spec((tn, k), lambda i, j: (j, 0)) if mode == 'nt' else spec((k, tn), lambda i, j: (0, j))
    dims = {'nn': NN, 'nt': NT, 'tn': TN}[mode]
    assert not (has_norm and mode != 'nn')
    n_ex = len(extras)

    def body(*refs):
        a_ref, b_ref = refs[0], refs[1]
        pos = 2
        g_ref = None
        if has_norm:
            g_ref = refs[pos]
            pos += 1
        ex_refs = refs[pos:pos + n_ex]
        pos += n_ex
        o_ref = refs[pos]
        pos += 1
        h_ref = None
        if emit_norm:
            h_ref = refs[pos]
            pos += 1
        if has_norm:
            an_ref = refs[pos]

            @pl.when(pl.program_id(1) == 0)
            def _():
                xf = a_ref[...].astype(F32)
                y = xf * lax.rsqrt(jnp.mean(xf * xf, axis=-1, keepdims=True) + EPS) * g_ref[...]
                an_ref[...] = y.astype(BF16)
                if emit_norm:
                    h_ref[...] = y.astype(BF16)

            av = an_ref[...]
        else:
            av = a_ref[...]
            if a_fn is not None:
                av = a_fn(av)
            av = av.astype(BF16)
        acc = _dot(av, b_ref[...].astype(BF16), dims)
        if epilogue is not None:
            acc = epilogue(acc, *[r[...] for r in ex_refs])
        acc, to_sum = acc if isinstance(acc, tuple) else (acc, acc)
        o_ref[...] = acc.astype(out_dtype)
        if col_sums:
            sum_ref = refs[pos]

            @pl.when(pl.program_id(1) == 0)
            def _():
                sum_ref[...] = jnp.zeros_like(sum_ref)

            sum_ref[...] += jnp.sum(to_sum, axis=0, keepdims=True)

    in_specs = [a_spec, b_spec]
    args = [a, b]
    if has_norm:
        in_specs.append(pl.BlockSpec((1, k), lambda i, j: (0, 0)))
        args.append(norm_g)
    for arr, kind, col in extras:
        if kind == 'mn':
            in_specs.append(spec((tm, tn), lambda i, j, o=col // tn: (i, j + o)))
        else:
            in_specs.append(spec((1, tn), lambda i, j, o=col // tn: (0, j + o)))
        args.append(arr)
    out_shape = [jax.ShapeDtypeStruct((m, n), out_dtype)]
    out_specs = [spec((tm, tn), lambda i, j: (i, j))]
    if emit_norm:
        out_shape.append(jax.ShapeDtypeStruct((m, k), BF16))
        out_specs.append(pl.BlockSpec((tm, k), lambda i, j: (i, 0)))
    if col_sums:
        out_shape.append(jax.ShapeDtypeStruct((1, n), F32))
        out_specs.append(spec((1, tn), lambda i, j: (0, j)))
    scratch = [pltpu.VMEM((tm, k), BF16)] if has_norm else []
    grid = (n // tn, m // tm) if col_sums else (m // tm, n // tn)
    res = pl.pallas_call(
        body, name=name, grid=grid, in_specs=in_specs, out_specs=out_specs, out_shape=out_shape,
        scratch_shapes=scratch,
        compiler_params=_params('arbitrary', 'arbitrary', extra_vmem=MM_VMEM_EXTRA if a_fn is not None else 0))(*args)
    return res if emit_norm or col_sums else res[0]


def _gated_merge(outs, ups, zg, bias, *, name, tm=1024, tn=512):
    s, n, nq = zg.shape[0], ups[0].shape[1], len(outs)
    tm, tn = min(tm, s), min(tn, n)
    per = n // tn

    def body(*refs):
        y = None
        for q in range(nq):
            o_ref, w_ref, z_ref, b_ref = refs[q], refs[nq + q], refs[2 * nq + q], refs[3 * nq + q]
            term = _sig(z_ref[...].astype(F32) + b_ref[...]) * _dot(o_ref[...], w_ref[...])
            y = term if y is None else y + term
        refs[4 * nq][...] = y.astype(BF16)

    in_specs = [pl.BlockSpec((tm, o.shape[1]), lambda i, j: (i, 0)) for o in outs]
    in_specs += [pl.BlockSpec((u.shape[0], tn), lambda i, j: (0, j)) for u in ups]
    in_specs += [pl.BlockSpec((tm, tn), lambda i, j, q=q: (i, j + q * per)) for q in range(nq)]
    in_specs += [pl.BlockSpec((1, tn), lambda i, j, q=q: (0, j + q * per)) for q in range(nq)]
    return pl.pallas_call(body, name=name, grid=(s // tm, per), in_specs=in_specs,
                          out_specs=pl.BlockSpec((tm, tn), lambda i, j: (i, j)),
                          out_shape=jax.ShapeDtypeStruct((s, n), BF16),
                          compiler_params=_params('arbitrary', 'arbitrary'))(*outs, *ups, *[zg] * nq, *[bias] * nq)


def _gated_merge_bwd(dy, zg, bias, outs, ups, do_dtypes, *, name, tm=512):
    s, n = dy.shape
    nq = len(outs)
    tm = min(tm, s)

    def body(*refs):
        dy_ref, zg_ref, b_ref = refs[:3]
        o_refs, w_refs = refs[3:3 + nq], refs[3 + nq:3 + 2 * nq]
        du_refs, do_refs = refs[3 + 2 * nq:3 + 3 * nq], refs[3 + 3 * nq:3 + 4 * nq]
        dz_ref, db_ref = refs[3 + 4 * nq:]

        @pl.when(pl.program_id(0) == 0)
        def _():
            db_ref[...] = jnp.zeros_like(db_ref)

        d = dy_ref[...].astype(F32)
        for q in range(nq):
            cols = slice(q * n, (q + 1) * n)
            g = _sig(zg_ref[:, cols].astype(F32) + b_ref[:, cols])
            du = (d * g).astype(BF16)
            du_refs[q][...] = du
            do_refs[q][...] = _dot(du, w_refs[q][...], NT).astype(do_dtypes[q])
            dz = d * _dot(o_refs[q][...], w_refs[q][...]) * g * (1.0 - g)
            dz_ref[:, cols] = dz.astype(BF16)
            db_ref[:, cols] += jnp.sum(dz, axis=0, keepdims=True)

    row = lambda w: pl.BlockSpec((tm, w), lambda i: (i, 0))
    whole = lambda a: pl.BlockSpec(a.shape, lambda i: (0, 0))
    in_specs = [row(n), row(nq * n), whole(bias)] + [row(o.shape[1]) for o in outs] + [whole(u) for u in ups]
    out_specs = [row(n)] * nq + [row(o.shape[1]) for o in outs] + [row(nq * n), pl.BlockSpec((1, nq * n), lambda i: (0, 0))]
    out_shape = ([jax.ShapeDtypeStruct((s, n), BF16)] * nq
                 + [jax.ShapeDtypeStruct((s, o.shape[1]), dt) for o, dt in zip(outs, do_dtypes)]
                 + [jax.ShapeDtypeStruct((s, nq * n), BF16), jax.ShapeDtypeStruct((1, nq * n), F32)])
    res = pl.pallas_call(body, name=name, grid=(s // tm,), in_specs=in_specs, out_specs=out_specs, out_shape=out_shape,
                         compiler_params=_params('arbitrary'))(dy, zg, bias, *outs, *ups)
    return res[:nq], res[nq:2 * nq], res[2 * nq], res[2 * nq + 1]


def _mn(col_off=0):
    return 'mn', col_off


def _nvec(col_off=0):
    return 'n', col_off


def _rowwise(fn, rows, consts, outs, sums=(), *, name, ts=512):
    views = [x if isinstance(x, Cols) else Cols(x, x.shape[1], 0) for x in rows]
    rows = [v.arr for v in views]
    r = rows[0].shape[0]
    ts = min(ts, r)
    assert r % ts == 0, (name, r, ts)
    nr, nc, no, ns = len(rows), len(consts), len(outs), len(sums)

    def body(*refs):
        vals = fn(*[x[...] for x in refs[:nr + nc]])
        for q in range(no):
            refs[nr + nc + q][...] = vals[q].astype(outs[q][1])
        if ns:
            @pl.when(pl.program_id(0) == 0)
            def _():
                for q in range(ns):
                    refs[nr + nc + no + q][...] = jnp.zeros((1, sums[q]), F32)

            for q in range(ns):
                refs[nr + nc + no + q][...] += jnp.sum(vals[no + q].astype(F32), axis=0, keepdims=True)

    in_specs = [pl.BlockSpec((ts, v.width), lambda i, blk=v.blk: (i, blk)) for v in views]
    in_specs += [pl.BlockSpec(x.shape, lambda i, nd=x.ndim: (0,) * nd) for x in consts]
    out_specs = [pl.BlockSpec((ts, w), lambda i: (i, 0)) for w, _ in outs]
    out_specs += [pl.BlockSpec((1, w), lambda i: (0, 0)) for w in sums]
    out_shape = [jax.ShapeDtypeStruct((r, w), dt) for w, dt in outs]
    out_shape += [jax.ShapeDtypeStruct((1, w), F32) for w in sums]
    return pl.pallas_call(body, name=name, grid=(r // ts,), in_specs=in_specs, out_specs=out_specs,
                          out_shape=out_shape, compiler_params=_params('arbitrary'))(*rows, *consts)


def _cumsum_rows(x, *, reverse, name, bs=256):
    s, w = x.shape
    bs = min(bs, s)
    nb = s // bs

    def body(x_ref, o_ref, carry):
        @pl.when(pl.program_id(0) == 0)
        def _():
            carry[...] = jnp.zeros_like(carry)

        r = lax.broadcasted_iota(jnp.int32, (bs, bs), 0)
        c = lax.broadcasted_iota(jnp.int32, (bs, bs), 1)
        tri = jnp.where((c >= r) if reverse else (c <= r), 1.0, 0.0).astype(BF16)
        xv = x_ref[...]
        o_ref[...] = _tri_dot(tri, xv) + carry[...]
        carry[...] += jnp.sum(xv, axis=0, keepdims=True)

    imap = (lambda i: (nb - 1 - i, 0)) if reverse else (lambda i: (i, 0))
    return pl.pallas_call(body, name=name, grid=(nb,), in_specs=[pl.BlockSpec((bs, w), imap)],
                          out_specs=pl.BlockSpec((bs, w), imap), out_shape=jax.ShapeDtypeStruct((s, w), F32),
                          scratch_shapes=[pltpu.VMEM((1, w), F32)], compiler_params=_params('arbitrary'))(x)


def _mask(mode, q0, k0, bq, bk):
    qpos = q0 + lax.broadcasted_iota(jnp.int32, (bq, bk), 0)
    kpos = k0 + lax.broadcasted_iota(jnp.int32, (bq, bk), 1)
    if mode == 'causal':
        return kpos <= qpos
    return kpos < (jnp.right_shift(qpos, int(math.log2(CHUNK))) + 1) * CHUNK


ROPE_SHIFT = int(math.log2(MLA_ROPE))
FOX_SCALE, MLA_SCALE, XA_SCALE = FOX_HD ** -0.5, (MLA_NOPE + MLA_ROPE) ** -0.5, XA_HD ** -0.5
ATTN_ROW_SLAB = 512


def _lane_masks(g, b, rope):
    lane = lax.broadcasted_iota(jnp.int32, (1, LANES), 1)
    heads = [None if g == 1 else (lane >= hh * (LANES // g)) & (lane < (hh + 1) * (LANES // g)) for hh in range(g)]
    ropes = [jnp.right_shift(lane, ROPE_SHIFT) == b * g + hh for hh in range(g)] if rope else [None] * g
    return heads, ropes


def _sel(mask, x):
    return x if mask is None else jnp.where(mask, x, jnp.zeros_like(x))


class Step(typing.NamedTuple):
    qi: typing.Any
    kj: typing.Any
    first: typing.Any
    last: typing.Any
    plain: typing.Any
    masked: typing.Any


def _fwd_steps(tri, nq, nk):
    if not tri:
        return (nq, nk), lambda i, j: Step(i, j, j == 0, j == nk - 1, True, False)
    if nq % 2:
        return (nq, nk), lambda i, j: Step(i, jnp.minimum(i, j), j == 0, j == nk - 1, j < i, j == i)

    def at(i, t):
        low = t <= i
        diag = (t == i) | (t == nq)
        return Step(jnp.where(low, i, nq - 1 - i), jnp.where(low, t, t - (i + 1)), (t == 0) | (t == i + 1), diag,
                    jnp.logical_not(diag), diag)

    return (nq // 2, nq + 1), at


def _bwd_steps(tri, nq, nk):
    if not tri:
        return (nk, nq), lambda j, i: Step(i, j, i == 0, i == nq - 1, True, False)
    if nk % 2:
        return (nk, nq), lambda j, i: Step(jnp.maximum(i, j), j, i == 0, i == nq - 1, i > j, i == j)

    def at(j, t):
        n1 = nq - j
        low = t < n1
        diag = (t == 0) | (t == n1)
        return Step(jnp.where(low, j + t, nk - 1 - j + t - n1), jnp.where(low, j, nk - 1 - j), diag,
                    (t == n1 - 1) | (t == nq), jnp.logical_not(diag), diag)

    return (nk // 2, nq + 1), at


def _carried(comm, refs, n_in, n_out):
    ci, co = len(comm.ins), len(comm.out_shapes)
    ins = refs[n_in:n_in + ci]
    outs = refs[n_in + ci + n_out:n_in + ci + n_out + co]
    rest = refs[:n_in] + refs[n_in + ci:n_in + ci + n_out] + refs[n_in + ci + n_out + co:-2]
    return rest, (ins, outs, refs[-2], refs[-1])


def _mattn_fwd(q, k, v, *, qc, kc, vc, nb, g, mode, name, dq_scale=1.0, ck=None, qr=None, qrc=0, kr=None, blk=512,
               comm=None):
    s, t = q.shape[0], k.shape[0]
    bq, bk = min(blk, s), min(blk, t)
    nq, nk = s // bq, t // bk
    tri = mode != 'full'
    bias, rope = ck is not None, qr is not None
    assert not tri or (bq == bk and bq % CHUNK == 0)
    rs = min(ATTN_ROW_SLAB, bq)
    n_in = 3 + bias + 2 * rope
    (n1, n2), step_at = _fwd_steps(tri, nq, nk)

    def body(*refs):
        refs = list(refs)
        b, p1, p2 = pl.program_id(0), pl.program_id(1), pl.program_id(2)
        st = step_at(p1, p2)
        i, j = st.qi, st.kj
        if comm is not None:
            refs, comm_refs = _carried(comm, refs, n_in, 2)
            pl.when((b == 0) & (p1 == 0) & (p2 == 0))(lambda: comm.start(*comm_refs))
        q_ref, k_ref, v_ref = refs[:3]
        pos = 3
        ck_ref = qr_ref = kr_ref = None
        if bias:
            ck_ref = refs[pos]
            pos += 1
        if rope:
            qr_ref, kr_ref = refs[pos:pos + 2]
            pos += 2
        o_ref, lse_ref, m_s, l_s, acc_s = refs[pos:]
        heads, ropes = _lane_masks(g, b, rope)

        @pl.when(st.first)
        def _():
            m_s[...] = jnp.full_like(m_s, MASK_VALUE)
            l_s[...] = jnp.zeros_like(l_s)
            acc_s[...] = jnp.zeros_like(acc_s)

        def compute(masked):
            k2, v2 = k_ref[...], v_ref[...]
            for r in range(bq // rs):
                rows = pl.ds(r * rs, rs)
                q2 = q_ref[rows, :]
                alphas, pvs = [], []
                for hh in range(g):
                    sc = _dot(_sel(heads[hh], q2), k2, NT)
                    if rope:
                        sc = sc + _dot(_sel(ropes[hh], qr_ref[rows, :]), kr_ref[...], NT)
                    if bias:
                        sc = sc - ck_ref[0, hh:hh + 1, :]
                    if masked:
                        sc = jnp.where(_mask(mode, i * bq + r * rs, j * bk, rs, bk), sc, MASK_VALUE)
                    m_prev = m_s[hh, rows]
                    m_new = jnp.maximum(m_prev, jnp.max(sc, axis=1, keepdims=True))
                    alpha = jnp.exp(m_prev - m_new)
                    p = jnp.exp(sc - m_new)
                    l_s[hh, rows] = alpha * l_s[hh, rows] + jnp.sum(p, axis=1, keepdims=True)
                    m_s[hh, rows] = m_new
                    alphas.append(alpha)
                    pvs.append(_dot(p.astype(BF16), _sel(heads[hh], v2)))
                alpha = alphas[0]
                for hh in range(1, g):
                    alpha = jnp.where(heads[hh], alphas[hh], alpha)
                acc_s[rows, :] = acc_s[rows, :] * alpha + sum(pvs[1:], pvs[0])

        if tri:
            pl.when(st.plain)(functools.partial(compute, False))
            pl.when(st.masked)(functools.partial(compute, True))
        else:
            compute(False)

        @pl.when(st.last)
        def _():
            lane = lax.broadcasted_iota(jnp.int32, (bq, LANES), 1)
            l_full, lse = l_s[0], jnp.zeros((bq, LANES), F32)
            for hh in range(g):
                if hh:
                    l_full = jnp.where(heads[hh], l_s[hh], l_full)
                lse = jnp.where(lane == hh, m_s[hh] + jnp.log(l_s[hh]), lse)
            o_ref[...] = (acc_s[...] / l_full).astype(o_ref.dtype)
            lse_ref[...] = lse

        if comm is not None:
            pl.when((b == nb - 1) & (p1 == n1 - 1) & (p2 == n2 - 1))(lambda: comm.finish(*comm_refs))

    qi = lambda p1, p2: step_at(p1, p2).qi
    kj = lambda p1, p2: step_at(p1, p2).kj
    in_specs = [pl.BlockSpec((bq, LANES), lambda b, p1, p2: (qi(p1, p2), qc + b)),
                pl.BlockSpec((bk, LANES), lambda b, p1, p2: (kj(p1, p2), kc + b)),
                pl.BlockSpec((bk, LANES), lambda b, p1, p2: (kj(p1, p2), vc + b))]
    args = [q, k, v]
    if bias:
        in_specs.append(pl.BlockSpec((1, 8, bk), lambda b, p1, p2: (b, 0, kj(p1, p2))))
        args.append(ck)
    if rope:
        in_specs += [pl.BlockSpec((bq, LANES), lambda b, p1, p2: (qi(p1, p2), qrc)),
                     pl.BlockSpec((bk, LANES), lambda b, p1, p2: (kj(p1, p2), 0))]
        args += [qr, kr]
    out = pl.BlockSpec((bq, LANES), lambda b, p1, p2: (qi(p1, p2), b))
    out_specs = [out, out]
    out_shape = [jax.ShapeDtypeStruct((s, LANES * nb), BF16), jax.ShapeDtypeStruct((s, LANES * nb), F32)]
    scratch = [pltpu.VMEM((g, bq, 1), F32), pltpu.VMEM((g, bq, 1), F32), pltpu.VMEM((bq, LANES), F32)]
    if comm is not None:
        in_specs += [ANY] * len(comm.ins)
        args += comm.ins
        out_specs += [ANY] * len(comm.out_shapes)
        out_shape += comm.out_shapes
        scratch += _sems(comm.n_sems, comm.n_sems)
    res = pl.pallas_call(body, name=name, grid=(nb, n1, n2), in_specs=in_specs, out_specs=out_specs, out_shape=out_shape,
                         scratch_shapes=scratch, compiler_params=_params('arbitrary', 'arbitrary', 'arbitrary'))(*args)
    return res if comm is None else (res[0], res[1], res[2:])


def _mattn_bwd(q, k, v, o, do, lse, *, qc, kc, vc, nb, g, mode, name, dq_scale=1.0, ck=None, qr=None, qrc=0, kr=None,
               blk=512, comm=None):
    s, t = q.shape[0], k.shape[0]
    bq, bk = min(blk, s), min(blk, t)
    nq, nk = s // bq, t // bk
    tri = mode != 'full'
    bias, rope = ck is not None, qr is not None
    rs = min(ATTN_ROW_SLAB, bq)
    n_in, n_out = 6 + bias + 2 * rope, 3 + 2 * bias + 2 * rope
    (n1, n2), step_at = _bwd_steps(tri, nq, nk)

    def body(*refs):
        refs = list(refs)
        if comm is not None:
            refs, comm_refs = _carried(comm, refs, n_in, n_out)
            first = (pl.program_id(0) == 0) & (pl.program_id(1) == 0) & (pl.program_id(2) == 0)
            pl.when(first)(lambda: comm.start(*comm_refs))
        q_ref, k_ref, v_ref, o_ref, do_ref, lse_ref = refs[:6]
        pos = 6
        ck_ref = qr_ref = kr_ref = dck_ref = dcq_ref = dqr_ref = dkr_ref = dck_s = None
        if bias:
            ck_ref = refs[pos]
            pos += 1
        if rope:
            qr_ref, kr_ref = refs[pos:pos + 2]
            pos += 2
        dq_ref, dk_ref, dv_ref = refs[pos:pos + 3]
        pos += 3
        if bias:
            dck_ref, dcq_ref = refs[pos:pos + 2]
            pos += 2
        if rope:
            dqr_ref, dkr_ref = refs[pos:pos + 2]
            pos += 2
        dk_s, dv_s = refs[pos:pos + 2]
        if bias:
            dck_s = refs[pos + 2]
        b, p1, p2 = pl.program_id(0), pl.program_id(1), pl.program_id(2)
        st = step_at(p1, p2)
        i, j = st.qi, st.kj
        heads, ropes = _lane_masks(g, b, rope)

        @pl.when((p1 == 0) & (p2 == 0))
        def _():
            dq_ref[...] = jnp.zeros_like(dq_ref)
            if bias:
                dcq_ref[...] = jnp.zeros_like(dcq_ref)

        if rope:
            @pl.when((b == 0) & (p1 == 0) & (p2 == 0))
            def _():
                dqr_ref[...] = jnp.zeros_like(dqr_ref)
                dkr_ref[...] = jnp.zeros_like(dkr_ref)

        @pl.when(st.first)
        def _():
            dk_s[...] = jnp.zeros_like(dk_s)
            dv_s[...] = jnp.zeros_like(dv_s)
            if bias:
                dck_s[...] = jnp.zeros_like(dck_s)

        def compute(masked):
            k2, v2 = k_ref[...], v_ref[...]
            lane = lax.broadcasted_iota(jnp.int32, (rs, LANES), 1)
            rk = pl.ds(pl.multiple_of(j * bk, bk), bk)
            add = lambda tot, x: x if tot is None else tot + x
            dv_t = dk_t = dkr_t = None
            dck_t = [None] * g
            for r in range(bq // rs):
                rows = pl.ds(r * rs, rs)
                rq = pl.ds(pl.multiple_of(i * bq + r * rs, rs), rs)
                q2, do2, lse2 = q_ref[rows, :], do_ref[rows, :], lse_ref[rows, :]
                dd = do2.astype(F32) * o_ref[rows, :].astype(F32)
                dq_t = dqr_t = dcq_t = None
                for hh in range(g):
                    qm = _sel(heads[hh], q2)
                    sc = _dot(qm, k2, NT)
                    if rope:
                        qrm = _sel(ropes[hh], qr_ref[rows, :])
                        sc = sc + _dot(qrm, kr_ref[...], NT)
                    if bias:
                        sc = sc - ck_ref[0, hh:hh + 1, :]
                    if masked:
                        sc = jnp.where(_mask(mode, i * bq + r * rs, j * bk, rs, bk), sc, MASK_VALUE)
                    p = jnp.exp(sc - jnp.sum(jnp.where(lane == hh, lse2, 0.0), axis=1, keepdims=True))
                    dom = _sel(heads[hh], do2)
                    dp = _dot(dom, v2, NT)
                    delta = jnp.sum(_sel(heads[hh], dd), axis=1, keepdims=True)
                    ds = p * (dp - delta)
                    dsb = ds.astype(BF16)
                    dv_t = add(dv_t, _dot(p.astype(BF16), dom, TN))
                    dk_t = add(dk_t, _dot(dsb, qm, TN))
                    dq_t = add(dq_t, _dot(dsb, _sel(heads[hh], k2)))
                    if rope:
                        dqr_t = add(dqr_t, _dot(dsb, _sel(ropes[hh], kr_ref[...])))
                        dkr_t = add(dkr_t, _dot(dsb, qrm, TN))
                    if bias:
                        dck_t[hh] = add(dck_t[hh], jnp.sum(ds, axis=0, keepdims=True))
                        dcq_t = add(dcq_t, jnp.where(lane == hh, jnp.sum(ds, axis=1, keepdims=True), 0.0))
                dq_ref[rq, :] += dq_t if dq_scale == 1.0 else dq_scale * dq_t
                if rope:
                    dqr_ref[rq, :] += dq_scale * dqr_t
                if bias:
                    dcq_ref[rq, :] += dcq_t
            dv_s[...] += dv_t
            dk_s[...] += dk_t
            if rope:
                dkr_ref[rk, :] += dkr_t
            if bias:
                for hh in range(g):
                    dck_s[hh:hh + 1, :] -= dck_t[hh]

        if tri:
            pl.when(st.plain)(functools.partial(compute, False))
            pl.when(st.masked)(functools.partial(compute, True))
        else:
            compute(False)

        @pl.when(st.last)
        def _():
            dk_ref[...] = dk_s[...]
            dv_ref[...] = dv_s[...]
            if bias:
                dck_ref[0] = dck_s[...]

        if comm is not None:
            pl.when((b == nb - 1) & (p1 == n1 - 1) & (p2 == n2 - 1))(lambda: comm.finish(*comm_refs))

    qrow = lambda col: pl.BlockSpec((bq, LANES), lambda b, p1, p2: (step_at(p1, p2).qi, col(b)))
    krow = lambda col: pl.BlockSpec((bk, LANES), lambda b, p1, p2: (step_at(p1, p2).kj, col(b)))
    in_specs = [qrow(lambda b: qc + b), krow(lambda b: kc + b), krow(lambda b: vc + b), qrow(lambda b: b),
                qrow(lambda b: b), qrow(lambda b: b)]
    args = [q, k, v, o, do, lse]
    whole = lambda rows: pl.BlockSpec((rows, LANES), lambda b, j, i: (0, b))
    out_specs = [whole(s), krow(lambda b: b), krow(lambda b: b)]
    out_shape = [jax.ShapeDtypeStruct((s, LANES * nb), F32), jax.ShapeDtypeStruct((t, LANES * nb), F32),
                 jax.ShapeDtypeStruct((t, LANES * nb), F32)]
    scratch = [pltpu.VMEM((bk, LANES), F32), pltpu.VMEM((bk, LANES), F32)]
    if bias:
        ckj = pl.BlockSpec((1, 8, bk), lambda b, p1, p2: (b, 0, step_at(p1, p2).kj))
        in_specs.append(ckj)
        args.append(ck)
        out_specs += [ckj, whole(s)]
        out_shape += [jax.ShapeDtypeStruct((nb, 8, t), F32), jax.ShapeDtypeStruct((s, LANES * nb), F32)]
    if rope:
        in_specs += [qrow(lambda b: qrc), krow(lambda b: 0)]
        args += [qr, kr]
        out_specs += [pl.BlockSpec((s, LANES), lambda b, j, i: (0, 0)), pl.BlockSpec((t, LANES), lambda b, j, i: (0, 0))]
        out_shape += [jax.ShapeDtypeStruct((s, LANES), F32), jax.ShapeDtypeStruct((t, LANES), F32)]
    if bias:
        scratch.append(pltpu.VMEM((8, bk), F32))
    if comm is not None:
        in_specs += [ANY] * len(comm.ins)
        args += comm.ins
        out_specs += [ANY] * len(comm.out_shapes)
        out_shape += comm.out_shapes
        scratch += _sems(comm.n_sems, comm.n_sems)
    res = pl.pallas_call(body, name=name, grid=(nb, n1, n2), in_specs=in_specs, out_specs=out_specs,
                         out_shape=out_shape, scratch_shapes=scratch,
                         compiler_params=_params('arbitrary', 'arbitrary', 'arbitrary'))(*args)
    return res if comm is None else (*res[:n_out], res[n_out:])


def _gla_chunk(la_c, k_c):
    r = lax.broadcasted_iota(jnp.int32, (CHUNK, CHUNK), 0)
    c = lax.broadcasted_iota(jnp.int32, (CHUNK, CHUNK), 1)
    tri = jnp.where(c <= r, 1.0, 0.0).astype(BF16)
    cum = _tri_dot(tri, la_c)
    end = jnp.sum(la_c, axis=0, keepdims=True)
    dec = jnp.exp(end - cum)
    return dec, k_c * dec, jnp.exp(end)


GLA_PAIRS = GLA_HEADS // 2


def _gla_fwd(z, la, *, qc, kc, vc, name, blk=512):
    s = z.shape[0]
    bs = min(blk, s)
    ncb = bs // CHUNK
    nblk = s // bs

    def body(q_ref, k_ref, va_ref, vb_ref, la_ref, o_ref, st_ref, st):
        @pl.when(pl.program_id(1) == 0)
        def _():
            st[...] = jnp.zeros_like(st)

        heads, _ = _lane_masks(2, 0, False)
        v_refs = (va_ref, vb_ref)
        for c in range(ncb):
            sl = pl.ds(c * CHUNK, CHUNK)
            _, kf, a = _gla_chunk(la_ref[sl, :], k_ref[sl, :])
            qs = q_ref[sl, :] * (GLA_DK ** -0.5)
            for hh in range(2):
                ut = _dot(v_refs[hh][sl, :].astype(BF16), _sel(heads[hh], kf).astype(BF16), TN)
                new = a * st[hh] + ut
                st[hh] = new
                st_ref[0, c, hh] = new
                o_ref[sl, hh * GLA_DV:(hh + 1) * GLA_DV] = _dot(_sel(heads[hh], qs).astype(BF16), new.astype(BF16), NT)

    col = lambda c0, m=1: pl.BlockSpec((bs, LANES), lambda b, i: (i, c0 + m * b))
    return pl.pallas_call(
        body, name=name, grid=(GLA_PAIRS, nblk),
        in_specs=[col(qc), col(kc), col(vc, 2), col(vc + 1, 2), col(0)],
        out_specs=[pl.BlockSpec((bs, 2 * GLA_DV), lambda b, i: (i, b)),
                   pl.BlockSpec((1, ncb, 2, GLA_DV, LANES), lambda b, i: (b, i, 0, 0, 0))],
        out_shape=[jax.ShapeDtypeStruct((s, GLA_HEADS * GLA_DV), F32),
                   jax.ShapeDtypeStruct((GLA_PAIRS, s // CHUNK, 2, GLA_DV, LANES), F32)],
        scratch_shapes=[pltpu.VMEM((2, GLA_DV, LANES), F32)],
        compiler_params=_params('arbitrary', 'arbitrary'))(z, z, z, z, la)


def _gla_bwd(z, la, st_all, st_prev, do, *, qc, kc, vc, name, blk=512):
    s = z.shape[0]
    bs = min(blk, s)
    ncb = bs // CHUNK
    nblk = s // bs

    def body(q_ref, k_ref, va_ref, vb_ref, la_ref, st_ref, sp_ref, do_ref, dq_ref, dk_ref, dv_ref, dla_ref, ga):
        @pl.when(pl.program_id(1) == 0)
        def _():
            ga[...] = jnp.zeros_like(ga)

        r = lax.broadcasted_iota(jnp.int32, (CHUNK, CHUNK), 0)
        cc = lax.broadcasted_iota(jnp.int32, (CHUNK, CHUNK), 1)
        tri_rev = jnp.where(cc >= r, 1.0, 0.0).astype(BF16)
        heads, _ = _lane_masks(2, 0, False)
        v_refs = (va_ref, vb_ref)
        for c in reversed(range(ncb)):
            sl = pl.ds(c * CHUNK, CHUNK)
            dec, kf, a = _gla_chunk(la_ref[sl, :], k_ref[sl, :])
            qs = q_ref[sl, :] * (GLA_DK ** -0.5)
            dq2 = jnp.zeros((CHUNK, LANES), F32)
            dkd = jnp.zeros((CHUNK, LANES), F32)
            da = jnp.zeros((1, LANES), F32)
            for hh in range(2):
                hv = slice(hh * GLA_DV, (hh + 1) * GLA_DV)
                dob = do_ref[sl, hv].astype(BF16)
                g = _dot(dob, _sel(heads[hh], qs).astype(BF16), TN) + ga[hh]
                gb = g.astype(BF16)
                dq2 = dq2 + _dot(dob, st_ref[0, c, hh].astype(BF16))
                dv_ref[sl, hv] = _dot(_sel(heads[hh], kf).astype(BF16), gb, NT)
                dkd = dkd + _dot(v_refs[hh][sl, :].astype(BF16), gb)
                da = da + jnp.sum(g * sp_ref[0, c, hh], axis=0, keepdims=True)
                ga[hh] = a * g
            dq_ref[sl, :] = (GLA_DK ** -0.5) * dq2
            dk_ref[sl, :] = dkd * dec
            e = dkd * kf
            dend = jnp.sum(e, axis=0, keepdims=True) + da * a
            dla_ref[sl, :] = dend - _tri_dot(tri_rev, e)

    rev = lambda i: nblk - 1 - i
    col = lambda c0, m=1: pl.BlockSpec((bs, LANES), lambda b, i: (rev(i), c0 + m * b))
    wide = pl.BlockSpec((bs, 2 * GLA_DV), lambda b, i: (rev(i), b))
    stspec = pl.BlockSpec((1, ncb, 2, GLA_DV, LANES), lambda b, i: (b, rev(i), 0, 0, 0))
    return pl.pallas_call(
        body, name=name, grid=(GLA_PAIRS, nblk),
        in_specs=[col(qc), col(kc), col(vc, 2), col(vc + 1, 2), col(0), stspec, stspec, wide],
        out_specs=[col(0), col(0), wide, col(0)],
        out_shape=[jax.ShapeDtypeStruct((s, GLA_HEADS * GLA_DK), F32), jax.ShapeDtypeStruct((s, GLA_HEADS * GLA_DK), F32),
                   jax.ShapeDtypeStruct((s, GLA_HEADS * GLA_DV), F32), jax.ShapeDtypeStruct((s, GLA_HEADS * GLA_DK), F32)],
        scratch_shapes=[pltpu.VMEM((2, GLA_DV, LANES), F32)],
        compiler_params=_params('arbitrary', 'arbitrary'))(z, z, z, z, la, st_all, st_prev, do)


def _place():
    return lax.axis_index('x'), lax.axis_index('y'), lax.axis_index('c')


ANY = pl.BlockSpec(memory_space=pl.ANY)


def _all_gather8(blk, *, name):
    m, n = blk.shape

    def body(x_ref, out_ref, send_sems, recv_sems, local_sem):
        x, y, c = _place()
        me, sibling = (x, y, c), (x, y, 1 - c)
        chips = [(1 - x, y), (x, 1 - y), (1 - x, 1 - y)]

        def slot(px, py, pc):
            return out_ref.at[4 * px + 2 * py + pc]

        def copy(q, block, to, src=None):
            return pltpu.make_async_remote_copy(
                src_ref=slot(*block) if src is None else src, dst_ref=slot(*block), send_sem=send_sems.at[q],
                recv_sem=recv_sems.at[q], device_id=to, device_id_type=MESH)

        mine = pltpu.make_async_copy(x_ref, slot(*me), local_sem)
        mine.start()
        first = [copy(0, me, sibling, src=x_ref)]
        first += [copy(1 + q, me, (*chip, c), src=x_ref) for q, chip in enumerate(chips)]
        for cp in first:
            cp.start()
        passed = [copy(4 + q, (*chip, c), sibling) for q, chip in enumerate(chips)]
        for q, chip in enumerate(chips):
            copy(1 + q, (*chip, c), me).wait_recv()
            passed[q].start()
        copy(0, sibling, me).wait_recv()
        for q, chip in enumerate(chips):
            copy(4 + q, (*chip, 1 - c), me).wait_recv()
        for cp in first + passed:
            cp.wait_send()
        mine.wait()

    return pl.pallas_call(
        body, name=name, in_specs=[ANY], out_specs=ANY, out_shape=jax.ShapeDtypeStruct((N_DEV, m, n), blk.dtype),
        scratch_shapes=[pltpu.SemaphoreType.DMA((7,)), pltpu.SemaphoreType.DMA((7,)), pltpu.SemaphoreType.DMA(())],
    )(blk)


def _sems(*counts):
    return [pltpu.SemaphoreType.DMA((n,)) for n in counts]


class Comm(typing.NamedTuple):
    ins: list
    out_shapes: list
    n_sems: int
    start: typing.Callable
    finish: typing.Callable


def _remote(src, dst, send_sems, recv_sems, idx, to):
    return lambda: pltpu.make_async_remote_copy(src_ref=src, dst_ref=dst, send_sem=send_sems.at[idx],
                                                recv_sem=recv_sems.at[idx], device_id=to, device_id_type=MESH)


def _comm_from(copies, ins, out_shapes, n_sems):
    def start(*refs):
        for cp in copies(*refs)[0]:
            cp().start()

    def finish(*refs):
        sent, received = copies(*refs)
        for cp in received:
            cp().wait_recv()
        for cp in sent:
            cp().wait_send()

    return Comm(list(ins), list(out_shapes), n_sems, start, finish)


def _run_comm(comm, *, name, alias=False):
    n_in, n_out = len(comm.ins), len(comm.out_shapes)

    def body(*refs):
        ins, outs, sems = refs[:n_in], refs[n_in:n_in + n_out], refs[n_in + n_out:]
        comm.start(ins, outs, *sems)
        comm.finish(ins, outs, *sems)

    return pl.pallas_call(body, name=name, in_specs=[ANY] * n_in, out_specs=[ANY] * n_out, out_shape=comm.out_shapes,
                          input_output_aliases={q: q for q in range(n_in)} if alias else {},
                          scratch_shapes=_sems(comm.n_sems, comm.n_sems))(*comm.ins)


def _half(rows, c):
    h = rows // 2
    return pl.ds(pl.multiple_of(c * h, h), h)


def _gathered(ref, chip, rows, side):
    if not side:
        return ref.at[chip, rows]
    n = ref.shape[1] // N_CHIPS
    return ref.at[rows, pl.ds(pl.multiple_of(chip * n, n), n)]


def _gather_over_ici(ws, side):
    def copies(ins, outs, send_sems, recv_sems):
        x, y, c = _place()
        me_chip = 2 * x + y
        sent, received = [], []
        for q, w in enumerate(ws):
            half, every = _half(w.shape[0], c), pl.ds(0, w.shape[0])
            for k, (px, py) in enumerate([(1 - x, y), (x, 1 - y), (1 - x, 1 - y)]):
                sent.append(_remote(ins[q].at[half], _gathered(outs[q], me_chip, half, side[q]), send_sems, recv_sems,
                                    4 * q + k, (px, py, c)))
                slot = _gathered(outs[q], 2 * px + py, half, side[q])
                received.append(_remote(slot, slot, send_sems, recv_sems, 4 * q + k, (px, py, c)))
            whole = _remote(ins[q], _gathered(outs[q], me_chip, every, side[q]), send_sems, recv_sems, 4 * q + 3,
                            (x, y, 1 - c))
            sent.append(whole)
            received.append(whole)
        return sent, received

    shapes = [jax.ShapeDtypeStruct((w.shape[0], N_CHIPS * w.shape[1]) if sd else (N_CHIPS,) + w.shape, w.dtype)
              for w, sd in zip(ws, side)]
    return _comm_from(copies, ws, shapes, 4 * len(ws))


def _gather_over_d2d(parts, side):
    def copies(ins, outs, send_sems, recv_sems):
        x, y, c = _place()
        sent, received = [], []
        for q, w in enumerate(parts):
            rows = w.shape[0] if side[q] else w.shape[1]
            for k, (px, py) in enumerate([(1 - x, y), (x, 1 - y), (1 - x, 1 - y)]):
                mine = _gathered(outs[q], 2 * px + py, _half(rows, c), side[q])
                theirs = _gathered(outs[q], 2 * px + py, _half(rows, 1 - c), side[q])
                sent.append(_remote(mine, mine, send_sems, recv_sems, 3 * q + k, (x, y, 1 - c)))
                received.append(_remote(theirs, theirs, send_sems, recv_sems, 3 * q + k, (x, y, 1 - c)))
        return sent, received

    return _comm_from(copies, parts, [jax.ShapeDtypeStruct(w.shape, w.dtype) for w in parts], 3 * len(parts))


def _to_sibling(gs, *, name):
    n = len(gs)

    def body(*refs):
        ins, outs = refs[:n], refs[n:2 * n]
        send_sems, recv_sems = refs[2 * n:]
        x, y, c = _place()
        cps = [pltpu.make_async_remote_copy(
            src_ref=ins[q], dst_ref=outs[q], send_sem=send_sems.at[q], recv_sem=recv_sems.at[q],
            device_id=(x, y, 1 - c), device_id_type=MESH) for q in range(n)]
        for cp in cps:
            cp.start()
        for cp in cps:
            cp.wait()

    return pl.pallas_call(body, name=name, in_specs=[ANY] * n, out_specs=[ANY] * n,
                          out_shape=[jax.ShapeDtypeStruct(g.shape, g.dtype) for g in gs],
                          scratch_shapes=_sems(n, n))(*gs)


def _chip_exchange(ps):
    def copies(ins, outs, send_sems, recv_sems):
        x, y, c = _place()
        cps = [_remote(ins[q].at[2 * px + py], outs[q].at[k], send_sems, recv_sems, 3 * q + k, (px, py, c))
               for q in range(len(ps)) for k, (px, py) in enumerate([(1 - x, y), (x, 1 - y), (1 - x, 1 - y)])]
        return cps, cps

    return _comm_from(copies, ps, [jax.ShapeDtypeStruct((3,) + p.shape[1:], p.dtype) for p in ps], 3 * len(ps))


def _sum_chips(own, r, *, name, ts=256):
    k, n = own.shape
    ts = min(ts, k)

    def body(own_ref, r_ref, o_ref):
        f = lambda q: r_ref[q].astype(F32)
        o_ref[...] = ((own_ref[...].astype(F32) + f(0)) + f(1)) + f(2)

    return pl.pallas_call(
        body, name=name, grid=(k // ts,),
        in_specs=[pl.BlockSpec((ts, n), lambda i: (i, 0)), pl.BlockSpec((3, ts, n), lambda i: (0, i, 0))],
        out_specs=pl.BlockSpec((ts, n), lambda i: (i, 0)), out_shape=jax.ShapeDtypeStruct((k, n), F32),
        compiler_params=_params('arbitrary'))(own, r)


WIN_SHARD = N_IN // N_CHIPS
WIN_PAD = -(-WIN_SHARD // LANES) * LANES
GATE_WIRE_ROWS = 32


def _full_layer(sh, axis):
    _, k, n = sh.shape
    if axis == 2:
        return sh.transpose(1, 0, 2).reshape(k, N_CHIPS * n)
    return sh.reshape(N_CHIPS * k, n)


def _win_cols(wp, o, n):
    parts = []
    while n > 0:
        j, r = divmod(o, WIN_SHARD)
        take = min(n, WIN_SHARD - r)
        parts.append(wp[:, j * WIN_PAD + r:j * WIN_PAD + r + take])
        o, n = o + take, n - take
    return parts[0] if len(parts) == 1 else jnp.concatenate(parts, axis=1)


def _split_full(full, axis):
    if full.ndim == 3:
        return full
    k, n = full.shape
    if axis == 2:
        return jnp.stack([full[:, j * (n // N_CHIPS):(j + 1) * (n // N_CHIPS)] for j in range(N_CHIPS)])
    return full.reshape(N_CHIPS, k // N_CHIPS, n)


def _padc(a, w):
    return jnp.pad(a, ((0, 0), (0, w - a.shape[1])))


def _swap16(a):
    return jnp.concatenate([a[..., 16:32], a[..., 0:16]], axis=-1)


B_GR, B_GQ, B_GK, B_GV, B_MQ, B_MKR, B_MKRS, B_FF, B_GLOW, B_MKV, B_END = (
    0, 512, 768, 1024, 1536, 1792, 1920, 2048, 2176, 2304, 2432)
B_W = 2560
O_FQ, O_FF, O_GQ, O_GLOW, O_GR, O_MQ, O_MKV, O_MKR, O_ZG = 0, 768, 772, 1796, 1812, 2324, 2580, 2708, 2740


def _repack_layer_weights(w):
    wi = functools.partial(_win_cols, w['w_in'])
    out = dict(w)
    out['in_a'] = jnp.concatenate([wi(O_FQ, 256) * FOX_SCALE, wi(O_FQ + 256, 512)], axis=1)
    kr = wi(O_MKR, 32)
    out['in_b'] = jnp.concatenate([
        wi(O_GR, 512), wi(O_GQ, 1024), wi(O_MQ, 256), jnp.tile(kr, (1, MLA_HEADS)), jnp.tile(_swap16(kr), (1, MLA_HEADS)),
        _padc(wi(O_FF, 4), 128), _padc(wi(O_GLOW, 16), 128), wi(O_MKV, 128),
        jnp.zeros((D_MODEL, B_W - B_END), kr.dtype)], axis=1)
    out['in_c'] = wi(O_ZG, 3072)
    uq = w['w_mla_uq'].reshape(MLA_Q_RANK, MLA_HEADS, MLA_NOPE + MLA_ROPE)
    rope = uq[:, :, MLA_NOPE:]
    out['uq'] = jnp.concatenate([uq[:, :, :MLA_NOPE].reshape(MLA_Q_RANK, -1), rope.reshape(MLA_Q_RANK, -1),
                                 _swap16(rope).reshape(MLA_Q_RANK, -1)], axis=1)
    ukv = w['w_mla_ukv'].reshape(MLA_KV_RANK, MLA_HEADS, MLA_NOPE + MLA_VD)
    out['ukv'] = jnp.concatenate([ukv[:, :, :MLA_NOPE].reshape(MLA_KV_RANK, -1),
                                  ukv[:, :, MLA_NOPE:].reshape(MLA_KV_RANK, -1)], axis=1)
    out['gate'] = jnp.pad(w['w_gla_gate'], ((0, 128 - GLA_RANK), (0, 0)))
    return out


def _unpack_layer_grads(g):
    a, b, c = g['in_a'], g['in_b'], g['in_c']
    fold = lambda o: sum(b[:, o + MLA_ROPE * q:o + MLA_ROPE * (q + 1)] for q in range(MLA_HEADS))
    kr = fold(B_MKR) + _swap16(fold(B_MKRS))
    pieces = [(a[:, :256] * FOX_SCALE, 0, 256), (a, 256, 512), (b, B_FF, 4), (b, B_GQ, 1024), (b, B_GLOW, 16),
              (b, B_GR, 512), (b, B_MQ, 256), (b, B_MKV, 128), (kr, 0, 32), (c, 0, 3072)]
    shards = []
    for j in range(N_CHIPS):
        lo, hi, cut, at = j * WIN_SHARD, (j + 1) * WIN_SHARD, [], 0
        for arr, first, width in pieces:
            l, h = max(lo, at), min(hi, at + width)
            if l < h:
                cut.append(arr[:, first + l - at:first + h - at])
            at += width
        shards.append(jnp.concatenate(cut, axis=1))
    w_in = jnp.stack(shards)
    uq = g['uq']
    nope = uq[:, :256].reshape(MLA_Q_RANK, MLA_HEADS, MLA_NOPE)
    rope = (uq[:, 256:384].reshape(MLA_Q_RANK, MLA_HEADS, MLA_ROPE)
            + _swap16(uq[:, 384:512].reshape(MLA_Q_RANK, MLA_HEADS, MLA_ROPE)))
    w_uq = jnp.concatenate([nope, rope], axis=2).reshape(MLA_Q_RANK, -1)
    ukv = g['ukv']
    w_ukv = jnp.concatenate([ukv[:, :256].reshape(MLA_KV_RANK, MLA_HEADS, MLA_NOPE),
                             ukv[:, 256:].reshape(MLA_KV_RANK, MLA_HEADS, MLA_VD)], axis=2).reshape(MLA_KV_RANK, -1)
    out = {'w_in': w_in, 'w_mla_uq': w_uq, 'w_mla_ukv': w_ukv, 'w_gla_gate': g['gate'][:GLA_RANK]}
    for nm in ('w_up_fox', 'w_up_gla', 'w_up_mla', 'w_out', 'w_xq', 'w_xkv', 'w_xo', 'w_mlp1', 'w_mlp2'):
        out[nm] = g[nm]
    return out


def _rope_tables(s):
    half = MLA_ROPE // 2
    inv = ROPE_BASE ** (-jnp.arange(half, dtype=F32) / half)
    ang = jnp.arange(s).astype(F32)[:, None] * inv[None, :]
    cos, sin = jnp.cos(ang), jnp.sin(ang)
    c1 = jnp.concatenate([cos, cos], axis=1)
    s1 = jnp.concatenate([-sin, sin], axis=1)
    return jnp.tile(c1, (1, MLA_HEADS)), jnp.tile(s1, (1, MLA_HEADS))


def _rms_bwd(x, dh, g):
    r = lax.rsqrt(jnp.mean(x * x, axis=-1, keepdims=True) + EPS)
    xh = x * r
    gd = dh * g
    return r * (gd - xh * jnp.mean(gd * xh, axis=-1, keepdims=True)), dh * xh


def _norm_bwd_epilogue(dh, x, dres, g):
    dx, dg = _rms_bwd(x, dh, g)
    return dres + dx, dg


def _norm_bwd_call(x, dh, g, dres, name):
    w = x.width if isinstance(x, Cols) else x.shape[1]

    def with_res(xv, dv, rv, gv):
        dx, dg = _rms_bwd(xv, dv.astype(F32), gv)
        return rv + dx, dg

    def plain(xv, dv, gv):
        return _rms_bwd(xv, dv.astype(F32), gv)

    if dres is None:
        return _rowwise(plain, [x, dh], [g], [(w, F32)], [w], name=name)
    return _rowwise(with_res, [x, dh, dres], [g], [(w, F32)], [w], name=name)


def _gla_out_fwd(oraw, gr, g_out):
    outs = []
    for hh in range(GLA_HEADS):
        sl = slice(hh * GLA_DV, (hh + 1) * GLA_DV)
        oh = oraw[:, sl]
        n = oh * lax.rsqrt(jnp.mean(oh * oh, axis=-1, keepdims=True) + EPS) * g_out
        r = gr[:, sl]
        outs.append(n * (r * _sig(r)))
    return (jnp.concatenate(outs, axis=1),)


def _gla_out_bwd(oraw, gr, dout, g_out):
    d_o, d_r, dg = [], [], 0.0
    for hh in range(GLA_HEADS):
        sl = slice(hh * GLA_DV, (hh + 1) * GLA_DV)
        oh, r, do = oraw[:, sl], gr[:, sl], dout[:, sl].astype(F32)
        rs = lax.rsqrt(jnp.mean(oh * oh, axis=-1, keepdims=True) + EPS)
        sg = _sig(r)
        dn = do * (r * sg)
        d_r.append(do * (oh * rs * g_out) * (sg + r * sg * (1.0 - sg)))
        dx, dgh = _rms_bwd(oh, dn, g_out)
        d_o.append(dx)
        dg = dg + dgh
    return jnp.concatenate(d_o, axis=1), jnp.concatenate(d_r, axis=1), dg


def _adam(w, g, m, v):
    m = ADAM_B1 * m + (1.0 - ADAM_B1) * g
    v = ADAM_B2 * v + (1.0 - ADAM_B2) * (g * g)
    m_hat = m / (1.0 - ADAM_B1 ** ADAM_STEP)
    v_hat = v / (1.0 - ADAM_B2 ** ADAM_STEP)
    return -ADAM_LR * (m_hat / (jnp.sqrt(v_hat) + ADAM_EPS) + ADAM_WD * w), m, v


def _layer_fwd(x, mem, w, p, tabs, tag, carry_fox=None, after_fox=None, carry_mla=None):
    c4, s4 = tabs
    sv = {'x0': x}
    nm = lambda t: f'{t}_{tag}'
    za, h = _mm(x, w['in_a'], mode='nn', out_dtype=BF16, norm_g=p['g_mix'], emit_norm=True, name=nm('in_a'))
    zb = _mm(h, w['in_b'], mode='nn', out_dtype=F32, name=nm('in_b'))
    zc = _mm(h, w['in_c'], mode='nn', out_dtype=F32, name=nm('in_c'))
    sv.update(h=h, zc=zc)
    ff = Cols(zb, 128, B_FF // 128)
    (lf,) = _rowwise(lambda f, b: (_logsig(f + b),), [ff], [p['b_fox']], [(128, F32)], name=nm('fox_lf'))
    cum = _cumsum_rows(lf, reverse=False, name=nm('fox_cum'))
    ckf = jnp.pad(cum[:, :FOX_HEADS].T.reshape(2, 2, x.shape[0]), ((0, 0), (0, 6), (0, 0)))
    fox = dict(qc=0, kc=2, vc=4, nb=2, g=2, mode='causal', ck=ckf)
    o_fox, lse_fox, *carried = _mattn_fwd(za, za, za, name=nm('fox_attn'), comm=carry_fox, **fox)
    if after_fox is not None:
        w = {**w, **after_fox(carried[0])}
    sv.update(ff=ff, za=za, fox=fox, o_fox=o_fox, lse_fox=lse_fox)
    glow = Cols(zb, 128, B_GLOW // 128)
    gr = Cols(zb, 512, B_GR // 512)

    def gate_fn(gl, wg, bg):
        return (_logsig(_dot(gl.astype(BF16), wg) + bg) / GLA_TAU,)

    (la,) = _rowwise(gate_fn, [glow], [w['gate'], p['b_gla']], [(256, F32)], name=nm('gla_gate'))
    gla = dict(qc=B_GQ // LANES, kc=B_GK // LANES, vc=B_GV // LANES)
    oraw, states = _gla_fwd(zb, la, name=nm('gla'), **gla)
    (o_gla,) = _rowwise(_gla_out_fwd, [oraw, gr], [p['g_gla_out']], [(512, BF16)], name=nm('gla_out'))
    sv.update(glow=glow, gr=gr, zb=zb, la=la, gla=gla, states=states, oraw=oraw, o_gla=o_gla)
    mq = Cols(zb, 256, B_MQ // 256)
    mkv = Cols(zb, 128, B_MKV // 128)
    mkr2 = Cols(zb, 256, B_MKR // 256)
    qp, cqn = _mm(mq, w['uq'], mode='nn', out_dtype=F32, norm_g=p['g_mla_q'], emit_norm=True, name=nm('mla_uq'))
    kvp, ckvn = _mm(mkv, w['ukv'], mode='nn', out_dtype=BF16, norm_g=p['g_mla_kv'], emit_norm=True,
                    name=nm('mla_ukv'))

    def rope_fn(qv, kr, c4v, s4v):
        q_rope = qv[:, 256:384] * c4v + qv[:, 384:512] * s4v
        q_scaled = jnp.concatenate([qv[:, 0:256], q_rope], axis=1) * MLA_SCALE
        return q_scaled, kr[:, 0:128] * c4v + kr[:, 128:256] * s4v

    qall, kr4 = _rowwise(rope_fn, [qp, mkr2, c4, s4], [], [(384, BF16), (128, BF16)], name=nm('rope'))
    mla = dict(qc=0, kc=0, vc=2, nb=2, g=2, dq_scale=MLA_SCALE, mode='chunk', qr=qall, qrc=2, kr=kr4)
    o_mla, lse_mla, *carried = _mattn_fwd(qall, kvp, kvp, name=nm('mla_attn'), comm=carry_mla, **mla)
    if carry_mla is not None:
        sv['carried_mla'] = carried[0]
    sv.update(mq=mq, mkv=mkv, cqn=cqn, ckvn=ckvn, qall=qall, kvp=kvp, mla=mla, o_mla=o_mla, lse_mla=lse_mla)
    of_m, om_m = o_fox, o_mla
    sv.update(of_m=of_m, om_m=om_m)
    b_br = p['b_branch']

    y = _gated_merge([of_m, o_gla, om_m], [w['w_up_fox'], w['w_up_gla'], w['w_up_mla']], zc, b_br, name=nm('up_merge'))
    add = lambda acc, res: res + acc
    x1 = _mm(y, w['w_out'], mode='nn', out_dtype=F32, name=nm('out'), epilogue=add, extras=[(x, *_mn())])
    sv.update(y=y, x1=x1)
    qx, hx = _mm(x1, w['w_xq'], mode='nn', out_dtype=BF16, norm_g=p['g_xa'], emit_norm=True, name=nm('xq'),
                 epilogue=lambda acc: acc * XA_SCALE)
    kvx, mn = _mm(mem, w['w_xkv'], mode='nn', out_dtype=BF16, norm_g=p['g_mem'], emit_norm=True, name=nm('xkv'))
    xa = dict(qc=0, kc=0, vc=4, nb=4, g=1, dq_scale=XA_SCALE, mode='full')
    ox_m, lse_x = _mattn_fwd(qx, kvx, kvx, name=nm('xa_attn'), **xa)
    x2 = _mm(ox_m, w['w_xo'], mode='nn', out_dtype=F32, name=nm('xo'), epilogue=add, extras=[(x1, *_mn())])
    sv.update(hx=hx, mn=mn, qx=qx, kvx=kvx, xa=xa, lse_x=lse_x, ox_m=ox_m, x2=x2)
    hpre, hm = _mm(x2, w['w_mlp1'], mode='nn', out_dtype=BF16, norm_g=p['g_mlp'], emit_norm=True, name=nm('mlp1'))
    relu2 = lambda t: jnp.square(jnp.maximum(t.astype(F32), 0.0))
    x3 = _mm(hpre, w['w_mlp2'], mode='nn', out_dtype=F32, name=nm('mlp2'), a_fn=relu2, epilogue=add,
             extras=[(x2, *_mn())])
    sv.update(hpre=hpre, hm=hm, w=w)
    return x3, sv


EARLY = ('w_mlp1', 'w_mlp2', 'w_xo', 'w_xq', 'w_xkv', 'w_out', 'w_up_fox', 'w_up_gla', 'w_up_mla')
LATE = ('w_in', 'w_gla_gate', 'w_mla_uq', 'w_mla_ukv')


def _layer_bwd(dx3, mem, w, p, tabs, sv, tag, carry_mla=None, early=None):
    c4, s4 = tabs
    nm = lambda t: f'{t}_{tag}'
    s = dx3.shape[0]
    gw, gs = {}, {}
    relu2 = lambda t: jnp.square(jnp.maximum(t.astype(F32), 0.0))
    gw['w_mlp2'] = _mm(sv['hpre'], dx3, mode='tn', out_dtype=F32, name=nm('d_mlp2'), a_fn=relu2)
    dact = lambda acc, hp: acc * (2.0 * jnp.maximum(hp.astype(F32), 0.0))
    dhpre = _mm(dx3, w['w_mlp2'], mode='nt', out_dtype=BF16, name=nm('d_act'), epilogue=dact,
                extras=[(sv['hpre'], *_mn())])
    gw['w_mlp1'] = _mm(sv['hm'], dhpre, mode='tn', out_dtype=F32, name=nm('d_mlp1'))
    dx2, gs['g_mlp'] = _mm(dhpre, w['w_mlp1'], mode='nt', out_dtype=F32, name=nm('d_hm'), epilogue=_norm_bwd_epilogue,
                           col_sums=True, full_rows=True,
                           extras=[(sv['x2'], *_mn()), (dx3, *_mn()), (p['g_mlp'], *_nvec())])
    gw['w_xo'] = _mm(sv['ox_m'], dx2, mode='tn', out_dtype=F32, name=nm('d_xo'))
    dox = _mm(dx2, w['w_xo'], mode='nt', out_dtype=BF16, name=nm('d_ox'))
    dqx_m, dkx, dvx = _mattn_bwd(sv['qx'], sv['kvx'], sv['kvx'], sv['ox_m'], dox, sv['lse_x'], name=nm('xa_bwd'),
                                 **sv['xa'])
    dkvx = jnp.concatenate([dkx, dvx], axis=1).astype(BF16)
    gw['w_xq'] = _mm(sv['hx'], dqx_m, mode='tn', out_dtype=F32, name=nm('d_xq'))
    dx1, gs['g_xa'] = _mm(dqx_m, w['w_xq'], mode='nt', out_dtype=F32, name=nm('d_hx'), epilogue=_norm_bwd_epilogue,
                          col_sums=True, full_rows=True,
                          extras=[(sv['x1'], *_mn()), (dx2, *_mn()), (p['g_xa'], *_nvec())])
    gw['w_xkv'] = _mm(sv['mn'], dkvx, mode='tn', out_dtype=F32, name=nm('d_xkv'))
    dmn = _mm(dkvx, w['w_xkv'], mode='nt', out_dtype=F32, name=nm('d_mn'))
    _, gs['g_mem'] = _norm_bwd_call(mem, dmn, p['g_mem'], None, nm('d_norm_mem'))
    gw['w_out'] = _mm(sv['y'], dx1, mode='tn', out_dtype=F32, name=nm('d_out'))
    dy = _mm(dx1, w['w_out'], mode='nt', out_dtype=BF16, name=nm('d_y'))
    zc, b_br = sv['zc'], p['b_branch']

    branches = (('w_up_fox', sv['of_m'], BF16), ('w_up_gla', sv['o_gla'], F32), ('w_up_mla', sv['om_m'], BF16))
    du, do_br, dzc, gs['b_branch'] = _gated_merge_bwd(dy, zc, b_br, [o for _, o, _ in branches],
                                                      [w[wn] for wn, _, _ in branches], [dt for _, _, dt in branches],
                                                      name=nm('d_merge'))
    for q, (wn, o_m, _) in enumerate(branches):
        gw[wn] = _mm(o_m, du[q], mode='tn', out_dtype=F32, name=nm(f'd_up{q}'))
    za = sv['za']
    carry_fox = None if early is None else early({nm_: gw[nm_] for nm_ in EARLY})
    dfq, dfk, dfv, dck, dcq, *carried_fox = _mattn_bwd(za, za, za, sv['o_fox'], do_br[0], sv['lse_fox'],
                                                       name=nm('fox_bwd'), comm=carry_fox, **sv['fox'])
    dcum = _padc(dck[:, :2, :].reshape(FOX_HEADS, s).T + dcq.reshape(s, 2, LANES)[:, :, :2].reshape(s, FOX_HEADS), 128)
    dlf = _cumsum_rows(dcum, reverse=True, name=nm('fox_dcum'))

    def dff_fn(dl, f, b):
        d = dl * _sig(-(f + b))
        return d, d

    dff, db_fox = _rowwise(dff_fn, [dlf, sv['ff']], [p['b_fox']], [(128, F32)], [128], name=nm('fox_dff'))
    gs['b_fox'] = db_fox
    dza = jnp.concatenate([dfq, dfk, dfv], axis=1).astype(BF16)
    dqn, dkn, dvv, dq_rope, dk_rope, *carried_mla = _mattn_bwd(sv['qall'], sv['kvp'], sv['kvp'], sv['o_mla'], do_br[2],
                                                               sv['lse_mla'], name=nm('mla_bwd'), comm=carry_mla,
                                                               **sv['mla'])

    def drope_fn(dn, dq, dk, c4v, s4v):
        return jnp.concatenate([dn, dq * c4v, dq * s4v], axis=1), jnp.concatenate([dk * c4v, dk * s4v], axis=1)

    dqp, dmkr2 = _rowwise(drope_fn, [dqn, dq_rope, dk_rope, c4, s4], [], [(512, BF16), (256, BF16)], name=nm('d_rope'))
    dkvp = jnp.concatenate([dkn, dvv], axis=1).astype(BF16)
    gw['uq'] = _mm(sv['cqn'], dqp, mode='tn', out_dtype=F32, name=nm('d_uq'))
    dcqn = _mm(dqp, w['uq'], mode='nt', out_dtype=F32, name=nm('d_cqn'))
    gw['ukv'] = _mm(sv['ckvn'], dkvp, mode='tn', out_dtype=F32, name=nm('d_ukv'))
    dckvn = _mm(dkvp, w['ukv'], mode='nt', out_dtype=F32, name=nm('d_ckvn'))
    dmq, gs['g_mla_q'] = _norm_bwd_call(sv['mq'], dcqn, p['g_mla_q'], None, nm('d_norm_q'))
    dmkv, gs['g_mla_kv'] = _norm_bwd_call(sv['mkv'], dckvn, p['g_mla_kv'], None, nm('d_norm_kv'))
    doraw, dgr, gs['g_gla_out'] = _rowwise(_gla_out_bwd, [sv['oraw'], sv['gr'], do_br[1]], [p['g_gla_out']],
                                           [(512, F32), (512, BF16)], [128], name=nm('d_gla_out'))
    st = sv['states']
    st_prev = jnp.concatenate([jnp.zeros_like(st[:, :1]), st[:, :-1]], axis=1)
    dgq, dgk, dgv, dla = _gla_bwd(sv['zb'], sv['la'], st, st_prev, doraw, name=nm('gla_bwd'), **sv['gla'])

    def dgate_fn(dl, gl, wg, bg):
        pre = _dot(gl.astype(BF16), wg) + bg
        dpre = dl * (1.0 / GLA_TAU) * _sig(-pre)
        return dpre, _dot(dpre.astype(BF16), wg, NT), dpre

    dpre, dglow, gs['b_gla'] = _rowwise(dgate_fn, [dla, sv['glow']], [w['gate'], p['b_gla']],
                                        [(256, BF16), (128, BF16)], [256], name=nm('d_gla_gate'))
    gw['gate'] = _mm(sv['glow'], dpre, mode='tn', out_dtype=F32, name=nm('d_wgate'))
    bf = lambda t: t.astype(BF16)
    dzb = jnp.concatenate([dgr, bf(dgq), bf(dgk), bf(dgv), bf(dmq), dmkr2, bf(dff), dglow, bf(dmkv),
                           jnp.zeros((s, B_W - B_END), BF16)], axis=1)
    h = sv['h']
    gw['in_a'] = _mm(h, dza, mode='tn', out_dtype=F32, name=nm('d_in_a'))
    gw['in_b'] = _mm(h, dzb, mode='tn', out_dtype=F32, name=nm('d_in_b'))
    gw['in_c'] = _mm(h, dzc, mode='tn', out_dtype=F32, name=nm('d_in_c'))
    add = lambda acc, prev: prev + acc
    dh = _mm(dza, w['in_a'], mode='nt', out_dtype=F32, name=nm('d_h_a'))
    dh = _mm(dzb, w['in_b'], mode='nt', out_dtype=F32, name=nm('d_h_b'), epilogue=add, extras=[(dh, *_mn())])
    dx0, gs['g_mix'] = _mm(dzc, w['in_c'], mode='nt', out_dtype=F32, name=nm('d_h_c'), col_sums=True, full_rows=True,
                           epilogue=lambda acc, prev, xv, rv, gv: _norm_bwd_epilogue(prev + acc, xv, rv, gv),
                           extras=[(dh, *_mn()), (sv['x0'], *_mn()), (dx1, *_mn()), (p['g_mix'], *_nvec())])
    return dx0, gw, gs, (carried_mla or [None])[0], (carried_fox or [None])[0]


def _loss_head(x, target, g_final):
    d = x.shape[1]

    def fn(xv, tv, gv):
        r = lax.rsqrt(jnp.mean(xv * xv, axis=-1, keepdims=True) + EPS)
        xh = xv * r
        e = xh * gv - tv
        dy = e * (1.0 / d)
        gd = dy * gv
        dx = r * (gd - xh * jnp.mean(gd * xh, axis=-1, keepdims=True))
        row_loss = 0.5 * jnp.mean(e * e, axis=-1, keepdims=True)
        return dx, dy * xh, jnp.broadcast_to(row_loss, (xv.shape[0], LANES))

    return _rowwise(fn, [x, target], [g_final], [(d, F32)], [d, LANES], name='loss_head')


def _step(args):
    shapes = {nm: args[nm].shape for nm in ORDER}
    x, mem, target = args['x'][0], args['mem'][0], args['loss_target'][0]
    s = x.shape[0]

    def wire(nm, l):
        w = args[nm][l].astype(BF16)
        if nm == 'w_in':
            w = jnp.pad(w, ((0, 0), (0, WIN_PAD - WIN_SHARD)))
        if nm == 'w_gla_gate':
            w = jnp.pad(w, ((0, GATE_WIRE_ROWS - GLA_RANK), (0, 0)))
        return w

    axis_of = dict(BIG)
    names = tuple(nm for nm, _ in BIG)
    wires = lambda l, nms: [wire(nm, l) for nm in nms]
    width = lambda nm: WIN_PAD if nm == 'w_in' else args[nm].shape[2]
    side_by_side = lambda nms: [axis_of[nm] == 2 and width(nm) % LANES == 0 for nm in nms]
    over_ici = lambda l, nms: _gather_over_ici(wires(l, nms), side_by_side(nms))

    def whole(parts, nms, tag):
        side = side_by_side(nms)
        parts = _run_comm(_gather_over_d2d(parts, side), name=f'gather_d2d_{tag}', alias=True)
        full = {nm: p if sd else _full_layer(p, axis_of[nm]) for nm, p, sd in zip(nms, parts, side)}
        if 'w_gla_gate' in full:
            full['w_gla_gate'] = full['w_gla_gate'][:GLA_RANK]
        return full

    tabs = _rope_tables(s)
    layers_p = []
    for l in range(DEPTH):
        layers_p.append({
            'g_mix': args['g_mix'][l][None], 'b_fox': _padc(args['b_fox_forget'][l][None], 128),
            'b_gla': args['b_gla_gate'][l][None], 'g_gla_out': args['g_gla_out'][l][None],
            'g_mla_q': args['g_mla_q'][l][None], 'g_mla_kv': args['g_mla_kv'][l][None],
            'b_branch': args['b_branch_gate'][l][None], 'g_xa': args['g_xa'][l][None],
            'g_mem': args['g_mem'][l][None], 'g_mlp': args['g_mlp'][l][None]})

    first = _run_comm(over_ici(0, LATE), name='gather_ici_first_l0')
    w_now = _repack_layer_weights(whole(first, LATE, 'first_l0'))
    saved = []
    xl = x
    for l in range(DEPTH):
        carry_fox = over_ici(0, EARLY) if l == 0 else None
        after_fox = (lambda parts: whole(parts, EARLY, 'rest_l0')) if l == 0 else None
        carry_mla = over_ici(l + 1, names) if l + 1 < DEPTH else None
        xl, sv = _layer_fwd(xl, mem, w_now, layers_p[l], tabs, f'l{l}', carry_fox=carry_fox, after_fox=after_fox,
                            carry_mla=carry_mla)
        saved.append(sv)
        if carry_mla is not None:
            w_now = _repack_layer_weights(whole(sv.pop('carried_mla'), names, f'l{l + 1}'))
    dx, dg_final, loss_lanes = _loss_head(xl, target, args['g_final'][None])
    cidx = lax.axis_index('c')
    chip = 2 * lax.axis_index('x') + lax.axis_index('y')

    def pair_sums(gw, nms, tag):
        mine, theirs = [], []
        for nm in nms:
            shards = _split_full(gw[nm], axis_of[nm]).astype(BF16)
            h = shards.shape[1] // 2
            mine.append(lax.dynamic_slice_in_dim(shards, cidx * h, h, axis=1))
            theirs.append(lax.dynamic_slice_in_dim(shards, (1 - cidx) * h, h, axis=1))
        got = _to_sibling(theirs, name=f'grads_swap_{tag}')
        pairs = []
        for nm, a, b in zip(nms, mine, got):
            _, h, n = a.shape
            (p,) = _rowwise(lambda u, v: (u.astype(F32) + v.astype(F32),),
                            [a.reshape(N_CHIPS * h, n), b.reshape(N_CHIPS * h, n)], [], [(n, BF16)],
                            name=f'pair_sum_{nm}_{tag}')
            pairs.append(p.reshape(N_CHIPS, h, n))
        return pairs

    def finish(pairs, from_chips, nms, tag):
        own = [lax.dynamic_index_in_dim(p, chip, axis=0, keepdims=False) for p in pairs]
        mine = [_sum_chips(o, r, name=f'chip_sum_{nm}_{tag}') for nm, o, r in zip(nms, own, from_chips)]
        theirs = _to_sibling(mine, name=f'grads_join_{tag}')
        return {nm: jnp.where(cidx == 0, jnp.concatenate([a, b]), jnp.concatenate([b, a]))
                for nm, a, b in zip(nms, mine, theirs)}

    gs_layers, done = [None] * DEPTH, [{} for _ in range(DEPTH)]
    above = None
    for l in reversed(range(DEPTH)):
        lowest, early_pairs = l == 0, []

        def early(gw_early, l=l, early_pairs=early_pairs):
            early_pairs.extend(pair_sums(gw_early, EARLY, f'early_l{l}'))
            return _chip_exchange(early_pairs)

        carry_mla = None if above is None else _chip_exchange(above[1])
        dx, gw, gs_layers[l], got_mla, got_fox = _layer_bwd(
            dx, mem, saved[l]['w'], layers_p[l], tabs, saved[l], f'l{l}', carry_mla=carry_mla,
            early=early if lowest else None)
        if above is not None:
            done[above[0]].update(finish(above[1], got_mla, names, f'l{above[0]}'))
        grads = _unpack_layer_grads(gw)
        if lowest:
            done[l].update(finish(early_pairs, got_fox, EARLY, f'early_l{l}'))
            late_pairs = pair_sums(grads, LATE, f'late_l{l}')
            from_late = _run_comm(_chip_exchange(late_pairs), name=f'grads_exchange_late_l{l}')
            done[l].update(finish(late_pairs, from_late, LATE, f'late_l{l}'))
        else:
            above = (l, pair_sums(grads, names, f'l{l}'))
    grad_x = dx[None]
    gshard = {nm: jnp.stack([done[l][nm] for l in range(DEPTH)]) for nm in names}

    small_g = []
    for nm, key in (('g_mix', 'g_mix'), ('b_fox_forget', 'b_fox'), ('b_gla_gate', 'b_gla'),
                    ('g_gla_out', 'g_gla_out'), ('g_mla_q', 'g_mla_q'), ('g_mla_kv', 'g_mla_kv'),
                    ('b_branch_gate', 'b_branch'), ('g_xa', 'g_xa'), ('g_mem', 'g_mem'), ('g_mlp', 'g_mlp')):
        width = shapes[nm][1]
        small_g.append(jnp.concatenate([gs_layers[l][key][0, :width] for l in range(DEPTH)]))
    small_g.append(dg_final[0])
    small_g.append(loss_lanes[0, :1])
    flat = jnp.concatenate(small_g)
    n_small = flat.shape[0]
    srows = -(-n_small // (8 * LANES)) * 8
    pad = lambda v: jnp.pad(v, (0, srows * LANES - v.shape[0])).reshape(srows, LANES)
    all_small = _all_gather8(pad(flat), name='gather_small')
    sw, sm, svv = (pad(jnp.concatenate([args[pre + nm].reshape(-1) for nm in SMALL] + [jnp.zeros((1,), F32)]))
                   for pre in ('', 'm_', 'v_'))

    def small_body(g_ref, w_ref, m_ref, v_ref, go_ref, d_ref, mo_ref, vo_ref):
        g = g_ref[0]
        for q in range(1, N_DEV):
            g = g + g_ref[q]
        go_ref[...] = g
        d_ref[...], mo_ref[...], vo_ref[...] = _adam(w_ref[...], g, m_ref[...], v_ref[...])

    sg, sd, snm, snv = pl.pallas_call(
        small_body, name='small_sum_adam', out_shape=[jax.ShapeDtypeStruct((srows, LANES), F32)] * 4,
        compiler_params=pltpu.CompilerParams(vmem_limit_bytes=VMEM_LIMIT))(all_small, sw, sm, svv)

    def unsmall(buf):
        v, out, off = buf.reshape(-1), {}, 0
        for nm in SMALL:
            nel = math.prod(shapes[nm])
            out[nm] = v[off:off + nel].reshape(shapes[nm])
            off += nel
        return out, v[off]

    res = {}
    (res['grad'], loss), (res['delta'], _), (res['m'], _), (res['v'], _) = (unsmall(t) for t in (sg, sd, snm, snv))

    for nm, _ in BIG:
        shp = args[nm].shape
        view = lambda t: t.reshape(shp[0] * shp[1], shp[2])
        d, m2, v2 = _rowwise(_adam, [view(args[nm]), view(gshard[nm]), view(args['m_' + nm]), view(args['v_' + nm])],
                             [], [(shp[2], F32)] * 3, name=f'adam_{nm}', ts=256)
        res['grad'][nm], res['delta'][nm], res['m'][nm], res['v'][nm] = (
            gshard[nm], d.reshape(shp), m2.reshape(shp), v2.reshape(shp))

    return (loss, grad_x, *[res['grad'][nm] for nm in ORDER], *[res['delta'][nm] for nm in ORDER],
            *[res['m'][nm] for nm in ORDER], *[res['v'][nm] for nm in ORDER])


def kernel(x, mem, g_mix, w_in, b_fox_forget, w_gla_gate, b_gla_gate, g_gla_out, g_mla_q, w_mla_uq, g_mla_kv, w_mla_ukv, b_branch_gate, w_up_fox, w_up_gla, w_up_mla, w_out, g_xa, g_mem, w_xq, w_xkv, w_xo, g_mlp, w_mlp1, w_mlp2, g_final, loss_target, m_g_mix, m_w_in, m_b_fox_forget, m_w_gla_gate, m_b_gla_gate, m_g_gla_out, m_g_mla_q, m_w_mla_uq, m_g_mla_kv, m_w_mla_ukv, m_b_branch_gate, m_w_up_fox, m_w_up_gla, m_w_up_mla, m_w_out, m_g_xa, m_g_mem, m_w_xq, m_w_xkv, m_w_xo, m_g_mlp, m_w_mlp1, m_w_mlp2, m_g_final, v_g_mix, v_w_in, v_b_fox_forget, v_w_gla_gate, v_b_gla_gate, v_g_gla_out, v_g_mla_q, v_w_mla_uq, v_g_mla_kv, v_w_mla_ukv, v_b_branch_gate, v_w_up_fox, v_w_up_gla, v_w_up_mla, v_w_out, v_g_xa, v_g_mem, v_w_xq, v_w_xkv, v_w_xo, v_g_mlp, v_w_mlp1, v_w_mlp2, v_g_final):
    return _step(dict(locals()))
```

```python
import functools
import math
import typing

import jax
import jax.numpy as jnp
from jax import lax
from jax.experimental import pallas as pl
from jax.experimental.pallas import tpu as pltpu

F32 = jnp.float32
BF16 = jnp.bfloat16
MESH = pl.DeviceIdType.MESH

D_MODEL = 1024
DEPTH = 2
CHUNK = 64
EPS = 1e-6
FOX_HEADS, FOX_HD = 4, 64
GLA_HEADS, GLA_DK, GLA_DV, GLA_RANK, GLA_TAU = 4, 64, 128, 16, 16.0
MLA_HEADS, MLA_Q_RANK, MLA_KV_RANK, MLA_NOPE, MLA_ROPE, MLA_VD = 4, 256, 128, 64, 32, 64
ROPE_BASE = 10000.0
XA_HEADS, XA_HD = 4, 128
D_FF = 4 * D_MODEL
IN_SIZES = (256, 256, 256, 4, 256, 256, 512, 16, 512, 256, 128, 32, 3072)
N_IN = sum(IN_SIZES)

ADAM_LR, ADAM_B1, ADAM_B2, ADAM_EPS, ADAM_WD, ADAM_STEP = 0.001, 0.9, 0.999, 1e-08, 0.01, 10

N_CHIPS = 4
N_DEV = 8
LANES = 128
VMEM_LIMIT = 48 * 1024 * 1024
MASK_VALUE = -1e30

BIG = (('w_in', 2), ('w_gla_gate', 2), ('w_mla_uq', 2), ('w_mla_ukv', 2), ('w_up_fox', 2), ('w_up_gla', 2),
       ('w_up_mla', 2), ('w_out', 1), ('w_xq', 1), ('w_xkv', 1), ('w_xo', 2), ('w_mlp1', 2), ('w_mlp2', 1))
SMALL = ('g_mix', 'b_fox_forget', 'b_gla_gate', 'g_gla_out', 'g_mla_q', 'g_mla_kv', 'b_branch_gate',
         'g_xa', 'g_mem', 'g_mlp', 'g_final')
ORDER = ('g_mix', 'w_in', 'b_fox_forget', 'w_gla_gate', 'b_gla_gate', 'g_gla_out', 'g_mla_q', 'w_mla_uq',
         'g_mla_kv', 'w_mla_ukv', 'b_branch_gate', 'w_up_fox', 'w_up_gla', 'w_up_mla', 'w_out', 'g_xa', 'g_mem',
         'w_xq', 'w_xkv', 'w_xo', 'g_mlp', 'w_mlp1', 'w_mlp2', 'g_final')


def _params(*sem, extra_vmem=0):
    return pltpu.CompilerParams(dimension_semantics=sem, vmem_limit_bytes=VMEM_LIMIT + extra_vmem)


def _sig(x):
    return 1.0 / (1.0 + jnp.exp(-x))


def _logsig(x):
    return jnp.minimum(x, 0.0) - jnp.log(1.0 + jnp.exp(-jnp.abs(x)))


NN = (((1,), (0,)), ((), ()))
NT = (((1,), (1,)), ((), ()))
TN = (((0,), (0,)), ((), ()))


def _dot(a, b, dims=NN):
    return lax.dot_general(a, b, dims, preferred_element_type=F32)


class Cols(typing.NamedTuple):
    arr: jax.Array
    width: int
    blk: int


def _tri_dot(tri, x):
    hi = x.astype(BF16)
    r1 = x - hi.astype(F32)
    mid = r1.astype(BF16)
    lo = (r1 - mid.astype(F32)).astype(BF16)
    return _dot(tri, hi) + _dot(tri, mid) + _dot(tri, lo)


MM_TILES = ((1024, 1024), (1024, 512), (512, 1024), (512, 512), (256, 1024), (512, 256), (256, 512), (256, 256),
            (128, 1024), (128, 128))
MM_VMEM_BUDGET = 38 * 1024 * 1024
MM_VMEM_EXTRA = 8 * 1024 * 1024


def _mm_tiles(m, n, k, a_bytes, b_bytes, out_bytes, ex_bytes, has_norm, emit_norm, has_fn, full_rows):
    for tm, tn in MM_TILES:
        tm, tn = min(tm, m), min(tn, n)
        if m % tm or n % tn or (full_rows and tn != n):
            continue
        blocks = tm * k * a_bytes + k * tn * b_bytes + tm * tn * (out_bytes + ex_bytes) + (tm * k * 2 if emit_norm else 0)
        temps = tm * tn * 4 + (tm * k * 2 if has_norm else 0) + (tm * k * 6 if has_fn or has_norm else 0)
        if 2 * blocks + temps <= MM_VMEM_BUDGET + (MM_VMEM_EXTRA if has_fn else 0):
            return tm, tn
    raise ValueError((m, n, k))


def _mm(a, b, *, mode, out_dtype, name, norm_g=None, emit_norm=False, a_fn=None, extras=(), epilogue=None,
        col_sums=False, full_rows=False):
    a_blk = 0
    if isinstance(a, Cols):
        a, width, a_blk = a
        a_shape = (a.shape[0], width)
    else:
        a_shape = a.shape
    if mode == 'tn':
        k, m = a_shape
    else:
        m, k = a_shape
    n = b.shape[0] if mode == 'nt' else b.shape[1]
    assert (b.shape[1] if mode == 'nt' else b.shape[0]) == k, (name, a.shape, b.shape)
    has_norm = norm_g is not None
    ex_bytes = sum(arr.dtype.itemsize for arr, kind, _ in extras if kind == 'mn')
    tm, tn = _mm_tiles(m, n, k, a.dtype.itemsize, b.dtype.itemsize, jnp.dtype(out_dtype).itemsize, ex_bytes, has_norm,
                       emit_norm, a_fn is not None, full_rows)
    assert all(col % tn == 0 for _, _, col in extras), (name, tn)
    assert a_blk == 0 or (mode == 'nn') or (mode == 'tn' and tm == m)
    assert not (col_sums and (has_norm or emit_norm))
    ij = (lambda f: lambda g0, g1: f(g1, g0)) if col_sums else (lambda f: f)
    spec = lambda blk, f: pl.BlockSpec(blk, ij(f))
    if mode == 'tn':
        a_spec = spec((k, tm), lambda i, j: (0, i + a_blk))
    else:
        a_spec = spec((tm, k), lambda i, j: (i, a_blk))
    b_spec = spec((tn, k), lambda i, j: (j, 0)) if mode == 'nt' else spec((k, tn), lambda i, j: (0, j))
    dims = {'nn': NN, 'nt': NT, 'tn': TN}[mode]
    assert not (has_norm and mode != 'nn')
    n_ex = len(extras)

    def body(*refs):
        a_ref, b_ref = refs[0], refs[1]
        pos = 2
        g_ref = None
        if has_norm:
            g_ref = refs[pos]
            pos += 1
        ex_refs = refs[pos:pos + n_ex]
        pos += n_ex
        o_ref = refs[pos]
        pos += 1
        h_ref = None
        if emit_norm:
            h_ref = refs[pos]
            pos += 1
        if has_norm:
            an_ref = refs[pos]

            @pl.when(pl.program_id(1) == 0)
            def _():
                xf = a_ref[...].astype(F32)
                y = xf * lax.rsqrt(jnp.mean(xf * xf, axis=-1, keepdims=True) + EPS) * g_ref[...]
                an_ref[...] = y.astype(BF16)
                if emit_norm:
                    h_ref[...] = y.astype(BF16)

            av = an_ref[...]
        else:
            av = a_ref[...]
            if a_fn is not None:
                av = a_fn(av)
            av = av.astype(BF16)
        acc = _dot(av, b_ref[...].astype(BF16), dims)
        if epilogue is not None:
            acc = epilogue(acc, *[r[...] for r in ex_refs])
        acc, to_sum = acc if isinstance(acc, tuple) else (acc, acc)
        o_ref[...] = acc.astype(out_dtype)
        if col_sums:
            sum_ref = refs[pos]

            @pl.when(pl.program_id(1) == 0)
            def _():
                sum_ref[...] = jnp.zeros_like(sum_ref)

            sum_ref[...] += jnp.sum(to_sum, axis=0, keepdims=True)

    in_specs = [a_spec, b_spec]
    args = [a, b]
    if has_norm:
        in_specs.append(pl.BlockSpec((1, k), lambda i, j: (0, 0)))
        args.append(norm_g)
    for arr, kind, col in extras:
        if kind == 'mn':
            in_specs.append(spec((tm, tn), lambda i, j, o=col // tn: (i, j + o)))
        else:
            in_specs.append(spec((1, tn), lambda i, j, o=col // tn: (0, j + o)))
        args.append(arr)
    out_shape = [jax.ShapeDtypeStruct((m, n), out_dtype)]
    out_specs = [spec((tm, tn), lambda i, j: (i, j))]
    if emit_norm:
        out_shape.append(jax.ShapeDtypeStruct((m, k), BF16))
        out_specs.append(pl.BlockSpec((tm, k), lambda i, j: (i, 0)))
    if col_sums:
        out_shape.append(jax.ShapeDtypeStruct((1, n), F32))
        out_specs.append(spec((1, tn), lambda i, j: (0, j)))
    scratch = [pltpu.VMEM((tm, k), BF16)] if has_norm else []
    grid = (n // tn, m // tm) if col_sums else (m // tm, n // tn)
    res = pl.pallas_call(
        body, name=name, grid=grid, in_specs=in_specs, out_specs=out_specs, out_shape=out_shape,
        scratch_shapes=scratch,
        compiler_params=_params('arbitrary', 'arbitrary', extra_vmem=MM_VMEM_EXTRA if a_fn is not None else 0))(*args)
    return res if emit_norm or col_sums else res[0]


def _gated_merge(outs, ups, zg, bias, *, name, tm=1024, tn=512):
    s, n, nq = zg.shape[0], ups[0].shape[1], len(outs)
    tm, tn = min(tm, s), min(tn, n)
    per = n // tn

    def body(*refs):
        y = None
        for q in range(nq):
            o_ref, w_ref, z_ref, b_ref = refs[q], refs[nq + q], refs[2 * nq + q], refs[3 * nq + q]
            term = _sig(z_ref[...].astype(F32) + b_ref[...]) * _dot(o_ref[...], w_ref[...])
            y = term if y is None else y + term
        refs[4 * nq][...] = y.astype(BF16)

    in_specs = [pl.BlockSpec((tm, o.shape[1]), lambda i, j: (i, 0)) for o in outs]
    in_specs += [pl.BlockSpec((u.shape[0], tn), lambda i, j: (0, j)) for u in ups]
    in_specs += [pl.BlockSpec((tm, tn), lambda i, j, q=q: (i, j + q * per)) for q in range(nq)]
    in_specs += [pl.BlockSpec((1, tn), lambda i, j, q=q: (0, j + q * per)) for q in range(nq)]
    return pl.pallas_call(body, name=name, grid=(s // tm, per), in_specs=in_specs,
                          out_specs=pl.BlockSpec((tm, tn), lambda i, j: (i, j)),
                          out_shape=jax.ShapeDtypeStruct((s, n), BF16),
                          compiler_params=_params('arbitrary', 'arbitrary'))(*outs, *ups, *[zg] * nq, *[bias] * nq)


def _gated_merge_bwd(dy, zg, bias, outs, ups, do_dtypes, *, name, tm=512):
    s, n = dy.shape
    nq = len(outs)
    tm = min(tm, s)

    def body(*refs):
        dy_ref, zg_ref, b_ref = refs[:3]
        o_refs, w_refs = refs[3:3 + nq], refs[3 + nq:3 + 2 * nq]
        du_refs, do_refs = refs[3 + 2 * nq:3 + 3 * nq], refs[3 + 3 * nq:3 + 4 * nq]
        dz_ref, db_ref = refs[3 + 4 * nq:]

        @pl.when(pl.program_id(0) == 0)
        def _():
            db_ref[...] = jnp.zeros_like(db_ref)

        d = dy_ref[...].astype(F32)
        for q in range(nq):
            cols = slice(q * n, (q + 1) * n)
            g = _sig(zg_ref[:, cols].astype(F32) + b_ref[:, cols])
            du = (d * g).astype(BF16)
            du_refs[q][...] = du
            do_refs[q][...] = _dot(du, w_refs[q][...], NT).astype(do_dtypes[q])
            dz = d * _dot(o_refs[q][...], w_refs[q][...]) * g * (1.0 - g)
            dz_ref[:, cols] = dz.astype(BF16)
            db_ref[:, cols] += jnp.sum(dz, axis=0, keepdims=True)

    row = lambda w: pl.BlockSpec((tm, w), lambda i: (i, 0))
    whole = lambda a: pl.BlockSpec(a.shape, lambda i: (0, 0))
    in_specs = [row(n), row(nq * n), whole(bias)] + [row(o.shape[1]) for o in outs] + [whole(u) for u in ups]
    out_specs = [row(n)] * nq + [row(o.shape[1]) for o in outs] + [row(nq * n), pl.BlockSpec((1, nq * n), lambda i: (0, 0))]
    out_shape = ([jax.ShapeDtypeStruct((s, n), BF16)] * nq
                 + [jax.ShapeDtypeStruct((s, o.shape[1]), dt) for o, dt in zip(outs, do_dtypes)]
                 + [jax.ShapeDtypeStruct((s, nq * n), BF16), jax.ShapeDtypeStruct((1, nq * n), F32)])
    res = pl.pallas_call(body, name=name, grid=(s // tm,), in_specs=in_specs, out_specs=out_specs, out_shape=out_shape,
                         compiler_params=_params('arbitrary'))(dy, zg, bias, *outs, *ups)
    return res[:nq], res[nq:2 * nq], res[2 * nq], res[2 * nq + 1]


def _mn(col_off=0):
    return 'mn', col_off


def _nvec(col_off=0):
    return 'n', col_off


def _rowwise(fn, rows, consts, outs, sums=(), *, name, ts=1024):
    views = [x if isinstance(x, Cols) else Cols(x, x.shape[1], 0) for x in rows]
    rows = [v.arr for v in views]
    r = rows[0].shape[0]
    ts = min(ts, r)
    assert r % ts == 0, (name, r, ts)
    nr, nc, no, ns = len(rows), len(consts), len(outs), len(sums)

    def body(*refs):
        vals = fn(*[x[...] for x in refs[:nr + nc]])
        for q in range(no):
            refs[nr + nc + q][...] = vals[q].astype(outs[q][1])
        if ns:
            @pl.when(pl.program_id(0) == 0)
            def _():
                for q in range(ns):
                    refs[nr + nc + no + q][...] = jnp.zeros((1, sums[q]), F32)

            for q in range(ns):
                refs[nr + nc + no + q][...] += jnp.sum(vals[no + q].astype(F32), axis=0, keepdims=True)

    in_specs = [pl.BlockSpec((ts, v.width), lambda i, blk=v.blk: (i, blk)) for v in views]
    in_specs += [pl.BlockSpec(x.shape, lambda i, nd=x.ndim: (0,) * nd) for x in consts]
    out_specs = [pl.BlockSpec((ts, w), lambda i: (i, 0)) for w, _ in outs]
    out_specs += [pl.BlockSpec((1, w), lambda i: (0, 0)) for w in sums]
    out_shape = [jax.ShapeDtypeStruct((r, w), dt) for w, dt in outs]
    out_shape += [jax.ShapeDtypeStruct((1, w), F32) for w in sums]
    return pl.pallas_call(body, name=name, grid=(r // ts,), in_specs=in_specs, out_specs=out_specs,
                          out_shape=out_shape, compiler_params=_params('arbitrary'))(*rows, *consts)


def _cumsum_rows(x, *, reverse, name, bs=256):
    s, w = x.shape
    bs = min(bs, s)
    nb = s // bs

    def body(x_ref, o_ref, carry):
        @pl.when(pl.program_id(0) == 0)
        def _():
            carry[...] = jnp.zeros_like(carry)

        r = lax.broadcasted_iota(jnp.int32, (bs, bs), 0)
        c = lax.broadcasted_iota(jnp.int32, (bs, bs), 1)
        tri = jnp.where((c >= r) if reverse else (c <= r), 1.0, 0.0).astype(BF16)
        xv = x_ref[...]
        o_ref[...] = _tri_dot(tri, xv) + carry[...]
        carry[...] += jnp.sum(xv, axis=0, keepdims=True)

    imap = (lambda i: (nb - 1 - i, 0)) if reverse else (lambda i: (i, 0))
    return pl.pallas_call(body, name=name, grid=(nb,), in_specs=[pl.BlockSpec((bs, w), imap)],
                          out_specs=pl.BlockSpec((bs, w), imap), out_shape=jax.ShapeDtypeStruct((s, w), F32),
                          scratch_shapes=[pltpu.VMEM((1, w), F32)], compiler_params=_params('arbitrary'))(x)


def _mask(mode, q0, k0, bq, bk):
    qpos = q0 + lax.broadcasted_iota(jnp.int32, (bq, bk), 0)
    kpos = k0 + lax.broadcasted_iota(jnp.int32, (bq, bk), 1)
    if mode == 'causal':
        return kpos <= qpos
    return kpos < (jnp.right_shift(qpos, int(math.log2(CHUNK))) + 1) * CHUNK


ROPE_SHIFT = int(math.log2(MLA_ROPE))
FOX_SCALE, MLA_SCALE, XA_SCALE = FOX_HD ** -0.5, (MLA_NOPE + MLA_ROPE) ** -0.5, XA_HD ** -0.5
ATTN_ROW_SLAB = 512


def _lane_masks(g, b, rope):
    lane = lax.broadcasted_iota(jnp.int32, (1, LANES), 1)
    heads = [None if g == 1 else (lane >= hh * (LANES // g)) & (lane < (hh + 1) * (LANES // g)) for hh in range(g)]
    ropes = [jnp.right_shift(lane, ROPE_SHIFT) == b * g + hh for hh in range(g)] if rope else [None] * g
    return heads, ropes


def _sel(mask, x):
    return x if mask is None else jnp.where(mask, x, jnp.zeros_like(x))


class Step(typing.NamedTuple):
    qi: typing.Any
    kj: typing.Any
    first: typing.Any
    last: typing.Any
    plain: typing.Any
    masked: typing.Any


def _fwd_steps(tri, nq, nk):
    if not tri:
        return (nq, nk), lambda i, j: Step(i, j, j == 0, j == nk - 1, True, False)
    if nq % 2:
        return (nq, nk), lambda i, j: Step(i, jnp.minimum(i, j), j == 0, j == nk - 1, j < i, j == i)

    def at(i, t):
        low = t <= i
        diag = (t == i) | (t == nq)
        return Step(jnp.where(low, i, nq - 1 - i), jnp.where(low, t, t - (i + 1)), (t == 0) | (t == i + 1), diag,
                    jnp.logical_not(diag), diag)

    return (nq // 2, nq + 1), at


def _bwd_steps(tri, nq, nk):
    if not tri:
        return (nk, nq), lambda j, i: Step(i, j, i == 0, i == nq - 1, True, False)
    if nk % 2:
        return (nk, nq), lambda j, i: Step(jnp.maximum(i, j), j, i == 0, i == nq - 1, i > j, i == j)

    def at(j, t):
        n1 = nq - j
        low = t < n1
        diag = (t == 0) | (t == n1)
        return Step(jnp.where(low, j + t, nk - 1 - j + t - n1), jnp.where(low, j, nk - 1 - j), diag,
                    (t == n1 - 1) | (t == nq), jnp.logical_not(diag), diag)

    return (nk // 2, nq + 1), at


def _carried(comm, refs, n_in, n_out):
    ci, co = len(comm.ins), len(comm.out_shapes)
    ins = refs[n_in:n_in + ci]
    outs = refs[n_in + ci + n_out:n_in + ci + n_out + co]
    rest = refs[:n_in] + refs[n_in + ci:n_in + ci + n_out] + refs[n_in + ci + n_out + co:-2]
    return rest, (ins, outs, refs[-2], refs[-1])


def _mattn_fwd(q, k, v, *, qc, kc, vc, nb, g, mode, name, dq_scale=1.0, ck=None, qr=None, qrc=0, kr=None, blk=512,
               comm=None):
    s, t = q.shape[0], k.shape[0]
    bq, bk = min(blk, s), min(blk, t)
    nq, nk = s // bq, t // bk
    tri = mode != 'full'
    bias, rope = ck is not None, qr is not None
    assert not tri or (bq == bk and bq % CHUNK == 0)
    rs = min(ATTN_ROW_SLAB, bq)
    n_in = 3 + bias + 2 * rope
    (n1, n2), step_at = _fwd_steps(tri, nq, nk)

    def body(*refs):
        refs = list(refs)
        b, p1, p2 = pl.program_id(0), pl.program_id(1), pl.program_id(2)
        st = step_at(p1, p2)
        i, j = st.qi, st.kj
        if comm is not None:
            refs, comm_refs = _carried(comm, refs, n_in, 2)
            pl.when((b == 0) & (p1 == 0) & (p2 == 0))(lambda: comm.start(*comm_refs))
        q_ref, k_ref, v_ref = refs[:3]
        pos = 3
        ck_ref = qr_ref = kr_ref = None
        if bias:
            ck_ref = refs[pos]
            pos += 1
        if rope:
            qr_ref, kr_ref = refs[pos:pos + 2]
            pos += 2
        o_ref, lse_ref, m_s, l_s, acc_s = refs[pos:]
        heads, ropes = _lane_masks(g, b, rope)

        @pl.when(st.first)
        def _():
            m_s[...] = jnp.full_like(m_s, MASK_VALUE)
            l_s[...] = jnp.zeros_like(l_s)
            acc_s[...] = jnp.zeros_like(acc_s)

        def compute(masked):
            k2, v2 = k_ref[...], v_ref[...]
            for r in range(bq // rs):
                rows = pl.ds(r * rs, rs)
                q2 = q_ref[rows, :]
                alphas, pvs = [], []
                for hh in range(g):
                    sc = _dot(_sel(heads[hh], q2), k2, NT)
                    if rope:
                        sc = sc + _dot(_sel(ropes[hh], qr_ref[rows, :]), kr_ref[...], NT)
                    if bias:
                        sc = sc - ck_ref[0, hh:hh + 1, :]
                    if masked:
                        sc = jnp.where(_mask(mode, i * bq + r * rs, j * bk, rs, bk), sc, MASK_VALUE)
                    m_prev = m_s[hh, rows]
                    m_new = jnp.maximum(m_prev, jnp.max(sc, axis=1, keepdims=True))
                    alpha = jnp.exp(m_prev - m_new)
                    p = jnp.exp(sc - m_new)
                    l_s[hh, rows] = alpha * l_s[hh, rows] + jnp.sum(p, axis=1, keepdims=True)
                    m_s[hh, rows] = m_new
                    alphas.append(alpha)
                    pvs.append(_dot(p.astype(BF16), _sel(heads[hh], v2)))
                alpha = alphas[0]
                for hh in range(1, g):
                    alpha = jnp.where(heads[hh], alphas[hh], alpha)
                acc_s[rows, :] = acc_s[rows, :] * alpha + sum(pvs[1:], pvs[0])

        if tri:
            pl.when(st.plain)(functools.partial(compute, False))
            pl.when(st.masked)(functools.partial(compute, True))
        else:
            compute(False)

        @pl.when(st.last)
        def _():
            lane = lax.broadcasted_iota(jnp.int32, (bq, LANES), 1)
            l_full, lse = l_s[0], jnp.zeros((bq, LANES), F32)
            for hh in range(g):
                if hh:
                    l_full = jnp.where(heads[hh], l_s[hh], l_full)
                lse = jnp.where(lane == hh, m_s[hh] + jnp.log(l_s[hh]), lse)
            o_ref[...] = (acc_s[...] / l_full).astype(o_ref.dtype)
            lse_ref[...] = lse

        if comm is not None:
            pl.when((b == nb - 1) & (p1 == n1 - 1) & (p2 == n2 - 1))(lambda: comm.finish(*comm_refs))

    qi = lambda p1, p2: step_at(p1, p2).qi
    kj = lambda p1, p2: step_at(p1, p2).kj
    in_specs = [pl.BlockSpec((bq, LANES), lambda b, p1, p2: (qi(p1, p2), qc + b)),
                pl.BlockSpec((bk, LANES), lambda b, p1, p2: (kj(p1, p2), kc + b)),
                pl.BlockSpec((bk, LANES), lambda b, p1, p2: (kj(p1, p2), vc + b))]
    args = [q, k, v]
    if bias:
        in_specs.append(pl.BlockSpec((1, 8, bk), lambda b, p1, p2: (b, 0, kj(p1, p2))))
        args.append(ck)
    if rope:
        in_specs += [pl.BlockSpec((bq, LANES), lambda b, p1, p2: (qi(p1, p2), qrc)),
                     pl.BlockSpec((bk, LANES), lambda b, p1, p2: (kj(p1, p2), 0))]
        args += [qr, kr]
    out = pl.BlockSpec((bq, LANES), lambda b, p1, p2: (qi(p1, p2), b))
    out_specs = [out, out]
    out_shape = [jax.ShapeDtypeStruct((s, LANES * nb), BF16), jax.ShapeDtypeStruct((s, LANES * nb), F32)]
    scratch = [pltpu.VMEM((g, bq, 1), F32), pltpu.VMEM((g, bq, 1), F32), pltpu.VMEM((bq, LANES), F32)]
    if comm is not None:
        in_specs += [ANY] * len(comm.ins)
        args += comm.ins
        out_specs += [ANY] * len(comm.out_shapes)
        out_shape += comm.out_shapes
        scratch += _sems(comm.n_sems, comm.n_sems)
    res = pl.pallas_call(body, name=name, grid=(nb, n1, n2), in_specs=in_specs, out_specs=out_specs, out_shape=out_shape,
                         scratch_shapes=scratch, compiler_params=_params('arbitrary', 'arbitrary', 'arbitrary'))(*args)
    return res if comm is None else (res[0], res[1], res[2:])


def _mattn_bwd(q, k, v, o, do, lse, *, qc, kc, vc, nb, g, mode, name, dq_scale=1.0, ck=None, qr=None, qrc=0, kr=None,
               blk=512, comm=None):
    s, t = q.shape[0], k.shape[0]
    bq, bk = min(blk, s), min(blk, t)
    nq, nk = s // bq, t // bk
    tri = mode != 'full'
    bias, rope = ck is not None, qr is not None
    rs = min(ATTN_ROW_SLAB, bq)
    n_in, n_out = 6 + bias + 2 * rope, 3 + 2 * bias + 2 * rope
    (n1, n2), step_at = _bwd_steps(tri, nq, nk)

    def body(*refs):
        refs = list(refs)
        if comm is not None:
            refs, comm_refs = _carried(comm, refs, n_in, n_out)
            first = (pl.program_id(0) == 0) & (pl.program_id(1) == 0) & (pl.program_id(2) == 0)
            pl.when(first)(lambda: comm.start(*comm_refs))
        q_ref, k_ref, v_ref, o_ref, do_ref, lse_ref = refs[:6]
        pos = 6
        ck_ref = qr_ref = kr_ref = dck_ref = dcq_ref = dqr_ref = dkr_ref = dck_s = None
        if bias:
            ck_ref = refs[pos]
            pos += 1
        if rope:
            qr_ref, kr_ref = refs[pos:pos + 2]
            pos += 2
        dq_ref, dk_ref, dv_ref = refs[pos:pos + 3]
        pos += 3
        if bias:
            dck_ref, dcq_ref = refs[pos:pos + 2]
            pos += 2
        if rope:
            dqr_ref, dkr_ref = refs[pos:pos + 2]
            pos += 2
        dk_s, dv_s = refs[pos:pos + 2]
        if bias:
            dck_s = refs[pos + 2]
        b, p1, p2 = pl.program_id(0), pl.program_id(1), pl.program_id(2)
        st = step_at(p1, p2)
        i, j = st.qi, st.kj
        heads, ropes = _lane_masks(g, b, rope)

        @pl.when((p1 == 0) & (p2 == 0))
        def _():
            dq_ref[...] = jnp.zeros_like(dq_ref)
            if bias:
                dcq_ref[...] = jnp.zeros_like(dcq_ref)

        if rope:
            @pl.when((b == 0) & (p1 == 0) & (p2 == 0))
            def _():
                dqr_ref[...] = jnp.zeros_like(dqr_ref)
                dkr_ref[...] = jnp.zeros_like(dkr_ref)

        @pl.when(st.first)
        def _():
            dk_s[...] = jnp.zeros_like(dk_s)
            dv_s[...] = jnp.zeros_like(dv_s)
            if bias:
                dck_s[...] = jnp.zeros_like(dck_s)

        def compute(masked):
            k2, v2 = k_ref[...], v_ref[...]
            lane = lax.broadcasted_iota(jnp.int32, (rs, LANES), 1)
            rk = pl.ds(pl.multiple_of(j * bk, bk), bk)
            add = lambda tot, x: x if tot is None else tot + x
            dv_t = dk_t = dkr_t = None
            dck_t = [None] * g
            for r in range(bq // rs):
                rows = pl.ds(r * rs, rs)
                rq = pl.ds(pl.multiple_of(i * bq + r * rs, rs), rs)
                q2, do2, lse2 = q_ref[rows, :], do_ref[rows, :], lse_ref[rows, :]
                dd = do2.astype(F32) * o_ref[rows, :].astype(F32)
                dq_t = dqr_t = dcq_t = None
                for hh in range(g):
                    qm = _sel(heads[hh], q2)
                    sc = _dot(qm, k2, NT)
                    if rope:
                        qrm = _sel(ropes[hh], qr_ref[rows, :])
                        sc = sc + _dot(qrm, kr_ref[...], NT)
                    if bias:
                        sc = sc - ck_ref[0, hh:hh + 1, :]
                    if masked:
                        sc = jnp.where(_mask(mode, i * bq + r * rs, j * bk, rs, bk), sc, MASK_VALUE)
                    p = jnp.exp(sc - jnp.sum(jnp.where(lane == hh, lse2, 0.0), axis=1, keepdims=True))
                    dom = _sel(heads[hh], do2)
                    dp = _dot(dom, v2, NT)
                    delta = jnp.sum(_sel(heads[hh], dd), axis=1, keepdims=True)
                    ds = p * (dp - delta)
                    dsb = ds.astype(BF16)
                    dv_t = add(dv_t, _dot(p.astype(BF16), dom, TN))
                    dk_t = add(dk_t, _dot(dsb, qm, TN))
                    dq_t = add(dq_t, _dot(dsb, _sel(heads[hh], k2)))
                    if rope:
                        dqr_t = add(dqr_t, _dot(dsb, _sel(ropes[hh], kr_ref[...])))
                        dkr_t = add(dkr_t, _dot(dsb, qrm, TN))
                    if bias:
                        dck_t[hh] = add(dck_t[hh], jnp.sum(ds, axis=0, keepdims=True))
                        dcq_t = add(dcq_t, jnp.where(lane == hh, jnp.sum(ds, axis=1, keepdims=True), 0.0))
                dq_ref[rq, :] += dq_t if dq_scale == 1.0 else dq_scale * dq_t
                if rope:
                    dqr_ref[rq, :] += dq_scale * dqr_t
                if bias:
                    dcq_ref[rq, :] += dcq_t
            dv_s[...] += dv_t
            dk_s[...] += dk_t
            if rope:
                dkr_ref[rk, :] += dkr_t
            if bias:
                for hh in range(g):
                    dck_s[hh:hh + 1, :] -= dck_t[hh]

        if tri:
            pl.when(st.plain)(functools.partial(compute, False))
            pl.when(st.masked)(functools.partial(compute, True))
        else:
            compute(False)

        @pl.when(st.last)
        def _():
            dk_ref[...] = dk_s[...]
            dv_ref[...] = dv_s[...]
            if bias:
                dck_ref[0] = dck_s[...]

        if comm is not None:
            pl.when((b == nb - 1) & (p1 == n1 - 1) & (p2 == n2 - 1))(lambda: comm.finish(*comm_refs))

    qrow = lambda col: pl.BlockSpec((bq, LANES), lambda b, p1, p2: (step_at(p1, p2).qi, col(b)))
    krow = lambda col: pl.BlockSpec((bk, LANES), lambda b, p1, p2: (step_at(p1, p2).kj, col(b)))
    in_specs = [qrow(lambda b: qc + b), krow(lambda b: kc + b), krow(lambda b: vc + b), qrow(lambda b: b),
                qrow(lambda b: b), qrow(lambda b: b)]
    args = [q, k, v, o, do, lse]
    whole = lambda rows: pl.BlockSpec((rows, LANES), lambda b, j, i: (0, b))
    out_specs = [whole(s), krow(lambda b: b), krow(lambda b: b)]
    out_shape = [jax.ShapeDtypeStruct((s, LANES * nb), F32), jax.ShapeDtypeStruct((t, LANES * nb), F32),
                 jax.ShapeDtypeStruct((t, LANES * nb), F32)]
    scratch = [pltpu.VMEM((bk, LANES), F32), pltpu.VMEM((bk, LANES), F32)]
    if bias:
        ckj = pl.BlockSpec((1, 8, bk), lambda b, p1, p2: (b, 0, step_at(p1, p2).kj))
        in_specs.append(ckj)
        args.append(ck)
        out_specs += [ckj, whole(s)]
        out_shape += [jax.ShapeDtypeStruct((nb, 8, t), F32), jax.ShapeDtypeStruct((s, LANES * nb), F32)]
    if rope:
        in_specs += [qrow(lambda b: qrc), krow(lambda b: 0)]
        args += [qr, kr]
        out_specs += [pl.BlockSpec((s, LANES), lambda b, j, i: (0, 0)), pl.BlockSpec((t, LANES), lambda b, j, i: (0, 0))]
        out_shape += [jax.ShapeDtypeStruct((s, LANES), F32), jax.ShapeDtypeStruct((t, LANES), F32)]
    if bias:
        scratch.append(pltpu.VMEM((8, bk), F32))
    if comm is not None:
        in_specs += [ANY] * len(comm.ins)
        args += comm.ins
        out_specs += [ANY] * len(comm.out_shapes)
        out_shape += comm.out_shapes
        scratch += _sems(comm.n_sems, comm.n_sems)
    res = pl.pallas_call(body, name=name, grid=(nb, n1, n2), in_specs=in_specs, out_specs=out_specs,
                         out_shape=out_shape, scratch_shapes=scratch,
                         compiler_params=_params('arbitrary', 'arbitrary', 'arbitrary'))(*args)
    return res if comm is None else (*res[:n_out], res[n_out:])


def _gla_chunk(la_c, k_c):
    r = lax.broadcasted_iota(jnp.int32, (CHUNK, CHUNK), 0)
    c = lax.broadcasted_iota(jnp.int32, (CHUNK, CHUNK), 1)
    tri = jnp.where(c <= r, 1.0, 0.0).astype(BF16)
    cum = _tri_dot(tri, la_c)
    end = jnp.sum(la_c, axis=0, keepdims=True)
    dec = jnp.exp(end - cum)
    return dec, k_c * dec, jnp.exp(end)


GLA_PAIRS = GLA_HEADS // 2


def _gla_fwd(z, la, *, qc, kc, vc, name, blk=512):
    s = z.shape[0]
    bs = min(blk, s)
    ncb = bs // CHUNK
    nblk = s // bs

    def body(q_ref, k_ref, va_ref, vb_ref, la_ref, o_ref, st_ref, st):
        @pl.when(pl.program_id(1) == 0)
        def _():
            st[...] = jnp.zeros_like(st)

        heads, _ = _lane_masks(2, 0, False)
        v_refs = (va_ref, vb_ref)
        for c in range(ncb):
            sl = pl.ds(c * CHUNK, CHUNK)
            _, kf, a = _gla_chunk(la_ref[sl, :], k_ref[sl, :])
            qs = q_ref[sl, :] * (GLA_DK ** -0.5)
            for hh in range(2):
                ut = _dot(v_refs[hh][sl, :].astype(BF16), _sel(heads[hh], kf).astype(BF16), TN)
                new = a * st[hh] + ut
                st[hh] = new
                st_ref[0, c, hh] = new
                o_ref[sl, hh * GLA_DV:(hh + 1) * GLA_DV] = _dot(_sel(heads[hh], qs).astype(BF16), new.astype(BF16), NT)

    col = lambda c0, m=1: pl.BlockSpec((bs, LANES), lambda b, i: (i, c0 + m * b))
    return pl.pallas_call(
        body, name=name, grid=(GLA_PAIRS, nblk),
        in_specs=[col(qc), col(kc), col(vc, 2), col(vc + 1, 2), col(0)],
        out_specs=[pl.BlockSpec((bs, 2 * GLA_DV), lambda b, i: (i, b)),
                   pl.BlockSpec((1, ncb, 2, GLA_DV, LANES), lambda b, i: (b, i, 0, 0, 0))],
        out_shape=[jax.ShapeDtypeStruct((s, GLA_HEADS * GLA_DV), F32),
                   jax.ShapeDtypeStruct((GLA_PAIRS, s // CHUNK, 2, GLA_DV, LANES), F32)],
        scratch_shapes=[pltpu.VMEM((2, GLA_DV, LANES), F32)],
        compiler_params=_params('arbitrary', 'arbitrary'))(z, z, z, z, la)


def _gla_bwd(z, la, st_all, st_prev, do, *, qc, kc, vc, name, blk=512):
    s = z.shape[0]
    bs = min(blk, s)
    ncb = bs // CHUNK
    nblk = s // bs

    def body(q_ref, k_ref, va_ref, vb_ref, la_ref, st_ref, sp_ref, do_ref, dq_ref, dk_ref, dv_ref, dla_ref, ga):
        @pl.when(pl.program_id(1) == 0)
        def _():
            ga[...] = jnp.zeros_like(ga)

        r = lax.broadcasted_iota(jnp.int32, (CHUNK, CHUNK), 0)
        cc = lax.broadcasted_iota(jnp.int32, (CHUNK, CHUNK), 1)
        tri_rev = jnp.where(cc >= r, 1.0, 0.0).astype(BF16)
        heads, _ = _lane_masks(2, 0, False)
        v_refs = (va_ref, vb_ref)
        for c in reversed(range(ncb)):
            sl = pl.ds(c * CHUNK, CHUNK)
            dec, kf, a = _gla_chunk(la_ref[sl, :], k_ref[sl, :])
            qs = q_ref[sl, :] * (GLA_DK ** -0.5)
            dq2 = jnp.zeros((CHUNK, LANES), F32)
            dkd = jnp.zeros((CHUNK, LANES), F32)
            da = jnp.zeros((1, LANES), F32)
            for hh in range(2):
                hv = slice(hh * GLA_DV, (hh + 1) * GLA_DV)
                dob = do_ref[sl, hv].astype(BF16)
                g = _dot(dob, _sel(heads[hh], qs).astype(BF16), TN) + ga[hh]
                gb = g.astype(BF16)
                dq2 = dq2 + _dot(dob, st_ref[0, c, hh].astype(BF16))
                dv_ref[sl, hv] = _dot(_sel(heads[hh], kf).astype(BF16), gb, NT)
                dkd = dkd + _dot(v_refs[hh][sl, :].astype(BF16), gb)
                da = da + jnp.sum(g * sp_ref[0, c, hh], axis=0, keepdims=True)
                ga[hh] = a * g
            dq_ref[sl, :] = (GLA_DK ** -0.5) * dq2
            dk_ref[sl, :] = dkd * dec
            e = dkd * kf
            dend = jnp.sum(e, axis=0, keepdims=True) + da * a
            dla_ref[sl, :] = dend - _tri_dot(tri_rev, e)

    rev = lambda i: nblk - 1 - i
    col = lambda c0, m=1: pl.BlockSpec((bs, LANES), lambda b, i: (rev(i), c0 + m * b))
    wide = pl.BlockSpec((bs, 2 * GLA_DV), lambda b, i: (rev(i), b))
    stspec = pl.BlockSpec((1, ncb, 2, GLA_DV, LANES), lambda b, i: (b, rev(i), 0, 0, 0))
    return pl.pallas_call(
        body, name=name, grid=(GLA_PAIRS, nblk),
        in_specs=[col(qc), col(kc), col(vc, 2), col(vc + 1, 2), col(0), stspec, stspec, wide],
        out_specs=[col(0), col(0), wide, col(0)],
        out_shape=[jax.ShapeDtypeStruct((s, GLA_HEADS * GLA_DK), F32), jax.ShapeDtypeStruct((s, GLA_HEADS * GLA_DK), F32),
                   jax.ShapeDtypeStruct((s, GLA_HEADS * GLA_DV), F32), jax.ShapeDtypeStruct((s, GLA_HEADS * GLA_DK), F32)],
        scratch_shapes=[pltpu.VMEM((2, GLA_DV, LANES), F32)],
        compiler_params=_params('arbitrary', 'arbitrary'))(z, z, z, z, la, st_all, st_prev, do)


def _place():
    return lax.axis_index('x'), lax.axis_index('y'), lax.axis_index('c')


ANY = pl.BlockSpec(memory_space=pl.ANY)


def _all_gather8(blk, *, name):
    m, n = blk.shape

    def body(x_ref, out_ref, send_sems, recv_sems, local_sem):
        x, y, c = _place()
        me, sibling = (x, y, c), (x, y, 1 - c)
        chips = [(1 - x, y), (x, 1 - y), (1 - x, 1 - y)]

        def slot(px, py, pc):
            return out_ref.at[4 * px + 2 * py + pc]

        def copy(q, block, to, src=None):
            return pltpu.make_async_remote_copy(
                src_ref=slot(*block) if src is None else src, dst_ref=slot(*block), send_sem=send_sems.at[q],
                recv_sem=recv_sems.at[q], device_id=to, device_id_type=MESH)

        mine = pltpu.make_async_copy(x_ref, slot(*me), local_sem)
        mine.start()
        first = [copy(0, me, sibling, src=x_ref)]
        first += [copy(1 + q, me, (*chip, c), src=x_ref) for q, chip in enumerate(chips)]
        for cp in first:
            cp.start()
        passed = [copy(4 + q, (*chip, c), sibling) for q, chip in enumerate(chips)]
        for q, chip in enumerate(chips):
            copy(1 + q, (*chip, c), me).wait_recv()
            passed[q].start()
        copy(0, sibling, me).wait_recv()
        for q, chip in enumerate(chips):
            copy(4 + q, (*chip, 1 - c), me).wait_recv()
        for cp in first + passed:
            cp.wait_send()
        mine.wait()

    return pl.pallas_call(
        body, name=name, in_specs=[ANY], out_specs=ANY, out_shape=jax.ShapeDtypeStruct((N_DEV, m, n), blk.dtype),
        scratch_shapes=[pltpu.SemaphoreType.DMA((7,)), pltpu.SemaphoreType.DMA((7,)), pltpu.SemaphoreType.DMA(())],
    )(blk)


def _sems(*counts):
    return [pltpu.SemaphoreType.DMA((n,)) for n in counts]


class Comm(typing.NamedTuple):
    ins: list
    out_shapes: list
    n_sems: int
    start: typing.Callable
    finish: typing.Callable


def _remote(src, dst, send_sems, recv_sems, idx, to):
    return lambda: pltpu.make_async_remote_copy(src_ref=src, dst_ref=dst, send_sem=send_sems.at[idx],
                                                recv_sem=recv_sems.at[idx], device_id=to, device_id_type=MESH)


def _comm_from(copies, ins, out_shapes, n_sems):
    def start(*refs):
        for cp in copies(*refs)[0]:
            cp().start()

    def finish(*refs):
        sent, received = copies(*refs)
        for cp in received:
            cp().wait_recv()
        for cp in sent:
            cp().wait_send()

    return Comm(list(ins), list(out_shapes), n_sems, start, finish)


def _run_comm(comm, *, name, alias=False):
    n_in, n_out = len(comm.ins), len(comm.out_shapes)

    def body(*refs):
        ins, outs, sems = refs[:n_in], refs[n_in:n_in + n_out], refs[n_in + n_out:]
        comm.start(ins, outs, *sems)
        comm.finish(ins, outs, *sems)

    return pl.pallas_call(body, name=name, in_specs=[ANY] * n_in, out_specs=[ANY] * n_out, out_shape=comm.out_shapes,
                          input_output_aliases={q: q for q in range(n_in)} if alias else {},
                          scratch_shapes=_sems(comm.n_sems, comm.n_sems))(*comm.ins)


def _half(rows, c):
    h = rows // 2
    return pl.ds(pl.multiple_of(c * h, h), h)


def _gathered(ref, chip, rows, side):
    if not side:
        return ref.at[chip, rows]
    n = ref.shape[1] // N_CHIPS
    return ref.at[rows, pl.ds(pl.multiple_of(chip * n, n), n)]


def _gather_over_ici(ws, side):
    def copies(ins, outs, send_sems, recv_sems):
        x, y, c = _place()
        me_chip = 2 * x + y
        sent, received = [], []
        for q, w in enumerate(ws):
            half, every = _half(w.shape[0], c), pl.ds(0, w.shape[0])
            for k, (px, py) in enumerate([(1 - x, y), (x, 1 - y), (1 - x, 1 - y)]):
                sent.append(_remote(ins[q].at[half], _gathered(outs[q], me_chip, half, side[q]), send_sems, recv_sems,
                                    4 * q + k, (px, py, c)))
                slot = _gathered(outs[q], 2 * px + py, half, side[q])
                received.append(_remote(slot, slot, send_sems, recv_sems, 4 * q + k, (px, py, c)))
            whole = _remote(ins[q], _gathered(outs[q], me_chip, every, side[q]), send_sems, recv_sems, 4 * q + 3,
                            (x, y, 1 - c))
            sent.append(whole)
            received.append(whole)
        return sent, received

    shapes = [jax.ShapeDtypeStruct((w.shape[0], N_CHIPS * w.shape[1]) if sd else (N_CHIPS,) + w.shape, w.dtype)
              for w, sd in zip(ws, side)]
    return _comm_from(copies, ws, shapes, 4 * len(ws))


def _gather_over_d2d(parts, side):
    def copies(ins, outs, send_sems, recv_sems):
        x, y, c = _place()
        sent, received = [], []
        for q, w in enumerate(parts):
            rows = w.shape[0] if side[q] else w.shape[1]
            for k, (px, py) in enumerate([(1 - x, y), (x, 1 - y), (1 - x, 1 - y)]):
                mine = _gathered(outs[q], 2 * px + py, _half(rows, c), side[q])
                theirs = _gathered(outs[q], 2 * px + py, _half(rows, 1 - c), side[q])
                sent.append(_remote(mine, mine, send_sems, recv_sems, 3 * q + k, (x, y, 1 - c)))
                received.append(_remote(theirs, theirs, send_sems, recv_sems, 3 * q + k, (x, y, 1 - c)))
        return sent, received

    return _comm_from(copies, parts, [jax.ShapeDtypeStruct(w.shape, w.dtype) for w in parts], 3 * len(parts))


def _to_sibling(gs, *, name):
    n = len(gs)

    def body(*refs):
        ins, outs = refs[:n], refs[n:2 * n]
        send_sems, recv_sems = refs[2 * n:]
        x, y, c = _place()
        cps = [pltpu.make_async_remote_copy(
            src_ref=ins[q], dst_ref=outs[q], send_sem=send_sems.at[q], recv_sem=recv_sems.at[q],
            device_id=(x, y, 1 - c), device_id_type=MESH) for q in range(n)]
        for cp in cps:
            cp.start()
        for cp in cps:
            cp.wait()

    return pl.pallas_call(body, name=name, in_specs=[ANY] * n, out_specs=[ANY] * n,
                          out_shape=[jax.ShapeDtypeStruct(g.shape, g.dtype) for g in gs],
                          scratch_shapes=_sems(n, n))(*gs)


def _chip_exchange(ps):
    def copies(ins, outs, send_sems, recv_sems):
        x, y, c = _place()
        cps = [_remote(ins[q].at[2 * px + py], outs[q].at[k], send_sems, recv_sems, 3 * q + k, (px, py, c))
               for q in range(len(ps)) for k, (px, py) in enumerate([(1 - x, y), (x, 1 - y), (1 - x, 1 - y)])]
        return cps, cps

    return _comm_from(copies, ps, [jax.ShapeDtypeStruct((3,) + p.shape[1:], p.dtype) for p in ps], 3 * len(ps))


def _sum_chips(own, r, *, name, ts=256):
    k, n = own.shape
    ts = min(ts, k)

    def body(own_ref, r_ref, o_ref):
        f = lambda q: r_ref[q].astype(F32)
        o_ref[...] = ((own_ref[...].astype(F32) + f(0)) + f(1)) + f(2)

    return pl.pallas_call(
        body, name=name, grid=(k // ts,),
        in_specs=[pl.BlockSpec((ts, n), lambda i: (i, 0)), pl.BlockSpec((3, ts, n), lambda i: (0, i, 0))],
        out_specs=pl.BlockSpec((ts, n), lambda i: (i, 0)), out_shape=jax.ShapeDtypeStruct((k, n), F32),
        compiler_params=_params('arbitrary'))(own, r)


WIN_SHARD = N_IN // N_CHIPS
WIN_PAD = -(-WIN_SHARD // LANES) * LANES
GATE_WIRE_ROWS = 32


def _full_layer(sh, axis):
    _, k, n = sh.shape
    if axis == 2:
        return sh.transpose(1, 0, 2).reshape(k, N_CHIPS * n)
    return sh.reshape(N_CHIPS * k, n)


def _win_cols(wp, o, n):
    parts = []
    while n > 0:
        j, r = divmod(o, WIN_SHARD)
        take = min(n, WIN_SHARD - r)
        parts.append(wp[:, j * WIN_PAD + r:j * WIN_PAD + r + take])
        o, n = o + take, n - take
    return parts[0] if len(parts) == 1 else jnp.concatenate(parts, axis=1)


def _split_full(full, axis):
    if full.ndim == 3:
        return full
    k, n = full.shape
    if axis == 2:
        return jnp.stack([full[:, j * (n // N_CHIPS):(j + 1) * (n // N_CHIPS)] for j in range(N_CHIPS)])
    return full.reshape(N_CHIPS, k // N_CHIPS, n)


def _padc(a, w):
    return jnp.pad(a, ((0, 0), (0, w - a.shape[1])))


def _swap16(a):
    return jnp.concatenate([a[..., 16:32], a[..., 0:16]], axis=-1)


B_GR, B_GQ, B_GK, B_GV, B_MQ, B_MKR, B_MKRS, B_FF, B_GLOW, B_MKV, B_END = (
    0, 512, 768, 1024, 1536, 1792, 1920, 2048, 2176, 2304, 2432)
B_W = 2560
O_FQ, O_FF, O_GQ, O_GLOW, O_GR, O_MQ, O_MKV, O_MKR, O_ZG = 0, 768, 772, 1796, 1812, 2324, 2580, 2708, 2740


def _repack_layer_weights(w):
    wi = functools.partial(_win_cols, w['w_in'])
    out = dict(w)
    out['in_a'] = jnp.concatenate([wi(O_FQ, 256) * FOX_SCALE, wi(O_FQ + 256, 512)], axis=1)
    kr = wi(O_MKR, 32)
    out['in_b'] = jnp.concatenate([
        wi(O_GR, 512), wi(O_GQ, 1024), wi(O_MQ, 256), jnp.tile(kr, (1, MLA_HEADS)), jnp.tile(_swap16(kr), (1, MLA_HEADS)),
        _padc(wi(O_FF, 4), 128), _padc(wi(O_GLOW, 16), 128), wi(O_MKV, 128),
        jnp.zeros((D_MODEL, B_W - B_END), kr.dtype)], axis=1)
    out['in_c'] = wi(O_ZG, 3072)
    uq = w['w_mla_uq'].reshape(MLA_Q_RANK, MLA_HEADS, MLA_NOPE + MLA_ROPE)
    rope = uq[:, :, MLA_NOPE:]
    out['uq'] = jnp.concatenate([uq[:, :, :MLA_NOPE].reshape(MLA_Q_RANK, -1), rope.reshape(MLA_Q_RANK, -1),
                                 _swap16(rope).reshape(MLA_Q_RANK, -1)], axis=1)
    ukv = w['w_mla_ukv'].reshape(MLA_KV_RANK, MLA_HEADS, MLA_NOPE + MLA_VD)
    out['ukv'] = jnp.concatenate([ukv[:, :, :MLA_NOPE].reshape(MLA_KV_RANK, -1),
                                  ukv[:, :, MLA_NOPE:].reshape(MLA_KV_RANK, -1)], axis=1)
    out['gate'] = jnp.pad(w['w_gla_gate'], ((0, 128 - GLA_RANK), (0, 0)))
    return out


def _unpack_layer_grads(g):
    a, b, c = g['in_a'], g['in_b'], g['in_c']
    fold = lambda o: sum(b[:, o + MLA_ROPE * q:o + MLA_ROPE * (q + 1)] for q in range(MLA_HEADS))
    kr = fold(B_MKR) + _swap16(fold(B_MKRS))
    pieces = [(a[:, :256] * FOX_SCALE, 0, 256), (a, 256, 512), (b, B_FF, 4), (b, B_GQ, 1024), (b, B_GLOW, 16),
              (b, B_GR, 512), (b, B_MQ, 256), (b, B_MKV, 128), (kr, 0, 32), (c, 0, 3072)]
    shards = []
    for j in range(N_CHIPS):
        lo, hi, cut, at = j * WIN_SHARD, (j + 1) * WIN_SHARD, [], 0
        for arr, first, width in pieces:
            l, h = max(lo, at), min(hi, at + width)
            if l < h:
                cut.append(arr[:, first + l - at:first + h - at])
            at += width
        shards.append(jnp.concatenate(cut, axis=1))
    w_in = jnp.stack(shards)
    uq = g['uq']
    nope = uq[:, :256].reshape(MLA_Q_RANK, MLA_HEADS, MLA_NOPE)
    rope = (uq[:, 256:384].reshape(MLA_Q_RANK, MLA_HEADS, MLA_ROPE)
            + _swap16(uq[:, 384:512].reshape(MLA_Q_RANK, MLA_HEADS, MLA_ROPE)))
    w_uq = jnp.concatenate([nope, rope], axis=2).reshape(MLA_Q_RANK, -1)
    ukv = g['ukv']
    w_ukv = jnp.concatenate([ukv[:, :256].reshape(MLA_KV_RANK, MLA_HEADS, MLA_NOPE),
                             ukv[:, 256:].reshape(MLA_KV_RANK, MLA_HEADS, MLA_VD)], axis=2).reshape(MLA_KV_RANK, -1)
    out = {'w_in': w_in, 'w_mla_uq': w_uq, 'w_mla_ukv': w_ukv, 'w_gla_gate': g['gate'][:GLA_RANK]}
    for nm in ('w_up_fox', 'w_up_gla', 'w_up_mla', 'w_out', 'w_xq', 'w_xkv', 'w_xo', 'w_mlp1', 'w_mlp2'):
        out[nm] = g[nm]
    return out


def _rope_tables(s):
    half = MLA_ROPE // 2
    inv = ROPE_BASE ** (-jnp.arange(half, dtype=F32) / half)
    ang = jnp.arange(s).astype(F32)[:, None] * inv[None, :]
    cos, sin = jnp.cos(ang), jnp.sin(ang)
    c1 = jnp.concatenate([cos, cos], axis=1)
    s1 = jnp.concatenate([-sin, sin], axis=1)
    return jnp.tile(c1, (1, MLA_HEADS)), jnp.tile(s1, (1, MLA_HEADS))


def _rms_bwd(x, dh, g):
    r = lax.rsqrt(jnp.mean(x * x, axis=-1, keepdims=True) + EPS)
    xh = x * r
    gd = dh * g
    return r * (gd - xh * jnp.mean(gd * xh, axis=-1, keepdims=True)), dh * xh


def _norm_bwd_epilogue(dh, x, dres, g):
    dx, dg = _rms_bwd(x, dh, g)
    return dres + dx, dg


def _norm_bwd_call(x, dh, g, dres, name):
    w = x.width if isinstance(x, Cols) else x.shape[1]

    def with_res(xv, dv, rv, gv):
        dx, dg = _rms_bwd(xv, dv.astype(F32), gv)
        return rv + dx, dg

    def plain(xv, dv, gv):
        return _rms_bwd(xv, dv.astype(F32), gv)

    if dres is None:
        return _rowwise(plain, [x, dh], [g], [(w, F32)], [w], name=name)
    return _rowwise(with_res, [x, dh, dres], [g], [(w, F32)], [w], name=name)


def _gla_out_fwd(oraw, gr, g_out):
    outs = []
    for hh in range(GLA_HEADS):
        sl = slice(hh * GLA_DV, (hh + 1) * GLA_DV)
        oh = oraw[:, sl]
        n = oh * lax.rsqrt(jnp.mean(oh * oh, axis=-1, keepdims=True) + EPS) * g_out
        r = gr[:, sl]
        outs.append(n * (r * _sig(r)))
    return (jnp.concatenate(outs, axis=1),)


def _gla_out_bwd(oraw, gr, dout, g_out):
    d_o, d_r, dg = [], [], 0.0
    for hh in range(GLA_HEADS):
        sl = slice(hh * GLA_DV, (hh + 1) * GLA_DV)
        oh, r, do = oraw[:, sl], gr[:, sl], dout[:, sl].astype(F32)
        rs = lax.rsqrt(jnp.mean(oh * oh, axis=-1, keepdims=True) + EPS)
        sg = _sig(r)
        dn = do * (r * sg)
        d_r.append(do * (oh * rs * g_out) * (sg + r * sg * (1.0 - sg)))
        dx, dgh = _rms_bwd(oh, dn, g_out)
        d_o.append(dx)
        dg = dg + dgh
    return jnp.concatenate(d_o, axis=1), jnp.concatenate(d_r, axis=1), dg


def _adam(w, g, m, v):
    m = ADAM_B1 * m + (1.0 - ADAM_B1) * g
    v = ADAM_B2 * v + (1.0 - ADAM_B2) * (g * g)
    m_hat = m / (1.0 - ADAM_B1 ** ADAM_STEP)
    v_hat = v / (1.0 - ADAM_B2 ** ADAM_STEP)
    return -ADAM_LR * (m_hat / (jnp.sqrt(v_hat) + ADAM_EPS) + ADAM_WD * w), m, v


def _layer_fwd(x, mem, w, p, tabs, tag, carry_fox=None, after_fox=None, carry_mla=None):
    c4, s4 = tabs
    sv = {'x0': x}
    nm = lambda t: f'{t}_{tag}'
    za, h = _mm(x, w['in_a'], mode='nn', out_dtype=BF16, norm_g=p['g_mix'], emit_norm=True, name=nm('in_a'))
    zb = _mm(h, w['in_b'], mode='nn', out_dtype=F32, name=nm('in_b'))
    zc = _mm(h, w['in_c'], mode='nn', out_dtype=F32, name=nm('in_c'))
    sv.update(h=h, zc=zc)
    ff = Cols(zb, 128, B_FF // 128)
    (lf,) = _rowwise(lambda f, b: (_logsig(f + b),), [ff], [p['b_fox']], [(128, F32)], name=nm('fox_lf'))
    cum = _cumsum_rows(lf, reverse=False, name=nm('fox_cum'))
    ckf = jnp.pad(cum[:, :FOX_HEADS].T.reshape(2, 2, x.shape[0]), ((0, 0), (0, 6), (0, 0)))
    fox = dict(qc=0, kc=2, vc=4, nb=2, g=2, mode='causal', ck=ckf)
    o_fox, lse_fox, *carried = _mattn_fwd(za, za, za, name=nm('fox_attn'), comm=carry_fox, **fox)
    if after_fox is not None:
        w = {**w, **after_fox(carried[0])}
    sv.update(ff=ff, za=za, fox=fox, o_fox=o_fox, lse_fox=lse_fox)
    glow = Cols(zb, 128, B_GLOW // 128)
    gr = Cols(zb, 512, B_GR // 512)

    def gate_fn(gl, wg, bg):
        return (_logsig(_dot(gl.astype(BF16), wg) + bg) / GLA_TAU,)

    (la,) = _rowwise(gate_fn, [glow], [w['gate'], p['b_gla']], [(256, F32)], name=nm('gla_gate'))
    gla = dict(qc=B_GQ // LANES, kc=B_GK // LANES, vc=B_GV // LANES)
    oraw, states = _gla_fwd(zb, la, name=nm('gla'), **gla)
    (o_gla,) = _rowwise(_gla_out_fwd, [oraw, gr], [p['g_gla_out']], [(512, BF16)], name=nm('gla_out'))
    sv.update(glow=glow, gr=gr, zb=zb, la=la, gla=gla, states=states, oraw=oraw, o_gla=o_gla)
    mq = Cols(zb, 256, B_MQ // 256)
    mkv = Cols(zb, 128, B_MKV // 128)
    mkr2 = Cols(zb, 256, B_MKR // 256)
    qp, cqn = _mm(mq, w['uq'], mode='nn', out_dtype=F32, norm_g=p['g_mla_q'], emit_norm=True, name=nm('mla_uq'))
    kvp, ckvn = _mm(mkv, w['ukv'], mode='nn', out_dtype=BF16, norm_g=p['g_mla_kv'], emit_norm=True,
                    name=nm('mla_ukv'))

    def rope_fn(qv, kr, c4v, s4v):
        q_rope = qv[:, 256:384] * c4v + qv[:, 384:512] * s4v
        q_scaled = jnp.concatenate([qv[:, 0:256], q_rope], axis=1) * MLA_SCALE
        return q_scaled, kr[:, 0:128] * c4v + kr[:, 128:256] * s4v

    qall, kr4 = _rowwise(rope_fn, [qp, mkr2, c4, s4], [], [(384, BF16), (128, BF16)], name=nm('rope'))
    mla = dict(qc=0, kc=0, vc=2, nb=2, g=2, dq_scale=MLA_SCALE, mode='chunk', qr=qall, qrc=2, kr=kr4)
    o_mla, lse_mla, *carried = _mattn_fwd(qall, kvp, kvp, name=nm('mla_attn'), comm=carry_mla, **mla)
    if carry_mla is not None:
        sv['carried_mla'] = carried[0]
    sv.update(mq=mq, mkv=mkv, cqn=cqn, ckvn=ckvn, qall=qall, kvp=kvp, mla=mla, o_mla=o_mla, lse_mla=lse_mla)
    of_m, om_m = o_fox, o_mla
    sv.update(of_m=of_m, om_m=om_m)
    b_br = p['b_branch']

    y = _gated_merge([of_m, o_gla, om_m], [w['w_up_fox'], w['w_up_gla'], w['w_up_mla']], zc, b_br, name=nm('up_merge'))
    add = lambda acc, res: res + acc
    x1 = _mm(y, w['w_out'], mode='nn', out_dtype=F32, name=nm('out'), epilogue=add, extras=[(x, *_mn())])
    sv.update(y=y, x1=x1)
    qx, hx = _mm(x1, w['w_xq'], mode='nn', out_dtype=BF16, norm_g=p['g_xa'], emit_norm=True, name=nm('xq'),
                 epilogue=lambda acc: acc * XA_SCALE)
    kvx, mn = _mm(mem, w['w_xkv'], mode='nn', out_dtype=BF16, norm_g=p['g_mem'], emit_norm=True, name=nm('xkv'))
    xa = dict(qc=0, kc=0, vc=4, nb=4, g=1, dq_scale=XA_SCALE, mode='full')
    ox_m, lse_x = _mattn_fwd(qx, kvx, kvx, name=nm('xa_attn'), **xa)
    x2 = _mm(ox_m, w['w_xo'], mode='nn', out_dtype=F32, name=nm('xo'), epilogue=add, extras=[(x1, *_mn())])
    sv.update(hx=hx, mn=mn, qx=qx, kvx=kvx, xa=xa, lse_x=lse_x, ox_m=ox_m, x2=x2)
    hpre, hm = _mm(x2, w['w_mlp1'], mode='nn', out_dtype=BF16, norm_g=p['g_mlp'], emit_norm=True, name=nm('mlp1'))
    relu2 = lambda t: jnp.square(jnp.maximum(t.astype(F32), 0.0))
    x3 = _mm(hpre, w['w_mlp2'], mode='nn', out_dtype=F32, name=nm('mlp2'), a_fn=relu2, epilogue=add,
             extras=[(x2, *_mn())])
    sv.update(hpre=hpre, hm=hm, w=w)
    return x3, sv


EARLY = ('w_mlp1', 'w_mlp2', 'w_xo', 'w_xq', 'w_xkv', 'w_out', 'w_up_fox', 'w_up_gla', 'w_up_mla')
LATE = ('w_in', 'w_gla_gate', 'w_mla_uq', 'w_mla_ukv')


def _layer_bwd(dx3, mem, w, p, tabs, sv, tag, carry_mla=None, early=None):
    c4, s4 = tabs
    nm = lambda t: f'{t}_{tag}'
    s = dx3.shape[0]
    gw, gs = {}, {}
    relu2 = lambda t: jnp.square(jnp.maximum(t.astype(F32), 0.0))
    gw['w_mlp2'] = _mm(sv['hpre'], dx3, mode='tn', out_dtype=F32, name=nm('d_mlp2'), a_fn=relu2)
    dact = lambda acc, hp: acc * (2.0 * jnp.maximum(hp.astype(F32), 0.0))
    dhpre = _mm(dx3, w['w_mlp2'], mode='nt', out_dtype=BF16, name=nm('d_act'), epilogue=dact,
                extras=[(sv['hpre'], *_mn())])
    gw['w_mlp1'] = _mm(sv['hm'], dhpre, mode='tn', out_dtype=F32, name=nm('d_mlp1'))
    dx2, gs['g_mlp'] = _mm(dhpre, w['w_mlp1'], mode='nt', out_dtype=F32, name=nm('d_hm'), epilogue=_norm_bwd_epilogue,
                           col_sums=True, full_rows=True,
                           extras=[(sv['x2'], *_mn()), (dx3, *_mn()), (p['g_mlp'], *_nvec())])
    gw['w_xo'] = _mm(sv['ox_m'], dx2, mode='tn', out_dtype=F32, name=nm('d_xo'))
    dox = _mm(dx2, w['w_xo'], mode='nt', out_dtype=BF16, name=nm('d_ox'))
    dqx_m, dkx, dvx = _mattn_bwd(sv['qx'], sv['kvx'], sv['kvx'], sv['ox_m'], dox, sv['lse_x'], name=nm('xa_bwd'),
                                 **sv['xa'])
    dkvx = jnp.concatenate([dkx, dvx], axis=1).astype(BF16)
    gw['w_xq'] = _mm(sv['hx'], dqx_m, mode='tn', out_dtype=F32, name=nm('d_xq'))
    dx1, gs['g_xa'] = _mm(dqx_m, w['w_xq'], mode='nt', out_dtype=F32, name=nm('d_hx'), epilogue=_norm_bwd_epilogue,
                          col_sums=True, full_rows=True,
                          extras=[(sv['x1'], *_mn()), (dx2, *_mn()), (p['g_xa'], *_nvec())])
    gw['w_xkv'] = _mm(sv['mn'], dkvx, mode='tn', out_dtype=F32, name=nm('d_xkv'))
    dmn = _mm(dkvx, w['w_xkv'], mode='nt', out_dtype=F32, name=nm('d_mn'))
    _, gs['g_mem'] = _norm_bwd_call(mem, dmn, p['g_mem'], None, nm('d_norm_mem'))
    gw['w_out'] = _mm(sv['y'], dx1, mode='tn', out_dtype=F32, name=nm('d_out'))
    dy = _mm(dx1, w['w_out'], mode='nt', out_dtype=BF16, name=nm('d_y'))
    zc, b_br = sv['zc'], p['b_branch']

    branches = (('w_up_fox', sv['of_m'], BF16), ('w_up_gla', sv['o_gla'], F32), ('w_up_mla', sv['om_m'], BF16))
    du, do_br, dzc, gs['b_branch'] = _gated_merge_bwd(dy, zc, b_br, [o for _, o, _ in branches],
                                                      [w[wn] for wn, _, _ in branches], [dt for _, _, dt in branches],
                                                      name=nm('d_merge'))
    for q, (wn, o_m, _) in enumerate(branches):
        gw[wn] = _mm(o_m, du[q], mode='tn', out_dtype=F32, name=nm(f'd_up{q}'))
    za = sv['za']
    carry_fox = None if early is None else early({nm_: gw[nm_] for nm_ in EARLY})
    dfq, dfk, dfv, dck, dcq, *carried_fox = _mattn_bwd(za, za, za, sv['o_fox'], do_br[0], sv['lse_fox'],
                                                       name=nm('fox_bwd'), comm=carry_fox, **sv['fox'])
    dcum = _padc(dck[:, :2, :].reshape(FOX_HEADS, s).T + dcq.reshape(s, 2, LANES)[:, :, :2].reshape(s, FOX_HEADS), 128)
    dlf = _cumsum_rows(dcum, reverse=True, name=nm('fox_dcum'))

    def dff_fn(dl, f, b):
        d = dl * _sig(-(f + b))
        return d, d

    dff, db_fox = _rowwise(dff_fn, [dlf, sv['ff']], [p['b_fox']], [(128, F32)], [128], name=nm('fox_dff'))
    gs['b_fox'] = db_fox
    dza = jnp.concatenate([dfq, dfk, dfv], axis=1).astype(BF16)
    dqn, dkn, dvv, dq_rope, dk_rope, *carried_mla = _mattn_bwd(sv['qall'], sv['kvp'], sv['kvp'], sv['o_mla'], do_br[2],
                                                               sv['lse_mla'], name=nm('mla_bwd'), comm=carry_mla,
                                                               **sv['mla'])

    def drope_fn(dn, dq, dk, c4v, s4v):
        return jnp.concatenate([dn, dq * c4v, dq * s4v], axis=1), jnp.concatenate([dk * c4v, dk * s4v], axis=1)

    dqp, dmkr2 = _rowwise(drope_fn, [dqn, dq_rope, dk_rope, c4, s4], [], [(512, BF16), (256, BF16)], name=nm('d_rope'))
    dkvp = jnp.concatenate([dkn, dvv], axis=1).astype(BF16)
    gw['uq'] = _mm(sv['cqn'], dqp, mode='tn', out_dtype=F32, name=nm('d_uq'))
    dcqn = _mm(dqp, w['uq'], mode='nt', out_dtype=F32, name=nm('d_cqn'))
    gw['ukv'] = _mm(sv['ckvn'], dkvp, mode='tn', out_dtype=F32, name=nm('d_ukv'))
    dckvn = _mm(dkvp, w['ukv'], mode='nt', out_dtype=F32, name=nm('d_ckvn'))
    dmq, gs['g_mla_q'] = _norm_bwd_call(sv['mq'], dcqn, p['g_mla_q'], None, nm('d_norm_q'))
    dmkv, gs['g_mla_kv'] = _norm_bwd_call(sv['mkv'], dckvn, p['g_mla_kv'], None, nm('d_norm_kv'))
    doraw, dgr, gs['g_gla_out'] = _rowwise(_gla_out_bwd, [sv['oraw'], sv['gr'], do_br[1]], [p['g_gla_out']],
                                           [(512, F32), (512, BF16)], [128], name=nm('d_gla_out'))
    st = sv['states']
    st_prev = jnp.concatenate([jnp.zeros_like(st[:, :1]), st[:, :-1]], axis=1)
    dgq, dgk, dgv, dla = _gla_bwd(sv['zb'], sv['la'], st, st_prev, doraw, name=nm('gla_bwd'), **sv['gla'])

    def dgate_fn(dl, gl, wg, bg):
        pre = _dot(gl.astype(BF16), wg) + bg
        dpre = dl * (1.0 / GLA_TAU) * _sig(-pre)
        return dpre, _dot(dpre.astype(BF16), wg, NT), dpre

    dpre, dglow, gs['b_gla'] = _rowwise(dgate_fn, [dla, sv['glow']], [w['gate'], p['b_gla']],
                                        [(256, BF16), (128, BF16)], [256], name=nm('d_gla_gate'))
    gw['gate'] = _mm(sv['glow'], dpre, mode='tn', out_dtype=F32, name=nm('d_wgate'))
    bf = lambda t: t.astype(BF16)
    dzb = jnp.concatenate([dgr, bf(dgq), bf(dgk), bf(dgv), bf(dmq), dmkr2, bf(dff), dglow, bf(dmkv),
                           jnp.zeros((s, B_W - B_END), BF16)], axis=1)
    h = sv['h']
    gw['in_a'] = _mm(h, dza, mode='tn', out_dtype=F32, name=nm('d_in_a'))
    gw['in_b'] = _mm(h, dzb, mode='tn', out_dtype=F32, name=nm('d_in_b'))
    gw['in_c'] = _mm(h, dzc, mode='tn', out_dtype=F32, name=nm('d_in_c'))
    add = lambda acc, prev: prev + acc
    dh = _mm(dza, w['in_a'], mode='nt', out_dtype=F32, name=nm('d_h_a'))
    dh = _mm(dzb, w['in_b'], mode='nt', out_dtype=F32, name=nm('d_h_b'), epilogue=add, extras=[(dh, *_mn())])
    dx0, gs['g_mix'] = _mm(dzc, w['in_c'], mode='nt', out_dtype=F32, name=nm('d_h_c'), col_sums=True, full_rows=True,
                           epilogue=lambda acc, prev, xv, rv, gv: _norm_bwd_epilogue(prev + acc, xv, rv, gv),
                           extras=[(dh, *_mn()), (sv['x0'], *_mn()), (dx1, *_mn()), (p['g_mix'], *_nvec())])
    return dx0, gw, gs, (carried_mla or [None])[0], (carried_fox or [None])[0]


def _loss_head(x, target, g_final):
    d = x.shape[1]

    def fn(xv, tv, gv):
        r = lax.rsqrt(jnp.mean(xv * xv, axis=-1, keepdims=True) + EPS)
        xh = xv * r
        e = xh * gv - tv
        dy = e * (1.0 / d)
        gd = dy * gv
        dx = r * (gd - xh * jnp.mean(gd * xh, axis=-1, keepdims=True))
        row_loss = 0.5 * jnp.mean(e * e, axis=-1, keepdims=True)
        return dx, dy * xh, jnp.broadcast_to(row_loss, (xv.shape[0], LANES))

    return _rowwise(fn, [x, target], [g_final], [(d, F32)], [d, LANES], name='loss_head', ts=512)


def _step(args):
    shapes = {nm: args[nm].shape for nm in ORDER}
    x, mem, target = args['x'][0], args['mem'][0], args['loss_target'][0]
    s = x.shape[0]

    def wire(nm, l):
        w = args[nm][l].astype(BF16)
        if nm == 'w_in':
            w = jnp.pad(w, ((0, 0), (0, WIN_PAD - WIN_SHARD)))
        if nm == 'w_gla_gate':
            w = jnp.pad(w, ((0, GATE_WIRE_ROWS - GLA_RANK), (0, 0)))
        return w

    axis_of = dict(BIG)
    names = tuple(nm for nm, _ in BIG)
    wires = lambda l, nms: [wire(nm, l) for nm in nms]
    width = lambda nm: WIN_PAD if nm == 'w_in' else args[nm].shape[2]
    side_by_side = lambda nms: [axis_of[nm] == 2 and width(nm) % LANES == 0 for nm in nms]
    over_ici = lambda l, nms: _gather_over_ici(wires(l, nms), side_by_side(nms))

    def whole(parts, nms, tag):
        side = side_by_side(nms)
        parts = _run_comm(_gather_over_d2d(parts, side), name=f'gather_d2d_{tag}', alias=True)
        full = {nm: p if sd else _full_layer(p, axis_of[nm]) for nm, p, sd in zip(nms, parts, side)}
        if 'w_gla_gate' in full:
            full['w_gla_gate'] = full['w_gla_gate'][:GLA_RANK]
        return full

    tabs = _rope_tables(s)
    layers_p = []
    for l in range(DEPTH):
        layers_p.append({
            'g_mix': args['g_mix'][l][None], 'b_fox': _padc(args['b_fox_forget'][l][None], 128),
            'b_gla': args['b_gla_gate'][l][None], 'g_gla_out': args['g_gla_out'][l][None],
            'g_mla_q': args['g_mla_q'][l][None], 'g_mla_kv': args['g_mla_kv'][l][None],
            'b_branch': args['b_branch_gate'][l][None], 'g_xa': args['g_xa'][l][None],
            'g_mem': args['g_mem'][l][None], 'g_mlp': args['g_mlp'][l][None]})

    first = _run_comm(over_ici(0, LATE), name='gather_ici_first_l0')
    w_now = _repack_layer_weights(whole(first, LATE, 'first_l0'))
    saved = []
    xl = x
    for l in range(DEPTH):
        carry_fox = over_ici(0, EARLY) if l == 0 else None
        after_fox = (lambda parts: whole(parts, EARLY, 'rest_l0')) if l == 0 else None
        carry_mla = over_ici(l + 1, names) if l + 1 < DEPTH else None
        xl, sv = _layer_fwd(xl, mem, w_now, layers_p[l], tabs, f'l{l}', carry_fox=carry_fox, after_fox=after_fox,
                            carry_mla=carry_mla)
        saved.append(sv)
        if carry_mla is not None:
            w_now = _repack_layer_weights(whole(sv.pop('carried_mla'), names, f'l{l + 1}'))
    dx, dg_final, loss_lanes = _loss_head(xl, target, args['g_final'][None])
    cidx = lax.axis_index('c')
    chip = 2 * lax.axis_index('x') + lax.axis_index('y')

    def pair_sums(gw, nms, tag):
        mine, theirs = [], []
        for nm in nms:
            shards = _split_full(gw[nm], axis_of[nm]).astype(BF16)
            h = shards.shape[1] // 2
            mine.append(lax.dynamic_slice_in_dim(shards, cidx * h, h, axis=1))
            theirs.append(lax.dynamic_slice_in_dim(shards, (1 - cidx) * h, h, axis=1))
        got = _to_sibling(theirs, name=f'grads_swap_{tag}')
        pairs = []
        for nm, a, b in zip(nms, mine, got):
            _, h, n = a.shape
            (p,) = _rowwise(lambda u, v: (u.astype(F32) + v.astype(F32),),
                            [a.reshape(N_CHIPS * h, n), b.reshape(N_CHIPS * h, n)], [], [(n, BF16)],
                            name=f'pair_sum_{nm}_{tag}')
            pairs.append(p.reshape(N_CHIPS, h, n))
        return pairs

    def finish(pairs, from_chips, nms, tag):
        own = [lax.dynamic_index_in_dim(p, chip, axis=0, keepdims=False) for p in pairs]
        mine = [_sum_chips(o, r, name=f'chip_sum_{nm}_{tag}') for nm, o, r in zip(nms, own, from_chips)]
        theirs = _to_sibling(mine, name=f'grads_join_{tag}')
        return {nm: jnp.where(cidx == 0, jnp.concatenate([a, b]), jnp.concatenate([b, a]))
                for nm, a, b in zip(nms, mine, theirs)}

    gs_layers, done = [None] * DEPTH, [{} for _ in range(DEPTH)]
    above = None
    for l in reversed(range(DEPTH)):
        lowest, early_pairs = l == 0, []

        def early(gw_early, l=l, early_pairs=early_pairs):
            early_pairs.extend(pair_sums(gw_early, EARLY, f'early_l{l}'))
            return _chip_exchange(early_pairs)

        carry_mla = None if above is None else _chip_exchange(above[1])
        dx, gw, gs_layers[l], got_mla, got_fox = _layer_bwd(
            dx, mem, saved[l]['w'], layers_p[l], tabs, saved[l], f'l{l}', carry_mla=carry_mla,
            early=early if lowest else None)
        if above is not None:
            done[above[0]].update(finish(above[1], got_mla, names, f'l{above[0]}'))
        grads = _unpack_layer_grads(gw)
        if lowest:
            done[l].update(finish(early_pairs, got_fox, EARLY, f'early_l{l}'))
            late_pairs = pair_sums(grads, LATE, f'late_l{l}')
            from_late = _run_comm(_chip_exchange(late_pairs), name=f'grads_exchange_late_l{l}')
            done[l].update(finish(late_pairs, from_late, LATE, f'late_l{l}'))
        else:
            above = (l, pair_sums(grads, names, f'l{l}'))
    grad_x = dx[None]
    gshard = {nm: jnp.stack([done[l][nm] for l in range(DEPTH)]) for nm in names}

    small_g = []
    for nm, key in (('g_mix', 'g_mix'), ('b_fox_forget', 'b_fox'), ('b_gla_gate', 'b_gla'),
                    ('g_gla_out', 'g_gla_out'), ('g_mla_q', 'g_mla_q'), ('g_mla_kv', 'g_mla_kv'),
                    ('b_branch_gate', 'b_branch'), ('g_xa', 'g_xa'), ('g_mem', 'g_mem'), ('g_mlp', 'g_mlp')):
        width = shapes[nm][1]
        small_g.append(jnp.concatenate([gs_layers[l][key][0, :width] for l in range(DEPTH)]))
    small_g.append(dg_final[0])
    small_g.append(loss_lanes[0, :1])
    flat = jnp.concatenate(small_g)
    n_small = flat.shape[0]
    srows = -(-n_small // (8 * LANES)) * 8
    pad = lambda v: jnp.pad(v, (0, srows * LANES - v.shape[0])).reshape(srows, LANES)
    all_small = _all_gather8(pad(flat), name='gather_small')
    sw, sm, svv = (pad(jnp.concatenate([args[pre + nm].reshape(-1) for nm in SMALL] + [jnp.zeros((1,), F32)]))
                   for pre in ('', 'm_', 'v_'))

    def small_body(g_ref, w_ref, m_ref, v_ref, go_ref, d_ref, mo_ref, vo_ref):
        g = g_ref[0]
        for q in range(1, N_DEV):
            g = g + g_ref[q]
        go_ref[...] = g
        d_ref[...], mo_ref[...], vo_ref[...] = _adam(w_ref[...], g, m_ref[...], v_ref[...])

    sg, sd, snm, snv = pl.pallas_call(
        small_body, name='small_sum_adam', out_shape=[jax.ShapeDtypeStruct((srows, LANES), F32)] * 4,
        compiler_params=pltpu.CompilerParams(vmem_limit_bytes=VMEM_LIMIT))(all_small, sw, sm, svv)

    def unsmall(buf):
        v, out, off = buf.reshape(-1), {}, 0
        for nm in SMALL:
            nel = math.prod(shapes[nm])
            out[nm] = v[off:off + nel].reshape(shapes[nm])
            off += nel
        return out, v[off]

    res = {}
    (res['grad'], loss), (res['delta'], _), (res['m'], _), (res['v'], _) = (unsmall(t) for t in (sg, sd, snm, snv))

    for nm, _ in BIG:
        shp = args[nm].shape
        view = lambda t: t.reshape(shp[0] * shp[1], shp[2])
        d, m2, v2 = _rowwise(_adam, [view(args[nm]), view(gshard[nm]), view(args['m_' + nm]), view(args['v_' + nm])],
                             [], [(shp[2], F32)] * 3, name=f'adam_{nm}',
                             ts=512 if shp[2] <= 1024 else 256)
        res['grad'][nm], res['delta'][nm], res['m'][nm], res['v'][nm] = (
            gshard[nm], d.reshape(shp), m2.reshape(shp), v2.reshape(shp))

    return (loss, grad_x, *[res['grad'][nm] for nm in ORDER], *[res['delta'][nm] for nm in ORDER],
            *[res['m'][nm] for nm in ORDER], *[res['v'][nm] for nm in ORDER])


def kernel(x, mem, g_mix, w_in, b_fox_forget, w_gla_gate, b_gla_gate, g_gla_out, g_mla_q, w_mla_uq, g_mla_kv, w_mla_ukv, b_branch_gate, w_up_fox, w_up_gla, w_up_mla, w_out, g_xa, g_mem, w_xq, w_xkv, w_xo, g_mlp, w_mlp1, w_mlp2, g_final, loss_target, m_g_mix, m_w_in, m_b_fox_forget, m_w_gla_gate, m_b_gla_gate, m_g_gla_out, m_g_mla_q, m_w_mla_uq, m_g_mla_kv, m_w_mla_ukv, m_b_branch_gate, m_w_up_fox, m_w_up_gla, m_w_up_mla, m_w_out, m_g_xa, m_g_mem, m_w_xq, m_w_xkv, m_w_xo, m_g_mlp, m_w_mlp1, m_w_mlp2, m_g_final, v_g_mix, v_w_in, v_b_fox_forget, v_w_gla_gate, v_b_gla_gate, v_g_gla_out, v_g_mla_q, v_w_mla_uq, v_g_mla_kv, v_w_mla_ukv, v_b_branch_gate, v_w_up_fox, v_w_up_gla, v_w_up_mla, v_w_out, v_g_xa, v_g_mem, v_w_xq, v_w_xkv, v_w_xo, v_g_mlp, v_w_mlp1, v_w_mlp2, v_g_final):
    return _step(dict(locals()))
```

```python
import functools
import math
import typing

import jax
import jax.numpy as jnp
from jax import lax
from jax.experimental import pallas as pl
from jax.experimental.pallas import tpu as pltpu

F32 = jnp.float32
BF16 = jnp.bfloat16
MESH = pl.DeviceIdType.MESH

D_MODEL = 1024
DEPTH = 2
CHUNK = 64
EPS = 1e-6
FOX_HEADS, FOX_HD = 4, 64
GLA_HEADS, GLA_DK, GLA_DV, GLA_RANK, GLA_TAU = 4, 64, 128, 16, 16.0
MLA_HEADS, MLA_Q_RANK, MLA_KV_RANK, MLA_NOPE, MLA_ROPE, MLA_VD = 4, 256, 128, 64, 32, 64
ROPE_BASE = 10000.0
XA_HEADS, XA_HD = 4, 128
D_FF = 4 * D_MODEL
IN_SIZES = (256, 256, 256, 4, 256, 256, 512, 16, 512, 256, 128, 32, 3072)
N_IN = sum(IN_SIZES)

ADAM_LR, ADAM_B1, ADAM_B2, ADAM_EPS, ADAM_WD, ADAM_STEP = 0.001, 0.9, 0.999, 1e-08, 0.01, 10

N_CHIPS = 4
N_DEV = 8
LANES = 128
VMEM_LIMIT = 48 * 1024 * 1024
MASK_VALUE = -1e30

BIG = (('w_in', 2), ('w_gla_gate', 2), ('w_mla_uq', 2), ('w_mla_ukv', 2), ('w_up_fox', 2), ('w_up_gla', 2),
       ('w_up_mla', 2), ('w_out', 1), ('w_xq', 1), ('w_xkv', 1), ('w_xo', 2), ('w_mlp1', 2), ('w_mlp2', 1))
SMALL = ('g_mix', 'b_fox_forget', 'b_gla_gate', 'g_gla_out', 'g_mla_q', 'g_mla_kv', 'b_branch_gate',
         'g_xa', 'g_mem', 'g_mlp', 'g_final')
ORDER = ('g_mix', 'w_in', 'b_fox_forget', 'w_gla_gate', 'b_gla_gate', 'g_gla_out', 'g_mla_q', 'w_mla_uq',
         'g_mla_kv', 'w_mla_ukv', 'b_branch_gate', 'w_up_fox', 'w_up_gla', 'w_up_mla', 'w_out', 'g_xa', 'g_mem',
         'w_xq', 'w_xkv', 'w_xo', 'g_mlp', 'w_mlp1', 'w_mlp2', 'g_final')


def _params(*sem, extra_vmem=0):
    return pltpu.CompilerParams(dimension_semantics=sem, vmem_limit_bytes=VMEM_LIMIT + extra_vmem)


def _sig(x):
    return 1.0 / (1.0 + jnp.exp(-x))


def _logsig(x):
    return jnp.minimum(x, 0.0) - jnp.log(1.0 + jnp.exp(-jnp.abs(x)))


NN = (((1,), (0,)), ((), ()))
NT = (((1,), (1,)), ((), ()))
TN = (((0,), (0,)), ((), ()))


def _dot(a, b, dims=NN):
    return lax.dot_general(a, b, dims, preferred_element_type=F32)


class Cols(typing.NamedTuple):
    arr: jax.Array
    width: int
    blk: int


def _tri_dot(tri, x):
    hi = x.astype(BF16)
    r1 = x - hi.astype(F32)
    mid = r1.astype(BF16)
    lo = (r1 - mid.astype(F32)).astype(BF16)
    return _dot(tri, hi) + _dot(tri, mid) + _dot(tri, lo)


MM_TILES = ((1024, 1024), (1024, 512), (512, 1024), (512, 512), (256, 1024), (512, 256), (256, 512), (256, 256),
            (128, 1024), (128, 128))
MM_VMEM_BUDGET = 38 * 1024 * 1024
MM_VMEM_EXTRA = 8 * 1024 * 1024


def _mm_tiles(m, n, k, a_bytes, b_bytes, out_bytes, ex_bytes, has_norm, emit_norm, has_fn, full_rows):
    for tm, tn in MM_TILES:
        tm, tn = min(tm, m), min(tn, n)
        if m % tm or n % tn or (full_rows and tn != n):
            continue
        blocks = tm * k * a_bytes + k * tn * b_bytes + tm * tn * (out_bytes + ex_bytes) + (tm * k * 2 if emit_norm else 0)
        temps = tm * tn * 4 + (tm * k * 2 if has_norm else 0) + (tm * k * 6 if has_fn or has_norm else 0)
        if 2 * blocks + temps <= MM_VMEM_BUDGET + (MM_VMEM_EXTRA if has_fn else 0):
            return tm, tn
    raise ValueError((m, n, k))


def _mm(a, b, *, mode, out_dtype, name, norm_g=None, emit_norm=False, a_fn=None, extras=(), epilogue=None,
        col_sums=False, full_rows=False):
    a_blk = 0
    if isinstance(a, Cols):
        a, width, a_blk = a
        a_shape = (a.shape[0], width)
    else:
        a_shape = a.shape
    if mode == 'tn':
        k, m = a_shape
    else:
        m, k = a_shape
    n = b.shape[0] if mode == 'nt' else b.shape[1]
    assert (b.shape[1] if mode == 'nt' else b.shape[0]) == k, (name, a.shape, b.shape)
    has_norm = norm_g is not None
    ex_bytes = sum(arr.dtype.itemsize for arr, kind, _ in extras if kind == 'mn')
    tm, tn = _mm_tiles(m, n, k, a.dtype.itemsize, b.dtype.itemsize, jnp.dtype(out_dtype).itemsize, ex_bytes, has_norm,
                       emit_norm, a_fn is not None, full_rows)
    assert all(col % tn == 0 for _, _, col in extras), (name, tn)
    assert a_blk == 0 or (mode == 'nn') or (mode == 'tn' and tm == m)
    assert not (col_sums and (has_norm or emit_norm))
    ij = (lambda f: lambda g0, g1: f(g1, g0)) if col_sums else (lambda f: f)
    spec = lambda blk, f: pl.BlockSpec(blk, ij(f))
    if mode == 'tn':
        a_spec = spec((k, tm), lambda i, j: (0, i + a_blk))
    else:
        a_spec = spec((tm, k), lambda i, j: (i, a_blk))
    b_spec = spec((tn, k), lambda i, j: (j, 0)) if mode == 'nt' else spec((k, tn), lambda i, j: (0, j))
    dims = {'nn': NN, 'nt': NT, 'tn': TN}[mode]
    assert not (has_norm and mode != 'nn')
    n_ex = len(extras)

    def body(*refs):
        a_ref, b_ref = refs[0], refs[1]
        pos = 2
        g_ref = None
        if has_norm:
            g_ref = refs[pos]
            pos += 1
        ex_refs = refs[pos:pos + n_ex]
        pos += n_ex
        o_ref = refs[pos]
        pos += 1
        h_ref = None
        if emit_norm:
            h_ref = refs[pos]
            pos += 1
        if has_norm:
            an_ref = refs[pos]

            @pl.when(pl.program_id(1) == 0)
            def _():
                xf = a_ref[...].astype(F32)
                y = xf * lax.rsqrt(jnp.mean(xf * xf, axis=-1, keepdims=True) + EPS) * g_ref[...]
                an_ref[...] = y.astype(BF16)
                if emit_norm:
                    h_ref[...] = y.astype(BF16)

            av = an_ref[...]
        else:
            av = a_ref[...]
            if a_fn is not None:
                av = a_fn(av)
            av = av.astype(BF16)
        acc = _dot(av, b_ref[...].astype(BF16), dims)
        if epilogue is not None:
            acc = epilogue(acc, *[r[...] for r in ex_refs])
        acc, to_sum = acc if isinstance(acc, tuple) else (acc, acc)
        o_ref[...] = acc.astype(out_dtype)
        if col_sums:
            sum_ref = refs[pos]

            @pl.when(pl.program_id(1) == 0)
            def _():
                sum_ref[...] = jnp.zeros_like(sum_ref)

            sum_ref[...] += jnp.sum(to_sum, axis=0, keepdims=True)

    in_specs = [a_spec, b_spec]
    args = [a, b]
    if has_norm:
        in_specs.append(pl.BlockSpec((1, k), lambda i, j: (0, 0)))
        args.append(norm_g)
    for arr, kind, col in extras:
        if kind == 'mn':
            in_specs.append(spec((tm, tn), lambda i, j, o=col // tn: (i, j + o)))
        else:
            in_specs.append(spec((1, tn), lambda i, j, o=col // tn: (0, j + o)))
        args.append(arr)
    out_shape = [jax.ShapeDtypeStruct((m, n), out_dtype)]
    out_specs = [spec((tm, tn), lambda i, j: (i, j))]
    if emit_norm:
        out_shape.append(jax.ShapeDtypeStruct((m, k), BF16))
        out_specs.append(pl.BlockSpec((tm, k), lambda i, j: (i, 0)))
    if col_sums:
        out_shape.append(jax.ShapeDtypeStruct((1, n), F32))
        out_specs.append(spec((1, tn), lambda i, j: (0, j)))
    scratch = [pltpu.VMEM((tm, k), BF16)] if has_norm else []
    grid = (n // tn, m // tm) if col_sums else (m // tm, n // tn)
    res = pl.pallas_call(
        body, name=name, grid=grid, in_specs=in_specs, out_specs=out_specs, out_shape=out_shape,
        scratch_shapes=scratch,
        compiler_params=_params('arbitrary', 'arbitrary', extra_vmem=MM_VMEM_EXTRA if a_fn is not None else 0))(*args)
    return res if emit_norm or col_sums else res[0]


def _gated_merge(outs, ups, zg, bias, *, name, tm=1024, tn=512):
    s, n, nq = zg.shape[0], ups[0].shape[1], len(outs)
    tm, tn = min(tm, s), min(tn, n)
    per = n // tn

    def body(*refs):
        y = None
        for q in range(nq):
            o_ref, w_ref, z_ref, b_ref = refs[q], refs[nq + q], refs[2 * nq + q], refs[3 * nq + q]
            term = _sig(z_ref[...].astype(F32) + b_ref[...]) * _dot(o_ref[...], w_ref[...])
            y = term if y is None else y + term
        refs[4 * nq][...] = y.astype(BF16)

    in_specs = [pl.BlockSpec((tm, o.shape[1]), lambda i, j: (i, 0)) for o in outs]
    in_specs += [pl.BlockSpec((u.shape[0], tn), lambda i, j: (0, j)) for u in ups]
    in_specs += [pl.BlockSpec((tm, tn), lambda i, j, q=q: (i, j + q * per)) for q in range(nq)]
    in_specs += [pl.BlockSpec((1, tn), lambda i, j, q=q: (0, j + q * per)) for q in range(nq)]
    return pl.pallas_call(body, name=name, grid=(s // tm, per), in_specs=in_specs,
                          out_specs=pl.BlockSpec((tm, tn), lambda i, j: (i, j)),
                          out_shape=jax.ShapeDtypeStruct((s, n), BF16),
                          compiler_params=_params('arbitrary', 'arbitrary'))(*outs, *ups, *[zg] * nq, *[bias] * nq)


def _gated_merge_bwd(dy, zg, bias, outs, ups, do_dtypes, *, name, tm=512):
    s, n = dy.shape
    nq = len(outs)
    tm = min(tm, s)

    def body(*refs):
        dy_ref, zg_ref, b_ref = refs[:3]
        o_refs, w_refs = refs[3:3 + nq], refs[3 + nq:3 + 2 * nq]
        du_refs, do_refs = refs[3 + 2 * nq:3 + 3 * nq], refs[3 + 3 * nq:3 + 4 * nq]
        dz_ref, db_ref = refs[3 + 4 * nq:]

        @pl.when(pl.program_id(0) == 0)
        def _():
            db_ref[...] = jnp.zeros_like(db_ref)

        d = dy_ref[...].astype(F32)
        for q in range(nq):
            cols = slice(q * n, (q + 1) * n)
            g = _sig(zg_ref[:, cols].astype(F32) + b_ref[:, cols])
            du = (d * g).astype(BF16)
            du_refs[q][...] = du
            do_refs[q][...] = _dot(du, w_refs[q][...], NT).astype(do_dtypes[q])
            dz = d * _dot(o_refs[q][...], w_refs[q][...]) * g * (1.0 - g)
            dz_ref[:, cols] = dz.astype(BF16)
            db_ref[:, cols] += jnp.sum(dz, axis=0, keepdims=True)

    row = lambda w: pl.BlockSpec((tm, w), lambda i: (i, 0))
    whole = lambda a: pl.BlockSpec(a.shape, lambda i: (0, 0))
    in_specs = [row(n), row(nq * n), whole(bias)] + [row(o.shape[1]) for o in outs] + [whole(u) for u in ups]
    out_specs = [row(n)] * nq + [row(o.shape[1]) for o in outs] + [row(nq * n), pl.BlockSpec((1, nq * n), lambda i: (0, 0))]
    out_shape = ([jax.ShapeDtypeStruct((s, n), BF16)] * nq
                 + [jax.ShapeDtypeStruct((s, o.shape[1]), dt) for o, dt in zip(outs, do_dtypes)]
                 + [jax.ShapeDtypeStruct((s, nq * n), BF16), jax.ShapeDtypeStruct((1, nq * n), F32)])
    res = pl.pallas_call(body, name=name, grid=(s // tm,), in_specs=in_specs, out_specs=out_specs, out_shape=out_shape,
                         compiler_params=_params('arbitrary'))(dy, zg, bias, *outs, *ups)
    return res[:nq], res[nq:2 * nq], res[2 * nq], res[2 * nq + 1]


def _mn(col_off=0):
    return 'mn', col_off


def _nvec(col_off=0):
    return 'n', col_off


def _rowwise(fn, rows, consts, outs, sums=(), *, name, ts=1024):
    views = [x if isinstance(x, Cols) else Cols(x, x.shape[1], 0) for x in rows]
    rows = [v.arr for v in views]
    r = rows[0].shape[0]
    ts = min(ts, r)
    assert r % ts == 0, (name, r, ts)
    nr, nc, no, ns = len(rows), len(consts), len(outs), len(sums)

    def body(*refs):
        vals = fn(*[x[...] for x in refs[:nr + nc]])
        for q in range(no):
            refs[nr + nc + q][...] = vals[q].astype(outs[q][1])
        if ns:
            @pl.when(pl.program_id(0) == 0)
            def _():
                for q in range(ns):
                    refs[nr + nc + no + q][...] = jnp.zeros((1, sums[q]), F32)

            for q in range(ns):
                refs[nr + nc + no + q][...] += jnp.sum(vals[no + q].astype(F32), axis=0, keepdims=True)

    in_specs = [pl.BlockSpec((ts, v.width), lambda i, blk=v.blk: (i, blk)) for v in views]
    in_specs += [pl.BlockSpec(x.shape, lambda i, nd=x.ndim: (0,) * nd) for x in consts]
    out_specs = [pl.BlockSpec((ts, w), lambda i: (i, 0)) for w, _ in outs]
    out_specs += [pl.BlockSpec((1, w), lambda i: (0, 0)) for w in sums]
    out_shape = [jax.ShapeDtypeStruct((r, w), dt) for w, dt in outs]
    out_shape += [jax.ShapeDtypeStruct((1, w), F32) for w in sums]
    return pl.pallas_call(body, name=name, grid=(r // ts,), in_specs=in_specs, out_specs=out_specs,
                          out_shape=out_shape, compiler_params=_params('arbitrary'))(*rows, *consts)


def _cumsum_rows(x, *, reverse, name, bs=256):
    s, w = x.shape
    bs = min(bs, s)
    nb = s // bs

    def body(x_ref, o_ref, carry):
        @pl.when(pl.program_id(0) == 0)
        def _():
            carry[...] = jnp.zeros_like(carry)

        r = lax.broadcasted_iota(jnp.int32, (bs, bs), 0)
        c = lax.broadcasted_iota(jnp.int32, (bs, bs), 1)
        tri = jnp.where((c >= r) if reverse else (c <= r), 1.0, 0.0).astype(BF16)
        xv = x_ref[...]
        o_ref[...] = _tri_dot(tri, xv) + carry[...]
        carry[...] += jnp.sum(xv, axis=0, keepdims=True)

    imap = (lambda i: (nb - 1 - i, 0)) if reverse else (lambda i: (i, 0))
    return pl.pallas_call(body, name=name, grid=(nb,), in_specs=[pl.BlockSpec((bs, w), imap)],
                          out_specs=pl.BlockSpec((bs, w), imap), out_shape=jax.ShapeDtypeStruct((s, w), F32),
                          scratch_shapes=[pltpu.VMEM((1, w), F32)], compiler_params=_params('arbitrary'))(x)


def _mask(mode, q0, k0, bq, bk):
    qpos = q0 + lax.broadcasted_iota(jnp.int32, (bq, bk), 0)
    kpos = k0 + lax.broadcasted_iota(jnp.int32, (bq, bk), 1)
    if mode == 'causal':
        return kpos <= qpos
    return kpos < (jnp.right_shift(qpos, int(math.log2(CHUNK))) + 1) * CHUNK


ROPE_SHIFT = int(math.log2(MLA_ROPE))
FOX_SCALE, MLA_SCALE, XA_SCALE = FOX_HD ** -0.5, (MLA_NOPE + MLA_ROPE) ** -0.5, XA_HD ** -0.5
ATTN_ROW_SLAB = 512


def _lane_masks(g, b, rope):
    lane = lax.broadcasted_iota(jnp.int32, (1, LANES), 1)
    heads = [None if g == 1 else (lane >= hh * (LANES // g)) & (lane < (hh + 1) * (LANES // g)) for hh in range(g)]
    ropes = [jnp.right_shift(lane, ROPE_SHIFT) == b * g + hh for hh in range(g)] if rope else [None] * g
    return heads, ropes


def _sel(mask, x):
    return x if mask is None else jnp.where(mask, x, jnp.zeros_like(x))


class Step(typing.NamedTuple):
    qi: typing.Any
    kj: typing.Any
    first: typing.Any
    last: typing.Any
    plain: typing.Any
    masked: typing.Any


def _fwd_steps(tri, nq, nk):
    if not tri:
        return (nq, nk), lambda i, j: Step(i, j, j == 0, j == nk - 1, True, False)
    if nq % 2:
        return (nq, nk), lambda i, j: Step(i, jnp.minimum(i, j), j == 0, j == nk - 1, j < i, j == i)

    def at(i, t):
        low = t <= i
        diag = (t == i) | (t == nq)
        return Step(jnp.where(low, i, nq - 1 - i), jnp.where(low, t, t - (i + 1)), (t == 0) | (t == i + 1), diag,
                    jnp.logical_not(diag), diag)

    return (nq // 2, nq + 1), at


def _bwd_steps(tri, nq, nk):
    if not tri:
        return (nk, nq), lambda j, i: Step(i, j, i == 0, i == nq - 1, True, False)
    if nk % 2:
        return (nk, nq), lambda j, i: Step(jnp.maximum(i, j), j, i == 0, i == nq - 1, i > j, i == j)

    def at(j, t):
        n1 = nq - j
        low = t < n1
        diag = (t == 0) | (t == n1)
        return Step(jnp.where(low, j + t, nk - 1 - j + t - n1), jnp.where(low, j, nk - 1 - j), diag,
                    (t == n1 - 1) | (t == nq), jnp.logical_not(diag), diag)

    return (nk // 2, nq + 1), at


def _carried(comm, refs, n_in, n_out):
    ci, co = len(comm.ins), len(comm.out_shapes)
    ins = refs[n_in:n_in + ci]
    outs = refs[n_in + ci + n_out:n_in + ci + n_out + co]
    rest = refs[:n_in] + refs[n_in + ci:n_in + ci + n_out] + refs[n_in + ci + n_out + co:-2]
    return rest, (ins, outs, refs[-2], refs[-1])


def _mattn_fwd(q, k, v, *, qc, kc, vc, nb, g, mode, name, dq_scale=1.0, ck=None, qr=None, qrc=0, kr=None, blk=512,
               comm=None):
    s, t = q.shape[0], k.shape[0]
    bq, bk = min(blk, s), min(blk, t)
    nq, nk = s // bq, t // bk
    tri = mode != 'full'
    bias, rope = ck is not None, qr is not None
    assert not tri or (bq == bk and bq % CHUNK == 0)
    rs = min(ATTN_ROW_SLAB, bq)
    n_in = 3 + bias + 2 * rope
    (n1, n2), step_at = _fwd_steps(tri, nq, nk)

    def body(*refs):
        refs = list(refs)
        b, p1, p2 = pl.program_id(0), pl.program_id(1), pl.program_id(2)
        st = step_at(p1, p2)
        i, j = st.qi, st.kj
        if comm is not None:
            refs, comm_refs = _carried(comm, refs, n_in, 2)
            pl.when((b == 0) & (p1 == 0) & (p2 == 0))(lambda: comm.start(*comm_refs))
        q_ref, k_ref, v_ref = refs[:3]
        pos = 3
        ck_ref = qr_ref = kr_ref = None
        if bias:
            ck_ref = refs[pos]
            pos += 1
        if rope:
            qr_ref, kr_ref = refs[pos:pos + 2]
            pos += 2
        o_ref, lse_ref, m_s, l_s, acc_s = refs[pos:]
        heads, ropes = _lane_masks(g, b, rope)

        @pl.when(st.first)
        def _():
            m_s[...] = jnp.full_like(m_s, MASK_VALUE)
            l_s[...] = jnp.zeros_like(l_s)
            acc_s[...] = jnp.zeros_like(acc_s)

        def compute(masked):
            k2, v2 = k_ref[...], v_ref[...]
            for r in range(bq // rs):
                rows = pl.ds(r * rs, rs)
                q2 = q_ref[rows, :]
                alphas, pvs = [], []
                for hh in range(g):
                    sc = _dot(_sel(heads[hh], q2), k2, NT)
                    if rope:
                        sc = sc + _dot(_sel(ropes[hh], qr_ref[rows, :]), kr_ref[...], NT)
                    if bias:
                        sc = sc - ck_ref[0, hh:hh + 1, :]
                    if masked:
                        sc = jnp.where(_mask(mode, i * bq + r * rs, j * bk, rs, bk), sc, MASK_VALUE)
                    m_prev = m_s[hh, rows]
                    m_new = jnp.maximum(m_prev, jnp.max(sc, axis=1, keepdims=True))
                    alpha = jnp.exp(m_prev - m_new)
                    p = jnp.exp(sc - m_new)
                    l_s[hh, rows] = alpha * l_s[hh, rows] + jnp.sum(p, axis=1, keepdims=True)
                    m_s[hh, rows] = m_new
                    alphas.append(alpha)
                    pvs.append(_dot(p.astype(BF16), _sel(heads[hh], v2)))
                alpha = alphas[0]
                for hh in range(1, g):
                    alpha = jnp.where(heads[hh], alphas[hh], alpha)
                acc_s[rows, :] = acc_s[rows, :] * alpha + sum(pvs[1:], pvs[0])

        if tri:
            pl.when(st.plain)(functools.partial(compute, False))
            pl.when(st.masked)(functools.partial(compute, True))
        else:
            compute(False)

        @pl.when(st.last)
        def _():
            lane = lax.broadcasted_iota(jnp.int32, (bq, LANES), 1)
            l_full, lse = l_s[0], jnp.zeros((bq, LANES), F32)
            for hh in range(g):
                if hh:
                    l_full = jnp.where(heads[hh], l_s[hh], l_full)
                lse = jnp.where(lane == hh, m_s[hh] + jnp.log(l_s[hh]), lse)
            o_ref[...] = (acc_s[...] / l_full).astype(o_ref.dtype)
            lse_ref[...] = lse

        if comm is not None:
            pl.when((b == nb - 1) & (p1 == n1 - 1) & (p2 == n2 - 1))(lambda: comm.finish(*comm_refs))

    qi = lambda p1, p2: step_at(p1, p2).qi
    kj = lambda p1, p2: step_at(p1, p2).kj
    in_specs = [pl.BlockSpec((bq, LANES), lambda b, p1, p2: (qi(p1, p2), qc + b)),
                pl.BlockSpec((bk, LANES), lambda b, p1, p2: (kj(p1, p2), kc + b)),
                pl.BlockSpec((bk, LANES), lambda b, p1, p2: (kj(p1, p2), vc + b))]
    args = [q, k, v]
    if bias:
        in_specs.append(pl.BlockSpec((1, 8, bk), lambda b, p1, p2: (b, 0, kj(p1, p2))))
        args.append(ck)
    if rope:
        in_specs += [pl.BlockSpec((bq, LANES), lambda b, p1, p2: (qi(p1, p2), qrc)),
                     pl.BlockSpec((bk, LANES), lambda b, p1, p2: (kj(p1, p2), 0))]
        args += [qr, kr]
    out = pl.BlockSpec((bq, LANES), lambda b, p1, p2: (qi(p1, p2), b))
    out_specs = [out, out]
    out_shape = [jax.ShapeDtypeStruct((s, LANES * nb), BF16), jax.ShapeDtypeStruct((s, LANES * nb), F32)]
    scratch = [pltpu.VMEM((g, bq, 1), F32), pltpu.VMEM((g, bq, 1), F32), pltpu.VMEM((bq, LANES), F32)]
    if comm is not None:
        in_specs += [ANY] * len(comm.ins)
        args += comm.ins
        out_specs += [ANY] * len(comm.out_shapes)
        out_shape += comm.out_shapes
        scratch += _sems(comm.n_sems, comm.n_sems)
    res = pl.pallas_call(body, name=name, grid=(nb, n1, n2), in_specs=in_specs, out_specs=out_specs, out_shape=out_shape,
                         scratch_shapes=scratch, compiler_params=_params('arbitrary', 'arbitrary', 'arbitrary'))(*args)
    return res if comm is None else (res[0], res[1], res[2:])


def _mattn_bwd(q, k, v, o, do, lse, *, qc, kc, vc, nb, g, mode, name, dq_scale=1.0, ck=None, qr=None, qrc=0, kr=None,
               blk=512, comm=None):
    s, t = q.shape[0], k.shape[0]
    bq, bk = min(blk, s), min(blk, t)
    nq, nk = s // bq, t // bk
    tri = mode != 'full'
    bias, rope = ck is not None, qr is not None
    rs = min(ATTN_ROW_SLAB, bq)
    n_in, n_out = 6 + bias + 2 * rope, 3 + 2 * bias + 2 * rope
    (n1, n2), step_at = _bwd_steps(tri, nq, nk)

    def body(*refs):
        refs = list(refs)
        if comm is not None:
            refs, comm_refs = _carried(comm, refs, n_in, n_out)
            first = (pl.program_id(0) == 0) & (pl.program_id(1) == 0) & (pl.program_id(2) == 0)
            pl.when(first)(lambda: comm.start(*comm_refs))
        q_ref, k_ref, v_ref, o_ref, do_ref, lse_ref = refs[:6]
        pos = 6
        ck_ref = qr_ref = kr_ref = dck_ref = dcq_ref = dqr_ref = dkr_ref = dck_s = None
        if bias:
            ck_ref = refs[pos]
            pos += 1
        if rope:
            qr_ref, kr_ref = refs[pos:pos + 2]
            pos += 2
        dq_ref, dk_ref, dv_ref = refs[pos:pos + 3]
        pos += 3
        if bias:
            dck_ref, dcq_ref = refs[pos:pos + 2]
            pos += 2
        if rope:
            dqr_ref, dkr_ref = refs[pos:pos + 2]
            pos += 2
        dk_s, dv_s = refs[pos:pos + 2]
        if bias:
            dck_s = refs[pos + 2]
        b, p1, p2 = pl.program_id(0), pl.program_id(1), pl.program_id(2)
        st = step_at(p1, p2)
        i, j = st.qi, st.kj
        heads, ropes = _lane_masks(g, b, rope)

        @pl.when((p1 == 0) & (p2 == 0))
        def _():
            dq_ref[...] = jnp.zeros_like(dq_ref)
            if bias:
                dcq_ref[...] = jnp.zeros_like(dcq_ref)

        if rope:
            @pl.when((b == 0) & (p1 == 0) & (p2 == 0))
            def _():
                dqr_ref[...] = jnp.zeros_like(dqr_ref)
                dkr_ref[...] = jnp.zeros_like(dkr_ref)

        @pl.when(st.first)
        def _():
            dk_s[...] = jnp.zeros_like(dk_s)
            dv_s[...] = jnp.zeros_like(dv_s)
            if bias:
                dck_s[...] = jnp.zeros_like(dck_s)

        def compute(masked):
            k2, v2 = k_ref[...], v_ref[...]
            lane = lax.broadcasted_iota(jnp.int32, (rs, LANES), 1)
            rk = pl.ds(pl.multiple_of(j * bk, bk), bk)
            add = lambda tot, x: x if tot is None else tot + x
            dv_t = dk_t = dkr_t = None
            dck_t = [None] * g
            for r in range(bq // rs):
                rows = pl.ds(r * rs, rs)
                rq = pl.ds(pl.multiple_of(i * bq + r * rs, rs), rs)
                q2, do2, lse2 = q_ref[rows, :], do_ref[rows, :], lse_ref[rows, :]
                dd = do2.astype(F32) * o_ref[rows, :].astype(F32)
                dq_t = dqr_t = dcq_t = None
                for hh in range(g):
                    qm = _sel(heads[hh], q2)
                    sc = _dot(qm, k2, NT)
                    if rope:
                        qrm = _sel(ropes[hh], qr_ref[rows, :])
                        sc = sc + _dot(qrm, kr_ref[...], NT)
                    if bias:
                        sc = sc - ck_ref[0, hh:hh + 1, :]
                    if masked:
                        sc = jnp.where(_mask(mode, i * bq + r * rs, j * bk, rs, bk), sc, MASK_VALUE)
                    p = jnp.exp(sc - jnp.sum(jnp.where(lane == hh, lse2, 0.0), axis=1, keepdims=True))
                    dom = _sel(heads[hh], do2)
                    dp = _dot(dom, v2, NT)
                    delta = jnp.sum(_sel(heads[hh], dd), axis=1, keepdims=True)
                    ds = p * (dp - delta)
                    dsb = ds.astype(BF16)
                    dv_t = add(dv_t, _dot(p.astype(BF16), dom, TN))
                    dk_t = add(dk_t, _dot(dsb, qm, TN))
                    dq_t = add(dq_t, _dot(dsb, _sel(heads[hh], k2)))
                    if rope:
                        dqr_t = add(dqr_t, _dot(dsb, _sel(ropes[hh], kr_ref[...])))
                        dkr_t = add(dkr_t, _dot(dsb, qrm, TN))
                    if bias:
                        dck_t[hh] = add(dck_t[hh], jnp.sum(ds, axis=0, keepdims=True))
                        dcq_t = add(dcq_t, jnp.where(lane == hh, jnp.sum(ds, axis=1, keepdims=True), 0.0))
                dq_ref[rq, :] += dq_t if dq_scale == 1.0 else dq_scale * dq_t
                if rope:
                    dqr_ref[rq, :] += dq_scale * dqr_t
                if bias:
                    dcq_ref[rq, :] += dcq_t
            dv_s[...] += dv_t
            dk_s[...] += dk_t
            if rope:
                dkr_ref[rk, :] += dkr_t
            if bias:
                for hh in range(g):
                    dck_s[hh:hh + 1, :] -= dck_t[hh]

        if tri:
            pl.when(st.plain)(functools.partial(compute, False))
            pl.when(st.masked)(functools.partial(compute, True))
        else:
            compute(False)

        @pl.when(st.last)
        def _():
            dk_ref[...] = dk_s[...]
            dv_ref[...] = dv_s[...]
            if bias:
                dck_ref[0] = dck_s[...]

        if comm is not None:
            pl.when((b == nb - 1) & (p1 == n1 - 1) & (p2 == n2 - 1))(lambda: comm.finish(*comm_refs))

    qrow = lambda col: pl.BlockSpec((bq, LANES), lambda b, p1, p2: (step_at(p1, p2).qi, col(b)))
    krow = lambda col: pl.BlockSpec((bk, LANES), lambda b, p1, p2: (step_at(p1, p2).kj, col(b)))
    in_specs = [qrow(lambda b: qc + b), krow(lambda b: kc + b), krow(lambda b: vc + b), qrow(lambda b: b),
                qrow(lambda b: b), qrow(lambda b: b)]
    args = [q, k, v, o, do, lse]
    whole = lambda rows: pl.BlockSpec((rows, LANES), lambda b, j, i: (0, b))
    out_specs = [whole(s), krow(lambda b: b), krow(lambda b: b)]
    out_shape = [jax.ShapeDtypeStruct((s, LANES * nb), F32), jax.ShapeDtypeStruct((t, LANES * nb), F32),
                 jax.ShapeDtypeStruct((t, LANES * nb), F32)]
    scratch = [pltpu.VMEM((bk, LANES), F32), pltpu.VMEM((bk, LANES), F32)]
    if bias:
        ckj = pl.BlockSpec((1, 8, bk), lambda b, p1, p2: (b, 0, step_at(p1, p2).kj))
        in_specs.append(ckj)
        args.append(ck)
        out_specs += [ckj, whole(s)]
        out_shape += [jax.ShapeDtypeStruct((nb, 8, t), F32), jax.ShapeDtypeStruct((s, LANES * nb), F32)]
    if rope:
        in_specs += [qrow(lambda b: qrc), krow(lambda b: 0)]
        args += [qr, kr]
        out_specs += [pl.BlockSpec((s, LANES), lambda b, j, i: (0, 0)), pl.BlockSpec((t, LANES), lambda b, j, i: (0, 0))]
        out_shape += [jax.ShapeDtypeStruct((s, LANES), F32), jax.ShapeDtypeStruct((t, LANES), F32)]
    if bias:
        scratch.append(pltpu.VMEM((8, bk), F32))
    if comm is not None:
        in_specs += [ANY] * len(comm.ins)
        args += comm.ins
        out_specs += [ANY] * len(comm.out_shapes)
        out_shape += comm.out_shapes
        scratch += _sems(comm.n_sems, comm.n_sems)
    res = pl.pallas_call(body, name=name, grid=(nb, n1, n2), in_specs=in_specs, out_specs=out_specs,
                         out_shape=out_shape, scratch_shapes=scratch,
                         compiler_params=_params('arbitrary', 'arbitrary', 'arbitrary'))(*args)
    return res if comm is None else (*res[:n_out], res[n_out:])


def _gla_chunk(la_c, k_c):
    r = lax.broadcasted_iota(jnp.int32, (CHUNK, CHUNK), 0)
    c = lax.broadcasted_iota(jnp.int32, (CHUNK, CHUNK), 1)
    tri = jnp.where(c <= r, 1.0, 0.0).astype(BF16)
    cum = _tri_dot(tri, la_c)
    end = jnp.sum(la_c, axis=0, keepdims=True)
    dec = jnp.exp(end - cum)
    return dec, k_c * dec, jnp.exp(end)


GLA_PAIRS = GLA_HEADS // 2


def _gla_fwd(z, la, *, qc, kc, vc, name, blk=512):
    s = z.shape[0]
    bs = min(blk, s)
    ncb = bs // CHUNK
    nblk = s // bs

    def body(q_ref, k_ref, va_ref, vb_ref, la_ref, o_ref, st_ref, st):
        @pl.when(pl.program_id(1) == 0)
        def _():
            st[...] = jnp.zeros_like(st)

        heads, _ = _lane_masks(2, 0, False)
        v_refs = (va_ref, vb_ref)
        for c in range(ncb):
            sl = pl.ds(c * CHUNK, CHUNK)
            _, kf, a = _gla_chunk(la_ref[sl, :], k_ref[sl, :])
            qs = q_ref[sl, :] * (GLA_DK ** -0.5)
            for hh in range(2):
                ut = _dot(v_refs[hh][sl, :].astype(BF16), _sel(heads[hh], kf).astype(BF16), TN)
                new = a * st[hh] + ut
                st[hh] = new
                st_ref[0, c, hh] = new
                o_ref[sl, hh * GLA_DV:(hh + 1) * GLA_DV] = _dot(_sel(heads[hh], qs).astype(BF16), new.astype(BF16), NT)

    col = lambda c0, m=1: pl.BlockSpec((bs, LANES), lambda b, i: (i, c0 + m * b))
    return pl.pallas_call(
        body, name=name, grid=(GLA_PAIRS, nblk),
        in_specs=[col(qc), col(kc), col(vc, 2), col(vc + 1, 2), col(0)],
        out_specs=[pl.BlockSpec((bs, 2 * GLA_DV), lambda b, i: (i, b)),
                   pl.BlockSpec((1, ncb, 2, GLA_DV, LANES), lambda b, i: (b, i, 0, 0, 0))],
        out_shape=[jax.ShapeDtypeStruct((s, GLA_HEADS * GLA_DV), F32),
                   jax.ShapeDtypeStruct((GLA_PAIRS, s // CHUNK, 2, GLA_DV, LANES), F32)],
        scratch_shapes=[pltpu.VMEM((2, GLA_DV, LANES), F32)],
        compiler_params=_params('arbitrary', 'arbitrary'))(z, z, z, z, la)


def _gla_bwd(z, la, st_all, st_prev, do, *, qc, kc, vc, name, blk=512):
    s = z.shape[0]
    bs = min(blk, s)
    ncb = bs // CHUNK
    nblk = s // bs

    def body(q_ref, k_ref, va_ref, vb_ref, la_ref, st_ref, sp_ref, do_ref, dq_ref, dk_ref, dv_ref, dla_ref, ga):
        @pl.when(pl.program_id(1) == 0)
        def _():
            ga[...] = jnp.zeros_like(ga)

        r = lax.broadcasted_iota(jnp.int32, (CHUNK, CHUNK), 0)
        cc = lax.broadcasted_iota(jnp.int32, (CHUNK, CHUNK), 1)
        tri_rev = jnp.where(cc >= r, 1.0, 0.0).astype(BF16)
        heads, _ = _lane_masks(2, 0, False)
        v_refs = (va_ref, vb_ref)
        for c in reversed(range(ncb)):
            sl = pl.ds(c * CHUNK, CHUNK)
            dec, kf, a = _gla_chunk(la_ref[sl, :], k_ref[sl, :])
            qs = q_ref[sl, :] * (GLA_DK ** -0.5)
            dq2 = jnp.zeros((CHUNK, LANES), F32)
            dkd = jnp.zeros((CHUNK, LANES), F32)
            da = jnp.zeros((1, LANES), F32)
            for hh in range(2):
                hv = slice(hh * GLA_DV, (hh + 1) * GLA_DV)
                dob = do_ref[sl, hv].astype(BF16)
                g = _dot(dob, _sel(heads[hh], qs).astype(BF16), TN) + ga[hh]
                gb = g.astype(BF16)
                dq2 = dq2 + _dot(dob, st_ref[0, c, hh].astype(BF16))
                dv_ref[sl, hv] = _dot(_sel(heads[hh], kf).astype(BF16), gb, NT)
                dkd = dkd + _dot(v_refs[hh][sl, :].astype(BF16), gb)
                da = da + jnp.sum(g * sp_ref[0, c, hh], axis=0, keepdims=True)
                ga[hh] = a * g
            dq_ref[sl, :] = (GLA_DK ** -0.5) * dq2
            dk_ref[sl, :] = dkd * dec
            e = dkd * kf
            dend = jnp.sum(e, axis=0, keepdims=True) + da * a
            dla_ref[sl, :] = dend - _tri_dot(tri_rev, e)

    rev = lambda i: nblk - 1 - i
    col = lambda c0, m=1: pl.BlockSpec((bs, LANES), lambda b, i: (rev(i), c0 + m * b))
    wide = pl.BlockSpec((bs, 2 * GLA_DV), lambda b, i: (rev(i), b))
    stspec = pl.BlockSpec((1, ncb, 2, GLA_DV, LANES), lambda b, i: (b, rev(i), 0, 0, 0))
    return pl.pallas_call(
        body, name=name, grid=(GLA_PAIRS, nblk),
        in_specs=[col(qc), col(kc), col(vc, 2), col(vc + 1, 2), col(0), stspec, stspec, wide],
        out_specs=[col(0), col(0), wide, col(0)],
        out_shape=[jax.ShapeDtypeStruct((s, GLA_HEADS * GLA_DK), F32), jax.ShapeDtypeStruct((s, GLA_HEADS * GLA_DK), F32),
                   jax.ShapeDtypeStruct((s, GLA_HEADS * GLA_DV), F32), jax.ShapeDtypeStruct((s, GLA_HEADS * GLA_DK), F32)],
        scratch_shapes=[pltpu.VMEM((2, GLA_DV, LANES), F32)],
        compiler_params=_params('arbitrary', 'arbitrary'))(z, z, z, z, la, st_all, st_prev, do)


def _place():
    return lax.axis_index('x'), lax.axis_index('y'), lax.axis_index('c')


ANY = pl.BlockSpec(memory_space=pl.ANY)


def _all_gather8(blk, *, name):
    m, n = blk.shape

    def body(x_ref, out_ref, send_sems, recv_sems, local_sem):
        x, y, c = _place()
        me, sibling = (x, y, c), (x, y, 1 - c)
        chips = [(1 - x, y), (x, 1 - y), (1 - x, 1 - y)]

        def slot(px, py, pc):
            return out_ref.at[4 * px + 2 * py + pc]

        def copy(q, block, to, src=None):
            return pltpu.make_async_remote_copy(
                src_ref=slot(*block) if src is None else src, dst_ref=slot(*block), send_sem=send_sems.at[q],
                recv_sem=recv_sems.at[q], device_id=to, device_id_type=MESH)

        mine = pltpu.make_async_copy(x_ref, slot(*me), local_sem)
        mine.start()
        first = [copy(0, me, sibling, src=x_ref)]
        first += [copy(1 + q, me, (*chip, c), src=x_ref) for q, chip in enumerate(chips)]
        for cp in first:
            cp.start()
        passed = [copy(4 + q, (*chip, c), sibling) for q, chip in enumerate(chips)]
        for q, chip in enumerate(chips):
            copy(1 + q, (*chip, c), me).wait_recv()
            passed[q].start()
        copy(0, sibling, me).wait_recv()
        for q, chip in enumerate(chips):
            copy(4 + q, (*chip, 1 - c), me).wait_recv()
        for cp in first + passed:
            cp.wait_send()
        mine.wait()

    return pl.pallas_call(
        body, name=name, in_specs=[ANY], out_specs=ANY, out_shape=jax.ShapeDtypeStruct((N_DEV, m, n), blk.dtype),
        scratch_shapes=[pltpu.SemaphoreType.DMA((7,)), pltpu.SemaphoreType.DMA((7,)), pltpu.SemaphoreType.DMA(())],
    )(blk)


def _sems(*counts):
    return [pltpu.SemaphoreType.DMA((n,)) for n in counts]


class Comm(typing.NamedTuple):
    ins: list
    out_shapes: list
    n_sems: int
    start: typing.Callable
    finish: typing.Callable


def _remote(src, dst, send_sems, recv_sems, idx, to):
    return lambda: pltpu.make_async_remote_copy(src_ref=src, dst_ref=dst, send_sem=send_sems.at[idx],
                                                recv_sem=recv_sems.at[idx], device_id=to, device_id_type=MESH)


def _comm_from(copies, ins, out_shapes, n_sems):
    def start(*refs):
        for cp in copies(*refs)[0]:
            cp().start()

    def finish(*refs):
        sent, received = copies(*refs)
        for cp in received:
            cp().wait_recv()
        for cp in sent:
            cp().wait_send()

    return Comm(list(ins), list(out_shapes), n_sems, start, finish)


def _run_comm(comm, *, name, alias=False):
    n_in, n_out = len(comm.ins), len(comm.out_shapes)

    def body(*refs):
        ins, outs, sems = refs[:n_in], refs[n_in:n_in + n_out], refs[n_in + n_out:]
        comm.start(ins, outs, *sems)
        comm.finish(ins, outs, *sems)

    return pl.pallas_call(body, name=name, in_specs=[ANY] * n_in, out_specs=[ANY] * n_out, out_shape=comm.out_shapes,
                          input_output_aliases={q: q for q in range(n_in)} if alias else {},
                          scratch_shapes=_sems(comm.n_sems, comm.n_sems))(*comm.ins)


def _half(rows, c):
    h = rows // 2
    return pl.ds(pl.multiple_of(c * h, h), h)


def _gathered(ref, chip, rows, side):
    if not side:
        return ref.at[chip, rows]
    n = ref.shape[1] // N_CHIPS
    return ref.at[rows, pl.ds(pl.multiple_of(chip * n, n), n)]


def _gather_over_ici(ws, side):
    def copies(ins, outs, send_sems, recv_sems):
        x, y, c = _place()
        me_chip = 2 * x + y
        sent, received = [], []
        for q, w in enumerate(ws):
            half, every = _half(w.shape[0], c), pl.ds(0, w.shape[0])
            for k, (px, py) in enumerate([(1 - x, y), (x, 1 - y), (1 - x, 1 - y)]):
                sent.append(_remote(ins[q].at[half], _gathered(outs[q], me_chip, half, side[q]), send_sems, recv_sems,
                                    4 * q + k, (px, py, c)))
                slot = _gathered(outs[q], 2 * px + py, half, side[q])
                received.append(_remote(slot, slot, send_sems, recv_sems, 4 * q + k, (px, py, c)))
            whole = _remote(ins[q], _gathered(outs[q], me_chip, every, side[q]), send_sems, recv_sems, 4 * q + 3,
                            (x, y, 1 - c))
            sent.append(whole)
            received.append(whole)
        return sent, received

    shapes = [jax.ShapeDtypeStruct((w.shape[0], N_CHIPS * w.shape[1]) if sd else (N_CHIPS,) + w.shape, w.dtype)
              for w, sd in zip(ws, side)]
    return _comm_from(copies, ws, shapes, 4 * len(ws))


def _gather_over_d2d(parts, side):
    def copies(ins, outs, send_sems, recv_sems):
        x, y, c = _place()
        sent, received = [], []
        for q, w in enumerate(parts):
            rows = w.shape[0] if side[q] else w.shape[1]
            for k, (px, py) in enumerate([(1 - x, y), (x, 1 - y), (1 - x, 1 - y)]):
                mine = _gathered(outs[q], 2 * px + py, _half(rows, c), side[q])
                theirs = _gathered(outs[q], 2 * px + py, _half(rows, 1 - c), side[q])
                sent.append(_remote(mine, mine, send_sems, recv_sems, 3 * q + k, (x, y, 1 - c)))
                received.append(_remote(theirs, theirs, send_sems, recv_sems, 3 * q + k, (x, y, 1 - c)))
        return sent, received

    return _comm_from(copies, parts, [jax.ShapeDtypeStruct(w.shape, w.dtype) for w in parts], 3 * len(parts))


def _to_sibling(gs, *, name):
    n = len(gs)

    def body(*refs):
        ins, outs = refs[:n], refs[n:2 * n]
        send_sems, recv_sems = refs[2 * n:]
        x, y, c = _place()
        cps = [pltpu.make_async_remote_copy(
            src_ref=ins[q], dst_ref=outs[q], send_sem=send_sems.at[q], recv_sem=recv_sems.at[q],
            device_id=(x, y, 1 - c), device_id_type=MESH) for q in range(n)]
        for cp in cps:
            cp.start()
        for cp in cps:
            cp.wait()

    return pl.pallas_call(body, name=name, in_specs=[ANY] * n, out_specs=[ANY] * n,
                          out_shape=[jax.ShapeDtypeStruct(g.shape, g.dtype) for g in gs],
                          scratch_shapes=_sems(n, n))(*gs)


def _chip_exchange(ps):
    def copies(ins, outs, send_sems, recv_sems):
        x, y, c = _place()
        cps = [_remote(ins[q].at[2 * px + py], outs[q].at[k], send_sems, recv_sems, 3 * q + k, (px, py, c))
               for q in range(len(ps)) for k, (px, py) in enumerate([(1 - x, y), (x, 1 - y), (1 - x, 1 - y)])]
        return cps, cps

    return _comm_from(copies, ps, [jax.ShapeDtypeStruct((3,) + p.shape[1:], p.dtype) for p in ps], 3 * len(ps))


def _sum_chips(own, r, *, name, ts=1024):
    k, n = own.shape
    ts = min(ts, k)

    def body(own_ref, r_ref, o_ref):
        f = lambda q: r_ref[q].astype(F32)
        o_ref[...] = ((own_ref[...].astype(F32) + f(0)) + f(1)) + f(2)

    return pl.pallas_call(
        body, name=name, grid=(k // ts,),
        in_specs=[pl.BlockSpec((ts, n), lambda i: (i, 0)), pl.BlockSpec((3, ts, n), lambda i: (0, i, 0))],
        out_specs=pl.BlockSpec((ts, n), lambda i: (i, 0)), out_shape=jax.ShapeDtypeStruct((k, n), F32),
        compiler_params=_params('arbitrary'))(own, r)


WIN_SHARD = N_IN // N_CHIPS
WIN_PAD = -(-WIN_SHARD // LANES) * LANES
GATE_WIRE_ROWS = 32


def _full_layer(sh, axis):
    _, k, n = sh.shape
    if axis == 2:
        return sh.transpose(1, 0, 2).reshape(k, N_CHIPS * n)
    return sh.reshape(N_CHIPS * k, n)


def _win_cols(wp, o, n):
    parts = []
    while n > 0:
        j, r = divmod(o, WIN_SHARD)
        take = min(n, WIN_SHARD - r)
        parts.append(wp[:, j * WIN_PAD + r:j * WIN_PAD + r + take])
        o, n = o + take, n - take
    return parts[0] if len(parts) == 1 else jnp.concatenate(parts, axis=1)


def _split_full(full, axis):
    if full.ndim == 3:
        return full
    k, n = full.shape
    if axis == 2:
        return jnp.stack([full[:, j * (n // N_CHIPS):(j + 1) * (n // N_CHIPS)] for j in range(N_CHIPS)])
    return full.reshape(N_CHIPS, k // N_CHIPS, n)


def _padc(a, w):
    return jnp.pad(a, ((0, 0), (0, w - a.shape[1])))


def _swap16(a):
    return jnp.concatenate([a[..., 16:32], a[..., 0:16]], axis=-1)


B_GR, B_GQ, B_GK, B_GV, B_MQ, B_MKR, B_MKRS, B_FF, B_GLOW, B_MKV, B_END = (
    0, 512, 768, 1024, 1536, 1792, 1920, 2048, 2176, 2304, 2432)
B_W = 2560
O_FQ, O_FF, O_GQ, O_GLOW, O_GR, O_MQ, O_MKV, O_MKR, O_ZG = 0, 768, 772, 1796, 1812, 2324, 2580, 2708, 2740


def _repack_layer_weights(w):
    wi = functools.partial(_win_cols, w['w_in'])
    out = dict(w)
    out['in_a'] = jnp.concatenate([wi(O_FQ, 256) * FOX_SCALE, wi(O_FQ + 256, 512)], axis=1)
    kr = wi(O_MKR, 32)
    out['in_b'] = jnp.concatenate([
        wi(O_GR, 512), wi(O_GQ, 1024), wi(O_MQ, 256), jnp.tile(kr, (1, MLA_HEADS)), jnp.tile(_swap16(kr), (1, MLA_HEADS)),
        _padc(wi(O_FF, 4), 128), _padc(wi(O_GLOW, 16), 128), wi(O_MKV, 128),
        jnp.zeros((D_MODEL, B_W - B_END), kr.dtype)], axis=1)
    out['in_c'] = wi(O_ZG, 3072)
    uq = w['w_mla_uq'].reshape(MLA_Q_RANK, MLA_HEADS, MLA_NOPE + MLA_ROPE)
    rope = uq[:, :, MLA_NOPE:]
    out['uq'] = jnp.concatenate([uq[:, :, :MLA_NOPE].reshape(MLA_Q_RANK, -1), rope.reshape(MLA_Q_RANK, -1),
                                 _swap16(rope).reshape(MLA_Q_RANK, -1)], axis=1)
    ukv = w['w_mla_ukv'].reshape(MLA_KV_RANK, MLA_HEADS, MLA_NOPE + MLA_VD)
    out['ukv'] = jnp.concatenate([ukv[:, :, :MLA_NOPE].reshape(MLA_KV_RANK, -1),
                                  ukv[:, :, MLA_NOPE:].reshape(MLA_KV_RANK, -1)], axis=1)
    out['gate'] = jnp.pad(w['w_gla_gate'], ((0, 128 - GLA_RANK), (0, 0)))
    return out


def _unpack_layer_grads(g):
    a, b, c = g['in_a'], g['in_b'], g['in_c']
    fold = lambda o: sum(b[:, o + MLA_ROPE * q:o + MLA_ROPE * (q + 1)] for q in range(MLA_HEADS))
    kr = fold(B_MKR) + _swap16(fold(B_MKRS))
    pieces = [(a[:, :256] * FOX_SCALE, 0, 256), (a, 256, 512), (b, B_FF, 4), (b, B_GQ, 1024), (b, B_GLOW, 16),
              (b, B_GR, 512), (b, B_MQ, 256), (b, B_MKV, 128), (kr, 0, 32), (c, 0, 3072)]
    shards = []
    for j in range(N_CHIPS):
        lo, hi, cut, at = j * WIN_SHARD, (j + 1) * WIN_SHARD, [], 0
        for arr, first, width in pieces:
            l, h = max(lo, at), min(hi, at + width)
            if l < h:
                cut.append(arr[:, first + l - at:first + h - at])
            at += width
        shards.append(jnp.concatenate(cut, axis=1))
    w_in = jnp.stack(shards)
    uq = g['uq']
    nope = uq[:, :256].reshape(MLA_Q_RANK, MLA_HEADS, MLA_NOPE)
    rope = (uq[:, 256:384].reshape(MLA_Q_RANK, MLA_HEADS, MLA_ROPE)
            + _swap16(uq[:, 384:512].reshape(MLA_Q_RANK, MLA_HEADS, MLA_ROPE)))
    w_uq = jnp.concatenate([nope, rope], axis=2).reshape(MLA_Q_RANK, -1)
    ukv = g['ukv']
    w_ukv = jnp.concatenate([ukv[:, :256].reshape(MLA_KV_RANK, MLA_HEADS, MLA_NOPE),
                             ukv[:, 256:].reshape(MLA_KV_RANK, MLA_HEADS, MLA_VD)], axis=2).reshape(MLA_KV_RANK, -1)
    out = {'w_in': w_in, 'w_mla_uq': w_uq, 'w_mla_ukv': w_ukv, 'w_gla_gate': g['gate'][:GLA_RANK]}
    for nm in ('w_up_fox', 'w_up_gla', 'w_up_mla', 'w_out', 'w_xq', 'w_xkv', 'w_xo', 'w_mlp1', 'w_mlp2'):
        out[nm] = g[nm]
    return out


def _rope_tables(s):
    half = MLA_ROPE // 2
    inv = ROPE_BASE ** (-jnp.arange(half, dtype=F32) / half)
    ang = jnp.arange(s).astype(F32)[:, None] * inv[None, :]
    cos, sin = jnp.cos(ang), jnp.sin(ang)
    c1 = jnp.concatenate([cos, cos], axis=1)
    s1 = jnp.concatenate([-sin, sin], axis=1)
    return jnp.tile(c1, (1, MLA_HEADS)), jnp.tile(s1, (1, MLA_HEADS))


def _rms_bwd(x, dh, g):
    r = lax.rsqrt(jnp.mean(x * x, axis=-1, keepdims=True) + EPS)
    xh = x * r
    gd = dh * g
    return r * (gd - xh * jnp.mean(gd * xh, axis=-1, keepdims=True)), dh * xh


def _norm_bwd_epilogue(dh, x, dres, g):
    dx, dg = _rms_bwd(x, dh, g)
    return dres + dx, dg


def _norm_bwd_call(x, dh, g, dres, name):
    w = x.width if isinstance(x, Cols) else x.shape[1]

    def with_res(xv, dv, rv, gv):
        dx, dg = _rms_bwd(xv, dv.astype(F32), gv)
        return rv + dx, dg

    def plain(xv, dv, gv):
        return _rms_bwd(xv, dv.astype(F32), gv)

    if dres is None:
        return _rowwise(plain, [x, dh], [g], [(w, F32)], [w], name=name)
    return _rowwise(with_res, [x, dh, dres], [g], [(w, F32)], [w], name=name)


def _gla_out_fwd(oraw, gr, g_out):
    outs = []
    for hh in range(GLA_HEADS):
        sl = slice(hh * GLA_DV, (hh + 1) * GLA_DV)
        oh = oraw[:, sl]
        n = oh * lax.rsqrt(jnp.mean(oh * oh, axis=-1, keepdims=True) + EPS) * g_out
        r = gr[:, sl]
        outs.append(n * (r * _sig(r)))
    return (jnp.concatenate(outs, axis=1),)


def _gla_out_bwd(oraw, gr, dout, g_out):
    d_o, d_r, dg = [], [], 0.0
    for hh in range(GLA_HEADS):
        sl = slice(hh * GLA_DV, (hh + 1) * GLA_DV)
        oh, r, do = oraw[:, sl], gr[:, sl], dout[:, sl].astype(F32)
        rs = lax.rsqrt(jnp.mean(oh * oh, axis=-1, keepdims=True) + EPS)
        sg = _sig(r)
        dn = do * (r * sg)
        d_r.append(do * (oh * rs * g_out) * (sg + r * sg * (1.0 - sg)))
        dx, dgh = _rms_bwd(oh, dn, g_out)
        d_o.append(dx)
        dg = dg + dgh
    return jnp.concatenate(d_o, axis=1), jnp.concatenate(d_r, axis=1), dg


def _adam(w, g, m, v):
    m = ADAM_B1 * m + (1.0 - ADAM_B1) * g
    v = ADAM_B2 * v + (1.0 - ADAM_B2) * (g * g)
    m_hat = m / (1.0 - ADAM_B1 ** ADAM_STEP)
    v_hat = v / (1.0 - ADAM_B2 ** ADAM_STEP)
    return -ADAM_LR * (m_hat / (jnp.sqrt(v_hat) + ADAM_EPS) + ADAM_WD * w), m, v


def _layer_fwd(x, mem, w, p, tabs, tag, carry_fox=None, after_fox=None, carry_mla=None):
    c4, s4 = tabs
    sv = {'x0': x}
    nm = lambda t: f'{t}_{tag}'
    za, h = _mm(x, w['in_a'], mode='nn', out_dtype=BF16, norm_g=p['g_mix'], emit_norm=True, name=nm('in_a'))
    zb = _mm(h, w['in_b'], mode='nn', out_dtype=F32, name=nm('in_b'))
    zc = _mm(h, w['in_c'], mode='nn', out_dtype=F32, name=nm('in_c'))
    sv.update(h=h, zc=zc)
    ff = Cols(zb, 128, B_FF // 128)
    (lf,) = _rowwise(lambda f, b: (_logsig(f + b),), [ff], [p['b_fox']], [(128, F32)], name=nm('fox_lf'))
    cum = _cumsum_rows(lf, reverse=False, name=nm('fox_cum'))
    ckf = jnp.pad(cum[:, :FOX_HEADS].T.reshape(2, 2, x.shape[0]), ((0, 0), (0, 6), (0, 0)))
    fox = dict(qc=0, kc=2, vc=4, nb=2, g=2, mode='causal', ck=ckf)
    o_fox, lse_fox, *carried = _mattn_fwd(za, za, za, name=nm('fox_attn'), comm=carry_fox, **fox)
    if after_fox is not None:
        w = {**w, **after_fox(carried[0])}
    sv.update(ff=ff, za=za, fox=fox, o_fox=o_fox, lse_fox=lse_fox)
    glow = Cols(zb, 128, B_GLOW // 128)
    gr = Cols(zb, 512, B_GR // 512)

    def gate_fn(gl, wg, bg):
        return (_logsig(_dot(gl.astype(BF16), wg) + bg) / GLA_TAU,)

    (la,) = _rowwise(gate_fn, [glow], [w['gate'], p['b_gla']], [(256, F32)], name=nm('gla_gate'))
    gla = dict(qc=B_GQ // LANES, kc=B_GK // LANES, vc=B_GV // LANES)
    oraw, states = _gla_fwd(zb, la, name=nm('gla'), **gla)
    (o_gla,) = _rowwise(_gla_out_fwd, [oraw, gr], [p['g_gla_out']], [(512, BF16)], name=nm('gla_out'))
    sv.update(glow=glow, gr=gr, zb=zb, la=la, gla=gla, states=states, oraw=oraw, o_gla=o_gla)
    mq = Cols(zb, 256, B_MQ // 256)
    mkv = Cols(zb, 128, B_MKV // 128)
    mkr2 = Cols(zb, 256, B_MKR // 256)
    qp, cqn = _mm(mq, w['uq'], mode='nn', out_dtype=F32, norm_g=p['g_mla_q'], emit_norm=True, name=nm('mla_uq'))
    kvp, ckvn = _mm(mkv, w['ukv'], mode='nn', out_dtype=BF16, norm_g=p['g_mla_kv'], emit_norm=True,
                    name=nm('mla_ukv'))

    def rope_fn(qv, kr, c4v, s4v):
        q_rope = qv[:, 256:384] * c4v + qv[:, 384:512] * s4v
        q_scaled = jnp.concatenate([qv[:, 0:256], q_rope], axis=1) * MLA_SCALE
        return q_scaled, kr[:, 0:128] * c4v + kr[:, 128:256] * s4v

    qall, kr4 = _rowwise(rope_fn, [qp, mkr2, c4, s4], [], [(384, BF16), (128, BF16)], name=nm('rope'))
    mla = dict(qc=0, kc=0, vc=2, nb=2, g=2, dq_scale=MLA_SCALE, mode='chunk', qr=qall, qrc=2, kr=kr4)
    o_mla, lse_mla, *carried = _mattn_fwd(qall, kvp, kvp, name=nm('mla_attn'), comm=carry_mla, **mla)
    if carry_mla is not None:
        sv['carried_mla'] = carried[0]
    sv.update(mq=mq, mkv=mkv, cqn=cqn, ckvn=ckvn, qall=qall, kvp=kvp, mla=mla, o_mla=o_mla, lse_mla=lse_mla)
    of_m, om_m = o_fox, o_mla
    sv.update(of_m=of_m, om_m=om_m)
    b_br = p['b_branch']

    y = _gated_merge([of_m, o_gla, om_m], [w['w_up_fox'], w['w_up_gla'], w['w_up_mla']], zc, b_br, name=nm('up_merge'))
    add = lambda acc, res: res + acc
    x1 = _mm(y, w['w_out'], mode='nn', out_dtype=F32, name=nm('out'), epilogue=add, extras=[(x, *_mn())])
    sv.update(y=y, x1=x1)
    qx, hx = _mm(x1, w['w_xq'], mode='nn', out_dtype=BF16, norm_g=p['g_xa'], emit_norm=True, name=nm('xq'),
                 epilogue=lambda acc: acc * XA_SCALE)
    kvx, mn = _mm(mem, w['w_xkv'], mode='nn', out_dtype=BF16, norm_g=p['g_mem'], emit_norm=True, name=nm('xkv'))
    xa = dict(qc=0, kc=0, vc=4, nb=4, g=1, dq_scale=XA_SCALE, mode='full')
    ox_m, lse_x = _mattn_fwd(qx, kvx, kvx, name=nm('xa_attn'), **xa)
    x2 = _mm(ox_m, w['w_xo'], mode='nn', out_dtype=F32, name=nm('xo'), epilogue=add, extras=[(x1, *_mn())])
    sv.update(hx=hx, mn=mn, qx=qx, kvx=kvx, xa=xa, lse_x=lse_x, ox_m=ox_m, x2=x2)
    hpre, hm = _mm(x2, w['w_mlp1'], mode='nn', out_dtype=BF16, norm_g=p['g_mlp'], emit_norm=True, name=nm('mlp1'))
    relu2 = lambda t: jnp.square(jnp.maximum(t.astype(F32), 0.0))
    x3 = _mm(hpre, w['w_mlp2'], mode='nn', out_dtype=F32, name=nm('mlp2'), a_fn=relu2, epilogue=add,
             extras=[(x2, *_mn())])
    sv.update(hpre=hpre, hm=hm, w=w)
    return x3, sv


EARLY = ('w_mlp1', 'w_mlp2', 'w_xo', 'w_xq', 'w_xkv', 'w_out', 'w_up_fox', 'w_up_gla', 'w_up_mla')
LATE = ('w_in', 'w_gla_gate', 'w_mla_uq', 'w_mla_ukv')


def _layer_bwd(dx3, mem, w, p, tabs, sv, tag, carry_mla=None, early=None):
    c4, s4 = tabs
    nm = lambda t: f'{t}_{tag}'
    s = dx3.shape[0]
    gw, gs = {}, {}
    relu2 = lambda t: jnp.square(jnp.maximum(t.astype(F32), 0.0))
    gw['w_mlp2'] = _mm(sv['hpre'], dx3, mode='tn', out_dtype=F32, name=nm('d_mlp2'), a_fn=relu2)
    dact = lambda acc, hp: acc * (2.0 * jnp.maximum(hp.astype(F32), 0.0))
    dhpre = _mm(dx3, w['w_mlp2'], mode='nt', out_dtype=BF16, name=nm('d_act'), epilogue=dact,
                extras=[(sv['hpre'], *_mn())])
    gw['w_mlp1'] = _mm(sv['hm'], dhpre, mode='tn', out_dtype=F32, name=nm('d_mlp1'))
    dx2, gs['g_mlp'] = _mm(dhpre, w['w_mlp1'], mode='nt', out_dtype=F32, name=nm('d_hm'), epilogue=_norm_bwd_epilogue,
                           col_sums=True, full_rows=True,
                           extras=[(sv['x2'], *_mn()), (dx3, *_mn()), (p['g_mlp'], *_nvec())])
    gw['w_xo'] = _mm(sv['ox_m'], dx2, mode='tn', out_dtype=F32, name=nm('d_xo'))
    dox = _mm(dx2, w['w_xo'], mode='nt', out_dtype=BF16, name=nm('d_ox'))
    dqx_m, dkx, dvx = _mattn_bwd(sv['qx'], sv['kvx'], sv['kvx'], sv['ox_m'], dox, sv['lse_x'], name=nm('xa_bwd'),
                                 **sv['xa'])
    dkvx = jnp.concatenate([dkx, dvx], axis=1).astype(BF16)
    gw['w_xq'] = _mm(sv['hx'], dqx_m, mode='tn', out_dtype=F32, name=nm('d_xq'))
    dx1, gs['g_xa'] = _mm(dqx_m, w['w_xq'], mode='nt', out_dtype=F32, name=nm('d_hx'), epilogue=_norm_bwd_epilogue,
                          col_sums=True, full_rows=True,
                          extras=[(sv['x1'], *_mn()), (dx2, *_mn()), (p['g_xa'], *_nvec())])
    gw['w_xkv'] = _mm(sv['mn'], dkvx, mode='tn', out_dtype=F32, name=nm('d_xkv'))
    dmn = _mm(dkvx, w['w_xkv'], mode='nt', out_dtype=F32, name=nm('d_mn'))
    _, gs['g_mem'] = _norm_bwd_call(mem, dmn, p['g_mem'], None, nm('d_norm_mem'))
    gw['w_out'] = _mm(sv['y'], dx1, mode='tn', out_dtype=F32, name=nm('d_out'))
    dy = _mm(dx1, w['w_out'], mode='nt', out_dtype=BF16, name=nm('d_y'))
    zc, b_br = sv['zc'], p['b_branch']

    branches = (('w_up_fox', sv['of_m'], BF16), ('w_up_gla', sv['o_gla'], F32), ('w_up_mla', sv['om_m'], BF16))
    du, do_br, dzc, gs['b_branch'] = _gated_merge_bwd(dy, zc, b_br, [o for _, o, _ in branches],
                                                      [w[wn] for wn, _, _ in branches], [dt for _, _, dt in branches],
                                                      name=nm('d_merge'))
    for q, (wn, o_m, _) in enumerate(branches):
        gw[wn] = _mm(o_m, du[q], mode='tn', out_dtype=F32, name=nm(f'd_up{q}'))
    za = sv['za']
    carry_fox = None if early is None else early({nm_: gw[nm_] for nm_ in EARLY})
    dfq, dfk, dfv, dck, dcq, *carried_fox = _mattn_bwd(za, za, za, sv['o_fox'], do_br[0], sv['lse_fox'],
                                                       name=nm('fox_bwd'), comm=carry_fox, **sv['fox'])
    dcum = _padc(dck[:, :2, :].reshape(FOX_HEADS, s).T + dcq.reshape(s, 2, LANES)[:, :, :2].reshape(s, FOX_HEADS), 128)
    dlf = _cumsum_rows(dcum, reverse=True, name=nm('fox_dcum'))

    def dff_fn(dl, f, b):
        d = dl * _sig(-(f + b))
        return d, d

    dff, db_fox = _rowwise(dff_fn, [dlf, sv['ff']], [p['b_fox']], [(128, F32)], [128], name=nm('fox_dff'))
    gs['b_fox'] = db_fox
    dza = jnp.concatenate([dfq, dfk, dfv], axis=1).astype(BF16)
    dqn, dkn, dvv, dq_rope, dk_rope, *carried_mla = _mattn_bwd(sv['qall'], sv['kvp'], sv['kvp'], sv['o_mla'], do_br[2],
                                                               sv['lse_mla'], name=nm('mla_bwd'), comm=carry_mla,
                                                               **sv['mla'])

    def drope_fn(dn, dq, dk, c4v, s4v):
        return jnp.concatenate([dn, dq * c4v, dq * s4v], axis=1), jnp.concatenate([dk * c4v, dk * s4v], axis=1)

    dqp, dmkr2 = _rowwise(drope_fn, [dqn, dq_rope, dk_rope, c4, s4], [], [(512, BF16), (256, BF16)], name=nm('d_rope'))
    dkvp = jnp.concatenate([dkn, dvv], axis=1).astype(BF16)
    gw['uq'] = _mm(sv['cqn'], dqp, mode='tn', out_dtype=F32, name=nm('d_uq'))
    dcqn = _mm(dqp, w['uq'], mode='nt', out_dtype=F32, name=nm('d_cqn'))
    gw['ukv'] = _mm(sv['ckvn'], dkvp, mode='tn', out_dtype=F32, name=nm('d_ukv'))
    dckvn = _mm(dkvp, w['ukv'], mode='nt', out_dtype=F32, name=nm('d_ckvn'))
    dmq, gs['g_mla_q'] = _norm_bwd_call(sv['mq'], dcqn, p['g_mla_q'], None, nm('d_norm_q'))
    dmkv, gs['g_mla_kv'] = _norm_bwd_call(sv['mkv'], dckvn, p['g_mla_kv'], None, nm('d_norm_kv'))
    doraw, dgr, gs['g_gla_out'] = _rowwise(_gla_out_bwd, [sv['oraw'], sv['gr'], do_br[1]], [p['g_gla_out']],
                                           [(512, F32), (512, BF16)], [128], name=nm('d_gla_out'))
    st = sv['states']
    st_prev = jnp.concatenate([jnp.zeros_like(st[:, :1]), st[:, :-1]], axis=1)
    dgq, dgk, dgv, dla = _gla_bwd(sv['zb'], sv['la'], st, st_prev, doraw, name=nm('gla_bwd'), **sv['gla'])

    def dgate_fn(dl, gl, wg, bg):
        pre = _dot(gl.astype(BF16), wg) + bg
        dpre = dl * (1.0 / GLA_TAU) * _sig(-pre)
        return dpre, _dot(dpre.astype(BF16), wg, NT), dpre

    dpre, dglow, gs['b_gla'] = _rowwise(dgate_fn, [dla, sv['glow']], [w['gate'], p['b_gla']],
                                        [(256, BF16), (128, BF16)], [256], name=nm('d_gla_gate'))
    gw['gate'] = _mm(sv['glow'], dpre, mode='tn', out_dtype=F32, name=nm('d_wgate'))
    bf = lambda t: t.astype(BF16)
    dzb = jnp.concatenate([dgr, bf(dgq), bf(dgk), bf(dgv), bf(dmq), dmkr2, bf(dff), dglow, bf(dmkv),
                           jnp.zeros((s, B_W - B_END), BF16)], axis=1)
    h = sv['h']
    gw['in_a'] = _mm(h, dza, mode='tn', out_dtype=F32, name=nm('d_in_a'))
    gw['in_b'] = _mm(h, dzb, mode='tn', out_dtype=F32, name=nm('d_in_b'))
    gw['in_c'] = _mm(h, dzc, mode='tn', out_dtype=F32, name=nm('d_in_c'))
    add = lambda acc, prev: prev + acc
    dh = _mm(dza, w['in_a'], mode='nt', out_dtype=F32, name=nm('d_h_a'))
    dh = _mm(dzb, w['in_b'], mode='nt', out_dtype=F32, name=nm('d_h_b'), epilogue=add, extras=[(dh, *_mn())])
    dx0, gs['g_mix'] = _mm(dzc, w['in_c'], mode='nt', out_dtype=F32, name=nm('d_h_c'), col_sums=True, full_rows=True,
                           epilogue=lambda acc, prev, xv, rv, gv: _norm_bwd_epilogue(prev + acc, xv, rv, gv),
                           extras=[(dh, *_mn()), (sv['x0'], *_mn()), (dx1, *_mn()), (p['g_mix'], *_nvec())])
    return dx0, gw, gs, (carried_mla or [None])[0], (carried_fox or [None])[0]


def _loss_head(x, target, g_final):
    d = x.shape[1]

    def fn(xv, tv, gv):
        r = lax.rsqrt(jnp.mean(xv * xv, axis=-1, keepdims=True) + EPS)
        xh = xv * r
        e = xh * gv - tv
        dy = e * (1.0 / d)
        gd = dy * gv
        dx = r * (gd - xh * jnp.mean(gd * xh, axis=-1, keepdims=True))
        row_loss = 0.5 * jnp.mean(e * e, axis=-1, keepdims=True)
        return dx, dy * xh, jnp.broadcast_to(row_loss, (xv.shape[0], LANES))

    return _rowwise(fn, [x, target], [g_final], [(d, F32)], [d, LANES], name='loss_head', ts=512)


def _step(args):
    shapes = {nm: args[nm].shape for nm in ORDER}
    x, mem, target = args['x'][0], args['mem'][0], args['loss_target'][0]
    s = x.shape[0]

    def wire(nm, l):
        w = args[nm][l].astype(BF16)
        if nm == 'w_in':
            w = jnp.pad(w, ((0, 0), (0, WIN_PAD - WIN_SHARD)))
        if nm == 'w_gla_gate':
            w = jnp.pad(w, ((0, GATE_WIRE_ROWS - GLA_RANK), (0, 0)))
        return w

    axis_of = dict(BIG)
    names = tuple(nm for nm, _ in BIG)
    wires = lambda l, nms: [wire(nm, l) for nm in nms]
    width = lambda nm: WIN_PAD if nm == 'w_in' else args[nm].shape[2]
    side_by_side = lambda nms: [axis_of[nm] == 2 and width(nm) % LANES == 0 for nm in nms]
    over_ici = lambda l, nms: _gather_over_ici(wires(l, nms), side_by_side(nms))

    def whole(parts, nms, tag):
        side = side_by_side(nms)
        parts = _run_comm(_gather_over_d2d(parts, side), name=f'gather_d2d_{tag}', alias=True)
        full = {nm: p if sd else _full_layer(p, axis_of[nm]) for nm, p, sd in zip(nms, parts, side)}
        if 'w_gla_gate' in full:
            full['w_gla_gate'] = full['w_gla_gate'][:GLA_RANK]
        return full

    tabs = _rope_tables(s)
    layers_p = []
    for l in range(DEPTH):
        layers_p.append({
            'g_mix': args['g_mix'][l][None], 'b_fox': _padc(args['b_fox_forget'][l][None], 128),
            'b_gla': args['b_gla_gate'][l][None], 'g_gla_out': args['g_gla_out'][l][None],
            'g_mla_q': args['g_mla_q'][l][None], 'g_mla_kv': args['g_mla_kv'][l][None],
            'b_branch': args['b_branch_gate'][l][None], 'g_xa': args['g_xa'][l][None],
            'g_mem': args['g_mem'][l][None], 'g_mlp': args['g_mlp'][l][None]})

    first = _run_comm(over_ici(0, LATE), name='gather_ici_first_l0')
    w_now = _repack_layer_weights(whole(first, LATE, 'first_l0'))
    saved = []
    xl = x
    for l in range(DEPTH):
        carry_fox = over_ici(0, EARLY) if l == 0 else None
        after_fox = (lambda parts: whole(parts, EARLY, 'rest_l0')) if l == 0 else None
        carry_mla = over_ici(l + 1, names) if l + 1 < DEPTH else None
        xl, sv = _layer_fwd(xl, mem, w_now, layers_p[l], tabs, f'l{l}', carry_fox=carry_fox, after_fox=after_fox,
                            carry_mla=carry_mla)
        saved.append(sv)
        if carry_mla is not None:
            w_now = _repack_layer_weights(whole(sv.pop('carried_mla'), names, f'l{l + 1}'))
    dx, dg_final, loss_lanes = _loss_head(xl, target, args['g_final'][None])
    cidx = lax.axis_index('c')
    chip = 2 * lax.axis_index('x') + lax.axis_index('y')

    def pair_sums(gw, nms, tag):
        mine, theirs = [], []
        for nm in nms:
            shards = _split_full(gw[nm], axis_of[nm]).astype(BF16)
            h = shards.shape[1] // 2
            mine.append(lax.dynamic_slice_in_dim(shards, cidx * h, h, axis=1))
            theirs.append(lax.dynamic_slice_in_dim(shards, (1 - cidx) * h, h, axis=1))
        got = _to_sibling(theirs, name=f'grads_swap_{tag}')
        pairs = []
        for nm, a, b in zip(nms, mine, got):
            _, h, n = a.shape
            (p,) = _rowwise(lambda u, v: (u.astype(F32) + v.astype(F32),),
                            [a.reshape(N_CHIPS * h, n), b.reshape(N_CHIPS * h, n)], [], [(n, BF16)],
                            name=f'pair_sum_{nm}_{tag}')
            pairs.append(p.reshape(N_CHIPS, h, n))
        return pairs

    def finish(pairs, from_chips, nms, tag):
        own = [lax.dynamic_index_in_dim(p, chip, axis=0, keepdims=False) for p in pairs]
        mine = [_sum_chips(o, r, name=f'chip_sum_{nm}_{tag}') for nm, o, r in zip(nms, own, from_chips)]
        theirs = _to_sibling(mine, name=f'grads_join_{tag}')
        return {nm: jnp.where(cidx == 0, jnp.concatenate([a, b]), jnp.concatenate([b, a]))
                for nm, a, b in zip(nms, mine, theirs)}

    gs_layers, done = [None] * DEPTH, [{} for _ in range(DEPTH)]
    above = None
    for l in reversed(range(DEPTH)):
        lowest, early_pairs = l == 0, []

        def early(gw_early, l=l, early_pairs=early_pairs):
            early_pairs.extend(pair_sums(gw_early, EARLY, f'early_l{l}'))
            return _chip_exchange(early_pairs)

        carry_mla = None if above is None else _chip_exchange(above[1])
        dx, gw, gs_layers[l], got_mla, got_fox = _layer_bwd(
            dx, mem, saved[l]['w'], layers_p[l], tabs, saved[l], f'l{l}', carry_mla=carry_mla,
            early=early if lowest else None)
        if above is not None:
            done[above[0]].update(finish(above[1], got_mla, names, f'l{above[0]}'))
        grads = _unpack_layer_grads(gw)
        if lowest:
            done[l].update(finish(early_pairs, got_fox, EARLY, f'early_l{l}'))
            late_pairs = pair_sums(grads, LATE, f'late_l{l}')
            from_late = _run_comm(_chip_exchange(late_pairs), name=f'grads_exchange_late_l{l}')
            done[l].update(finish(late_pairs, from_late, LATE, f'late_l{l}'))
        else:
            above = (l, pair_sums(grads, names, f'l{l}'))
    grad_x = dx[None]
    gshard = {nm: jnp.stack([done[l][nm] for l in range(DEPTH)]) for nm in names}

    small_g = []
    for nm, key in (('g_mix', 'g_mix'), ('b_fox_forget', 'b_fox'), ('b_gla_gate', 'b_gla'),
                    ('g_gla_out', 'g_gla_out'), ('g_mla_q', 'g_mla_q'), ('g_mla_kv', 'g_mla_kv'),
                    ('b_branch_gate', 'b_branch'), ('g_xa', 'g_xa'), ('g_mem', 'g_mem'), ('g_mlp', 'g_mlp')):
        width = shapes[nm][1]
        small_g.append(jnp.concatenate([gs_layers[l][key][0, :width] for l in range(DEPTH)]))
    small_g.append(dg_final[0])
    small_g.append(loss_lanes[0, :1])
    flat = jnp.concatenate(small_g)
    n_small = flat.shape[0]
    srows = -(-n_small // (8 * LANES)) * 8
    pad = lambda v: jnp.pad(v, (0, srows * LANES - v.shape[0])).reshape(srows, LANES)
    all_small = _all_gather8(pad(flat), name='gather_small')
    sw, sm, svv = (pad(jnp.concatenate([args[pre + nm].reshape(-1) for nm in SMALL] + [jnp.zeros((1,), F32)]))
                   for pre in ('', 'm_', 'v_'))

    def small_body(g_ref, w_ref, m_ref, v_ref, go_ref, d_ref, mo_ref, vo_ref):
        g = g_ref[0]
        for q in range(1, N_DEV):
            g = g + g_ref[q]
        go_ref[...] = g
        d_ref[...], mo_ref[...], vo_ref[...] = _adam(w_ref[...], g, m_ref[...], v_ref[...])

    sg, sd, snm, snv = pl.pallas_call(
        small_body, name='small_sum_adam', out_shape=[jax.ShapeDtypeStruct((srows, LANES), F32)] * 4,
        compiler_params=pltpu.CompilerParams(vmem_limit_bytes=VMEM_LIMIT))(all_small, sw, sm, svv)

    def unsmall(buf):
        v, out, off = buf.reshape(-1), {}, 0
        for nm in SMALL:
            nel = math.prod(shapes[nm])
            out[nm] = v[off:off + nel].reshape(shapes[nm])
            off += nel
        return out, v[off]

    res = {}
    (res['grad'], loss), (res['delta'], _), (res['m'], _), (res['v'], _) = (unsmall(t) for t in (sg, sd, snm, snv))

    for nm, _ in BIG:
        shp = args[nm].shape
        view = lambda t: t.reshape(shp[0] * shp[1], shp[2])
        d, m2, v2 = _rowwise(_adam, [view(args[nm]), view(gshard[nm]), view(args['m_' + nm]), view(args['v_' + nm])],
                             [], [(shp[2], F32)] * 3, name=f'adam_{nm}',
                             ts=512 if shp[2] <= 1024 else 256)
        res['grad'][nm], res['delta'][nm], res['m'][nm], res['v'][nm] = (
            gshard[nm], d.reshape(shp), m2.reshape(shp), v2.reshape(shp))

    return (loss, grad_x, *[res['grad'][nm] for nm in ORDER], *[res['delta'][nm] for nm in ORDER],
            *[res['m'][nm] for nm in ORDER], *[res['v'][nm] for nm in ORDER])


def kernel(x, mem, g_mix, w_in, b_fox_forget, w_gla_gate, b_gla_gate, g_gla_out, g_mla_q, w_mla_uq, g_mla_kv, w_mla_ukv, b_branch_gate, w_up_fox, w_up_gla, w_up_mla, w_out, g_xa, g_mem, w_xq, w_xkv, w_xo, g_mlp, w_mlp1, w_mlp2, g_final, loss_target, m_g_mix, m_w_in, m_b_fox_forget, m_w_gla_gate, m_b_gla_gate, m_g_gla_out, m_g_mla_q, m_w_mla_uq, m_g_mla_kv, m_w_mla_ukv, m_b_branch_gate, m_w_up_fox, m_w_up_gla, m_w_up_mla, m_w_out, m_g_xa, m_g_mem, m_w_xq, m_w_xkv, m_w_xo, m_g_mlp, m_w_mlp1, m_w_mlp2, m_g_final, v_g_mix, v_w_in, v_b_fox_forget, v_w_gla_gate, v_b_gla_gate, v_g_gla_out, v_g_mla_q, v_w_mla_uq, v_g_mla_kv, v_w_mla_ukv, v_b_branch_gate, v_w_up_fox, v_w_up_gla, v_w_up_mla, v_w_out, v_g_xa, v_g_mem, v_w_xq, v_w_xkv, v_w_xo, v_g_mlp, v_w_mlp1, v_w_mlp2, v_g_final):
    return _step(dict(locals()))
```
